```python
import math
import jax, jax.numpy as jnp
from jax import lax
import numpy as np

D_MODEL = 1024
BATCH = 8
SEQ = 4096
DEPTH = 2

N_A_LAYERS = DEPTH // 2
N_B_LAYERS = DEPTH - N_A_LAYERS

A_HEADS = 16
A_HEAD_DIM = D_MODEL // A_HEADS
A_QKV_WIDTH = 3 * A_HEADS * A_HEAD_DIM
Q_BLOCK = 128

B_GROUPS = ((128, 1), (512, 4), (2048, 16))
B_N_GROUPS = len(B_GROUPS)
B_HEADS_PER_GROUP = 8
B_HEAD_DIM = 64
B_WINDOW_STEPS = 128
B_Q_WIDTH = B_N_GROUPS * B_HEADS_PER_GROUP * B_HEAD_DIM
B_OUT_WIDTH = B_HEADS_PER_GROUP * B_HEAD_DIM
B_KV_WIDTH = 2 * B_Q_WIDTH
ALIBI_TOTAL_HEADS = B_N_GROUPS * B_HEADS_PER_GROUP

D_FF = 2816
CONV_WIDTH = 3

RMS_EPS = 1e-6

kernel_name = "yoco_fox_dilated_convffn_hybrid"


def rms_norm(x, g):
    xf = x.astype(jnp.float32)
    y = xf * lax.rsqrt(jnp.mean(xf * xf, axis=-1, keepdims=True) + RMS_EPS)
    return (y * g.astype(jnp.float32)).astype(x.dtype)


def conv_ffn(h, w_up, conv_w, conv_b, w_down):
    u = h @ w_up
    c = u.shape[-1]
    u = lax.conv_general_dilated(
        u, conv_w[:, None, :].astype(u.dtype), window_strides=(1,),
        padding=[(CONV_WIDTH - 1, 0)],
        dimension_numbers=("NWC", "WIO", "NWC"),
        feature_group_count=c) + conv_b
    a, gate = jnp.split(u, 2, axis=-1)
    return (jax.nn.silu(gate) * a) @ w_down


def forgetting_attention(q, k, v, log_f):
    S = q.shape[1]
    dh = q.shape[-1]
    scale = dh ** -0.5
    c = jnp.cumsum(log_f.astype(jnp.float32), axis=1).transpose(0, 2, 1)
    outs = []
    for i in range(S // Q_BLOCK):
        q0 = i * Q_BLOCK
        q1 = q0 + Q_BLOCK
        s = jnp.einsum("bqhd,bkhd->bhqk", q[:, q0:q1], k[:, :q1]).astype(jnp.float32) * scale
        s = s + c[:, :, q0:q1, None] - c[:, :, None, :q1]
        causal = jnp.arange(q0, q1)[:, None] >= jnp.arange(q1)[None, :]
        p = jax.nn.softmax(jnp.where(causal, s, -jnp.inf), axis=-1)
        outs.append(jnp.einsum("bhqk,bkhd->bqhd", p.astype(v.dtype), v[:, :q1]))
    return jnp.concatenate(outs, axis=1)


def dilated_branch(q, k, v, dil, slopes):
    B, S, H, Dh = q.shape
    W = B_WINDOW_STEPS
    period = dil * W
    Sp = -(-S // period) * period
    nb = Sp // period
    pad = ((0, 0), (0, Sp - S), (0, 0), (0, 0))

    def blocks(t):
        return jnp.pad(t, pad).reshape(B, nb, W, dil, H, Dh)

    def with_prev(t):
        prev = jnp.pad(t, ((0, 0), (1, 0), (0, 0), (0, 0), (0, 0), (0, 0)))[:, :nb]
        return jnp.concatenate([prev, t], axis=2)

    qb = blocks(q)
    kk = with_prev(blocks(k))
    vv = with_prev(blocks(v))
    s = jnp.einsum("bnirhd,bnjrhd->bnrhij", qb, kk).astype(jnp.float32) * (Dh ** -0.5)
    qi = jnp.arange(W)[:, None]
    kj = jnp.arange(2 * W)[None, :]
    dist = qi + W - kj
    band = (dist >= 0) & (dist <= W)
    first = (jnp.arange(nb)[:, None, None] == 0) & (kj < W)[None]
    valid = band[None] & ~first
    bias = -slopes[:, None, None] * (dil * dist).astype(jnp.float32)
    s = jnp.where(valid[None, :, None, None], s + bias, -jnp.inf)
    m = jnp.max(s, axis=-1, keepdims=True)
    p = jnp.exp(s - m)
    l = jnp.sum(p, axis=-1, keepdims=True)
    o = jnp.einsum("bnrhij,bnjrhd->bnirhd", (p / l).astype(v.dtype), vv)
    lse = (m + jnp.log(l))[..., 0]
    o = o.reshape(B, Sp, H, Dh)[:, :S]
    lse = lse.transpose(0, 1, 4, 2, 3).reshape(B, Sp, H)[:, :S]
    return o, lse


def _fwd_setup_inputs(seed: int = 0) -> dict:
    key = jax.random.key(seed)
    ks = jax.random.split(key, 16)
    f32 = jnp.float32

    def nrm(k, shape, fan_in):
        return jax.random.normal(k, shape, f32) * (fan_in ** -0.5)

    def gain(k, shape):
        return 1.0 + 0.05 * jax.random.normal(k, shape, f32)

    return {
        "x": jax.random.normal(ks[0], (BATCH, SEQ, D_MODEL), f32),
        "a_w_in": nrm(ks[1], (N_A_LAYERS, D_MODEL, A_QKV_WIDTH + A_HEADS), D_MODEL),
        "a_b_f": 2.0 + 0.1 * jax.random.normal(ks[2], (N_A_LAYERS, A_HEADS), f32),
        "a_w_out": nrm(ks[3], (N_A_LAYERS, A_HEADS * A_HEAD_DIM, D_MODEL), A_HEADS * A_HEAD_DIM),
        "b_w_q": nrm(ks[4], (N_B_LAYERS, D_MODEL, B_Q_WIDTH), D_MODEL),
        "b_w_out": nrm(ks[5], (N_B_LAYERS, B_OUT_WIDTH, D_MODEL), B_OUT_WIDTH),
        "kv_norm_g": gain(ks[6], (D_MODEL,)),
        "w_kv": nrm(ks[7], (D_MODEL, B_KV_WIDTH), D_MODEL),
        "mix_norm_g": gain(ks[8], (DEPTH, D_MODEL)),
        "ffn_norm_g": gain(ks[9], (DEPTH, D_MODEL)),
        "ffn_w_up": nrm(ks[10], (DEPTH, D_MODEL, 2 * D_FF), D_MODEL),
        "ffn_conv_w": nrm(ks[11], (DEPTH, CONV_WIDTH, 2 * D_FF), CONV_WIDTH),
        "ffn_conv_b": 0.02 * jax.random.normal(ks[12], (DEPTH, 2 * D_FF), f32),
        "ffn_w_down": nrm(ks[13], (DEPTH, D_FF, D_MODEL), D_FF),
        "final_norm_g": gain(ks[14], (D_MODEL,)),
    }


def _fwd_reference(x, a_w_in, a_b_f, a_w_out, b_w_q, b_w_out, kv_norm_g, w_kv,
              mix_norm_g, ffn_norm_g, ffn_w_up, ffn_conv_w, ffn_conv_b, ffn_w_down,
              final_norm_g):
    B, S, D = x.shape
    slopes = jnp.exp2(-8.0 * jnp.arange(1, ALIBI_TOTAL_HEADS + 1, dtype=jnp.float32)
                      / ALIBI_TOTAL_HEADS).reshape(B_N_GROUPS, B_HEADS_PER_GROUP)
    kv = None
    for layer in range(DEPTH):
        if layer < N_A_LAYERS:
            h = rms_norm(x, mix_norm_g[layer])
            proj = h @ a_w_in[layer]
            qkv = proj[..., :A_QKV_WIDTH].reshape(B, S, 3, A_HEADS, A_HEAD_DIM)
            log_f = jax.nn.log_sigmoid(
                proj[..., A_QKV_WIDTH:].astype(jnp.float32) + a_b_f[layer].astype(jnp.float32))
            o = forgetting_attention(qkv[:, :, 0], qkv[:, :, 1], qkv[:, :, 2], log_f)
            x = x + o.reshape(B, S, A_HEADS * A_HEAD_DIM) @ a_w_out[layer]
        else:
            if kv is None:
                kv = (rms_norm(x, kv_norm_g) @ w_kv).reshape(
                    B, S, 2, B_N_GROUPS, B_HEADS_PER_GROUP, B_HEAD_DIM)
            bl = layer - N_A_LAYERS
            h = rms_norm(x, mix_norm_g[layer])
            q = (h @ b_w_q[bl]).reshape(B, S, B_N_GROUPS, B_HEADS_PER_GROUP, B_HEAD_DIM)
            outs, lses = [], []
            for g, (window, dil) in enumerate(B_GROUPS):
                o_g, lse_g = dilated_branch(q[:, :, g], kv[:, :, 0, g], kv[:, :, 1, g],
                                            dil, slopes[g])
                outs.append(o_g)
                lses.append(lse_g)
            alpha = jax.nn.softmax(jnp.stack(lses, axis=0), axis=0)
            o = jnp.sum(alpha[..., None].astype(outs[0].dtype) * jnp.stack(outs, axis=0), axis=0)
            x = x + o.reshape(B, S, B_OUT_WIDTH) @ b_w_out[bl]
        h = rms_norm(x, ffn_norm_g[layer])
        x = x + conv_ffn(h, ffn_w_up[layer], ffn_conv_w[layer], ffn_conv_b[layer],
                         ffn_w_down[layer])
    return rms_norm(x, final_norm_g)


import jax as _jax
import jax.numpy as _jnp

TWIN_FORMAT = 'train_step'
FWD_PARAMS = ['x', 'a_w_in', 'a_b_f', 'a_w_out', 'b_w_q', 'b_w_out', 'kv_norm_g', 'w_kv', 'mix_norm_g', 'ffn_norm_g', 'ffn_w_up', 'ffn_conv_w', 'ffn_conv_b', 'ffn_w_down', 'final_norm_g']
TWIN_WEIGHTS = ['a_w_in', 'a_b_f', 'a_w_out', 'b_w_q', 'b_w_out', 'kv_norm_g', 'w_kv', 'mix_norm_g', 'ffn_norm_g', 'ffn_w_up', 'ffn_conv_w', 'ffn_conv_b', 'ffn_w_down', 'final_norm_g']
TWIN_DIFF_INPUT = 'x'
TWIN_INPUTS = ['x', 'a_w_in', 'a_b_f', 'a_w_out', 'b_w_q', 'b_w_out', 'kv_norm_g', 'w_kv', 'mix_norm_g', 'ffn_norm_g', 'ffn_w_up', 'ffn_conv_w', 'ffn_conv_b', 'ffn_w_down', 'final_norm_g', 'loss_target', 'm_a_w_in', 'm_a_b_f', 'm_a_w_out', 'm_b_w_q', 'm_b_w_out', 'm_kv_norm_g', 'm_w_kv', 'm_mix_norm_g', 'm_ffn_norm_g', 'm_ffn_w_up', 'm_ffn_conv_w', 'm_ffn_conv_b', 'm_ffn_w_down', 'm_final_norm_g', 'v_a_w_in', 'v_a_b_f', 'v_a_w_out', 'v_b_w_q', 'v_b_w_out', 'v_kv_norm_g', 'v_w_kv', 'v_mix_norm_g', 'v_ffn_norm_g', 'v_ffn_w_up', 'v_ffn_conv_w', 'v_ffn_conv_b', 'v_ffn_w_down', 'v_final_norm_g']
TWIN_OUTPUTS = ['loss', 'grad_x', 'grad_a_w_in', 'grad_a_b_f', 'grad_a_w_out', 'grad_b_w_q', 'grad_b_w_out', 'grad_kv_norm_g', 'grad_w_kv', 'grad_mix_norm_g', 'grad_ffn_norm_g', 'grad_ffn_w_up', 'grad_ffn_conv_w', 'grad_ffn_conv_b', 'grad_ffn_w_down', 'grad_final_norm_g', 'delta_a_w_in', 'delta_a_b_f', 'delta_a_w_out', 'delta_b_w_q', 'delta_b_w_out', 'delta_kv_norm_g', 'delta_w_kv', 'delta_mix_norm_g', 'delta_ffn_norm_g', 'delta_ffn_w_up', 'delta_ffn_conv_w', 'delta_ffn_conv_b', 'delta_ffn_w_down', 'delta_final_norm_g', 'new_m_a_w_in', 'new_m_a_b_f', 'new_m_a_w_out', 'new_m_b_w_q', 'new_m_b_w_out', 'new_m_kv_norm_g', 'new_m_w_kv', 'new_m_mix_norm_g', 'new_m_ffn_norm_g', 'new_m_ffn_w_up', 'new_m_ffn_conv_w', 'new_m_ffn_conv_b', 'new_m_ffn_w_down', 'new_m_final_norm_g', 'new_v_a_w_in', 'new_v_a_b_f', 'new_v_a_w_out', 'new_v_b_w_q', 'new_v_b_w_out', 'new_v_kv_norm_g', 'new_v_w_kv', 'new_v_mix_norm_g', 'new_v_ffn_norm_g', 'new_v_ffn_w_up', 'new_v_ffn_conv_w', 'new_v_ffn_conv_b', 'new_v_ffn_w_down', 'new_v_final_norm_g']
TWIN_LEAF_KINDS = {'loss': 'loss', 'grad_x': 'grad_x', 'grad_a_w_in': 'grad_w', 'grad_a_b_f': 'grad_w', 'grad_a_w_out': 'grad_w', 'grad_b_w_q': 'grad_w', 'grad_b_w_out': 'grad_w', 'grad_kv_norm_g': 'grad_w', 'grad_w_kv': 'grad_w', 'grad_mix_norm_g': 'grad_w', 'grad_ffn_norm_g': 'grad_w', 'grad_ffn_w_up': 'grad_w', 'grad_ffn_conv_w': 'grad_w', 'grad_ffn_conv_b': 'grad_w', 'grad_ffn_w_down': 'grad_w', 'grad_final_norm_g': 'grad_w', 'delta_a_w_in': 'delta_w', 'delta_a_b_f': 'delta_w', 'delta_a_w_out': 'delta_w', 'delta_b_w_q': 'delta_w', 'delta_b_w_out': 'delta_w', 'delta_kv_norm_g': 'delta_w', 'delta_w_kv': 'delta_w', 'delta_mix_norm_g': 'delta_w', 'delta_ffn_norm_g': 'delta_w', 'delta_ffn_w_up': 'delta_w', 'delta_ffn_conv_w': 'delta_w', 'delta_ffn_conv_b': 'delta_w', 'delta_ffn_w_down': 'delta_w', 'delta_final_norm_g': 'delta_w', 'new_m_a_w_in': 'new_m', 'new_m_a_b_f': 'new_m', 'new_m_a_w_out': 'new_m', 'new_m_b_w_q': 'new_m', 'new_m_b_w_out': 'new_m', 'new_m_kv_norm_g': 'new_m', 'new_m_w_kv': 'new_m', 'new_m_mix_norm_g': 'new_m', 'new_m_ffn_norm_g': 'new_m', 'new_m_ffn_w_up': 'new_m', 'new_m_ffn_conv_w': 'new_m', 'new_m_ffn_conv_b': 'new_m', 'new_m_ffn_w_down': 'new_m', 'new_m_final_norm_g': 'new_m', 'new_v_a_w_in': 'new_v', 'new_v_a_b_f': 'new_v', 'new_v_a_w_out': 'new_v', 'new_v_b_w_q': 'new_v', 'new_v_b_w_out': 'new_v', 'new_v_kv_norm_g': 'new_v', 'new_v_w_kv': 'new_v', 'new_v_mix_norm_g': 'new_v', 'new_v_ffn_norm_g': 'new_v', 'new_v_ffn_w_up': 'new_v', 'new_v_ffn_conv_w': 'new_v', 'new_v_ffn_conv_b': 'new_v', 'new_v_ffn_w_down': 'new_v', 'new_v_final_norm_g': 'new_v'}


def _forward(args):
    return _fwd_reference(*[args[k] for k in FWD_PARAMS])


def _output_shape():
    out = _jax.eval_shape(lambda: _forward(_fwd_setup_inputs(0)))
    return out.shape, out.dtype

N_MICROBATCH = 1
ADAM_LR = 0.001
ADAM_B1 = 0.9
ADAM_B2 = 0.999
ADAM_EPS = 1e-08
ADAM_WD = 0.01
ADAM_STEP = 10
PER_EXAMPLE_BATCH_AXIS = {'x': 0, 'loss_target': 0}
SHARED_INPUTS = []
_WEIGHT_DTYPES = {'a_w_in': _jnp.float32, 'a_b_f': _jnp.float32, 'a_w_out': _jnp.float32, 'b_w_q': _jnp.float32, 'b_w_out': _jnp.float32, 'kv_norm_g': _jnp.float32, 'w_kv': _jnp.float32, 'mix_norm_g': _jnp.float32, 'ffn_norm_g': _jnp.float32, 'ffn_w_up': _jnp.float32, 'ffn_conv_w': _jnp.float32, 'ffn_conv_b': _jnp.float32, 'ffn_w_down': _jnp.float32, 'final_norm_g': _jnp.float32}
MOMENT_SCALE = {'a_w_in': 7.511762e-02, 'a_b_f': 9.979137e-01, 'a_w_out': 8.794197e-02, 'b_w_q': 3.377564e-02, 'b_w_out': 5.150226e-02, 'kv_norm_g': 6.713553e-02, 'w_kv': 3.806346e-02, 'mix_norm_g': 9.873292e-02, 'ffn_norm_g': 1.297398e-01, 'ffn_w_up': 5.543601e-02, 'ffn_conv_w': 5.497023e-02, 'ffn_conv_b': 5.442214e-02, 'ffn_w_down': 9.092826e-02, 'final_norm_g': 3.197271e+01}


def _to_microbatches(a, axis):
    t = _jnp.moveaxis(a, axis, 0)
    t = t.reshape((N_MICROBATCH, t.shape[0] // N_MICROBATCH) + t.shape[1:])
    return _jnp.moveaxis(t, 1, axis + 1)


def setup_inputs(seed: int = 0) -> dict:
    inp = _fwd_setup_inputs(seed)
    key = _jax.random.fold_in(_jax.random.key(seed), 7919)
    shape, _ = _output_shape()
    out = dict(inp)
    out["loss_target"] = _jax.random.normal(_jax.random.fold_in(key, 0), shape, _jnp.float32)
    for i, name in enumerate(TWIN_WEIGHTS):
        w = inp[name].astype(_jnp.float32)
        if MOMENT_SCALE is None:
            s = _jnp.sqrt(_jnp.mean(_jnp.square(w)) + 1e-30)
        else:
            s = MOMENT_SCALE[name]
        km, kv = _jax.random.split(_jax.random.fold_in(key, i + 1))
        out[name] = w
        out["m_" + name] = s * _jax.random.normal(km, w.shape, _jnp.float32)
        out["v_" + name] = (s * s) * _jax.random.uniform(kv, w.shape, _jnp.float32, 0.5, 1.5)
    if N_MICROBATCH > 1:
        for name, axis in PER_EXAMPLE_BATCH_AXIS.items():
            out[name] = _to_microbatches(out[name], axis)
    return {'x': out['x'], 'a_w_in': out['a_w_in'], 'a_b_f': out['a_b_f'], 'a_w_out': out['a_w_out'], 'b_w_q': out['b_w_q'], 'b_w_out': out['b_w_out'], 'kv_norm_g': out['kv_norm_g'], 'w_kv': out['w_kv'], 'mix_norm_g': out['mix_norm_g'], 'ffn_norm_g': out['ffn_norm_g'], 'ffn_w_up': out['ffn_w_up'], 'ffn_conv_w': out['ffn_conv_w'], 'ffn_conv_b': out['ffn_conv_b'], 'ffn_w_down': out['ffn_w_down'], 'final_norm_g': out['final_norm_g'], 'loss_target': out['loss_target'], 'm_a_w_in': out['m_a_w_in'], 'm_a_b_f': out['m_a_b_f'], 'm_a_w_out': out['m_a_w_out'], 'm_b_w_q': out['m_b_w_q'], 'm_b_w_out': out['m_b_w_out'], 'm_kv_norm_g': out['m_kv_norm_g'], 'm_w_kv': out['m_w_kv'], 'm_mix_norm_g': out['m_mix_norm_g'], 'm_ffn_norm_g': out['m_ffn_norm_g'], 'm_ffn_w_up': out['m_ffn_w_up'], 'm_ffn_conv_w': out['m_ffn_conv_w'], 'm_ffn_conv_b': out['m_ffn_conv_b'], 'm_ffn_w_down': out['m_ffn_w_down'], 'm_final_norm_g': out['m_final_norm_g'], 'v_a_w_in': out['v_a_w_in'], 'v_a_b_f': out['v_a_b_f'], 'v_a_w_out': out['v_a_w_out'], 'v_b_w_q': out['v_b_w_q'], 'v_b_w_out': out['v_b_w_out'], 'v_kv_norm_g': out['v_kv_norm_g'], 'v_w_kv': out['v_w_kv'], 'v_mix_norm_g': out['v_mix_norm_g'], 'v_ffn_norm_g': out['v_ffn_norm_g'], 'v_ffn_w_up': out['v_ffn_w_up'], 'v_ffn_conv_w': out['v_ffn_conv_w'], 'v_ffn_conv_b': out['v_ffn_conv_b'], 'v_ffn_w_down': out['v_ffn_w_down'], 'v_final_norm_g': out['v_final_norm_g']}


def _loss(weights, diff, rest, loss_target):
    with _jax.named_scope("forward"):
        args = {**rest, TWIN_DIFF_INPUT: diff, **{k: w.astype(_WEIGHT_DTYPES[k]) for k, w in weights.items()}}
        y = _forward(args)
    with _jax.named_scope("loss_head"):
        err = _jnp.square(y.astype(_jnp.float32) - loss_target)
        return 0.5 * _jnp.sum(_jnp.mean(err, axis=-1)) if err.ndim else 0.5 * err


def _adamw(w, g, m, v):
    m = ADAM_B1 * m + (1.0 - ADAM_B1) * g
    v = ADAM_B2 * v + (1.0 - ADAM_B2) * _jnp.square(g)
    m_hat = m / (1.0 - ADAM_B1 ** ADAM_STEP)
    v_hat = v / (1.0 - ADAM_B2 ** ADAM_STEP)
    delta = -ADAM_LR * (m_hat / (_jnp.sqrt(v_hat) + ADAM_EPS) + ADAM_WD * w)
    return delta, m, v


def reference(x, a_w_in, a_b_f, a_w_out, b_w_q, b_w_out, kv_norm_g, w_kv, mix_norm_g, ffn_norm_g, ffn_w_up, ffn_conv_w, ffn_conv_b, ffn_w_down, final_norm_g, loss_target, m_a_w_in, m_a_b_f, m_a_w_out, m_b_w_q, m_b_w_out, m_kv_norm_g, m_w_kv, m_mix_norm_g, m_ffn_norm_g, m_ffn_w_up, m_ffn_conv_w, m_ffn_conv_b, m_ffn_w_down, m_final_norm_g, v_a_w_in, v_a_b_f, v_a_w_out, v_b_w_q, v_b_w_out, v_kv_norm_g, v_w_kv, v_mix_norm_g, v_ffn_norm_g, v_ffn_w_up, v_ffn_conv_w, v_ffn_conv_b, v_ffn_w_down, v_final_norm_g):
    given = dict(x=x, a_w_in=a_w_in, a_b_f=a_b_f, a_w_out=a_w_out, b_w_q=b_w_q, b_w_out=b_w_out, kv_norm_g=kv_norm_g, w_kv=w_kv, mix_norm_g=mix_norm_g, ffn_norm_g=ffn_norm_g, ffn_w_up=ffn_w_up, ffn_conv_w=ffn_conv_w, ffn_conv_b=ffn_conv_b, ffn_w_down=ffn_w_down, final_norm_g=final_norm_g, loss_target=loss_target, m_a_w_in=m_a_w_in, m_a_b_f=m_a_b_f, m_a_w_out=m_a_w_out, m_b_w_q=m_b_w_q, m_b_w_out=m_b_w_out, m_kv_norm_g=m_kv_norm_g, m_w_kv=m_w_kv, m_mix_norm_g=m_mix_norm_g, m_ffn_norm_g=m_ffn_norm_g, m_ffn_w_up=m_ffn_w_up, m_ffn_conv_w=m_ffn_conv_w, m_ffn_conv_b=m_ffn_conv_b, m_ffn_w_down=m_ffn_w_down, m_final_norm_g=m_final_norm_g, v_a_w_in=v_a_w_in, v_a_b_f=v_a_b_f, v_a_w_out=v_a_w_out, v_b_w_q=v_b_w_q, v_b_w_out=v_b_w_out, v_kv_norm_g=v_kv_norm_g, v_w_kv=v_w_kv, v_mix_norm_g=v_mix_norm_g, v_ffn_norm_g=v_ffn_norm_g, v_ffn_w_up=v_ffn_w_up, v_ffn_conv_w=v_ffn_conv_w, v_ffn_conv_b=v_ffn_conv_b, v_ffn_w_down=v_ffn_w_down, v_final_norm_g=v_final_norm_g)
    weights = {n: given[n] for n in TWIN_WEIGHTS}
    shared = {n: given[n] for n in SHARED_INPUTS}
    per_example = {n: given[n] for n in ['x']}
    grad_fn = _jax.value_and_grad(_loss, argnums=(0, 1))

    def one_microbatch(ex, loss_target):
        ex = dict(ex)
        diff = ex.pop(TWIN_DIFF_INPUT)
        return grad_fn(weights, diff, {**shared, **ex}, loss_target)

    if N_MICROBATCH == 1:
        loss, (grad_w, grad_x) = one_microbatch(per_example, given["loss_target"])
    else:
        def body(carry, xs):
            loss_sum, grad_sum = carry
            l_k, (gw_k, gx_k) = one_microbatch(xs[0], xs[1])
            with _jax.named_scope("update"):
                return (loss_sum + l_k, _jax.tree.map(_jnp.add, grad_sum, gw_k)), gx_k

        init = (_jnp.zeros((), _jnp.float32), _jax.tree.map(_jnp.zeros_like, weights))
        (loss, grad_w), grad_x = _jax.lax.scan(body, init, (per_example, given["loss_target"]))
    with _jax.named_scope("update"):
        delta_w, new_m, new_v = {}, {}, {}
        for n in TWIN_WEIGHTS:
            delta_w[n], new_m[n], new_v[n] = _adamw(weights[n], grad_w[n], given["m_" + n], given["v_" + n])
    return (loss, grad_x, *[grad_w[n] for n in TWIN_WEIGHTS], *[delta_w[n] for n in TWIN_WEIGHTS],
            *[new_m[n] for n in TWIN_WEIGHTS], *[new_v[n] for n in TWIN_WEIGHTS])
```

```python
import functools
import math

import numpy as np
import jax
import jax.numpy as jnp
from jax import lax
from jax.experimental import pallas as pl
from jax.experimental.pallas import tpu as pltpu

F32 = jnp.float32
BF16 = jnp.bfloat16
U16 = jnp.uint16
MESH = pl.DeviceIdType.MESH

S = 4096
D = 1024
A_HEADS = 16
HEAD_DIM = 64
QKV_W = 3 * A_HEADS * HEAD_DIM
B_GROUPS = ((128, 1), (512, 4), (2048, 16))
B_HPG = 8
B_Q_W = 3 * B_HPG * HEAD_DIM
B_OUT_W = B_HPG * HEAD_DIM
B_KV_W = 2 * B_Q_W
B_WIN = 128
FF = 2816
RMS_EPS = 1e-6
SCALE = HEAD_DIM ** -0.5
N_CHIPS = 4

ADAM_LR, ADAM_B1, ADAM_B2, ADAM_EPS, ADAM_WD, ADAM_STEP = 0.001, 0.9, 0.999, 1e-08, 0.01, 10

V7X_VMEM_LIMIT = 48 * 1024 * 1024
LANES = 128
NEG_INF = float("-inf")

_SEGS = (("a_w_in", 772, 784), ("a_w_out", 256, 256), ("b_w_q", 384, 384), ("b_w_out", 128, 128),
         ("w_kv", 768, 768), ("ffn_w_up", 2816, 2816), ("ffn_w_down", 1408, 1408), ("ffn_conv_w", 9, 32))
FLAT_ROWS = 6656
HALF_ROWS = FLAT_ROWS // 2
SMALL_ROWS = 24


def _cparams(sem=None, **kw):
    return pltpu.CompilerParams(dimension_semantics=sem, vmem_limit_bytes=V7X_VMEM_LIMIT, **kw)


_DN = {"nn": (((1,), (0,)), ((), ())), "nt": (((1,), (1,)), ((), ())), "tn": (((0,), (0,)), ((), ()))}


def _mm(a, b, *, mode, tm, tn, tk, name, out_dtype=F32, res=None, a_split=0, b_split=0, o_split=0):
    if mode == "tn":
        K = a.shape[0]
        M = a.shape[1]
    else:
        M = a.shape[-2]
        K = a.shape[-1] * (2 if a_split else 1)
    if mode == "nt":
        N = b.shape[0]
    else:
        N = b.shape[-1] * (2 if b_split else 1)
    assert M % tm == 0 and N % tn == 0 and K % tk == 0, (name, M, N, K, tm, tn, tk)
    nk = K // tk

    if mode == "tn":
        a_spec = pl.BlockSpec((tk, tm), lambda i, j, k: (k, i))
    elif a_split:
        a_spec = pl.BlockSpec((None, tm, tk), lambda i, j, k: (k // a_split, i, k % a_split))
    else:
        a_spec = pl.BlockSpec((tm, tk), lambda i, j, k: (i, k))
    if mode == "nt":
        b_spec = pl.BlockSpec((tn, tk), lambda i, j, k: (j, k))
    elif b_split:
        b_spec = pl.BlockSpec((None, tk, tn), lambda i, j, k: (j // b_split, k, j % b_split))
    else:
        b_spec = pl.BlockSpec((tk, tn), lambda i, j, k: (k, j))
    if o_split:
        o_spec = pl.BlockSpec((None, tm, tn), lambda i, j, k: (j // o_split, i, j % o_split))
        out_shape = jax.ShapeDtypeStruct((2, M, N // 2), out_dtype)
    else:
        o_spec = pl.BlockSpec((tm, tn), lambda i, j, k: (i, j))
        out_shape = jax.ShapeDtypeStruct((M, N), out_dtype)
    in_specs = [a_spec, b_spec]
    args = [a, b]
    if res is not None:
        in_specs.append(pl.BlockSpec((tm, tn), lambda i, j, k: (i, j)))
        args.append(res)

    def body(*refs):
        if res is not None:
            a_ref, b_ref, r_ref, o_ref = refs[:4]
        else:
            a_ref, b_ref, o_ref = refs[:3]
            r_ref = None
        p = lax.dot_general(a_ref[...].astype(BF16), b_ref[...].astype(BF16), _DN[mode], preferred_element_type=F32)

        def finish(r):
            if r_ref is not None:
                r = r + r_ref[...]
            o_ref[...] = r.astype(out_dtype)

        if nk == 1:
            finish(p)
        else:
            acc = refs[-1]
            k = pl.program_id(2)

            @pl.when(k == 0)
            def _():
                acc[...] = p

            @pl.when(k > 0)
            def _():
                acc[...] += p

            @pl.when(k == nk - 1)
            def _():
                finish(acc[...])

    return pl.pallas_call(
        body, out_shape=out_shape, grid=(M // tm, N // tn, nk), in_specs=in_specs, out_specs=o_spec,
        scratch_shapes=[pltpu.VMEM((tm, tn), F32)] if nk > 1 else [],
        compiler_params=_cparams(("parallel", "parallel", "arbitrary")), name=name,
    )(*args)


NORM_ROWS = 256


def _rms_fwd(x, g, name):
    def body(x_ref, g_ref, o_ref):
        xv = x_ref[...]
        r = lax.rsqrt(jnp.mean(xv * xv, axis=-1, keepdims=True) + RMS_EPS)
        o_ref[...] = (xv * r * g_ref[...]).astype(BF16)

    row = pl.BlockSpec((NORM_ROWS, D), lambda i: (i, 0))
    return pl.pallas_call(
        body, out_shape=jax.ShapeDtypeStruct((S, D), BF16), grid=(S // NORM_ROWS,),
        in_specs=[row, pl.BlockSpec((1, D), lambda i: (0, 0))], out_specs=row,
        compiler_params=_cparams(("parallel",)), name=name,
    )(x, g.reshape(1, D))


def _rms_bwd(x, dres, pairs, name):
    n = len(pairs)

    def body(*refs):
        x_ref, dres_ref = refs[0], refs[1]
        g_refs = refs[2:2 + 2 * n:2]
        dh_refs = refs[3:3 + 2 * n:2]
        dx_ref, dxb_ref = refs[2 + 2 * n], refs[3 + 2 * n]
        dg_refs = refs[4 + 2 * n:]
        i = pl.program_id(0)
        xv = x_ref[...]
        r = lax.rsqrt(jnp.mean(xv * xv, axis=-1, keepdims=True) + RMS_EPS)
        y = xv * r
        dx = dres_ref[...]
        for g_ref, dh_ref, dg_ref in zip(g_refs, dh_refs, dg_refs):
            dh = dh_ref[...]
            dy = dh * g_ref[...]
            dx = dx + r * (dy - y * jnp.mean(dy * y, axis=-1, keepdims=True))
            part = jnp.sum(dh * y, axis=0, keepdims=True)

            @pl.when(i == 0)
            def _():
                dg_ref[...] = part

            @pl.when(i > 0)
            def _():
                dg_ref[...] += part

        dx_ref[...] = dx
        dxb_ref[...] = dx.astype(BF16)

    row = pl.BlockSpec((NORM_ROWS, D), lambda i: (i, 0))
    vec = pl.BlockSpec((1, D), lambda i: (0, 0))
    in_specs = [row, row]
    args = [x, dres]
    for g, dh in pairs:
        in_specs += [vec, row]
        args += [g.reshape(1, D), dh]
    outs = pl.pallas_call(
        body,
        out_shape=[jax.ShapeDtypeStruct((S, D), F32), jax.ShapeDtypeStruct((S, D), BF16)]
        + [jax.ShapeDtypeStruct((1, D), F32)] * n,
        grid=(S // NORM_ROWS,), in_specs=in_specs, out_specs=[row, row] + [vec] * n,
        compiler_params=_cparams(("arbitrary",)), name=name,
    )(*args)
    return outs[0], outs[1], list(outs[2:])


def _loss_head(x, g, target, name):
    def body(x_ref, g_ref, t_ref, loss_ref, dx_ref, dxb_ref, dg_ref):
        i = pl.program_id(0)
        xv = x_ref[...]
        gv = g_ref[...]
        r = lax.rsqrt(jnp.mean(xv * xv, axis=-1, keepdims=True) + RMS_EPS)
        y = xv * r
        err = y * gv - t_ref[...]
        lpart = jnp.broadcast_to(jnp.sum(err * err, keepdims=True) * (0.5 / D), (1, LANES))
        dh = err * (1.0 / D)
        dy = dh * gv
        dx = r * (dy - y * jnp.mean(dy * y, axis=-1, keepdims=True))
        part = jnp.sum(dh * y, axis=0, keepdims=True)

        @pl.when(i == 0)
        def _():
            dg_ref[...] = part
            loss_ref[...] = lpart

        @pl.when(i > 0)
        def _():
            dg_ref[...] += part
            loss_ref[...] += lpart

        dx_ref[...] = dx
        dxb_ref[...] = dx.astype(BF16)

    row = pl.BlockSpec((NORM_ROWS, D), lambda i: (i, 0))
    vec = pl.BlockSpec((1, D), lambda i: (0, 0))
    return pl.pallas_call(
        body,
        out_shape=[jax.ShapeDtypeStruct((1, LANES), F32), jax.ShapeDtypeStruct((S, D), F32),
                   jax.ShapeDtypeStruct((S, D), BF16), jax.ShapeDtypeStruct((1, D), F32)],
        grid=(S // NORM_ROWS,), in_specs=[row, vec, row],
        out_specs=[pl.BlockSpec((1, LANES), lambda i: (0, 0)), row, row, vec],
        compiler_params=_cparams(("arbitrary",)), name=name,
    )(x, g.reshape(1, D), target)


SCAN_ROWS = 256


def _split3(v):
    hi = v.astype(BF16)
    r1 = v - hi.astype(F32)
    mid = r1.astype(BF16)
    lo = (r1 - mid.astype(F32)).astype(BF16)
    return hi, mid, lo


def _tri_dot(tri, v):
    hi, mid, lo = _split3(v)
    dn = _DN["nn"]
    return (lax.dot_general(tri, hi, dn, preferred_element_type=F32)
            + lax.dot_general(tri, mid, dn, preferred_element_type=F32)
            + lax.dot_general(tri, lo, dn, preferred_element_type=F32))


def _log_sigmoid(z):
    return jnp.minimum(z, 0.0) - jnp.log(1.0 + jnp.exp(-jnp.abs(z)))


def _fgate_fwd(pf, bias, name):
    tri = jnp.tril(jnp.ones((SCAN_ROWS, SCAN_ROWS), F32)).astype(BF16)

    def body(pf_ref, b_ref, tri_ref, c_ref):
        carry = jnp.zeros((1, LANES), F32)
        for blk in range(S // SCAN_ROWS):
            rows = pl.ds(blk * SCAN_ROWS, SCAN_ROWS)
            lf = _log_sigmoid(pf_ref[rows, :] + b_ref[...])
            c_ref[rows, :] = _tri_dot(tri_ref[...], lf) + carry
            carry = c_ref[pl.ds(blk * SCAN_ROWS + SCAN_ROWS - 1, 1), :]

    return pl.pallas_call(
        body, out_shape=jax.ShapeDtypeStruct((S, LANES), F32),
        compiler_params=_cparams(), name=name,
    )(pf, bias, tri)


def _fgate_bwd(pf, bias, dc_key, dc_query, name):
    triu = jnp.triu(jnp.ones((SCAN_ROWS, SCAN_ROWS), F32)).astype(BF16)

    def body(pf_ref, b_ref, dck_ref, dcq_ref, tri_ref, dpf_ref, db_ref, dlf_ref):
        carry = jnp.zeros((1, LANES), F32)
        db = jnp.zeros((1, LANES), F32)
        lane = lax.broadcasted_iota(jnp.int32, (SCAN_ROWS, LANES), 1)
        for blk in reversed(range(S // SCAN_ROWS)):
            rows = pl.ds(blk * SCAN_ROWS, SCAN_ROWS)
            dc = dck_ref[rows, :] + dcq_ref[rows, :]
            dlf_ref[rows, :] = _tri_dot(tri_ref[...], dc) + carry
            carry = dlf_ref[pl.ds(blk * SCAN_ROWS, 1), :]
            z = pf_ref[rows, :] + b_ref[...]
            e = jnp.exp(-jnp.abs(z))
            sig_neg = jnp.where(z >= 0.0, e, 1.0) / (1.0 + e)
            dz = jnp.where(lane < A_HEADS, dlf_ref[rows, :] * sig_neg, 0.0)
            dpf_ref[rows, :] = dz.astype(BF16)
            db = db + jnp.sum(dz, axis=0, keepdims=True)
        db_ref[...] = db

    return pl.pallas_call(
        body, out_shape=[jax.ShapeDtypeStruct((S, LANES), BF16), jax.ShapeDtypeStruct((1, LANES), F32)],
        scratch_shapes=[pltpu.VMEM((S, LANES), F32)],
        compiler_params=_cparams(), name=name,
    )(pf, bias, dc_key, dc_query, triu)


FOX_T = 512


def _fox_fwd(q, k, v, ccol, crow, name):
    H = q.shape[0]
    T = FOX_T
    nq = S // T

    def body(q_ref, k_ref, v_ref, cc_ref, cr_ref, o_ref, lse_ref, m_sc, l_sc, acc_sc):
        i = pl.program_id(1)
        j = pl.program_id(2)

        @pl.when(j == 0)
        def _():
            m_sc[...] = jnp.full((T, 1), NEG_INF, F32)
            l_sc[...] = jnp.zeros((T, 1), F32)
            acc_sc[...] = jnp.zeros((T, HEAD_DIM), F32)

        def step(diagonal):
            s = lax.dot_general(q_ref[...], k_ref[...], _DN["nt"], preferred_element_type=F32) * SCALE
            s = s + (cc_ref[...] - cr_ref[...])
            if diagonal:
                row = lax.broadcasted_iota(jnp.int32, (T, T), 0)
                col = lax.broadcasted_iota(jnp.int32, (T, T), 1)
                s = jnp.where(row >= col, s, NEG_INF)
            m_prev = m_sc[...]
            m_new = jnp.maximum(m_prev, jnp.max(s, axis=1, keepdims=True))
            alpha = jnp.exp(m_prev - m_new)
            p = jnp.exp(s - m_new)
            l_sc[...] = alpha * l_sc[...] + jnp.sum(p, axis=1, keepdims=True)
            acc_sc[...] = alpha * acc_sc[...] + lax.dot_general(p.astype(BF16), v_ref[...], _DN["nn"],
                                                                preferred_element_type=F32)
            m_sc[...] = m_new

        @pl.when(j < i)
        def _():
            step(False)

        @pl.when(j == i)
        def _():
            step(True)
            o_ref[...] = (acc_sc[...] / l_sc[...]).astype(BF16)
            lse_ref[...] = m_sc[...] + jnp.log(l_sc[...])

    qs = pl.BlockSpec((None, T, HEAD_DIM), lambda h, i, j: (h, i, 0))
    ks = pl.BlockSpec((None, T, HEAD_DIM), lambda h, i, j: (h, jnp.minimum(i, j), 0))
    col = pl.BlockSpec((None, T, 1), lambda h, i, j: (h, i, 0))
    rowk = pl.BlockSpec((None, 1, T), lambda h, i, j: (h, 0, jnp.minimum(i, j)))
    return pl.pallas_call(
        body, out_shape=[jax.ShapeDtypeStruct((H, S, HEAD_DIM), BF16), jax.ShapeDtypeStruct((H, S, 1), F32)],
        grid=(H, nq, nq), in_specs=[qs, ks, ks, col, rowk], out_specs=[qs, col],
        scratch_shapes=[pltpu.VMEM((T, 1), F32), pltpu.VMEM((T, 1), F32), pltpu.VMEM((T, HEAD_DIM), F32)],
        compiler_params=_cparams(("parallel", "parallel", "arbitrary")), name=name,
    )(q, k, v, ccol, crow)


def _fox_bwd(q, k, v, do, lse_row, delta_row, cq_row, ck_col, name):
    H = q.shape[0]
    T = FOX_T
    nq = S // T

    def body(q_ref, k_ref, v_ref, do_ref, lse_ref, dl_ref, cq_ref, ck_ref, dq_ref, dk_ref, dv_ref, dc_ref, dcq_ref,
             dk_sc, dv_sc, dc_sc):
        j = pl.program_id(1)
        i = pl.program_id(2)

        @pl.when(jnp.logical_and(j == 0, i == 0))
        def _():
            dq_ref[...] = jnp.zeros((S, HEAD_DIM), F32)
            dcq_ref[...] = jnp.zeros((nq, 1, T), F32)

        @pl.when(i == j)
        def _():
            dk_sc[...] = jnp.zeros((T, HEAD_DIM), F32)
            dv_sc[...] = jnp.zeros((T, HEAD_DIM), F32)
            dc_sc[...] = jnp.zeros((T, 1), F32)

        def step(diagonal):
            qv = q_ref[...]
            kv = k_ref[...]
            dov = do_ref[...]
            st = lax.dot_general(kv, qv, _DN["nt"], preferred_element_type=F32) * SCALE
            st = st + (cq_ref[...] - ck_ref[...])
            if diagonal:
                row = lax.broadcasted_iota(jnp.int32, (T, T), 0)
                col = lax.broadcasted_iota(jnp.int32, (T, T), 1)
                st = jnp.where(col >= row, st, NEG_INF)
            pt = jnp.exp(st - lse_ref[...])
            dv_sc[...] += lax.dot_general(pt.astype(BF16), dov, _DN["nn"], preferred_element_type=F32)
            dpt = lax.dot_general(v_ref[...], dov, _DN["nt"], preferred_element_type=F32)
            dst = pt * (dpt - dl_ref[...])
            dc_sc[...] -= jnp.sum(dst, axis=1, keepdims=True)
            dcq_ref[i] += jnp.sum(dst, axis=0, keepdims=True)
            dsb = (dst * SCALE).astype(BF16)
            dk_sc[...] += lax.dot_general(dsb, qv, _DN["nn"], preferred_element_type=F32)
            rows = pl.ds(pl.multiple_of(i * T, T), T)
            dq_ref[rows, :] += lax.dot_general(dsb, kv, _DN["tn"], preferred_element_type=F32)

        @pl.when(i > j)
        def _():
            step(False)

        @pl.when(i == j)
        def _():
            step(True)

        @pl.when(i == nq - 1)
        def _():
            dk_ref[...] = dk_sc[...].astype(BF16)
            dv_ref[...] = dv_sc[...].astype(BF16)
            dc_ref[...] = dc_sc[...]

    qs = pl.BlockSpec((None, T, HEAD_DIM), lambda h, j, i: (h, jnp.maximum(i, j), 0))
    qrow = pl.BlockSpec((None, 1, T), lambda h, j, i: (h, 0, jnp.maximum(i, j)))
    ks = pl.BlockSpec((None, T, HEAD_DIM), lambda h, j, i: (h, j, 0))
    kcol = pl.BlockSpec((None, T, 1), lambda h, j, i: (h, j, 0))
    dqs = pl.BlockSpec((None, S, HEAD_DIM), lambda h, j, i: (h, 0, 0))
    dcqs = pl.BlockSpec((None, nq, 1, T), lambda h, j, i: (h, 0, 0, 0))
    return pl.pallas_call(
        body,
        out_shape=[jax.ShapeDtypeStruct((H, S, HEAD_DIM), F32), jax.ShapeDtypeStruct((H, S, HEAD_DIM), BF16),
                   jax.ShapeDtypeStruct((H, S, HEAD_DIM), BF16), jax.ShapeDtypeStruct((H, S, 1), F32),
                   jax.ShapeDtypeStruct((H, nq, 1, T), F32)],
        grid=(H, nq, nq), in_specs=[qs, ks, ks, qs, qrow, qrow, qrow, kcol], out_specs=[dqs, ks, ks, kcol, dcqs],
        scratch_shapes=[pltpu.VMEM((T, HEAD_DIM), F32), pltpu.VMEM((T, HEAD_DIM), F32), pltpu.VMEM((T, 1), F32)],
        compiler_params=_cparams(("parallel", "arbitrary", "arbitrary")), name=name,
    )(q, k, v, do, lse_row, delta_row, cq_row, ck_col)


def _rowdot(a, b, name):
    H = a.shape[0]
    T = 1024

    def body(a_ref, b_ref, o_ref):
        o_ref[...] = jnp.sum(a_ref[...].astype(F32) * b_ref[...].astype(F32), axis=-1, keepdims=True)

    blk = pl.BlockSpec((None, T, HEAD_DIM), lambda h, i: (h, i, 0))
    return pl.pallas_call(
        body, out_shape=jax.ShapeDtypeStruct((H, S, 1), F32), grid=(H, S // T), in_specs=[blk, blk],
        out_specs=pl.BlockSpec((None, T, 1), lambda h, i: (h, i, 0)),
        compiler_params=_cparams(("parallel", "parallel")), name=name,
    )(a, b)


W = B_WIN
N_HG = 3 * B_HPG
N_BLK = S // W


def _dil_tables():
    slopes = np.exp2((-8.0 * np.arange(1, N_HG + 1, dtype=np.float32) / N_HG).astype(np.float32)).astype(np.float32)
    dil = np.repeat(np.array([d for _, d in B_GROUPS], np.float32), B_HPG)
    coef = (slopes * dil).astype(np.float32)
    nbs = np.repeat(np.array([S // d // W for _, d in B_GROUPS], np.int32), B_HPG)
    return jnp.asarray(coef), jnp.asarray(nbs)


def _dil_scores(q, k, coef, prev, dead, transposed):
    if transposed:
        s = lax.dot_general(k, q, _DN["nt"], preferred_element_type=F32) * SCALE
        kj = lax.broadcasted_iota(jnp.int32, (W, W), 0)
        qi = lax.broadcasted_iota(jnp.int32, (W, W), 1)
    else:
        s = lax.dot_general(q, k, _DN["nt"], preferred_element_type=F32) * SCALE
        qi = lax.broadcasted_iota(jnp.int32, (W, W), 0)
        kj = lax.broadcasted_iota(jnp.int32, (W, W), 1)
    if prev:
        dist = qi + W - kj
        valid = kj >= qi
    else:
        dist = qi - kj
        valid = qi >= kj
    s = jnp.where(valid, s - coef * dist.astype(F32), NEG_INF)
    if dead is not None:
        s = jnp.where(dead, NEG_INF, s)
    return s


def _dil_fwd(q, k, v, name):
    coef_t, nbs_t = _dil_tables()

    def body(coef_ref, nbs_ref, q_ref, kp_ref, kc_ref, vp_ref, vc_ref, o_ref, lse_ref):
        hg = pl.program_id(0)
        n = pl.program_id(1)
        coef = coef_ref[hg]
        first = lax.rem(n, nbs_ref[hg]) == 0
        qv = q_ref[...]
        sp = _dil_scores(qv, kp_ref[...], coef, True, first, False)
        sc = _dil_scores(qv, kc_ref[...], coef, False, None, False)
        m = jnp.maximum(jnp.max(sp, axis=1, keepdims=True), jnp.max(sc, axis=1, keepdims=True))
        pp = jnp.exp(sp - m)
        pc = jnp.exp(sc - m)
        l = jnp.sum(pp, axis=1, keepdims=True) + jnp.sum(pc, axis=1, keepdims=True)
        acc = (lax.dot_general(pp.astype(BF16), vp_ref[...], _DN["nn"], preferred_element_type=F32)
               + lax.dot_general(pc.astype(BF16), vc_ref[...], _DN["nn"], preferred_element_type=F32))
        o_ref[...] = acc / l
        lse_ref[...] = m + jnp.log(l)

    smem = pl.BlockSpec(memory_space=pltpu.SMEM)
    cur = pl.BlockSpec((None, W, HEAD_DIM), lambda h, n: (h, n, 0))
    prv = pl.BlockSpec((None, W, HEAD_DIM), lambda h, n: (h, jnp.maximum(n - 1, 0), 0))
    return pl.pallas_call(
        body, out_shape=[jax.ShapeDtypeStruct((N_HG, S, HEAD_DIM), F32), jax.ShapeDtypeStruct((N_HG, S, 1), F32)],
        grid=(N_HG, N_BLK), in_specs=[smem, smem, cur, prv, cur, prv, cur],
        out_specs=[cur, pl.BlockSpec((None, W, 1), lambda h, n: (h, n, 0))],
        compiler_params=_cparams(("parallel", "parallel")), name=name,
    )(coef_t, nbs_t, q, k, k, v, v)


def _dil_merge(o, lse, name):
    T = 1024

    def body(o_ref, lse_ref, om_ref, omb_ref, l_ref):
        l0, l1, l2 = lse_ref[0], lse_ref[1], lse_ref[2]
        m = jnp.maximum(jnp.maximum(l0, l1), l2)
        e0, e1, e2 = jnp.exp(l0 - m), jnp.exp(l1 - m), jnp.exp(l2 - m)
        den = e0 + e1 + e2
        om = (e0 / den) * o_ref[0] + (e1 / den) * o_ref[1] + (e2 / den) * o_ref[2]
        om_ref[...] = om
        omb_ref[...] = om.astype(BF16)
        l_ref[...] = m + jnp.log(den)

    ob = pl.BlockSpec((None, T, HEAD_DIM), lambda h, i: (h, i, 0))
    lb = pl.BlockSpec((None, T, 1), lambda h, i: (h, i, 0))
    return pl.pallas_call(
        body,
        out_shape=[jax.ShapeDtypeStruct((B_HPG, S, HEAD_DIM), F32), jax.ShapeDtypeStruct((B_HPG, S, HEAD_DIM), BF16),
                   jax.ShapeDtypeStruct((B_HPG, S, 1), F32)],
        grid=(B_HPG, S // T),
        in_specs=[pl.BlockSpec((3, None, T, HEAD_DIM), lambda h, i: (0, h, i, 0)),
                  pl.BlockSpec((3, None, T, 1), lambda h, i: (0, h, i, 0))],
        out_specs=[ob, ob, lb], compiler_params=_cparams(("parallel", "parallel")), name=name,
    )(o, lse)


def _dil_bwd_dq(q, k, v, do, lcol, dcol, name):
    coef_t, nbs_t = _dil_tables()

    def body(coef_ref, nbs_ref, q_ref, kp_ref, kc_ref, vp_ref, vc_ref, do_ref, l_ref, d_ref, dq_ref):
        hg = pl.program_id(0)
        n = pl.program_id(1)
        coef = coef_ref[hg]
        first = lax.rem(n, nbs_ref[hg]) == 0
        qv = q_ref[...]
        dov = do_ref[...]
        acc = jnp.zeros((W, HEAD_DIM), F32)
        for k_ref, v_ref, prev in ((kp_ref, vp_ref, True), (kc_ref, vc_ref, False)):
            kv = k_ref[...]
            s = _dil_scores(qv, kv, coef, prev, first if prev else None, False)
            p = jnp.exp(s - l_ref[...])
            dp = lax.dot_general(dov, v_ref[...], _DN["nt"], preferred_element_type=F32)
            ds = (p * (dp - d_ref[...]) * SCALE).astype(BF16)
            acc = acc + lax.dot_general(ds, kv, _DN["nn"], preferred_element_type=F32)
        dq_ref[...] = acc.astype(BF16)

    smem = pl.BlockSpec(memory_space=pltpu.SMEM)
    cur = pl.BlockSpec((None, W, HEAD_DIM), lambda h, n: (h, n, 0))
    prv = pl.BlockSpec((None, W, HEAD_DIM), lambda h, n: (h, jnp.maximum(n - 1, 0), 0))
    col = pl.BlockSpec((None, W, 1), lambda h, n: (h, n, 0))
    return pl.pallas_call(
        body, out_shape=jax.ShapeDtypeStruct((N_HG, S, HEAD_DIM), BF16), grid=(N_HG, N_BLK),
        in_specs=[smem, smem, cur, prv, cur, prv, cur, cur, col, col], out_specs=cur,
        compiler_params=_cparams(("parallel", "parallel")), name=name,
    )(coef_t, nbs_t, q, k, k, v, v, do, lcol, dcol)


def _dil_bwd_dkv(q, k, v, do, lrow, drow, name):
    coef_t, nbs_t = _dil_tables()

    def body(coef_ref, nbs_ref, k_ref, v_ref, qc_ref, qn_ref, doc_ref, don_ref, lc_ref, ln_ref, dc_ref, dn_ref,
             dk_ref, dv_ref):
        hg = pl.program_id(0)
        m = pl.program_id(1)
        coef = coef_ref[hg]
        no_next = lax.rem(m + 1, nbs_ref[hg]) == 0
        kv = k_ref[...]
        vv = v_ref[...]
        dk = jnp.zeros((W, HEAD_DIM), F32)
        dv = jnp.zeros((W, HEAD_DIM), F32)
        for q_ref, do_ref, l_ref, d_ref, prev in ((qc_ref, doc_ref, lc_ref, dc_ref, False),
                                                  (qn_ref, don_ref, ln_ref, dn_ref, True)):
            qv = q_ref[...]
            dov = do_ref[...]
            st = _dil_scores(qv, kv, coef, prev, no_next if prev else None, True)
            pt = jnp.exp(st - l_ref[...])
            dv = dv + lax.dot_general(pt.astype(BF16), dov, _DN["nn"], preferred_element_type=F32)
            dpt = lax.dot_general(vv, dov, _DN["nt"], preferred_element_type=F32)
            dst = (pt * (dpt - d_ref[...]) * SCALE).astype(BF16)
            dk = dk + lax.dot_general(dst, qv, _DN["nn"], preferred_element_type=F32)
        dk_ref[...] = dk.astype(BF16)
        dv_ref[...] = dv.astype(BF16)

    smem = pl.BlockSpec(memory_space=pltpu.SMEM)
    cur = pl.BlockSpec((None, W, HEAD_DIM), lambda h, m: (h, m, 0))
    nxt = pl.BlockSpec((None, W, HEAD_DIM), lambda h, m: (h, jnp.minimum(m + 1, N_BLK - 1), 0))
    rcur = pl.BlockSpec((None, 1, W), lambda h, m: (h, 0, m))
    rnxt = pl.BlockSpec((None, 1, W), lambda h, m: (h, 0, jnp.minimum(m + 1, N_BLK - 1)))
    return pl.pallas_call(
        body, out_shape=[jax.ShapeDtypeStruct((N_HG, S, HEAD_DIM), BF16)] * 2, grid=(N_HG, N_BLK),
        in_specs=[smem, smem, cur, cur, cur, nxt, cur, nxt, rcur, rnxt, rcur, rnxt], out_specs=[cur, cur],
        compiler_params=_cparams(("parallel", "parallel")), name=name,
    )(coef_t, nbs_t, k, v, q, q, do, do, lrow, lrow, drow, drow)


FFN_ROWS = 512
FFN_COLS = 256
HALO = 8


def _shifted(u, halo, back):
    T = u.shape[0]
    rows = lax.broadcasted_iota(jnp.int32, u.shape, 0)
    if back:
        s1 = jnp.where(rows == 0, halo[HALO - 1:HALO, :], pltpu.roll(u, 1, 0))
        s2 = jnp.where(rows == 0, halo[HALO - 2:HALO - 1, :],
                       jnp.where(rows == 1, halo[HALO - 1:HALO, :], pltpu.roll(u, 2, 0)))
    else:
        s1 = jnp.where(rows == T - 1, halo[0:1, :], pltpu.roll(u, T - 1, 0))
        s2 = jnp.where(rows == T - 1, halo[1:2, :],
                       jnp.where(rows == T - 2, halo[0:1, :], pltpu.roll(u, T - 2, 0)))
    return s1, s2


def _conv_parts(u_ref, h_ref, w_ref, b_ref, first):
    out = []
    for p in range(2):
        u = u_ref[p]
        halo = jnp.where(first, 0.0, h_ref[p])
        u1, u2 = _shifted(u, halo, True)
        w = w_ref[p]
        out.append((w[0:1, :] * u2 + w[1:2, :] * u1 + w[2:3, :] * u + b_ref[p], u1, u2, u))
    return out


def _ffn_specs():
    T, C = FFN_ROWS, FFN_COLS
    blk = pl.BlockSpec((2, T, C), lambda j, i: (0, i, j))
    prev = pl.BlockSpec((2, HALO, C), lambda j, i: (0, jnp.maximum(i * (T // HALO) - 1, 0), j))
    nxt = pl.BlockSpec((2, HALO, C), lambda j, i: (0, jnp.minimum((i + 1) * (T // HALO), S // HALO - 1), j))
    wsp = pl.BlockSpec((2, 3, C), lambda j, i: (0, 0, j))
    bsp = pl.BlockSpec((2, 1, C), lambda j, i: (0, 0, j))
    one = pl.BlockSpec((T, C), lambda j, i: (i, j))
    return blk, prev, nxt, wsp, bsp, one


def _ffn_act_fwd(u, w, b, name):
    blk, prev, _, wsp, bsp, one = _ffn_specs()

    def body(u_ref, h_ref, w_ref, b_ref, o_ref):
        (a, _, _, _), (g, _, _, _) = _conv_parts(u_ref, h_ref, w_ref, b_ref, pl.program_id(1) == 0)
        o_ref[...] = (g / (1.0 + jnp.exp(-g)) * a).astype(BF16)

    return pl.pallas_call(
        body, out_shape=jax.ShapeDtypeStruct((S, FF), BF16), grid=(FF // FFN_COLS, S // FFN_ROWS),
        in_specs=[blk, prev, wsp, bsp], out_specs=one,
        compiler_params=_cparams(("parallel", "parallel")), name=name,
    )(u, u, w, b)


def _ffn_act_bwd(u, dact, w, b, name):
    blk, prev, _, wsp, bsp, one = _ffn_specs()

    def body(u_ref, h_ref, da_ref, w_ref, b_ref, duc_ref, dwb_ref):
        i = pl.program_id(1)
        (a, a1, a2, a0), (g, g1, g2, g0) = _conv_parts(u_ref, h_ref, w_ref, b_ref, i == 0)
        dact_v = da_ref[...]
        sg = 1.0 / (1.0 + jnp.exp(-g))
        d_a = dact_v * (g * sg)
        d_g = dact_v * a * (sg * (1.0 + g * (1.0 - sg)))
        duc_ref[0] = d_a
        duc_ref[1] = d_g

        @pl.when(i == 0)
        def _():
            dwb_ref[...] = jnp.zeros(dwb_ref.shape, F32)

        for p, (d, s2, s1, s0) in enumerate(((d_a, a2, a1, a0), (d_g, g2, g1, g0))):
            dwb_ref[p, 0:1, :] += jnp.sum(d * s2, axis=0, keepdims=True)
            dwb_ref[p, 1:2, :] += jnp.sum(d * s1, axis=0, keepdims=True)
            dwb_ref[p, 2:3, :] += jnp.sum(d * s0, axis=0, keepdims=True)
            dwb_ref[p, 3:4, :] += jnp.sum(d, axis=0, keepdims=True)

    return pl.pallas_call(
        body, out_shape=[jax.ShapeDtypeStruct((2, S, FF), F32), jax.ShapeDtypeStruct((2, 8, FF), F32)],
        grid=(FF // FFN_COLS, S // FFN_ROWS), in_specs=[blk, prev, one, wsp, bsp],
        out_specs=[blk, pl.BlockSpec((2, 8, FFN_COLS), lambda j, i: (0, 0, j))],
        compiler_params=_cparams(("parallel", "arbitrary")), name=name,
    )(u, u, dact, w, b)


def _ffn_conv_bwd(duc, w, name):
    blk, _, nxt, wsp, _, _ = _ffn_specs()
    last = S // FFN_ROWS - 1

    def body(d_ref, h_ref, w_ref, du_ref):
        is_last = pl.program_id(1) == last
        for p in range(2):
            d = d_ref[p]
            halo = jnp.where(is_last, 0.0, h_ref[p])
            d1, d2 = _shifted(d, halo, False)
            wv = w_ref[p]
            du_ref[p] = (wv[2:3, :] * d + wv[1:2, :] * d1 + wv[0:1, :] * d2).astype(BF16)

    return pl.pallas_call(
        body, out_shape=jax.ShapeDtypeStruct((2, S, FF), BF16), grid=(FF // FFN_COLS, S // FFN_ROWS),
        in_specs=[blk, nxt, wsp], out_specs=blk,
        compiler_params=_cparams(("parallel", "parallel")), name=name,
    )(duc, duc, w)


def _adamw(w, g, m, v, name):
    rows = w.shape[0]
    T = 8
    for cand in (256, 128, 64, 32, 16, 8):
        if rows % cand == 0:
            T = cand
            break
    c1 = 1.0 / (1.0 - ADAM_B1 ** ADAM_STEP)
    c2 = 1.0 / (1.0 - ADAM_B2 ** ADAM_STEP)

    def body(w_ref, g_ref, m_ref, v_ref, d_ref, mo_ref, vo_ref):
        gv = g_ref[...]
        mn = ADAM_B1 * m_ref[...] + (1.0 - ADAM_B1) * gv
        vn = ADAM_B2 * v_ref[...] + (1.0 - ADAM_B2) * (gv * gv)
        d_ref[...] = -ADAM_LR * ((mn * c1) / (jnp.sqrt(vn * c2) + ADAM_EPS) + ADAM_WD * w_ref[...])
        mo_ref[...] = mn
        vo_ref[...] = vn

    blk = pl.BlockSpec((T, 1024), lambda i: (i, 0))
    sds = jax.ShapeDtypeStruct(w.shape, F32)
    return pl.pallas_call(
        body, out_shape=[sds, sds, sds], grid=(rows // T,), in_specs=[blk] * 4, out_specs=[blk] * 3,
        compiler_params=_cparams(("parallel",)), name=name,
    )(w, g, m, v)


ANY = pl.BlockSpec(memory_space=pl.ANY)


def _place():
    x, y, c = lax.axis_index("x"), lax.axis_index("y"), lax.axis_index("c")
    chips = [(1 - x, y), (x, 1 - y), (1 - x, 1 - y)]
    return x, y, c, chips


def _allgather_shards(w):
    def body(w_ref, g_ref, send_sems, recv_sems, local_sem):
        x, y, c, chips = _place()
        myk = 2 * x + y
        sibling = (x, y, 1 - c)
        h0 = pl.multiple_of(c * HALF_ROWS, 16)
        h1 = pl.multiple_of((1 - c) * HALF_ROWS, 16)

        def half(k, start):
            return g_ref.at[k, pl.ds(start, HALF_ROWS), :]

        def rcopy(sem, src, dst, to):
            return pltpu.make_async_remote_copy(src_ref=src, dst_ref=dst, send_sem=send_sems.at[sem],
                                                recv_sem=recv_sems.at[sem], device_id=to, device_id_type=MESH)

        mine = pltpu.make_async_copy(w_ref, g_ref.at[myk], local_sem)
        mine.start()
        ici = [rcopy(r, w_ref.at[pl.ds(h0, HALF_ROWS), :], half(myk, h0), (*chip, c)) for r, chip in enumerate(chips)]
        for cp in ici:
            cp.start()
        ks = [2 * cx + cy for cx, cy in chips]
        fwd = [rcopy(3 + r, half(ks[r], h0), half(ks[r], h0), sibling) for r in range(3)]
        for r in range(3):
            rcopy(r, half(ks[r], h0), half(ks[r], h0), (*chips[r], c)).wait_recv()
            fwd[r].start()
        for r in range(3):
            rcopy(3 + r, half(ks[r], h1), half(ks[r], h1), sibling).wait_recv()
        for cp in ici + fwd:
            cp.wait_send()
        mine.wait()

    return pl.pallas_call(
        body, out_shape=jax.ShapeDtypeStruct((N_CHIPS, FLAT_ROWS, 1024), w.dtype), in_specs=[ANY], out_specs=ANY,
        scratch_shapes=[pltpu.SemaphoreType.DMA((6,)), pltpu.SemaphoreType.DMA((6,)), pltpu.SemaphoreType.DMA],
        compiler_params=pltpu.CompilerParams(has_side_effects=True), name="allgather_shards",
    )(w)


def _sibling_swap_half(g):
    def body(g_ref, o_ref, send_sem, recv_sem):
        x, y, c, _ = _place()
        theirs = pl.multiple_of((1 - c) * HALF_ROWS, 8)
        cp = pltpu.make_async_remote_copy(src_ref=g_ref.at[:, pl.ds(theirs, HALF_ROWS), :], dst_ref=o_ref,
                                          send_sem=send_sem, recv_sem=recv_sem, device_id=(x, y, 1 - c),
                                          device_id_type=MESH)
        cp.start()
        cp.wait()

    return pl.pallas_call(
        body, out_shape=jax.ShapeDtypeStruct((N_CHIPS, HALF_ROWS, 1024), F32), in_specs=[ANY], out_specs=ANY,
        scratch_shapes=[pltpu.SemaphoreType.DMA, pltpu.SemaphoreType.DMA],
        compiler_params=pltpu.CompilerParams(has_side_effects=True), name="rs_sibling_swap",
    )(g)


def _pair_sum(g, other, c_arr):
    T = 256

    def body(c_ref, g_ref, o_ref, s_ref):
        s_ref[...] = (g_ref[...] + o_ref[...]).astype(BF16)

    nb = HALF_ROWS // T
    return pl.pallas_call(
        body, out_shape=jax.ShapeDtypeStruct((N_CHIPS, HALF_ROWS, 1024), BF16),
        grid_spec=pltpu.PrefetchScalarGridSpec(
            num_scalar_prefetch=1, grid=(N_CHIPS, nb),
            in_specs=[pl.BlockSpec((None, T, 1024), lambda k, i, c: (k, c[0] * nb + i, 0)),
                      pl.BlockSpec((None, T, 1024), lambda k, i, c: (k, i, 0))],
            out_specs=pl.BlockSpec((None, T, 1024), lambda k, i, c: (k, i, 0))),
        compiler_params=_cparams(("parallel", "parallel")), name="rs_pair_sum",
    )(c_arr, g, other)


def _chip_exchange(s):
    def body(s_ref, o_ref, send_sems, recv_sems, local_sem):
        x, y, c, chips = _place()
        myk = 2 * x + y
        mine = pltpu.make_async_copy(s_ref.at[myk], o_ref.at[myk], local_sem)
        mine.start()
        cps = []
        for r, (cx, cy) in enumerate(chips):
            cps.append(pltpu.make_async_remote_copy(
                src_ref=s_ref.at[2 * cx + cy], dst_ref=o_ref.at[myk], send_sem=send_sems.at[r],
                recv_sem=recv_sems.at[r], device_id=(cx, cy, c), device_id_type=MESH))
        for cp in cps:
            cp.start()
        for r, (cx, cy) in enumerate(chips):
            pltpu.make_async_remote_copy(
                src_ref=s_ref.at[myk], dst_ref=o_ref.at[2 * cx + cy], send_sem=send_sems.at[r],
                recv_sem=recv_sems.at[r], device_id=(cx, cy, c), device_id_type=MESH).wait_recv()
        for cp in cps:
            cp.wait_send()
        mine.wait()

    return pl.pallas_call(
        body, out_shape=jax.ShapeDtypeStruct((N_CHIPS, HALF_ROWS, 1024), BF16), in_specs=[ANY], out_specs=ANY,
        scratch_shapes=[pltpu.SemaphoreType.DMA((3,)), pltpu.SemaphoreType.DMA((3,)), pltpu.SemaphoreType.DMA],
        compiler_params=pltpu.CompilerParams(has_side_effects=True), name="rs_chip_exchange",
    )(s)


def _chip_sum(r):
    T = 256

    def body(r_ref, o_ref):
        o_ref[...] = ((r_ref[0].astype(F32) + r_ref[1].astype(F32)) + r_ref[2].astype(F32)) + r_ref[3].astype(F32)

    return pl.pallas_call(
        body, out_shape=jax.ShapeDtypeStruct((HALF_ROWS, 1024), F32), grid=(HALF_ROWS // T,),
        in_specs=[pl.BlockSpec((N_CHIPS, T, 1024), lambda i: (0, i, 0))],
        out_specs=pl.BlockSpec((T, 1024), lambda i: (i, 0)),
        compiler_params=_cparams(("parallel",)), name="rs_chip_sum",
    )(r)


def _sibling_join(t):
    def body(t_ref, o_ref, send_sem, recv_sem, local_sem):
        x, y, c, _ = _place()
        mine_rows = pl.multiple_of(c * HALF_ROWS, 8)
        their_rows = pl.multiple_of((1 - c) * HALF_ROWS, 8)
        mine = pltpu.make_async_copy(t_ref, o_ref.at[pl.ds(mine_rows, HALF_ROWS), :], local_sem)
        mine.start()
        cp = pltpu.make_async_remote_copy(src_ref=t_ref, dst_ref=o_ref.at[pl.ds(mine_rows, HALF_ROWS), :],
                                          send_sem=send_sem, recv_sem=recv_sem, device_id=(x, y, 1 - c),
                                          device_id_type=MESH)
        cp.start()
        pltpu.make_async_remote_copy(src_ref=t_ref, dst_ref=o_ref.at[pl.ds(their_rows, HALF_ROWS), :],
                                     send_sem=send_sem, recv_sem=recv_sem, device_id=(x, y, 1 - c),
                                     device_id_type=MESH).wait_recv()
        cp.wait_send()
        mine.wait()

    return pl.pallas_call(
        body, out_shape=jax.ShapeDtypeStruct((FLAT_ROWS, 1024), F32), in_specs=[ANY], out_specs=ANY,
        scratch_shapes=[pltpu.SemaphoreType.DMA, pltpu.SemaphoreType.DMA, pltpu.SemaphoreType.DMA],
        compiler_params=pltpu.CompilerParams(has_side_effects=True), name="rs_sibling_join",
    )(t)


def _allreduce_small(v):
    def body(v_ref, o_ref, buf, send_sems, recv_sems):
        x, y, c, _ = _place()
        me = 4 * x + 2 * y + c
        buf[me] = v_ref[...]
        cps = []
        for mask in range(1, 8):
            a, b, d = (mask >> 2) & 1, (mask >> 1) & 1, mask & 1
            peer = (x + a - 2 * a * x, y + b - 2 * b * y, c + d - 2 * d * c)
            cps.append(pltpu.make_async_remote_copy(
                src_ref=v_ref, dst_ref=buf.at[me], send_sem=send_sems.at[mask - 1], recv_sem=recv_sems.at[mask - 1],
                device_id=peer, device_id_type=MESH))
        for cp in cps:
            cp.start()
        for cp in cps:
            cp.wait()
        total = buf[0]
        for dev in range(1, 8):
            total = total + buf[dev]
        o_ref[...] = total

    vm = pl.BlockSpec(memory_space=pltpu.VMEM)
    return pl.pallas_call(
        body, out_shape=jax.ShapeDtypeStruct((SMALL_ROWS, 1024), F32), in_specs=[vm], out_specs=vm,
        scratch_shapes=[pltpu.VMEM((8, SMALL_ROWS, 1024), F32), pltpu.SemaphoreType.DMA((7,)),
                        pltpu.SemaphoreType.DMA((7,))],
        compiler_params=pltpu.CompilerParams(has_side_effects=True), name="allreduce_small",
    )(v)


def _heads(t, n):
    return t.reshape(S, n, HEAD_DIM).transpose(1, 0, 2)


def _unheads(t):
    return t.transpose(1, 0, 2).reshape(S, t.shape[0] * HEAD_DIM)


def _to_residue(t, d):
    c = t.shape[-1]
    return t.reshape(B_HPG, S // d, d, c).transpose(0, 2, 1, 3).reshape(B_HPG, S, c)


def _from_residue(t, d):
    c = t.shape[-1]
    return t.reshape(B_HPG, d, S // d, c).transpose(0, 2, 1, 3).reshape(B_HPG, S, c)


def _dil_pack(t):
    return jnp.concatenate([_to_residue(t[g], d) for g, (_, d) in enumerate(B_GROUPS)], axis=0)


def _dil_unpack(t):
    return jnp.stack([_from_residue(t[g * B_HPG:(g + 1) * B_HPG], d) for g, (_, d) in enumerate(B_GROUPS)], axis=0)


def _col_to_row(t):
    return t.reshape(t.shape[0], 1, S)


def _ffn_fwd(x, g, w_up, cw, cb, w_down, tag):
    h = _rms_fwd(x, g, f"{tag}_norm")
    u = _mm(h, w_up, mode="nn", tm=1024, tn=1408, tk=1024, o_split=2, name=f"{tag}_up")
    act = _ffn_act_fwd(u, cw, cb, f"{tag}_act")
    x_out = _mm(act, w_down, mode="nn", tm=1024, tn=512, tk=FF, res=x, name=f"{tag}_down")
    return x_out, (h, u, act)


def _ffn_bwd(x, g, w_up, cw, cb, w_down, saved, dx, dxb, tag):
    h, u, act = saved
    d_w_down = _mm(act, dxb, mode="tn", tm=1408, tn=512, tk=1024, name=f"{tag}_dwdown")
    dact = _mm(dxb, w_down, mode="nt", tm=1024, tn=1408, tk=1024, name=f"{tag}_dact")
    duc, dwb = _ffn_act_bwd(u, dact, cw, cb, f"{tag}_dgate")
    du = _ffn_conv_bwd(duc, cw, f"{tag}_dconv")
    d_w_up = _mm(h, du, mode="tn", tm=1024, tn=1408, tk=1024, b_split=2, name=f"{tag}_dwup")
    dh = _mm(du, w_up, mode="nt", tm=1024, tn=512, tk=1408, a_split=2, name=f"{tag}_dh")
    dx_new, dxb_new, (dg,) = _rms_bwd(x, dx, [(g, dh)], f"{tag}_dnorm")
    d_cw = dwb[:, 0:3, :].transpose(1, 0, 2).reshape(3, 2 * FF)
    d_cb = dwb[:, 3, :].reshape(2 * FF)
    return dx_new, dxb_new, dict(w_up=d_w_up, w_down=d_w_down, conv_w=d_cw, conv_b=d_cb, norm_g=dg.reshape(D))


def _local_step(x, target, p):
    g = {}
    h1 = _rms_fwd(x, p["mix_norm_g"][0], "a_norm")
    w_qkv = p["a_w_in"][:, :QKV_W]
    w_f = jnp.pad(p["a_w_in"][:, QKV_W:], ((0, 0), (0, LANES - A_HEADS)))
    b_f = jnp.pad(p["a_b_f"].reshape(1, A_HEADS), ((0, 0), (0, LANES - A_HEADS)))
    qkv = _mm(h1, w_qkv, mode="nn", tm=1024, tn=512, tk=1024, out_dtype=BF16, name="a_qkv")
    pf = _mm(h1, w_f, mode="nn", tm=1024, tn=LANES, tk=1024, name="a_gate")
    cum = _fgate_fwd(pf, b_f, "a_gate_scan")
    qkv_h = _heads(qkv, 3 * A_HEADS)
    qa, ka, va = qkv_h[:A_HEADS], qkv_h[A_HEADS:2 * A_HEADS], qkv_h[2 * A_HEADS:]
    c_hs = cum[:, :A_HEADS].T
    c_col, c_row = c_hs.reshape(A_HEADS, S, 1), c_hs.reshape(A_HEADS, 1, S)
    oa, lse_a = _fox_fwd(qa, ka, va, c_col, c_row, "a_attn")
    oa2 = _unheads(oa)
    x1 = _mm(oa2, p["a_w_out"], mode="nn", tm=1024, tn=512, tk=1024, res=x, name="a_out")
    x2, ffn0 = _ffn_fwd(x1, p["ffn_norm_g"][0], p["ffn_w_up"][0], p["conv_w"][0], p["conv_b"][0], p["ffn_w_down"][0], "f0")
    hk = _rms_fwd(x2, p["kv_norm_g"], "kv_norm")
    kvb = _mm(hk, p["w_kv"], mode="nn", tm=1024, tn=512, tk=1024, out_dtype=BF16, name="kv_proj")
    h3 = _rms_fwd(x2, p["mix_norm_g"][1], "b_norm")
    qb = _mm(h3, p["b_w_q"], mode="nn", tm=1024, tn=512, tk=1024, out_dtype=BF16, name="b_q")
    qd = _dil_pack(_heads(qb, N_HG).reshape(3, B_HPG, S, HEAD_DIM))
    kv_h = _heads(kvb, 2 * N_HG)
    kd = _dil_pack(kv_h[:N_HG].reshape(3, B_HPG, S, HEAD_DIM))
    vd = _dil_pack(kv_h[N_HG:].reshape(3, B_HPG, S, HEAD_DIM))
    od, lsed = _dil_fwd(qd, kd, vd, "b_attn")
    ob, obb, lse_b = _dil_merge(_dil_unpack(od), _dil_unpack(lsed), "b_merge")
    ob2 = _unheads(obb)
    x3 = _mm(ob2, p["b_w_out"], mode="nn", tm=1024, tn=512, tk=B_OUT_W, res=x2, name="b_out")
    x4, ffn1 = _ffn_fwd(x3, p["ffn_norm_g"][1], p["ffn_w_up"][1], p["conv_w"][1], p["conv_b"][1], p["ffn_w_down"][1], "f1")
    loss, dx, dxb, dg_final = _loss_head(x4, p["final_norm_g"], target, "loss_head")
    g["final_norm_g"] = dg_final.reshape(D)

    dx, dxb, gf1 = _ffn_bwd(x3, p["ffn_norm_g"][1], p["ffn_w_up"][1], p["conv_w"][1], p["conv_b"][1], p["ffn_w_down"][1],
                            ffn1, dx, dxb, "f1")
    g["b_w_out"] = _mm(ob2, dxb, mode="tn", tm=B_OUT_W, tn=512, tk=1024, name="b_dwout")
    dob = _heads(_mm(dxb, p["b_w_out"], mode="nt", tm=1024, tn=B_OUT_W, tk=1024, name="b_do"), B_HPG)
    delta_b = _rowdot(dob, ob, "b_delta")
    rep = lambda t: jnp.broadcast_to(t[None], (3,) + t.shape)
    dod = _dil_pack(rep(dob.astype(BF16)))
    l_d = _dil_pack(rep(lse_b))
    dl_d = _dil_pack(rep(delta_b))
    dqd = _dil_bwd_dq(qd, kd, vd, dod, l_d, dl_d, "b_dq")
    dkd, dvd = _dil_bwd_dkv(qd, kd, vd, dod, _col_to_row(l_d), _col_to_row(dl_d), "b_dkv")
    dqb = _unheads(_dil_unpack(dqd).reshape(N_HG, S, HEAD_DIM))
    dkvb = _unheads(jnp.concatenate([_dil_unpack(dkd).reshape(N_HG, S, HEAD_DIM),
                                     _dil_unpack(dvd).reshape(N_HG, S, HEAD_DIM)], axis=0))
    g["b_w_q"] = _mm(h3, dqb, mode="tn", tm=1024, tn=512, tk=1024, name="b_dwq")
    dh3 = _mm(dqb, p["b_w_q"], mode="nt", tm=1024, tn=512, tk=B_Q_W, name="b_dh")
    g["w_kv"] = _mm(hk, dkvb, mode="tn", tm=1024, tn=512, tk=1024, name="kv_dw")
    dhk = _mm(dkvb, p["w_kv"], mode="nt", tm=1024, tn=512, tk=1536, name="kv_dh")
    dx, dxb, (dg_mix1, dg_kv) = _rms_bwd(x2, dx, [(p["mix_norm_g"][1], dh3), (p["kv_norm_g"], dhk)], "b_dnorm")
    g["kv_norm_g"] = dg_kv.reshape(D)
    dx, dxb, gf0 = _ffn_bwd(x1, p["ffn_norm_g"][0], p["ffn_w_up"][0], p["conv_w"][0], p["conv_b"][0], p["ffn_w_down"][0],
                            ffn0, dx, dxb, "f0")
    g["a_w_out"] = _mm(oa2, dxb, mode="tn", tm=1024, tn=512, tk=1024, name="a_dwout")
    doa = _heads(_mm(dxb, p["a_w_out"], mode="nt", tm=1024, tn=512, tk=1024, name="a_do"), A_HEADS)
    delta_a = _rowdot(doa, oa, "a_delta")
    dqa, dka, dva, dck, dcq = _fox_bwd(qa, ka, va, doa.astype(BF16), _col_to_row(lse_a), _col_to_row(delta_a), c_row,
                                       c_col, "a_dattn")
    dqkv = _unheads(jnp.concatenate([dqa.astype(BF16), dka, dva], axis=0))
    pad_heads = lambda t: jnp.pad(t.reshape(A_HEADS, S).T, ((0, 0), (0, LANES - A_HEADS)))
    dpf, db_f = _fgate_bwd(pf, b_f, pad_heads(dck), pad_heads(dcq), "a_dgate_scan")
    g["a_b_f"] = db_f[:, :A_HEADS]
    d_w_qkv = _mm(h1, dqkv, mode="tn", tm=1024, tn=512, tk=1024, name="a_dwqkv")
    d_w_f = _mm(h1, dpf, mode="tn", tm=1024, tn=LANES, tk=1024, name="a_dwgate")
    g["a_w_in"] = jnp.concatenate([d_w_qkv, d_w_f[:, :A_HEADS]], axis=1)
    dh1 = _mm(dqkv, w_qkv, mode="nt", tm=1024, tn=512, tk=1536, name="a_dh")
    dh1 = _mm(dpf, w_f, mode="nt", tm=1024, tn=512, tk=LANES, res=dh1, name="a_dh_gate")
    dx, _, (dg_mix0,) = _rms_bwd(x, dx, [(p["mix_norm_g"][0], dh1)], "a_dnorm")

    g["mix_norm_g"] = jnp.stack([dg_mix0.reshape(D), dg_mix1.reshape(D)])
    g["ffn_norm_g"] = jnp.stack([gf0["norm_g"], gf1["norm_g"]])
    g["ffn_w_up"] = jnp.stack([gf0["w_up"], gf1["w_up"]])
    g["ffn_w_down"] = jnp.stack([gf0["w_down"], gf1["w_down"]])
    g["ffn_conv_w"] = jnp.stack([gf0["conv_w"], gf1["conv_w"]])
    g["ffn_conv_b"] = jnp.stack([gf0["conv_b"], gf1["conv_b"]])
    return loss[0, 0], dx, g


_SHARD_SHAPES = {"a_w_in": (1, 1024, 772), "a_w_out": (1, 256, 1024), "b_w_q": (1, 1024, 384), "b_w_out": (1, 512, 256),
                 "w_kv": (1024, 768), "ffn_w_up": (2, 1024, 1408), "ffn_w_down": (2, 704, 1024), "ffn_conv_w": (2, 3, 1408)}
_SHARD_AXIS = {"a_w_in": 2, "a_w_out": 1, "b_w_q": 2, "b_w_out": 2, "w_kv": 1, "ffn_w_up": 2, "ffn_w_down": 1,
               "ffn_conv_w": 2}
_SMALL = (("kv_norm_g", (1024,)), ("mix_norm_g", (2, 1024)), ("ffn_norm_g", (2, 1024)), ("final_norm_g", (1024,)),
          ("a_b_f", (1, 16)), ("ffn_conv_b", (2, 5632)))


def _rows(t, pad_to):
    flat = t.reshape(-1)
    flat = jnp.pad(flat, (0, pad_to * 1024 - flat.shape[0]))
    return flat.reshape(pad_to, 1024)


def _pack_flat(shards, dtype):
    parts = []
    for name, _, padded in _SEGS:
        t = shards[name].astype(F32)
        if dtype != U16:
            parts.append(_rows(t, padded).astype(dtype))
        elif name == "ffn_conv_w":
            bits = lax.bitcast_convert_type(_rows(t, padded // 2), U16)
            parts.append(bits.reshape(padded // 2, 2048).reshape(padded, 1024))
        else:
            parts.append(lax.bitcast_convert_type(_rows(t, padded).astype(BF16), U16))
    used = sum(p for _, _, p in _SEGS)
    parts.append(jnp.zeros((FLAT_ROWS - used, 1024), dtype))
    return jnp.concatenate(parts, axis=0)


def _unpack_flat(flat):
    out = {}
    r0 = 0
    for name, _, padded in _SEGS:
        seg = flat[r0:r0 + padded]
        r0 += padded
        shape = _SHARD_SHAPES[name]
        n = int(np.prod(shape))
        if flat.dtype == U16 and name == "ffn_conv_w":
            seg = lax.bitcast_convert_type(seg.reshape(padded // 2, 2048).reshape(padded // 2, 1024, 2), F32)
        elif flat.dtype == U16:
            seg = lax.bitcast_convert_type(seg, BF16)
        out[name] = seg.reshape(-1)[:n].reshape(shape)
    return out


def _pack_small(vals, loss=None):
    parts = [vals[name].astype(F32).reshape(-1) for name, _ in _SMALL]
    if loss is not None:
        parts.append(loss.reshape(1))
    flat = jnp.concatenate(parts)
    return jnp.pad(flat, (0, SMALL_ROWS * 1024 - flat.shape[0])).reshape(SMALL_ROWS, 1024)


def _unpack_small(flat):
    flat = flat.reshape(-1)
    out = {}
    o = 0
    for name, shape in _SMALL:
        n = int(np.prod(shape))
        out[name] = flat[o:o + n].reshape(shape)
        o += n
    return out, flat[o]


def _full_weights(gathered):
    per_chip = [_unpack_flat(gathered[k]) for k in range(N_CHIPS)]
    full = {name: jnp.concatenate([pc[name] for pc in per_chip], axis=_SHARD_AXIS[name]) for name in _SHARD_SHAPES}
    cw = full["ffn_conv_w"].reshape(2, 3, 2, FF).transpose(0, 2, 1, 3)
    return dict(a_w_in=full["a_w_in"][0], a_w_out=full["a_w_out"][0], b_w_q=full["b_w_q"][0], b_w_out=full["b_w_out"][0],
                w_kv=full["w_kv"], ffn_w_up=full["ffn_w_up"], ffn_w_down=full["ffn_w_down"], conv_w=cw)


def _shard_grads(g):
    full = {"a_w_in": g["a_w_in"][None], "a_w_out": g["a_w_out"][None], "b_w_q": g["b_w_q"][None],
            "b_w_out": g["b_w_out"][None], "w_kv": g["w_kv"], "ffn_w_up": g["ffn_w_up"], "ffn_w_down": g["ffn_w_down"],
            "ffn_conv_w": g["ffn_conv_w"]}
    flats = []
    for k in range(N_CHIPS):
        shards = {}
        for name, shape in _SHARD_SHAPES.items():
            ax = _SHARD_AXIS[name]
            n = shape[ax]
            shards[name] = lax.slice_in_dim(full[name], k * n, (k + 1) * n, axis=ax)
        flats.append(_pack_flat(shards, F32))
    return jnp.stack(flats, axis=0)


_WEIGHTS = ["a_w_in", "a_b_f", "a_w_out", "b_w_q", "b_w_out", "kv_norm_g", "w_kv", "mix_norm_g", "ffn_norm_g", "ffn_w_up",
            "ffn_conv_w", "ffn_conv_b", "ffn_w_down", "final_norm_g"]


def kernel(x, a_w_in, a_b_f, a_w_out, b_w_q, b_w_out, kv_norm_g, w_kv, mix_norm_g, ffn_norm_g, ffn_w_up, ffn_conv_w, ffn_conv_b, ffn_w_down, final_norm_g, loss_target, m_a_w_in, m_a_b_f, m_a_w_out, m_b_w_q, m_b_w_out, m_kv_norm_g, m_w_kv, m_mix_norm_g, m_ffn_norm_g, m_ffn_w_up, m_ffn_conv_w, m_ffn_conv_b, m_ffn_w_down, m_final_norm_g, v_a_w_in, v_a_b_f, v_a_w_out, v_b_w_q, v_b_w_out, v_kv_norm_g, v_w_kv, v_mix_norm_g, v_ffn_norm_g, v_ffn_w_up, v_ffn_conv_w, v_ffn_conv_b, v_ffn_w_down, v_final_norm_g):
    w = dict(a_w_in=a_w_in, a_b_f=a_b_f, a_w_out=a_w_out, b_w_q=b_w_q, b_w_out=b_w_out, kv_norm_g=kv_norm_g, w_kv=w_kv,
             mix_norm_g=mix_norm_g, ffn_norm_g=ffn_norm_g, ffn_w_up=ffn_w_up, ffn_conv_w=ffn_conv_w, ffn_conv_b=ffn_conv_b,
             ffn_w_down=ffn_w_down, final_norm_g=final_norm_g)
    m = dict(a_w_in=m_a_w_in, a_b_f=m_a_b_f, a_w_out=m_a_w_out, b_w_q=m_b_w_q, b_w_out=m_b_w_out, kv_norm_g=m_kv_norm_g,
             w_kv=m_w_kv, mix_norm_g=m_mix_norm_g, ffn_norm_g=m_ffn_norm_g, ffn_w_up=m_ffn_w_up, ffn_conv_w=m_ffn_conv_w,
             ffn_conv_b=m_ffn_conv_b, ffn_w_down=m_ffn_w_down, final_norm_g=m_final_norm_g)
    v = dict(a_w_in=v_a_w_in, a_b_f=v_a_b_f, a_w_out=v_a_w_out, b_w_q=v_b_w_q, b_w_out=v_b_w_out, kv_norm_g=v_kv_norm_g,
             w_kv=v_w_kv, mix_norm_g=v_mix_norm_g, ffn_norm_g=v_ffn_norm_g, ffn_w_up=v_ffn_w_up, ffn_conv_w=v_ffn_conv_w,
             ffn_conv_b=v_ffn_conv_b, ffn_w_down=v_ffn_w_down, final_norm_g=v_final_norm_g)

    gathered = _allgather_shards(_pack_flat(w, U16))
    p = _full_weights(gathered)
    cb = ffn_conv_b.reshape(2, 2, 1, FF)
    p.update(a_b_f=a_b_f, kv_norm_g=kv_norm_g, mix_norm_g=mix_norm_g, ffn_norm_g=ffn_norm_g, final_norm_g=final_norm_g,
             conv_b=cb)

    loss_part, grad_x, g = _local_step(x[0], loss_target[0], p)

    c_arr = lax.axis_index("c").astype(jnp.int32).reshape(1)
    gflat = _shard_grads(g)
    pair = _pair_sum(gflat, _sibling_swap_half(gflat), c_arr)
    gsh = _sibling_join(_chip_sum(_chip_exchange(pair)))
    small, loss = _unpack_small(_allreduce_small(_pack_small(g, loss_part)))

    dw, mn, vn = _adamw(_pack_flat(w, F32), gsh, _pack_flat(m, F32), _pack_flat(v, F32), "adamw_shards")
    dws, mns, vns = _adamw(_pack_small(w), _pack_small(small), _pack_small(m), _pack_small(v), "adamw_small")
    big = [_unpack_flat(t) for t in (gsh, dw, mn, vn)]
    sml = [small] + [_unpack_small(t)[0] for t in (dws, mns, vns)]
    outs = [loss, grad_x[None]]
    for b, s in zip(big, sml):
        outs += [b[n] if n in b else s[n] for n in _WEIGHTS]
    return tuple(outs)
```

```python
import functools
import math

import numpy as np
import jax
import jax.numpy as jnp
from jax import lax
from jax.experimental import pallas as pl
from jax.experimental.pallas import tpu as pltpu

F32 = jnp.float32
BF16 = jnp.bfloat16
MESH = pl.DeviceIdType.MESH

S = 4096
D = 1024
A_HEADS = 16
HEAD_DIM = 64
QKV_W = 3 * A_HEADS * HEAD_DIM
B_GROUPS = ((128, 1), (512, 4), (2048, 16))
B_HPG = 8
B_Q_W = 3 * B_HPG * HEAD_DIM
B_OUT_W = B_HPG * HEAD_DIM
B_KV_W = 2 * B_Q_W
B_WIN = 128
FF = 2816
RMS_EPS = 1e-6
SCALE = HEAD_DIM ** -0.5
N_CHIPS = 4

ADAM_LR, ADAM_B1, ADAM_B2, ADAM_EPS, ADAM_WD, ADAM_STEP = 0.001, 0.9, 0.999, 1e-08, 0.01, 10

V7X_VMEM_LIMIT = 48 * 1024 * 1024
LANES = 128
NEG_INF = float("-inf")

_SEGS = (("a_w_in", 772, 784), ("a_w_out", 256, 256), ("b_w_q", 384, 384), ("b_w_out", 128, 128),
         ("w_kv", 768, 768), ("ffn_w_up", 2816, 2816), ("ffn_w_down", 1408, 1408), ("ffn_conv_w", 9, 48))
FLAT_ROWS = 6656
HALF_ROWS = FLAT_ROWS // 2
CONV_TERM_ROWS = 16
SMALL_ROWS = 24


def _cparams(sem=None, **kw):
    return pltpu.CompilerParams(dimension_semantics=sem, vmem_limit_bytes=V7X_VMEM_LIMIT, **kw)


_DN = {"nn": (((1,), (0,)), ((), ())), "nt": (((1,), (1,)), ((), ())), "tn": (((0,), (0,)), ((), ()))}


def _mm(a, b, *, mode, tm, tn, tk, name, out_dtype=F32, res=None, a_split=0, b_split=0, o_split=0):
    if mode == "tn":
        K = a.shape[0]
        M = a.shape[1]
    else:
        M = a.shape[-2]
        K = a.shape[-1] * (2 if a_split else 1)
    if mode == "nt":
        N = b.shape[0]
    else:
        N = b.shape[-1] * (2 if b_split else 1)
    assert M % tm == 0 and N % tn == 0 and K % tk == 0, (name, M, N, K, tm, tn, tk)
    nk = K // tk

    if mode == "tn":
        a_spec = pl.BlockSpec((tk, tm), lambda i, j, k: (k, i))
    elif a_split:
        a_spec = pl.BlockSpec((None, tm, tk), lambda i, j, k: (k // a_split, i, k % a_split))
    else:
        a_spec = pl.BlockSpec((tm, tk), lambda i, j, k: (i, k))
    if mode == "nt":
        b_spec = pl.BlockSpec((tn, tk), lambda i, j, k: (j, k))
    elif b_split:
        b_spec = pl.BlockSpec((None, tk, tn), lambda i, j, k: (j // b_split, k, j % b_split))
    else:
        b_spec = pl.BlockSpec((tk, tn), lambda i, j, k: (k, j))
    if o_split:
        o_spec = pl.BlockSpec((None, tm, tn), lambda i, j, k: (j // o_split, i, j % o_split))
        out_shape = jax.ShapeDtypeStruct((2, M, N // 2), out_dtype)
    else:
        o_spec = pl.BlockSpec((tm, tn), lambda i, j, k: (i, j))
        out_shape = jax.ShapeDtypeStruct((M, N), out_dtype)
    in_specs = [a_spec, b_spec]
    args = [a, b]
    if res is not None:
        in_specs.append(pl.BlockSpec((tm, tn), lambda i, j, k: (i, j)))
        args.append(res)

    def body(*refs):
        if res is not None:
            a_ref, b_ref, r_ref, o_ref = refs[:4]
        else:
            a_ref, b_ref, o_ref = refs[:3]
            r_ref = None
        p = lax.dot_general(a_ref[...].astype(BF16), b_ref[...].astype(BF16), _DN[mode], preferred_element_type=F32)

        def finish(r):
            if r_ref is not None:
                r = r + r_ref[...]
            o_ref[...] = r.astype(out_dtype)

        if nk == 1:
            finish(p)
        else:
            acc = refs[-1]
            k = pl.program_id(2)

            @pl.when(k == 0)
            def _():
                acc[...] = p

            @pl.when(k > 0)
            def _():
                acc[...] += p

            @pl.when(k == nk - 1)
            def _():
                finish(acc[...])

    return pl.pallas_call(
        body, out_shape=out_shape, grid=(M // tm, N // tn, nk), in_specs=in_specs, out_specs=o_spec,
        scratch_shapes=[pltpu.VMEM((tm, tn), F32)] if nk > 1 else [],
        compiler_params=_cparams(("parallel", "parallel", "arbitrary")), name=name,
    )(*args)


NORM_ROWS = 256


def _rms_fwd(x, g, name):
    def body(x_ref, g_ref, o_ref):
        xv = x_ref[...]
        r = lax.rsqrt(jnp.mean(xv * xv, axis=-1, keepdims=True) + RMS_EPS)
        o_ref[...] = (xv * r * g_ref[...]).astype(BF16)

    row = pl.BlockSpec((NORM_ROWS, D), lambda i: (i, 0))
    return pl.pallas_call(
        body, out_shape=jax.ShapeDtypeStruct((S, D), BF16), grid=(S // NORM_ROWS,),
        in_specs=[row, pl.BlockSpec((1, D), lambda i: (0, 0))], out_specs=row,
        compiler_params=_cparams(("parallel",)), name=name,
    )(x, g.reshape(1, D))


def _rms_bwd(x, dres, pairs, name):
    n = len(pairs)

    def body(*refs):
        x_ref, dres_ref = refs[0], refs[1]
        g_refs = refs[2:2 + 2 * n:2]
        dh_refs = refs[3:3 + 2 * n:2]
        dx_ref, dxb_ref = refs[2 + 2 * n], refs[3 + 2 * n]
        dg_refs = refs[4 + 2 * n:]
        i = pl.program_id(0)
        xv = x_ref[...]
        r = lax.rsqrt(jnp.mean(xv * xv, axis=-1, keepdims=True) + RMS_EPS)
        y = xv * r
        dx = dres_ref[...]
        for g_ref, dh_ref, dg_ref in zip(g_refs, dh_refs, dg_refs):
            dh = dh_ref[...]
            dy = dh * g_ref[...]
            dx = dx + r * (dy - y * jnp.mean(dy * y, axis=-1, keepdims=True))
            part = jnp.sum(dh * y, axis=0, keepdims=True)

            @pl.when(i == 0)
            def _():
                dg_ref[...] = part

            @pl.when(i > 0)
            def _():
                dg_ref[...] += part

        dx_ref[...] = dx
        dxb_ref[...] = dx.astype(BF16)

    row = pl.BlockSpec((NORM_ROWS, D), lambda i: (i, 0))
    vec = pl.BlockSpec((1, D), lambda i: (0, 0))
    in_specs = [row, row]
    args = [x, dres]
    for g, dh in pairs:
        in_specs += [vec, row]
        args += [g.reshape(1, D), dh]
    outs = pl.pallas_call(
        body,
        out_shape=[jax.ShapeDtypeStruct((S, D), F32), jax.ShapeDtypeStruct((S, D), BF16)]
        + [jax.ShapeDtypeStruct((1, D), F32)] * n,
        grid=(S // NORM_ROWS,), in_specs=in_specs, out_specs=[row, row] + [vec] * n,
        compiler_params=_cparams(("arbitrary",)), name=name,
    )(*args)
    return outs[0], outs[1], list(outs[2:])


def _loss_head(x, g, target, name):
    def body(x_ref, g_ref, t_ref, loss_ref, dx_ref, dxb_ref, dg_ref):
        i = pl.program_id(0)
        xv = x_ref[...]
        gv = g_ref[...]
        r = lax.rsqrt(jnp.mean(xv * xv, axis=-1, keepdims=True) + RMS_EPS)
        y = xv * r
        err = y * gv - t_ref[...]
        lpart = jnp.broadcast_to(jnp.sum(err * err, keepdims=True) * (0.5 / D), (1, LANES))
        dh = err * (1.0 / D)
        dy = dh * gv
        dx = r * (dy - y * jnp.mean(dy * y, axis=-1, keepdims=True))
        part = jnp.sum(dh * y, axis=0, keepdims=True)

        @pl.when(i == 0)
        def _():
            dg_ref[...] = part
            loss_ref[...] = lpart

        @pl.when(i > 0)
        def _():
            dg_ref[...] += part
            loss_ref[...] += lpart

        dx_ref[...] = dx
        dxb_ref[...] = dx.astype(BF16)

    row = pl.BlockSpec((NORM_ROWS, D), lambda i: (i, 0))
    vec = pl.BlockSpec((1, D), lambda i: (0, 0))
    return pl.pallas_call(
        body,
        out_shape=[jax.ShapeDtypeStruct((1, LANES), F32), jax.ShapeDtypeStruct((S, D), F32),
                   jax.ShapeDtypeStruct((S, D), BF16), jax.ShapeDtypeStruct((1, D), F32)],
        grid=(S // NORM_ROWS,), in_specs=[row, vec, row],
        out_specs=[pl.BlockSpec((1, LANES), lambda i: (0, 0)), row, row, vec],
        compiler_params=_cparams(("arbitrary",)), name=name,
    )(x, g.reshape(1, D), target)


SCAN_ROWS = 256


def _split3(v):
    hi = v.astype(BF16)
    r1 = v - hi.astype(F32)
    mid = r1.astype(BF16)
    lo = (r1 - mid.astype(F32)).astype(BF16)
    return hi, mid, lo


def _tri_dot(tri, v):
    hi, mid, lo = _split3(v)
    dn = _DN["nn"]
    return (lax.dot_general(tri, hi, dn, preferred_element_type=F32)
            + lax.dot_general(tri, mid, dn, preferred_element_type=F32)
            + lax.dot_general(tri, lo, dn, preferred_element_type=F32))


def _log_sigmoid(z):
    return jnp.minimum(z, 0.0) - jnp.log(1.0 + jnp.exp(-jnp.abs(z)))


def _fgate_fwd(pf, bias, name):
    tri = jnp.tril(jnp.ones((SCAN_ROWS, SCAN_ROWS), F32)).astype(BF16)

    def body(pf_ref, b_ref, tri_ref, c_ref):
        carry = jnp.zeros((1, LANES), F32)
        for blk in range(S // SCAN_ROWS):
            rows = pl.ds(blk * SCAN_ROWS, SCAN_ROWS)
            lf = _log_sigmoid(pf_ref[rows, :] + b_ref[...])
            c_ref[rows, :] = _tri_dot(tri_ref[...], lf) + carry
            carry = c_ref[pl.ds(blk * SCAN_ROWS + SCAN_ROWS - 1, 1), :]

    return pl.pallas_call(
        body, out_shape=jax.ShapeDtypeStruct((S, LANES), F32),
        compiler_params=_cparams(), name=name,
    )(pf, bias, tri)


def _fgate_bwd(pf, bias, dc_key, dc_query, name):
    triu = jnp.triu(jnp.ones((SCAN_ROWS, SCAN_ROWS), F32)).astype(BF16)

    def body(pf_ref, b_ref, dck_ref, dcq_ref, tri_ref, dpf_ref, db_ref, dlf_ref):
        carry = jnp.zeros((1, LANES), F32)
        db = jnp.zeros((1, LANES), F32)
        lane = lax.broadcasted_iota(jnp.int32, (SCAN_ROWS, LANES), 1)
        for blk in reversed(range(S // SCAN_ROWS)):
            rows = pl.ds(blk * SCAN_ROWS, SCAN_ROWS)
            dc = dck_ref[rows, :] + dcq_ref[rows, :]
            dlf_ref[rows, :] = _tri_dot(tri_ref[...], dc) + carry
            carry = dlf_ref[pl.ds(blk * SCAN_ROWS, 1), :]
            z = pf_ref[rows, :] + b_ref[...]
            e = jnp.exp(-jnp.abs(z))
            sig_neg = jnp.where(z >= 0.0, e, 1.0) / (1.0 + e)
            dz = jnp.where(lane < A_HEADS, dlf_ref[rows, :] * sig_neg, 0.0)
            dpf_ref[rows, :] = dz.astype(BF16)
            db = db + jnp.sum(dz, axis=0, keepdims=True)
        db_ref[...] = db

    return pl.pallas_call(
        body, out_shape=[jax.ShapeDtypeStruct((S, LANES), BF16), jax.ShapeDtypeStruct((1, LANES), F32)],
        scratch_shapes=[pltpu.VMEM((S, LANES), F32)],
        compiler_params=_cparams(), name=name,
    )(pf, bias, dc_key, dc_query, triu)


FOX_T = 512


def _fox_fwd(q, k, v, ccol, crow, name):
    H = q.shape[0]
    T = FOX_T
    nq = S // T

    def body(q_ref, k_ref, v_ref, cc_ref, cr_ref, o_ref, lse_ref, m_sc, l_sc, acc_sc):
        i = pl.program_id(1)
        j = pl.program_id(2)

        @pl.when(j == 0)
        def _():
            m_sc[...] = jnp.full((T, 1), NEG_INF, F32)
            l_sc[...] = jnp.zeros((T, 1), F32)
            acc_sc[...] = jnp.zeros((T, HEAD_DIM), F32)

        def step(diagonal):
            s = lax.dot_general(q_ref[...], k_ref[...], _DN["nt"], preferred_element_type=F32) * SCALE
            s = s + (cc_ref[...] - cr_ref[...])
            if diagonal:
                row = lax.broadcasted_iota(jnp.int32, (T, T), 0)
                col = lax.broadcasted_iota(jnp.int32, (T, T), 1)
                s = jnp.where(row >= col, s, NEG_INF)
            m_prev = m_sc[...]
            m_new = jnp.maximum(m_prev, jnp.max(s, axis=1, keepdims=True))
            alpha = jnp.exp(m_prev - m_new)
            p = jnp.exp(s - m_new)
            l_sc[...] = alpha * l_sc[...] + jnp.sum(p, axis=1, keepdims=True)
            acc_sc[...] = alpha * acc_sc[...] + lax.dot_general(p.astype(BF16), v_ref[...], _DN["nn"],
                                                                preferred_element_type=F32)
            m_sc[...] = m_new

        @pl.when(j < i)
        def _():
            step(False)

        @pl.when(j == i)
        def _():
            step(True)
            o_ref[...] = (acc_sc[...] / l_sc[...]).astype(BF16)
            lse_ref[...] = m_sc[...] + jnp.log(l_sc[...])

    qs = pl.BlockSpec((None, T, HEAD_DIM), lambda h, i, j: (h, i, 0))
    ks = pl.BlockSpec((None, T, HEAD_DIM), lambda h, i, j: (h, jnp.minimum(i, j), 0))
    col = pl.BlockSpec((None, T, 1), lambda h, i, j: (h, i, 0))
    rowk = pl.BlockSpec((None, 1, T), lambda h, i, j: (h, 0, jnp.minimum(i, j)))
    return pl.pallas_call(
        body, out_shape=[jax.ShapeDtypeStruct((H, S, HEAD_DIM), BF16), jax.ShapeDtypeStruct((H, S, 1), F32)],
        grid=(H, nq, nq), in_specs=[qs, ks, ks, col, rowk], out_specs=[qs, col],
        scratch_shapes=[pltpu.VMEM((T, 1), F32), pltpu.VMEM((T, 1), F32), pltpu.VMEM((T, HEAD_DIM), F32)],
        compiler_params=_cparams(("parallel", "parallel", "arbitrary")), name=name,
    )(q, k, v, ccol, crow)


def _fox_bwd(q, k, v, do, lse_row, delta_row, cq_row, ck_col, name):
    H = q.shape[0]
    T = FOX_T
    nq = S // T

    def body(q_ref, k_ref, v_ref, do_ref, lse_ref, dl_ref, cq_ref, ck_ref, dq_ref, dk_ref, dv_ref, dc_ref, dcq_ref,
             dk_sc, dv_sc, dc_sc):
        j = pl.program_id(1)
        i = pl.program_id(2)

        @pl.when(jnp.logical_and(j == 0, i == 0))
        def _():
            dq_ref[...] = jnp.zeros((S, HEAD_DIM), F32)
            dcq_ref[...] = jnp.zeros((nq, 1, T), F32)

        @pl.when(i == j)
        def _():
            dk_sc[...] = jnp.zeros((T, HEAD_DIM), F32)
            dv_sc[...] = jnp.zeros((T, HEAD_DIM), F32)
            dc_sc[...] = jnp.zeros((T, 1), F32)

        def step(diagonal):
            qv = q_ref[...]
            kv = k_ref[...]
            dov = do_ref[...]
            st = lax.dot_general(kv, qv, _DN["nt"], preferred_element_type=F32) * SCALE
            st = st + (cq_ref[...] - ck_ref[...])
            if diagonal:
                row = lax.broadcasted_iota(jnp.int32, (T, T), 0)
                col = lax.broadcasted_iota(jnp.int32, (T, T), 1)
                st = jnp.where(col >= row, st, NEG_INF)
            pt = jnp.exp(st - lse_ref[...])
            dv_sc[...] += lax.dot_general(pt.astype(BF16), dov, _DN["nn"], preferred_element_type=F32)
            dpt = lax.dot_general(v_ref[...], dov, _DN["nt"], preferred_element_type=F32)
            dst = pt * (dpt - dl_ref[...])
            dc_sc[...] -= jnp.sum(dst, axis=1, keepdims=True)
            dcq_ref[i] += jnp.sum(dst, axis=0, keepdims=True)
            dsb = (dst * SCALE).astype(BF16)
            dk_sc[...] += lax.dot_general(dsb, qv, _DN["nn"], preferred_element_type=F32)
            rows = pl.ds(pl.multiple_of(i * T, T), T)
            dq_ref[rows, :] += lax.dot_general(dsb, kv, _DN["tn"], preferred_element_type=F32)

        @pl.when(i > j)
        def _():
            step(False)

        @pl.when(i == j)
        def _():
            step(True)

        @pl.when(i == nq - 1)
        def _():
            dk_ref[...] = dk_sc[...].astype(BF16)
            dv_ref[...] = dv_sc[...].astype(BF16)
            dc_ref[...] = dc_sc[...]

    qs = pl.BlockSpec((None, T, HEAD_DIM), lambda h, j, i: (h, jnp.maximum(i, j), 0))
    qrow = pl.BlockSpec((None, 1, T), lambda h, j, i: (h, 0, jnp.maximum(i, j)))
    ks = pl.BlockSpec((None, T, HEAD_DIM), lambda h, j, i: (h, j, 0))
    kcol = pl.BlockSpec((None, T, 1), lambda h, j, i: (h, j, 0))
    dqs = pl.BlockSpec((None, S, HEAD_DIM), lambda h, j, i: (h, 0, 0))
    dcqs = pl.BlockSpec((None, nq, 1, T), lambda h, j, i: (h, 0, 0, 0))
    return pl.pallas_call(
        body,
        out_shape=[jax.ShapeDtypeStruct((H, S, HEAD_DIM), F32), jax.ShapeDtypeStruct((H, S, HEAD_DIM), BF16),
                   jax.ShapeDtypeStruct((H, S, HEAD_DIM), BF16), jax.ShapeDtypeStruct((H, S, 1), F32),
                   jax.ShapeDtypeStruct((H, nq, 1, T), F32)],
        grid=(H, nq, nq), in_specs=[qs, ks, ks, qs, qrow, qrow, qrow, kcol], out_specs=[dqs, ks, ks, kcol, dcqs],
        scratch_shapes=[pltpu.VMEM((T, HEAD_DIM), F32), pltpu.VMEM((T, HEAD_DIM), F32), pltpu.VMEM((T, 1), F32)],
        compiler_params=_cparams(("parallel", "arbitrary", "arbitrary")), name=name,
    )(q, k, v, do, lse_row, delta_row, cq_row, ck_col)


def _rowdot(a, b, name):
    H = a.shape[0]
    T = 1024

    def body(a_ref, b_ref, o_ref):
        o_ref[...] = jnp.sum(a_ref[...].astype(F32) * b_ref[...].astype(F32), axis=-1, keepdims=True)

    blk = pl.BlockSpec((None, T, HEAD_DIM), lambda h, i: (h, i, 0))
    return pl.pallas_call(
        body, out_shape=jax.ShapeDtypeStruct((H, S, 1), F32), grid=(H, S // T), in_specs=[blk, blk],
        out_specs=pl.BlockSpec((None, T, 1), lambda h, i: (h, i, 0)),
        compiler_params=_cparams(("parallel", "parallel")), name=name,
    )(a, b)


W = B_WIN
N_HG = 3 * B_HPG
N_BLK = S // W


def _dil_tables():
    slopes = np.exp2((-8.0 * np.arange(1, N_HG + 1, dtype=np.float32) / N_HG).astype(np.float32)).astype(np.float32)
    dil = np.repeat(np.array([d for _, d in B_GROUPS], np.float32), B_HPG)
    coef = (slopes * dil).astype(np.float32)
    nbs = np.repeat(np.array([S // d // W for _, d in B_GROUPS], np.int32), B_HPG)
    return jnp.asarray(coef), jnp.asarray(nbs)


DIL_SUB = 8
DIL_ROWS = DIL_SUB * W
DIL_STEPS = S // DIL_ROWS


def _dil_bias(coef, transposed):
    row = lax.broadcasted_iota(jnp.int32, (W, 2 * W), 0)
    col = lax.broadcasted_iota(jnp.int32, (W, 2 * W), 1)
    dist = (col - row) if transposed else (row + W - col)
    valid = jnp.logical_and(dist >= 0, dist <= W)
    return jnp.where(valid, -coef * dist.astype(F32), NEG_INF), col


def _dil_specs():
    blk = pl.BlockSpec((None, DIL_ROWS, HEAD_DIM), lambda h, n: (h, n, 0))
    prev = pl.BlockSpec((None, W, HEAD_DIM), lambda h, n: (h, jnp.maximum(n * DIL_SUB - 1, 0), 0))
    nxt = pl.BlockSpec((None, W, HEAD_DIM), lambda h, n: (h, jnp.minimum((n + 1) * DIL_SUB, N_BLK - 1), 0))
    col = pl.BlockSpec((None, DIL_ROWS, 1), lambda h, n: (h, n, 0))
    row = pl.BlockSpec((None, 1, DIL_ROWS), lambda h, n: (h, 0, n))
    rnxt = pl.BlockSpec((None, 1, W), lambda h, n: (h, 0, jnp.minimum((n + 1) * DIL_SUB, N_BLK - 1)))
    smem = pl.BlockSpec(memory_space=pltpu.SMEM)
    return blk, prev, nxt, col, row, rnxt, smem


def _dil_fwd(q, k, v, name):
    coef_t, nbs_t = _dil_tables()

    def body(coef_ref, nbs_ref, q_ref, kh_ref, k_ref, vh_ref, v_ref, o_ref, lse_ref, kf, vf):
        hg = pl.program_id(0)
        n = pl.program_id(1)
        nbs = nbs_ref[hg]
        kf[0:W, :] = kh_ref[...]
        kf[W:, :] = k_ref[...]
        vf[0:W, :] = vh_ref[...]
        vf[W:, :] = v_ref[...]
        bias, col = _dil_bias(coef_ref[hg], False)
        for b in range(DIL_SUB):
            first = lax.rem(n * DIL_SUB + b, nbs) == 0
            rows = slice(b * W, (b + 1) * W)
            both = slice(b * W, (b + 2) * W)
            s = lax.dot_general(q_ref[rows, :], kf[both, :], _DN["nt"], preferred_element_type=F32) * SCALE + bias
            s = jnp.where(jnp.logical_and(first, col < W), NEG_INF, s)
            m = jnp.max(s, axis=1, keepdims=True)
            p = jnp.exp(s - m)
            l = jnp.sum(p, axis=1, keepdims=True)
            acc = lax.dot_general(p.astype(BF16), vf[both, :], _DN["nn"], preferred_element_type=F32)
            o_ref[rows, :] = acc / l
            lse_ref[rows, :] = m + jnp.log(l)

    blk, prev, _, col, _, _, smem = _dil_specs()
    return pl.pallas_call(
        body, out_shape=[jax.ShapeDtypeStruct((N_HG, S, HEAD_DIM), F32), jax.ShapeDtypeStruct((N_HG, S, 1), F32)],
        grid=(N_HG, DIL_STEPS), in_specs=[smem, smem, blk, prev, blk, prev, blk], out_specs=[blk, col],
        scratch_shapes=[pltpu.VMEM((DIL_ROWS + W, HEAD_DIM), BF16)] * 2,
        compiler_params=_cparams(("parallel", "parallel")), name=name,
    )(coef_t, nbs_t, q, k, k, v, v)


def _dil_merge(o, lse, name):
    T = 1024

    def body(o_ref, lse_ref, om_ref, omb_ref, l_ref):
        l0, l1, l2 = lse_ref[0], lse_ref[1], lse_ref[2]
        m = jnp.maximum(jnp.maximum(l0, l1), l2)
        e0, e1, e2 = jnp.exp(l0 - m), jnp.exp(l1 - m), jnp.exp(l2 - m)
        den = e0 + e1 + e2
        om = (e0 / den) * o_ref[0] + (e1 / den) * o_ref[1] + (e2 / den) * o_ref[2]
        om_ref[...] = om
        omb_ref[...] = om.astype(BF16)
        l_ref[...] = m + jnp.log(den)

    ob = pl.BlockSpec((None, T, HEAD_DIM), lambda h, i: (h, i, 0))
    lb = pl.BlockSpec((None, T, 1), lambda h, i: (h, i, 0))
    return pl.pallas_call(
        body,
        out_shape=[jax.ShapeDtypeStruct((B_HPG, S, HEAD_DIM), F32), jax.ShapeDtypeStruct((B_HPG, S, HEAD_DIM), BF16),
                   jax.ShapeDtypeStruct((B_HPG, S, 1), F32)],
        grid=(B_HPG, S // T),
        in_specs=[pl.BlockSpec((3, None, T, HEAD_DIM), lambda h, i: (0, h, i, 0)),
                  pl.BlockSpec((3, None, T, 1), lambda h, i: (0, h, i, 0))],
        out_specs=[ob, ob, lb], compiler_params=_cparams(("parallel", "parallel")), name=name,
    )(o, lse)


def _dil_bwd_dq(q, k, v, do, lcol, dcol, name):
    coef_t, nbs_t = _dil_tables()

    def body(coef_ref, nbs_ref, q_ref, kh_ref, k_ref, vh_ref, v_ref, do_ref, l_ref, d_ref, dq_ref, kf, vf):
        hg = pl.program_id(0)
        n = pl.program_id(1)
        nbs = nbs_ref[hg]
        kf[0:W, :] = kh_ref[...]
        kf[W:, :] = k_ref[...]
        vf[0:W, :] = vh_ref[...]
        vf[W:, :] = v_ref[...]
        bias, col = _dil_bias(coef_ref[hg], False)
        for b in range(DIL_SUB):
            first = lax.rem(n * DIL_SUB + b, nbs) == 0
            rows = slice(b * W, (b + 1) * W)
            both = slice(b * W, (b + 2) * W)
            kk = kf[both, :]
            s = lax.dot_general(q_ref[rows, :], kk, _DN["nt"], preferred_element_type=F32) * SCALE + bias
            s = jnp.where(jnp.logical_and(first, col < W), NEG_INF, s)
            p = jnp.exp(s - l_ref[rows, :])
            dp = lax.dot_general(do_ref[rows, :], vf[both, :], _DN["nt"], preferred_element_type=F32)
            ds = (p * (dp - d_ref[rows, :]) * SCALE).astype(BF16)
            dq_ref[rows, :] = lax.dot_general(ds, kk, _DN["nn"], preferred_element_type=F32).astype(BF16)

    blk, prev, _, col, _, _, smem = _dil_specs()
    return pl.pallas_call(
        body, out_shape=jax.ShapeDtypeStruct((N_HG, S, HEAD_DIM), BF16), grid=(N_HG, DIL_STEPS),
        in_specs=[smem, smem, blk, prev, blk, prev, blk, blk, col, col], out_specs=blk,
        scratch_shapes=[pltpu.VMEM((DIL_ROWS + W, HEAD_DIM), BF16)] * 2,
        compiler_params=_cparams(("parallel", "parallel")), name=name,
    )(coef_t, nbs_t, q, k, k, v, v, do, lcol, dcol)


def _dil_bwd_dkv(q, k, v, do, lrow, drow, name):
    coef_t, nbs_t = _dil_tables()

    def body(coef_ref, nbs_ref, k_ref, v_ref, q_ref, qn_ref, do_ref, don_ref, l_ref, ln_ref, d_ref, dn_ref,
             dk_ref, dv_ref, qf, dof, lf, df):
        hg = pl.program_id(0)
        n = pl.program_id(1)
        nbs = nbs_ref[hg]
        qf[0:DIL_ROWS, :] = q_ref[...]
        qf[DIL_ROWS:, :] = qn_ref[...]
        dof[0:DIL_ROWS, :] = do_ref[...]
        dof[DIL_ROWS:, :] = don_ref[...]
        lf[:, 0:DIL_ROWS] = l_ref[...]
        lf[:, DIL_ROWS:] = ln_ref[...]
        df[:, 0:DIL_ROWS] = d_ref[...]
        df[:, DIL_ROWS:] = dn_ref[...]
        bias, col = _dil_bias(coef_ref[hg], True)
        for b in range(DIL_SUB):
            no_next = lax.rem(n * DIL_SUB + b + 1, nbs) == 0
            rows = slice(b * W, (b + 1) * W)
            both = slice(b * W, (b + 2) * W)
            qq = qf[both, :]
            dd = dof[both, :]
            st = lax.dot_general(k_ref[rows, :], qq, _DN["nt"], preferred_element_type=F32) * SCALE + bias
            st = jnp.where(jnp.logical_and(no_next, col >= W), NEG_INF, st)
            pt = jnp.exp(st - lf[:, both])
            dv_ref[rows, :] = lax.dot_general(pt.astype(BF16), dd, _DN["nn"], preferred_element_type=F32).astype(BF16)
            dpt = lax.dot_general(v_ref[rows, :], dd, _DN["nt"], preferred_element_type=F32)
            dst = (pt * (dpt - df[:, both]) * SCALE).astype(BF16)
            dk_ref[rows, :] = lax.dot_general(dst, qq, _DN["nn"], preferred_element_type=F32).astype(BF16)

    blk, _, nxt, _, row, rnxt, smem = _dil_specs()
    return pl.pallas_call(
        body, out_shape=[jax.ShapeDtypeStruct((N_HG, S, HEAD_DIM), BF16)] * 2, grid=(N_HG, DIL_STEPS),
        in_specs=[smem, smem, blk, blk, blk, nxt, blk, nxt, row, rnxt, row, rnxt], out_specs=[blk, blk],
        scratch_shapes=[pltpu.VMEM((DIL_ROWS + W, HEAD_DIM), BF16)] * 2 + [pltpu.VMEM((1, DIL_ROWS + W), F32)] * 2,
        compiler_params=_cparams(("parallel", "parallel")), name=name,
    )(coef_t, nbs_t, k, v, q, q, do, do, lrow, lrow, drow, drow)


FFN_ROWS = 512
FFN_COLS = 256
HALO = 8


def _shifted(u, halo, back):
    T = u.shape[0]
    rows = lax.broadcasted_iota(jnp.int32, u.shape, 0)
    if back:
        s1 = jnp.where(rows == 0, halo[HALO - 1:HALO, :], pltpu.roll(u, 1, 0))
        s2 = jnp.where(rows == 0, halo[HALO - 2:HALO - 1, :],
                       jnp.where(rows == 1, halo[HALO - 1:HALO, :], pltpu.roll(u, 2, 0)))
    else:
        s1 = jnp.where(rows == T - 1, halo[0:1, :], pltpu.roll(u, T - 1, 0))
        s2 = jnp.where(rows == T - 1, halo[1:2, :],
                       jnp.where(rows == T - 2, halo[0:1, :], pltpu.roll(u, T - 2, 0)))
    return s1, s2


def _conv_parts(u_ref, h_ref, w_ref, b_ref, first):
    out = []
    for p in range(2):
        u = u_ref[p]
        halo = jnp.where(first, 0.0, h_ref[p])
        u1, u2 = _shifted(u, halo, True)
        w = w_ref[p]
        out.append((w[0:1, :] * u2 + w[1:2, :] * u1 + w[2:3, :] * u + b_ref[p], u1, u2, u))
    return out


def _ffn_specs():
    T, C = FFN_ROWS, FFN_COLS
    blk = pl.BlockSpec((2, T, C), lambda j, i: (0, i, j))
    prev = pl.BlockSpec((2, HALO, C), lambda j, i: (0, jnp.maximum(i * (T // HALO) - 1, 0), j))
    nxt = pl.BlockSpec((2, HALO, C), lambda j, i: (0, jnp.minimum((i + 1) * (T // HALO), S // HALO - 1), j))
    wsp = pl.BlockSpec((2, 3, C), lambda j, i: (0, 0, j))
    bsp = pl.BlockSpec((2, 1, C), lambda j, i: (0, 0, j))
    one = pl.BlockSpec((T, C), lambda j, i: (i, j))
    return blk, prev, nxt, wsp, bsp, one


def _ffn_act_fwd(u, w, b, name):
    blk, prev, _, wsp, bsp, one = _ffn_specs()

    def body(u_ref, h_ref, w_ref, b_ref, o_ref):
        (a, _, _, _), (g, _, _, _) = _conv_parts(u_ref, h_ref, w_ref, b_ref, pl.program_id(1) == 0)
        o_ref[...] = (g / (1.0 + jnp.exp(-g)) * a).astype(BF16)

    return pl.pallas_call(
        body, out_shape=jax.ShapeDtypeStruct((S, FF), BF16), grid=(FF // FFN_COLS, S // FFN_ROWS),
        in_specs=[blk, prev, wsp, bsp], out_specs=one,
        compiler_params=_cparams(("parallel", "parallel")), name=name,
    )(u, u, w, b)


def _ffn_act_bwd(u, dact, w, b, name):
    blk, prev, _, wsp, bsp, one = _ffn_specs()

    def body(u_ref, h_ref, da_ref, w_ref, b_ref, duc_ref, dwb_ref):
        i = pl.program_id(1)
        (a, a1, a2, a0), (g, g1, g2, g0) = _conv_parts(u_ref, h_ref, w_ref, b_ref, i == 0)
        dact_v = da_ref[...]
        sg = 1.0 / (1.0 + jnp.exp(-g))
        d_a = dact_v * (g * sg)
        d_g = dact_v * a * (sg * (1.0 + g * (1.0 - sg)))
        duc_ref[0] = d_a
        duc_ref[1] = d_g

        @pl.when(i == 0)
        def _():
            dwb_ref[...] = jnp.zeros(dwb_ref.shape, F32)

        for p, (d, s2, s1, s0) in enumerate(((d_a, a2, a1, a0), (d_g, g2, g1, g0))):
            dwb_ref[p, 0:1, :] += jnp.sum(d * s2, axis=0, keepdims=True)
            dwb_ref[p, 1:2, :] += jnp.sum(d * s1, axis=0, keepdims=True)
            dwb_ref[p, 2:3, :] += jnp.sum(d * s0, axis=0, keepdims=True)
            dwb_ref[p, 3:4, :] += jnp.sum(d, axis=0, keepdims=True)

    return pl.pallas_call(
        body, out_shape=[jax.ShapeDtypeStruct((2, S, FF), F32), jax.ShapeDtypeStruct((2, 8, FF), F32)],
        grid=(FF // FFN_COLS, S // FFN_ROWS), in_specs=[blk, prev, one, wsp, bsp],
        out_specs=[blk, pl.BlockSpec((2, 8, FFN_COLS), lambda j, i: (0, 0, j))],
        compiler_params=_cparams(("parallel", "arbitrary")), name=name,
    )(u, u, dact, w, b)


def _ffn_conv_bwd(duc, w, name):
    blk, _, nxt, wsp, _, _ = _ffn_specs()
    last = S // FFN_ROWS - 1

    def body(d_ref, h_ref, w_ref, du_ref):
        is_last = pl.program_id(1) == last
        for p in range(2):
            d = d_ref[p]
            halo = jnp.where(is_last, 0.0, h_ref[p])
            d1, d2 = _shifted(d, halo, False)
            wv = w_ref[p]
            du_ref[p] = (wv[2:3, :] * d + wv[1:2, :] * d1 + wv[0:1, :] * d2).astype(BF16)

    return pl.pallas_call(
        body, out_shape=jax.ShapeDtypeStruct((2, S, FF), BF16), grid=(FF // FFN_COLS, S // FFN_ROWS),
        in_specs=[blk, nxt, wsp], out_specs=blk,
        compiler_params=_cparams(("parallel", "parallel")), name=name,
    )(duc, duc, w)


def _adam_update(w, gv, m, v):
    c1 = 1.0 / (1.0 - ADAM_B1 ** ADAM_STEP)
    c2 = 1.0 / (1.0 - ADAM_B2 ** ADAM_STEP)
    mn = ADAM_B1 * m + (1.0 - ADAM_B1) * gv
    vn = ADAM_B2 * v + (1.0 - ADAM_B2) * (gv * gv)
    return -ADAM_LR * ((mn * c1) / (jnp.sqrt(vn * c2) + ADAM_EPS) + ADAM_WD * w), mn, vn


def _adamw_halves(w, g_mine, g_other, m, v, c_arr, name):
    T = 256
    nb = HALF_ROWS // T

    def body(c_ref, w_ref, gm_ref, go_ref, m_ref, v_ref, g_ref, d_ref, mo_ref, vo_ref):
        is_mine = (pl.program_id(0) // nb) == c_ref[0]
        gv = jnp.where(is_mine, gm_ref[...], go_ref[...])
        g_ref[...] = gv
        d_ref[...], mo_ref[...], vo_ref[...] = _adam_update(w_ref[...], gv, m_ref[...], v_ref[...])

    blk = pl.BlockSpec((T, 1024), lambda i, c: (i, 0))
    mine = pl.BlockSpec((T, 1024), lambda i, c: (jnp.clip(i - c[0] * nb, 0, nb - 1), 0))
    other = pl.BlockSpec((T, 1024), lambda i, c: (jnp.clip(i - (1 - c[0]) * nb, 0, nb - 1), 0))
    sds = jax.ShapeDtypeStruct((FLAT_ROWS, 1024), F32)
    return pl.pallas_call(
        body, out_shape=[sds] * 4,
        grid_spec=pltpu.PrefetchScalarGridSpec(num_scalar_prefetch=1, grid=(FLAT_ROWS // T,),
                                               in_specs=[blk, mine, other, blk, blk], out_specs=[blk] * 4),
        compiler_params=_cparams(("parallel",)), name=name,
    )(c_arr, w, g_mine, g_other, m, v)


def _adamw(w, g, m, v, name):
    rows = w.shape[0]
    T = 8
    for cand in (256, 128, 64, 32, 16, 8):
        if rows % cand == 0:
            T = cand
            break

    def body(w_ref, g_ref, m_ref, v_ref, d_ref, mo_ref, vo_ref):
        d_ref[...], mo_ref[...], vo_ref[...] = _adam_update(w_ref[...], g_ref[...], m_ref[...], v_ref[...])

    blk = pl.BlockSpec((T, 1024), lambda i: (i, 0))
    sds = jax.ShapeDtypeStruct(w.shape, F32)
    return pl.pallas_call(
        body, out_shape=[sds, sds, sds], grid=(rows // T,), in_specs=[blk] * 4, out_specs=[blk] * 3,
        compiler_params=_cparams(("parallel",)), name=name,
    )(w, g, m, v)


ANY = pl.BlockSpec(memory_space=pl.ANY)


def _place():
    x, y, c = lax.axis_index("x"), lax.axis_index("y"), lax.axis_index("c")
    chips = [(1 - x, y), (x, 1 - y), (1 - x, 1 - y)]
    return x, y, c, chips


def _place_own(w, k_arr):
    T = 256

    def body(k_ref, w_ref, o_ref):
        o_ref[...] = w_ref[...]

    return pl.pallas_call(
        body, out_shape=jax.ShapeDtypeStruct((N_CHIPS, FLAT_ROWS, 1024), w.dtype),
        grid_spec=pltpu.PrefetchScalarGridSpec(
            num_scalar_prefetch=1, grid=(FLAT_ROWS // T,),
            in_specs=[pl.BlockSpec((T, 1024), lambda i, k: (i, 0))],
            out_specs=pl.BlockSpec((None, T, 1024), lambda i, k: (k[0], i, 0))),
        compiler_params=_cparams(("parallel",)), name="gather_place_own",
    )(k_arr, w)


def _allgather_shards(w, buf):
    def body(w_ref, buf_ref, g_ref, send_sems, recv_sems):
        x, y, c, chips = _place()
        myk = 2 * x + y
        sibling = (x, y, 1 - c)
        h0 = pl.multiple_of(c * HALF_ROWS, 16)
        h1 = pl.multiple_of((1 - c) * HALF_ROWS, 16)

        def half(k, start):
            return g_ref.at[k, pl.ds(start, HALF_ROWS), :]

        def rcopy(sem, src, dst, to):
            return pltpu.make_async_remote_copy(src_ref=src, dst_ref=dst, send_sem=send_sems.at[sem],
                                                recv_sem=recv_sems.at[sem], device_id=to, device_id_type=MESH)

        ici = [rcopy(r, w_ref.at[pl.ds(h0, HALF_ROWS), :], half(myk, h0), (*chip, c)) for r, chip in enumerate(chips)]
        for cp in ici:
            cp.start()
        ks = [2 * cx + cy for cx, cy in chips]
        fwd = [rcopy(3 + r, half(ks[r], h0), half(ks[r], h0), sibling) for r in range(3)]
        for r in range(3):
            rcopy(r, half(ks[r], h0), half(ks[r], h0), (*chips[r], c)).wait_recv()
            fwd[r].start()
        for r in range(3):
            rcopy(3 + r, half(ks[r], h1), half(ks[r], h1), sibling).wait_recv()
        for cp in ici + fwd:
            cp.wait_send()

    return pl.pallas_call(
        body, out_shape=jax.ShapeDtypeStruct((N_CHIPS, FLAT_ROWS, 1024), w.dtype), in_specs=[ANY, ANY], out_specs=ANY,
        scratch_shapes=[pltpu.SemaphoreType.DMA((6,)), pltpu.SemaphoreType.DMA((6,))],
        input_output_aliases={1: 0},
        compiler_params=pltpu.CompilerParams(has_side_effects=True), name="allgather_shards",
    )(w, buf)


def _sibling_swap_half(g):
    def body(g_ref, o_ref, send_sem, recv_sem):
        x, y, c, _ = _place()
        theirs = pl.multiple_of((1 - c) * HALF_ROWS, 8)
        cp = pltpu.make_async_remote_copy(src_ref=g_ref.at[:, pl.ds(theirs, HALF_ROWS), :], dst_ref=o_ref,
                                          send_sem=send_sem, recv_sem=recv_sem, device_id=(x, y, 1 - c),
                                          device_id_type=MESH)
        cp.start()
        cp.wait()

    return pl.pallas_call(
        body, out_shape=jax.ShapeDtypeStruct((N_CHIPS, HALF_ROWS, 1024), F32), in_specs=[ANY], out_specs=ANY,
        scratch_shapes=[pltpu.SemaphoreType.DMA, pltpu.SemaphoreType.DMA],
        compiler_params=pltpu.CompilerParams(has_side_effects=True), name="rs_sibling_swap",
    )(g)


def _pair_sum(g, other, c_arr):
    T = 256

    def body(c_ref, g_ref, o_ref, s_ref):
        s_ref[...] = (g_ref[...] + o_ref[...]).astype(BF16)

    nb = HALF_ROWS // T
    return pl.pallas_call(
        body, out_shape=jax.ShapeDtypeStruct((N_CHIPS, HALF_ROWS, 1024), BF16),
        grid_spec=pltpu.PrefetchScalarGridSpec(
            num_scalar_prefetch=1, grid=(N_CHIPS, nb),
            in_specs=[pl.BlockSpec((None, T, 1024), lambda k, i, c: (k, c[0] * nb + i, 0)),
                      pl.BlockSpec((None, T, 1024), lambda k, i, c: (k, i, 0))],
            out_specs=pl.BlockSpec((None, T, 1024), lambda k, i, c: (k, i, 0))),
        compiler_params=_cparams(("parallel", "parallel")), name="rs_pair_sum",
    )(c_arr, g, other)


def _chip_exchange(s):
    def body(s_ref, o_ref, send_sems, recv_sems):
        x, y, c, chips = _place()
        cps = []
        for r, (cx, cy) in enumerate(chips):
            cps.append(pltpu.make_async_remote_copy(
                src_ref=s_ref.at[2 * cx + cy], dst_ref=o_ref.at[r], send_sem=send_sems.at[r],
                recv_sem=recv_sems.at[r], device_id=(cx, cy, c), device_id_type=MESH))
        for cp in cps:
            cp.start()
        for cp in cps:
            cp.wait()

    return pl.pallas_call(
        body, out_shape=jax.ShapeDtypeStruct((3, HALF_ROWS, 1024), BF16), in_specs=[ANY], out_specs=ANY,
        scratch_shapes=[pltpu.SemaphoreType.DMA((3,)), pltpu.SemaphoreType.DMA((3,))],
        compiler_params=pltpu.CompilerParams(has_side_effects=True), name="rs_chip_exchange",
    )(s)


def _chip_sum(s, r, k_arr):
    T = 256

    def body(k_ref, s_ref, r_ref, o_ref):
        o_ref[...] = ((s_ref[...].astype(F32) + r_ref[0].astype(F32)) + r_ref[1].astype(F32)) + r_ref[2].astype(F32)

    return pl.pallas_call(
        body, out_shape=jax.ShapeDtypeStruct((HALF_ROWS, 1024), F32),
        grid_spec=pltpu.PrefetchScalarGridSpec(
            num_scalar_prefetch=1, grid=(HALF_ROWS // T,),
            in_specs=[pl.BlockSpec((None, T, 1024), lambda i, k: (k[0], i, 0)),
                      pl.BlockSpec((3, T, 1024), lambda i, k: (0, i, 0))],
            out_specs=pl.BlockSpec((T, 1024), lambda i, k: (i, 0))),
        compiler_params=_cparams(("parallel",)), name="rs_chip_sum",
    )(k_arr, s, r)


def _sibling_send(t):
    def body(t_ref, o_ref, send_sem, recv_sem):
        x, y, c, _ = _place()
        cp = pltpu.make_async_remote_copy(src_ref=t_ref, dst_ref=o_ref, send_sem=send_sem, recv_sem=recv_sem,
                                          device_id=(x, y, 1 - c), device_id_type=MESH)
        cp.start()
        cp.wait()

    return pl.pallas_call(
        body, out_shape=jax.ShapeDtypeStruct((HALF_ROWS, 1024), F32), in_specs=[ANY], out_specs=ANY,
        scratch_shapes=[pltpu.SemaphoreType.DMA, pltpu.SemaphoreType.DMA],
        compiler_params=pltpu.CompilerParams(has_side_effects=True), name="rs_sibling_send",
    )(t)


def _allreduce_small(v):
    def body(v_ref, o_ref, buf, send_sems, recv_sems):
        x, y, c, _ = _place()
        me = 4 * x + 2 * y + c
        buf[me] = v_ref[...]
        cps = []
        for mask in range(1, 8):
            a, b, d = (mask >> 2) & 1, (mask >> 1) & 1, mask & 1
            peer = (x + a - 2 * a * x, y + b - 2 * b * y, c + d - 2 * d * c)
            cps.append(pltpu.make_async_remote_copy(
                src_ref=v_ref, dst_ref=buf.at[me], send_sem=send_sems.at[mask - 1], recv_sem=recv_sems.at[mask - 1],
                device_id=peer, device_id_type=MESH))
        for cp in cps:
            cp.start()
        for cp in cps:
            cp.wait()
        total = buf[0]
        for dev in range(1, 8):
            total = total + buf[dev]
        o_ref[...] = total

    vm = pl.BlockSpec(memory_space=pltpu.VMEM)
    return pl.pallas_call(
        body, out_shape=jax.ShapeDtypeStruct((SMALL_ROWS, 1024), F32), in_specs=[vm], out_specs=vm,
        scratch_shapes=[pltpu.VMEM((8, SMALL_ROWS, 1024), F32), pltpu.SemaphoreType.DMA((7,)),
                        pltpu.SemaphoreType.DMA((7,))],
        compiler_params=pltpu.CompilerParams(has_side_effects=True), name="allreduce_small",
    )(v)


def _heads(t, n):
    return t.reshape(S, n, HEAD_DIM).transpose(1, 0, 2)


def _unheads(t):
    return t.transpose(1, 0, 2).reshape(S, t.shape[0] * HEAD_DIM)


def _to_residue(t, d):
    c = t.shape[-1]
    return t.reshape(B_HPG, S // d, d, c).transpose(0, 2, 1, 3).reshape(B_HPG, S, c)


def _from_residue(t, d):
    c = t.shape[-1]
    return t.reshape(B_HPG, d, S // d, c).transpose(0, 2, 1, 3).reshape(B_HPG, S, c)


def _dil_pack(t):
    return jnp.concatenate([_to_residue(t[g], d) for g, (_, d) in enumerate(B_GROUPS)], axis=0)


def _dil_unpack(t):
    return jnp.stack([_from_residue(t[g * B_HPG:(g + 1) * B_HPG], d) for g, (_, d) in enumerate(B_GROUPS)], axis=0)


def _col_to_row(t):
    return t.reshape(t.shape[0], 1, S)


def _ffn_fwd(x, g, w_up, cw, cb, w_down, tag):
    h = _rms_fwd(x, g, f"{tag}_norm")
    u = _mm(h, w_up, mode="nn", tm=1024, tn=1408, tk=1024, o_split=2, name=f"{tag}_up")
    act = _ffn_act_fwd(u, cw, cb, f"{tag}_act")
    x_out = _mm(act, w_down, mode="nn", tm=1024, tn=512, tk=FF, res=x, name=f"{tag}_down")
    return x_out, (h, u, act)


def _ffn_bwd(x, g, w_up, cw, cb, w_down, saved, dx, dxb, tag):
    h, u, act = saved
    d_w_down = _mm(act, dxb, mode="tn", tm=1408, tn=512, tk=1024, name=f"{tag}_dwdown")
    dact = _mm(dxb, w_down, mode="nt", tm=1024, tn=1408, tk=1024, name=f"{tag}_dact")
    duc, dwb = _ffn_act_bwd(u, dact, cw, cb, f"{tag}_dgate")
    du = _ffn_conv_bwd(duc, cw, f"{tag}_dconv")
    d_w_up = _mm(h, du, mode="tn", tm=1024, tn=1408, tk=1024, b_split=2, name=f"{tag}_dwup")
    dh = _mm(du, w_up, mode="nt", tm=1024, tn=512, tk=1408, a_split=2, name=f"{tag}_dh")
    dx_new, dxb_new, (dg,) = _rms_bwd(x, dx, [(g, dh)], f"{tag}_dnorm")
    d_cw = dwb[:, 0:3, :].transpose(1, 0, 2).reshape(3, 2 * FF)
    d_cb = dwb[:, 3, :].reshape(2 * FF)
    return dx_new, dxb_new, dict(w_up=d_w_up, w_down=d_w_down, conv_w=d_cw, conv_b=d_cb, norm_g=dg.reshape(D))


def _local_step(x, target, p):
    g = {}
    h1 = _rms_fwd(x, p["mix_norm_g"][0], "a_norm")
    w_qkv = p["a_w_in"][:, :QKV_W]
    w_f = jnp.pad(p["a_w_in"][:, QKV_W:], ((0, 0), (0, LANES - A_HEADS)))
    b_f = jnp.pad(p["a_b_f"].reshape(1, A_HEADS), ((0, 0), (0, LANES - A_HEADS)))
    qkv = _mm(h1, w_qkv, mode="nn", tm=1024, tn=512, tk=1024, out_dtype=BF16, name="a_qkv")
    pf = _mm(h1, w_f, mode="nn", tm=1024, tn=LANES, tk=1024, name="a_gate")
    cum = _fgate_fwd(pf, b_f, "a_gate_scan")
    qkv_h = _heads(qkv, 3 * A_HEADS)
    qa, ka, va = qkv_h[:A_HEADS], qkv_h[A_HEADS:2 * A_HEADS], qkv_h[2 * A_HEADS:]
    c_hs = cum[:, :A_HEADS].T
    c_col, c_row = c_hs.reshape(A_HEADS, S, 1), c_hs.reshape(A_HEADS, 1, S)
    oa, lse_a = _fox_fwd(qa, ka, va, c_col, c_row, "a_attn")
    oa2 = _unheads(oa)
    x1 = _mm(oa2, p["a_w_out"], mode="nn", tm=1024, tn=512, tk=1024, res=x, name="a_out")
    x2, ffn0 = _ffn_fwd(x1, p["ffn_norm_g"][0], p["ffn_w_up"][0], p["conv_w"][0], p["conv_b"][0], p["ffn_w_down"][0], "f0")
    hk = _rms_fwd(x2, p["kv_norm_g"], "kv_norm")
    kvb = _mm(hk, p["w_kv"], mode="nn", tm=1024, tn=512, tk=1024, out_dtype=BF16, name="kv_proj")
    h3 = _rms_fwd(x2, p["mix_norm_g"][1], "b_norm")
    qb = _mm(h3, p["b_w_q"], mode="nn", tm=1024, tn=512, tk=1024, out_dtype=BF16, name="b_q")
    qd = _dil_pack(_heads(qb, N_HG).reshape(3, B_HPG, S, HEAD_DIM))
    kv_h = _heads(kvb, 2 * N_HG)
    kd = _dil_pack(kv_h[:N_HG].reshape(3, B_HPG, S, HEAD_DIM))
    vd = _dil_pack(kv_h[N_HG:].reshape(3, B_HPG, S, HEAD_DIM))
    od, lsed = _dil_fwd(qd, kd, vd, "b_attn")
    ob, obb, lse_b = _dil_merge(_dil_unpack(od), _dil_unpack(lsed), "b_merge")
    ob2 = _unheads(obb)
    x3 = _mm(ob2, p["b_w_out"], mode="nn", tm=1024, tn=512, tk=B_OUT_W, res=x2, name="b_out")
    x4, ffn1 = _ffn_fwd(x3, p["ffn_norm_g"][1], p["ffn_w_up"][1], p["conv_w"][1], p["conv_b"][1], p["ffn_w_down"][1], "f1")
    loss, dx, dxb, dg_final = _loss_head(x4, p["final_norm_g"], target, "loss_head")
    g["final_norm_g"] = dg_final.reshape(D)

    dx, dxb, gf1 = _ffn_bwd(x3, p["ffn_norm_g"][1], p["ffn_w_up"][1], p["conv_w"][1], p["conv_b"][1], p["ffn_w_down"][1],
                            ffn1, dx, dxb, "f1")
    g["b_w_out"] = _mm(ob2, dxb, mode="tn", tm=B_OUT_W, tn=512, tk=1024, name="b_dwout")
    dob = _heads(_mm(dxb, p["b_w_out"], mode="nt", tm=1024, tn=B_OUT_W, tk=1024, name="b_do"), B_HPG)
    delta_b = _rowdot(dob, ob, "b_delta")
    rep = lambda t: jnp.broadcast_to(t[None], (3,) + t.shape)
    dod = _dil_pack(rep(dob.astype(BF16)))
    l_d = _dil_pack(rep(lse_b))
    dl_d = _dil_pack(rep(delta_b))
    dqd = _dil_bwd_dq(qd, kd, vd, dod, l_d, dl_d, "b_dq")
    dkd, dvd = _dil_bwd_dkv(qd, kd, vd, dod, _col_to_row(l_d), _col_to_row(dl_d), "b_dkv")
    dqb = _unheads(_dil_unpack(dqd).reshape(N_HG, S, HEAD_DIM))
    dkvb = _unheads(jnp.concatenate([_dil_unpack(dkd).reshape(N_HG, S, HEAD_DIM),
                                     _dil_unpack(dvd).reshape(N_HG, S, HEAD_DIM)], axis=0))
    g["b_w_q"] = _mm(h3, dqb, mode="tn", tm=1024, tn=512, tk=1024, name="b_dwq")
    dh3 = _mm(dqb, p["b_w_q"], mode="nt", tm=1024, tn=512, tk=B_Q_W, name="b_dh")
    g["w_kv"] = _mm(hk, dkvb, mode="tn", tm=1024, tn=512, tk=1024, name="kv_dw")
    dhk = _mm(dkvb, p["w_kv"], mode="nt", tm=1024, tn=512, tk=1536, name="kv_dh")
    dx, dxb, (dg_mix1, dg_kv) = _rms_bwd(x2, dx, [(p["mix_norm_g"][1], dh3), (p["kv_norm_g"], dhk)], "b_dnorm")
    g["kv_norm_g"] = dg_kv.reshape(D)
    dx, dxb, gf0 = _ffn_bwd(x1, p["ffn_norm_g"][0], p["ffn_w_up"][0], p["conv_w"][0], p["conv_b"][0], p["ffn_w_down"][0],
                            ffn0, dx, dxb, "f0")
    g["a_w_out"] = _mm(oa2, dxb, mode="tn", tm=1024, tn=512, tk=1024, name="a_dwout")
    doa = _heads(_mm(dxb, p["a_w_out"], mode="nt", tm=1024, tn=512, tk=1024, name="a_do"), A_HEADS)
    delta_a = _rowdot(doa, oa, "a_delta")
    dqa, dka, dva, dck, dcq = _fox_bwd(qa, ka, va, doa.astype(BF16), _col_to_row(lse_a), _col_to_row(delta_a), c_row,
                                       c_col, "a_dattn")
    dqkv = _unheads(jnp.concatenate([dqa.astype(BF16), dka, dva], axis=0))
    pad_heads = lambda t: jnp.pad(t.reshape(A_HEADS, S).T, ((0, 0), (0, LANES - A_HEADS)))
    dpf, db_f = _fgate_bwd(pf, b_f, pad_heads(dck), pad_heads(dcq), "a_dgate_scan")
    g["a_b_f"] = db_f[:, :A_HEADS]
    d_w_qkv = _mm(h1, dqkv, mode="tn", tm=1024, tn=512, tk=1024, name="a_dwqkv")
    d_w_f = _mm(h1, dpf, mode="tn", tm=1024, tn=LANES, tk=1024, name="a_dwgate")
    g["a_w_in"] = jnp.concatenate([d_w_qkv, d_w_f[:, :A_HEADS]], axis=1)
    dh1 = _mm(dqkv, w_qkv, mode="nt", tm=1024, tn=512, tk=1536, name="a_dh")
    dh1 = _mm(dpf, w_f, mode="nt", tm=1024, tn=512, tk=LANES, res=dh1, name="a_dh_gate")
    dx, _, (dg_mix0,) = _rms_bwd(x, dx, [(p["mix_norm_g"][0], dh1)], "a_dnorm")

    g["mix_norm_g"] = jnp.stack([dg_mix0.reshape(D), dg_mix1.reshape(D)])
    g["ffn_norm_g"] = jnp.stack([gf0["norm_g"], gf1["norm_g"]])
    g["ffn_w_up"] = jnp.stack([gf0["w_up"], gf1["w_up"]])
    g["ffn_w_down"] = jnp.stack([gf0["w_down"], gf1["w_down"]])
    g["ffn_conv_w"] = jnp.stack([gf0["conv_w"], gf1["conv_w"]])
    g["ffn_conv_b"] = jnp.stack([gf0["conv_b"], gf1["conv_b"]])
    return loss[0, 0], dx, g


_SHARD_SHAPES = {"a_w_in": (1, 1024, 772), "a_w_out": (1, 256, 1024), "b_w_q": (1, 1024, 384), "b_w_out": (1, 512, 256),
                 "w_kv": (1024, 768), "ffn_w_up": (2, 1024, 1408), "ffn_w_down": (2, 704, 1024), "ffn_conv_w": (2, 3, 1408)}
_SHARD_AXIS = {"a_w_in": 2, "a_w_out": 1, "b_w_q": 2, "b_w_out": 2, "w_kv": 1, "ffn_w_up": 2, "ffn_w_down": 1,
               "ffn_conv_w": 2}
_SMALL = (("kv_norm_g", (1024,)), ("mix_norm_g", (2, 1024)), ("ffn_norm_g", (2, 1024)), ("final_norm_g", (1024,)),
          ("a_b_f", (1, 16)), ("ffn_conv_b", (2, 5632)))


def _rows(t, pad_to):
    flat = t.reshape(-1)
    flat = jnp.pad(flat, (0, pad_to * 1024 - flat.shape[0]))
    return flat.reshape(pad_to, 1024)


def _pack_flat(shards, dtype):
    parts = []
    for name, _, padded in _SEGS:
        t = shards[name].astype(F32)
        if dtype == BF16 and name == "ffn_conv_w":
            parts += [_rows(term, CONV_TERM_ROWS) for term in _split3(t)]
            parts.append(jnp.zeros((padded - 3 * CONV_TERM_ROWS, 1024), BF16))
        else:
            parts.append(_rows(t, padded).astype(dtype))
    used = sum(p for _, _, p in _SEGS)
    parts.append(jnp.zeros((FLAT_ROWS - used, 1024), dtype))
    return jnp.concatenate(parts, axis=0)


def _unpack_flat(flat):
    out = {}
    r0 = 0
    for name, _, padded in _SEGS:
        seg = flat[r0:r0 + padded]
        r0 += padded
        shape = _SHARD_SHAPES[name]
        n = int(np.prod(shape))
        if flat.dtype == BF16 and name == "ffn_conv_w":
            hi, mid, lo = (seg[i * CONV_TERM_ROWS:(i + 1) * CONV_TERM_ROWS].astype(F32) for i in range(3))
            seg = (hi + mid) + lo
        out[name] = seg.reshape(-1)[:n].reshape(shape)
    return out


def _pack_small(vals, loss=None):
    parts = [vals[name].astype(F32).reshape(-1) for name, _ in _SMALL]
    if loss is not None:
        parts.append(loss.reshape(1))
    flat = jnp.concatenate(parts)
    return jnp.pad(flat, (0, SMALL_ROWS * 1024 - flat.shape[0])).reshape(SMALL_ROWS, 1024)


def _unpack_small(flat):
    flat = flat.reshape(-1)
    out = {}
    o = 0
    for name, shape in _SMALL:
        n = int(np.prod(shape))
        out[name] = flat[o:o + n].reshape(shape)
        o += n
    return out, flat[o]


def _full_weights(gathered):
    per_chip = [_unpack_flat(gathered[k]) for k in range(N_CHIPS)]
    full = {name: jnp.concatenate([pc[name] for pc in per_chip], axis=_SHARD_AXIS[name]) for name in _SHARD_SHAPES}
    cw = full["ffn_conv_w"].reshape(2, 3, 2, FF).transpose(0, 2, 1, 3)
    return dict(a_w_in=full["a_w_in"][0], a_w_out=full["a_w_out"][0], b_w_q=full["b_w_q"][0], b_w_out=full["b_w_out"][0],
                w_kv=full["w_kv"], ffn_w_up=full["ffn_w_up"], ffn_w_down=full["ffn_w_down"], conv_w=cw)


def _shard_grads(g):
    full = {"a_w_in": g["a_w_in"][None], "a_w_out": g["a_w_out"][None], "b_w_q": g["b_w_q"][None],
            "b_w_out": g["b_w_out"][None], "w_kv": g["w_kv"], "ffn_w_up": g["ffn_w_up"], "ffn_w_down": g["ffn_w_down"],
            "ffn_conv_w": g["ffn_conv_w"]}
    flats = []
    for k in range(N_CHIPS):
        shards = {}
        for name, shape in _SHARD_SHAPES.items():
            ax = _SHARD_AXIS[name]
            n = shape[ax]
            shards[name] = lax.slice_in_dim(full[name], k * n, (k + 1) * n, axis=ax)
        flats.append(_pack_flat(shards, F32))
    return jnp.stack(flats, axis=0)


_WEIGHTS = ["a_w_in", "a_b_f", "a_w_out", "b_w_q", "b_w_out", "kv_norm_g", "w_kv", "mix_norm_g", "ffn_norm_g", "ffn_w_up",
            "ffn_conv_w", "ffn_conv_b", "ffn_w_down", "final_norm_g"]


def kernel(x, a_w_in, a_b_f, a_w_out, b_w_q, b_w_out, kv_norm_g, w_kv, mix_norm_g, ffn_norm_g, ffn_w_up, ffn_conv_w, ffn_conv_b, ffn_w_down, final_norm_g, loss_target, m_a_w_in, m_a_b_f, m_a_w_out, m_b_w_q, m_b_w_out, m_kv_norm_g, m_w_kv, m_mix_norm_g, m_ffn_norm_g, m_ffn_w_up, m_ffn_conv_w, m_ffn_conv_b, m_ffn_w_down, m_final_norm_g, v_a_w_in, v_a_b_f, v_a_w_out, v_b_w_q, v_b_w_out, v_kv_norm_g, v_w_kv, v_mix_norm_g, v_ffn_norm_g, v_ffn_w_up, v_ffn_conv_w, v_ffn_conv_b, v_ffn_w_down, v_final_norm_g):
    w = dict(a_w_in=a_w_in, a_b_f=a_b_f, a_w_out=a_w_out, b_w_q=b_w_q, b_w_out=b_w_out, kv_norm_g=kv_norm_g, w_kv=w_kv,
             mix_norm_g=mix_norm_g, ffn_norm_g=ffn_norm_g, ffn_w_up=ffn_w_up, ffn_conv_w=ffn_conv_w, ffn_conv_b=ffn_conv_b,
             ffn_w_down=ffn_w_down, final_norm_g=final_norm_g)
    m = dict(a_w_in=m_a_w_in, a_b_f=m_a_b_f, a_w_out=m_a_w_out, b_w_q=m_b_w_q, b_w_out=m_b_w_out, kv_norm_g=m_kv_norm_g,
             w_kv=m_w_kv, mix_norm_g=m_mix_norm_g, ffn_norm_g=m_ffn_norm_g, ffn_w_up=m_ffn_w_up, ffn_conv_w=m_ffn_conv_w,
             ffn_conv_b=m_ffn_conv_b, ffn_w_down=m_ffn_w_down, final_norm_g=m_final_norm_g)
    v = dict(a_w_in=v_a_w_in, a_b_f=v_a_b_f, a_w_out=v_a_w_out, b_w_q=v_b_w_q, b_w_out=v_b_w_out, kv_norm_g=v_kv_norm_g,
             w_kv=v_w_kv, mix_norm_g=v_mix_norm_g, ffn_norm_g=v_ffn_norm_g, ffn_w_up=v_ffn_w_up, ffn_conv_w=v_ffn_conv_w,
             ffn_conv_b=v_ffn_conv_b, ffn_w_down=v_ffn_w_down, final_norm_g=v_final_norm_g)

    c_arr = lax.axis_index("c").astype(jnp.int32).reshape(1)
    k_arr = (2 * lax.axis_index("x") + lax.axis_index("y")).astype(jnp.int32).reshape(1)
    w16 = _pack_flat(w, BF16)
    gathered = _allgather_shards(w16, _place_own(w16, k_arr))
    p = _full_weights(gathered)
    cb = ffn_conv_b.reshape(2, 2, 1, FF)
    p.update(a_b_f=a_b_f, kv_norm_g=kv_norm_g, mix_norm_g=mix_norm_g, ffn_norm_g=ffn_norm_g, final_norm_g=final_norm_g,
             conv_b=cb)

    loss_part, grad_x, g = _local_step(x[0], loss_target[0], p)

    gflat = _shard_grads(g)
    pair = _pair_sum(gflat, _sibling_swap_half(gflat), c_arr)
    g_mine = _chip_sum(pair, _chip_exchange(pair), k_arr)
    g_other = _sibling_send(g_mine)
    small, loss = _unpack_small(_allreduce_small(_pack_small(g, loss_part)))

    gsh, dw, mn, vn = _adamw_halves(_pack_flat(w, F32), g_mine, g_other, _pack_flat(m, F32), _pack_flat(v, F32), c_arr,
                                    "adamw_shards")
    dws, mns, vns = _adamw(_pack_small(w), _pack_small(small), _pack_small(m), _pack_small(v), "adamw_small")
    big = [_unpack_flat(t) for t in (gsh, dw, mn, vn)]
    sml = [small] + [_unpack_small(t)[0] for t in (dws, mns, vns)]
    outs = [loss, grad_x[None]]
    for b, s in zip(big, sml):
        outs += [b[n] if n in b else s[n] for n in _WEIGHTS]
    return tuple(outs)
```

```python
import functools
import math

import numpy as np
import jax
import jax.numpy as jnp
from jax import lax
from jax.experimental import pallas as pl
from jax.experimental.pallas import tpu as pltpu

F32 = jnp.float32
BF16 = jnp.bfloat16
MESH = pl.DeviceIdType.MESH

S = 4096
D = 1024
A_HEADS = 16
HEAD_DIM = 64
QKV_W = 3 * A_HEADS * HEAD_DIM
B_GROUPS = ((128, 1), (512, 4), (2048, 16))
B_HPG = 8
B_Q_W = 3 * B_HPG * HEAD_DIM
B_OUT_W = B_HPG * HEAD_DIM
B_KV_W = 2 * B_Q_W
B_WIN = 128
FF = 2816
RMS_EPS = 1e-6
SCALE = HEAD_DIM ** -0.5
N_CHIPS = 4

ADAM_LR, ADAM_B1, ADAM_B2, ADAM_EPS, ADAM_WD, ADAM_STEP = 0.001, 0.9, 0.999, 1e-08, 0.01, 10

V7X_VMEM_LIMIT = 48 * 1024 * 1024
LANES = 128
NEG_INF = float("-inf")

FLAT_W = LANES
_SEGS = (("a_w_in", 1, 1024, 772, 1024), ("a_w_out", 1, 256, 1024, 256), ("b_w_q", 1, 1024, 384, 1024),
         ("b_w_out", 1, 512, 256, 512), ("w_kv", 1, 1024, 768, 1024), ("ffn_w_up", 2, 1024, 1408, 1024),
         ("ffn_w_down", 2, 704, 1024, 704), ("ffn_conv_w", 1, 6, 1408, 16))
CONV_TERMS = 3


def _seg_rows(name, L, R, C, rpad):
    return (CONV_TERMS if name == "ffn_conv_w" else 1) * L * (-(-C // FLAT_W)) * rpad


FLAT_T = 2048
FLAT_ROWS = 57344
HALF_ROWS = FLAT_ROWS // 2
assert sum(_seg_rows(*s) for s in _SEGS) <= FLAT_ROWS and HALF_ROWS % FLAT_T == 0
SMALL_ROWS = 24


def _cparams(sem=None, **kw):
    return pltpu.CompilerParams(dimension_semantics=sem, vmem_limit_bytes=V7X_VMEM_LIMIT, **kw)


_DN = {"nn": (((1,), (0,)), ((), ())), "nt": (((1,), (1,)), ((), ())), "tn": (((0,), (0,)), ((), ()))}


def _mm(a, b, *, mode, tm, tn, tk, name, out_dtype=F32, res=None, a_split=0, b_split=0, o_split=0):
    if mode == "tn":
        K = a.shape[0]
        M = a.shape[1]
    else:
        M = a.shape[-2]
        K = a.shape[-1] * (2 if a_split else 1)
    if mode == "nt":
        N = b.shape[0]
    else:
        N = b.shape[-1] * (2 if b_split else 1)
    assert M % tm == 0 and N % tn == 0 and K % tk == 0, (name, M, N, K, tm, tn, tk)
    nk = K // tk

    if mode == "tn":
        a_spec = pl.BlockSpec((tk, tm), lambda i, j, k: (k, i))
    elif a_split:
        a_spec = pl.BlockSpec((None, tm, tk), lambda i, j, k: (k // a_split, i, k % a_split))
    else:
        a_spec = pl.BlockSpec((tm, tk), lambda i, j, k: (i, k))
    if mode == "nt":
        b_spec = pl.BlockSpec((tn, tk), lambda i, j, k: (j, k))
    elif b_split:
        b_spec = pl.BlockSpec((None, tk, tn), lambda i, j, k: (j // b_split, k, j % b_split))
    else:
        b_spec = pl.BlockSpec((tk, tn), lambda i, j, k: (k, j))
    if o_split:
        o_spec = pl.BlockSpec((None, tm, tn), lambda i, j, k: (j // o_split, i, j % o_split))
        out_shape = jax.ShapeDtypeStruct((2, M, N // 2), out_dtype)
    else:
        o_spec = pl.BlockSpec((tm, tn), lambda i, j, k: (i, j))
        out_shape = jax.ShapeDtypeStruct((M, N), out_dtype)
    in_specs = [a_spec, b_spec]
    args = [a, b]
    if res is not None:
        in_specs.append(pl.BlockSpec((tm, tn), lambda i, j, k: (i, j)))
        args.append(res)

    def body(*refs):
        if res is not None:
            a_ref, b_ref, r_ref, o_ref = refs[:4]
        else:
            a_ref, b_ref, o_ref = refs[:3]
            r_ref = None
        p = lax.dot_general(a_ref[...].astype(BF16), b_ref[...].astype(BF16), _DN[mode], preferred_element_type=F32)

        def finish(r):
            if r_ref is not None:
                r = r + r_ref[...]
            o_ref[...] = r.astype(out_dtype)

        if nk == 1:
            finish(p)
        else:
            acc = refs[-1]
            k = pl.program_id(2)

            @pl.when(k == 0)
            def _():
                acc[...] = p

            @pl.when(k > 0)
            def _():
                acc[...] += p

            @pl.when(k == nk - 1)
            def _():
                finish(acc[...])

    return pl.pallas_call(
        body, out_shape=out_shape, grid=(M // tm, N // tn, nk), in_specs=in_specs, out_specs=o_spec,
        scratch_shapes=[pltpu.VMEM((tm, tn), F32)] if nk > 1 else [],
        compiler_params=_cparams(("parallel", "parallel", "arbitrary")), name=name,
    )(*args)


NORM_ROWS = 256


def _rms_fwd(x, g, name):
    def body(x_ref, g_ref, o_ref):
        xv = x_ref[...]
        r = lax.rsqrt(jnp.mean(xv * xv, axis=-1, keepdims=True) + RMS_EPS)
        o_ref[...] = (xv * r * g_ref[...]).astype(BF16)

    row = pl.BlockSpec((NORM_ROWS, D), lambda i: (i, 0))
    return pl.pallas_call(
        body, out_shape=jax.ShapeDtypeStruct((S, D), BF16), grid=(S // NORM_ROWS,),
        in_specs=[row, pl.BlockSpec((1, D), lambda i: (0, 0))], out_specs=row,
        compiler_params=_cparams(("parallel",)), name=name,
    )(x, g.reshape(1, D))


def _rms_bwd(x, dres, pairs, name):
    n = len(pairs)

    def body(*refs):
        x_ref, dres_ref = refs[0], refs[1]
        g_refs = refs[2:2 + 2 * n:2]
        dh_refs = refs[3:3 + 2 * n:2]
        dx_ref, dxb_ref = refs[2 + 2 * n], refs[3 + 2 * n]
        dg_refs = refs[4 + 2 * n:]
        i = pl.program_id(0)
        xv = x_ref[...]
        r = lax.rsqrt(jnp.mean(xv * xv, axis=-1, keepdims=True) + RMS_EPS)
        y = xv * r
        dx = dres_ref[...]
        for g_ref, dh_ref, dg_ref in zip(g_refs, dh_refs, dg_refs):
            dh = dh_ref[...]
            dy = dh * g_ref[...]
            dx = dx + r * (dy - y * jnp.mean(dy * y, axis=-1, keepdims=True))
            part = jnp.sum(dh * y, axis=0, keepdims=True)

            @pl.when(i == 0)
            def _():
                dg_ref[...] = part

            @pl.when(i > 0)
            def _():
                dg_ref[...] += part

        dx_ref[...] = dx
        dxb_ref[...] = dx.astype(BF16)

    row = pl.BlockSpec((NORM_ROWS, D), lambda i: (i, 0))
    vec = pl.BlockSpec((1, D), lambda i: (0, 0))
    in_specs = [row, row]
    args = [x, dres]
    for g, dh in pairs:
        in_specs += [vec, row]
        args += [g.reshape(1, D), dh]
    outs = pl.pallas_call(
        body,
        out_shape=[jax.ShapeDtypeStruct((S, D), F32), jax.ShapeDtypeStruct((S, D), BF16)]
        + [jax.ShapeDtypeStruct((1, D), F32)] * n,
        grid=(S // NORM_ROWS,), in_specs=in_specs, out_specs=[row, row] + [vec] * n,
        compiler_params=_cparams(("arbitrary",)), name=name,
    )(*args)
    return outs[0], outs[1], list(outs[2:])


def _loss_head(x, g, target, name):
    def body(x_ref, g_ref, t_ref, loss_ref, dx_ref, dxb_ref, dg_ref):
        i = pl.program_id(0)
        xv = x_ref[...]
        gv = g_ref[...]
        r = lax.rsqrt(jnp.mean(xv * xv, axis=-1, keepdims=True) + RMS_EPS)
        y = xv * r
        err = y * gv - t_ref[...]
        lpart = jnp.broadcast_to(jnp.sum(err * err, keepdims=True) * (0.5 / D), (1, LANES))
        dh = err * (1.0 / D)
        dy = dh * gv
        dx = r * (dy - y * jnp.mean(dy * y, axis=-1, keepdims=True))
        part = jnp.sum(dh * y, axis=0, keepdims=True)

        @pl.when(i == 0)
        def _():
            dg_ref[...] = part
            loss_ref[...] = lpart

        @pl.when(i > 0)
        def _():
            dg_ref[...] += part
            loss_ref[...] += lpart

        dx_ref[...] = dx
        dxb_ref[...] = dx.astype(BF16)

    row = pl.BlockSpec((NORM_ROWS, D), lambda i: (i, 0))
    vec = pl.BlockSpec((1, D), lambda i: (0, 0))
    return pl.pallas_call(
        body,
        out_shape=[jax.ShapeDtypeStruct((1, LANES), F32), jax.ShapeDtypeStruct((S, D), F32),
                   jax.ShapeDtypeStruct((S, D), BF16), jax.ShapeDtypeStruct((1, D), F32)],
        grid=(S // NORM_ROWS,), in_specs=[row, vec, row],
        out_specs=[pl.BlockSpec((1, LANES), lambda i: (0, 0)), row, row, vec],
        compiler_params=_cparams(("arbitrary",)), name=name,
    )(x, g.reshape(1, D), target)


SCAN_ROWS = 256


def _split3(v):
    hi = v.astype(BF16)
    r1 = v - hi.astype(F32)
    mid = r1.astype(BF16)
    lo = (r1 - mid.astype(F32)).astype(BF16)
    return hi, mid, lo


def _tri_dot(tri, v):
    hi, mid, lo = _split3(v)
    dn = _DN["nn"]
    return (lax.dot_general(tri, hi, dn, preferred_element_type=F32)
            + lax.dot_general(tri, mid, dn, preferred_element_type=F32)
            + lax.dot_general(tri, lo, dn, preferred_element_type=F32))


def _log_sigmoid(z):
    return jnp.minimum(z, 0.0) - jnp.log(1.0 + jnp.exp(-jnp.abs(z)))


def _fgate_fwd(pf, bias, name):
    tri = jnp.tril(jnp.ones((SCAN_ROWS, SCAN_ROWS), F32)).astype(BF16)

    def body(pf_ref, b_ref, tri_ref, c_ref):
        carry = jnp.zeros((1, LANES), F32)
        for blk in range(S // SCAN_ROWS):
            rows = pl.ds(blk * SCAN_ROWS, SCAN_ROWS)
            lf = _log_sigmoid(pf_ref[rows, :] + b_ref[...])
            c_ref[rows, :] = _tri_dot(tri_ref[...], lf) + carry
            carry = c_ref[pl.ds(blk * SCAN_ROWS + SCAN_ROWS - 1, 1), :]

    return pl.pallas_call(
        body, out_shape=jax.ShapeDtypeStruct((S, LANES), F32),
        compiler_params=_cparams(), name=name,
    )(pf, bias, tri)


def _fgate_bwd(pf, bias, dc_key, dc_query, name):
    triu = jnp.triu(jnp.ones((SCAN_ROWS, SCAN_ROWS), F32)).astype(BF16)

    def body(pf_ref, b_ref, dck_ref, dcq_ref, tri_ref, dpf_ref, db_ref, dlf_ref):
        carry = jnp.zeros((1, LANES), F32)
        db = jnp.zeros((1, LANES), F32)
        lane = lax.broadcasted_iota(jnp.int32, (SCAN_ROWS, LANES), 1)
        for blk in reversed(range(S // SCAN_ROWS)):
            rows = pl.ds(blk * SCAN_ROWS, SCAN_ROWS)
            dc = dck_ref[rows, :] + dcq_ref[rows, :]
            dlf_ref[rows, :] = _tri_dot(tri_ref[...], dc) + carry
            carry = dlf_ref[pl.ds(blk * SCAN_ROWS, 1), :]
            z = pf_ref[rows, :] + b_ref[...]
            e = jnp.exp(-jnp.abs(z))
            sig_neg = jnp.where(z >= 0.0, e, 1.0) / (1.0 + e)
            dz = jnp.where(lane < A_HEADS, dlf_ref[rows, :] * sig_neg, 0.0)
            dpf_ref[rows, :] = dz.astype(BF16)
            db = db + jnp.sum(dz, axis=0, keepdims=True)
        db_ref[...] = db

    return pl.pallas_call(
        body, out_shape=[jax.ShapeDtypeStruct((S, LANES), BF16), jax.ShapeDtypeStruct((1, LANES), F32)],
        scratch_shapes=[pltpu.VMEM((S, LANES), F32)],
        compiler_params=_cparams(), name=name,
    )(pf, bias, dc_key, dc_query, triu)


FOX_T = 512


def _fox_fwd(q, k, v, ccol, crow, name):
    H = q.shape[0]
    T = FOX_T
    nq = S // T

    def body(q_ref, k_ref, v_ref, cc_ref, cr_ref, o_ref, lse_ref, m_sc, l_sc, acc_sc):
        i = pl.program_id(1)
        j = pl.program_id(2)

        @pl.when(j == 0)
        def _():
            m_sc[...] = jnp.full((T, 1), NEG_INF, F32)
            l_sc[...] = jnp.zeros((T, 1), F32)
            acc_sc[...] = jnp.zeros((T, HEAD_DIM), F32)

        def step(diagonal):
            s = lax.dot_general(q_ref[...], k_ref[...], _DN["nt"], preferred_element_type=F32) * SCALE
            s = s + (cc_ref[...] - cr_ref[...])
            if diagonal:
                row = lax.broadcasted_iota(jnp.int32, (T, T), 0)
                col = lax.broadcasted_iota(jnp.int32, (T, T), 1)
                s = jnp.where(row >= col, s, NEG_INF)
            m_prev = m_sc[...]
            m_new = jnp.maximum(m_prev, jnp.max(s, axis=1, keepdims=True))
            alpha = jnp.exp(m_prev - m_new)
            p = jnp.exp(s - m_new)
            l_sc[...] = alpha * l_sc[...] + jnp.sum(p, axis=1, keepdims=True)
            acc_sc[...] = alpha * acc_sc[...] + lax.dot_general(p.astype(BF16), v_ref[...], _DN["nn"],
                                                                preferred_element_type=F32)
            m_sc[...] = m_new

        @pl.when(j < i)
        def _():
            step(False)

        @pl.when(j == i)
        def _():
            step(True)
            o_ref[...] = (acc_sc[...] / l_sc[...]).astype(BF16)
            lse_ref[...] = m_sc[...] + jnp.log(l_sc[...])

    qs = pl.BlockSpec((None, T, HEAD_DIM), lambda h, i, j: (h, i, 0))
    ks = pl.BlockSpec((None, T, HEAD_DIM), lambda h, i, j: (h, jnp.minimum(i, j), 0))
    col = pl.BlockSpec((None, T, 1), lambda h, i, j: (h, i, 0))
    rowk = pl.BlockSpec((None, 1, T), lambda h, i, j: (h, 0, jnp.minimum(i, j)))
    return pl.pallas_call(
        body, out_shape=[jax.ShapeDtypeStruct((H, S, HEAD_DIM), BF16), jax.ShapeDtypeStruct((H, S, 1), F32)],
        grid=(H, nq, nq), in_specs=[qs, ks, ks, col, rowk], out_specs=[qs, col],
        scratch_shapes=[pltpu.VMEM((T, 1), F32), pltpu.VMEM((T, 1), F32), pltpu.VMEM((T, HEAD_DIM), F32)],
        compiler_params=_cparams(("parallel", "parallel", "arbitrary")), name=name,
    )(q, k, v, ccol, crow)


def _fox_bwd(q, k, v, do, lse_row, delta_row, cq_row, ck_col, name):
    H = q.shape[0]
    T = FOX_T
    nq = S // T

    def body(q_ref, k_ref, v_ref, do_ref, lse_ref, dl_ref, cq_ref, ck_ref, dq_ref, dk_ref, dv_ref, dc_ref, dcq_ref,
             dk_sc, dv_sc, dc_sc):
        j = pl.program_id(1)
        i = pl.program_id(2)

        @pl.when(jnp.logical_and(j == 0, i == 0))
        def _():
            dq_ref[...] = jnp.zeros((S, HEAD_DIM), F32)
            dcq_ref[...] = jnp.zeros((nq, 1, T), F32)

        @pl.when(i == j)
        def _():
            dk_sc[...] = jnp.zeros((T, HEAD_DIM), F32)
            dv_sc[...] = jnp.zeros((T, HEAD_DIM), F32)
            dc_sc[...] = jnp.zeros((T, 1), F32)

        def step(diagonal):
            qv = q_ref[...]
            kv = k_ref[...]
            dov = do_ref[...]
            st = lax.dot_general(kv, qv, _DN["nt"], preferred_element_type=F32) * SCALE
            st = st + (cq_ref[...] - ck_ref[...])
            if diagonal:
                row = lax.broadcasted_iota(jnp.int32, (T, T), 0)
                col = lax.broadcasted_iota(jnp.int32, (T, T), 1)
                st = jnp.where(col >= row, st, NEG_INF)
            pt = jnp.exp(st - lse_ref[...])
            dv_sc[...] += lax.dot_general(pt.astype(BF16), dov, _DN["nn"], preferred_element_type=F32)
            dpt = lax.dot_general(v_ref[...], dov, _DN["nt"], preferred_element_type=F32)
            dst = pt * (dpt - dl_ref[...])
            dc_sc[...] -= jnp.sum(dst, axis=1, keepdims=True)
            dcq_ref[i] += jnp.sum(dst, axis=0, keepdims=True)
            dsb = (dst * SCALE).astype(BF16)
            dk_sc[...] += lax.dot_general(dsb, qv, _DN["nn"], preferred_element_type=F32)
            rows = pl.ds(pl.multiple_of(i * T, T), T)
            dq_ref[rows, :] += lax.dot_general(dsb, kv, _DN["tn"], preferred_element_type=F32)

        @pl.when(i > j)
        def _():
            step(False)

        @pl.when(i == j)
        def _():
            step(True)

        @pl.when(i == nq - 1)
        def _():
            dk_ref[...] = dk_sc[...].astype(BF16)
            dv_ref[...] = dv_sc[...].astype(BF16)
            dc_ref[...] = dc_sc[...]

    qs = pl.BlockSpec((None, T, HEAD_DIM), lambda h, j, i: (h, jnp.maximum(i, j), 0))
    qrow = pl.BlockSpec((None, 1, T), lambda h, j, i: (h, 0, jnp.maximum(i, j)))
    ks = pl.BlockSpec((None, T, HEAD_DIM), lambda h, j, i: (h, j, 0))
    kcol = pl.BlockSpec((None, T, 1), lambda h, j, i: (h, j, 0))
    dqs = pl.BlockSpec((None, S, HEAD_DIM), lambda h, j, i: (h, 0, 0))
    dcqs = pl.BlockSpec((None, nq, 1, T), lambda h, j, i: (h, 0, 0, 0))
    return pl.pallas_call(
        body,
        out_shape=[jax.ShapeDtypeStruct((H, S, HEAD_DIM), F32), jax.ShapeDtypeStruct((H, S, HEAD_DIM), BF16),
                   jax.ShapeDtypeStruct((H, S, HEAD_DIM), BF16), jax.ShapeDtypeStruct((H, S, 1), F32),
                   jax.ShapeDtypeStruct((H, nq, 1, T), F32)],
        grid=(H, nq, nq), in_specs=[qs, ks, ks, qs, qrow, qrow, qrow, kcol], out_specs=[dqs, ks, ks, kcol, dcqs],
        scratch_shapes=[pltpu.VMEM((T, HEAD_DIM), F32), pltpu.VMEM((T, HEAD_DIM), F32), pltpu.VMEM((T, 1), F32)],
        compiler_params=_cparams(("parallel", "arbitrary", "arbitrary")), name=name,
    )(q, k, v, do, lse_row, delta_row, cq_row, ck_col)


def _rowdot(a, b, name):
    H = a.shape[0]
    T = 1024

    def body(a_ref, b_ref, o_ref):
        o_ref[...] = jnp.sum(a_ref[...].astype(F32) * b_ref[...].astype(F32), axis=-1, keepdims=True)

    blk = pl.BlockSpec((None, T, HEAD_DIM), lambda h, i: (h, i, 0))
    return pl.pallas_call(
        body, out_shape=jax.ShapeDtypeStruct((H, S, 1), F32), grid=(H, S // T), in_specs=[blk, blk],
        out_specs=pl.BlockSpec((None, T, 1), lambda h, i: (h, i, 0)),
        compiler_params=_cparams(("parallel", "parallel")), name=name,
    )(a, b)


W = B_WIN
N_HG = 3 * B_HPG
N_BLK = S // W


def _dil_tables():
    slopes = np.exp2((-8.0 * np.arange(1, N_HG + 1, dtype=np.float32) / N_HG).astype(np.float32)).astype(np.float32)
    dil = np.repeat(np.array([d for _, d in B_GROUPS], np.float32), B_HPG)
    coef = (slopes * dil).astype(np.float32)
    nbs = np.repeat(np.array([S // d // W for _, d in B_GROUPS], np.int32), B_HPG)
    return jnp.asarray(coef), jnp.asarray(nbs)


DIL_SUB = 8
DIL_ROWS = DIL_SUB * W
DIL_STEPS = S // DIL_ROWS


def _dil_bias(coef, transposed):
    row = lax.broadcasted_iota(jnp.int32, (W, 2 * W), 0)
    col = lax.broadcasted_iota(jnp.int32, (W, 2 * W), 1)
    dist = (col - row) if transposed else (row + W - col)
    valid = jnp.logical_and(dist >= 0, dist <= W)
    return jnp.where(valid, -coef * dist.astype(F32), NEG_INF), col


def _dil_specs():
    blk = pl.BlockSpec((None, DIL_ROWS, HEAD_DIM), lambda h, n: (h, n, 0))
    prev = pl.BlockSpec((None, W, HEAD_DIM), lambda h, n: (h, jnp.maximum(n * DIL_SUB - 1, 0), 0))
    nxt = pl.BlockSpec((None, W, HEAD_DIM), lambda h, n: (h, jnp.minimum((n + 1) * DIL_SUB, N_BLK - 1), 0))
    col = pl.BlockSpec((None, DIL_ROWS, 1), lambda h, n: (h, n, 0))
    row = pl.BlockSpec((None, 1, DIL_ROWS), lambda h, n: (h, 0, n))
    rnxt = pl.BlockSpec((None, 1, W), lambda h, n: (h, 0, jnp.minimum((n + 1) * DIL_SUB, N_BLK - 1)))
    smem = pl.BlockSpec(memory_space=pltpu.SMEM)
    return blk, prev, nxt, col, row, rnxt, smem


def _dil_fwd(q, k, v, name):
    coef_t, nbs_t = _dil_tables()

    def body(coef_ref, nbs_ref, q_ref, kh_ref, k_ref, vh_ref, v_ref, o_ref, lse_ref, kf, vf):
        hg = pl.program_id(0)
        n = pl.program_id(1)
        nbs = nbs_ref[hg]
        kf[0:W, :] = kh_ref[...]
        kf[W:, :] = k_ref[...]
        vf[0:W, :] = vh_ref[...]
        vf[W:, :] = v_ref[...]
        bias, col = _dil_bias(coef_ref[hg], False)
        for b in range(DIL_SUB):
            first = lax.rem(n * DIL_SUB + b, nbs) == 0
            rows = slice(b * W, (b + 1) * W)
            both = slice(b * W, (b + 2) * W)
            s = lax.dot_general(q_ref[rows, :], kf[both, :], _DN["nt"], preferred_element_type=F32) * SCALE + bias
            s = jnp.where(jnp.logical_and(first, col < W), NEG_INF, s)
            m = jnp.max(s, axis=1, keepdims=True)
            p = jnp.exp(s - m)
            l = jnp.sum(p, axis=1, keepdims=True)
            acc = lax.dot_general(p.astype(BF16), vf[both, :], _DN["nn"], preferred_element_type=F32)
            o_ref[rows, :] = acc / l
            lse_ref[rows, :] = m + jnp.log(l)

    blk, prev, _, col, _, _, smem = _dil_specs()
    return pl.pallas_call(
        body, out_shape=[jax.ShapeDtypeStruct((N_HG, S, HEAD_DIM), F32), jax.ShapeDtypeStruct((N_HG, S, 1), F32)],
        grid=(N_HG, DIL_STEPS), in_specs=[smem, smem, blk, prev, blk, prev, blk], out_specs=[blk, col],
        scratch_shapes=[pltpu.VMEM((DIL_ROWS + W, HEAD_DIM), BF16)] * 2,
        compiler_params=_cparams(("parallel", "parallel")), name=name,
    )(coef_t, nbs_t, q, k, k, v, v)


def _dil_merge(o, lse, name):
    T = 1024

    def body(o_ref, lse_ref, om_ref, omb_ref, l_ref):
        l0, l1, l2 = lse_ref[0], lse_ref[1], lse_ref[2]
        m = jnp.maximum(jnp.maximum(l0, l1), l2)
        e0, e1, e2 = jnp.exp(l0 - m), jnp.exp(l1 - m), jnp.exp(l2 - m)
        den = e0 + e1 + e2
        om = (e0 / den) * o_ref[0] + (e1 / den) * o_ref[1] + (e2 / den) * o_ref[2]
        om_ref[...] = om
        omb_ref[...] = om.astype(BF16)
        l_ref[...] = m + jnp.log(den)

    ob = pl.BlockSpec((None, T, HEAD_DIM), lambda h, i: (h, i, 0))
    lb = pl.BlockSpec((None, T, 1), lambda h, i: (h, i, 0))
    return pl.pallas_call(
        body,
        out_shape=[jax.ShapeDtypeStruct((B_HPG, S, HEAD_DIM), F32), jax.ShapeDtypeStruct((B_HPG, S, HEAD_DIM), BF16),
                   jax.ShapeDtypeStruct((B_HPG, S, 1), F32)],
        grid=(B_HPG, S // T),
        in_specs=[pl.BlockSpec((3, None, T, HEAD_DIM), lambda h, i: (0, h, i, 0)),
                  pl.BlockSpec((3, None, T, 1), lambda h, i: (0, h, i, 0))],
        out_specs=[ob, ob, lb], compiler_params=_cparams(("parallel", "parallel")), name=name,
    )(o, lse)


def _dil_bwd_dq(q, k, v, do, lcol, dcol, name):
    coef_t, nbs_t = _dil_tables()

    def body(coef_ref, nbs_ref, q_ref, kh_ref, k_ref, vh_ref, v_ref, do_ref, l_ref, d_ref, dq_ref, kf, vf):
        hg = pl.program_id(0)
        n = pl.program_id(1)
        nbs = nbs_ref[hg]
        kf[0:W, :] = kh_ref[...]
        kf[W:, :] = k_ref[...]
        vf[0:W, :] = vh_ref[...]
        vf[W:, :] = v_ref[...]
        bias, col = _dil_bias(coef_ref[hg], False)
        for b in range(DIL_SUB):
            first = lax.rem(n * DIL_SUB + b, nbs) == 0
            rows = slice(b * W, (b + 1) * W)
            both = slice(b * W, (b + 2) * W)
            kk = kf[both, :]
            s = lax.dot_general(q_ref[rows, :], kk, _DN["nt"], preferred_element_type=F32) * SCALE + bias
            s = jnp.where(jnp.logical_and(first, col < W), NEG_INF, s)
            p = jnp.exp(s - l_ref[rows, :])
            dp = lax.dot_general(do_ref[rows, :], vf[both, :], _DN["nt"], preferred_element_type=F32)
            ds = (p * (dp - d_ref[rows, :]) * SCALE).astype(BF16)
            dq_ref[rows, :] = lax.dot_general(ds, kk, _DN["nn"], preferred_element_type=F32).astype(BF16)

    blk, prev, _, col, _, _, smem = _dil_specs()
    return pl.pallas_call(
        body, out_shape=jax.ShapeDtypeStruct((N_HG, S, HEAD_DIM), BF16), grid=(N_HG, DIL_STEPS),
        in_specs=[smem, smem, blk, prev, blk, prev, blk, blk, col, col], out_specs=blk,
        scratch_shapes=[pltpu.VMEM((DIL_ROWS + W, HEAD_DIM), BF16)] * 2,
        compiler_params=_cparams(("parallel", "parallel")), name=name,
    )(coef_t, nbs_t, q, k, k, v, v, do, lcol, dcol)


def _dil_bwd_dkv(q, k, v, do, lrow, drow, name):
    coef_t, nbs_t = _dil_tables()

    def body(coef_ref, nbs_ref, k_ref, v_ref, q_ref, qn_ref, do_ref, don_ref, l_ref, ln_ref, d_ref, dn_ref,
             dk_ref, dv_ref, qf, dof, lf, df):
        hg = pl.program_id(0)
        n = pl.program_id(1)
        nbs = nbs_ref[hg]
        qf[0:DIL_ROWS, :] = q_ref[...]
        qf[DIL_ROWS:, :] = qn_ref[...]
        dof[0:DIL_ROWS, :] = do_ref[...]
        dof[DIL_ROWS:, :] = don_ref[...]
        lf[:, 0:DIL_ROWS] = l_ref[...]
        lf[:, DIL_ROWS:] = ln_ref[...]
        df[:, 0:DIL_ROWS] = d_ref[...]
        df[:, DIL_ROWS:] = dn_ref[...]
        bias, col = _dil_bias(coef_ref[hg], True)
        for b in range(DIL_SUB):
            no_next = lax.rem(n * DIL_SUB + b + 1, nbs) == 0
            rows = slice(b * W, (b + 1) * W)
            both = slice(b * W, (b + 2) * W)
            qq = qf[both, :]
            dd = dof[both, :]
            st = lax.dot_general(k_ref[rows, :], qq, _DN["nt"], preferred_element_type=F32) * SCALE + bias
            st = jnp.where(jnp.logical_and(no_next, col >= W), NEG_INF, st)
            pt = jnp.exp(st - lf[:, both])
            dv_ref[rows, :] = lax.dot_general(pt.astype(BF16), dd, _DN["nn"], preferred_element_type=F32).astype(BF16)
            dpt = lax.dot_general(v_ref[rows, :], dd, _DN["nt"], preferred_element_type=F32)
            dst = (pt * (dpt - df[:, both]) * SCALE).astype(BF16)
            dk_ref[rows, :] = lax.dot_general(dst, qq, _DN["nn"], preferred_element_type=F32).astype(BF16)

    blk, _, nxt, _, row, rnxt, smem = _dil_specs()
    return pl.pallas_call(
        body, out_shape=[jax.ShapeDtypeStruct((N_HG, S, HEAD_DIM), BF16)] * 2, grid=(N_HG, DIL_STEPS),
        in_specs=[smem, smem, blk, blk, blk, nxt, blk, nxt, row, rnxt, row, rnxt], out_specs=[blk, blk],
        scratch_shapes=[pltpu.VMEM((DIL_ROWS + W, HEAD_DIM), BF16)] * 2 + [pltpu.VMEM((1, DIL_ROWS + W), F32)] * 2,
        compiler_params=_cparams(("parallel", "parallel")), name=name,
    )(coef_t, nbs_t, k, v, q, q, do, do, lrow, lrow, drow, drow)


FFN_ROWS = 512
FFN_COLS = 256
HALO = 8


def _shifted(u, halo, back):
    T = u.shape[0]
    rows = lax.broadcasted_iota(jnp.int32, u.shape, 0)
    if back:
        s1 = jnp.where(rows == 0, halo[HALO - 1:HALO, :], pltpu.roll(u, 1, 0))
        s2 = jnp.where(rows == 0, halo[HALO - 2:HALO - 1, :],
                       jnp.where(rows == 1, halo[HALO - 1:HALO, :], pltpu.roll(u, 2, 0)))
    else:
        s1 = jnp.where(rows == T - 1, halo[0:1, :], pltpu.roll(u, T - 1, 0))
        s2 = jnp.where(rows == T - 1, halo[1:2, :],
                       jnp.where(rows == T - 2, halo[0:1, :], pltpu.roll(u, T - 2, 0)))
    return s1, s2


def _conv_parts(u_ref, h_ref, w_ref, b_ref, first):
    out = []
    for p in range(2):
        u = u_ref[p]
        halo = jnp.where(first, 0.0, h_ref[p])
        u1, u2 = _shifted(u, halo, True)
        w = w_ref[p]
        out.append((w[0:1, :] * u2 + w[1:2, :] * u1 + w[2:3, :] * u + b_ref[p], u1, u2, u))
    return out


def _ffn_specs():
    T, C = FFN_ROWS, FFN_COLS
    blk = pl.BlockSpec((2, T, C), lambda j, i: (0, i, j))
    prev = pl.BlockSpec((2, HALO, C), lambda j, i: (0, jnp.maximum(i * (T // HALO) - 1, 0), j))
    nxt = pl.BlockSpec((2, HALO, C), lambda j, i: (0, jnp.minimum((i + 1) * (T // HALO), S // HALO - 1), j))
    wsp = pl.BlockSpec((2, 3, C), lambda j, i: (0, 0, j))
    bsp = pl.BlockSpec((2, 1, C), lambda j, i: (0, 0, j))
    one = pl.BlockSpec((T, C), lambda j, i: (i, j))
    return blk, prev, nxt, wsp, bsp, one


def _ffn_act_fwd(u, w, b, name):
    blk, prev, _, wsp, bsp, one = _ffn_specs()

    def body(u_ref, h_ref, w_ref, b_ref, o_ref):
        (a, _, _, _), (g, _, _, _) = _conv_parts(u_ref, h_ref, w_ref, b_ref, pl.program_id(1) == 0)
        o_ref[...] = (g / (1.0 + jnp.exp(-g)) * a).astype(BF16)

    return pl.pallas_call(
        body, out_shape=jax.ShapeDtypeStruct((S, FF), BF16), grid=(FF // FFN_COLS, S // FFN_ROWS),
        in_specs=[blk, prev, wsp, bsp], out_specs=one,
        compiler_params=_cparams(("parallel", "parallel")), name=name,
    )(u, u, w, b)


def _ffn_act_bwd(u, dact, w, b, name):
    blk, prev, _, wsp, bsp, one = _ffn_specs()

    def body(u_ref, h_ref, da_ref, w_ref, b_ref, duc_ref, dwb_ref):
        i = pl.program_id(1)
        (a, a1, a2, a0), (g, g1, g2, g0) = _conv_parts(u_ref, h_ref, w_ref, b_ref, i == 0)
        dact_v = da_ref[...]
        sg = 1.0 / (1.0 + jnp.exp(-g))
        d_a = dact_v * (g * sg)
        d_g = dact_v * a * (sg * (1.0 + g * (1.0 - sg)))
        duc_ref[0] = d_a
        duc_ref[1] = d_g

        @pl.when(i == 0)
        def _():
            dwb_ref[...] = jnp.zeros(dwb_ref.shape, F32)

        for p, (d, s2, s1, s0) in enumerate(((d_a, a2, a1, a0), (d_g, g2, g1, g0))):
            dwb_ref[p, 0:1, :] += jnp.sum(d * s2, axis=0, keepdims=True)
            dwb_ref[p, 1:2, :] += jnp.sum(d * s1, axis=0, keepdims=True)
            dwb_ref[p, 2:3, :] += jnp.sum(d * s0, axis=0, keepdims=True)
            dwb_ref[p, 3:4, :] += jnp.sum(d, axis=0, keepdims=True)

    return pl.pallas_call(
        body, out_shape=[jax.ShapeDtypeStruct((2, S, FF), F32), jax.ShapeDtypeStruct((2, 8, FF), F32)],
        grid=(FF // FFN_COLS, S // FFN_ROWS), in_specs=[blk, prev, one, wsp, bsp],
        out_specs=[blk, pl.BlockSpec((2, 8, FFN_COLS), lambda j, i: (0, 0, j))],
        compiler_params=_cparams(("parallel", "arbitrary")), name=name,
    )(u, u, dact, w, b)


def _ffn_conv_bwd(duc, w, name):
    blk, _, nxt, wsp, _, _ = _ffn_specs()
    last = S // FFN_ROWS - 1

    def body(d_ref, h_ref, w_ref, du_ref):
        is_last = pl.program_id(1) == last
        for p in range(2):
            d = d_ref[p]
            halo = jnp.where(is_last, 0.0, h_ref[p])
            d1, d2 = _shifted(d, halo, False)
            wv = w_ref[p]
            du_ref[p] = (wv[2:3, :] * d + wv[1:2, :] * d1 + wv[0:1, :] * d2).astype(BF16)

    return pl.pallas_call(
        body, out_shape=jax.ShapeDtypeStruct((2, S, FF), BF16), grid=(FF // FFN_COLS, S // FFN_ROWS),
        in_specs=[blk, nxt, wsp], out_specs=blk,
        compiler_params=_cparams(("parallel", "parallel")), name=name,
    )(duc, duc, w)


def _adam_update(w, gv, m, v):
    c1 = 1.0 / (1.0 - ADAM_B1 ** ADAM_STEP)
    c2 = 1.0 / (1.0 - ADAM_B2 ** ADAM_STEP)
    mn = ADAM_B1 * m + (1.0 - ADAM_B1) * gv
    vn = ADAM_B2 * v + (1.0 - ADAM_B2) * (gv * gv)
    return -ADAM_LR * ((mn * c1) / (jnp.sqrt(vn * c2) + ADAM_EPS) + ADAM_WD * w), mn, vn


def _adamw_halves(w, g_mine, g_other, m, v, c_arr, name):
    T = FLAT_T
    nb = HALF_ROWS // T

    def body(c_ref, w_ref, gm_ref, go_ref, m_ref, v_ref, g_ref, d_ref, mo_ref, vo_ref):
        is_mine = (pl.program_id(0) // nb) == c_ref[0]
        gv = jnp.where(is_mine, gm_ref[...], go_ref[...])
        g_ref[...] = gv
        d_ref[...], mo_ref[...], vo_ref[...] = _adam_update(w_ref[...], gv, m_ref[...], v_ref[...])

    blk = pl.BlockSpec((T, FLAT_W), lambda i, c: (i, 0))
    mine = pl.BlockSpec((T, FLAT_W), lambda i, c: (jnp.clip(i - c[0] * nb, 0, nb - 1), 0))
    other = pl.BlockSpec((T, FLAT_W), lambda i, c: (jnp.clip(i - (1 - c[0]) * nb, 0, nb - 1), 0))
    sds = jax.ShapeDtypeStruct((FLAT_ROWS, FLAT_W), F32)
    return pl.pallas_call(
        body, out_shape=[sds] * 4,
        grid_spec=pltpu.PrefetchScalarGridSpec(num_scalar_prefetch=1, grid=(FLAT_ROWS // T,),
                                               in_specs=[blk, mine, other, blk, blk], out_specs=[blk] * 4),
        compiler_params=_cparams(("parallel",)), name=name,
    )(c_arr, w, g_mine, g_other, m, v)


def _adamw(w, g, m, v, name):
    rows = w.shape[0]
    T = 8
    for cand in (256, 128, 64, 32, 16, 8):
        if rows % cand == 0:
            T = cand
            break

    def body(w_ref, g_ref, m_ref, v_ref, d_ref, mo_ref, vo_ref):
        d_ref[...], mo_ref[...], vo_ref[...] = _adam_update(w_ref[...], g_ref[...], m_ref[...], v_ref[...])

    blk = pl.BlockSpec((T, w.shape[1]), lambda i: (i, 0))
    sds = jax.ShapeDtypeStruct(w.shape, F32)
    return pl.pallas_call(
        body, out_shape=[sds, sds, sds], grid=(rows // T,), in_specs=[blk] * 4, out_specs=[blk] * 3,
        compiler_params=_cparams(("parallel",)), name=name,
    )(w, g, m, v)


ANY = pl.BlockSpec(memory_space=pl.ANY)


def _place():
    x, y, c = lax.axis_index("x"), lax.axis_index("y"), lax.axis_index("c")
    chips = [(1 - x, y), (x, 1 - y), (1 - x, 1 - y)]
    return x, y, c, chips


def _place_own(w, k_arr):
    T = FLAT_T

    def body(k_ref, w_ref, o_ref):
        o_ref[...] = w_ref[...]

    return pl.pallas_call(
        body, out_shape=jax.ShapeDtypeStruct((N_CHIPS, FLAT_ROWS, FLAT_W), w.dtype),
        grid_spec=pltpu.PrefetchScalarGridSpec(
            num_scalar_prefetch=1, grid=(FLAT_ROWS // T,),
            in_specs=[pl.BlockSpec((T, FLAT_W), lambda i, k: (i, 0))],
            out_specs=pl.BlockSpec((None, T, FLAT_W), lambda i, k: (k[0], i, 0))),
        compiler_params=_cparams(("parallel",)), name="gather_place_own",
    )(k_arr, w)


def _allgather_shards(w, buf):
    def body(w_ref, buf_ref, g_ref, send_sems, recv_sems):
        x, y, c, chips = _place()
        myk = 2 * x + y
        sibling = (x, y, 1 - c)
        h0 = pl.multiple_of(c * HALF_ROWS, 16)
        h1 = pl.multiple_of((1 - c) * HALF_ROWS, 16)

        def half(k, start):
            return g_ref.at[k, pl.ds(start, HALF_ROWS), :]

        def rcopy(sem, src, dst, to):
            return pltpu.make_async_remote_copy(src_ref=src, dst_ref=dst, send_sem=send_sems.at[sem],
                                                recv_sem=recv_sems.at[sem], device_id=to, device_id_type=MESH)

        ici = [rcopy(r, w_ref.at[pl.ds(h0, HALF_ROWS), :], half(myk, h0), (*chip, c)) for r, chip in enumerate(chips)]
        for cp in ici:
            cp.start()
        ks = [2 * cx + cy for cx, cy in chips]
        fwd = [rcopy(3 + r, half(ks[r], h0), half(ks[r], h0), sibling) for r in range(3)]
        for r in range(3):
            rcopy(r, half(ks[r], h0), half(ks[r], h0), (*chips[r], c)).wait_recv()
            fwd[r].start()
        for r in range(3):
            rcopy(3 + r, half(ks[r], h1), half(ks[r], h1), sibling).wait_recv()
        for cp in ici + fwd:
            cp.wait_send()

    return pl.pallas_call(
        body, out_shape=jax.ShapeDtypeStruct((N_CHIPS, FLAT_ROWS, FLAT_W), w.dtype), in_specs=[ANY, ANY], out_specs=ANY,
        scratch_shapes=[pltpu.SemaphoreType.DMA((6,)), pltpu.SemaphoreType.DMA((6,))],
        input_output_aliases={1: 0},
        compiler_params=pltpu.CompilerParams(has_side_effects=True), name="allgather_shards",
    )(w, buf)


def _sibling_swap_half(g):
    def body(g_ref, o_ref, send_sem, recv_sem):
        x, y, c, _ = _place()
        theirs = pl.multiple_of((1 - c) * HALF_ROWS, 8)
        cp = pltpu.make_async_remote_copy(src_ref=g_ref.at[:, pl.ds(theirs, HALF_ROWS), :], dst_ref=o_ref,
                                          send_sem=send_sem, recv_sem=recv_sem, device_id=(x, y, 1 - c),
                                          device_id_type=MESH)
        cp.start()
        cp.wait()

    return pl.pallas_call(
        body, out_shape=jax.ShapeDtypeStruct((N_CHIPS, HALF_ROWS, FLAT_W), F32), in_specs=[ANY], out_specs=ANY,
        scratch_shapes=[pltpu.SemaphoreType.DMA, pltpu.SemaphoreType.DMA],
        compiler_params=pltpu.CompilerParams(has_side_effects=True), name="rs_sibling_swap",
    )(g)


def _pair_sum(g, other, c_arr):
    T = FLAT_T

    def body(c_ref, g_ref, o_ref, s_ref):
        s_ref[...] = (g_ref[...] + o_ref[...]).astype(BF16)

    nb = HALF_ROWS // T
    return pl.pallas_call(
        body, out_shape=jax.ShapeDtypeStruct((N_CHIPS, HALF_ROWS, FLAT_W), BF16),
        grid_spec=pltpu.PrefetchScalarGridSpec(
            num_scalar_prefetch=1, grid=(N_CHIPS, nb),
            in_specs=[pl.BlockSpec((None, T, FLAT_W), lambda k, i, c: (k, c[0] * nb + i, 0)),
                      pl.BlockSpec((None, T, FLAT_W), lambda k, i, c: (k, i, 0))],
            out_specs=pl.BlockSpec((None, T, FLAT_W), lambda k, i, c: (k, i, 0))),
        compiler_params=_cparams(("parallel", "parallel")), name="rs_pair_sum",
    )(c_arr, g, other)


def _chip_exchange(s):
    def body(s_ref, o_ref, send_sems, recv_sems):
        x, y, c, chips = _place()
        cps = []
        for r, (cx, cy) in enumerate(chips):
            cps.append(pltpu.make_async_remote_copy(
                src_ref=s_ref.at[2 * cx + cy], dst_ref=o_ref.at[r], send_sem=send_sems.at[r],
                recv_sem=recv_sems.at[r], device_id=(cx, cy, c), device_id_type=MESH))
        for cp in cps:
            cp.start()
        for cp in cps:
            cp.wait()

    return pl.pallas_call(
        body, out_shape=jax.ShapeDtypeStruct((3, HALF_ROWS, FLAT_W), BF16), in_specs=[ANY], out_specs=ANY,
        scratch_shapes=[pltpu.SemaphoreType.DMA((3,)), pltpu.SemaphoreType.DMA((3,))],
        compiler_params=pltpu.CompilerParams(has_side_effects=True), name="rs_chip_exchange",
    )(s)


def _chip_sum(s, r, k_arr):
    T = FLAT_T

    def body(k_ref, s_ref, r_ref, o_ref):
        o_ref[...] = ((s_ref[...].astype(F32) + r_ref[0].astype(F32)) + r_ref[1].astype(F32)) + r_ref[2].astype(F32)

    return pl.pallas_call(
        body, out_shape=jax.ShapeDtypeStruct((HALF_ROWS, FLAT_W), F32),
        grid_spec=pltpu.PrefetchScalarGridSpec(
            num_scalar_prefetch=1, grid=(HALF_ROWS // T,),
            in_specs=[pl.BlockSpec((None, T, FLAT_W), lambda i, k: (k[0], i, 0)),
                      pl.BlockSpec((3, T, FLAT_W), lambda i, k: (0, i, 0))],
            out_specs=pl.BlockSpec((T, FLAT_W), lambda i, k: (i, 0))),
        compiler_params=_cparams(("parallel",)), name="rs_chip_sum",
    )(k_arr, s, r)


def _sibling_send(t):
    def body(t_ref, o_ref, send_sem, recv_sem):
        x, y, c, _ = _place()
        cp = pltpu.make_async_remote_copy(src_ref=t_ref, dst_ref=o_ref, send_sem=send_sem, recv_sem=recv_sem,
                                          device_id=(x, y, 1 - c), device_id_type=MESH)
        cp.start()
        cp.wait()

    return pl.pallas_call(
        body, out_shape=jax.ShapeDtypeStruct((HALF_ROWS, FLAT_W), F32), in_specs=[ANY], out_specs=ANY,
        scratch_shapes=[pltpu.SemaphoreType.DMA, pltpu.SemaphoreType.DMA],
        compiler_params=pltpu.CompilerParams(has_side_effects=True), name="rs_sibling_send",
    )(t)


def _allreduce_small(v):
    def body(v_ref, o_ref, buf, send_sems, recv_sems):
        x, y, c, _ = _place()
        me = 4 * x + 2 * y + c
        buf[me] = v_ref[...]
        cps = []
        for mask in range(1, 8):
            a, b, d = (mask >> 2) & 1, (mask >> 1) & 1, mask & 1
            peer = (x + a - 2 * a * x, y + b - 2 * b * y, c + d - 2 * d * c)
            cps.append(pltpu.make_async_remote_copy(
                src_ref=v_ref, dst_ref=buf.at[me], send_sem=send_sems.at[mask - 1], recv_sem=recv_sems.at[mask - 1],
                device_id=peer, device_id_type=MESH))
        for cp in cps:
            cp.start()
        for cp in cps:
            cp.wait()
        total = buf[0]
        for dev in range(1, 8):
            total = total + buf[dev]
        o_ref[...] = total

    vm = pl.BlockSpec(memory_space=pltpu.VMEM)
    return pl.pallas_call(
        body, out_shape=jax.ShapeDtypeStruct((SMALL_ROWS, 1024), F32), in_specs=[vm], out_specs=vm,
        scratch_shapes=[pltpu.VMEM((8, SMALL_ROWS, 1024), F32), pltpu.SemaphoreType.DMA((7,)),
                        pltpu.SemaphoreType.DMA((7,))],
        compiler_params=pltpu.CompilerParams(has_side_effects=True), name="allreduce_small",
    )(v)


def _heads(t, n):
    return t.reshape(S, n, HEAD_DIM).transpose(1, 0, 2)


def _unheads(t):
    return t.transpose(1, 0, 2).reshape(S, t.shape[0] * HEAD_DIM)


def _to_residue(t, d):
    c = t.shape[-1]
    return t.reshape(B_HPG, S // d, d, c).transpose(0, 2, 1, 3).reshape(B_HPG, S, c)


def _from_residue(t, d):
    c = t.shape[-1]
    return t.reshape(B_HPG, d, S // d, c).transpose(0, 2, 1, 3).reshape(B_HPG, S, c)


def _dil_pack(t):
    return jnp.concatenate([_to_residue(t[g], d) for g, (_, d) in enumerate(B_GROUPS)], axis=0)


def _dil_unpack(t):
    return jnp.stack([_from_residue(t[g * B_HPG:(g + 1) * B_HPG], d) for g, (_, d) in enumerate(B_GROUPS)], axis=0)


def _col_to_row(t):
    return t.reshape(t.shape[0], 1, S)


def _ffn_fwd(x, g, w_up, cw, cb, w_down, tag):
    h = _rms_fwd(x, g, f"{tag}_norm")
    u = _mm(h, w_up, mode="nn", tm=1024, tn=1408, tk=1024, o_split=2, name=f"{tag}_up")
    act = _ffn_act_fwd(u, cw, cb, f"{tag}_act")
    x_out = _mm(act, w_down, mode="nn", tm=1024, tn=512, tk=FF, res=x, name=f"{tag}_down")
    return x_out, (h, u, act)


def _ffn_bwd(x, g, w_up, cw, cb, w_down, saved, dx, dxb, tag):
    h, u, act = saved
    d_w_down = _mm(act, dxb, mode="tn", tm=1408, tn=512, tk=1024, name=f"{tag}_dwdown")
    dact = _mm(dxb, w_down, mode="nt", tm=1024, tn=1408, tk=1024, name=f"{tag}_dact")
    duc, dwb = _ffn_act_bwd(u, dact, cw, cb, f"{tag}_dgate")
    du = _ffn_conv_bwd(duc, cw, f"{tag}_dconv")
    d_w_up = _mm(h, du, mode="tn", tm=1024, tn=1408, tk=1024, b_split=2, name=f"{tag}_dwup")
    dh = _mm(du, w_up, mode="nt", tm=1024, tn=512, tk=1408, a_split=2, name=f"{tag}_dh")
    dx_new, dxb_new, (dg,) = _rms_bwd(x, dx, [(g, dh)], f"{tag}_dnorm")
    d_cw = dwb[:, 0:3, :].transpose(1, 0, 2).reshape(3, 2 * FF)
    d_cb = dwb[:, 3, :].reshape(2 * FF)
    return dx_new, dxb_new, dict(w_up=d_w_up, w_down=d_w_down, conv_w=d_cw, conv_b=d_cb, norm_g=dg.reshape(D))


def _local_step(x, target, p):
    g = {}
    h1 = _rms_fwd(x, p["mix_norm_g"][0], "a_norm")
    w_qkv = p["a_w_in"][:, :QKV_W]
    w_f = jnp.pad(p["a_w_in"][:, QKV_W:], ((0, 0), (0, LANES - A_HEADS)))
    b_f = jnp.pad(p["a_b_f"].reshape(1, A_HEADS), ((0, 0), (0, LANES - A_HEADS)))
    qkv = _mm(h1, w_qkv, mode="nn", tm=1024, tn=512, tk=1024, out_dtype=BF16, name="a_qkv")
    pf = _mm(h1, w_f, mode="nn", tm=1024, tn=LANES, tk=1024, name="a_gate")
    cum = _fgate_fwd(pf, b_f, "a_gate_scan")
    qkv_h = _heads(qkv, 3 * A_HEADS)
    qa, ka, va = qkv_h[:A_HEADS], qkv_h[A_HEADS:2 * A_HEADS], qkv_h[2 * A_HEADS:]
    c_hs = cum[:, :A_HEADS].T
    c_col, c_row = c_hs.reshape(A_HEADS, S, 1), c_hs.reshape(A_HEADS, 1, S)
    oa, lse_a = _fox_fwd(qa, ka, va, c_col, c_row, "a_attn")
    oa2 = _unheads(oa)
    x1 = _mm(oa2, p["a_w_out"], mode="nn", tm=1024, tn=512, tk=1024, res=x, name="a_out")
    x2, ffn0 = _ffn_fwd(x1, p["ffn_norm_g"][0], p["ffn_w_up"][0], p["conv_w"][0], p["conv_b"][0], p["ffn_w_down"][0], "f0")
    hk = _rms_fwd(x2, p["kv_norm_g"], "kv_norm")
    kvb = _mm(hk, p["w_kv"], mode="nn", tm=1024, tn=512, tk=1024, out_dtype=BF16, name="kv_proj")
    h3 = _rms_fwd(x2, p["mix_norm_g"][1], "b_norm")
    qb = _mm(h3, p["b_w_q"], mode="nn", tm=1024, tn=512, tk=1024, out_dtype=BF16, name="b_q")
    qd = _dil_pack(_heads(qb, N_HG).reshape(3, B_HPG, S, HEAD_DIM))
    kv_h = _heads(kvb, 2 * N_HG)
    kd = _dil_pack(kv_h[:N_HG].reshape(3, B_HPG, S, HEAD_DIM))
    vd = _dil_pack(kv_h[N_HG:].reshape(3, B_HPG, S, HEAD_DIM))
    od, lsed = _dil_fwd(qd, kd, vd, "b_attn")
    ob, obb, lse_b = _dil_merge(_dil_unpack(od), _dil_unpack(lsed), "b_merge")
    ob2 = _unheads(obb)
    x3 = _mm(ob2, p["b_w_out"], mode="nn", tm=1024, tn=512, tk=B_OUT_W, res=x2, name="b_out")
    x4, ffn1 = _ffn_fwd(x3, p["ffn_norm_g"][1], p["ffn_w_up"][1], p["conv_w"][1], p["conv_b"][1], p["ffn_w_down"][1], "f1")
    loss, dx, dxb, dg_final = _loss_head(x4, p["final_norm_g"], target, "loss_head")
    g["final_norm_g"] = dg_final.reshape(D)

    dx, dxb, gf1 = _ffn_bwd(x3, p["ffn_norm_g"][1], p["ffn_w_up"][1], p["conv_w"][1], p["conv_b"][1], p["ffn_w_down"][1],
                            ffn1, dx, dxb, "f1")
    g["b_w_out"] = _mm(ob2, dxb, mode="tn", tm=B_OUT_W, tn=512, tk=1024, name="b_dwout")
    dob = _heads(_mm(dxb, p["b_w_out"], mode="nt", tm=1024, tn=B_OUT_W, tk=1024, name="b_do"), B_HPG)
    delta_b = _rowdot(dob, ob, "b_delta")
    rep = lambda t: jnp.broadcast_to(t[None], (3,) + t.shape)
    dod = _dil_pack(rep(dob.astype(BF16)))
    l_d = _dil_pack(rep(lse_b))
    dl_d = _dil_pack(rep(delta_b))
    dqd = _dil_bwd_dq(qd, kd, vd, dod, l_d, dl_d, "b_dq")
    dkd, dvd = _dil_bwd_dkv(qd, kd, vd, dod, _col_to_row(l_d), _col_to_row(dl_d), "b_dkv")
    dqb = _unheads(_dil_unpack(dqd).reshape(N_HG, S, HEAD_DIM))
    dkvb = _unheads(jnp.concatenate([_dil_unpack(dkd).reshape(N_HG, S, HEAD_DIM),
                                     _dil_unpack(dvd).reshape(N_HG, S, HEAD_DIM)], axis=0))
    g["b_w_q"] = _mm(h3, dqb, mode="tn", tm=1024, tn=512, tk=1024, name="b_dwq")
    dh3 = _mm(dqb, p["b_w_q"], mode="nt", tm=1024, tn=512, tk=B_Q_W, name="b_dh")
    g["w_kv"] = _mm(hk, dkvb, mode="tn", tm=1024, tn=512, tk=1024, name="kv_dw")
    dhk = _mm(dkvb, p["w_kv"], mode="nt", tm=1024, tn=512, tk=1536, name="kv_dh")
    dx, dxb, (dg_mix1, dg_kv) = _rms_bwd(x2, dx, [(p["mix_norm_g"][1], dh3), (p["kv_norm_g"], dhk)], "b_dnorm")
    g["kv_norm_g"] = dg_kv.reshape(D)
    dx, dxb, gf0 = _ffn_bwd(x1, p["ffn_norm_g"][0], p["ffn_w_up"][0], p["conv_w"][0], p["conv_b"][0], p["ffn_w_down"][0],
                            ffn0, dx, dxb, "f0")
    g["a_w_out"] = _mm(oa2, dxb, mode="tn", tm=1024, tn=512, tk=1024, name="a_dwout")
    doa = _heads(_mm(dxb, p["a_w_out"], mode="nt", tm=1024, tn=512, tk=1024, name="a_do"), A_HEADS)
    delta_a = _rowdot(doa, oa, "a_delta")
    dqa, dka, dva, dck, dcq = _fox_bwd(qa, ka, va, doa.astype(BF16), _col_to_row(lse_a), _col_to_row(delta_a), c_row,
                                       c_col, "a_dattn")
    dqkv = _unheads(jnp.concatenate([dqa.astype(BF16), dka, dva], axis=0))
    pad_heads = lambda t: jnp.pad(t.reshape(A_HEADS, S).T, ((0, 0), (0, LANES - A_HEADS)))
    dpf, db_f = _fgate_bwd(pf, b_f, pad_heads(dck), pad_heads(dcq), "a_dgate_scan")
    g["a_b_f"] = db_f[:, :A_HEADS]
    d_w_qkv = _mm(h1, dqkv, mode="tn", tm=1024, tn=512, tk=1024, name="a_dwqkv")
    d_w_f = _mm(h1, dpf, mode="tn", tm=1024, tn=LANES, tk=1024, name="a_dwgate")
    g["a_w_in"] = jnp.concatenate([d_w_qkv, d_w_f[:, :A_HEADS]], axis=1)
    dh1 = _mm(dqkv, w_qkv, mode="nt", tm=1024, tn=512, tk=1536, name="a_dh")
    dh1 = _mm(dpf, w_f, mode="nt", tm=1024, tn=512, tk=LANES, res=dh1, name="a_dh_gate")
    dx, _, (dg_mix0,) = _rms_bwd(x, dx, [(p["mix_norm_g"][0], dh1)], "a_dnorm")

    g["mix_norm_g"] = jnp.stack([dg_mix0.reshape(D), dg_mix1.reshape(D)])
    g["ffn_norm_g"] = jnp.stack([gf0["norm_g"], gf1["norm_g"]])
    g["ffn_w_up"] = jnp.stack([gf0["w_up"], gf1["w_up"]])
    g["ffn_w_down"] = jnp.stack([gf0["w_down"], gf1["w_down"]])
    g["ffn_conv_w"] = jnp.stack([gf0["conv_w"], gf1["conv_w"]])
    g["ffn_conv_b"] = jnp.stack([gf0["conv_b"], gf1["conv_b"]])
    return loss[0, 0], dx, g


_SHARD_SHAPES = {"a_w_in": (1, 1024, 772), "a_w_out": (1, 256, 1024), "b_w_q": (1, 1024, 384), "b_w_out": (1, 512, 256),
                 "w_kv": (1024, 768), "ffn_w_up": (2, 1024, 1408), "ffn_w_down": (2, 704, 1024), "ffn_conv_w": (2, 3, 1408)}
_SHARD_AXIS = {"a_w_in": 2, "a_w_out": 1, "b_w_q": 2, "b_w_out": 2, "w_kv": 1, "ffn_w_up": 2, "ffn_w_down": 1,
               "ffn_conv_w": 2}
_SMALL = (("kv_norm_g", (1024,)), ("mix_norm_g", (2, 1024)), ("ffn_norm_g", (2, 1024)), ("final_norm_g", (1024,)),
          ("a_b_f", (1, 16)), ("ffn_conv_b", (2, 5632)))


def _slabs(t, L, R, C, rpad):
    nc = -(-C // FLAT_W)
    t = jnp.pad(t.reshape(L, R, C), ((0, 0), (0, rpad - R), (0, nc * FLAT_W - C)))
    return t.reshape(L, rpad, nc, FLAT_W).transpose(0, 2, 1, 3).reshape(L * nc * rpad, FLAT_W)


def _unslabs(rows, L, R, C, rpad):
    nc = -(-C // FLAT_W)
    return rows.reshape(L, nc, rpad, FLAT_W).transpose(0, 2, 1, 3).reshape(L, rpad, nc * FLAT_W)[:, :R, :C]


def _pack_flat(shards, dtype):
    parts = []
    for name, L, R, C, rpad in _SEGS:
        t = shards[name].astype(F32)
        if name == "ffn_conv_w" and dtype == BF16:
            parts += [_slabs(term, L, R, C, rpad) for term in _split3(t)]
        elif name == "ffn_conv_w":
            part = _slabs(t, L, R, C, rpad)
            parts += [part, jnp.zeros(((CONV_TERMS - 1) * part.shape[0], FLAT_W), dtype)]
        else:
            parts.append(_slabs(t, L, R, C, rpad).astype(dtype))
    used = sum(_seg_rows(*s) for s in _SEGS)
    parts.append(jnp.zeros((FLAT_ROWS - used, FLAT_W), dtype))
    return jnp.concatenate(parts, axis=0)


def _unpack_flat(flat):
    out = {}
    r0 = 0
    for seg in _SEGS:
        name, L, R, C, rpad = seg
        n = _seg_rows(*seg)
        rows = flat[r0:r0 + n]
        r0 += n
        if name == "ffn_conv_w":
            n1 = n // CONV_TERMS
            t = _unslabs(rows[:n1], L, R, C, rpad)
            if flat.dtype == BF16:
                mid, lo = (_unslabs(rows[i * n1:(i + 1) * n1], L, R, C, rpad).astype(F32) for i in (1, 2))
                t = (t.astype(F32) + mid) + lo
        else:
            t = _unslabs(rows, L, R, C, rpad)
        out[name] = t.reshape(_SHARD_SHAPES[name])
    return out


def _pack_small(vals, loss=None):
    parts = [vals[name].astype(F32).reshape(-1) for name, _ in _SMALL]
    if loss is not None:
        parts.append(loss.reshape(1))
    flat = jnp.concatenate(parts)
    return jnp.pad(flat, (0, SMALL_ROWS * 1024 - flat.shape[0])).reshape(SMALL_ROWS, 1024)


def _unpack_small(flat):
    flat = flat.reshape(-1)
    out = {}
    o = 0
    for name, shape in _SMALL:
        n = int(np.prod(shape))
        out[name] = flat[o:o + n].reshape(shape)
        o += n
    return out, flat[o]


def _full_weights(gathered):
    per_chip = [_unpack_flat(gathered[k]) for k in range(N_CHIPS)]
    full = {name: jnp.concatenate([pc[name] for pc in per_chip], axis=_SHARD_AXIS[name]) for name in _SHARD_SHAPES}
    cw = full["ffn_conv_w"].reshape(2, 3, 2, FF).transpose(0, 2, 1, 3)
    return dict(a_w_in=full["a_w_in"][0], a_w_out=full["a_w_out"][0], b_w_q=full["b_w_q"][0], b_w_out=full["b_w_out"][0],
                w_kv=full["w_kv"], ffn_w_up=full["ffn_w_up"], ffn_w_down=full["ffn_w_down"], conv_w=cw)


def _shard_grads(g):
    full = {"a_w_in": g["a_w_in"][None], "a_w_out": g["a_w_out"][None], "b_w_q": g["b_w_q"][None],
            "b_w_out": g["b_w_out"][None], "w_kv": g["w_kv"], "ffn_w_up": g["ffn_w_up"], "ffn_w_down": g["ffn_w_down"],
            "ffn_conv_w": g["ffn_conv_w"]}
    flats = []
    for k in range(N_CHIPS):
        shards = {}
        for name, shape in _SHARD_SHAPES.items():
            ax = _SHARD_AXIS[name]
            n = shape[ax]
            shards[name] = lax.slice_in_dim(full[name], k * n, (k + 1) * n, axis=ax)
        flats.append(_pack_flat(shards, F32))
    return jnp.stack(flats, axis=0)


_WEIGHTS = ["a_w_in", "a_b_f", "a_w_out", "b_w_q", "b_w_out", "kv_norm_g", "w_kv", "mix_norm_g", "ffn_norm_g", "ffn_w_up",
            "ffn_conv_w", "ffn_conv_b", "ffn_w_down", "final_norm_g"]


def kernel(x, a_w_in, a_b_f, a_w_out, b_w_q, b_w_out, kv_norm_g, w_kv, mix_norm_g, ffn_norm_g, ffn_w_up, ffn_conv_w, ffn_conv_b, ffn_w_down, final_norm_g, loss_target, m_a_w_in, m_a_b_f, m_a_w_out, m_b_w_q, m_b_w_out, m_kv_norm_g, m_w_kv, m_mix_norm_g, m_ffn_norm_g, m_ffn_w_up, m_ffn_conv_w, m_ffn_conv_b, m_ffn_w_down, m_final_norm_g, v_a_w_in, v_a_b_f, v_a_w_out, v_b_w_q, v_b_w_out, v_kv_norm_g, v_w_kv, v_mix_norm_g, v_ffn_norm_g, v_ffn_w_up, v_ffn_conv_w, v_ffn_conv_b, v_ffn_w_down, v_final_norm_g):
    w = dict(a_w_in=a_w_in, a_b_f=a_b_f, a_w_out=a_w_out, b_w_q=b_w_q, b_w_out=b_w_out, kv_norm_g=kv_norm_g, w_kv=w_kv,
             mix_norm_g=mix_norm_g, ffn_norm_g=ffn_norm_g, ffn_w_up=ffn_w_up, ffn_conv_w=ffn_conv_w, ffn_conv_b=ffn_conv_b,
             ffn_w_down=ffn_w_down, final_norm_g=final_norm_g)
    m = dict(a_w_in=m_a_w_in, a_b_f=m_a_b_f, a_w_out=m_a_w_out, b_w_q=m_b_w_q, b_w_out=m_b_w_out, kv_norm_g=m_kv_norm_g,
             w_kv=m_w_kv, mix_norm_g=m_mix_norm_g, ffn_norm_g=m_ffn_norm_g, ffn_w_up=m_ffn_w_up, ffn_conv_w=m_ffn_conv_w,
             ffn_conv_b=m_ffn_conv_b, ffn_w_down=m_ffn_w_down, final_norm_g=m_final_norm_g)
    v = dict(a_w_in=v_a_w_in, a_b_f=v_a_b_f, a_w_out=v_a_w_out, b_w_q=v_b_w_q, b_w_out=v_b_w_out, kv_norm_g=v_kv_norm_g,
             w_kv=v_w_kv, mix_norm_g=v_mix_norm_g, ffn_norm_g=v_ffn_norm_g, ffn_w_up=v_ffn_w_up, ffn_conv_w=v_ffn_conv_w,
             ffn_conv_b=v_ffn_conv_b, ffn_w_down=v_ffn_w_down, final_norm_g=v_final_norm_g)

    c_arr = lax.axis_index("c").astype(jnp.int32).reshape(1)
    k_arr = (2 * lax.axis_index("x") + lax.axis_index("y")).astype(jnp.int32).reshape(1)
    w16 = _pack_flat(w, BF16)
    gathered = _allgather_shards(w16, _place_own(w16, k_arr))
    p = _full_weights(gathered)
    cb = ffn_conv_b.reshape(2, 2, 1, FF)
    p.update(a_b_f=a_b_f, kv_norm_g=kv_norm_g, mix_norm_g=mix_norm_g, ffn_norm_g=ffn_norm_g, final_norm_g=final_norm_g,
             conv_b=cb)

    loss_part, grad_x, g = _local_step(x[0], loss_target[0], p)

    gflat = _shard_grads(g)
    pair = _pair_sum(gflat, _sibling_swap_half(gflat), c_arr)
    g_mine = _chip_sum(pair, _chip_exchange(pair), k_arr)
    g_other = _sibling_send(g_mine)
    small, loss = _unpack_small(_allreduce_small(_pack_small(g, loss_part)))

    gsh, dw, mn, vn = _adamw_halves(_pack_flat(w, F32), g_mine, g_other, _pack_flat(m, F32), _pack_flat(v, F32), c_arr,
                                    "adamw_shards")
    dws, mns, vns = _adamw(_pack_small(w), _pack_small(small), _pack_small(m), _pack_small(v), "adamw_small")
    big = [_unpack_flat(t) for t in (gsh, dw, mn, vn)]
    sml = [small] + [_unpack_small(t)[0] for t in (dws, mns, vns)]
    outs = [loss, grad_x[None]]
    for b, s in zip(big, sml):
        outs += [b[n] if n in b else s[n] for n in _WEIGHTS]
    return tuple(outs)
```

```python
import functools
import math

import numpy as np
import jax
import jax.numpy as jnp
from jax import lax
from jax.experimental import pallas as pl
from jax.experimental.pallas import tpu as pltpu

F32 = jnp.float32
BF16 = jnp.bfloat16
MESH = pl.DeviceIdType.MESH

S = 4096
D = 1024
A_HEADS = 16
HEAD_DIM = 64
QKV_W = 3 * A_HEADS * HEAD_DIM
B_GROUPS = ((128, 1), (512, 4), (2048, 16))
B_HPG = 8
B_Q_W = 3 * B_HPG * HEAD_DIM
B_OUT_W = B_HPG * HEAD_DIM
B_KV_W = 2 * B_Q_W
B_WIN = 128
FF = 2816
RMS_EPS = 1e-6
SCALE = HEAD_DIM ** -0.5
N_CHIPS = 4

ADAM_LR, ADAM_B1, ADAM_B2, ADAM_EPS, ADAM_WD, ADAM_STEP = 0.001, 0.9, 0.999, 1e-08, 0.01, 10

V7X_VMEM_LIMIT = 48 * 1024 * 1024
LANES = 128
NEG_INF = float("-inf")

FLAT_W = LANES
_SEGS = (("ffn_w_down", 2, 704, 1024, 704), ("a_w_in", 1, 1024, 772, 1024), ("a_w_out", 1, 256, 1024, 256),
         ("b_w_q", 1, 1024, 384, 1024), ("b_w_out", 1, 512, 256, 512), ("w_kv", 1, 1024, 768, 1024),
         ("ffn_w_up", 2, 1024, 1408, 1024), ("ffn_conv_w", 1, 6, 1408, 16))
CONV_TERMS = 3


def _seg_rows(name, L, R, C, rpad):
    return (CONV_TERMS if name == "ffn_conv_w" else 1) * L * (-(-C // FLAT_W)) * rpad


FLAT_T = 2048
FLAT_ROWS = 57344
HALF_ROWS = FLAT_ROWS // 2
assert sum(_seg_rows(*s) for s in _SEGS) <= FLAT_ROWS and HALF_ROWS % FLAT_T == 0
SMALL_ROWS = 24


def _cparams(sem=None, **kw):
    return pltpu.CompilerParams(dimension_semantics=sem, vmem_limit_bytes=V7X_VMEM_LIMIT, **kw)


_DN = {"nn": (((1,), (0,)), ((), ())), "nt": (((1,), (1,)), ((), ())), "tn": (((0,), (0,)), ((), ()))}


def _mm(a, b, *, mode, tm, tn, tk, name, out_dtype=F32, res=None, a_split=0, b_split=0, o_split=0):
    if mode == "tn":
        K = a.shape[0]
        M = a.shape[1]
    else:
        M = a.shape[-2]
        K = a.shape[-1] * (2 if a_split else 1)
    if mode == "nt":
        N = b.shape[0]
    else:
        N = b.shape[-1] * (2 if b_split else 1)
    assert M % tm == 0 and N % tn == 0 and K % tk == 0, (name, M, N, K, tm, tn, tk)
    nk = K // tk

    if mode == "tn":
        a_spec = pl.BlockSpec((tk, tm), lambda i, j, k: (k, i))
    elif a_split:
        a_spec = pl.BlockSpec((None, tm, tk), lambda i, j, k: (k // a_split, i, k % a_split))
    else:
        a_spec = pl.BlockSpec((tm, tk), lambda i, j, k: (i, k))
    if mode == "nt":
        b_spec = pl.BlockSpec((tn, tk), lambda i, j, k: (j, k))
    elif b_split:
        b_spec = pl.BlockSpec((None, tk, tn), lambda i, j, k: (j // b_split, k, j % b_split))
    else:
        b_spec = pl.BlockSpec((tk, tn), lambda i, j, k: (k, j))
    if o_split:
        o_spec = pl.BlockSpec((None, tm, tn), lambda i, j, k: (j // o_split, i, j % o_split))
        out_shape = jax.ShapeDtypeStruct((2, M, N // 2), out_dtype)
    else:
        o_spec = pl.BlockSpec((tm, tn), lambda i, j, k: (i, j))
        out_shape = jax.ShapeDtypeStruct((M, N), out_dtype)
    in_specs = [a_spec, b_spec]
    args = [a, b]
    if res is not None:
        in_specs.append(pl.BlockSpec((tm, tn), lambda i, j, k: (i, j)))
        args.append(res)

    def body(*refs):
        if res is not None:
            a_ref, b_ref, r_ref, o_ref = refs[:4]
        else:
            a_ref, b_ref, o_ref = refs[:3]
            r_ref = None
        p = lax.dot_general(a_ref[...].astype(BF16), b_ref[...].astype(BF16), _DN[mode], preferred_element_type=F32)

        def finish(r):
            if r_ref is not None:
                r = r + r_ref[...]
            o_ref[...] = r.astype(out_dtype)

        if nk == 1:
            finish(p)
        else:
            acc = refs[-1]
            k = pl.program_id(2)

            @pl.when(k == 0)
            def _():
                acc[...] = p

            @pl.when(k > 0)
            def _():
                acc[...] += p

            @pl.when(k == nk - 1)
            def _():
                finish(acc[...])

    return pl.pallas_call(
        body, out_shape=out_shape, grid=(M // tm, N // tn, nk), in_specs=in_specs, out_specs=o_spec,
        scratch_shapes=[pltpu.VMEM((tm, tn), F32)] if nk > 1 else [],
        compiler_params=_cparams(("parallel", "parallel", "arbitrary")), name=name,
    )(*args)


NORM_ROWS = 256


def _rms_fwd(x, g, name):
    def body(x_ref, g_ref, o_ref):
        xv = x_ref[...]
        r = lax.rsqrt(jnp.mean(xv * xv, axis=-1, keepdims=True) + RMS_EPS)
        o_ref[...] = (xv * r * g_ref[...]).astype(BF16)

    row = pl.BlockSpec((NORM_ROWS, D), lambda i: (i, 0))
    return pl.pallas_call(
        body, out_shape=jax.ShapeDtypeStruct((S, D), BF16), grid=(S // NORM_ROWS,),
        in_specs=[row, pl.BlockSpec((1, D), lambda i: (0, 0))], out_specs=row,
        compiler_params=_cparams(("parallel",)), name=name,
    )(x, g.reshape(1, D))


def _rms_bwd(x, dres, pairs, name):
    n = len(pairs)

    def body(*refs):
        x_ref, dres_ref = refs[0], refs[1]
        g_refs = refs[2:2 + 2 * n:2]
        dh_refs = refs[3:3 + 2 * n:2]
        dx_ref, dxb_ref = refs[2 + 2 * n], refs[3 + 2 * n]
        dg_refs = refs[4 + 2 * n:]
        i = pl.program_id(0)
        xv = x_ref[...]
        r = lax.rsqrt(jnp.mean(xv * xv, axis=-1, keepdims=True) + RMS_EPS)
        y = xv * r
        dx = dres_ref[...]
        for g_ref, dh_ref, dg_ref in zip(g_refs, dh_refs, dg_refs):
            dh = dh_ref[...]
            dy = dh * g_ref[...]
            dx = dx + r * (dy - y * jnp.mean(dy * y, axis=-1, keepdims=True))
            part = jnp.sum(dh * y, axis=0, keepdims=True)

            @pl.when(i == 0)
            def _():
                dg_ref[...] = part

            @pl.when(i > 0)
            def _():
                dg_ref[...] += part

        dx_ref[...] = dx
        dxb_ref[...] = dx.astype(BF16)

    row = pl.BlockSpec((NORM_ROWS, D), lambda i: (i, 0))
    vec = pl.BlockSpec((1, D), lambda i: (0, 0))
    in_specs = [row, row]
    args = [x, dres]
    for g, dh in pairs:
        in_specs += [vec, row]
        args += [g.reshape(1, D), dh]
    outs = pl.pallas_call(
        body,
        out_shape=[jax.ShapeDtypeStruct((S, D), F32), jax.ShapeDtypeStruct((S, D), BF16)]
        + [jax.ShapeDtypeStruct((1, D), F32)] * n,
        grid=(S // NORM_ROWS,), in_specs=in_specs, out_specs=[row, row] + [vec] * n,
        compiler_params=_cparams(("arbitrary",)), name=name,
    )(*args)
    return outs[0], outs[1], list(outs[2:])


def _loss_head(x, g, target, name):
    def body(x_ref, g_ref, t_ref, loss_ref, dx_ref, dxb_ref, dg_ref):
        i = pl.program_id(0)
        xv = x_ref[...]
        gv = g_ref[...]
        r = lax.rsqrt(jnp.mean(xv * xv, axis=-1, keepdims=True) + RMS_EPS)
        y = xv * r
        err = y * gv - t_ref[...]
        lpart = jnp.broadcast_to(jnp.sum(err * err, keepdims=True) * (0.5 / D), (1, LANES))
        dh = err * (1.0 / D)
        dy = dh * gv
        dx = r * (dy - y * jnp.mean(dy * y, axis=-1, keepdims=True))
        part = jnp.sum(dh * y, axis=0, keepdims=True)

        @pl.when(i == 0)
        def _():
            dg_ref[...] = part
            loss_ref[...] = lpart

        @pl.when(i > 0)
        def _():
            dg_ref[...] += part
            loss_ref[...] += lpart

        dx_ref[...] = dx
        dxb_ref[...] = dx.astype(BF16)

    row = pl.BlockSpec((NORM_ROWS, D), lambda i: (i, 0))
    vec = pl.BlockSpec((1, D), lambda i: (0, 0))
    return pl.pallas_call(
        body,
        out_shape=[jax.ShapeDtypeStruct((1, LANES), F32), jax.ShapeDtypeStruct((S, D), F32),
                   jax.ShapeDtypeStruct((S, D), BF16), jax.ShapeDtypeStruct((1, D), F32)],
        grid=(S // NORM_ROWS,), in_specs=[row, vec, row],
        out_specs=[pl.BlockSpec((1, LANES), lambda i: (0, 0)), row, row, vec],
        compiler_params=_cparams(("arbitrary",)), name=name,
    )(x, g.reshape(1, D), target)


SCAN_ROWS = 256


def _split3(v):
    hi = v.astype(BF16)
    r1 = v - hi.astype(F32)
    mid = r1.astype(BF16)
    lo = (r1 - mid.astype(F32)).astype(BF16)
    return hi, mid, lo


def _tri_dot(tri, v):
    hi, mid, lo = _split3(v)
    dn = _DN["nn"]
    return (lax.dot_general(tri, hi, dn, preferred_element_type=F32)
            + lax.dot_general(tri, mid, dn, preferred_element_type=F32)
            + lax.dot_general(tri, lo, dn, preferred_element_type=F32))


def _log_sigmoid(z):
    return jnp.minimum(z, 0.0) - jnp.log(1.0 + jnp.exp(-jnp.abs(z)))


def _fgate_fwd(pf, bias, name):
    tri = jnp.tril(jnp.ones((SCAN_ROWS, SCAN_ROWS), F32)).astype(BF16)

    def body(pf_ref, b_ref, tri_ref, c_ref):
        carry = jnp.zeros((1, LANES), F32)
        for blk in range(S // SCAN_ROWS):
            rows = pl.ds(blk * SCAN_ROWS, SCAN_ROWS)
            lf = _log_sigmoid(pf_ref[rows, :] + b_ref[...])
            c_ref[rows, :] = _tri_dot(tri_ref[...], lf) + carry
            carry = c_ref[pl.ds(blk * SCAN_ROWS + SCAN_ROWS - 1, 1), :]

    return pl.pallas_call(
        body, out_shape=jax.ShapeDtypeStruct((S, LANES), F32),
        compiler_params=_cparams(), name=name,
    )(pf, bias, tri)


def _fgate_bwd(pf, bias, dc_key, dc_query, name):
    triu = jnp.triu(jnp.ones((SCAN_ROWS, SCAN_ROWS), F32)).astype(BF16)

    def body(pf_ref, b_ref, dck_ref, dcq_ref, tri_ref, dpf_ref, db_ref, dlf_ref):
        carry = jnp.zeros((1, LANES), F32)
        db = jnp.zeros((1, LANES), F32)
        lane = lax.broadcasted_iota(jnp.int32, (SCAN_ROWS, LANES), 1)
        for blk in reversed(range(S // SCAN_ROWS)):
            rows = pl.ds(blk * SCAN_ROWS, SCAN_ROWS)
            dc = dck_ref[rows, :] + dcq_ref[rows, :]
            dlf_ref[rows, :] = _tri_dot(tri_ref[...], dc) + carry
            carry = dlf_ref[pl.ds(blk * SCAN_ROWS, 1), :]
            z = pf_ref[rows, :] + b_ref[...]
            e = jnp.exp(-jnp.abs(z))
            sig_neg = jnp.where(z >= 0.0, e, 1.0) / (1.0 + e)
            dz = jnp.where(lane < A_HEADS, dlf_ref[rows, :] * sig_neg, 0.0)
            dpf_ref[rows, :] = dz.astype(BF16)
            db = db + jnp.sum(dz, axis=0, keepdims=True)
        db_ref[...] = db

    return pl.pallas_call(
        body, out_shape=[jax.ShapeDtypeStruct((S, LANES), BF16), jax.ShapeDtypeStruct((1, LANES), F32)],
        scratch_shapes=[pltpu.VMEM((S, LANES), F32)],
        compiler_params=_cparams(), name=name,
    )(pf, bias, dc_key, dc_query, triu)


FOX_T = 512


def _fox_fwd(q, k, v, ccol, crow, name):
    H = q.shape[0]
    T = FOX_T
    nq = S // T

    def body(q_ref, k_ref, v_ref, cc_ref, cr_ref, o_ref, lse_ref, m_sc, l_sc, acc_sc):
        i = pl.program_id(1)
        j = pl.program_id(2)

        @pl.when(j == 0)
        def _():
            m_sc[...] = jnp.full((T, 1), NEG_INF, F32)
            l_sc[...] = jnp.zeros((T, 1), F32)
            acc_sc[...] = jnp.zeros((T, HEAD_DIM), F32)

        def step(diagonal):
            s = lax.dot_general(q_ref[...], k_ref[...], _DN["nt"], preferred_element_type=F32) * SCALE
            s = s + (cc_ref[...] - cr_ref[...])
            if diagonal:
                row = lax.broadcasted_iota(jnp.int32, (T, T), 0)
                col = lax.broadcasted_iota(jnp.int32, (T, T), 1)
                s = jnp.where(row >= col, s, NEG_INF)
            m_prev = m_sc[...]
            m_new = jnp.maximum(m_prev, jnp.max(s, axis=1, keepdims=True))
            alpha = jnp.exp(m_prev - m_new)
            p = jnp.exp(s - m_new)
            l_sc[...] = alpha * l_sc[...] + jnp.sum(p, axis=1, keepdims=True)
            acc_sc[...] = alpha * acc_sc[...] + lax.dot_general(p.astype(BF16), v_ref[...], _DN["nn"],
                                                                preferred_element_type=F32)
            m_sc[...] = m_new

        @pl.when(j < i)
        def _():
            step(False)

        @pl.when(j == i)
        def _():
            step(True)
            o_ref[...] = (acc_sc[...] / l_sc[...]).astype(BF16)
            lse_ref[...] = m_sc[...] + jnp.log(l_sc[...])

    qs = pl.BlockSpec((None, T, HEAD_DIM), lambda h, i, j: (h, i, 0))
    ks = pl.BlockSpec((None, T, HEAD_DIM), lambda h, i, j: (h, jnp.minimum(i, j), 0))
    col = pl.BlockSpec((None, T, 1), lambda h, i, j: (h, i, 0))
    rowk = pl.BlockSpec((None, 1, T), lambda h, i, j: (h, 0, jnp.minimum(i, j)))
    return pl.pallas_call(
        body, out_shape=[jax.ShapeDtypeStruct((H, S, HEAD_DIM), BF16), jax.ShapeDtypeStruct((H, S, 1), F32)],
        grid=(H, nq, nq), in_specs=[qs, ks, ks, col, rowk], out_specs=[qs, col],
        scratch_shapes=[pltpu.VMEM((T, 1), F32), pltpu.VMEM((T, 1), F32), pltpu.VMEM((T, HEAD_DIM), F32)],
        compiler_params=_cparams(("parallel", "parallel", "arbitrary")), name=name,
    )(q, k, v, ccol, crow)


def _fox_bwd(q, k, v, do, lse_row, delta_row, cq_row, ck_col, name):
    H = q.shape[0]
    T = FOX_T
    nq = S // T

    def body(q_ref, k_ref, v_ref, do_ref, lse_ref, dl_ref, cq_ref, ck_ref, dq_ref, dk_ref, dv_ref, dc_ref, dcq_ref,
             dk_sc, dv_sc, dc_sc):
        j = pl.program_id(1)
        i = pl.program_id(2)

        @pl.when(jnp.logical_and(j == 0, i == 0))
        def _():
            dq_ref[...] = jnp.zeros((S, HEAD_DIM), F32)
            dcq_ref[...] = jnp.zeros((nq, 1, T), F32)

        @pl.when(i == j)
        def _():
            dk_sc[...] = jnp.zeros((T, HEAD_DIM), F32)
            dv_sc[...] = jnp.zeros((T, HEAD_DIM), F32)
            dc_sc[...] = jnp.zeros((T, 1), F32)

        def step(diagonal):
            qv = q_ref[...]
            kv = k_ref[...]
            dov = do_ref[...]
            st = lax.dot_general(kv, qv, _DN["nt"], preferred_element_type=F32) * SCALE
            st = st + (cq_ref[...] - ck_ref[...])
            if diagonal:
                row = lax.broadcasted_iota(jnp.int32, (T, T), 0)
                col = lax.broadcasted_iota(jnp.int32, (T, T), 1)
                st = jnp.where(col >= row, st, NEG_INF)
            pt = jnp.exp(st - lse_ref[...])
            dv_sc[...] += lax.dot_general(pt.astype(BF16), dov, _DN["nn"], preferred_element_type=F32)
            dpt = lax.dot_general(v_ref[...], dov, _DN["nt"], preferred_element_type=F32)
            dst = pt * (dpt - dl_ref[...])
            dc_sc[...] -= jnp.sum(dst, axis=1, keepdims=True)
            dcq_ref[i] += jnp.sum(dst, axis=0, keepdims=True)
            dsb = (dst * SCALE).astype(BF16)
            dk_sc[...] += lax.dot_general(dsb, qv, _DN["nn"], preferred_element_type=F32)
            rows = pl.ds(pl.multiple_of(i * T, T), T)
            dq_ref[rows, :] += lax.dot_general(dsb, kv, _DN["tn"], preferred_element_type=F32)

        @pl.when(i > j)
        def _():
            step(False)

        @pl.when(i == j)
        def _():
            step(True)

        @pl.when(i == nq - 1)
        def _():
            dk_ref[...] = dk_sc[...].astype(BF16)
            dv_ref[...] = dv_sc[...].astype(BF16)
            dc_ref[...] = dc_sc[...]

    qs = pl.BlockSpec((None, T, HEAD_DIM), lambda h, j, i: (h, jnp.maximum(i, j), 0))
    qrow = pl.BlockSpec((None, 1, T), lambda h, j, i: (h, 0, jnp.maximum(i, j)))
    ks = pl.BlockSpec((None, T, HEAD_DIM), lambda h, j, i: (h, j, 0))
    kcol = pl.BlockSpec((None, T, 1), lambda h, j, i: (h, j, 0))
    dqs = pl.BlockSpec((None, S, HEAD_DIM), lambda h, j, i: (h, 0, 0))
    dcqs = pl.BlockSpec((None, nq, 1, T), lambda h, j, i: (h, 0, 0, 0))
    return pl.pallas_call(
        body,
        out_shape=[jax.ShapeDtypeStruct((H, S, HEAD_DIM), F32), jax.ShapeDtypeStruct((H, S, HEAD_DIM), BF16),
                   jax.ShapeDtypeStruct((H, S, HEAD_DIM), BF16), jax.ShapeDtypeStruct((H, S, 1), F32),
                   jax.ShapeDtypeStruct((H, nq, 1, T), F32)],
        grid=(H, nq, nq), in_specs=[qs, ks, ks, qs, qrow, qrow, qrow, kcol], out_specs=[dqs, ks, ks, kcol, dcqs],
        scratch_shapes=[pltpu.VMEM((T, HEAD_DIM), F32), pltpu.VMEM((T, HEAD_DIM), F32), pltpu.VMEM((T, 1), F32)],
        compiler_params=_cparams(("parallel", "arbitrary", "arbitrary")), name=name,
    )(q, k, v, do, lse_row, delta_row, cq_row, ck_col)


def _rowdot(a, b, name):
    H = a.shape[0]
    T = 1024

    def body(a_ref, b_ref, o_ref):
        o_ref[...] = jnp.sum(a_ref[...].astype(F32) * b_ref[...].astype(F32), axis=-1, keepdims=True)

    blk = pl.BlockSpec((None, T, HEAD_DIM), lambda h, i: (h, i, 0))
    return pl.pallas_call(
        body, out_shape=jax.ShapeDtypeStruct((H, S, 1), F32), grid=(H, S // T), in_specs=[blk, blk],
        out_specs=pl.BlockSpec((None, T, 1), lambda h, i: (h, i, 0)),
        compiler_params=_cparams(("parallel", "parallel")), name=name,
    )(a, b)


W = B_WIN
N_HG = 3 * B_HPG
N_BLK = S // W


def _dil_tables():
    slopes = np.exp2((-8.0 * np.arange(1, N_HG + 1, dtype=np.float32) / N_HG).astype(np.float32)).astype(np.float32)
    dil = np.repeat(np.array([d for _, d in B_GROUPS], np.float32), B_HPG)
    coef = (slopes * dil).astype(np.float32)
    nbs = np.repeat(np.array([S // d // W for _, d in B_GROUPS], np.int32), B_HPG)
    return jnp.asarray(coef), jnp.asarray(nbs)


DIL_SUB = 8
DIL_ROWS = DIL_SUB * W
DIL_STEPS = S // DIL_ROWS


def _dil_bias(coef, transposed):
    row = lax.broadcasted_iota(jnp.int32, (W, 2 * W), 0)
    col = lax.broadcasted_iota(jnp.int32, (W, 2 * W), 1)
    dist = (col - row) if transposed else (row + W - col)
    valid = jnp.logical_and(dist >= 0, dist <= W)
    return jnp.where(valid, -coef * dist.astype(F32), NEG_INF), col


def _dil_specs():
    blk = pl.BlockSpec((None, DIL_ROWS, HEAD_DIM), lambda h, n: (h, n, 0))
    prev = pl.BlockSpec((None, W, HEAD_DIM), lambda h, n: (h, jnp.maximum(n * DIL_SUB - 1, 0), 0))
    nxt = pl.BlockSpec((None, W, HEAD_DIM), lambda h, n: (h, jnp.minimum((n + 1) * DIL_SUB, N_BLK - 1), 0))
    col = pl.BlockSpec((None, DIL_ROWS, 1), lambda h, n: (h, n, 0))
    row = pl.BlockSpec((None, 1, DIL_ROWS), lambda h, n: (h, 0, n))
    rnxt = pl.BlockSpec((None, 1, W), lambda h, n: (h, 0, jnp.minimum((n + 1) * DIL_SUB, N_BLK - 1)))
    smem = pl.BlockSpec(memory_space=pltpu.SMEM)
    return blk, prev, nxt, col, row, rnxt, smem


def _dil_fwd(q, k, v, name):
    coef_t, nbs_t = _dil_tables()

    def body(coef_ref, nbs_ref, q_ref, kh_ref, k_ref, vh_ref, v_ref, o_ref, lse_ref, kf, vf):
        hg = pl.program_id(0)
        n = pl.program_id(1)
        nbs = nbs_ref[hg]
        kf[0:W, :] = kh_ref[...]
        kf[W:, :] = k_ref[...]
        vf[0:W, :] = vh_ref[...]
        vf[W:, :] = v_ref[...]
        bias, col = _dil_bias(coef_ref[hg], False)
        for b in range(DIL_SUB):
            first = lax.rem(n * DIL_SUB + b, nbs) == 0
            rows = slice(b * W, (b + 1) * W)
            both = slice(b * W, (b + 2) * W)
            s = lax.dot_general(q_ref[rows, :], kf[both, :], _DN["nt"], preferred_element_type=F32) * SCALE + bias
            s = jnp.where(jnp.logical_and(first, col < W), NEG_INF, s)
            m = jnp.max(s, axis=1, keepdims=True)
            p = jnp.exp(s - m)
            l = jnp.sum(p, axis=1, keepdims=True)
            acc = lax.dot_general(p.astype(BF16), vf[both, :], _DN["nn"], preferred_element_type=F32)
            o_ref[rows, :] = acc / l
            lse_ref[rows, :] = m + jnp.log(l)

    blk, prev, _, col, _, _, smem = _dil_specs()
    return pl.pallas_call(
        body, out_shape=[jax.ShapeDtypeStruct((N_HG, S, HEAD_DIM), F32), jax.ShapeDtypeStruct((N_HG, S, 1), F32)],
        grid=(N_HG, DIL_STEPS), in_specs=[smem, smem, blk, prev, blk, prev, blk], out_specs=[blk, col],
        scratch_shapes=[pltpu.VMEM((DIL_ROWS + W, HEAD_DIM), BF16)] * 2,
        compiler_params=_cparams(("parallel", "parallel")), name=name,
    )(coef_t, nbs_t, q, k, k, v, v)


def _dil_merge(o, lse, name):
    T = 1024

    def body(o_ref, lse_ref, om_ref, omb_ref, l_ref):
        l0, l1, l2 = lse_ref[0], lse_ref[1], lse_ref[2]
        m = jnp.maximum(jnp.maximum(l0, l1), l2)
        e0, e1, e2 = jnp.exp(l0 - m), jnp.exp(l1 - m), jnp.exp(l2 - m)
        den = e0 + e1 + e2
        om = (e0 / den) * o_ref[0] + (e1 / den) * o_ref[1] + (e2 / den) * o_ref[2]
        om_ref[...] = om
        omb_ref[...] = om.astype(BF16)
        l_ref[...] = m + jnp.log(den)

    ob = pl.BlockSpec((None, T, HEAD_DIM), lambda h, i: (h, i, 0))
    lb = pl.BlockSpec((None, T, 1), lambda h, i: (h, i, 0))
    return pl.pallas_call(
        body,
        out_shape=[jax.ShapeDtypeStruct((B_HPG, S, HEAD_DIM), F32), jax.ShapeDtypeStruct((B_HPG, S, HEAD_DIM), BF16),
                   jax.ShapeDtypeStruct((B_HPG, S, 1), F32)],
        grid=(B_HPG, S // T),
        in_specs=[pl.BlockSpec((3, None, T, HEAD_DIM), lambda h, i: (0, h, i, 0)),
                  pl.BlockSpec((3, None, T, 1), lambda h, i: (0, h, i, 0))],
        out_specs=[ob, ob, lb], compiler_params=_cparams(("parallel", "parallel")), name=name,
    )(o, lse)


def _dil_bwd_dq(q, k, v, do, lcol, dcol, name):
    coef_t, nbs_t = _dil_tables()

    def body(coef_ref, nbs_ref, q_ref, kh_ref, k_ref, vh_ref, v_ref, do_ref, l_ref, d_ref, dq_ref, kf, vf):
        hg = pl.program_id(0)
        n = pl.program_id(1)
        nbs = nbs_ref[hg]
        kf[0:W, :] = kh_ref[...]
        kf[W:, :] = k_ref[...]
        vf[0:W, :] = vh_ref[...]
        vf[W:, :] = v_ref[...]
        bias, col = _dil_bias(coef_ref[hg], False)
        for b in range(DIL_SUB):
            first = lax.rem(n * DIL_SUB + b, nbs) == 0
            rows = slice(b * W, (b + 1) * W)
            both = slice(b * W, (b + 2) * W)
            kk = kf[both, :]
            s = lax.dot_general(q_ref[rows, :], kk, _DN["nt"], preferred_element_type=F32) * SCALE + bias
            s = jnp.where(jnp.logical_and(first, col < W), NEG_INF, s)
            p = jnp.exp(s - l_ref[rows, :])
            dp = lax.dot_general(do_ref[rows, :], vf[both, :], _DN["nt"], preferred_element_type=F32)
            ds = (p * (dp - d_ref[rows, :]) * SCALE).astype(BF16)
            dq_ref[rows, :] = lax.dot_general(ds, kk, _DN["nn"], preferred_element_type=F32).astype(BF16)

    blk, prev, _, col, _, _, smem = _dil_specs()
    return pl.pallas_call(
        body, out_shape=jax.ShapeDtypeStruct((N_HG, S, HEAD_DIM), BF16), grid=(N_HG, DIL_STEPS),
        in_specs=[smem, smem, blk, prev, blk, prev, blk, blk, col, col], out_specs=blk,
        scratch_shapes=[pltpu.VMEM((DIL_ROWS + W, HEAD_DIM), BF16)] * 2,
        compiler_params=_cparams(("parallel", "parallel")), name=name,
    )(coef_t, nbs_t, q, k, k, v, v, do, lcol, dcol)


def _dil_bwd_dkv(q, k, v, do, lrow, drow, name):
    coef_t, nbs_t = _dil_tables()

    def body(coef_ref, nbs_ref, k_ref, v_ref, q_ref, qn_ref, do_ref, don_ref, l_ref, ln_ref, d_ref, dn_ref,
             dk_ref, dv_ref, qf, dof, lf, df):
        hg = pl.program_id(0)
        n = pl.program_id(1)
        nbs = nbs_ref[hg]
        qf[0:DIL_ROWS, :] = q_ref[...]
        qf[DIL_ROWS:, :] = qn_ref[...]
        dof[0:DIL_ROWS, :] = do_ref[...]
        dof[DIL_ROWS:, :] = don_ref[...]
        lf[:, 0:DIL_ROWS] = l_ref[...]
        lf[:, DIL_ROWS:] = ln_ref[...]
        df[:, 0:DIL_ROWS] = d_ref[...]
        df[:, DIL_ROWS:] = dn_ref[...]
        bias, col = _dil_bias(coef_ref[hg], True)
        for b in range(DIL_SUB):
            no_next = lax.rem(n * DIL_SUB + b + 1, nbs) == 0
            rows = slice(b * W, (b + 1) * W)
            both = slice(b * W, (b + 2) * W)
            qq = qf[both, :]
            dd = dof[both, :]
            st = lax.dot_general(k_ref[rows, :], qq, _DN["nt"], preferred_element_type=F32) * SCALE + bias
            st = jnp.where(jnp.logical_and(no_next, col >= W), NEG_INF, st)
            pt = jnp.exp(st - lf[:, both])
            dv_ref[rows, :] = lax.dot_general(pt.astype(BF16), dd, _DN["nn"], preferred_element_type=F32).astype(BF16)
            dpt = lax.dot_general(v_ref[rows, :], dd, _DN["nt"], preferred_element_type=F32)
            dst = (pt * (dpt - df[:, both]) * SCALE).astype(BF16)
            dk_ref[rows, :] = lax.dot_general(dst, qq, _DN["nn"], preferred_element_type=F32).astype(BF16)

    blk, _, nxt, _, row, rnxt, smem = _dil_specs()
    return pl.pallas_call(
        body, out_shape=[jax.ShapeDtypeStruct((N_HG, S, HEAD_DIM), BF16)] * 2, grid=(N_HG, DIL_STEPS),
        in_specs=[smem, smem, blk, blk, blk, nxt, blk, nxt, row, rnxt, row, rnxt], out_specs=[blk, blk],
        scratch_shapes=[pltpu.VMEM((DIL_ROWS + W, HEAD_DIM), BF16)] * 2 + [pltpu.VMEM((1, DIL_ROWS + W), F32)] * 2,
        compiler_params=_cparams(("parallel", "parallel")), name=name,
    )(coef_t, nbs_t, k, v, q, q, do, do, lrow, lrow, drow, drow)


FFN_ROWS = 512
FFN_COLS = 256
HALO = 8


def _shifted(u, halo, back):
    T = u.shape[0]
    rows = lax.broadcasted_iota(jnp.int32, u.shape, 0)
    if back:
        s1 = jnp.where(rows == 0, halo[HALO - 1:HALO, :], pltpu.roll(u, 1, 0))
        s2 = jnp.where(rows == 0, halo[HALO - 2:HALO - 1, :],
                       jnp.where(rows == 1, halo[HALO - 1:HALO, :], pltpu.roll(u, 2, 0)))
    else:
        s1 = jnp.where(rows == T - 1, halo[0:1, :], pltpu.roll(u, T - 1, 0))
        s2 = jnp.where(rows == T - 1, halo[1:2, :],
                       jnp.where(rows == T - 2, halo[0:1, :], pltpu.roll(u, T - 2, 0)))
    return s1, s2


def _conv_parts(u_ref, h_ref, w_ref, b_ref, first):
    out = []
    for p in range(2):
        u = u_ref[p]
        halo = jnp.where(first, 0.0, h_ref[p])
        u1, u2 = _shifted(u, halo, True)
        w = w_ref[p]
        out.append((w[0:1, :] * u2 + w[1:2, :] * u1 + w[2:3, :] * u + b_ref[p], u1, u2, u))
    return out


def _ffn_specs():
    T, C = FFN_ROWS, FFN_COLS
    blk = pl.BlockSpec((2, T, C), lambda j, i: (0, i, j))
    prev = pl.BlockSpec((2, HALO, C), lambda j, i: (0, jnp.maximum(i * (T // HALO) - 1, 0), j))
    nxt = pl.BlockSpec((2, HALO, C), lambda j, i: (0, jnp.minimum((i + 1) * (T // HALO), S // HALO - 1), j))
    wsp = pl.BlockSpec((2, 3, C), lambda j, i: (0, 0, j))
    bsp = pl.BlockSpec((2, 1, C), lambda j, i: (0, 0, j))
    one = pl.BlockSpec((T, C), lambda j, i: (i, j))
    return blk, prev, nxt, wsp, bsp, one


def _ffn_act_fwd(u, w, b, name):
    blk, prev, _, wsp, bsp, one = _ffn_specs()

    def body(u_ref, h_ref, w_ref, b_ref, o_ref):
        (a, _, _, _), (g, _, _, _) = _conv_parts(u_ref, h_ref, w_ref, b_ref, pl.program_id(1) == 0)
        o_ref[...] = (g / (1.0 + jnp.exp(-g)) * a).astype(BF16)

    return pl.pallas_call(
        body, out_shape=jax.ShapeDtypeStruct((S, FF), BF16), grid=(FF // FFN_COLS, S // FFN_ROWS),
        in_specs=[blk, prev, wsp, bsp], out_specs=one,
        compiler_params=_cparams(("parallel", "parallel")), name=name,
    )(u, u, w, b)


def _ffn_act_bwd(u, dact, w, b, name):
    blk, prev, _, wsp, bsp, one = _ffn_specs()

    def body(u_ref, h_ref, da_ref, w_ref, b_ref, duc_ref, dwb_ref):
        i = pl.program_id(1)
        (a, a1, a2, a0), (g, g1, g2, g0) = _conv_parts(u_ref, h_ref, w_ref, b_ref, i == 0)
        dact_v = da_ref[...]
        sg = 1.0 / (1.0 + jnp.exp(-g))
        d_a = dact_v * (g * sg)
        d_g = dact_v * a * (sg * (1.0 + g * (1.0 - sg)))
        duc_ref[0] = d_a
        duc_ref[1] = d_g

        @pl.when(i == 0)
        def _():
            dwb_ref[...] = jnp.zeros(dwb_ref.shape, F32)

        for p, (d, s2, s1, s0) in enumerate(((d_a, a2, a1, a0), (d_g, g2, g1, g0))):
            dwb_ref[p, 0:1, :] += jnp.sum(d * s2, axis=0, keepdims=True)
            dwb_ref[p, 1:2, :] += jnp.sum(d * s1, axis=0, keepdims=True)
            dwb_ref[p, 2:3, :] += jnp.sum(d * s0, axis=0, keepdims=True)
            dwb_ref[p, 3:4, :] += jnp.sum(d, axis=0, keepdims=True)

    return pl.pallas_call(
        body, out_shape=[jax.ShapeDtypeStruct((2, S, FF), F32), jax.ShapeDtypeStruct((2, 8, FF), F32)],
        grid=(FF // FFN_COLS, S // FFN_ROWS), in_specs=[blk, prev, one, wsp, bsp],
        out_specs=[blk, pl.BlockSpec((2, 8, FFN_COLS), lambda j, i: (0, 0, j))],
        compiler_params=_cparams(("parallel", "arbitrary")), name=name,
    )(u, u, dact, w, b)


def _ffn_conv_bwd(duc, w, name):
    blk, _, nxt, wsp, _, _ = _ffn_specs()
    last = S // FFN_ROWS - 1

    def body(d_ref, h_ref, w_ref, du_ref):
        is_last = pl.program_id(1) == last
        for p in range(2):
            d = d_ref[p]
            halo = jnp.where(is_last, 0.0, h_ref[p])
            d1, d2 = _shifted(d, halo, False)
            wv = w_ref[p]
            du_ref[p] = (wv[2:3, :] * d + wv[1:2, :] * d1 + wv[0:1, :] * d2).astype(BF16)

    return pl.pallas_call(
        body, out_shape=jax.ShapeDtypeStruct((2, S, FF), BF16), grid=(FF // FFN_COLS, S // FFN_ROWS),
        in_specs=[blk, nxt, wsp], out_specs=blk,
        compiler_params=_cparams(("parallel", "parallel")), name=name,
    )(duc, duc, w)


def _adam_update(w, gv, m, v):
    c1 = 1.0 / (1.0 - ADAM_B1 ** ADAM_STEP)
    c2 = 1.0 / (1.0 - ADAM_B2 ** ADAM_STEP)
    mn = ADAM_B1 * m + (1.0 - ADAM_B1) * gv
    vn = ADAM_B2 * v + (1.0 - ADAM_B2) * (gv * gv)
    return -ADAM_LR * ((mn * c1) / (jnp.sqrt(vn * c2) + ADAM_EPS) + ADAM_WD * w), mn, vn


def _adamw(w, g, m, v, name):
    rows = w.shape[0]
    T = 8
    for cand in (256, 128, 64, 32, 16, 8):
        if rows % cand == 0:
            T = cand
            break

    def body(w_ref, g_ref, m_ref, v_ref, d_ref, mo_ref, vo_ref):
        d_ref[...], mo_ref[...], vo_ref[...] = _adam_update(w_ref[...], g_ref[...], m_ref[...], v_ref[...])

    blk = pl.BlockSpec((T, w.shape[1]), lambda i: (i, 0))
    sds = jax.ShapeDtypeStruct(w.shape, F32)
    return pl.pallas_call(
        body, out_shape=[sds, sds, sds], grid=(rows // T,), in_specs=[blk] * 4, out_specs=[blk] * 3,
        compiler_params=_cparams(("parallel",)), name=name,
    )(w, g, m, v)


ANY = pl.BlockSpec(memory_space=pl.ANY)


def _place():
    x, y, c = lax.axis_index("x"), lax.axis_index("y"), lax.axis_index("c")
    chips = [(1 - x, y), (x, 1 - y), (1 - x, 1 - y)]
    return x, y, c, chips


def _place_own(w, k_arr):
    T = FLAT_T

    def body(k_ref, w_ref, o_ref):
        o_ref[...] = w_ref[...]

    return pl.pallas_call(
        body, out_shape=jax.ShapeDtypeStruct((N_CHIPS, FLAT_ROWS, FLAT_W), w.dtype),
        grid_spec=pltpu.PrefetchScalarGridSpec(
            num_scalar_prefetch=1, grid=(FLAT_ROWS // T,),
            in_specs=[pl.BlockSpec((T, FLAT_W), lambda i, k: (i, 0))],
            out_specs=pl.BlockSpec((None, T, FLAT_W), lambda i, k: (k[0], i, 0))),
        compiler_params=_cparams(("parallel",)), name="gather_place_own",
    )(k_arr, w)


def _allgather_shards(w, buf):
    def body(w_ref, buf_ref, g_ref, send_sems, recv_sems):
        x, y, c, chips = _place()
        myk = 2 * x + y
        sibling = (x, y, 1 - c)
        h0 = pl.multiple_of(c * HALF_ROWS, 16)
        h1 = pl.multiple_of((1 - c) * HALF_ROWS, 16)

        def half(k, start):
            return g_ref.at[k, pl.ds(start, HALF_ROWS), :]

        def rcopy(sem, src, dst, to):
            return pltpu.make_async_remote_copy(src_ref=src, dst_ref=dst, send_sem=send_sems.at[sem],
                                                recv_sem=recv_sems.at[sem], device_id=to, device_id_type=MESH)

        ici = [rcopy(r, w_ref.at[pl.ds(h0, HALF_ROWS), :], half(myk, h0), (*chip, c)) for r, chip in enumerate(chips)]
        for cp in ici:
            cp.start()
        ks = [2 * cx + cy for cx, cy in chips]
        fwd = [rcopy(3 + r, half(ks[r], h0), half(ks[r], h0), sibling) for r in range(3)]
        for r in range(3):
            rcopy(r, half(ks[r], h0), half(ks[r], h0), (*chips[r], c)).wait_recv()
            fwd[r].start()
        for r in range(3):
            rcopy(3 + r, half(ks[r], h1), half(ks[r], h1), sibling).wait_recv()
        for cp in ici + fwd:
            cp.wait_send()

    return pl.pallas_call(
        body, out_shape=jax.ShapeDtypeStruct((N_CHIPS, FLAT_ROWS, FLAT_W), w.dtype), in_specs=[ANY, ANY], out_specs=ANY,
        scratch_shapes=[pltpu.SemaphoreType.DMA((6,)), pltpu.SemaphoreType.DMA((6,))],
        input_output_aliases={1: 0},
        compiler_params=pltpu.CompilerParams(has_side_effects=True), name="allgather_shards",
    )(w, buf)


def _sibling_swap_half(g):
    def body(g_ref, o_ref, send_sem, recv_sem):
        x, y, c, _ = _place()
        theirs = pl.multiple_of((1 - c) * HALF_ROWS, 8)
        cp = pltpu.make_async_remote_copy(src_ref=g_ref.at[:, pl.ds(theirs, HALF_ROWS), :], dst_ref=o_ref,
                                          send_sem=send_sem, recv_sem=recv_sem, device_id=(x, y, 1 - c),
                                          device_id_type=MESH)
        cp.start()
        cp.wait()

    return pl.pallas_call(
        body, out_shape=jax.ShapeDtypeStruct((N_CHIPS, HALF_ROWS, FLAT_W), F32), in_specs=[ANY], out_specs=ANY,
        scratch_shapes=[pltpu.SemaphoreType.DMA, pltpu.SemaphoreType.DMA],
        compiler_params=pltpu.CompilerParams(has_side_effects=True), name="rs_sibling_swap",
    )(g)


def _pair_sum(g, other, c_arr):
    T = FLAT_T

    def body(c_ref, g_ref, o_ref, s_ref):
        s_ref[...] = (g_ref[...] + o_ref[...]).astype(BF16)

    nb = HALF_ROWS // T
    return pl.pallas_call(
        body, out_shape=jax.ShapeDtypeStruct((N_CHIPS, HALF_ROWS, FLAT_W), BF16),
        grid_spec=pltpu.PrefetchScalarGridSpec(
            num_scalar_prefetch=1, grid=(N_CHIPS, nb),
            in_specs=[pl.BlockSpec((None, T, FLAT_W), lambda k, i, c: (k, c[0] * nb + i, 0)),
                      pl.BlockSpec((None, T, FLAT_W), lambda k, i, c: (k, i, 0))],
            out_specs=pl.BlockSpec((None, T, FLAT_W), lambda k, i, c: (k, i, 0))),
        compiler_params=_cparams(("parallel", "parallel")), name="rs_pair_sum",
    )(c_arr, g, other)


def _chip_exchange(s):
    def body(s_ref, o_ref, send_sems, recv_sems):
        x, y, c, chips = _place()
        cps = []
        for r, (cx, cy) in enumerate(chips):
            cps.append(pltpu.make_async_remote_copy(
                src_ref=s_ref.at[2 * cx + cy], dst_ref=o_ref.at[r], send_sem=send_sems.at[r],
                recv_sem=recv_sems.at[r], device_id=(cx, cy, c), device_id_type=MESH))
        for cp in cps:
            cp.start()
        for cp in cps:
            cp.wait()

    return pl.pallas_call(
        body, out_shape=jax.ShapeDtypeStruct((3, HALF_ROWS, FLAT_W), BF16), in_specs=[ANY], out_specs=ANY,
        scratch_shapes=[pltpu.SemaphoreType.DMA((3,)), pltpu.SemaphoreType.DMA((3,))],
        compiler_params=pltpu.CompilerParams(has_side_effects=True), name="rs_chip_exchange",
    )(s)


def _chip_sum(s, r, k_arr):
    T = FLAT_T

    def body(k_ref, s_ref, r_ref, o_ref):
        o_ref[...] = ((s_ref[...].astype(F32) + r_ref[0].astype(F32)) + r_ref[1].astype(F32)) + r_ref[2].astype(F32)

    return pl.pallas_call(
        body, out_shape=jax.ShapeDtypeStruct((HALF_ROWS, FLAT_W), F32),
        grid_spec=pltpu.PrefetchScalarGridSpec(
            num_scalar_prefetch=1, grid=(HALF_ROWS // T,),
            in_specs=[pl.BlockSpec((None, T, FLAT_W), lambda i, k: (k[0], i, 0)),
                      pl.BlockSpec((3, T, FLAT_W), lambda i, k: (0, i, 0))],
            out_specs=pl.BlockSpec((T, FLAT_W), lambda i, k: (i, 0))),
        compiler_params=_cparams(("parallel",)), name="rs_chip_sum",
    )(k_arr, s, r)


def _sibling_send(t):
    def body(t_ref, o_ref, send_sem, recv_sem):
        x, y, c, _ = _place()
        cp = pltpu.make_async_remote_copy(src_ref=t_ref, dst_ref=o_ref, send_sem=send_sem, recv_sem=recv_sem,
                                          device_id=(x, y, 1 - c), device_id_type=MESH)
        cp.start()
        cp.wait()

    return pl.pallas_call(
        body, out_shape=jax.ShapeDtypeStruct((HALF_ROWS, FLAT_W), F32), in_specs=[ANY], out_specs=ANY,
        scratch_shapes=[pltpu.SemaphoreType.DMA, pltpu.SemaphoreType.DMA],
        compiler_params=pltpu.CompilerParams(has_side_effects=True), name="rs_sibling_send",
    )(t)


def _allreduce_small(v):
    def body(v_ref, o_ref, buf, send_sems, recv_sems):
        x, y, c, _ = _place()
        me = 4 * x + 2 * y + c
        buf[me] = v_ref[...]
        cps = []
        for mask in range(1, 8):
            a, b, d = (mask >> 2) & 1, (mask >> 1) & 1, mask & 1
            peer = (x + a - 2 * a * x, y + b - 2 * b * y, c + d - 2 * d * c)
            cps.append(pltpu.make_async_remote_copy(
                src_ref=v_ref, dst_ref=buf.at[me], send_sem=send_sems.at[mask - 1], recv_sem=recv_sems.at[mask - 1],
                device_id=peer, device_id_type=MESH))
        for cp in cps:
            cp.start()
        for cp in cps:
            cp.wait()
        total = buf[0]
        for dev in range(1, 8):
            total = total + buf[dev]
        o_ref[...] = total

    vm = pl.BlockSpec(memory_space=pltpu.VMEM)
    return pl.pallas_call(
        body, out_shape=jax.ShapeDtypeStruct((SMALL_ROWS, 1024), F32), in_specs=[vm], out_specs=vm,
        scratch_shapes=[pltpu.VMEM((8, SMALL_ROWS, 1024), F32), pltpu.SemaphoreType.DMA((7,)),
                        pltpu.SemaphoreType.DMA((7,))],
        compiler_params=pltpu.CompilerParams(has_side_effects=True), name="allreduce_small",
    )(v)


def _heads(t, n):
    return t.reshape(S, n, HEAD_DIM).transpose(1, 0, 2)


def _unheads(t):
    return t.transpose(1, 0, 2).reshape(S, t.shape[0] * HEAD_DIM)


def _to_residue(t, d):
    c = t.shape[-1]
    return t.reshape(B_HPG, S // d, d, c).transpose(0, 2, 1, 3).reshape(B_HPG, S, c)


def _from_residue(t, d):
    c = t.shape[-1]
    return t.reshape(B_HPG, d, S // d, c).transpose(0, 2, 1, 3).reshape(B_HPG, S, c)


def _dil_pack(t):
    return jnp.concatenate([_to_residue(t[g], d) for g, (_, d) in enumerate(B_GROUPS)], axis=0)


def _dil_unpack(t):
    return jnp.stack([_from_residue(t[g * B_HPG:(g + 1) * B_HPG], d) for g, (_, d) in enumerate(B_GROUPS)], axis=0)


def _col_to_row(t):
    return t.reshape(t.shape[0], 1, S)


def _ffn_fwd(x, g, w_up, cw, cb, w_down, tag):
    h = _rms_fwd(x, g, f"{tag}_norm")
    u = _mm(h, w_up, mode="nn", tm=1024, tn=1408, tk=1024, o_split=2, name=f"{tag}_up")
    act = _ffn_act_fwd(u, cw, cb, f"{tag}_act")
    x_out = _mm(act, w_down, mode="nn", tm=1024, tn=512, tk=FF, res=x, name=f"{tag}_down")
    return x_out, (h, u, act)


def _ffn_bwd(x, g, w_up, cw, cb, w_down, saved, dx, dxb, tag):
    h, u, act = saved
    d_w_down = _mm(act, dxb, mode="tn", tm=1408, tn=512, tk=1024, name=f"{tag}_dwdown")
    dact = _mm(dxb, w_down, mode="nt", tm=1024, tn=1408, tk=1024, name=f"{tag}_dact")
    duc, dwb = _ffn_act_bwd(u, dact, cw, cb, f"{tag}_dgate")
    du = _ffn_conv_bwd(duc, cw, f"{tag}_dconv")
    d_w_up = _mm(h, du, mode="tn", tm=1024, tn=1408, tk=1024, b_split=2, name=f"{tag}_dwup")
    dh = _mm(du, w_up, mode="nt", tm=1024, tn=512, tk=1408, a_split=2, name=f"{tag}_dh")
    dx_new, dxb_new, (dg,) = _rms_bwd(x, dx, [(g, dh)], f"{tag}_dnorm")
    d_cw = dwb[:, 0:3, :].transpose(1, 0, 2).reshape(3, 2 * FF)
    d_cb = dwb[:, 3, :].reshape(2 * FF)
    return dx_new, dxb_new, dict(w_up=d_w_up, w_down=d_w_down, conv_w=d_cw, conv_b=d_cb, norm_g=dg.reshape(D))


def _local_step(x, target, p):
    g = {}
    h1 = _rms_fwd(x, p["mix_norm_g"][0], "a_norm")
    w_qkv = p["a_w_in"][:, :QKV_W]
    w_f = jnp.pad(p["a_w_in"][:, QKV_W:], ((0, 0), (0, LANES - A_HEADS)))
    b_f = jnp.pad(p["a_b_f"].reshape(1, A_HEADS), ((0, 0), (0, LANES - A_HEADS)))
    qkv = _mm(h1, w_qkv, mode="nn", tm=1024, tn=512, tk=1024, out_dtype=BF16, name="a_qkv")
    pf = _mm(h1, w_f, mode="nn", tm=1024, tn=LANES, tk=1024, name="a_gate")
    cum = _fgate_fwd(pf, b_f, "a_gate_scan")
    qkv_h = _heads(qkv, 3 * A_HEADS)
    qa, ka, va = qkv_h[:A_HEADS], qkv_h[A_HEADS:2 * A_HEADS], qkv_h[2 * A_HEADS:]
    c_hs = cum[:, :A_HEADS].T
    c_col, c_row = c_hs.reshape(A_HEADS, S, 1), c_hs.reshape(A_HEADS, 1, S)
    oa, lse_a = _fox_fwd(qa, ka, va, c_col, c_row, "a_attn")
    oa2 = _unheads(oa)
    x1 = _mm(oa2, p["a_w_out"], mode="nn", tm=1024, tn=512, tk=1024, res=x, name="a_out")
    x2, ffn0 = _ffn_fwd(x1, p["ffn_norm_g"][0], p["ffn_w_up"][0], p["conv_w"][0], p["conv_b"][0], p["ffn_w_down"][0], "f0")
    hk = _rms_fwd(x2, p["kv_norm_g"], "kv_norm")
    kvb = _mm(hk, p["w_kv"], mode="nn", tm=1024, tn=512, tk=1024, out_dtype=BF16, name="kv_proj")
    h3 = _rms_fwd(x2, p["mix_norm_g"][1], "b_norm")
    qb = _mm(h3, p["b_w_q"], mode="nn", tm=1024, tn=512, tk=1024, out_dtype=BF16, name="b_q")
    qd = _dil_pack(_heads(qb, N_HG).reshape(3, B_HPG, S, HEAD_DIM))
    kv_h = _heads(kvb, 2 * N_HG)
    kd = _dil_pack(kv_h[:N_HG].reshape(3, B_HPG, S, HEAD_DIM))
    vd = _dil_pack(kv_h[N_HG:].reshape(3, B_HPG, S, HEAD_DIM))
    od, lsed = _dil_fwd(qd, kd, vd, "b_attn")
    ob, obb, lse_b = _dil_merge(_dil_unpack(od), _dil_unpack(lsed), "b_merge")
    ob2 = _unheads(obb)
    x3 = _mm(ob2, p["b_w_out"], mode="nn", tm=1024, tn=512, tk=B_OUT_W, res=x2, name="b_out")
    x4, ffn1 = _ffn_fwd(x3, p["ffn_norm_g"][1], p["ffn_w_up"][1], p["conv_w"][1], p["conv_b"][1], p["ffn_w_down"][1], "f1")
    loss, dx, dxb, dg_final = _loss_head(x4, p["final_norm_g"], target, "loss_head")
    g["final_norm_g"] = dg_final.reshape(D)

    dx, dxb, gf1 = _ffn_bwd(x3, p["ffn_norm_g"][1], p["ffn_w_up"][1], p["conv_w"][1], p["conv_b"][1], p["ffn_w_down"][1],
                            ffn1, dx, dxb, "f1")
    g["b_w_out"] = _mm(ob2, dxb, mode="tn", tm=B_OUT_W, tn=512, tk=1024, name="b_dwout")
    dob = _heads(_mm(dxb, p["b_w_out"], mode="nt", tm=1024, tn=B_OUT_W, tk=1024, name="b_do"), B_HPG)
    delta_b = _rowdot(dob, ob, "b_delta")
    rep = lambda t: jnp.broadcast_to(t[None], (3,) + t.shape)
    dod = _dil_pack(rep(dob.astype(BF16)))
    l_d = _dil_pack(rep(lse_b))
    dl_d = _dil_pack(rep(delta_b))
    dqd = _dil_bwd_dq(qd, kd, vd, dod, l_d, dl_d, "b_dq")
    dkd, dvd = _dil_bwd_dkv(qd, kd, vd, dod, _col_to_row(l_d), _col_to_row(dl_d), "b_dkv")
    dqb = _unheads(_dil_unpack(dqd).reshape(N_HG, S, HEAD_DIM))
    dkvb = _unheads(jnp.concatenate([_dil_unpack(dkd).reshape(N_HG, S, HEAD_DIM),
                                     _dil_unpack(dvd).reshape(N_HG, S, HEAD_DIM)], axis=0))
    g["b_w_q"] = _mm(h3, dqb, mode="tn", tm=1024, tn=512, tk=1024, name="b_dwq")
    dh3 = _mm(dqb, p["b_w_q"], mode="nt", tm=1024, tn=512, tk=B_Q_W, name="b_dh")
    g["w_kv"] = _mm(hk, dkvb, mode="tn", tm=1024, tn=512, tk=1024, name="kv_dw")
    dhk = _mm(dkvb, p["w_kv"], mode="nt", tm=1024, tn=512, tk=1536, name="kv_dh")
    dx, dxb, (dg_mix1, dg_kv) = _rms_bwd(x2, dx, [(p["mix_norm_g"][1], dh3), (p["kv_norm_g"], dhk)], "b_dnorm")
    g["kv_norm_g"] = dg_kv.reshape(D)
    dx, dxb, gf0 = _ffn_bwd(x1, p["ffn_norm_g"][0], p["ffn_w_up"][0], p["conv_w"][0], p["conv_b"][0], p["ffn_w_down"][0],
                            ffn0, dx, dxb, "f0")
    g["a_w_out"] = _mm(oa2, dxb, mode="tn", tm=1024, tn=512, tk=1024, name="a_dwout")
    doa = _heads(_mm(dxb, p["a_w_out"], mode="nt", tm=1024, tn=512, tk=1024, name="a_do"), A_HEADS)
    delta_a = _rowdot(doa, oa, "a_delta")
    dqa, dka, dva, dck, dcq = _fox_bwd(qa, ka, va, doa.astype(BF16), _col_to_row(lse_a), _col_to_row(delta_a), c_row,
                                       c_col, "a_dattn")
    dqkv = _unheads(jnp.concatenate([dqa.astype(BF16), dka, dva], axis=0))
    pad_heads = lambda t: jnp.pad(t.reshape(A_HEADS, S).T, ((0, 0), (0, LANES - A_HEADS)))
    dpf, db_f = _fgate_bwd(pf, b_f, pad_heads(dck), pad_heads(dcq), "a_dgate_scan")
    g["a_b_f"] = db_f[:, :A_HEADS]
    d_w_qkv = _mm(h1, dqkv, mode="tn", tm=1024, tn=512, tk=1024, name="a_dwqkv")
    d_w_f = _mm(h1, dpf, mode="tn", tm=1024, tn=LANES, tk=1024, name="a_dwgate")
    g["a_w_in"] = jnp.concatenate([d_w_qkv, d_w_f[:, :A_HEADS]], axis=1)
    dh1 = _mm(dqkv, w_qkv, mode="nt", tm=1024, tn=512, tk=1536, name="a_dh")
    dh1 = _mm(dpf, w_f, mode="nt", tm=1024, tn=512, tk=LANES, res=dh1, name="a_dh_gate")
    dx, _, (dg_mix0,) = _rms_bwd(x, dx, [(p["mix_norm_g"][0], dh1)], "a_dnorm")

    g["mix_norm_g"] = jnp.stack([dg_mix0.reshape(D), dg_mix1.reshape(D)])
    g["ffn_norm_g"] = jnp.stack([gf0["norm_g"], gf1["norm_g"]])
    g["ffn_w_up"] = jnp.stack([gf0["w_up"], gf1["w_up"]])
    g["ffn_w_down"] = jnp.stack([gf0["w_down"], gf1["w_down"]])
    g["ffn_conv_w"] = jnp.stack([gf0["conv_w"], gf1["conv_w"]])
    g["ffn_conv_b"] = jnp.stack([gf0["conv_b"], gf1["conv_b"]])
    return loss[0, 0], dx, g


_SHARD_SHAPES = {"a_w_in": (1, 1024, 772), "a_w_out": (1, 256, 1024), "b_w_q": (1, 1024, 384), "b_w_out": (1, 512, 256),
                 "w_kv": (1024, 768), "ffn_w_up": (2, 1024, 1408), "ffn_w_down": (2, 704, 1024), "ffn_conv_w": (2, 3, 1408)}
_SHARD_AXIS = {"a_w_in": 2, "a_w_out": 1, "b_w_q": 2, "b_w_out": 2, "w_kv": 1, "ffn_w_up": 2, "ffn_w_down": 1,
               "ffn_conv_w": 2}
_SMALL = (("kv_norm_g", (1024,)), ("mix_norm_g", (2, 1024)), ("ffn_norm_g", (2, 1024)), ("final_norm_g", (1024,)),
          ("a_b_f", (1, 16)), ("ffn_conv_b", (2, 5632)))


def _slabs(t, L, R, C, rpad):
    nc = -(-C // FLAT_W)
    t = jnp.pad(t.reshape(L, R, C), ((0, 0), (0, rpad - R), (0, nc * FLAT_W - C)))
    return t.reshape(L, rpad, nc, FLAT_W).transpose(0, 2, 1, 3).reshape(L * nc * rpad, FLAT_W)


def _unslabs(rows, L, R, C, rpad):
    nc = -(-C // FLAT_W)
    return rows.reshape(L, nc, rpad, FLAT_W).transpose(0, 2, 1, 3).reshape(L, rpad, nc * FLAT_W)[:, :R, :C]


_SEG_RT = {"ffn_w_down": 704, "a_w_in": 1024, "a_w_out": 256, "b_w_q": 1024, "b_w_out": 512, "w_kv": 1024,
           "ffn_w_up": 1024, "ffn_conv_w": 16}
_ROW_SHARDED = ("a_w_out", "ffn_w_down")


def _seg(name):
    off = 0
    for s in _SEGS:
        if s[0] == name:
            _, L, R, C, rpad = s
            rt = _SEG_RT[name]
            assert off % rt == 0 and rpad % rt == 0 and (HALF_ROWS % rt == 0 or off + _seg_rows(*s) <= HALF_ROWS)
            return dict(L=L, R=R, C=C, rpad=rpad, nc=-(-C // FLAT_W), rt=rt, off=off, ni=rpad // rt)
        off += _seg_rows(*s)
    raise KeyError(name)


def _flat_block(sg, term=0):
    base = (sg["off"] + term * sg["L"] * sg["nc"] * sg["rpad"]) // sg["rt"]
    return lambda l, j, i: base + (l * sg["nc"] + j) * sg["ni"] + i


def _native3(t, name):
    sg = _seg(name)
    t = t.reshape(sg["L"], sg["R"], sg["C"])
    return jnp.pad(t, ((0, 0), (0, sg["rpad"] - sg["R"]), (0, 0))) if sg["rpad"] != sg["R"] else t


def _slab_pack(flat, t, name, term=None):
    sg = _seg(name)
    rt = sg["rt"]
    rb = _flat_block(sg, term or 0)

    def body(*refs):
        t_ref, o_ref = refs[-2], refs[-1]
        val = t_ref[...]
        o_ref[...] = val.astype(BF16) if term is None else _split3(val)[term]

    in_specs = [pl.BlockSpec((None, rt, FLAT_W), lambda l, j, i: (l, i, j))]
    args = [t]
    if flat is not None:
        in_specs, args = [ANY] + in_specs, [flat] + args
    return pl.pallas_call(
        body, out_shape=jax.ShapeDtypeStruct((FLAT_ROWS, FLAT_W), BF16), grid=(sg["L"], sg["nc"], sg["ni"]),
        in_specs=in_specs, out_specs=pl.BlockSpec((rt, FLAT_W), lambda l, j, i: (rb(l, j, i), 0)),
        input_output_aliases={0: 0} if flat is not None else {},
        compiler_params=_cparams(("parallel", "parallel", "parallel")), name=f"pack_{name}_{term or 0}",
    )(*args)


def _full_spec(sg, name):
    rt, nc, ni = sg["rt"], sg["nc"], sg["ni"]
    if name in _ROW_SHARDED:
        return (sg["L"], N_CHIPS * sg["R"], sg["C"]), pl.BlockSpec((None, rt, FLAT_W), lambda k, l, j, i: (l, k * ni + i, j))
    return ((sg["L"], sg["rpad"], N_CHIPS * nc * FLAT_W),
            pl.BlockSpec((None, rt, FLAT_W), lambda k, l, j, i: (l, i, k * nc + j)))


def _slab_unpack(gathered, name):
    sg = _seg(name)
    rb = _flat_block(sg)
    shape, spec = _full_spec(sg, name)

    def body(g_ref, o_ref):
        o_ref[...] = g_ref[...]

    return pl.pallas_call(
        body, out_shape=jax.ShapeDtypeStruct(shape, BF16), grid=(N_CHIPS, sg["L"], sg["nc"], sg["ni"]),
        in_specs=[pl.BlockSpec((None, sg["rt"], FLAT_W), lambda k, l, j, i: (k, rb(l, j, i), 0))], out_specs=spec,
        compiler_params=_cparams(("parallel",) * 4), name=f"unpack_{name}",
    )(gathered)


def _slab_pack_grad(flat4, g, name):
    sg = _seg(name)
    rb = _flat_block(sg)
    shape, spec = _full_spec(sg, name)
    assert g.shape == shape, (name, g.shape, shape)

    def body(*refs):
        refs[-1][...] = refs[-2][...]

    in_specs, args = [spec], [g]
    if flat4 is not None:
        in_specs, args = [pl.BlockSpec(memory_space=pl.ANY)] + in_specs, [flat4] + args
    return pl.pallas_call(
        body, out_shape=jax.ShapeDtypeStruct((N_CHIPS, FLAT_ROWS, FLAT_W), F32), grid=(N_CHIPS, sg["L"], sg["nc"], sg["ni"]),
        in_specs=in_specs, out_specs=pl.BlockSpec((None, sg["rt"], FLAT_W), lambda k, l, j, i: (k, rb(l, j, i), 0)),
        input_output_aliases={0: 0} if flat4 is not None else {},
        compiler_params=_cparams(("parallel",) * 4), name=f"packgrad_{name}",
    )(*args)


def _adamw_shard(w, m, v, g_mine, g_other, c_arr, name):
    sg = _seg(name)
    rt = sg["rt"]
    rb = _flat_block(sg)
    per_half = HALF_ROWS // rt

    def half_of(l, j, i):
        return (rb(l, j, i) * rt) // HALF_ROWS

    def body(c_ref, w_ref, m_ref, v_ref, gm_ref, go_ref, g_ref, d_ref, mo_ref, vo_ref):
        is_mine = half_of(pl.program_id(0), pl.program_id(1), pl.program_id(2)) == c_ref[0]
        gv = jnp.where(is_mine, gm_ref[...], go_ref[...])
        g_ref[...] = gv
        d_ref[...], mo_ref[...], vo_ref[...] = _adam_update(w_ref[...], gv, m_ref[...], v_ref[...])

    nat = pl.BlockSpec((None, rt, FLAT_W), lambda l, j, i, c: (l, i, j))
    half = pl.BlockSpec((rt, FLAT_W), lambda l, j, i, c: (rb(l, j, i) - half_of(l, j, i) * per_half, 0))
    sds = jax.ShapeDtypeStruct(w.shape, F32)
    return pl.pallas_call(
        body, out_shape=[sds] * 4,
        grid_spec=pltpu.PrefetchScalarGridSpec(num_scalar_prefetch=1, grid=(sg["L"], sg["nc"], sg["ni"]),
                                               in_specs=[nat, nat, nat, half, half], out_specs=[nat] * 4),
        compiler_params=_cparams(("parallel", "parallel", "parallel")), name=f"adamw_{name}",
    )(c_arr, w, m, v, g_mine, g_other)


def _pack_small(vals, loss=None):
    parts = [vals[name].astype(F32).reshape(-1) for name, _ in _SMALL]
    if loss is not None:
        parts.append(loss.reshape(1))
    flat = jnp.concatenate(parts)
    return jnp.pad(flat, (0, SMALL_ROWS * 1024 - flat.shape[0])).reshape(SMALL_ROWS, 1024)


def _unpack_small(flat):
    flat = flat.reshape(-1)
    out = {}
    o = 0
    for name, shape in _SMALL:
        n = int(np.prod(shape))
        out[name] = flat[o:o + n].reshape(shape)
        o += n
    return out, flat[o]


_BIG = ("a_w_in", "a_w_out", "b_w_q", "b_w_out", "w_kv", "ffn_w_up", "ffn_w_down", "ffn_conv_w")
A_IN_PAD = 896


def _pack_weights(w):
    flat = None
    for name in _BIG:
        t = _native3(w[name], name)
        for term in ((0, 1, 2) if name == "ffn_conv_w" else (None,)):
            flat = _slab_pack(flat, t, name, term)
    return flat


def _full_weights(gathered):
    full = {name: _slab_unpack(gathered, name) for name in _BIG if name != "ffn_conv_w"}
    a_in = full["a_w_in"].reshape(D, N_CHIPS, A_IN_PAD)[:, :, :772].reshape(D, N_CHIPS * 772)
    sg = _seg("ffn_conv_w")
    n1 = sg["nc"] * sg["rpad"]
    per_chip = []
    for k in range(N_CHIPS):
        terms = [_unslabs(gathered[k, sg["off"] + i * n1:sg["off"] + (i + 1) * n1], 1, sg["R"], sg["C"], sg["rpad"]).astype(F32)
                 for i in range(CONV_TERMS)]
        per_chip.append((terms[0] + terms[1]) + terms[2])
    cw = jnp.concatenate(per_chip, axis=2).reshape(2, 3, 2, FF).transpose(0, 2, 1, 3)
    return dict(a_w_in=a_in, a_w_out=full["a_w_out"][0], b_w_q=full["b_w_q"][0], b_w_out=full["b_w_out"][0],
                w_kv=full["w_kv"][0], ffn_w_up=full["ffn_w_up"], ffn_w_down=full["ffn_w_down"], conv_w=cw)


def _shard_grads(g):
    a_in = jnp.pad(g["a_w_in"].reshape(D, N_CHIPS, 772), ((0, 0), (0, 0), (0, A_IN_PAD - 772)))
    sgc = _seg("ffn_conv_w")
    full = {"a_w_in": a_in.reshape(1, D, N_CHIPS * A_IN_PAD), "a_w_out": g["a_w_out"][None], "b_w_q": g["b_w_q"][None],
            "b_w_out": g["b_w_out"][None], "w_kv": g["w_kv"][None], "ffn_w_up": g["ffn_w_up"],
            "ffn_w_down": g["ffn_w_down"],
            "ffn_conv_w": jnp.pad(g["ffn_conv_w"].reshape(1, sgc["R"], 2 * FF), ((0, 0), (0, sgc["rpad"] - sgc["R"]), (0, 0)))}
    flat4 = None
    for name in _BIG:
        flat4 = _slab_pack_grad(flat4, full[name], name)
    return flat4


_WEIGHTS = ["a_w_in", "a_b_f", "a_w_out", "b_w_q", "b_w_out", "kv_norm_g", "w_kv", "mix_norm_g", "ffn_norm_g", "ffn_w_up",
            "ffn_conv_w", "ffn_conv_b", "ffn_w_down", "final_norm_g"]


def kernel(x, a_w_in, a_b_f, a_w_out, b_w_q, b_w_out, kv_norm_g, w_kv, mix_norm_g, ffn_norm_g, ffn_w_up, ffn_conv_w, ffn_conv_b, ffn_w_down, final_norm_g, loss_target, m_a_w_in, m_a_b_f, m_a_w_out, m_b_w_q, m_b_w_out, m_kv_norm_g, m_w_kv, m_mix_norm_g, m_ffn_norm_g, m_ffn_w_up, m_ffn_conv_w, m_ffn_conv_b, m_ffn_w_down, m_final_norm_g, v_a_w_in, v_a_b_f, v_a_w_out, v_b_w_q, v_b_w_out, v_kv_norm_g, v_w_kv, v_mix_norm_g, v_ffn_norm_g, v_ffn_w_up, v_ffn_conv_w, v_ffn_conv_b, v_ffn_w_down, v_final_norm_g):
    w = dict(a_w_in=a_w_in, a_b_f=a_b_f, a_w_out=a_w_out, b_w_q=b_w_q, b_w_out=b_w_out, kv_norm_g=kv_norm_g, w_kv=w_kv,
             mix_norm_g=mix_norm_g, ffn_norm_g=ffn_norm_g, ffn_w_up=ffn_w_up, ffn_conv_w=ffn_conv_w, ffn_conv_b=ffn_conv_b,
             ffn_w_down=ffn_w_down, final_norm_g=final_norm_g)
    m = dict(a_w_in=m_a_w_in, a_b_f=m_a_b_f, a_w_out=m_a_w_out, b_w_q=m_b_w_q, b_w_out=m_b_w_out, kv_norm_g=m_kv_norm_g,
             w_kv=m_w_kv, mix_norm_g=m_mix_norm_g, ffn_norm_g=m_ffn_norm_g, ffn_w_up=m_ffn_w_up, ffn_conv_w=m_ffn_conv_w,
             ffn_conv_b=m_ffn_conv_b, ffn_w_down=m_ffn_w_down, final_norm_g=m_final_norm_g)
    v = dict(a_w_in=v_a_w_in, a_b_f=v_a_b_f, a_w_out=v_a_w_out, b_w_q=v_b_w_q, b_w_out=v_b_w_out, kv_norm_g=v_kv_norm_g,
             w_kv=v_w_kv, mix_norm_g=v_mix_norm_g, ffn_norm_g=v_ffn_norm_g, ffn_w_up=v_ffn_w_up, ffn_conv_w=v_ffn_conv_w,
             ffn_conv_b=v_ffn_conv_b, ffn_w_down=v_ffn_w_down, final_norm_g=v_final_norm_g)

    c_arr = lax.axis_index("c").astype(jnp.int32).reshape(1)
    k_arr = (2 * lax.axis_index("x") + lax.axis_index("y")).astype(jnp.int32).reshape(1)
    w16 = _pack_weights(w)
    gathered = _allgather_shards(w16, _place_own(w16, k_arr))
    p = _full_weights(gathered)
    cb = ffn_conv_b.reshape(2, 2, 1, FF)
    p.update(a_b_f=a_b_f, kv_norm_g=kv_norm_g, mix_norm_g=mix_norm_g, ffn_norm_g=ffn_norm_g, final_norm_g=final_norm_g,
             conv_b=cb)

    loss_part, grad_x, g = _local_step(x[0], loss_target[0], p)

    gflat = _shard_grads(g)
    pair = _pair_sum(gflat, _sibling_swap_half(gflat), c_arr)
    g_mine = _chip_sum(pair, _chip_exchange(pair), k_arr)
    g_other = _sibling_send(g_mine)
    small, loss = _unpack_small(_allreduce_small(_pack_small(g, loss_part)))

    big = [{}, {}, {}, {}]
    for name in _BIG:
        sg = _seg(name)
        res = _adamw_shard(_native3(w[name], name), _native3(m[name], name), _native3(v[name], name), g_mine, g_other,
                           c_arr, name)
        for store, t in zip(big, res):
            store[name] = t[:, :sg["R"], :].reshape(_SHARD_SHAPES[name])
    dws, mns, vns = _adamw(_pack_small(w), _pack_small(small), _pack_small(m), _pack_small(v), "adamw_small")
    sml = [small] + [_unpack_small(t)[0] for t in (dws, mns, vns)]
    outs = [loss, grad_x[None]]
    for b, s in zip(big, sml):
        outs += [b[n] if n in b else s[n] for n in _WEIGHTS]
    return tuple(outs)
```

```python
import functools
import math

import numpy as np
import jax
import jax.numpy as jnp
from jax import lax
from jax.experimental import pallas as pl
from jax.experimental.pallas import tpu as pltpu

F32 = jnp.float32
BF16 = jnp.bfloat16
MESH = pl.DeviceIdType.MESH

S = 4096
D = 1024
A_HEADS = 16
HEAD_DIM = 64
QKV_W = 3 * A_HEADS * HEAD_DIM
B_GROUPS = ((128, 1), (512, 4), (2048, 16))
B_HPG = 8
B_Q_W = 3 * B_HPG * HEAD_DIM
B_OUT_W = B_HPG * HEAD_DIM
B_KV_W = 2 * B_Q_W
B_WIN = 128
FF = 2816
RMS_EPS = 1e-6
SCALE = HEAD_DIM ** -0.5
N_CHIPS = 4

ADAM_LR, ADAM_B1, ADAM_B2, ADAM_EPS, ADAM_WD, ADAM_STEP = 0.001, 0.9, 0.999, 1e-08, 0.01, 10

V7X_VMEM_LIMIT = 48 * 1024 * 1024
LANES = 128
NEG_INF = float("-inf")

FLAT_W = LANES
_SEGS = (("ffn_w_down", 2, 704, 1024, 704), ("a_w_in", 1, 1024, 772, 1024), ("a_w_out", 1, 256, 1024, 256),
         ("b_w_q", 1, 1024, 384, 1024), ("b_w_out", 1, 512, 256, 512), ("w_kv", 1, 1024, 768, 1024),
         ("ffn_w_up", 2, 1024, 1408, 1024), ("ffn_conv_w", 1, 6, 1408, 16))
CONV_TERMS = 3


def _seg_rows(name, L, R, C, rpad):
    return (CONV_TERMS if name == "ffn_conv_w" else 1) * L * (-(-C // FLAT_W)) * rpad


FLAT_T = 2048
FLAT_ROWS = 57344
HALF_ROWS = FLAT_ROWS // 2
assert sum(_seg_rows(*s) for s in _SEGS) <= FLAT_ROWS and HALF_ROWS % FLAT_T == 0
SMALL_ROWS = 24


def _cparams(sem=None, **kw):
    return pltpu.CompilerParams(dimension_semantics=sem, vmem_limit_bytes=V7X_VMEM_LIMIT, **kw)


_DN = {"nn": (((1,), (0,)), ((), ())), "nt": (((1,), (1,)), ((), ())), "tn": (((0,), (0,)), ((), ()))}


def _mm(a, b, *, mode, tm, tn, tk, name, out_dtype=F32, res=None, a_split=0, b_split=0, o_split=0):
    if mode == "tn":
        K = a.shape[0]
        M = a.shape[1]
    else:
        M = a.shape[-2]
        K = a.shape[-1] * (2 if a_split else 1)
    if mode == "nt":
        N = b.shape[0]
    else:
        N = b.shape[-1] * (2 if b_split else 1)
    assert M % tm == 0 and N % tn == 0 and K % tk == 0, (name, M, N, K, tm, tn, tk)
    nk = K // tk

    if mode == "tn":
        a_spec = pl.BlockSpec((tk, tm), lambda i, j, k: (k, i))
    elif a_split:
        a_spec = pl.BlockSpec((None, tm, tk), lambda i, j, k: (k // a_split, i, k % a_split))
    else:
        a_spec = pl.BlockSpec((tm, tk), lambda i, j, k: (i, k))
    if mode == "nt":
        b_spec = pl.BlockSpec((tn, tk), lambda i, j, k: (j, k))
    elif b_split:
        b_spec = pl.BlockSpec((None, tk, tn), lambda i, j, k: (j // b_split, k, j % b_split))
    else:
        b_spec = pl.BlockSpec((tk, tn), lambda i, j, k: (k, j))
    if o_split:
        o_spec = pl.BlockSpec((None, tm, tn), lambda i, j, k: (j // o_split, i, j % o_split))
        out_shape = jax.ShapeDtypeStruct((2, M, N // 2), out_dtype)
    else:
        o_spec = pl.BlockSpec((tm, tn), lambda i, j, k: (i, j))
        out_shape = jax.ShapeDtypeStruct((M, N), out_dtype)
    in_specs = [a_spec, b_spec]
    args = [a, b]
    if res is not None:
        in_specs.append(pl.BlockSpec((tm, tn), lambda i, j, k: (i, j)))
        args.append(res)

    def body(*refs):
        if res is not None:
            a_ref, b_ref, r_ref, o_ref = refs[:4]
        else:
            a_ref, b_ref, o_ref = refs[:3]
            r_ref = None
        p = lax.dot_general(a_ref[...].astype(BF16), b_ref[...].astype(BF16), _DN[mode], preferred_element_type=F32)

        def finish(r):
            if r_ref is not None:
                r = r + r_ref[...]
            o_ref[...] = r.astype(out_dtype)

        if nk == 1:
            finish(p)
        else:
            acc = refs[-1]
            k = pl.program_id(2)

            @pl.when(k == 0)
            def _():
                acc[...] = p

            @pl.when(k > 0)
            def _():
                acc[...] += p

            @pl.when(k == nk - 1)
            def _():
                finish(acc[...])

    return pl.pallas_call(
        body, out_shape=out_shape, grid=(M // tm, N // tn, nk), in_specs=in_specs, out_specs=o_spec,
        scratch_shapes=[pltpu.VMEM((tm, tn), F32)] if nk > 1 else [],
        compiler_params=_cparams(("parallel", "parallel", "arbitrary")), name=name,
    )(*args)


NORM_ROWS = 256


def _rms_fwd(x, g, name):
    def body(x_ref, g_ref, o_ref):
        xv = x_ref[...]
        r = lax.rsqrt(jnp.mean(xv * xv, axis=-1, keepdims=True) + RMS_EPS)
        o_ref[...] = (xv * r * g_ref[...]).astype(BF16)

    row = pl.BlockSpec((NORM_ROWS, D), lambda i: (i, 0))
    return pl.pallas_call(
        body, out_shape=jax.ShapeDtypeStruct((S, D), BF16), grid=(S // NORM_ROWS,),
        in_specs=[row, pl.BlockSpec((1, D), lambda i: (0, 0))], out_specs=row,
        compiler_params=_cparams(("parallel",)), name=name,
    )(x, g.reshape(1, D))


def _rms_bwd(x, dres, pairs, name):
    n = len(pairs)

    def body(*refs):
        x_ref, dres_ref = refs[0], refs[1]
        g_refs = refs[2:2 + 2 * n:2]
        dh_refs = refs[3:3 + 2 * n:2]
        dx_ref, dxb_ref = refs[2 + 2 * n], refs[3 + 2 * n]
        dg_refs = refs[4 + 2 * n:]
        i = pl.program_id(0)
        xv = x_ref[...]
        r = lax.rsqrt(jnp.mean(xv * xv, axis=-1, keepdims=True) + RMS_EPS)
        y = xv * r
        dx = dres_ref[...]
        for g_ref, dh_ref, dg_ref in zip(g_refs, dh_refs, dg_refs):
            dh = dh_ref[...]
            dy = dh * g_ref[...]
            dx = dx + r * (dy - y * jnp.mean(dy * y, axis=-1, keepdims=True))
            part = jnp.sum(dh * y, axis=0, keepdims=True)

            @pl.when(i == 0)
            def _():
                dg_ref[...] = part

            @pl.when(i > 0)
            def _():
                dg_ref[...] += part

        dx_ref[...] = dx
        dxb_ref[...] = dx.astype(BF16)

    row = pl.BlockSpec((NORM_ROWS, D), lambda i: (i, 0))
    vec = pl.BlockSpec((1, D), lambda i: (0, 0))
    in_specs = [row, row]
    args = [x, dres]
    for g, dh in pairs:
        in_specs += [vec, row]
        args += [g.reshape(1, D), dh]
    outs = pl.pallas_call(
        body,
        out_shape=[jax.ShapeDtypeStruct((S, D), F32), jax.ShapeDtypeStruct((S, D), BF16)]
        + [jax.ShapeDtypeStruct((1, D), F32)] * n,
        grid=(S // NORM_ROWS,), in_specs=in_specs, out_specs=[row, row] + [vec] * n,
        compiler_params=_cparams(("arbitrary",)), name=name,
    )(*args)
    return outs[0], outs[1], list(outs[2:])


def _loss_head(x, g, target, name):
    def body(x_ref, g_ref, t_ref, loss_ref, dx_ref, dxb_ref, dg_ref):
        i = pl.program_id(0)
        xv = x_ref[...]
        gv = g_ref[...]
        r = lax.rsqrt(jnp.mean(xv * xv, axis=-1, keepdims=True) + RMS_EPS)
        y = xv * r
        err = y * gv - t_ref[...]
        lpart = jnp.broadcast_to(jnp.sum(err * err, keepdims=True) * (0.5 / D), (1, LANES))
        dh = err * (1.0 / D)
        dy = dh * gv
        dx = r * (dy - y * jnp.mean(dy * y, axis=-1, keepdims=True))
        part = jnp.sum(dh * y, axis=0, keepdims=True)

        @pl.when(i == 0)
        def _():
            dg_ref[...] = part
            loss_ref[...] = lpart

        @pl.when(i > 0)
        def _():
            dg_ref[...] += part
            loss_ref[...] += lpart

        dx_ref[...] = dx
        dxb_ref[...] = dx.astype(BF16)

    row = pl.BlockSpec((NORM_ROWS, D), lambda i: (i, 0))
    vec = pl.BlockSpec((1, D), lambda i: (0, 0))
    return pl.pallas_call(
        body,
        out_shape=[jax.ShapeDtypeStruct((1, LANES), F32), jax.ShapeDtypeStruct((S, D), F32),
                   jax.ShapeDtypeStruct((S, D), BF16), jax.ShapeDtypeStruct((1, D), F32)],
        grid=(S // NORM_ROWS,), in_specs=[row, vec, row],
        out_specs=[pl.BlockSpec((1, LANES), lambda i: (0, 0)), row, row, vec],
        compiler_params=_cparams(("arbitrary",)), name=name,
    )(x, g.reshape(1, D), target)


SCAN_ROWS = 256


def _split3(v):
    hi = v.astype(BF16)
    r1 = v - hi.astype(F32)
    mid = r1.astype(BF16)
    lo = (r1 - mid.astype(F32)).astype(BF16)
    return hi, mid, lo


def _tri_dot(tri, v):
    hi, mid, lo = _split3(v)
    dn = _DN["nn"]
    return (lax.dot_general(tri, hi, dn, preferred_element_type=F32)
            + lax.dot_general(tri, mid, dn, preferred_element_type=F32)
            + lax.dot_general(tri, lo, dn, preferred_element_type=F32))


def _log_sigmoid(z):
    return jnp.minimum(z, 0.0) - jnp.log(1.0 + jnp.exp(-jnp.abs(z)))


def _fgate_fwd(pf, bias, name):
    tri = jnp.tril(jnp.ones((SCAN_ROWS, SCAN_ROWS), F32)).astype(BF16)

    def body(pf_ref, b_ref, tri_ref, c_ref):
        carry = jnp.zeros((1, LANES), F32)
        for blk in range(S // SCAN_ROWS):
            rows = pl.ds(blk * SCAN_ROWS, SCAN_ROWS)
            lf = _log_sigmoid(pf_ref[rows, :] + b_ref[...])
            c_ref[rows, :] = _tri_dot(tri_ref[...], lf) + carry
            carry = c_ref[pl.ds(blk * SCAN_ROWS + SCAN_ROWS - 1, 1), :]

    return pl.pallas_call(
        body, out_shape=jax.ShapeDtypeStruct((S, LANES), F32),
        compiler_params=_cparams(), name=name,
    )(pf, bias, tri)


def _fgate_bwd(pf, bias, dc_key, dc_query, name):
    triu = jnp.triu(jnp.ones((SCAN_ROWS, SCAN_ROWS), F32)).astype(BF16)

    def body(pf_ref, b_ref, dck_ref, dcq_ref, tri_ref, dpf_ref, db_ref, dlf_ref):
        carry = jnp.zeros((1, LANES), F32)
        db = jnp.zeros((1, LANES), F32)
        lane = lax.broadcasted_iota(jnp.int32, (SCAN_ROWS, LANES), 1)
        for blk in reversed(range(S // SCAN_ROWS)):
            rows = pl.ds(blk * SCAN_ROWS, SCAN_ROWS)
            dc = dck_ref[rows, :] + dcq_ref[rows, :]
            dlf_ref[rows, :] = _tri_dot(tri_ref[...], dc) + carry
            carry = dlf_ref[pl.ds(blk * SCAN_ROWS, 1), :]
            z = pf_ref[rows, :] + b_ref[...]
            e = jnp.exp(-jnp.abs(z))
            sig_neg = jnp.where(z >= 0.0, e, 1.0) / (1.0 + e)
            dz = jnp.where(lane < A_HEADS, dlf_ref[rows, :] * sig_neg, 0.0)
            dpf_ref[rows, :] = dz.astype(BF16)
            db = db + jnp.sum(dz, axis=0, keepdims=True)
        db_ref[...] = db

    return pl.pallas_call(
        body, out_shape=[jax.ShapeDtypeStruct((S, LANES), BF16), jax.ShapeDtypeStruct((1, LANES), F32)],
        scratch_shapes=[pltpu.VMEM((S, LANES), F32)],
        compiler_params=_cparams(), name=name,
    )(pf, bias, dc_key, dc_query, triu)


FOX_T = 512


def _fox_fwd(q, k, v, ccol, crow, name):
    H = q.shape[0]
    T = FOX_T
    nq = S // T

    def body(q_ref, k_ref, v_ref, cc_ref, cr_ref, o_ref, lse_ref, m_sc, l_sc, acc_sc):
        i = pl.program_id(1)
        j = pl.program_id(2)

        @pl.when(j == 0)
        def _():
            m_sc[...] = jnp.full((T, 1), NEG_INF, F32)
            l_sc[...] = jnp.zeros((T, 1), F32)
            acc_sc[...] = jnp.zeros((T, HEAD_DIM), F32)

        def step(diagonal):
            s = lax.dot_general(q_ref[...], k_ref[...], _DN["nt"], preferred_element_type=F32) * SCALE
            s = s + (cc_ref[...] - cr_ref[...])
            if diagonal:
                row = lax.broadcasted_iota(jnp.int32, (T, T), 0)
                col = lax.broadcasted_iota(jnp.int32, (T, T), 1)
                s = jnp.where(row >= col, s, NEG_INF)
            m_prev = m_sc[...]
            m_new = jnp.maximum(m_prev, jnp.max(s, axis=1, keepdims=True))
            alpha = jnp.exp(m_prev - m_new)
            p = jnp.exp(s - m_new)
            l_sc[...] = alpha * l_sc[...] + jnp.sum(p, axis=1, keepdims=True)
            acc_sc[...] = alpha * acc_sc[...] + lax.dot_general(p.astype(BF16), v_ref[...], _DN["nn"],
                                                                preferred_element_type=F32)
            m_sc[...] = m_new

        @pl.when(j < i)
        def _():
            step(False)

        @pl.when(j == i)
        def _():
            step(True)
            o_ref[...] = (acc_sc[...] / l_sc[...]).astype(BF16)
            lse_ref[...] = m_sc[...] + jnp.log(l_sc[...])

    qs = pl.BlockSpec((None, T, HEAD_DIM), lambda h, i, j: (h, i, 0))
    ks = pl.BlockSpec((None, T, HEAD_DIM), lambda h, i, j: (h, jnp.minimum(i, j), 0))
    col = pl.BlockSpec((None, T, 1), lambda h, i, j: (h, i, 0))
    rowk = pl.BlockSpec((None, 1, T), lambda h, i, j: (h, 0, jnp.minimum(i, j)))
    return pl.pallas_call(
        body, out_shape=[jax.ShapeDtypeStruct((H, S, HEAD_DIM), BF16), jax.ShapeDtypeStruct((H, S, 1), F32)],
        grid=(H, nq, nq), in_specs=[qs, ks, ks, col, rowk], out_specs=[qs, col],
        scratch_shapes=[pltpu.VMEM((T, 1), F32), pltpu.VMEM((T, 1), F32), pltpu.VMEM((T, HEAD_DIM), F32)],
        compiler_params=_cparams(("parallel", "parallel", "arbitrary")), name=name,
    )(q, k, v, ccol, crow)


def _fox_bwd(q, k, v, do, lse_row, delta_row, cq_row, ck_col, name):
    H = q.shape[0]
    T = FOX_T
    nq = S // T

    def body(q_ref, k_ref, v_ref, do_ref, lse_ref, dl_ref, cq_ref, ck_ref, dq_ref, dk_ref, dv_ref, dc_ref, dcq_ref,
             dk_sc, dv_sc, dc_sc):
        j = pl.program_id(1)
        i = pl.program_id(2)

        @pl.when(jnp.logical_and(j == 0, i == 0))
        def _():
            dq_ref[...] = jnp.zeros((S, HEAD_DIM), F32)
            dcq_ref[...] = jnp.zeros((nq, 1, T), F32)

        @pl.when(i == j)
        def _():
            dk_sc[...] = jnp.zeros((T, HEAD_DIM), F32)
            dv_sc[...] = jnp.zeros((T, HEAD_DIM), F32)
            dc_sc[...] = jnp.zeros((T, 1), F32)

        def step(diagonal):
            qv = q_ref[...]
            kv = k_ref[...]
            dov = do_ref[...]
            st = lax.dot_general(kv, qv, _DN["nt"], preferred_element_type=F32) * SCALE
            st = st + (cq_ref[...] - ck_ref[...])
            if diagonal:
                row = lax.broadcasted_iota(jnp.int32, (T, T), 0)
                col = lax.broadcasted_iota(jnp.int32, (T, T), 1)
                st = jnp.where(col >= row, st, NEG_INF)
            pt = jnp.exp(st - lse_ref[...])
            dv_sc[...] += lax.dot_general(pt.astype(BF16), dov, _DN["nn"], preferred_element_type=F32)
            dpt = lax.dot_general(v_ref[...], dov, _DN["nt"], preferred_element_type=F32)
            dst = pt * (dpt - dl_ref[...])
            dc_sc[...] -= jnp.sum(dst, axis=1, keepdims=True)
            dcq_ref[i] += jnp.sum(dst, axis=0, keepdims=True)
            dsb = (dst * SCALE).astype(BF16)
            dk_sc[...] += lax.dot_general(dsb, qv, _DN["nn"], preferred_element_type=F32)
            rows = pl.ds(pl.multiple_of(i * T, T), T)
            dq_ref[rows, :] += lax.dot_general(dsb, kv, _DN["tn"], preferred_element_type=F32)

        @pl.when(i > j)
        def _():
            step(False)

        @pl.when(i == j)
        def _():
            step(True)

        @pl.when(i == nq - 1)
        def _():
            dk_ref[...] = dk_sc[...].astype(BF16)
            dv_ref[...] = dv_sc[...].astype(BF16)
            dc_ref[...] = dc_sc[...]

    qs = pl.BlockSpec((None, T, HEAD_DIM), lambda h, j, i: (h, jnp.maximum(i, j), 0))
    qrow = pl.BlockSpec((None, 1, T), lambda h, j, i: (h, 0, jnp.maximum(i, j)))
    ks = pl.BlockSpec((None, T, HEAD_DIM), lambda h, j, i: (h, j, 0))
    kcol = pl.BlockSpec((None, T, 1), lambda h, j, i: (h, j, 0))
    dqs = pl.BlockSpec((None, S, HEAD_DIM), lambda h, j, i: (h, 0, 0))
    dcqs = pl.BlockSpec((None, nq, 1, T), lambda h, j, i: (h, 0, 0, 0))
    return pl.pallas_call(
        body,
        out_shape=[jax.ShapeDtypeStruct((H, S, HEAD_DIM), F32), jax.ShapeDtypeStruct((H, S, HEAD_DIM), BF16),
                   jax.ShapeDtypeStruct((H, S, HEAD_DIM), BF16), jax.ShapeDtypeStruct((H, S, 1), F32),
                   jax.ShapeDtypeStruct((H, nq, 1, T), F32)],
        grid=(H, nq, nq), in_specs=[qs, ks, ks, qs, qrow, qrow, qrow, kcol], out_specs=[dqs, ks, ks, kcol, dcqs],
        scratch_shapes=[pltpu.VMEM((T, HEAD_DIM), F32), pltpu.VMEM((T, HEAD_DIM), F32), pltpu.VMEM((T, 1), F32)],
        compiler_params=_cparams(("parallel", "arbitrary", "arbitrary")), name=name,
    )(q, k, v, do, lse_row, delta_row, cq_row, ck_col)


def _rowdot(a, b, name):
    H = a.shape[0]
    T = 1024

    def body(a_ref, b_ref, o_ref):
        o_ref[...] = jnp.sum(a_ref[...].astype(F32) * b_ref[...].astype(F32), axis=-1, keepdims=True)

    blk = pl.BlockSpec((None, T, HEAD_DIM), lambda h, i: (h, i, 0))
    return pl.pallas_call(
        body, out_shape=jax.ShapeDtypeStruct((H, S, 1), F32), grid=(H, S // T), in_specs=[blk, blk],
        out_specs=pl.BlockSpec((None, T, 1), lambda h, i: (h, i, 0)),
        compiler_params=_cparams(("parallel", "parallel")), name=name,
    )(a, b)


def _first_head(shape):
    return lax.broadcasted_iota(jnp.int32, shape, len(shape) - 1) < HEAD_DIM


def _each_head(x, lo):
    zero = jnp.zeros_like(x)
    return jnp.where(lo, x, zero), jnp.where(lo, zero, x)


def _fox_pair_fwd(qkv, ccol, crow, name):
    T = FOX_T
    nq = S // T
    NP = A_HEADS // 2

    def body(q_ref, k_ref, v_ref, cc_ref, cr_ref, o_ref, lse_ref, m_sc, l_sc, acc_sc):
        i = pl.program_id(1)
        j = pl.program_id(2)
        lo = _first_head((T, LANES))

        @pl.when(j == 0)
        def _():
            m_sc[...] = jnp.full((2, T, 1), NEG_INF, F32)
            l_sc[...] = jnp.zeros((2, T, 1), F32)
            acc_sc[...] = jnp.zeros((T, LANES), F32)

        def step(diagonal):
            kv = k_ref[...]
            if diagonal:
                causal = lax.broadcasted_iota(jnp.int32, (T, T), 0) >= lax.broadcasted_iota(jnp.int32, (T, T), 1)
            pv, alphas = None, []
            for h, (qh, vh) in enumerate(zip(_each_head(q_ref[...], lo), _each_head(v_ref[...], lo))):
                s = lax.dot_general(qh, kv, _DN["nt"], preferred_element_type=F32) * SCALE + (cc_ref[h] - cr_ref[h])
                if diagonal:
                    s = jnp.where(causal, s, NEG_INF)
                m_prev = m_sc[h]
                m_new = jnp.maximum(m_prev, jnp.max(s, axis=1, keepdims=True))
                alpha = jnp.exp(m_prev - m_new)
                p = jnp.exp(s - m_new)
                l_sc[h] = alpha * l_sc[h] + jnp.sum(p, axis=1, keepdims=True)
                m_sc[h] = m_new
                d = lax.dot_general(p.astype(BF16), vh, _DN["nn"], preferred_element_type=F32)
                pv = d if pv is None else pv + d
                alphas.append(alpha)
            acc_sc[...] = jnp.where(lo, alphas[0], alphas[1]) * acc_sc[...] + pv

        @pl.when(j < i)
        def _():
            step(False)

        @pl.when(j == i)
        def _():
            step(True)
            o_ref[...] = (acc_sc[...] * jnp.where(lo, 1.0 / l_sc[0], 1.0 / l_sc[1])).astype(BF16)
            for h in range(2):
                lse_ref[h] = m_sc[h] + jnp.log(l_sc[h])

    qs = pl.BlockSpec((T, LANES), lambda p, i, j: (i, p))
    ks = pl.BlockSpec((T, LANES), lambda p, i, j: (jnp.minimum(i, j), NP + p))
    vs = pl.BlockSpec((T, LANES), lambda p, i, j: (jnp.minimum(i, j), 2 * NP + p))
    col = pl.BlockSpec((2, T, 1), lambda p, i, j: (p, i, 0))
    rowk = pl.BlockSpec((2, 1, T), lambda p, i, j: (p, 0, jnp.minimum(i, j)))
    return pl.pallas_call(
        body, out_shape=[jax.ShapeDtypeStruct((S, A_HEADS * HEAD_DIM), BF16), jax.ShapeDtypeStruct((A_HEADS, S, 1), F32)],
        grid=(NP, nq, nq), in_specs=[qs, ks, vs, col, rowk], out_specs=[qs, col],
        scratch_shapes=[pltpu.VMEM((2, T, 1), F32), pltpu.VMEM((2, T, 1), F32), pltpu.VMEM((T, LANES), F32)],
        compiler_params=_cparams(("parallel", "parallel", "arbitrary")), name=name,
    )(qkv, qkv, qkv, ccol, crow)


def _fox_pair_bwd(qkv, do, lse_row, delta_row, cq_row, ck_col, name):
    T = FOX_T
    nq = S // T
    NP = A_HEADS // 2

    def body(q_ref, k_ref, v_ref, do_ref, lse_ref, dl_ref, cq_ref, ck_ref, dq_ref, dk_ref, dv_ref, dc_ref, dcq_ref,
             dq_sc, dk_sc, dv_sc, dc_sc):
        j = pl.program_id(1)
        i = pl.program_id(2)
        lo = _first_head((T, LANES))

        @pl.when(jnp.logical_and(j == 0, i == 0))
        def _():
            dq_sc[...] = jnp.zeros((S, LANES), F32)
            dcq_ref[...] = jnp.zeros((2, nq, 1, T), F32)

        @pl.when(i == j)
        def _():
            dk_sc[...] = jnp.zeros((T, LANES), F32)
            dv_sc[...] = jnp.zeros((T, LANES), F32)
            dc_sc[...] = jnp.zeros((2, T, 1), F32)

        def step(diagonal):
            qv = q_ref[...]
            dov = do_ref[...].astype(BF16)
            if diagonal:
                causal = lax.broadcasted_iota(jnp.int32, (T, T), 1) >= lax.broadcasted_iota(jnp.int32, (T, T), 0)
            dv = dk = dq = None
            for h, (kh, vh, qh, doh) in enumerate(zip(_each_head(k_ref[...], lo), _each_head(v_ref[...], lo),
                                                      _each_head(qv, lo), _each_head(dov, lo))):
                st = lax.dot_general(kh, qv, _DN["nt"], preferred_element_type=F32) * SCALE + (cq_ref[h] - ck_ref[h])
                if diagonal:
                    st = jnp.where(causal, st, NEG_INF)
                pt = jnp.exp(st - lse_ref[h])
                d = lax.dot_general(pt.astype(BF16), doh, _DN["nn"], preferred_element_type=F32)
                dv = d if dv is None else dv + d
                dpt = lax.dot_general(vh, dov, _DN["nt"], preferred_element_type=F32)
                dst = pt * (dpt - dl_ref[h])
                dc_sc[h] -= jnp.sum(dst, axis=1, keepdims=True)
                dcq_ref[h, i] += jnp.sum(dst, axis=0, keepdims=True)
                dsb = (dst * SCALE).astype(BF16)
                d = lax.dot_general(dsb, qh, _DN["nn"], preferred_element_type=F32)
                dk = d if dk is None else dk + d
                d = lax.dot_general(dsb, kh, _DN["tn"], preferred_element_type=F32)
                dq = d if dq is None else dq + d
            dv_sc[...] += dv
            dk_sc[...] += dk
            rows = pl.ds(pl.multiple_of(i * T, T), T)
            dq_sc[rows, :] += dq

        @pl.when(i > j)
        def _():
            step(False)

        @pl.when(i == j)
        def _():
            step(True)

        @pl.when(i == nq - 1)
        def _():
            dk_ref[...] = dk_sc[...].astype(BF16)
            dv_ref[...] = dv_sc[...].astype(BF16)
            dc_ref[...] = dc_sc[...]

        @pl.when(jnp.logical_and(j == nq - 1, i == nq - 1))
        def _():
            dq_ref[...] = dq_sc[...].astype(BF16)

    qs = pl.BlockSpec((T, LANES), lambda p, j, i: (jnp.maximum(i, j), p))
    qrow = pl.BlockSpec((2, 1, T), lambda p, j, i: (p, 0, jnp.maximum(i, j)))
    ks = pl.BlockSpec((T, LANES), lambda p, j, i: (j, NP + p))
    vs = pl.BlockSpec((T, LANES), lambda p, j, i: (j, 2 * NP + p))
    kout = pl.BlockSpec((T, LANES), lambda p, j, i: (j, p))
    kcol = pl.BlockSpec((2, T, 1), lambda p, j, i: (p, j, 0))
    dqs = pl.BlockSpec((S, LANES), lambda p, j, i: (0, p))
    dcqs = pl.BlockSpec((2, nq, 1, T), lambda p, j, i: (p, 0, 0, 0))
    wide = jax.ShapeDtypeStruct((S, A_HEADS * HEAD_DIM), BF16)
    return pl.pallas_call(
        body,
        out_shape=[wide, wide, wide, jax.ShapeDtypeStruct((A_HEADS, S, 1), F32),
                   jax.ShapeDtypeStruct((A_HEADS, nq, 1, T), F32)],
        grid=(NP, nq, nq), in_specs=[qs, ks, vs, qs, qrow, qrow, qrow, kcol], out_specs=[dqs, kout, kout, kcol, dcqs],
        scratch_shapes=[pltpu.VMEM((S, LANES), F32), pltpu.VMEM((T, LANES), F32), pltpu.VMEM((T, LANES), F32),
                        pltpu.VMEM((2, T, 1), F32)],
        compiler_params=_cparams(("parallel", "arbitrary", "arbitrary")), name=name,
    )(qkv, qkv, qkv, do, lse_row, delta_row, cq_row, ck_col)


def _pair_rowdot(a, b, name):
    n = a.shape[1] // HEAD_DIM
    T = 1024

    def body(a_ref, b_ref, o_ref):
        prod = a_ref[...].astype(F32) * b_ref[...].astype(F32)
        lo = _first_head(prod.shape)
        o_ref[0] = jnp.sum(jnp.where(lo, prod, 0.0), axis=1, keepdims=True)
        o_ref[1] = jnp.sum(jnp.where(lo, 0.0, prod), axis=1, keepdims=True)

    blk = pl.BlockSpec((T, LANES), lambda p, i: (i, p))
    return pl.pallas_call(
        body, out_shape=jax.ShapeDtypeStruct((n, S, 1), F32), grid=(n // 2, S // T), in_specs=[blk, blk],
        out_specs=pl.BlockSpec((2, T, 1), lambda p, i: (p, i, 0)),
        compiler_params=_cparams(("parallel", "parallel")), name=name,
    )(a, b)


W = B_WIN
N_HG = 3 * B_HPG
N_BLK = S // W


def _dil_tables():
    slopes = np.exp2((-8.0 * np.arange(1, N_HG + 1, dtype=np.float32) / N_HG).astype(np.float32)).astype(np.float32)
    dil = np.repeat(np.array([d for _, d in B_GROUPS], np.float32), B_HPG)
    coef = (slopes * dil).astype(np.float32)
    nbs = np.repeat(np.array([S // d // W for _, d in B_GROUPS], np.int32), B_HPG)
    return jnp.asarray(coef), jnp.asarray(nbs)


DIL_SUB = 8
DIL_ROWS = DIL_SUB * W
DIL_STEPS = S // DIL_ROWS


def _dil_bias(coef, transposed):
    row = lax.broadcasted_iota(jnp.int32, (W, 2 * W), 0)
    col = lax.broadcasted_iota(jnp.int32, (W, 2 * W), 1)
    dist = (col - row) if transposed else (row + W - col)
    valid = jnp.logical_and(dist >= 0, dist <= W)
    return jnp.where(valid, -coef * dist.astype(F32), NEG_INF), col


def _dil_specs():
    blk = pl.BlockSpec((None, DIL_ROWS, HEAD_DIM), lambda h, n: (h, n, 0))
    prev = pl.BlockSpec((None, W, HEAD_DIM), lambda h, n: (h, jnp.maximum(n * DIL_SUB - 1, 0), 0))
    nxt = pl.BlockSpec((None, W, HEAD_DIM), lambda h, n: (h, jnp.minimum((n + 1) * DIL_SUB, N_BLK - 1), 0))
    col = pl.BlockSpec((None, DIL_ROWS, 1), lambda h, n: (h, n, 0))
    row = pl.BlockSpec((None, 1, DIL_ROWS), lambda h, n: (h, 0, n))
    rnxt = pl.BlockSpec((None, 1, W), lambda h, n: (h, 0, jnp.minimum((n + 1) * DIL_SUB, N_BLK - 1)))
    smem = pl.BlockSpec(memory_space=pltpu.SMEM)
    return blk, prev, nxt, col, row, rnxt, smem


def _dil_fwd(q, k, v, name):
    coef_t, nbs_t = _dil_tables()

    def body(coef_ref, nbs_ref, q_ref, kh_ref, k_ref, vh_ref, v_ref, o_ref, lse_ref, kf, vf):
        hg = pl.program_id(0)
        n = pl.program_id(1)
        nbs = nbs_ref[hg]
        kf[0:W, :] = kh_ref[...]
        kf[W:, :] = k_ref[...]
        vf[0:W, :] = vh_ref[...]
        vf[W:, :] = v_ref[...]
        bias, col = _dil_bias(coef_ref[hg], False)
        for b in range(DIL_SUB):
            first = lax.rem(n * DIL_SUB + b, nbs) == 0
            rows = slice(b * W, (b + 1) * W)
            both = slice(b * W, (b + 2) * W)
            s = lax.dot_general(q_ref[rows, :], kf[both, :], _DN["nt"], preferred_element_type=F32) * SCALE + bias
            s = jnp.where(jnp.logical_and(first, col < W), NEG_INF, s)
            m = jnp.max(s, axis=1, keepdims=True)
            p = jnp.exp(s - m)
            l = jnp.sum(p, axis=1, keepdims=True)
            acc = lax.dot_general(p.astype(BF16), vf[both, :], _DN["nn"], preferred_element_type=F32)
            o_ref[rows, :] = acc / l
            lse_ref[rows, :] = m + jnp.log(l)

    blk, prev, _, col, _, _, smem = _dil_specs()
    return pl.pallas_call(
        body, out_shape=[jax.ShapeDtypeStruct((N_HG, S, HEAD_DIM), F32), jax.ShapeDtypeStruct((N_HG, S, 1), F32)],
        grid=(N_HG, DIL_STEPS), in_specs=[smem, smem, blk, prev, blk, prev, blk], out_specs=[blk, col],
        scratch_shapes=[pltpu.VMEM((DIL_ROWS + W, HEAD_DIM), BF16)] * 2,
        compiler_params=_cparams(("parallel", "parallel")), name=name,
    )(coef_t, nbs_t, q, k, k, v, v)


def _dil_merge(o, lse, name):
    T = 1024

    def body(o_ref, lse_ref, om_ref, omb_ref, l_ref):
        l0, l1, l2 = lse_ref[0], lse_ref[1], lse_ref[2]
        m = jnp.maximum(jnp.maximum(l0, l1), l2)
        e0, e1, e2 = jnp.exp(l0 - m), jnp.exp(l1 - m), jnp.exp(l2 - m)
        den = e0 + e1 + e2
        om = (e0 / den) * o_ref[0] + (e1 / den) * o_ref[1] + (e2 / den) * o_ref[2]
        om_ref[...] = om
        omb_ref[...] = om.astype(BF16)
        l_ref[...] = m + jnp.log(den)

    ob = pl.BlockSpec((None, T, HEAD_DIM), lambda h, i: (h, i, 0))
    lb = pl.BlockSpec((None, T, 1), lambda h, i: (h, i, 0))
    return pl.pallas_call(
        body,
        out_shape=[jax.ShapeDtypeStruct((B_HPG, S, HEAD_DIM), F32), jax.ShapeDtypeStruct((B_HPG, S, HEAD_DIM), BF16),
                   jax.ShapeDtypeStruct((B_HPG, S, 1), F32)],
        grid=(B_HPG, S // T),
        in_specs=[pl.BlockSpec((3, None, T, HEAD_DIM), lambda h, i: (0, h, i, 0)),
                  pl.BlockSpec((3, None, T, 1), lambda h, i: (0, h, i, 0))],
        out_specs=[ob, ob, lb], compiler_params=_cparams(("parallel", "parallel")), name=name,
    )(o, lse)


def _dil_bwd_dq(q, k, v, do, lcol, dcol, name):
    coef_t, nbs_t = _dil_tables()

    def body(coef_ref, nbs_ref, q_ref, kh_ref, k_ref, vh_ref, v_ref, do_ref, l_ref, d_ref, dq_ref, kf, vf):
        hg = pl.program_id(0)
        n = pl.program_id(1)
        nbs = nbs_ref[hg]
        kf[0:W, :] = kh_ref[...]
        kf[W:, :] = k_ref[...]
        vf[0:W, :] = vh_ref[...]
        vf[W:, :] = v_ref[...]
        bias, col = _dil_bias(coef_ref[hg], False)
        for b in range(DIL_SUB):
            first = lax.rem(n * DIL_SUB + b, nbs) == 0
            rows = slice(b * W, (b + 1) * W)
            both = slice(b * W, (b + 2) * W)
            kk = kf[both, :]
            s = lax.dot_general(q_ref[rows, :], kk, _DN["nt"], preferred_element_type=F32) * SCALE + bias
            s = jnp.where(jnp.logical_and(first, col < W), NEG_INF, s)
            p = jnp.exp(s - l_ref[rows, :])
            dp = lax.dot_general(do_ref[rows, :], vf[both, :], _DN["nt"], preferred_element_type=F32)
            ds = (p * (dp - d_ref[rows, :]) * SCALE).astype(BF16)
            dq_ref[rows, :] = lax.dot_general(ds, kk, _DN["nn"], preferred_element_type=F32).astype(BF16)

    blk, prev, _, col, _, _, smem = _dil_specs()
    return pl.pallas_call(
        body, out_shape=jax.ShapeDtypeStruct((N_HG, S, HEAD_DIM), BF16), grid=(N_HG, DIL_STEPS),
        in_specs=[smem, smem, blk, prev, blk, prev, blk, blk, col, col], out_specs=blk,
        scratch_shapes=[pltpu.VMEM((DIL_ROWS + W, HEAD_DIM), BF16)] * 2,
        compiler_params=_cparams(("parallel", "parallel")), name=name,
    )(coef_t, nbs_t, q, k, k, v, v, do, lcol, dcol)


def _dil_bwd_dkv(q, k, v, do, lrow, drow, name):
    coef_t, nbs_t = _dil_tables()

    def body(coef_ref, nbs_ref, k_ref, v_ref, q_ref, qn_ref, do_ref, don_ref, l_ref, ln_ref, d_ref, dn_ref,
             dk_ref, dv_ref, qf, dof, lf, df):
        hg = pl.program_id(0)
        n = pl.program_id(1)
        nbs = nbs_ref[hg]
        qf[0:DIL_ROWS, :] = q_ref[...]
        qf[DIL_ROWS:, :] = qn_ref[...]
        dof[0:DIL_ROWS, :] = do_ref[...]
        dof[DIL_ROWS:, :] = don_ref[...]
        lf[:, 0:DIL_ROWS] = l_ref[...]
        lf[:, DIL_ROWS:] = ln_ref[...]
        df[:, 0:DIL_ROWS] = d_ref[...]
        df[:, DIL_ROWS:] = dn_ref[...]
        bias, col = _dil_bias(coef_ref[hg], True)
        for b in range(DIL_SUB):
            no_next = lax.rem(n * DIL_SUB + b + 1, nbs) == 0
            rows = slice(b * W, (b + 1) * W)
            both = slice(b * W, (b + 2) * W)
            qq = qf[both, :]
            dd = dof[both, :]
            st = lax.dot_general(k_ref[rows, :], qq, _DN["nt"], preferred_element_type=F32) * SCALE + bias
            st = jnp.where(jnp.logical_and(no_next, col >= W), NEG_INF, st)
            pt = jnp.exp(st - lf[:, both])
            dv_ref[rows, :] = lax.dot_general(pt.astype(BF16), dd, _DN["nn"], preferred_element_type=F32).astype(BF16)
            dpt = lax.dot_general(v_ref[rows, :], dd, _DN["nt"], preferred_element_type=F32)
            dst = (pt * (dpt - df[:, both]) * SCALE).astype(BF16)
            dk_ref[rows, :] = lax.dot_general(dst, qq, _DN["nn"], preferred_element_type=F32).astype(BF16)

    blk, _, nxt, _, row, rnxt, smem = _dil_specs()
    return pl.pallas_call(
        body, out_shape=[jax.ShapeDtypeStruct((N_HG, S, HEAD_DIM), BF16)] * 2, grid=(N_HG, DIL_STEPS),
        in_specs=[smem, smem, blk, blk, blk, nxt, blk, nxt, row, rnxt, row, rnxt], out_specs=[blk, blk],
        scratch_shapes=[pltpu.VMEM((DIL_ROWS + W, HEAD_DIM), BF16)] * 2 + [pltpu.VMEM((1, DIL_ROWS + W), F32)] * 2,
        compiler_params=_cparams(("parallel", "parallel")), name=name,
    )(coef_t, nbs_t, k, v, q, q, do, do, lrow, lrow, drow, drow)


N_HP = N_HG // 2


def _dil_pair_specs():
    prev_blk = lambda n: jnp.maximum(n * DIL_SUB - 1, 0)
    next_blk = lambda n: jnp.minimum((n + 1) * DIL_SUB, N_BLK - 1)
    return dict(
        q=pl.BlockSpec((DIL_ROWS, LANES), lambda h, n: (n, h)),
        q_next=pl.BlockSpec((W, LANES), lambda h, n: (next_blk(n), h)),
        k=pl.BlockSpec((DIL_ROWS, LANES), lambda h, n: (n, h)),
        k_prev=pl.BlockSpec((W, LANES), lambda h, n: (prev_blk(n), h)),
        v=pl.BlockSpec((DIL_ROWS, LANES), lambda h, n: (n, N_HP + h)),
        v_prev=pl.BlockSpec((W, LANES), lambda h, n: (prev_blk(n), N_HP + h)),
        col=pl.BlockSpec((2, DIL_ROWS, 1), lambda h, n: (h, n, 0)),
        row=pl.BlockSpec((2, 1, DIL_ROWS), lambda h, n: (h, 0, n)),
        row_next=pl.BlockSpec((2, 1, W), lambda h, n: (h, 0, next_blk(n))),
        smem=pl.BlockSpec(memory_space=pltpu.SMEM))


def _dil_pair_fwd(q, kv, name):
    coef_t, nbs_t = _dil_tables()

    def body(coef_ref, nbs_ref, q_ref, kh_ref, k_ref, vh_ref, v_ref, o_ref, lse_ref, kf, vf):
        hp = pl.program_id(0)
        n = pl.program_id(1)
        nbs = nbs_ref[2 * hp]
        kf[0:W, :] = kh_ref[...]
        kf[W:, :] = k_ref[...]
        vf[0:W, :] = vh_ref[...]
        vf[W:, :] = v_ref[...]
        biases = [_dil_bias(coef_ref[2 * hp + h], False) for h in range(2)]
        col = biases[0][1]
        lo = _first_head((W, LANES))
        lo2 = _first_head((2 * W, LANES))
        for b in range(DIL_SUB):
            first = lax.rem(n * DIL_SUB + b, nbs) == 0
            rows = slice(b * W, (b + 1) * W)
            both = slice(b * W, (b + 2) * W)
            qv = q_ref[rows, :]
            acc, inv = None, []
            for h, (kh, vh) in enumerate(zip(_each_head(kf[both, :], lo2), _each_head(vf[both, :], lo2))):
                s = lax.dot_general(qv, kh, _DN["nt"], preferred_element_type=F32) * SCALE + biases[h][0]
                s = jnp.where(jnp.logical_and(first, col < W), NEG_INF, s)
                m = jnp.max(s, axis=1, keepdims=True)
                p = jnp.exp(s - m)
                l = jnp.sum(p, axis=1, keepdims=True)
                d = lax.dot_general(p.astype(BF16), vh, _DN["nn"], preferred_element_type=F32)
                acc = d if acc is None else acc + d
                inv.append(1.0 / l)
                lse_ref[h, rows, :] = m + jnp.log(l)
            o_ref[rows, :] = acc * jnp.where(lo, inv[0], inv[1])

    sp = _dil_pair_specs()
    return pl.pallas_call(
        body, out_shape=[jax.ShapeDtypeStruct((S, B_Q_W), F32), jax.ShapeDtypeStruct((N_HG, S, 1), F32)],
        grid=(N_HP, DIL_STEPS), in_specs=[sp["smem"], sp["smem"], sp["q"], sp["k_prev"], sp["k"], sp["v_prev"], sp["v"]],
        out_specs=[sp["q"], sp["col"]], scratch_shapes=[pltpu.VMEM((DIL_ROWS + W, LANES), BF16)] * 2,
        compiler_params=_cparams(("parallel", "parallel")), name=name,
    )(coef_t, nbs_t, q, kv, kv, kv, kv)


def _dil_pair_merge(o, lse, name):
    T = 1024
    NPG = B_HPG // 2

    def body(o0_ref, o1_ref, o2_ref, l0_ref, l1_ref, l2_ref, om_ref, omb_ref, l_ref):
        lo = _first_head((T, LANES))
        weights = []
        for h in range(2):
            l0, l1, l2 = l0_ref[h], l1_ref[h], l2_ref[h]
            m = jnp.maximum(jnp.maximum(l0, l1), l2)
            e0, e1, e2 = jnp.exp(l0 - m), jnp.exp(l1 - m), jnp.exp(l2 - m)
            den = e0 + e1 + e2
            weights.append((e0 / den, e1 / den, e2 / den))
            l_ref[h] = m + jnp.log(den)
        om = (jnp.where(lo, weights[0][0], weights[1][0]) * o0_ref[...]
              + jnp.where(lo, weights[0][1], weights[1][1]) * o1_ref[...]
              + jnp.where(lo, weights[0][2], weights[1][2]) * o2_ref[...])
        om_ref[...] = om
        omb_ref[...] = om.astype(BF16)

    og = [pl.BlockSpec((T, LANES), lambda p, i, g=g: (i, g * NPG + p)) for g in range(3)]
    lg = [pl.BlockSpec((2, T, 1), lambda p, i, g=g: (g * NPG + p, i, 0)) for g in range(3)]
    ob = pl.BlockSpec((T, LANES), lambda p, i: (i, p))
    return pl.pallas_call(
        body,
        out_shape=[jax.ShapeDtypeStruct((S, B_OUT_W), F32), jax.ShapeDtypeStruct((S, B_OUT_W), BF16),
                   jax.ShapeDtypeStruct((B_HPG, S, 1), F32)],
        grid=(NPG, S // T), in_specs=og + lg, out_specs=[ob, ob, pl.BlockSpec((2, T, 1), lambda p, i: (p, i, 0))],
        compiler_params=_cparams(("parallel", "parallel")), name=name,
    )(o, o, o, lse, lse, lse)


def _dil_pair_dq(q, kv, do, lcol, dcol, name):
    coef_t, nbs_t = _dil_tables()

    def body(coef_ref, nbs_ref, q_ref, kh_ref, k_ref, vh_ref, v_ref, do_ref, l_ref, d_ref, dq_ref, kf, vf):
        hp = pl.program_id(0)
        n = pl.program_id(1)
        nbs = nbs_ref[2 * hp]
        kf[0:W, :] = kh_ref[...]
        kf[W:, :] = k_ref[...]
        vf[0:W, :] = vh_ref[...]
        vf[W:, :] = v_ref[...]
        biases = [_dil_bias(coef_ref[2 * hp + h], False) for h in range(2)]
        col = biases[0][1]
        lo2 = _first_head((2 * W, LANES))
        for b in range(DIL_SUB):
            first = lax.rem(n * DIL_SUB + b, nbs) == 0
            rows = slice(b * W, (b + 1) * W)
            both = slice(b * W, (b + 2) * W)
            qv = q_ref[rows, :]
            dov = do_ref[rows, :]
            acc = None
            for h, (kh, vh) in enumerate(zip(_each_head(kf[both, :], lo2), _each_head(vf[both, :], lo2))):
                s = lax.dot_general(qv, kh, _DN["nt"], preferred_element_type=F32) * SCALE + biases[h][0]
                s = jnp.where(jnp.logical_and(first, col < W), NEG_INF, s)
                p = jnp.exp(s - l_ref[h, rows, :])
                dp = lax.dot_general(dov, vh, _DN["nt"], preferred_element_type=F32)
                ds = (p * (dp - d_ref[h, rows, :]) * SCALE).astype(BF16)
                d = lax.dot_general(ds, kh, _DN["nn"], preferred_element_type=F32)
                acc = d if acc is None else acc + d
            dq_ref[rows, :] = acc.astype(BF16)

    sp = _dil_pair_specs()
    return pl.pallas_call(
        body, out_shape=jax.ShapeDtypeStruct((S, B_Q_W), BF16), grid=(N_HP, DIL_STEPS),
        in_specs=[sp["smem"], sp["smem"], sp["q"], sp["k_prev"], sp["k"], sp["v_prev"], sp["v"], sp["q"], sp["col"],
                  sp["col"]],
        out_specs=sp["q"], scratch_shapes=[pltpu.VMEM((DIL_ROWS + W, LANES), BF16)] * 2,
        compiler_params=_cparams(("parallel", "parallel")), name=name,
    )(coef_t, nbs_t, q, kv, kv, kv, kv, do, lcol, dcol)


def _dil_pair_dkv(q, kv, do, lrow, drow, name):
    coef_t, nbs_t = _dil_tables()

    def body(coef_ref, nbs_ref, k_ref, v_ref, q_ref, qn_ref, do_ref, don_ref, l_ref, ln_ref, d_ref, dn_ref,
             dk_ref, dv_ref, qf, dof, lf, df):
        hp = pl.program_id(0)
        n = pl.program_id(1)
        nbs = nbs_ref[2 * hp]
        qf[0:DIL_ROWS, :] = q_ref[...]
        qf[DIL_ROWS:, :] = qn_ref[...]
        dof[0:DIL_ROWS, :] = do_ref[...]
        dof[DIL_ROWS:, :] = don_ref[...]
        lf[:, :, 0:DIL_ROWS] = l_ref[...]
        lf[:, :, DIL_ROWS:] = ln_ref[...]
        df[:, :, 0:DIL_ROWS] = d_ref[...]
        df[:, :, DIL_ROWS:] = dn_ref[...]
        biases = [_dil_bias(coef_ref[2 * hp + h], True) for h in range(2)]
        col = biases[0][1]
        lo = _first_head((W, LANES))
        lo2 = _first_head((2 * W, LANES))
        for b in range(DIL_SUB):
            no_next = lax.rem(n * DIL_SUB + b + 1, nbs) == 0
            rows = slice(b * W, (b + 1) * W)
            both = slice(b * W, (b + 2) * W)
            dd = dof[both, :]
            dk = dv = None
            for h, (kh, vh, qh, ddh) in enumerate(zip(_each_head(k_ref[rows, :], lo), _each_head(v_ref[rows, :], lo),
                                                      _each_head(qf[both, :], lo2), _each_head(dd, lo2))):
                st = lax.dot_general(kh, qh, _DN["nt"], preferred_element_type=F32) * SCALE + biases[h][0]
                st = jnp.where(jnp.logical_and(no_next, col >= W), NEG_INF, st)
                pt = jnp.exp(st - lf[h, :, both])
                d = lax.dot_general(pt.astype(BF16), ddh, _DN["nn"], preferred_element_type=F32)
                dv = d if dv is None else dv + d
                dpt = lax.dot_general(vh, dd, _DN["nt"], preferred_element_type=F32)
                dst = (pt * (dpt - df[h, :, both]) * SCALE).astype(BF16)
                d = lax.dot_general(dst, qh, _DN["nn"], preferred_element_type=F32)
                dk = d if dk is None else dk + d
            dk_ref[rows, :] = dk.astype(BF16)
            dv_ref[rows, :] = dv.astype(BF16)

    sp = _dil_pair_specs()
    wide = jax.ShapeDtypeStruct((S, B_Q_W), BF16)
    return pl.pallas_call(
        body, out_shape=[wide, wide], grid=(N_HP, DIL_STEPS),
        in_specs=[sp["smem"], sp["smem"], sp["k"], sp["v"], sp["q"], sp["q_next"], sp["q"], sp["q_next"], sp["row"],
                  sp["row_next"], sp["row"], sp["row_next"]],
        out_specs=[sp["q"], sp["q"]],
        scratch_shapes=[pltpu.VMEM((DIL_ROWS + W, LANES), BF16)] * 2 + [pltpu.VMEM((2, 1, DIL_ROWS + W), F32)] * 2,
        compiler_params=_cparams(("parallel", "parallel")), name=name,
    )(coef_t, nbs_t, kv, kv, q, q, do, do, lrow, lrow, drow, drow)


FFN_ROWS = 512
FFN_COLS = 256
HALO = 8


def _shifted(u, halo, back):
    T = u.shape[0]
    rows = lax.broadcasted_iota(jnp.int32, u.shape, 0)
    if back:
        s1 = jnp.where(rows == 0, halo[HALO - 1:HALO, :], pltpu.roll(u, 1, 0))
        s2 = jnp.where(rows == 0, halo[HALO - 2:HALO - 1, :],
                       jnp.where(rows == 1, halo[HALO - 1:HALO, :], pltpu.roll(u, 2, 0)))
    else:
        s1 = jnp.where(rows == T - 1, halo[0:1, :], pltpu.roll(u, T - 1, 0))
        s2 = jnp.where(rows == T - 1, halo[1:2, :],
                       jnp.where(rows == T - 2, halo[0:1, :], pltpu.roll(u, T - 2, 0)))
    return s1, s2


def _conv_parts(u_ref, h_ref, w_ref, b_ref, first):
    out = []
    for p in range(2):
        u = u_ref[p]
        halo = jnp.where(first, 0.0, h_ref[p])
        u1, u2 = _shifted(u, halo, True)
        w = w_ref[p]
        out.append((w[0:1, :] * u2 + w[1:2, :] * u1 + w[2:3, :] * u + b_ref[p], u1, u2, u))
    return out


def _ffn_specs():
    T, C = FFN_ROWS, FFN_COLS
    blk = pl.BlockSpec((2, T, C), lambda j, i: (0, i, j))
    prev = pl.BlockSpec((2, HALO, C), lambda j, i: (0, jnp.maximum(i * (T // HALO) - 1, 0), j))
    nxt = pl.BlockSpec((2, HALO, C), lambda j, i: (0, jnp.minimum((i + 1) * (T // HALO), S // HALO - 1), j))
    wsp = pl.BlockSpec((2, 3, C), lambda j, i: (0, 0, j))
    bsp = pl.BlockSpec((2, 1, C), lambda j, i: (0, 0, j))
    one = pl.BlockSpec((T, C), lambda j, i: (i, j))
    return blk, prev, nxt, wsp, bsp, one


def _ffn_act_fwd(u, w, b, name):
    blk, prev, _, wsp, bsp, one = _ffn_specs()

    def body(u_ref, h_ref, w_ref, b_ref, o_ref):
        (a, _, _, _), (g, _, _, _) = _conv_parts(u_ref, h_ref, w_ref, b_ref, pl.program_id(1) == 0)
        o_ref[...] = (g / (1.0 + jnp.exp(-g)) * a).astype(BF16)

    return pl.pallas_call(
        body, out_shape=jax.ShapeDtypeStruct((S, FF), BF16), grid=(FF // FFN_COLS, S // FFN_ROWS),
        in_specs=[blk, prev, wsp, bsp], out_specs=one,
        compiler_params=_cparams(("parallel", "parallel")), name=name,
    )(u, u, w, b)


def _ffn_act_bwd(u, dact, w, b, name):
    blk, prev, _, wsp, bsp, one = _ffn_specs()

    def body(u_ref, h_ref, da_ref, w_ref, b_ref, duc_ref, dwb_ref):
        i = pl.program_id(1)
        (a, a1, a2, a0), (g, g1, g2, g0) = _conv_parts(u_ref, h_ref, w_ref, b_ref, i == 0)
        dact_v = da_ref[...]
        sg = 1.0 / (1.0 + jnp.exp(-g))
        d_a = dact_v * (g * sg)
        d_g = dact_v * a * (sg * (1.0 + g * (1.0 - sg)))
        duc_ref[0] = d_a
        duc_ref[1] = d_g

        @pl.when(i == 0)
        def _():
            dwb_ref[...] = jnp.zeros(dwb_ref.shape, F32)

        for p, (d, s2, s1, s0) in enumerate(((d_a, a2, a1, a0), (d_g, g2, g1, g0))):
            dwb_ref[p, 0:1, :] += jnp.sum(d * s2, axis=0, keepdims=True)
            dwb_ref[p, 1:2, :] += jnp.sum(d * s1, axis=0, keepdims=True)
            dwb_ref[p, 2:3, :] += jnp.sum(d * s0, axis=0, keepdims=True)
            dwb_ref[p, 3:4, :] += jnp.sum(d, axis=0, keepdims=True)

    return pl.pallas_call(
        body, out_shape=[jax.ShapeDtypeStruct((2, S, FF), F32), jax.ShapeDtypeStruct((2, 8, FF), F32)],
        grid=(FF // FFN_COLS, S // FFN_ROWS), in_specs=[blk, prev, one, wsp, bsp],
        out_specs=[blk, pl.BlockSpec((2, 8, FFN_COLS), lambda j, i: (0, 0, j))],
        compiler_params=_cparams(("parallel", "arbitrary")), name=name,
    )(u, u, dact, w, b)


def _ffn_conv_bwd(duc, w, name):
    blk, _, nxt, wsp, _, _ = _ffn_specs()
    last = S // FFN_ROWS - 1

    def body(d_ref, h_ref, w_ref, du_ref):
        is_last = pl.program_id(1) == last
        for p in range(2):
            d = d_ref[p]
            halo = jnp.where(is_last, 0.0, h_ref[p])
            d1, d2 = _shifted(d, halo, False)
            wv = w_ref[p]
            du_ref[p] = (wv[2:3, :] * d + wv[1:2, :] * d1 + wv[0:1, :] * d2).astype(BF16)

    return pl.pallas_call(
        body, out_shape=jax.ShapeDtypeStruct((2, S, FF), BF16), grid=(FF // FFN_COLS, S // FFN_ROWS),
        in_specs=[blk, nxt, wsp], out_specs=blk,
        compiler_params=_cparams(("parallel", "parallel")), name=name,
    )(duc, duc, w)


def _adam_update(w, gv, m, v):
    c1 = 1.0 / (1.0 - ADAM_B1 ** ADAM_STEP)
    c2 = 1.0 / (1.0 - ADAM_B2 ** ADAM_STEP)
    mn = ADAM_B1 * m + (1.0 - ADAM_B1) * gv
    vn = ADAM_B2 * v + (1.0 - ADAM_B2) * (gv * gv)
    return -ADAM_LR * ((mn * c1) / (jnp.sqrt(vn * c2) + ADAM_EPS) + ADAM_WD * w), mn, vn


def _adamw(w, g, m, v, name):
    rows = w.shape[0]
    T = 8
    for cand in (256, 128, 64, 32, 16, 8):
        if rows % cand == 0:
            T = cand
            break

    def body(w_ref, g_ref, m_ref, v_ref, d_ref, mo_ref, vo_ref):
        d_ref[...], mo_ref[...], vo_ref[...] = _adam_update(w_ref[...], g_ref[...], m_ref[...], v_ref[...])

    blk = pl.BlockSpec((T, w.shape[1]), lambda i: (i, 0))
    sds = jax.ShapeDtypeStruct(w.shape, F32)
    return pl.pallas_call(
        body, out_shape=[sds, sds, sds], grid=(rows // T,), in_specs=[blk] * 4, out_specs=[blk] * 3,
        compiler_params=_cparams(("parallel",)), name=name,
    )(w, g, m, v)


ANY = pl.BlockSpec(memory_space=pl.ANY)


def _place():
    x, y, c = lax.axis_index("x"), lax.axis_index("y"), lax.axis_index("c")
    chips = [(1 - x, y), (x, 1 - y), (1 - x, 1 - y)]
    return x, y, c, chips


def _place_own(w, k_arr):
    T = FLAT_T

    def body(k_ref, w_ref, o_ref):
        o_ref[...] = w_ref[...]

    return pl.pallas_call(
        body, out_shape=jax.ShapeDtypeStruct((N_CHIPS, FLAT_ROWS, FLAT_W), w.dtype),
        grid_spec=pltpu.PrefetchScalarGridSpec(
            num_scalar_prefetch=1, grid=(FLAT_ROWS // T,),
            in_specs=[pl.BlockSpec((T, FLAT_W), lambda i, k: (i, 0))],
            out_specs=pl.BlockSpec((None, T, FLAT_W), lambda i, k: (k[0], i, 0))),
        compiler_params=_cparams(("parallel",)), name="gather_place_own",
    )(k_arr, w)


def _allgather_shards(w, buf):
    def body(w_ref, buf_ref, g_ref, send_sems, recv_sems):
        x, y, c, chips = _place()
        myk = 2 * x + y
        sibling = (x, y, 1 - c)
        h0 = pl.multiple_of(c * HALF_ROWS, 16)
        h1 = pl.multiple_of((1 - c) * HALF_ROWS, 16)

        def half(k, start):
            return g_ref.at[k, pl.ds(start, HALF_ROWS), :]

        def rcopy(sem, src, dst, to):
            return pltpu.make_async_remote_copy(src_ref=src, dst_ref=dst, send_sem=send_sems.at[sem],
                                                recv_sem=recv_sems.at[sem], device_id=to, device_id_type=MESH)

        ici = [rcopy(r, w_ref.at[pl.ds(h0, HALF_ROWS), :], half(myk, h0), (*chip, c)) for r, chip in enumerate(chips)]
        for cp in ici:
            cp.start()
        ks = [2 * cx + cy for cx, cy in chips]
        fwd = [rcopy(3 + r, half(ks[r], h0), half(ks[r], h0), sibling) for r in range(3)]
        for r in range(3):
            rcopy(r, half(ks[r], h0), half(ks[r], h0), (*chips[r], c)).wait_recv()
            fwd[r].start()
        for r in range(3):
            rcopy(3 + r, half(ks[r], h1), half(ks[r], h1), sibling).wait_recv()
        for cp in ici + fwd:
            cp.wait_send()

    return pl.pallas_call(
        body, out_shape=jax.ShapeDtypeStruct((N_CHIPS, FLAT_ROWS, FLAT_W), w.dtype), in_specs=[ANY, ANY], out_specs=ANY,
        scratch_shapes=[pltpu.SemaphoreType.DMA((6,)), pltpu.SemaphoreType.DMA((6,))],
        input_output_aliases={1: 0},
        compiler_params=pltpu.CompilerParams(has_side_effects=True), name="allgather_shards",
    )(w, buf)


def _sibling_swap_half(g):
    def body(g_ref, o_ref, send_sem, recv_sem):
        x, y, c, _ = _place()
        theirs = pl.multiple_of((1 - c) * HALF_ROWS, 8)
        cp = pltpu.make_async_remote_copy(src_ref=g_ref.at[:, pl.ds(theirs, HALF_ROWS), :], dst_ref=o_ref,
                                          send_sem=send_sem, recv_sem=recv_sem, device_id=(x, y, 1 - c),
                                          device_id_type=MESH)
        cp.start()
        cp.wait()

    return pl.pallas_call(
        body, out_shape=jax.ShapeDtypeStruct((N_CHIPS, HALF_ROWS, FLAT_W), F32), in_specs=[ANY], out_specs=ANY,
        scratch_shapes=[pltpu.SemaphoreType.DMA, pltpu.SemaphoreType.DMA],
        compiler_params=pltpu.CompilerParams(has_side_effects=True), name="rs_sibling_swap",
    )(g)


def _pair_sum(g, other, c_arr):
    T = FLAT_T

    def body(c_ref, g_ref, o_ref, s_ref):
        s_ref[...] = (g_ref[...] + o_ref[...]).astype(BF16)

    nb = HALF_ROWS // T
    return pl.pallas_call(
        body, out_shape=jax.ShapeDtypeStruct((N_CHIPS, HALF_ROWS, FLAT_W), BF16),
        grid_spec=pltpu.PrefetchScalarGridSpec(
            num_scalar_prefetch=1, grid=(N_CHIPS, nb),
            in_specs=[pl.BlockSpec((None, T, FLAT_W), lambda k, i, c: (k, c[0] * nb + i, 0)),
                      pl.BlockSpec((None, T, FLAT_W), lambda k, i, c: (k, i, 0))],
            out_specs=pl.BlockSpec((None, T, FLAT_W), lambda k, i, c: (k, i, 0))),
        compiler_params=_cparams(("parallel", "parallel")), name="rs_pair_sum",
    )(c_arr, g, other)


def _chip_exchange(s):
    def body(s_ref, o_ref, send_sems, recv_sems):
        x, y, c, chips = _place()
        cps = []
        for r, (cx, cy) in enumerate(chips):
            cps.append(pltpu.make_async_remote_copy(
                src_ref=s_ref.at[2 * cx + cy], dst_ref=o_ref.at[r], send_sem=send_sems.at[r],
                recv_sem=recv_sems.at[r], device_id=(cx, cy, c), device_id_type=MESH))
        for cp in cps:
            cp.start()
        for cp in cps:
            cp.wait()

    return pl.pallas_call(
        body, out_shape=jax.ShapeDtypeStruct((3, HALF_ROWS, FLAT_W), BF16), in_specs=[ANY], out_specs=ANY,
        scratch_shapes=[pltpu.SemaphoreType.DMA((3,)), pltpu.SemaphoreType.DMA((3,))],
        compiler_params=pltpu.CompilerParams(has_side_effects=True), name="rs_chip_exchange",
    )(s)


def _chip_sum(s, r, k_arr):
    T = FLAT_T

    def body(k_ref, s_ref, r_ref, o_ref):
        o_ref[...] = ((s_ref[...].astype(F32) + r_ref[0].astype(F32)) + r_ref[1].astype(F32)) + r_ref[2].astype(F32)

    return pl.pallas_call(
        body, out_shape=jax.ShapeDtypeStruct((HALF_ROWS, FLAT_W), F32),
        grid_spec=pltpu.PrefetchScalarGridSpec(
            num_scalar_prefetch=1, grid=(HALF_ROWS // T,),
            in_specs=[pl.BlockSpec((None, T, FLAT_W), lambda i, k: (k[0], i, 0)),
                      pl.BlockSpec((3, T, FLAT_W), lambda i, k: (0, i, 0))],
            out_specs=pl.BlockSpec((T, FLAT_W), lambda i, k: (i, 0))),
        compiler_params=_cparams(("parallel",)), name="rs_chip_sum",
    )(k_arr, s, r)


def _sibling_send(t):
    def body(t_ref, o_ref, send_sem, recv_sem):
        x, y, c, _ = _place()
        cp = pltpu.make_async_remote_copy(src_ref=t_ref, dst_ref=o_ref, send_sem=send_sem, recv_sem=recv_sem,
                                          device_id=(x, y, 1 - c), device_id_type=MESH)
        cp.start()
        cp.wait()

    return pl.pallas_call(
        body, out_shape=jax.ShapeDtypeStruct((HALF_ROWS, FLAT_W), F32), in_specs=[ANY], out_specs=ANY,
        scratch_shapes=[pltpu.SemaphoreType.DMA, pltpu.SemaphoreType.DMA],
        compiler_params=pltpu.CompilerParams(has_side_effects=True), name="rs_sibling_send",
    )(t)


def _allreduce_small(v):
    def body(v_ref, o_ref, buf, send_sems, recv_sems):
        x, y, c, _ = _place()
        me = 4 * x + 2 * y + c
        buf[me] = v_ref[...]
        cps = []
        for mask in range(1, 8):
            a, b, d = (mask >> 2) & 1, (mask >> 1) & 1, mask & 1
            peer = (x + a - 2 * a * x, y + b - 2 * b * y, c + d - 2 * d * c)
            cps.append(pltpu.make_async_remote_copy(
                src_ref=v_ref, dst_ref=buf.at[me], send_sem=send_sems.at[mask - 1], recv_sem=recv_sems.at[mask - 1],
                device_id=peer, device_id_type=MESH))
        for cp in cps:
            cp.start()
        for cp in cps:
            cp.wait()
        total = buf[0]
        for dev in range(1, 8):
            total = total + buf[dev]
        o_ref[...] = total

    vm = pl.BlockSpec(memory_space=pltpu.VMEM)
    return pl.pallas_call(
        body, out_shape=jax.ShapeDtypeStruct((SMALL_ROWS, 1024), F32), in_specs=[vm], out_specs=vm,
        scratch_shapes=[pltpu.VMEM((8, SMALL_ROWS, 1024), F32), pltpu.SemaphoreType.DMA((7,)),
                        pltpu.SemaphoreType.DMA((7,))],
        compiler_params=pltpu.CompilerParams(has_side_effects=True), name="allreduce_small",
    )(v)


def _heads(t, n):
    return t.reshape(S, n, HEAD_DIM).transpose(1, 0, 2)


def _unheads(t):
    return t.transpose(1, 0, 2).reshape(S, t.shape[0] * HEAD_DIM)


def _to_residue(t, d):
    c = t.shape[-1]
    return t.reshape(B_HPG, S // d, d, c).transpose(0, 2, 1, 3).reshape(B_HPG, S, c)


def _from_residue(t, d):
    c = t.shape[-1]
    return t.reshape(B_HPG, d, S // d, c).transpose(0, 2, 1, 3).reshape(B_HPG, S, c)


def _dil_pack(t):
    return jnp.concatenate([_to_residue(t[g], d) for g, (_, d) in enumerate(B_GROUPS)], axis=0)


def _dil_unpack(t):
    return jnp.stack([_from_residue(t[g * B_HPG:(g + 1) * B_HPG], d) for g, (_, d) in enumerate(B_GROUPS)], axis=0)


def _col_to_row(t):
    return t.reshape(t.shape[0], 1, S)


GROUP_W = B_HPG * HEAD_DIM


def _residue_rows(t, inverse=False):
    n = t.shape[1] // B_Q_W
    t4 = t.reshape(S, n, 3, GROUP_W)
    parts = []
    for g, (_, d) in enumerate(B_GROUPS):
        x = t4[:, :, g, :]
        if d > 1:
            shape = (d, S // d) if inverse else (S // d, d)
            x = x.reshape(shape + (n, GROUP_W)).transpose(1, 0, 2, 3).reshape(S, n, GROUP_W)
        parts.append(x)
    return jnp.stack(parts, axis=2).reshape(S, n * B_Q_W)


def _residue_vecs(t, inverse=False):
    parts = []
    for g, (_, d) in enumerate(B_GROUPS):
        x = t[g * B_HPG:(g + 1) * B_HPG].reshape(B_HPG, S)
        if d > 1:
            shape = (d, S // d) if inverse else (S // d, d)
            x = x.reshape((B_HPG,) + shape).transpose(0, 2, 1).reshape(B_HPG, S)
        parts.append(x)
    return jnp.concatenate(parts, axis=0).reshape(N_HG, S, 1)


def _ffn_fwd(x, g, w_up, cw, cb, w_down, tag):
    h = _rms_fwd(x, g, f"{tag}_norm")
    u = _mm(h, w_up, mode="nn", tm=1024, tn=1408, tk=1024, o_split=2, name=f"{tag}_up")
    act = _ffn_act_fwd(u, cw, cb, f"{tag}_act")
    x_out = _mm(act, w_down, mode="nn", tm=1024, tn=512, tk=FF, res=x, name=f"{tag}_down")
    return x_out, (h, u, act)


def _ffn_bwd(x, g, w_up, cw, cb, w_down, saved, dx, dxb, tag):
    h, u, act = saved
    d_w_down = _mm(act, dxb, mode="tn", tm=1408, tn=512, tk=1024, name=f"{tag}_dwdown")
    dact = _mm(dxb, w_down, mode="nt", tm=1024, tn=1408, tk=1024, name=f"{tag}_dact")
    duc, dwb = _ffn_act_bwd(u, dact, cw, cb, f"{tag}_dgate")
    du = _ffn_conv_bwd(duc, cw, f"{tag}_dconv")
    d_w_up = _mm(h, du, mode="tn", tm=1024, tn=1408, tk=1024, b_split=2, name=f"{tag}_dwup")
    dh = _mm(du, w_up, mode="nt", tm=1024, tn=512, tk=1408, a_split=2, name=f"{tag}_dh")
    dx_new, dxb_new, (dg,) = _rms_bwd(x, dx, [(g, dh)], f"{tag}_dnorm")
    d_cw = dwb[:, 0:3, :].transpose(1, 0, 2).reshape(3, 2 * FF)
    d_cb = dwb[:, 3, :].reshape(2 * FF)
    return dx_new, dxb_new, dict(w_up=d_w_up, w_down=d_w_down, conv_w=d_cw, conv_b=d_cb, norm_g=dg.reshape(D))


def _local_step(x, target, p):
    g = {}
    h1 = _rms_fwd(x, p["mix_norm_g"][0], "a_norm")
    w_qkv = p["a_w_in"][:, :QKV_W]
    w_f = jnp.pad(p["a_w_in"][:, QKV_W:], ((0, 0), (0, LANES - A_HEADS)))
    b_f = jnp.pad(p["a_b_f"].reshape(1, A_HEADS), ((0, 0), (0, LANES - A_HEADS)))
    qkv = _mm(h1, w_qkv, mode="nn", tm=1024, tn=512, tk=1024, out_dtype=BF16, name="a_qkv")
    pf = _mm(h1, w_f, mode="nn", tm=1024, tn=LANES, tk=1024, name="a_gate")
    cum = _fgate_fwd(pf, b_f, "a_gate_scan")
    c_hs = cum[:, :A_HEADS].T
    c_col, c_row = c_hs.reshape(A_HEADS, S, 1), c_hs.reshape(A_HEADS, 1, S)
    oa2, lse_a = _fox_pair_fwd(qkv, c_col, c_row, "a_attn")
    x1 = _mm(oa2, p["a_w_out"], mode="nn", tm=1024, tn=512, tk=1024, res=x, name="a_out")
    x2, ffn0 = _ffn_fwd(x1, p["ffn_norm_g"][0], p["ffn_w_up"][0], p["conv_w"][0], p["conv_b"][0], p["ffn_w_down"][0], "f0")
    hk = _rms_fwd(x2, p["kv_norm_g"], "kv_norm")
    kvb = _mm(hk, p["w_kv"], mode="nn", tm=1024, tn=512, tk=1024, out_dtype=BF16, name="kv_proj")
    h3 = _rms_fwd(x2, p["mix_norm_g"][1], "b_norm")
    qb = _mm(h3, p["b_w_q"], mode="nn", tm=1024, tn=512, tk=1024, out_dtype=BF16, name="b_q")
    qd = _residue_rows(qb)
    kvd = _residue_rows(kvb)
    od, lsed = _dil_pair_fwd(qd, kvd, "b_attn")
    ob, ob2, lse_b = _dil_pair_merge(_residue_rows(od, inverse=True), _residue_vecs(lsed, inverse=True), "b_merge")
    x3 = _mm(ob2, p["b_w_out"], mode="nn", tm=1024, tn=512, tk=B_OUT_W, res=x2, name="b_out")
    x4, ffn1 = _ffn_fwd(x3, p["ffn_norm_g"][1], p["ffn_w_up"][1], p["conv_w"][1], p["conv_b"][1], p["ffn_w_down"][1], "f1")
    loss, dx, dxb, dg_final = _loss_head(x4, p["final_norm_g"], target, "loss_head")
    g["final_norm_g"] = dg_final.reshape(D)

    dx, dxb, gf1 = _ffn_bwd(x3, p["ffn_norm_g"][1], p["ffn_w_up"][1], p["conv_w"][1], p["conv_b"][1], p["ffn_w_down"][1],
                            ffn1, dx, dxb, "f1")
    g["b_w_out"] = _mm(ob2, dxb, mode="tn", tm=B_OUT_W, tn=512, tk=1024, name="b_dwout")
    dob = _mm(dxb, p["b_w_out"], mode="nt", tm=1024, tn=B_OUT_W, tk=1024, name="b_do")
    delta_b = _pair_rowdot(dob, ob, "b_delta")
    dod = _residue_rows(jnp.tile(dob.astype(BF16), (1, 3)))
    l_d = _residue_vecs(jnp.tile(lse_b, (3, 1, 1)))
    dl_d = _residue_vecs(jnp.tile(delta_b, (3, 1, 1)))
    dqd = _dil_pair_dq(qd, kvd, dod, l_d, dl_d, "b_dq")
    dkd, dvd = _dil_pair_dkv(qd, kvd, dod, _col_to_row(l_d), _col_to_row(dl_d), "b_dkv")
    dqb = _residue_rows(dqd, inverse=True)
    dkvb = _residue_rows(jnp.concatenate([dkd, dvd], axis=1), inverse=True)
    g["b_w_q"] = _mm(h3, dqb, mode="tn", tm=1024, tn=512, tk=1024, name="b_dwq")
    dh3 = _mm(dqb, p["b_w_q"], mode="nt", tm=1024, tn=512, tk=B_Q_W, name="b_dh")
    g["w_kv"] = _mm(hk, dkvb, mode="tn", tm=1024, tn=512, tk=1024, name="kv_dw")
    dhk = _mm(dkvb, p["w_kv"], mode="nt", tm=1024, tn=512, tk=1536, name="kv_dh")
    dx, dxb, (dg_mix1, dg_kv) = _rms_bwd(x2, dx, [(p["mix_norm_g"][1], dh3), (p["kv_norm_g"], dhk)], "b_dnorm")
    g["kv_norm_g"] = dg_kv.reshape(D)
    dx, dxb, gf0 = _ffn_bwd(x1, p["ffn_norm_g"][0], p["ffn_w_up"][0], p["conv_w"][0], p["conv_b"][0], p["ffn_w_down"][0],
                            ffn0, dx, dxb, "f0")
    g["a_w_out"] = _mm(oa2, dxb, mode="tn", tm=1024, tn=512, tk=1024, name="a_dwout")
    doa = _mm(dxb, p["a_w_out"], mode="nt", tm=1024, tn=512, tk=1024, name="a_do")
    delta_a = _pair_rowdot(doa, oa2, "a_delta")
    dqa, dka, dva, dck, dcq = _fox_pair_bwd(qkv, doa, _col_to_row(lse_a), _col_to_row(delta_a), c_row, c_col, "a_dattn")
    dqkv = jnp.concatenate([dqa, dka, dva], axis=1)
    pad_heads = lambda t: jnp.pad(t.reshape(A_HEADS, S).T, ((0, 0), (0, LANES - A_HEADS)))
    dpf, db_f = _fgate_bwd(pf, b_f, pad_heads(dck), pad_heads(dcq), "a_dgate_scan")
    g["a_b_f"] = db_f[:, :A_HEADS]
    d_w_qkv = _mm(h1, dqkv, mode="tn", tm=1024, tn=512, tk=1024, name="a_dwqkv")
    d_w_f = _mm(h1, dpf, mode="tn", tm=1024, tn=LANES, tk=1024, name="a_dwgate")
    g["a_w_in"] = jnp.concatenate([d_w_qkv, d_w_f[:, :A_HEADS]], axis=1)
    dh1 = _mm(dqkv, w_qkv, mode="nt", tm=1024, tn=512, tk=1536, name="a_dh")
    dh1 = _mm(dpf, w_f, mode="nt", tm=1024, tn=512, tk=LANES, res=dh1, name="a_dh_gate")
    dx, _, (dg_mix0,) = _rms_bwd(x, dx, [(p["mix_norm_g"][0], dh1)], "a_dnorm")

    g["mix_norm_g"] = jnp.stack([dg_mix0.reshape(D), dg_mix1.reshape(D)])
    g["ffn_norm_g"] = jnp.stack([gf0["norm_g"], gf1["norm_g"]])
    g["ffn_w_up"] = jnp.stack([gf0["w_up"], gf1["w_up"]])
    g["ffn_w_down"] = jnp.stack([gf0["w_down"], gf1["w_down"]])
    g["ffn_conv_w"] = jnp.stack([gf0["conv_w"], gf1["conv_w"]])
    g["ffn_conv_b"] = jnp.stack([gf0["conv_b"], gf1["conv_b"]])
    return loss[0, 0], dx, g


_SHARD_SHAPES = {"a_w_in": (1, 1024, 772), "a_w_out": (1, 256, 1024), "b_w_q": (1, 1024, 384), "b_w_out": (1, 512, 256),
                 "w_kv": (1024, 768), "ffn_w_up": (2, 1024, 1408), "ffn_w_down": (2, 704, 1024), "ffn_conv_w": (2, 3, 1408)}
_SHARD_AXIS = {"a_w_in": 2, "a_w_out": 1, "b_w_q": 2, "b_w_out": 2, "w_kv": 1, "ffn_w_up": 2, "ffn_w_down": 1,
               "ffn_conv_w": 2}
_SMALL = (("kv_norm_g", (1024,)), ("mix_norm_g", (2, 1024)), ("ffn_norm_g", (2, 1024)), ("final_norm_g", (1024,)),
          ("a_b_f", (1, 16)), ("ffn_conv_b", (2, 5632)))


def _slabs(t, L, R, C, rpad):
    nc = -(-C // FLAT_W)
    t = jnp.pad(t.reshape(L, R, C), ((0, 0), (0, rpad - R), (0, nc * FLAT_W - C)))
    return t.reshape(L, rpad, nc, FLAT_W).transpose(0, 2, 1, 3).reshape(L * nc * rpad, FLAT_W)


def _unslabs(rows, L, R, C, rpad):
    nc = -(-C // FLAT_W)
    return rows.reshape(L, nc, rpad, FLAT_W).transpose(0, 2, 1, 3).reshape(L, rpad, nc * FLAT_W)[:, :R, :C]


_SEG_RT = {"ffn_w_down": 704, "a_w_in": 1024, "a_w_out": 256, "b_w_q": 1024, "b_w_out": 512, "w_kv": 1024,
           "ffn_w_up": 1024, "ffn_conv_w": 16}
_ROW_SHARDED = ("a_w_out", "ffn_w_down")


def _seg(name):
    off = 0
    for s in _SEGS:
        if s[0] == name:
            _, L, R, C, rpad = s
            rt = _SEG_RT[name]
            assert off % rt == 0 and rpad % rt == 0 and (HALF_ROWS % rt == 0 or off + _seg_rows(*s) <= HALF_ROWS)
            return dict(L=L, R=R, C=C, rpad=rpad, nc=-(-C // FLAT_W), rt=rt, off=off, ni=rpad // rt)
        off += _seg_rows(*s)
    raise KeyError(name)


def _flat_block(sg, term=0):
    base = (sg["off"] + term * sg["L"] * sg["nc"] * sg["rpad"]) // sg["rt"]
    return lambda l, j, i: base + (l * sg["nc"] + j) * sg["ni"] + i


def _native3(t, name):
    sg = _seg(name)
    t = t.reshape(sg["L"], sg["R"], sg["C"])
    return jnp.pad(t, ((0, 0), (0, sg["rpad"] - sg["R"]), (0, 0))) if sg["rpad"] != sg["R"] else t


def _slab_pack(flat, t, name, term=None):
    sg = _seg(name)
    rt = sg["rt"]
    rb = _flat_block(sg, term or 0)

    def body(*refs):
        t_ref, o_ref = refs[-2], refs[-1]
        val = t_ref[...]
        o_ref[...] = val.astype(BF16) if term is None else _split3(val)[term]

    in_specs = [pl.BlockSpec((None, rt, FLAT_W), lambda l, j, i: (l, i, j))]
    args = [t]
    if flat is not None:
        in_specs, args = [ANY] + in_specs, [flat] + args
    return pl.pallas_call(
        body, out_shape=jax.ShapeDtypeStruct((FLAT_ROWS, FLAT_W), BF16), grid=(sg["L"], sg["nc"], sg["ni"]),
        in_specs=in_specs, out_specs=pl.BlockSpec((rt, FLAT_W), lambda l, j, i: (rb(l, j, i), 0)),
        input_output_aliases={0: 0} if flat is not None else {},
        compiler_params=_cparams(("parallel", "parallel", "parallel")), name=f"pack_{name}_{term or 0}",
    )(*args)


def _full_spec(sg, name):
    rt, nc, ni = sg["rt"], sg["nc"], sg["ni"]
    if name in _ROW_SHARDED:
        return (sg["L"], N_CHIPS * sg["R"], sg["C"]), pl.BlockSpec((None, rt, FLAT_W), lambda k, l, j, i: (l, k * ni + i, j))
    return ((sg["L"], sg["rpad"], N_CHIPS * nc * FLAT_W),
            pl.BlockSpec((None, rt, FLAT_W), lambda k, l, j, i: (l, i, k * nc + j)))


def _slab_unpack(gathered, name):
    sg = _seg(name)
    rb = _flat_block(sg)
    shape, spec = _full_spec(sg, name)

    def body(g_ref, o_ref):
        o_ref[...] = g_ref[...]

    return pl.pallas_call(
        body, out_shape=jax.ShapeDtypeStruct(shape, BF16), grid=(N_CHIPS, sg["L"], sg["nc"], sg["ni"]),
        in_specs=[pl.BlockSpec((None, sg["rt"], FLAT_W), lambda k, l, j, i: (k, rb(l, j, i), 0))], out_specs=spec,
        compiler_params=_cparams(("parallel",) * 4), name=f"unpack_{name}",
    )(gathered)


def _slab_pack_grad(flat4, g, name):
    sg = _seg(name)
    rb = _flat_block(sg)
    shape, spec = _full_spec(sg, name)
    assert g.shape == shape, (name, g.shape, shape)

    def body(*refs):
        refs[-1][...] = refs[-2][...]

    in_specs, args = [spec], [g]
    if flat4 is not None:
        in_specs, args = [pl.BlockSpec(memory_space=pl.ANY)] + in_specs, [flat4] + args
    return pl.pallas_call(
        body, out_shape=jax.ShapeDtypeStruct((N_CHIPS, FLAT_ROWS, FLAT_W), F32), grid=(N_CHIPS, sg["L"], sg["nc"], sg["ni"]),
        in_specs=in_specs, out_specs=pl.BlockSpec((None, sg["rt"], FLAT_W), lambda k, l, j, i: (k, rb(l, j, i), 0)),
        input_output_aliases={0: 0} if flat4 is not None else {},
        compiler_params=_cparams(("parallel",) * 4), name=f"packgrad_{name}",
    )(*args)


def _adamw_shard(w, m, v, g_mine, g_other, c_arr, name):
    sg = _seg(name)
    rt = sg["rt"]
    rb = _flat_block(sg)
    per_half = HALF_ROWS // rt

    def half_of(l, j, i):
        return (rb(l, j, i) * rt) // HALF_ROWS

    def body(c_ref, w_ref, m_ref, v_ref, gm_ref, go_ref, g_ref, d_ref, mo_ref, vo_ref):
        is_mine = half_of(pl.program_id(0), pl.program_id(1), pl.program_id(2)) == c_ref[0]
        gv = jnp.where(is_mine, gm_ref[...], go_ref[...])
        g_ref[...] = gv
        d_ref[...], mo_ref[...], vo_ref[...] = _adam_update(w_ref[...], gv, m_ref[...], v_ref[...])

    nat = pl.BlockSpec((None, rt, FLAT_W), lambda l, j, i, c: (l, i, j))
    half = pl.BlockSpec((rt, FLAT_W), lambda l, j, i, c: (rb(l, j, i) - half_of(l, j, i) * per_half, 0))
    sds = jax.ShapeDtypeStruct(w.shape, F32)
    return pl.pallas_call(
        body, out_shape=[sds] * 4,
        grid_spec=pltpu.PrefetchScalarGridSpec(num_scalar_prefetch=1, grid=(sg["L"], sg["nc"], sg["ni"]),
                                               in_specs=[nat, nat, nat, half, half], out_specs=[nat] * 4),
        compiler_params=_cparams(("parallel", "parallel", "parallel")), name=f"adamw_{name}",
    )(c_arr, w, m, v, g_mine, g_other)


def _pack_small(vals, loss=None):
    parts = [vals[name].astype(F32).reshape(-1) for name, _ in _SMALL]
    if loss is not None:
        parts.append(loss.reshape(1))
    flat = jnp.concatenate(parts)
    return jnp.pad(flat, (0, SMALL_ROWS * 1024 - flat.shape[0])).reshape(SMALL_ROWS, 1024)


def _unpack_small(flat):
    flat = flat.reshape(-1)
    out = {}
    o = 0
    for name, shape in _SMALL:
        n = int(np.prod(shape))
        out[name] = flat[o:o + n].reshape(shape)
        o += n
    return out, flat[o]


_BIG = ("a_w_in", "a_w_out", "b_w_q", "b_w_out", "w_kv", "ffn_w_up", "ffn_w_down", "ffn_conv_w")
A_IN_PAD = 896


def _pack_weights(w):
    flat = None
    for name in _BIG:
        t = _native3(w[name], name)
        for term in ((0, 1, 2) if name == "ffn_conv_w" else (None,)):
            flat = _slab_pack(flat, t, name, term)
    return flat


def _full_weights(gathered):
    full = {name: _slab_unpack(gathered, name) for name in _BIG if name != "ffn_conv_w"}
    a_in = full["a_w_in"].reshape(D, N_CHIPS, A_IN_PAD)[:, :, :772].reshape(D, N_CHIPS * 772)
    sg = _seg("ffn_conv_w")
    n1 = sg["nc"] * sg["rpad"]
    per_chip = []
    for k in range(N_CHIPS):
        terms = [_unslabs(gathered[k, sg["off"] + i * n1:sg["off"] + (i + 1) * n1], 1, sg["R"], sg["C"], sg["rpad"]).astype(F32)
                 for i in range(CONV_TERMS)]
        per_chip.append((terms[0] + terms[1]) + terms[2])
    cw = jnp.concatenate(per_chip, axis=2).reshape(2, 3, 2, FF).transpose(0, 2, 1, 3)
    return dict(a_w_in=a_in, a_w_out=full["a_w_out"][0], b_w_q=full["b_w_q"][0], b_w_out=full["b_w_out"][0],
                w_kv=full["w_kv"][0], ffn_w_up=full["ffn_w_up"], ffn_w_down=full["ffn_w_down"], conv_w=cw)


def _shard_grads(g):
    a_in = jnp.pad(g["a_w_in"].reshape(D, N_CHIPS, 772), ((0, 0), (0, 0), (0, A_IN_PAD - 772)))
    sgc = _seg("ffn_conv_w")
    full = {"a_w_in": a_in.reshape(1, D, N_CHIPS * A_IN_PAD), "a_w_out": g["a_w_out"][None], "b_w_q": g["b_w_q"][None],
            "b_w_out": g["b_w_out"][None], "w_kv": g["w_kv"][None], "ffn_w_up": g["ffn_w_up"],
            "ffn_w_down": g["ffn_w_down"],
            "ffn_conv_w": jnp.pad(g["ffn_conv_w"].reshape(1, sgc["R"], 2 * FF), ((0, 0), (0, sgc["rpad"] - sgc["R"]), (0, 0)))}
    flat4 = None
    for name in _BIG:
        flat4 = _slab_pack_grad(flat4, full[name], name)
    return flat4


_WEIGHTS = ["a_w_in", "a_b_f", "a_w_out", "b_w_q", "b_w_out", "kv_norm_g", "w_kv", "mix_norm_g", "ffn_norm_g", "ffn_w_up",
            "ffn_conv_w", "ffn_conv_b", "ffn_w_down", "final_norm_g"]


def kernel(x, a_w_in, a_b_f, a_w_out, b_w_q, b_w_out, kv_norm_g, w_kv, mix_norm_g, ffn_norm_g, ffn_w_up, ffn_conv_w, ffn_conv_b, ffn_w_down, final_norm_g, loss_target, m_a_w_in, m_a_b_f, m_a_w_out, m_b_w_q, m_b_w_out, m_kv_norm_g, m_w_kv, m_mix_norm_g, m_ffn_norm_g, m_ffn_w_up, m_ffn_conv_w, m_ffn_conv_b, m_ffn_w_down, m_final_norm_g, v_a_w_in, v_a_b_f, v_a_w_out, v_b_w_q, v_b_w_out, v_kv_norm_g, v_w_kv, v_mix_norm_g, v_ffn_norm_g, v_ffn_w_up, v_ffn_conv_w, v_ffn_conv_b, v_ffn_w_down, v_final_norm_g):
    w = dict(a_w_in=a_w_in, a_b_f=a_b_f, a_w_out=a_w_out, b_w_q=b_w_q, b_w_out=b_w_out, kv_norm_g=kv_norm_g, w_kv=w_kv,
             mix_norm_g=mix_norm_g, ffn_norm_g=ffn_norm_g, ffn_w_up=ffn_w_up, ffn_conv_w=ffn_conv_w, ffn_conv_b=ffn_conv_b,
             ffn_w_down=ffn_w_down, final_norm_g=final_norm_g)
    m = dict(a_w_in=m_a_w_in, a_b_f=m_a_b_f, a_w_out=m_a_w_out, b_w_q=m_b_w_q, b_w_out=m_b_w_out, kv_norm_g=m_kv_norm_g,
             w_kv=m_w_kv, mix_norm_g=m_mix_norm_g, ffn_norm_g=m_ffn_norm_g, ffn_w_up=m_ffn_w_up, ffn_conv_w=m_ffn_conv_w,
             ffn_conv_b=m_ffn_conv_b, ffn_w_down=m_ffn_w_down, final_norm_g=m_final_norm_g)
    v = dict(a_w_in=v_a_w_in, a_b_f=v_a_b_f, a_w_out=v_a_w_out, b_w_q=v_b_w_q, b_w_out=v_b_w_out, kv_norm_g=v_kv_norm_g,
             w_kv=v_w_kv, mix_norm_g=v_mix_norm_g, ffn_norm_g=v_ffn_norm_g, ffn_w_up=v_ffn_w_up, ffn_conv_w=v_ffn_conv_w,
             ffn_conv_b=v_ffn_conv_b, ffn_w_down=v_ffn_w_down, final_norm_g=v_final_norm_g)

    c_arr = lax.axis_index("c").astype(jnp.int32).reshape(1)
    k_arr = (2 * lax.axis_index("x") + lax.axis_index("y")).astype(jnp.int32).reshape(1)
    w16 = _pack_weights(w)
    gathered = _allgather_shards(w16, _place_own(w16, k_arr))
    p = _full_weights(gathered)
    cb = ffn_conv_b.reshape(2, 2, 1, FF)
    p.update(a_b_f=a_b_f, kv_norm_g=kv_norm_g, mix_norm_g=mix_norm_g, ffn_norm_g=ffn_norm_g, final_norm_g=final_norm_g,
             conv_b=cb)

    loss_part, grad_x, g = _local_step(x[0], loss_target[0], p)

    gflat = _shard_grads(g)
    pair = _pair_sum(gflat, _sibling_swap_half(gflat), c_arr)
    g_mine = _chip_sum(pair, _chip_exchange(pair), k_arr)
    g_other = _sibling_send(g_mine)
    small, loss = _unpack_small(_allreduce_small(_pack_small(g, loss_part)))

    big = [{}, {}, {}, {}]
    for name in _BIG:
        sg = _seg(name)
        res = _adamw_shard(_native3(w[name], name), _native3(m[name], name), _native3(v[name], name), g_mine, g_other,
                           c_arr, name)
        for store, t in zip(big, res):
            store[name] = t[:, :sg["R"], :].reshape(_SHARD_SHAPES[name])
    dws, mns, vns = _adamw(_pack_small(w), _pack_small(small), _pack_small(m), _pack_small(v), "adamw_small")
    sml = [small] + [_unpack_small(t)[0] for t in (dws, mns, vns)]
    outs = [loss, grad_x[None]]
    for b, s in zip(big, sml):
        outs += [b[n] if n in b else s[n] for n in _WEIGHTS]
    return tuple(outs)
```

```python
import functools
import math

import numpy as np
import jax
import jax.numpy as jnp
from jax import lax
from jax.experimental import pallas as pl
from jax.experimental.pallas import tpu as pltpu

F32 = jnp.float32
BF16 = jnp.bfloat16
MESH = pl.DeviceIdType.MESH

S = 4096
D = 1024
A_HEADS = 16
HEAD_DIM = 64
QKV_W = 3 * A_HEADS * HEAD_DIM
B_GROUPS = ((128, 1), (512, 4), (2048, 16))
B_HPG = 8
B_Q_W = 3 * B_HPG * HEAD_DIM
B_OUT_W = B_HPG * HEAD_DIM
B_KV_W = 2 * B_Q_W
B_WIN = 128
FF = 2816
RMS_EPS = 1e-6
SCALE = HEAD_DIM ** -0.5
N_CHIPS = 4

ADAM_LR, ADAM_B1, ADAM_B2, ADAM_EPS, ADAM_WD, ADAM_STEP = 0.001, 0.9, 0.999, 1e-08, 0.01, 10

V7X_VMEM_LIMIT = 48 * 1024 * 1024
LANES = 128
NEG_INF = float("-inf")

FLAT_W = LANES
_SEGS = (("ffn_w_down", 2, 704, 1024, 704), ("a_w_in", 1, 1024, 772, 1024), ("a_w_out", 1, 256, 1024, 256),
         ("b_w_q", 1, 1024, 384, 1024), ("b_w_out", 1, 512, 256, 512), ("w_kv", 1, 1024, 768, 1024),
         ("ffn_w_up", 2, 1024, 1408, 1024), ("ffn_conv_w", 1, 6, 1408, 16))
CONV_TERMS = 3


def _seg_rows(name, L, R, C, rpad):
    return (CONV_TERMS if name == "ffn_conv_w" else 1) * L * (-(-C // FLAT_W)) * rpad


FLAT_T = 2048
FLAT_ROWS = 57344
HALF_ROWS = FLAT_ROWS // 2
assert sum(_seg_rows(*s) for s in _SEGS) <= FLAT_ROWS and HALF_ROWS % FLAT_T == 0
SMALL_ROWS = 24


def _cparams(sem=None, **kw):
    return pltpu.CompilerParams(dimension_semantics=sem, vmem_limit_bytes=V7X_VMEM_LIMIT, **kw)


_DN = {"nn": (((1,), (0,)), ((), ())), "nt": (((1,), (1,)), ((), ())), "tn": (((0,), (0,)), ((), ()))}


def _mm(a, b, *, mode, tm, tn, tk, name, out_dtype=F32, res=None, a_split=0, b_split=0, o_split=0):
    if mode == "tn":
        K = a.shape[0]
        M = a.shape[1]
    else:
        M = a.shape[-2]
        K = a.shape[-1] * (2 if a_split else 1)
    if mode == "nt":
        N = b.shape[0]
    else:
        N = b.shape[-1] * (2 if b_split else 1)
    assert M % tm == 0 and N % tn == 0 and K % tk == 0, (name, M, N, K, tm, tn, tk)
    nk = K // tk

    if mode == "tn":
        a_spec = pl.BlockSpec((tk, tm), lambda i, j, k: (k, i))
    elif a_split:
        a_spec = pl.BlockSpec((None, tm, tk), lambda i, j, k: (k // a_split, i, k % a_split))
    else:
        a_spec = pl.BlockSpec((tm, tk), lambda i, j, k: (i, k))
    if mode == "nt":
        b_spec = pl.BlockSpec((tn, tk), lambda i, j, k: (j, k))
    elif b_split:
        b_spec = pl.BlockSpec((None, tk, tn), lambda i, j, k: (j // b_split, k, j % b_split))
    else:
        b_spec = pl.BlockSpec((tk, tn), lambda i, j, k: (k, j))
    if o_split:
        o_spec = pl.BlockSpec((None, tm, tn), lambda i, j, k: (j // o_split, i, j % o_split))
        out_shape = jax.ShapeDtypeStruct((2, M, N // 2), out_dtype)
    else:
        o_spec = pl.BlockSpec((tm, tn), lambda i, j, k: (i, j))
        out_shape = jax.ShapeDtypeStruct((M, N), out_dtype)
    in_specs = [a_spec, b_spec]
    args = [a, b]
    if res is not None:
        in_specs.append(pl.BlockSpec((tm, tn), lambda i, j, k: (i, j)))
        args.append(res)

    def body(*refs):
        if res is not None:
            a_ref, b_ref, r_ref, o_ref = refs[:4]
        else:
            a_ref, b_ref, o_ref = refs[:3]
            r_ref = None
        p = lax.dot_general(a_ref[...].astype(BF16), b_ref[...].astype(BF16), _DN[mode], preferred_element_type=F32)

        def finish(r):
            if r_ref is not None:
                r = r + r_ref[...]
            o_ref[...] = r.astype(out_dtype)

        if nk == 1:
            finish(p)
        else:
            acc = refs[-1]
            k = pl.program_id(2)

            @pl.when(k == 0)
            def _():
                acc[...] = p

            @pl.when(k > 0)
            def _():
                acc[...] += p

            @pl.when(k == nk - 1)
            def _():
                finish(acc[...])

    return pl.pallas_call(
        body, out_shape=out_shape, grid=(M // tm, N // tn, nk), in_specs=in_specs, out_specs=o_spec,
        scratch_shapes=[pltpu.VMEM((tm, tn), F32)] if nk > 1 else [],
        compiler_params=_cparams(("parallel", "parallel", "arbitrary")), name=name,
    )(*args)


NORM_ROWS = 256


def _rms_fwd(x, g, name):
    def body(x_ref, g_ref, o_ref):
        xv = x_ref[...]
        r = lax.rsqrt(jnp.mean(xv * xv, axis=-1, keepdims=True) + RMS_EPS)
        o_ref[...] = (xv * r * g_ref[...]).astype(BF16)

    row = pl.BlockSpec((NORM_ROWS, D), lambda i: (i, 0))
    return pl.pallas_call(
        body, out_shape=jax.ShapeDtypeStruct((S, D), BF16), grid=(S // NORM_ROWS,),
        in_specs=[row, pl.BlockSpec((1, D), lambda i: (0, 0))], out_specs=row,
        compiler_params=_cparams(("parallel",)), name=name,
    )(x, g.reshape(1, D))


def _rms_bwd(x, dres, pairs, name):
    n = len(pairs)

    def body(*refs):
        x_ref, dres_ref = refs[0], refs[1]
        g_refs = refs[2:2 + 2 * n:2]
        dh_refs = refs[3:3 + 2 * n:2]
        dx_ref, dxb_ref = refs[2 + 2 * n], refs[3 + 2 * n]
        dg_refs = refs[4 + 2 * n:]
        i = pl.program_id(0)
        xv = x_ref[...]
        r = lax.rsqrt(jnp.mean(xv * xv, axis=-1, keepdims=True) + RMS_EPS)
        y = xv * r
        dx = dres_ref[...]
        for g_ref, dh_ref, dg_ref in zip(g_refs, dh_refs, dg_refs):
            dh = dh_ref[...]
            dy = dh * g_ref[...]
            dx = dx + r * (dy - y * jnp.mean(dy * y, axis=-1, keepdims=True))
            part = jnp.sum(dh * y, axis=0, keepdims=True)

            @pl.when(i == 0)
            def _():
                dg_ref[...] = part

            @pl.when(i > 0)
            def _():
                dg_ref[...] += part

        dx_ref[...] = dx
        dxb_ref[...] = dx.astype(BF16)

    row = pl.BlockSpec((NORM_ROWS, D), lambda i: (i, 0))
    vec = pl.BlockSpec((1, D), lambda i: (0, 0))
    in_specs = [row, row]
    args = [x, dres]
    for g, dh in pairs:
        in_specs += [vec, row]
        args += [g.reshape(1, D), dh]
    outs = pl.pallas_call(
        body,
        out_shape=[jax.ShapeDtypeStruct((S, D), F32), jax.ShapeDtypeStruct((S, D), BF16)]
        + [jax.ShapeDtypeStruct((1, D), F32)] * n,
        grid=(S // NORM_ROWS,), in_specs=in_specs, out_specs=[row, row] + [vec] * n,
        compiler_params=_cparams(("arbitrary",)), name=name,
    )(*args)
    return outs[0], outs[1], list(outs[2:])


def _loss_head(x, g, target, name):
    def body(x_ref, g_ref, t_ref, loss_ref, dx_ref, dxb_ref, dg_ref):
        i = pl.program_id(0)
        xv = x_ref[...]
        gv = g_ref[...]
        r = lax.rsqrt(jnp.mean(xv * xv, axis=-1, keepdims=True) + RMS_EPS)
        y = xv * r
        err = y * gv - t_ref[...]
        lpart = jnp.broadcast_to(jnp.sum(err * err, keepdims=True) * (0.5 / D), (1, LANES))
        dh = err * (1.0 / D)
        dy = dh * gv
        dx = r * (dy - y * jnp.mean(dy * y, axis=-1, keepdims=True))
        part = jnp.sum(dh * y, axis=0, keepdims=True)

        @pl.when(i == 0)
        def _():
            dg_ref[...] = part
            loss_ref[...] = lpart

        @pl.when(i > 0)
        def _():
            dg_ref[...] += part
            loss_ref[...] += lpart

        dx_ref[...] = dx
        dxb_ref[...] = dx.astype(BF16)

    row = pl.BlockSpec((NORM_ROWS, D), lambda i: (i, 0))
    vec = pl.BlockSpec((1, D), lambda i: (0, 0))
    return pl.pallas_call(
        body,
        out_shape=[jax.ShapeDtypeStruct((1, LANES), F32), jax.ShapeDtypeStruct((S, D), F32),
                   jax.ShapeDtypeStruct((S, D), BF16), jax.ShapeDtypeStruct((1, D), F32)],
        grid=(S // NORM_ROWS,), in_specs=[row, vec, row],
        out_specs=[pl.BlockSpec((1, LANES), lambda i: (0, 0)), row, row, vec],
        compiler_params=_cparams(("arbitrary",)), name=name,
    )(x, g.reshape(1, D), target)


SCAN_ROWS = 256


def _split3(v):
    hi = v.astype(BF16)
    r1 = v - hi.astype(F32)
    mid = r1.astype(BF16)
    lo = (r1 - mid.astype(F32)).astype(BF16)
    return hi, mid, lo


def _tri_dot(tri, v):
    hi, mid, lo = _split3(v)
    dn = _DN["nn"]
    return (lax.dot_general(tri, hi, dn, preferred_element_type=F32)
            + lax.dot_general(tri, mid, dn, preferred_element_type=F32)
            + lax.dot_general(tri, lo, dn, preferred_element_type=F32))


def _log_sigmoid(z):
    return jnp.minimum(z, 0.0) - jnp.log(1.0 + jnp.exp(-jnp.abs(z)))


def _fgate_fwd(pf, bias, name):
    tri = jnp.tril(jnp.ones((SCAN_ROWS, SCAN_ROWS), F32)).astype(BF16)

    def body(pf_ref, b_ref, tri_ref, c_ref):
        carry = jnp.zeros((1, LANES), F32)
        for blk in range(S // SCAN_ROWS):
            rows = pl.ds(blk * SCAN_ROWS, SCAN_ROWS)
            lf = _log_sigmoid(pf_ref[rows, :] + b_ref[...])
            c_ref[rows, :] = _tri_dot(tri_ref[...], lf) + carry
            carry = c_ref[pl.ds(blk * SCAN_ROWS + SCAN_ROWS - 1, 1), :]

    return pl.pallas_call(
        body, out_shape=jax.ShapeDtypeStruct((S, LANES), F32),
        compiler_params=_cparams(), name=name,
    )(pf, bias, tri)


def _fgate_bwd(pf, bias, dc_key, dc_query, name):
    triu = jnp.triu(jnp.ones((SCAN_ROWS, SCAN_ROWS), F32)).astype(BF16)

    def body(pf_ref, b_ref, dck_ref, dcq_ref, tri_ref, dpf_ref, db_ref, dlf_ref):
        carry = jnp.zeros((1, LANES), F32)
        db = jnp.zeros((1, LANES), F32)
        lane = lax.broadcasted_iota(jnp.int32, (SCAN_ROWS, LANES), 1)
        for blk in reversed(range(S // SCAN_ROWS)):
            rows = pl.ds(blk * SCAN_ROWS, SCAN_ROWS)
            dc = dck_ref[rows, :] + dcq_ref[rows, :]
            dlf_ref[rows, :] = _tri_dot(tri_ref[...], dc) + carry
            carry = dlf_ref[pl.ds(blk * SCAN_ROWS, 1), :]
            z = pf_ref[rows, :] + b_ref[...]
            e = jnp.exp(-jnp.abs(z))
            sig_neg = jnp.where(z >= 0.0, e, 1.0) / (1.0 + e)
            dz = jnp.where(lane < A_HEADS, dlf_ref[rows, :] * sig_neg, 0.0)
            dpf_ref[rows, :] = dz.astype(BF16)
            db = db + jnp.sum(dz, axis=0, keepdims=True)
        db_ref[...] = db

    return pl.pallas_call(
        body, out_shape=[jax.ShapeDtypeStruct((S, LANES), BF16), jax.ShapeDtypeStruct((1, LANES), F32)],
        scratch_shapes=[pltpu.VMEM((S, LANES), F32)],
        compiler_params=_cparams(), name=name,
    )(pf, bias, dc_key, dc_query, triu)


FOX_T = 512


def _fox_fwd(q, k, v, ccol, crow, name):
    H = q.shape[0]
    T = FOX_T
    nq = S // T

    def body(q_ref, k_ref, v_ref, cc_ref, cr_ref, o_ref, lse_ref, m_sc, l_sc, acc_sc):
        i = pl.program_id(1)
        j = pl.program_id(2)

        @pl.when(j == 0)
        def _():
            m_sc[...] = jnp.full((T, 1), NEG_INF, F32)
            l_sc[...] = jnp.zeros((T, 1), F32)
            acc_sc[...] = jnp.zeros((T, HEAD_DIM), F32)

        def step(diagonal):
            s = lax.dot_general(q_ref[...], k_ref[...], _DN["nt"], preferred_element_type=F32) * SCALE
            s = s + (cc_ref[...] - cr_ref[...])
            if diagonal:
                row = lax.broadcasted_iota(jnp.int32, (T, T), 0)
                col = lax.broadcasted_iota(jnp.int32, (T, T), 1)
                s = jnp.where(row >= col, s, NEG_INF)
            m_prev = m_sc[...]
            m_new = jnp.maximum(m_prev, jnp.max(s, axis=1, keepdims=True))
            alpha = jnp.exp(m_prev - m_new)
            p = jnp.exp(s - m_new)
            l_sc[...] = alpha * l_sc[...] + jnp.sum(p, axis=1, keepdims=True)
            acc_sc[...] = alpha * acc_sc[...] + lax.dot_general(p.astype(BF16), v_ref[...], _DN["nn"],
                                                                preferred_element_type=F32)
            m_sc[...] = m_new

        @pl.when(j < i)
        def _():
            step(False)

        @pl.when(j == i)
        def _():
            step(True)
            o_ref[...] = (acc_sc[...] / l_sc[...]).astype(BF16)
            lse_ref[...] = m_sc[...] + jnp.log(l_sc[...])

    qs = pl.BlockSpec((None, T, HEAD_DIM), lambda h, i, j: (h, i, 0))
    ks = pl.BlockSpec((None, T, HEAD_DIM), lambda h, i, j: (h, jnp.minimum(i, j), 0))
    col = pl.BlockSpec((None, T, 1), lambda h, i, j: (h, i, 0))
    rowk = pl.BlockSpec((None, 1, T), lambda h, i, j: (h, 0, jnp.minimum(i, j)))
    return pl.pallas_call(
        body, out_shape=[jax.ShapeDtypeStruct((H, S, HEAD_DIM), BF16), jax.ShapeDtypeStruct((H, S, 1), F32)],
        grid=(H, nq, nq), in_specs=[qs, ks, ks, col, rowk], out_specs=[qs, col],
        scratch_shapes=[pltpu.VMEM((T, 1), F32), pltpu.VMEM((T, 1), F32), pltpu.VMEM((T, HEAD_DIM), F32)],
        compiler_params=_cparams(("parallel", "parallel", "arbitrary")), name=name,
    )(q, k, v, ccol, crow)


def _fox_bwd(q, k, v, do, lse_row, delta_row, cq_row, ck_col, name):
    H = q.shape[0]
    T = FOX_T
    nq = S // T

    def body(q_ref, k_ref, v_ref, do_ref, lse_ref, dl_ref, cq_ref, ck_ref, dq_ref, dk_ref, dv_ref, dc_ref, dcq_ref,
             dk_sc, dv_sc, dc_sc):
        j = pl.program_id(1)
        i = pl.program_id(2)

        @pl.when(jnp.logical_and(j == 0, i == 0))
        def _():
            dq_ref[...] = jnp.zeros((S, HEAD_DIM), F32)
            dcq_ref[...] = jnp.zeros((nq, 1, T), F32)

        @pl.when(i == j)
        def _():
            dk_sc[...] = jnp.zeros((T, HEAD_DIM), F32)
            dv_sc[...] = jnp.zeros((T, HEAD_DIM), F32)
            dc_sc[...] = jnp.zeros((T, 1), F32)

        def step(diagonal):
            qv = q_ref[...]
            kv = k_ref[...]
            dov = do_ref[...]
            st = lax.dot_general(kv, qv, _DN["nt"], preferred_element_type=F32) * SCALE
            st = st + (cq_ref[...] - ck_ref[...])
            if diagonal:
                row = lax.broadcasted_iota(jnp.int32, (T, T), 0)
                col = lax.broadcasted_iota(jnp.int32, (T, T), 1)
                st = jnp.where(col >= row, st, NEG_INF)
            pt = jnp.exp(st - lse_ref[...])
            dv_sc[...] += lax.dot_general(pt.astype(BF16), dov, _DN["nn"], preferred_element_type=F32)
            dpt = lax.dot_general(v_ref[...], dov, _DN["nt"], preferred_element_type=F32)
            dst = pt * (dpt - dl_ref[...])
            dc_sc[...] -= jnp.sum(dst, axis=1, keepdims=True)
            dcq_ref[i] += jnp.sum(dst, axis=0, keepdims=True)
            dsb = (dst * SCALE).astype(BF16)
            dk_sc[...] += lax.dot_general(dsb, qv, _DN["nn"], preferred_element_type=F32)
            rows = pl.ds(pl.multiple_of(i * T, T), T)
            dq_ref[rows, :] += lax.dot_general(dsb, kv, _DN["tn"], preferred_element_type=F32)

        @pl.when(i > j)
        def _():
            step(False)

        @pl.when(i == j)
        def _():
            step(True)

        @pl.when(i == nq - 1)
        def _():
            dk_ref[...] = dk_sc[...].astype(BF16)
            dv_ref[...] = dv_sc[...].astype(BF16)
            dc_ref[...] = dc_sc[...]

    qs = pl.BlockSpec((None, T, HEAD_DIM), lambda h, j, i: (h, jnp.maximum(i, j), 0))
    qrow = pl.BlockSpec((None, 1, T), lambda h, j, i: (h, 0, jnp.maximum(i, j)))
    ks = pl.BlockSpec((None, T, HEAD_DIM), lambda h, j, i: (h, j, 0))
    kcol = pl.BlockSpec((None, T, 1), lambda h, j, i: (h, j, 0))
    dqs = pl.BlockSpec((None, S, HEAD_DIM), lambda h, j, i: (h, 0, 0))
    dcqs = pl.BlockSpec((None, nq, 1, T), lambda h, j, i: (h, 0, 0, 0))
    return pl.pallas_call(
        body,
        out_shape=[jax.ShapeDtypeStruct((H, S, HEAD_DIM), F32), jax.ShapeDtypeStruct((H, S, HEAD_DIM), BF16),
                   jax.ShapeDtypeStruct((H, S, HEAD_DIM), BF16), jax.ShapeDtypeStruct((H, S, 1), F32),
                   jax.ShapeDtypeStruct((H, nq, 1, T), F32)],
        grid=(H, nq, nq), in_specs=[qs, ks, ks, qs, qrow, qrow, qrow, kcol], out_specs=[dqs, ks, ks, kcol, dcqs],
        scratch_shapes=[pltpu.VMEM((T, HEAD_DIM), F32), pltpu.VMEM((T, HEAD_DIM), F32), pltpu.VMEM((T, 1), F32)],
        compiler_params=_cparams(("parallel", "arbitrary", "arbitrary")), name=name,
    )(q, k, v, do, lse_row, delta_row, cq_row, ck_col)


def _rowdot(a, b, name):
    H = a.shape[0]
    T = 1024

    def body(a_ref, b_ref, o_ref):
        o_ref[...] = jnp.sum(a_ref[...].astype(F32) * b_ref[...].astype(F32), axis=-1, keepdims=True)

    blk = pl.BlockSpec((None, T, HEAD_DIM), lambda h, i: (h, i, 0))
    return pl.pallas_call(
        body, out_shape=jax.ShapeDtypeStruct((H, S, 1), F32), grid=(H, S // T), in_specs=[blk, blk],
        out_specs=pl.BlockSpec((None, T, 1), lambda h, i: (h, i, 0)),
        compiler_params=_cparams(("parallel", "parallel")), name=name,
    )(a, b)


def _first_head(shape):
    return lax.broadcasted_iota(jnp.int32, shape, len(shape) - 1) < HEAD_DIM


def _each_head(x, lo):
    zero = jnp.zeros_like(x)
    return jnp.where(lo, x, zero), jnp.where(lo, zero, x)


GATE_LANES = 6


def _gate_lanes(cum):
    to_bf16_grid = lambda t: lax.reduce_precision(t, exponent_bits=8, mantissa_bits=7)
    c = cum[:, :A_HEADS]
    hi = to_bf16_grid(c)
    mid = to_bf16_grid(c - hi)
    lo = to_bf16_grid((c - hi) - mid)
    one = jnp.ones_like(hi)
    def place(cols):
        t = jnp.stack(cols, axis=-1)
        t = jnp.pad(t, ((0, 0), (0, 0), (0, HEAD_DIM - GATE_LANES)))
        return t.reshape(S, A_HEADS // 2, 2, HEAD_DIM)[:, :, ::-1, :].reshape(S, A_HEADS * HEAD_DIM).astype(BF16)
    return place([hi, mid, lo, one, one, one]), place([one, one, one, -hi, -mid, -lo])


def _fox_pair_fwd(qkv, aug_q, aug_k, name):
    T = FOX_T
    nq = S // T
    NP = A_HEADS // 2

    def body(q_ref, k_ref, v_ref, aq_ref, ak_ref, o_ref, lse_ref, m_sc, l_sc, acc_sc):
        i = pl.program_id(1)
        j = pl.program_id(2)
        lo = _first_head((T, LANES))

        @pl.when(j == 0)
        def _():
            m_sc[...] = jnp.full((2, T, LANES), NEG_INF, F32)
            l_sc[...] = jnp.zeros((2, T, LANES), F32)
            acc_sc[...] = jnp.zeros((T, LANES), F32)

        def step(diagonal):
            qs = q_ref[...] * jnp.asarray(SCALE, BF16)
            aq, ak, kv = aq_ref[...], ak_ref[...], k_ref[...]
            q2 = (jnp.where(lo, qs, aq), jnp.where(lo, aq, qs))
            k2 = (jnp.where(lo, kv, ak), jnp.where(lo, ak, kv))
            if diagonal:
                causal = lax.broadcasted_iota(jnp.int32, (T, T), 0) >= lax.broadcasted_iota(jnp.int32, (T, T), 1)
            pv, alphas = None, []
            for h, vh in enumerate(_each_head(v_ref[...], lo)):
                s = lax.dot_general(q2[h], k2[h], _DN["nt"], preferred_element_type=F32)
                if diagonal:
                    s = jnp.where(causal, s, NEG_INF)
                m_prev = m_sc[h]
                m_new = jnp.maximum(m_prev, jnp.max(s, axis=1, keepdims=True))
                alpha = jnp.exp(m_prev - m_new)
                p = jnp.exp(s - jnp.tile(m_new, (1, T // LANES)))
                l_sc[h] = alpha * l_sc[h] + jnp.sum(p, axis=1, keepdims=True)
                m_sc[h] = m_new
                d = lax.dot_general(p.astype(BF16), vh, _DN["nn"], preferred_element_type=F32)
                pv = d if pv is None else pv + d
                alphas.append(alpha)
            acc_sc[...] = jnp.where(lo, alphas[0], alphas[1]) * acc_sc[...] + pv

        @pl.when(j < i)
        def _():
            step(False)

        @pl.when(j == i)
        def _():
            step(True)
            o_ref[...] = (acc_sc[...] * jnp.where(lo, 1.0 / l_sc[0], 1.0 / l_sc[1])).astype(BF16)
            for h in range(2):
                lse_ref[h] = (m_sc[h] + jnp.log(l_sc[h]))[:, 0:1]

    qs_ = pl.BlockSpec((T, LANES), lambda p, i, j: (i, p))
    ks = pl.BlockSpec((T, LANES), lambda p, i, j: (jnp.minimum(i, j), NP + p))
    vs = pl.BlockSpec((T, LANES), lambda p, i, j: (jnp.minimum(i, j), 2 * NP + p))
    aks = pl.BlockSpec((T, LANES), lambda p, i, j: (jnp.minimum(i, j), p))
    col = pl.BlockSpec((2, T, 1), lambda p, i, j: (p, i, 0))
    return pl.pallas_call(
        body, out_shape=[jax.ShapeDtypeStruct((S, A_HEADS * HEAD_DIM), BF16), jax.ShapeDtypeStruct((A_HEADS, S, 1), F32)],
        grid=(NP, nq, nq), in_specs=[qs_, ks, vs, qs_, aks], out_specs=[qs_, col],
        scratch_shapes=[pltpu.VMEM((2, T, LANES), F32), pltpu.VMEM((2, T, LANES), F32), pltpu.VMEM((T, LANES), F32)],
        compiler_params=_cparams(("parallel", "parallel", "arbitrary")), name=name,
    )(qkv, qkv, qkv, aug_q, aug_k)


def _fox_pair_bwd(qkv, do, lse_row, delta_row, aug_q, aug_k, name):
    T = FOX_T
    nq = S // T
    NP = A_HEADS // 2

    def body(q_ref, k_ref, v_ref, do_ref, lse_ref, dl_ref, aq_ref, ak_ref, dq_ref, dk_ref, dv_ref, dc_ref, dcq_ref,
             dq_sc, dk_sc, dv_sc, dc_sc):
        j = pl.program_id(1)
        i = pl.program_id(2)
        lo = _first_head((T, LANES))

        @pl.when(jnp.logical_and(j == 0, i == 0))
        def _():
            dq_sc[...] = jnp.zeros((S, LANES), F32)
            dcq_ref[...] = jnp.zeros((2, nq, 1, T), F32)

        @pl.when(i == j)
        def _():
            dk_sc[...] = jnp.zeros((T, LANES), F32)
            dv_sc[...] = jnp.zeros((T, LANES), F32)
            dc_sc[...] = jnp.zeros((2, T, 1), F32)

        def step(diagonal):
            qv = q_ref[...]
            kv = k_ref[...]
            dov = do_ref[...].astype(BF16)
            qs = qv * jnp.asarray(SCALE, BF16)
            aq, ak = aq_ref[...], ak_ref[...]
            q2 = (jnp.where(lo, qs, aq), jnp.where(lo, aq, qs))
            k2 = (jnp.where(lo, kv, ak), jnp.where(lo, ak, kv))
            if diagonal:
                causal = lax.broadcasted_iota(jnp.int32, (T, T), 1) >= lax.broadcasted_iota(jnp.int32, (T, T), 0)
            dv = dk = dq = None
            for h, (kh, vh, qh, doh) in enumerate(zip(_each_head(kv, lo), _each_head(v_ref[...], lo),
                                                      _each_head(qv, lo), _each_head(dov, lo))):
                st = lax.dot_general(k2[h], q2[h], _DN["nt"], preferred_element_type=F32)
                if diagonal:
                    st = jnp.where(causal, st, NEG_INF)
                pt = jnp.exp(st - lse_ref[h])
                d = lax.dot_general(pt.astype(BF16), doh, _DN["nn"], preferred_element_type=F32)
                dv = d if dv is None else dv + d
                dpt = lax.dot_general(vh, dov, _DN["nt"], preferred_element_type=F32)
                dst = pt * (dpt - dl_ref[h])
                dc_sc[h] -= jnp.sum(dst, axis=1, keepdims=True)
                dcq_ref[h, i] += jnp.sum(dst, axis=0, keepdims=True)
                dsb = (dst * SCALE).astype(BF16)
                d = lax.dot_general(dsb, qh, _DN["nn"], preferred_element_type=F32)
                dk = d if dk is None else dk + d
                d = lax.dot_general(dsb, kh, _DN["tn"], preferred_element_type=F32)
                dq = d if dq is None else dq + d
            dv_sc[...] += dv
            dk_sc[...] += dk
            rows = pl.ds(pl.multiple_of(i * T, T), T)
            dq_sc[rows, :] += dq

        @pl.when(i > j)
        def _():
            step(False)

        @pl.when(i == j)
        def _():
            step(True)

        @pl.when(i == nq - 1)
        def _():
            dk_ref[...] = dk_sc[...].astype(BF16)
            dv_ref[...] = dv_sc[...].astype(BF16)
            dc_ref[...] = dc_sc[...]

        @pl.when(jnp.logical_and(j == nq - 1, i == nq - 1))
        def _():
            dq_ref[...] = dq_sc[...].astype(BF16)

    qs = pl.BlockSpec((T, LANES), lambda p, j, i: (jnp.maximum(i, j), p))
    qrow = pl.BlockSpec((2, 1, T), lambda p, j, i: (p, 0, jnp.maximum(i, j)))
    ks = pl.BlockSpec((T, LANES), lambda p, j, i: (j, NP + p))
    vs = pl.BlockSpec((T, LANES), lambda p, j, i: (j, 2 * NP + p))
    kout = pl.BlockSpec((T, LANES), lambda p, j, i: (j, p))
    kcol = pl.BlockSpec((2, T, 1), lambda p, j, i: (p, j, 0))
    dqs = pl.BlockSpec((S, LANES), lambda p, j, i: (0, p))
    dcqs = pl.BlockSpec((2, nq, 1, T), lambda p, j, i: (p, 0, 0, 0))
    wide = jax.ShapeDtypeStruct((S, A_HEADS * HEAD_DIM), BF16)
    return pl.pallas_call(
        body,
        out_shape=[wide, wide, wide, jax.ShapeDtypeStruct((A_HEADS, S, 1), F32),
                   jax.ShapeDtypeStruct((A_HEADS, nq, 1, T), F32)],
        grid=(NP, nq, nq), in_specs=[qs, ks, vs, qs, qrow, qrow, qs, kout], out_specs=[dqs, kout, kout, kcol, dcqs],
        scratch_shapes=[pltpu.VMEM((S, LANES), F32), pltpu.VMEM((T, LANES), F32), pltpu.VMEM((T, LANES), F32),
                        pltpu.VMEM((2, T, 1), F32)],
        compiler_params=_cparams(("parallel", "arbitrary", "arbitrary")), name=name,
    )(qkv, qkv, qkv, do, lse_row, delta_row, aug_q, aug_k)


def _pair_rowdot(a, b, name):
    n = a.shape[1] // HEAD_DIM
    T = 1024

    def body(a_ref, b_ref, o_ref):
        prod = a_ref[...].astype(F32) * b_ref[...].astype(F32)
        lo = _first_head(prod.shape)
        o_ref[0] = jnp.sum(jnp.where(lo, prod, 0.0), axis=1, keepdims=True)
        o_ref[1] = jnp.sum(jnp.where(lo, 0.0, prod), axis=1, keepdims=True)

    blk = pl.BlockSpec((T, LANES), lambda p, i: (i, p))
    return pl.pallas_call(
        body, out_shape=jax.ShapeDtypeStruct((n, S, 1), F32), grid=(n // 2, S // T), in_specs=[blk, blk],
        out_specs=pl.BlockSpec((2, T, 1), lambda p, i: (p, i, 0)),
        compiler_params=_cparams(("parallel", "parallel")), name=name,
    )(a, b)


W = B_WIN
N_HG = 3 * B_HPG
N_BLK = S // W


def _dil_tables():
    slopes = np.exp2((-8.0 * np.arange(1, N_HG + 1, dtype=np.float32) / N_HG).astype(np.float32)).astype(np.float32)
    dil = np.repeat(np.array([d for _, d in B_GROUPS], np.float32), B_HPG)
    coef = (slopes * dil).astype(np.float32)
    nbs = np.repeat(np.array([S // d // W for _, d in B_GROUPS], np.int32), B_HPG)
    return jnp.asarray(coef), jnp.asarray(nbs)


DIL_SUB = 8
DIL_ROWS = DIL_SUB * W
DIL_STEPS = S // DIL_ROWS


def _dil_bias(coef, transposed):
    row = lax.broadcasted_iota(jnp.int32, (W, 2 * W), 0)
    col = lax.broadcasted_iota(jnp.int32, (W, 2 * W), 1)
    dist = (col - row) if transposed else (row + W - col)
    valid = jnp.logical_and(dist >= 0, dist <= W)
    return jnp.where(valid, -coef * dist.astype(F32), NEG_INF), col


def _dil_specs():
    blk = pl.BlockSpec((None, DIL_ROWS, HEAD_DIM), lambda h, n: (h, n, 0))
    prev = pl.BlockSpec((None, W, HEAD_DIM), lambda h, n: (h, jnp.maximum(n * DIL_SUB - 1, 0), 0))
    nxt = pl.BlockSpec((None, W, HEAD_DIM), lambda h, n: (h, jnp.minimum((n + 1) * DIL_SUB, N_BLK - 1), 0))
    col = pl.BlockSpec((None, DIL_ROWS, 1), lambda h, n: (h, n, 0))
    row = pl.BlockSpec((None, 1, DIL_ROWS), lambda h, n: (h, 0, n))
    rnxt = pl.BlockSpec((None, 1, W), lambda h, n: (h, 0, jnp.minimum((n + 1) * DIL_SUB, N_BLK - 1)))
    smem = pl.BlockSpec(memory_space=pltpu.SMEM)
    return blk, prev, nxt, col, row, rnxt, smem


def _dil_fwd(q, k, v, name):
    coef_t, nbs_t = _dil_tables()

    def body(coef_ref, nbs_ref, q_ref, kh_ref, k_ref, vh_ref, v_ref, o_ref, lse_ref, kf, vf):
        hg = pl.program_id(0)
        n = pl.program_id(1)
        nbs = nbs_ref[hg]
        kf[0:W, :] = kh_ref[...]
        kf[W:, :] = k_ref[...]
        vf[0:W, :] = vh_ref[...]
        vf[W:, :] = v_ref[...]
        bias, col = _dil_bias(coef_ref[hg], False)
        for b in range(DIL_SUB):
            first = lax.rem(n * DIL_SUB + b, nbs) == 0
            rows = slice(b * W, (b + 1) * W)
            both = slice(b * W, (b + 2) * W)
            s = lax.dot_general(q_ref[rows, :], kf[both, :], _DN["nt"], preferred_element_type=F32) * SCALE + bias
            s = jnp.where(jnp.logical_and(first, col < W), NEG_INF, s)
            m = jnp.max(s, axis=1, keepdims=True)
            p = jnp.exp(s - m)
            l = jnp.sum(p, axis=1, keepdims=True)
            acc = lax.dot_general(p.astype(BF16), vf[both, :], _DN["nn"], preferred_element_type=F32)
            o_ref[rows, :] = acc / l
            lse_ref[rows, :] = m + jnp.log(l)

    blk, prev, _, col, _, _, smem = _dil_specs()
    return pl.pallas_call(
        body, out_shape=[jax.ShapeDtypeStruct((N_HG, S, HEAD_DIM), F32), jax.ShapeDtypeStruct((N_HG, S, 1), F32)],
        grid=(N_HG, DIL_STEPS), in_specs=[smem, smem, blk, prev, blk, prev, blk], out_specs=[blk, col],
        scratch_shapes=[pltpu.VMEM((DIL_ROWS + W, HEAD_DIM), BF16)] * 2,
        compiler_params=_cparams(("parallel", "parallel")), name=name,
    )(coef_t, nbs_t, q, k, k, v, v)


def _dil_merge(o, lse, name):
    T = 1024

    def body(o_ref, lse_ref, om_ref, omb_ref, l_ref):
        l0, l1, l2 = lse_ref[0], lse_ref[1], lse_ref[2]
        m = jnp.maximum(jnp.maximum(l0, l1), l2)
        e0, e1, e2 = jnp.exp(l0 - m), jnp.exp(l1 - m), jnp.exp(l2 - m)
        den = e0 + e1 + e2
        om = (e0 / den) * o_ref[0] + (e1 / den) * o_ref[1] + (e2 / den) * o_ref[2]
        om_ref[...] = om
        omb_ref[...] = om.astype(BF16)
        l_ref[...] = m + jnp.log(den)

    ob = pl.BlockSpec((None, T, HEAD_DIM), lambda h, i: (h, i, 0))
    lb = pl.BlockSpec((None, T, 1), lambda h, i: (h, i, 0))
    return pl.pallas_call(
        body,
        out_shape=[jax.ShapeDtypeStruct((B_HPG, S, HEAD_DIM), F32), jax.ShapeDtypeStruct((B_HPG, S, HEAD_DIM), BF16),
                   jax.ShapeDtypeStruct((B_HPG, S, 1), F32)],
        grid=(B_HPG, S // T),
        in_specs=[pl.BlockSpec((3, None, T, HEAD_DIM), lambda h, i: (0, h, i, 0)),
                  pl.BlockSpec((3, None, T, 1), lambda h, i: (0, h, i, 0))],
        out_specs=[ob, ob, lb], compiler_params=_cparams(("parallel", "parallel")), name=name,
    )(o, lse)


def _dil_bwd_dq(q, k, v, do, lcol, dcol, name):
    coef_t, nbs_t = _dil_tables()

    def body(coef_ref, nbs_ref, q_ref, kh_ref, k_ref, vh_ref, v_ref, do_ref, l_ref, d_ref, dq_ref, kf, vf):
        hg = pl.program_id(0)
        n = pl.program_id(1)
        nbs = nbs_ref[hg]
        kf[0:W, :] = kh_ref[...]
        kf[W:, :] = k_ref[...]
        vf[0:W, :] = vh_ref[...]
        vf[W:, :] = v_ref[...]
        bias, col = _dil_bias(coef_ref[hg], False)
        for b in range(DIL_SUB):
            first = lax.rem(n * DIL_SUB + b, nbs) == 0
            rows = slice(b * W, (b + 1) * W)
            both = slice(b * W, (b + 2) * W)
            kk = kf[both, :]
            s = lax.dot_general(q_ref[rows, :], kk, _DN["nt"], preferred_element_type=F32) * SCALE + bias
            s = jnp.where(jnp.logical_and(first, col < W), NEG_INF, s)
            p = jnp.exp(s - l_ref[rows, :])
            dp = lax.dot_general(do_ref[rows, :], vf[both, :], _DN["nt"], preferred_element_type=F32)
            ds = (p * (dp - d_ref[rows, :]) * SCALE).astype(BF16)
            dq_ref[rows, :] = lax.dot_general(ds, kk, _DN["nn"], preferred_element_type=F32).astype(BF16)

    blk, prev, _, col, _, _, smem = _dil_specs()
    return pl.pallas_call(
        body, out_shape=jax.ShapeDtypeStruct((N_HG, S, HEAD_DIM), BF16), grid=(N_HG, DIL_STEPS),
        in_specs=[smem, smem, blk, prev, blk, prev, blk, blk, col, col], out_specs=blk,
        scratch_shapes=[pltpu.VMEM((DIL_ROWS + W, HEAD_DIM), BF16)] * 2,
        compiler_params=_cparams(("parallel", "parallel")), name=name,
    )(coef_t, nbs_t, q, k, k, v, v, do, lcol, dcol)


def _dil_bwd_dkv(q, k, v, do, lrow, drow, name):
    coef_t, nbs_t = _dil_tables()

    def body(coef_ref, nbs_ref, k_ref, v_ref, q_ref, qn_ref, do_ref, don_ref, l_ref, ln_ref, d_ref, dn_ref,
             dk_ref, dv_ref, qf, dof, lf, df):
        hg = pl.program_id(0)
        n = pl.program_id(1)
        nbs = nbs_ref[hg]
        qf[0:DIL_ROWS, :] = q_ref[...]
        qf[DIL_ROWS:, :] = qn_ref[...]
        dof[0:DIL_ROWS, :] = do_ref[...]
        dof[DIL_ROWS:, :] = don_ref[...]
        lf[:, 0:DIL_ROWS] = l_ref[...]
        lf[:, DIL_ROWS:] = ln_ref[...]
        df[:, 0:DIL_ROWS] = d_ref[...]
        df[:, DIL_ROWS:] = dn_ref[...]
        bias, col = _dil_bias(coef_ref[hg], True)
        for b in range(DIL_SUB):
            no_next = lax.rem(n * DIL_SUB + b + 1, nbs) == 0
            rows = slice(b * W, (b + 1) * W)
            both = slice(b * W, (b + 2) * W)
            qq = qf[both, :]
            dd = dof[both, :]
            st = lax.dot_general(k_ref[rows, :], qq, _DN["nt"], preferred_element_type=F32) * SCALE + bias
            st = jnp.where(jnp.logical_and(no_next, col >= W), NEG_INF, st)
            pt = jnp.exp(st - lf[:, both])
            dv_ref[rows, :] = lax.dot_general(pt.astype(BF16), dd, _DN["nn"], preferred_element_type=F32).astype(BF16)
            dpt = lax.dot_general(v_ref[rows, :], dd, _DN["nt"], preferred_element_type=F32)
            dst = (pt * (dpt - df[:, both]) * SCALE).astype(BF16)
            dk_ref[rows, :] = lax.dot_general(dst, qq, _DN["nn"], preferred_element_type=F32).astype(BF16)

    blk, _, nxt, _, row, rnxt, smem = _dil_specs()
    return pl.pallas_call(
        body, out_shape=[jax.ShapeDtypeStruct((N_HG, S, HEAD_DIM), BF16)] * 2, grid=(N_HG, DIL_STEPS),
        in_specs=[smem, smem, blk, blk, blk, nxt, blk, nxt, row, rnxt, row, rnxt], out_specs=[blk, blk],
        scratch_shapes=[pltpu.VMEM((DIL_ROWS + W, HEAD_DIM), BF16)] * 2 + [pltpu.VMEM((1, DIL_ROWS + W), F32)] * 2,
        compiler_params=_cparams(("parallel", "parallel")), name=name,
    )(coef_t, nbs_t, k, v, q, q, do, do, lrow, lrow, drow, drow)


NPG = B_HPG // 2
GROUP_W = B_HPG * HEAD_DIM


def _dil_pair_specs(qoff, koff, voff):
    prev_blk = lambda n: jnp.maximum(n * DIL_SUB - 1, 0)
    next_blk = lambda n: jnp.minimum((n + 1) * DIL_SUB, N_BLK - 1)
    return dict(
        o=pl.BlockSpec((DIL_ROWS, LANES), lambda h, n: (n, h)),
        o_next=pl.BlockSpec((W, LANES), lambda h, n: (next_blk(n), h)),
        q=pl.BlockSpec((DIL_ROWS, LANES), lambda h, n: (n, qoff + h)),
        q_next=pl.BlockSpec((W, LANES), lambda h, n: (next_blk(n), qoff + h)),
        k=pl.BlockSpec((DIL_ROWS, LANES), lambda h, n: (n, koff + h)),
        k_prev=pl.BlockSpec((W, LANES), lambda h, n: (prev_blk(n), koff + h)),
        v=pl.BlockSpec((DIL_ROWS, LANES), lambda h, n: (n, voff + h)),
        v_prev=pl.BlockSpec((W, LANES), lambda h, n: (prev_blk(n), voff + h)),
        col=pl.BlockSpec((2, DIL_ROWS, 1), lambda h, n: (h, n, 0)),
        row=pl.BlockSpec((2, 1, DIL_ROWS), lambda h, n: (h, 0, n)),
        row_next=pl.BlockSpec((2, 1, W), lambda h, n: (h, 0, next_blk(n))),
        smem=pl.BlockSpec(memory_space=pltpu.SMEM))


def _dil_pair_fwd(g, q, k, v, qoff, koff, voff, name):
    coef_t, nbs_t = _dil_tables()

    def body(coef_ref, nbs_ref, q_ref, kh_ref, k_ref, vh_ref, v_ref, o_ref, lse_ref, kf, vf):
        hp = pl.program_id(0)
        n = pl.program_id(1)
        nbs = nbs_ref[B_HPG * g + 2 * hp]
        kf[0:W, :] = kh_ref[...]
        kf[W:, :] = k_ref[...]
        vf[0:W, :] = vh_ref[...]
        vf[W:, :] = v_ref[...]
        biases = [_dil_bias(coef_ref[B_HPG * g + 2 * hp + h], False) for h in range(2)]
        col = biases[0][1]
        lo = _first_head((W, LANES))
        lo2 = _first_head((2 * W, LANES))
        for b in range(DIL_SUB):
            first = lax.rem(n * DIL_SUB + b, nbs) == 0
            rows = slice(b * W, (b + 1) * W)
            both = slice(b * W, (b + 2) * W)
            qv = q_ref[rows, :]
            acc, inv = None, []
            for h, (kh, vh) in enumerate(zip(_each_head(kf[both, :], lo2), _each_head(vf[both, :], lo2))):
                s = lax.dot_general(qv, kh, _DN["nt"], preferred_element_type=F32) * SCALE + biases[h][0]
                s = jnp.where(jnp.logical_and(first, col < W), NEG_INF, s)
                m = jnp.max(s, axis=1, keepdims=True)
                p = jnp.exp(s - m)
                l = jnp.sum(p, axis=1, keepdims=True)
                d = lax.dot_general(p.astype(BF16), vh, _DN["nn"], preferred_element_type=F32)
                acc = d if acc is None else acc + d
                inv.append(1.0 / l)
                lse_ref[h, rows, :] = m + jnp.log(l)
            o_ref[rows, :] = acc * jnp.where(lo, inv[0], inv[1])

    sp = _dil_pair_specs(qoff, koff, voff)
    return pl.pallas_call(
        body, out_shape=[jax.ShapeDtypeStruct((S, GROUP_W), F32), jax.ShapeDtypeStruct((B_HPG, S, 1), F32)],
        grid=(NPG, DIL_STEPS), in_specs=[sp["smem"], sp["smem"], sp["q"], sp["k_prev"], sp["k"], sp["v_prev"], sp["v"]],
        out_specs=[sp["o"], sp["col"]], scratch_shapes=[pltpu.VMEM((DIL_ROWS + W, LANES), BF16)] * 2,
        compiler_params=_cparams(("parallel", "parallel")), name=name,
    )(coef_t, nbs_t, q, k, k, v, v)


def _dil_pair_merge(os, lses, name):
    T = 1024

    def body(o0_ref, o1_ref, o2_ref, l0_ref, l1_ref, l2_ref, om_ref, omb_ref, l_ref):
        lo = _first_head((T, LANES))
        weights = []
        for h in range(2):
            l0, l1, l2 = l0_ref[h], l1_ref[h], l2_ref[h]
            m = jnp.maximum(jnp.maximum(l0, l1), l2)
            e0, e1, e2 = jnp.exp(l0 - m), jnp.exp(l1 - m), jnp.exp(l2 - m)
            den = e0 + e1 + e2
            weights.append((e0 / den, e1 / den, e2 / den))
            l_ref[h] = m + jnp.log(den)
        om = (jnp.where(lo, weights[0][0], weights[1][0]) * o0_ref[...]
              + jnp.where(lo, weights[0][1], weights[1][1]) * o1_ref[...]
              + jnp.where(lo, weights[0][2], weights[1][2]) * o2_ref[...])
        om_ref[...] = om
        omb_ref[...] = om.astype(BF16)

    ob = pl.BlockSpec((T, LANES), lambda p, i: (i, p))
    lb = pl.BlockSpec((2, T, 1), lambda p, i: (p, i, 0))
    return pl.pallas_call(
        body,
        out_shape=[jax.ShapeDtypeStruct((S, B_OUT_W), F32), jax.ShapeDtypeStruct((S, B_OUT_W), BF16),
                   jax.ShapeDtypeStruct((B_HPG, S, 1), F32)],
        grid=(NPG, S // T), in_specs=[ob] * 3 + [lb] * 3, out_specs=[ob, ob, lb],
        compiler_params=_cparams(("parallel", "parallel")), name=name,
    )(*os, *lses)


def _dil_pair_dq(g, q, k, v, qoff, koff, voff, do, lcol, dcol, name):
    coef_t, nbs_t = _dil_tables()

    def body(coef_ref, nbs_ref, q_ref, kh_ref, k_ref, vh_ref, v_ref, do_ref, l_ref, d_ref, dq_ref, kf, vf):
        hp = pl.program_id(0)
        n = pl.program_id(1)
        nbs = nbs_ref[B_HPG * g + 2 * hp]
        kf[0:W, :] = kh_ref[...]
        kf[W:, :] = k_ref[...]
        vf[0:W, :] = vh_ref[...]
        vf[W:, :] = v_ref[...]
        biases = [_dil_bias(coef_ref[B_HPG * g + 2 * hp + h], False) for h in range(2)]
        col = biases[0][1]
        lo2 = _first_head((2 * W, LANES))
        for b in range(DIL_SUB):
            first = lax.rem(n * DIL_SUB + b, nbs) == 0
            rows = slice(b * W, (b + 1) * W)
            both = slice(b * W, (b + 2) * W)
            qv = q_ref[rows, :]
            dov = do_ref[rows, :]
            acc = None
            for h, (kh, vh) in enumerate(zip(_each_head(kf[both, :], lo2), _each_head(vf[both, :], lo2))):
                s = lax.dot_general(qv, kh, _DN["nt"], preferred_element_type=F32) * SCALE + biases[h][0]
                s = jnp.where(jnp.logical_and(first, col < W), NEG_INF, s)
                p = jnp.exp(s - l_ref[h, rows, :])
                dp = lax.dot_general(dov, vh, _DN["nt"], preferred_element_type=F32)
                ds = (p * (dp - d_ref[h, rows, :]) * SCALE).astype(BF16)
                d = lax.dot_general(ds, kh, _DN["nn"], preferred_element_type=F32)
                acc = d if acc is None else acc + d
            dq_ref[rows, :] = acc.astype(BF16)

    sp = _dil_pair_specs(qoff, koff, voff)
    return pl.pallas_call(
        body, out_shape=jax.ShapeDtypeStruct((S, GROUP_W), BF16), grid=(NPG, DIL_STEPS),
        in_specs=[sp["smem"], sp["smem"], sp["q"], sp["k_prev"], sp["k"], sp["v_prev"], sp["v"], sp["o"], sp["col"],
                  sp["col"]],
        out_specs=sp["o"], scratch_shapes=[pltpu.VMEM((DIL_ROWS + W, LANES), BF16)] * 2,
        compiler_params=_cparams(("parallel", "parallel")), name=name,
    )(coef_t, nbs_t, q, k, k, v, v, do, lcol, dcol)


def _dil_pair_dkv(g, q, k, v, qoff, koff, voff, do, lrow, drow, name):
    coef_t, nbs_t = _dil_tables()

    def body(coef_ref, nbs_ref, k_ref, v_ref, q_ref, qn_ref, do_ref, don_ref, l_ref, ln_ref, d_ref, dn_ref,
             dk_ref, dv_ref, qf, dof, lf, df):
        hp = pl.program_id(0)
        n = pl.program_id(1)
        nbs = nbs_ref[B_HPG * g + 2 * hp]
        qf[0:DIL_ROWS, :] = q_ref[...]
        qf[DIL_ROWS:, :] = qn_ref[...]
        dof[0:DIL_ROWS, :] = do_ref[...]
        dof[DIL_ROWS:, :] = don_ref[...]
        lf[:, :, 0:DIL_ROWS] = l_ref[...]
        lf[:, :, DIL_ROWS:] = ln_ref[...]
        df[:, :, 0:DIL_ROWS] = d_ref[...]
        df[:, :, DIL_ROWS:] = dn_ref[...]
        biases = [_dil_bias(coef_ref[B_HPG * g + 2 * hp + h], True) for h in range(2)]
        col = biases[0][1]
        lo = _first_head((W, LANES))
        lo2 = _first_head((2 * W, LANES))
        for b in range(DIL_SUB):
            no_next = lax.rem(n * DIL_SUB + b + 1, nbs) == 0
            rows = slice(b * W, (b + 1) * W)
            both = slice(b * W, (b + 2) * W)
            dd = dof[both, :]
            dk = dv = None
            for h, (kh, vh, qh, ddh) in enumerate(zip(_each_head(k_ref[rows, :], lo), _each_head(v_ref[rows, :], lo),
                                                      _each_head(qf[both, :], lo2), _each_head(dd, lo2))):
                st = lax.dot_general(kh, qh, _DN["nt"], preferred_element_type=F32) * SCALE + biases[h][0]
                st = jnp.where(jnp.logical_and(no_next, col >= W), NEG_INF, st)
                pt = jnp.exp(st - lf[h, :, both])
                d = lax.dot_general(pt.astype(BF16), ddh, _DN["nn"], preferred_element_type=F32)
                dv = d if dv is None else dv + d
                dpt = lax.dot_general(vh, dd, _DN["nt"], preferred_element_type=F32)
                dst = (pt * (dpt - df[h, :, both]) * SCALE).astype(BF16)
                d = lax.dot_general(dst, qh, _DN["nn"], preferred_element_type=F32)
                dk = d if dk is None else dk + d
            dk_ref[rows, :] = dk.astype(BF16)
            dv_ref[rows, :] = dv.astype(BF16)

    sp = _dil_pair_specs(qoff, koff, voff)
    wide = jax.ShapeDtypeStruct((S, GROUP_W), BF16)
    return pl.pallas_call(
        body, out_shape=[wide, wide], grid=(NPG, DIL_STEPS),
        in_specs=[sp["smem"], sp["smem"], sp["k"], sp["v"], sp["q"], sp["q_next"], sp["o"], sp["o_next"], sp["row"],
                  sp["row_next"], sp["row"], sp["row_next"]],
        out_specs=[sp["o"], sp["o"]],
        scratch_shapes=[pltpu.VMEM((DIL_ROWS + W, LANES), BF16)] * 2 + [pltpu.VMEM((2, 1, DIL_ROWS + W), F32)] * 2,
        compiler_params=_cparams(("parallel", "parallel")), name=name,
    )(coef_t, nbs_t, k, v, q, q, do, do, lrow, lrow, drow, drow)


FFN_ROWS = 512
FFN_COLS = 256
HALO = 8


def _shifted(u, halo, back):
    T = u.shape[0]
    rows = lax.broadcasted_iota(jnp.int32, u.shape, 0)
    if back:
        s1 = jnp.where(rows == 0, halo[HALO - 1:HALO, :], pltpu.roll(u, 1, 0))
        s2 = jnp.where(rows == 0, halo[HALO - 2:HALO - 1, :],
                       jnp.where(rows == 1, halo[HALO - 1:HALO, :], pltpu.roll(u, 2, 0)))
    else:
        s1 = jnp.where(rows == T - 1, halo[0:1, :], pltpu.roll(u, T - 1, 0))
        s2 = jnp.where(rows == T - 1, halo[1:2, :],
                       jnp.where(rows == T - 2, halo[0:1, :], pltpu.roll(u, T - 2, 0)))
    return s1, s2


def _conv_parts(u_ref, h_ref, w_ref, b_ref, first):
    out = []
    for p in range(2):
        u = u_ref[p]
        halo = jnp.where(first, 0.0, h_ref[p])
        u1, u2 = _shifted(u, halo, True)
        w = w_ref[p]
        out.append((w[0:1, :] * u2 + w[1:2, :] * u1 + w[2:3, :] * u + b_ref[p], u1, u2, u))
    return out


def _ffn_specs():
    T, C = FFN_ROWS, FFN_COLS
    blk = pl.BlockSpec((2, T, C), lambda j, i: (0, i, j))
    prev = pl.BlockSpec((2, HALO, C), lambda j, i: (0, jnp.maximum(i * (T // HALO) - 1, 0), j))
    nxt = pl.BlockSpec((2, HALO, C), lambda j, i: (0, jnp.minimum((i + 1) * (T // HALO), S // HALO - 1), j))
    wsp = pl.BlockSpec((2, 3, C), lambda j, i: (0, 0, j))
    bsp = pl.BlockSpec((2, 1, C), lambda j, i: (0, 0, j))
    one = pl.BlockSpec((T, C), lambda j, i: (i, j))
    return blk, prev, nxt, wsp, bsp, one


def _ffn_act_fwd(u, w, b, name):
    blk, prev, _, wsp, bsp, one = _ffn_specs()

    def body(u_ref, h_ref, w_ref, b_ref, o_ref):
        (a, _, _, _), (g, _, _, _) = _conv_parts(u_ref, h_ref, w_ref, b_ref, pl.program_id(1) == 0)
        o_ref[...] = (g / (1.0 + jnp.exp(-g)) * a).astype(BF16)

    return pl.pallas_call(
        body, out_shape=jax.ShapeDtypeStruct((S, FF), BF16), grid=(FF // FFN_COLS, S // FFN_ROWS),
        in_specs=[blk, prev, wsp, bsp], out_specs=one,
        compiler_params=_cparams(("parallel", "parallel")), name=name,
    )(u, u, w, b)


def _ffn_act_bwd(u, dact, w, b, name):
    blk, prev, _, wsp, bsp, one = _ffn_specs()

    def body(u_ref, h_ref, da_ref, w_ref, b_ref, duc_ref, dwb_ref):
        i = pl.program_id(1)
        (a, a1, a2, a0), (g, g1, g2, g0) = _conv_parts(u_ref, h_ref, w_ref, b_ref, i == 0)
        dact_v = da_ref[...]
        sg = 1.0 / (1.0 + jnp.exp(-g))
        d_a = dact_v * (g * sg)
        d_g = dact_v * a * (sg * (1.0 + g * (1.0 - sg)))
        duc_ref[0] = d_a
        duc_ref[1] = d_g

        @pl.when(i == 0)
        def _():
            dwb_ref[...] = jnp.zeros(dwb_ref.shape, F32)

        for p, (d, s2, s1, s0) in enumerate(((d_a, a2, a1, a0), (d_g, g2, g1, g0))):
            dwb_ref[p, 0:1, :] += jnp.sum(d * s2, axis=0, keepdims=True)
            dwb_ref[p, 1:2, :] += jnp.sum(d * s1, axis=0, keepdims=True)
            dwb_ref[p, 2:3, :] += jnp.sum(d * s0, axis=0, keepdims=True)
            dwb_ref[p, 3:4, :] += jnp.sum(d, axis=0, keepdims=True)

    return pl.pallas_call(
        body, out_shape=[jax.ShapeDtypeStruct((2, S, FF), F32), jax.ShapeDtypeStruct((2, 8, FF), F32)],
        grid=(FF // FFN_COLS, S // FFN_ROWS), in_specs=[blk, prev, one, wsp, bsp],
        out_specs=[blk, pl.BlockSpec((2, 8, FFN_COLS), lambda j, i: (0, 0, j))],
        compiler_params=_cparams(("parallel", "arbitrary")), name=name,
    )(u, u, dact, w, b)


def _ffn_conv_bwd(duc, w, name):
    blk, _, nxt, wsp, _, _ = _ffn_specs()
    last = S // FFN_ROWS - 1

    def body(d_ref, h_ref, w_ref, du_ref):
        is_last = pl.program_id(1) == last
        for p in range(2):
            d = d_ref[p]
            halo = jnp.where(is_last, 0.0, h_ref[p])
            d1, d2 = _shifted(d, halo, False)
            wv = w_ref[p]
            du_ref[p] = (wv[2:3, :] * d + wv[1:2, :] * d1 + wv[0:1, :] * d2).astype(BF16)

    return pl.pallas_call(
        body, out_shape=jax.ShapeDtypeStruct((2, S, FF), BF16), grid=(FF // FFN_COLS, S // FFN_ROWS),
        in_specs=[blk, nxt, wsp], out_specs=blk,
        compiler_params=_cparams(("parallel", "parallel")), name=name,
    )(duc, duc, w)


def _adam_update(w, gv, m, v):
    c1 = 1.0 / (1.0 - ADAM_B1 ** ADAM_STEP)
    c2 = 1.0 / (1.0 - ADAM_B2 ** ADAM_STEP)
    mn = ADAM_B1 * m + (1.0 - ADAM_B1) * gv
    vn = ADAM_B2 * v + (1.0 - ADAM_B2) * (gv * gv)
    return -ADAM_LR * ((mn * c1) / (jnp.sqrt(vn * c2) + ADAM_EPS) + ADAM_WD * w), mn, vn


def _adamw(w, g, m, v, name):
    rows = w.shape[0]
    T = 8
    for cand in (256, 128, 64, 32, 16, 8):
        if rows % cand == 0:
            T = cand
            break

    def body(w_ref, g_ref, m_ref, v_ref, d_ref, mo_ref, vo_ref):
        d_ref[...], mo_ref[...], vo_ref[...] = _adam_update(w_ref[...], g_ref[...], m_ref[...], v_ref[...])

    blk = pl.BlockSpec((T, w.shape[1]), lambda i: (i, 0))
    sds = jax.ShapeDtypeStruct(w.shape, F32)
    return pl.pallas_call(
        body, out_shape=[sds, sds, sds], grid=(rows // T,), in_specs=[blk] * 4, out_specs=[blk] * 3,
        compiler_params=_cparams(("parallel",)), name=name,
    )(w, g, m, v)


ANY = pl.BlockSpec(memory_space=pl.ANY)


def _place():
    x, y, c = lax.axis_index("x"), lax.axis_index("y"), lax.axis_index("c")
    chips = [(1 - x, y), (x, 1 - y), (1 - x, 1 - y)]
    return x, y, c, chips


def _place_own(w, k_arr):
    T = FLAT_T

    def body(k_ref, w_ref, o_ref):
        o_ref[...] = w_ref[...]

    return pl.pallas_call(
        body, out_shape=jax.ShapeDtypeStruct((N_CHIPS, FLAT_ROWS, FLAT_W), w.dtype),
        grid_spec=pltpu.PrefetchScalarGridSpec(
            num_scalar_prefetch=1, grid=(FLAT_ROWS // T,),
            in_specs=[pl.BlockSpec((T, FLAT_W), lambda i, k: (i, 0))],
            out_specs=pl.BlockSpec((None, T, FLAT_W), lambda i, k: (k[0], i, 0))),
        compiler_params=_cparams(("parallel",)), name="gather_place_own",
    )(k_arr, w)


def _allgather_shards(w, buf):
    def body(w_ref, buf_ref, g_ref, send_sems, recv_sems):
        x, y, c, chips = _place()
        myk = 2 * x + y
        sibling = (x, y, 1 - c)
        h0 = pl.multiple_of(c * HALF_ROWS, 16)
        h1 = pl.multiple_of((1 - c) * HALF_ROWS, 16)

        def half(k, start):
            return g_ref.at[k, pl.ds(start, HALF_ROWS), :]

        def rcopy(sem, src, dst, to):
            return pltpu.make_async_remote_copy(src_ref=src, dst_ref=dst, send_sem=send_sems.at[sem],
                                                recv_sem=recv_sems.at[sem], device_id=to, device_id_type=MESH)

        ici = [rcopy(r, w_ref.at[pl.ds(h0, HALF_ROWS), :], half(myk, h0), (*chip, c)) for r, chip in enumerate(chips)]
        for cp in ici:
            cp.start()
        ks = [2 * cx + cy for cx, cy in chips]
        fwd = [rcopy(3 + r, half(ks[r], h0), half(ks[r], h0), sibling) for r in range(3)]
        for r in range(3):
            rcopy(r, half(ks[r], h0), half(ks[r], h0), (*chips[r], c)).wait_recv()
            fwd[r].start()
        for r in range(3):
            rcopy(3 + r, half(ks[r], h1), half(ks[r], h1), sibling).wait_recv()
        for cp in ici + fwd:
            cp.wait_send()

    return pl.pallas_call(
        body, out_shape=jax.ShapeDtypeStruct((N_CHIPS, FLAT_ROWS, FLAT_W), w.dtype), in_specs=[ANY, ANY], out_specs=ANY,
        scratch_shapes=[pltpu.SemaphoreType.DMA((6,)), pltpu.SemaphoreType.DMA((6,))],
        input_output_aliases={1: 0},
        compiler_params=pltpu.CompilerParams(has_side_effects=True), name="allgather_shards",
    )(w, buf)


def _sibling_swap_half(g):
    def body(g_ref, o_ref, send_sem, recv_sem):
        x, y, c, _ = _place()
        theirs = pl.multiple_of((1 - c) * HALF_ROWS, 8)
        cp = pltpu.make_async_remote_copy(src_ref=g_ref.at[:, pl.ds(theirs, HALF_ROWS), :], dst_ref=o_ref,
                                          send_sem=send_sem, recv_sem=recv_sem, device_id=(x, y, 1 - c),
                                          device_id_type=MESH)
        cp.start()
        cp.wait()

    return pl.pallas_call(
        body, out_shape=jax.ShapeDtypeStruct((N_CHIPS, HALF_ROWS, FLAT_W), F32), in_specs=[ANY], out_specs=ANY,
        scratch_shapes=[pltpu.SemaphoreType.DMA, pltpu.SemaphoreType.DMA],
        compiler_params=pltpu.CompilerParams(has_side_effects=True), name="rs_sibling_swap",
    )(g)


def _pair_sum(g, other, c_arr):
    T = FLAT_T

    def body(c_ref, g_ref, o_ref, s_ref):
        s_ref[...] = (g_ref[...] + o_ref[...]).astype(BF16)

    nb = HALF_ROWS // T
    return pl.pallas_call(
        body, out_shape=jax.ShapeDtypeStruct((N_CHIPS, HALF_ROWS, FLAT_W), BF16),
        grid_spec=pltpu.PrefetchScalarGridSpec(
            num_scalar_prefetch=1, grid=(N_CHIPS, nb),
            in_specs=[pl.BlockSpec((None, T, FLAT_W), lambda k, i, c: (k, c[0] * nb + i, 0)),
                      pl.BlockSpec((None, T, FLAT_W), lambda k, i, c: (k, i, 0))],
            out_specs=pl.BlockSpec((None, T, FLAT_W), lambda k, i, c: (k, i, 0))),
        compiler_params=_cparams(("parallel", "parallel")), name="rs_pair_sum",
    )(c_arr, g, other)


def _chip_exchange(s):
    def body(s_ref, o_ref, send_sems, recv_sems):
        x, y, c, chips = _place()
        cps = []
        for r, (cx, cy) in enumerate(chips):
            cps.append(pltpu.make_async_remote_copy(
                src_ref=s_ref.at[2 * cx + cy], dst_ref=o_ref.at[r], send_sem=send_sems.at[r],
                recv_sem=recv_sems.at[r], device_id=(cx, cy, c), device_id_type=MESH))
        for cp in cps:
            cp.start()
        for cp in cps:
            cp.wait()

    return pl.pallas_call(
        body, out_shape=jax.ShapeDtypeStruct((3, HALF_ROWS, FLAT_W), BF16), in_specs=[ANY], out_specs=ANY,
        scratch_shapes=[pltpu.SemaphoreType.DMA((3,)), pltpu.SemaphoreType.DMA((3,))],
        compiler_params=pltpu.CompilerParams(has_side_effects=True), name="rs_chip_exchange",
    )(s)


def _chip_sum(s, r, k_arr):
    T = FLAT_T

    def body(k_ref, s_ref, r_ref, o_ref):
        o_ref[...] = ((s_ref[...].astype(F32) + r_ref[0].astype(F32)) + r_ref[1].astype(F32)) + r_ref[2].astype(F32)

    return pl.pallas_call(
        body, out_shape=jax.ShapeDtypeStruct((HALF_ROWS, FLAT_W), F32),
        grid_spec=pltpu.PrefetchScalarGridSpec(
            num_scalar_prefetch=1, grid=(HALF_ROWS // T,),
            in_specs=[pl.BlockSpec((None, T, FLAT_W), lambda i, k: (k[0], i, 0)),
                      pl.BlockSpec((3, T, FLAT_W), lambda i, k: (0, i, 0))],
            out_specs=pl.BlockSpec((T, FLAT_W), lambda i, k: (i, 0))),
        compiler_params=_cparams(("parallel",)), name="rs_chip_sum",
    )(k_arr, s, r)


def _sibling_send(t):
    def body(t_ref, o_ref, send_sem, recv_sem):
        x, y, c, _ = _place()
        cp = pltpu.make_async_remote_copy(src_ref=t_ref, dst_ref=o_ref, send_sem=send_sem, recv_sem=recv_sem,
                                          device_id=(x, y, 1 - c), device_id_type=MESH)
        cp.start()
        cp.wait()

    return pl.pallas_call(
        body, out_shape=jax.ShapeDtypeStruct((HALF_ROWS, FLAT_W), F32), in_specs=[ANY], out_specs=ANY,
        scratch_shapes=[pltpu.SemaphoreType.DMA, pltpu.SemaphoreType.DMA],
        compiler_params=pltpu.CompilerParams(has_side_effects=True), name="rs_sibling_send",
    )(t)


def _allreduce_small(v):
    def body(v_ref, o_ref, buf, send_sems, recv_sems):
        x, y, c, _ = _place()
        me = 4 * x + 2 * y + c
        buf[me] = v_ref[...]
        cps = []
        for mask in range(1, 8):
            a, b, d = (mask >> 2) & 1, (mask >> 1) & 1, mask & 1
            peer = (x + a - 2 * a * x, y + b - 2 * b * y, c + d - 2 * d * c)
            cps.append(pltpu.make_async_remote_copy(
                src_ref=v_ref, dst_ref=buf.at[me], send_sem=send_sems.at[mask - 1], recv_sem=recv_sems.at[mask - 1],
                device_id=peer, device_id_type=MESH))
        for cp in cps:
            cp.start()
        for cp in cps:
            cp.wait()
        total = buf[0]
        for dev in range(1, 8):
            total = total + buf[dev]
        o_ref[...] = total

    vm = pl.BlockSpec(memory_space=pltpu.VMEM)
    return pl.pallas_call(
        body, out_shape=jax.ShapeDtypeStruct((SMALL_ROWS, 1024), F32), in_specs=[vm], out_specs=vm,
        scratch_shapes=[pltpu.VMEM((8, SMALL_ROWS, 1024), F32), pltpu.SemaphoreType.DMA((7,)),
                        pltpu.SemaphoreType.DMA((7,))],
        compiler_params=pltpu.CompilerParams(has_side_effects=True), name="allreduce_small",
    )(v)


def _heads(t, n):
    return t.reshape(S, n, HEAD_DIM).transpose(1, 0, 2)


def _unheads(t):
    return t.transpose(1, 0, 2).reshape(S, t.shape[0] * HEAD_DIM)


def _to_residue(t, d):
    c = t.shape[-1]
    return t.reshape(B_HPG, S // d, d, c).transpose(0, 2, 1, 3).reshape(B_HPG, S, c)


def _from_residue(t, d):
    c = t.shape[-1]
    return t.reshape(B_HPG, d, S // d, c).transpose(0, 2, 1, 3).reshape(B_HPG, S, c)


def _dil_pack(t):
    return jnp.concatenate([_to_residue(t[g], d) for g, (_, d) in enumerate(B_GROUPS)], axis=0)


def _dil_unpack(t):
    return jnp.stack([_from_residue(t[g * B_HPG:(g + 1) * B_HPG], d) for g, (_, d) in enumerate(B_GROUPS)], axis=0)


def _col_to_row(t):
    return t.reshape(t.shape[0], 1, S)


def _residue_rows(t, d, inverse=False):
    if d == 1:
        return t
    shape = (d, S // d) if inverse else (S // d, d)
    return t.reshape(shape + t.shape[1:]).transpose(1, 0, 2).reshape(t.shape)


def _residue_vecs(t, d, inverse=False):
    if d == 1:
        return t
    shape = (d, S // d) if inverse else (S // d, d)
    return t.reshape((B_HPG,) + shape).transpose(0, 2, 1).reshape(B_HPG, S, 1)


def _ffn_fwd(x, g, w_up, cw, cb, w_down, tag):
    h = _rms_fwd(x, g, f"{tag}_norm")
    u = _mm(h, w_up, mode="nn", tm=1024, tn=1408, tk=1024, o_split=2, name=f"{tag}_up")
    act = _ffn_act_fwd(u, cw, cb, f"{tag}_act")
    x_out = _mm(act, w_down, mode="nn", tm=1024, tn=512, tk=FF, res=x, name=f"{tag}_down")
    return x_out, (h, u, act)


def _ffn_bwd(x, g, w_up, cw, cb, w_down, saved, dx, dxb, tag):
    h, u, act = saved
    d_w_down = _mm(act, dxb, mode="tn", tm=1408, tn=512, tk=1024, name=f"{tag}_dwdown")
    dact = _mm(dxb, w_down, mode="nt", tm=1024, tn=1408, tk=1024, name=f"{tag}_dact")
    duc, dwb = _ffn_act_bwd(u, dact, cw, cb, f"{tag}_dgate")
    du = _ffn_conv_bwd(duc, cw, f"{tag}_dconv")
    d_w_up = _mm(h, du, mode="tn", tm=1024, tn=1408, tk=1024, b_split=2, name=f"{tag}_dwup")
    dh = _mm(du, w_up, mode="nt", tm=1024, tn=512, tk=1408, a_split=2, name=f"{tag}_dh")
    dx_new, dxb_new, (dg,) = _rms_bwd(x, dx, [(g, dh)], f"{tag}_dnorm")
    d_cw = dwb[:, 0:3, :].transpose(1, 0, 2).reshape(3, 2 * FF)
    d_cb = dwb[:, 3, :].reshape(2 * FF)
    return dx_new, dxb_new, dict(w_up=d_w_up, w_down=d_w_down, conv_w=d_cw, conv_b=d_cb, norm_g=dg.reshape(D))


def _local_step(x, target, p):
    g = {}
    h1 = _rms_fwd(x, p["mix_norm_g"][0], "a_norm")
    w_qkv = p["a_w_in"][:, :QKV_W]
    w_f = jnp.pad(p["a_w_in"][:, QKV_W:], ((0, 0), (0, LANES - A_HEADS)))
    b_f = jnp.pad(p["a_b_f"].reshape(1, A_HEADS), ((0, 0), (0, LANES - A_HEADS)))
    qkv = _mm(h1, w_qkv, mode="nn", tm=1024, tn=512, tk=1024, out_dtype=BF16, name="a_qkv")
    pf = _mm(h1, w_f, mode="nn", tm=1024, tn=LANES, tk=1024, name="a_gate")
    cum = _fgate_fwd(pf, b_f, "a_gate_scan")
    aug_q, aug_k = _gate_lanes(cum)
    oa2, lse_a = _fox_pair_fwd(qkv, aug_q, aug_k, "a_attn")
    x1 = _mm(oa2, p["a_w_out"], mode="nn", tm=1024, tn=512, tk=1024, res=x, name="a_out")
    x2, ffn0 = _ffn_fwd(x1, p["ffn_norm_g"][0], p["ffn_w_up"][0], p["conv_w"][0], p["conv_b"][0], p["ffn_w_down"][0], "f0")
    hk = _rms_fwd(x2, p["kv_norm_g"], "kv_norm")
    kvb = _mm(hk, p["w_kv"], mode="nn", tm=1024, tn=512, tk=1024, out_dtype=BF16, name="kv_proj")
    h3 = _rms_fwd(x2, p["mix_norm_g"][1], "b_norm")
    qb = _mm(h3, p["b_w_q"], mode="nn", tm=1024, tn=512, tk=1024, out_dtype=BF16, name="b_q")
    dil_in = []
    for gi, (_, d) in enumerate(B_GROUPS):
        if d == 1:
            dil_in.append((qb, kvb, kvb, gi * NPG, gi * NPG, (3 + gi) * NPG))
        else:
            qg = _residue_rows(qb[:, gi * GROUP_W:(gi + 1) * GROUP_W], d)
            kvg = _residue_rows(kvb.reshape(S, 2, 3, GROUP_W)[:, :, gi, :].reshape(S, 2 * GROUP_W), d)
            dil_in.append((qg, kvg, kvg, 0, 0, NPG))
    o_g, lse_g = [], []
    for gi, (_, d) in enumerate(B_GROUPS):
        qg, kg, vg, qoff, koff, voff = dil_in[gi]
        og, lg = _dil_pair_fwd(gi, qg, kg, vg, qoff, koff, voff, f"b_attn{gi}")
        o_g.append(_residue_rows(og, d, inverse=True))
        lse_g.append(_residue_vecs(lg, d, inverse=True))
    ob, ob2, lse_b = _dil_pair_merge(o_g, lse_g, "b_merge")
    x3 = _mm(ob2, p["b_w_out"], mode="nn", tm=1024, tn=512, tk=B_OUT_W, res=x2, name="b_out")
    x4, ffn1 = _ffn_fwd(x3, p["ffn_norm_g"][1], p["ffn_w_up"][1], p["conv_w"][1], p["conv_b"][1], p["ffn_w_down"][1], "f1")
    loss, dx, dxb, dg_final = _loss_head(x4, p["final_norm_g"], target, "loss_head")
    g["final_norm_g"] = dg_final.reshape(D)

    dx, dxb, gf1 = _ffn_bwd(x3, p["ffn_norm_g"][1], p["ffn_w_up"][1], p["conv_w"][1], p["conv_b"][1], p["ffn_w_down"][1],
                            ffn1, dx, dxb, "f1")
    g["b_w_out"] = _mm(ob2, dxb, mode="tn", tm=B_OUT_W, tn=512, tk=1024, name="b_dwout")
    dob = _mm(dxb, p["b_w_out"], mode="nt", tm=1024, tn=B_OUT_W, tk=1024, name="b_do")
    delta_b = _pair_rowdot(dob, ob, "b_delta")
    dob16 = dob.astype(BF16)
    dq_g, dk_g, dv_g = [], [], []
    for gi, (_, d) in enumerate(B_GROUPS):
        qg, kg, vg, qoff, koff, voff = dil_in[gi]
        dog, l_d, dl_d = _residue_rows(dob16, d), _residue_vecs(lse_b, d), _residue_vecs(delta_b, d)
        dqd = _dil_pair_dq(gi, qg, kg, vg, qoff, koff, voff, dog, l_d, dl_d, f"b_dq{gi}")
        dkd, dvd = _dil_pair_dkv(gi, qg, kg, vg, qoff, koff, voff, dog, _col_to_row(l_d), _col_to_row(dl_d), f"b_dkv{gi}")
        dq_g.append(_residue_rows(dqd, d, inverse=True))
        dk_g.append(_residue_rows(dkd, d, inverse=True))
        dv_g.append(_residue_rows(dvd, d, inverse=True))
    dqb = jnp.concatenate(dq_g, axis=1)
    dkvb = jnp.concatenate(dk_g + dv_g, axis=1)
    g["b_w_q"] = _mm(h3, dqb, mode="tn", tm=1024, tn=512, tk=1024, name="b_dwq")
    dh3 = _mm(dqb, p["b_w_q"], mode="nt", tm=1024, tn=512, tk=B_Q_W, name="b_dh")
    g["w_kv"] = _mm(hk, dkvb, mode="tn", tm=1024, tn=512, tk=1024, name="kv_dw")
    dhk = _mm(dkvb, p["w_kv"], mode="nt", tm=1024, tn=512, tk=1536, name="kv_dh")
    dx, dxb, (dg_mix1, dg_kv) = _rms_bwd(x2, dx, [(p["mix_norm_g"][1], dh3), (p["kv_norm_g"], dhk)], "b_dnorm")
    g["kv_norm_g"] = dg_kv.reshape(D)
    dx, dxb, gf0 = _ffn_bwd(x1, p["ffn_norm_g"][0], p["ffn_w_up"][0], p["conv_w"][0], p["conv_b"][0], p["ffn_w_down"][0],
                            ffn0, dx, dxb, "f0")
    g["a_w_out"] = _mm(oa2, dxb, mode="tn", tm=1024, tn=512, tk=1024, name="a_dwout")
    doa = _mm(dxb, p["a_w_out"], mode="nt", tm=1024, tn=512, tk=1024, name="a_do")
    delta_a = _pair_rowdot(doa, oa2, "a_delta")
    dqa, dka, dva, dck, dcq = _fox_pair_bwd(qkv, doa, _col_to_row(lse_a), _col_to_row(delta_a), aug_q, aug_k, "a_dattn")
    dqkv = jnp.concatenate([dqa, dka, dva], axis=1)
    pad_heads = lambda t: jnp.pad(t.reshape(A_HEADS, S).T, ((0, 0), (0, LANES - A_HEADS)))
    dpf, db_f = _fgate_bwd(pf, b_f, pad_heads(dck), pad_heads(dcq), "a_dgate_scan")
    g["a_b_f"] = db_f[:, :A_HEADS]
    d_w_qkv = _mm(h1, dqkv, mode="tn", tm=1024, tn=512, tk=1024, name="a_dwqkv")
    d_w_f = _mm(h1, dpf, mode="tn", tm=1024, tn=LANES, tk=1024, name="a_dwgate")
    g["a_w_in"] = jnp.concatenate([d_w_qkv, d_w_f[:, :A_HEADS]], axis=1)
    dh1 = _mm(dqkv, w_qkv, mode="nt", tm=1024, tn=512, tk=1536, name="a_dh")
    dh1 = _mm(dpf, w_f, mode="nt", tm=1024, tn=512, tk=LANES, res=dh1, name="a_dh_gate")
    dx, _, (dg_mix0,) = _rms_bwd(x, dx, [(p["mix_norm_g"][0], dh1)], "a_dnorm")

    g["mix_norm_g"] = jnp.stack([dg_mix0.reshape(D), dg_mix1.reshape(D)])
    g["ffn_norm_g"] = jnp.stack([gf0["norm_g"], gf1["norm_g"]])
    g["ffn_w_up"] = jnp.stack([gf0["w_up"], gf1["w_up"]])
    g["ffn_w_down"] = jnp.stack([gf0["w_down"], gf1["w_down"]])
    g["ffn_conv_w"] = jnp.stack([gf0["conv_w"], gf1["conv_w"]])
    g["ffn_conv_b"] = jnp.stack([gf0["conv_b"], gf1["conv_b"]])
    return loss[0, 0], dx, g


_SHARD_SHAPES = {"a_w_in": (1, 1024, 772), "a_w_out": (1, 256, 1024), "b_w_q": (1, 1024, 384), "b_w_out": (1, 512, 256),
                 "w_kv": (1024, 768), "ffn_w_up": (2, 1024, 1408), "ffn_w_down": (2, 704, 1024), "ffn_conv_w": (2, 3, 1408)}
_SHARD_AXIS = {"a_w_in": 2, "a_w_out": 1, "b_w_q": 2, "b_w_out": 2, "w_kv": 1, "ffn_w_up": 2, "ffn_w_down": 1,
               "ffn_conv_w": 2}
_SMALL = (("kv_norm_g", (1024,)), ("mix_norm_g", (2, 1024)), ("ffn_norm_g", (2, 1024)), ("final_norm_g", (1024,)),
          ("a_b_f", (1, 16)), ("ffn_conv_b", (2, 5632)))


def _slabs(t, L, R, C, rpad):
    nc = -(-C // FLAT_W)
    t = jnp.pad(t.reshape(L, R, C), ((0, 0), (0, rpad - R), (0, nc * FLAT_W - C)))
    return t.reshape(L, rpad, nc, FLAT_W).transpose(0, 2, 1, 3).reshape(L * nc * rpad, FLAT_W)


def _unslabs(rows, L, R, C, rpad):
    nc = -(-C // FLAT_W)
    return rows.reshape(L, nc, rpad, FLAT_W).transpose(0, 2, 1, 3).reshape(L, rpad, nc * FLAT_W)[:, :R, :C]


_SEG_RT = {"ffn_w_down": 704, "a_w_in": 1024, "a_w_out": 256, "b_w_q": 1024, "b_w_out": 512, "w_kv": 1024,
           "ffn_w_up": 1024, "ffn_conv_w": 16}
_ROW_SHARDED = ("a_w_out", "ffn_w_down")


def _seg(name):
    off = 0
    for s in _SEGS:
        if s[0] == name:
            _, L, R, C, rpad = s
            rt = _SEG_RT[name]
            assert off % rt == 0 and rpad % rt == 0 and (HALF_ROWS % rt == 0 or off + _seg_rows(*s) <= HALF_ROWS)
            return dict(L=L, R=R, C=C, rpad=rpad, nc=-(-C // FLAT_W), rt=rt, off=off, ni=rpad // rt)
        off += _seg_rows(*s)
    raise KeyError(name)


def _flat_block(sg, term=0):
    base = (sg["off"] + term * sg["L"] * sg["nc"] * sg["rpad"]) // sg["rt"]
    return lambda l, j, i: base + (l * sg["nc"] + j) * sg["ni"] + i


def _native3(t, name):
    sg = _seg(name)
    t = t.reshape(sg["L"], sg["R"], sg["C"])
    return jnp.pad(t, ((0, 0), (0, sg["rpad"] - sg["R"]), (0, 0))) if sg["rpad"] != sg["R"] else t


def _slab_pack(flat, t, name, term=None):
    sg = _seg(name)
    rt = sg["rt"]
    rb = _flat_block(sg, term or 0)

    def body(*refs):
        t_ref, o_ref = refs[-2], refs[-1]
        val = t_ref[...]
        o_ref[...] = val.astype(BF16) if term is None else _split3(val)[term]

    in_specs = [pl.BlockSpec((None, rt, FLAT_W), lambda l, j, i: (l, i, j))]
    args = [t]
    if flat is not None:
        in_specs, args = [ANY] + in_specs, [flat] + args
    return pl.pallas_call(
        body, out_shape=jax.ShapeDtypeStruct((FLAT_ROWS, FLAT_W), BF16), grid=(sg["L"], sg["nc"], sg["ni"]),
        in_specs=in_specs, out_specs=pl.BlockSpec((rt, FLAT_W), lambda l, j, i: (rb(l, j, i), 0)),
        input_output_aliases={0: 0} if flat is not None else {},
        compiler_params=_cparams(("parallel", "parallel", "parallel")), name=f"pack_{name}_{term or 0}",
    )(*args)


def _full_spec(sg, name):
    rt, nc, ni = sg["rt"], sg["nc"], sg["ni"]
    if name in _ROW_SHARDED:
        return (sg["L"], N_CHIPS * sg["R"], sg["C"]), pl.BlockSpec((None, rt, FLAT_W), lambda k, l, j, i: (l, k * ni + i, j))
    return ((sg["L"], sg["rpad"], N_CHIPS * nc * FLAT_W),
            pl.BlockSpec((None, rt, FLAT_W), lambda k, l, j, i: (l, i, k * nc + j)))


def _slab_unpack(gathered, name):
    sg = _seg(name)
    rb = _flat_block(sg)
    shape, spec = _full_spec(sg, name)

    def body(g_ref, o_ref):
        o_ref[...] = g_ref[...]

    return pl.pallas_call(
        body, out_shape=jax.ShapeDtypeStruct(shape, BF16), grid=(N_CHIPS, sg["L"], sg["nc"], sg["ni"]),
        in_specs=[pl.BlockSpec((None, sg["rt"], FLAT_W), lambda k, l, j, i: (k, rb(l, j, i), 0))], out_specs=spec,
        compiler_params=_cparams(("parallel",) * 4), name=f"unpack_{name}",
    )(gathered)


def _slab_pack_grad(flat4, g, name):
    sg = _seg(name)
    rb = _flat_block(sg)
    shape, spec = _full_spec(sg, name)
    assert g.shape == shape, (name, g.shape, shape)

    def body(*refs):
        refs[-1][...] = refs[-2][...]

    in_specs, args = [spec], [g]
    if flat4 is not None:
        in_specs, args = [pl.BlockSpec(memory_space=pl.ANY)] + in_specs, [flat4] + args
    return pl.pallas_call(
        body, out_shape=jax.ShapeDtypeStruct((N_CHIPS, FLAT_ROWS, FLAT_W), F32), grid=(N_CHIPS, sg["L"], sg["nc"], sg["ni"]),
        in_specs=in_specs, out_specs=pl.BlockSpec((None, sg["rt"], FLAT_W), lambda k, l, j, i: (k, rb(l, j, i), 0)),
        input_output_aliases={0: 0} if flat4 is not None else {},
        compiler_params=_cparams(("parallel",) * 4), name=f"packgrad_{name}",
    )(*args)


def _adamw_shard(w, m, v, g_mine, g_other, c_arr, name):
    sg = _seg(name)
    rt = sg["rt"]
    rb = _flat_block(sg)
    per_half = HALF_ROWS // rt

    def half_of(l, j, i):
        return (rb(l, j, i) * rt) // HALF_ROWS

    def body(c_ref, w_ref, m_ref, v_ref, gm_ref, go_ref, g_ref, d_ref, mo_ref, vo_ref):
        is_mine = half_of(pl.program_id(0), pl.program_id(1), pl.program_id(2)) == c_ref[0]
        gv = jnp.where(is_mine, gm_ref[...], go_ref[...])
        g_ref[...] = gv
        d_ref[...], mo_ref[...], vo_ref[...] = _adam_update(w_ref[...], gv, m_ref[...], v_ref[...])

    nat = pl.BlockSpec((None, rt, FLAT_W), lambda l, j, i, c: (l, i, j))
    half = pl.BlockSpec((rt, FLAT_W), lambda l, j, i, c: (rb(l, j, i) - half_of(l, j, i) * per_half, 0))
    sds = jax.ShapeDtypeStruct(w.shape, F32)
    return pl.pallas_call(
        body, out_shape=[sds] * 4,
        grid_spec=pltpu.PrefetchScalarGridSpec(num_scalar_prefetch=1, grid=(sg["L"], sg["nc"], sg["ni"]),
                                               in_specs=[nat, nat, nat, half, half], out_specs=[nat] * 4),
        compiler_params=_cparams(("parallel", "parallel", "parallel")), name=f"adamw_{name}",
    )(c_arr, w, m, v, g_mine, g_other)


def _pack_small(vals, loss=None):
    parts = [vals[name].astype(F32).reshape(-1) for name, _ in _SMALL]
    if loss is not None:
        parts.append(loss.reshape(1))
    flat = jnp.concatenate(parts)
    return jnp.pad(flat, (0, SMALL_ROWS * 1024 - flat.shape[0])).reshape(SMALL_ROWS, 1024)


def _unpack_small(flat):
    flat = flat.reshape(-1)
    out = {}
    o = 0
    for name, shape in _SMALL:
        n = int(np.prod(shape))
        out[name] = flat[o:o + n].reshape(shape)
        o += n
    return out, flat[o]


_BIG = ("a_w_in", "a_w_out", "b_w_q", "b_w_out", "w_kv", "ffn_w_up", "ffn_w_down", "ffn_conv_w")
A_IN_PAD = 896


def _pack_weights(w):
    flat = None
    for name in _BIG:
        t = _native3(w[name], name)
        for term in ((0, 1, 2) if name == "ffn_conv_w" else (None,)):
            flat = _slab_pack(flat, t, name, term)
    return flat


def _full_weights(gathered):
    full = {name: _slab_unpack(gathered, name) for name in _BIG if name != "ffn_conv_w"}
    a_in = full["a_w_in"].reshape(D, N_CHIPS, A_IN_PAD)[:, :, :772].reshape(D, N_CHIPS * 772)
    sg = _seg("ffn_conv_w")
    n1 = sg["nc"] * sg["rpad"]
    per_chip = []
    for k in range(N_CHIPS):
        terms = [_unslabs(gathered[k, sg["off"] + i * n1:sg["off"] + (i + 1) * n1], 1, sg["R"], sg["C"], sg["rpad"]).astype(F32)
                 for i in range(CONV_TERMS)]
        per_chip.append((terms[0] + terms[1]) + terms[2])
    cw = jnp.concatenate(per_chip, axis=2).reshape(2, 3, 2, FF).transpose(0, 2, 1, 3)
    return dict(a_w_in=a_in, a_w_out=full["a_w_out"][0], b_w_q=full["b_w_q"][0], b_w_out=full["b_w_out"][0],
                w_kv=full["w_kv"][0], ffn_w_up=full["ffn_w_up"], ffn_w_down=full["ffn_w_down"], conv_w=cw)


def _shard_grads(g):
    a_in = jnp.pad(g["a_w_in"].reshape(D, N_CHIPS, 772), ((0, 0), (0, 0), (0, A_IN_PAD - 772)))
    sgc = _seg("ffn_conv_w")
    full = {"a_w_in": a_in.reshape(1, D, N_CHIPS * A_IN_PAD), "a_w_out": g["a_w_out"][None], "b_w_q": g["b_w_q"][None],
            "b_w_out": g["b_w_out"][None], "w_kv": g["w_kv"][None], "ffn_w_up": g["ffn_w_up"],
            "ffn_w_down": g["ffn_w_down"],
            "ffn_conv_w": jnp.pad(g["ffn_conv_w"].reshape(1, sgc["R"], 2 * FF), ((0, 0), (0, sgc["rpad"] - sgc["R"]), (0, 0)))}
    flat4 = None
    for name in _BIG:
        flat4 = _slab_pack_grad(flat4, full[name], name)
    return flat4


_WEIGHTS = ["a_w_in", "a_b_f", "a_w_out", "b_w_q", "b_w_out", "kv_norm_g", "w_kv", "mix_norm_g", "ffn_norm_g", "ffn_w_up",
            "ffn_conv_w", "ffn_conv_b", "ffn_w_down", "final_norm_g"]


def kernel(x, a_w_in, a_b_f, a_w_out, b_w_q, b_w_out, kv_norm_g, w_kv, mix_norm_g, ffn_norm_g, ffn_w_up, ffn_conv_w, ffn_conv_b, ffn_w_down, final_norm_g, loss_target, m_a_w_in, m_a_b_f, m_a_w_out, m_b_w_q, m_b_w_out, m_kv_norm_g, m_w_kv, m_mix_norm_g, m_ffn_norm_g, m_ffn_w_up, m_ffn_conv_w, m_ffn_conv_b, m_ffn_w_down, m_final_norm_g, v_a_w_in, v_a_b_f, v_a_w_out, v_b_w_q, v_b_w_out, v_kv_norm_g, v_w_kv, v_mix_norm_g, v_ffn_norm_g, v_ffn_w_up, v_ffn_conv_w, v_ffn_conv_b, v_ffn_w_down, v_final_norm_g):
    w = dict(a_w_in=a_w_in, a_b_f=a_b_f, a_w_out=a_w_out, b_w_q=b_w_q, b_w_out=b_w_out, kv_norm_g=kv_norm_g, w_kv=w_kv,
             mix_norm_g=mix_norm_g, ffn_norm_g=ffn_norm_g, ffn_w_up=ffn_w_up, ffn_conv_w=ffn_conv_w, ffn_conv_b=ffn_conv_b,
             ffn_w_down=ffn_w_down, final_norm_g=final_norm_g)
    m = dict(a_w_in=m_a_w_in, a_b_f=m_a_b_f, a_w_out=m_a_w_out, b_w_q=m_b_w_q, b_w_out=m_b_w_out, kv_norm_g=m_kv_norm_g,
             w_kv=m_w_kv, mix_norm_g=m_mix_norm_g, ffn_norm_g=m_ffn_norm_g, ffn_w_up=m_ffn_w_up, ffn_conv_w=m_ffn_conv_w,
             ffn_conv_b=m_ffn_conv_b, ffn_w_down=m_ffn_w_down, final_norm_g=m_final_norm_g)
    v = dict(a_w_in=v_a_w_in, a_b_f=v_a_b_f, a_w_out=v_a_w_out, b_w_q=v_b_w_q, b_w_out=v_b_w_out, kv_norm_g=v_kv_norm_g,
             w_kv=v_w_kv, mix_norm_g=v_mix_norm_g, ffn_norm_g=v_ffn_norm_g, ffn_w_up=v_ffn_w_up, ffn_conv_w=v_ffn_conv_w,
             ffn_conv_b=v_ffn_conv_b, ffn_w_down=v_ffn_w_down, final_norm_g=v_final_norm_g)

    c_arr = lax.axis_index("c").astype(jnp.int32).reshape(1)
    k_arr = (2 * lax.axis_index("x") + lax.axis_index("y")).astype(jnp.int32).reshape(1)
    w16 = _pack_weights(w)
    gathered = _allgather_shards(w16, _place_own(w16, k_arr))
    p = _full_weights(gathered)
    cb = ffn_conv_b.reshape(2, 2, 1, FF)
    p.update(a_b_f=a_b_f, kv_norm_g=kv_norm_g, mix_norm_g=mix_norm_g, ffn_norm_g=ffn_norm_g, final_norm_g=final_norm_g,
             conv_b=cb)

    loss_part, grad_x, g = _local_step(x[0], loss_target[0], p)

    gflat = _shard_grads(g)
    pair = _pair_sum(gflat, _sibling_swap_half(gflat), c_arr)
    g_mine = _chip_sum(pair, _chip_exchange(pair), k_arr)
    g_other = _sibling_send(g_mine)
    small, loss = _unpack_small(_allreduce_small(_pack_small(g, loss_part)))

    big = [{}, {}, {}, {}]
    for name in _BIG:
        sg = _seg(name)
        res = _adamw_shard(_native3(w[name], name), _native3(m[name], name), _native3(v[name], name), g_mine, g_other,
                           c_arr, name)
        for store, t in zip(big, res):
            store[name] = t[:, :sg["R"], :].reshape(_SHARD_SHAPES[name])
    dws, mns, vns = _adamw(_pack_small(w), _pack_small(small), _pack_small(m), _pack_small(v), "adamw_small")
    sml = [small] + [_unpack_small(t)[0] for t in (dws, mns, vns)]
    outs = [loss, grad_x[None]]
    for b, s in zip(big, sml):
        outs += [b[n] if n in b else s[n] for n in _WEIGHTS]
    return tuple(outs)
```

```python
import functools
import math

import numpy as np
import jax
import jax.numpy as jnp
from jax import lax
from jax.experimental import pallas as pl
from jax.experimental.pallas import tpu as pltpu

F32 = jnp.float32
BF16 = jnp.bfloat16
MESH = pl.DeviceIdType.MESH

S = 4096
D = 1024
A_HEADS = 16
HEAD_DIM = 64
QKV_W = 3 * A_HEADS * HEAD_DIM
B_GROUPS = ((128, 1), (512, 4), (2048, 16))
B_HPG = 8
B_Q_W = 3 * B_HPG * HEAD_DIM
B_OUT_W = B_HPG * HEAD_DIM
B_KV_W = 2 * B_Q_W
B_WIN = 128
FF = 2816
RMS_EPS = 1e-6
SCALE = HEAD_DIM ** -0.5
N_CHIPS = 4

ADAM_LR, ADAM_B1, ADAM_B2, ADAM_EPS, ADAM_WD, ADAM_STEP = 0.001, 0.9, 0.999, 1e-08, 0.01, 10

V7X_VMEM_LIMIT = 48 * 1024 * 1024
LANES = 128
NEG_INF = float("-inf")

FLAT_W = LANES
_SEGS = (("ffn_w_down", 2, 704, 1024, 704), ("a_w_in", 1, 1024, 772, 1024), ("a_w_out", 1, 256, 1024, 256),
         ("b_w_q", 1, 1024, 384, 1024), ("b_w_out", 1, 512, 256, 512), ("w_kv", 1, 1024, 768, 1024),
         ("ffn_w_up", 2, 1024, 1408, 1024), ("ffn_conv_w", 1, 6, 1408, 16))
CONV_TERMS = 3


def _seg_rows(name, L, R, C, rpad):
    return (CONV_TERMS if name == "ffn_conv_w" else 1) * L * (-(-C // FLAT_W)) * rpad


FLAT_T = 2048
FLAT_ROWS = 57344
HALF_ROWS = FLAT_ROWS // 2
assert sum(_seg_rows(*s) for s in _SEGS) <= FLAT_ROWS and HALF_ROWS % FLAT_T == 0
SMALL_ROWS = 24


def _cparams(sem=None, **kw):
    return pltpu.CompilerParams(dimension_semantics=sem, vmem_limit_bytes=V7X_VMEM_LIMIT, **kw)


_DN = {"nn": (((1,), (0,)), ((), ())), "nt": (((1,), (1,)), ((), ())), "tn": (((0,), (0,)), ((), ()))}


def _mm(a, b, *, mode, tm, tn, tk, name, out_dtype=F32, res=None, a_split=0, b_split=0, o_split=0):
    if mode == "tn":
        K = a.shape[0]
        M = a.shape[1]
    else:
        M = a.shape[-2]
        K = a.shape[-1] * (2 if a_split else 1)
    if mode == "nt":
        N = b.shape[0]
    else:
        N = b.shape[-1] * (2 if b_split else 1)
    assert M % tm == 0 and N % tn == 0 and K % tk == 0, (name, M, N, K, tm, tn, tk)
    nk = K // tk

    if mode == "tn":
        a_spec = pl.BlockSpec((tk, tm), lambda i, j, k: (k, i))
    elif a_split:
        a_spec = pl.BlockSpec((None, tm, tk), lambda i, j, k: (k // a_split, i, k % a_split))
    else:
        a_spec = pl.BlockSpec((tm, tk), lambda i, j, k: (i, k))
    if mode == "nt":
        b_spec = pl.BlockSpec((tn, tk), lambda i, j, k: (j, k))
    elif b_split:
        b_spec = pl.BlockSpec((None, tk, tn), lambda i, j, k: (j // b_split, k, j % b_split))
    else:
        b_spec = pl.BlockSpec((tk, tn), lambda i, j, k: (k, j))
    if o_split:
        o_spec = pl.BlockSpec((None, tm, tn), lambda i, j, k: (j // o_split, i, j % o_split))
        out_shape = jax.ShapeDtypeStruct((2, M, N // 2), out_dtype)
    else:
        o_spec = pl.BlockSpec((tm, tn), lambda i, j, k: (i, j))
        out_shape = jax.ShapeDtypeStruct((M, N), out_dtype)
    in_specs = [a_spec, b_spec]
    args = [a, b]
    if res is not None:
        in_specs.append(pl.BlockSpec((tm, tn), lambda i, j, k: (i, j)))
        args.append(res)

    def body(*refs):
        if res is not None:
            a_ref, b_ref, r_ref, o_ref = refs[:4]
        else:
            a_ref, b_ref, o_ref = refs[:3]
            r_ref = None
        p = lax.dot_general(a_ref[...].astype(BF16), b_ref[...].astype(BF16), _DN[mode], preferred_element_type=F32)

        def finish(r):
            if r_ref is not None:
                r = r + r_ref[...]
            o_ref[...] = r.astype(out_dtype)

        if nk == 1:
            finish(p)
        else:
            acc = refs[-1]
            k = pl.program_id(2)

            @pl.when(k == 0)
            def _():
                acc[...] = p

            @pl.when(k > 0)
            def _():
                acc[...] += p

            @pl.when(k == nk - 1)
            def _():
                finish(acc[...])

    return pl.pallas_call(
        body, out_shape=out_shape, grid=(M // tm, N // tn, nk), in_specs=in_specs, out_specs=o_spec,
        scratch_shapes=[pltpu.VMEM((tm, tn), F32)] if nk > 1 else [],
        compiler_params=_cparams(("parallel", "parallel", "arbitrary")), name=name,
    )(*args)


NORM_ROWS = 256


def _rms_fwd(x, g, name):
    def body(x_ref, g_ref, o_ref):
        xv = x_ref[...]
        r = lax.rsqrt(jnp.mean(xv * xv, axis=-1, keepdims=True) + RMS_EPS)
        o_ref[...] = (xv * r * g_ref[...]).astype(BF16)

    row = pl.BlockSpec((NORM_ROWS, D), lambda i: (i, 0))
    return pl.pallas_call(
        body, out_shape=jax.ShapeDtypeStruct((S, D), BF16), grid=(S // NORM_ROWS,),
        in_specs=[row, pl.BlockSpec((1, D), lambda i: (0, 0))], out_specs=row,
        compiler_params=_cparams(("parallel",)), name=name,
    )(x, g.reshape(1, D))


def _rms_bwd(x, dres, pairs, name):
    n = len(pairs)

    def body(*refs):
        x_ref, dres_ref = refs[0], refs[1]
        g_refs = refs[2:2 + 2 * n:2]
        dh_refs = refs[3:3 + 2 * n:2]
        dx_ref, dxb_ref = refs[2 + 2 * n], refs[3 + 2 * n]
        dg_refs = refs[4 + 2 * n:]
        i = pl.program_id(0)
        xv = x_ref[...]
        r = lax.rsqrt(jnp.mean(xv * xv, axis=-1, keepdims=True) + RMS_EPS)
        y = xv * r
        dx = dres_ref[...]
        for g_ref, dh_ref, dg_ref in zip(g_refs, dh_refs, dg_refs):
            dh = dh_ref[...]
            dy = dh * g_ref[...]
            dx = dx + r * (dy - y * jnp.mean(dy * y, axis=-1, keepdims=True))
            part = jnp.sum(dh * y, axis=0, keepdims=True)

            @pl.when(i == 0)
            def _():
                dg_ref[...] = part

            @pl.when(i > 0)
            def _():
                dg_ref[...] += part

        dx_ref[...] = dx
        dxb_ref[...] = dx.astype(BF16)

    row = pl.BlockSpec((NORM_ROWS, D), lambda i: (i, 0))
    vec = pl.BlockSpec((1, D), lambda i: (0, 0))
    in_specs = [row, row]
    args = [x, dres]
    for g, dh in pairs:
        in_specs += [vec, row]
        args += [g.reshape(1, D), dh]
    outs = pl.pallas_call(
        body,
        out_shape=[jax.ShapeDtypeStruct((S, D), F32), jax.ShapeDtypeStruct((S, D), BF16)]
        + [jax.ShapeDtypeStruct((1, D), F32)] * n,
        grid=(S // NORM_ROWS,), in_specs=in_specs, out_specs=[row, row] + [vec] * n,
        compiler_params=_cparams(("arbitrary",)), name=name,
    )(*args)
    return outs[0], outs[1], list(outs[2:])


def _loss_head(x, g, target, name):
    def body(x_ref, g_ref, t_ref, loss_ref, dx_ref, dxb_ref, dg_ref):
        i = pl.program_id(0)
        xv = x_ref[...]
        gv = g_ref[...]
        r = lax.rsqrt(jnp.mean(xv * xv, axis=-1, keepdims=True) + RMS_EPS)
        y = xv * r
        err = y * gv - t_ref[...]
        lpart = jnp.broadcast_to(jnp.sum(err * err, keepdims=True) * (0.5 / D), (1, LANES))
        dh = err * (1.0 / D)
        dy = dh * gv
        dx = r * (dy - y * jnp.mean(dy * y, axis=-1, keepdims=True))
        part = jnp.sum(dh * y, axis=0, keepdims=True)

        @pl.when(i == 0)
        def _():
            dg_ref[...] = part
            loss_ref[...] = lpart

        @pl.when(i > 0)
        def _():
            dg_ref[...] += part
            loss_ref[...] += lpart

        dx_ref[...] = dx
        dxb_ref[...] = dx.astype(BF16)

    row = pl.BlockSpec((NORM_ROWS, D), lambda i: (i, 0))
    vec = pl.BlockSpec((1, D), lambda i: (0, 0))
    return pl.pallas_call(
        body,
        out_shape=[jax.ShapeDtypeStruct((1, LANES), F32), jax.ShapeDtypeStruct((S, D), F32),
                   jax.ShapeDtypeStruct((S, D), BF16), jax.ShapeDtypeStruct((1, D), F32)],
        grid=(S // NORM_ROWS,), in_specs=[row, vec, row],
        out_specs=[pl.BlockSpec((1, LANES), lambda i: (0, 0)), row, row, vec],
        compiler_params=_cparams(("arbitrary",)), name=name,
    )(x, g.reshape(1, D), target)


SCAN_ROWS = 256


def _split3(v):
    hi = v.astype(BF16)
    r1 = v - hi.astype(F32)
    mid = r1.astype(BF16)
    lo = (r1 - mid.astype(F32)).astype(BF16)
    return hi, mid, lo


def _tri_dot(tri, v):
    hi, mid, lo = _split3(v)
    dn = _DN["nn"]
    return (lax.dot_general(tri, hi, dn, preferred_element_type=F32)
            + lax.dot_general(tri, mid, dn, preferred_element_type=F32)
            + lax.dot_general(tri, lo, dn, preferred_element_type=F32))


def _log_sigmoid(z):
    return jnp.minimum(z, 0.0) - jnp.log(1.0 + jnp.exp(-jnp.abs(z)))


def _fgate_fwd(pf, bias, name):
    tri = jnp.tril(jnp.ones((SCAN_ROWS, SCAN_ROWS), F32)).astype(BF16)

    def body(pf_ref, b_ref, tri_ref, c_ref):
        carry = jnp.zeros((1, LANES), F32)
        for blk in range(S // SCAN_ROWS):
            rows = pl.ds(blk * SCAN_ROWS, SCAN_ROWS)
            lf = _log_sigmoid(pf_ref[rows, :] + b_ref[...])
            c_ref[rows, :] = _tri_dot(tri_ref[...], lf) + carry
            carry = c_ref[pl.ds(blk * SCAN_ROWS + SCAN_ROWS - 1, 1), :]

    return pl.pallas_call(
        body, out_shape=jax.ShapeDtypeStruct((S, LANES), F32),
        compiler_params=_cparams(), name=name,
    )(pf, bias, tri)


def _fgate_bwd(pf, bias, dc_key, dc_query, name):
    triu = jnp.triu(jnp.ones((SCAN_ROWS, SCAN_ROWS), F32)).astype(BF16)

    def body(pf_ref, b_ref, dck_ref, dcq_ref, tri_ref, dpf_ref, db_ref, dlf_ref):
        carry = jnp.zeros((1, LANES), F32)
        db = jnp.zeros((1, LANES), F32)
        lane = lax.broadcasted_iota(jnp.int32, (SCAN_ROWS, LANES), 1)
        for blk in reversed(range(S // SCAN_ROWS)):
            rows = pl.ds(blk * SCAN_ROWS, SCAN_ROWS)
            dc = dck_ref[rows, :] + dcq_ref[rows, :]
            dlf_ref[rows, :] = _tri_dot(tri_ref[...], dc) + carry
            carry = dlf_ref[pl.ds(blk * SCAN_ROWS, 1), :]
            z = pf_ref[rows, :] + b_ref[...]
            e = jnp.exp(-jnp.abs(z))
            sig_neg = jnp.where(z >= 0.0, e, 1.0) / (1.0 + e)
            dz = jnp.where(lane < A_HEADS, dlf_ref[rows, :] * sig_neg, 0.0)
            dpf_ref[rows, :] = dz.astype(BF16)
            db = db + jnp.sum(dz, axis=0, keepdims=True)
        db_ref[...] = db

    return pl.pallas_call(
        body, out_shape=[jax.ShapeDtypeStruct((S, LANES), BF16), jax.ShapeDtypeStruct((1, LANES), F32)],
        scratch_shapes=[pltpu.VMEM((S, LANES), F32)],
        compiler_params=_cparams(), name=name,
    )(pf, bias, dc_key, dc_query, triu)


FOX_T = 512


def _fox_fwd(q, k, v, ccol, crow, name):
    H = q.shape[0]
    T = FOX_T
    nq = S // T

    def body(q_ref, k_ref, v_ref, cc_ref, cr_ref, o_ref, lse_ref, m_sc, l_sc, acc_sc):
        i = pl.program_id(1)
        j = pl.program_id(2)

        @pl.when(j == 0)
        def _():
            m_sc[...] = jnp.full((T, 1), NEG_INF, F32)
            l_sc[...] = jnp.zeros((T, 1), F32)
            acc_sc[...] = jnp.zeros((T, HEAD_DIM), F32)

        def step(diagonal):
            s = lax.dot_general(q_ref[...], k_ref[...], _DN["nt"], preferred_element_type=F32) * SCALE
            s = s + (cc_ref[...] - cr_ref[...])
            if diagonal:
                row = lax.broadcasted_iota(jnp.int32, (T, T), 0)
                col = lax.broadcasted_iota(jnp.int32, (T, T), 1)
                s = jnp.where(row >= col, s, NEG_INF)
            m_prev = m_sc[...]
            m_new = jnp.maximum(m_prev, jnp.max(s, axis=1, keepdims=True))
            alpha = jnp.exp(m_prev - m_new)
            p = jnp.exp(s - m_new)
            l_sc[...] = alpha * l_sc[...] + jnp.sum(p, axis=1, keepdims=True)
            acc_sc[...] = alpha * acc_sc[...] + lax.dot_general(p.astype(BF16), v_ref[...], _DN["nn"],
                                                                preferred_element_type=F32)
            m_sc[...] = m_new

        @pl.when(j < i)
        def _():
            step(False)

        @pl.when(j == i)
        def _():
            step(True)
            o_ref[...] = (acc_sc[...] / l_sc[...]).astype(BF16)
            lse_ref[...] = m_sc[...] + jnp.log(l_sc[...])

    qs = pl.BlockSpec((None, T, HEAD_DIM), lambda h, i, j: (h, i, 0))
    ks = pl.BlockSpec((None, T, HEAD_DIM), lambda h, i, j: (h, jnp.minimum(i, j), 0))
    col = pl.BlockSpec((None, T, 1), lambda h, i, j: (h, i, 0))
    rowk = pl.BlockSpec((None, 1, T), lambda h, i, j: (h, 0, jnp.minimum(i, j)))
    return pl.pallas_call(
        body, out_shape=[jax.ShapeDtypeStruct((H, S, HEAD_DIM), BF16), jax.ShapeDtypeStruct((H, S, 1), F32)],
        grid=(H, nq, nq), in_specs=[qs, ks, ks, col, rowk], out_specs=[qs, col],
        scratch_shapes=[pltpu.VMEM((T, 1), F32), pltpu.VMEM((T, 1), F32), pltpu.VMEM((T, HEAD_DIM), F32)],
        compiler_params=_cparams(("parallel", "parallel", "arbitrary")), name=name,
    )(q, k, v, ccol, crow)


def _fox_bwd(q, k, v, do, lse_row, delta_row, cq_row, ck_col, name):
    H = q.shape[0]
    T = FOX_T
    nq = S // T

    def body(q_ref, k_ref, v_ref, do_ref, lse_ref, dl_ref, cq_ref, ck_ref, dq_ref, dk_ref, dv_ref, dc_ref, dcq_ref,
             dk_sc, dv_sc, dc_sc):
        j = pl.program_id(1)
        i = pl.program_id(2)

        @pl.when(jnp.logical_and(j == 0, i == 0))
        def _():
            dq_ref[...] = jnp.zeros((S, HEAD_DIM), F32)
            dcq_ref[...] = jnp.zeros((nq, 1, T), F32)

        @pl.when(i == j)
        def _():
            dk_sc[...] = jnp.zeros((T, HEAD_DIM), F32)
            dv_sc[...] = jnp.zeros((T, HEAD_DIM), F32)
            dc_sc[...] = jnp.zeros((T, 1), F32)

        def step(diagonal):
            qv = q_ref[...]
            kv = k_ref[...]
            dov = do_ref[...]
            st = lax.dot_general(kv, qv, _DN["nt"], preferred_element_type=F32) * SCALE
            st = st + (cq_ref[...] - ck_ref[...])
            if diagonal:
                row = lax.broadcasted_iota(jnp.int32, (T, T), 0)
                col = lax.broadcasted_iota(jnp.int32, (T, T), 1)
                st = jnp.where(col >= row, st, NEG_INF)
            pt = jnp.exp(st - lse_ref[...])
            dv_sc[...] += lax.dot_general(pt.astype(BF16), dov, _DN["nn"], preferred_element_type=F32)
            dpt = lax.dot_general(v_ref[...], dov, _DN["nt"], preferred_element_type=F32)
            dst = pt * (dpt - dl_ref[...])
            dc_sc[...] -= jnp.sum(dst, axis=1, keepdims=True)
            dcq_ref[i] += jnp.sum(dst, axis=0, keepdims=True)
            dsb = (dst * SCALE).astype(BF16)
            dk_sc[...] += lax.dot_general(dsb, qv, _DN["nn"], preferred_element_type=F32)
            rows = pl.ds(pl.multiple_of(i * T, T), T)
            dq_ref[rows, :] += lax.dot_general(dsb, kv, _DN["tn"], preferred_element_type=F32)

        @pl.when(i > j)
        def _():
            step(False)

        @pl.when(i == j)
        def _():
            step(True)

        @pl.when(i == nq - 1)
        def _():
            dk_ref[...] = dk_sc[...].astype(BF16)
            dv_ref[...] = dv_sc[...].astype(BF16)
            dc_ref[...] = dc_sc[...]

    qs = pl.BlockSpec((None, T, HEAD_DIM), lambda h, j, i: (h, jnp.maximum(i, j), 0))
    qrow = pl.BlockSpec((None, 1, T), lambda h, j, i: (h, 0, jnp.maximum(i, j)))
    ks = pl.BlockSpec((None, T, HEAD_DIM), lambda h, j, i: (h, j, 0))
    kcol = pl.BlockSpec((None, T, 1), lambda h, j, i: (h, j, 0))
    dqs = pl.BlockSpec((None, S, HEAD_DIM), lambda h, j, i: (h, 0, 0))
    dcqs = pl.BlockSpec((None, nq, 1, T), lambda h, j, i: (h, 0, 0, 0))
    return pl.pallas_call(
        body,
        out_shape=[jax.ShapeDtypeStruct((H, S, HEAD_DIM), F32), jax.ShapeDtypeStruct((H, S, HEAD_DIM), BF16),
                   jax.ShapeDtypeStruct((H, S, HEAD_DIM), BF16), jax.ShapeDtypeStruct((H, S, 1), F32),
                   jax.ShapeDtypeStruct((H, nq, 1, T), F32)],
        grid=(H, nq, nq), in_specs=[qs, ks, ks, qs, qrow, qrow, qrow, kcol], out_specs=[dqs, ks, ks, kcol, dcqs],
        scratch_shapes=[pltpu.VMEM((T, HEAD_DIM), F32), pltpu.VMEM((T, HEAD_DIM), F32), pltpu.VMEM((T, 1), F32)],
        compiler_params=_cparams(("parallel", "arbitrary", "arbitrary")), name=name,
    )(q, k, v, do, lse_row, delta_row, cq_row, ck_col)


def _rowdot(a, b, name):
    H = a.shape[0]
    T = 1024

    def body(a_ref, b_ref, o_ref):
        o_ref[...] = jnp.sum(a_ref[...].astype(F32) * b_ref[...].astype(F32), axis=-1, keepdims=True)

    blk = pl.BlockSpec((None, T, HEAD_DIM), lambda h, i: (h, i, 0))
    return pl.pallas_call(
        body, out_shape=jax.ShapeDtypeStruct((H, S, 1), F32), grid=(H, S // T), in_specs=[blk, blk],
        out_specs=pl.BlockSpec((None, T, 1), lambda h, i: (h, i, 0)),
        compiler_params=_cparams(("parallel", "parallel")), name=name,
    )(a, b)


def _first_head(shape):
    return lax.broadcasted_iota(jnp.int32, shape, len(shape) - 1) < HEAD_DIM


def _each_head(x, lo):
    zero = jnp.zeros_like(x)
    return jnp.where(lo, x, zero), jnp.where(lo, zero, x)


GATE_LANES = 6


def _gate_lanes(cum):
    to_bf16_grid = lambda t: lax.reduce_precision(t, exponent_bits=8, mantissa_bits=7)
    c = cum[:, :A_HEADS]
    hi = to_bf16_grid(c)
    mid = to_bf16_grid(c - hi)
    lo = to_bf16_grid((c - hi) - mid)
    one = jnp.ones_like(hi)
    def place(cols):
        t = jnp.stack(cols, axis=-1)
        t = jnp.pad(t, ((0, 0), (0, 0), (0, HEAD_DIM - GATE_LANES)))
        return t.reshape(S, A_HEADS // 2, 2, HEAD_DIM)[:, :, ::-1, :].reshape(S, A_HEADS * HEAD_DIM).astype(BF16)
    return place([hi, mid, lo, one, one, one]), place([one, one, one, -hi, -mid, -lo])


def _fox_pair_fwd(qkv, aug_q, aug_k, name):
    T = FOX_T
    nq = S // T
    NP = A_HEADS // 2

    def body(q_ref, k_ref, v_ref, aq_ref, ak_ref, o_ref, lse_ref, m_sc, l_sc, acc_sc):
        i = pl.program_id(1)
        j = pl.program_id(2)
        lo = _first_head((T, LANES))

        @pl.when(j == 0)
        def _():
            m_sc[...] = jnp.full((2, T, LANES), NEG_INF, F32)
            l_sc[...] = jnp.zeros((2, T, LANES), F32)
            acc_sc[...] = jnp.zeros((T, LANES), F32)

        def step(diagonal):
            qs = q_ref[...] * jnp.asarray(SCALE, BF16)
            aq, ak, kv = aq_ref[...], ak_ref[...], k_ref[...]
            q2 = (jnp.where(lo, qs, aq), jnp.where(lo, aq, qs))
            k2 = (jnp.where(lo, kv, ak), jnp.where(lo, ak, kv))
            if diagonal:
                causal = lax.broadcasted_iota(jnp.int32, (T, T), 0) >= lax.broadcasted_iota(jnp.int32, (T, T), 1)
            pv, alphas = None, []
            for h, vh in enumerate(_each_head(v_ref[...], lo)):
                s = lax.dot_general(q2[h], k2[h], _DN["nt"], preferred_element_type=F32)
                if diagonal:
                    s = jnp.where(causal, s, NEG_INF)
                m_prev = m_sc[h]
                m_new = jnp.maximum(m_prev, jnp.max(s, axis=1, keepdims=True))
                alpha = jnp.exp(m_prev - m_new)
                p = jnp.exp(s - jnp.tile(m_new, (1, T // LANES)))
                l_sc[h] = alpha * l_sc[h] + jnp.sum(p, axis=1, keepdims=True)
                m_sc[h] = m_new
                d = lax.dot_general(p.astype(BF16), vh, _DN["nn"], preferred_element_type=F32)
                pv = d if pv is None else pv + d
                alphas.append(alpha)
            acc_sc[...] = jnp.where(lo, alphas[0], alphas[1]) * acc_sc[...] + pv

        @pl.when(j < i)
        def _():
            step(False)

        @pl.when(j == i)
        def _():
            step(True)
            o_ref[...] = (acc_sc[...] * jnp.where(lo, 1.0 / l_sc[0], 1.0 / l_sc[1])).astype(BF16)
            for h in range(2):
                lse_ref[h] = (m_sc[h] + jnp.log(l_sc[h]))[:, 0:1]

    qs_ = pl.BlockSpec((T, LANES), lambda p, i, j: (i, p))
    ks = pl.BlockSpec((T, LANES), lambda p, i, j: (jnp.minimum(i, j), NP + p))
    vs = pl.BlockSpec((T, LANES), lambda p, i, j: (jnp.minimum(i, j), 2 * NP + p))
    aks = pl.BlockSpec((T, LANES), lambda p, i, j: (jnp.minimum(i, j), p))
    col = pl.BlockSpec((2, T, 1), lambda p, i, j: (p, i, 0))
    return pl.pallas_call(
        body, out_shape=[jax.ShapeDtypeStruct((S, A_HEADS * HEAD_DIM), BF16), jax.ShapeDtypeStruct((A_HEADS, S, 1), F32)],
        grid=(NP, nq, nq), in_specs=[qs_, ks, vs, qs_, aks], out_specs=[qs_, col],
        scratch_shapes=[pltpu.VMEM((2, T, LANES), F32), pltpu.VMEM((2, T, LANES), F32), pltpu.VMEM((T, LANES), F32)],
        compiler_params=_cparams(("parallel", "parallel", "arbitrary")), name=name,
    )(qkv, qkv, qkv, aug_q, aug_k)


def _fox_pair_bwd(qkv, do, lse_row, delta_row, aug_q, aug_k, name):
    T = FOX_T
    nq = S // T
    NP = A_HEADS // 2

    def body(q_ref, k_ref, v_ref, do_ref, lse_ref, dl_ref, aq_ref, ak_ref, dq_ref, dk_ref, dv_ref, dc_ref, dcq_ref,
             dq_sc, dk_sc, dv_sc, dc_sc):
        j = pl.program_id(1)
        i = pl.program_id(2)
        lo = _first_head((T, LANES))

        @pl.when(jnp.logical_and(j == 0, i == 0))
        def _():
            dq_sc[...] = jnp.zeros((S, LANES), F32)
            dcq_ref[...] = jnp.zeros((2, nq, 1, T), F32)

        @pl.when(i == j)
        def _():
            dk_sc[...] = jnp.zeros((T, LANES), F32)
            dv_sc[...] = jnp.zeros((T, LANES), F32)
            dc_sc[...] = jnp.zeros((2, T, 1), F32)

        def step(diagonal):
            qv = q_ref[...]
            kv = k_ref[...]
            dov = do_ref[...].astype(BF16)
            qs = qv * jnp.asarray(SCALE, BF16)
            aq, ak = aq_ref[...], ak_ref[...]
            q2 = (jnp.where(lo, qs, aq), jnp.where(lo, aq, qs))
            k2 = (jnp.where(lo, kv, ak), jnp.where(lo, ak, kv))
            if diagonal:
                causal = lax.broadcasted_iota(jnp.int32, (T, T), 1) >= lax.broadcasted_iota(jnp.int32, (T, T), 0)
            dv = dk = dq = None
            for h, (kh, vh, qh, doh) in enumerate(zip(_each_head(kv, lo), _each_head(v_ref[...], lo),
                                                      _each_head(qv, lo), _each_head(dov, lo))):
                st = lax.dot_general(k2[h], q2[h], _DN["nt"], preferred_element_type=F32)
                if diagonal:
                    st = jnp.where(causal, st, NEG_INF)
                pt = jnp.exp(st - lse_ref[h])
                d = lax.dot_general(pt.astype(BF16), doh, _DN["nn"], preferred_element_type=F32)
                dv = d if dv is None else dv + d
                dpt = lax.dot_general(vh, dov, _DN["nt"], preferred_element_type=F32)
                dst = pt * (dpt - dl_ref[h])
                dc_sc[h] -= jnp.sum(dst, axis=1, keepdims=True)
                dcq_ref[h, i] += jnp.sum(dst, axis=0, keepdims=True)
                dsb = (dst * SCALE).astype(BF16)
                d = lax.dot_general(dsb, qh, _DN["nn"], preferred_element_type=F32)
                dk = d if dk is None else dk + d
                d = lax.dot_general(dsb, kh, _DN["tn"], preferred_element_type=F32)
                dq = d if dq is None else dq + d
            dv_sc[...] += dv
            dk_sc[...] += dk
            rows = pl.ds(pl.multiple_of(i * T, T), T)
            dq_sc[rows, :] += dq

        @pl.when(i > j)
        def _():
            step(False)

        @pl.when(i == j)
        def _():
            step(True)

        @pl.when(i == nq - 1)
        def _():
            dk_ref[...] = dk_sc[...].astype(BF16)
            dv_ref[...] = dv_sc[...].astype(BF16)
            dc_ref[...] = dc_sc[...]

        @pl.when(jnp.logical_and(j == nq - 1, i == nq - 1))
        def _():
            dq_ref[...] = dq_sc[...].astype(BF16)

    qs = pl.BlockSpec((T, LANES), lambda p, j, i: (jnp.maximum(i, j), p))
    qrow = pl.BlockSpec((2, 1, T), lambda p, j, i: (p, 0, jnp.maximum(i, j)))
    ks = pl.BlockSpec((T, LANES), lambda p, j, i: (j, NP + p))
    vs = pl.BlockSpec((T, LANES), lambda p, j, i: (j, 2 * NP + p))
    kout = pl.BlockSpec((T, LANES), lambda p, j, i: (j, p))
    kcol = pl.BlockSpec((2, T, 1), lambda p, j, i: (p, j, 0))
    dqs = pl.BlockSpec((S, LANES), lambda p, j, i: (0, p))
    dcqs = pl.BlockSpec((2, nq, 1, T), lambda p, j, i: (p, 0, 0, 0))
    wide = jax.ShapeDtypeStruct((S, A_HEADS * HEAD_DIM), BF16)
    return pl.pallas_call(
        body,
        out_shape=[wide, wide, wide, jax.ShapeDtypeStruct((A_HEADS, S, 1), F32),
                   jax.ShapeDtypeStruct((A_HEADS, nq, 1, T), F32)],
        grid=(NP, nq, nq), in_specs=[qs, ks, vs, qs, qrow, qrow, qs, kout], out_specs=[dqs, kout, kout, kcol, dcqs],
        scratch_shapes=[pltpu.VMEM((S, LANES), F32), pltpu.VMEM((T, LANES), F32), pltpu.VMEM((T, LANES), F32),
                        pltpu.VMEM((2, T, 1), F32)],
        compiler_params=_cparams(("parallel", "arbitrary", "arbitrary")), name=name,
    )(qkv, qkv, qkv, do, lse_row, delta_row, aug_q, aug_k)


def _pair_rowdot(a, b, name):
    n = a.shape[1] // HEAD_DIM
    T = 1024

    def body(a_ref, b_ref, o_ref):
        prod = a_ref[...].astype(F32) * b_ref[...].astype(F32)
        lo = _first_head(prod.shape)
        o_ref[0] = jnp.sum(jnp.where(lo, prod, 0.0), axis=1, keepdims=True)
        o_ref[1] = jnp.sum(jnp.where(lo, 0.0, prod), axis=1, keepdims=True)

    blk = pl.BlockSpec((T, LANES), lambda p, i: (i, p))
    return pl.pallas_call(
        body, out_shape=jax.ShapeDtypeStruct((n, S, 1), F32), grid=(n // 2, S // T), in_specs=[blk, blk],
        out_specs=pl.BlockSpec((2, T, 1), lambda p, i: (p, i, 0)),
        compiler_params=_cparams(("parallel", "parallel")), name=name,
    )(a, b)


W = B_WIN
N_HG = 3 * B_HPG
N_BLK = S // W


def _dil_tables():
    slopes = np.exp2((-8.0 * np.arange(1, N_HG + 1, dtype=np.float32) / N_HG).astype(np.float32)).astype(np.float32)
    dil = np.repeat(np.array([d for _, d in B_GROUPS], np.float32), B_HPG)
    coef = (slopes * dil).astype(np.float32)
    nbs = np.repeat(np.array([S // d // W for _, d in B_GROUPS], np.int32), B_HPG)
    return jnp.asarray(coef), jnp.asarray(nbs)


DIL_SUB = 8
DIL_ROWS = DIL_SUB * W
DIL_STEPS = S // DIL_ROWS


def _dil_bias(coef, transposed):
    row = lax.broadcasted_iota(jnp.int32, (W, 2 * W), 0)
    col = lax.broadcasted_iota(jnp.int32, (W, 2 * W), 1)
    dist = (col - row) if transposed else (row + W - col)
    valid = jnp.logical_and(dist >= 0, dist <= W)
    return jnp.where(valid, -coef * dist.astype(F32), NEG_INF), col


def _dil_specs():
    blk = pl.BlockSpec((None, DIL_ROWS, HEAD_DIM), lambda h, n: (h, n, 0))
    prev = pl.BlockSpec((None, W, HEAD_DIM), lambda h, n: (h, jnp.maximum(n * DIL_SUB - 1, 0), 0))
    nxt = pl.BlockSpec((None, W, HEAD_DIM), lambda h, n: (h, jnp.minimum((n + 1) * DIL_SUB, N_BLK - 1), 0))
    col = pl.BlockSpec((None, DIL_ROWS, 1), lambda h, n: (h, n, 0))
    row = pl.BlockSpec((None, 1, DIL_ROWS), lambda h, n: (h, 0, n))
    rnxt = pl.BlockSpec((None, 1, W), lambda h, n: (h, 0, jnp.minimum((n + 1) * DIL_SUB, N_BLK - 1)))
    smem = pl.BlockSpec(memory_space=pltpu.SMEM)
    return blk, prev, nxt, col, row, rnxt, smem


def _dil_fwd(q, k, v, name):
    coef_t, nbs_t = _dil_tables()

    def body(coef_ref, nbs_ref, q_ref, kh_ref, k_ref, vh_ref, v_ref, o_ref, lse_ref, kf, vf):
        hg = pl.program_id(0)
        n = pl.program_id(1)
        nbs = nbs_ref[hg]
        kf[0:W, :] = kh_ref[...]
        kf[W:, :] = k_ref[...]
        vf[0:W, :] = vh_ref[...]
        vf[W:, :] = v_ref[...]
        bias, col = _dil_bias(coef_ref[hg], False)
        for b in range(DIL_SUB):
            first = lax.rem(n * DIL_SUB + b, nbs) == 0
            rows = slice(b * W, (b + 1) * W)
            both = slice(b * W, (b + 2) * W)
            s = lax.dot_general(q_ref[rows, :], kf[both, :], _DN["nt"], preferred_element_type=F32) * SCALE + bias
            s = jnp.where(jnp.logical_and(first, col < W), NEG_INF, s)
            m = jnp.max(s, axis=1, keepdims=True)
            p = jnp.exp(s - m)
            l = jnp.sum(p, axis=1, keepdims=True)
            acc = lax.dot_general(p.astype(BF16), vf[both, :], _DN["nn"], preferred_element_type=F32)
            o_ref[rows, :] = acc / l
            lse_ref[rows, :] = m + jnp.log(l)

    blk, prev, _, col, _, _, smem = _dil_specs()
    return pl.pallas_call(
        body, out_shape=[jax.ShapeDtypeStruct((N_HG, S, HEAD_DIM), F32), jax.ShapeDtypeStruct((N_HG, S, 1), F32)],
        grid=(N_HG, DIL_STEPS), in_specs=[smem, smem, blk, prev, blk, prev, blk], out_specs=[blk, col],
        scratch_shapes=[pltpu.VMEM((DIL_ROWS + W, HEAD_DIM), BF16)] * 2,
        compiler_params=_cparams(("parallel", "parallel")), name=name,
    )(coef_t, nbs_t, q, k, k, v, v)


def _dil_merge(o, lse, name):
    T = 1024

    def body(o_ref, lse_ref, om_ref, omb_ref, l_ref):
        l0, l1, l2 = lse_ref[0], lse_ref[1], lse_ref[2]
        m = jnp.maximum(jnp.maximum(l0, l1), l2)
        e0, e1, e2 = jnp.exp(l0 - m), jnp.exp(l1 - m), jnp.exp(l2 - m)
        den = e0 + e1 + e2
        om = (e0 / den) * o_ref[0] + (e1 / den) * o_ref[1] + (e2 / den) * o_ref[2]
        om_ref[...] = om
        omb_ref[...] = om.astype(BF16)
        l_ref[...] = m + jnp.log(den)

    ob = pl.BlockSpec((None, T, HEAD_DIM), lambda h, i: (h, i, 0))
    lb = pl.BlockSpec((None, T, 1), lambda h, i: (h, i, 0))
    return pl.pallas_call(
        body,
        out_shape=[jax.ShapeDtypeStruct((B_HPG, S, HEAD_DIM), F32), jax.ShapeDtypeStruct((B_HPG, S, HEAD_DIM), BF16),
                   jax.ShapeDtypeStruct((B_HPG, S, 1), F32)],
        grid=(B_HPG, S // T),
        in_specs=[pl.BlockSpec((3, None, T, HEAD_DIM), lambda h, i: (0, h, i, 0)),
                  pl.BlockSpec((3, None, T, 1), lambda h, i: (0, h, i, 0))],
        out_specs=[ob, ob, lb], compiler_params=_cparams(("parallel", "parallel")), name=name,
    )(o, lse)


def _dil_bwd_dq(q, k, v, do, lcol, dcol, name):
    coef_t, nbs_t = _dil_tables()

    def body(coef_ref, nbs_ref, q_ref, kh_ref, k_ref, vh_ref, v_ref, do_ref, l_ref, d_ref, dq_ref, kf, vf):
        hg = pl.program_id(0)
        n = pl.program_id(1)
        nbs = nbs_ref[hg]
        kf[0:W, :] = kh_ref[...]
        kf[W:, :] = k_ref[...]
        vf[0:W, :] = vh_ref[...]
        vf[W:, :] = v_ref[...]
        bias, col = _dil_bias(coef_ref[hg], False)
        for b in range(DIL_SUB):
            first = lax.rem(n * DIL_SUB + b, nbs) == 0
            rows = slice(b * W, (b + 1) * W)
            both = slice(b * W, (b + 2) * W)
            kk = kf[both, :]
            s = lax.dot_general(q_ref[rows, :], kk, _DN["nt"], preferred_element_type=F32) * SCALE + bias
            s = jnp.where(jnp.logical_and(first, col < W), NEG_INF, s)
            p = jnp.exp(s - l_ref[rows, :])
            dp = lax.dot_general(do_ref[rows, :], vf[both, :], _DN["nt"], preferred_element_type=F32)
            ds = (p * (dp - d_ref[rows, :]) * SCALE).astype(BF16)
            dq_ref[rows, :] = lax.dot_general(ds, kk, _DN["nn"], preferred_element_type=F32).astype(BF16)

    blk, prev, _, col, _, _, smem = _dil_specs()
    return pl.pallas_call(
        body, out_shape=jax.ShapeDtypeStruct((N_HG, S, HEAD_DIM), BF16), grid=(N_HG, DIL_STEPS),
        in_specs=[smem, smem, blk, prev, blk, prev, blk, blk, col, col], out_specs=blk,
        scratch_shapes=[pltpu.VMEM((DIL_ROWS + W, HEAD_DIM), BF16)] * 2,
        compiler_params=_cparams(("parallel", "parallel")), name=name,
    )(coef_t, nbs_t, q, k, k, v, v, do, lcol, dcol)


def _dil_bwd_dkv(q, k, v, do, lrow, drow, name):
    coef_t, nbs_t = _dil_tables()

    def body(coef_ref, nbs_ref, k_ref, v_ref, q_ref, qn_ref, do_ref, don_ref, l_ref, ln_ref, d_ref, dn_ref,
             dk_ref, dv_ref, qf, dof, lf, df):
        hg = pl.program_id(0)
        n = pl.program_id(1)
        nbs = nbs_ref[hg]
        qf[0:DIL_ROWS, :] = q_ref[...]
        qf[DIL_ROWS:, :] = qn_ref[...]
        dof[0:DIL_ROWS, :] = do_ref[...]
        dof[DIL_ROWS:, :] = don_ref[...]
        lf[:, 0:DIL_ROWS] = l_ref[...]
        lf[:, DIL_ROWS:] = ln_ref[...]
        df[:, 0:DIL_ROWS] = d_ref[...]
        df[:, DIL_ROWS:] = dn_ref[...]
        bias, col = _dil_bias(coef_ref[hg], True)
        for b in range(DIL_SUB):
            no_next = lax.rem(n * DIL_SUB + b + 1, nbs) == 0
            rows = slice(b * W, (b + 1) * W)
            both = slice(b * W, (b + 2) * W)
            qq = qf[both, :]
            dd = dof[both, :]
            st = lax.dot_general(k_ref[rows, :], qq, _DN["nt"], preferred_element_type=F32) * SCALE + bias
            st = jnp.where(jnp.logical_and(no_next, col >= W), NEG_INF, st)
            pt = jnp.exp(st - lf[:, both])
            dv_ref[rows, :] = lax.dot_general(pt.astype(BF16), dd, _DN["nn"], preferred_element_type=F32).astype(BF16)
            dpt = lax.dot_general(v_ref[rows, :], dd, _DN["nt"], preferred_element_type=F32)
            dst = (pt * (dpt - df[:, both]) * SCALE).astype(BF16)
            dk_ref[rows, :] = lax.dot_general(dst, qq, _DN["nn"], preferred_element_type=F32).astype(BF16)

    blk, _, nxt, _, row, rnxt, smem = _dil_specs()
    return pl.pallas_call(
        body, out_shape=[jax.ShapeDtypeStruct((N_HG, S, HEAD_DIM), BF16)] * 2, grid=(N_HG, DIL_STEPS),
        in_specs=[smem, smem, blk, blk, blk, nxt, blk, nxt, row, rnxt, row, rnxt], out_specs=[blk, blk],
        scratch_shapes=[pltpu.VMEM((DIL_ROWS + W, HEAD_DIM), BF16)] * 2 + [pltpu.VMEM((1, DIL_ROWS + W), F32)] * 2,
        compiler_params=_cparams(("parallel", "parallel")), name=name,
    )(coef_t, nbs_t, k, v, q, q, do, do, lrow, lrow, drow, drow)


NPG = B_HPG // 2
GROUP_W = B_HPG * HEAD_DIM


def _dil_pair_specs(qoff, koff, voff):
    prev_blk = lambda n: jnp.maximum(n * DIL_SUB - 1, 0)
    next_blk = lambda n: jnp.minimum((n + 1) * DIL_SUB, N_BLK - 1)
    return dict(
        o=pl.BlockSpec((DIL_ROWS, LANES), lambda h, n: (n, h)),
        o_next=pl.BlockSpec((W, LANES), lambda h, n: (next_blk(n), h)),
        q=pl.BlockSpec((DIL_ROWS, LANES), lambda h, n: (n, qoff + h)),
        q_next=pl.BlockSpec((W, LANES), lambda h, n: (next_blk(n), qoff + h)),
        k=pl.BlockSpec((DIL_ROWS, LANES), lambda h, n: (n, koff + h)),
        k_prev=pl.BlockSpec((W, LANES), lambda h, n: (prev_blk(n), koff + h)),
        v=pl.BlockSpec((DIL_ROWS, LANES), lambda h, n: (n, voff + h)),
        v_prev=pl.BlockSpec((W, LANES), lambda h, n: (prev_blk(n), voff + h)),
        col=pl.BlockSpec((2, DIL_ROWS, 1), lambda h, n: (h, n, 0)),
        row=pl.BlockSpec((2, 1, DIL_ROWS), lambda h, n: (h, 0, n)),
        row_next=pl.BlockSpec((2, 1, W), lambda h, n: (h, 0, next_blk(n))),
        smem=pl.BlockSpec(memory_space=pltpu.SMEM))


def _dil_pair_fwd(g, q, k, v, qoff, koff, voff, name):
    coef_t, nbs_t = _dil_tables()

    def body(coef_ref, nbs_ref, q_ref, kh_ref, k_ref, vh_ref, v_ref, o_ref, lse_ref, kf, vf):
        hp = pl.program_id(0)
        n = pl.program_id(1)
        nbs = nbs_ref[B_HPG * g + 2 * hp]
        kf[0:W, :] = kh_ref[...]
        kf[W:, :] = k_ref[...]
        vf[0:W, :] = vh_ref[...]
        vf[W:, :] = v_ref[...]
        biases = [_dil_bias(coef_ref[B_HPG * g + 2 * hp + h], False) for h in range(2)]
        col = biases[0][1]
        lo = _first_head((W, LANES))
        lo2 = _first_head((2 * W, LANES))
        for b in range(DIL_SUB):
            first = lax.rem(n * DIL_SUB + b, nbs) == 0
            rows = slice(b * W, (b + 1) * W)
            both = slice(b * W, (b + 2) * W)
            qv = q_ref[rows, :]
            acc, inv = None, []
            for h, (kh, vh) in enumerate(zip(_each_head(kf[both, :], lo2), _each_head(vf[both, :], lo2))):
                s = lax.dot_general(qv, kh, _DN["nt"], preferred_element_type=F32) * SCALE + biases[h][0]
                s = jnp.where(jnp.logical_and(first, col < W), NEG_INF, s)
                m = jnp.max(s, axis=1, keepdims=True)
                p = jnp.exp(s - m)
                l = jnp.sum(p, axis=1, keepdims=True)
                d = lax.dot_general(p.astype(BF16), vh, _DN["nn"], preferred_element_type=F32)
                acc = d if acc is None else acc + d
                inv.append(1.0 / l)
                lse_ref[h, rows, :] = m + jnp.log(l)
            o_ref[rows, :] = acc * jnp.where(lo, inv[0], inv[1])

    sp = _dil_pair_specs(qoff, koff, voff)
    return pl.pallas_call(
        body, out_shape=[jax.ShapeDtypeStruct((S, GROUP_W), F32), jax.ShapeDtypeStruct((B_HPG, S, 1), F32)],
        grid=(NPG, DIL_STEPS), in_specs=[sp["smem"], sp["smem"], sp["q"], sp["k_prev"], sp["k"], sp["v_prev"], sp["v"]],
        out_specs=[sp["o"], sp["col"]], scratch_shapes=[pltpu.VMEM((DIL_ROWS + W, LANES), BF16)] * 2,
        compiler_params=_cparams(("parallel", "parallel")), name=name,
    )(coef_t, nbs_t, q, k, k, v, v)


def _dil_pair_merge(os, lses, name):
    T = 1024

    def body(o0_ref, o1_ref, o2_ref, l0_ref, l1_ref, l2_ref, om_ref, omb_ref, l_ref):
        lo = _first_head((T, LANES))
        weights = []
        for h in range(2):
            l0, l1, l2 = l0_ref[h], l1_ref[h], l2_ref[h]
            m = jnp.maximum(jnp.maximum(l0, l1), l2)
            e0, e1, e2 = jnp.exp(l0 - m), jnp.exp(l1 - m), jnp.exp(l2 - m)
            den = e0 + e1 + e2
            weights.append((e0 / den, e1 / den, e2 / den))
            l_ref[h] = m + jnp.log(den)
        om = (jnp.where(lo, weights[0][0], weights[1][0]) * o0_ref[...]
              + jnp.where(lo, weights[0][1], weights[1][1]) * o1_ref[...]
              + jnp.where(lo, weights[0][2], weights[1][2]) * o2_ref[...])
        om_ref[...] = om
        omb_ref[...] = om.astype(BF16)

    ob = pl.BlockSpec((T, LANES), lambda p, i: (i, p))
    lb = pl.BlockSpec((2, T, 1), lambda p, i: (p, i, 0))
    return pl.pallas_call(
        body,
        out_shape=[jax.ShapeDtypeStruct((S, B_OUT_W), F32), jax.ShapeDtypeStruct((S, B_OUT_W), BF16),
                   jax.ShapeDtypeStruct((B_HPG, S, 1), F32)],
        grid=(NPG, S // T), in_specs=[ob] * 3 + [lb] * 3, out_specs=[ob, ob, lb],
        compiler_params=_cparams(("parallel", "parallel")), name=name,
    )(*os, *lses)


def _dil_pair_dq(g, q, k, v, qoff, koff, voff, do, lcol, dcol, name):
    coef_t, nbs_t = _dil_tables()

    def body(coef_ref, nbs_ref, q_ref, kh_ref, k_ref, vh_ref, v_ref, do_ref, l_ref, d_ref, dq_ref, kf, vf):
        hp = pl.program_id(0)
        n = pl.program_id(1)
        nbs = nbs_ref[B_HPG * g + 2 * hp]
        kf[0:W, :] = kh_ref[...]
        kf[W:, :] = k_ref[...]
        vf[0:W, :] = vh_ref[...]
        vf[W:, :] = v_ref[...]
        biases = [_dil_bias(coef_ref[B_HPG * g + 2 * hp + h], False) for h in range(2)]
        col = biases[0][1]
        lo2 = _first_head((2 * W, LANES))
        for b in range(DIL_SUB):
            first = lax.rem(n * DIL_SUB + b, nbs) == 0
            rows = slice(b * W, (b + 1) * W)
            both = slice(b * W, (b + 2) * W)
            qv = q_ref[rows, :]
            dov = do_ref[rows, :]
            acc = None
            for h, (kh, vh) in enumerate(zip(_each_head(kf[both, :], lo2), _each_head(vf[both, :], lo2))):
                s = lax.dot_general(qv, kh, _DN["nt"], preferred_element_type=F32) * SCALE + biases[h][0]
                s = jnp.where(jnp.logical_and(first, col < W), NEG_INF, s)
                p = jnp.exp(s - l_ref[h, rows, :])
                dp = lax.dot_general(dov, vh, _DN["nt"], preferred_element_type=F32)
                ds = (p * (dp - d_ref[h, rows, :]) * SCALE).astype(BF16)
                d = lax.dot_general(ds, kh, _DN["nn"], preferred_element_type=F32)
                acc = d if acc is None else acc + d
            dq_ref[rows, :] = acc.astype(BF16)

    sp = _dil_pair_specs(qoff, koff, voff)
    return pl.pallas_call(
        body, out_shape=jax.ShapeDtypeStruct((S, GROUP_W), BF16), grid=(NPG, DIL_STEPS),
        in_specs=[sp["smem"], sp["smem"], sp["q"], sp["k_prev"], sp["k"], sp["v_prev"], sp["v"], sp["o"], sp["col"],
                  sp["col"]],
        out_specs=sp["o"], scratch_shapes=[pltpu.VMEM((DIL_ROWS + W, LANES), BF16)] * 2,
        compiler_params=_cparams(("parallel", "parallel")), name=name,
    )(coef_t, nbs_t, q, k, k, v, v, do, lcol, dcol)


def _dil_pair_dkv(g, q, k, v, qoff, koff, voff, do, lrow, drow, name):
    coef_t, nbs_t = _dil_tables()

    def body(coef_ref, nbs_ref, k_ref, v_ref, q_ref, qn_ref, do_ref, don_ref, l_ref, ln_ref, d_ref, dn_ref,
             dk_ref, dv_ref, qf, dof, lf, df):
        hp = pl.program_id(0)
        n = pl.program_id(1)
        nbs = nbs_ref[B_HPG * g + 2 * hp]
        qf[0:DIL_ROWS, :] = q_ref[...]
        qf[DIL_ROWS:, :] = qn_ref[...]
        dof[0:DIL_ROWS, :] = do_ref[...]
        dof[DIL_ROWS:, :] = don_ref[...]
        lf[:, :, 0:DIL_ROWS] = l_ref[...]
        lf[:, :, DIL_ROWS:] = ln_ref[...]
        df[:, :, 0:DIL_ROWS] = d_ref[...]
        df[:, :, DIL_ROWS:] = dn_ref[...]
        biases = [_dil_bias(coef_ref[B_HPG * g + 2 * hp + h], True) for h in range(2)]
        col = biases[0][1]
        lo = _first_head((W, LANES))
        lo2 = _first_head((2 * W, LANES))
        for b in range(DIL_SUB):
            no_next = lax.rem(n * DIL_SUB + b + 1, nbs) == 0
            rows = slice(b * W, (b + 1) * W)
            both = slice(b * W, (b + 2) * W)
            dd = dof[both, :]
            dk = dv = None
            for h, (kh, vh, qh, ddh) in enumerate(zip(_each_head(k_ref[rows, :], lo), _each_head(v_ref[rows, :], lo),
                                                      _each_head(qf[both, :], lo2), _each_head(dd, lo2))):
                st = lax.dot_general(kh, qh, _DN["nt"], preferred_element_type=F32) * SCALE + biases[h][0]
                st = jnp.where(jnp.logical_and(no_next, col >= W), NEG_INF, st)
                pt = jnp.exp(st - lf[h, :, both])
                d = lax.dot_general(pt.astype(BF16), ddh, _DN["nn"], preferred_element_type=F32)
                dv = d if dv is None else dv + d
                dpt = lax.dot_general(vh, dd, _DN["nt"], preferred_element_type=F32)
                dst = (pt * (dpt - df[h, :, both]) * SCALE).astype(BF16)
                d = lax.dot_general(dst, qh, _DN["nn"], preferred_element_type=F32)
                dk = d if dk is None else dk + d
            dk_ref[rows, :] = dk.astype(BF16)
            dv_ref[rows, :] = dv.astype(BF16)

    sp = _dil_pair_specs(qoff, koff, voff)
    wide = jax.ShapeDtypeStruct((S, GROUP_W), BF16)
    return pl.pallas_call(
        body, out_shape=[wide, wide], grid=(NPG, DIL_STEPS),
        in_specs=[sp["smem"], sp["smem"], sp["k"], sp["v"], sp["q"], sp["q_next"], sp["o"], sp["o_next"], sp["row"],
                  sp["row_next"], sp["row"], sp["row_next"]],
        out_specs=[sp["o"], sp["o"]],
        scratch_shapes=[pltpu.VMEM((DIL_ROWS + W, LANES), BF16)] * 2 + [pltpu.VMEM((2, 1, DIL_ROWS + W), F32)] * 2,
        compiler_params=_cparams(("parallel", "parallel")), name=name,
    )(coef_t, nbs_t, k, v, q, q, do, do, lrow, lrow, drow, drow)


FFN_ROWS = 512
FFN_COLS = 256
HALO = 8


def _shifted(u, halo, back):
    T = u.shape[0]
    rows = lax.broadcasted_iota(jnp.int32, u.shape, 0)
    if back:
        s1 = jnp.where(rows == 0, halo[HALO - 1:HALO, :], pltpu.roll(u, 1, 0))
        s2 = jnp.where(rows == 0, halo[HALO - 2:HALO - 1, :],
                       jnp.where(rows == 1, halo[HALO - 1:HALO, :], pltpu.roll(u, 2, 0)))
    else:
        s1 = jnp.where(rows == T - 1, halo[0:1, :], pltpu.roll(u, T - 1, 0))
        s2 = jnp.where(rows == T - 1, halo[1:2, :],
                       jnp.where(rows == T - 2, halo[0:1, :], pltpu.roll(u, T - 2, 0)))
    return s1, s2


def _conv_parts(u_ref, h_ref, w_ref, b_ref, first):
    out = []
    for p in range(2):
        u = u_ref[p]
        halo = jnp.where(first, 0.0, h_ref[p])
        u1, u2 = _shifted(u, halo, True)
        w = w_ref[p]
        out.append((w[0:1, :] * u2 + w[1:2, :] * u1 + w[2:3, :] * u + b_ref[p], u1, u2, u))
    return out


def _ffn_specs():
    T, C = FFN_ROWS, FFN_COLS
    blk = pl.BlockSpec((2, T, C), lambda j, i: (0, i, j))
    prev = pl.BlockSpec((2, HALO, C), lambda j, i: (0, jnp.maximum(i * (T // HALO) - 1, 0), j))
    nxt = pl.BlockSpec((2, HALO, C), lambda j, i: (0, jnp.minimum((i + 1) * (T // HALO), S // HALO - 1), j))
    wsp = pl.BlockSpec((2, 3, C), lambda j, i: (0, 0, j))
    bsp = pl.BlockSpec((2, 1, C), lambda j, i: (0, 0, j))
    one = pl.BlockSpec((T, C), lambda j, i: (i, j))
    return blk, prev, nxt, wsp, bsp, one


def _ffn_act_fwd(u, w, b, name):
    blk, prev, _, wsp, bsp, one = _ffn_specs()

    def body(u_ref, h_ref, w_ref, b_ref, o_ref):
        (a, _, _, _), (g, _, _, _) = _conv_parts(u_ref, h_ref, w_ref, b_ref, pl.program_id(1) == 0)
        o_ref[...] = (g / (1.0 + jnp.exp(-g)) * a).astype(BF16)

    return pl.pallas_call(
        body, out_shape=jax.ShapeDtypeStruct((S, FF), BF16), grid=(FF // FFN_COLS, S // FFN_ROWS),
        in_specs=[blk, prev, wsp, bsp], out_specs=one,
        compiler_params=_cparams(("parallel", "parallel")), name=name,
    )(u, u, w, b)


def _ffn_act_bwd(u, dact, w, b, name):
    blk, prev, _, wsp, bsp, one = _ffn_specs()

    def body(u_ref, h_ref, da_ref, w_ref, b_ref, duc_ref, dwb_ref):
        i = pl.program_id(1)
        (a, a1, a2, a0), (g, g1, g2, g0) = _conv_parts(u_ref, h_ref, w_ref, b_ref, i == 0)
        dact_v = da_ref[...]
        sg = 1.0 / (1.0 + jnp.exp(-g))
        d_a = dact_v * (g * sg)
        d_g = dact_v * a * (sg * (1.0 + g * (1.0 - sg)))
        duc_ref[0] = d_a
        duc_ref[1] = d_g

        @pl.when(i == 0)
        def _():
            dwb_ref[...] = jnp.zeros(dwb_ref.shape, F32)

        for p, (d, s2, s1, s0) in enumerate(((d_a, a2, a1, a0), (d_g, g2, g1, g0))):
            dwb_ref[p, 0:1, :] += jnp.sum(d * s2, axis=0, keepdims=True)
            dwb_ref[p, 1:2, :] += jnp.sum(d * s1, axis=0, keepdims=True)
            dwb_ref[p, 2:3, :] += jnp.sum(d * s0, axis=0, keepdims=True)
            dwb_ref[p, 3:4, :] += jnp.sum(d, axis=0, keepdims=True)

    return pl.pallas_call(
        body, out_shape=[jax.ShapeDtypeStruct((2, S, FF), F32), jax.ShapeDtypeStruct((2, 8, FF), F32)],
        grid=(FF // FFN_COLS, S // FFN_ROWS), in_specs=[blk, prev, one, wsp, bsp],
        out_specs=[blk, pl.BlockSpec((2, 8, FFN_COLS), lambda j, i: (0, 0, j))],
        compiler_params=_cparams(("parallel", "arbitrary")), name=name,
    )(u, u, dact, w, b)


def _ffn_conv_bwd(duc, w, name):
    blk, _, nxt, wsp, _, _ = _ffn_specs()
    last = S // FFN_ROWS - 1

    def body(d_ref, h_ref, w_ref, du_ref):
        is_last = pl.program_id(1) == last
        for p in range(2):
            d = d_ref[p]
            halo = jnp.where(is_last, 0.0, h_ref[p])
            d1, d2 = _shifted(d, halo, False)
            wv = w_ref[p]
            du_ref[p] = (wv[2:3, :] * d + wv[1:2, :] * d1 + wv[0:1, :] * d2).astype(BF16)

    return pl.pallas_call(
        body, out_shape=jax.ShapeDtypeStruct((2, S, FF), BF16), grid=(FF // FFN_COLS, S // FFN_ROWS),
        in_specs=[blk, nxt, wsp], out_specs=blk,
        compiler_params=_cparams(("parallel", "parallel")), name=name,
    )(duc, duc, w)


def _adam_update(w, gv, m, v):
    c1 = 1.0 / (1.0 - ADAM_B1 ** ADAM_STEP)
    c2 = 1.0 / (1.0 - ADAM_B2 ** ADAM_STEP)
    mn = ADAM_B1 * m + (1.0 - ADAM_B1) * gv
    vn = ADAM_B2 * v + (1.0 - ADAM_B2) * (gv * gv)
    return -ADAM_LR * ((mn * c1) / (jnp.sqrt(vn * c2) + ADAM_EPS) + ADAM_WD * w), mn, vn


def _adamw(w, g, m, v, name):
    rows = w.shape[0]
    T = 8
    for cand in (256, 128, 64, 32, 16, 8):
        if rows % cand == 0:
            T = cand
            break

    def body(w_ref, g_ref, m_ref, v_ref, d_ref, mo_ref, vo_ref):
        d_ref[...], mo_ref[...], vo_ref[...] = _adam_update(w_ref[...], g_ref[...], m_ref[...], v_ref[...])

    blk = pl.BlockSpec((T, w.shape[1]), lambda i: (i, 0))
    sds = jax.ShapeDtypeStruct(w.shape, F32)
    return pl.pallas_call(
        body, out_shape=[sds, sds, sds], grid=(rows // T,), in_specs=[blk] * 4, out_specs=[blk] * 3,
        compiler_params=_cparams(("parallel",)), name=name,
    )(w, g, m, v)


ANY = pl.BlockSpec(memory_space=pl.ANY)


def _place():
    x, y, c = lax.axis_index("x"), lax.axis_index("y"), lax.axis_index("c")
    chips = [(1 - x, y), (x, 1 - y), (1 - x, 1 - y)]
    return x, y, c, chips


def _place_own(w, slot_arr, name):
    rows = w.shape[0]
    T = 16
    for cand in (2048, 1024, 512, 256, 128, 64, 32, 16):
        if rows % cand == 0:
            T = cand
            break

    def body(k_ref, w_ref, o_ref):
        o_ref[...] = w_ref[...]

    return pl.pallas_call(
        body, out_shape=jax.ShapeDtypeStruct((N_CHIPS, rows, FLAT_W), w.dtype),
        grid_spec=pltpu.PrefetchScalarGridSpec(
            num_scalar_prefetch=1, grid=(rows // T,),
            in_specs=[pl.BlockSpec((T, FLAT_W), lambda i, k: (i, 0))],
            out_specs=pl.BlockSpec((None, T, FLAT_W), lambda i, k: (k[0], i, 0))),
        compiler_params=_cparams(("parallel",)), name=name,
    )(slot_arr, w)


def _allgather_shards(w, buf):
    half_rows = w.shape[0] // 2
    assert half_rows % 16 == 0

    def body(w_ref, buf_ref, g_ref, send_sems, recv_sems):
        x, y, c, chips = _place()
        myk = 2 * x + y
        sibling = (x, y, 1 - c)
        h0 = pl.multiple_of(c * half_rows, 16)
        h1 = pl.multiple_of((1 - c) * half_rows, 16)

        def half(k, start):
            return g_ref.at[k, pl.ds(start, half_rows), :]

        def rcopy(sem, src, dst, to):
            return pltpu.make_async_remote_copy(src_ref=src, dst_ref=dst, send_sem=send_sems.at[sem],
                                                recv_sem=recv_sems.at[sem], device_id=to, device_id_type=MESH)

        ici = [rcopy(r, w_ref.at[pl.ds(h0, half_rows), :], half(myk, h0), (*chip, c)) for r, chip in enumerate(chips)]
        for cp in ici:
            cp.start()
        ks = [2 * cx + cy for cx, cy in chips]
        fwd = [rcopy(3 + r, half(ks[r], h0), half(ks[r], h0), sibling) for r in range(3)]
        for r in range(3):
            rcopy(r, half(ks[r], h0), half(ks[r], h0), (*chips[r], c)).wait_recv()
            fwd[r].start()
        for r in range(3):
            rcopy(3 + r, half(ks[r], h1), half(ks[r], h1), sibling).wait_recv()
        for cp in ici + fwd:
            cp.wait_send()

    return pl.pallas_call(
        body, out_shape=jax.ShapeDtypeStruct(buf.shape, w.dtype), in_specs=[ANY, ANY], out_specs=ANY,
        scratch_shapes=[pltpu.SemaphoreType.DMA((6,)), pltpu.SemaphoreType.DMA((6,))],
        input_output_aliases={1: 0},
        compiler_params=pltpu.CompilerParams(has_side_effects=True), name="allgather_shards",
    )(w, buf)


HBM_SPEC = pl.BlockSpec(memory_space=pltpu.HBM)
SEM_SPEC = pl.BlockSpec(memory_space=pltpu.SEMAPHORE)
DATAFLOW = pltpu.SideEffectType.DATAFLOW_SIDE_EFFECTING
OWN_SLOT = 3


def _late_gather_start(w, land):
    def body(w_ref, land_ref, send_sems, recv_sems, w_thru, land_thru, token):
        x, y, c, chips = _place()
        for r, chip in enumerate(chips):
            pltpu.make_async_remote_copy(src_ref=w_ref, dst_ref=land_ref.at[r], send_sem=send_sems.at[r],
                                         recv_sem=recv_sems.at[r], device_id=(*chip, c), device_id_type=MESH).start()
        token[...] = jnp.zeros_like(token)

    return pl.pallas_call(
        body, name="late_gather_start",
        out_shape=(pltpu.SemaphoreType.DMA((3,)), pltpu.SemaphoreType.DMA((3,)), pltpu.HBM(w.shape, w.dtype),
                   pltpu.HBM(land.shape, land.dtype), jax.ShapeDtypeStruct((8, LANES), F32)),
        in_specs=(HBM_SPEC, HBM_SPEC),
        out_specs=(SEM_SPEC, SEM_SPEC, HBM_SPEC, HBM_SPEC, pl.BlockSpec(memory_space=pltpu.VMEM)),
        input_output_aliases={0: 2, 1: 3}, compiler_params=pltpu.CompilerParams(has_side_effects=DATAFLOW),
    )(pltpu.with_memory_space_constraint(w, pltpu.HBM), pltpu.with_memory_space_constraint(land, pltpu.HBM))


def _late_gather_wait(send_sems, recv_sems, w_thru, land_thru, after):
    def body(w_ref, land_ref, send_sems, recv_sems, after_ref, w_dead, got_ref):
        x, y, c, chips = _place()
        for r, chip in enumerate(chips):
            cp = pltpu.make_async_remote_copy(src_ref=w_ref, dst_ref=land_ref.at[r], send_sem=send_sems.at[r],
                                              recv_sem=recv_sems.at[r], device_id=(*chip, c), device_id_type=MESH)
            cp.wait_send()
            cp.wait_recv()

    return pl.pallas_call(
        body, name="late_gather_wait",
        out_shape=(pltpu.HBM(w_thru.shape, w_thru.dtype), pltpu.HBM(land_thru.shape, land_thru.dtype)),
        in_specs=(HBM_SPEC, HBM_SPEC, SEM_SPEC, SEM_SPEC, pl.BlockSpec(memory_space=pl.ANY)),
        out_specs=(HBM_SPEC, HBM_SPEC), input_output_aliases={0: 0, 1: 1},
        compiler_params=pltpu.CompilerParams(has_side_effects=DATAFLOW),
    )(w_thru, land_thru, send_sems, recv_sems, after)[1]


def _sibling_swap_half(g):
    def body(g_ref, o_ref, send_sem, recv_sem):
        x, y, c, _ = _place()
        theirs = pl.multiple_of((1 - c) * HALF_ROWS, 8)
        cp = pltpu.make_async_remote_copy(src_ref=g_ref.at[:, pl.ds(theirs, HALF_ROWS), :], dst_ref=o_ref,
                                          send_sem=send_sem, recv_sem=recv_sem, device_id=(x, y, 1 - c),
                                          device_id_type=MESH)
        cp.start()
        cp.wait()

    return pl.pallas_call(
        body, out_shape=jax.ShapeDtypeStruct((N_CHIPS, HALF_ROWS, FLAT_W), F32), in_specs=[ANY], out_specs=ANY,
        scratch_shapes=[pltpu.SemaphoreType.DMA, pltpu.SemaphoreType.DMA],
        compiler_params=pltpu.CompilerParams(has_side_effects=True), name="rs_sibling_swap",
    )(g)


def _pair_sum(g, other, c_arr):
    T = FLAT_T

    def body(c_ref, g_ref, o_ref, s_ref):
        s_ref[...] = (g_ref[...] + o_ref[...]).astype(BF16)

    nb = HALF_ROWS // T
    return pl.pallas_call(
        body, out_shape=jax.ShapeDtypeStruct((N_CHIPS, HALF_ROWS, FLAT_W), BF16),
        grid_spec=pltpu.PrefetchScalarGridSpec(
            num_scalar_prefetch=1, grid=(N_CHIPS, nb),
            in_specs=[pl.BlockSpec((None, T, FLAT_W), lambda k, i, c: (k, c[0] * nb + i, 0)),
                      pl.BlockSpec((None, T, FLAT_W), lambda k, i, c: (k, i, 0))],
            out_specs=pl.BlockSpec((None, T, FLAT_W), lambda k, i, c: (k, i, 0))),
        compiler_params=_cparams(("parallel", "parallel")), name="rs_pair_sum",
    )(c_arr, g, other)


def _chip_exchange(s):
    def body(s_ref, o_ref, send_sems, recv_sems):
        x, y, c, chips = _place()
        cps = []
        for r, (cx, cy) in enumerate(chips):
            cps.append(pltpu.make_async_remote_copy(
                src_ref=s_ref.at[2 * cx + cy], dst_ref=o_ref.at[r], send_sem=send_sems.at[r],
                recv_sem=recv_sems.at[r], device_id=(cx, cy, c), device_id_type=MESH))
        for cp in cps:
            cp.start()
        for cp in cps:
            cp.wait()

    return pl.pallas_call(
        body, out_shape=jax.ShapeDtypeStruct((3, HALF_ROWS, FLAT_W), BF16), in_specs=[ANY], out_specs=ANY,
        scratch_shapes=[pltpu.SemaphoreType.DMA((3,)), pltpu.SemaphoreType.DMA((3,))],
        compiler_params=pltpu.CompilerParams(has_side_effects=True), name="rs_chip_exchange",
    )(s)


def _chip_sum(s, r, k_arr):
    T = FLAT_T

    def body(k_ref, s_ref, r_ref, o_ref):
        o_ref[...] = ((s_ref[...].astype(F32) + r_ref[0].astype(F32)) + r_ref[1].astype(F32)) + r_ref[2].astype(F32)

    return pl.pallas_call(
        body, out_shape=jax.ShapeDtypeStruct((HALF_ROWS, FLAT_W), F32),
        grid_spec=pltpu.PrefetchScalarGridSpec(
            num_scalar_prefetch=1, grid=(HALF_ROWS // T,),
            in_specs=[pl.BlockSpec((None, T, FLAT_W), lambda i, k: (k[0], i, 0)),
                      pl.BlockSpec((3, T, FLAT_W), lambda i, k: (0, i, 0))],
            out_specs=pl.BlockSpec((T, FLAT_W), lambda i, k: (i, 0))),
        compiler_params=_cparams(("parallel",)), name="rs_chip_sum",
    )(k_arr, s, r)


def _sibling_send(t):
    def body(t_ref, o_ref, send_sem, recv_sem):
        x, y, c, _ = _place()
        cp = pltpu.make_async_remote_copy(src_ref=t_ref, dst_ref=o_ref, send_sem=send_sem, recv_sem=recv_sem,
                                          device_id=(x, y, 1 - c), device_id_type=MESH)
        cp.start()
        cp.wait()

    return pl.pallas_call(
        body, out_shape=jax.ShapeDtypeStruct((HALF_ROWS, FLAT_W), F32), in_specs=[ANY], out_specs=ANY,
        scratch_shapes=[pltpu.SemaphoreType.DMA, pltpu.SemaphoreType.DMA],
        compiler_params=pltpu.CompilerParams(has_side_effects=True), name="rs_sibling_send",
    )(t)


def _allreduce_small(v):
    def body(v_ref, o_ref, buf, send_sems, recv_sems):
        x, y, c, _ = _place()
        me = 4 * x + 2 * y + c
        buf[me] = v_ref[...]
        cps = []
        for mask in range(1, 8):
            a, b, d = (mask >> 2) & 1, (mask >> 1) & 1, mask & 1
            peer = (x + a - 2 * a * x, y + b - 2 * b * y, c + d - 2 * d * c)
            cps.append(pltpu.make_async_remote_copy(
                src_ref=v_ref, dst_ref=buf.at[me], send_sem=send_sems.at[mask - 1], recv_sem=recv_sems.at[mask - 1],
                device_id=peer, device_id_type=MESH))
        for cp in cps:
            cp.start()
        for cp in cps:
            cp.wait()
        total = buf[0]
        for dev in range(1, 8):
            total = total + buf[dev]
        o_ref[...] = total

    vm = pl.BlockSpec(memory_space=pltpu.VMEM)
    return pl.pallas_call(
        body, out_shape=jax.ShapeDtypeStruct((SMALL_ROWS, 1024), F32), in_specs=[vm], out_specs=vm,
        scratch_shapes=[pltpu.VMEM((8, SMALL_ROWS, 1024), F32), pltpu.SemaphoreType.DMA((7,)),
                        pltpu.SemaphoreType.DMA((7,))],
        compiler_params=pltpu.CompilerParams(has_side_effects=True), name="allreduce_small",
    )(v)


def _heads(t, n):
    return t.reshape(S, n, HEAD_DIM).transpose(1, 0, 2)


def _unheads(t):
    return t.transpose(1, 0, 2).reshape(S, t.shape[0] * HEAD_DIM)


def _to_residue(t, d):
    c = t.shape[-1]
    return t.reshape(B_HPG, S // d, d, c).transpose(0, 2, 1, 3).reshape(B_HPG, S, c)


def _from_residue(t, d):
    c = t.shape[-1]
    return t.reshape(B_HPG, d, S // d, c).transpose(0, 2, 1, 3).reshape(B_HPG, S, c)


def _dil_pack(t):
    return jnp.concatenate([_to_residue(t[g], d) for g, (_, d) in enumerate(B_GROUPS)], axis=0)


def _dil_unpack(t):
    return jnp.stack([_from_residue(t[g * B_HPG:(g + 1) * B_HPG], d) for g, (_, d) in enumerate(B_GROUPS)], axis=0)


def _col_to_row(t):
    return t.reshape(t.shape[0], 1, S)


def _residue_rows(t, d, inverse=False):
    if d == 1:
        return t
    shape = (d, S // d) if inverse else (S // d, d)
    return t.reshape(shape + t.shape[1:]).transpose(1, 0, 2).reshape(t.shape)


def _residue_vecs(t, d, inverse=False):
    if d == 1:
        return t
    shape = (d, S // d) if inverse else (S // d, d)
    return t.reshape((B_HPG,) + shape).transpose(0, 2, 1).reshape(B_HPG, S, 1)


def _ffn_fwd(x, g, w_up, cw, cb, w_down, tag):
    h = _rms_fwd(x, g, f"{tag}_norm")
    u = _mm(h, w_up, mode="nn", tm=1024, tn=1408, tk=1024, o_split=2, name=f"{tag}_up")
    act = _ffn_act_fwd(u, cw, cb, f"{tag}_act")
    x_out = _mm(act, w_down, mode="nn", tm=1024, tn=512, tk=FF, res=x, name=f"{tag}_down")
    return x_out, (h, u, act)


def _ffn_bwd(x, g, w_up, cw, cb, w_down, saved, dx, dxb, tag):
    h, u, act = saved
    d_w_down = _mm(act, dxb, mode="tn", tm=1408, tn=512, tk=1024, name=f"{tag}_dwdown")
    dact = _mm(dxb, w_down, mode="nt", tm=1024, tn=1408, tk=1024, name=f"{tag}_dact")
    duc, dwb = _ffn_act_bwd(u, dact, cw, cb, f"{tag}_dgate")
    du = _ffn_conv_bwd(duc, cw, f"{tag}_dconv")
    d_w_up = _mm(h, du, mode="tn", tm=1024, tn=1408, tk=1024, b_split=2, name=f"{tag}_dwup")
    dh = _mm(du, w_up, mode="nt", tm=1024, tn=512, tk=1408, a_split=2, name=f"{tag}_dh")
    dx_new, dxb_new, (dg,) = _rms_bwd(x, dx, [(g, dh)], f"{tag}_dnorm")
    d_cw = dwb[:, 0:3, :].transpose(1, 0, 2).reshape(3, 2 * FF)
    d_cb = dwb[:, 3, :].reshape(2 * FF)
    return dx_new, dxb_new, dict(w_up=d_w_up, w_down=d_w_down, conv_w=d_cw, conv_b=d_cb, norm_g=dg.reshape(D))


def _local_step(x, target, p, late_weights):
    g = {}
    h1 = _rms_fwd(x, p["mix_norm_g"][0], "a_norm")
    w_qkv = p["a_w_in"][:, :QKV_W]
    w_f = jnp.pad(p["a_w_in"][:, QKV_W:], ((0, 0), (0, LANES - A_HEADS)))
    b_f = jnp.pad(p["a_b_f"].reshape(1, A_HEADS), ((0, 0), (0, LANES - A_HEADS)))
    qkv = _mm(h1, w_qkv, mode="nn", tm=1024, tn=512, tk=1024, out_dtype=BF16, name="a_qkv")
    pf = _mm(h1, w_f, mode="nn", tm=1024, tn=LANES, tk=1024, name="a_gate")
    cum = _fgate_fwd(pf, b_f, "a_gate_scan")
    aug_q, aug_k = _gate_lanes(cum)
    oa2, lse_a = _fox_pair_fwd(qkv, aug_q, aug_k, "a_attn")
    x1 = _mm(oa2, p["a_w_out"], mode="nn", tm=1024, tn=512, tk=1024, res=x, name="a_out")
    p = {**p, **late_weights(x1)}
    x2, ffn0 = _ffn_fwd(x1, p["ffn_norm_g"][0], p["ffn_w_up"][0], p["conv_w"][0], p["conv_b"][0], p["ffn_w_down"][0], "f0")
    hk = _rms_fwd(x2, p["kv_norm_g"], "kv_norm")
    kvb = _mm(hk, p["w_kv"], mode="nn", tm=1024, tn=512, tk=1024, out_dtype=BF16, name="kv_proj")
    h3 = _rms_fwd(x2, p["mix_norm_g"][1], "b_norm")
    qb = _mm(h3, p["b_w_q"], mode="nn", tm=1024, tn=512, tk=1024, out_dtype=BF16, name="b_q")
    dil_in = []
    for gi, (_, d) in enumerate(B_GROUPS):
        if d == 1:
            dil_in.append((qb, kvb, kvb, gi * NPG, gi * NPG, (3 + gi) * NPG))
        else:
            qg = _residue_rows(qb[:, gi * GROUP_W:(gi + 1) * GROUP_W], d)
            kvg = _residue_rows(kvb.reshape(S, 2, 3, GROUP_W)[:, :, gi, :].reshape(S, 2 * GROUP_W), d)
            dil_in.append((qg, kvg, kvg, 0, 0, NPG))
    o_g, lse_g = [], []
    for gi, (_, d) in enumerate(B_GROUPS):
        qg, kg, vg, qoff, koff, voff = dil_in[gi]
        og, lg = _dil_pair_fwd(gi, qg, kg, vg, qoff, koff, voff, f"b_attn{gi}")
        o_g.append(_residue_rows(og, d, inverse=True))
        lse_g.append(_residue_vecs(lg, d, inverse=True))
    ob, ob2, lse_b = _dil_pair_merge(o_g, lse_g, "b_merge")
    x3 = _mm(ob2, p["b_w_out"], mode="nn", tm=1024, tn=512, tk=B_OUT_W, res=x2, name="b_out")
    x4, ffn1 = _ffn_fwd(x3, p["ffn_norm_g"][1], p["ffn_w_up"][1], p["conv_w"][1], p["conv_b"][1], p["ffn_w_down"][1], "f1")
    loss, dx, dxb, dg_final = _loss_head(x4, p["final_norm_g"], target, "loss_head")
    g["final_norm_g"] = dg_final.reshape(D)

    dx, dxb, gf1 = _ffn_bwd(x3, p["ffn_norm_g"][1], p["ffn_w_up"][1], p["conv_w"][1], p["conv_b"][1], p["ffn_w_down"][1],
                            ffn1, dx, dxb, "f1")
    g["b_w_out"] = _mm(ob2, dxb, mode="tn", tm=B_OUT_W, tn=512, tk=1024, name="b_dwout")
    dob = _mm(dxb, p["b_w_out"], mode="nt", tm=1024, tn=B_OUT_W, tk=1024, name="b_do")
    delta_b = _pair_rowdot(dob, ob, "b_delta")
    dob16 = dob.astype(BF16)
    dq_g, dk_g, dv_g = [], [], []
    for gi, (_, d) in enumerate(B_GROUPS):
        qg, kg, vg, qoff, koff, voff = dil_in[gi]
        dog, l_d, dl_d = _residue_rows(dob16, d), _residue_vecs(lse_b, d), _residue_vecs(delta_b, d)
        dqd = _dil_pair_dq(gi, qg, kg, vg, qoff, koff, voff, dog, l_d, dl_d, f"b_dq{gi}")
        dkd, dvd = _dil_pair_dkv(gi, qg, kg, vg, qoff, koff, voff, dog, _col_to_row(l_d), _col_to_row(dl_d), f"b_dkv{gi}")
        dq_g.append(_residue_rows(dqd, d, inverse=True))
        dk_g.append(_residue_rows(dkd, d, inverse=True))
        dv_g.append(_residue_rows(dvd, d, inverse=True))
    dqb = jnp.concatenate(dq_g, axis=1)
    dkvb = jnp.concatenate(dk_g + dv_g, axis=1)
    g["b_w_q"] = _mm(h3, dqb, mode="tn", tm=1024, tn=512, tk=1024, name="b_dwq")
    dh3 = _mm(dqb, p["b_w_q"], mode="nt", tm=1024, tn=512, tk=B_Q_W, name="b_dh")
    g["w_kv"] = _mm(hk, dkvb, mode="tn", tm=1024, tn=512, tk=1024, name="kv_dw")
    dhk = _mm(dkvb, p["w_kv"], mode="nt", tm=1024, tn=512, tk=1536, name="kv_dh")
    dx, dxb, (dg_mix1, dg_kv) = _rms_bwd(x2, dx, [(p["mix_norm_g"][1], dh3), (p["kv_norm_g"], dhk)], "b_dnorm")
    g["kv_norm_g"] = dg_kv.reshape(D)
    dx, dxb, gf0 = _ffn_bwd(x1, p["ffn_norm_g"][0], p["ffn_w_up"][0], p["conv_w"][0], p["conv_b"][0], p["ffn_w_down"][0],
                            ffn0, dx, dxb, "f0")
    g["a_w_out"] = _mm(oa2, dxb, mode="tn", tm=1024, tn=512, tk=1024, name="a_dwout")
    doa = _mm(dxb, p["a_w_out"], mode="nt", tm=1024, tn=512, tk=1024, name="a_do")
    delta_a = _pair_rowdot(doa, oa2, "a_delta")
    dqa, dka, dva, dck, dcq = _fox_pair_bwd(qkv, doa, _col_to_row(lse_a), _col_to_row(delta_a), aug_q, aug_k, "a_dattn")
    dqkv = jnp.concatenate([dqa, dka, dva], axis=1)
    pad_heads = lambda t: jnp.pad(t.reshape(A_HEADS, S).T, ((0, 0), (0, LANES - A_HEADS)))
    dpf, db_f = _fgate_bwd(pf, b_f, pad_heads(dck), pad_heads(dcq), "a_dgate_scan")
    g["a_b_f"] = db_f[:, :A_HEADS]
    d_w_qkv = _mm(h1, dqkv, mode="tn", tm=1024, tn=512, tk=1024, name="a_dwqkv")
    d_w_f = _mm(h1, dpf, mode="tn", tm=1024, tn=LANES, tk=1024, name="a_dwgate")
    g["a_w_in"] = jnp.concatenate([d_w_qkv, d_w_f[:, :A_HEADS]], axis=1)
    dh1 = _mm(dqkv, w_qkv, mode="nt", tm=1024, tn=512, tk=1536, name="a_dh")
    dh1 = _mm(dpf, w_f, mode="nt", tm=1024, tn=512, tk=LANES, res=dh1, name="a_dh_gate")
    dx, _, (dg_mix0,) = _rms_bwd(x, dx, [(p["mix_norm_g"][0], dh1)], "a_dnorm")

    g["mix_norm_g"] = jnp.stack([dg_mix0.reshape(D), dg_mix1.reshape(D)])
    g["ffn_norm_g"] = jnp.stack([gf0["norm_g"], gf1["norm_g"]])
    g["ffn_w_up"] = jnp.stack([gf0["w_up"], gf1["w_up"]])
    g["ffn_w_down"] = jnp.stack([gf0["w_down"], gf1["w_down"]])
    g["ffn_conv_w"] = jnp.stack([gf0["conv_w"], gf1["conv_w"]])
    g["ffn_conv_b"] = jnp.stack([gf0["conv_b"], gf1["conv_b"]])
    return loss[0, 0], dx, g


_SHARD_SHAPES = {"a_w_in": (1, 1024, 772), "a_w_out": (1, 256, 1024), "b_w_q": (1, 1024, 384), "b_w_out": (1, 512, 256),
                 "w_kv": (1024, 768), "ffn_w_up": (2, 1024, 1408), "ffn_w_down": (2, 704, 1024), "ffn_conv_w": (2, 3, 1408)}
_SHARD_AXIS = {"a_w_in": 2, "a_w_out": 1, "b_w_q": 2, "b_w_out": 2, "w_kv": 1, "ffn_w_up": 2, "ffn_w_down": 1,
               "ffn_conv_w": 2}
_SMALL = (("kv_norm_g", (1024,)), ("mix_norm_g", (2, 1024)), ("ffn_norm_g", (2, 1024)), ("final_norm_g", (1024,)),
          ("a_b_f", (1, 16)), ("ffn_conv_b", (2, 5632)))


def _slabs(t, L, R, C, rpad):
    nc = -(-C // FLAT_W)
    t = jnp.pad(t.reshape(L, R, C), ((0, 0), (0, rpad - R), (0, nc * FLAT_W - C)))
    return t.reshape(L, rpad, nc, FLAT_W).transpose(0, 2, 1, 3).reshape(L * nc * rpad, FLAT_W)


def _unslabs(rows, L, R, C, rpad):
    nc = -(-C // FLAT_W)
    return rows.reshape(L, nc, rpad, FLAT_W).transpose(0, 2, 1, 3).reshape(L, rpad, nc * FLAT_W)[:, :R, :C]


_SEG_RT = {"ffn_w_down": 704, "a_w_in": 1024, "a_w_out": 256, "b_w_q": 1024, "b_w_out": 512, "w_kv": 1024,
           "ffn_w_up": 1024, "ffn_conv_w": 16}
_ROW_SHARDED = ("a_w_out", "ffn_w_down")


_LAYOUTS = {"grad": tuple(s[0] for s in _SEGS), "early": ("a_w_in", "a_w_out"),
            "late": ("ffn_w_down", "b_w_q", "b_w_out", "w_kv", "ffn_w_up", "ffn_conv_w")}


def _layout_rows(layout):
    if layout == "grad":
        return FLAT_ROWS
    return sum(_seg_rows(*s) for s in _SEGS if s[0] in _LAYOUTS[layout])


def _seg(name, layout="grad"):
    off = 0
    for s in sorted((s for s in _SEGS if s[0] in _LAYOUTS[layout]), key=lambda s: _LAYOUTS[layout].index(s[0])):
        if s[0] == name:
            _, L, R, C, rpad = s
            rt = _SEG_RT[name]
            assert off % rt == 0 and rpad % rt == 0
            assert layout != "grad" or HALF_ROWS % rt == 0 or off + _seg_rows(*s) <= HALF_ROWS
            return dict(L=L, R=R, C=C, rpad=rpad, nc=-(-C // FLAT_W), rt=rt, off=off, ni=rpad // rt)
        off += _seg_rows(*s)
    raise KeyError(name)


def _flat_block(sg, term=0):
    base = (sg["off"] + term * sg["L"] * sg["nc"] * sg["rpad"]) // sg["rt"]
    return lambda l, j, i: base + (l * sg["nc"] + j) * sg["ni"] + i


def _native3(t, name):
    sg = _seg(name)
    t = t.reshape(sg["L"], sg["R"], sg["C"])
    return jnp.pad(t, ((0, 0), (0, sg["rpad"] - sg["R"]), (0, 0))) if sg["rpad"] != sg["R"] else t


def _slab_pack(flat, t, name, layout, term=None):
    sg = _seg(name, layout)
    rt = sg["rt"]
    rb = _flat_block(sg, term or 0)

    def body(*refs):
        t_ref, o_ref = refs[-2], refs[-1]
        val = t_ref[...]
        o_ref[...] = val.astype(BF16) if term is None else _split3(val)[term]

    in_specs = [pl.BlockSpec((None, rt, FLAT_W), lambda l, j, i: (l, i, j))]
    args = [t]
    if flat is not None:
        in_specs, args = [ANY] + in_specs, [flat] + args
    return pl.pallas_call(
        body, out_shape=jax.ShapeDtypeStruct((_layout_rows(layout), FLAT_W), BF16), grid=(sg["L"], sg["nc"], sg["ni"]),
        in_specs=in_specs, out_specs=pl.BlockSpec((rt, FLAT_W), lambda l, j, i: (rb(l, j, i), 0)),
        input_output_aliases={0: 0} if flat is not None else {},
        compiler_params=_cparams(("parallel", "parallel", "parallel")), name=f"pack_{name}_{term or 0}",
    )(*args)


def _full_spec(sg, name):
    rt, nc, ni = sg["rt"], sg["nc"], sg["ni"]
    if name in _ROW_SHARDED:
        return (sg["L"], N_CHIPS * sg["R"], sg["C"]), pl.BlockSpec((None, rt, FLAT_W), lambda k, l, j, i: (l, k * ni + i, j))
    return ((sg["L"], sg["rpad"], N_CHIPS * nc * FLAT_W),
            pl.BlockSpec((None, rt, FLAT_W), lambda k, l, j, i: (l, i, k * nc + j)))


def _slab_unpack(gathered, slots, name, layout):
    sg = _seg(name, layout)
    rb = _flat_block(sg)
    shape, _ = _full_spec(sg, name)
    rt, nc, ni = sg["rt"], sg["nc"], sg["ni"]
    width = nc * FLAT_W

    def body(*refs):
        o_ref = refs[-1]
        for j in range(nc):
            o_ref[:, j * FLAT_W:(j + 1) * FLAT_W] = refs[1 + j][...]

    if name in _ROW_SHARDED:
        o_spec = pl.BlockSpec((None, rt, width), lambda k, l, i, s: (l, k * ni + i, 0))
    else:
        o_spec = pl.BlockSpec((None, rt, width), lambda k, l, i, s: (l, i, k))
    in_specs = [pl.BlockSpec((None, rt, FLAT_W), lambda k, l, i, s, j=j: (s[k], rb(l, j, i), 0)) for j in range(nc)]
    return pl.pallas_call(
        body, out_shape=jax.ShapeDtypeStruct(shape, BF16),
        grid_spec=pltpu.PrefetchScalarGridSpec(num_scalar_prefetch=1, grid=(N_CHIPS, sg["L"], ni), in_specs=in_specs,
                                               out_specs=o_spec),
        compiler_params=_cparams(("parallel",) * 3), name=f"unpack_{name}",
    )(slots, *([gathered] * nc))


def _slab_pack_grad(flat4, g, name):
    sg = _seg(name)
    rb = _flat_block(sg)
    shape, spec = _full_spec(sg, name)
    assert g.shape == shape, (name, g.shape, shape)

    def body(*refs):
        refs[-1][...] = refs[-2][...]

    in_specs, args = [spec], [g]
    if flat4 is not None:
        in_specs, args = [pl.BlockSpec(memory_space=pl.ANY)] + in_specs, [flat4] + args
    return pl.pallas_call(
        body, out_shape=jax.ShapeDtypeStruct((N_CHIPS, FLAT_ROWS, FLAT_W), F32), grid=(N_CHIPS, sg["L"], sg["nc"], sg["ni"]),
        in_specs=in_specs, out_specs=pl.BlockSpec((None, sg["rt"], FLAT_W), lambda k, l, j, i: (k, rb(l, j, i), 0)),
        input_output_aliases={0: 0} if flat4 is not None else {},
        compiler_params=_cparams(("parallel",) * 4), name=f"packgrad_{name}",
    )(*args)


def _adamw_shard(w, m, v, g_mine, g_other, c_arr, name):
    sg = _seg(name)
    rt = sg["rt"]
    rb = _flat_block(sg)
    per_half = HALF_ROWS // rt

    def half_of(l, j, i):
        return (rb(l, j, i) * rt) // HALF_ROWS

    def body(c_ref, w_ref, m_ref, v_ref, gm_ref, go_ref, g_ref, d_ref, mo_ref, vo_ref):
        is_mine = half_of(pl.program_id(0), pl.program_id(1), pl.program_id(2)) == c_ref[0]
        gv = jnp.where(is_mine, gm_ref[...], go_ref[...])
        g_ref[...] = gv
        d_ref[...], mo_ref[...], vo_ref[...] = _adam_update(w_ref[...], gv, m_ref[...], v_ref[...])

    nat = pl.BlockSpec((None, rt, FLAT_W), lambda l, j, i, c: (l, i, j))
    half = pl.BlockSpec((rt, FLAT_W), lambda l, j, i, c: (rb(l, j, i) - half_of(l, j, i) * per_half, 0))
    sds = jax.ShapeDtypeStruct(w.shape, F32)
    return pl.pallas_call(
        body, out_shape=[sds] * 4,
        grid_spec=pltpu.PrefetchScalarGridSpec(num_scalar_prefetch=1, grid=(sg["L"], sg["nc"], sg["ni"]),
                                               in_specs=[nat, nat, nat, half, half], out_specs=[nat] * 4),
        compiler_params=_cparams(("parallel", "parallel", "parallel")), name=f"adamw_{name}",
    )(c_arr, w, m, v, g_mine, g_other)


def _pack_small(vals, loss=None):
    parts = [vals[name].astype(F32).reshape(-1) for name, _ in _SMALL]
    if loss is not None:
        parts.append(loss.reshape(1))
    flat = jnp.concatenate(parts)
    return jnp.pad(flat, (0, SMALL_ROWS * 1024 - flat.shape[0])).reshape(SMALL_ROWS, 1024)


def _unpack_small(flat):
    flat = flat.reshape(-1)
    out = {}
    o = 0
    for name, shape in _SMALL:
        n = int(np.prod(shape))
        out[name] = flat[o:o + n].reshape(shape)
        o += n
    return out, flat[o]


_BIG = ("a_w_in", "a_w_out", "b_w_q", "b_w_out", "w_kv", "ffn_w_up", "ffn_w_down", "ffn_conv_w")
A_IN_PAD = 896


def _pack_weights(w, layout):
    flat = None
    for name in _LAYOUTS[layout]:
        t = _native3(w[name], name)
        for term in ((0, 1, 2) if name == "ffn_conv_w" else (None,)):
            flat = _slab_pack(flat, t, name, layout, term)
    return flat


def _early_weights(gathered, slots):
    a_in = _slab_unpack(gathered, slots, "a_w_in", "early")
    a_in = a_in.reshape(D, N_CHIPS, A_IN_PAD)[:, :, :772].reshape(D, N_CHIPS * 772)
    return dict(a_w_in=a_in, a_w_out=_slab_unpack(gathered, slots, "a_w_out", "early")[0])


def _late_weights(gathered, slots):
    full = {name: _slab_unpack(gathered, slots, name, "late") for name in _LAYOUTS["late"] if name != "ffn_conv_w"}
    sg = _seg("ffn_conv_w", "late")
    n1 = sg["nc"] * sg["rpad"]
    conv_rows = gathered[:, sg["off"]:sg["off"] + CONV_TERMS * n1]
    per_chip = []
    for k in range(N_CHIPS):
        rows = lax.dynamic_index_in_dim(conv_rows, slots[k], axis=0, keepdims=False)
        terms = [_unslabs(rows[i * n1:(i + 1) * n1], 1, sg["R"], sg["C"], sg["rpad"]).astype(F32) for i in range(CONV_TERMS)]
        per_chip.append((terms[0] + terms[1]) + terms[2])
    cw = jnp.concatenate(per_chip, axis=2).reshape(2, 3, 2, FF).transpose(0, 2, 1, 3)
    return dict(b_w_q=full["b_w_q"][0], b_w_out=full["b_w_out"][0], w_kv=full["w_kv"][0], ffn_w_up=full["ffn_w_up"],
                ffn_w_down=full["ffn_w_down"], conv_w=cw)


def _shard_grads(g):
    a_in = jnp.pad(g["a_w_in"].reshape(D, N_CHIPS, 772), ((0, 0), (0, 0), (0, A_IN_PAD - 772)))
    sgc = _seg("ffn_conv_w")
    full = {"a_w_in": a_in.reshape(1, D, N_CHIPS * A_IN_PAD), "a_w_out": g["a_w_out"][None], "b_w_q": g["b_w_q"][None],
            "b_w_out": g["b_w_out"][None], "w_kv": g["w_kv"][None], "ffn_w_up": g["ffn_w_up"],
            "ffn_w_down": g["ffn_w_down"],
            "ffn_conv_w": jnp.pad(g["ffn_conv_w"].reshape(1, sgc["R"], 2 * FF), ((0, 0), (0, sgc["rpad"] - sgc["R"]), (0, 0)))}
    flat4 = None
    for name in _BIG:
        flat4 = _slab_pack_grad(flat4, full[name], name)
    return flat4


_WEIGHTS = ["a_w_in", "a_b_f", "a_w_out", "b_w_q", "b_w_out", "kv_norm_g", "w_kv", "mix_norm_g", "ffn_norm_g", "ffn_w_up",
            "ffn_conv_w", "ffn_conv_b", "ffn_w_down", "final_norm_g"]


def kernel(x, a_w_in, a_b_f, a_w_out, b_w_q, b_w_out, kv_norm_g, w_kv, mix_norm_g, ffn_norm_g, ffn_w_up, ffn_conv_w, ffn_conv_b, ffn_w_down, final_norm_g, loss_target, m_a_w_in, m_a_b_f, m_a_w_out, m_b_w_q, m_b_w_out, m_kv_norm_g, m_w_kv, m_mix_norm_g, m_ffn_norm_g, m_ffn_w_up, m_ffn_conv_w, m_ffn_conv_b, m_ffn_w_down, m_final_norm_g, v_a_w_in, v_a_b_f, v_a_w_out, v_b_w_q, v_b_w_out, v_kv_norm_g, v_w_kv, v_mix_norm_g, v_ffn_norm_g, v_ffn_w_up, v_ffn_conv_w, v_ffn_conv_b, v_ffn_w_down, v_final_norm_g):
    w = dict(a_w_in=a_w_in, a_b_f=a_b_f, a_w_out=a_w_out, b_w_q=b_w_q, b_w_out=b_w_out, kv_norm_g=kv_norm_g, w_kv=w_kv,
             mix_norm_g=mix_norm_g, ffn_norm_g=ffn_norm_g, ffn_w_up=ffn_w_up, ffn_conv_w=ffn_conv_w, ffn_conv_b=ffn_conv_b,
             ffn_w_down=ffn_w_down, final_norm_g=final_norm_g)
    m = dict(a_w_in=m_a_w_in, a_b_f=m_a_b_f, a_w_out=m_a_w_out, b_w_q=m_b_w_q, b_w_out=m_b_w_out, kv_norm_g=m_kv_norm_g,
             w_kv=m_w_kv, mix_norm_g=m_mix_norm_g, ffn_norm_g=m_ffn_norm_g, ffn_w_up=m_ffn_w_up, ffn_conv_w=m_ffn_conv_w,
             ffn_conv_b=m_ffn_conv_b, ffn_w_down=m_ffn_w_down, final_norm_g=m_final_norm_g)
    v = dict(a_w_in=v_a_w_in, a_b_f=v_a_b_f, a_w_out=v_a_w_out, b_w_q=v_b_w_q, b_w_out=v_b_w_out, kv_norm_g=v_kv_norm_g,
             w_kv=v_w_kv, mix_norm_g=v_mix_norm_g, ffn_norm_g=v_ffn_norm_g, ffn_w_up=v_ffn_w_up, ffn_conv_w=v_ffn_conv_w,
             ffn_conv_b=v_ffn_conv_b, ffn_w_down=v_ffn_w_down, final_norm_g=v_final_norm_g)

    c_arr = lax.axis_index("c").astype(jnp.int32).reshape(1)
    k_arr = (2 * lax.axis_index("x") + lax.axis_index("y")).astype(jnp.int32).reshape(1)
    xi, yi = lax.axis_index("x"), lax.axis_index("y")
    late_slots = jnp.stack([jnp.where(k == k_arr[0], OWN_SLOT, 2 * ((k & 1) ^ yi) + ((k >> 1) ^ xi) - 1)
                            for k in range(N_CHIPS)]).astype(jnp.int32)
    w_late = _pack_weights(w, "late")
    land = _place_own(w_late, jnp.full((1,), OWN_SLOT, jnp.int32), "late_place_own")
    send_sems, recv_sems, w_thru, land_thru, token = _late_gather_start(w_late, land)
    w_early = _pack_weights(w, "early")
    early = _allgather_shards(w_early, _place_own(w_early, k_arr, "early_place_own"))
    p = _early_weights(early, jnp.arange(N_CHIPS, dtype=jnp.int32))
    cb = ffn_conv_b.reshape(2, 2, 1, FF)
    p.update(a_b_f=a_b_f, kv_norm_g=kv_norm_g, mix_norm_g=mix_norm_g + token[0, 0], ffn_norm_g=ffn_norm_g,
             final_norm_g=final_norm_g, conv_b=cb)

    def late_weights(after):
        return _late_weights(_late_gather_wait(send_sems, recv_sems, w_thru, land_thru, after), late_slots)

    loss_part, grad_x, g = _local_step(x[0], loss_target[0], p, late_weights)

    gflat = _shard_grads(g)
    pair = _pair_sum(gflat, _sibling_swap_half(gflat), c_arr)
    g_mine = _chip_sum(pair, _chip_exchange(pair), k_arr)
    g_other = _sibling_send(g_mine)
    small, loss = _unpack_small(_allreduce_small(_pack_small(g, loss_part)))

    big = [{}, {}, {}, {}]
    for name in _BIG:
        sg = _seg(name)
        res = _adamw_shard(_native3(w[name], name), _native3(m[name], name), _native3(v[name], name), g_mine, g_other,
                           c_arr, name)
        for store, t in zip(big, res):
            store[name] = t[:, :sg["R"], :].reshape(_SHARD_SHAPES[name])
    dws, mns, vns = _adamw(_pack_small(w), _pack_small(small), _pack_small(m), _pack_small(v), "adamw_small")
    sml = [small] + [_unpack_small(t)[0] for t in (dws, mns, vns)]
    outs = [loss, grad_x[None]]
    for b, s in zip(big, sml):
        outs += [b[n] if n in b else s[n] for n in _WEIGHTS]
    return tuple(outs)
```

```python
import functools
import math

import numpy as np
import jax
import jax.numpy as jnp
from jax import lax
from jax.experimental import pallas as pl
from jax.experimental.pallas import tpu as pltpu

F32 = jnp.float32
BF16 = jnp.bfloat16
MESH = pl.DeviceIdType.MESH

S = 4096
D = 1024
A_HEADS = 16
HEAD_DIM = 64
QKV_W = 3 * A_HEADS * HEAD_DIM
B_GROUPS = ((128, 1), (512, 4), (2048, 16))
B_HPG = 8
B_Q_W = 3 * B_HPG * HEAD_DIM
B_OUT_W = B_HPG * HEAD_DIM
B_KV_W = 2 * B_Q_W
B_WIN = 128
FF = 2816
RMS_EPS = 1e-6
SCALE = HEAD_DIM ** -0.5
N_CHIPS = 4

ADAM_LR, ADAM_B1, ADAM_B2, ADAM_EPS, ADAM_WD, ADAM_STEP = 0.001, 0.9, 0.999, 1e-08, 0.01, 10

V7X_VMEM_LIMIT = 48 * 1024 * 1024
LANES = 128
NEG_INF = float("-inf")

FLAT_W = LANES
_SEGS = (("ffn_w_down", 2, 704, 1024, 704), ("a_w_in", 1, 1024, 772, 1024), ("a_w_out", 1, 256, 1024, 256),
         ("b_w_q", 1, 1024, 384, 1024), ("b_w_out", 1, 512, 256, 512), ("w_kv", 1, 1024, 768, 1024),
         ("ffn_w_up", 2, 1024, 1408, 1024), ("ffn_conv_w", 1, 6, 1408, 16))
CONV_TERMS = 3


def _seg_rows(name, L, R, C, rpad):
    return (CONV_TERMS if name == "ffn_conv_w" else 1) * L * (-(-C // FLAT_W)) * rpad


FLAT_T = 2048
FLAT_ROWS = 57344
HALF_ROWS = FLAT_ROWS // 2
assert sum(_seg_rows(*s) for s in _SEGS) <= FLAT_ROWS and HALF_ROWS % FLAT_T == 0
SMALL_ROWS = 24


def _cparams(sem=None, **kw):
    return pltpu.CompilerParams(dimension_semantics=sem, vmem_limit_bytes=V7X_VMEM_LIMIT, **kw)


_DN = {"nn": (((1,), (0,)), ((), ())), "nt": (((1,), (1,)), ((), ())), "tn": (((0,), (0,)), ((), ()))}


def _mm(a, b, *, mode, tm, tn, tk, name, out_dtype=F32, res=None, a_split=0, b_split=0, o_split=0):
    if mode == "tn":
        K = a.shape[0]
        M = a.shape[1]
    else:
        M = a.shape[-2]
        K = a.shape[-1] * (2 if a_split else 1)
    if mode == "nt":
        N = b.shape[0]
    else:
        N = b.shape[-1] * (2 if b_split else 1)
    assert M % tm == 0 and N % tn == 0 and K % tk == 0, (name, M, N, K, tm, tn, tk)
    nk = K // tk

    if mode == "tn":
        a_spec = pl.BlockSpec((tk, tm), lambda i, j, k: (k, i))
    elif a_split:
        a_spec = pl.BlockSpec((None, tm, tk), lambda i, j, k: (k // a_split, i, k % a_split))
    else:
        a_spec = pl.BlockSpec((tm, tk), lambda i, j, k: (i, k))
    if mode == "nt":
        b_spec = pl.BlockSpec((tn, tk), lambda i, j, k: (j, k))
    elif b_split:
        b_spec = pl.BlockSpec((None, tk, tn), lambda i, j, k: (j // b_split, k, j % b_split))
    else:
        b_spec = pl.BlockSpec((tk, tn), lambda i, j, k: (k, j))
    if o_split:
        o_spec = pl.BlockSpec((None, tm, tn), lambda i, j, k: (j // o_split, i, j % o_split))
        out_shape = jax.ShapeDtypeStruct((2, M, N // 2), out_dtype)
    else:
        o_spec = pl.BlockSpec((tm, tn), lambda i, j, k: (i, j))
        out_shape = jax.ShapeDtypeStruct((M, N), out_dtype)
    in_specs = [a_spec, b_spec]
    args = [a, b]
    if res is not None:
        in_specs.append(pl.BlockSpec((tm, tn), lambda i, j, k: (i, j)))
        args.append(res)

    def body(*refs):
        if res is not None:
            a_ref, b_ref, r_ref, o_ref = refs[:4]
        else:
            a_ref, b_ref, o_ref = refs[:3]
            r_ref = None
        p = lax.dot_general(a_ref[...].astype(BF16), b_ref[...].astype(BF16), _DN[mode], preferred_element_type=F32)

        def finish(r):
            if r_ref is not None:
                r = r + r_ref[...]
            o_ref[...] = r.astype(out_dtype)

        if nk == 1:
            finish(p)
        else:
            acc = refs[-1]
            k = pl.program_id(2)

            @pl.when(k == 0)
            def _():
                acc[...] = p

            @pl.when(k > 0)
            def _():
                acc[...] += p

            @pl.when(k == nk - 1)
            def _():
                finish(acc[...])

    return pl.pallas_call(
        body, out_shape=out_shape, grid=(M // tm, N // tn, nk), in_specs=in_specs, out_specs=o_spec,
        scratch_shapes=[pltpu.VMEM((tm, tn), F32)] if nk > 1 else [],
        compiler_params=_cparams(("parallel", "parallel", "arbitrary")), name=name,
    )(*args)


NORM_ROWS = 256


def _rms_fwd(x, g, name):
    def body(x_ref, g_ref, o_ref):
        xv = x_ref[...]
        r = lax.rsqrt(jnp.mean(xv * xv, axis=-1, keepdims=True) + RMS_EPS)
        o_ref[...] = (xv * r * g_ref[...]).astype(BF16)

    row = pl.BlockSpec((NORM_ROWS, D), lambda i: (i, 0))
    return pl.pallas_call(
        body, out_shape=jax.ShapeDtypeStruct((S, D), BF16), grid=(S // NORM_ROWS,),
        in_specs=[row, pl.BlockSpec((1, D), lambda i: (0, 0))], out_specs=row,
        compiler_params=_cparams(("parallel",)), name=name,
    )(x, g.reshape(1, D))


def _rms_bwd(x, dres, pairs, name):
    n = len(pairs)

    def body(*refs):
        x_ref, dres_ref = refs[0], refs[1]
        g_refs = refs[2:2 + 2 * n:2]
        dh_refs = refs[3:3 + 2 * n:2]
        dx_ref, dxb_ref = refs[2 + 2 * n], refs[3 + 2 * n]
        dg_refs = refs[4 + 2 * n:]
        i = pl.program_id(0)
        xv = x_ref[...]
        r = lax.rsqrt(jnp.mean(xv * xv, axis=-1, keepdims=True) + RMS_EPS)
        y = xv * r
        dx = dres_ref[...]
        for g_ref, dh_ref, dg_ref in zip(g_refs, dh_refs, dg_refs):
            dh = dh_ref[...]
            dy = dh * g_ref[...]
            dx = dx + r * (dy - y * jnp.mean(dy * y, axis=-1, keepdims=True))
            part = jnp.sum(dh * y, axis=0, keepdims=True)

            @pl.when(i == 0)
            def _():
                dg_ref[...] = part

            @pl.when(i > 0)
            def _():
                dg_ref[...] += part

        dx_ref[...] = dx
        dxb_ref[...] = dx.astype(BF16)

    row = pl.BlockSpec((NORM_ROWS, D), lambda i: (i, 0))
    vec = pl.BlockSpec((1, D), lambda i: (0, 0))
    in_specs = [row, row]
    args = [x, dres]
    for g, dh in pairs:
        in_specs += [vec, row]
        args += [g.reshape(1, D), dh]
    outs = pl.pallas_call(
        body,
        out_shape=[jax.ShapeDtypeStruct((S, D), F32), jax.ShapeDtypeStruct((S, D), BF16)]
        + [jax.ShapeDtypeStruct((1, D), F32)] * n,
        grid=(S // NORM_ROWS,), in_specs=in_specs, out_specs=[row, row] + [vec] * n,
        compiler_params=_cparams(("arbitrary",)), name=name,
    )(*args)
    return outs[0], outs[1], list(outs[2:])


def _loss_head(x, g, target, name):
    def body(x_ref, g_ref, t_ref, loss_ref, dx_ref, dxb_ref, dg_ref):
        i = pl.program_id(0)
        xv = x_ref[...]
        gv = g_ref[...]
        r = lax.rsqrt(jnp.mean(xv * xv, axis=-1, keepdims=True) + RMS_EPS)
        y = xv * r
        err = y * gv - t_ref[...]
        lpart = jnp.broadcast_to(jnp.sum(err * err, keepdims=True) * (0.5 / D), (1, LANES))
        dh = err * (1.0 / D)
        dy = dh * gv
        dx = r * (dy - y * jnp.mean(dy * y, axis=-1, keepdims=True))
        part = jnp.sum(dh * y, axis=0, keepdims=True)

        @pl.when(i == 0)
        def _():
            dg_ref[...] = part
            loss_ref[...] = lpart

        @pl.when(i > 0)
        def _():
            dg_ref[...] += part
            loss_ref[...] += lpart

        dx_ref[...] = dx
        dxb_ref[...] = dx.astype(BF16)

    row = pl.BlockSpec((NORM_ROWS, D), lambda i: (i, 0))
    vec = pl.BlockSpec((1, D), lambda i: (0, 0))
    return pl.pallas_call(
        body,
        out_shape=[jax.ShapeDtypeStruct((1, LANES), F32), jax.ShapeDtypeStruct((S, D), F32),
                   jax.ShapeDtypeStruct((S, D), BF16), jax.ShapeDtypeStruct((1, D), F32)],
        grid=(S // NORM_ROWS,), in_specs=[row, vec, row],
        out_specs=[pl.BlockSpec((1, LANES), lambda i: (0, 0)), row, row, vec],
        compiler_params=_cparams(("arbitrary",)), name=name,
    )(x, g.reshape(1, D), target)


SCAN_ROWS = 256


def _split3(v):
    hi = v.astype(BF16)
    r1 = v - hi.astype(F32)
    mid = r1.astype(BF16)
    lo = (r1 - mid.astype(F32)).astype(BF16)
    return hi, mid, lo


def _tri_dot(tri, v):
    hi, mid, lo = _split3(v)
    dn = _DN["nn"]
    return (lax.dot_general(tri, hi, dn, preferred_element_type=F32)
            + lax.dot_general(tri, mid, dn, preferred_element_type=F32)
            + lax.dot_general(tri, lo, dn, preferred_element_type=F32))


def _log_sigmoid(z):
    return jnp.minimum(z, 0.0) - jnp.log(1.0 + jnp.exp(-jnp.abs(z)))


def _fgate_fwd(pf, bias, name):
    tri = jnp.tril(jnp.ones((SCAN_ROWS, SCAN_ROWS), F32)).astype(BF16)

    def body(pf_ref, b_ref, tri_ref, c_ref):
        carry = jnp.zeros((1, LANES), F32)
        for blk in range(S // SCAN_ROWS):
            rows = pl.ds(blk * SCAN_ROWS, SCAN_ROWS)
            lf = _log_sigmoid(pf_ref[rows, :] + b_ref[...])
            c_ref[rows, :] = _tri_dot(tri_ref[...], lf) + carry
            carry = c_ref[pl.ds(blk * SCAN_ROWS + SCAN_ROWS - 1, 1), :]

    return pl.pallas_call(
        body, out_shape=jax.ShapeDtypeStruct((S, LANES), F32),
        compiler_params=_cparams(), name=name,
    )(pf, bias, tri)


def _fgate_bwd(pf, bias, dc_key, dc_query, name):
    triu = jnp.triu(jnp.ones((SCAN_ROWS, SCAN_ROWS), F32)).astype(BF16)

    def body(pf_ref, b_ref, dck_ref, dcq_ref, tri_ref, dpf_ref, db_ref, dlf_ref):
        carry = jnp.zeros((1, LANES), F32)
        db = jnp.zeros((1, LANES), F32)
        lane = lax.broadcasted_iota(jnp.int32, (SCAN_ROWS, LANES), 1)
        for blk in reversed(range(S // SCAN_ROWS)):
            rows = pl.ds(blk * SCAN_ROWS, SCAN_ROWS)
            dc = dck_ref[rows, :] + dcq_ref[rows, :]
            dlf_ref[rows, :] = _tri_dot(tri_ref[...], dc) + carry
            carry = dlf_ref[pl.ds(blk * SCAN_ROWS, 1), :]
            z = pf_ref[rows, :] + b_ref[...]
            e = jnp.exp(-jnp.abs(z))
            sig_neg = jnp.where(z >= 0.0, e, 1.0) / (1.0 + e)
            dz = jnp.where(lane < A_HEADS, dlf_ref[rows, :] * sig_neg, 0.0)
            dpf_ref[rows, :] = dz.astype(BF16)
            db = db + jnp.sum(dz, axis=0, keepdims=True)
        db_ref[...] = db

    return pl.pallas_call(
        body, out_shape=[jax.ShapeDtypeStruct((S, LANES), BF16), jax.ShapeDtypeStruct((1, LANES), F32)],
        scratch_shapes=[pltpu.VMEM((S, LANES), F32)],
        compiler_params=_cparams(), name=name,
    )(pf, bias, dc_key, dc_query, triu)


FOX_T = 512


def _fox_fwd(q, k, v, ccol, crow, name):
    H = q.shape[0]
    T = FOX_T
    nq = S // T

    def body(q_ref, k_ref, v_ref, cc_ref, cr_ref, o_ref, lse_ref, m_sc, l_sc, acc_sc):
        i = pl.program_id(1)
        j = pl.program_id(2)

        @pl.when(j == 0)
        def _():
            m_sc[...] = jnp.full((T, 1), NEG_INF, F32)
            l_sc[...] = jnp.zeros((T, 1), F32)
            acc_sc[...] = jnp.zeros((T, HEAD_DIM), F32)

        def step(diagonal):
            s = lax.dot_general(q_ref[...], k_ref[...], _DN["nt"], preferred_element_type=F32) * SCALE
            s = s + (cc_ref[...] - cr_ref[...])
            if diagonal:
                row = lax.broadcasted_iota(jnp.int32, (T, T), 0)
                col = lax.broadcasted_iota(jnp.int32, (T, T), 1)
                s = jnp.where(row >= col, s, NEG_INF)
            m_prev = m_sc[...]
            m_new = jnp.maximum(m_prev, jnp.max(s, axis=1, keepdims=True))
            alpha = jnp.exp(m_prev - m_new)
            p = jnp.exp(s - m_new)
            l_sc[...] = alpha * l_sc[...] + jnp.sum(p, axis=1, keepdims=True)
            acc_sc[...] = alpha * acc_sc[...] + lax.dot_general(p.astype(BF16), v_ref[...], _DN["nn"],
                                                                preferred_element_type=F32)
            m_sc[...] = m_new

        @pl.when(j < i)
        def _():
            step(False)

        @pl.when(j == i)
        def _():
            step(True)
            o_ref[...] = (acc_sc[...] / l_sc[...]).astype(BF16)
            lse_ref[...] = m_sc[...] + jnp.log(l_sc[...])

    qs = pl.BlockSpec((None, T, HEAD_DIM), lambda h, i, j: (h, i, 0))
    ks = pl.BlockSpec((None, T, HEAD_DIM), lambda h, i, j: (h, jnp.minimum(i, j), 0))
    col = pl.BlockSpec((None, T, 1), lambda h, i, j: (h, i, 0))
    rowk = pl.BlockSpec((None, 1, T), lambda h, i, j: (h, 0, jnp.minimum(i, j)))
    return pl.pallas_call(
        body, out_shape=[jax.ShapeDtypeStruct((H, S, HEAD_DIM), BF16), jax.ShapeDtypeStruct((H, S, 1), F32)],
        grid=(H, nq, nq), in_specs=[qs, ks, ks, col, rowk], out_specs=[qs, col],
        scratch_shapes=[pltpu.VMEM((T, 1), F32), pltpu.VMEM((T, 1), F32), pltpu.VMEM((T, HEAD_DIM), F32)],
        compiler_params=_cparams(("parallel", "parallel", "arbitrary")), name=name,
    )(q, k, v, ccol, crow)


def _fox_bwd(q, k, v, do, lse_row, delta_row, cq_row, ck_col, name):
    H = q.shape[0]
    T = FOX_T
    nq = S // T

    def body(q_ref, k_ref, v_ref, do_ref, lse_ref, dl_ref, cq_ref, ck_ref, dq_ref, dk_ref, dv_ref, dc_ref, dcq_ref,
             dk_sc, dv_sc, dc_sc):
        j = pl.program_id(1)
        i = pl.program_id(2)

        @pl.when(jnp.logical_and(j == 0, i == 0))
        def _():
            dq_ref[...] = jnp.zeros((S, HEAD_DIM), F32)
            dcq_ref[...] = jnp.zeros((nq, 1, T), F32)

        @pl.when(i == j)
        def _():
            dk_sc[...] = jnp.zeros((T, HEAD_DIM), F32)
            dv_sc[...] = jnp.zeros((T, HEAD_DIM), F32)
            dc_sc[...] = jnp.zeros((T, 1), F32)

        def step(diagonal):
            qv = q_ref[...]
            kv = k_ref[...]
            dov = do_ref[...]
            st = lax.dot_general(kv, qv, _DN["nt"], preferred_element_type=F32) * SCALE
            st = st + (cq_ref[...] - ck_ref[...])
            if diagonal:
                row = lax.broadcasted_iota(jnp.int32, (T, T), 0)
                col = lax.broadcasted_iota(jnp.int32, (T, T), 1)
                st = jnp.where(col >= row, st, NEG_INF)
            pt = jnp.exp(st - lse_ref[...])
            dv_sc[...] += lax.dot_general(pt.astype(BF16), dov, _DN["nn"], preferred_element_type=F32)
            dpt = lax.dot_general(v_ref[...], dov, _DN["nt"], preferred_element_type=F32)
            dst = pt * (dpt - dl_ref[...])
            dc_sc[...] -= jnp.sum(dst, axis=1, keepdims=True)
            dcq_ref[i] += jnp.sum(dst, axis=0, keepdims=True)
            dsb = (dst * SCALE).astype(BF16)
            dk_sc[...] += lax.dot_general(dsb, qv, _DN["nn"], preferred_element_type=F32)
            rows = pl.ds(pl.multiple_of(i * T, T), T)
            dq_ref[rows, :] += lax.dot_general(dsb, kv, _DN["tn"], preferred_element_type=F32)

        @pl.when(i > j)
        def _():
            step(False)

        @pl.when(i == j)
        def _():
            step(True)

        @pl.when(i == nq - 1)
        def _():
            dk_ref[...] = dk_sc[...].astype(BF16)
            dv_ref[...] = dv_sc[...].astype(BF16)
            dc_ref[...] = dc_sc[...]

    qs = pl.BlockSpec((None, T, HEAD_DIM), lambda h, j, i: (h, jnp.maximum(i, j), 0))
    qrow = pl.BlockSpec((None, 1, T), lambda h, j, i: (h, 0, jnp.maximum(i, j)))
    ks = pl.BlockSpec((None, T, HEAD_DIM), lambda h, j, i: (h, j, 0))
    kcol = pl.BlockSpec((None, T, 1), lambda h, j, i: (h, j, 0))
    dqs = pl.BlockSpec((None, S, HEAD_DIM), lambda h, j, i: (h, 0, 0))
    dcqs = pl.BlockSpec((None, nq, 1, T), lambda h, j, i: (h, 0, 0, 0))
    return pl.pallas_call(
        body,
        out_shape=[jax.ShapeDtypeStruct((H, S, HEAD_DIM), F32), jax.ShapeDtypeStruct((H, S, HEAD_DIM), BF16),
                   jax.ShapeDtypeStruct((H, S, HEAD_DIM), BF16), jax.ShapeDtypeStruct((H, S, 1), F32),
                   jax.ShapeDtypeStruct((H, nq, 1, T), F32)],
        grid=(H, nq, nq), in_specs=[qs, ks, ks, qs, qrow, qrow, qrow, kcol], out_specs=[dqs, ks, ks, kcol, dcqs],
        scratch_shapes=[pltpu.VMEM((T, HEAD_DIM), F32), pltpu.VMEM((T, HEAD_DIM), F32), pltpu.VMEM((T, 1), F32)],
        compiler_params=_cparams(("parallel", "arbitrary", "arbitrary")), name=name,
    )(q, k, v, do, lse_row, delta_row, cq_row, ck_col)


def _rowdot(a, b, name):
    H = a.shape[0]
    T = 1024

    def body(a_ref, b_ref, o_ref):
        o_ref[...] = jnp.sum(a_ref[...].astype(F32) * b_ref[...].astype(F32), axis=-1, keepdims=True)

    blk = pl.BlockSpec((None, T, HEAD_DIM), lambda h, i: (h, i, 0))
    return pl.pallas_call(
        body, out_shape=jax.ShapeDtypeStruct((H, S, 1), F32), grid=(H, S // T), in_specs=[blk, blk],
        out_specs=pl.BlockSpec((None, T, 1), lambda h, i: (h, i, 0)),
        compiler_params=_cparams(("parallel", "parallel")), name=name,
    )(a, b)


def _first_head(shape):
    return lax.broadcasted_iota(jnp.int32, shape, len(shape) - 1) < HEAD_DIM


def _each_head(x, lo):
    zero = jnp.zeros_like(x)
    return jnp.where(lo, x, zero), jnp.where(lo, zero, x)


GATE_LANES = 6


def _gate_lanes(cum):
    to_bf16_grid = lambda t: lax.reduce_precision(t, exponent_bits=8, mantissa_bits=7)
    c = cum[:, :A_HEADS]
    hi = to_bf16_grid(c)
    mid = to_bf16_grid(c - hi)
    lo = to_bf16_grid((c - hi) - mid)
    one = jnp.ones_like(hi)
    def place(cols):
        t = jnp.stack(cols, axis=-1)
        t = jnp.pad(t, ((0, 0), (0, 0), (0, HEAD_DIM - GATE_LANES)))
        return t.reshape(S, A_HEADS // 2, 2, HEAD_DIM)[:, :, ::-1, :].reshape(S, A_HEADS * HEAD_DIM).astype(BF16)
    return place([hi, mid, lo, one, one, one]), place([one, one, one, -hi, -mid, -lo])


def _fox_pair_fwd(qkv, aug_q, aug_k, name):
    T = FOX_T
    nq = S // T
    NP = A_HEADS // 2

    def body(q_ref, k_ref, v_ref, aq_ref, ak_ref, o_ref, lse_ref, m_sc, l_sc, acc_sc):
        i = pl.program_id(1)
        j = pl.program_id(2)
        lo = _first_head((T, LANES))

        @pl.when(j == 0)
        def _():
            m_sc[...] = jnp.full((2, T, LANES), NEG_INF, F32)
            l_sc[...] = jnp.zeros((2, T, LANES), F32)
            acc_sc[...] = jnp.zeros((T, LANES), F32)

        def step(diagonal):
            qs = q_ref[...] * jnp.asarray(SCALE, BF16)
            aq, ak, kv = aq_ref[...], ak_ref[...], k_ref[...]
            q2 = (jnp.where(lo, qs, aq), jnp.where(lo, aq, qs))
            k2 = (jnp.where(lo, kv, ak), jnp.where(lo, ak, kv))
            if diagonal:
                causal = lax.broadcasted_iota(jnp.int32, (T, T), 0) >= lax.broadcasted_iota(jnp.int32, (T, T), 1)
            pv, alphas = None, []
            for h, vh in enumerate(_each_head(v_ref[...], lo)):
                s = lax.dot_general(q2[h], k2[h], _DN["nt"], preferred_element_type=F32)
                if diagonal:
                    s = jnp.where(causal, s, NEG_INF)
                m_prev = m_sc[h]
                m_new = jnp.maximum(m_prev, jnp.max(s, axis=1, keepdims=True))
                alpha = jnp.exp(m_prev - m_new)
                p = jnp.exp(s - jnp.tile(m_new, (1, T // LANES)))
                l_sc[h] = alpha * l_sc[h] + jnp.sum(p, axis=1, keepdims=True)
                m_sc[h] = m_new
                d = lax.dot_general(p.astype(BF16), vh, _DN["nn"], preferred_element_type=F32)
                pv = d if pv is None else pv + d
                alphas.append(alpha)
            acc_sc[...] = jnp.where(lo, alphas[0], alphas[1]) * acc_sc[...] + pv

        @pl.when(j < i)
        def _():
            step(False)

        @pl.when(j == i)
        def _():
            step(True)
            o_ref[...] = (acc_sc[...] * jnp.where(lo, 1.0 / l_sc[0], 1.0 / l_sc[1])).astype(BF16)
            for h in range(2):
                lse_ref[h] = (m_sc[h] + jnp.log(l_sc[h]))[:, 0:1]

    qs_ = pl.BlockSpec((T, LANES), lambda p, i, j: (i, p))
    ks = pl.BlockSpec((T, LANES), lambda p, i, j: (jnp.minimum(i, j), NP + p))
    vs = pl.BlockSpec((T, LANES), lambda p, i, j: (jnp.minimum(i, j), 2 * NP + p))
    aks = pl.BlockSpec((T, LANES), lambda p, i, j: (jnp.minimum(i, j), p))
    col = pl.BlockSpec((2, T, 1), lambda p, i, j: (p, i, 0))
    return pl.pallas_call(
        body, out_shape=[jax.ShapeDtypeStruct((S, A_HEADS * HEAD_DIM), BF16), jax.ShapeDtypeStruct((A_HEADS, S, 1), F32)],
        grid=(NP, nq, nq), in_specs=[qs_, ks, vs, qs_, aks], out_specs=[qs_, col],
        scratch_shapes=[pltpu.VMEM((2, T, LANES), F32), pltpu.VMEM((2, T, LANES), F32), pltpu.VMEM((T, LANES), F32)],
        compiler_params=_cparams(("parallel", "parallel", "arbitrary")), name=name,
    )(qkv, qkv, qkv, aug_q, aug_k)


def _fox_pair_bwd(qkv, do, lse_row, delta_row, aug_q, aug_k, name):
    T = FOX_T
    nq = S // T
    NP = A_HEADS // 2

    def body(q_ref, k_ref, v_ref, do_ref, lse_ref, dl_ref, aq_ref, ak_ref, dq_ref, dk_ref, dv_ref, dc_ref, dcq_ref,
             dq_sc, dk_sc, dv_sc, dc_sc):
        j = pl.program_id(1)
        i = pl.program_id(2)
        lo = _first_head((T, LANES))

        @pl.when(jnp.logical_and(j == 0, i == 0))
        def _():
            dq_sc[...] = jnp.zeros((S, LANES), F32)
            dcq_ref[...] = jnp.zeros((2, nq, 1, T), F32)

        @pl.when(i == j)
        def _():
            dk_sc[...] = jnp.zeros((T, LANES), F32)
            dv_sc[...] = jnp.zeros((T, LANES), F32)
            dc_sc[...] = jnp.zeros((2, T, 1), F32)

        def step(diagonal):
            qv = q_ref[...]
            kv = k_ref[...]
            dov = do_ref[...].astype(BF16)
            qs = qv * jnp.asarray(SCALE, BF16)
            aq, ak = aq_ref[...], ak_ref[...]
            q2 = (jnp.where(lo, qs, aq), jnp.where(lo, aq, qs))
            k2 = (jnp.where(lo, kv, ak), jnp.where(lo, ak, kv))
            if diagonal:
                causal = lax.broadcasted_iota(jnp.int32, (T, T), 1) >= lax.broadcasted_iota(jnp.int32, (T, T), 0)
            dv = dk = dq = None
            for h, (kh, vh, qh, doh) in enumerate(zip(_each_head(kv, lo), _each_head(v_ref[...], lo),
                                                      _each_head(qv, lo), _each_head(dov, lo))):
                st = lax.dot_general(k2[h], q2[h], _DN["nt"], preferred_element_type=F32)
                if diagonal:
                    st = jnp.where(causal, st, NEG_INF)
                pt = jnp.exp(st - lse_ref[h])
                d = lax.dot_general(pt.astype(BF16), doh, _DN["nn"], preferred_element_type=F32)
                dv = d if dv is None else dv + d
                dpt = lax.dot_general(vh, dov, _DN["nt"], preferred_element_type=F32)
                dst = pt * (dpt - dl_ref[h])
                dc_sc[h] -= jnp.sum(dst, axis=1, keepdims=True)
                dcq_ref[h, i] += jnp.sum(dst, axis=0, keepdims=True)
                dsb = (dst * SCALE).astype(BF16)
                d = lax.dot_general(dsb, qh, _DN["nn"], preferred_element_type=F32)
                dk = d if dk is None else dk + d
                d = lax.dot_general(dsb, kh, _DN["tn"], preferred_element_type=F32)
                dq = d if dq is None else dq + d
            dv_sc[...] += dv
            dk_sc[...] += dk
            rows = pl.ds(pl.multiple_of(i * T, T), T)
            dq_sc[rows, :] += dq

        @pl.when(i > j)
        def _():
            step(False)

        @pl.when(i == j)
        def _():
            step(True)

        @pl.when(i == nq - 1)
        def _():
            dk_ref[...] = dk_sc[...].astype(BF16)
            dv_ref[...] = dv_sc[...].astype(BF16)
            dc_ref[...] = dc_sc[...]

        @pl.when(jnp.logical_and(j == nq - 1, i == nq - 1))
        def _():
            dq_ref[...] = dq_sc[...].astype(BF16)

    qs = pl.BlockSpec((T, LANES), lambda p, j, i: (jnp.maximum(i, j), p))
    qrow = pl.BlockSpec((2, 1, T), lambda p, j, i: (p, 0, jnp.maximum(i, j)))
    ks = pl.BlockSpec((T, LANES), lambda p, j, i: (j, NP + p))
    vs = pl.BlockSpec((T, LANES), lambda p, j, i: (j, 2 * NP + p))
    kout = pl.BlockSpec((T, LANES), lambda p, j, i: (j, p))
    kcol = pl.BlockSpec((2, T, 1), lambda p, j, i: (p, j, 0))
    dqs = pl.BlockSpec((S, LANES), lambda p, j, i: (0, p))
    dcqs = pl.BlockSpec((2, nq, 1, T), lambda p, j, i: (p, 0, 0, 0))
    wide = jax.ShapeDtypeStruct((S, A_HEADS * HEAD_DIM), BF16)
    return pl.pallas_call(
        body,
        out_shape=[wide, wide, wide, jax.ShapeDtypeStruct((A_HEADS, S, 1), F32),
                   jax.ShapeDtypeStruct((A_HEADS, nq, 1, T), F32)],
        grid=(NP, nq, nq), in_specs=[qs, ks, vs, qs, qrow, qrow, qs, kout], out_specs=[dqs, kout, kout, kcol, dcqs],
        scratch_shapes=[pltpu.VMEM((S, LANES), F32), pltpu.VMEM((T, LANES), F32), pltpu.VMEM((T, LANES), F32),
                        pltpu.VMEM((2, T, 1), F32)],
        compiler_params=_cparams(("parallel", "arbitrary", "arbitrary")), name=name,
    )(qkv, qkv, qkv, do, lse_row, delta_row, aug_q, aug_k)


def _pair_rowdot(a, b, name):
    n = a.shape[1] // HEAD_DIM
    T = 1024

    def body(a_ref, b_ref, o_ref):
        prod = a_ref[...].astype(F32) * b_ref[...].astype(F32)
        lo = _first_head(prod.shape)
        o_ref[0] = jnp.sum(jnp.where(lo, prod, 0.0), axis=1, keepdims=True)
        o_ref[1] = jnp.sum(jnp.where(lo, 0.0, prod), axis=1, keepdims=True)

    blk = pl.BlockSpec((T, LANES), lambda p, i: (i, p))
    return pl.pallas_call(
        body, out_shape=jax.ShapeDtypeStruct((n, S, 1), F32), grid=(n // 2, S // T), in_specs=[blk, blk],
        out_specs=pl.BlockSpec((2, T, 1), lambda p, i: (p, i, 0)),
        compiler_params=_cparams(("parallel", "parallel")), name=name,
    )(a, b)


W = B_WIN
N_HG = 3 * B_HPG
N_BLK = S // W


def _dil_tables():
    slopes = np.exp2((-8.0 * np.arange(1, N_HG + 1, dtype=np.float32) / N_HG).astype(np.float32)).astype(np.float32)
    dil = np.repeat(np.array([d for _, d in B_GROUPS], np.float32), B_HPG)
    coef = (slopes * dil).astype(np.float32)
    nbs = np.repeat(np.array([S // d // W for _, d in B_GROUPS], np.int32), B_HPG)
    return jnp.asarray(coef), jnp.asarray(nbs)


DIL_SUB = 8
DIL_ROWS = DIL_SUB * W
DIL_STEPS = S // DIL_ROWS


def _dil_bias(coef, transposed):
    row = lax.broadcasted_iota(jnp.int32, (W, 2 * W), 0)
    col = lax.broadcasted_iota(jnp.int32, (W, 2 * W), 1)
    dist = (col - row) if transposed else (row + W - col)
    valid = jnp.logical_and(dist >= 0, dist <= W)
    return jnp.where(valid, -coef * dist.astype(F32), NEG_INF), col


def _dil_specs():
    blk = pl.BlockSpec((None, DIL_ROWS, HEAD_DIM), lambda h, n: (h, n, 0))
    prev = pl.BlockSpec((None, W, HEAD_DIM), lambda h, n: (h, jnp.maximum(n * DIL_SUB - 1, 0), 0))
    nxt = pl.BlockSpec((None, W, HEAD_DIM), lambda h, n: (h, jnp.minimum((n + 1) * DIL_SUB, N_BLK - 1), 0))
    col = pl.BlockSpec((None, DIL_ROWS, 1), lambda h, n: (h, n, 0))
    row = pl.BlockSpec((None, 1, DIL_ROWS), lambda h, n: (h, 0, n))
    rnxt = pl.BlockSpec((None, 1, W), lambda h, n: (h, 0, jnp.minimum((n + 1) * DIL_SUB, N_BLK - 1)))
    smem = pl.BlockSpec(memory_space=pltpu.SMEM)
    return blk, prev, nxt, col, row, rnxt, smem


def _dil_fwd(q, k, v, name):
    coef_t, nbs_t = _dil_tables()

    def body(coef_ref, nbs_ref, q_ref, kh_ref, k_ref, vh_ref, v_ref, o_ref, lse_ref, kf, vf):
        hg = pl.program_id(0)
        n = pl.program_id(1)
        nbs = nbs_ref[hg]
        kf[0:W, :] = kh_ref[...]
        kf[W:, :] = k_ref[...]
        vf[0:W, :] = vh_ref[...]
        vf[W:, :] = v_ref[...]
        bias, col = _dil_bias(coef_ref[hg], False)
        for b in range(DIL_SUB):
            first = lax.rem(n * DIL_SUB + b, nbs) == 0
            rows = slice(b * W, (b + 1) * W)
            both = slice(b * W, (b + 2) * W)
            s = lax.dot_general(q_ref[rows, :], kf[both, :], _DN["nt"], preferred_element_type=F32) * SCALE + bias
            s = jnp.where(jnp.logical_and(first, col < W), NEG_INF, s)
            m = jnp.max(s, axis=1, keepdims=True)
            p = jnp.exp(s - m)
            l = jnp.sum(p, axis=1, keepdims=True)
            acc = lax.dot_general(p.astype(BF16), vf[both, :], _DN["nn"], preferred_element_type=F32)
            o_ref[rows, :] = acc / l
            lse_ref[rows, :] = m + jnp.log(l)

    blk, prev, _, col, _, _, smem = _dil_specs()
    return pl.pallas_call(
        body, out_shape=[jax.ShapeDtypeStruct((N_HG, S, HEAD_DIM), F32), jax.ShapeDtypeStruct((N_HG, S, 1), F32)],
        grid=(N_HG, DIL_STEPS), in_specs=[smem, smem, blk, prev, blk, prev, blk], out_specs=[blk, col],
        scratch_shapes=[pltpu.VMEM((DIL_ROWS + W, HEAD_DIM), BF16)] * 2,
        compiler_params=_cparams(("parallel", "parallel")), name=name,
    )(coef_t, nbs_t, q, k, k, v, v)


def _dil_merge(o, lse, name):
    T = 1024

    def body(o_ref, lse_ref, om_ref, omb_ref, l_ref):
        l0, l1, l2 = lse_ref[0], lse_ref[1], lse_ref[2]
        m = jnp.maximum(jnp.maximum(l0, l1), l2)
        e0, e1, e2 = jnp.exp(l0 - m), jnp.exp(l1 - m), jnp.exp(l2 - m)
        den = e0 + e1 + e2
        om = (e0 / den) * o_ref[0] + (e1 / den) * o_ref[1] + (e2 / den) * o_ref[2]
        om_ref[...] = om
        omb_ref[...] = om.astype(BF16)
        l_ref[...] = m + jnp.log(den)

    ob = pl.BlockSpec((None, T, HEAD_DIM), lambda h, i: (h, i, 0))
    lb = pl.BlockSpec((None, T, 1), lambda h, i: (h, i, 0))
    return pl.pallas_call(
        body,
        out_shape=[jax.ShapeDtypeStruct((B_HPG, S, HEAD_DIM), F32), jax.ShapeDtypeStruct((B_HPG, S, HEAD_DIM), BF16),
                   jax.ShapeDtypeStruct((B_HPG, S, 1), F32)],
        grid=(B_HPG, S // T),
        in_specs=[pl.BlockSpec((3, None, T, HEAD_DIM), lambda h, i: (0, h, i, 0)),
                  pl.BlockSpec((3, None, T, 1), lambda h, i: (0, h, i, 0))],
        out_specs=[ob, ob, lb], compiler_params=_cparams(("parallel", "parallel")), name=name,
    )(o, lse)


def _dil_bwd_dq(q, k, v, do, lcol, dcol, name):
    coef_t, nbs_t = _dil_tables()

    def body(coef_ref, nbs_ref, q_ref, kh_ref, k_ref, vh_ref, v_ref, do_ref, l_ref, d_ref, dq_ref, kf, vf):
        hg = pl.program_id(0)
        n = pl.program_id(1)
        nbs = nbs_ref[hg]
        kf[0:W, :] = kh_ref[...]
        kf[W:, :] = k_ref[...]
        vf[0:W, :] = vh_ref[...]
        vf[W:, :] = v_ref[...]
        bias, col = _dil_bias(coef_ref[hg], False)
        for b in range(DIL_SUB):
            first = lax.rem(n * DIL_SUB + b, nbs) == 0
            rows = slice(b * W, (b + 1) * W)
            both = slice(b * W, (b + 2) * W)
            kk = kf[both, :]
            s = lax.dot_general(q_ref[rows, :], kk, _DN["nt"], preferred_element_type=F32) * SCALE + bias
            s = jnp.where(jnp.logical_and(first, col < W), NEG_INF, s)
            p = jnp.exp(s - l_ref[rows, :])
            dp = lax.dot_general(do_ref[rows, :], vf[both, :], _DN["nt"], preferred_element_type=F32)
            ds = (p * (dp - d_ref[rows, :]) * SCALE).astype(BF16)
            dq_ref[rows, :] = lax.dot_general(ds, kk, _DN["nn"], preferred_element_type=F32).astype(BF16)

    blk, prev, _, col, _, _, smem = _dil_specs()
    return pl.pallas_call(
        body, out_shape=jax.ShapeDtypeStruct((N_HG, S, HEAD_DIM), BF16), grid=(N_HG, DIL_STEPS),
        in_specs=[smem, smem, blk, prev, blk, prev, blk, blk, col, col], out_specs=blk,
        scratch_shapes=[pltpu.VMEM((DIL_ROWS + W, HEAD_DIM), BF16)] * 2,
        compiler_params=_cparams(("parallel", "parallel")), name=name,
    )(coef_t, nbs_t, q, k, k, v, v, do, lcol, dcol)


def _dil_bwd_dkv(q, k, v, do, lrow, drow, name):
    coef_t, nbs_t = _dil_tables()

    def body(coef_ref, nbs_ref, k_ref, v_ref, q_ref, qn_ref, do_ref, don_ref, l_ref, ln_ref, d_ref, dn_ref,
             dk_ref, dv_ref, qf, dof, lf, df):
        hg = pl.program_id(0)
        n = pl.program_id(1)
        nbs = nbs_ref[hg]
        qf[0:DIL_ROWS, :] = q_ref[...]
        qf[DIL_ROWS:, :] = qn_ref[...]
        dof[0:DIL_ROWS, :] = do_ref[...]
        dof[DIL_ROWS:, :] = don_ref[...]
        lf[:, 0:DIL_ROWS] = l_ref[...]
        lf[:, DIL_ROWS:] = ln_ref[...]
        df[:, 0:DIL_ROWS] = d_ref[...]
        df[:, DIL_ROWS:] = dn_ref[...]
        bias, col = _dil_bias(coef_ref[hg], True)
        for b in range(DIL_SUB):
            no_next = lax.rem(n * DIL_SUB + b + 1, nbs) == 0
            rows = slice(b * W, (b + 1) * W)
            both = slice(b * W, (b + 2) * W)
            qq = qf[both, :]
            dd = dof[both, :]
            st = lax.dot_general(k_ref[rows, :], qq, _DN["nt"], preferred_element_type=F32) * SCALE + bias
            st = jnp.where(jnp.logical_and(no_next, col >= W), NEG_INF, st)
            pt = jnp.exp(st - lf[:, both])
            dv_ref[rows, :] = lax.dot_general(pt.astype(BF16), dd, _DN["nn"], preferred_element_type=F32).astype(BF16)
            dpt = lax.dot_general(v_ref[rows, :], dd, _DN["nt"], preferred_element_type=F32)
            dst = (pt * (dpt - df[:, both]) * SCALE).astype(BF16)
            dk_ref[rows, :] = lax.dot_general(dst, qq, _DN["nn"], preferred_element_type=F32).astype(BF16)

    blk, _, nxt, _, row, rnxt, smem = _dil_specs()
    return pl.pallas_call(
        body, out_shape=[jax.ShapeDtypeStruct((N_HG, S, HEAD_DIM), BF16)] * 2, grid=(N_HG, DIL_STEPS),
        in_specs=[smem, smem, blk, blk, blk, nxt, blk, nxt, row, rnxt, row, rnxt], out_specs=[blk, blk],
        scratch_shapes=[pltpu.VMEM((DIL_ROWS + W, HEAD_DIM), BF16)] * 2 + [pltpu.VMEM((1, DIL_ROWS + W), F32)] * 2,
        compiler_params=_cparams(("parallel", "parallel")), name=name,
    )(coef_t, nbs_t, k, v, q, q, do, do, lrow, lrow, drow, drow)


NPG = B_HPG // 2
GROUP_W = B_HPG * HEAD_DIM


def _dil_pair_specs(qoff, koff, voff):
    prev_blk = lambda n: jnp.maximum(n * DIL_SUB - 1, 0)
    next_blk = lambda n: jnp.minimum((n + 1) * DIL_SUB, N_BLK - 1)
    return dict(
        o=pl.BlockSpec((DIL_ROWS, LANES), lambda h, n: (n, h)),
        o_next=pl.BlockSpec((W, LANES), lambda h, n: (next_blk(n), h)),
        q=pl.BlockSpec((DIL_ROWS, LANES), lambda h, n: (n, qoff + h)),
        q_next=pl.BlockSpec((W, LANES), lambda h, n: (next_blk(n), qoff + h)),
        k=pl.BlockSpec((DIL_ROWS, LANES), lambda h, n: (n, koff + h)),
        k_prev=pl.BlockSpec((W, LANES), lambda h, n: (prev_blk(n), koff + h)),
        v=pl.BlockSpec((DIL_ROWS, LANES), lambda h, n: (n, voff + h)),
        v_prev=pl.BlockSpec((W, LANES), lambda h, n: (prev_blk(n), voff + h)),
        col=pl.BlockSpec((2, DIL_ROWS, 1), lambda h, n: (h, n, 0)),
        row=pl.BlockSpec((2, 1, DIL_ROWS), lambda h, n: (h, 0, n)),
        row_next=pl.BlockSpec((2, 1, W), lambda h, n: (h, 0, next_blk(n))),
        smem=pl.BlockSpec(memory_space=pltpu.SMEM))


def _dil_pair_fwd(g, q, k, v, qoff, koff, voff, name):
    coef_t, nbs_t = _dil_tables()

    def body(coef_ref, nbs_ref, q_ref, kh_ref, k_ref, vh_ref, v_ref, o_ref, lse_ref, kf, vf):
        hp = pl.program_id(0)
        n = pl.program_id(1)
        nbs = nbs_ref[B_HPG * g + 2 * hp]
        kf[0:W, :] = kh_ref[...]
        kf[W:, :] = k_ref[...]
        vf[0:W, :] = vh_ref[...]
        vf[W:, :] = v_ref[...]
        biases = [_dil_bias(coef_ref[B_HPG * g + 2 * hp + h], False) for h in range(2)]
        col = biases[0][1]
        lo = _first_head((W, LANES))
        lo2 = _first_head((2 * W, LANES))
        for b in range(DIL_SUB):
            first = lax.rem(n * DIL_SUB + b, nbs) == 0
            rows = slice(b * W, (b + 1) * W)
            both = slice(b * W, (b + 2) * W)
            qv = q_ref[rows, :]
            acc, inv = None, []
            for h, (kh, vh) in enumerate(zip(_each_head(kf[both, :], lo2), _each_head(vf[both, :], lo2))):
                s = lax.dot_general(qv, kh, _DN["nt"], preferred_element_type=F32) * SCALE + biases[h][0]
                s = jnp.where(jnp.logical_and(first, col < W), NEG_INF, s)
                m = jnp.max(s, axis=1, keepdims=True)
                p = jnp.exp(s - m)
                l = jnp.sum(p, axis=1, keepdims=True)
                d = lax.dot_general(p.astype(BF16), vh, _DN["nn"], preferred_element_type=F32)
                acc = d if acc is None else acc + d
                inv.append(1.0 / l)
                lse_ref[h, rows, :] = m + jnp.log(l)
            o_ref[rows, :] = acc * jnp.where(lo, inv[0], inv[1])

    sp = _dil_pair_specs(qoff, koff, voff)
    return pl.pallas_call(
        body, out_shape=[jax.ShapeDtypeStruct((S, GROUP_W), F32), jax.ShapeDtypeStruct((B_HPG, S, 1), F32)],
        grid=(NPG, DIL_STEPS), in_specs=[sp["smem"], sp["smem"], sp["q"], sp["k_prev"], sp["k"], sp["v_prev"], sp["v"]],
        out_specs=[sp["o"], sp["col"]], scratch_shapes=[pltpu.VMEM((DIL_ROWS + W, LANES), BF16)] * 2,
        compiler_params=_cparams(("parallel", "parallel")), name=name,
    )(coef_t, nbs_t, q, k, k, v, v)


def _dil_pair_merge(os, lses, name):
    T = 1024

    def body(o0_ref, o1_ref, o2_ref, l0_ref, l1_ref, l2_ref, om_ref, omb_ref, l_ref):
        lo = _first_head((T, LANES))
        weights = []
        for h in range(2):
            l0, l1, l2 = l0_ref[h], l1_ref[h], l2_ref[h]
            m = jnp.maximum(jnp.maximum(l0, l1), l2)
            e0, e1, e2 = jnp.exp(l0 - m), jnp.exp(l1 - m), jnp.exp(l2 - m)
            den = e0 + e1 + e2
            weights.append((e0 / den, e1 / den, e2 / den))
            l_ref[h] = m + jnp.log(den)
        om = (jnp.where(lo, weights[0][0], weights[1][0]) * o0_ref[...]
              + jnp.where(lo, weights[0][1], weights[1][1]) * o1_ref[...]
              + jnp.where(lo, weights[0][2], weights[1][2]) * o2_ref[...])
        om_ref[...] = om
        omb_ref[...] = om.astype(BF16)

    ob = pl.BlockSpec((T, LANES), lambda p, i: (i, p))
    lb = pl.BlockSpec((2, T, 1), lambda p, i: (p, i, 0))
    return pl.pallas_call(
        body,
        out_shape=[jax.ShapeDtypeStruct((S, B_OUT_W), F32), jax.ShapeDtypeStruct((S, B_OUT_W), BF16),
                   jax.ShapeDtypeStruct((B_HPG, S, 1), F32)],
        grid=(NPG, S // T), in_specs=[ob] * 3 + [lb] * 3, out_specs=[ob, ob, lb],
        compiler_params=_cparams(("parallel", "parallel")), name=name,
    )(*os, *lses)


def _dil_pair_dq(g, q, k, v, qoff, koff, voff, do, lcol, dcol, name):
    coef_t, nbs_t = _dil_tables()

    def body(coef_ref, nbs_ref, q_ref, kh_ref, k_ref, vh_ref, v_ref, do_ref, l_ref, d_ref, dq_ref, kf, vf):
        hp = pl.program_id(0)
        n = pl.program_id(1)
        nbs = nbs_ref[B_HPG * g + 2 * hp]
        kf[0:W, :] = kh_ref[...]
        kf[W:, :] = k_ref[...]
        vf[0:W, :] = vh_ref[...]
        vf[W:, :] = v_ref[...]
        biases = [_dil_bias(coef_ref[B_HPG * g + 2 * hp + h], False) for h in range(2)]
        col = biases[0][1]
        lo2 = _first_head((2 * W, LANES))
        for b in range(DIL_SUB):
            first = lax.rem(n * DIL_SUB + b, nbs) == 0
            rows = slice(b * W, (b + 1) * W)
            both = slice(b * W, (b + 2) * W)
            qv = q_ref[rows, :]
            dov = do_ref[rows, :]
            acc = None
            for h, (kh, vh) in enumerate(zip(_each_head(kf[both, :], lo2), _each_head(vf[both, :], lo2))):
                s = lax.dot_general(qv, kh, _DN["nt"], preferred_element_type=F32) * SCALE + biases[h][0]
                s = jnp.where(jnp.logical_and(first, col < W), NEG_INF, s)
                p = jnp.exp(s - l_ref[h, rows, :])
                dp = lax.dot_general(dov, vh, _DN["nt"], preferred_element_type=F32)
                ds = (p * (dp - d_ref[h, rows, :]) * SCALE).astype(BF16)
                d = lax.dot_general(ds, kh, _DN["nn"], preferred_element_type=F32)
                acc = d if acc is None else acc + d
            dq_ref[rows, :] = acc.astype(BF16)

    sp = _dil_pair_specs(qoff, koff, voff)
    return pl.pallas_call(
        body, out_shape=jax.ShapeDtypeStruct((S, GROUP_W), BF16), grid=(NPG, DIL_STEPS),
        in_specs=[sp["smem"], sp["smem"], sp["q"], sp["k_prev"], sp["k"], sp["v_prev"], sp["v"], sp["o"], sp["col"],
                  sp["col"]],
        out_specs=sp["o"], scratch_shapes=[pltpu.VMEM((DIL_ROWS + W, LANES), BF16)] * 2,
        compiler_params=_cparams(("parallel", "parallel")), name=name,
    )(coef_t, nbs_t, q, k, k, v, v, do, lcol, dcol)


def _dil_pair_dkv(g, q, k, v, qoff, koff, voff, do, lrow, drow, name):
    coef_t, nbs_t = _dil_tables()

    def body(coef_ref, nbs_ref, k_ref, v_ref, q_ref, qn_ref, do_ref, don_ref, l_ref, ln_ref, d_ref, dn_ref,
             dk_ref, dv_ref, qf, dof, lf, df):
        hp = pl.program_id(0)
        n = pl.program_id(1)
        nbs = nbs_ref[B_HPG * g + 2 * hp]
        qf[0:DIL_ROWS, :] = q_ref[...]
        qf[DIL_ROWS:, :] = qn_ref[...]
        dof[0:DIL_ROWS, :] = do_ref[...]
        dof[DIL_ROWS:, :] = don_ref[...]
        lf[:, :, 0:DIL_ROWS] = l_ref[...]
        lf[:, :, DIL_ROWS:] = ln_ref[...]
        df[:, :, 0:DIL_ROWS] = d_ref[...]
        df[:, :, DIL_ROWS:] = dn_ref[...]
        biases = [_dil_bias(coef_ref[B_HPG * g + 2 * hp + h], True) for h in range(2)]
        col = biases[0][1]
        lo = _first_head((W, LANES))
        lo2 = _first_head((2 * W, LANES))
        for b in range(DIL_SUB):
            no_next = lax.rem(n * DIL_SUB + b + 1, nbs) == 0
            rows = slice(b * W, (b + 1) * W)
            both = slice(b * W, (b + 2) * W)
            dd = dof[both, :]
            dk = dv = None
            for h, (kh, vh, qh, ddh) in enumerate(zip(_each_head(k_ref[rows, :], lo), _each_head(v_ref[rows, :], lo),
                                                      _each_head(qf[both, :], lo2), _each_head(dd, lo2))):
                st = lax.dot_general(kh, qh, _DN["nt"], preferred_element_type=F32) * SCALE + biases[h][0]
                st = jnp.where(jnp.logical_and(no_next, col >= W), NEG_INF, st)
                pt = jnp.exp(st - lf[h, :, both])
                d = lax.dot_general(pt.astype(BF16), ddh, _DN["nn"], preferred_element_type=F32)
                dv = d if dv is None else dv + d
                dpt = lax.dot_general(vh, dd, _DN["nt"], preferred_element_type=F32)
                dst = (pt * (dpt - df[h, :, both]) * SCALE).astype(BF16)
                d = lax.dot_general(dst, qh, _DN["nn"], preferred_element_type=F32)
                dk = d if dk is None else dk + d
            dk_ref[rows, :] = dk.astype(BF16)
            dv_ref[rows, :] = dv.astype(BF16)

    sp = _dil_pair_specs(qoff, koff, voff)
    wide = jax.ShapeDtypeStruct((S, GROUP_W), BF16)
    return pl.pallas_call(
        body, out_shape=[wide, wide], grid=(NPG, DIL_STEPS),
        in_specs=[sp["smem"], sp["smem"], sp["k"], sp["v"], sp["q"], sp["q_next"], sp["o"], sp["o_next"], sp["row"],
                  sp["row_next"], sp["row"], sp["row_next"]],
        out_specs=[sp["o"], sp["o"]],
        scratch_shapes=[pltpu.VMEM((DIL_ROWS + W, LANES), BF16)] * 2 + [pltpu.VMEM((2, 1, DIL_ROWS + W), F32)] * 2,
        compiler_params=_cparams(("parallel", "parallel")), name=name,
    )(coef_t, nbs_t, k, v, q, q, do, do, lrow, lrow, drow, drow)


FFN_ROWS = 512
FFN_COLS = 256
HALO = 8


def _shifted(u, halo, back):
    T = u.shape[0]
    rows = lax.broadcasted_iota(jnp.int32, u.shape, 0)
    if back:
        s1 = jnp.where(rows == 0, halo[HALO - 1:HALO, :], pltpu.roll(u, 1, 0))
        s2 = jnp.where(rows == 0, halo[HALO - 2:HALO - 1, :],
                       jnp.where(rows == 1, halo[HALO - 1:HALO, :], pltpu.roll(u, 2, 0)))
    else:
        s1 = jnp.where(rows == T - 1, halo[0:1, :], pltpu.roll(u, T - 1, 0))
        s2 = jnp.where(rows == T - 1, halo[1:2, :],
                       jnp.where(rows == T - 2, halo[0:1, :], pltpu.roll(u, T - 2, 0)))
    return s1, s2


def _conv_parts(u_ref, h_ref, w_ref, b_ref, first):
    out = []
    for p in range(2):
        u = u_ref[p]
        halo = jnp.where(first, 0.0, h_ref[p])
        u1, u2 = _shifted(u, halo, True)
        w = w_ref[p]
        out.append((w[0:1, :] * u2 + w[1:2, :] * u1 + w[2:3, :] * u + b_ref[p], u1, u2, u))
    return out


def _ffn_specs():
    T, C = FFN_ROWS, FFN_COLS
    blk = pl.BlockSpec((2, T, C), lambda j, i: (0, i, j))
    prev = pl.BlockSpec((2, HALO, C), lambda j, i: (0, jnp.maximum(i * (T // HALO) - 1, 0), j))
    nxt = pl.BlockSpec((2, HALO, C), lambda j, i: (0, jnp.minimum((i + 1) * (T // HALO), S // HALO - 1), j))
    wsp = pl.BlockSpec((2, 3, C), lambda j, i: (0, 0, j))
    bsp = pl.BlockSpec((2, 1, C), lambda j, i: (0, 0, j))
    one = pl.BlockSpec((T, C), lambda j, i: (i, j))
    return blk, prev, nxt, wsp, bsp, one


def _ffn_act_fwd(u, w, b, name):
    blk, prev, _, wsp, bsp, one = _ffn_specs()

    def body(u_ref, h_ref, w_ref, b_ref, o_ref):
        (a, _, _, _), (g, _, _, _) = _conv_parts(u_ref, h_ref, w_ref, b_ref, pl.program_id(1) == 0)
        o_ref[...] = (g / (1.0 + jnp.exp(-g)) * a).astype(BF16)

    return pl.pallas_call(
        body, out_shape=jax.ShapeDtypeStruct((S, FF), BF16), grid=(FF // FFN_COLS, S // FFN_ROWS),
        in_specs=[blk, prev, wsp, bsp], out_specs=one,
        compiler_params=_cparams(("parallel", "parallel")), name=name,
    )(u, u, w, b)


def _ffn_act_bwd(u, dact, w, b, name):
    blk, prev, _, wsp, bsp, one = _ffn_specs()

    def body(u_ref, h_ref, da_ref, w_ref, b_ref, duc_ref, dwb_ref):
        i = pl.program_id(1)
        (a, a1, a2, a0), (g, g1, g2, g0) = _conv_parts(u_ref, h_ref, w_ref, b_ref, i == 0)
        dact_v = da_ref[...]
        sg = 1.0 / (1.0 + jnp.exp(-g))
        d_a = dact_v * (g * sg)
        d_g = dact_v * a * (sg * (1.0 + g * (1.0 - sg)))
        duc_ref[0] = d_a
        duc_ref[1] = d_g

        @pl.when(i == 0)
        def _():
            dwb_ref[...] = jnp.zeros(dwb_ref.shape, F32)

        for p, (d, s2, s1, s0) in enumerate(((d_a, a2, a1, a0), (d_g, g2, g1, g0))):
            dwb_ref[p, 0:1, :] += jnp.sum(d * s2, axis=0, keepdims=True)
            dwb_ref[p, 1:2, :] += jnp.sum(d * s1, axis=0, keepdims=True)
            dwb_ref[p, 2:3, :] += jnp.sum(d * s0, axis=0, keepdims=True)
            dwb_ref[p, 3:4, :] += jnp.sum(d, axis=0, keepdims=True)

    return pl.pallas_call(
        body, out_shape=[jax.ShapeDtypeStruct((2, S, FF), F32), jax.ShapeDtypeStruct((2, 8, FF), F32)],
        grid=(FF // FFN_COLS, S // FFN_ROWS), in_specs=[blk, prev, one, wsp, bsp],
        out_specs=[blk, pl.BlockSpec((2, 8, FFN_COLS), lambda j, i: (0, 0, j))],
        compiler_params=_cparams(("parallel", "arbitrary")), name=name,
    )(u, u, dact, w, b)


def _ffn_conv_bwd(duc, w, name):
    blk, _, nxt, wsp, _, _ = _ffn_specs()
    last = S // FFN_ROWS - 1

    def body(d_ref, h_ref, w_ref, du_ref):
        is_last = pl.program_id(1) == last
        for p in range(2):
            d = d_ref[p]
            halo = jnp.where(is_last, 0.0, h_ref[p])
            d1, d2 = _shifted(d, halo, False)
            wv = w_ref[p]
            du_ref[p] = (wv[2:3, :] * d + wv[1:2, :] * d1 + wv[0:1, :] * d2).astype(BF16)

    return pl.pallas_call(
        body, out_shape=jax.ShapeDtypeStruct((2, S, FF), BF16), grid=(FF // FFN_COLS, S // FFN_ROWS),
        in_specs=[blk, nxt, wsp], out_specs=blk,
        compiler_params=_cparams(("parallel", "parallel")), name=name,
    )(duc, duc, w)


def _adam_update(w, gv, m, v):
    c1 = 1.0 / (1.0 - ADAM_B1 ** ADAM_STEP)
    c2 = 1.0 / (1.0 - ADAM_B2 ** ADAM_STEP)
    mn = ADAM_B1 * m + (1.0 - ADAM_B1) * gv
    vn = ADAM_B2 * v + (1.0 - ADAM_B2) * (gv * gv)
    return -ADAM_LR * ((mn * c1) / (jnp.sqrt(vn * c2) + ADAM_EPS) + ADAM_WD * w), mn, vn


def _adamw(w, g, m, v, name):
    rows = w.shape[0]
    T = 8
    for cand in (256, 128, 64, 32, 16, 8):
        if rows % cand == 0:
            T = cand
            break

    def body(w_ref, g_ref, m_ref, v_ref, d_ref, mo_ref, vo_ref):
        d_ref[...], mo_ref[...], vo_ref[...] = _adam_update(w_ref[...], g_ref[...], m_ref[...], v_ref[...])

    blk = pl.BlockSpec((T, w.shape[1]), lambda i: (i, 0))
    sds = jax.ShapeDtypeStruct(w.shape, F32)
    return pl.pallas_call(
        body, out_shape=[sds, sds, sds], grid=(rows // T,), in_specs=[blk] * 4, out_specs=[blk] * 3,
        compiler_params=_cparams(("parallel",)), name=name,
    )(w, g, m, v)


ANY = pl.BlockSpec(memory_space=pl.ANY)


def _place():
    x, y, c = lax.axis_index("x"), lax.axis_index("y"), lax.axis_index("c")
    chips = [(1 - x, y), (x, 1 - y), (1 - x, 1 - y)]
    return x, y, c, chips


def _place_own(w, slot_arr, name):
    rows = w.shape[0]
    T = 16
    for cand in (2048, 1024, 512, 256, 128, 64, 32, 16):
        if rows % cand == 0:
            T = cand
            break

    def body(k_ref, w_ref, o_ref):
        o_ref[...] = w_ref[...]

    return pl.pallas_call(
        body, out_shape=jax.ShapeDtypeStruct((N_CHIPS, rows, FLAT_W), w.dtype),
        grid_spec=pltpu.PrefetchScalarGridSpec(
            num_scalar_prefetch=1, grid=(rows // T,),
            in_specs=[pl.BlockSpec((T, FLAT_W), lambda i, k: (i, 0))],
            out_specs=pl.BlockSpec((None, T, FLAT_W), lambda i, k: (k[0], i, 0))),
        compiler_params=_cparams(("parallel",)), name=name,
    )(slot_arr, w)


def _allgather_shards(w, buf):
    half_rows = w.shape[0] // 2
    assert half_rows % 16 == 0

    def body(w_ref, buf_ref, g_ref, send_sems, recv_sems):
        x, y, c, chips = _place()
        myk = 2 * x + y
        sibling = (x, y, 1 - c)
        h0 = pl.multiple_of(c * half_rows, 16)
        h1 = pl.multiple_of((1 - c) * half_rows, 16)

        def half(k, start):
            return g_ref.at[k, pl.ds(start, half_rows), :]

        def rcopy(sem, src, dst, to):
            return pltpu.make_async_remote_copy(src_ref=src, dst_ref=dst, send_sem=send_sems.at[sem],
                                                recv_sem=recv_sems.at[sem], device_id=to, device_id_type=MESH)

        ici = [rcopy(r, w_ref.at[pl.ds(h0, half_rows), :], half(myk, h0), (*chip, c)) for r, chip in enumerate(chips)]
        for cp in ici:
            cp.start()
        ks = [2 * cx + cy for cx, cy in chips]
        fwd = [rcopy(3 + r, half(ks[r], h0), half(ks[r], h0), sibling) for r in range(3)]
        for r in range(3):
            rcopy(r, half(ks[r], h0), half(ks[r], h0), (*chips[r], c)).wait_recv()
            fwd[r].start()
        for r in range(3):
            rcopy(3 + r, half(ks[r], h1), half(ks[r], h1), sibling).wait_recv()
        for cp in ici + fwd:
            cp.wait_send()

    return pl.pallas_call(
        body, out_shape=jax.ShapeDtypeStruct(buf.shape, w.dtype), in_specs=[ANY, ANY], out_specs=ANY,
        scratch_shapes=[pltpu.SemaphoreType.DMA((6,)), pltpu.SemaphoreType.DMA((6,))],
        input_output_aliases={1: 0},
        compiler_params=pltpu.CompilerParams(has_side_effects=True), name="allgather_shards",
    )(w, buf)


HBM_SPEC = pl.BlockSpec(memory_space=pltpu.HBM)
SEM_SPEC = pl.BlockSpec(memory_space=pltpu.SEMAPHORE)
DATAFLOW = pltpu.SideEffectType.DATAFLOW_SIDE_EFFECTING
OWN_SLOT = 3


def _late_gather_start(w, land):
    def body(w_ref, land_ref, send_sems, recv_sems, w_thru, land_thru, token):
        x, y, c, chips = _place()
        for r, chip in enumerate(chips):
            pltpu.make_async_remote_copy(src_ref=w_ref, dst_ref=land_ref.at[r], send_sem=send_sems.at[r],
                                         recv_sem=recv_sems.at[r], device_id=(*chip, c), device_id_type=MESH).start()
        token[...] = jnp.zeros_like(token)

    return pl.pallas_call(
        body, name="late_gather_start",
        out_shape=(pltpu.SemaphoreType.DMA((3,)), pltpu.SemaphoreType.DMA((3,)), pltpu.HBM(w.shape, w.dtype),
                   pltpu.HBM(land.shape, land.dtype), jax.ShapeDtypeStruct((8, LANES), F32)),
        in_specs=(HBM_SPEC, HBM_SPEC),
        out_specs=(SEM_SPEC, SEM_SPEC, HBM_SPEC, HBM_SPEC, pl.BlockSpec(memory_space=pltpu.VMEM)),
        input_output_aliases={0: 2, 1: 3}, compiler_params=pltpu.CompilerParams(has_side_effects=DATAFLOW),
    )(pltpu.with_memory_space_constraint(w, pltpu.HBM), pltpu.with_memory_space_constraint(land, pltpu.HBM))


def _late_gather_wait(send_sems, recv_sems, w_thru, land_thru, after):
    def body(w_ref, land_ref, send_sems, recv_sems, after_ref, w_dead, got_ref):
        x, y, c, chips = _place()
        for r, chip in enumerate(chips):
            cp = pltpu.make_async_remote_copy(src_ref=w_ref, dst_ref=land_ref.at[r], send_sem=send_sems.at[r],
                                              recv_sem=recv_sems.at[r], device_id=(*chip, c), device_id_type=MESH)
            cp.wait_send()
            cp.wait_recv()

    return pl.pallas_call(
        body, name="late_gather_wait",
        out_shape=(pltpu.HBM(w_thru.shape, w_thru.dtype), pltpu.HBM(land_thru.shape, land_thru.dtype)),
        in_specs=(HBM_SPEC, HBM_SPEC, SEM_SPEC, SEM_SPEC, pl.BlockSpec(memory_space=pl.ANY)),
        out_specs=(HBM_SPEC, HBM_SPEC), input_output_aliases={0: 0, 1: 1},
        compiler_params=pltpu.CompilerParams(has_side_effects=DATAFLOW),
    )(w_thru, land_thru, send_sems, recv_sems, after)


def _sibling_swap_half(g):
    def body(g_ref, o_ref, send_sem, recv_sem):
        x, y, c, _ = _place()
        theirs = pl.multiple_of((1 - c) * HALF_ROWS, 8)
        cp = pltpu.make_async_remote_copy(src_ref=g_ref.at[:, pl.ds(theirs, HALF_ROWS), :], dst_ref=o_ref,
                                          send_sem=send_sem, recv_sem=recv_sem, device_id=(x, y, 1 - c),
                                          device_id_type=MESH)
        cp.start()
        cp.wait()

    return pl.pallas_call(
        body, out_shape=jax.ShapeDtypeStruct((N_CHIPS, HALF_ROWS, FLAT_W), F32), in_specs=[ANY], out_specs=ANY,
        scratch_shapes=[pltpu.SemaphoreType.DMA, pltpu.SemaphoreType.DMA],
        compiler_params=pltpu.CompilerParams(has_side_effects=True), name="rs_sibling_swap",
    )(g)


def _pair_sum(g, other, c_arr):
    T = FLAT_T

    def body(c_ref, g_ref, o_ref, s_ref):
        s_ref[...] = (g_ref[...] + o_ref[...]).astype(BF16)

    nb = HALF_ROWS // T
    return pl.pallas_call(
        body, out_shape=jax.ShapeDtypeStruct((N_CHIPS, HALF_ROWS, FLAT_W), BF16),
        grid_spec=pltpu.PrefetchScalarGridSpec(
            num_scalar_prefetch=1, grid=(N_CHIPS, nb),
            in_specs=[pl.BlockSpec((None, T, FLAT_W), lambda k, i, c: (k, c[0] * nb + i, 0)),
                      pl.BlockSpec((None, T, FLAT_W), lambda k, i, c: (k, i, 0))],
            out_specs=pl.BlockSpec((None, T, FLAT_W), lambda k, i, c: (k, i, 0))),
        compiler_params=_cparams(("parallel", "parallel")), name="rs_pair_sum",
    )(c_arr, g, other)


def _chip_exchange(s):
    def body(s_ref, o_ref, send_sems, recv_sems):
        x, y, c, chips = _place()
        cps = []
        for r, (cx, cy) in enumerate(chips):
            cps.append(pltpu.make_async_remote_copy(
                src_ref=s_ref.at[2 * cx + cy], dst_ref=o_ref.at[r], send_sem=send_sems.at[r],
                recv_sem=recv_sems.at[r], device_id=(cx, cy, c), device_id_type=MESH))
        for cp in cps:
            cp.start()
        for cp in cps:
            cp.wait()

    return pl.pallas_call(
        body, out_shape=jax.ShapeDtypeStruct((3, HALF_ROWS, FLAT_W), BF16), in_specs=[ANY], out_specs=ANY,
        scratch_shapes=[pltpu.SemaphoreType.DMA((3,)), pltpu.SemaphoreType.DMA((3,))],
        compiler_params=pltpu.CompilerParams(has_side_effects=True), name="rs_chip_exchange",
    )(s)


def _chip_sum(s, r, k_arr):
    T = FLAT_T

    def body(k_ref, s_ref, r_ref, o_ref):
        o_ref[...] = ((s_ref[...].astype(F32) + r_ref[0].astype(F32)) + r_ref[1].astype(F32)) + r_ref[2].astype(F32)

    return pl.pallas_call(
        body, out_shape=jax.ShapeDtypeStruct((HALF_ROWS, FLAT_W), F32),
        grid_spec=pltpu.PrefetchScalarGridSpec(
            num_scalar_prefetch=1, grid=(HALF_ROWS // T,),
            in_specs=[pl.BlockSpec((None, T, FLAT_W), lambda i, k: (k[0], i, 0)),
                      pl.BlockSpec((3, T, FLAT_W), lambda i, k: (0, i, 0))],
            out_specs=pl.BlockSpec((T, FLAT_W), lambda i, k: (i, 0))),
        compiler_params=_cparams(("parallel",)), name="rs_chip_sum",
    )(k_arr, s, r)


def _sibling_send(t):
    def body(t_ref, o_ref, send_sem, recv_sem):
        x, y, c, _ = _place()
        cp = pltpu.make_async_remote_copy(src_ref=t_ref, dst_ref=o_ref, send_sem=send_sem, recv_sem=recv_sem,
                                          device_id=(x, y, 1 - c), device_id_type=MESH)
        cp.start()
        cp.wait()

    return pl.pallas_call(
        body, out_shape=jax.ShapeDtypeStruct((HALF_ROWS, FLAT_W), F32), in_specs=[ANY], out_specs=ANY,
        scratch_shapes=[pltpu.SemaphoreType.DMA, pltpu.SemaphoreType.DMA],
        compiler_params=pltpu.CompilerParams(has_side_effects=True), name="rs_sibling_send",
    )(t)


def _allreduce_small(v):
    def body(v_ref, o_ref, buf, send_sems, recv_sems):
        x, y, c, _ = _place()
        me = 4 * x + 2 * y + c
        buf[me] = v_ref[...]
        cps = []
        for mask in range(1, 8):
            a, b, d = (mask >> 2) & 1, (mask >> 1) & 1, mask & 1
            peer = (x + a - 2 * a * x, y + b - 2 * b * y, c + d - 2 * d * c)
            cps.append(pltpu.make_async_remote_copy(
                src_ref=v_ref, dst_ref=buf.at[me], send_sem=send_sems.at[mask - 1], recv_sem=recv_sems.at[mask - 1],
                device_id=peer, device_id_type=MESH))
        for cp in cps:
            cp.start()
        for cp in cps:
            cp.wait()
        total = buf[0]
        for dev in range(1, 8):
            total = total + buf[dev]
        o_ref[...] = total

    vm = pl.BlockSpec(memory_space=pltpu.VMEM)
    return pl.pallas_call(
        body, out_shape=jax.ShapeDtypeStruct((SMALL_ROWS, 1024), F32), in_specs=[vm], out_specs=vm,
        scratch_shapes=[pltpu.VMEM((8, SMALL_ROWS, 1024), F32), pltpu.SemaphoreType.DMA((7,)),
                        pltpu.SemaphoreType.DMA((7,))],
        compiler_params=pltpu.CompilerParams(has_side_effects=True), name="allreduce_small",
    )(v)


def _heads(t, n):
    return t.reshape(S, n, HEAD_DIM).transpose(1, 0, 2)


def _unheads(t):
    return t.transpose(1, 0, 2).reshape(S, t.shape[0] * HEAD_DIM)


def _to_residue(t, d):
    c = t.shape[-1]
    return t.reshape(B_HPG, S // d, d, c).transpose(0, 2, 1, 3).reshape(B_HPG, S, c)


def _from_residue(t, d):
    c = t.shape[-1]
    return t.reshape(B_HPG, d, S // d, c).transpose(0, 2, 1, 3).reshape(B_HPG, S, c)


def _dil_pack(t):
    return jnp.concatenate([_to_residue(t[g], d) for g, (_, d) in enumerate(B_GROUPS)], axis=0)


def _dil_unpack(t):
    return jnp.stack([_from_residue(t[g * B_HPG:(g + 1) * B_HPG], d) for g, (_, d) in enumerate(B_GROUPS)], axis=0)


def _col_to_row(t):
    return t.reshape(t.shape[0], 1, S)


def _residue_rows(t, d, inverse=False):
    if d == 1:
        return t
    shape = (d, S // d) if inverse else (S // d, d)
    return t.reshape(shape + t.shape[1:]).transpose(1, 0, 2).reshape(t.shape)


def _residue_vecs(t, d, inverse=False):
    if d == 1:
        return t
    shape = (d, S // d) if inverse else (S // d, d)
    return t.reshape((B_HPG,) + shape).transpose(0, 2, 1).reshape(B_HPG, S, 1)


def _ffn_fwd(x, g, w_up, cw, cb, w_down, tag):
    h = _rms_fwd(x, g, f"{tag}_norm")
    u = _mm(h, w_up, mode="nn", tm=1024, tn=1408, tk=1024, o_split=2, name=f"{tag}_up")
    act = _ffn_act_fwd(u, cw, cb, f"{tag}_act")
    x_out = _mm(act, w_down, mode="nn", tm=1024, tn=512, tk=FF, res=x, name=f"{tag}_down")
    return x_out, (h, u, act)


def _ffn_bwd(x, g, w_up, cw, cb, w_down, saved, dx, dxb, tag):
    h, u, act = saved
    d_w_down = _mm(act, dxb, mode="tn", tm=1408, tn=512, tk=1024, name=f"{tag}_dwdown")
    dact = _mm(dxb, w_down, mode="nt", tm=1024, tn=1408, tk=1024, name=f"{tag}_dact")
    duc, dwb = _ffn_act_bwd(u, dact, cw, cb, f"{tag}_dgate")
    du = _ffn_conv_bwd(duc, cw, f"{tag}_dconv")
    d_w_up = _mm(h, du, mode="tn", tm=1024, tn=1408, tk=1024, b_split=2, name=f"{tag}_dwup")
    dh = _mm(du, w_up, mode="nt", tm=1024, tn=512, tk=1408, a_split=2, name=f"{tag}_dh")
    dx_new, dxb_new, (dg,) = _rms_bwd(x, dx, [(g, dh)], f"{tag}_dnorm")
    d_cw = dwb[:, 0:3, :].transpose(1, 0, 2).reshape(3, 2 * FF)
    d_cb = dwb[:, 3, :].reshape(2 * FF)
    return dx_new, dxb_new, dict(w_up=d_w_up, w_down=d_w_down, conv_w=d_cw, conv_b=d_cb, norm_g=dg.reshape(D))


def _local_step(x, target, p, late_weights):
    g = {}
    h1 = _rms_fwd(x, p["mix_norm_g"][0], "a_norm")
    w_qkv = p["a_w_in"][:, :QKV_W]
    w_f = jnp.pad(p["a_w_in"][:, QKV_W:], ((0, 0), (0, LANES - A_HEADS)))
    b_f = jnp.pad(p["a_b_f"].reshape(1, A_HEADS), ((0, 0), (0, LANES - A_HEADS)))
    qkv = _mm(h1, w_qkv, mode="nn", tm=1024, tn=512, tk=1024, out_dtype=BF16, name="a_qkv")
    pf = _mm(h1, w_f, mode="nn", tm=1024, tn=LANES, tk=1024, name="a_gate")
    cum = _fgate_fwd(pf, b_f, "a_gate_scan")
    aug_q, aug_k = _gate_lanes(cum)
    oa2, lse_a = _fox_pair_fwd(qkv, aug_q, aug_k, "a_attn")
    x1 = _mm(oa2, p["a_w_out"], mode="nn", tm=1024, tn=512, tk=1024, res=x, name="a_out")
    p = {**p, **late_weights(x1)}
    x2, ffn0 = _ffn_fwd(x1, p["ffn_norm_g"][0], p["ffn_w_up"][0], p["conv_w"][0], p["conv_b"][0], p["ffn_w_down"][0], "f0")
    hk = _rms_fwd(x2, p["kv_norm_g"], "kv_norm")
    kvb = _mm(hk, p["w_kv"], mode="nn", tm=1024, tn=512, tk=1024, out_dtype=BF16, name="kv_proj")
    h3 = _rms_fwd(x2, p["mix_norm_g"][1], "b_norm")
    qb = _mm(h3, p["b_w_q"], mode="nn", tm=1024, tn=512, tk=1024, out_dtype=BF16, name="b_q")
    dil_in = []
    for gi, (_, d) in enumerate(B_GROUPS):
        if d == 1:
            dil_in.append((qb, kvb, kvb, gi * NPG, gi * NPG, (3 + gi) * NPG))
        else:
            qg = _residue_rows(qb[:, gi * GROUP_W:(gi + 1) * GROUP_W], d)
            kvg = _residue_rows(kvb.reshape(S, 2, 3, GROUP_W)[:, :, gi, :].reshape(S, 2 * GROUP_W), d)
            dil_in.append((qg, kvg, kvg, 0, 0, NPG))
    o_g, lse_g = [], []
    for gi, (_, d) in enumerate(B_GROUPS):
        qg, kg, vg, qoff, koff, voff = dil_in[gi]
        og, lg = _dil_pair_fwd(gi, qg, kg, vg, qoff, koff, voff, f"b_attn{gi}")
        o_g.append(_residue_rows(og, d, inverse=True))
        lse_g.append(_residue_vecs(lg, d, inverse=True))
    ob, ob2, lse_b = _dil_pair_merge(o_g, lse_g, "b_merge")
    x3 = _mm(ob2, p["b_w_out"], mode="nn", tm=1024, tn=512, tk=B_OUT_W, res=x2, name="b_out")
    x4, ffn1 = _ffn_fwd(x3, p["ffn_norm_g"][1], p["ffn_w_up"][1], p["conv_w"][1], p["conv_b"][1], p["ffn_w_down"][1], "f1")
    loss, dx, dxb, dg_final = _loss_head(x4, p["final_norm_g"], target, "loss_head")
    g["final_norm_g"] = dg_final.reshape(D)

    dx, dxb, gf1 = _ffn_bwd(x3, p["ffn_norm_g"][1], p["ffn_w_up"][1], p["conv_w"][1], p["conv_b"][1], p["ffn_w_down"][1],
                            ffn1, dx, dxb, "f1")
    g["b_w_out"] = _mm(ob2, dxb, mode="tn", tm=B_OUT_W, tn=512, tk=1024, name="b_dwout")
    dob = _mm(dxb, p["b_w_out"], mode="nt", tm=1024, tn=B_OUT_W, tk=1024, name="b_do")
    delta_b = _pair_rowdot(dob, ob, "b_delta")
    dob16 = dob.astype(BF16)
    dq_g, dk_g, dv_g = [], [], []
    for gi, (_, d) in enumerate(B_GROUPS):
        qg, kg, vg, qoff, koff, voff = dil_in[gi]
        dog, l_d, dl_d = _residue_rows(dob16, d), _residue_vecs(lse_b, d), _residue_vecs(delta_b, d)
        dqd = _dil_pair_dq(gi, qg, kg, vg, qoff, koff, voff, dog, l_d, dl_d, f"b_dq{gi}")
        dkd, dvd = _dil_pair_dkv(gi, qg, kg, vg, qoff, koff, voff, dog, _col_to_row(l_d), _col_to_row(dl_d), f"b_dkv{gi}")
        dq_g.append(_residue_rows(dqd, d, inverse=True))
        dk_g.append(_residue_rows(dkd, d, inverse=True))
        dv_g.append(_residue_rows(dvd, d, inverse=True))
    dqb = jnp.concatenate(dq_g, axis=1)
    dkvb = jnp.concatenate(dk_g + dv_g, axis=1)
    g["b_w_q"] = _mm(h3, dqb, mode="tn", tm=1024, tn=512, tk=1024, name="b_dwq")
    dh3 = _mm(dqb, p["b_w_q"], mode="nt", tm=1024, tn=512, tk=B_Q_W, name="b_dh")
    g["w_kv"] = _mm(hk, dkvb, mode="tn", tm=1024, tn=512, tk=1024, name="kv_dw")
    dhk = _mm(dkvb, p["w_kv"], mode="nt", tm=1024, tn=512, tk=1536, name="kv_dh")
    dx, dxb, (dg_mix1, dg_kv) = _rms_bwd(x2, dx, [(p["mix_norm_g"][1], dh3), (p["kv_norm_g"], dhk)], "b_dnorm")
    g["kv_norm_g"] = dg_kv.reshape(D)
    dx, dxb, gf0 = _ffn_bwd(x1, p["ffn_norm_g"][0], p["ffn_w_up"][0], p["conv_w"][0], p["conv_b"][0], p["ffn_w_down"][0],
                            ffn0, dx, dxb, "f0")
    g["a_w_out"] = _mm(oa2, dxb, mode="tn", tm=1024, tn=512, tk=1024, name="a_dwout")
    doa = _mm(dxb, p["a_w_out"], mode="nt", tm=1024, tn=512, tk=1024, name="a_do")
    delta_a = _pair_rowdot(doa, oa2, "a_delta")
    dqa, dka, dva, dck, dcq = _fox_pair_bwd(qkv, doa, _col_to_row(lse_a), _col_to_row(delta_a), aug_q, aug_k, "a_dattn")
    dqkv = jnp.concatenate([dqa, dka, dva], axis=1)
    pad_heads = lambda t: jnp.pad(t.reshape(A_HEADS, S).T, ((0, 0), (0, LANES - A_HEADS)))
    dpf, db_f = _fgate_bwd(pf, b_f, pad_heads(dck), pad_heads(dcq), "a_dgate_scan")
    g["a_b_f"] = db_f[:, :A_HEADS]
    d_w_qkv = _mm(h1, dqkv, mode="tn", tm=1024, tn=512, tk=1024, name="a_dwqkv")
    d_w_f = _mm(h1, dpf, mode="tn", tm=1024, tn=LANES, tk=1024, name="a_dwgate")
    g["a_w_in"] = jnp.concatenate([d_w_qkv, d_w_f[:, :A_HEADS]], axis=1)
    dh1 = _mm(dqkv, w_qkv, mode="nt", tm=1024, tn=512, tk=1536, name="a_dh")
    dh1 = _mm(dpf, w_f, mode="nt", tm=1024, tn=512, tk=LANES, res=dh1, name="a_dh_gate")
    dx, _, (dg_mix0,) = _rms_bwd(x, dx, [(p["mix_norm_g"][0], dh1)], "a_dnorm")

    g["mix_norm_g"] = jnp.stack([dg_mix0.reshape(D), dg_mix1.reshape(D)])
    g["ffn_norm_g"] = jnp.stack([gf0["norm_g"], gf1["norm_g"]])
    g["ffn_w_up"] = jnp.stack([gf0["w_up"], gf1["w_up"]])
    g["ffn_w_down"] = jnp.stack([gf0["w_down"], gf1["w_down"]])
    g["ffn_conv_w"] = jnp.stack([gf0["conv_w"], gf1["conv_w"]])
    g["ffn_conv_b"] = jnp.stack([gf0["conv_b"], gf1["conv_b"]])
    return loss[0, 0], dx, g


_SHARD_SHAPES = {"a_w_in": (1, 1024, 772), "a_w_out": (1, 256, 1024), "b_w_q": (1, 1024, 384), "b_w_out": (1, 512, 256),
                 "w_kv": (1024, 768), "ffn_w_up": (2, 1024, 1408), "ffn_w_down": (2, 704, 1024), "ffn_conv_w": (2, 3, 1408)}
_SHARD_AXIS = {"a_w_in": 2, "a_w_out": 1, "b_w_q": 2, "b_w_out": 2, "w_kv": 1, "ffn_w_up": 2, "ffn_w_down": 1,
               "ffn_conv_w": 2}
_SMALL = (("kv_norm_g", (1024,)), ("mix_norm_g", (2, 1024)), ("ffn_norm_g", (2, 1024)), ("final_norm_g", (1024,)),
          ("a_b_f", (1, 16)), ("ffn_conv_b", (2, 5632)))


def _slabs(t, L, R, C, rpad):
    nc = -(-C // FLAT_W)
    t = jnp.pad(t.reshape(L, R, C), ((0, 0), (0, rpad - R), (0, nc * FLAT_W - C)))
    return t.reshape(L, rpad, nc, FLAT_W).transpose(0, 2, 1, 3).reshape(L * nc * rpad, FLAT_W)


def _unslabs(rows, L, R, C, rpad):
    nc = -(-C // FLAT_W)
    return rows.reshape(L, nc, rpad, FLAT_W).transpose(0, 2, 1, 3).reshape(L, rpad, nc * FLAT_W)[:, :R, :C]


_SEG_RT = {"ffn_w_down": 704, "a_w_in": 1024, "a_w_out": 256, "b_w_q": 1024, "b_w_out": 512, "w_kv": 1024,
           "ffn_w_up": 1024, "ffn_conv_w": 16}
_ROW_SHARDED = ("a_w_out", "ffn_w_down")


_LAYOUTS = {"grad": tuple(s[0] for s in _SEGS), "early": ("a_w_in", "a_w_out"),
            "late": ("ffn_w_down", "b_w_q", "b_w_out", "w_kv", "ffn_w_up", "ffn_conv_w")}


def _layout_rows(layout):
    if layout == "grad":
        return FLAT_ROWS
    return sum(_seg_rows(*s) for s in _SEGS if s[0] in _LAYOUTS[layout])


def _seg(name, layout="grad"):
    off = 0
    for s in sorted((s for s in _SEGS if s[0] in _LAYOUTS[layout]), key=lambda s: _LAYOUTS[layout].index(s[0])):
        if s[0] == name:
            _, L, R, C, rpad = s
            rt = _SEG_RT[name]
            assert off % rt == 0 and rpad % rt == 0
            assert layout != "grad" or HALF_ROWS % rt == 0 or off + _seg_rows(*s) <= HALF_ROWS
            return dict(L=L, R=R, C=C, rpad=rpad, nc=-(-C // FLAT_W), rt=rt, off=off, ni=rpad // rt)
        off += _seg_rows(*s)
    raise KeyError(name)


def _flat_block(sg, term=0):
    base = (sg["off"] + term * sg["L"] * sg["nc"] * sg["rpad"]) // sg["rt"]
    return lambda l, j, i: base + (l * sg["nc"] + j) * sg["ni"] + i


def _native3(t, name):
    sg = _seg(name)
    t = t.reshape(sg["L"], sg["R"], sg["C"])
    return jnp.pad(t, ((0, 0), (0, sg["rpad"] - sg["R"]), (0, 0))) if sg["rpad"] != sg["R"] else t


def _slab_pack(flat, t, name, layout, term=None):
    sg = _seg(name, layout)
    rt = sg["rt"]
    rb = _flat_block(sg, term or 0)

    def body(*refs):
        t_ref, o_ref = refs[-2], refs[-1]
        val = t_ref[...]
        o_ref[...] = val.astype(BF16) if term is None else _split3(val)[term]

    in_specs = [pl.BlockSpec((None, rt, FLAT_W), lambda l, j, i: (l, i, j))]
    args = [t]
    if flat is not None:
        in_specs, args = [ANY] + in_specs, [flat] + args
    return pl.pallas_call(
        body, out_shape=jax.ShapeDtypeStruct((_layout_rows(layout), FLAT_W), BF16), grid=(sg["L"], sg["nc"], sg["ni"]),
        in_specs=in_specs, out_specs=pl.BlockSpec((rt, FLAT_W), lambda l, j, i: (rb(l, j, i), 0)),
        input_output_aliases={0: 0} if flat is not None else {},
        compiler_params=_cparams(("parallel", "parallel", "parallel")), name=f"pack_{name}_{term or 0}",
    )(*args)


def _full_spec(sg, name):
    rt, nc, ni = sg["rt"], sg["nc"], sg["ni"]
    if name in _ROW_SHARDED:
        return (sg["L"], N_CHIPS * sg["R"], sg["C"]), pl.BlockSpec((None, rt, FLAT_W), lambda k, l, j, i: (l, k * ni + i, j))
    return ((sg["L"], sg["rpad"], N_CHIPS * nc * FLAT_W),
            pl.BlockSpec((None, rt, FLAT_W), lambda k, l, j, i: (l, i, k * nc + j)))


def _slab_unpack(gathered, slots, name, layout, own=None):
    sg = _seg(name, layout)
    rb = _flat_block(sg)
    shape, _ = _full_spec(sg, name)
    rt, nc, ni = sg["rt"], sg["nc"], sg["ni"]
    width = nc * FLAT_W
    last = gathered.shape[0] - 1

    def body(*refs):
        s_ref, o_ref = refs[0], refs[-1]
        is_own = s_ref[pl.program_id(0)] == OWN_SLOT
        for j in range(nc):
            val = refs[1 + j][...]
            if own is not None:
                val = jnp.where(is_own, refs[1 + nc + j][...], val)
            o_ref[:, j * FLAT_W:(j + 1) * FLAT_W] = val

    if name in _ROW_SHARDED:
        o_spec = pl.BlockSpec((None, rt, width), lambda k, l, i, s: (l, k * ni + i, 0))
    else:
        o_spec = pl.BlockSpec((None, rt, width), lambda k, l, i, s: (l, i, k))
    in_specs = [pl.BlockSpec((None, rt, FLAT_W), lambda k, l, i, s, j=j: (jnp.minimum(s[k], last), rb(l, j, i), 0))
                for j in range(nc)]
    args = [gathered] * nc
    if own is not None:
        in_specs += [pl.BlockSpec((rt, FLAT_W), lambda k, l, i, s, j=j: (rb(l, j, i), 0)) for j in range(nc)]
        args += [own] * nc
    return pl.pallas_call(
        body, out_shape=jax.ShapeDtypeStruct(shape, BF16),
        grid_spec=pltpu.PrefetchScalarGridSpec(num_scalar_prefetch=1, grid=(N_CHIPS, sg["L"], ni), in_specs=in_specs,
                                               out_specs=o_spec),
        compiler_params=_cparams(("parallel",) * 3), name=f"unpack_{name}",
    )(slots, *args)


def _slab_pack_grad(flat4, g, name):
    sg = _seg(name)
    rb = _flat_block(sg)
    shape, spec = _full_spec(sg, name)
    assert g.shape == shape, (name, g.shape, shape)

    def body(*refs):
        refs[-1][...] = refs[-2][...]

    in_specs, args = [spec], [g]
    if flat4 is not None:
        in_specs, args = [pl.BlockSpec(memory_space=pl.ANY)] + in_specs, [flat4] + args
    return pl.pallas_call(
        body, out_shape=jax.ShapeDtypeStruct((N_CHIPS, FLAT_ROWS, FLAT_W), F32), grid=(N_CHIPS, sg["L"], sg["nc"], sg["ni"]),
        in_specs=in_specs, out_specs=pl.BlockSpec((None, sg["rt"], FLAT_W), lambda k, l, j, i: (k, rb(l, j, i), 0)),
        input_output_aliases={0: 0} if flat4 is not None else {},
        compiler_params=_cparams(("parallel",) * 4), name=f"packgrad_{name}",
    )(*args)


def _adamw_shard(w, m, v, g_mine, g_other, c_arr, name):
    sg = _seg(name)
    rt = sg["rt"]
    rb = _flat_block(sg)
    per_half = HALF_ROWS // rt

    def half_of(l, j, i):
        return (rb(l, j, i) * rt) // HALF_ROWS

    def body(c_ref, w_ref, m_ref, v_ref, gm_ref, go_ref, g_ref, d_ref, mo_ref, vo_ref):
        is_mine = half_of(pl.program_id(0), pl.program_id(1), pl.program_id(2)) == c_ref[0]
        gv = jnp.where(is_mine, gm_ref[...], go_ref[...])
        g_ref[...] = gv
        d_ref[...], mo_ref[...], vo_ref[...] = _adam_update(w_ref[...], gv, m_ref[...], v_ref[...])

    nat = pl.BlockSpec((None, rt, FLAT_W), lambda l, j, i, c: (l, i, j))
    half = pl.BlockSpec((rt, FLAT_W), lambda l, j, i, c: (rb(l, j, i) - half_of(l, j, i) * per_half, 0))
    sds = jax.ShapeDtypeStruct(w.shape, F32)
    return pl.pallas_call(
        body, out_shape=[sds] * 4,
        grid_spec=pltpu.PrefetchScalarGridSpec(num_scalar_prefetch=1, grid=(sg["L"], sg["nc"], sg["ni"]),
                                               in_specs=[nat, nat, nat, half, half], out_specs=[nat] * 4),
        compiler_params=_cparams(("parallel", "parallel", "parallel")), name=f"adamw_{name}",
    )(c_arr, w, m, v, g_mine, g_other)


def _pack_small(vals, loss=None):
    parts = [vals[name].astype(F32).reshape(-1) for name, _ in _SMALL]
    if loss is not None:
        parts.append(loss.reshape(1))
    flat = jnp.concatenate(parts)
    return jnp.pad(flat, (0, SMALL_ROWS * 1024 - flat.shape[0])).reshape(SMALL_ROWS, 1024)


def _unpack_small(flat):
    flat = flat.reshape(-1)
    out = {}
    o = 0
    for name, shape in _SMALL:
        n = int(np.prod(shape))
        out[name] = flat[o:o + n].reshape(shape)
        o += n
    return out, flat[o]


_BIG = ("a_w_in", "a_w_out", "b_w_q", "b_w_out", "w_kv", "ffn_w_up", "ffn_w_down", "ffn_conv_w")
A_IN_PAD = 896


def _pack_weights(w, layout):
    flat = None
    for name in _LAYOUTS[layout]:
        t = _native3(w[name], name)
        for term in ((0, 1, 2) if name == "ffn_conv_w" else (None,)):
            flat = _slab_pack(flat, t, name, layout, term)
    return flat


def _early_weights(gathered, slots):
    a_in = _slab_unpack(gathered, slots, "a_w_in", "early")
    a_in = a_in.reshape(D, N_CHIPS, A_IN_PAD)[:, :, :772].reshape(D, N_CHIPS * 772)
    return dict(a_w_in=a_in, a_w_out=_slab_unpack(gathered, slots, "a_w_out", "early")[0])


def _late_weights(landed, slots, own):
    full = {name: _slab_unpack(landed, slots, name, "late", own) for name in _LAYOUTS["late"] if name != "ffn_conv_w"}
    sg = _seg("ffn_conv_w", "late")
    n1 = sg["nc"] * sg["rpad"]
    conv = slice(sg["off"], sg["off"] + CONV_TERMS * n1)
    conv_rows = jnp.concatenate([landed[:, conv], own[None, conv]], axis=0)
    per_chip = []
    for k in range(N_CHIPS):
        rows = lax.dynamic_index_in_dim(conv_rows, slots[k], axis=0, keepdims=False)
        terms = [_unslabs(rows[i * n1:(i + 1) * n1], 1, sg["R"], sg["C"], sg["rpad"]).astype(F32) for i in range(CONV_TERMS)]
        per_chip.append((terms[0] + terms[1]) + terms[2])
    cw = jnp.concatenate(per_chip, axis=2).reshape(2, 3, 2, FF).transpose(0, 2, 1, 3)
    return dict(b_w_q=full["b_w_q"][0], b_w_out=full["b_w_out"][0], w_kv=full["w_kv"][0], ffn_w_up=full["ffn_w_up"],
                ffn_w_down=full["ffn_w_down"], conv_w=cw)


def _shard_grads(g):
    a_in = jnp.pad(g["a_w_in"].reshape(D, N_CHIPS, 772), ((0, 0), (0, 0), (0, A_IN_PAD - 772)))
    sgc = _seg("ffn_conv_w")
    full = {"a_w_in": a_in.reshape(1, D, N_CHIPS * A_IN_PAD), "a_w_out": g["a_w_out"][None], "b_w_q": g["b_w_q"][None],
            "b_w_out": g["b_w_out"][None], "w_kv": g["w_kv"][None], "ffn_w_up": g["ffn_w_up"],
            "ffn_w_down": g["ffn_w_down"],
            "ffn_conv_w": jnp.pad(g["ffn_conv_w"].reshape(1, sgc["R"], 2 * FF), ((0, 0), (0, sgc["rpad"] - sgc["R"]), (0, 0)))}
    flat4 = None
    for name in _BIG:
        flat4 = _slab_pack_grad(flat4, full[name], name)
    return flat4


_WEIGHTS = ["a_w_in", "a_b_f", "a_w_out", "b_w_q", "b_w_out", "kv_norm_g", "w_kv", "mix_norm_g", "ffn_norm_g", "ffn_w_up",
            "ffn_conv_w", "ffn_conv_b", "ffn_w_down", "final_norm_g"]


def kernel(x, a_w_in, a_b_f, a_w_out, b_w_q, b_w_out, kv_norm_g, w_kv, mix_norm_g, ffn_norm_g, ffn_w_up, ffn_conv_w, ffn_conv_b, ffn_w_down, final_norm_g, loss_target, m_a_w_in, m_a_b_f, m_a_w_out, m_b_w_q, m_b_w_out, m_kv_norm_g, m_w_kv, m_mix_norm_g, m_ffn_norm_g, m_ffn_w_up, m_ffn_conv_w, m_ffn_conv_b, m_ffn_w_down, m_final_norm_g, v_a_w_in, v_a_b_f, v_a_w_out, v_b_w_q, v_b_w_out, v_kv_norm_g, v_w_kv, v_mix_norm_g, v_ffn_norm_g, v_ffn_w_up, v_ffn_conv_w, v_ffn_conv_b, v_ffn_w_down, v_final_norm_g):
    w = dict(a_w_in=a_w_in, a_b_f=a_b_f, a_w_out=a_w_out, b_w_q=b_w_q, b_w_out=b_w_out, kv_norm_g=kv_norm_g, w_kv=w_kv,
             mix_norm_g=mix_norm_g, ffn_norm_g=ffn_norm_g, ffn_w_up=ffn_w_up, ffn_conv_w=ffn_conv_w, ffn_conv_b=ffn_conv_b,
             ffn_w_down=ffn_w_down, final_norm_g=final_norm_g)
    m = dict(a_w_in=m_a_w_in, a_b_f=m_a_b_f, a_w_out=m_a_w_out, b_w_q=m_b_w_q, b_w_out=m_b_w_out, kv_norm_g=m_kv_norm_g,
             w_kv=m_w_kv, mix_norm_g=m_mix_norm_g, ffn_norm_g=m_ffn_norm_g, ffn_w_up=m_ffn_w_up, ffn_conv_w=m_ffn_conv_w,
             ffn_conv_b=m_ffn_conv_b, ffn_w_down=m_ffn_w_down, final_norm_g=m_final_norm_g)
    v = dict(a_w_in=v_a_w_in, a_b_f=v_a_b_f, a_w_out=v_a_w_out, b_w_q=v_b_w_q, b_w_out=v_b_w_out, kv_norm_g=v_kv_norm_g,
             w_kv=v_w_kv, mix_norm_g=v_mix_norm_g, ffn_norm_g=v_ffn_norm_g, ffn_w_up=v_ffn_w_up, ffn_conv_w=v_ffn_conv_w,
             ffn_conv_b=v_ffn_conv_b, ffn_w_down=v_ffn_w_down, final_norm_g=v_final_norm_g)

    c_arr = lax.axis_index("c").astype(jnp.int32).reshape(1)
    k_arr = (2 * lax.axis_index("x") + lax.axis_index("y")).astype(jnp.int32).reshape(1)
    xi, yi = lax.axis_index("x"), lax.axis_index("y")
    late_slots = jnp.stack([jnp.where(k == k_arr[0], OWN_SLOT, 2 * ((k & 1) ^ yi) + ((k >> 1) ^ xi) - 1)
                            for k in range(N_CHIPS)]).astype(jnp.int32)
    w_late = _pack_weights(w, "late")
    land = lax.empty((OWN_SLOT,) + w_late.shape, BF16)
    send_sems, recv_sems, w_thru, land_thru, token = _late_gather_start(w_late, land)
    w_early = _pack_weights(w, "early")
    early = _allgather_shards(w_early, _place_own(w_early, k_arr, "early_place_own"))
    p = _early_weights(early, jnp.arange(N_CHIPS, dtype=jnp.int32))
    cb = ffn_conv_b.reshape(2, 2, 1, FF)
    p.update(a_b_f=a_b_f, kv_norm_g=kv_norm_g, mix_norm_g=mix_norm_g + token[0, 0], ffn_norm_g=ffn_norm_g,
             final_norm_g=final_norm_g, conv_b=cb)

    def late_weights(after):
        own, landed = _late_gather_wait(send_sems, recv_sems, w_thru, land_thru, after)
        return _late_weights(landed, late_slots, own)

    loss_part, grad_x, g = _local_step(x[0], loss_target[0], p, late_weights)

    gflat = _shard_grads(g)
    pair = _pair_sum(gflat, _sibling_swap_half(gflat), c_arr)
    g_mine = _chip_sum(pair, _chip_exchange(pair), k_arr)
    g_other = _sibling_send(g_mine)
    small, loss = _unpack_small(_allreduce_small(_pack_small(g, loss_part)))

    big = [{}, {}, {}, {}]
    for name in _BIG:
        sg = _seg(name)
        res = _adamw_shard(_native3(w[name], name), _native3(m[name], name), _native3(v[name], name), g_mine, g_other,
                           c_arr, name)
        for store, t in zip(big, res):
            store[name] = t[:, :sg["R"], :].reshape(_SHARD_SHAPES[name])
    dws, mns, vns = _adamw(_pack_small(w), _pack_small(small), _pack_small(m), _pack_small(v), "adamw_small")
    sml = [small] + [_unpack_small(t)[0] for t in (dws, mns, vns)]
    outs = [loss, grad_x[None]]
    for b, s in zip(big, sml):
        outs += [b[n] if n in b else s[n] for n in _WEIGHTS]
    return tuple(outs)
```

```python
import functools
import math

import numpy as np
import jax
import jax.numpy as jnp
from jax import lax
from jax.experimental import pallas as pl
from jax.experimental.pallas import tpu as pltpu

F32 = jnp.float32
BF16 = jnp.bfloat16
MESH = pl.DeviceIdType.MESH

S = 4096
D = 1024
A_HEADS = 16
HEAD_DIM = 64
QKV_W = 3 * A_HEADS * HEAD_DIM
B_GROUPS = ((128, 1), (512, 4), (2048, 16))
B_HPG = 8
B_Q_W = 3 * B_HPG * HEAD_DIM
B_OUT_W = B_HPG * HEAD_DIM
B_KV_W = 2 * B_Q_W
B_WIN = 128
FF = 2816
RMS_EPS = 1e-6
SCALE = HEAD_DIM ** -0.5
N_CHIPS = 4

ADAM_LR, ADAM_B1, ADAM_B2, ADAM_EPS, ADAM_WD, ADAM_STEP = 0.001, 0.9, 0.999, 1e-08, 0.01, 10

V7X_VMEM_LIMIT = 48 * 1024 * 1024
LANES = 128
NEG_INF = float("-inf")

FLAT_W = LANES
_SEGS = (("ffn_w_down", 2, 704, 1024, 704), ("a_w_in", 1, 1024, 772, 1024), ("a_w_out", 1, 256, 1024, 256),
         ("b_w_q", 1, 1024, 384, 1024), ("b_w_out", 1, 512, 256, 512), ("w_kv", 1, 1024, 768, 1024),
         ("ffn_w_up", 2, 1024, 1408, 1024), ("ffn_conv_w", 1, 6, 1408, 16))
CONV_TERMS = 3


def _seg_rows(name, L, R, C, rpad):
    return (CONV_TERMS if name == "ffn_conv_w" else 1) * L * (-(-C // FLAT_W)) * rpad


SMALL_ROWS = 24


def _cparams(sem=None, **kw):
    return pltpu.CompilerParams(dimension_semantics=sem, vmem_limit_bytes=V7X_VMEM_LIMIT, **kw)


_DN = {"nn": (((1,), (0,)), ((), ())), "nt": (((1,), (1,)), ((), ())), "tn": (((0,), (0,)), ((), ()))}


def _mm(a, b, *, mode, tm, tn, tk, name, out_dtype=F32, res=None, a_split=0, b_split=0, o_split=0):
    if mode == "tn":
        K = a.shape[0]
        M = a.shape[1]
    else:
        M = a.shape[-2]
        K = a.shape[-1] * (2 if a_split else 1)
    if mode == "nt":
        N = b.shape[0]
    else:
        N = b.shape[-1] * (2 if b_split else 1)
    assert M % tm == 0 and N % tn == 0 and K % tk == 0, (name, M, N, K, tm, tn, tk)
    nk = K // tk

    if mode == "tn":
        a_spec = pl.BlockSpec((tk, tm), lambda i, j, k: (k, i))
    elif a_split:
        a_spec = pl.BlockSpec((None, tm, tk), lambda i, j, k: (k // a_split, i, k % a_split))
    else:
        a_spec = pl.BlockSpec((tm, tk), lambda i, j, k: (i, k))
    if mode == "nt":
        b_spec = pl.BlockSpec((tn, tk), lambda i, j, k: (j, k))
    elif b_split:
        b_spec = pl.BlockSpec((None, tk, tn), lambda i, j, k: (j // b_split, k, j % b_split))
    else:
        b_spec = pl.BlockSpec((tk, tn), lambda i, j, k: (k, j))
    if o_split:
        o_spec = pl.BlockSpec((None, tm, tn), lambda i, j, k: (j // o_split, i, j % o_split))
        out_shape = jax.ShapeDtypeStruct((2, M, N // 2), out_dtype)
    else:
        o_spec = pl.BlockSpec((tm, tn), lambda i, j, k: (i, j))
        out_shape = jax.ShapeDtypeStruct((M, N), out_dtype)
    in_specs = [a_spec, b_spec]
    args = [a, b]
    if res is not None:
        in_specs.append(pl.BlockSpec((tm, tn), lambda i, j, k: (i, j)))
        args.append(res)

    def body(*refs):
        if res is not None:
            a_ref, b_ref, r_ref, o_ref = refs[:4]
        else:
            a_ref, b_ref, o_ref = refs[:3]
            r_ref = None
        p = lax.dot_general(a_ref[...].astype(BF16), b_ref[...].astype(BF16), _DN[mode], preferred_element_type=F32)

        def finish(r):
            if r_ref is not None:
                r = r + r_ref[...]
            o_ref[...] = r.astype(out_dtype)

        if nk == 1:
            finish(p)
        else:
            acc = refs[-1]
            k = pl.program_id(2)

            @pl.when(k == 0)
            def _():
                acc[...] = p

            @pl.when(k > 0)
            def _():
                acc[...] += p

            @pl.when(k == nk - 1)
            def _():
                finish(acc[...])

    return pl.pallas_call(
        body, out_shape=out_shape, grid=(M // tm, N // tn, nk), in_specs=in_specs, out_specs=o_spec,
        scratch_shapes=[pltpu.VMEM((tm, tn), F32)] if nk > 1 else [],
        compiler_params=_cparams(("parallel", "parallel", "arbitrary")), name=name,
    )(*args)


NORM_ROWS = 256


def _rms_fwd(x, g, name):
    def body(x_ref, g_ref, o_ref):
        xv = x_ref[...]
        r = lax.rsqrt(jnp.mean(xv * xv, axis=-1, keepdims=True) + RMS_EPS)
        o_ref[...] = (xv * r * g_ref[...]).astype(BF16)

    row = pl.BlockSpec((NORM_ROWS, D), lambda i: (i, 0))
    return pl.pallas_call(
        body, out_shape=jax.ShapeDtypeStruct((S, D), BF16), grid=(S // NORM_ROWS,),
        in_specs=[row, pl.BlockSpec((1, D), lambda i: (0, 0))], out_specs=row,
        compiler_params=_cparams(("parallel",)), name=name,
    )(x, g.reshape(1, D))


def _rms_bwd(x, dres, pairs, name):
    n = len(pairs)

    def body(*refs):
        x_ref, dres_ref = refs[0], refs[1]
        g_refs = refs[2:2 + 2 * n:2]
        dh_refs = refs[3:3 + 2 * n:2]
        dx_ref, dxb_ref = refs[2 + 2 * n], refs[3 + 2 * n]
        dg_refs = refs[4 + 2 * n:]
        i = pl.program_id(0)
        xv = x_ref[...]
        r = lax.rsqrt(jnp.mean(xv * xv, axis=-1, keepdims=True) + RMS_EPS)
        y = xv * r
        dx = dres_ref[...]
        for g_ref, dh_ref, dg_ref in zip(g_refs, dh_refs, dg_refs):
            dh = dh_ref[...]
            dy = dh * g_ref[...]
            dx = dx + r * (dy - y * jnp.mean(dy * y, axis=-1, keepdims=True))
            part = jnp.sum(dh * y, axis=0, keepdims=True)

            @pl.when(i == 0)
            def _():
                dg_ref[...] = part

            @pl.when(i > 0)
            def _():
                dg_ref[...] += part

        dx_ref[...] = dx
        dxb_ref[...] = dx.astype(BF16)

    row = pl.BlockSpec((NORM_ROWS, D), lambda i: (i, 0))
    vec = pl.BlockSpec((1, D), lambda i: (0, 0))
    in_specs = [row, row]
    args = [x, dres]
    for g, dh in pairs:
        in_specs += [vec, row]
        args += [g.reshape(1, D), dh]
    outs = pl.pallas_call(
        body,
        out_shape=[jax.ShapeDtypeStruct((S, D), F32), jax.ShapeDtypeStruct((S, D), BF16)]
        + [jax.ShapeDtypeStruct((1, D), F32)] * n,
        grid=(S // NORM_ROWS,), in_specs=in_specs, out_specs=[row, row] + [vec] * n,
        compiler_params=_cparams(("arbitrary",)), name=name,
    )(*args)
    return outs[0], outs[1], list(outs[2:])


def _loss_head(x, g, target, name):
    def body(x_ref, g_ref, t_ref, loss_ref, dx_ref, dxb_ref, dg_ref):
        i = pl.program_id(0)
        xv = x_ref[...]
        gv = g_ref[...]
        r = lax.rsqrt(jnp.mean(xv * xv, axis=-1, keepdims=True) + RMS_EPS)
        y = xv * r
        err = y * gv - t_ref[...]
        lpart = jnp.broadcast_to(jnp.sum(err * err, keepdims=True) * (0.5 / D), (1, LANES))
        dh = err * (1.0 / D)
        dy = dh * gv
        dx = r * (dy - y * jnp.mean(dy * y, axis=-1, keepdims=True))
        part = jnp.sum(dh * y, axis=0, keepdims=True)

        @pl.when(i == 0)
        def _():
            dg_ref[...] = part
            loss_ref[...] = lpart

        @pl.when(i > 0)
        def _():
            dg_ref[...] += part
            loss_ref[...] += lpart

        dx_ref[...] = dx
        dxb_ref[...] = dx.astype(BF16)

    row = pl.BlockSpec((NORM_ROWS, D), lambda i: (i, 0))
    vec = pl.BlockSpec((1, D), lambda i: (0, 0))
    return pl.pallas_call(
        body,
        out_shape=[jax.ShapeDtypeStruct((1, LANES), F32), jax.ShapeDtypeStruct((S, D), F32),
                   jax.ShapeDtypeStruct((S, D), BF16), jax.ShapeDtypeStruct((1, D), F32)],
        grid=(S // NORM_ROWS,), in_specs=[row, vec, row],
        out_specs=[pl.BlockSpec((1, LANES), lambda i: (0, 0)), row, row, vec],
        compiler_params=_cparams(("arbitrary",)), name=name,
    )(x, g.reshape(1, D), target)


SCAN_ROWS = 256


def _split3(v):
    hi = v.astype(BF16)
    r1 = v - hi.astype(F32)
    mid = r1.astype(BF16)
    lo = (r1 - mid.astype(F32)).astype(BF16)
    return hi, mid, lo


def _tri_dot(tri, v):
    hi, mid, lo = _split3(v)
    dn = _DN["nn"]
    return (lax.dot_general(tri, hi, dn, preferred_element_type=F32)
            + lax.dot_general(tri, mid, dn, preferred_element_type=F32)
            + lax.dot_general(tri, lo, dn, preferred_element_type=F32))


def _log_sigmoid(z):
    return jnp.minimum(z, 0.0) - jnp.log(1.0 + jnp.exp(-jnp.abs(z)))


def _fgate_fwd(pf, bias, name):
    tri = jnp.tril(jnp.ones((SCAN_ROWS, SCAN_ROWS), F32)).astype(BF16)

    def body(pf_ref, b_ref, tri_ref, c_ref):
        carry = jnp.zeros((1, LANES), F32)
        for blk in range(S // SCAN_ROWS):
            rows = pl.ds(blk * SCAN_ROWS, SCAN_ROWS)
            lf = _log_sigmoid(pf_ref[rows, :] + b_ref[...])
            c_ref[rows, :] = _tri_dot(tri_ref[...], lf) + carry
            carry = c_ref[pl.ds(blk * SCAN_ROWS + SCAN_ROWS - 1, 1), :]

    return pl.pallas_call(
        body, out_shape=jax.ShapeDtypeStruct((S, LANES), F32),
        compiler_params=_cparams(), name=name,
    )(pf, bias, tri)


def _fgate_bwd(pf, bias, dc_key, dc_query, name):
    triu = jnp.triu(jnp.ones((SCAN_ROWS, SCAN_ROWS), F32)).astype(BF16)

    def body(pf_ref, b_ref, dck_ref, dcq_ref, tri_ref, dpf_ref, db_ref, dlf_ref):
        carry = jnp.zeros((1, LANES), F32)
        db = jnp.zeros((1, LANES), F32)
        lane = lax.broadcasted_iota(jnp.int32, (SCAN_ROWS, LANES), 1)
        for blk in reversed(range(S // SCAN_ROWS)):
            rows = pl.ds(blk * SCAN_ROWS, SCAN_ROWS)
            dc = dck_ref[rows, :] + dcq_ref[rows, :]
            dlf_ref[rows, :] = _tri_dot(tri_ref[...], dc) + carry
            carry = dlf_ref[pl.ds(blk * SCAN_ROWS, 1), :]
            z = pf_ref[rows, :] + b_ref[...]
            e = jnp.exp(-jnp.abs(z))
            sig_neg = jnp.where(z >= 0.0, e, 1.0) / (1.0 + e)
            dz = jnp.where(lane < A_HEADS, dlf_ref[rows, :] * sig_neg, 0.0)
            dpf_ref[rows, :] = dz.astype(BF16)
            db = db + jnp.sum(dz, axis=0, keepdims=True)
        db_ref[...] = db

    return pl.pallas_call(
        body, out_shape=[jax.ShapeDtypeStruct((S, LANES), BF16), jax.ShapeDtypeStruct((1, LANES), F32)],
        scratch_shapes=[pltpu.VMEM((S, LANES), F32)],
        compiler_params=_cparams(), name=name,
    )(pf, bias, dc_key, dc_query, triu)


FOX_T = 512


def _fox_fwd(q, k, v, ccol, crow, name):
    H = q.shape[0]
    T = FOX_T
    nq = S // T

    def body(q_ref, k_ref, v_ref, cc_ref, cr_ref, o_ref, lse_ref, m_sc, l_sc, acc_sc):
        i = pl.program_id(1)
        j = pl.program_id(2)

        @pl.when(j == 0)
        def _():
            m_sc[...] = jnp.full((T, 1), NEG_INF, F32)
            l_sc[...] = jnp.zeros((T, 1), F32)
            acc_sc[...] = jnp.zeros((T, HEAD_DIM), F32)

        def step(diagonal):
            s = lax.dot_general(q_ref[...], k_ref[...], _DN["nt"], preferred_element_type=F32) * SCALE
            s = s + (cc_ref[...] - cr_ref[...])
            if diagonal:
                row = lax.broadcasted_iota(jnp.int32, (T, T), 0)
                col = lax.broadcasted_iota(jnp.int32, (T, T), 1)
                s = jnp.where(row >= col, s, NEG_INF)
            m_prev = m_sc[...]
            m_new = jnp.maximum(m_prev, jnp.max(s, axis=1, keepdims=True))
            alpha = jnp.exp(m_prev - m_new)
            p = jnp.exp(s - m_new)
            l_sc[...] = alpha * l_sc[...] + jnp.sum(p, axis=1, keepdims=True)
            acc_sc[...] = alpha * acc_sc[...] + lax.dot_general(p.astype(BF16), v_ref[...], _DN["nn"],
                                                                preferred_element_type=F32)
            m_sc[...] = m_new

        @pl.when(j < i)
        def _():
            step(False)

        @pl.when(j == i)
        def _():
            step(True)
            o_ref[...] = (acc_sc[...] / l_sc[...]).astype(BF16)
            lse_ref[...] = m_sc[...] + jnp.log(l_sc[...])

    qs = pl.BlockSpec((None, T, HEAD_DIM), lambda h, i, j: (h, i, 0))
    ks = pl.BlockSpec((None, T, HEAD_DIM), lambda h, i, j: (h, jnp.minimum(i, j), 0))
    col = pl.BlockSpec((None, T, 1), lambda h, i, j: (h, i, 0))
    rowk = pl.BlockSpec((None, 1, T), lambda h, i, j: (h, 0, jnp.minimum(i, j)))
    return pl.pallas_call(
        body, out_shape=[jax.ShapeDtypeStruct((H, S, HEAD_DIM), BF16), jax.ShapeDtypeStruct((H, S, 1), F32)],
        grid=(H, nq, nq), in_specs=[qs, ks, ks, col, rowk], out_specs=[qs, col],
        scratch_shapes=[pltpu.VMEM((T, 1), F32), pltpu.VMEM((T, 1), F32), pltpu.VMEM((T, HEAD_DIM), F32)],
        compiler_params=_cparams(("parallel", "parallel", "arbitrary")), name=name,
    )(q, k, v, ccol, crow)


def _fox_bwd(q, k, v, do, lse_row, delta_row, cq_row, ck_col, name):
    H = q.shape[0]
    T = FOX_T
    nq = S // T

    def body(q_ref, k_ref, v_ref, do_ref, lse_ref, dl_ref, cq_ref, ck_ref, dq_ref, dk_ref, dv_ref, dc_ref, dcq_ref,
             dk_sc, dv_sc, dc_sc):
        j = pl.program_id(1)
        i = pl.program_id(2)

        @pl.when(jnp.logical_and(j == 0, i == 0))
        def _():
            dq_ref[...] = jnp.zeros((S, HEAD_DIM), F32)
            dcq_ref[...] = jnp.zeros((nq, 1, T), F32)

        @pl.when(i == j)
        def _():
            dk_sc[...] = jnp.zeros((T, HEAD_DIM), F32)
            dv_sc[...] = jnp.zeros((T, HEAD_DIM), F32)
            dc_sc[...] = jnp.zeros((T, 1), F32)

        def step(diagonal):
            qv = q_ref[...]
            kv = k_ref[...]
            dov = do_ref[...]
            st = lax.dot_general(kv, qv, _DN["nt"], preferred_element_type=F32) * SCALE
            st = st + (cq_ref[...] - ck_ref[...])
            if diagonal:
                row = lax.broadcasted_iota(jnp.int32, (T, T), 0)
                col = lax.broadcasted_iota(jnp.int32, (T, T), 1)
                st = jnp.where(col >= row, st, NEG_INF)
            pt = jnp.exp(st - lse_ref[...])
            dv_sc[...] += lax.dot_general(pt.astype(BF16), dov, _DN["nn"], preferred_element_type=F32)
            dpt = lax.dot_general(v_ref[...], dov, _DN["nt"], preferred_element_type=F32)
            dst = pt * (dpt - dl_ref[...])
            dc_sc[...] -= jnp.sum(dst, axis=1, keepdims=True)
            dcq_ref[i] += jnp.sum(dst, axis=0, keepdims=True)
            dsb = (dst * SCALE).astype(BF16)
            dk_sc[...] += lax.dot_general(dsb, qv, _DN["nn"], preferred_element_type=F32)
            rows = pl.ds(pl.multiple_of(i * T, T), T)
            dq_ref[rows, :] += lax.dot_general(dsb, kv, _DN["tn"], preferred_element_type=F32)

        @pl.when(i > j)
        def _():
            step(False)

        @pl.when(i == j)
        def _():
            step(True)

        @pl.when(i == nq - 1)
        def _():
            dk_ref[...] = dk_sc[...].astype(BF16)
            dv_ref[...] = dv_sc[...].astype(BF16)
            dc_ref[...] = dc_sc[...]

    qs = pl.BlockSpec((None, T, HEAD_DIM), lambda h, j, i: (h, jnp.maximum(i, j), 0))
    qrow = pl.BlockSpec((None, 1, T), lambda h, j, i: (h, 0, jnp.maximum(i, j)))
    ks = pl.BlockSpec((None, T, HEAD_DIM), lambda h, j, i: (h, j, 0))
    kcol = pl.BlockSpec((None, T, 1), lambda h, j, i: (h, j, 0))
    dqs = pl.BlockSpec((None, S, HEAD_DIM), lambda h, j, i: (h, 0, 0))
    dcqs = pl.BlockSpec((None, nq, 1, T), lambda h, j, i: (h, 0, 0, 0))
    return pl.pallas_call(
        body,
        out_shape=[jax.ShapeDtypeStruct((H, S, HEAD_DIM), F32), jax.ShapeDtypeStruct((H, S, HEAD_DIM), BF16),
                   jax.ShapeDtypeStruct((H, S, HEAD_DIM), BF16), jax.ShapeDtypeStruct((H, S, 1), F32),
                   jax.ShapeDtypeStruct((H, nq, 1, T), F32)],
        grid=(H, nq, nq), in_specs=[qs, ks, ks, qs, qrow, qrow, qrow, kcol], out_specs=[dqs, ks, ks, kcol, dcqs],
        scratch_shapes=[pltpu.VMEM((T, HEAD_DIM), F32), pltpu.VMEM((T, HEAD_DIM), F32), pltpu.VMEM((T, 1), F32)],
        compiler_params=_cparams(("parallel", "arbitrary", "arbitrary")), name=name,
    )(q, k, v, do, lse_row, delta_row, cq_row, ck_col)


def _rowdot(a, b, name):
    H = a.shape[0]
    T = 1024

    def body(a_ref, b_ref, o_ref):
        o_ref[...] = jnp.sum(a_ref[...].astype(F32) * b_ref[...].astype(F32), axis=-1, keepdims=True)

    blk = pl.BlockSpec((None, T, HEAD_DIM), lambda h, i: (h, i, 0))
    return pl.pallas_call(
        body, out_shape=jax.ShapeDtypeStruct((H, S, 1), F32), grid=(H, S // T), in_specs=[blk, blk],
        out_specs=pl.BlockSpec((None, T, 1), lambda h, i: (h, i, 0)),
        compiler_params=_cparams(("parallel", "parallel")), name=name,
    )(a, b)


def _first_head(shape):
    return lax.broadcasted_iota(jnp.int32, shape, len(shape) - 1) < HEAD_DIM


def _each_head(x, lo):
    zero = jnp.zeros_like(x)
    return jnp.where(lo, x, zero), jnp.where(lo, zero, x)


GATE_LANES = 6


def _gate_lanes(cum):
    to_bf16_grid = lambda t: lax.reduce_precision(t, exponent_bits=8, mantissa_bits=7)
    c = cum[:, :A_HEADS]
    hi = to_bf16_grid(c)
    mid = to_bf16_grid(c - hi)
    lo = to_bf16_grid((c - hi) - mid)
    one = jnp.ones_like(hi)
    def place(cols):
        t = jnp.stack(cols, axis=-1)
        t = jnp.pad(t, ((0, 0), (0, 0), (0, HEAD_DIM - GATE_LANES)))
        return t.reshape(S, A_HEADS // 2, 2, HEAD_DIM)[:, :, ::-1, :].reshape(S, A_HEADS * HEAD_DIM).astype(BF16)
    return place([hi, mid, lo, one, one, one]), place([one, one, one, -hi, -mid, -lo])


def _fox_pair_fwd(qkv, aug_q, aug_k, name):
    T = FOX_T
    nq = S // T
    NP = A_HEADS // 2

    def body(q_ref, k_ref, v_ref, aq_ref, ak_ref, o_ref, lse_ref, m_sc, l_sc, acc_sc):
        i = pl.program_id(1)
        j = pl.program_id(2)
        lo = _first_head((T, LANES))

        @pl.when(j == 0)
        def _():
            m_sc[...] = jnp.full((2, T, LANES), NEG_INF, F32)
            l_sc[...] = jnp.zeros((2, T, LANES), F32)
            acc_sc[...] = jnp.zeros((T, LANES), F32)

        def step(diagonal):
            qs = q_ref[...] * jnp.asarray(SCALE, BF16)
            aq, ak, kv = aq_ref[...], ak_ref[...], k_ref[...]
            q2 = (jnp.where(lo, qs, aq), jnp.where(lo, aq, qs))
            k2 = (jnp.where(lo, kv, ak), jnp.where(lo, ak, kv))
            if diagonal:
                causal = lax.broadcasted_iota(jnp.int32, (T, T), 0) >= lax.broadcasted_iota(jnp.int32, (T, T), 1)
            pv, alphas = None, []
            for h, vh in enumerate(_each_head(v_ref[...], lo)):
                s = lax.dot_general(q2[h], k2[h], _DN["nt"], preferred_element_type=F32)
                if diagonal:
                    s = jnp.where(causal, s, NEG_INF)
                m_prev = m_sc[h]
                m_new = jnp.maximum(m_prev, jnp.max(s, axis=1, keepdims=True))
                alpha = jnp.exp(m_prev - m_new)
                p = jnp.exp(s - jnp.tile(m_new, (1, T // LANES)))
                l_sc[h] = alpha * l_sc[h] + jnp.sum(p, axis=1, keepdims=True)
                m_sc[h] = m_new
                d = lax.dot_general(p.astype(BF16), vh, _DN["nn"], preferred_element_type=F32)
                pv = d if pv is None else pv + d
                alphas.append(alpha)
            acc_sc[...] = jnp.where(lo, alphas[0], alphas[1]) * acc_sc[...] + pv

        @pl.when(j < i)
        def _():
            step(False)

        @pl.when(j == i)
        def _():
            step(True)
            o_ref[...] = (acc_sc[...] * jnp.where(lo, 1.0 / l_sc[0], 1.0 / l_sc[1])).astype(BF16)
            for h in range(2):
                lse_ref[h] = (m_sc[h] + jnp.log(l_sc[h]))[:, 0:1]

    qs_ = pl.BlockSpec((T, LANES), lambda p, i, j: (i, p))
    ks = pl.BlockSpec((T, LANES), lambda p, i, j: (jnp.minimum(i, j), NP + p))
    vs = pl.BlockSpec((T, LANES), lambda p, i, j: (jnp.minimum(i, j), 2 * NP + p))
    aks = pl.BlockSpec((T, LANES), lambda p, i, j: (jnp.minimum(i, j), p))
    col = pl.BlockSpec((2, T, 1), lambda p, i, j: (p, i, 0))
    return pl.pallas_call(
        body, out_shape=[jax.ShapeDtypeStruct((S, A_HEADS * HEAD_DIM), BF16), jax.ShapeDtypeStruct((A_HEADS, S, 1), F32)],
        grid=(NP, nq, nq), in_specs=[qs_, ks, vs, qs_, aks], out_specs=[qs_, col],
        scratch_shapes=[pltpu.VMEM((2, T, LANES), F32), pltpu.VMEM((2, T, LANES), F32), pltpu.VMEM((T, LANES), F32)],
        compiler_params=_cparams(("parallel", "parallel", "arbitrary")), name=name,
    )(qkv, qkv, qkv, aug_q, aug_k)


def _fox_pair_bwd(qkv, do, lse_row, delta_row, aug_q, aug_k, name):
    T = FOX_T
    nq = S // T
    NP = A_HEADS // 2

    def body(q_ref, k_ref, v_ref, do_ref, lse_ref, dl_ref, aq_ref, ak_ref, dq_ref, dk_ref, dv_ref, dc_ref, dcq_ref,
             dq_sc, dk_sc, dv_sc, dc_sc):
        j = pl.program_id(1)
        i = pl.program_id(2)
        lo = _first_head((T, LANES))

        @pl.when(jnp.logical_and(j == 0, i == 0))
        def _():
            dq_sc[...] = jnp.zeros((S, LANES), F32)
            dcq_ref[...] = jnp.zeros((2, nq, 1, T), F32)

        @pl.when(i == j)
        def _():
            dk_sc[...] = jnp.zeros((T, LANES), F32)
            dv_sc[...] = jnp.zeros((T, LANES), F32)
            dc_sc[...] = jnp.zeros((2, T, 1), F32)

        def step(diagonal):
            qv = q_ref[...]
            kv = k_ref[...]
            dov = do_ref[...].astype(BF16)
            qs = qv * jnp.asarray(SCALE, BF16)
            aq, ak = aq_ref[...], ak_ref[...]
            q2 = (jnp.where(lo, qs, aq), jnp.where(lo, aq, qs))
            k2 = (jnp.where(lo, kv, ak), jnp.where(lo, ak, kv))
            if diagonal:
                causal = lax.broadcasted_iota(jnp.int32, (T, T), 1) >= lax.broadcasted_iota(jnp.int32, (T, T), 0)
            dv = dk = dq = None
            for h, (kh, vh, qh, doh) in enumerate(zip(_each_head(kv, lo), _each_head(v_ref[...], lo),
                                                      _each_head(qv, lo), _each_head(dov, lo))):
                st = lax.dot_general(k2[h], q2[h], _DN["nt"], preferred_element_type=F32)
                if diagonal:
                    st = jnp.where(causal, st, NEG_INF)
                pt = jnp.exp(st - lse_ref[h])
                d = lax.dot_general(pt.astype(BF16), doh, _DN["nn"], preferred_element_type=F32)
                dv = d if dv is None else dv + d
                dpt = lax.dot_general(vh, dov, _DN["nt"], preferred_element_type=F32)
                dst = pt * (dpt - dl_ref[h])
                dc_sc[h] -= jnp.sum(dst, axis=1, keepdims=True)
                dcq_ref[h, i] += jnp.sum(dst, axis=0, keepdims=True)
                dsb = (dst * SCALE).astype(BF16)
                d = lax.dot_general(dsb, qh, _DN["nn"], preferred_element_type=F32)
                dk = d if dk is None else dk + d
                d = lax.dot_general(dsb, kh, _DN["tn"], preferred_element_type=F32)
                dq = d if dq is None else dq + d
            dv_sc[...] += dv
            dk_sc[...] += dk
            rows = pl.ds(pl.multiple_of(i * T, T), T)
            dq_sc[rows, :] += dq

        @pl.when(i > j)
        def _():
            step(False)

        @pl.when(i == j)
        def _():
            step(True)

        @pl.when(i == nq - 1)
        def _():
            dk_ref[...] = dk_sc[...].astype(BF16)
            dv_ref[...] = dv_sc[...].astype(BF16)
            dc_ref[...] = dc_sc[...]

        @pl.when(jnp.logical_and(j == nq - 1, i == nq - 1))
        def _():
            dq_ref[...] = dq_sc[...].astype(BF16)

    qs = pl.BlockSpec((T, LANES), lambda p, j, i: (jnp.maximum(i, j), p))
    qrow = pl.BlockSpec((2, 1, T), lambda p, j, i: (p, 0, jnp.maximum(i, j)))
    ks = pl.BlockSpec((T, LANES), lambda p, j, i: (j, NP + p))
    vs = pl.BlockSpec((T, LANES), lambda p, j, i: (j, 2 * NP + p))
    kout = pl.BlockSpec((T, LANES), lambda p, j, i: (j, p))
    kcol = pl.BlockSpec((2, T, 1), lambda p, j, i: (p, j, 0))
    dqs = pl.BlockSpec((S, LANES), lambda p, j, i: (0, p))
    dcqs = pl.BlockSpec((2, nq, 1, T), lambda p, j, i: (p, 0, 0, 0))
    wide = jax.ShapeDtypeStruct((S, A_HEADS * HEAD_DIM), BF16)
    return pl.pallas_call(
        body,
        out_shape=[wide, wide, wide, jax.ShapeDtypeStruct((A_HEADS, S, 1), F32),
                   jax.ShapeDtypeStruct((A_HEADS, nq, 1, T), F32)],
        grid=(NP, nq, nq), in_specs=[qs, ks, vs, qs, qrow, qrow, qs, kout], out_specs=[dqs, kout, kout, kcol, dcqs],
        scratch_shapes=[pltpu.VMEM((S, LANES), F32), pltpu.VMEM((T, LANES), F32), pltpu.VMEM((T, LANES), F32),
                        pltpu.VMEM((2, T, 1), F32)],
        compiler_params=_cparams(("parallel", "arbitrary", "arbitrary")), name=name,
    )(qkv, qkv, qkv, do, lse_row, delta_row, aug_q, aug_k)


def _pair_rowdot(a, b, name):
    n = a.shape[1] // HEAD_DIM
    T = 1024

    def body(a_ref, b_ref, o_ref):
        prod = a_ref[...].astype(F32) * b_ref[...].astype(F32)
        lo = _first_head(prod.shape)
        o_ref[0] = jnp.sum(jnp.where(lo, prod, 0.0), axis=1, keepdims=True)
        o_ref[1] = jnp.sum(jnp.where(lo, 0.0, prod), axis=1, keepdims=True)

    blk = pl.BlockSpec((T, LANES), lambda p, i: (i, p))
    return pl.pallas_call(
        body, out_shape=jax.ShapeDtypeStruct((n, S, 1), F32), grid=(n // 2, S // T), in_specs=[blk, blk],
        out_specs=pl.BlockSpec((2, T, 1), lambda p, i: (p, i, 0)),
        compiler_params=_cparams(("parallel", "parallel")), name=name,
    )(a, b)


W = B_WIN
N_HG = 3 * B_HPG
N_BLK = S // W


def _dil_tables():
    slopes = np.exp2((-8.0 * np.arange(1, N_HG + 1, dtype=np.float32) / N_HG).astype(np.float32)).astype(np.float32)
    dil = np.repeat(np.array([d for _, d in B_GROUPS], np.float32), B_HPG)
    coef = (slopes * dil).astype(np.float32)
    nbs = np.repeat(np.array([S // d // W for _, d in B_GROUPS], np.int32), B_HPG)
    return jnp.asarray(coef), jnp.asarray(nbs)


DIL_SUB = 8
DIL_ROWS = DIL_SUB * W
DIL_STEPS = S // DIL_ROWS


def _dil_bias(coef, transposed):
    row = lax.broadcasted_iota(jnp.int32, (W, 2 * W), 0)
    col = lax.broadcasted_iota(jnp.int32, (W, 2 * W), 1)
    dist = (col - row) if transposed else (row + W - col)
    valid = jnp.logical_and(dist >= 0, dist <= W)
    return jnp.where(valid, -coef * dist.astype(F32), NEG_INF), col


def _dil_specs():
    blk = pl.BlockSpec((None, DIL_ROWS, HEAD_DIM), lambda h, n: (h, n, 0))
    prev = pl.BlockSpec((None, W, HEAD_DIM), lambda h, n: (h, jnp.maximum(n * DIL_SUB - 1, 0), 0))
    nxt = pl.BlockSpec((None, W, HEAD_DIM), lambda h, n: (h, jnp.minimum((n + 1) * DIL_SUB, N_BLK - 1), 0))
    col = pl.BlockSpec((None, DIL_ROWS, 1), lambda h, n: (h, n, 0))
    row = pl.BlockSpec((None, 1, DIL_ROWS), lambda h, n: (h, 0, n))
    rnxt = pl.BlockSpec((None, 1, W), lambda h, n: (h, 0, jnp.minimum((n + 1) * DIL_SUB, N_BLK - 1)))
    smem = pl.BlockSpec(memory_space=pltpu.SMEM)
    return blk, prev, nxt, col, row, rnxt, smem


def _dil_fwd(q, k, v, name):
    coef_t, nbs_t = _dil_tables()

    def body(coef_ref, nbs_ref, q_ref, kh_ref, k_ref, vh_ref, v_ref, o_ref, lse_ref, kf, vf):
        hg = pl.program_id(0)
        n = pl.program_id(1)
        nbs = nbs_ref[hg]
        kf[0:W, :] = kh_ref[...]
        kf[W:, :] = k_ref[...]
        vf[0:W, :] = vh_ref[...]
        vf[W:, :] = v_ref[...]
        bias, col = _dil_bias(coef_ref[hg], False)
        for b in range(DIL_SUB):
            first = lax.rem(n * DIL_SUB + b, nbs) == 0
            rows = slice(b * W, (b + 1) * W)
            both = slice(b * W, (b + 2) * W)
            s = lax.dot_general(q_ref[rows, :], kf[both, :], _DN["nt"], preferred_element_type=F32) * SCALE + bias
            s = jnp.where(jnp.logical_and(first, col < W), NEG_INF, s)
            m = jnp.max(s, axis=1, keepdims=True)
            p = jnp.exp(s - m)
            l = jnp.sum(p, axis=1, keepdims=True)
            acc = lax.dot_general(p.astype(BF16), vf[both, :], _DN["nn"], preferred_element_type=F32)
            o_ref[rows, :] = acc / l
            lse_ref[rows, :] = m + jnp.log(l)

    blk, prev, _, col, _, _, smem = _dil_specs()
    return pl.pallas_call(
        body, out_shape=[jax.ShapeDtypeStruct((N_HG, S, HEAD_DIM), F32), jax.ShapeDtypeStruct((N_HG, S, 1), F32)],
        grid=(N_HG, DIL_STEPS), in_specs=[smem, smem, blk, prev, blk, prev, blk], out_specs=[blk, col],
        scratch_shapes=[pltpu.VMEM((DIL_ROWS + W, HEAD_DIM), BF16)] * 2,
        compiler_params=_cparams(("parallel", "parallel")), name=name,
    )(coef_t, nbs_t, q, k, k, v, v)


def _dil_merge(o, lse, name):
    T = 1024

    def body(o_ref, lse_ref, om_ref, omb_ref, l_ref):
        l0, l1, l2 = lse_ref[0], lse_ref[1], lse_ref[2]
        m = jnp.maximum(jnp.maximum(l0, l1), l2)
        e0, e1, e2 = jnp.exp(l0 - m), jnp.exp(l1 - m), jnp.exp(l2 - m)
        den = e0 + e1 + e2
        om = (e0 / den) * o_ref[0] + (e1 / den) * o_ref[1] + (e2 / den) * o_ref[2]
        om_ref[...] = om
        omb_ref[...] = om.astype(BF16)
        l_ref[...] = m + jnp.log(den)

    ob = pl.BlockSpec((None, T, HEAD_DIM), lambda h, i: (h, i, 0))
    lb = pl.BlockSpec((None, T, 1), lambda h, i: (h, i, 0))
    return pl.pallas_call(
        body,
        out_shape=[jax.ShapeDtypeStruct((B_HPG, S, HEAD_DIM), F32), jax.ShapeDtypeStruct((B_HPG, S, HEAD_DIM), BF16),
                   jax.ShapeDtypeStruct((B_HPG, S, 1), F32)],
        grid=(B_HPG, S // T),
        in_specs=[pl.BlockSpec((3, None, T, HEAD_DIM), lambda h, i: (0, h, i, 0)),
                  pl.BlockSpec((3, None, T, 1), lambda h, i: (0, h, i, 0))],
        out_specs=[ob, ob, lb], compiler_params=_cparams(("parallel", "parallel")), name=name,
    )(o, lse)


def _dil_bwd_dq(q, k, v, do, lcol, dcol, name):
    coef_t, nbs_t = _dil_tables()

    def body(coef_ref, nbs_ref, q_ref, kh_ref, k_ref, vh_ref, v_ref, do_ref, l_ref, d_ref, dq_ref, kf, vf):
        hg = pl.program_id(0)
        n = pl.program_id(1)
        nbs = nbs_ref[hg]
        kf[0:W, :] = kh_ref[...]
        kf[W:, :] = k_ref[...]
        vf[0:W, :] = vh_ref[...]
        vf[W:, :] = v_ref[...]
        bias, col = _dil_bias(coef_ref[hg], False)
        for b in range(DIL_SUB):
            first = lax.rem(n * DIL_SUB + b, nbs) == 0
            rows = slice(b * W, (b + 1) * W)
            both = slice(b * W, (b + 2) * W)
            kk = kf[both, :]
            s = lax.dot_general(q_ref[rows, :], kk, _DN["nt"], preferred_element_type=F32) * SCALE + bias
            s = jnp.where(jnp.logical_and(first, col < W), NEG_INF, s)
            p = jnp.exp(s - l_ref[rows, :])
            dp = lax.dot_general(do_ref[rows, :], vf[both, :], _DN["nt"], preferred_element_type=F32)
            ds = (p * (dp - d_ref[rows, :]) * SCALE).astype(BF16)
            dq_ref[rows, :] = lax.dot_general(ds, kk, _DN["nn"], preferred_element_type=F32).astype(BF16)

    blk, prev, _, col, _, _, smem = _dil_specs()
    return pl.pallas_call(
        body, out_shape=jax.ShapeDtypeStruct((N_HG, S, HEAD_DIM), BF16), grid=(N_HG, DIL_STEPS),
        in_specs=[smem, smem, blk, prev, blk, prev, blk, blk, col, col], out_specs=blk,
        scratch_shapes=[pltpu.VMEM((DIL_ROWS + W, HEAD_DIM), BF16)] * 2,
        compiler_params=_cparams(("parallel", "parallel")), name=name,
    )(coef_t, nbs_t, q, k, k, v, v, do, lcol, dcol)


def _dil_bwd_dkv(q, k, v, do, lrow, drow, name):
    coef_t, nbs_t = _dil_tables()

    def body(coef_ref, nbs_ref, k_ref, v_ref, q_ref, qn_ref, do_ref, don_ref, l_ref, ln_ref, d_ref, dn_ref,
             dk_ref, dv_ref, qf, dof, lf, df):
        hg = pl.program_id(0)
        n = pl.program_id(1)
        nbs = nbs_ref[hg]
        qf[0:DIL_ROWS, :] = q_ref[...]
        qf[DIL_ROWS:, :] = qn_ref[...]
        dof[0:DIL_ROWS, :] = do_ref[...]
        dof[DIL_ROWS:, :] = don_ref[...]
        lf[:, 0:DIL_ROWS] = l_ref[...]
        lf[:, DIL_ROWS:] = ln_ref[...]
        df[:, 0:DIL_ROWS] = d_ref[...]
        df[:, DIL_ROWS:] = dn_ref[...]
        bias, col = _dil_bias(coef_ref[hg], True)
        for b in range(DIL_SUB):
            no_next = lax.rem(n * DIL_SUB + b + 1, nbs) == 0
            rows = slice(b * W, (b + 1) * W)
            both = slice(b * W, (b + 2) * W)
            qq = qf[both, :]
            dd = dof[both, :]
            st = lax.dot_general(k_ref[rows, :], qq, _DN["nt"], preferred_element_type=F32) * SCALE + bias
            st = jnp.where(jnp.logical_and(no_next, col >= W), NEG_INF, st)
            pt = jnp.exp(st - lf[:, both])
            dv_ref[rows, :] = lax.dot_general(pt.astype(BF16), dd, _DN["nn"], preferred_element_type=F32).astype(BF16)
            dpt = lax.dot_general(v_ref[rows, :], dd, _DN["nt"], preferred_element_type=F32)
            dst = (pt * (dpt - df[:, both]) * SCALE).astype(BF16)
            dk_ref[rows, :] = lax.dot_general(dst, qq, _DN["nn"], preferred_element_type=F32).astype(BF16)

    blk, _, nxt, _, row, rnxt, smem = _dil_specs()
    return pl.pallas_call(
        body, out_shape=[jax.ShapeDtypeStruct((N_HG, S, HEAD_DIM), BF16)] * 2, grid=(N_HG, DIL_STEPS),
        in_specs=[smem, smem, blk, blk, blk, nxt, blk, nxt, row, rnxt, row, rnxt], out_specs=[blk, blk],
        scratch_shapes=[pltpu.VMEM((DIL_ROWS + W, HEAD_DIM), BF16)] * 2 + [pltpu.VMEM((1, DIL_ROWS + W), F32)] * 2,
        compiler_params=_cparams(("parallel", "parallel")), name=name,
    )(coef_t, nbs_t, k, v, q, q, do, do, lrow, lrow, drow, drow)


NPG = B_HPG // 2
GROUP_W = B_HPG * HEAD_DIM


def _dil_pair_specs(qoff, koff, voff):
    prev_blk = lambda n: jnp.maximum(n * DIL_SUB - 1, 0)
    next_blk = lambda n: jnp.minimum((n + 1) * DIL_SUB, N_BLK - 1)
    return dict(
        o=pl.BlockSpec((DIL_ROWS, LANES), lambda h, n: (n, h)),
        o_next=pl.BlockSpec((W, LANES), lambda h, n: (next_blk(n), h)),
        q=pl.BlockSpec((DIL_ROWS, LANES), lambda h, n: (n, qoff + h)),
        q_next=pl.BlockSpec((W, LANES), lambda h, n: (next_blk(n), qoff + h)),
        k=pl.BlockSpec((DIL_ROWS, LANES), lambda h, n: (n, koff + h)),
        k_prev=pl.BlockSpec((W, LANES), lambda h, n: (prev_blk(n), koff + h)),
        v=pl.BlockSpec((DIL_ROWS, LANES), lambda h, n: (n, voff + h)),
        v_prev=pl.BlockSpec((W, LANES), lambda h, n: (prev_blk(n), voff + h)),
        col=pl.BlockSpec((2, DIL_ROWS, 1), lambda h, n: (h, n, 0)),
        row=pl.BlockSpec((2, 1, DIL_ROWS), lambda h, n: (h, 0, n)),
        row_next=pl.BlockSpec((2, 1, W), lambda h, n: (h, 0, next_blk(n))),
        smem=pl.BlockSpec(memory_space=pltpu.SMEM))


def _dil_pair_fwd(g, q, k, v, qoff, koff, voff, name):
    coef_t, nbs_t = _dil_tables()

    def body(coef_ref, nbs_ref, q_ref, kh_ref, k_ref, vh_ref, v_ref, o_ref, lse_ref, kf, vf):
        hp = pl.program_id(0)
        n = pl.program_id(1)
        nbs = nbs_ref[B_HPG * g + 2 * hp]
        kf[0:W, :] = kh_ref[...]
        kf[W:, :] = k_ref[...]
        vf[0:W, :] = vh_ref[...]
        vf[W:, :] = v_ref[...]
        biases = [_dil_bias(coef_ref[B_HPG * g + 2 * hp + h], False) for h in range(2)]
        col = biases[0][1]
        lo = _first_head((W, LANES))
        lo2 = _first_head((2 * W, LANES))
        for b in range(DIL_SUB):
            first = lax.rem(n * DIL_SUB + b, nbs) == 0
            rows = slice(b * W, (b + 1) * W)
            both = slice(b * W, (b + 2) * W)
            qv = q_ref[rows, :]
            acc, inv = None, []
            for h, (kh, vh) in enumerate(zip(_each_head(kf[both, :], lo2), _each_head(vf[both, :], lo2))):
                s = lax.dot_general(qv, kh, _DN["nt"], preferred_element_type=F32) * SCALE + biases[h][0]
                s = jnp.where(jnp.logical_and(first, col < W), NEG_INF, s)
                m = jnp.max(s, axis=1, keepdims=True)
                p = jnp.exp(s - m)
                l = jnp.sum(p, axis=1, keepdims=True)
                d = lax.dot_general(p.astype(BF16), vh, _DN["nn"], preferred_element_type=F32)
                acc = d if acc is None else acc + d
                inv.append(1.0 / l)
                lse_ref[h, rows, :] = m + jnp.log(l)
            o_ref[rows, :] = acc * jnp.where(lo, inv[0], inv[1])

    sp = _dil_pair_specs(qoff, koff, voff)
    return pl.pallas_call(
        body, out_shape=[jax.ShapeDtypeStruct((S, GROUP_W), F32), jax.ShapeDtypeStruct((B_HPG, S, 1), F32)],
        grid=(NPG, DIL_STEPS), in_specs=[sp["smem"], sp["smem"], sp["q"], sp["k_prev"], sp["k"], sp["v_prev"], sp["v"]],
        out_specs=[sp["o"], sp["col"]], scratch_shapes=[pltpu.VMEM((DIL_ROWS + W, LANES), BF16)] * 2,
        compiler_params=_cparams(("parallel", "parallel")), name=name,
    )(coef_t, nbs_t, q, k, k, v, v)


def _dil_pair_merge(os, lses, name):
    T = 1024

    def body(o0_ref, o1_ref, o2_ref, l0_ref, l1_ref, l2_ref, om_ref, omb_ref, l_ref):
        lo = _first_head((T, LANES))
        weights = []
        for h in range(2):
            l0, l1, l2 = l0_ref[h], l1_ref[h], l2_ref[h]
            m = jnp.maximum(jnp.maximum(l0, l1), l2)
            e0, e1, e2 = jnp.exp(l0 - m), jnp.exp(l1 - m), jnp.exp(l2 - m)
            den = e0 + e1 + e2
            weights.append((e0 / den, e1 / den, e2 / den))
            l_ref[h] = m + jnp.log(den)
        om = (jnp.where(lo, weights[0][0], weights[1][0]) * o0_ref[...]
              + jnp.where(lo, weights[0][1], weights[1][1]) * o1_ref[...]
              + jnp.where(lo, weights[0][2], weights[1][2]) * o2_ref[...])
        om_ref[...] = om
        omb_ref[...] = om.astype(BF16)

    ob = pl.BlockSpec((T, LANES), lambda p, i: (i, p))
    lb = pl.BlockSpec((2, T, 1), lambda p, i: (p, i, 0))
    return pl.pallas_call(
        body,
        out_shape=[jax.ShapeDtypeStruct((S, B_OUT_W), F32), jax.ShapeDtypeStruct((S, B_OUT_W), BF16),
                   jax.ShapeDtypeStruct((B_HPG, S, 1), F32)],
        grid=(NPG, S // T), in_specs=[ob] * 3 + [lb] * 3, out_specs=[ob, ob, lb],
        compiler_params=_cparams(("parallel", "parallel")), name=name,
    )(*os, *lses)


def _dil_pair_dq(g, q, k, v, qoff, koff, voff, do, lcol, dcol, name):
    coef_t, nbs_t = _dil_tables()

    def body(coef_ref, nbs_ref, q_ref, kh_ref, k_ref, vh_ref, v_ref, do_ref, l_ref, d_ref, dq_ref, kf, vf):
        hp = pl.program_id(0)
        n = pl.program_id(1)
        nbs = nbs_ref[B_HPG * g + 2 * hp]
        kf[0:W, :] = kh_ref[...]
        kf[W:, :] = k_ref[...]
        vf[0:W, :] = vh_ref[...]
        vf[W:, :] = v_ref[...]
        biases = [_dil_bias(coef_ref[B_HPG * g + 2 * hp + h], False) for h in range(2)]
        col = biases[0][1]
        lo2 = _first_head((2 * W, LANES))
        for b in range(DIL_SUB):
            first = lax.rem(n * DIL_SUB + b, nbs) == 0
            rows = slice(b * W, (b + 1) * W)
            both = slice(b * W, (b + 2) * W)
            qv = q_ref[rows, :]
            dov = do_ref[rows, :]
            acc = None
            for h, (kh, vh) in enumerate(zip(_each_head(kf[both, :], lo2), _each_head(vf[both, :], lo2))):
                s = lax.dot_general(qv, kh, _DN["nt"], preferred_element_type=F32) * SCALE + biases[h][0]
                s = jnp.where(jnp.logical_and(first, col < W), NEG_INF, s)
                p = jnp.exp(s - l_ref[h, rows, :])
                dp = lax.dot_general(dov, vh, _DN["nt"], preferred_element_type=F32)
                ds = (p * (dp - d_ref[h, rows, :]) * SCALE).astype(BF16)
                d = lax.dot_general(ds, kh, _DN["nn"], preferred_element_type=F32)
                acc = d if acc is None else acc + d
            dq_ref[rows, :] = acc.astype(BF16)

    sp = _dil_pair_specs(qoff, koff, voff)
    return pl.pallas_call(
        body, out_shape=jax.ShapeDtypeStruct((S, GROUP_W), BF16), grid=(NPG, DIL_STEPS),
        in_specs=[sp["smem"], sp["smem"], sp["q"], sp["k_prev"], sp["k"], sp["v_prev"], sp["v"], sp["o"], sp["col"],
                  sp["col"]],
        out_specs=sp["o"], scratch_shapes=[pltpu.VMEM((DIL_ROWS + W, LANES), BF16)] * 2,
        compiler_params=_cparams(("parallel", "parallel")), name=name,
    )(coef_t, nbs_t, q, k, k, v, v, do, lcol, dcol)


def _dil_pair_dkv(g, q, k, v, qoff, koff, voff, do, lrow, drow, name):
    coef_t, nbs_t = _dil_tables()

    def body(coef_ref, nbs_ref, k_ref, v_ref, q_ref, qn_ref, do_ref, don_ref, l_ref, ln_ref, d_ref, dn_ref,
             dk_ref, dv_ref, qf, dof, lf, df):
        hp = pl.program_id(0)
        n = pl.program_id(1)
        nbs = nbs_ref[B_HPG * g + 2 * hp]
        qf[0:DIL_ROWS, :] = q_ref[...]
        qf[DIL_ROWS:, :] = qn_ref[...]
        dof[0:DIL_ROWS, :] = do_ref[...]
        dof[DIL_ROWS:, :] = don_ref[...]
        lf[:, :, 0:DIL_ROWS] = l_ref[...]
        lf[:, :, DIL_ROWS:] = ln_ref[...]
        df[:, :, 0:DIL_ROWS] = d_ref[...]
        df[:, :, DIL_ROWS:] = dn_ref[...]
        biases = [_dil_bias(coef_ref[B_HPG * g + 2 * hp + h], True) for h in range(2)]
        col = biases[0][1]
        lo = _first_head((W, LANES))
        lo2 = _first_head((2 * W, LANES))
        for b in range(DIL_SUB):
            no_next = lax.rem(n * DIL_SUB + b + 1, nbs) == 0
            rows = slice(b * W, (b + 1) * W)
            both = slice(b * W, (b + 2) * W)
            dd = dof[both, :]
            dk = dv = None
            for h, (kh, vh, qh, ddh) in enumerate(zip(_each_head(k_ref[rows, :], lo), _each_head(v_ref[rows, :], lo),
                                                      _each_head(qf[both, :], lo2), _each_head(dd, lo2))):
                st = lax.dot_general(kh, qh, _DN["nt"], preferred_element_type=F32) * SCALE + biases[h][0]
                st = jnp.where(jnp.logical_and(no_next, col >= W), NEG_INF, st)
                pt = jnp.exp(st - lf[h, :, both])
                d = lax.dot_general(pt.astype(BF16), ddh, _DN["nn"], preferred_element_type=F32)
                dv = d if dv is None else dv + d
                dpt = lax.dot_general(vh, dd, _DN["nt"], preferred_element_type=F32)
                dst = (pt * (dpt - df[h, :, both]) * SCALE).astype(BF16)
                d = lax.dot_general(dst, qh, _DN["nn"], preferred_element_type=F32)
                dk = d if dk is None else dk + d
            dk_ref[rows, :] = dk.astype(BF16)
            dv_ref[rows, :] = dv.astype(BF16)

    sp = _dil_pair_specs(qoff, koff, voff)
    wide = jax.ShapeDtypeStruct((S, GROUP_W), BF16)
    return pl.pallas_call(
        body, out_shape=[wide, wide], grid=(NPG, DIL_STEPS),
        in_specs=[sp["smem"], sp["smem"], sp["k"], sp["v"], sp["q"], sp["q_next"], sp["o"], sp["o_next"], sp["row"],
                  sp["row_next"], sp["row"], sp["row_next"]],
        out_specs=[sp["o"], sp["o"]],
        scratch_shapes=[pltpu.VMEM((DIL_ROWS + W, LANES), BF16)] * 2 + [pltpu.VMEM((2, 1, DIL_ROWS + W), F32)] * 2,
        compiler_params=_cparams(("parallel", "parallel")), name=name,
    )(coef_t, nbs_t, k, v, q, q, do, do, lrow, lrow, drow, drow)


FFN_ROWS = 512
FFN_COLS = 256
HALO = 8


def _shifted(u, halo, back):
    T = u.shape[0]
    rows = lax.broadcasted_iota(jnp.int32, u.shape, 0)
    if back:
        s1 = jnp.where(rows == 0, halo[HALO - 1:HALO, :], pltpu.roll(u, 1, 0))
        s2 = jnp.where(rows == 0, halo[HALO - 2:HALO - 1, :],
                       jnp.where(rows == 1, halo[HALO - 1:HALO, :], pltpu.roll(u, 2, 0)))
    else:
        s1 = jnp.where(rows == T - 1, halo[0:1, :], pltpu.roll(u, T - 1, 0))
        s2 = jnp.where(rows == T - 1, halo[1:2, :],
                       jnp.where(rows == T - 2, halo[0:1, :], pltpu.roll(u, T - 2, 0)))
    return s1, s2


def _conv_parts(u_ref, h_ref, w_ref, b_ref, first):
    out = []
    for p in range(2):
        u = u_ref[p]
        halo = jnp.where(first, 0.0, h_ref[p])
        u1, u2 = _shifted(u, halo, True)
        w = w_ref[p]
        out.append((w[0:1, :] * u2 + w[1:2, :] * u1 + w[2:3, :] * u + b_ref[p], u1, u2, u))
    return out


def _ffn_specs():
    T, C = FFN_ROWS, FFN_COLS
    blk = pl.BlockSpec((2, T, C), lambda j, i: (0, i, j))
    prev = pl.BlockSpec((2, HALO, C), lambda j, i: (0, jnp.maximum(i * (T // HALO) - 1, 0), j))
    nxt = pl.BlockSpec((2, HALO, C), lambda j, i: (0, jnp.minimum((i + 1) * (T // HALO), S // HALO - 1), j))
    wsp = pl.BlockSpec((2, 3, C), lambda j, i: (0, 0, j))
    bsp = pl.BlockSpec((2, 1, C), lambda j, i: (0, 0, j))
    one = pl.BlockSpec((T, C), lambda j, i: (i, j))
    return blk, prev, nxt, wsp, bsp, one


def _ffn_act_fwd(u, w, b, name):
    blk, prev, _, wsp, bsp, one = _ffn_specs()

    def body(u_ref, h_ref, w_ref, b_ref, o_ref):
        (a, _, _, _), (g, _, _, _) = _conv_parts(u_ref, h_ref, w_ref, b_ref, pl.program_id(1) == 0)
        o_ref[...] = (g / (1.0 + jnp.exp(-g)) * a).astype(BF16)

    return pl.pallas_call(
        body, out_shape=jax.ShapeDtypeStruct((S, FF), BF16), grid=(FF // FFN_COLS, S // FFN_ROWS),
        in_specs=[blk, prev, wsp, bsp], out_specs=one,
        compiler_params=_cparams(("parallel", "parallel")), name=name,
    )(u, u, w, b)


def _ffn_act_bwd(u, dact, w, b, name):
    blk, prev, _, wsp, bsp, one = _ffn_specs()

    def body(u_ref, h_ref, da_ref, w_ref, b_ref, duc_ref, dwb_ref):
        i = pl.program_id(1)
        (a, a1, a2, a0), (g, g1, g2, g0) = _conv_parts(u_ref, h_ref, w_ref, b_ref, i == 0)
        dact_v = da_ref[...]
        sg = 1.0 / (1.0 + jnp.exp(-g))
        d_a = dact_v * (g * sg)
        d_g = dact_v * a * (sg * (1.0 + g * (1.0 - sg)))
        duc_ref[0] = d_a
        duc_ref[1] = d_g

        @pl.when(i == 0)
        def _():
            dwb_ref[...] = jnp.zeros(dwb_ref.shape, F32)

        for p, (d, s2, s1, s0) in enumerate(((d_a, a2, a1, a0), (d_g, g2, g1, g0))):
            dwb_ref[p, 0:1, :] += jnp.sum(d * s2, axis=0, keepdims=True)
            dwb_ref[p, 1:2, :] += jnp.sum(d * s1, axis=0, keepdims=True)
            dwb_ref[p, 2:3, :] += jnp.sum(d * s0, axis=0, keepdims=True)
            dwb_ref[p, 3:4, :] += jnp.sum(d, axis=0, keepdims=True)

    return pl.pallas_call(
        body, out_shape=[jax.ShapeDtypeStruct((2, S, FF), F32), jax.ShapeDtypeStruct((2, 8, FF), F32)],
        grid=(FF // FFN_COLS, S // FFN_ROWS), in_specs=[blk, prev, one, wsp, bsp],
        out_specs=[blk, pl.BlockSpec((2, 8, FFN_COLS), lambda j, i: (0, 0, j))],
        compiler_params=_cparams(("parallel", "arbitrary")), name=name,
    )(u, u, dact, w, b)


def _ffn_conv_bwd(duc, w, name):
    blk, _, nxt, wsp, _, _ = _ffn_specs()
    last = S // FFN_ROWS - 1

    def body(d_ref, h_ref, w_ref, du_ref):
        is_last = pl.program_id(1) == last
        for p in range(2):
            d = d_ref[p]
            halo = jnp.where(is_last, 0.0, h_ref[p])
            d1, d2 = _shifted(d, halo, False)
            wv = w_ref[p]
            du_ref[p] = (wv[2:3, :] * d + wv[1:2, :] * d1 + wv[0:1, :] * d2).astype(BF16)

    return pl.pallas_call(
        body, out_shape=jax.ShapeDtypeStruct((2, S, FF), BF16), grid=(FF // FFN_COLS, S // FFN_ROWS),
        in_specs=[blk, nxt, wsp], out_specs=blk,
        compiler_params=_cparams(("parallel", "parallel")), name=name,
    )(duc, duc, w)


def _adam_update(w, gv, m, v):
    c1 = 1.0 / (1.0 - ADAM_B1 ** ADAM_STEP)
    c2 = 1.0 / (1.0 - ADAM_B2 ** ADAM_STEP)
    mn = ADAM_B1 * m + (1.0 - ADAM_B1) * gv
    vn = ADAM_B2 * v + (1.0 - ADAM_B2) * (gv * gv)
    return -ADAM_LR * ((mn * c1) / (jnp.sqrt(vn * c2) + ADAM_EPS) + ADAM_WD * w), mn, vn


def _adamw(w, g, m, v, name):
    rows = w.shape[0]
    T = 8
    for cand in (256, 128, 64, 32, 16, 8):
        if rows % cand == 0:
            T = cand
            break

    def body(w_ref, g_ref, m_ref, v_ref, d_ref, mo_ref, vo_ref):
        d_ref[...], mo_ref[...], vo_ref[...] = _adam_update(w_ref[...], g_ref[...], m_ref[...], v_ref[...])

    blk = pl.BlockSpec((T, w.shape[1]), lambda i: (i, 0))
    sds = jax.ShapeDtypeStruct(w.shape, F32)
    return pl.pallas_call(
        body, out_shape=[sds, sds, sds], grid=(rows // T,), in_specs=[blk] * 4, out_specs=[blk] * 3,
        compiler_params=_cparams(("parallel",)), name=name,
    )(w, g, m, v)


ANY = pl.BlockSpec(memory_space=pl.ANY)


def _place():
    x, y, c = lax.axis_index("x"), lax.axis_index("y"), lax.axis_index("c")
    chips = [(1 - x, y), (x, 1 - y), (1 - x, 1 - y)]
    return x, y, c, chips


def _place_own(w, slot_arr, name):
    rows = w.shape[0]
    T = 16
    for cand in (2048, 1024, 512, 256, 128, 64, 32, 16):
        if rows % cand == 0:
            T = cand
            break

    def body(k_ref, w_ref, o_ref):
        o_ref[...] = w_ref[...]

    return pl.pallas_call(
        body, out_shape=jax.ShapeDtypeStruct((N_CHIPS, rows, FLAT_W), w.dtype),
        grid_spec=pltpu.PrefetchScalarGridSpec(
            num_scalar_prefetch=1, grid=(rows // T,),
            in_specs=[pl.BlockSpec((T, FLAT_W), lambda i, k: (i, 0))],
            out_specs=pl.BlockSpec((None, T, FLAT_W), lambda i, k: (k[0], i, 0))),
        compiler_params=_cparams(("parallel",)), name=name,
    )(slot_arr, w)


def _allgather_shards(w, buf):
    half_rows = w.shape[0] // 2
    assert half_rows % 16 == 0

    def body(w_ref, buf_ref, g_ref, send_sems, recv_sems):
        x, y, c, chips = _place()
        myk = 2 * x + y
        sibling = (x, y, 1 - c)
        h0 = pl.multiple_of(c * half_rows, 16)
        h1 = pl.multiple_of((1 - c) * half_rows, 16)

        def half(k, start):
            return g_ref.at[k, pl.ds(start, half_rows), :]

        def rcopy(sem, src, dst, to):
            return pltpu.make_async_remote_copy(src_ref=src, dst_ref=dst, send_sem=send_sems.at[sem],
                                                recv_sem=recv_sems.at[sem], device_id=to, device_id_type=MESH)

        ici = [rcopy(r, w_ref.at[pl.ds(h0, half_rows), :], half(myk, h0), (*chip, c)) for r, chip in enumerate(chips)]
        for cp in ici:
            cp.start()
        ks = [2 * cx + cy for cx, cy in chips]
        fwd = [rcopy(3 + r, half(ks[r], h0), half(ks[r], h0), sibling) for r in range(3)]
        for r in range(3):
            rcopy(r, half(ks[r], h0), half(ks[r], h0), (*chips[r], c)).wait_recv()
            fwd[r].start()
        for r in range(3):
            rcopy(3 + r, half(ks[r], h1), half(ks[r], h1), sibling).wait_recv()
        for cp in ici + fwd:
            cp.wait_send()

    return pl.pallas_call(
        body, out_shape=jax.ShapeDtypeStruct(buf.shape, w.dtype), in_specs=[ANY, ANY], out_specs=ANY,
        scratch_shapes=[pltpu.SemaphoreType.DMA((6,)), pltpu.SemaphoreType.DMA((6,))],
        input_output_aliases={1: 0},
        compiler_params=pltpu.CompilerParams(has_side_effects=True), name="allgather_shards",
    )(w, buf)


HBM_SPEC = pl.BlockSpec(memory_space=pltpu.HBM)
SEM_SPEC = pl.BlockSpec(memory_space=pltpu.SEMAPHORE)
DATAFLOW = pltpu.SideEffectType.DATAFLOW_SIDE_EFFECTING
OWN_SLOT = 3


def _late_gather_start(w, land):
    def body(w_ref, land_ref, send_sems, recv_sems, w_thru, land_thru, token):
        x, y, c, chips = _place()
        for r, chip in enumerate(chips):
            pltpu.make_async_remote_copy(src_ref=w_ref, dst_ref=land_ref.at[r], send_sem=send_sems.at[r],
                                         recv_sem=recv_sems.at[r], device_id=(*chip, c), device_id_type=MESH).start()
        token[...] = jnp.zeros_like(token)

    return pl.pallas_call(
        body, name="late_gather_start",
        out_shape=(pltpu.SemaphoreType.DMA((3,)), pltpu.SemaphoreType.DMA((3,)), pltpu.HBM(w.shape, w.dtype),
                   pltpu.HBM(land.shape, land.dtype), jax.ShapeDtypeStruct((8, LANES), F32)),
        in_specs=(HBM_SPEC, HBM_SPEC),
        out_specs=(SEM_SPEC, SEM_SPEC, HBM_SPEC, HBM_SPEC, pl.BlockSpec(memory_space=pltpu.VMEM)),
        input_output_aliases={0: 2, 1: 3}, compiler_params=pltpu.CompilerParams(has_side_effects=DATAFLOW),
    )(pltpu.with_memory_space_constraint(w, pltpu.HBM), pltpu.with_memory_space_constraint(land, pltpu.HBM))


def _late_gather_wait(send_sems, recv_sems, w_thru, land_thru, after):
    def body(w_ref, land_ref, send_sems, recv_sems, after_ref, w_dead, got_ref):
        x, y, c, chips = _place()
        for r, chip in enumerate(chips):
            cp = pltpu.make_async_remote_copy(src_ref=w_ref, dst_ref=land_ref.at[r], send_sem=send_sems.at[r],
                                              recv_sem=recv_sems.at[r], device_id=(*chip, c), device_id_type=MESH)
            cp.wait_send()
            cp.wait_recv()

    return pl.pallas_call(
        body, name="late_gather_wait",
        out_shape=(pltpu.HBM(w_thru.shape, w_thru.dtype), pltpu.HBM(land_thru.shape, land_thru.dtype)),
        in_specs=(HBM_SPEC, HBM_SPEC, SEM_SPEC, SEM_SPEC, pl.BlockSpec(memory_space=pl.ANY)),
        out_specs=(HBM_SPEC, HBM_SPEC), input_output_aliases={0: 0, 1: 1},
        compiler_params=pltpu.CompilerParams(has_side_effects=DATAFLOW),
    )(w_thru, land_thru, send_sems, recv_sems, after)


def _flat_tile(rows):
    return next(t for t in (2048, 1024, 512, 256, 128, 64, 32, 16) if rows % t == 0)


def _sibling_swap_half(g, tag):
    half = g.shape[1] // 2

    def body(g_ref, o_ref, send_sem, recv_sem):
        x, y, c, _ = _place()
        theirs = pl.multiple_of((1 - c) * half, 8)
        cp = pltpu.make_async_remote_copy(src_ref=g_ref.at[:, pl.ds(theirs, half), :], dst_ref=o_ref,
                                          send_sem=send_sem, recv_sem=recv_sem, device_id=(x, y, 1 - c),
                                          device_id_type=MESH)
        cp.start()
        cp.wait()

    return pl.pallas_call(
        body, out_shape=jax.ShapeDtypeStruct((N_CHIPS, half, FLAT_W), F32), in_specs=[ANY], out_specs=ANY,
        scratch_shapes=[pltpu.SemaphoreType.DMA, pltpu.SemaphoreType.DMA],
        compiler_params=pltpu.CompilerParams(has_side_effects=True), name=f"rs_sibling_swap_{tag}",
    )(g)


def _pair_sum(g, other, c_arr, tag):
    half = other.shape[1]
    T = _flat_tile(half)

    def body(c_ref, g_ref, o_ref, s_ref):
        s_ref[...] = (g_ref[...] + o_ref[...]).astype(BF16)

    nb = half // T
    return pl.pallas_call(
        body, out_shape=jax.ShapeDtypeStruct((N_CHIPS, half, FLAT_W), BF16),
        grid_spec=pltpu.PrefetchScalarGridSpec(
            num_scalar_prefetch=1, grid=(N_CHIPS, nb),
            in_specs=[pl.BlockSpec((None, T, FLAT_W), lambda k, i, c: (k, c[0] * nb + i, 0)),
                      pl.BlockSpec((None, T, FLAT_W), lambda k, i, c: (k, i, 0))],
            out_specs=pl.BlockSpec((None, T, FLAT_W), lambda k, i, c: (k, i, 0))),
        compiler_params=_cparams(("parallel", "parallel")), name=f"rs_pair_sum_{tag}",
    )(c_arr, g, other)


def _chip_exchange(s, tag):
    def body(s_ref, o_ref, send_sems, recv_sems):
        x, y, c, chips = _place()
        cps = []
        for r, (cx, cy) in enumerate(chips):
            cps.append(pltpu.make_async_remote_copy(
                src_ref=s_ref.at[2 * cx + cy], dst_ref=o_ref.at[r], send_sem=send_sems.at[r],
                recv_sem=recv_sems.at[r], device_id=(cx, cy, c), device_id_type=MESH))
        for cp in cps:
            cp.start()
        for cp in cps:
            cp.wait()

    return pl.pallas_call(
        body, out_shape=jax.ShapeDtypeStruct((3,) + s.shape[1:], BF16), in_specs=[ANY], out_specs=ANY,
        scratch_shapes=[pltpu.SemaphoreType.DMA((3,)), pltpu.SemaphoreType.DMA((3,))],
        compiler_params=pltpu.CompilerParams(has_side_effects=True), name=f"rs_chip_exchange_{tag}",
    )(s)


def _chip_exchange_start(s, land):
    def body(s_ref, land_ref, send_sems, recv_sems, s_thru, land_thru, token):
        x, y, c, chips = _place()
        for r, (cx, cy) in enumerate(chips):
            pltpu.make_async_remote_copy(src_ref=s_ref.at[2 * cx + cy], dst_ref=land_ref.at[r], send_sem=send_sems.at[r],
                                         recv_sem=recv_sems.at[r], device_id=(cx, cy, c), device_id_type=MESH).start()
        token[...] = jnp.zeros_like(token)

    return pl.pallas_call(
        body, name="rs_exchange_start",
        out_shape=(pltpu.SemaphoreType.DMA((3,)), pltpu.SemaphoreType.DMA((3,)), pltpu.HBM(s.shape, s.dtype),
                   pltpu.HBM(land.shape, land.dtype), jax.ShapeDtypeStruct((8, LANES), F32)),
        in_specs=(HBM_SPEC, HBM_SPEC),
        out_specs=(SEM_SPEC, SEM_SPEC, HBM_SPEC, HBM_SPEC, pl.BlockSpec(memory_space=pltpu.VMEM)),
        input_output_aliases={0: 2, 1: 3}, compiler_params=pltpu.CompilerParams(has_side_effects=DATAFLOW),
    )(pltpu.with_memory_space_constraint(s, pltpu.HBM), pltpu.with_memory_space_constraint(land, pltpu.HBM))


def _chip_exchange_wait(send_sems, recv_sems, s_thru, land_thru, after):
    def body(s_ref, land_ref, send_sems, recv_sems, after_ref, s_done, got_ref):
        x, y, c, chips = _place()
        for r, (cx, cy) in enumerate(chips):
            cp = pltpu.make_async_remote_copy(src_ref=s_ref.at[2 * cx + cy], dst_ref=land_ref.at[r],
                                              send_sem=send_sems.at[r], recv_sem=recv_sems.at[r], device_id=(cx, cy, c),
                                              device_id_type=MESH)
            cp.wait_send()
            cp.wait_recv()

    return pl.pallas_call(
        body, name="rs_exchange_wait",
        out_shape=(pltpu.HBM(s_thru.shape, s_thru.dtype), pltpu.HBM(land_thru.shape, land_thru.dtype)),
        in_specs=(HBM_SPEC, HBM_SPEC, SEM_SPEC, SEM_SPEC, pl.BlockSpec(memory_space=pl.ANY)),
        out_specs=(HBM_SPEC, HBM_SPEC), input_output_aliases={0: 0, 1: 1},
        compiler_params=pltpu.CompilerParams(has_side_effects=DATAFLOW),
    )(s_thru, land_thru, send_sems, recv_sems, after)


def _chip_sum(s, r, k_arr, tag):
    half = s.shape[1]
    T = _flat_tile(half)

    def body(k_ref, s_ref, r_ref, o_ref):
        o_ref[...] = ((s_ref[...].astype(F32) + r_ref[0].astype(F32)) + r_ref[1].astype(F32)) + r_ref[2].astype(F32)

    return pl.pallas_call(
        body, out_shape=jax.ShapeDtypeStruct((half, FLAT_W), F32),
        grid_spec=pltpu.PrefetchScalarGridSpec(
            num_scalar_prefetch=1, grid=(half // T,),
            in_specs=[pl.BlockSpec((None, T, FLAT_W), lambda i, k: (k[0], i, 0)),
                      pl.BlockSpec((3, T, FLAT_W), lambda i, k: (0, i, 0))],
            out_specs=pl.BlockSpec((T, FLAT_W), lambda i, k: (i, 0))),
        compiler_params=_cparams(("parallel",)), name=f"rs_chip_sum_{tag}",
    )(k_arr, s, r)


def _sibling_send(t, tag):
    def body(t_ref, o_ref, send_sem, recv_sem):
        x, y, c, _ = _place()
        cp = pltpu.make_async_remote_copy(src_ref=t_ref, dst_ref=o_ref, send_sem=send_sem, recv_sem=recv_sem,
                                          device_id=(x, y, 1 - c), device_id_type=MESH)
        cp.start()
        cp.wait()

    return pl.pallas_call(
        body, out_shape=jax.ShapeDtypeStruct(t.shape, F32), in_specs=[ANY], out_specs=ANY,
        scratch_shapes=[pltpu.SemaphoreType.DMA, pltpu.SemaphoreType.DMA],
        compiler_params=pltpu.CompilerParams(has_side_effects=True), name=f"rs_sibling_send_{tag}",
    )(t)


def _allreduce_small(v):
    def body(v_ref, o_ref, buf, send_sems, recv_sems):
        x, y, c, _ = _place()
        me = 4 * x + 2 * y + c
        buf[me] = v_ref[...]
        cps = []
        for mask in range(1, 8):
            a, b, d = (mask >> 2) & 1, (mask >> 1) & 1, mask & 1
            peer = (x + a - 2 * a * x, y + b - 2 * b * y, c + d - 2 * d * c)
            cps.append(pltpu.make_async_remote_copy(
                src_ref=v_ref, dst_ref=buf.at[me], send_sem=send_sems.at[mask - 1], recv_sem=recv_sems.at[mask - 1],
                device_id=peer, device_id_type=MESH))
        for cp in cps:
            cp.start()
        for cp in cps:
            cp.wait()
        total = buf[0]
        for dev in range(1, 8):
            total = total + buf[dev]
        o_ref[...] = total

    vm = pl.BlockSpec(memory_space=pltpu.VMEM)
    return pl.pallas_call(
        body, out_shape=jax.ShapeDtypeStruct((SMALL_ROWS, 1024), F32), in_specs=[vm], out_specs=vm,
        scratch_shapes=[pltpu.VMEM((8, SMALL_ROWS, 1024), F32), pltpu.SemaphoreType.DMA((7,)),
                        pltpu.SemaphoreType.DMA((7,))],
        compiler_params=pltpu.CompilerParams(has_side_effects=True), name="allreduce_small",
    )(v)


def _heads(t, n):
    return t.reshape(S, n, HEAD_DIM).transpose(1, 0, 2)


def _unheads(t):
    return t.transpose(1, 0, 2).reshape(S, t.shape[0] * HEAD_DIM)


def _to_residue(t, d):
    c = t.shape[-1]
    return t.reshape(B_HPG, S // d, d, c).transpose(0, 2, 1, 3).reshape(B_HPG, S, c)


def _from_residue(t, d):
    c = t.shape[-1]
    return t.reshape(B_HPG, d, S // d, c).transpose(0, 2, 1, 3).reshape(B_HPG, S, c)


def _dil_pack(t):
    return jnp.concatenate([_to_residue(t[g], d) for g, (_, d) in enumerate(B_GROUPS)], axis=0)


def _dil_unpack(t):
    return jnp.stack([_from_residue(t[g * B_HPG:(g + 1) * B_HPG], d) for g, (_, d) in enumerate(B_GROUPS)], axis=0)


def _col_to_row(t):
    return t.reshape(t.shape[0], 1, S)


def _residue_rows(t, d, inverse=False):
    if d == 1:
        return t
    shape = (d, S // d) if inverse else (S // d, d)
    return t.reshape(shape + t.shape[1:]).transpose(1, 0, 2).reshape(t.shape)


def _residue_vecs(t, d, inverse=False):
    if d == 1:
        return t
    shape = (d, S // d) if inverse else (S // d, d)
    return t.reshape((B_HPG,) + shape).transpose(0, 2, 1).reshape(B_HPG, S, 1)


def _ffn_fwd(x, g, w_up, cw, cb, w_down, tag):
    h = _rms_fwd(x, g, f"{tag}_norm")
    u = _mm(h, w_up, mode="nn", tm=1024, tn=1408, tk=1024, o_split=2, name=f"{tag}_up")
    act = _ffn_act_fwd(u, cw, cb, f"{tag}_act")
    x_out = _mm(act, w_down, mode="nn", tm=1024, tn=512, tk=FF, res=x, name=f"{tag}_down")
    return x_out, (h, u, act)


def _ffn_bwd(x, g, w_up, cw, cb, w_down, saved, dx, dxb, tag):
    h, u, act = saved
    d_w_down = _mm(act, dxb, mode="tn", tm=1408, tn=512, tk=1024, name=f"{tag}_dwdown")
    dact = _mm(dxb, w_down, mode="nt", tm=1024, tn=1408, tk=1024, name=f"{tag}_dact")
    duc, dwb = _ffn_act_bwd(u, dact, cw, cb, f"{tag}_dgate")
    du = _ffn_conv_bwd(duc, cw, f"{tag}_dconv")
    d_w_up = _mm(h, du, mode="tn", tm=1024, tn=1408, tk=1024, b_split=2, name=f"{tag}_dwup")
    dh = _mm(du, w_up, mode="nt", tm=1024, tn=512, tk=1408, a_split=2, name=f"{tag}_dh")
    dx_new, dxb_new, (dg,) = _rms_bwd(x, dx, [(g, dh)], f"{tag}_dnorm")
    d_cw = dwb[:, 0:3, :].transpose(1, 0, 2).reshape(3, 2 * FF)
    d_cb = dwb[:, 3, :].reshape(2 * FF)
    return dx_new, dxb_new, dict(w_up=d_w_up, w_down=d_w_down, conv_w=d_cw, conv_b=d_cb, norm_g=dg.reshape(D))


def _local_step(x, target, p, late_weights, late_grads_ready):
    g = {}
    h1 = _rms_fwd(x, p["mix_norm_g"][0], "a_norm")
    w_qkv = p["a_w_in"][:, :QKV_W]
    w_f = jnp.pad(p["a_w_in"][:, QKV_W:], ((0, 0), (0, LANES - A_HEADS)))
    b_f = jnp.pad(p["a_b_f"].reshape(1, A_HEADS), ((0, 0), (0, LANES - A_HEADS)))
    qkv = _mm(h1, w_qkv, mode="nn", tm=1024, tn=512, tk=1024, out_dtype=BF16, name="a_qkv")
    pf = _mm(h1, w_f, mode="nn", tm=1024, tn=LANES, tk=1024, name="a_gate")
    cum = _fgate_fwd(pf, b_f, "a_gate_scan")
    aug_q, aug_k = _gate_lanes(cum)
    oa2, lse_a = _fox_pair_fwd(qkv, aug_q, aug_k, "a_attn")
    x1 = _mm(oa2, p["a_w_out"], mode="nn", tm=1024, tn=512, tk=1024, res=x, name="a_out")
    p = {**p, **late_weights(x1)}
    x2, ffn0 = _ffn_fwd(x1, p["ffn_norm_g"][0], p["ffn_w_up"][0], p["conv_w"][0], p["conv_b"][0], p["ffn_w_down"][0], "f0")
    hk = _rms_fwd(x2, p["kv_norm_g"], "kv_norm")
    kvb = _mm(hk, p["w_kv"], mode="nn", tm=1024, tn=512, tk=1024, out_dtype=BF16, name="kv_proj")
    h3 = _rms_fwd(x2, p["mix_norm_g"][1], "b_norm")
    qb = _mm(h3, p["b_w_q"], mode="nn", tm=1024, tn=512, tk=1024, out_dtype=BF16, name="b_q")
    dil_in = []
    for gi, (_, d) in enumerate(B_GROUPS):
        if d == 1:
            dil_in.append((qb, kvb, kvb, gi * NPG, gi * NPG, (3 + gi) * NPG))
        else:
            qg = _residue_rows(qb[:, gi * GROUP_W:(gi + 1) * GROUP_W], d)
            kvg = _residue_rows(kvb.reshape(S, 2, 3, GROUP_W)[:, :, gi, :].reshape(S, 2 * GROUP_W), d)
            dil_in.append((qg, kvg, kvg, 0, 0, NPG))
    o_g, lse_g = [], []
    for gi, (_, d) in enumerate(B_GROUPS):
        qg, kg, vg, qoff, koff, voff = dil_in[gi]
        og, lg = _dil_pair_fwd(gi, qg, kg, vg, qoff, koff, voff, f"b_attn{gi}")
        o_g.append(_residue_rows(og, d, inverse=True))
        lse_g.append(_residue_vecs(lg, d, inverse=True))
    ob, ob2, lse_b = _dil_pair_merge(o_g, lse_g, "b_merge")
    x3 = _mm(ob2, p["b_w_out"], mode="nn", tm=1024, tn=512, tk=B_OUT_W, res=x2, name="b_out")
    x4, ffn1 = _ffn_fwd(x3, p["ffn_norm_g"][1], p["ffn_w_up"][1], p["conv_w"][1], p["conv_b"][1], p["ffn_w_down"][1], "f1")
    loss, dx, dxb, dg_final = _loss_head(x4, p["final_norm_g"], target, "loss_head")
    g["final_norm_g"] = dg_final.reshape(D)

    dx, dxb, gf1 = _ffn_bwd(x3, p["ffn_norm_g"][1], p["ffn_w_up"][1], p["conv_w"][1], p["conv_b"][1], p["ffn_w_down"][1],
                            ffn1, dx, dxb, "f1")
    g["b_w_out"] = _mm(ob2, dxb, mode="tn", tm=B_OUT_W, tn=512, tk=1024, name="b_dwout")
    dob = _mm(dxb, p["b_w_out"], mode="nt", tm=1024, tn=B_OUT_W, tk=1024, name="b_do")
    delta_b = _pair_rowdot(dob, ob, "b_delta")
    dob16 = dob.astype(BF16)
    dq_g, dk_g, dv_g = [], [], []
    for gi, (_, d) in enumerate(B_GROUPS):
        qg, kg, vg, qoff, koff, voff = dil_in[gi]
        dog, l_d, dl_d = _residue_rows(dob16, d), _residue_vecs(lse_b, d), _residue_vecs(delta_b, d)
        dqd = _dil_pair_dq(gi, qg, kg, vg, qoff, koff, voff, dog, l_d, dl_d, f"b_dq{gi}")
        dkd, dvd = _dil_pair_dkv(gi, qg, kg, vg, qoff, koff, voff, dog, _col_to_row(l_d), _col_to_row(dl_d), f"b_dkv{gi}")
        dq_g.append(_residue_rows(dqd, d, inverse=True))
        dk_g.append(_residue_rows(dkd, d, inverse=True))
        dv_g.append(_residue_rows(dvd, d, inverse=True))
    dqb = jnp.concatenate(dq_g, axis=1)
    dkvb = jnp.concatenate(dk_g + dv_g, axis=1)
    g["b_w_q"] = _mm(h3, dqb, mode="tn", tm=1024, tn=512, tk=1024, name="b_dwq")
    dh3 = _mm(dqb, p["b_w_q"], mode="nt", tm=1024, tn=512, tk=B_Q_W, name="b_dh")
    g["w_kv"] = _mm(hk, dkvb, mode="tn", tm=1024, tn=512, tk=1024, name="kv_dw")
    dhk = _mm(dkvb, p["w_kv"], mode="nt", tm=1024, tn=512, tk=1536, name="kv_dh")
    dx, dxb, (dg_mix1, dg_kv) = _rms_bwd(x2, dx, [(p["mix_norm_g"][1], dh3), (p["kv_norm_g"], dhk)], "b_dnorm")
    g["kv_norm_g"] = dg_kv.reshape(D)
    dx, dxb, gf0 = _ffn_bwd(x1, p["ffn_norm_g"][0], p["ffn_w_up"][0], p["conv_w"][0], p["conv_b"][0], p["ffn_w_down"][0],
                            ffn0, dx, dxb, "f0")
    g["ffn_w_up"] = jnp.stack([gf0["w_up"], gf1["w_up"]])
    g["ffn_w_down"] = jnp.stack([gf0["w_down"], gf1["w_down"]])
    g["ffn_conv_w"] = jnp.stack([gf0["conv_w"], gf1["conv_w"]])
    token = late_grads_ready(g)
    a_w_out_t = p["a_w_out"] + token[0, 0].astype(BF16)
    g["a_w_out"] = _mm(oa2, dxb, mode="tn", tm=1024, tn=512, tk=1024, name="a_dwout")
    doa = _mm(dxb, a_w_out_t, mode="nt", tm=1024, tn=512, tk=1024, name="a_do")
    delta_a = _pair_rowdot(doa, oa2, "a_delta")
    dqa, dka, dva, dck, dcq = _fox_pair_bwd(qkv, doa, _col_to_row(lse_a), _col_to_row(delta_a), aug_q, aug_k, "a_dattn")
    dqkv = jnp.concatenate([dqa, dka, dva], axis=1)
    pad_heads = lambda t: jnp.pad(t.reshape(A_HEADS, S).T, ((0, 0), (0, LANES - A_HEADS)))
    dpf, db_f = _fgate_bwd(pf, b_f, pad_heads(dck), pad_heads(dcq), "a_dgate_scan")
    g["a_b_f"] = db_f[:, :A_HEADS]
    d_w_qkv = _mm(h1, dqkv, mode="tn", tm=1024, tn=512, tk=1024, name="a_dwqkv")
    d_w_f = _mm(h1, dpf, mode="tn", tm=1024, tn=LANES, tk=1024, name="a_dwgate")
    g["a_w_in"] = jnp.concatenate([d_w_qkv, d_w_f[:, :A_HEADS]], axis=1)
    dh1 = _mm(dqkv, w_qkv, mode="nt", tm=1024, tn=512, tk=1536, name="a_dh")
    dh1 = _mm(dpf, w_f, mode="nt", tm=1024, tn=512, tk=LANES, res=dh1, name="a_dh_gate")
    dx, _, (dg_mix0,) = _rms_bwd(x, dx, [(p["mix_norm_g"][0], dh1)], "a_dnorm")

    g["mix_norm_g"] = jnp.stack([dg_mix0.reshape(D), dg_mix1.reshape(D)])
    g["ffn_norm_g"] = jnp.stack([gf0["norm_g"], gf1["norm_g"]])
    g["ffn_conv_b"] = jnp.stack([gf0["conv_b"], gf1["conv_b"]])
    return loss[0, 0], dx, g


_SHARD_SHAPES = {"a_w_in": (1, 1024, 772), "a_w_out": (1, 256, 1024), "b_w_q": (1, 1024, 384), "b_w_out": (1, 512, 256),
                 "w_kv": (1024, 768), "ffn_w_up": (2, 1024, 1408), "ffn_w_down": (2, 704, 1024), "ffn_conv_w": (2, 3, 1408)}
_SHARD_AXIS = {"a_w_in": 2, "a_w_out": 1, "b_w_q": 2, "b_w_out": 2, "w_kv": 1, "ffn_w_up": 2, "ffn_w_down": 1,
               "ffn_conv_w": 2}
_SMALL = (("kv_norm_g", (1024,)), ("mix_norm_g", (2, 1024)), ("ffn_norm_g", (2, 1024)), ("final_norm_g", (1024,)),
          ("a_b_f", (1, 16)), ("ffn_conv_b", (2, 5632)))


def _slabs(t, L, R, C, rpad):
    nc = -(-C // FLAT_W)
    t = jnp.pad(t.reshape(L, R, C), ((0, 0), (0, rpad - R), (0, nc * FLAT_W - C)))
    return t.reshape(L, rpad, nc, FLAT_W).transpose(0, 2, 1, 3).reshape(L * nc * rpad, FLAT_W)


def _unslabs(rows, L, R, C, rpad):
    nc = -(-C // FLAT_W)
    return rows.reshape(L, nc, rpad, FLAT_W).transpose(0, 2, 1, 3).reshape(L, rpad, nc * FLAT_W)[:, :R, :C]


_SEG_RT = {"ffn_w_down": 704, "a_w_in": 1024, "a_w_out": 256, "b_w_q": 1024, "b_w_out": 512, "w_kv": 1024,
           "ffn_w_up": 1024, "ffn_conv_w": 16}
_ROW_SHARDED = ("a_w_out", "ffn_w_down")


_LAYOUTS = {"early": ("a_w_in", "a_w_out"), "late": ("ffn_w_down", "b_w_q", "b_w_out", "w_kv", "ffn_w_up", "ffn_conv_w"),
            "grad_early": ("a_w_out", "a_w_in"),
            "grad_late": ("ffn_w_down", "b_w_q", "b_w_out", "w_kv", "ffn_w_up", "ffn_conv_w")}
_GRAD_ROWS = {"grad_early": 10240, "grad_late": 45056}


def _layout_rows(layout):
    used = sum(_seg_rows(*s) for s in _SEGS if s[0] in _LAYOUTS[layout])
    rows = _GRAD_ROWS.get(layout, used)
    assert rows >= used
    return rows


def _grad_layout(name):
    return "grad_early" if name in _LAYOUTS["grad_early"] else "grad_late"


def _seg(name, layout=None):
    layout = layout or _grad_layout(name)
    off = 0
    for s in sorted((s for s in _SEGS if s[0] in _LAYOUTS[layout]), key=lambda s: _LAYOUTS[layout].index(s[0])):
        if s[0] == name:
            _, L, R, C, rpad = s
            rt = _SEG_RT[name]
            assert off % rt == 0 and rpad % rt == 0
            half = _layout_rows(layout) // 2
            assert layout not in _GRAD_ROWS or half % rt == 0 or off + _seg_rows(*s) <= half
            return dict(L=L, R=R, C=C, rpad=rpad, nc=-(-C // FLAT_W), rt=rt, off=off, ni=rpad // rt, half=half)
        off += _seg_rows(*s)
    raise KeyError(name)


def _flat_block(sg, term=0):
    base = (sg["off"] + term * sg["L"] * sg["nc"] * sg["rpad"]) // sg["rt"]
    return lambda l, j, i: base + (l * sg["nc"] + j) * sg["ni"] + i


def _native3(t, name):
    sg = _seg(name)
    t = t.reshape(sg["L"], sg["R"], sg["C"])
    return jnp.pad(t, ((0, 0), (0, sg["rpad"] - sg["R"]), (0, 0))) if sg["rpad"] != sg["R"] else t


def _slab_pack(flat, t, name, layout, term=None):
    sg = _seg(name, layout)
    rt = sg["rt"]
    rb = _flat_block(sg, term or 0)

    def body(*refs):
        t_ref, o_ref = refs[-2], refs[-1]
        val = t_ref[...]
        o_ref[...] = val.astype(BF16) if term is None else _split3(val)[term]

    in_specs = [pl.BlockSpec((None, rt, FLAT_W), lambda l, j, i: (l, i, j))]
    args = [t]
    if flat is not None:
        in_specs, args = [ANY] + in_specs, [flat] + args
    return pl.pallas_call(
        body, out_shape=jax.ShapeDtypeStruct((_layout_rows(layout), FLAT_W), BF16), grid=(sg["L"], sg["nc"], sg["ni"]),
        in_specs=in_specs, out_specs=pl.BlockSpec((rt, FLAT_W), lambda l, j, i: (rb(l, j, i), 0)),
        input_output_aliases={0: 0} if flat is not None else {},
        compiler_params=_cparams(("parallel", "parallel", "parallel")), name=f"pack_{name}_{term or 0}",
    )(*args)


def _full_spec(sg, name):
    rt, nc, ni = sg["rt"], sg["nc"], sg["ni"]
    if name in _ROW_SHARDED:
        return (sg["L"], N_CHIPS * sg["R"], sg["C"]), pl.BlockSpec((None, rt, FLAT_W), lambda k, l, j, i: (l, k * ni + i, j))
    return ((sg["L"], sg["rpad"], N_CHIPS * nc * FLAT_W),
            pl.BlockSpec((None, rt, FLAT_W), lambda k, l, j, i: (l, i, k * nc + j)))


def _slab_unpack(gathered, slots, name, layout, own=None):
    sg = _seg(name, layout)
    rb = _flat_block(sg)
    shape, _ = _full_spec(sg, name)
    rt, nc, ni = sg["rt"], sg["nc"], sg["ni"]
    width = nc * FLAT_W
    last = gathered.shape[0] - 1

    def body(*refs):
        s_ref, o_ref = refs[0], refs[-1]
        is_own = s_ref[pl.program_id(0)] == OWN_SLOT
        for j in range(nc):
            val = refs[1 + j][...]
            if own is not None:
                val = jnp.where(is_own, refs[1 + nc + j][...], val)
            o_ref[:, j * FLAT_W:(j + 1) * FLAT_W] = val

    if name in _ROW_SHARDED:
        o_spec = pl.BlockSpec((None, rt, width), lambda k, l, i, s: (l, k * ni + i, 0))
    else:
        o_spec = pl.BlockSpec((None, rt, width), lambda k, l, i, s: (l, i, k))
    in_specs = [pl.BlockSpec((None, rt, FLAT_W), lambda k, l, i, s, j=j: (jnp.minimum(s[k], last), rb(l, j, i), 0))
                for j in range(nc)]
    args = [gathered] * nc
    if own is not None:
        in_specs += [pl.BlockSpec((rt, FLAT_W), lambda k, l, i, s, j=j: (rb(l, j, i), 0)) for j in range(nc)]
        args += [own] * nc
    return pl.pallas_call(
        body, out_shape=jax.ShapeDtypeStruct(shape, BF16),
        grid_spec=pltpu.PrefetchScalarGridSpec(num_scalar_prefetch=1, grid=(N_CHIPS, sg["L"], ni), in_specs=in_specs,
                                               out_specs=o_spec),
        compiler_params=_cparams(("parallel",) * 3), name=f"unpack_{name}",
    )(slots, *args)


def _slab_pack_grad(flat4, g, name):
    sg = _seg(name)
    rows = _layout_rows(_grad_layout(name))
    rb = _flat_block(sg)
    shape, spec = _full_spec(sg, name)
    assert g.shape == shape, (name, g.shape, shape)

    def body(*refs):
        refs[-1][...] = refs[-2][...]

    in_specs, args = [spec], [g]
    if flat4 is not None:
        in_specs, args = [pl.BlockSpec(memory_space=pl.ANY)] + in_specs, [flat4] + args
    return pl.pallas_call(
        body, out_shape=jax.ShapeDtypeStruct((N_CHIPS, rows, FLAT_W), F32), grid=(N_CHIPS, sg["L"], sg["nc"], sg["ni"]),
        in_specs=in_specs, out_specs=pl.BlockSpec((None, sg["rt"], FLAT_W), lambda k, l, j, i: (k, rb(l, j, i), 0)),
        input_output_aliases={0: 0} if flat4 is not None else {},
        compiler_params=_cparams(("parallel",) * 4), name=f"packgrad_{name}",
    )(*args)


def _adamw_shard(w, m, v, g_mine, g_other, c_arr, name):
    sg = _seg(name)
    rt = sg["rt"]
    rb = _flat_block(sg)
    per_half = sg["half"] // rt

    def half_of(l, j, i):
        return (rb(l, j, i) * rt) // sg["half"]

    def body(c_ref, w_ref, m_ref, v_ref, gm_ref, go_ref, g_ref, d_ref, mo_ref, vo_ref):
        is_mine = half_of(pl.program_id(0), pl.program_id(1), pl.program_id(2)) == c_ref[0]
        gv = jnp.where(is_mine, gm_ref[...], go_ref[...])
        g_ref[...] = gv
        d_ref[...], mo_ref[...], vo_ref[...] = _adam_update(w_ref[...], gv, m_ref[...], v_ref[...])

    nat = pl.BlockSpec((None, rt, FLAT_W), lambda l, j, i, c: (l, i, j))
    half = pl.BlockSpec((rt, FLAT_W), lambda l, j, i, c: (rb(l, j, i) - half_of(l, j, i) * per_half, 0))
    sds = jax.ShapeDtypeStruct(w.shape, F32)
    return pl.pallas_call(
        body, out_shape=[sds] * 4,
        grid_spec=pltpu.PrefetchScalarGridSpec(num_scalar_prefetch=1, grid=(sg["L"], sg["nc"], sg["ni"]),
                                               in_specs=[nat, nat, nat, half, half], out_specs=[nat] * 4),
        compiler_params=_cparams(("parallel", "parallel", "parallel")), name=f"adamw_{name}",
    )(c_arr, w, m, v, g_mine, g_other)


def _pack_small(vals, loss=None):
    parts = [vals[name].astype(F32).reshape(-1) for name, _ in _SMALL]
    if loss is not None:
        parts.append(loss.reshape(1))
    flat = jnp.concatenate(parts)
    return jnp.pad(flat, (0, SMALL_ROWS * 1024 - flat.shape[0])).reshape(SMALL_ROWS, 1024)


def _unpack_small(flat):
    flat = flat.reshape(-1)
    out = {}
    o = 0
    for name, shape in _SMALL:
        n = int(np.prod(shape))
        out[name] = flat[o:o + n].reshape(shape)
        o += n
    return out, flat[o]


_BIG = ("a_w_in", "a_w_out", "b_w_q", "b_w_out", "w_kv", "ffn_w_up", "ffn_w_down", "ffn_conv_w")
A_IN_PAD = 896


def _pack_weights(w, layout):
    flat = None
    for name in _LAYOUTS[layout]:
        t = _native3(w[name], name)
        for term in ((0, 1, 2) if name == "ffn_conv_w" else (None,)):
            flat = _slab_pack(flat, t, name, layout, term)
    return flat


def _early_weights(gathered, slots):
    a_in = _slab_unpack(gathered, slots, "a_w_in", "early")
    a_in = a_in.reshape(D, N_CHIPS, A_IN_PAD)[:, :, :772].reshape(D, N_CHIPS * 772)
    return dict(a_w_in=a_in, a_w_out=_slab_unpack(gathered, slots, "a_w_out", "early")[0])


def _late_weights(landed, slots, own):
    full = {name: _slab_unpack(landed, slots, name, "late", own) for name in _LAYOUTS["late"] if name != "ffn_conv_w"}
    sg = _seg("ffn_conv_w", "late")
    n1 = sg["nc"] * sg["rpad"]
    conv = slice(sg["off"], sg["off"] + CONV_TERMS * n1)
    conv_rows = jnp.concatenate([landed[:, conv], own[None, conv]], axis=0)
    per_chip = []
    for k in range(N_CHIPS):
        rows = lax.dynamic_index_in_dim(conv_rows, slots[k], axis=0, keepdims=False)
        terms = [_unslabs(rows[i * n1:(i + 1) * n1], 1, sg["R"], sg["C"], sg["rpad"]).astype(F32) for i in range(CONV_TERMS)]
        per_chip.append((terms[0] + terms[1]) + terms[2])
    cw = jnp.concatenate(per_chip, axis=2).reshape(2, 3, 2, FF).transpose(0, 2, 1, 3)
    return dict(b_w_q=full["b_w_q"][0], b_w_out=full["b_w_out"][0], w_kv=full["w_kv"][0], ffn_w_up=full["ffn_w_up"],
                ffn_w_down=full["ffn_w_down"], conv_w=cw)


def _shard_grads(g, layout):
    def full(name):
        if name == "a_w_in":
            a_in = jnp.pad(g[name].reshape(D, N_CHIPS, 772), ((0, 0), (0, 0), (0, A_IN_PAD - 772)))
            return a_in.reshape(1, D, N_CHIPS * A_IN_PAD)
        if name == "ffn_conv_w":
            sgc = _seg(name)
            return jnp.pad(g[name].reshape(1, sgc["R"], 2 * FF), ((0, 0), (0, sgc["rpad"] - sgc["R"]), (0, 0)))
        return g[name] if g[name].ndim == 3 else g[name][None]

    flat4 = None
    for name in _LAYOUTS[layout]:
        flat4 = _slab_pack_grad(flat4, full(name), name)
    return flat4


_WEIGHTS = ["a_w_in", "a_b_f", "a_w_out", "b_w_q", "b_w_out", "kv_norm_g", "w_kv", "mix_norm_g", "ffn_norm_g", "ffn_w_up",
            "ffn_conv_w", "ffn_conv_b", "ffn_w_down", "final_norm_g"]


def kernel(x, a_w_in, a_b_f, a_w_out, b_w_q, b_w_out, kv_norm_g, w_kv, mix_norm_g, ffn_norm_g, ffn_w_up, ffn_conv_w, ffn_conv_b, ffn_w_down, final_norm_g, loss_target, m_a_w_in, m_a_b_f, m_a_w_out, m_b_w_q, m_b_w_out, m_kv_norm_g, m_w_kv, m_mix_norm_g, m_ffn_norm_g, m_ffn_w_up, m_ffn_conv_w, m_ffn_conv_b, m_ffn_w_down, m_final_norm_g, v_a_w_in, v_a_b_f, v_a_w_out, v_b_w_q, v_b_w_out, v_kv_norm_g, v_w_kv, v_mix_norm_g, v_ffn_norm_g, v_ffn_w_up, v_ffn_conv_w, v_ffn_conv_b, v_ffn_w_down, v_final_norm_g):
    w = dict(a_w_in=a_w_in, a_b_f=a_b_f, a_w_out=a_w_out, b_w_q=b_w_q, b_w_out=b_w_out, kv_norm_g=kv_norm_g, w_kv=w_kv,
             mix_norm_g=mix_norm_g, ffn_norm_g=ffn_norm_g, ffn_w_up=ffn_w_up, ffn_conv_w=ffn_conv_w, ffn_conv_b=ffn_conv_b,
             ffn_w_down=ffn_w_down, final_norm_g=final_norm_g)
    m = dict(a_w_in=m_a_w_in, a_b_f=m_a_b_f, a_w_out=m_a_w_out, b_w_q=m_b_w_q, b_w_out=m_b_w_out, kv_norm_g=m_kv_norm_g,
             w_kv=m_w_kv, mix_norm_g=m_mix_norm_g, ffn_norm_g=m_ffn_norm_g, ffn_w_up=m_ffn_w_up, ffn_conv_w=m_ffn_conv_w,
             ffn_conv_b=m_ffn_conv_b, ffn_w_down=m_ffn_w_down, final_norm_g=m_final_norm_g)
    v = dict(a_w_in=v_a_w_in, a_b_f=v_a_b_f, a_w_out=v_a_w_out, b_w_q=v_b_w_q, b_w_out=v_b_w_out, kv_norm_g=v_kv_norm_g,
             w_kv=v_w_kv, mix_norm_g=v_mix_norm_g, ffn_norm_g=v_ffn_norm_g, ffn_w_up=v_ffn_w_up, ffn_conv_w=v_ffn_conv_w,
             ffn_conv_b=v_ffn_conv_b, ffn_w_down=v_ffn_w_down, final_norm_g=v_final_norm_g)

    c_arr = lax.axis_index("c").astype(jnp.int32).reshape(1)
    k_arr = (2 * lax.axis_index("x") + lax.axis_index("y")).astype(jnp.int32).reshape(1)
    xi, yi = lax.axis_index("x"), lax.axis_index("y")
    late_slots = jnp.stack([jnp.where(k == k_arr[0], OWN_SLOT, 2 * ((k & 1) ^ yi) + ((k >> 1) ^ xi) - 1)
                            for k in range(N_CHIPS)]).astype(jnp.int32)
    w_late = _pack_weights(w, "late")
    land = lax.empty((OWN_SLOT,) + w_late.shape, BF16)
    send_sems, recv_sems, w_thru, land_thru, token = _late_gather_start(w_late, land)
    w_early = _pack_weights(w, "early")
    early = _allgather_shards(w_early, _place_own(w_early, k_arr, "early_place_own"))
    p = _early_weights(early, jnp.arange(N_CHIPS, dtype=jnp.int32))
    cb = ffn_conv_b.reshape(2, 2, 1, FF)
    p.update(a_b_f=a_b_f, kv_norm_g=kv_norm_g, mix_norm_g=mix_norm_g + token[0, 0], ffn_norm_g=ffn_norm_g,
             final_norm_g=final_norm_g, conv_b=cb)

    def late_weights(after):
        own, landed = _late_gather_wait(send_sems, recv_sems, w_thru, land_thru, after)
        return _late_weights(landed, late_slots, own)

    started = {}

    def late_grads_ready(g_so_far):
        gflat = _shard_grads(g_so_far, "grad_late")
        pair = _pair_sum(gflat, _sibling_swap_half(gflat, "late"), c_arr, "late")
        land = lax.empty((3,) + pair.shape[1:], BF16)
        *handles, token = _chip_exchange_start(pair, land)
        started["handles"] = handles
        return token

    loss_part, grad_x, g = _local_step(x[0], loss_target[0], p, late_weights, late_grads_ready)

    halves = {}
    pair, landed = _chip_exchange_wait(*started["handles"], grad_x)
    g_mine = _chip_sum(pair, landed, k_arr, "late")
    halves["grad_late"] = (g_mine, _sibling_send(g_mine, "late"))
    gflat = _shard_grads(g, "grad_early")
    pair = _pair_sum(gflat, _sibling_swap_half(gflat, "early"), c_arr, "early")
    g_mine = _chip_sum(pair, _chip_exchange(pair, "early"), k_arr, "early")
    halves["grad_early"] = (g_mine, _sibling_send(g_mine, "early"))
    small, loss = _unpack_small(_allreduce_small(_pack_small(g, loss_part)))

    big = [{}, {}, {}, {}]
    for name in _BIG:
        sg = _seg(name)
        g_mine, g_other = halves[_grad_layout(name)]
        res = _adamw_shard(_native3(w[name], name), _native3(m[name], name), _native3(v[name], name), g_mine, g_other,
                           c_arr, name)
        for store, t in zip(big, res):
            store[name] = t[:, :sg["R"], :].reshape(_SHARD_SHAPES[name])
    dws, mns, vns = _adamw(_pack_small(w), _pack_small(small), _pack_small(m), _pack_small(v), "adamw_small")
    sml = [small] + [_unpack_small(t)[0] for t in (dws, mns, vns)]
    outs = [loss, grad_x[None]]
    for b, s in zip(big, sml):
        outs += [b[n] if n in b else s[n] for n in _WEIGHTS]
    return tuple(outs)
```

```python
import numpy as np
import jax
import jax.numpy as jnp
from jax import lax
from jax.experimental import pallas as pl
from jax.experimental.pallas import tpu as pltpu

F32 = jnp.float32
BF16 = jnp.bfloat16
MESH = pl.DeviceIdType.MESH

S = 4096
D = 1024
A_HEADS = 16
HEAD_DIM = 64
QKV_W = 3 * A_HEADS * HEAD_DIM
B_GROUPS = ((128, 1), (512, 4), (2048, 16))
B_HPG = 8
B_Q_W = 3 * B_HPG * HEAD_DIM
B_OUT_W = B_HPG * HEAD_DIM
B_KV_W = 2 * B_Q_W
B_WIN = 128
FF = 2816
RMS_EPS = 1e-6
SCALE = HEAD_DIM ** -0.5
N_CHIPS = 4

ADAM_LR, ADAM_B1, ADAM_B2, ADAM_EPS, ADAM_WD, ADAM_STEP = 0.001, 0.9, 0.999, 1e-08, 0.01, 10

V7X_VMEM_LIMIT = 48 * 1024 * 1024
LANES = 128
NEG_INF = float("-inf")

FLAT_W = LANES
_SEGS = (("ffn_w_down", 2, 704, 1024, 704), ("a_w_in", 1, 1024, 772, 1024), ("a_w_out", 1, 256, 1024, 256),
         ("b_w_q", 1, 1024, 384, 1024), ("b_w_out", 1, 512, 256, 512), ("w_kv", 1, 1024, 768, 1024),
         ("ffn_w_up", 2, 1024, 1408, 1024), ("ffn_conv_w", 1, 6, 1408, 16))
CONV_TERMS = 3


def _seg_rows(name, L, R, C, rpad):
    return (CONV_TERMS if name == "ffn_conv_w" else 1) * L * (-(-C // FLAT_W)) * rpad


SMALL_ROWS = 24


def _cparams(sem=None, **kw):
    return pltpu.CompilerParams(dimension_semantics=sem, vmem_limit_bytes=V7X_VMEM_LIMIT, **kw)


_DN = {"nn": (((1,), (0,)), ((), ())), "nt": (((1,), (1,)), ((), ())), "tn": (((0,), (0,)), ((), ()))}


def _mm(a, b, *, mode, tm, tn, tk, name, out_dtype=F32, res=None, a_split=0, b_split=0, o_split=0):
    if mode == "tn":
        K = a.shape[0]
        M = a.shape[1]
    else:
        M = a.shape[-2]
        K = a.shape[-1] * (2 if a_split else 1)
    if mode == "nt":
        N = b.shape[0]
    else:
        N = b.shape[-1] * (2 if b_split else 1)
    assert M % tm == 0 and N % tn == 0 and K % tk == 0, (name, M, N, K, tm, tn, tk)
    nk = K // tk

    if mode == "tn":
        a_spec = pl.BlockSpec((tk, tm), lambda i, j, k: (k, i))
    elif a_split:
        a_spec = pl.BlockSpec((None, tm, tk), lambda i, j, k: (k // a_split, i, k % a_split))
    else:
        a_spec = pl.BlockSpec((tm, tk), lambda i, j, k: (i, k))
    if mode == "nt":
        b_spec = pl.BlockSpec((tn, tk), lambda i, j, k: (j, k))
    elif b_split:
        b_spec = pl.BlockSpec((None, tk, tn), lambda i, j, k: (j // b_split, k, j % b_split))
    else:
        b_spec = pl.BlockSpec((tk, tn), lambda i, j, k: (k, j))
    if o_split:
        o_spec = pl.BlockSpec((None, tm, tn), lambda i, j, k: (j // o_split, i, j % o_split))
        out_shape = jax.ShapeDtypeStruct((2, M, N // 2), out_dtype)
    else:
        o_spec = pl.BlockSpec((tm, tn), lambda i, j, k: (i, j))
        out_shape = jax.ShapeDtypeStruct((M, N), out_dtype)
    in_specs = [a_spec, b_spec]
    args = [a, b]
    if res is not None:
        in_specs.append(pl.BlockSpec((tm, tn), lambda i, j, k: (i, j)))
        args.append(res)

    def body(*refs):
        if res is not None:
            a_ref, b_ref, r_ref, o_ref = refs[:4]
        else:
            a_ref, b_ref, o_ref = refs[:3]
            r_ref = None
        p = lax.dot_general(a_ref[...].astype(BF16), b_ref[...].astype(BF16), _DN[mode], preferred_element_type=F32)

        def finish(r):
            if r_ref is not None:
                r = r + r_ref[...]
            o_ref[...] = r.astype(out_dtype)

        if nk == 1:
            finish(p)
        else:
            acc = refs[-1]
            k = pl.program_id(2)

            @pl.when(k == 0)
            def _():
                acc[...] = p

            @pl.when(k > 0)
            def _():
                acc[...] += p

            @pl.when(k == nk - 1)
            def _():
                finish(acc[...])

    return pl.pallas_call(
        body, out_shape=out_shape, grid=(M // tm, N // tn, nk), in_specs=in_specs, out_specs=o_spec,
        scratch_shapes=[pltpu.VMEM((tm, tn), F32)] if nk > 1 else [],
        compiler_params=_cparams(("parallel", "parallel", "arbitrary")), name=name,
    )(*args)


NORM_ROWS = 256


def _rms_fwd(x, g, name):
    def body(x_ref, g_ref, o_ref):
        xv = x_ref[...]
        r = lax.rsqrt(jnp.mean(xv * xv, axis=-1, keepdims=True) + RMS_EPS)
        o_ref[...] = (xv * r * g_ref[...]).astype(BF16)

    row = pl.BlockSpec((NORM_ROWS, D), lambda i: (i, 0))
    return pl.pallas_call(
        body, out_shape=jax.ShapeDtypeStruct((S, D), BF16), grid=(S // NORM_ROWS,),
        in_specs=[row, pl.BlockSpec((1, D), lambda i: (0, 0))], out_specs=row,
        compiler_params=_cparams(("parallel",)), name=name,
    )(x, g.reshape(1, D))


def _rms_bwd(x, dres, pairs, name):
    n = len(pairs)

    def body(*refs):
        x_ref, dres_ref = refs[0], refs[1]
        g_refs = refs[2:2 + 2 * n:2]
        dh_refs = refs[3:3 + 2 * n:2]
        dx_ref, dxb_ref = refs[2 + 2 * n], refs[3 + 2 * n]
        dg_refs = refs[4 + 2 * n:]
        i = pl.program_id(0)
        xv = x_ref[...]
        r = lax.rsqrt(jnp.mean(xv * xv, axis=-1, keepdims=True) + RMS_EPS)
        y = xv * r
        dx = dres_ref[...]
        for g_ref, dh_ref, dg_ref in zip(g_refs, dh_refs, dg_refs):
            dh = dh_ref[...]
            dy = dh * g_ref[...]
            dx = dx + r * (dy - y * jnp.mean(dy * y, axis=-1, keepdims=True))
            part = jnp.sum(dh * y, axis=0, keepdims=True)

            @pl.when(i == 0)
            def _():
                dg_ref[...] = part

            @pl.when(i > 0)
            def _():
                dg_ref[...] += part

        dx_ref[...] = dx
        dxb_ref[...] = dx.astype(BF16)

    row = pl.BlockSpec((NORM_ROWS, D), lambda i: (i, 0))
    vec = pl.BlockSpec((1, D), lambda i: (0, 0))
    in_specs = [row, row]
    args = [x, dres]
    for g, dh in pairs:
        in_specs += [vec, row]
        args += [g.reshape(1, D), dh]
    outs = pl.pallas_call(
        body,
        out_shape=[jax.ShapeDtypeStruct((S, D), F32), jax.ShapeDtypeStruct((S, D), BF16)]
        + [jax.ShapeDtypeStruct((1, D), F32)] * n,
        grid=(S // NORM_ROWS,), in_specs=in_specs, out_specs=[row, row] + [vec] * n,
        compiler_params=_cparams(("arbitrary",)), name=name,
    )(*args)
    return outs[0], outs[1], list(outs[2:])


def _loss_head(x, g, target, name):
    def body(x_ref, g_ref, t_ref, loss_ref, dx_ref, dxb_ref, dg_ref):
        i = pl.program_id(0)
        xv = x_ref[...]
        gv = g_ref[...]
        r = lax.rsqrt(jnp.mean(xv * xv, axis=-1, keepdims=True) + RMS_EPS)
        y = xv * r
        err = y * gv - t_ref[...]
        lpart = jnp.broadcast_to(jnp.sum(err * err, keepdims=True) * (0.5 / D), (1, LANES))
        dh = err * (1.0 / D)
        dy = dh * gv
        dx = r * (dy - y * jnp.mean(dy * y, axis=-1, keepdims=True))
        part = jnp.sum(dh * y, axis=0, keepdims=True)

        @pl.when(i == 0)
        def _():
            dg_ref[...] = part
            loss_ref[...] = lpart

        @pl.when(i > 0)
        def _():
            dg_ref[...] += part
            loss_ref[...] += lpart

        dx_ref[...] = dx
        dxb_ref[...] = dx.astype(BF16)

    row = pl.BlockSpec((NORM_ROWS, D), lambda i: (i, 0))
    vec = pl.BlockSpec((1, D), lambda i: (0, 0))
    return pl.pallas_call(
        body,
        out_shape=[jax.ShapeDtypeStruct((1, LANES), F32), jax.ShapeDtypeStruct((S, D), F32),
                   jax.ShapeDtypeStruct((S, D), BF16), jax.ShapeDtypeStruct((1, D), F32)],
        grid=(S // NORM_ROWS,), in_specs=[row, vec, row],
        out_specs=[pl.BlockSpec((1, LANES), lambda i: (0, 0)), row, row, vec],
        compiler_params=_cparams(("arbitrary",)), name=name,
    )(x, g.reshape(1, D), target)


SCAN_ROWS = 256


def _split3(v):
    hi = v.astype(BF16)
    r1 = v - hi.astype(F32)
    mid = r1.astype(BF16)
    lo = (r1 - mid.astype(F32)).astype(BF16)
    return hi, mid, lo


def _tri_dot(tri, v):
    hi, mid, lo = _split3(v)
    dn = _DN["nn"]
    return (lax.dot_general(tri, hi, dn, preferred_element_type=F32)
            + lax.dot_general(tri, mid, dn, preferred_element_type=F32)
            + lax.dot_general(tri, lo, dn, preferred_element_type=F32))


def _log_sigmoid(z):
    return jnp.minimum(z, 0.0) - jnp.log(1.0 + jnp.exp(-jnp.abs(z)))


def _fgate_fwd(pf, bias, name):
    tri = jnp.tril(jnp.ones((SCAN_ROWS, SCAN_ROWS), F32)).astype(BF16)

    def body(pf_ref, b_ref, tri_ref, c_ref):
        carry = jnp.zeros((1, LANES), F32)
        for blk in range(S // SCAN_ROWS):
            rows = pl.ds(blk * SCAN_ROWS, SCAN_ROWS)
            lf = _log_sigmoid(pf_ref[rows, :] + b_ref[...])
            c_ref[rows, :] = _tri_dot(tri_ref[...], lf) + carry
            carry = c_ref[pl.ds(blk * SCAN_ROWS + SCAN_ROWS - 1, 1), :]

    return pl.pallas_call(
        body, out_shape=jax.ShapeDtypeStruct((S, LANES), F32),
        compiler_params=_cparams(), name=name,
    )(pf, bias, tri)


def _fgate_bwd(pf, bias, dc_key, dc_query, name):
    triu = jnp.triu(jnp.ones((SCAN_ROWS, SCAN_ROWS), F32)).astype(BF16)

    def body(pf_ref, b_ref, dck_ref, dcq_ref, tri_ref, dpf_ref, db_ref, dlf_ref):
        carry = jnp.zeros((1, LANES), F32)
        db = jnp.zeros((1, LANES), F32)
        lane = lax.broadcasted_iota(jnp.int32, (SCAN_ROWS, LANES), 1)
        for blk in reversed(range(S // SCAN_ROWS)):
            rows = pl.ds(blk * SCAN_ROWS, SCAN_ROWS)
            dc = dck_ref[rows, :] + dcq_ref[rows, :]
            dlf_ref[rows, :] = _tri_dot(tri_ref[...], dc) + carry
            carry = dlf_ref[pl.ds(blk * SCAN_ROWS, 1), :]
            z = pf_ref[rows, :] + b_ref[...]
            e = jnp.exp(-jnp.abs(z))
            sig_neg = jnp.where(z >= 0.0, e, 1.0) / (1.0 + e)
            dz = jnp.where(lane < A_HEADS, dlf_ref[rows, :] * sig_neg, 0.0)
            dpf_ref[rows, :] = dz.astype(BF16)
            db = db + jnp.sum(dz, axis=0, keepdims=True)
        db_ref[...] = db

    return pl.pallas_call(
        body, out_shape=[jax.ShapeDtypeStruct((S, LANES), BF16), jax.ShapeDtypeStruct((1, LANES), F32)],
        scratch_shapes=[pltpu.VMEM((S, LANES), F32)],
        compiler_params=_cparams(), name=name,
    )(pf, bias, dc_key, dc_query, triu)


FOX_T = 512


def _first_head(shape):
    return lax.broadcasted_iota(jnp.int32, shape, len(shape) - 1) < HEAD_DIM


def _each_head(x, lo):
    zero = jnp.zeros_like(x)
    return jnp.where(lo, x, zero), jnp.where(lo, zero, x)


GATE_LANES = 6


def _gate_lanes(cum):
    to_bf16_grid = lambda t: lax.reduce_precision(t, exponent_bits=8, mantissa_bits=7)
    c = cum[:, :A_HEADS]
    hi = to_bf16_grid(c)
    mid = to_bf16_grid(c - hi)
    lo = to_bf16_grid((c - hi) - mid)
    one = jnp.ones_like(hi)
    def place(cols):
        t = jnp.stack(cols, axis=-1)
        t = jnp.pad(t, ((0, 0), (0, 0), (0, HEAD_DIM - GATE_LANES)))
        return t.reshape(S, A_HEADS // 2, 2, HEAD_DIM)[:, :, ::-1, :].reshape(S, A_HEADS * HEAD_DIM).astype(BF16)
    return place([hi, mid, lo, one, one, one]), place([one, one, one, -hi, -mid, -lo])


def _fox_pair_fwd(qkv, aug_q, aug_k, name):
    T = FOX_T
    nq = S // T
    NP = A_HEADS // 2

    def body(q_ref, k_ref, v_ref, aq_ref, ak_ref, o_ref, lse_ref, m_sc, l_sc, acc_sc):
        i = pl.program_id(1)
        j = pl.program_id(2)
        lo = _first_head((T, LANES))

        @pl.when(j == 0)
        def _():
            m_sc[...] = jnp.full((2, T, LANES), NEG_INF, F32)
            l_sc[...] = jnp.zeros((2, T, LANES), F32)
            acc_sc[...] = jnp.zeros((T, LANES), F32)

        def step(diagonal):
            qs = q_ref[...] * jnp.asarray(SCALE, BF16)
            aq, ak, kv = aq_ref[...], ak_ref[...], k_ref[...]
            q2 = (jnp.where(lo, qs, aq), jnp.where(lo, aq, qs))
            k2 = (jnp.where(lo, kv, ak), jnp.where(lo, ak, kv))
            if diagonal:
                causal = lax.broadcasted_iota(jnp.int32, (T, T), 0) >= lax.broadcasted_iota(jnp.int32, (T, T), 1)
            pv, alphas = None, []
            for h, vh in enumerate(_each_head(v_ref[...], lo)):
                s = lax.dot_general(q2[h], k2[h], _DN["nt"], preferred_element_type=F32)
                if diagonal:
                    s = jnp.where(causal, s, NEG_INF)
                m_prev = m_sc[h]
                m_new = jnp.maximum(m_prev, jnp.max(s, axis=1, keepdims=True))
                alpha = jnp.exp(m_prev - m_new)
                p = jnp.exp(s - jnp.tile(m_new, (1, T // LANES)))
                l_sc[h] = alpha * l_sc[h] + jnp.sum(p, axis=1, keepdims=True)
                m_sc[h] = m_new
                d = lax.dot_general(p.astype(BF16), vh, _DN["nn"], preferred_element_type=F32)
                pv = d if pv is None else pv + d
                alphas.append(alpha)
            acc_sc[...] = jnp.where(lo, alphas[0], alphas[1]) * acc_sc[...] + pv

        @pl.when(j < i)
        def _():
            step(False)

        @pl.when(j == i)
        def _():
            step(True)
            o_ref[...] = (acc_sc[...] * jnp.where(lo, 1.0 / l_sc[0], 1.0 / l_sc[1])).astype(BF16)
            for h in range(2):
                lse_ref[h] = (m_sc[h] + jnp.log(l_sc[h]))[:, 0:1]

    qs_ = pl.BlockSpec((T, LANES), lambda p, i, j: (i, p))
    ks = pl.BlockSpec((T, LANES), lambda p, i, j: (jnp.minimum(i, j), NP + p))
    vs = pl.BlockSpec((T, LANES), lambda p, i, j: (jnp.minimum(i, j), 2 * NP + p))
    aks = pl.BlockSpec((T, LANES), lambda p, i, j: (jnp.minimum(i, j), p))
    col = pl.BlockSpec((2, T, 1), lambda p, i, j: (p, i, 0))
    return pl.pallas_call(
        body, out_shape=[jax.ShapeDtypeStruct((S, A_HEADS * HEAD_DIM), BF16), jax.ShapeDtypeStruct((A_HEADS, S, 1), F32)],
        grid=(NP, nq, nq), in_specs=[qs_, ks, vs, qs_, aks], out_specs=[qs_, col],
        scratch_shapes=[pltpu.VMEM((2, T, LANES), F32), pltpu.VMEM((2, T, LANES), F32), pltpu.VMEM((T, LANES), F32)],
        compiler_params=_cparams(("parallel", "parallel", "arbitrary")), name=name,
    )(qkv, qkv, qkv, aug_q, aug_k)


def _fox_pair_bwd(qkv, do, lse_row, delta_row, aug_q, aug_k, name):
    T = FOX_T
    nq = S // T
    NP = A_HEADS // 2

    def body(q_ref, k_ref, v_ref, do_ref, lse_ref, dl_ref, aq_ref, ak_ref, dq_ref, dk_ref, dv_ref, dc_ref, dcq_ref,
             dq_sc, dk_sc, dv_sc, dc_sc):
        j = pl.program_id(1)
        i = pl.program_id(2)
        lo = _first_head((T, LANES))

        @pl.when(jnp.logical_and(j == 0, i == 0))
        def _():
            dq_sc[...] = jnp.zeros((S, LANES), F32)
            dcq_ref[...] = jnp.zeros((2, nq, 1, T), F32)

        @pl.when(i == j)
        def _():
            dk_sc[...] = jnp.zeros((T, LANES), F32)
            dv_sc[...] = jnp.zeros((T, LANES), F32)
            dc_sc[...] = jnp.zeros((2, T, 1), F32)

        def step(diagonal):
            qv = q_ref[...]
            kv = k_ref[...]
            dov = do_ref[...].astype(BF16)
            qs = qv * jnp.asarray(SCALE, BF16)
            aq, ak = aq_ref[...], ak_ref[...]
            q2 = (jnp.where(lo, qs, aq), jnp.where(lo, aq, qs))
            k2 = (jnp.where(lo, kv, ak), jnp.where(lo, ak, kv))
            if diagonal:
                causal = lax.broadcasted_iota(jnp.int32, (T, T), 1) >= lax.broadcasted_iota(jnp.int32, (T, T), 0)
            dv = dk = dq = None
            for h, (kh, vh, qh, doh) in enumerate(zip(_each_head(kv, lo), _each_head(v_ref[...], lo),
                                                      _each_head(qv, lo), _each_head(dov, lo))):
                st = lax.dot_general(k2[h], q2[h], _DN["nt"], preferred_element_type=F32)
                if diagonal:
                    st = jnp.where(causal, st, NEG_INF)
                pt = jnp.exp(st - lse_ref[h])
                d = lax.dot_general(pt.astype(BF16), doh, _DN["nn"], preferred_element_type=F32)
                dv = d if dv is None else dv + d
                dpt = lax.dot_general(vh, dov, _DN["nt"], preferred_element_type=F32)
                dst = pt * (dpt - dl_ref[h])
                dc_sc[h] -= jnp.sum(dst, axis=1, keepdims=True)
                dcq_ref[h, i] += jnp.sum(dst, axis=0, keepdims=True)
                dsb = (dst * SCALE).astype(BF16)
                d = lax.dot_general(dsb, qh, _DN["nn"], preferred_element_type=F32)
                dk = d if dk is None else dk + d
                d = lax.dot_general(dsb, kh, _DN["tn"], preferred_element_type=F32)
                dq = d if dq is None else dq + d
            dv_sc[...] += dv
            dk_sc[...] += dk
            rows = pl.ds(pl.multiple_of(i * T, T), T)
            dq_sc[rows, :] += dq

        @pl.when(i > j)
        def _():
            step(False)

        @pl.when(i == j)
        def _():
            step(True)

        @pl.when(i == nq - 1)
        def _():
            dk_ref[...] = dk_sc[...].astype(BF16)
            dv_ref[...] = dv_sc[...].astype(BF16)
            dc_ref[...] = dc_sc[...]

        @pl.when(jnp.logical_and(j == nq - 1, i == nq - 1))
        def _():
            dq_ref[...] = dq_sc[...].astype(BF16)

    qs = pl.BlockSpec((T, LANES), lambda p, j, i: (jnp.maximum(i, j), p))
    qrow = pl.BlockSpec((2, 1, T), lambda p, j, i: (p, 0, jnp.maximum(i, j)))
    ks = pl.BlockSpec((T, LANES), lambda p, j, i: (j, NP + p))
    vs = pl.BlockSpec((T, LANES), lambda p, j, i: (j, 2 * NP + p))
    kout = pl.BlockSpec((T, LANES), lambda p, j, i: (j, p))
    kcol = pl.BlockSpec((2, T, 1), lambda p, j, i: (p, j, 0))
    dqs = pl.BlockSpec((S, LANES), lambda p, j, i: (0, p))
    dcqs = pl.BlockSpec((2, nq, 1, T), lambda p, j, i: (p, 0, 0, 0))
    wide = jax.ShapeDtypeStruct((S, A_HEADS * HEAD_DIM), BF16)
    return pl.pallas_call(
        body,
        out_shape=[wide, wide, wide, jax.ShapeDtypeStruct((A_HEADS, S, 1), F32),
                   jax.ShapeDtypeStruct((A_HEADS, nq, 1, T), F32)],
        grid=(NP, nq, nq), in_specs=[qs, ks, vs, qs, qrow, qrow, qs, kout], out_specs=[dqs, kout, kout, kcol, dcqs],
        scratch_shapes=[pltpu.VMEM((S, LANES), F32), pltpu.VMEM((T, LANES), F32), pltpu.VMEM((T, LANES), F32),
                        pltpu.VMEM((2, T, 1), F32)],
        compiler_params=_cparams(("parallel", "arbitrary", "arbitrary")), name=name,
    )(qkv, qkv, qkv, do, lse_row, delta_row, aug_q, aug_k)


def _pair_rowdot(a, b, name):
    n = a.shape[1] // HEAD_DIM
    T = 1024

    def body(a_ref, b_ref, o_ref):
        prod = a_ref[...].astype(F32) * b_ref[...].astype(F32)
        lo = _first_head(prod.shape)
        o_ref[0] = jnp.sum(jnp.where(lo, prod, 0.0), axis=1, keepdims=True)
        o_ref[1] = jnp.sum(jnp.where(lo, 0.0, prod), axis=1, keepdims=True)

    blk = pl.BlockSpec((T, LANES), lambda p, i: (i, p))
    return pl.pallas_call(
        body, out_shape=jax.ShapeDtypeStruct((n, S, 1), F32), grid=(n // 2, S // T), in_specs=[blk, blk],
        out_specs=pl.BlockSpec((2, T, 1), lambda p, i: (p, i, 0)),
        compiler_params=_cparams(("parallel", "parallel")), name=name,
    )(a, b)


W = B_WIN
N_HG = 3 * B_HPG
N_BLK = S // W


def _dil_tables():
    slopes = np.exp2((-8.0 * np.arange(1, N_HG + 1, dtype=np.float32) / N_HG).astype(np.float32)).astype(np.float32)
    dil = np.repeat(np.array([d for _, d in B_GROUPS], np.float32), B_HPG)
    coef = (slopes * dil).astype(np.float32)
    nbs = np.repeat(np.array([S // d // W for _, d in B_GROUPS], np.int32), B_HPG)
    return jnp.asarray(coef), jnp.asarray(nbs)


DIL_SUB = 8
DIL_ROWS = DIL_SUB * W
DIL_STEPS = S // DIL_ROWS


def _dil_bias(coef, transposed):
    row = lax.broadcasted_iota(jnp.int32, (W, 2 * W), 0)
    col = lax.broadcasted_iota(jnp.int32, (W, 2 * W), 1)
    dist = (col - row) if transposed else (row + W - col)
    valid = jnp.logical_and(dist >= 0, dist <= W)
    return jnp.where(valid, -coef * dist.astype(F32), NEG_INF), col


NPG = B_HPG // 2
GROUP_W = B_HPG * HEAD_DIM


def _dil_pair_specs(qoff, koff, voff):
    prev_blk = lambda n: jnp.maximum(n * DIL_SUB - 1, 0)
    next_blk = lambda n: jnp.minimum((n + 1) * DIL_SUB, N_BLK - 1)
    return dict(
        o=pl.BlockSpec((DIL_ROWS, LANES), lambda h, n: (n, h)),
        o_next=pl.BlockSpec((W, LANES), lambda h, n: (next_blk(n), h)),
        q=pl.BlockSpec((DIL_ROWS, LANES), lambda h, n: (n, qoff + h)),
        q_next=pl.BlockSpec((W, LANES), lambda h, n: (next_blk(n), qoff + h)),
        k=pl.BlockSpec((DIL_ROWS, LANES), lambda h, n: (n, koff + h)),
        k_prev=pl.BlockSpec((W, LANES), lambda h, n: (prev_blk(n), koff + h)),
        v=pl.BlockSpec((DIL_ROWS, LANES), lambda h, n: (n, voff + h)),
        v_prev=pl.BlockSpec((W, LANES), lambda h, n: (prev_blk(n), voff + h)),
        col=pl.BlockSpec((2, DIL_ROWS, 1), lambda h, n: (h, n, 0)),
        row=pl.BlockSpec((2, 1, DIL_ROWS), lambda h, n: (h, 0, n)),
        row_next=pl.BlockSpec((2, 1, W), lambda h, n: (h, 0, next_blk(n))),
        smem=pl.BlockSpec(memory_space=pltpu.SMEM))


def _dil_pair_fwd(g, q, k, v, qoff, koff, voff, name):
    coef_t, nbs_t = _dil_tables()

    def body(coef_ref, nbs_ref, q_ref, kh_ref, k_ref, vh_ref, v_ref, o_ref, lse_ref, kf, vf):
        hp = pl.program_id(0)
        n = pl.program_id(1)
        nbs = nbs_ref[B_HPG * g + 2 * hp]
        kf[0:W, :] = kh_ref[...]
        kf[W:, :] = k_ref[...]
        vf[0:W, :] = vh_ref[...]
        vf[W:, :] = v_ref[...]
        biases = [_dil_bias(coef_ref[B_HPG * g + 2 * hp + h], False) for h in range(2)]
        col = biases[0][1]
        lo = _first_head((W, LANES))
        lo2 = _first_head((2 * W, LANES))
        for b in range(DIL_SUB):
            first = lax.rem(n * DIL_SUB + b, nbs) == 0
            rows = slice(b * W, (b + 1) * W)
            both = slice(b * W, (b + 2) * W)
            qv = q_ref[rows, :]
            acc, inv = None, []
            for h, (kh, vh) in enumerate(zip(_each_head(kf[both, :], lo2), _each_head(vf[both, :], lo2))):
                s = lax.dot_general(qv, kh, _DN["nt"], preferred_element_type=F32) * SCALE + biases[h][0]
                s = jnp.where(jnp.logical_and(first, col < W), NEG_INF, s)
                m = jnp.max(s, axis=1, keepdims=True)
                p = jnp.exp(s - m)
                l = jnp.sum(p, axis=1, keepdims=True)
                d = lax.dot_general(p.astype(BF16), vh, _DN["nn"], preferred_element_type=F32)
                acc = d if acc is None else acc + d
                inv.append(1.0 / l)
                lse_ref[h, rows, :] = m + jnp.log(l)
            o_ref[rows, :] = acc * jnp.where(lo, inv[0], inv[1])

    sp = _dil_pair_specs(qoff, koff, voff)
    return pl.pallas_call(
        body, out_shape=[jax.ShapeDtypeStruct((S, GROUP_W), F32), jax.ShapeDtypeStruct((B_HPG, S, 1), F32)],
        grid=(NPG, DIL_STEPS), in_specs=[sp["smem"], sp["smem"], sp["q"], sp["k_prev"], sp["k"], sp["v_prev"], sp["v"]],
        out_specs=[sp["o"], sp["col"]], scratch_shapes=[pltpu.VMEM((DIL_ROWS + W, LANES), BF16)] * 2,
        compiler_params=_cparams(("parallel", "parallel")), name=name,
    )(coef_t, nbs_t, q, k, k, v, v)


def _dil_pair_merge(os, lses, name):
    T = 1024

    def body(o0_ref, o1_ref, o2_ref, l0_ref, l1_ref, l2_ref, om_ref, omb_ref, l_ref):
        lo = _first_head((T, LANES))
        weights = []
        for h in range(2):
            l0, l1, l2 = l0_ref[h], l1_ref[h], l2_ref[h]
            m = jnp.maximum(jnp.maximum(l0, l1), l2)
            e0, e1, e2 = jnp.exp(l0 - m), jnp.exp(l1 - m), jnp.exp(l2 - m)
            den = e0 + e1 + e2
            weights.append((e0 / den, e1 / den, e2 / den))
            l_ref[h] = m + jnp.log(den)
        om = (jnp.where(lo, weights[0][0], weights[1][0]) * o0_ref[...]
              + jnp.where(lo, weights[0][1], weights[1][1]) * o1_ref[...]
              + jnp.where(lo, weights[0][2], weights[1][2]) * o2_ref[...])
        om_ref[...] = om
        omb_ref[...] = om.astype(BF16)

    ob = pl.BlockSpec((T, LANES), lambda p, i: (i, p))
    lb = pl.BlockSpec((2, T, 1), lambda p, i: (p, i, 0))
    return pl.pallas_call(
        body,
        out_shape=[jax.ShapeDtypeStruct((S, B_OUT_W), F32), jax.ShapeDtypeStruct((S, B_OUT_W), BF16),
                   jax.ShapeDtypeStruct((B_HPG, S, 1), F32)],
        grid=(NPG, S // T), in_specs=[ob] * 3 + [lb] * 3, out_specs=[ob, ob, lb],
        compiler_params=_cparams(("parallel", "parallel")), name=name,
    )(*os, *lses)


def _dil_pair_dq(g, q, k, v, qoff, koff, voff, do, lcol, dcol, name):
    coef_t, nbs_t = _dil_tables()

    def body(coef_ref, nbs_ref, q_ref, kh_ref, k_ref, vh_ref, v_ref, do_ref, l_ref, d_ref, dq_ref, kf, vf):
        hp = pl.program_id(0)
        n = pl.program_id(1)
        nbs = nbs_ref[B_HPG * g + 2 * hp]
        kf[0:W, :] = kh_ref[...]
        kf[W:, :] = k_ref[...]
        vf[0:W, :] = vh_ref[...]
        vf[W:, :] = v_ref[...]
        biases = [_dil_bias(coef_ref[B_HPG * g + 2 * hp + h], False) for h in range(2)]
        col = biases[0][1]
        lo2 = _first_head((2 * W, LANES))
        for b in range(DIL_SUB):
            first = lax.rem(n * DIL_SUB + b, nbs) == 0
            rows = slice(b * W, (b + 1) * W)
            both = slice(b * W, (b + 2) * W)
            qv = q_ref[rows, :]
            dov = do_ref[rows, :]
            acc = None
            for h, (kh, vh) in enumerate(zip(_each_head(kf[both, :], lo2), _each_head(vf[both, :], lo2))):
                s = lax.dot_general(qv, kh, _DN["nt"], preferred_element_type=F32) * SCALE + biases[h][0]
                s = jnp.where(jnp.logical_and(first, col < W), NEG_INF, s)
                p = jnp.exp(s - l_ref[h, rows, :])
                dp = lax.dot_general(dov, vh, _DN["nt"], preferred_element_type=F32)
                ds = (p * (dp - d_ref[h, rows, :]) * SCALE).astype(BF16)
                d = lax.dot_general(ds, kh, _DN["nn"], preferred_element_type=F32)
                acc = d if acc is None else acc + d
            dq_ref[rows, :] = acc.astype(BF16)

    sp = _dil_pair_specs(qoff, koff, voff)
    return pl.pallas_call(
        body, out_shape=jax.ShapeDtypeStruct((S, GROUP_W), BF16), grid=(NPG, DIL_STEPS),
        in_specs=[sp["smem"], sp["smem"], sp["q"], sp["k_prev"], sp["k"], sp["v_prev"], sp["v"], sp["o"], sp["col"],
                  sp["col"]],
        out_specs=sp["o"], scratch_shapes=[pltpu.VMEM((DIL_ROWS + W, LANES), BF16)] * 2,
        compiler_params=_cparams(("parallel", "parallel")), name=name,
    )(coef_t, nbs_t, q, k, k, v, v, do, lcol, dcol)


def _dil_pair_dkv(g, q, k, v, qoff, koff, voff, do, lrow, drow, name):
    coef_t, nbs_t = _dil_tables()

    def body(coef_ref, nbs_ref, k_ref, v_ref, q_ref, qn_ref, do_ref, don_ref, l_ref, ln_ref, d_ref, dn_ref,
             dk_ref, dv_ref, qf, dof, lf, df):
        hp = pl.program_id(0)
        n = pl.program_id(1)
        nbs = nbs_ref[B_HPG * g + 2 * hp]
        qf[0:DIL_ROWS, :] = q_ref[...]
        qf[DIL_ROWS:, :] = qn_ref[...]
        dof[0:DIL_ROWS, :] = do_ref[...]
        dof[DIL_ROWS:, :] = don_ref[...]
        lf[:, :, 0:DIL_ROWS] = l_ref[...]
        lf[:, :, DIL_ROWS:] = ln_ref[...]
        df[:, :, 0:DIL_ROWS] = d_ref[...]
        df[:, :, DIL_ROWS:] = dn_ref[...]
        biases = [_dil_bias(coef_ref[B_HPG * g + 2 * hp + h], True) for h in range(2)]
        col = biases[0][1]
        lo = _first_head((W, LANES))
        lo2 = _first_head((2 * W, LANES))
        for b in range(DIL_SUB):
            no_next = lax.rem(n * DIL_SUB + b + 1, nbs) == 0
            rows = slice(b * W, (b + 1) * W)
            both = slice(b * W, (b + 2) * W)
            dd = dof[both, :]
            dk = dv = None
            for h, (kh, vh, qh, ddh) in enumerate(zip(_each_head(k_ref[rows, :], lo), _each_head(v_ref[rows, :], lo),
                                                      _each_head(qf[both, :], lo2), _each_head(dd, lo2))):
                st = lax.dot_general(kh, qh, _DN["nt"], preferred_element_type=F32) * SCALE + biases[h][0]
                st = jnp.where(jnp.logical_and(no_next, col >= W), NEG_INF, st)
                pt = jnp.exp(st - lf[h, :, both])
                d = lax.dot_general(pt.astype(BF16), ddh, _DN["nn"], preferred_element_type=F32)
                dv = d if dv is None else dv + d
                dpt = lax.dot_general(vh, dd, _DN["nt"], preferred_element_type=F32)
                dst = (pt * (dpt - df[h, :, both]) * SCALE).astype(BF16)
                d = lax.dot_general(dst, qh, _DN["nn"], preferred_element_type=F32)
                dk = d if dk is None else dk + d
            dk_ref[rows, :] = dk.astype(BF16)
            dv_ref[rows, :] = dv.astype(BF16)

    sp = _dil_pair_specs(qoff, koff, voff)
    wide = jax.ShapeDtypeStruct((S, GROUP_W), BF16)
    return pl.pallas_call(
        body, out_shape=[wide, wide], grid=(NPG, DIL_STEPS),
        in_specs=[sp["smem"], sp["smem"], sp["k"], sp["v"], sp["q"], sp["q_next"], sp["o"], sp["o_next"], sp["row"],
                  sp["row_next"], sp["row"], sp["row_next"]],
        out_specs=[sp["o"], sp["o"]],
        scratch_shapes=[pltpu.VMEM((DIL_ROWS + W, LANES), BF16)] * 2 + [pltpu.VMEM((2, 1, DIL_ROWS + W), F32)] * 2,
        compiler_params=_cparams(("parallel", "parallel")), name=name,
    )(coef_t, nbs_t, k, v, q, q, do, do, lrow, lrow, drow, drow)


FFN_ROWS = 512
FFN_COLS = 256
HALO = 8


def _shifted(u, halo, back):
    T = u.shape[0]
    rows = lax.broadcasted_iota(jnp.int32, u.shape, 0)
    if back:
        s1 = jnp.where(rows == 0, halo[HALO - 1:HALO, :], pltpu.roll(u, 1, 0))
        s2 = jnp.where(rows == 0, halo[HALO - 2:HALO - 1, :],
                       jnp.where(rows == 1, halo[HALO - 1:HALO, :], pltpu.roll(u, 2, 0)))
    else:
        s1 = jnp.where(rows == T - 1, halo[0:1, :], pltpu.roll(u, T - 1, 0))
        s2 = jnp.where(rows == T - 1, halo[1:2, :],
                       jnp.where(rows == T - 2, halo[0:1, :], pltpu.roll(u, T - 2, 0)))
    return s1, s2


def _conv_parts(u_ref, h_ref, w_ref, b_ref, first):
    out = []
    for p in range(2):
        u = u_ref[p]
        halo = jnp.where(first, 0.0, h_ref[p])
        u1, u2 = _shifted(u, halo, True)
        w = w_ref[p]
        out.append((w[0:1, :] * u2 + w[1:2, :] * u1 + w[2:3, :] * u + b_ref[p], u1, u2, u))
    return out


def _ffn_specs():
    T, C = FFN_ROWS, FFN_COLS
    blk = pl.BlockSpec((2, T, C), lambda j, i: (0, i, j))
    prev = pl.BlockSpec((2, HALO, C), lambda j, i: (0, jnp.maximum(i * (T // HALO) - 1, 0), j))
    nxt = pl.BlockSpec((2, HALO, C), lambda j, i: (0, jnp.minimum((i + 1) * (T // HALO), S // HALO - 1), j))
    wsp = pl.BlockSpec((2, 3, C), lambda j, i: (0, 0, j))
    bsp = pl.BlockSpec((2, 1, C), lambda j, i: (0, 0, j))
    one = pl.BlockSpec((T, C), lambda j, i: (i, j))
    return blk, prev, nxt, wsp, bsp, one


def _ffn_act_fwd(u, w, b, name):
    blk, prev, _, wsp, bsp, one = _ffn_specs()

    def body(u_ref, h_ref, w_ref, b_ref, o_ref):
        (a, _, _, _), (g, _, _, _) = _conv_parts(u_ref, h_ref, w_ref, b_ref, pl.program_id(1) == 0)
        o_ref[...] = (g / (1.0 + jnp.exp(-g)) * a).astype(BF16)

    return pl.pallas_call(
        body, out_shape=jax.ShapeDtypeStruct((S, FF), BF16), grid=(FF // FFN_COLS, S // FFN_ROWS),
        in_specs=[blk, prev, wsp, bsp], out_specs=one,
        compiler_params=_cparams(("parallel", "parallel")), name=name,
    )(u, u, w, b)


def _ffn_act_bwd(u, dact, w, b, name):
    blk, prev, _, wsp, bsp, one = _ffn_specs()

    def body(u_ref, h_ref, da_ref, w_ref, b_ref, duc_ref, dwb_ref):
        i = pl.program_id(1)
        (a, a1, a2, a0), (g, g1, g2, g0) = _conv_parts(u_ref, h_ref, w_ref, b_ref, i == 0)
        dact_v = da_ref[...]
        sg = 1.0 / (1.0 + jnp.exp(-g))
        d_a = dact_v * (g * sg)
        d_g = dact_v * a * (sg * (1.0 + g * (1.0 - sg)))
        duc_ref[0] = d_a
        duc_ref[1] = d_g

        @pl.when(i == 0)
        def _():
            dwb_ref[...] = jnp.zeros(dwb_ref.shape, F32)

        for p, (d, s2, s1, s0) in enumerate(((d_a, a2, a1, a0), (d_g, g2, g1, g0))):
            dwb_ref[p, 0:1, :] += jnp.sum(d * s2, axis=0, keepdims=True)
            dwb_ref[p, 1:2, :] += jnp.sum(d * s1, axis=0, keepdims=True)
            dwb_ref[p, 2:3, :] += jnp.sum(d * s0, axis=0, keepdims=True)
            dwb_ref[p, 3:4, :] += jnp.sum(d, axis=0, keepdims=True)

    return pl.pallas_call(
        body, out_shape=[jax.ShapeDtypeStruct((2, S, FF), F32), jax.ShapeDtypeStruct((2, 8, FF), F32)],
        grid=(FF // FFN_COLS, S // FFN_ROWS), in_specs=[blk, prev, one, wsp, bsp],
        out_specs=[blk, pl.BlockSpec((2, 8, FFN_COLS), lambda j, i: (0, 0, j))],
        compiler_params=_cparams(("parallel", "arbitrary")), name=name,
    )(u, u, dact, w, b)


def _ffn_conv_bwd(duc, w, name):
    blk, _, nxt, wsp, _, _ = _ffn_specs()
    last = S // FFN_ROWS - 1

    def body(d_ref, h_ref, w_ref, du_ref):
        is_last = pl.program_id(1) == last
        for p in range(2):
            d = d_ref[p]
            halo = jnp.where(is_last, 0.0, h_ref[p])
            d1, d2 = _shifted(d, halo, False)
            wv = w_ref[p]
            du_ref[p] = (wv[2:3, :] * d + wv[1:2, :] * d1 + wv[0:1, :] * d2).astype(BF16)

    return pl.pallas_call(
        body, out_shape=jax.ShapeDtypeStruct((2, S, FF), BF16), grid=(FF // FFN_COLS, S // FFN_ROWS),
        in_specs=[blk, nxt, wsp], out_specs=blk,
        compiler_params=_cparams(("parallel", "parallel")), name=name,
    )(duc, duc, w)


def _adam_update(w, gv, m, v):
    c1 = 1.0 / (1.0 - ADAM_B1 ** ADAM_STEP)
    c2 = 1.0 / (1.0 - ADAM_B2 ** ADAM_STEP)
    mn = ADAM_B1 * m + (1.0 - ADAM_B1) * gv
    vn = ADAM_B2 * v + (1.0 - ADAM_B2) * (gv * gv)
    return -ADAM_LR * ((mn * c1) / (jnp.sqrt(vn * c2) + ADAM_EPS) + ADAM_WD * w), mn, vn


def _adamw(w, g, m, v, name):
    rows = w.shape[0]
    T = 8
    for cand in (256, 128, 64, 32, 16, 8):
        if rows % cand == 0:
            T = cand
            break

    def body(w_ref, g_ref, m_ref, v_ref, d_ref, mo_ref, vo_ref):
        d_ref[...], mo_ref[...], vo_ref[...] = _adam_update(w_ref[...], g_ref[...], m_ref[...], v_ref[...])

    blk = pl.BlockSpec((T, w.shape[1]), lambda i: (i, 0))
    sds = jax.ShapeDtypeStruct(w.shape, F32)
    return pl.pallas_call(
        body, out_shape=[sds, sds, sds], grid=(rows // T,), in_specs=[blk] * 4, out_specs=[blk] * 3,
        compiler_params=_cparams(("parallel",)), name=name,
    )(w, g, m, v)


ANY = pl.BlockSpec(memory_space=pl.ANY)


def _place():
    x, y, c = lax.axis_index("x"), lax.axis_index("y"), lax.axis_index("c")
    chips = [(1 - x, y), (x, 1 - y), (1 - x, 1 - y)]
    return x, y, c, chips


def _place_own(w, slot_arr, name):
    rows = w.shape[0]
    T = 16
    for cand in (2048, 1024, 512, 256, 128, 64, 32, 16):
        if rows % cand == 0:
            T = cand
            break

    def body(k_ref, w_ref, o_ref):
        o_ref[...] = w_ref[...]

    return pl.pallas_call(
        body, out_shape=jax.ShapeDtypeStruct((N_CHIPS, rows, FLAT_W), w.dtype),
        grid_spec=pltpu.PrefetchScalarGridSpec(
            num_scalar_prefetch=1, grid=(rows // T,),
            in_specs=[pl.BlockSpec((T, FLAT_W), lambda i, k: (i, 0))],
            out_specs=pl.BlockSpec((None, T, FLAT_W), lambda i, k: (k[0], i, 0))),
        compiler_params=_cparams(("parallel",)), name=name,
    )(slot_arr, w)


def _allgather_shards(w, buf):
    half_rows = w.shape[0] // 2
    assert half_rows % 16 == 0

    def body(w_ref, buf_ref, g_ref, send_sems, recv_sems):
        x, y, c, chips = _place()
        myk = 2 * x + y
        sibling = (x, y, 1 - c)
        h0 = pl.multiple_of(c * half_rows, 16)
        h1 = pl.multiple_of((1 - c) * half_rows, 16)

        def half(k, start):
            return g_ref.at[k, pl.ds(start, half_rows), :]

        def rcopy(sem, src, dst, to):
            return pltpu.make_async_remote_copy(src_ref=src, dst_ref=dst, send_sem=send_sems.at[sem],
                                                recv_sem=recv_sems.at[sem], device_id=to, device_id_type=MESH)

        ici = [rcopy(r, w_ref.at[pl.ds(h0, half_rows), :], half(myk, h0), (*chip, c)) for r, chip in enumerate(chips)]
        for cp in ici:
            cp.start()
        ks = [2 * cx + cy for cx, cy in chips]
        fwd = [rcopy(3 + r, half(ks[r], h0), half(ks[r], h0), sibling) for r in range(3)]
        for r in range(3):
            rcopy(r, half(ks[r], h0), half(ks[r], h0), (*chips[r], c)).wait_recv()
            fwd[r].start()
        for r in range(3):
            rcopy(3 + r, half(ks[r], h1), half(ks[r], h1), sibling).wait_recv()
        for cp in ici + fwd:
            cp.wait_send()

    return pl.pallas_call(
        body, out_shape=jax.ShapeDtypeStruct(buf.shape, w.dtype), in_specs=[ANY, ANY], out_specs=ANY,
        scratch_shapes=[pltpu.SemaphoreType.DMA((6,)), pltpu.SemaphoreType.DMA((6,))],
        input_output_aliases={1: 0},
        compiler_params=pltpu.CompilerParams(has_side_effects=True), name="allgather_shards",
    )(w, buf)


HBM_SPEC = pl.BlockSpec(memory_space=pltpu.HBM)
SEM_SPEC = pl.BlockSpec(memory_space=pltpu.SEMAPHORE)
DATAFLOW = pltpu.SideEffectType.DATAFLOW_SIDE_EFFECTING
OWN_SLOT = 3


def _late_gather_start(w, land):
    def body(w_ref, land_ref, send_sems, recv_sems, w_thru, land_thru, token):
        x, y, c, chips = _place()
        for r, chip in enumerate(chips):
            pltpu.make_async_remote_copy(src_ref=w_ref, dst_ref=land_ref.at[r], send_sem=send_sems.at[r],
                                         recv_sem=recv_sems.at[r], device_id=(*chip, c), device_id_type=MESH).start()
        token[...] = jnp.zeros_like(token)

    return pl.pallas_call(
        body, name="late_gather_start",
        out_shape=(pltpu.SemaphoreType.DMA((3,)), pltpu.SemaphoreType.DMA((3,)), pltpu.HBM(w.shape, w.dtype),
                   pltpu.HBM(land.shape, land.dtype), jax.ShapeDtypeStruct((8, LANES), F32)),
        in_specs=(HBM_SPEC, HBM_SPEC),
        out_specs=(SEM_SPEC, SEM_SPEC, HBM_SPEC, HBM_SPEC, pl.BlockSpec(memory_space=pltpu.VMEM)),
        input_output_aliases={0: 2, 1: 3}, compiler_params=pltpu.CompilerParams(has_side_effects=DATAFLOW),
    )(pltpu.with_memory_space_constraint(w, pltpu.HBM), pltpu.with_memory_space_constraint(land, pltpu.HBM))


def _late_gather_wait(send_sems, recv_sems, w_thru, land_thru, after):
    def body(w_ref, land_ref, send_sems, recv_sems, after_ref, w_dead, got_ref):
        x, y, c, chips = _place()
        for r, chip in enumerate(chips):
            cp = pltpu.make_async_remote_copy(src_ref=w_ref, dst_ref=land_ref.at[r], send_sem=send_sems.at[r],
                                              recv_sem=recv_sems.at[r], device_id=(*chip, c), device_id_type=MESH)
            cp.wait_send()
            cp.wait_recv()

    return pl.pallas_call(
        body, name="late_gather_wait",
        out_shape=(pltpu.HBM(w_thru.shape, w_thru.dtype), pltpu.HBM(land_thru.shape, land_thru.dtype)),
        in_specs=(HBM_SPEC, HBM_SPEC, SEM_SPEC, SEM_SPEC, pl.BlockSpec(memory_space=pl.ANY)),
        out_specs=(HBM_SPEC, HBM_SPEC), input_output_aliases={0: 0, 1: 1},
        compiler_params=pltpu.CompilerParams(has_side_effects=DATAFLOW),
    )(w_thru, land_thru, send_sems, recv_sems, after)


def _flat_tile(rows):
    return next(t for t in (2048, 1024, 512, 256, 128, 64, 32, 16) if rows % t == 0)


def _sibling_swap_half(g, tag):
    half = g.shape[1] // 2

    def body(g_ref, o_ref, send_sem, recv_sem):
        x, y, c, _ = _place()
        theirs = pl.multiple_of((1 - c) * half, 8)
        cp = pltpu.make_async_remote_copy(src_ref=g_ref.at[:, pl.ds(theirs, half), :], dst_ref=o_ref,
                                          send_sem=send_sem, recv_sem=recv_sem, device_id=(x, y, 1 - c),
                                          device_id_type=MESH)
        cp.start()
        cp.wait()

    return pl.pallas_call(
        body, out_shape=jax.ShapeDtypeStruct((N_CHIPS, half, FLAT_W), F32), in_specs=[ANY], out_specs=ANY,
        scratch_shapes=[pltpu.SemaphoreType.DMA, pltpu.SemaphoreType.DMA],
        compiler_params=pltpu.CompilerParams(has_side_effects=True), name=f"rs_sibling_swap_{tag}",
    )(g)


def _pair_sum(g, other, c_arr, tag):
    half = other.shape[1]
    T = _flat_tile(half)

    def body(c_ref, g_ref, o_ref, s_ref):
        s_ref[...] = (g_ref[...] + o_ref[...]).astype(BF16)

    nb = half // T
    return pl.pallas_call(
        body, out_shape=jax.ShapeDtypeStruct((N_CHIPS, half, FLAT_W), BF16),
        grid_spec=pltpu.PrefetchScalarGridSpec(
            num_scalar_prefetch=1, grid=(N_CHIPS, nb),
            in_specs=[pl.BlockSpec((None, T, FLAT_W), lambda k, i, c: (k, c[0] * nb + i, 0)),
                      pl.BlockSpec((None, T, FLAT_W), lambda k, i, c: (k, i, 0))],
            out_specs=pl.BlockSpec((None, T, FLAT_W), lambda k, i, c: (k, i, 0))),
        compiler_params=_cparams(("parallel", "parallel")), name=f"rs_pair_sum_{tag}",
    )(c_arr, g, other)


def _chip_exchange(s, tag):
    def body(s_ref, o_ref, send_sems, recv_sems):
        x, y, c, chips = _place()
        cps = []
        for r, (cx, cy) in enumerate(chips):
            cps.append(pltpu.make_async_remote_copy(
                src_ref=s_ref.at[2 * cx + cy], dst_ref=o_ref.at[r], send_sem=send_sems.at[r],
                recv_sem=recv_sems.at[r], device_id=(cx, cy, c), device_id_type=MESH))
        for cp in cps:
            cp.start()
        for cp in cps:
            cp.wait()

    return pl.pallas_call(
        body, out_shape=jax.ShapeDtypeStruct((3,) + s.shape[1:], BF16), in_specs=[ANY], out_specs=ANY,
        scratch_shapes=[pltpu.SemaphoreType.DMA((3,)), pltpu.SemaphoreType.DMA((3,))],
        compiler_params=pltpu.CompilerParams(has_side_effects=True), name=f"rs_chip_exchange_{tag}",
    )(s)


def _chip_exchange_start(s, land):
    def body(s_ref, land_ref, send_sems, recv_sems, s_thru, land_thru, token):
        x, y, c, chips = _place()
        for r, (cx, cy) in enumerate(chips):
            pltpu.make_async_remote_copy(src_ref=s_ref.at[2 * cx + cy], dst_ref=land_ref.at[r], send_sem=send_sems.at[r],
                                         recv_sem=recv_sems.at[r], device_id=(cx, cy, c), device_id_type=MESH).start()
        token[...] = jnp.zeros_like(token)

    return pl.pallas_call(
        body, name="rs_exchange_start",
        out_shape=(pltpu.SemaphoreType.DMA((3,)), pltpu.SemaphoreType.DMA((3,)), pltpu.HBM(s.shape, s.dtype),
                   pltpu.HBM(land.shape, land.dtype), jax.ShapeDtypeStruct((8, LANES), F32)),
        in_specs=(HBM_SPEC, HBM_SPEC),
        out_specs=(SEM_SPEC, SEM_SPEC, HBM_SPEC, HBM_SPEC, pl.BlockSpec(memory_space=pltpu.VMEM)),
        input_output_aliases={0: 2, 1: 3}, compiler_params=pltpu.CompilerParams(has_side_effects=DATAFLOW),
    )(pltpu.with_memory_space_constraint(s, pltpu.HBM), pltpu.with_memory_space_constraint(land, pltpu.HBM))


def _chip_exchange_wait(send_sems, recv_sems, s_thru, land_thru, after):
    def body(s_ref, land_ref, send_sems, recv_sems, after_ref, s_done, got_ref):
        x, y, c, chips = _place()
        for r, (cx, cy) in enumerate(chips):
            cp = pltpu.make_async_remote_copy(src_ref=s_ref.at[2 * cx + cy], dst_ref=land_ref.at[r],
                                              send_sem=send_sems.at[r], recv_sem=recv_sems.at[r], device_id=(cx, cy, c),
                                              device_id_type=MESH)
            cp.wait_send()
            cp.wait_recv()

    return pl.pallas_call(
        body, name="rs_exchange_wait",
        out_shape=(pltpu.HBM(s_thru.shape, s_thru.dtype), pltpu.HBM(land_thru.shape, land_thru.dtype)),
        in_specs=(HBM_SPEC, HBM_SPEC, SEM_SPEC, SEM_SPEC, pl.BlockSpec(memory_space=pl.ANY)),
        out_specs=(HBM_SPEC, HBM_SPEC), input_output_aliases={0: 0, 1: 1},
        compiler_params=pltpu.CompilerParams(has_side_effects=DATAFLOW),
    )(s_thru, land_thru, send_sems, recv_sems, after)


def _chip_sum(s, r, k_arr, tag):
    half = s.shape[1]
    T = _flat_tile(half)

    def body(k_ref, s_ref, r_ref, o_ref):
        o_ref[...] = ((s_ref[...].astype(F32) + r_ref[0].astype(F32)) + r_ref[1].astype(F32)) + r_ref[2].astype(F32)

    return pl.pallas_call(
        body, out_shape=jax.ShapeDtypeStruct((half, FLAT_W), F32),
        grid_spec=pltpu.PrefetchScalarGridSpec(
            num_scalar_prefetch=1, grid=(half // T,),
            in_specs=[pl.BlockSpec((None, T, FLAT_W), lambda i, k: (k[0], i, 0)),
                      pl.BlockSpec((3, T, FLAT_W), lambda i, k: (0, i, 0))],
            out_specs=pl.BlockSpec((T, FLAT_W), lambda i, k: (i, 0))),
        compiler_params=_cparams(("parallel",)), name=f"rs_chip_sum_{tag}",
    )(k_arr, s, r)


def _sibling_send(t, tag):
    def body(t_ref, o_ref, send_sem, recv_sem):
        x, y, c, _ = _place()
        cp = pltpu.make_async_remote_copy(src_ref=t_ref, dst_ref=o_ref, send_sem=send_sem, recv_sem=recv_sem,
                                          device_id=(x, y, 1 - c), device_id_type=MESH)
        cp.start()
        cp.wait()

    return pl.pallas_call(
        body, out_shape=jax.ShapeDtypeStruct(t.shape, F32), in_specs=[ANY], out_specs=ANY,
        scratch_shapes=[pltpu.SemaphoreType.DMA, pltpu.SemaphoreType.DMA],
        compiler_params=pltpu.CompilerParams(has_side_effects=True), name=f"rs_sibling_send_{tag}",
    )(t)


def _allreduce_small(v):
    def body(v_ref, o_ref, buf, send_sems, recv_sems):
        x, y, c, _ = _place()
        me = 4 * x + 2 * y + c
        buf[me] = v_ref[...]
        cps = []
        for mask in range(1, 8):
            a, b, d = (mask >> 2) & 1, (mask >> 1) & 1, mask & 1
            peer = (x + a - 2 * a * x, y + b - 2 * b * y, c + d - 2 * d * c)
            cps.append(pltpu.make_async_remote_copy(
                src_ref=v_ref, dst_ref=buf.at[me], send_sem=send_sems.at[mask - 1], recv_sem=recv_sems.at[mask - 1],
                device_id=peer, device_id_type=MESH))
        for cp in cps:
            cp.start()
        for cp in cps:
            cp.wait()
        total = buf[0]
        for dev in range(1, 8):
            total = total + buf[dev]
        o_ref[...] = total

    vm = pl.BlockSpec(memory_space=pltpu.VMEM)
    return pl.pallas_call(
        body, out_shape=jax.ShapeDtypeStruct((SMALL_ROWS, 1024), F32), in_specs=[vm], out_specs=vm,
        scratch_shapes=[pltpu.VMEM((8, SMALL_ROWS, 1024), F32), pltpu.SemaphoreType.DMA((7,)),
                        pltpu.SemaphoreType.DMA((7,))],
        compiler_params=pltpu.CompilerParams(has_side_effects=True), name="allreduce_small",
    )(v)


def _col_to_row(t):
    return t.reshape(t.shape[0], 1, S)


def _residue_rows(t, d, inverse=False):
    if d == 1:
        return t
    shape = (d, S // d) if inverse else (S // d, d)
    return t.reshape(shape + t.shape[1:]).transpose(1, 0, 2).reshape(t.shape)


def _residue_vecs(t, d, inverse=False):
    if d == 1:
        return t
    shape = (d, S // d) if inverse else (S // d, d)
    return t.reshape((B_HPG,) + shape).transpose(0, 2, 1).reshape(B_HPG, S, 1)


def _ffn_fwd(x, g, w_up, cw, cb, w_down, tag):
    h = _rms_fwd(x, g, f"{tag}_norm")
    u = _mm(h, w_up, mode="nn", tm=1024, tn=1408, tk=1024, o_split=2, name=f"{tag}_up")
    act = _ffn_act_fwd(u, cw, cb, f"{tag}_act")
    x_out = _mm(act, w_down, mode="nn", tm=1024, tn=512, tk=FF, res=x, name=f"{tag}_down")
    return x_out, (h, u, act)


def _ffn_bwd(x, g, w_up, cw, cb, w_down, saved, dx, dxb, tag):
    h, u, act = saved
    d_w_down = _mm(act, dxb, mode="tn", tm=1408, tn=512, tk=1024, name=f"{tag}_dwdown")
    dact = _mm(dxb, w_down, mode="nt", tm=1024, tn=1408, tk=1024, name=f"{tag}_dact")
    duc, dwb = _ffn_act_bwd(u, dact, cw, cb, f"{tag}_dgate")
    du = _ffn_conv_bwd(duc, cw, f"{tag}_dconv")
    d_w_up = _mm(h, du, mode="tn", tm=1024, tn=1408, tk=1024, b_split=2, name=f"{tag}_dwup")
    dh = _mm(du, w_up, mode="nt", tm=1024, tn=512, tk=1408, a_split=2, name=f"{tag}_dh")
    dx_new, dxb_new, (dg,) = _rms_bwd(x, dx, [(g, dh)], f"{tag}_dnorm")
    d_cw = dwb[:, 0:3, :].transpose(1, 0, 2).reshape(3, 2 * FF)
    d_cb = dwb[:, 3, :].reshape(2 * FF)
    return dx_new, dxb_new, dict(w_up=d_w_up, w_down=d_w_down, conv_w=d_cw, conv_b=d_cb, norm_g=dg.reshape(D))


def _local_step(x, target, p, late_weights, late_grads_ready):
    g = {}
    h1 = _rms_fwd(x, p["mix_norm_g"][0], "a_norm")
    w_qkv = p["a_w_in"][:, :QKV_W]
    w_f = jnp.pad(p["a_w_in"][:, QKV_W:], ((0, 0), (0, LANES - A_HEADS)))
    b_f = jnp.pad(p["a_b_f"].reshape(1, A_HEADS), ((0, 0), (0, LANES - A_HEADS)))
    qkv = _mm(h1, w_qkv, mode="nn", tm=1024, tn=512, tk=1024, out_dtype=BF16, name="a_qkv")
    pf = _mm(h1, w_f, mode="nn", tm=1024, tn=LANES, tk=1024, name="a_gate")
    cum = _fgate_fwd(pf, b_f, "a_gate_scan")
    aug_q, aug_k = _gate_lanes(cum)
    oa2, lse_a = _fox_pair_fwd(qkv, aug_q, aug_k, "a_attn")
    x1 = _mm(oa2, p["a_w_out"], mode="nn", tm=1024, tn=512, tk=1024, res=x, name="a_out")
    p = {**p, **late_weights(x1)}
    x2, ffn0 = _ffn_fwd(x1, p["ffn_norm_g"][0], p["ffn_w_up"][0], p["conv_w"][0], p["conv_b"][0], p["ffn_w_down"][0], "f0")
    hk = _rms_fwd(x2, p["kv_norm_g"], "kv_norm")
    kvb = _mm(hk, p["w_kv"], mode="nn", tm=1024, tn=512, tk=1024, out_dtype=BF16, name="kv_proj")
    h3 = _rms_fwd(x2, p["mix_norm_g"][1], "b_norm")
    qb = _mm(h3, p["b_w_q"], mode="nn", tm=1024, tn=512, tk=1024, out_dtype=BF16, name="b_q")
    dil_in = []
    for gi, (_, d) in enumerate(B_GROUPS):
        if d == 1:
            dil_in.append((qb, kvb, kvb, gi * NPG, gi * NPG, (3 + gi) * NPG))
        else:
            qg = _residue_rows(qb[:, gi * GROUP_W:(gi + 1) * GROUP_W], d)
            kvg = _residue_rows(kvb.reshape(S, 2, 3, GROUP_W)[:, :, gi, :].reshape(S, 2 * GROUP_W), d)
            dil_in.append((qg, kvg, kvg, 0, 0, NPG))
    o_g, lse_g = [], []
    for gi, (_, d) in enumerate(B_GROUPS):
        qg, kg, vg, qoff, koff, voff = dil_in[gi]
        og, lg = _dil_pair_fwd(gi, qg, kg, vg, qoff, koff, voff, f"b_attn{gi}")
        o_g.append(_residue_rows(og, d, inverse=True))
        lse_g.append(_residue_vecs(lg, d, inverse=True))
    ob, ob2, lse_b = _dil_pair_merge(o_g, lse_g, "b_merge")
    x3 = _mm(ob2, p["b_w_out"], mode="nn", tm=1024, tn=512, tk=B_OUT_W, res=x2, name="b_out")
    x4, ffn1 = _ffn_fwd(x3, p["ffn_norm_g"][1], p["ffn_w_up"][1], p["conv_w"][1], p["conv_b"][1], p["ffn_w_down"][1], "f1")
    loss, dx, dxb, dg_final = _loss_head(x4, p["final_norm_g"], target, "loss_head")
    g["final_norm_g"] = dg_final.reshape(D)

    dx, dxb, gf1 = _ffn_bwd(x3, p["ffn_norm_g"][1], p["ffn_w_up"][1], p["conv_w"][1], p["conv_b"][1], p["ffn_w_down"][1],
                            ffn1, dx, dxb, "f1")
    g["b_w_out"] = _mm(ob2, dxb, mode="tn", tm=B_OUT_W, tn=512, tk=1024, name="b_dwout")
    dob = _mm(dxb, p["b_w_out"], mode="nt", tm=1024, tn=B_OUT_W, tk=1024, name="b_do")
    delta_b = _pair_rowdot(dob, ob, "b_delta")
    dob16 = dob.astype(BF16)
    dq_g, dk_g, dv_g = [], [], []
    for gi, (_, d) in enumerate(B_GROUPS):
        qg, kg, vg, qoff, koff, voff = dil_in[gi]
        dog, l_d, dl_d = _residue_rows(dob16, d), _residue_vecs(lse_b, d), _residue_vecs(delta_b, d)
        dqd = _dil_pair_dq(gi, qg, kg, vg, qoff, koff, voff, dog, l_d, dl_d, f"b_dq{gi}")
        dkd, dvd = _dil_pair_dkv(gi, qg, kg, vg, qoff, koff, voff, dog, _col_to_row(l_d), _col_to_row(dl_d), f"b_dkv{gi}")
        dq_g.append(_residue_rows(dqd, d, inverse=True))
        dk_g.append(_residue_rows(dkd, d, inverse=True))
        dv_g.append(_residue_rows(dvd, d, inverse=True))
    dqb = jnp.concatenate(dq_g, axis=1)
    dkvb = jnp.concatenate(dk_g + dv_g, axis=1)
    g["b_w_q"] = _mm(h3, dqb, mode="tn", tm=1024, tn=512, tk=1024, name="b_dwq")
    dh3 = _mm(dqb, p["b_w_q"], mode="nt", tm=1024, tn=512, tk=B_Q_W, name="b_dh")
    g["w_kv"] = _mm(hk, dkvb, mode="tn", tm=1024, tn=512, tk=1024, name="kv_dw")
    dhk = _mm(dkvb, p["w_kv"], mode="nt", tm=1024, tn=512, tk=1536, name="kv_dh")
    dx, dxb, (dg_mix1, dg_kv) = _rms_bwd(x2, dx, [(p["mix_norm_g"][1], dh3), (p["kv_norm_g"], dhk)], "b_dnorm")
    g["kv_norm_g"] = dg_kv.reshape(D)
    dx, dxb, gf0 = _ffn_bwd(x1, p["ffn_norm_g"][0], p["ffn_w_up"][0], p["conv_w"][0], p["conv_b"][0], p["ffn_w_down"][0],
                            ffn0, dx, dxb, "f0")
    g["ffn_w_up"] = jnp.stack([gf0["w_up"], gf1["w_up"]])
    g["ffn_w_down"] = jnp.stack([gf0["w_down"], gf1["w_down"]])
    g["ffn_conv_w"] = jnp.stack([gf0["conv_w"], gf1["conv_w"]])
    token = late_grads_ready(g)
    a_w_out_t = p["a_w_out"] + token[0, 0].astype(BF16)
    g["a_w_out"] = _mm(oa2, dxb, mode="tn", tm=1024, tn=512, tk=1024, name="a_dwout")
    doa = _mm(dxb, a_w_out_t, mode="nt", tm=1024, tn=512, tk=1024, name="a_do")
    delta_a = _pair_rowdot(doa, oa2, "a_delta")
    dqa, dka, dva, dck, dcq = _fox_pair_bwd(qkv, doa, _col_to_row(lse_a), _col_to_row(delta_a), aug_q, aug_k, "a_dattn")
    dqkv = jnp.concatenate([dqa, dka, dva], axis=1)
    pad_heads = lambda t: jnp.pad(t.reshape(A_HEADS, S).T, ((0, 0), (0, LANES - A_HEADS)))
    dpf, db_f = _fgate_bwd(pf, b_f, pad_heads(dck), pad_heads(dcq), "a_dgate_scan")
    g["a_b_f"] = db_f[:, :A_HEADS]
    d_w_qkv = _mm(h1, dqkv, mode="tn", tm=1024, tn=512, tk=1024, name="a_dwqkv")
    d_w_f = _mm(h1, dpf, mode="tn", tm=1024, tn=LANES, tk=1024, name="a_dwgate")
    g["a_w_in"] = jnp.concatenate([d_w_qkv, d_w_f[:, :A_HEADS]], axis=1)
    dh1 = _mm(dqkv, w_qkv, mode="nt", tm=1024, tn=512, tk=1536, name="a_dh")
    dh1 = _mm(dpf, w_f, mode="nt", tm=1024, tn=512, tk=LANES, res=dh1, name="a_dh_gate")
    dx, _, (dg_mix0,) = _rms_bwd(x, dx, [(p["mix_norm_g"][0], dh1)], "a_dnorm")

    g["mix_norm_g"] = jnp.stack([dg_mix0.reshape(D), dg_mix1.reshape(D)])
    g["ffn_norm_g"] = jnp.stack([gf0["norm_g"], gf1["norm_g"]])
    g["ffn_conv_b"] = jnp.stack([gf0["conv_b"], gf1["conv_b"]])
    return loss[0, 0], dx, g


_SHARD_SHAPES = {"a_w_in": (1, 1024, 772), "a_w_out": (1, 256, 1024), "b_w_q": (1, 1024, 384), "b_w_out": (1, 512, 256),
                 "w_kv": (1024, 768), "ffn_w_up": (2, 1024, 1408), "ffn_w_down": (2, 704, 1024), "ffn_conv_w": (2, 3, 1408)}
_SMALL = (("kv_norm_g", (1024,)), ("mix_norm_g", (2, 1024)), ("ffn_norm_g", (2, 1024)), ("final_norm_g", (1024,)),
          ("a_b_f", (1, 16)), ("ffn_conv_b", (2, 5632)))


def _unslabs(rows, L, R, C, rpad):
    nc = -(-C // FLAT_W)
    return rows.reshape(L, nc, rpad, FLAT_W).transpose(0, 2, 1, 3).reshape(L, rpad, nc * FLAT_W)[:, :R, :C]


_SEG_RT = {"ffn_w_down": 704, "a_w_in": 1024, "a_w_out": 256, "b_w_q": 1024, "b_w_out": 512, "w_kv": 1024,
           "ffn_w_up": 1024, "ffn_conv_w": 16}
_ROW_SHARDED = ("a_w_out", "ffn_w_down")


_LAYOUTS = {"early": ("a_w_in", "a_w_out"), "late": ("ffn_w_down", "b_w_q", "b_w_out", "w_kv", "ffn_w_up", "ffn_conv_w"),
            "grad_early": ("a_w_in", "a_w_out"),
            "grad_late": ("ffn_w_up", "ffn_w_down", "b_w_q", "w_kv", "b_w_out", "ffn_conv_w")}
_GRAD_ROWS = {"grad_early": 10240, "grad_late": 45056}


def _layout_rows(layout):
    used = sum(_seg_rows(*s) for s in _SEGS if s[0] in _LAYOUTS[layout])
    rows = _GRAD_ROWS.get(layout, used)
    assert rows >= used
    return rows


def _grad_layout(name):
    return "grad_early" if name in _LAYOUTS["grad_early"] else "grad_late"


def _seg(name, layout=None):
    layout = layout or _grad_layout(name)
    off = 0
    for s in sorted((s for s in _SEGS if s[0] in _LAYOUTS[layout]), key=lambda s: _LAYOUTS[layout].index(s[0])):
        _, L, R, C, rpad = s
        if layout in _GRAD_ROWS:
            per_layer = -(-C // FLAT_W) * rpad
            off = -(-off // per_layer) * per_layer
        if s[0] == name:
            rt = _SEG_RT[name]
            assert off % rt == 0 and rpad % rt == 0
            half = _layout_rows(layout) // 2
            assert off + _seg_rows(*s) <= 2 * half
            assert layout not in _GRAD_ROWS or half % rt == 0 or off + _seg_rows(*s) <= half
            return dict(L=L, R=R, C=C, rpad=rpad, nc=-(-C // FLAT_W), rt=rt, off=off, ni=rpad // rt, half=half)
        off += _seg_rows(*s)
    raise KeyError(name)


def _flat_block(sg, term=0):
    base = (sg["off"] + term * sg["L"] * sg["nc"] * sg["rpad"]) // sg["rt"]
    return lambda l, j, i: base + (l * sg["nc"] + j) * sg["ni"] + i


def _native3(t, name):
    sg = _seg(name)
    t = t.reshape(sg["L"], sg["R"], sg["C"])
    return jnp.pad(t, ((0, 0), (0, sg["rpad"] - sg["R"]), (0, 0))) if sg["rpad"] != sg["R"] else t


def _slab_pack(flat, t, name, layout, term=None):
    sg = _seg(name, layout)
    rt = sg["rt"]
    rb = _flat_block(sg, term or 0)

    def body(*refs):
        t_ref, o_ref = refs[-2], refs[-1]
        val = t_ref[...]
        o_ref[...] = val.astype(BF16) if term is None else _split3(val)[term]

    in_specs = [pl.BlockSpec((None, rt, FLAT_W), lambda l, j, i: (l, i, j))]
    args = [t]
    if flat is not None:
        in_specs, args = [ANY] + in_specs, [flat] + args
    return pl.pallas_call(
        body, out_shape=jax.ShapeDtypeStruct((_layout_rows(layout), FLAT_W), BF16), grid=(sg["L"], sg["nc"], sg["ni"]),
        in_specs=in_specs, out_specs=pl.BlockSpec((rt, FLAT_W), lambda l, j, i: (rb(l, j, i), 0)),
        input_output_aliases={0: 0} if flat is not None else {},
        compiler_params=_cparams(("parallel", "parallel", "parallel")), name=f"pack_{name}_{term or 0}",
    )(*args)


def _full_spec(sg, name):
    rt, nc, ni = sg["rt"], sg["nc"], sg["ni"]
    if name in _ROW_SHARDED:
        return (sg["L"], N_CHIPS * sg["R"], sg["C"]), pl.BlockSpec((None, rt, FLAT_W), lambda k, l, j, i: (l, k * ni + i, j))
    return ((sg["L"], sg["rpad"], N_CHIPS * nc * FLAT_W),
            pl.BlockSpec((None, rt, FLAT_W), lambda k, l, j, i: (l, i, k * nc + j)))


def _slab_unpack(gathered, slots, name, layout, own=None):
    sg = _seg(name, layout)
    rb = _flat_block(sg)
    shape, _ = _full_spec(sg, name)
    rt, nc, ni = sg["rt"], sg["nc"], sg["ni"]
    width = nc * FLAT_W
    last = gathered.shape[0] - 1

    def body(*refs):
        s_ref, o_ref = refs[0], refs[-1]
        is_own = s_ref[pl.program_id(0)] == OWN_SLOT
        for j in range(nc):
            val = refs[1 + j][...]
            if own is not None:
                val = jnp.where(is_own, refs[1 + nc + j][...], val)
            o_ref[:, j * FLAT_W:(j + 1) * FLAT_W] = val

    if name in _ROW_SHARDED:
        o_spec = pl.BlockSpec((None, rt, width), lambda k, l, i, s: (l, k * ni + i, 0))
    else:
        o_spec = pl.BlockSpec((None, rt, width), lambda k, l, i, s: (l, i, k))
    in_specs = [pl.BlockSpec((None, rt, FLAT_W), lambda k, l, i, s, j=j: (jnp.minimum(s[k], last), rb(l, j, i), 0))
                for j in range(nc)]
    args = [gathered] * nc
    if own is not None:
        in_specs += [pl.BlockSpec((rt, FLAT_W), lambda k, l, i, s, j=j: (rb(l, j, i), 0)) for j in range(nc)]
        args += [own] * nc
    return pl.pallas_call(
        body, out_shape=jax.ShapeDtypeStruct(shape, BF16),
        grid_spec=pltpu.PrefetchScalarGridSpec(num_scalar_prefetch=1, grid=(N_CHIPS, sg["L"], ni), in_specs=in_specs,
                                               out_specs=o_spec),
        compiler_params=_cparams(("parallel",) * 3), name=f"unpack_{name}",
    )(slots, *args)


def _slab_pack_grad(flat4, g, name):
    sg = _seg(name)
    rows = _layout_rows(_grad_layout(name))
    shape, _ = _full_spec(sg, name)
    assert g.shape == shape, (name, g.shape, shape)
    rt, nc = sg["rt"], sg["nc"]
    assert sg["ni"] == 1 and sg["off"] % (nc * rt) == 0
    base = sg["off"] // (nc * rt)

    def body(*refs):
        g_ref, o_ref = refs[-2], refs[-1]
        for j in range(nc):
            o_ref[j * rt:(j + 1) * rt, :] = g_ref[:, j * FLAT_W:(j + 1) * FLAT_W]

    if name in _ROW_SHARDED:
        spec = pl.BlockSpec((None, rt, nc * FLAT_W), lambda k, l: (l, k, 0))
    else:
        spec = pl.BlockSpec((None, rt, nc * FLAT_W), lambda k, l: (l, 0, k))
    in_specs, args = [spec], [g]
    if flat4 is not None:
        in_specs, args = [pl.BlockSpec(memory_space=pl.ANY)] + in_specs, [flat4] + args
    return pl.pallas_call(
        body, out_shape=jax.ShapeDtypeStruct((N_CHIPS, rows, FLAT_W), F32), grid=(N_CHIPS, sg["L"]),
        in_specs=in_specs, out_specs=pl.BlockSpec((None, nc * rt, FLAT_W), lambda k, l: (k, base + l, 0)),
        input_output_aliases={0: 0} if flat4 is not None else {},
        compiler_params=_cparams(("parallel",) * 2), name=f"packgrad_{name}",
    )(*args)


def _adamw_shard(w, m, v, g_mine, g_other, c_arr, name):
    sg = _seg(name)
    rt = sg["rt"]
    rb = _flat_block(sg)
    per_half = sg["half"] // rt

    def half_of(l, j, i):
        return (rb(l, j, i) * rt) // sg["half"]

    def body(c_ref, w_ref, m_ref, v_ref, gm_ref, go_ref, g_ref, d_ref, mo_ref, vo_ref):
        is_mine = half_of(pl.program_id(0), pl.program_id(1), pl.program_id(2)) == c_ref[0]
        gv = jnp.where(is_mine, gm_ref[...], go_ref[...])
        g_ref[...] = gv
        d_ref[...], mo_ref[...], vo_ref[...] = _adam_update(w_ref[...], gv, m_ref[...], v_ref[...])

    nat = pl.BlockSpec((None, rt, FLAT_W), lambda l, j, i, c: (l, i, j))
    half = pl.BlockSpec((rt, FLAT_W), lambda l, j, i, c: (rb(l, j, i) - half_of(l, j, i) * per_half, 0))
    sds = jax.ShapeDtypeStruct(w.shape, F32)
    return pl.pallas_call(
        body, out_shape=[sds] * 4,
        grid_spec=pltpu.PrefetchScalarGridSpec(num_scalar_prefetch=1, grid=(sg["L"], sg["nc"], sg["ni"]),
                                               in_specs=[nat, nat, nat, half, half], out_specs=[nat] * 4),
        compiler_params=_cparams(("parallel", "parallel", "parallel")), name=f"adamw_{name}",
    )(c_arr, w, m, v, g_mine, g_other)


def _pack_small(vals, loss=None):
    parts = [vals[name].astype(F32).reshape(-1) for name, _ in _SMALL]
    if loss is not None:
        parts.append(loss.reshape(1))
    flat = jnp.concatenate(parts)
    return jnp.pad(flat, (0, SMALL_ROWS * 1024 - flat.shape[0])).reshape(SMALL_ROWS, 1024)


def _unpack_small(flat):
    flat = flat.reshape(-1)
    out = {}
    o = 0
    for name, shape in _SMALL:
        n = int(np.prod(shape))
        out[name] = flat[o:o + n].reshape(shape)
        o += n
    return out, flat[o]


_BIG = ("a_w_in", "a_w_out", "b_w_q", "b_w_out", "w_kv", "ffn_w_up", "ffn_w_down", "ffn_conv_w")
A_IN_PAD = 896


def _pack_weights(w, layout):
    flat = None
    for name in _LAYOUTS[layout]:
        t = _native3(w[name], name)
        for term in ((0, 1, 2) if name == "ffn_conv_w" else (None,)):
            flat = _slab_pack(flat, t, name, layout, term)
    return flat


def _early_weights(gathered, slots):
    a_in = _slab_unpack(gathered, slots, "a_w_in", "early")
    a_in = a_in.reshape(D, N_CHIPS, A_IN_PAD)[:, :, :772].reshape(D, N_CHIPS * 772)
    return dict(a_w_in=a_in, a_w_out=_slab_unpack(gathered, slots, "a_w_out", "early")[0])


def _late_weights(landed, slots, own):
    full = {name: _slab_unpack(landed, slots, name, "late", own) for name in _LAYOUTS["late"] if name != "ffn_conv_w"}
    sg = _seg("ffn_conv_w", "late")
    n1 = sg["nc"] * sg["rpad"]
    conv = slice(sg["off"], sg["off"] + CONV_TERMS * n1)
    conv_rows = jnp.concatenate([landed[:, conv], own[None, conv]], axis=0)
    per_chip = []
    for k in range(N_CHIPS):
        rows = lax.dynamic_index_in_dim(conv_rows, slots[k], axis=0, keepdims=False)
        terms = [_unslabs(rows[i * n1:(i + 1) * n1], 1, sg["R"], sg["C"], sg["rpad"]).astype(F32) for i in range(CONV_TERMS)]
        per_chip.append((terms[0] + terms[1]) + terms[2])
    cw = jnp.concatenate(per_chip, axis=2).reshape(2, 3, 2, FF).transpose(0, 2, 1, 3)
    return dict(b_w_q=full["b_w_q"][0], b_w_out=full["b_w_out"][0], w_kv=full["w_kv"][0], ffn_w_up=full["ffn_w_up"],
                ffn_w_down=full["ffn_w_down"], conv_w=cw)


def _shard_grads(g, layout):
    def full(name):
        if name == "a_w_in":
            a_in = jnp.pad(g[name].reshape(D, N_CHIPS, 772), ((0, 0), (0, 0), (0, A_IN_PAD - 772)))
            return a_in.reshape(1, D, N_CHIPS * A_IN_PAD)
        if name == "ffn_conv_w":
            sgc = _seg(name)
            return jnp.pad(g[name].reshape(1, sgc["R"], 2 * FF), ((0, 0), (0, sgc["rpad"] - sgc["R"]), (0, 0)))
        return g[name] if g[name].ndim == 3 else g[name][None]

    flat4 = None
    for name in _LAYOUTS[layout]:
        flat4 = _slab_pack_grad(flat4, full(name), name)
    return flat4


_WEIGHTS = ["a_w_in", "a_b_f", "a_w_out", "b_w_q", "b_w_out", "kv_norm_g", "w_kv", "mix_norm_g", "ffn_norm_g", "ffn_w_up",
            "ffn_conv_w", "ffn_conv_b", "ffn_w_down", "final_norm_g"]


def kernel(x, a_w_in, a_b_f, a_w_out, b_w_q, b_w_out, kv_norm_g, w_kv, mix_norm_g, ffn_norm_g, ffn_w_up, ffn_conv_w, ffn_conv_b, ffn_w_down, final_norm_g, loss_target, m_a_w_in, m_a_b_f, m_a_w_out, m_b_w_q, m_b_w_out, m_kv_norm_g, m_w_kv, m_mix_norm_g, m_ffn_norm_g, m_ffn_w_up, m_ffn_conv_w, m_ffn_conv_b, m_ffn_w_down, m_final_norm_g, v_a_w_in, v_a_b_f, v_a_w_out, v_b_w_q, v_b_w_out, v_kv_norm_g, v_w_kv, v_mix_norm_g, v_ffn_norm_g, v_ffn_w_up, v_ffn_conv_w, v_ffn_conv_b, v_ffn_w_down, v_final_norm_g):
    w = dict(a_w_in=a_w_in, a_b_f=a_b_f, a_w_out=a_w_out, b_w_q=b_w_q, b_w_out=b_w_out, kv_norm_g=kv_norm_g, w_kv=w_kv,
             mix_norm_g=mix_norm_g, ffn_norm_g=ffn_norm_g, ffn_w_up=ffn_w_up, ffn_conv_w=ffn_conv_w, ffn_conv_b=ffn_conv_b,
             ffn_w_down=ffn_w_down, final_norm_g=final_norm_g)
    m = dict(a_w_in=m_a_w_in, a_b_f=m_a_b_f, a_w_out=m_a_w_out, b_w_q=m_b_w_q, b_w_out=m_b_w_out, kv_norm_g=m_kv_norm_g,
             w_kv=m_w_kv, mix_norm_g=m_mix_norm_g, ffn_norm_g=m_ffn_norm_g, ffn_w_up=m_ffn_w_up, ffn_conv_w=m_ffn_conv_w,
             ffn_conv_b=m_ffn_conv_b, ffn_w_down=m_ffn_w_down, final_norm_g=m_final_norm_g)
    v = dict(a_w_in=v_a_w_in, a_b_f=v_a_b_f, a_w_out=v_a_w_out, b_w_q=v_b_w_q, b_w_out=v_b_w_out, kv_norm_g=v_kv_norm_g,
             w_kv=v_w_kv, mix_norm_g=v_mix_norm_g, ffn_norm_g=v_ffn_norm_g, ffn_w_up=v_ffn_w_up, ffn_conv_w=v_ffn_conv_w,
             ffn_conv_b=v_ffn_conv_b, ffn_w_down=v_ffn_w_down, final_norm_g=v_final_norm_g)

    c_arr = lax.axis_index("c").astype(jnp.int32).reshape(1)
    k_arr = (2 * lax.axis_index("x") + lax.axis_index("y")).astype(jnp.int32).reshape(1)
    xi, yi = lax.axis_index("x"), lax.axis_index("y")
    late_slots = jnp.stack([jnp.where(k == k_arr[0], OWN_SLOT, 2 * ((k & 1) ^ yi) + ((k >> 1) ^ xi) - 1)
                            for k in range(N_CHIPS)]).astype(jnp.int32)
    w_late = _pack_weights(w, "late")
    land = lax.empty((OWN_SLOT,) + w_late.shape, BF16)
    send_sems, recv_sems, w_thru, land_thru, token = _late_gather_start(w_late, land)
    w_early = _pack_weights(w, "early")
    early = _allgather_shards(w_early, _place_own(w_early, k_arr, "early_place_own"))
    p = _early_weights(early, jnp.arange(N_CHIPS, dtype=jnp.int32))
    cb = ffn_conv_b.reshape(2, 2, 1, FF)
    p.update(a_b_f=a_b_f, kv_norm_g=kv_norm_g, mix_norm_g=mix_norm_g + token[0, 0], ffn_norm_g=ffn_norm_g,
             final_norm_g=final_norm_g, conv_b=cb)

    def late_weights(after):
        own, landed = _late_gather_wait(send_sems, recv_sems, w_thru, land_thru, after)
        return _late_weights(landed, late_slots, own)

    started = {}

    def late_grads_ready(g_so_far):
        gflat = _shard_grads(g_so_far, "grad_late")
        pair = _pair_sum(gflat, _sibling_swap_half(gflat, "late"), c_arr, "late")
        land = lax.empty((3,) + pair.shape[1:], BF16)
        *handles, token = _chip_exchange_start(pair, land)
        started["handles"] = handles
        return token

    loss_part, grad_x, g = _local_step(x[0], loss_target[0], p, late_weights, late_grads_ready)

    halves = {}
    pair, landed = _chip_exchange_wait(*started["handles"], grad_x)
    g_mine = _chip_sum(pair, landed, k_arr, "late")
    halves["grad_late"] = (g_mine, _sibling_send(g_mine, "late"))
    gflat = _shard_grads(g, "grad_early")
    pair = _pair_sum(gflat, _sibling_swap_half(gflat, "early"), c_arr, "early")
    g_mine = _chip_sum(pair, _chip_exchange(pair, "early"), k_arr, "early")
    halves["grad_early"] = (g_mine, _sibling_send(g_mine, "early"))
    small, loss = _unpack_small(_allreduce_small(_pack_small(g, loss_part)))

    big = [{}, {}, {}, {}]
    for name in _BIG:
        sg = _seg(name)
        g_mine, g_other = halves[_grad_layout(name)]
        res = _adamw_shard(_native3(w[name], name), _native3(m[name], name), _native3(v[name], name), g_mine, g_other,
                           c_arr, name)
        for store, t in zip(big, res):
            store[name] = t[:, :sg["R"], :].reshape(_SHARD_SHAPES[name])
    dws, mns, vns = _adamw(_pack_small(w), _pack_small(small), _pack_small(m), _pack_small(v), "adamw_small")
    sml = [small] + [_unpack_small(t)[0] for t in (dws, mns, vns)]
    outs = [loss, grad_x[None]]
    for b, s in zip(big, sml):
        outs += [b[n] if n in b else s[n] for n in _WEIGHTS]
    return tuple(outs)
```

```python
import numpy as np
import jax
import jax.numpy as jnp
from jax import lax
from jax.experimental import pallas as pl
from jax.experimental.pallas import tpu as pltpu

F32 = jnp.float32
BF16 = jnp.bfloat16
MESH = pl.DeviceIdType.MESH

S = 4096
D = 1024
A_HEADS = 16
HEAD_DIM = 64
QKV_W = 3 * A_HEADS * HEAD_DIM
B_GROUPS = ((128, 1), (512, 4), (2048, 16))
B_HPG = 8
B_Q_W = 3 * B_HPG * HEAD_DIM
B_OUT_W = B_HPG * HEAD_DIM
B_KV_W = 2 * B_Q_W
B_WIN = 128
FF = 2816
RMS_EPS = 1e-6
SCALE = HEAD_DIM ** -0.5
N_CHIPS = 4

ADAM_LR, ADAM_B1, ADAM_B2, ADAM_EPS, ADAM_WD, ADAM_STEP = 0.001, 0.9, 0.999, 1e-08, 0.01, 10

V7X_VMEM_LIMIT = 48 * 1024 * 1024
LANES = 128
NEG_INF = float("-inf")

FLAT_W = LANES
_SEGS = (("ffn_w_down", 2, 704, 1024, 704), ("a_w_in", 1, 1024, 772, 1024), ("a_w_out", 1, 256, 1024, 256),
         ("b_w_q", 1, 1024, 384, 1024), ("b_w_out", 1, 512, 256, 512), ("w_kv", 1, 1024, 768, 1024),
         ("ffn_w_up", 2, 1024, 1408, 1024), ("ffn_conv_w", 1, 6, 1408, 16))
CONV_TERMS = 3


def _seg_rows(name, L, R, C, rpad):
    return (CONV_TERMS if name == "ffn_conv_w" else 1) * L * (-(-C // FLAT_W)) * rpad


SMALL_ROWS = 24


def _cparams(sem=None, **kw):
    return pltpu.CompilerParams(dimension_semantics=sem, vmem_limit_bytes=V7X_VMEM_LIMIT, **kw)


_DN = {"nn": (((1,), (0,)), ((), ())), "nt": (((1,), (1,)), ((), ())), "tn": (((0,), (0,)), ((), ()))}


def _mm(a, b, *, mode, tm, tn, tk, name, out_dtype=F32, res=None, a_split=0, b_split=0, o_split=0):
    if mode == "tn":
        K = a.shape[0]
        M = a.shape[1]
    else:
        M = a.shape[-2]
        K = a.shape[-1] * (2 if a_split else 1)
    if mode == "nt":
        N = b.shape[0]
    else:
        N = b.shape[-1] * (2 if b_split else 1)
    assert M % tm == 0 and N % tn == 0 and K % tk == 0, (name, M, N, K, tm, tn, tk)
    nk = K // tk

    if mode == "tn":
        a_spec = pl.BlockSpec((tk, tm), lambda i, j, k: (k, i))
    elif a_split:
        a_spec = pl.BlockSpec((None, tm, tk), lambda i, j, k: (k // a_split, i, k % a_split))
    else:
        a_spec = pl.BlockSpec((tm, tk), lambda i, j, k: (i, k))
    if mode == "nt":
        b_spec = pl.BlockSpec((tn, tk), lambda i, j, k: (j, k))
    elif b_split:
        b_spec = pl.BlockSpec((None, tk, tn), lambda i, j, k: (j // b_split, k, j % b_split))
    else:
        b_spec = pl.BlockSpec((tk, tn), lambda i, j, k: (k, j))
    if o_split:
        o_spec = pl.BlockSpec((None, tm, tn), lambda i, j, k: (j // o_split, i, j % o_split))
        out_shape = jax.ShapeDtypeStruct((2, M, N // 2), out_dtype)
    else:
        o_spec = pl.BlockSpec((tm, tn), lambda i, j, k: (i, j))
        out_shape = jax.ShapeDtypeStruct((M, N), out_dtype)
    in_specs = [a_spec, b_spec]
    args = [a, b]
    if res is not None:
        in_specs.append(pl.BlockSpec((tm, tn), lambda i, j, k: (i, j)))
        args.append(res)

    def body(*refs):
        if res is not None:
            a_ref, b_ref, r_ref, o_ref = refs[:4]
        else:
            a_ref, b_ref, o_ref = refs[:3]
            r_ref = None
        p = lax.dot_general(a_ref[...].astype(BF16), b_ref[...].astype(BF16), _DN[mode], preferred_element_type=F32)

        def finish(r):
            if r_ref is not None:
                r = r + r_ref[...]
            o_ref[...] = r.astype(out_dtype)

        if nk == 1:
            finish(p)
        else:
            acc = refs[-1]
            k = pl.program_id(2)

            @pl.when(k == 0)
            def _():
                acc[...] = p

            @pl.when(k > 0)
            def _():
                acc[...] += p

            @pl.when(k == nk - 1)
            def _():
                finish(acc[...])

    return pl.pallas_call(
        body, out_shape=out_shape, grid=(M // tm, N // tn, nk), in_specs=in_specs, out_specs=o_spec,
        scratch_shapes=[pltpu.VMEM((tm, tn), F32)] if nk > 1 else [],
        compiler_params=_cparams(("parallel", "parallel", "arbitrary")), name=name,
    )(*args)


NORM_ROWS = 256


def _rms_fwd(x, g, name):
    def body(x_ref, g_ref, o_ref):
        xv = x_ref[...]
        r = lax.rsqrt(jnp.mean(xv * xv, axis=-1, keepdims=True) + RMS_EPS)
        o_ref[...] = (xv * r * g_ref[...]).astype(BF16)

    row = pl.BlockSpec((NORM_ROWS, D), lambda i: (i, 0))
    return pl.pallas_call(
        body, out_shape=jax.ShapeDtypeStruct((S, D), BF16), grid=(S // NORM_ROWS,),
        in_specs=[row, pl.BlockSpec((1, D), lambda i: (0, 0))], out_specs=row,
        compiler_params=_cparams(("parallel",)), name=name,
    )(x, g.reshape(1, D))


def _rms_bwd(x, dres, pairs, name):
    n = len(pairs)

    def body(*refs):
        x_ref, dres_ref = refs[0], refs[1]
        g_refs = refs[2:2 + 2 * n:2]
        dh_refs = refs[3:3 + 2 * n:2]
        dx_ref, dxb_ref = refs[2 + 2 * n], refs[3 + 2 * n]
        dg_refs = refs[4 + 2 * n:]
        i = pl.program_id(0)
        xv = x_ref[...]
        r = lax.rsqrt(jnp.mean(xv * xv, axis=-1, keepdims=True) + RMS_EPS)
        y = xv * r
        dx = dres_ref[...]
        for g_ref, dh_ref, dg_ref in zip(g_refs, dh_refs, dg_refs):
            dh = dh_ref[...]
            dy = dh * g_ref[...]
            dx = dx + r * (dy - y * jnp.mean(dy * y, axis=-1, keepdims=True))
            part = jnp.sum(dh * y, axis=0, keepdims=True)

            @pl.when(i == 0)
            def _():
                dg_ref[...] = part

            @pl.when(i > 0)
            def _():
                dg_ref[...] += part

        dx_ref[...] = dx
        dxb_ref[...] = dx.astype(BF16)

    row = pl.BlockSpec((NORM_ROWS, D), lambda i: (i, 0))
    vec = pl.BlockSpec((1, D), lambda i: (0, 0))
    in_specs = [row, row]
    args = [x, dres]
    for g, dh in pairs:
        in_specs += [vec, row]
        args += [g.reshape(1, D), dh]
    outs = pl.pallas_call(
        body,
        out_shape=[jax.ShapeDtypeStruct((S, D), F32), jax.ShapeDtypeStruct((S, D), BF16)]
        + [jax.ShapeDtypeStruct((1, D), F32)] * n,
        grid=(S // NORM_ROWS,), in_specs=in_specs, out_specs=[row, row] + [vec] * n,
        compiler_params=_cparams(("arbitrary",)), name=name,
    )(*args)
    return outs[0], outs[1], list(outs[2:])


def _loss_head(x, g, target, name):
    def body(x_ref, g_ref, t_ref, loss_ref, dx_ref, dxb_ref, dg_ref):
        i = pl.program_id(0)
        xv = x_ref[...]
        gv = g_ref[...]
        r = lax.rsqrt(jnp.mean(xv * xv, axis=-1, keepdims=True) + RMS_EPS)
        y = xv * r
        err = y * gv - t_ref[...]
        lpart = jnp.broadcast_to(jnp.sum(err * err, keepdims=True) * (0.5 / D), (1, LANES))
        dh = err * (1.0 / D)
        dy = dh * gv
        dx = r * (dy - y * jnp.mean(dy * y, axis=-1, keepdims=True))
        part = jnp.sum(dh * y, axis=0, keepdims=True)

        @pl.when(i == 0)
        def _():
            dg_ref[...] = part
            loss_ref[...] = lpart

        @pl.when(i > 0)
        def _():
            dg_ref[...] += part
            loss_ref[...] += lpart

        dx_ref[...] = dx
        dxb_ref[...] = dx.astype(BF16)

    row = pl.BlockSpec((NORM_ROWS, D), lambda i: (i, 0))
    vec = pl.BlockSpec((1, D), lambda i: (0, 0))
    return pl.pallas_call(
        body,
        out_shape=[jax.ShapeDtypeStruct((1, LANES), F32), jax.ShapeDtypeStruct((S, D), F32),
                   jax.ShapeDtypeStruct((S, D), BF16), jax.ShapeDtypeStruct((1, D), F32)],
        grid=(S // NORM_ROWS,), in_specs=[row, vec, row],
        out_specs=[pl.BlockSpec((1, LANES), lambda i: (0, 0)), row, row, vec],
        compiler_params=_cparams(("arbitrary",)), name=name,
    )(x, g.reshape(1, D), target)


SCAN_ROWS = 256


def _split3(v):
    hi = v.astype(BF16)
    r1 = v - hi.astype(F32)
    mid = r1.astype(BF16)
    lo = (r1 - mid.astype(F32)).astype(BF16)
    return hi, mid, lo


def _tri_dot(tri, v):
    hi, mid, lo = _split3(v)
    dn = _DN["nn"]
    return (lax.dot_general(tri, hi, dn, preferred_element_type=F32)
            + lax.dot_general(tri, mid, dn, preferred_element_type=F32)
            + lax.dot_general(tri, lo, dn, preferred_element_type=F32))


def _log_sigmoid(z):
    return jnp.minimum(z, 0.0) - jnp.log(1.0 + jnp.exp(-jnp.abs(z)))


GATE_LANES = 6


def _gate_lane_tables():
    pq = np.zeros((3 * LANES, A_HEADS * HEAD_DIM), np.float32)
    pk = np.zeros((3 * LANES, A_HEADS * HEAD_DIM), np.float32)
    one_q = np.zeros((1, A_HEADS * HEAD_DIM), np.float32)
    one_k = np.zeros((1, A_HEADS * HEAD_DIM), np.float32)
    for h in range(A_HEADS):
        pos = (h // 2) * LANES + (HEAD_DIM if h % 2 == 0 else 0)
        for term in range(3):
            pq[term * LANES + h, pos + term] = 1.0
            pk[term * LANES + h, pos + 3 + term] = -1.0
        one_q[0, pos + 3:pos + GATE_LANES] = 1.0
        one_k[0, pos:pos + 3] = 1.0
    return jnp.asarray(pq, BF16), jnp.asarray(pk, BF16), jnp.asarray(one_q), jnp.asarray(one_k)


def _fgate_fwd(pf, bias, name):
    tri = jnp.tril(jnp.ones((SCAN_ROWS, SCAN_ROWS), F32)).astype(BF16)
    pq, pk, one_q, one_k = _gate_lane_tables()

    def body(pf_ref, b_ref, tri_ref, pq_ref, pk_ref, oq_ref, ok_ref, aq_ref, ak_ref, c_sc):
        carry = jnp.zeros((1, LANES), F32)
        for blk in range(S // SCAN_ROWS):
            rows = pl.ds(blk * SCAN_ROWS, SCAN_ROWS)
            lf = _log_sigmoid(pf_ref[rows, :] + b_ref[...])
            c_sc[...] = _tri_dot(tri_ref[...], lf) + carry
            carry = c_sc[pl.ds(SCAN_ROWS - 1, 1), :]
            terms = jnp.concatenate(_split3(c_sc[...]), axis=1)
            aq = lax.dot_general(terms, pq_ref[...], _DN["nn"], preferred_element_type=F32) + oq_ref[...]
            ak = lax.dot_general(terms, pk_ref[...], _DN["nn"], preferred_element_type=F32) + ok_ref[...]
            aq_ref[rows, :] = aq.astype(BF16)
            ak_ref[rows, :] = ak.astype(BF16)

    wide = jax.ShapeDtypeStruct((S, A_HEADS * HEAD_DIM), BF16)
    return pl.pallas_call(
        body, out_shape=[wide, wide], scratch_shapes=[pltpu.VMEM((SCAN_ROWS, LANES), F32)],
        compiler_params=_cparams(), name=name,
    )(pf, bias, tri, pq, pk, one_q, one_k)


def _fgate_bwd(pf, bias, dc_key, dc_query, name):
    triu = jnp.triu(jnp.ones((SCAN_ROWS, SCAN_ROWS), F32)).astype(BF16)

    def body(pf_ref, b_ref, dck_ref, dcq_ref, tri_ref, dpf_ref, db_ref, dlf_ref):
        carry = jnp.zeros((1, LANES), F32)
        db = jnp.zeros((1, LANES), F32)
        lane = lax.broadcasted_iota(jnp.int32, (SCAN_ROWS, LANES), 1)
        for blk in reversed(range(S // SCAN_ROWS)):
            rows = pl.ds(blk * SCAN_ROWS, SCAN_ROWS)
            dc = dck_ref[rows, :] + dcq_ref[rows, :]
            dlf_ref[rows, :] = _tri_dot(tri_ref[...], dc) + carry
            carry = dlf_ref[pl.ds(blk * SCAN_ROWS, 1), :]
            z = pf_ref[rows, :] + b_ref[...]
            e = jnp.exp(-jnp.abs(z))
            sig_neg = jnp.where(z >= 0.0, e, 1.0) / (1.0 + e)
            dz = jnp.where(lane < A_HEADS, dlf_ref[rows, :] * sig_neg, 0.0)
            dpf_ref[rows, :] = dz.astype(BF16)
            db = db + jnp.sum(dz, axis=0, keepdims=True)
        db_ref[...] = db

    return pl.pallas_call(
        body, out_shape=[jax.ShapeDtypeStruct((S, LANES), BF16), jax.ShapeDtypeStruct((1, LANES), F32)],
        scratch_shapes=[pltpu.VMEM((S, LANES), F32)],
        compiler_params=_cparams(), name=name,
    )(pf, bias, dc_key, dc_query, triu)


FOX_T = 512


def _first_head(shape):
    return lax.broadcasted_iota(jnp.int32, shape, len(shape) - 1) < HEAD_DIM


def _each_head(x, lo):
    zero = jnp.zeros_like(x)
    return jnp.where(lo, x, zero), jnp.where(lo, zero, x)


def _fox_pair_fwd(qkv, aug_q, aug_k, name):
    T = FOX_T
    nq = S // T
    NP = A_HEADS // 2

    def body(q_ref, k_ref, v_ref, aq_ref, ak_ref, o_ref, lse_ref, m_sc, l_sc, acc_sc):
        i = pl.program_id(1)
        j = pl.program_id(2)
        lo = _first_head((T, LANES))

        @pl.when(j == 0)
        def _():
            m_sc[...] = jnp.full((2, T, LANES), NEG_INF, F32)
            l_sc[...] = jnp.zeros((2, T, LANES), F32)
            acc_sc[...] = jnp.zeros((T, LANES), F32)

        def step(diagonal):
            qs = q_ref[...] * jnp.asarray(SCALE, BF16)
            aq, ak, kv = aq_ref[...], ak_ref[...], k_ref[...]
            q2 = (jnp.where(lo, qs, aq), jnp.where(lo, aq, qs))
            k2 = (jnp.where(lo, kv, ak), jnp.where(lo, ak, kv))
            if diagonal:
                causal = lax.broadcasted_iota(jnp.int32, (T, T), 0) >= lax.broadcasted_iota(jnp.int32, (T, T), 1)
            pv, alphas = None, []
            for h, vh in enumerate(_each_head(v_ref[...], lo)):
                s = lax.dot_general(q2[h], k2[h], _DN["nt"], preferred_element_type=F32)
                if diagonal:
                    s = jnp.where(causal, s, NEG_INF)
                m_prev = m_sc[h]
                m_new = jnp.maximum(m_prev, jnp.max(s, axis=1, keepdims=True))
                alpha = jnp.exp(m_prev - m_new)
                p = jnp.exp(s - jnp.tile(m_new, (1, T // LANES)))
                l_sc[h] = alpha * l_sc[h] + jnp.sum(p, axis=1, keepdims=True)
                m_sc[h] = m_new
                d = lax.dot_general(p.astype(BF16), vh, _DN["nn"], preferred_element_type=F32)
                pv = d if pv is None else pv + d
                alphas.append(alpha)
            acc_sc[...] = jnp.where(lo, alphas[0], alphas[1]) * acc_sc[...] + pv

        @pl.when(j < i)
        def _():
            step(False)

        @pl.when(j == i)
        def _():
            step(True)
            o_ref[...] = (acc_sc[...] * jnp.where(lo, 1.0 / l_sc[0], 1.0 / l_sc[1])).astype(BF16)
            for h in range(2):
                lse_ref[h] = (m_sc[h] + jnp.log(l_sc[h]))[:, 0:1]

    qs_ = pl.BlockSpec((T, LANES), lambda p, i, j: (i, p))
    ks = pl.BlockSpec((T, LANES), lambda p, i, j: (jnp.minimum(i, j), NP + p))
    vs = pl.BlockSpec((T, LANES), lambda p, i, j: (jnp.minimum(i, j), 2 * NP + p))
    aks = pl.BlockSpec((T, LANES), lambda p, i, j: (jnp.minimum(i, j), p))
    col = pl.BlockSpec((2, T, 1), lambda p, i, j: (p, i, 0))
    return pl.pallas_call(
        body, out_shape=[jax.ShapeDtypeStruct((S, A_HEADS * HEAD_DIM), BF16), jax.ShapeDtypeStruct((A_HEADS, S, 1), F32)],
        grid=(NP, nq, nq), in_specs=[qs_, ks, vs, qs_, aks], out_specs=[qs_, col],
        scratch_shapes=[pltpu.VMEM((2, T, LANES), F32), pltpu.VMEM((2, T, LANES), F32), pltpu.VMEM((T, LANES), F32)],
        compiler_params=_cparams(("parallel", "parallel", "arbitrary")), name=name,
    )(qkv, qkv, qkv, aug_q, aug_k)


def _fox_pair_bwd(qkv, do, lse_row, delta_row, aug_q, aug_k, name):
    T = FOX_T
    nq = S // T
    NP = A_HEADS // 2

    def body(q_ref, k_ref, v_ref, do_ref, lse_ref, dl_ref, aq_ref, ak_ref, dq_ref, dk_ref, dv_ref, dc_ref, dcq_ref,
             dq_sc, dk_sc, dv_sc, dc_sc):
        j = pl.program_id(1)
        i = pl.program_id(2)
        lo = _first_head((T, LANES))

        @pl.when(jnp.logical_and(j == 0, i == 0))
        def _():
            dq_sc[...] = jnp.zeros((S, LANES), F32)
            dcq_ref[...] = jnp.zeros((2, nq, 1, T), F32)

        @pl.when(i == j)
        def _():
            dk_sc[...] = jnp.zeros((T, LANES), F32)
            dv_sc[...] = jnp.zeros((T, LANES), F32)
            dc_sc[...] = jnp.zeros((2, T, 1), F32)

        def step(diagonal):
            qv = q_ref[...]
            kv = k_ref[...]
            dov = do_ref[...].astype(BF16)
            qs = qv * jnp.asarray(SCALE, BF16)
            aq, ak = aq_ref[...], ak_ref[...]
            q2 = (jnp.where(lo, qs, aq), jnp.where(lo, aq, qs))
            k2 = (jnp.where(lo, kv, ak), jnp.where(lo, ak, kv))
            if diagonal:
                causal = lax.broadcasted_iota(jnp.int32, (T, T), 1) >= lax.broadcasted_iota(jnp.int32, (T, T), 0)
            dv = dk = dq = None
            for h, (kh, vh, qh, doh) in enumerate(zip(_each_head(kv, lo), _each_head(v_ref[...], lo),
                                                      _each_head(qv, lo), _each_head(dov, lo))):
                st = lax.dot_general(k2[h], q2[h], _DN["nt"], preferred_element_type=F32)
                if diagonal:
                    st = jnp.where(causal, st, NEG_INF)
                pt = jnp.exp(st - lse_ref[h])
                d = lax.dot_general(pt.astype(BF16), doh, _DN["nn"], preferred_element_type=F32)
                dv = d if dv is None else dv + d
                dpt = lax.dot_general(vh, dov, _DN["nt"], preferred_element_type=F32)
                dst = pt * (dpt - dl_ref[h])
                dc_sc[h] -= jnp.sum(dst, axis=1, keepdims=True)
                dcq_ref[h, i] += jnp.sum(dst, axis=0, keepdims=True)
                dsb = (dst * SCALE).astype(BF16)
                d = lax.dot_general(dsb, qh, _DN["nn"], preferred_element_type=F32)
                dk = d if dk is None else dk + d
                d = lax.dot_general(dsb, kh, _DN["tn"], preferred_element_type=F32)
                dq = d if dq is None else dq + d
            dv_sc[...] += dv
            dk_sc[...] += dk
            rows = pl.ds(pl.multiple_of(i * T, T), T)
            dq_sc[rows, :] += dq

        @pl.when(i > j)
        def _():
            step(False)

        @pl.when(i == j)
        def _():
            step(True)

        @pl.when(i == nq - 1)
        def _():
            dk_ref[...] = dk_sc[...].astype(BF16)
            dv_ref[...] = dv_sc[...].astype(BF16)
            dc_ref[...] = dc_sc[...]

        @pl.when(jnp.logical_and(j == nq - 1, i == nq - 1))
        def _():
            dq_ref[...] = dq_sc[...].astype(BF16)

    qs = pl.BlockSpec((T, LANES), lambda p, j, i: (jnp.maximum(i, j), p))
    qrow = pl.BlockSpec((2, 1, T), lambda p, j, i: (p, 0, jnp.maximum(i, j)))
    ks = pl.BlockSpec((T, LANES), lambda p, j, i: (j, NP + p))
    vs = pl.BlockSpec((T, LANES), lambda p, j, i: (j, 2 * NP + p))
    kout = pl.BlockSpec((T, LANES), lambda p, j, i: (j, p))
    kcol = pl.BlockSpec((2, T, 1), lambda p, j, i: (p, j, 0))
    dqs = pl.BlockSpec((S, LANES), lambda p, j, i: (0, p))
    dcqs = pl.BlockSpec((2, nq, 1, T), lambda p, j, i: (p, 0, 0, 0))
    wide = jax.ShapeDtypeStruct((S, A_HEADS * HEAD_DIM), BF16)
    return pl.pallas_call(
        body,
        out_shape=[wide, wide, wide, jax.ShapeDtypeStruct((A_HEADS, S, 1), F32),
                   jax.ShapeDtypeStruct((A_HEADS, nq, 1, T), F32)],
        grid=(NP, nq, nq), in_specs=[qs, ks, vs, qs, qrow, qrow, qs, kout], out_specs=[dqs, kout, kout, kcol, dcqs],
        scratch_shapes=[pltpu.VMEM((S, LANES), F32), pltpu.VMEM((T, LANES), F32), pltpu.VMEM((T, LANES), F32),
                        pltpu.VMEM((2, T, 1), F32)],
        compiler_params=_cparams(("parallel", "arbitrary", "arbitrary")), name=name,
    )(qkv, qkv, qkv, do, lse_row, delta_row, aug_q, aug_k)


def _pair_rowdot(a, b, name):
    n = a.shape[1] // HEAD_DIM
    T = 1024

    def body(a_ref, b_ref, o_ref):
        prod = a_ref[...].astype(F32) * b_ref[...].astype(F32)
        lo = _first_head(prod.shape)
        o_ref[0] = jnp.sum(jnp.where(lo, prod, 0.0), axis=1, keepdims=True)
        o_ref[1] = jnp.sum(jnp.where(lo, 0.0, prod), axis=1, keepdims=True)

    blk = pl.BlockSpec((T, LANES), lambda p, i: (i, p))
    return pl.pallas_call(
        body, out_shape=jax.ShapeDtypeStruct((n, S, 1), F32), grid=(n // 2, S // T), in_specs=[blk, blk],
        out_specs=pl.BlockSpec((2, T, 1), lambda p, i: (p, i, 0)),
        compiler_params=_cparams(("parallel", "parallel")), name=name,
    )(a, b)


W = B_WIN
N_HG = 3 * B_HPG
N_BLK = S // W


def _dil_tables():
    slopes = np.exp2((-8.0 * np.arange(1, N_HG + 1, dtype=np.float32) / N_HG).astype(np.float32)).astype(np.float32)
    dil = np.repeat(np.array([d for _, d in B_GROUPS], np.float32), B_HPG)
    coef = (slopes * dil).astype(np.float32)
    nbs = np.repeat(np.array([S // d // W for _, d in B_GROUPS], np.int32), B_HPG)
    return jnp.asarray(coef), jnp.asarray(nbs)


DIL_SUB = 8
DIL_ROWS = DIL_SUB * W
DIL_STEPS = S // DIL_ROWS


def _dil_bias(coef, transposed):
    row = lax.broadcasted_iota(jnp.int32, (W, 2 * W), 0)
    col = lax.broadcasted_iota(jnp.int32, (W, 2 * W), 1)
    dist = (col - row) if transposed else (row + W - col)
    valid = jnp.logical_and(dist >= 0, dist <= W)
    return jnp.where(valid, -coef * dist.astype(F32), NEG_INF), col


NPG = B_HPG // 2
GROUP_W = B_HPG * HEAD_DIM


def _dil_pair_specs(qoff, koff, voff):
    prev_blk = lambda n: jnp.maximum(n * DIL_SUB - 1, 0)
    next_blk = lambda n: jnp.minimum((n + 1) * DIL_SUB, N_BLK - 1)
    return dict(
        o=pl.BlockSpec((DIL_ROWS, LANES), lambda h, n: (n, h)),
        o_next=pl.BlockSpec((W, LANES), lambda h, n: (next_blk(n), h)),
        q=pl.BlockSpec((DIL_ROWS, LANES), lambda h, n: (n, qoff + h)),
        q_next=pl.BlockSpec((W, LANES), lambda h, n: (next_blk(n), qoff + h)),
        k=pl.BlockSpec((DIL_ROWS, LANES), lambda h, n: (n, koff + h)),
        k_prev=pl.BlockSpec((W, LANES), lambda h, n: (prev_blk(n), koff + h)),
        v=pl.BlockSpec((DIL_ROWS, LANES), lambda h, n: (n, voff + h)),
        v_prev=pl.BlockSpec((W, LANES), lambda h, n: (prev_blk(n), voff + h)),
        col=pl.BlockSpec((2, DIL_ROWS, 1), lambda h, n: (h, n, 0)),
        row=pl.BlockSpec((2, 1, DIL_ROWS), lambda h, n: (h, 0, n)),
        row_next=pl.BlockSpec((2, 1, W), lambda h, n: (h, 0, next_blk(n))),
        smem=pl.BlockSpec(memory_space=pltpu.SMEM))


def _dil_pair_fwd(g, q, k, v, qoff, koff, voff, name):
    coef_t, nbs_t = _dil_tables()

    def body(coef_ref, nbs_ref, q_ref, kh_ref, k_ref, vh_ref, v_ref, o_ref, lse_ref, kf, vf):
        hp = pl.program_id(0)
        n = pl.program_id(1)
        nbs = nbs_ref[B_HPG * g + 2 * hp]
        kf[0:W, :] = kh_ref[...]
        kf[W:, :] = k_ref[...]
        vf[0:W, :] = vh_ref[...]
        vf[W:, :] = v_ref[...]
        biases = [_dil_bias(coef_ref[B_HPG * g + 2 * hp + h], False) for h in range(2)]
        col = biases[0][1]
        lo = _first_head((W, LANES))
        lo2 = _first_head((2 * W, LANES))
        for b in range(DIL_SUB):
            first = lax.rem(n * DIL_SUB + b, nbs) == 0
            rows = slice(b * W, (b + 1) * W)
            both = slice(b * W, (b + 2) * W)
            qv = q_ref[rows, :]
            acc, inv = None, []
            for h, (kh, vh) in enumerate(zip(_each_head(kf[both, :], lo2), _each_head(vf[both, :], lo2))):
                s = lax.dot_general(qv, kh, _DN["nt"], preferred_element_type=F32) * SCALE + biases[h][0]
                s = jnp.where(jnp.logical_and(first, col < W), NEG_INF, s)
                m = jnp.max(s, axis=1, keepdims=True)
                p = jnp.exp(s - m)
                l = jnp.sum(p, axis=1, keepdims=True)
                d = lax.dot_general(p.astype(BF16), vh, _DN["nn"], preferred_element_type=F32)
                acc = d if acc is None else acc + d
                inv.append(1.0 / l)
                lse_ref[h, rows, :] = m + jnp.log(l)
            o_ref[rows, :] = acc * jnp.where(lo, inv[0], inv[1])

    sp = _dil_pair_specs(qoff, koff, voff)
    return pl.pallas_call(
        body, out_shape=[jax.ShapeDtypeStruct((S, GROUP_W), F32), jax.ShapeDtypeStruct((B_HPG, S, 1), F32)],
        grid=(NPG, DIL_STEPS), in_specs=[sp["smem"], sp["smem"], sp["q"], sp["k_prev"], sp["k"], sp["v_prev"], sp["v"]],
        out_specs=[sp["o"], sp["col"]], scratch_shapes=[pltpu.VMEM((DIL_ROWS + W, LANES), BF16)] * 2,
        compiler_params=_cparams(("parallel", "parallel")), name=name,
    )(coef_t, nbs_t, q, k, k, v, v)


def _dil_pair_merge(os, lses, name):
    T = 1024

    def body(o0_ref, o1_ref, o2_ref, l0_ref, l1_ref, l2_ref, om_ref, omb_ref, l_ref):
        lo = _first_head((T, LANES))
        weights = []
        for h in range(2):
            l0, l1, l2 = l0_ref[h], l1_ref[h], l2_ref[h]
            m = jnp.maximum(jnp.maximum(l0, l1), l2)
            e0, e1, e2 = jnp.exp(l0 - m), jnp.exp(l1 - m), jnp.exp(l2 - m)
            den = e0 + e1 + e2
            weights.append((e0 / den, e1 / den, e2 / den))
            l_ref[h] = m + jnp.log(den)
        om = (jnp.where(lo, weights[0][0], weights[1][0]) * o0_ref[...]
              + jnp.where(lo, weights[0][1], weights[1][1]) * o1_ref[...]
              + jnp.where(lo, weights[0][2], weights[1][2]) * o2_ref[...])
        om_ref[...] = om
        omb_ref[...] = om.astype(BF16)

    ob = pl.BlockSpec((T, LANES), lambda p, i: (i, p))
    lb = pl.BlockSpec((2, T, 1), lambda p, i: (p, i, 0))
    return pl.pallas_call(
        body,
        out_shape=[jax.ShapeDtypeStruct((S, B_OUT_W), F32), jax.ShapeDtypeStruct((S, B_OUT_W), BF16),
                   jax.ShapeDtypeStruct((B_HPG, S, 1), F32)],
        grid=(NPG, S // T), in_specs=[ob] * 3 + [lb] * 3, out_specs=[ob, ob, lb],
        compiler_params=_cparams(("parallel", "parallel")), name=name,
    )(*os, *lses)


def _dil_pair_dq(g, q, k, v, qoff, koff, voff, do, lcol, dcol, name):
    coef_t, nbs_t = _dil_tables()

    def body(coef_ref, nbs_ref, q_ref, kh_ref, k_ref, vh_ref, v_ref, do_ref, l_ref, d_ref, dq_ref, kf, vf):
        hp = pl.program_id(0)
        n = pl.program_id(1)
        nbs = nbs_ref[B_HPG * g + 2 * hp]
        kf[0:W, :] = kh_ref[...]
        kf[W:, :] = k_ref[...]
        vf[0:W, :] = vh_ref[...]
        vf[W:, :] = v_ref[...]
        biases = [_dil_bias(coef_ref[B_HPG * g + 2 * hp + h], False) for h in range(2)]
        col = biases[0][1]
        lo2 = _first_head((2 * W, LANES))
        for b in range(DIL_SUB):
            first = lax.rem(n * DIL_SUB + b, nbs) == 0
            rows = slice(b * W, (b + 1) * W)
            both = slice(b * W, (b + 2) * W)
            qv = q_ref[rows, :]
            dov = do_ref[rows, :]
            acc = None
            for h, (kh, vh) in enumerate(zip(_each_head(kf[both, :], lo2), _each_head(vf[both, :], lo2))):
                s = lax.dot_general(qv, kh, _DN["nt"], preferred_element_type=F32) * SCALE + biases[h][0]
                s = jnp.where(jnp.logical_and(first, col < W), NEG_INF, s)
                p = jnp.exp(s - l_ref[h, rows, :])
                dp = lax.dot_general(dov, vh, _DN["nt"], preferred_element_type=F32)
                ds = (p * (dp - d_ref[h, rows, :]) * SCALE).astype(BF16)
                d = lax.dot_general(ds, kh, _DN["nn"], preferred_element_type=F32)
                acc = d if acc is None else acc + d
            dq_ref[rows, :] = acc.astype(BF16)

    sp = _dil_pair_specs(qoff, koff, voff)
    return pl.pallas_call(
        body, out_shape=jax.ShapeDtypeStruct((S, GROUP_W), BF16), grid=(NPG, DIL_STEPS),
        in_specs=[sp["smem"], sp["smem"], sp["q"], sp["k_prev"], sp["k"], sp["v_prev"], sp["v"], sp["o"], sp["col"],
                  sp["col"]],
        out_specs=sp["o"], scratch_shapes=[pltpu.VMEM((DIL_ROWS + W, LANES), BF16)] * 2,
        compiler_params=_cparams(("parallel", "parallel")), name=name,
    )(coef_t, nbs_t, q, k, k, v, v, do, lcol, dcol)


def _dil_pair_dkv(g, q, k, v, qoff, koff, voff, do, lrow, drow, name):
    coef_t, nbs_t = _dil_tables()

    def body(coef_ref, nbs_ref, k_ref, v_ref, q_ref, qn_ref, do_ref, don_ref, l_ref, ln_ref, d_ref, dn_ref,
             dk_ref, dv_ref, qf, dof, lf, df):
        hp = pl.program_id(0)
        n = pl.program_id(1)
        nbs = nbs_ref[B_HPG * g + 2 * hp]
        qf[0:DIL_ROWS, :] = q_ref[...]
        qf[DIL_ROWS:, :] = qn_ref[...]
        dof[0:DIL_ROWS, :] = do_ref[...]
        dof[DIL_ROWS:, :] = don_ref[...]
        lf[:, :, 0:DIL_ROWS] = l_ref[...]
        lf[:, :, DIL_ROWS:] = ln_ref[...]
        df[:, :, 0:DIL_ROWS] = d_ref[...]
        df[:, :, DIL_ROWS:] = dn_ref[...]
        biases = [_dil_bias(coef_ref[B_HPG * g + 2 * hp + h], True) for h in range(2)]
        col = biases[0][1]
        lo = _first_head((W, LANES))
        lo2 = _first_head((2 * W, LANES))
        for b in range(DIL_SUB):
            no_next = lax.rem(n * DIL_SUB + b + 1, nbs) == 0
            rows = slice(b * W, (b + 1) * W)
            both = slice(b * W, (b + 2) * W)
            dd = dof[both, :]
            dk = dv = None
            for h, (kh, vh, qh, ddh) in enumerate(zip(_each_head(k_ref[rows, :], lo), _each_head(v_ref[rows, :], lo),
                                                      _each_head(qf[both, :], lo2), _each_head(dd, lo2))):
                st = lax.dot_general(kh, qh, _DN["nt"], preferred_element_type=F32) * SCALE + biases[h][0]
                st = jnp.where(jnp.logical_and(no_next, col >= W), NEG_INF, st)
                pt = jnp.exp(st - lf[h, :, both])
                d = lax.dot_general(pt.astype(BF16), ddh, _DN["nn"], preferred_element_type=F32)
                dv = d if dv is None else dv + d
                dpt = lax.dot_general(vh, dd, _DN["nt"], preferred_element_type=F32)
                dst = (pt * (dpt - df[h, :, both]) * SCALE).astype(BF16)
                d = lax.dot_general(dst, qh, _DN["nn"], preferred_element_type=F32)
                dk = d if dk is None else dk + d
            dk_ref[rows, :] = dk.astype(BF16)
            dv_ref[rows, :] = dv.astype(BF16)

    sp = _dil_pair_specs(qoff, koff, voff)
    wide = jax.ShapeDtypeStruct((S, GROUP_W), BF16)
    return pl.pallas_call(
        body, out_shape=[wide, wide], grid=(NPG, DIL_STEPS),
        in_specs=[sp["smem"], sp["smem"], sp["k"], sp["v"], sp["q"], sp["q_next"], sp["o"], sp["o_next"], sp["row"],
                  sp["row_next"], sp["row"], sp["row_next"]],
        out_specs=[sp["o"], sp["o"]],
        scratch_shapes=[pltpu.VMEM((DIL_ROWS + W, LANES), BF16)] * 2 + [pltpu.VMEM((2, 1, DIL_ROWS + W), F32)] * 2,
        compiler_params=_cparams(("parallel", "parallel")), name=name,
    )(coef_t, nbs_t, k, v, q, q, do, do, lrow, lrow, drow, drow)


FFN_ROWS = 512
FFN_COLS = 256
HALO = 8


def _shifted(u, halo, back):
    T = u.shape[0]
    rows = lax.broadcasted_iota(jnp.int32, u.shape, 0)
    if back:
        s1 = jnp.where(rows == 0, halo[HALO - 1:HALO, :], pltpu.roll(u, 1, 0))
        s2 = jnp.where(rows == 0, halo[HALO - 2:HALO - 1, :],
                       jnp.where(rows == 1, halo[HALO - 1:HALO, :], pltpu.roll(u, 2, 0)))
    else:
        s1 = jnp.where(rows == T - 1, halo[0:1, :], pltpu.roll(u, T - 1, 0))
        s2 = jnp.where(rows == T - 1, halo[1:2, :],
                       jnp.where(rows == T - 2, halo[0:1, :], pltpu.roll(u, T - 2, 0)))
    return s1, s2


def _conv_parts(u_ref, h_ref, w_ref, b_ref, first):
    out = []
    for p in range(2):
        u = u_ref[p]
        halo = jnp.where(first, 0.0, h_ref[p])
        u1, u2 = _shifted(u, halo, True)
        w = w_ref[p]
        out.append((w[0:1, :] * u2 + w[1:2, :] * u1 + w[2:3, :] * u + b_ref[p], u1, u2, u))
    return out


def _ffn_specs():
    T, C = FFN_ROWS, FFN_COLS
    blk = pl.BlockSpec((2, T, C), lambda j, i: (0, i, j))
    prev = pl.BlockSpec((2, HALO, C), lambda j, i: (0, jnp.maximum(i * (T // HALO) - 1, 0), j))
    nxt = pl.BlockSpec((2, HALO, C), lambda j, i: (0, jnp.minimum((i + 1) * (T // HALO), S // HALO - 1), j))
    wsp = pl.BlockSpec((2, 3, C), lambda j, i: (0, 0, j))
    bsp = pl.BlockSpec((2, 1, C), lambda j, i: (0, 0, j))
    one = pl.BlockSpec((T, C), lambda j, i: (i, j))
    return blk, prev, nxt, wsp, bsp, one


def _ffn_act_fwd(u, w, b, name):
    blk, prev, _, wsp, bsp, one = _ffn_specs()

    def body(u_ref, h_ref, w_ref, b_ref, o_ref):
        (a, _, _, _), (g, _, _, _) = _conv_parts(u_ref, h_ref, w_ref, b_ref, pl.program_id(1) == 0)
        o_ref[...] = (g / (1.0 + jnp.exp(-g)) * a).astype(BF16)

    return pl.pallas_call(
        body, out_shape=jax.ShapeDtypeStruct((S, FF), BF16), grid=(FF // FFN_COLS, S // FFN_ROWS),
        in_specs=[blk, prev, wsp, bsp], out_specs=one,
        compiler_params=_cparams(("parallel", "parallel")), name=name,
    )(u, u, w, b)


def _ffn_act_bwd(u, dact, w, b, name):
    blk, prev, _, wsp, bsp, one = _ffn_specs()

    def body(u_ref, h_ref, da_ref, w_ref, b_ref, duc_ref, dwb_ref):
        i = pl.program_id(1)
        (a, a1, a2, a0), (g, g1, g2, g0) = _conv_parts(u_ref, h_ref, w_ref, b_ref, i == 0)
        dact_v = da_ref[...]
        sg = 1.0 / (1.0 + jnp.exp(-g))
        d_a = dact_v * (g * sg)
        d_g = dact_v * a * (sg * (1.0 + g * (1.0 - sg)))
        duc_ref[0] = d_a
        duc_ref[1] = d_g

        @pl.when(i == 0)
        def _():
            dwb_ref[...] = jnp.zeros(dwb_ref.shape, F32)

        for p, (d, s2, s1, s0) in enumerate(((d_a, a2, a1, a0), (d_g, g2, g1, g0))):
            dwb_ref[p, 0:1, :] += jnp.sum(d * s2, axis=0, keepdims=True)
            dwb_ref[p, 1:2, :] += jnp.sum(d * s1, axis=0, keepdims=True)
            dwb_ref[p, 2:3, :] += jnp.sum(d * s0, axis=0, keepdims=True)
            dwb_ref[p, 3:4, :] += jnp.sum(d, axis=0, keepdims=True)

    return pl.pallas_call(
        body, out_shape=[jax.ShapeDtypeStruct((2, S, FF), F32), jax.ShapeDtypeStruct((2, 8, FF), F32)],
        grid=(FF // FFN_COLS, S // FFN_ROWS), in_specs=[blk, prev, one, wsp, bsp],
        out_specs=[blk, pl.BlockSpec((2, 8, FFN_COLS), lambda j, i: (0, 0, j))],
        compiler_params=_cparams(("parallel", "arbitrary")), name=name,
    )(u, u, dact, w, b)


def _ffn_conv_bwd(duc, w, name):
    blk, _, nxt, wsp, _, _ = _ffn_specs()
    last = S // FFN_ROWS - 1

    def body(d_ref, h_ref, w_ref, du_ref):
        is_last = pl.program_id(1) == last
        for p in range(2):
            d = d_ref[p]
            halo = jnp.where(is_last, 0.0, h_ref[p])
            d1, d2 = _shifted(d, halo, False)
            wv = w_ref[p]
            du_ref[p] = (wv[2:3, :] * d + wv[1:2, :] * d1 + wv[0:1, :] * d2).astype(BF16)

    return pl.pallas_call(
        body, out_shape=jax.ShapeDtypeStruct((2, S, FF), BF16), grid=(FF // FFN_COLS, S // FFN_ROWS),
        in_specs=[blk, nxt, wsp], out_specs=blk,
        compiler_params=_cparams(("parallel", "parallel")), name=name,
    )(duc, duc, w)


def _adam_update(w, gv, m, v):
    c1 = 1.0 / (1.0 - ADAM_B1 ** ADAM_STEP)
    c2 = 1.0 / (1.0 - ADAM_B2 ** ADAM_STEP)
    mn = ADAM_B1 * m + (1.0 - ADAM_B1) * gv
    vn = ADAM_B2 * v + (1.0 - ADAM_B2) * (gv * gv)
    return -ADAM_LR * ((mn * c1) / (jnp.sqrt(vn * c2) + ADAM_EPS) + ADAM_WD * w), mn, vn


def _adamw(w, g, m, v, name):
    rows = w.shape[0]
    T = 8
    for cand in (256, 128, 64, 32, 16, 8):
        if rows % cand == 0:
            T = cand
            break

    def body(w_ref, g_ref, m_ref, v_ref, d_ref, mo_ref, vo_ref):
        d_ref[...], mo_ref[...], vo_ref[...] = _adam_update(w_ref[...], g_ref[...], m_ref[...], v_ref[...])

    blk = pl.BlockSpec((T, w.shape[1]), lambda i: (i, 0))
    sds = jax.ShapeDtypeStruct(w.shape, F32)
    return pl.pallas_call(
        body, out_shape=[sds, sds, sds], grid=(rows // T,), in_specs=[blk] * 4, out_specs=[blk] * 3,
        compiler_params=_cparams(("parallel",)), name=name,
    )(w, g, m, v)


ANY = pl.BlockSpec(memory_space=pl.ANY)


def _place():
    x, y, c = lax.axis_index("x"), lax.axis_index("y"), lax.axis_index("c")
    chips = [(1 - x, y), (x, 1 - y), (1 - x, 1 - y)]
    return x, y, c, chips


def _place_own(w, slot_arr, name):
    rows = w.shape[0]
    T = 16
    for cand in (2048, 1024, 512, 256, 128, 64, 32, 16):
        if rows % cand == 0:
            T = cand
            break

    def body(k_ref, w_ref, o_ref):
        o_ref[...] = w_ref[...]

    return pl.pallas_call(
        body, out_shape=jax.ShapeDtypeStruct((N_CHIPS, rows, FLAT_W), w.dtype),
        grid_spec=pltpu.PrefetchScalarGridSpec(
            num_scalar_prefetch=1, grid=(rows // T,),
            in_specs=[pl.BlockSpec((T, FLAT_W), lambda i, k: (i, 0))],
            out_specs=pl.BlockSpec((None, T, FLAT_W), lambda i, k: (k[0], i, 0))),
        compiler_params=_cparams(("parallel",)), name=name,
    )(slot_arr, w)


def _allgather_shards(w, buf):
    half_rows = w.shape[0] // 2
    assert half_rows % 16 == 0

    def body(w_ref, buf_ref, g_ref, send_sems, recv_sems):
        x, y, c, chips = _place()
        myk = 2 * x + y
        sibling = (x, y, 1 - c)
        h0 = pl.multiple_of(c * half_rows, 16)
        h1 = pl.multiple_of((1 - c) * half_rows, 16)

        def half(k, start):
            return g_ref.at[k, pl.ds(start, half_rows), :]

        def rcopy(sem, src, dst, to):
            return pltpu.make_async_remote_copy(src_ref=src, dst_ref=dst, send_sem=send_sems.at[sem],
                                                recv_sem=recv_sems.at[sem], device_id=to, device_id_type=MESH)

        ici = [rcopy(r, w_ref.at[pl.ds(h0, half_rows), :], half(myk, h0), (*chip, c)) for r, chip in enumerate(chips)]
        for cp in ici:
            cp.start()
        ks = [2 * cx + cy for cx, cy in chips]
        fwd = [rcopy(3 + r, half(ks[r], h0), half(ks[r], h0), sibling) for r in range(3)]
        for r in range(3):
            rcopy(r, half(ks[r], h0), half(ks[r], h0), (*chips[r], c)).wait_recv()
            fwd[r].start()
        for r in range(3):
            rcopy(3 + r, half(ks[r], h1), half(ks[r], h1), sibling).wait_recv()
        for cp in ici + fwd:
            cp.wait_send()

    return pl.pallas_call(
        body, out_shape=jax.ShapeDtypeStruct(buf.shape, w.dtype), in_specs=[ANY, ANY], out_specs=ANY,
        scratch_shapes=[pltpu.SemaphoreType.DMA((6,)), pltpu.SemaphoreType.DMA((6,))],
        input_output_aliases={1: 0},
        compiler_params=pltpu.CompilerParams(has_side_effects=True), name="allgather_shards",
    )(w, buf)


HBM_SPEC = pl.BlockSpec(memory_space=pltpu.HBM)
SEM_SPEC = pl.BlockSpec(memory_space=pltpu.SEMAPHORE)
DATAFLOW = pltpu.SideEffectType.DATAFLOW_SIDE_EFFECTING
OWN_SLOT = 3


def _late_gather_start(w, land):
    def body(w_ref, land_ref, send_sems, recv_sems, w_thru, land_thru, token):
        x, y, c, chips = _place()
        for r, chip in enumerate(chips):
            pltpu.make_async_remote_copy(src_ref=w_ref, dst_ref=land_ref.at[r], send_sem=send_sems.at[r],
                                         recv_sem=recv_sems.at[r], device_id=(*chip, c), device_id_type=MESH).start()
        token[...] = jnp.zeros_like(token)

    return pl.pallas_call(
        body, name="late_gather_start",
        out_shape=(pltpu.SemaphoreType.DMA((3,)), pltpu.SemaphoreType.DMA((3,)), pltpu.HBM(w.shape, w.dtype),
                   pltpu.HBM(land.shape, land.dtype), jax.ShapeDtypeStruct((8, LANES), F32)),
        in_specs=(HBM_SPEC, HBM_SPEC),
        out_specs=(SEM_SPEC, SEM_SPEC, HBM_SPEC, HBM_SPEC, pl.BlockSpec(memory_space=pltpu.VMEM)),
        input_output_aliases={0: 2, 1: 3}, compiler_params=pltpu.CompilerParams(has_side_effects=DATAFLOW),
    )(pltpu.with_memory_space_constraint(w, pltpu.HBM), pltpu.with_memory_space_constraint(land, pltpu.HBM))


def _late_gather_wait(send_sems, recv_sems, w_thru, land_thru, after):
    def body(w_ref, land_ref, send_sems, recv_sems, after_ref, w_dead, got_ref):
        x, y, c, chips = _place()
        for r, chip in enumerate(chips):
            cp = pltpu.make_async_remote_copy(src_ref=w_ref, dst_ref=land_ref.at[r], send_sem=send_sems.at[r],
                                              recv_sem=recv_sems.at[r], device_id=(*chip, c), device_id_type=MESH)
            cp.wait_send()
            cp.wait_recv()

    return pl.pallas_call(
        body, name="late_gather_wait",
        out_shape=(pltpu.HBM(w_thru.shape, w_thru.dtype), pltpu.HBM(land_thru.shape, land_thru.dtype)),
        in_specs=(HBM_SPEC, HBM_SPEC, SEM_SPEC, SEM_SPEC, pl.BlockSpec(memory_space=pl.ANY)),
        out_specs=(HBM_SPEC, HBM_SPEC), input_output_aliases={0: 0, 1: 1},
        compiler_params=pltpu.CompilerParams(has_side_effects=DATAFLOW),
    )(w_thru, land_thru, send_sems, recv_sems, after)


def _flat_tile(rows):
    return next(t for t in (2048, 1024, 512, 256, 128, 64, 32, 16) if rows % t == 0)


def _sibling_swap_half(g, tag):
    half = g.shape[1] // 2

    def body(g_ref, o_ref, send_sem, recv_sem):
        x, y, c, _ = _place()
        theirs = pl.multiple_of((1 - c) * half, 8)
        cp = pltpu.make_async_remote_copy(src_ref=g_ref.at[:, pl.ds(theirs, half), :], dst_ref=o_ref,
                                          send_sem=send_sem, recv_sem=recv_sem, device_id=(x, y, 1 - c),
                                          device_id_type=MESH)
        cp.start()
        cp.wait()

    return pl.pallas_call(
        body, out_shape=jax.ShapeDtypeStruct((N_CHIPS, half, FLAT_W), F32), in_specs=[ANY], out_specs=ANY,
        scratch_shapes=[pltpu.SemaphoreType.DMA, pltpu.SemaphoreType.DMA],
        compiler_params=pltpu.CompilerParams(has_side_effects=True), name=f"rs_sibling_swap_{tag}",
    )(g)


def _pair_sum(g, other, c_arr, tag):
    half = other.shape[1]
    T = _flat_tile(half)

    def body(c_ref, g_ref, o_ref, s_ref):
        s_ref[...] = (g_ref[...] + o_ref[...]).astype(BF16)

    nb = half // T
    return pl.pallas_call(
        body, out_shape=jax.ShapeDtypeStruct((N_CHIPS, half, FLAT_W), BF16),
        grid_spec=pltpu.PrefetchScalarGridSpec(
            num_scalar_prefetch=1, grid=(N_CHIPS, nb),
            in_specs=[pl.BlockSpec((None, T, FLAT_W), lambda k, i, c: (k, c[0] * nb + i, 0)),
                      pl.BlockSpec((None, T, FLAT_W), lambda k, i, c: (k, i, 0))],
            out_specs=pl.BlockSpec((None, T, FLAT_W), lambda k, i, c: (k, i, 0))),
        compiler_params=_cparams(("parallel", "parallel")), name=f"rs_pair_sum_{tag}",
    )(c_arr, g, other)


def _chip_exchange(s, tag):
    def body(s_ref, o_ref, send_sems, recv_sems):
        x, y, c, chips = _place()
        cps = []
        for r, (cx, cy) in enumerate(chips):
            cps.append(pltpu.make_async_remote_copy(
                src_ref=s_ref.at[2 * cx + cy], dst_ref=o_ref.at[r], send_sem=send_sems.at[r],
                recv_sem=recv_sems.at[r], device_id=(cx, cy, c), device_id_type=MESH))
        for cp in cps:
            cp.start()
        for cp in cps:
            cp.wait()

    return pl.pallas_call(
        body, out_shape=jax.ShapeDtypeStruct((3,) + s.shape[1:], BF16), in_specs=[ANY], out_specs=ANY,
        scratch_shapes=[pltpu.SemaphoreType.DMA((3,)), pltpu.SemaphoreType.DMA((3,))],
        compiler_params=pltpu.CompilerParams(has_side_effects=True), name=f"rs_chip_exchange_{tag}",
    )(s)


def _chip_exchange_start(s, land):
    def body(s_ref, land_ref, send_sems, recv_sems, s_thru, land_thru, token):
        x, y, c, chips = _place()
        for r, (cx, cy) in enumerate(chips):
            pltpu.make_async_remote_copy(src_ref=s_ref.at[2 * cx + cy], dst_ref=land_ref.at[r], send_sem=send_sems.at[r],
                                         recv_sem=recv_sems.at[r], device_id=(cx, cy, c), device_id_type=MESH).start()
        token[...] = jnp.zeros_like(token)

    return pl.pallas_call(
        body, name="rs_exchange_start",
        out_shape=(pltpu.SemaphoreType.DMA((3,)), pltpu.SemaphoreType.DMA((3,)), pltpu.HBM(s.shape, s.dtype),
                   pltpu.HBM(land.shape, land.dtype), jax.ShapeDtypeStruct((8, LANES), F32)),
        in_specs=(HBM_SPEC, HBM_SPEC),
        out_specs=(SEM_SPEC, SEM_SPEC, HBM_SPEC, HBM_SPEC, pl.BlockSpec(memory_space=pltpu.VMEM)),
        input_output_aliases={0: 2, 1: 3}, compiler_params=pltpu.CompilerParams(has_side_effects=DATAFLOW),
    )(pltpu.with_memory_space_constraint(s, pltpu.HBM), pltpu.with_memory_space_constraint(land, pltpu.HBM))


def _chip_exchange_wait(send_sems, recv_sems, s_thru, land_thru, after):
    def body(s_ref, land_ref, send_sems, recv_sems, after_ref, s_done, got_ref):
        x, y, c, chips = _place()
        for r, (cx, cy) in enumerate(chips):
            cp = pltpu.make_async_remote_copy(src_ref=s_ref.at[2 * cx + cy], dst_ref=land_ref.at[r],
                                              send_sem=send_sems.at[r], recv_sem=recv_sems.at[r], device_id=(cx, cy, c),
                                              device_id_type=MESH)
            cp.wait_send()
            cp.wait_recv()

    return pl.pallas_call(
        body, name="rs_exchange_wait",
        out_shape=(pltpu.HBM(s_thru.shape, s_thru.dtype), pltpu.HBM(land_thru.shape, land_thru.dtype)),
        in_specs=(HBM_SPEC, HBM_SPEC, SEM_SPEC, SEM_SPEC, pl.BlockSpec(memory_space=pl.ANY)),
        out_specs=(HBM_SPEC, HBM_SPEC), input_output_aliases={0: 0, 1: 1},
        compiler_params=pltpu.CompilerParams(has_side_effects=DATAFLOW),
    )(s_thru, land_thru, send_sems, recv_sems, after)


def _chip_sum(s, r, k_arr, tag):
    half = s.shape[1]
    T = _flat_tile(half)

    def body(k_ref, s_ref, r_ref, o_ref):
        o_ref[...] = ((s_ref[...].astype(F32) + r_ref[0].astype(F32)) + r_ref[1].astype(F32)) + r_ref[2].astype(F32)

    return pl.pallas_call(
        body, out_shape=jax.ShapeDtypeStruct((half, FLAT_W), F32),
        grid_spec=pltpu.PrefetchScalarGridSpec(
            num_scalar_prefetch=1, grid=(half // T,),
            in_specs=[pl.BlockSpec((None, T, FLAT_W), lambda i, k: (k[0], i, 0)),
                      pl.BlockSpec((3, T, FLAT_W), lambda i, k: (0, i, 0))],
            out_specs=pl.BlockSpec((T, FLAT_W), lambda i, k: (i, 0))),
        compiler_params=_cparams(("parallel",)), name=f"rs_chip_sum_{tag}",
    )(k_arr, s, r)


def _sibling_send(t, tag):
    def body(t_ref, o_ref, send_sem, recv_sem):
        x, y, c, _ = _place()
        cp = pltpu.make_async_remote_copy(src_ref=t_ref, dst_ref=o_ref, send_sem=send_sem, recv_sem=recv_sem,
                                          device_id=(x, y, 1 - c), device_id_type=MESH)
        cp.start()
        cp.wait()

    return pl.pallas_call(
        body, out_shape=jax.ShapeDtypeStruct(t.shape, F32), in_specs=[ANY], out_specs=ANY,
        scratch_shapes=[pltpu.SemaphoreType.DMA, pltpu.SemaphoreType.DMA],
        compiler_params=pltpu.CompilerParams(has_side_effects=True), name=f"rs_sibling_send_{tag}",
    )(t)


def _allreduce_small(v):
    def body(v_ref, o_ref, buf, send_sems, recv_sems):
        x, y, c, _ = _place()
        me = 4 * x + 2 * y + c
        buf[me] = v_ref[...]
        cps = []
        for mask in range(1, 8):
            a, b, d = (mask >> 2) & 1, (mask >> 1) & 1, mask & 1
            peer = (x + a - 2 * a * x, y + b - 2 * b * y, c + d - 2 * d * c)
            cps.append(pltpu.make_async_remote_copy(
                src_ref=v_ref, dst_ref=buf.at[me], send_sem=send_sems.at[mask - 1], recv_sem=recv_sems.at[mask - 1],
                device_id=peer, device_id_type=MESH))
        for cp in cps:
            cp.start()
        for cp in cps:
            cp.wait()
        total = buf[0]
        for dev in range(1, 8):
            total = total + buf[dev]
        o_ref[...] = total

    vm = pl.BlockSpec(memory_space=pltpu.VMEM)
    return pl.pallas_call(
        body, out_shape=jax.ShapeDtypeStruct((SMALL_ROWS, 1024), F32), in_specs=[vm], out_specs=vm,
        scratch_shapes=[pltpu.VMEM((8, SMALL_ROWS, 1024), F32), pltpu.SemaphoreType.DMA((7,)),
                        pltpu.SemaphoreType.DMA((7,))],
        compiler_params=pltpu.CompilerParams(has_side_effects=True), name="allreduce_small",
    )(v)


def _col_to_row(t):
    return t.reshape(t.shape[0], 1, S)


def _residue_rows(t, d, inverse=False):
    if d == 1:
        return t
    shape = (d, S // d) if inverse else (S // d, d)
    return t.reshape(shape + t.shape[1:]).transpose(1, 0, 2).reshape(t.shape)


def _residue_vecs(t, d, inverse=False):
    if d == 1:
        return t
    shape = (d, S // d) if inverse else (S // d, d)
    return t.reshape((B_HPG,) + shape).transpose(0, 2, 1).reshape(B_HPG, S, 1)


def _ffn_fwd(x, g, w_up, cw, cb, w_down, tag):
    h = _rms_fwd(x, g, f"{tag}_norm")
    u = _mm(h, w_up, mode="nn", tm=1024, tn=1408, tk=1024, o_split=2, name=f"{tag}_up")
    act = _ffn_act_fwd(u, cw, cb, f"{tag}_act")
    x_out = _mm(act, w_down, mode="nn", tm=1024, tn=512, tk=FF, res=x, name=f"{tag}_down")
    return x_out, (h, u, act)


def _ffn_bwd(x, g, w_up, cw, cb, w_down, saved, dx, dxb, tag):
    h, u, act = saved
    d_w_down = _mm(act, dxb, mode="tn", tm=1408, tn=512, tk=1024, name=f"{tag}_dwdown")
    dact = _mm(dxb, w_down, mode="nt", tm=1024, tn=1408, tk=1024, name=f"{tag}_dact")
    duc, dwb = _ffn_act_bwd(u, dact, cw, cb, f"{tag}_dgate")
    du = _ffn_conv_bwd(duc, cw, f"{tag}_dconv")
    d_w_up = _mm(h, du, mode="tn", tm=1024, tn=1408, tk=1024, b_split=2, name=f"{tag}_dwup")
    dh = _mm(du, w_up, mode="nt", tm=1024, tn=512, tk=1408, a_split=2, name=f"{tag}_dh")
    dx_new, dxb_new, (dg,) = _rms_bwd(x, dx, [(g, dh)], f"{tag}_dnorm")
    d_cw = dwb[:, 0:3, :].transpose(1, 0, 2).reshape(3, 2 * FF)
    d_cb = dwb[:, 3, :].reshape(2 * FF)
    return dx_new, dxb_new, dict(w_up=d_w_up, w_down=d_w_down, conv_w=d_cw, conv_b=d_cb, norm_g=dg.reshape(D))


def _local_step(x, target, p, late_weights, late_grads_ready):
    g = {}
    h1 = _rms_fwd(x, p["mix_norm_g"][0], "a_norm")
    w_qkv = p["a_w_in"][:, :QKV_W]
    w_f = jnp.pad(p["a_w_in"][:, QKV_W:], ((0, 0), (0, LANES - A_HEADS)))
    b_f = jnp.pad(p["a_b_f"].reshape(1, A_HEADS), ((0, 0), (0, LANES - A_HEADS)))
    qkv = _mm(h1, w_qkv, mode="nn", tm=1024, tn=512, tk=1024, out_dtype=BF16, name="a_qkv")
    pf = _mm(h1, w_f, mode="nn", tm=1024, tn=LANES, tk=1024, name="a_gate")
    aug_q, aug_k = _fgate_fwd(pf, b_f, "a_gate_scan")
    oa2, lse_a = _fox_pair_fwd(qkv, aug_q, aug_k, "a_attn")
    x1 = _mm(oa2, p["a_w_out"], mode="nn", tm=1024, tn=512, tk=1024, res=x, name="a_out")
    p = {**p, **late_weights(x1)}
    x2, ffn0 = _ffn_fwd(x1, p["ffn_norm_g"][0], p["ffn_w_up"][0], p["conv_w"][0], p["conv_b"][0], p["ffn_w_down"][0], "f0")
    hk = _rms_fwd(x2, p["kv_norm_g"], "kv_norm")
    kvb = _mm(hk, p["w_kv"], mode="nn", tm=1024, tn=512, tk=1024, out_dtype=BF16, name="kv_proj")
    h3 = _rms_fwd(x2, p["mix_norm_g"][1], "b_norm")
    qb = _mm(h3, p["b_w_q"], mode="nn", tm=1024, tn=512, tk=1024, out_dtype=BF16, name="b_q")
    dil_in = []
    for gi, (_, d) in enumerate(B_GROUPS):
        if d == 1:
            dil_in.append((qb, kvb, kvb, gi * NPG, gi * NPG, (3 + gi) * NPG))
        else:
            qg = _residue_rows(qb[:, gi * GROUP_W:(gi + 1) * GROUP_W], d)
            kvg = _residue_rows(kvb.reshape(S, 2, 3, GROUP_W)[:, :, gi, :].reshape(S, 2 * GROUP_W), d)
            dil_in.append((qg, kvg, kvg, 0, 0, NPG))
    o_g, lse_g = [], []
    for gi, (_, d) in enumerate(B_GROUPS):
        qg, kg, vg, qoff, koff, voff = dil_in[gi]
        og, lg = _dil_pair_fwd(gi, qg, kg, vg, qoff, koff, voff, f"b_attn{gi}")
        o_g.append(_residue_rows(og, d, inverse=True))
        lse_g.append(_residue_vecs(lg, d, inverse=True))
    ob, ob2, lse_b = _dil_pair_merge(o_g, lse_g, "b_merge")
    x3 = _mm(ob2, p["b_w_out"], mode="nn", tm=1024, tn=512, tk=B_OUT_W, res=x2, name="b_out")
    x4, ffn1 = _ffn_fwd(x3, p["ffn_norm_g"][1], p["ffn_w_up"][1], p["conv_w"][1], p["conv_b"][1], p["ffn_w_down"][1], "f1")
    loss, dx, dxb, dg_final = _loss_head(x4, p["final_norm_g"], target, "loss_head")
    g["final_norm_g"] = dg_final.reshape(D)

    dx, dxb, gf1 = _ffn_bwd(x3, p["ffn_norm_g"][1], p["ffn_w_up"][1], p["conv_w"][1], p["conv_b"][1], p["ffn_w_down"][1],
                            ffn1, dx, dxb, "f1")
    g["b_w_out"] = _mm(ob2, dxb, mode="tn", tm=B_OUT_W, tn=512, tk=1024, name="b_dwout")
    dob = _mm(dxb, p["b_w_out"], mode="nt", tm=1024, tn=B_OUT_W, tk=1024, name="b_do")
    delta_b = _pair_rowdot(dob, ob, "b_delta")
    dob16 = dob.astype(BF16)
    dq_g, dk_g, dv_g = [], [], []
    for gi, (_, d) in enumerate(B_GROUPS):
        qg, kg, vg, qoff, koff, voff = dil_in[gi]
        dog, l_d, dl_d = _residue_rows(dob16, d), _residue_vecs(lse_b, d), _residue_vecs(delta_b, d)
        dqd = _dil_pair_dq(gi, qg, kg, vg, qoff, koff, voff, dog, l_d, dl_d, f"b_dq{gi}")
        dkd, dvd = _dil_pair_dkv(gi, qg, kg, vg, qoff, koff, voff, dog, _col_to_row(l_d), _col_to_row(dl_d), f"b_dkv{gi}")
        dq_g.append(_residue_rows(dqd, d, inverse=True))
        dk_g.append(_residue_rows(dkd, d, inverse=True))
        dv_g.append(_residue_rows(dvd, d, inverse=True))
    dqb = jnp.concatenate(dq_g, axis=1)
    dkvb = jnp.concatenate(dk_g + dv_g, axis=1)
    g["b_w_q"] = _mm(h3, dqb, mode="tn", tm=1024, tn=512, tk=1024, name="b_dwq")
    dh3 = _mm(dqb, p["b_w_q"], mode="nt", tm=1024, tn=512, tk=B_Q_W, name="b_dh")
    g["w_kv"] = _mm(hk, dkvb, mode="tn", tm=1024, tn=512, tk=1024, name="kv_dw")
    dhk = _mm(dkvb, p["w_kv"], mode="nt", tm=1024, tn=512, tk=1536, name="kv_dh")
    dx, dxb, (dg_mix1, dg_kv) = _rms_bwd(x2, dx, [(p["mix_norm_g"][1], dh3), (p["kv_norm_g"], dhk)], "b_dnorm")
    g["kv_norm_g"] = dg_kv.reshape(D)
    dx, dxb, gf0 = _ffn_bwd(x1, p["ffn_norm_g"][0], p["ffn_w_up"][0], p["conv_w"][0], p["conv_b"][0], p["ffn_w_down"][0],
                            ffn0, dx, dxb, "f0")
    g["ffn_w_up"] = [gf0["w_up"], gf1["w_up"]]
    g["ffn_w_down"] = [gf0["w_down"], gf1["w_down"]]
    g["ffn_conv_w"] = jnp.stack([gf0["conv_w"], gf1["conv_w"]])
    token = late_grads_ready(g)
    a_w_out_t = p["a_w_out"] + token[0, 0].astype(BF16)
    g["a_w_out"] = _mm(oa2, dxb, mode="tn", tm=1024, tn=512, tk=1024, name="a_dwout")
    doa = _mm(dxb, a_w_out_t, mode="nt", tm=1024, tn=512, tk=1024, name="a_do")
    delta_a = _pair_rowdot(doa, oa2, "a_delta")
    dqa, dka, dva, dck, dcq = _fox_pair_bwd(qkv, doa, _col_to_row(lse_a), _col_to_row(delta_a), aug_q, aug_k, "a_dattn")
    dqkv = jnp.concatenate([dqa, dka, dva], axis=1)
    pad_heads = lambda t: jnp.pad(t.reshape(A_HEADS, S).T, ((0, 0), (0, LANES - A_HEADS)))
    dpf, db_f = _fgate_bwd(pf, b_f, pad_heads(dck), pad_heads(dcq), "a_dgate_scan")
    g["a_b_f"] = db_f[:, :A_HEADS]
    d_w_qkv = _mm(h1, dqkv, mode="tn", tm=1024, tn=512, tk=1024, name="a_dwqkv")
    d_w_f = _mm(h1, dpf, mode="tn", tm=1024, tn=LANES, tk=1024, name="a_dwgate")
    g["a_w_in"] = jnp.concatenate([d_w_qkv, d_w_f[:, :A_HEADS]], axis=1)
    dh1 = _mm(dqkv, w_qkv, mode="nt", tm=1024, tn=512, tk=1536, name="a_dh")
    dh1 = _mm(dpf, w_f, mode="nt", tm=1024, tn=512, tk=LANES, res=dh1, name="a_dh_gate")
    dx, _, (dg_mix0,) = _rms_bwd(x, dx, [(p["mix_norm_g"][0], dh1)], "a_dnorm")

    g["mix_norm_g"] = jnp.stack([dg_mix0.reshape(D), dg_mix1.reshape(D)])
    g["ffn_norm_g"] = jnp.stack([gf0["norm_g"], gf1["norm_g"]])
    g["ffn_conv_b"] = jnp.stack([gf0["conv_b"], gf1["conv_b"]])
    return loss[0, 0], dx, g


_SHARD_SHAPES = {"a_w_in": (1, 1024, 772), "a_w_out": (1, 256, 1024), "b_w_q": (1, 1024, 384), "b_w_out": (1, 512, 256),
                 "w_kv": (1024, 768), "ffn_w_up": (2, 1024, 1408), "ffn_w_down": (2, 704, 1024), "ffn_conv_w": (2, 3, 1408)}
_SMALL = (("kv_norm_g", (1024,)), ("mix_norm_g", (2, 1024)), ("ffn_norm_g", (2, 1024)), ("final_norm_g", (1024,)),
          ("a_b_f", (1, 16)), ("ffn_conv_b", (2, 5632)))


def _unslabs(rows, L, R, C, rpad):
    nc = -(-C // FLAT_W)
    return rows.reshape(L, nc, rpad, FLAT_W).transpose(0, 2, 1, 3).reshape(L, rpad, nc * FLAT_W)[:, :R, :C]


_SEG_RT = {"ffn_w_down": 704, "a_w_in": 1024, "a_w_out": 256, "b_w_q": 1024, "b_w_out": 512, "w_kv": 1024,
           "ffn_w_up": 1024, "ffn_conv_w": 16}
_ROW_SHARDED = ("a_w_out", "ffn_w_down")


_LAYOUTS = {"early": ("a_w_in", "a_w_out"), "late": ("ffn_w_down", "b_w_q", "b_w_out", "w_kv", "ffn_w_up", "ffn_conv_w"),
            "grad_early": ("a_w_in", "a_w_out"),
            "grad_late": ("ffn_w_up", "ffn_w_down", "b_w_q", "w_kv", "b_w_out", "ffn_conv_w")}
_GRAD_ROWS = {"grad_early": 10240, "grad_late": 45056}


def _layout_rows(layout):
    used = sum(_seg_rows(*s) for s in _SEGS if s[0] in _LAYOUTS[layout])
    rows = _GRAD_ROWS.get(layout, used)
    assert rows >= used
    return rows


def _grad_layout(name):
    return "grad_early" if name in _LAYOUTS["grad_early"] else "grad_late"


def _seg(name, layout=None):
    layout = layout or _grad_layout(name)
    off = 0
    for s in sorted((s for s in _SEGS if s[0] in _LAYOUTS[layout]), key=lambda s: _LAYOUTS[layout].index(s[0])):
        _, L, R, C, rpad = s
        if layout in _GRAD_ROWS:
            per_layer = -(-C // FLAT_W) * rpad
            off = -(-off // per_layer) * per_layer
        if s[0] == name:
            rt = _SEG_RT[name]
            assert off % rt == 0 and rpad % rt == 0
            half = _layout_rows(layout) // 2
            assert off + _seg_rows(*s) <= 2 * half
            assert layout not in _GRAD_ROWS or half % rt == 0 or off + _seg_rows(*s) <= half
            return dict(L=L, R=R, C=C, rpad=rpad, nc=-(-C // FLAT_W), rt=rt, off=off, ni=rpad // rt, half=half)
        off += _seg_rows(*s)
    raise KeyError(name)


def _flat_block(sg, term=0):
    base = (sg["off"] + term * sg["L"] * sg["nc"] * sg["rpad"]) // sg["rt"]
    return lambda l, j, i: base + (l * sg["nc"] + j) * sg["ni"] + i


def _native3(t, name):
    sg = _seg(name)
    t = t.reshape(sg["L"], sg["R"], sg["C"])
    return jnp.pad(t, ((0, 0), (0, sg["rpad"] - sg["R"]), (0, 0))) if sg["rpad"] != sg["R"] else t


def _slab_pack(flat, t, name, layout, term=None):
    sg = _seg(name, layout)
    rt = sg["rt"]
    rb = _flat_block(sg, term or 0)

    def body(*refs):
        t_ref, o_ref = refs[-2], refs[-1]
        val = t_ref[...]
        o_ref[...] = val.astype(BF16) if term is None else _split3(val)[term]

    in_specs = [pl.BlockSpec((None, rt, FLAT_W), lambda l, j, i: (l, i, j))]
    args = [t]
    if flat is not None:
        in_specs, args = [ANY] + in_specs, [flat] + args
    return pl.pallas_call(
        body, out_shape=jax.ShapeDtypeStruct((_layout_rows(layout), FLAT_W), BF16), grid=(sg["L"], sg["nc"], sg["ni"]),
        in_specs=in_specs, out_specs=pl.BlockSpec((rt, FLAT_W), lambda l, j, i: (rb(l, j, i), 0)),
        input_output_aliases={0: 0} if flat is not None else {},
        compiler_params=_cparams(("parallel", "parallel", "parallel")), name=f"pack_{name}_{term or 0}",
    )(*args)


def _full_spec(sg, name):
    rt, nc, ni = sg["rt"], sg["nc"], sg["ni"]
    if name in _ROW_SHARDED:
        return (sg["L"], N_CHIPS * sg["R"], sg["C"]), pl.BlockSpec((None, rt, FLAT_W), lambda k, l, j, i: (l, k * ni + i, j))
    return ((sg["L"], sg["rpad"], N_CHIPS * nc * FLAT_W),
            pl.BlockSpec((None, rt, FLAT_W), lambda k, l, j, i: (l, i, k * nc + j)))


def _slab_unpack(gathered, slots, name, layout, own=None):
    sg = _seg(name, layout)
    rb = _flat_block(sg)
    shape, _ = _full_spec(sg, name)
    rt, nc, ni = sg["rt"], sg["nc"], sg["ni"]
    width = nc * FLAT_W
    last = gathered.shape[0] - 1

    def body(*refs):
        s_ref, o_ref = refs[0], refs[-1]
        is_own = s_ref[pl.program_id(0)] == OWN_SLOT
        for j in range(nc):
            val = refs[1 + j][...]
            if own is not None:
                val = jnp.where(is_own, refs[1 + nc + j][...], val)
            o_ref[:, j * FLAT_W:(j + 1) * FLAT_W] = val

    if name in _ROW_SHARDED:
        o_spec = pl.BlockSpec((None, rt, width), lambda k, l, i, s: (l, k * ni + i, 0))
    else:
        o_spec = pl.BlockSpec((None, rt, width), lambda k, l, i, s: (l, i, k))
    in_specs = [pl.BlockSpec((None, rt, FLAT_W), lambda k, l, i, s, j=j: (jnp.minimum(s[k], last), rb(l, j, i), 0))
                for j in range(nc)]
    args = [gathered] * nc
    if own is not None:
        in_specs += [pl.BlockSpec((rt, FLAT_W), lambda k, l, i, s, j=j: (rb(l, j, i), 0)) for j in range(nc)]
        args += [own] * nc
    return pl.pallas_call(
        body, out_shape=jax.ShapeDtypeStruct(shape, BF16),
        grid_spec=pltpu.PrefetchScalarGridSpec(num_scalar_prefetch=1, grid=(N_CHIPS, sg["L"], ni), in_specs=in_specs,
                                               out_specs=o_spec),
        compiler_params=_cparams(("parallel",) * 3), name=f"unpack_{name}",
    )(slots, *args)


def _slab_pack_grad(flat4, g, name, layer=None):
    sg = _seg(name)
    rows = _layout_rows(_grad_layout(name))
    shape, _ = _full_spec(sg, name)
    n_layers = sg["L"] if layer is None else 1
    assert g.shape == (n_layers,) + shape[1:], (name, g.shape, shape)
    rt, nc = sg["rt"], sg["nc"]
    assert sg["ni"] == 1 and sg["off"] % (nc * rt) == 0
    base = sg["off"] // (nc * rt) + (layer or 0)

    def body(*refs):
        g_ref, o_ref = refs[-2], refs[-1]
        for j in range(nc):
            o_ref[j * rt:(j + 1) * rt, :] = g_ref[:, j * FLAT_W:(j + 1) * FLAT_W]

    if name in _ROW_SHARDED:
        spec = pl.BlockSpec((None, rt, nc * FLAT_W), lambda k, l: (l, k, 0))
    else:
        spec = pl.BlockSpec((None, rt, nc * FLAT_W), lambda k, l: (l, 0, k))
    in_specs, args = [spec], [g]
    if flat4 is not None:
        in_specs, args = [pl.BlockSpec(memory_space=pl.ANY)] + in_specs, [flat4] + args
    return pl.pallas_call(
        body, out_shape=jax.ShapeDtypeStruct((N_CHIPS, rows, FLAT_W), F32), grid=(N_CHIPS, n_layers),
        in_specs=in_specs, out_specs=pl.BlockSpec((None, nc * rt, FLAT_W), lambda k, l: (k, base + l, 0)),
        input_output_aliases={0: 0} if flat4 is not None else {},
        compiler_params=_cparams(("parallel",) * 2), name=f"packgrad_{name}_{layer or 0}",
    )(*args)


def _adamw_shard(w, m, v, g_mine, g_other, c_arr, name):
    sg = _seg(name)
    rt = sg["rt"]
    rb = _flat_block(sg)
    per_half = sg["half"] // rt

    def half_of(l, j, i):
        return (rb(l, j, i) * rt) // sg["half"]

    def body(c_ref, w_ref, m_ref, v_ref, gm_ref, go_ref, g_ref, d_ref, mo_ref, vo_ref):
        is_mine = half_of(pl.program_id(0), pl.program_id(1), pl.program_id(2)) == c_ref[0]
        gv = jnp.where(is_mine, gm_ref[...], go_ref[...])
        g_ref[...] = gv
        d_ref[...], mo_ref[...], vo_ref[...] = _adam_update(w_ref[...], gv, m_ref[...], v_ref[...])

    nat = pl.BlockSpec((None, rt, FLAT_W), lambda l, j, i, c: (l, i, j))
    half = pl.BlockSpec((rt, FLAT_W), lambda l, j, i, c: (rb(l, j, i) - half_of(l, j, i) * per_half, 0))
    sds = jax.ShapeDtypeStruct(w.shape, F32)
    return pl.pallas_call(
        body, out_shape=[sds] * 4,
        grid_spec=pltpu.PrefetchScalarGridSpec(num_scalar_prefetch=1, grid=(sg["L"], sg["nc"], sg["ni"]),
                                               in_specs=[nat, nat, nat, half, half], out_specs=[nat] * 4),
        compiler_params=_cparams(("parallel", "parallel", "parallel")), name=f"adamw_{name}",
    )(c_arr, w, m, v, g_mine, g_other)


def _pack_small(vals, loss=None):
    parts = [vals[name].astype(F32).reshape(-1) for name, _ in _SMALL]
    if loss is not None:
        parts.append(loss.reshape(1))
    flat = jnp.concatenate(parts)
    return jnp.pad(flat, (0, SMALL_ROWS * 1024 - flat.shape[0])).reshape(SMALL_ROWS, 1024)


def _unpack_small(flat):
    flat = flat.reshape(-1)
    out = {}
    o = 0
    for name, shape in _SMALL:
        n = int(np.prod(shape))
        out[name] = flat[o:o + n].reshape(shape)
        o += n
    return out, flat[o]


_BIG = ("a_w_in", "a_w_out", "b_w_q", "b_w_out", "w_kv", "ffn_w_up", "ffn_w_down", "ffn_conv_w")
A_IN_PAD = 896


def _pack_weights(w, layout):
    flat = None
    for name in _LAYOUTS[layout]:
        t = _native3(w[name], name)
        for term in ((0, 1, 2) if name == "ffn_conv_w" else (None,)):
            flat = _slab_pack(flat, t, name, layout, term)
    return flat


def _early_weights(gathered, slots):
    a_in = _slab_unpack(gathered, slots, "a_w_in", "early")
    a_in = a_in.reshape(D, N_CHIPS, A_IN_PAD)[:, :, :772].reshape(D, N_CHIPS * 772)
    return dict(a_w_in=a_in, a_w_out=_slab_unpack(gathered, slots, "a_w_out", "early")[0])


def _late_weights(landed, slots, own):
    full = {name: _slab_unpack(landed, slots, name, "late", own) for name in _LAYOUTS["late"] if name != "ffn_conv_w"}
    sg = _seg("ffn_conv_w", "late")
    n1 = sg["nc"] * sg["rpad"]
    conv = slice(sg["off"], sg["off"] + CONV_TERMS * n1)
    conv_rows = jnp.concatenate([landed[:, conv], own[None, conv]], axis=0)
    per_chip = []
    for k in range(N_CHIPS):
        rows = lax.dynamic_index_in_dim(conv_rows, slots[k], axis=0, keepdims=False)
        terms = [_unslabs(rows[i * n1:(i + 1) * n1], 1, sg["R"], sg["C"], sg["rpad"]).astype(F32) for i in range(CONV_TERMS)]
        per_chip.append((terms[0] + terms[1]) + terms[2])
    cw = jnp.concatenate(per_chip, axis=2).reshape(2, 3, 2, FF).transpose(0, 2, 1, 3)
    return dict(b_w_q=full["b_w_q"][0], b_w_out=full["b_w_out"][0], w_kv=full["w_kv"][0], ffn_w_up=full["ffn_w_up"],
                ffn_w_down=full["ffn_w_down"], conv_w=cw)


def _shard_grads(g, layout):
    def full(name):
        if name == "a_w_in":
            a_in = jnp.pad(g[name].reshape(D, N_CHIPS, 772), ((0, 0), (0, 0), (0, A_IN_PAD - 772)))
            return a_in.reshape(1, D, N_CHIPS * A_IN_PAD)
        if name == "ffn_conv_w":
            sgc = _seg(name)
            return jnp.pad(g[name].reshape(1, sgc["R"], 2 * FF), ((0, 0), (0, sgc["rpad"] - sgc["R"]), (0, 0)))
        return g[name] if g[name].ndim == 3 else g[name][None]

    flat4 = None
    for name in _LAYOUTS[layout]:
        if isinstance(g[name], (list, tuple)):
            for layer, t in enumerate(g[name]):
                flat4 = _slab_pack_grad(flat4, t[None], name, layer)
        else:
            flat4 = _slab_pack_grad(flat4, full(name), name)
    return flat4


_WEIGHTS = ["a_w_in", "a_b_f", "a_w_out", "b_w_q", "b_w_out", "kv_norm_g", "w_kv", "mix_norm_g", "ffn_norm_g", "ffn_w_up",
            "ffn_conv_w", "ffn_conv_b", "ffn_w_down", "final_norm_g"]


def kernel(x, a_w_in, a_b_f, a_w_out, b_w_q, b_w_out, kv_norm_g, w_kv, mix_norm_g, ffn_norm_g, ffn_w_up, ffn_conv_w, ffn_conv_b, ffn_w_down, final_norm_g, loss_target, m_a_w_in, m_a_b_f, m_a_w_out, m_b_w_q, m_b_w_out, m_kv_norm_g, m_w_kv, m_mix_norm_g, m_ffn_norm_g, m_ffn_w_up, m_ffn_conv_w, m_ffn_conv_b, m_ffn_w_down, m_final_norm_g, v_a_w_in, v_a_b_f, v_a_w_out, v_b_w_q, v_b_w_out, v_kv_norm_g, v_w_kv, v_mix_norm_g, v_ffn_norm_g, v_ffn_w_up, v_ffn_conv_w, v_ffn_conv_b, v_ffn_w_down, v_final_norm_g):
    w = dict(a_w_in=a_w_in, a_b_f=a_b_f, a_w_out=a_w_out, b_w_q=b_w_q, b_w_out=b_w_out, kv_norm_g=kv_norm_g, w_kv=w_kv,
             mix_norm_g=mix_norm_g, ffn_norm_g=ffn_norm_g, ffn_w_up=ffn_w_up, ffn_conv_w=ffn_conv_w, ffn_conv_b=ffn_conv_b,
             ffn_w_down=ffn_w_down, final_norm_g=final_norm_g)
    m = dict(a_w_in=m_a_w_in, a_b_f=m_a_b_f, a_w_out=m_a_w_out, b_w_q=m_b_w_q, b_w_out=m_b_w_out, kv_norm_g=m_kv_norm_g,
             w_kv=m_w_kv, mix_norm_g=m_mix_norm_g, ffn_norm_g=m_ffn_norm_g, ffn_w_up=m_ffn_w_up, ffn_conv_w=m_ffn_conv_w,
             ffn_conv_b=m_ffn_conv_b, ffn_w_down=m_ffn_w_down, final_norm_g=m_final_norm_g)
    v = dict(a_w_in=v_a_w_in, a_b_f=v_a_b_f, a_w_out=v_a_w_out, b_w_q=v_b_w_q, b_w_out=v_b_w_out, kv_norm_g=v_kv_norm_g,
             w_kv=v_w_kv, mix_norm_g=v_mix_norm_g, ffn_norm_g=v_ffn_norm_g, ffn_w_up=v_ffn_w_up, ffn_conv_w=v_ffn_conv_w,
             ffn_conv_b=v_ffn_conv_b, ffn_w_down=v_ffn_w_down, final_norm_g=v_final_norm_g)

    c_arr = lax.axis_index("c").astype(jnp.int32).reshape(1)
    k_arr = (2 * lax.axis_index("x") + lax.axis_index("y")).astype(jnp.int32).reshape(1)
    xi, yi = lax.axis_index("x"), lax.axis_index("y")
    late_slots = jnp.stack([jnp.where(k == k_arr[0], OWN_SLOT, 2 * ((k & 1) ^ yi) + ((k >> 1) ^ xi) - 1)
                            for k in range(N_CHIPS)]).astype(jnp.int32)
    w_late = _pack_weights(w, "late")
    land = lax.empty((OWN_SLOT,) + w_late.shape, BF16)
    send_sems, recv_sems, w_thru, land_thru, token = _late_gather_start(w_late, land)
    w_early = _pack_weights(w, "early")
    early = _allgather_shards(w_early, _place_own(w_early, k_arr, "early_place_own"))
    p = _early_weights(early, jnp.arange(N_CHIPS, dtype=jnp.int32))
    cb = ffn_conv_b.reshape(2, 2, 1, FF)
    p.update(a_b_f=a_b_f, kv_norm_g=kv_norm_g, mix_norm_g=mix_norm_g + token[0, 0], ffn_norm_g=ffn_norm_g,
             final_norm_g=final_norm_g, conv_b=cb)

    def late_weights(after):
        own, landed = _late_gather_wait(send_sems, recv_sems, w_thru, land_thru, after)
        return _late_weights(landed, late_slots, own)

    started = {}

    def late_grads_ready(g_so_far):
        gflat = _shard_grads(g_so_far, "grad_late")
        pair = _pair_sum(gflat, _sibling_swap_half(gflat, "late"), c_arr, "late")
        land = lax.empty((3,) + pair.shape[1:], BF16)
        *handles, token = _chip_exchange_start(pair, land)
        started["handles"] = handles
        return token

    loss_part, grad_x, g = _local_step(x[0], loss_target[0], p, late_weights, late_grads_ready)

    halves = {}
    pair, landed = _chip_exchange_wait(*started["handles"], grad_x)
    g_mine = _chip_sum(pair, landed, k_arr, "late")
    halves["grad_late"] = (g_mine, _sibling_send(g_mine, "late"))
    gflat = _shard_grads(g, "grad_early")
    pair = _pair_sum(gflat, _sibling_swap_half(gflat, "early"), c_arr, "early")
    g_mine = _chip_sum(pair, _chip_exchange(pair, "early"), k_arr, "early")
    halves["grad_early"] = (g_mine, _sibling_send(g_mine, "early"))
    small, loss = _unpack_small(_allreduce_small(_pack_small(g, loss_part)))

    big = [{}, {}, {}, {}]
    for name in _BIG:
        sg = _seg(name)
        g_mine, g_other = halves[_grad_layout(name)]
        res = _adamw_shard(_native3(w[name], name), _native3(m[name], name), _native3(v[name], name), g_mine, g_other,
                           c_arr, name)
        for store, t in zip(big, res):
            store[name] = t[:, :sg["R"], :].reshape(_SHARD_SHAPES[name])
    dws, mns, vns = _adamw(_pack_small(w), _pack_small(small), _pack_small(m), _pack_small(v), "adamw_small")
    sml = [small] + [_unpack_small(t)[0] for t in (dws, mns, vns)]
    outs = [loss, grad_x[None]]
    for b, s in zip(big, sml):
        outs += [b[n] if n in b else s[n] for n in _WEIGHTS]
    return tuple(outs)
```

```python
import numpy as np
import jax
import jax.numpy as jnp
from jax import lax
from jax.experimental import pallas as pl
from jax.experimental.pallas import tpu as pltpu

F32 = jnp.float32
BF16 = jnp.bfloat16
MESH = pl.DeviceIdType.MESH

S = 4096
D = 1024
A_HEADS = 16
HEAD_DIM = 64
QKV_W = 3 * A_HEADS * HEAD_DIM
B_GROUPS = ((128, 1), (512, 4), (2048, 16))
B_HPG = 8
B_Q_W = 3 * B_HPG * HEAD_DIM
B_OUT_W = B_HPG * HEAD_DIM
B_KV_W = 2 * B_Q_W
B_WIN = 128
FF = 2816
RMS_EPS = 1e-6
SCALE = HEAD_DIM ** -0.5
N_CHIPS = 4

ADAM_LR, ADAM_B1, ADAM_B2, ADAM_EPS, ADAM_WD, ADAM_STEP = 0.001, 0.9, 0.999, 1e-08, 0.01, 10

V7X_VMEM_LIMIT = 48 * 1024 * 1024
LANES = 128
NEG_INF = float("-inf")

FLAT_W = LANES
_SEGS = (("ffn_w_down", 2, 704, 1024, 704), ("a_w_in", 1, 1024, 772, 1024), ("a_w_out", 1, 256, 1024, 256),
         ("b_w_q", 1, 1024, 384, 1024), ("b_w_out", 1, 512, 256, 512), ("w_kv", 1, 1024, 768, 1024),
         ("ffn_w_up", 2, 1024, 1408, 1024), ("ffn_conv_w", 1, 6, 1408, 16))
CONV_TERMS = 3


def _seg_rows(name, L, R, C, rpad):
    return (CONV_TERMS if name == "ffn_conv_w" else 1) * L * (-(-C // FLAT_W)) * rpad


SMALL_ROWS = 24


def _cparams(sem=None, **kw):
    return pltpu.CompilerParams(dimension_semantics=sem, vmem_limit_bytes=V7X_VMEM_LIMIT, **kw)


_DN = {"nn": (((1,), (0,)), ((), ())), "nt": (((1,), (1,)), ((), ())), "tn": (((0,), (0,)), ((), ()))}


def _mm(a, b, *, mode, tm, tn, tk, name, out_dtype=F32, res=None, a_split=0, b_split=0, o_split=0):
    if mode == "tn":
        K = a.shape[0]
        M = a.shape[1]
    else:
        M = a.shape[-2]
        K = a.shape[-1] * (2 if a_split else 1)
    if mode == "nt":
        N = b.shape[0]
    else:
        N = b.shape[-1] * (2 if b_split else 1)
    assert M % tm == 0 and N % tn == 0 and K % tk == 0, (name, M, N, K, tm, tn, tk)
    nk = K // tk

    if mode == "tn":
        a_spec = pl.BlockSpec((tk, tm), lambda i, j, k: (k, i))
    elif a_split:
        a_spec = pl.BlockSpec((None, tm, tk), lambda i, j, k: (k // a_split, i, k % a_split))
    else:
        a_spec = pl.BlockSpec((tm, tk), lambda i, j, k: (i, k))
    if mode == "nt":
        b_spec = pl.BlockSpec((tn, tk), lambda i, j, k: (j, k))
    elif b_split:
        b_spec = pl.BlockSpec((None, tk, tn), lambda i, j, k: (j // b_split, k, j % b_split))
    else:
        b_spec = pl.BlockSpec((tk, tn), lambda i, j, k: (k, j))
    if o_split:
        o_spec = pl.BlockSpec((None, tm, tn), lambda i, j, k: (j // o_split, i, j % o_split))
        out_shape = jax.ShapeDtypeStruct((2, M, N // 2), out_dtype)
    else:
        o_spec = pl.BlockSpec((tm, tn), lambda i, j, k: (i, j))
        out_shape = jax.ShapeDtypeStruct((M, N), out_dtype)
    in_specs = [a_spec, b_spec]
    args = [a, b]
    if res is not None:
        in_specs.append(pl.BlockSpec((tm, tn), lambda i, j, k: (i, j)))
        args.append(res)

    def body(*refs):
        if res is not None:
            a_ref, b_ref, r_ref, o_ref = refs[:4]
        else:
            a_ref, b_ref, o_ref = refs[:3]
            r_ref = None
        p = lax.dot_general(a_ref[...].astype(BF16), b_ref[...].astype(BF16), _DN[mode], preferred_element_type=F32)

        def finish(r):
            if r_ref is not None:
                r = r + r_ref[...]
            o_ref[...] = r.astype(out_dtype)

        if nk == 1:
            finish(p)
        else:
            acc = refs[-1]
            k = pl.program_id(2)

            @pl.when(k == 0)
            def _():
                acc[...] = p

            @pl.when(k > 0)
            def _():
                acc[...] += p

            @pl.when(k == nk - 1)
            def _():
                finish(acc[...])

    return pl.pallas_call(
        body, out_shape=out_shape, grid=(M // tm, N // tn, nk), in_specs=in_specs, out_specs=o_spec,
        scratch_shapes=[pltpu.VMEM((tm, tn), F32)] if nk > 1 else [],
        compiler_params=_cparams(("parallel", "parallel", "arbitrary")), name=name,
    )(*args)


NORM_ROWS = 256


def _rms_fwd(x, gains, name):
    n = len(gains)

    def body(x_ref, *refs):
        xv = x_ref[...]
        y = xv * lax.rsqrt(jnp.mean(xv * xv, axis=-1, keepdims=True) + RMS_EPS)
        for g_ref, o_ref in zip(refs[:n], refs[n:]):
            o_ref[...] = (y * g_ref[...]).astype(BF16)

    row = pl.BlockSpec((NORM_ROWS, D), lambda i: (i, 0))
    return pl.pallas_call(
        body, out_shape=[jax.ShapeDtypeStruct((S, D), BF16)] * n, grid=(S // NORM_ROWS,),
        in_specs=[row] + [pl.BlockSpec((1, D), lambda i: (0, 0))] * n, out_specs=[row] * n,
        compiler_params=_cparams(("parallel",)), name=name,
    )(x, *[g.reshape(1, D) for g in gains])


def _rms_bwd(x, dres, pairs, name):
    n = len(pairs)

    def body(*refs):
        x_ref, dres_ref = refs[0], refs[1]
        g_refs = refs[2:2 + 2 * n:2]
        dh_refs = refs[3:3 + 2 * n:2]
        dx_ref, dxb_ref = refs[2 + 2 * n], refs[3 + 2 * n]
        dg_refs = refs[4 + 2 * n:]
        i = pl.program_id(0)
        xv = x_ref[...]
        r = lax.rsqrt(jnp.mean(xv * xv, axis=-1, keepdims=True) + RMS_EPS)
        y = xv * r
        dx = dres_ref[...]
        for g_ref, dh_ref, dg_ref in zip(g_refs, dh_refs, dg_refs):
            dh = dh_ref[...]
            dy = dh * g_ref[...]
            dx = dx + r * (dy - y * jnp.mean(dy * y, axis=-1, keepdims=True))
            part = jnp.sum(dh * y, axis=0, keepdims=True)

            @pl.when(i == 0)
            def _():
                dg_ref[...] = part

            @pl.when(i > 0)
            def _():
                dg_ref[...] += part

        dx_ref[...] = dx
        dxb_ref[...] = dx.astype(BF16)

    row = pl.BlockSpec((NORM_ROWS, D), lambda i: (i, 0))
    vec = pl.BlockSpec((1, D), lambda i: (0, 0))
    in_specs = [row, row]
    args = [x, dres]
    for g, dh in pairs:
        in_specs += [vec, row]
        args += [g.reshape(1, D), dh]
    outs = pl.pallas_call(
        body,
        out_shape=[jax.ShapeDtypeStruct((S, D), F32), jax.ShapeDtypeStruct((S, D), BF16)]
        + [jax.ShapeDtypeStruct((1, D), F32)] * n,
        grid=(S // NORM_ROWS,), in_specs=in_specs, out_specs=[row, row] + [vec] * n,
        compiler_params=_cparams(("arbitrary",)), name=name,
    )(*args)
    return outs[0], outs[1], list(outs[2:])


def _loss_head(x, g, target, name):
    def body(x_ref, g_ref, t_ref, loss_ref, dx_ref, dxb_ref, dg_ref):
        i = pl.program_id(0)
        xv = x_ref[...]
        gv = g_ref[...]
        r = lax.rsqrt(jnp.mean(xv * xv, axis=-1, keepdims=True) + RMS_EPS)
        y = xv * r
        err = y * gv - t_ref[...]
        lpart = jnp.broadcast_to(jnp.sum(err * err, keepdims=True) * (0.5 / D), (1, LANES))
        dh = err * (1.0 / D)
        dy = dh * gv
        dx = r * (dy - y * jnp.mean(dy * y, axis=-1, keepdims=True))
        part = jnp.sum(dh * y, axis=0, keepdims=True)

        @pl.when(i == 0)
        def _():
            dg_ref[...] = part
            loss_ref[...] = lpart

        @pl.when(i > 0)
        def _():
            dg_ref[...] += part
            loss_ref[...] += lpart

        dx_ref[...] = dx
        dxb_ref[...] = dx.astype(BF16)

    row = pl.BlockSpec((NORM_ROWS, D), lambda i: (i, 0))
    vec = pl.BlockSpec((1, D), lambda i: (0, 0))
    return pl.pallas_call(
        body,
        out_shape=[jax.ShapeDtypeStruct((1, LANES), F32), jax.ShapeDtypeStruct((S, D), F32),
                   jax.ShapeDtypeStruct((S, D), BF16), jax.ShapeDtypeStruct((1, D), F32)],
        grid=(S // NORM_ROWS,), in_specs=[row, vec, row],
        out_specs=[pl.BlockSpec((1, LANES), lambda i: (0, 0)), row, row, vec],
        compiler_params=_cparams(("arbitrary",)), name=name,
    )(x, g.reshape(1, D), target)


SCAN_ROWS = 256


def _split3(v):
    hi = v.astype(BF16)
    r1 = v - hi.astype(F32)
    mid = r1.astype(BF16)
    lo = (r1 - mid.astype(F32)).astype(BF16)
    return hi, mid, lo


def _tri_dot(tri, v):
    hi, mid, lo = _split3(v)
    dn = _DN["nn"]
    return (lax.dot_general(tri, hi, dn, preferred_element_type=F32)
            + lax.dot_general(tri, mid, dn, preferred_element_type=F32)
            + lax.dot_general(tri, lo, dn, preferred_element_type=F32))


def _log_sigmoid(z):
    return jnp.minimum(z, 0.0) - jnp.log(1.0 + jnp.exp(-jnp.abs(z)))


GATE_LANES = 6


def _gate_lane_tables():
    pq = np.zeros((3 * LANES, A_HEADS * HEAD_DIM), np.float32)
    pk = np.zeros((3 * LANES, A_HEADS * HEAD_DIM), np.float32)
    one_q = np.zeros((1, A_HEADS * HEAD_DIM), np.float32)
    one_k = np.zeros((1, A_HEADS * HEAD_DIM), np.float32)
    for h in range(A_HEADS):
        pos = (h // 2) * LANES + (HEAD_DIM if h % 2 == 0 else 0)
        for term in range(3):
            pq[term * LANES + h, pos + term] = 1.0
            pk[term * LANES + h, pos + 3 + term] = -1.0
        one_q[0, pos + 3:pos + GATE_LANES] = 1.0
        one_k[0, pos:pos + 3] = 1.0
    return jnp.asarray(pq, BF16), jnp.asarray(pk, BF16), jnp.asarray(one_q), jnp.asarray(one_k)


def _fgate_fwd(pf, bias, name):
    tri = jnp.tril(jnp.ones((SCAN_ROWS, SCAN_ROWS), F32)).astype(BF16)
    pq, pk, one_q, one_k = _gate_lane_tables()

    def body(pf_ref, b_ref, tri_ref, pq_ref, pk_ref, oq_ref, ok_ref, aq_ref, ak_ref, c_sc):
        carry = jnp.zeros((1, LANES), F32)
        for blk in range(S // SCAN_ROWS):
            rows = pl.ds(blk * SCAN_ROWS, SCAN_ROWS)
            lf = _log_sigmoid(pf_ref[rows, :] + b_ref[...])
            c_sc[...] = _tri_dot(tri_ref[...], lf) + carry
            carry = c_sc[pl.ds(SCAN_ROWS - 1, 1), :]
            terms = jnp.concatenate(_split3(c_sc[...]), axis=1)
            aq = lax.dot_general(terms, pq_ref[...], _DN["nn"], preferred_element_type=F32) + oq_ref[...]
            ak = lax.dot_general(terms, pk_ref[...], _DN["nn"], preferred_element_type=F32) + ok_ref[...]
            aq_ref[rows, :] = aq.astype(BF16)
            ak_ref[rows, :] = ak.astype(BF16)

    wide = jax.ShapeDtypeStruct((S, A_HEADS * HEAD_DIM), BF16)
    return pl.pallas_call(
        body, out_shape=[wide, wide], scratch_shapes=[pltpu.VMEM((SCAN_ROWS, LANES), F32)],
        compiler_params=_cparams(), name=name,
    )(pf, bias, tri, pq, pk, one_q, one_k)


def _fgate_bwd(pf, bias, dc_key, dc_query, name):
    triu = jnp.triu(jnp.ones((SCAN_ROWS, SCAN_ROWS), F32)).astype(BF16)

    def body(pf_ref, b_ref, dck_ref, dcq_ref, tri_ref, dpf_ref, db_ref, dlf_ref):
        carry = jnp.zeros((1, LANES), F32)
        db = jnp.zeros((1, LANES), F32)
        lane = lax.broadcasted_iota(jnp.int32, (SCAN_ROWS, LANES), 1)
        for blk in reversed(range(S // SCAN_ROWS)):
            rows = pl.ds(blk * SCAN_ROWS, SCAN_ROWS)
            dc = dck_ref[rows, :] + dcq_ref[rows, :]
            dlf_ref[rows, :] = _tri_dot(tri_ref[...], dc) + carry
            carry = dlf_ref[pl.ds(blk * SCAN_ROWS, 1), :]
            z = pf_ref[rows, :] + b_ref[...]
            e = jnp.exp(-jnp.abs(z))
            sig_neg = jnp.where(z >= 0.0, e, 1.0) / (1.0 + e)
            dz = jnp.where(lane < A_HEADS, dlf_ref[rows, :] * sig_neg, 0.0)
            dpf_ref[rows, :] = dz.astype(BF16)
            db = db + jnp.sum(dz, axis=0, keepdims=True)
        db_ref[...] = db

    return pl.pallas_call(
        body, out_shape=[jax.ShapeDtypeStruct((S, LANES), BF16), jax.ShapeDtypeStruct((1, LANES), F32)],
        scratch_shapes=[pltpu.VMEM((S, LANES), F32)],
        compiler_params=_cparams(), name=name,
    )(pf, bias, dc_key, dc_query, triu)


FOX_T = 512


def _first_head(shape):
    return lax.broadcasted_iota(jnp.int32, shape, len(shape) - 1) < HEAD_DIM


def _each_head(x, lo):
    zero = jnp.zeros_like(x)
    return jnp.where(lo, x, zero), jnp.where(lo, zero, x)


def _fox_pair_fwd(qkv, aug_q, aug_k, name):
    T = FOX_T
    nq = S // T
    NP = A_HEADS // 2

    def body(q_ref, k_ref, v_ref, aq_ref, ak_ref, o_ref, lse_ref, m_sc, l_sc, acc_sc):
        i = pl.program_id(1)
        j = pl.program_id(2)
        lo = _first_head((T, LANES))

        @pl.when(j == 0)
        def _():
            m_sc[...] = jnp.full((2, T, LANES), NEG_INF, F32)
            l_sc[...] = jnp.zeros((2, T, LANES), F32)
            acc_sc[...] = jnp.zeros((T, LANES), F32)

        def step(diagonal):
            qs = q_ref[...] * jnp.asarray(SCALE, BF16)
            aq, ak, kv = aq_ref[...], ak_ref[...], k_ref[...]
            q2 = (jnp.where(lo, qs, aq), jnp.where(lo, aq, qs))
            k2 = (jnp.where(lo, kv, ak), jnp.where(lo, ak, kv))
            if diagonal:
                causal = lax.broadcasted_iota(jnp.int32, (T, T), 0) >= lax.broadcasted_iota(jnp.int32, (T, T), 1)
            pv, alphas = None, []
            for h, vh in enumerate(_each_head(v_ref[...], lo)):
                s = lax.dot_general(q2[h], k2[h], _DN["nt"], preferred_element_type=F32)
                if diagonal:
                    s = jnp.where(causal, s, NEG_INF)
                m_prev = m_sc[h]
                m_new = jnp.maximum(m_prev, jnp.max(s, axis=1, keepdims=True))
                alpha = jnp.exp(m_prev - m_new)
                p = jnp.exp(s - jnp.tile(m_new, (1, T // LANES)))
                l_sc[h] = alpha * l_sc[h] + jnp.sum(p, axis=1, keepdims=True)
                m_sc[h] = m_new
                d = lax.dot_general(p.astype(BF16), vh, _DN["nn"], preferred_element_type=F32)
                pv = d if pv is None else pv + d
                alphas.append(alpha)
            acc_sc[...] = jnp.where(lo, alphas[0], alphas[1]) * acc_sc[...] + pv

        @pl.when(j < i)
        def _():
            step(False)

        @pl.when(j == i)
        def _():
            step(True)
            o_ref[...] = (acc_sc[...] * jnp.where(lo, 1.0 / l_sc[0], 1.0 / l_sc[1])).astype(BF16)
            for h in range(2):
                lse_ref[h] = (m_sc[h] + jnp.log(l_sc[h]))[:, 0:1]

    qs_ = pl.BlockSpec((T, LANES), lambda p, i, j: (i, p))
    ks = pl.BlockSpec((T, LANES), lambda p, i, j: (jnp.minimum(i, j), NP + p))
    vs = pl.BlockSpec((T, LANES), lambda p, i, j: (jnp.minimum(i, j), 2 * NP + p))
    aks = pl.BlockSpec((T, LANES), lambda p, i, j: (jnp.minimum(i, j), p))
    col = pl.BlockSpec((2, T, 1), lambda p, i, j: (p, i, 0))
    return pl.pallas_call(
        body, out_shape=[jax.ShapeDtypeStruct((S, A_HEADS * HEAD_DIM), BF16), jax.ShapeDtypeStruct((A_HEADS, S, 1), F32)],
        grid=(NP, nq, nq), in_specs=[qs_, ks, vs, qs_, aks], out_specs=[qs_, col],
        scratch_shapes=[pltpu.VMEM((2, T, LANES), F32), pltpu.VMEM((2, T, LANES), F32), pltpu.VMEM((T, LANES), F32)],
        compiler_params=_cparams(("parallel", "parallel", "arbitrary")), name=name,
    )(qkv, qkv, qkv, aug_q, aug_k)


def _fox_pair_bwd(qkv, do, lse_row, delta_row, aug_q, aug_k, name):
    T = FOX_T
    nq = S // T
    NP = A_HEADS // 2

    def body(q_ref, k_ref, v_ref, do_ref, lse_ref, dl_ref, aq_ref, ak_ref, dq_ref, dk_ref, dv_ref, dc_ref, dcq_ref,
             dq_sc, dk_sc, dv_sc, dc_sc):
        j = pl.program_id(1)
        i = pl.program_id(2)
        lo = _first_head((T, LANES))

        @pl.when(jnp.logical_and(j == 0, i == 0))
        def _():
            dq_sc[...] = jnp.zeros((S, LANES), F32)
            dcq_ref[...] = jnp.zeros((2, nq, 1, T), F32)

        @pl.when(i == j)
        def _():
            dk_sc[...] = jnp.zeros((T, LANES), F32)
            dv_sc[...] = jnp.zeros((T, LANES), F32)
            dc_sc[...] = jnp.zeros((2, T, 1), F32)

        def step(diagonal):
            qv = q_ref[...]
            kv = k_ref[...]
            dov = do_ref[...].astype(BF16)
            qs = qv * jnp.asarray(SCALE, BF16)
            aq, ak = aq_ref[...], ak_ref[...]
            q2 = (jnp.where(lo, qs, aq), jnp.where(lo, aq, qs))
            k2 = (jnp.where(lo, kv, ak), jnp.where(lo, ak, kv))
            if diagonal:
                causal = lax.broadcasted_iota(jnp.int32, (T, T), 1) >= lax.broadcasted_iota(jnp.int32, (T, T), 0)
            dv = dk = dq = None
            for h, (kh, vh, qh, doh) in enumerate(zip(_each_head(kv, lo), _each_head(v_ref[...], lo),
                                                      _each_head(qv, lo), _each_head(dov, lo))):
                st = lax.dot_general(k2[h], q2[h], _DN["nt"], preferred_element_type=F32)
                if diagonal:
                    st = jnp.where(causal, st, NEG_INF)
                pt = jnp.exp(st - lse_ref[h])
                d = lax.dot_general(pt.astype(BF16), doh, _DN["nn"], preferred_element_type=F32)
                dv = d if dv is None else dv + d
                dpt = lax.dot_general(vh, dov, _DN["nt"], preferred_element_type=F32)
                dst = pt * (dpt - dl_ref[h])
                dc_sc[h] -= jnp.sum(dst, axis=1, keepdims=True)
                dcq_ref[h, i] += jnp.sum(dst, axis=0, keepdims=True)
                dsb = (dst * SCALE).astype(BF16)
                d = lax.dot_general(dsb, qh, _DN["nn"], preferred_element_type=F32)
                dk = d if dk is None else dk + d
                d = lax.dot_general(dsb, kh, _DN["tn"], preferred_element_type=F32)
                dq = d if dq is None else dq + d
            dv_sc[...] += dv
            dk_sc[...] += dk
            rows = pl.ds(pl.multiple_of(i * T, T), T)
            dq_sc[rows, :] += dq

        @pl.when(i > j)
        def _():
            step(False)

        @pl.when(i == j)
        def _():
            step(True)

        @pl.when(i == nq - 1)
        def _():
            dk_ref[...] = dk_sc[...].astype(BF16)
            dv_ref[...] = dv_sc[...].astype(BF16)
            dc_ref[...] = dc_sc[...]

        @pl.when(jnp.logical_and(j == nq - 1, i == nq - 1))
        def _():
            dq_ref[...] = dq_sc[...].astype(BF16)

    qs = pl.BlockSpec((T, LANES), lambda p, j, i: (jnp.maximum(i, j), p))
    qrow = pl.BlockSpec((2, 1, T), lambda p, j, i: (p, 0, jnp.maximum(i, j)))
    ks = pl.BlockSpec((T, LANES), lambda p, j, i: (j, NP + p))
    vs = pl.BlockSpec((T, LANES), lambda p, j, i: (j, 2 * NP + p))
    kout = pl.BlockSpec((T, LANES), lambda p, j, i: (j, p))
    kcol = pl.BlockSpec((2, T, 1), lambda p, j, i: (p, j, 0))
    dqs = pl.BlockSpec((S, LANES), lambda p, j, i: (0, p))
    dcqs = pl.BlockSpec((2, nq, 1, T), lambda p, j, i: (p, 0, 0, 0))
    wide = jax.ShapeDtypeStruct((S, A_HEADS * HEAD_DIM), BF16)
    return pl.pallas_call(
        body,
        out_shape=[wide, wide, wide, jax.ShapeDtypeStruct((A_HEADS, S, 1), F32),
                   jax.ShapeDtypeStruct((A_HEADS, nq, 1, T), F32)],
        grid=(NP, nq, nq), in_specs=[qs, ks, vs, qs, qrow, qrow, qs, kout], out_specs=[dqs, kout, kout, kcol, dcqs],
        scratch_shapes=[pltpu.VMEM((S, LANES), F32), pltpu.VMEM((T, LANES), F32), pltpu.VMEM((T, LANES), F32),
                        pltpu.VMEM((2, T, 1), F32)],
        compiler_params=_cparams(("parallel", "arbitrary", "arbitrary")), name=name,
    )(qkv, qkv, qkv, do, lse_row, delta_row, aug_q, aug_k)


def _pair_rowdot(a, b, name):
    n = a.shape[1] // HEAD_DIM
    T = 1024

    def body(a_ref, b_ref, o_ref):
        prod = a_ref[...].astype(F32) * b_ref[...].astype(F32)
        lo = _first_head(prod.shape)
        o_ref[0] = jnp.sum(jnp.where(lo, prod, 0.0), axis=1, keepdims=True)
        o_ref[1] = jnp.sum(jnp.where(lo, 0.0, prod), axis=1, keepdims=True)

    blk = pl.BlockSpec((T, LANES), lambda p, i: (i, p))
    return pl.pallas_call(
        body, out_shape=jax.ShapeDtypeStruct((n, S, 1), F32), grid=(n // 2, S // T), in_specs=[blk, blk],
        out_specs=pl.BlockSpec((2, T, 1), lambda p, i: (p, i, 0)),
        compiler_params=_cparams(("parallel", "parallel")), name=name,
    )(a, b)


W = B_WIN
N_HG = 3 * B_HPG
N_BLK = S // W


def _dil_tables():
    slopes = np.exp2((-8.0 * np.arange(1, N_HG + 1, dtype=np.float32) / N_HG).astype(np.float32)).astype(np.float32)
    dil = np.repeat(np.array([d for _, d in B_GROUPS], np.float32), B_HPG)
    coef = (slopes * dil).astype(np.float32)
    nbs = np.repeat(np.array([S // d // W for _, d in B_GROUPS], np.int32), B_HPG)
    return jnp.asarray(coef), jnp.asarray(nbs)


DIL_SUB = 8
DIL_ROWS = DIL_SUB * W
DIL_STEPS = S // DIL_ROWS


def _dil_bias(coef, transposed):
    row = lax.broadcasted_iota(jnp.int32, (W, 2 * W), 0)
    col = lax.broadcasted_iota(jnp.int32, (W, 2 * W), 1)
    dist = (col - row) if transposed else (row + W - col)
    valid = jnp.logical_and(dist >= 0, dist <= W)
    return jnp.where(valid, -coef * dist.astype(F32), NEG_INF), col


NPG = B_HPG // 2
GROUP_W = B_HPG * HEAD_DIM


def _dil_pair_specs(qoff, koff, voff):
    prev_blk = lambda n: jnp.maximum(n * DIL_SUB - 1, 0)
    next_blk = lambda n: jnp.minimum((n + 1) * DIL_SUB, N_BLK - 1)
    return dict(
        o=pl.BlockSpec((DIL_ROWS, LANES), lambda h, n: (n, h)),
        o_next=pl.BlockSpec((W, LANES), lambda h, n: (next_blk(n), h)),
        q=pl.BlockSpec((DIL_ROWS, LANES), lambda h, n: (n, qoff + h)),
        q_next=pl.BlockSpec((W, LANES), lambda h, n: (next_blk(n), qoff + h)),
        k=pl.BlockSpec((DIL_ROWS, LANES), lambda h, n: (n, koff + h)),
        k_prev=pl.BlockSpec((W, LANES), lambda h, n: (prev_blk(n), koff + h)),
        v=pl.BlockSpec((DIL_ROWS, LANES), lambda h, n: (n, voff + h)),
        v_prev=pl.BlockSpec((W, LANES), lambda h, n: (prev_blk(n), voff + h)),
        col=pl.BlockSpec((2, DIL_ROWS, 1), lambda h, n: (h, n, 0)),
        row=pl.BlockSpec((2, 1, DIL_ROWS), lambda h, n: (h, 0, n)),
        row_next=pl.BlockSpec((2, 1, W), lambda h, n: (h, 0, next_blk(n))),
        smem=pl.BlockSpec(memory_space=pltpu.SMEM))


def _dil_pair_fwd(g, q, k, v, qoff, koff, voff, name):
    coef_t, nbs_t = _dil_tables()

    def body(coef_ref, nbs_ref, q_ref, kh_ref, k_ref, vh_ref, v_ref, o_ref, lse_ref, kf, vf):
        hp = pl.program_id(0)
        n = pl.program_id(1)
        nbs = nbs_ref[B_HPG * g + 2 * hp]
        kf[0:W, :] = kh_ref[...]
        kf[W:, :] = k_ref[...]
        vf[0:W, :] = vh_ref[...]
        vf[W:, :] = v_ref[...]
        biases = [_dil_bias(coef_ref[B_HPG * g + 2 * hp + h], False) for h in range(2)]
        col = biases[0][1]
        lo = _first_head((W, LANES))
        lo2 = _first_head((2 * W, LANES))
        for b in range(DIL_SUB):
            first = lax.rem(n * DIL_SUB + b, nbs) == 0
            rows = slice(b * W, (b + 1) * W)
            both = slice(b * W, (b + 2) * W)
            qv = q_ref[rows, :]
            acc, inv = None, []
            for h, (kh, vh) in enumerate(zip(_each_head(kf[both, :], lo2), _each_head(vf[both, :], lo2))):
                s = lax.dot_general(qv, kh, _DN["nt"], preferred_element_type=F32) * SCALE + biases[h][0]
                s = jnp.where(jnp.logical_and(first, col < W), NEG_INF, s)
                m = jnp.max(s, axis=1, keepdims=True)
                p = jnp.exp(s - m)
                l = jnp.sum(p, axis=1, keepdims=True)
                d = lax.dot_general(p.astype(BF16), vh, _DN["nn"], preferred_element_type=F32)
                acc = d if acc is None else acc + d
                inv.append(1.0 / l)
                lse_ref[h, rows, :] = m + jnp.log(l)
            o_ref[rows, :] = acc * jnp.where(lo, inv[0], inv[1])

    sp = _dil_pair_specs(qoff, koff, voff)
    return pl.pallas_call(
        body, out_shape=[jax.ShapeDtypeStruct((S, GROUP_W), F32), jax.ShapeDtypeStruct((B_HPG, S, 1), F32)],
        grid=(NPG, DIL_STEPS), in_specs=[sp["smem"], sp["smem"], sp["q"], sp["k_prev"], sp["k"], sp["v_prev"], sp["v"]],
        out_specs=[sp["o"], sp["col"]], scratch_shapes=[pltpu.VMEM((DIL_ROWS + W, LANES), BF16)] * 2,
        compiler_params=_cparams(("parallel", "parallel")), name=name,
    )(coef_t, nbs_t, q, k, k, v, v)


def _dil_pair_merge(os, lses, name):
    T = 1024

    def body(o0_ref, o1_ref, o2_ref, l0_ref, l1_ref, l2_ref, om_ref, omb_ref, l_ref):
        lo = _first_head((T, LANES))
        weights = []
        for h in range(2):
            l0, l1, l2 = l0_ref[h], l1_ref[h], l2_ref[h]
            m = jnp.maximum(jnp.maximum(l0, l1), l2)
            e0, e1, e2 = jnp.exp(l0 - m), jnp.exp(l1 - m), jnp.exp(l2 - m)
            den = e0 + e1 + e2
            weights.append((e0 / den, e1 / den, e2 / den))
            l_ref[h] = m + jnp.log(den)
        om = (jnp.where(lo, weights[0][0], weights[1][0]) * o0_ref[...]
              + jnp.where(lo, weights[0][1], weights[1][1]) * o1_ref[...]
              + jnp.where(lo, weights[0][2], weights[1][2]) * o2_ref[...])
        om_ref[...] = om
        omb_ref[...] = om.astype(BF16)

    ob = pl.BlockSpec((T, LANES), lambda p, i: (i, p))
    lb = pl.BlockSpec((2, T, 1), lambda p, i: (p, i, 0))
    return pl.pallas_call(
        body,
        out_shape=[jax.ShapeDtypeStruct((S, B_OUT_W), F32), jax.ShapeDtypeStruct((S, B_OUT_W), BF16),
                   jax.ShapeDtypeStruct((B_HPG, S, 1), F32)],
        grid=(NPG, S // T), in_specs=[ob] * 3 + [lb] * 3, out_specs=[ob, ob, lb],
        compiler_params=_cparams(("parallel", "parallel")), name=name,
    )(*os, *lses)


def _dil_pair_dq(g, q, k, v, qoff, koff, voff, do, lcol, dcol, name):
    coef_t, nbs_t = _dil_tables()

    def body(coef_ref, nbs_ref, q_ref, kh_ref, k_ref, vh_ref, v_ref, do_ref, l_ref, d_ref, dq_ref, kf, vf):
        hp = pl.program_id(0)
        n = pl.program_id(1)
        nbs = nbs_ref[B_HPG * g + 2 * hp]
        kf[0:W, :] = kh_ref[...]
        kf[W:, :] = k_ref[...]
        vf[0:W, :] = vh_ref[...]
        vf[W:, :] = v_ref[...]
        biases = [_dil_bias(coef_ref[B_HPG * g + 2 * hp + h], False) for h in range(2)]
        col = biases[0][1]
        lo2 = _first_head((2 * W, LANES))
        for b in range(DIL_SUB):
            first = lax.rem(n * DIL_SUB + b, nbs) == 0
            rows = slice(b * W, (b + 1) * W)
            both = slice(b * W, (b + 2) * W)
            qv = q_ref[rows, :]
            dov = do_ref[rows, :]
            acc = None
            for h, (kh, vh) in enumerate(zip(_each_head(kf[both, :], lo2), _each_head(vf[both, :], lo2))):
                s = lax.dot_general(qv, kh, _DN["nt"], preferred_element_type=F32) * SCALE + biases[h][0]
                s = jnp.where(jnp.logical_and(first, col < W), NEG_INF, s)
                p = jnp.exp(s - l_ref[h, rows, :])
                dp = lax.dot_general(dov, vh, _DN["nt"], preferred_element_type=F32)
                ds = (p * (dp - d_ref[h, rows, :]) * SCALE).astype(BF16)
                d = lax.dot_general(ds, kh, _DN["nn"], preferred_element_type=F32)
                acc = d if acc is None else acc + d
            dq_ref[rows, :] = acc.astype(BF16)

    sp = _dil_pair_specs(qoff, koff, voff)
    return pl.pallas_call(
        body, out_shape=jax.ShapeDtypeStruct((S, GROUP_W), BF16), grid=(NPG, DIL_STEPS),
        in_specs=[sp["smem"], sp["smem"], sp["q"], sp["k_prev"], sp["k"], sp["v_prev"], sp["v"], sp["o"], sp["col"],
                  sp["col"]],
        out_specs=sp["o"], scratch_shapes=[pltpu.VMEM((DIL_ROWS + W, LANES), BF16)] * 2,
        compiler_params=_cparams(("parallel", "parallel")), name=name,
    )(coef_t, nbs_t, q, k, k, v, v, do, lcol, dcol)


def _dil_pair_dkv(g, q, k, v, qoff, koff, voff, do, lrow, drow, name):
    coef_t, nbs_t = _dil_tables()

    def body(coef_ref, nbs_ref, k_ref, v_ref, q_ref, qn_ref, do_ref, don_ref, l_ref, ln_ref, d_ref, dn_ref,
             dk_ref, dv_ref, qf, dof, lf, df):
        hp = pl.program_id(0)
        n = pl.program_id(1)
        nbs = nbs_ref[B_HPG * g + 2 * hp]
        qf[0:DIL_ROWS, :] = q_ref[...]
        qf[DIL_ROWS:, :] = qn_ref[...]
        dof[0:DIL_ROWS, :] = do_ref[...]
        dof[DIL_ROWS:, :] = don_ref[...]
        lf[:, :, 0:DIL_ROWS] = l_ref[...]
        lf[:, :, DIL_ROWS:] = ln_ref[...]
        df[:, :, 0:DIL_ROWS] = d_ref[...]
        df[:, :, DIL_ROWS:] = dn_ref[...]
        biases = [_dil_bias(coef_ref[B_HPG * g + 2 * hp + h], True) for h in range(2)]
        col = biases[0][1]
        lo = _first_head((W, LANES))
        lo2 = _first_head((2 * W, LANES))
        for b in range(DIL_SUB):
            no_next = lax.rem(n * DIL_SUB + b + 1, nbs) == 0
            rows = slice(b * W, (b + 1) * W)
            both = slice(b * W, (b + 2) * W)
            dd = dof[both, :]
            dk = dv = None
            for h, (kh, vh, qh, ddh) in enumerate(zip(_each_head(k_ref[rows, :], lo), _each_head(v_ref[rows, :], lo),
                                                      _each_head(qf[both, :], lo2), _each_head(dd, lo2))):
                st = lax.dot_general(kh, qh, _DN["nt"], preferred_element_type=F32) * SCALE + biases[h][0]
                st = jnp.where(jnp.logical_and(no_next, col >= W), NEG_INF, st)
                pt = jnp.exp(st - lf[h, :, both])
                d = lax.dot_general(pt.astype(BF16), ddh, _DN["nn"], preferred_element_type=F32)
                dv = d if dv is None else dv + d
                dpt = lax.dot_general(vh, dd, _DN["nt"], preferred_element_type=F32)
                dst = (pt * (dpt - df[h, :, both]) * SCALE).astype(BF16)
                d = lax.dot_general(dst, qh, _DN["nn"], preferred_element_type=F32)
                dk = d if dk is None else dk + d
            dk_ref[rows, :] = dk.astype(BF16)
            dv_ref[rows, :] = dv.astype(BF16)

    sp = _dil_pair_specs(qoff, koff, voff)
    wide = jax.ShapeDtypeStruct((S, GROUP_W), BF16)
    return pl.pallas_call(
        body, out_shape=[wide, wide], grid=(NPG, DIL_STEPS),
        in_specs=[sp["smem"], sp["smem"], sp["k"], sp["v"], sp["q"], sp["q_next"], sp["o"], sp["o_next"], sp["row"],
                  sp["row_next"], sp["row"], sp["row_next"]],
        out_specs=[sp["o"], sp["o"]],
        scratch_shapes=[pltpu.VMEM((DIL_ROWS + W, LANES), BF16)] * 2 + [pltpu.VMEM((2, 1, DIL_ROWS + W), F32)] * 2,
        compiler_params=_cparams(("parallel", "parallel")), name=name,
    )(coef_t, nbs_t, k, v, q, q, do, do, lrow, lrow, drow, drow)


FFN_ROWS = 512
FFN_COLS = 256
HALO = 8


def _shifted(u, halo, back):
    T = u.shape[0]
    rows = lax.broadcasted_iota(jnp.int32, u.shape, 0)
    if back:
        s1 = jnp.where(rows == 0, halo[HALO - 1:HALO, :], pltpu.roll(u, 1, 0))
        s2 = jnp.where(rows == 0, halo[HALO - 2:HALO - 1, :],
                       jnp.where(rows == 1, halo[HALO - 1:HALO, :], pltpu.roll(u, 2, 0)))
    else:
        s1 = jnp.where(rows == T - 1, halo[0:1, :], pltpu.roll(u, T - 1, 0))
        s2 = jnp.where(rows == T - 1, halo[1:2, :],
                       jnp.where(rows == T - 2, halo[0:1, :], pltpu.roll(u, T - 2, 0)))
    return s1, s2


def _conv_parts(u_ref, h_ref, w_ref, b_ref, first):
    out = []
    for p in range(2):
        u = u_ref[p]
        halo = jnp.where(first, 0.0, h_ref[p])
        u1, u2 = _shifted(u, halo, True)
        w = w_ref[p]
        out.append((w[0:1, :] * u2 + w[1:2, :] * u1 + w[2:3, :] * u + b_ref[p], u1, u2, u))
    return out


def _ffn_specs():
    T, C = FFN_ROWS, FFN_COLS
    blk = pl.BlockSpec((2, T, C), lambda j, i: (0, i, j))
    prev = pl.BlockSpec((2, HALO, C), lambda j, i: (0, jnp.maximum(i * (T // HALO) - 1, 0), j))
    nxt = pl.BlockSpec((2, HALO, C), lambda j, i: (0, jnp.minimum((i + 1) * (T // HALO), S // HALO - 1), j))
    wsp = pl.BlockSpec((2, 3, C), lambda j, i: (0, 0, j))
    bsp = pl.BlockSpec((2, 1, C), lambda j, i: (0, 0, j))
    one = pl.BlockSpec((T, C), lambda j, i: (i, j))
    return blk, prev, nxt, wsp, bsp, one


def _ffn_act_fwd(u, w, b, name):
    blk, prev, _, wsp, bsp, one = _ffn_specs()

    def body(u_ref, h_ref, w_ref, b_ref, o_ref):
        (a, _, _, _), (g, _, _, _) = _conv_parts(u_ref, h_ref, w_ref, b_ref, pl.program_id(1) == 0)
        o_ref[...] = (g / (1.0 + jnp.exp(-g)) * a).astype(BF16)

    return pl.pallas_call(
        body, out_shape=jax.ShapeDtypeStruct((S, FF), BF16), grid=(FF // FFN_COLS, S // FFN_ROWS),
        in_specs=[blk, prev, wsp, bsp], out_specs=one,
        compiler_params=_cparams(("parallel", "parallel")), name=name,
    )(u, u, w, b)


def _ffn_act_bwd(u, dact, w, b, name):
    blk, prev, _, wsp, bsp, one = _ffn_specs()

    def body(u_ref, h_ref, da_ref, w_ref, b_ref, duc_ref, dwb_ref):
        i = pl.program_id(1)
        (a, a1, a2, a0), (g, g1, g2, g0) = _conv_parts(u_ref, h_ref, w_ref, b_ref, i == 0)
        dact_v = da_ref[...]
        sg = 1.0 / (1.0 + jnp.exp(-g))
        d_a = dact_v * (g * sg)
        d_g = dact_v * a * (sg * (1.0 + g * (1.0 - sg)))
        duc_ref[0] = d_a
        duc_ref[1] = d_g

        @pl.when(i == 0)
        def _():
            dwb_ref[...] = jnp.zeros(dwb_ref.shape, F32)

        for p, (d, s2, s1, s0) in enumerate(((d_a, a2, a1, a0), (d_g, g2, g1, g0))):
            dwb_ref[p, 0:1, :] += jnp.sum(d * s2, axis=0, keepdims=True)
            dwb_ref[p, 1:2, :] += jnp.sum(d * s1, axis=0, keepdims=True)
            dwb_ref[p, 2:3, :] += jnp.sum(d * s0, axis=0, keepdims=True)
            dwb_ref[p, 3:4, :] += jnp.sum(d, axis=0, keepdims=True)

    return pl.pallas_call(
        body, out_shape=[jax.ShapeDtypeStruct((2, S, FF), F32), jax.ShapeDtypeStruct((2, 8, FF), F32)],
        grid=(FF // FFN_COLS, S // FFN_ROWS), in_specs=[blk, prev, one, wsp, bsp],
        out_specs=[blk, pl.BlockSpec((2, 8, FFN_COLS), lambda j, i: (0, 0, j))],
        compiler_params=_cparams(("parallel", "arbitrary")), name=name,
    )(u, u, dact, w, b)


def _ffn_conv_bwd(duc, w, name):
    blk, _, nxt, wsp, _, _ = _ffn_specs()
    last = S // FFN_ROWS - 1

    def body(d_ref, h_ref, w_ref, du_ref):
        is_last = pl.program_id(1) == last
        for p in range(2):
            d = d_ref[p]
            halo = jnp.where(is_last, 0.0, h_ref[p])
            d1, d2 = _shifted(d, halo, False)
            wv = w_ref[p]
            du_ref[p] = (wv[2:3, :] * d + wv[1:2, :] * d1 + wv[0:1, :] * d2).astype(BF16)

    return pl.pallas_call(
        body, out_shape=jax.ShapeDtypeStruct((2, S, FF), BF16), grid=(FF // FFN_COLS, S // FFN_ROWS),
        in_specs=[blk, nxt, wsp], out_specs=blk,
        compiler_params=_cparams(("parallel", "parallel")), name=name,
    )(duc, duc, w)


def _adam_update(w, gv, m, v):
    c1 = 1.0 / (1.0 - ADAM_B1 ** ADAM_STEP)
    c2 = 1.0 / (1.0 - ADAM_B2 ** ADAM_STEP)
    mn = ADAM_B1 * m + (1.0 - ADAM_B1) * gv
    vn = ADAM_B2 * v + (1.0 - ADAM_B2) * (gv * gv)
    return -ADAM_LR * ((mn * c1) / (jnp.sqrt(vn * c2) + ADAM_EPS) + ADAM_WD * w), mn, vn


def _adamw(w, g, m, v, name):
    rows = w.shape[0]
    T = 8
    for cand in (256, 128, 64, 32, 16, 8):
        if rows % cand == 0:
            T = cand
            break

    def body(w_ref, g_ref, m_ref, v_ref, d_ref, mo_ref, vo_ref):
        d_ref[...], mo_ref[...], vo_ref[...] = _adam_update(w_ref[...], g_ref[...], m_ref[...], v_ref[...])

    blk = pl.BlockSpec((T, w.shape[1]), lambda i: (i, 0))
    sds = jax.ShapeDtypeStruct(w.shape, F32)
    return pl.pallas_call(
        body, out_shape=[sds, sds, sds], grid=(rows // T,), in_specs=[blk] * 4, out_specs=[blk] * 3,
        compiler_params=_cparams(("parallel",)), name=name,
    )(w, g, m, v)


ANY = pl.BlockSpec(memory_space=pl.ANY)


def _place():
    x, y, c = lax.axis_index("x"), lax.axis_index("y"), lax.axis_index("c")
    chips = [(1 - x, y), (x, 1 - y), (1 - x, 1 - y)]
    return x, y, c, chips


def _place_own(w, slot_arr, name):
    rows = w.shape[0]
    T = 16
    for cand in (2048, 1024, 512, 256, 128, 64, 32, 16):
        if rows % cand == 0:
            T = cand
            break

    def body(k_ref, w_ref, o_ref):
        o_ref[...] = w_ref[...]

    return pl.pallas_call(
        body, out_shape=jax.ShapeDtypeStruct((N_CHIPS, rows, FLAT_W), w.dtype),
        grid_spec=pltpu.PrefetchScalarGridSpec(
            num_scalar_prefetch=1, grid=(rows // T,),
            in_specs=[pl.BlockSpec((T, FLAT_W), lambda i, k: (i, 0))],
            out_specs=pl.BlockSpec((None, T, FLAT_W), lambda i, k: (k[0], i, 0))),
        compiler_params=_cparams(("parallel",)), name=name,
    )(slot_arr, w)


def _allgather_shards(w, buf):
    half_rows = w.shape[0] // 2
    assert half_rows % 16 == 0

    def body(w_ref, buf_ref, g_ref, send_sems, recv_sems):
        x, y, c, chips = _place()
        myk = 2 * x + y
        sibling = (x, y, 1 - c)
        h0 = pl.multiple_of(c * half_rows, 16)
        h1 = pl.multiple_of((1 - c) * half_rows, 16)

        def half(k, start):
            return g_ref.at[k, pl.ds(start, half_rows), :]

        def rcopy(sem, src, dst, to):
            return pltpu.make_async_remote_copy(src_ref=src, dst_ref=dst, send_sem=send_sems.at[sem],
                                                recv_sem=recv_sems.at[sem], device_id=to, device_id_type=MESH)

        ici = [rcopy(r, w_ref.at[pl.ds(h0, half_rows), :], half(myk, h0), (*chip, c)) for r, chip in enumerate(chips)]
        for cp in ici:
            cp.start()
        ks = [2 * cx + cy for cx, cy in chips]
        fwd = [rcopy(3 + r, half(ks[r], h0), half(ks[r], h0), sibling) for r in range(3)]
        for r in range(3):
            rcopy(r, half(ks[r], h0), half(ks[r], h0), (*chips[r], c)).wait_recv()
            fwd[r].start()
        for r in range(3):
            rcopy(3 + r, half(ks[r], h1), half(ks[r], h1), sibling).wait_recv()
        for cp in ici + fwd:
            cp.wait_send()

    return pl.pallas_call(
        body, out_shape=jax.ShapeDtypeStruct(buf.shape, w.dtype), in_specs=[ANY, ANY], out_specs=ANY,
        scratch_shapes=[pltpu.SemaphoreType.DMA((6,)), pltpu.SemaphoreType.DMA((6,))],
        input_output_aliases={1: 0},
        compiler_params=pltpu.CompilerParams(has_side_effects=True), name="allgather_shards",
    )(w, buf)


HBM_SPEC = pl.BlockSpec(memory_space=pltpu.HBM)
SEM_SPEC = pl.BlockSpec(memory_space=pltpu.SEMAPHORE)
DATAFLOW = pltpu.SideEffectType.DATAFLOW_SIDE_EFFECTING
OWN_SLOT = 3


def _late_gather_start(w, land):
    def body(w_ref, land_ref, send_sems, recv_sems, w_thru, land_thru, token):
        x, y, c, chips = _place()
        for r, chip in enumerate(chips):
            pltpu.make_async_remote_copy(src_ref=w_ref, dst_ref=land_ref.at[r], send_sem=send_sems.at[r],
                                         recv_sem=recv_sems.at[r], device_id=(*chip, c), device_id_type=MESH).start()
        token[...] = jnp.zeros_like(token)

    return pl.pallas_call(
        body, name="late_gather_start",
        out_shape=(pltpu.SemaphoreType.DMA((3,)), pltpu.SemaphoreType.DMA((3,)), pltpu.HBM(w.shape, w.dtype),
                   pltpu.HBM(land.shape, land.dtype), jax.ShapeDtypeStruct((8, LANES), F32)),
        in_specs=(HBM_SPEC, HBM_SPEC),
        out_specs=(SEM_SPEC, SEM_SPEC, HBM_SPEC, HBM_SPEC, pl.BlockSpec(memory_space=pltpu.VMEM)),
        input_output_aliases={0: 2, 1: 3}, compiler_params=pltpu.CompilerParams(has_side_effects=DATAFLOW),
    )(pltpu.with_memory_space_constraint(w, pltpu.HBM), pltpu.with_memory_space_constraint(land, pltpu.HBM))


def _late_gather_wait(send_sems, recv_sems, w_thru, land_thru, after):
    def body(w_ref, land_ref, send_sems, recv_sems, after_ref, w_dead, got_ref):
        x, y, c, chips = _place()
        for r, chip in enumerate(chips):
            cp = pltpu.make_async_remote_copy(src_ref=w_ref, dst_ref=land_ref.at[r], send_sem=send_sems.at[r],
                                              recv_sem=recv_sems.at[r], device_id=(*chip, c), device_id_type=MESH)
            cp.wait_send()
            cp.wait_recv()

    return pl.pallas_call(
        body, name="late_gather_wait",
        out_shape=(pltpu.HBM(w_thru.shape, w_thru.dtype), pltpu.HBM(land_thru.shape, land_thru.dtype)),
        in_specs=(HBM_SPEC, HBM_SPEC, SEM_SPEC, SEM_SPEC, pl.BlockSpec(memory_space=pl.ANY)),
        out_specs=(HBM_SPEC, HBM_SPEC), input_output_aliases={0: 0, 1: 1},
        compiler_params=pltpu.CompilerParams(has_side_effects=DATAFLOW),
    )(w_thru, land_thru, send_sems, recv_sems, after)


def _flat_tile(rows):
    return next(t for t in (2048, 1024, 512, 256, 128, 64, 32, 16) if rows % t == 0)


def _sibling_swap_half(g, tag):
    half = g.shape[1] // 2

    def body(g_ref, o_ref, send_sem, recv_sem):
        x, y, c, _ = _place()
        theirs = pl.multiple_of((1 - c) * half, 8)
        cp = pltpu.make_async_remote_copy(src_ref=g_ref.at[:, pl.ds(theirs, half), :], dst_ref=o_ref,
                                          send_sem=send_sem, recv_sem=recv_sem, device_id=(x, y, 1 - c),
                                          device_id_type=MESH)
        cp.start()
        cp.wait()

    return pl.pallas_call(
        body, out_shape=jax.ShapeDtypeStruct((N_CHIPS, half, FLAT_W), F32), in_specs=[ANY], out_specs=ANY,
        scratch_shapes=[pltpu.SemaphoreType.DMA, pltpu.SemaphoreType.DMA],
        compiler_params=pltpu.CompilerParams(has_side_effects=True), name=f"rs_sibling_swap_{tag}",
    )(g)


def _pair_sum(g, other, c_arr, tag):
    half = other.shape[1]
    T = _flat_tile(half)

    def body(c_ref, g_ref, o_ref, s_ref):
        s_ref[...] = (g_ref[...] + o_ref[...]).astype(BF16)

    nb = half // T
    return pl.pallas_call(
        body, out_shape=jax.ShapeDtypeStruct((N_CHIPS, half, FLAT_W), BF16),
        grid_spec=pltpu.PrefetchScalarGridSpec(
            num_scalar_prefetch=1, grid=(N_CHIPS, nb),
            in_specs=[pl.BlockSpec((None, T, FLAT_W), lambda k, i, c: (k, c[0] * nb + i, 0)),
                      pl.BlockSpec((None, T, FLAT_W), lambda k, i, c: (k, i, 0))],
            out_specs=pl.BlockSpec((None, T, FLAT_W), lambda k, i, c: (k, i, 0))),
        compiler_params=_cparams(("parallel", "parallel")), name=f"rs_pair_sum_{tag}",
    )(c_arr, g, other)


def _chip_exchange(s, tag):
    def body(s_ref, o_ref, send_sems, recv_sems):
        x, y, c, chips = _place()
        cps = []
        for r, (cx, cy) in enumerate(chips):
            cps.append(pltpu.make_async_remote_copy(
                src_ref=s_ref.at[2 * cx + cy], dst_ref=o_ref.at[r], send_sem=send_sems.at[r],
                recv_sem=recv_sems.at[r], device_id=(cx, cy, c), device_id_type=MESH))
        for cp in cps:
            cp.start()
        for cp in cps:
            cp.wait()

    return pl.pallas_call(
        body, out_shape=jax.ShapeDtypeStruct((3,) + s.shape[1:], BF16), in_specs=[ANY], out_specs=ANY,
        scratch_shapes=[pltpu.SemaphoreType.DMA((3,)), pltpu.SemaphoreType.DMA((3,))],
        compiler_params=pltpu.CompilerParams(has_side_effects=True), name=f"rs_chip_exchange_{tag}",
    )(s)


def _chip_exchange_start(s, land):
    def body(s_ref, land_ref, send_sems, recv_sems, s_thru, land_thru, token):
        x, y, c, chips = _place()
        for r, (cx, cy) in enumerate(chips):
            pltpu.make_async_remote_copy(src_ref=s_ref.at[2 * cx + cy], dst_ref=land_ref.at[r], send_sem=send_sems.at[r],
                                         recv_sem=recv_sems.at[r], device_id=(cx, cy, c), device_id_type=MESH).start()
        token[...] = jnp.zeros_like(token)

    return pl.pallas_call(
        body, name="rs_exchange_start",
        out_shape=(pltpu.SemaphoreType.DMA((3,)), pltpu.SemaphoreType.DMA((3,)), pltpu.HBM(s.shape, s.dtype),
                   pltpu.HBM(land.shape, land.dtype), jax.ShapeDtypeStruct((8, LANES), F32)),
        in_specs=(HBM_SPEC, HBM_SPEC),
        out_specs=(SEM_SPEC, SEM_SPEC, HBM_SPEC, HBM_SPEC, pl.BlockSpec(memory_space=pltpu.VMEM)),
        input_output_aliases={0: 2, 1: 3}, compiler_params=pltpu.CompilerParams(has_side_effects=DATAFLOW),
    )(pltpu.with_memory_space_constraint(s, pltpu.HBM), pltpu.with_memory_space_constraint(land, pltpu.HBM))


def _chip_exchange_wait(send_sems, recv_sems, s_thru, land_thru, after):
    def body(s_ref, land_ref, send_sems, recv_sems, after_ref, s_done, got_ref):
        x, y, c, chips = _place()
        for r, (cx, cy) in enumerate(chips):
            cp = pltpu.make_async_remote_copy(src_ref=s_ref.at[2 * cx + cy], dst_ref=land_ref.at[r],
                                              send_sem=send_sems.at[r], recv_sem=recv_sems.at[r], device_id=(cx, cy, c),
                                              device_id_type=MESH)
            cp.wait_send()
            cp.wait_recv()

    return pl.pallas_call(
        body, name="rs_exchange_wait",
        out_shape=(pltpu.HBM(s_thru.shape, s_thru.dtype), pltpu.HBM(land_thru.shape, land_thru.dtype)),
        in_specs=(HBM_SPEC, HBM_SPEC, SEM_SPEC, SEM_SPEC, pl.BlockSpec(memory_space=pl.ANY)),
        out_specs=(HBM_SPEC, HBM_SPEC), input_output_aliases={0: 0, 1: 1},
        compiler_params=pltpu.CompilerParams(has_side_effects=DATAFLOW),
    )(s_thru, land_thru, send_sems, recv_sems, after)


def _chip_sum(s, r, k_arr, tag):
    half = s.shape[1]
    T = _flat_tile(half)

    def body(k_ref, s_ref, r_ref, o_ref):
        o_ref[...] = ((s_ref[...].astype(F32) + r_ref[0].astype(F32)) + r_ref[1].astype(F32)) + r_ref[2].astype(F32)

    return pl.pallas_call(
        body, out_shape=jax.ShapeDtypeStruct((half, FLAT_W), F32),
        grid_spec=pltpu.PrefetchScalarGridSpec(
            num_scalar_prefetch=1, grid=(half // T,),
            in_specs=[pl.BlockSpec((None, T, FLAT_W), lambda i, k: (k[0], i, 0)),
                      pl.BlockSpec((3, T, FLAT_W), lambda i, k: (0, i, 0))],
            out_specs=pl.BlockSpec((T, FLAT_W), lambda i, k: (i, 0))),
        compiler_params=_cparams(("parallel",)), name=f"rs_chip_sum_{tag}",
    )(k_arr, s, r)


def _sibling_send(t, tag):
    def body(t_ref, o_ref, send_sem, recv_sem):
        x, y, c, _ = _place()
        cp = pltpu.make_async_remote_copy(src_ref=t_ref, dst_ref=o_ref, send_sem=send_sem, recv_sem=recv_sem,
                                          device_id=(x, y, 1 - c), device_id_type=MESH)
        cp.start()
        cp.wait()

    return pl.pallas_call(
        body, out_shape=jax.ShapeDtypeStruct(t.shape, F32), in_specs=[ANY], out_specs=ANY,
        scratch_shapes=[pltpu.SemaphoreType.DMA, pltpu.SemaphoreType.DMA],
        compiler_params=pltpu.CompilerParams(has_side_effects=True), name=f"rs_sibling_send_{tag}",
    )(t)


def _allreduce_small(v):
    def body(v_ref, o_ref, buf, send_sems, recv_sems):
        x, y, c, _ = _place()
        me = 4 * x + 2 * y + c
        buf[me] = v_ref[...]
        cps = []
        for mask in range(1, 8):
            a, b, d = (mask >> 2) & 1, (mask >> 1) & 1, mask & 1
            peer = (x + a - 2 * a * x, y + b - 2 * b * y, c + d - 2 * d * c)
            cps.append(pltpu.make_async_remote_copy(
                src_ref=v_ref, dst_ref=buf.at[me], send_sem=send_sems.at[mask - 1], recv_sem=recv_sems.at[mask - 1],
                device_id=peer, device_id_type=MESH))
        for cp in cps:
            cp.start()
        for cp in cps:
            cp.wait()
        total = buf[0]
        for dev in range(1, 8):
            total = total + buf[dev]
        o_ref[...] = total

    vm = pl.BlockSpec(memory_space=pltpu.VMEM)
    return pl.pallas_call(
        body, out_shape=jax.ShapeDtypeStruct((SMALL_ROWS, 1024), F32), in_specs=[vm], out_specs=vm,
        scratch_shapes=[pltpu.VMEM((8, SMALL_ROWS, 1024), F32), pltpu.SemaphoreType.DMA((7,)),
                        pltpu.SemaphoreType.DMA((7,))],
        compiler_params=pltpu.CompilerParams(has_side_effects=True), name="allreduce_small",
    )(v)


def _col_to_row(t):
    return t.reshape(t.shape[0], 1, S)


def _residue_rows(t, d, inverse=False):
    if d == 1:
        return t
    shape = (d, S // d) if inverse else (S // d, d)
    return t.reshape(shape + t.shape[1:]).transpose(1, 0, 2).reshape(t.shape)


def _residue_vecs(t, d, inverse=False):
    if d == 1:
        return t
    shape = (d, S // d) if inverse else (S // d, d)
    return t.reshape((B_HPG,) + shape).transpose(0, 2, 1).reshape(B_HPG, S, 1)


def _ffn_fwd(x, g, w_up, cw, cb, w_down, tag):
    (h,) = _rms_fwd(x, [g], f"{tag}_norm")
    u = _mm(h, w_up, mode="nn", tm=1024, tn=1408, tk=1024, o_split=2, name=f"{tag}_up")
    act = _ffn_act_fwd(u, cw, cb, f"{tag}_act")
    x_out = _mm(act, w_down, mode="nn", tm=1024, tn=512, tk=FF, res=x, name=f"{tag}_down")
    return x_out, (h, u, act)


def _ffn_bwd(x, g, w_up, cw, cb, w_down, saved, dx, dxb, tag):
    h, u, act = saved
    d_w_down = _mm(act, dxb, mode="tn", tm=1408, tn=512, tk=1024, name=f"{tag}_dwdown")
    dact = _mm(dxb, w_down, mode="nt", tm=1024, tn=1408, tk=1024, name=f"{tag}_dact")
    duc, dwb = _ffn_act_bwd(u, dact, cw, cb, f"{tag}_dgate")
    du = _ffn_conv_bwd(duc, cw, f"{tag}_dconv")
    d_w_up = _mm(h, du, mode="tn", tm=1024, tn=1408, tk=1024, b_split=2, name=f"{tag}_dwup")
    dh = _mm(du, w_up, mode="nt", tm=1024, tn=512, tk=1408, a_split=2, name=f"{tag}_dh")
    dx_new, dxb_new, (dg,) = _rms_bwd(x, dx, [(g, dh)], f"{tag}_dnorm")
    d_cw = dwb[:, 0:3, :].transpose(1, 0, 2).reshape(3, 2 * FF)
    d_cb = dwb[:, 3, :].reshape(2 * FF)
    return dx_new, dxb_new, dict(w_up=d_w_up, w_down=d_w_down, conv_w=d_cw, conv_b=d_cb, norm_g=dg.reshape(D))


def _local_step(x, target, p, late_weights, late_grads_ready):
    g = {}
    (h1,) = _rms_fwd(x, [p["mix_norm_g"][0]], "a_norm")
    w_qkv = p["a_w_in"][:, :QKV_W]
    w_f = jnp.pad(p["a_w_in"][:, QKV_W:], ((0, 0), (0, LANES - A_HEADS)))
    b_f = jnp.pad(p["a_b_f"].reshape(1, A_HEADS), ((0, 0), (0, LANES - A_HEADS)))
    qkv = _mm(h1, w_qkv, mode="nn", tm=1024, tn=512, tk=1024, out_dtype=BF16, name="a_qkv")
    pf = _mm(h1, w_f, mode="nn", tm=1024, tn=LANES, tk=1024, name="a_gate")
    aug_q, aug_k = _fgate_fwd(pf, b_f, "a_gate_scan")
    oa2, lse_a = _fox_pair_fwd(qkv, aug_q, aug_k, "a_attn")
    x1 = _mm(oa2, p["a_w_out"], mode="nn", tm=1024, tn=512, tk=1024, res=x, name="a_out")
    p = {**p, **late_weights(x1)}
    x2, ffn0 = _ffn_fwd(x1, p["ffn_norm_g"][0], p["ffn_w_up"][0], p["conv_w"][0], p["conv_b"][0], p["ffn_w_down"][0], "f0")
    hk, h3 = _rms_fwd(x2, [p["kv_norm_g"], p["mix_norm_g"][1]], "kv_b_norm")
    kvb = _mm(hk, p["w_kv"], mode="nn", tm=1024, tn=512, tk=1024, out_dtype=BF16, name="kv_proj")
    qb = _mm(h3, p["b_w_q"], mode="nn", tm=1024, tn=512, tk=1024, out_dtype=BF16, name="b_q")
    dil_in = []
    for gi, (_, d) in enumerate(B_GROUPS):
        if d == 1:
            dil_in.append((qb, kvb, kvb, gi * NPG, gi * NPG, (3 + gi) * NPG))
        else:
            qg = _residue_rows(qb[:, gi * GROUP_W:(gi + 1) * GROUP_W], d)
            kvg = _residue_rows(kvb.reshape(S, 2, 3, GROUP_W)[:, :, gi, :].reshape(S, 2 * GROUP_W), d)
            dil_in.append((qg, kvg, kvg, 0, 0, NPG))
    o_g, lse_g = [], []
    for gi, (_, d) in enumerate(B_GROUPS):
        qg, kg, vg, qoff, koff, voff = dil_in[gi]
        og, lg = _dil_pair_fwd(gi, qg, kg, vg, qoff, koff, voff, f"b_attn{gi}")
        o_g.append(_residue_rows(og, d, inverse=True))
        lse_g.append(_residue_vecs(lg, d, inverse=True))
    ob, ob2, lse_b = _dil_pair_merge(o_g, lse_g, "b_merge")
    x3 = _mm(ob2, p["b_w_out"], mode="nn", tm=1024, tn=512, tk=B_OUT_W, res=x2, name="b_out")
    x4, ffn1 = _ffn_fwd(x3, p["ffn_norm_g"][1], p["ffn_w_up"][1], p["conv_w"][1], p["conv_b"][1], p["ffn_w_down"][1], "f1")
    loss, dx, dxb, dg_final = _loss_head(x4, p["final_norm_g"], target, "loss_head")
    g["final_norm_g"] = dg_final.reshape(D)

    dx, dxb, gf1 = _ffn_bwd(x3, p["ffn_norm_g"][1], p["ffn_w_up"][1], p["conv_w"][1], p["conv_b"][1], p["ffn_w_down"][1],
                            ffn1, dx, dxb, "f1")
    g["b_w_out"] = _mm(ob2, dxb, mode="tn", tm=B_OUT_W, tn=512, tk=1024, name="b_dwout")
    dob = _mm(dxb, p["b_w_out"], mode="nt", tm=1024, tn=B_OUT_W, tk=1024, name="b_do")
    delta_b = _pair_rowdot(dob, ob, "b_delta")
    dob16 = dob.astype(BF16)
    dq_g, dk_g, dv_g = [], [], []
    for gi, (_, d) in enumerate(B_GROUPS):
        qg, kg, vg, qoff, koff, voff = dil_in[gi]
        dog, l_d, dl_d = _residue_rows(dob16, d), _residue_vecs(lse_b, d), _residue_vecs(delta_b, d)
        dqd = _dil_pair_dq(gi, qg, kg, vg, qoff, koff, voff, dog, l_d, dl_d, f"b_dq{gi}")
        dkd, dvd = _dil_pair_dkv(gi, qg, kg, vg, qoff, koff, voff, dog, _col_to_row(l_d), _col_to_row(dl_d), f"b_dkv{gi}")
        dq_g.append(_residue_rows(dqd, d, inverse=True))
        dk_g.append(_residue_rows(dkd, d, inverse=True))
        dv_g.append(_residue_rows(dvd, d, inverse=True))
    dqb = jnp.concatenate(dq_g, axis=1)
    dkvb = jnp.concatenate(dk_g + dv_g, axis=1)
    g["b_w_q"] = _mm(h3, dqb, mode="tn", tm=1024, tn=512, tk=1024, name="b_dwq")
    dh3 = _mm(dqb, p["b_w_q"], mode="nt", tm=1024, tn=512, tk=B_Q_W, name="b_dh")
    g["w_kv"] = _mm(hk, dkvb, mode="tn", tm=1024, tn=512, tk=1024, name="kv_dw")
    dhk = _mm(dkvb, p["w_kv"], mode="nt", tm=1024, tn=512, tk=1536, name="kv_dh")
    dx, dxb, (dg_mix1, dg_kv) = _rms_bwd(x2, dx, [(p["mix_norm_g"][1], dh3), (p["kv_norm_g"], dhk)], "b_dnorm")
    g["kv_norm_g"] = dg_kv.reshape(D)
    dx, dxb, gf0 = _ffn_bwd(x1, p["ffn_norm_g"][0], p["ffn_w_up"][0], p["conv_w"][0], p["conv_b"][0], p["ffn_w_down"][0],
                            ffn0, dx, dxb, "f0")
    g["ffn_w_up"] = [gf0["w_up"], gf1["w_up"]]
    g["ffn_w_down"] = [gf0["w_down"], gf1["w_down"]]
    g["ffn_conv_w"] = jnp.stack([gf0["conv_w"], gf1["conv_w"]])
    token = late_grads_ready(g)
    a_w_out_t = p["a_w_out"] + token[0, 0].astype(BF16)
    g["a_w_out"] = _mm(oa2, dxb, mode="tn", tm=1024, tn=512, tk=1024, name="a_dwout")
    doa = _mm(dxb, a_w_out_t, mode="nt", tm=1024, tn=512, tk=1024, name="a_do")
    delta_a = _pair_rowdot(doa, oa2, "a_delta")
    dqa, dka, dva, dck, dcq = _fox_pair_bwd(qkv, doa, _col_to_row(lse_a), _col_to_row(delta_a), aug_q, aug_k, "a_dattn")
    dqkv = jnp.concatenate([dqa, dka, dva], axis=1)
    pad_heads = lambda t: jnp.pad(t.reshape(A_HEADS, S).T, ((0, 0), (0, LANES - A_HEADS)))
    dpf, db_f = _fgate_bwd(pf, b_f, pad_heads(dck), pad_heads(dcq), "a_dgate_scan")
    g["a_b_f"] = db_f[:, :A_HEADS]
    d_w_qkv = _mm(h1, dqkv, mode="tn", tm=1024, tn=512, tk=1024, name="a_dwqkv")
    d_w_f = _mm(h1, dpf, mode="tn", tm=1024, tn=LANES, tk=1024, name="a_dwgate")
    g["a_w_in"] = jnp.concatenate([d_w_qkv, d_w_f[:, :A_HEADS]], axis=1)
    dh1 = _mm(dqkv, w_qkv, mode="nt", tm=1024, tn=512, tk=1536, name="a_dh")
    dh1 = _mm(dpf, w_f, mode="nt", tm=1024, tn=512, tk=LANES, res=dh1, name="a_dh_gate")
    dx, _, (dg_mix0,) = _rms_bwd(x, dx, [(p["mix_norm_g"][0], dh1)], "a_dnorm")

    g["mix_norm_g"] = jnp.stack([dg_mix0.reshape(D), dg_mix1.reshape(D)])
    g["ffn_norm_g"] = jnp.stack([gf0["norm_g"], gf1["norm_g"]])
    g["ffn_conv_b"] = jnp.stack([gf0["conv_b"], gf1["conv_b"]])
    return loss[0, 0], dx, g


_SHARD_SHAPES = {"a_w_in": (1, 1024, 772), "a_w_out": (1, 256, 1024), "b_w_q": (1, 1024, 384), "b_w_out": (1, 512, 256),
                 "w_kv": (1024, 768), "ffn_w_up": (2, 1024, 1408), "ffn_w_down": (2, 704, 1024), "ffn_conv_w": (2, 3, 1408)}
_SMALL = (("kv_norm_g", (1024,)), ("mix_norm_g", (2, 1024)), ("ffn_norm_g", (2, 1024)), ("final_norm_g", (1024,)),
          ("a_b_f", (1, 16)), ("ffn_conv_b", (2, 5632)))


def _unslabs(rows, L, R, C, rpad):
    nc = -(-C // FLAT_W)
    return rows.reshape(L, nc, rpad, FLAT_W).transpose(0, 2, 1, 3).reshape(L, rpad, nc * FLAT_W)[:, :R, :C]


_SEG_RT = {"ffn_w_down": 704, "a_w_in": 1024, "a_w_out": 256, "b_w_q": 1024, "b_w_out": 512, "w_kv": 1024,
           "ffn_w_up": 1024, "ffn_conv_w": 16}
_ROW_SHARDED = ("a_w_out", "ffn_w_down")


_LAYOUTS = {"early": ("a_w_in", "a_w_out"), "late": ("ffn_w_down", "b_w_q", "b_w_out", "w_kv", "ffn_w_up", "ffn_conv_w"),
            "grad_early": ("a_w_in", "a_w_out"),
            "grad_late": ("ffn_w_up", "ffn_w_down", "b_w_q", "w_kv", "b_w_out", "ffn_conv_w")}
_GRAD_ROWS = {"grad_early": 10240, "grad_late": 45056}


def _layout_rows(layout):
    used = sum(_seg_rows(*s) for s in _SEGS if s[0] in _LAYOUTS[layout])
    rows = _GRAD_ROWS.get(layout, used)
    assert rows >= used
    return rows


def _grad_layout(name):
    return "grad_early" if name in _LAYOUTS["grad_early"] else "grad_late"


def _seg(name, layout=None):
    layout = layout or _grad_layout(name)
    off = 0
    for s in sorted((s for s in _SEGS if s[0] in _LAYOUTS[layout]), key=lambda s: _LAYOUTS[layout].index(s[0])):
        _, L, R, C, rpad = s
        if layout in _GRAD_ROWS:
            per_layer = -(-C // FLAT_W) * rpad
            off = -(-off // per_layer) * per_layer
        if s[0] == name:
            rt = _SEG_RT[name]
            assert off % rt == 0 and rpad % rt == 0
            half = _layout_rows(layout) // 2
            assert off + _seg_rows(*s) <= 2 * half
            assert layout not in _GRAD_ROWS or half % rt == 0 or off + _seg_rows(*s) <= half
            return dict(L=L, R=R, C=C, rpad=rpad, nc=-(-C // FLAT_W), rt=rt, off=off, ni=rpad // rt, half=half)
        off += _seg_rows(*s)
    raise KeyError(name)


def _flat_block(sg, term=0):
    base = (sg["off"] + term * sg["L"] * sg["nc"] * sg["rpad"]) // sg["rt"]
    return lambda l, j, i: base + (l * sg["nc"] + j) * sg["ni"] + i


def _native3(t, name):
    sg = _seg(name)
    t = t.reshape(sg["L"], sg["R"], sg["C"])
    return jnp.pad(t, ((0, 0), (0, sg["rpad"] - sg["R"]), (0, 0))) if sg["rpad"] != sg["R"] else t


def _slab_pack(flat, t, name, layout, term=None):
    sg = _seg(name, layout)
    rt = sg["rt"]
    rb = _flat_block(sg, term or 0)

    def body(*refs):
        t_ref, o_ref = refs[-2], refs[-1]
        val = t_ref[...]
        o_ref[...] = val.astype(BF16) if term is None else _split3(val)[term]

    in_specs = [pl.BlockSpec((None, rt, FLAT_W), lambda l, j, i: (l, i, j))]
    args = [t]
    if flat is not None:
        in_specs, args = [ANY] + in_specs, [flat] + args
    return pl.pallas_call(
        body, out_shape=jax.ShapeDtypeStruct((_layout_rows(layout), FLAT_W), BF16), grid=(sg["L"], sg["nc"], sg["ni"]),
        in_specs=in_specs, out_specs=pl.BlockSpec((rt, FLAT_W), lambda l, j, i: (rb(l, j, i), 0)),
        input_output_aliases={0: 0} if flat is not None else {},
        compiler_params=_cparams(("parallel", "parallel", "parallel")), name=f"pack_{name}_{term or 0}",
    )(*args)


def _full_spec(sg, name):
    rt, nc, ni = sg["rt"], sg["nc"], sg["ni"]
    if name in _ROW_SHARDED:
        return (sg["L"], N_CHIPS * sg["R"], sg["C"]), pl.BlockSpec((None, rt, FLAT_W), lambda k, l, j, i: (l, k * ni + i, j))
    return ((sg["L"], sg["rpad"], N_CHIPS * nc * FLAT_W),
            pl.BlockSpec((None, rt, FLAT_W), lambda k, l, j, i: (l, i, k * nc + j)))


def _slab_unpack(gathered, slots, name, layout, own=None):
    sg = _seg(name, layout)
    rb = _flat_block(sg)
    shape, _ = _full_spec(sg, name)
    rt, nc, ni = sg["rt"], sg["nc"], sg["ni"]
    width = nc * FLAT_W
    last = gathered.shape[0] - 1

    def body(*refs):
        s_ref, o_ref = refs[0], refs[-1]
        is_own = s_ref[pl.program_id(0)] == OWN_SLOT
        for j in range(nc):
            val = refs[1 + j][...]
            if own is not None:
                val = jnp.where(is_own, refs[1 + nc + j][...], val)
            o_ref[:, j * FLAT_W:(j + 1) * FLAT_W] = val

    if name in _ROW_SHARDED:
        o_spec = pl.BlockSpec((None, rt, width), lambda k, l, i, s: (l, k * ni + i, 0))
    else:
        o_spec = pl.BlockSpec((None, rt, width), lambda k, l, i, s: (l, i, k))
    in_specs = [pl.BlockSpec((None, rt, FLAT_W), lambda k, l, i, s, j=j: (jnp.minimum(s[k], last), rb(l, j, i), 0))
                for j in range(nc)]
    args = [gathered] * nc
    if own is not None:
        in_specs += [pl.BlockSpec((rt, FLAT_W), lambda k, l, i, s, j=j: (rb(l, j, i), 0)) for j in range(nc)]
        args += [own] * nc
    return pl.pallas_call(
        body, out_shape=jax.ShapeDtypeStruct(shape, BF16),
        grid_spec=pltpu.PrefetchScalarGridSpec(num_scalar_prefetch=1, grid=(N_CHIPS, sg["L"], ni), in_specs=in_specs,
                                               out_specs=o_spec),
        compiler_params=_cparams(("parallel",) * 3), name=f"unpack_{name}",
    )(slots, *args)


def _slab_pack_grad(flat4, g, name, layer=None):
    sg = _seg(name)
    rows = _layout_rows(_grad_layout(name))
    shape, _ = _full_spec(sg, name)
    n_layers = sg["L"] if layer is None else 1
    assert g.shape == (n_layers,) + shape[1:], (name, g.shape, shape)
    rt, nc = sg["rt"], sg["nc"]
    assert sg["ni"] == 1 and sg["off"] % (nc * rt) == 0
    base = sg["off"] // (nc * rt) + (layer or 0)

    def body(*refs):
        g_ref, o_ref = refs[-2], refs[-1]
        for j in range(nc):
            o_ref[j * rt:(j + 1) * rt, :] = g_ref[:, j * FLAT_W:(j + 1) * FLAT_W]

    if name in _ROW_SHARDED:
        spec = pl.BlockSpec((None, rt, nc * FLAT_W), lambda k, l: (l, k, 0))
    else:
        spec = pl.BlockSpec((None, rt, nc * FLAT_W), lambda k, l: (l, 0, k))
    in_specs, args = [spec], [g]
    if flat4 is not None:
        in_specs, args = [pl.BlockSpec(memory_space=pl.ANY)] + in_specs, [flat4] + args
    return pl.pallas_call(
        body, out_shape=jax.ShapeDtypeStruct((N_CHIPS, rows, FLAT_W), F32), grid=(N_CHIPS, n_layers),
        in_specs=in_specs, out_specs=pl.BlockSpec((None, nc * rt, FLAT_W), lambda k, l: (k, base + l, 0)),
        input_output_aliases={0: 0} if flat4 is not None else {},
        compiler_params=_cparams(("parallel",) * 2), name=f"packgrad_{name}_{layer or 0}",
    )(*args)


def _adamw_shard(w, m, v, g_mine, g_other, c_arr, name):
    sg = _seg(name)
    rt = sg["rt"]
    rb = _flat_block(sg)
    per_half = sg["half"] // rt

    def half_of(l, j, i):
        return (rb(l, j, i) * rt) // sg["half"]

    def body(c_ref, w_ref, m_ref, v_ref, gm_ref, go_ref, g_ref, d_ref, mo_ref, vo_ref):
        is_mine = half_of(pl.program_id(0), pl.program_id(1), pl.program_id(2)) == c_ref[0]
        gv = jnp.where(is_mine, gm_ref[...], go_ref[...])
        g_ref[...] = gv
        d_ref[...], mo_ref[...], vo_ref[...] = _adam_update(w_ref[...], gv, m_ref[...], v_ref[...])

    nat = pl.BlockSpec((None, rt, FLAT_W), lambda l, j, i, c: (l, i, j))
    half = pl.BlockSpec((rt, FLAT_W), lambda l, j, i, c: (rb(l, j, i) - half_of(l, j, i) * per_half, 0))
    sds = jax.ShapeDtypeStruct(w.shape, F32)
    return pl.pallas_call(
        body, out_shape=[sds] * 4,
        grid_spec=pltpu.PrefetchScalarGridSpec(num_scalar_prefetch=1, grid=(sg["L"], sg["nc"], sg["ni"]),
                                               in_specs=[nat, nat, nat, half, half], out_specs=[nat] * 4),
        compiler_params=_cparams(("parallel", "parallel", "parallel")), name=f"adamw_{name}",
    )(c_arr, w, m, v, g_mine, g_other)


def _pack_small(vals, loss=None):
    parts = [vals[name].astype(F32).reshape(-1) for name, _ in _SMALL]
    if loss is not None:
        parts.append(loss.reshape(1))
    flat = jnp.concatenate(parts)
    return jnp.pad(flat, (0, SMALL_ROWS * 1024 - flat.shape[0])).reshape(SMALL_ROWS, 1024)


def _unpack_small(flat):
    flat = flat.reshape(-1)
    out = {}
    o = 0
    for name, shape in _SMALL:
        n = int(np.prod(shape))
        out[name] = flat[o:o + n].reshape(shape)
        o += n
    return out, flat[o]


_BIG = ("a_w_in", "a_w_out", "b_w_q", "b_w_out", "w_kv", "ffn_w_up", "ffn_w_down", "ffn_conv_w")
A_IN_PAD = 896


def _pack_weights(w, layout):
    flat = None
    for name in _LAYOUTS[layout]:
        t = _native3(w[name], name)
        for term in ((0, 1, 2) if name == "ffn_conv_w" else (None,)):
            flat = _slab_pack(flat, t, name, layout, term)
    return flat


def _early_weights(gathered, slots):
    a_in = _slab_unpack(gathered, slots, "a_w_in", "early")
    a_in = a_in.reshape(D, N_CHIPS, A_IN_PAD)[:, :, :772].reshape(D, N_CHIPS * 772)
    return dict(a_w_in=a_in, a_w_out=_slab_unpack(gathered, slots, "a_w_out", "early")[0])


def _late_weights(landed, slots, own):
    full = {name: _slab_unpack(landed, slots, name, "late", own) for name in _LAYOUTS["late"] if name != "ffn_conv_w"}
    sg = _seg("ffn_conv_w", "late")
    n1 = sg["nc"] * sg["rpad"]
    conv = slice(sg["off"], sg["off"] + CONV_TERMS * n1)
    conv_rows = jnp.concatenate([landed[:, conv], own[None, conv]], axis=0)
    per_chip = []
    for k in range(N_CHIPS):
        rows = lax.dynamic_index_in_dim(conv_rows, slots[k], axis=0, keepdims=False)
        terms = [_unslabs(rows[i * n1:(i + 1) * n1], 1, sg["R"], sg["C"], sg["rpad"]).astype(F32) for i in range(CONV_TERMS)]
        per_chip.append((terms[0] + terms[1]) + terms[2])
    cw = jnp.concatenate(per_chip, axis=2).reshape(2, 3, 2, FF).transpose(0, 2, 1, 3)
    return dict(b_w_q=full["b_w_q"][0], b_w_out=full["b_w_out"][0], w_kv=full["w_kv"][0], ffn_w_up=full["ffn_w_up"],
                ffn_w_down=full["ffn_w_down"], conv_w=cw)


def _shard_grads(g, layout):
    def full(name):
        if name == "a_w_in":
            a_in = jnp.pad(g[name].reshape(D, N_CHIPS, 772), ((0, 0), (0, 0), (0, A_IN_PAD - 772)))
            return a_in.reshape(1, D, N_CHIPS * A_IN_PAD)
        if name == "ffn_conv_w":
            sgc = _seg(name)
            return jnp.pad(g[name].reshape(1, sgc["R"], 2 * FF), ((0, 0), (0, sgc["rpad"] - sgc["R"]), (0, 0)))
        return g[name] if g[name].ndim == 3 else g[name][None]

    flat4 = None
    for name in _LAYOUTS[layout]:
        if isinstance(g[name], (list, tuple)):
            for layer, t in enumerate(g[name]):
                flat4 = _slab_pack_grad(flat4, t[None], name, layer)
        else:
            flat4 = _slab_pack_grad(flat4, full(name), name)
    return flat4


_WEIGHTS = ["a_w_in", "a_b_f", "a_w_out", "b_w_q", "b_w_out", "kv_norm_g", "w_kv", "mix_norm_g", "ffn_norm_g", "ffn_w_up",
            "ffn_conv_w", "ffn_conv_b", "ffn_w_down", "final_norm_g"]


def kernel(x, a_w_in, a_b_f, a_w_out, b_w_q, b_w_out, kv_norm_g, w_kv, mix_norm_g, ffn_norm_g, ffn_w_up, ffn_conv_w, ffn_conv_b, ffn_w_down, final_norm_g, loss_target, m_a_w_in, m_a_b_f, m_a_w_out, m_b_w_q, m_b_w_out, m_kv_norm_g, m_w_kv, m_mix_norm_g, m_ffn_norm_g, m_ffn_w_up, m_ffn_conv_w, m_ffn_conv_b, m_ffn_w_down, m_final_norm_g, v_a_w_in, v_a_b_f, v_a_w_out, v_b_w_q, v_b_w_out, v_kv_norm_g, v_w_kv, v_mix_norm_g, v_ffn_norm_g, v_ffn_w_up, v_ffn_conv_w, v_ffn_conv_b, v_ffn_w_down, v_final_norm_g):
    w = dict(a_w_in=a_w_in, a_b_f=a_b_f, a_w_out=a_w_out, b_w_q=b_w_q, b_w_out=b_w_out, kv_norm_g=kv_norm_g, w_kv=w_kv,
             mix_norm_g=mix_norm_g, ffn_norm_g=ffn_norm_g, ffn_w_up=ffn_w_up, ffn_conv_w=ffn_conv_w, ffn_conv_b=ffn_conv_b,
             ffn_w_down=ffn_w_down, final_norm_g=final_norm_g)
    m = dict(a_w_in=m_a_w_in, a_b_f=m_a_b_f, a_w_out=m_a_w_out, b_w_q=m_b_w_q, b_w_out=m_b_w_out, kv_norm_g=m_kv_norm_g,
             w_kv=m_w_kv, mix_norm_g=m_mix_norm_g, ffn_norm_g=m_ffn_norm_g, ffn_w_up=m_ffn_w_up, ffn_conv_w=m_ffn_conv_w,
             ffn_conv_b=m_ffn_conv_b, ffn_w_down=m_ffn_w_down, final_norm_g=m_final_norm_g)
    v = dict(a_w_in=v_a_w_in, a_b_f=v_a_b_f, a_w_out=v_a_w_out, b_w_q=v_b_w_q, b_w_out=v_b_w_out, kv_norm_g=v_kv_norm_g,
             w_kv=v_w_kv, mix_norm_g=v_mix_norm_g, ffn_norm_g=v_ffn_norm_g, ffn_w_up=v_ffn_w_up, ffn_conv_w=v_ffn_conv_w,
             ffn_conv_b=v_ffn_conv_b, ffn_w_down=v_ffn_w_down, final_norm_g=v_final_norm_g)

    c_arr = lax.axis_index("c").astype(jnp.int32).reshape(1)
    k_arr = (2 * lax.axis_index("x") + lax.axis_index("y")).astype(jnp.int32).reshape(1)
    xi, yi = lax.axis_index("x"), lax.axis_index("y")
    late_slots = jnp.stack([jnp.where(k == k_arr[0], OWN_SLOT, 2 * ((k & 1) ^ yi) + ((k >> 1) ^ xi) - 1)
                            for k in range(N_CHIPS)]).astype(jnp.int32)
    w_late = _pack_weights(w, "late")
    land = lax.empty((OWN_SLOT,) + w_late.shape, BF16)
    send_sems, recv_sems, w_thru, land_thru, token = _late_gather_start(w_late, land)
    w_early = _pack_weights(w, "early")
    early = _allgather_shards(w_early, _place_own(w_early, k_arr, "early_place_own"))
    p = _early_weights(early, jnp.arange(N_CHIPS, dtype=jnp.int32))
    cb = ffn_conv_b.reshape(2, 2, 1, FF)
    p.update(a_b_f=a_b_f, kv_norm_g=kv_norm_g, mix_norm_g=mix_norm_g + token[0, 0], ffn_norm_g=ffn_norm_g,
             final_norm_g=final_norm_g, conv_b=cb)

    def late_weights(after):
        own, landed = _late_gather_wait(send_sems, recv_sems, w_thru, land_thru, after)
        return _late_weights(landed, late_slots, own)

    started = {}

    def late_grads_ready(g_so_far):
        gflat = _shard_grads(g_so_far, "grad_late")
        pair = _pair_sum(gflat, _sibling_swap_half(gflat, "late"), c_arr, "late")
        land = lax.empty((3,) + pair.shape[1:], BF16)
        *handles, token = _chip_exchange_start(pair, land)
        started["handles"] = handles
        return token

    loss_part, grad_x, g = _local_step(x[0], loss_target[0], p, late_weights, late_grads_ready)

    halves = {}
    pair, landed = _chip_exchange_wait(*started["handles"], grad_x)
    g_mine = _chip_sum(pair, landed, k_arr, "late")
    halves["grad_late"] = (g_mine, _sibling_send(g_mine, "late"))
    gflat = _shard_grads(g, "grad_early")
    pair = _pair_sum(gflat, _sibling_swap_half(gflat, "early"), c_arr, "early")
    g_mine = _chip_sum(pair, _chip_exchange(pair, "early"), k_arr, "early")
    halves["grad_early"] = (g_mine, _sibling_send(g_mine, "early"))
    small, loss = _unpack_small(_allreduce_small(_pack_small(g, loss_part)))

    big = [{}, {}, {}, {}]
    for name in _BIG:
        sg = _seg(name)
        g_mine, g_other = halves[_grad_layout(name)]
        res = _adamw_shard(_native3(w[name], name), _native3(m[name], name), _native3(v[name], name), g_mine, g_other,
                           c_arr, name)
        for store, t in zip(big, res):
            store[name] = t[:, :sg["R"], :].reshape(_SHARD_SHAPES[name])
    dws, mns, vns = _adamw(_pack_small(w), _pack_small(small), _pack_small(m), _pack_small(v), "adamw_small")
    sml = [small] + [_unpack_small(t)[0] for t in (dws, mns, vns)]
    outs = [loss, grad_x[None]]
    for b, s in zip(big, sml):
        outs += [b[n] if n in b else s[n] for n in _WEIGHTS]
    return tuple(outs)
```

```python
import numpy as np
import jax
import jax.numpy as jnp
from jax import lax
from jax.experimental import pallas as pl
from jax.experimental.pallas import tpu as pltpu

F32 = jnp.float32
BF16 = jnp.bfloat16
MESH = pl.DeviceIdType.MESH

S = 4096
D = 1024
A_HEADS = 16
HEAD_DIM = 64
QKV_W = 3 * A_HEADS * HEAD_DIM
B_GROUPS = ((128, 1), (512, 4), (2048, 16))
B_HPG = 8
B_Q_W = 3 * B_HPG * HEAD_DIM
B_OUT_W = B_HPG * HEAD_DIM
B_KV_W = 2 * B_Q_W
B_WIN = 128
FF = 2816
RMS_EPS = 1e-6
SCALE = HEAD_DIM ** -0.5
N_CHIPS = 4

ADAM_LR, ADAM_B1, ADAM_B2, ADAM_EPS, ADAM_WD, ADAM_STEP = 0.001, 0.9, 0.999, 1e-08, 0.01, 10

V7X_VMEM_LIMIT = 48 * 1024 * 1024
LANES = 128
NEG_INF = float("-inf")

FLAT_W = LANES
_SEGS = (("ffn_w_down", 2, 704, 1024, 704), ("a_w_in", 1, 1024, 772, 1024), ("a_w_out", 1, 256, 1024, 256),
         ("b_w_q", 1, 1024, 384, 1024), ("b_w_out", 1, 512, 256, 512), ("w_kv", 1, 1024, 768, 1024),
         ("ffn_w_up", 2, 1024, 1408, 1024), ("ffn_conv_w", 1, 6, 1408, 16))
CONV_TERMS = 3


def _seg_rows(name, L, R, C, rpad):
    return (CONV_TERMS if name == "ffn_conv_w" else 1) * L * (-(-C // FLAT_W)) * rpad


SMALL_ROWS = 24


def _cparams(sem=None, **kw):
    return pltpu.CompilerParams(dimension_semantics=sem, vmem_limit_bytes=V7X_VMEM_LIMIT, **kw)


_DN = {"nn": (((1,), (0,)), ((), ())), "nt": (((1,), (1,)), ((), ())), "tn": (((0,), (0,)), ((), ()))}


def _mm(a, b, *, mode, tm, tn, tk, name, out_dtype=F32, res=None, a_split=0, b_split=0, o_split=0):
    if mode == "tn":
        K = a.shape[0]
        M = a.shape[1]
    else:
        M = a.shape[-2]
        K = a.shape[-1] * (2 if a_split else 1)
    if mode == "nt":
        N = b.shape[0]
    else:
        N = b.shape[-1] * (2 if b_split else 1)
    assert M % tm == 0 and N % tn == 0 and K % tk == 0, (name, M, N, K, tm, tn, tk)
    nk = K // tk

    if mode == "tn":
        a_spec = pl.BlockSpec((tk, tm), lambda i, j, k: (k, i))
    elif a_split:
        a_spec = pl.BlockSpec((None, tm, tk), lambda i, j, k: (k // a_split, i, k % a_split))
    else:
        a_spec = pl.BlockSpec((tm, tk), lambda i, j, k: (i, k))
    if mode == "nt":
        b_spec = pl.BlockSpec((tn, tk), lambda i, j, k: (j, k))
    elif b_split:
        b_spec = pl.BlockSpec((None, tk, tn), lambda i, j, k: (j // b_split, k, j % b_split))
    else:
        b_spec = pl.BlockSpec((tk, tn), lambda i, j, k: (k, j))
    if o_split:
        o_spec = pl.BlockSpec((None, tm, tn), lambda i, j, k: (j // o_split, i, j % o_split))
        out_shape = jax.ShapeDtypeStruct((2, M, N // 2), out_dtype)
    else:
        o_spec = pl.BlockSpec((tm, tn), lambda i, j, k: (i, j))
        out_shape = jax.ShapeDtypeStruct((M, N), out_dtype)
    in_specs = [a_spec, b_spec]
    args = [a, b]
    if res is not None:
        in_specs.append(pl.BlockSpec((tm, tn), lambda i, j, k: (i, j)))
        args.append(res)

    def body(*refs):
        if res is not None:
            a_ref, b_ref, r_ref, o_ref = refs[:4]
        else:
            a_ref, b_ref, o_ref = refs[:3]
            r_ref = None
        p = lax.dot_general(a_ref[...].astype(BF16), b_ref[...].astype(BF16), _DN[mode], preferred_element_type=F32)

        def finish(r):
            if r_ref is not None:
                r = r + r_ref[...]
            o_ref[...] = r.astype(out_dtype)

        if nk == 1:
            finish(p)
        else:
            acc = refs[-1]
            k = pl.program_id(2)

            @pl.when(k == 0)
            def _():
                acc[...] = p

            @pl.when(k > 0)
            def _():
                acc[...] += p

            @pl.when(k == nk - 1)
            def _():
                finish(acc[...])

    return pl.pallas_call(
        body, out_shape=out_shape, grid=(M // tm, N // tn, nk), in_specs=in_specs, out_specs=o_spec,
        scratch_shapes=[pltpu.VMEM((tm, tn), F32)] if nk > 1 else [],
        compiler_params=_cparams(("parallel", "parallel", "arbitrary")), name=name,
    )(*args)


NORM_ROWS = 256


def _rms_fwd(x, gains, name):
    n = len(gains)

    def body(x_ref, *refs):
        xv = x_ref[...]
        y = xv * lax.rsqrt(jnp.mean(xv * xv, axis=-1, keepdims=True) + RMS_EPS)
        for g_ref, o_ref in zip(refs[:n], refs[n:]):
            o_ref[...] = (y * g_ref[...]).astype(BF16)

    row = pl.BlockSpec((NORM_ROWS, D), lambda i: (i, 0))
    return pl.pallas_call(
        body, out_shape=[jax.ShapeDtypeStruct((S, D), BF16)] * n, grid=(S // NORM_ROWS,),
        in_specs=[row] + [pl.BlockSpec((1, D), lambda i: (0, 0))] * n, out_specs=[row] * n,
        compiler_params=_cparams(("parallel",)), name=name,
    )(x, *[g.reshape(1, D) for g in gains])


def _rms_bwd(x, dres, pairs, name):
    n = len(pairs)

    def body(*refs):
        x_ref, dres_ref = refs[0], refs[1]
        g_refs = refs[2:2 + 2 * n:2]
        dh_refs = refs[3:3 + 2 * n:2]
        dx_ref, dxb_ref = refs[2 + 2 * n], refs[3 + 2 * n]
        dg_refs = refs[4 + 2 * n:]
        i = pl.program_id(0)
        xv = x_ref[...]
        r = lax.rsqrt(jnp.mean(xv * xv, axis=-1, keepdims=True) + RMS_EPS)
        y = xv * r
        dx = dres_ref[...]
        for g_ref, dh_ref, dg_ref in zip(g_refs, dh_refs, dg_refs):
            dh = dh_ref[...]
            dy = dh * g_ref[...]
            dx = dx + r * (dy - y * jnp.mean(dy * y, axis=-1, keepdims=True))
            part = jnp.sum(dh * y, axis=0, keepdims=True)

            @pl.when(i == 0)
            def _():
                dg_ref[...] = part

            @pl.when(i > 0)
            def _():
                dg_ref[...] += part

        dx_ref[...] = dx
        dxb_ref[...] = dx.astype(BF16)

    row = pl.BlockSpec((NORM_ROWS, D), lambda i: (i, 0))
    vec = pl.BlockSpec((1, D), lambda i: (0, 0))
    in_specs = [row, row]
    args = [x, dres]
    for g, dh in pairs:
        in_specs += [vec, row]
        args += [g.reshape(1, D), dh]
    outs = pl.pallas_call(
        body,
        out_shape=[jax.ShapeDtypeStruct((S, D), F32), jax.ShapeDtypeStruct((S, D), BF16)]
        + [jax.ShapeDtypeStruct((1, D), F32)] * n,
        grid=(S // NORM_ROWS,), in_specs=in_specs, out_specs=[row, row] + [vec] * n,
        compiler_params=_cparams(("arbitrary",)), name=name,
    )(*args)
    return outs[0], outs[1], list(outs[2:])


def _loss_head(x, g, target, name):
    def body(x_ref, g_ref, t_ref, loss_ref, dx_ref, dxb_ref, dg_ref):
        i = pl.program_id(0)
        xv = x_ref[...]
        gv = g_ref[...]
        r = lax.rsqrt(jnp.mean(xv * xv, axis=-1, keepdims=True) + RMS_EPS)
        y = xv * r
        err = y * gv - t_ref[...]
        lpart = jnp.broadcast_to(jnp.sum(err * err, keepdims=True) * (0.5 / D), (1, LANES))
        dh = err * (1.0 / D)
        dy = dh * gv
        dx = r * (dy - y * jnp.mean(dy * y, axis=-1, keepdims=True))
        part = jnp.sum(dh * y, axis=0, keepdims=True)

        @pl.when(i == 0)
        def _():
            dg_ref[...] = part
            loss_ref[...] = lpart

        @pl.when(i > 0)
        def _():
            dg_ref[...] += part
            loss_ref[...] += lpart

        dx_ref[...] = dx
        dxb_ref[...] = dx.astype(BF16)

    row = pl.BlockSpec((NORM_ROWS, D), lambda i: (i, 0))
    vec = pl.BlockSpec((1, D), lambda i: (0, 0))
    return pl.pallas_call(
        body,
        out_shape=[jax.ShapeDtypeStruct((1, LANES), F32), jax.ShapeDtypeStruct((S, D), F32),
                   jax.ShapeDtypeStruct((S, D), BF16), jax.ShapeDtypeStruct((1, D), F32)],
        grid=(S // NORM_ROWS,), in_specs=[row, vec, row],
        out_specs=[pl.BlockSpec((1, LANES), lambda i: (0, 0)), row, row, vec],
        compiler_params=_cparams(("arbitrary",)), name=name,
    )(x, g.reshape(1, D), target)


SCAN_ROWS = 256


def _split3(v):
    hi = v.astype(BF16)
    r1 = v - hi.astype(F32)
    mid = r1.astype(BF16)
    lo = (r1 - mid.astype(F32)).astype(BF16)
    return hi, mid, lo


def _tri_dot(tri, v):
    hi, mid, lo = _split3(v)
    dn = _DN["nn"]
    return (lax.dot_general(tri, hi, dn, preferred_element_type=F32)
            + lax.dot_general(tri, mid, dn, preferred_element_type=F32)
            + lax.dot_general(tri, lo, dn, preferred_element_type=F32))


def _log_sigmoid(z):
    return jnp.minimum(z, 0.0) - jnp.log(1.0 + jnp.exp(-jnp.abs(z)))


GATE_LANES = 6


def _gate_lane_tables():
    pq = np.zeros((3 * LANES, A_HEADS * HEAD_DIM), np.float32)
    pk = np.zeros((3 * LANES, A_HEADS * HEAD_DIM), np.float32)
    one_q = np.zeros((1, A_HEADS * HEAD_DIM), np.float32)
    one_k = np.zeros((1, A_HEADS * HEAD_DIM), np.float32)
    for h in range(A_HEADS):
        pos = (h // 2) * LANES + (HEAD_DIM if h % 2 == 0 else 0)
        for term in range(3):
            pq[term * LANES + h, pos + term] = 1.0
            pk[term * LANES + h, pos + 3 + term] = -1.0
        one_q[0, pos + 3:pos + GATE_LANES] = 1.0
        one_k[0, pos:pos + 3] = 1.0
    return jnp.asarray(pq, BF16), jnp.asarray(pk, BF16), jnp.asarray(one_q), jnp.asarray(one_k)


def _fgate_fwd(pf, bias, name):
    tri = jnp.tril(jnp.ones((SCAN_ROWS, SCAN_ROWS), F32)).astype(BF16)
    pq, pk, one_q, one_k = _gate_lane_tables()

    def body(pf_ref, b_ref, tri_ref, pq_ref, pk_ref, oq_ref, ok_ref, aq_ref, ak_ref, c_sc):
        carry = jnp.zeros((1, LANES), F32)
        for blk in range(S // SCAN_ROWS):
            rows = pl.ds(blk * SCAN_ROWS, SCAN_ROWS)
            lf = _log_sigmoid(pf_ref[rows, :] + b_ref[...])
            c_sc[...] = _tri_dot(tri_ref[...], lf) + carry
            carry = c_sc[pl.ds(SCAN_ROWS - 1, 1), :]
            terms = jnp.concatenate(_split3(c_sc[...]), axis=1)
            aq = lax.dot_general(terms, pq_ref[...], _DN["nn"], preferred_element_type=F32) + oq_ref[...]
            ak = lax.dot_general(terms, pk_ref[...], _DN["nn"], preferred_element_type=F32) + ok_ref[...]
            aq_ref[rows, :] = aq.astype(BF16)
            ak_ref[rows, :] = ak.astype(BF16)

    wide = jax.ShapeDtypeStruct((S, A_HEADS * HEAD_DIM), BF16)
    return pl.pallas_call(
        body, out_shape=[wide, wide], scratch_shapes=[pltpu.VMEM((SCAN_ROWS, LANES), F32)],
        compiler_params=_cparams(), name=name,
    )(pf, bias, tri, pq, pk, one_q, one_k)


def _fgate_bwd(pf, bias, dc_key, dc_query, name):
    triu = jnp.triu(jnp.ones((SCAN_ROWS, SCAN_ROWS), F32)).astype(BF16)

    def body(pf_ref, b_ref, dck_ref, dcq_ref, tri_ref, dpf_ref, db_ref, dlf_ref):
        carry = jnp.zeros((1, LANES), F32)
        db = jnp.zeros((1, LANES), F32)
        lane = lax.broadcasted_iota(jnp.int32, (SCAN_ROWS, LANES), 1)
        for blk in reversed(range(S // SCAN_ROWS)):
            rows = pl.ds(blk * SCAN_ROWS, SCAN_ROWS)
            dc = dck_ref[rows, :] + dcq_ref[rows, :]
            dlf_ref[rows, :] = _tri_dot(tri_ref[...], dc) + carry
            carry = dlf_ref[pl.ds(blk * SCAN_ROWS, 1), :]
            z = pf_ref[rows, :] + b_ref[...]
            e = jnp.exp(-jnp.abs(z))
            sig_neg = jnp.where(z >= 0.0, e, 1.0) / (1.0 + e)
            dz = jnp.where(lane < A_HEADS, dlf_ref[rows, :] * sig_neg, 0.0)
            dpf_ref[rows, :] = dz.astype(BF16)
            db = db + jnp.sum(dz, axis=0, keepdims=True)
        db_ref[...] = db

    return pl.pallas_call(
        body, out_shape=[jax.ShapeDtypeStruct((S, LANES), BF16), jax.ShapeDtypeStruct((1, LANES), F32)],
        scratch_shapes=[pltpu.VMEM((S, LANES), F32)],
        compiler_params=_cparams(), name=name,
    )(pf, bias, dc_key, dc_query, triu)


FOX_T = 512


def _first_head(shape):
    return lax.broadcasted_iota(jnp.int32, shape, len(shape) - 1) < HEAD_DIM


def _each_head(x, lo):
    zero = jnp.zeros_like(x)
    return jnp.where(lo, x, zero), jnp.where(lo, zero, x)


def _fox_pair_fwd(qkv, aug_q, aug_k, name):
    T = FOX_T
    nq = S // T
    NP = A_HEADS // 2

    def body(q_ref, k_ref, v_ref, aq_ref, ak_ref, o_ref, lse_ref, m_sc, l_sc, acc_sc):
        i = pl.program_id(1)
        j = pl.program_id(2)
        lo = _first_head((T, LANES))

        @pl.when(j == 0)
        def _():
            m_sc[...] = jnp.full((2, T, LANES), NEG_INF, F32)
            l_sc[...] = jnp.zeros((2, T, LANES), F32)
            acc_sc[...] = jnp.zeros((T, LANES), F32)

        def step(diagonal):
            qs = q_ref[...] * jnp.asarray(SCALE, BF16)
            aq, ak, kv = aq_ref[...], ak_ref[...], k_ref[...]
            q2 = (jnp.where(lo, qs, aq), jnp.where(lo, aq, qs))
            k2 = (jnp.where(lo, kv, ak), jnp.where(lo, ak, kv))
            if diagonal:
                causal = lax.broadcasted_iota(jnp.int32, (T, T), 0) >= lax.broadcasted_iota(jnp.int32, (T, T), 1)
            pv, alphas = None, []
            for h, vh in enumerate(_each_head(v_ref[...], lo)):
                s = lax.dot_general(q2[h], k2[h], _DN["nt"], preferred_element_type=F32)
                if diagonal:
                    s = jnp.where(causal, s, NEG_INF)
                m_prev = m_sc[h]
                m_new = jnp.maximum(m_prev, jnp.max(s, axis=1, keepdims=True))
                alpha = jnp.exp(m_prev - m_new)
                p = jnp.exp(s - jnp.tile(m_new, (1, T // LANES)))
                l_sc[h] = alpha * l_sc[h] + jnp.sum(p, axis=1, keepdims=True)
                m_sc[h] = m_new
                d = lax.dot_general(p.astype(BF16), vh, _DN["nn"], preferred_element_type=F32)
                pv = d if pv is None else pv + d
                alphas.append(alpha)
            acc_sc[...] = jnp.where(lo, alphas[0], alphas[1]) * acc_sc[...] + pv

        @pl.when(j < i)
        def _():
            step(False)

        @pl.when(j == i)
        def _():
            step(True)
            o_ref[...] = (acc_sc[...] * jnp.where(lo, 1.0 / l_sc[0], 1.0 / l_sc[1])).astype(BF16)
            for h in range(2):
                lse_ref[h] = (m_sc[h] + jnp.log(l_sc[h]))[:, 0:1]

    qs_ = pl.BlockSpec((T, LANES), lambda p, i, j: (i, p))
    ks = pl.BlockSpec((T, LANES), lambda p, i, j: (jnp.minimum(i, j), NP + p))
    vs = pl.BlockSpec((T, LANES), lambda p, i, j: (jnp.minimum(i, j), 2 * NP + p))
    aks = pl.BlockSpec((T, LANES), lambda p, i, j: (jnp.minimum(i, j), p))
    col = pl.BlockSpec((2, T, 1), lambda p, i, j: (p, i, 0))
    return pl.pallas_call(
        body, out_shape=[jax.ShapeDtypeStruct((S, A_HEADS * HEAD_DIM), BF16), jax.ShapeDtypeStruct((A_HEADS, S, 1), F32)],
        grid=(NP, nq, nq), in_specs=[qs_, ks, vs, qs_, aks], out_specs=[qs_, col],
        scratch_shapes=[pltpu.VMEM((2, T, LANES), F32), pltpu.VMEM((2, T, LANES), F32), pltpu.VMEM((T, LANES), F32)],
        compiler_params=_cparams(("parallel", "parallel", "arbitrary")), name=name,
    )(qkv, qkv, qkv, aug_q, aug_k)


def _fox_pair_bwd(qkv, do, lse_row, delta_row, aug_q, aug_k, name):
    T = FOX_T
    nq = S // T
    NP = A_HEADS // 2

    def body(q_ref, k_ref, v_ref, do_ref, lse_ref, dl_ref, aq_ref, ak_ref, dq_ref, dk_ref, dv_ref, dc_ref, dcq_ref,
             dq_sc, dk_sc, dv_sc, dc_sc):
        j = pl.program_id(1)
        i = pl.program_id(2)
        lo = _first_head((T, LANES))

        @pl.when(jnp.logical_and(j == 0, i == 0))
        def _():
            dq_sc[...] = jnp.zeros((S, LANES), F32)
            dcq_ref[...] = jnp.zeros((2, nq, 1, T), F32)

        @pl.when(i == j)
        def _():
            dk_sc[...] = jnp.zeros((T, LANES), F32)
            dv_sc[...] = jnp.zeros((T, LANES), F32)
            dc_sc[...] = jnp.zeros((2, T, 1), F32)

        def step(diagonal):
            qv = q_ref[...]
            kv = k_ref[...]
            dov = do_ref[...].astype(BF16)
            qs = qv * jnp.asarray(SCALE, BF16)
            aq, ak = aq_ref[...], ak_ref[...]
            q2 = (jnp.where(lo, qs, aq), jnp.where(lo, aq, qs))
            k2 = (jnp.where(lo, kv, ak), jnp.where(lo, ak, kv))
            if diagonal:
                causal = lax.broadcasted_iota(jnp.int32, (T, T), 1) >= lax.broadcasted_iota(jnp.int32, (T, T), 0)
            dv = dk = dq = None
            for h, (kh, vh, qh, doh) in enumerate(zip(_each_head(kv, lo), _each_head(v_ref[...], lo),
                                                      _each_head(qv, lo), _each_head(dov, lo))):
                st = lax.dot_general(k2[h], q2[h], _DN["nt"], preferred_element_type=F32)
                if diagonal:
                    st = jnp.where(causal, st, NEG_INF)
                pt = jnp.exp(st - lse_ref[h])
                d = lax.dot_general(pt.astype(BF16), doh, _DN["nn"], preferred_element_type=F32)
                dv = d if dv is None else dv + d
                dpt = lax.dot_general(vh, dov, _DN["nt"], preferred_element_type=F32)
                dst = pt * (dpt - dl_ref[h])
                dc_sc[h] -= jnp.sum(dst, axis=1, keepdims=True)
                dcq_ref[h, i] += jnp.sum(dst, axis=0, keepdims=True)
                dsb = (dst * SCALE).astype(BF16)
                d = lax.dot_general(dsb, qh, _DN["nn"], preferred_element_type=F32)
                dk = d if dk is None else dk + d
                d = lax.dot_general(dsb, kh, _DN["tn"], preferred_element_type=F32)
                dq = d if dq is None else dq + d
            dv_sc[...] += dv
            dk_sc[...] += dk
            rows = pl.ds(pl.multiple_of(i * T, T), T)
            dq_sc[rows, :] += dq

        @pl.when(i > j)
        def _():
            step(False)

        @pl.when(i == j)
        def _():
            step(True)

        @pl.when(i == nq - 1)
        def _():
            dk_ref[...] = dk_sc[...].astype(BF16)
            dv_ref[...] = dv_sc[...].astype(BF16)
            dc_ref[...] = dc_sc[...]

        @pl.when(jnp.logical_and(j == nq - 1, i == nq - 1))
        def _():
            dq_ref[...] = dq_sc[...].astype(BF16)

    qs = pl.BlockSpec((T, LANES), lambda p, j, i: (jnp.maximum(i, j), p))
    qrow = pl.BlockSpec((2, 1, T), lambda p, j, i: (p, 0, jnp.maximum(i, j)))
    ks = pl.BlockSpec((T, LANES), lambda p, j, i: (j, NP + p))
    vs = pl.BlockSpec((T, LANES), lambda p, j, i: (j, 2 * NP + p))
    kout = pl.BlockSpec((T, LANES), lambda p, j, i: (j, p))
    kcol = pl.BlockSpec((2, T, 1), lambda p, j, i: (p, j, 0))
    dqs = pl.BlockSpec((S, LANES), lambda p, j, i: (0, p))
    dcqs = pl.BlockSpec((2, nq, 1, T), lambda p, j, i: (p, 0, 0, 0))
    wide = jax.ShapeDtypeStruct((S, A_HEADS * HEAD_DIM), BF16)
    return pl.pallas_call(
        body,
        out_shape=[wide, wide, wide, jax.ShapeDtypeStruct((A_HEADS, S, 1), F32),
                   jax.ShapeDtypeStruct((A_HEADS, nq, 1, T), F32)],
        grid=(NP, nq, nq), in_specs=[qs, ks, vs, qs, qrow, qrow, qs, kout], out_specs=[dqs, kout, kout, kcol, dcqs],
        scratch_shapes=[pltpu.VMEM((S, LANES), F32), pltpu.VMEM((T, LANES), F32), pltpu.VMEM((T, LANES), F32),
                        pltpu.VMEM((2, T, 1), F32)],
        compiler_params=_cparams(("parallel", "arbitrary", "arbitrary")), name=name,
    )(qkv, qkv, qkv, do, lse_row, delta_row, aug_q, aug_k)


def _pair_rowdot(a, b, name):
    n = a.shape[1] // HEAD_DIM
    T = 1024

    def body(a_ref, b_ref, o_ref):
        prod = a_ref[...].astype(F32) * b_ref[...].astype(F32)
        lo = _first_head(prod.shape)
        o_ref[0] = jnp.sum(jnp.where(lo, prod, 0.0), axis=1, keepdims=True)
        o_ref[1] = jnp.sum(jnp.where(lo, 0.0, prod), axis=1, keepdims=True)

    blk = pl.BlockSpec((T, LANES), lambda p, i: (i, p))
    return pl.pallas_call(
        body, out_shape=jax.ShapeDtypeStruct((n, S, 1), F32), grid=(n // 2, S // T), in_specs=[blk, blk],
        out_specs=pl.BlockSpec((2, T, 1), lambda p, i: (p, i, 0)),
        compiler_params=_cparams(("parallel", "parallel")), name=name,
    )(a, b)


W = B_WIN
N_HG = 3 * B_HPG
N_BLK = S // W


def _dil_tables():
    slopes = np.exp2((-8.0 * np.arange(1, N_HG + 1, dtype=np.float32) / N_HG).astype(np.float32)).astype(np.float32)
    dil = np.repeat(np.array([d for _, d in B_GROUPS], np.float32), B_HPG)
    coef = (slopes * dil).astype(np.float32)
    nbs = np.repeat(np.array([S // d // W for _, d in B_GROUPS], np.int32), B_HPG)
    return jnp.asarray(coef), jnp.asarray(nbs)


DIL_SUB = 8
DIL_ROWS = DIL_SUB * W
DIL_STEPS = S // DIL_ROWS


def _dil_bias(coef, transposed):
    row = lax.broadcasted_iota(jnp.int32, (W, 2 * W), 0)
    col = lax.broadcasted_iota(jnp.int32, (W, 2 * W), 1)
    dist = (col - row) if transposed else (row + W - col)
    valid = jnp.logical_and(dist >= 0, dist <= W)
    return jnp.where(valid, -coef * dist.astype(F32), NEG_INF), col


NPG = B_HPG // 2
GROUP_W = B_HPG * HEAD_DIM


def _dil_pair_specs(qoff, koff, voff):
    prev_blk = lambda n: jnp.maximum(n * DIL_SUB - 1, 0)
    next_blk = lambda n: jnp.minimum((n + 1) * DIL_SUB, N_BLK - 1)
    return dict(
        o=pl.BlockSpec((DIL_ROWS, LANES), lambda h, n: (n, h)),
        o_next=pl.BlockSpec((W, LANES), lambda h, n: (next_blk(n), h)),
        q=pl.BlockSpec((DIL_ROWS, LANES), lambda h, n: (n, qoff + h)),
        q_next=pl.BlockSpec((W, LANES), lambda h, n: (next_blk(n), qoff + h)),
        k=pl.BlockSpec((DIL_ROWS, LANES), lambda h, n: (n, koff + h)),
        k_prev=pl.BlockSpec((W, LANES), lambda h, n: (prev_blk(n), koff + h)),
        v=pl.BlockSpec((DIL_ROWS, LANES), lambda h, n: (n, voff + h)),
        v_prev=pl.BlockSpec((W, LANES), lambda h, n: (prev_blk(n), voff + h)),
        col=pl.BlockSpec((2, DIL_ROWS, 1), lambda h, n: (h, n, 0)),
        col2=pl.BlockSpec((2, DIL_ROWS, 1), lambda h, n: (NPG + h, n, 0)),
        row=pl.BlockSpec((2, 1, DIL_ROWS), lambda h, n: (h, 0, n)),
        row_next=pl.BlockSpec((2, 1, W), lambda h, n: (h, 0, next_blk(n))),
        row2=pl.BlockSpec((2, 1, DIL_ROWS), lambda h, n: (NPG + h, 0, n)),
        row2_next=pl.BlockSpec((2, 1, W), lambda h, n: (NPG + h, 0, next_blk(n))),
        smem=pl.BlockSpec(memory_space=pltpu.SMEM))


def _dil_pair_fwd(g, q, k, v, qoff, koff, voff, name):
    coef_t, nbs_t = _dil_tables()

    def body(coef_ref, nbs_ref, q_ref, kh_ref, k_ref, vh_ref, v_ref, o_ref, lse_ref, kf, vf):
        hp = pl.program_id(0)
        n = pl.program_id(1)
        nbs = nbs_ref[B_HPG * g + 2 * hp]
        kf[0:W, :] = kh_ref[...]
        kf[W:, :] = k_ref[...]
        vf[0:W, :] = vh_ref[...]
        vf[W:, :] = v_ref[...]
        biases = [_dil_bias(coef_ref[B_HPG * g + 2 * hp + h], False) for h in range(2)]
        col = biases[0][1]
        lo = _first_head((W, LANES))
        lo2 = _first_head((2 * W, LANES))
        for b in range(DIL_SUB):
            first = lax.rem(n * DIL_SUB + b, nbs) == 0
            rows = slice(b * W, (b + 1) * W)
            both = slice(b * W, (b + 2) * W)
            qv = q_ref[rows, :]
            acc, inv = None, []
            for h, (kh, vh) in enumerate(zip(_each_head(kf[both, :], lo2), _each_head(vf[both, :], lo2))):
                s = lax.dot_general(qv, kh, _DN["nt"], preferred_element_type=F32) * SCALE + biases[h][0]
                s = jnp.where(jnp.logical_and(first, col < W), NEG_INF, s)
                m = jnp.max(s, axis=1, keepdims=True)
                p = jnp.exp(s - m)
                l = jnp.sum(p, axis=1, keepdims=True)
                d = lax.dot_general(p.astype(BF16), vh, _DN["nn"], preferred_element_type=F32)
                acc = d if acc is None else acc + d
                inv.append(1.0 / l)
                lse_ref[h, rows, :] = m + jnp.log(l)
            o_ref[rows, :] = acc * jnp.where(lo, inv[0], inv[1])

    sp = _dil_pair_specs(qoff, koff, voff)
    return pl.pallas_call(
        body, out_shape=[jax.ShapeDtypeStruct((S, GROUP_W), F32), jax.ShapeDtypeStruct((B_HPG, S, 1), F32)],
        grid=(NPG, DIL_STEPS), in_specs=[sp["smem"], sp["smem"], sp["q"], sp["k_prev"], sp["k"], sp["v_prev"], sp["v"]],
        out_specs=[sp["o"], sp["col"]], scratch_shapes=[pltpu.VMEM((DIL_ROWS + W, LANES), BF16)] * 2,
        compiler_params=_cparams(("parallel", "parallel")), name=name,
    )(coef_t, nbs_t, q, k, k, v, v)


def _dil_pair_merge(os, lses, name):
    T = 1024

    def body(o0_ref, o1_ref, o2_ref, l0_ref, l1_ref, l2_ref, om_ref, omb_ref, l_ref):
        lo = _first_head((T, LANES))
        weights = []
        for h in range(2):
            l0, l1, l2 = l0_ref[h], l1_ref[h], l2_ref[h]
            m = jnp.maximum(jnp.maximum(l0, l1), l2)
            e0, e1, e2 = jnp.exp(l0 - m), jnp.exp(l1 - m), jnp.exp(l2 - m)
            den = e0 + e1 + e2
            weights.append((e0 / den, e1 / den, e2 / den))
            l_ref[h] = m + jnp.log(den)
        om = (jnp.where(lo, weights[0][0], weights[1][0]) * o0_ref[...]
              + jnp.where(lo, weights[0][1], weights[1][1]) * o1_ref[...]
              + jnp.where(lo, weights[0][2], weights[1][2]) * o2_ref[...])
        om_ref[...] = om
        omb_ref[...] = om.astype(BF16)

    ob = pl.BlockSpec((T, LANES), lambda p, i: (i, p))
    lb = pl.BlockSpec((2, T, 1), lambda p, i: (p, i, 0))
    return pl.pallas_call(
        body,
        out_shape=[jax.ShapeDtypeStruct((S, B_OUT_W), F32), jax.ShapeDtypeStruct((S, B_OUT_W), BF16),
                   jax.ShapeDtypeStruct((B_HPG, S, 1), F32)],
        grid=(NPG, S // T), in_specs=[ob] * 3 + [lb] * 3, out_specs=[ob, ob, lb],
        compiler_params=_cparams(("parallel", "parallel")), name=name,
    )(*os, *lses)


def _dil_pair_dq(g, q, k, v, qoff, koff, voff, do, stats, name):
    coef_t, nbs_t = _dil_tables()

    def body(coef_ref, nbs_ref, q_ref, kh_ref, k_ref, vh_ref, v_ref, do_ref, l_ref, d_ref, dq_ref, kf, vf):
        hp = pl.program_id(0)
        n = pl.program_id(1)
        nbs = nbs_ref[B_HPG * g + 2 * hp]
        kf[0:W, :] = kh_ref[...]
        kf[W:, :] = k_ref[...]
        vf[0:W, :] = vh_ref[...]
        vf[W:, :] = v_ref[...]
        biases = [_dil_bias(coef_ref[B_HPG * g + 2 * hp + h], False) for h in range(2)]
        col = biases[0][1]
        lo2 = _first_head((2 * W, LANES))
        for b in range(DIL_SUB):
            first = lax.rem(n * DIL_SUB + b, nbs) == 0
            rows = slice(b * W, (b + 1) * W)
            both = slice(b * W, (b + 2) * W)
            qv = q_ref[rows, :]
            dov = do_ref[rows, :]
            acc = None
            for h, (kh, vh) in enumerate(zip(_each_head(kf[both, :], lo2), _each_head(vf[both, :], lo2))):
                s = lax.dot_general(qv, kh, _DN["nt"], preferred_element_type=F32) * SCALE + biases[h][0]
                s = jnp.where(jnp.logical_and(first, col < W), NEG_INF, s)
                p = jnp.exp(s - l_ref[h, rows, :])
                dp = lax.dot_general(dov, vh, _DN["nt"], preferred_element_type=F32)
                ds = (p * (dp - d_ref[h, rows, :]) * SCALE).astype(BF16)
                d = lax.dot_general(ds, kh, _DN["nn"], preferred_element_type=F32)
                acc = d if acc is None else acc + d
            dq_ref[rows, :] = acc.astype(BF16)

    sp = _dil_pair_specs(qoff, koff, voff)
    return pl.pallas_call(
        body, out_shape=jax.ShapeDtypeStruct((S, GROUP_W), BF16), grid=(NPG, DIL_STEPS),
        in_specs=[sp["smem"], sp["smem"], sp["q"], sp["k_prev"], sp["k"], sp["v_prev"], sp["v"], sp["o"], sp["col"],
                  sp["col2"]],
        out_specs=sp["o"], scratch_shapes=[pltpu.VMEM((DIL_ROWS + W, LANES), BF16)] * 2,
        compiler_params=_cparams(("parallel", "parallel")), name=name,
    )(coef_t, nbs_t, q, k, k, v, v, do, stats, stats)


def _dil_pair_dkv(g, q, k, v, qoff, koff, voff, do, stats, name):
    coef_t, nbs_t = _dil_tables()

    def body(coef_ref, nbs_ref, k_ref, v_ref, q_ref, qn_ref, do_ref, don_ref, l_ref, ln_ref, d_ref, dn_ref,
             dk_ref, dv_ref, qf, dof, lf, df):
        hp = pl.program_id(0)
        n = pl.program_id(1)
        nbs = nbs_ref[B_HPG * g + 2 * hp]
        qf[0:DIL_ROWS, :] = q_ref[...]
        qf[DIL_ROWS:, :] = qn_ref[...]
        dof[0:DIL_ROWS, :] = do_ref[...]
        dof[DIL_ROWS:, :] = don_ref[...]
        lf[:, :, 0:DIL_ROWS] = l_ref[...]
        lf[:, :, DIL_ROWS:] = ln_ref[...]
        df[:, :, 0:DIL_ROWS] = d_ref[...]
        df[:, :, DIL_ROWS:] = dn_ref[...]
        biases = [_dil_bias(coef_ref[B_HPG * g + 2 * hp + h], True) for h in range(2)]
        col = biases[0][1]
        lo = _first_head((W, LANES))
        lo2 = _first_head((2 * W, LANES))
        for b in range(DIL_SUB):
            no_next = lax.rem(n * DIL_SUB + b + 1, nbs) == 0
            rows = slice(b * W, (b + 1) * W)
            both = slice(b * W, (b + 2) * W)
            dd = dof[both, :]
            dk = dv = None
            for h, (kh, vh, qh, ddh) in enumerate(zip(_each_head(k_ref[rows, :], lo), _each_head(v_ref[rows, :], lo),
                                                      _each_head(qf[both, :], lo2), _each_head(dd, lo2))):
                st = lax.dot_general(kh, qh, _DN["nt"], preferred_element_type=F32) * SCALE + biases[h][0]
                st = jnp.where(jnp.logical_and(no_next, col >= W), NEG_INF, st)
                pt = jnp.exp(st - lf[h, :, both])
                d = lax.dot_general(pt.astype(BF16), ddh, _DN["nn"], preferred_element_type=F32)
                dv = d if dv is None else dv + d
                dpt = lax.dot_general(vh, dd, _DN["nt"], preferred_element_type=F32)
                dst = (pt * (dpt - df[h, :, both]) * SCALE).astype(BF16)
                d = lax.dot_general(dst, qh, _DN["nn"], preferred_element_type=F32)
                dk = d if dk is None else dk + d
            dk_ref[rows, :] = dk.astype(BF16)
            dv_ref[rows, :] = dv.astype(BF16)

    sp = _dil_pair_specs(qoff, koff, voff)
    wide = jax.ShapeDtypeStruct((S, GROUP_W), BF16)
    return pl.pallas_call(
        body, out_shape=[wide, wide], grid=(NPG, DIL_STEPS),
        in_specs=[sp["smem"], sp["smem"], sp["k"], sp["v"], sp["q"], sp["q_next"], sp["o"], sp["o_next"], sp["row"],
                  sp["row_next"], sp["row2"], sp["row2_next"]],
        out_specs=[sp["o"], sp["o"]],
        scratch_shapes=[pltpu.VMEM((DIL_ROWS + W, LANES), BF16)] * 2 + [pltpu.VMEM((2, 1, DIL_ROWS + W), F32)] * 2,
        compiler_params=_cparams(("parallel", "parallel")), name=name,
    )(coef_t, nbs_t, k, v, q, q, do, do, stats, stats, stats, stats)


FFN_ROWS = 512
FFN_COLS = 256
HALO = 8


def _shifted(u, halo, back):
    T = u.shape[0]
    rows = lax.broadcasted_iota(jnp.int32, u.shape, 0)
    if back:
        s1 = jnp.where(rows == 0, halo[HALO - 1:HALO, :], pltpu.roll(u, 1, 0))
        s2 = jnp.where(rows == 0, halo[HALO - 2:HALO - 1, :],
                       jnp.where(rows == 1, halo[HALO - 1:HALO, :], pltpu.roll(u, 2, 0)))
    else:
        s1 = jnp.where(rows == T - 1, halo[0:1, :], pltpu.roll(u, T - 1, 0))
        s2 = jnp.where(rows == T - 1, halo[1:2, :],
                       jnp.where(rows == T - 2, halo[0:1, :], pltpu.roll(u, T - 2, 0)))
    return s1, s2


def _conv_parts(u_ref, h_ref, w_ref, b_ref, first):
    out = []
    for p in range(2):
        u = u_ref[p]
        halo = jnp.where(first, 0.0, h_ref[p])
        u1, u2 = _shifted(u, halo, True)
        w = w_ref[p]
        out.append((w[0:1, :] * u2 + w[1:2, :] * u1 + w[2:3, :] * u + b_ref[p], u1, u2, u))
    return out


def _ffn_specs():
    T, C = FFN_ROWS, FFN_COLS
    blk = pl.BlockSpec((2, T, C), lambda j, i: (0, i, j))
    prev = pl.BlockSpec((2, HALO, C), lambda j, i: (0, jnp.maximum(i * (T // HALO) - 1, 0), j))
    nxt = pl.BlockSpec((2, HALO, C), lambda j, i: (0, jnp.minimum((i + 1) * (T // HALO), S // HALO - 1), j))
    wsp = pl.BlockSpec((2, 3, C), lambda j, i: (0, 0, j))
    bsp = pl.BlockSpec((2, 1, C), lambda j, i: (0, 0, j))
    one = pl.BlockSpec((T, C), lambda j, i: (i, j))
    return blk, prev, nxt, wsp, bsp, one


def _ffn_act_fwd(u, w, b, name):
    blk, prev, _, wsp, bsp, one = _ffn_specs()

    def body(u_ref, h_ref, w_ref, b_ref, o_ref):
        (a, _, _, _), (g, _, _, _) = _conv_parts(u_ref, h_ref, w_ref, b_ref, pl.program_id(1) == 0)
        o_ref[...] = (g / (1.0 + jnp.exp(-g)) * a).astype(BF16)

    return pl.pallas_call(
        body, out_shape=jax.ShapeDtypeStruct((S, FF), BF16), grid=(FF // FFN_COLS, S // FFN_ROWS),
        in_specs=[blk, prev, wsp, bsp], out_specs=one,
        compiler_params=_cparams(("parallel", "parallel")), name=name,
    )(u, u, w, b)


def _ffn_act_bwd(u, dact, w, b, name):
    blk, prev, _, wsp, bsp, one = _ffn_specs()

    def body(u_ref, h_ref, da_ref, w_ref, b_ref, duc_ref, dwb_ref):
        i = pl.program_id(1)
        (a, a1, a2, a0), (g, g1, g2, g0) = _conv_parts(u_ref, h_ref, w_ref, b_ref, i == 0)
        dact_v = da_ref[...]
        sg = 1.0 / (1.0 + jnp.exp(-g))
        d_a = dact_v * (g * sg)
        d_g = dact_v * a * (sg * (1.0 + g * (1.0 - sg)))
        duc_ref[0] = d_a
        duc_ref[1] = d_g

        @pl.when(i == 0)
        def _():
            dwb_ref[...] = jnp.zeros(dwb_ref.shape, F32)

        for p, (d, s2, s1, s0) in enumerate(((d_a, a2, a1, a0), (d_g, g2, g1, g0))):
            dwb_ref[p, 0:1, :] += jnp.sum(d * s2, axis=0, keepdims=True)
            dwb_ref[p, 1:2, :] += jnp.sum(d * s1, axis=0, keepdims=True)
            dwb_ref[p, 2:3, :] += jnp.sum(d * s0, axis=0, keepdims=True)
            dwb_ref[p, 3:4, :] += jnp.sum(d, axis=0, keepdims=True)

    return pl.pallas_call(
        body, out_shape=[jax.ShapeDtypeStruct((2, S, FF), F32), jax.ShapeDtypeStruct((2, 8, FF), F32)],
        grid=(FF // FFN_COLS, S // FFN_ROWS), in_specs=[blk, prev, one, wsp, bsp],
        out_specs=[blk, pl.BlockSpec((2, 8, FFN_COLS), lambda j, i: (0, 0, j))],
        compiler_params=_cparams(("parallel", "arbitrary")), name=name,
    )(u, u, dact, w, b)


def _ffn_conv_bwd(duc, w, name):
    blk, _, nxt, wsp, _, _ = _ffn_specs()
    last = S // FFN_ROWS - 1

    def body(d_ref, h_ref, w_ref, du_ref):
        is_last = pl.program_id(1) == last
        for p in range(2):
            d = d_ref[p]
            halo = jnp.where(is_last, 0.0, h_ref[p])
            d1, d2 = _shifted(d, halo, False)
            wv = w_ref[p]
            du_ref[p] = (wv[2:3, :] * d + wv[1:2, :] * d1 + wv[0:1, :] * d2).astype(BF16)

    return pl.pallas_call(
        body, out_shape=jax.ShapeDtypeStruct((2, S, FF), BF16), grid=(FF // FFN_COLS, S // FFN_ROWS),
        in_specs=[blk, nxt, wsp], out_specs=blk,
        compiler_params=_cparams(("parallel", "parallel")), name=name,
    )(duc, duc, w)


def _adam_update(w, gv, m, v):
    c1 = 1.0 / (1.0 - ADAM_B1 ** ADAM_STEP)
    c2 = 1.0 / (1.0 - ADAM_B2 ** ADAM_STEP)
    mn = ADAM_B1 * m + (1.0 - ADAM_B1) * gv
    vn = ADAM_B2 * v + (1.0 - ADAM_B2) * (gv * gv)
    return -ADAM_LR * ((mn * c1) / (jnp.sqrt(vn * c2) + ADAM_EPS) + ADAM_WD * w), mn, vn


def _adamw(w, g, m, v, name):
    rows = w.shape[0]
    T = 8
    for cand in (256, 128, 64, 32, 16, 8):
        if rows % cand == 0:
            T = cand
            break

    def body(w_ref, g_ref, m_ref, v_ref, d_ref, mo_ref, vo_ref):
        d_ref[...], mo_ref[...], vo_ref[...] = _adam_update(w_ref[...], g_ref[...], m_ref[...], v_ref[...])

    blk = pl.BlockSpec((T, w.shape[1]), lambda i: (i, 0))
    sds = jax.ShapeDtypeStruct(w.shape, F32)
    return pl.pallas_call(
        body, out_shape=[sds, sds, sds], grid=(rows // T,), in_specs=[blk] * 4, out_specs=[blk] * 3,
        compiler_params=_cparams(("parallel",)), name=name,
    )(w, g, m, v)


ANY = pl.BlockSpec(memory_space=pl.ANY)


def _place():
    x, y, c = lax.axis_index("x"), lax.axis_index("y"), lax.axis_index("c")
    chips = [(1 - x, y), (x, 1 - y), (1 - x, 1 - y)]
    return x, y, c, chips


def _place_own(w, slot_arr, name):
    rows = w.shape[0]
    T = 16
    for cand in (2048, 1024, 512, 256, 128, 64, 32, 16):
        if rows % cand == 0:
            T = cand
            break

    def body(k_ref, w_ref, o_ref):
        o_ref[...] = w_ref[...]

    return pl.pallas_call(
        body, out_shape=jax.ShapeDtypeStruct((N_CHIPS, rows, FLAT_W), w.dtype),
        grid_spec=pltpu.PrefetchScalarGridSpec(
            num_scalar_prefetch=1, grid=(rows // T,),
            in_specs=[pl.BlockSpec((T, FLAT_W), lambda i, k: (i, 0))],
            out_specs=pl.BlockSpec((None, T, FLAT_W), lambda i, k: (k[0], i, 0))),
        compiler_params=_cparams(("parallel",)), name=name,
    )(slot_arr, w)


def _allgather_shards(w, buf):
    half_rows = w.shape[0] // 2
    assert half_rows % 16 == 0

    def body(w_ref, buf_ref, g_ref, send_sems, recv_sems):
        x, y, c, chips = _place()
        myk = 2 * x + y
        sibling = (x, y, 1 - c)
        h0 = pl.multiple_of(c * half_rows, 16)
        h1 = pl.multiple_of((1 - c) * half_rows, 16)

        def half(k, start):
            return g_ref.at[k, pl.ds(start, half_rows), :]

        def rcopy(sem, src, dst, to):
            return pltpu.make_async_remote_copy(src_ref=src, dst_ref=dst, send_sem=send_sems.at[sem],
                                                recv_sem=recv_sems.at[sem], device_id=to, device_id_type=MESH)

        ici = [rcopy(r, w_ref.at[pl.ds(h0, half_rows), :], half(myk, h0), (*chip, c)) for r, chip in enumerate(chips)]
        for cp in ici:
            cp.start()
        ks = [2 * cx + cy for cx, cy in chips]
        fwd = [rcopy(3 + r, half(ks[r], h0), half(ks[r], h0), sibling) for r in range(3)]
        for r in range(3):
            rcopy(r, half(ks[r], h0), half(ks[r], h0), (*chips[r], c)).wait_recv()
            fwd[r].start()
        for r in range(3):
            rcopy(3 + r, half(ks[r], h1), half(ks[r], h1), sibling).wait_recv()
        for cp in ici + fwd:
            cp.wait_send()

    return pl.pallas_call(
        body, out_shape=jax.ShapeDtypeStruct(buf.shape, w.dtype), in_specs=[ANY, ANY], out_specs=ANY,
        scratch_shapes=[pltpu.SemaphoreType.DMA((6,)), pltpu.SemaphoreType.DMA((6,))],
        input_output_aliases={1: 0},
        compiler_params=pltpu.CompilerParams(has_side_effects=True), name="allgather_shards",
    )(w, buf)


HBM_SPEC = pl.BlockSpec(memory_space=pltpu.HBM)
SEM_SPEC = pl.BlockSpec(memory_space=pltpu.SEMAPHORE)
DATAFLOW = pltpu.SideEffectType.DATAFLOW_SIDE_EFFECTING
OWN_SLOT = 3


def _late_gather_start(w, land):
    def body(w_ref, land_ref, send_sems, recv_sems, w_thru, land_thru, token):
        x, y, c, chips = _place()
        for r, chip in enumerate(chips):
            pltpu.make_async_remote_copy(src_ref=w_ref, dst_ref=land_ref.at[r], send_sem=send_sems.at[r],
                                         recv_sem=recv_sems.at[r], device_id=(*chip, c), device_id_type=MESH).start()
        token[...] = jnp.zeros_like(token)

    return pl.pallas_call(
        body, name="late_gather_start",
        out_shape=(pltpu.SemaphoreType.DMA((3,)), pltpu.SemaphoreType.DMA((3,)), pltpu.HBM(w.shape, w.dtype),
                   pltpu.HBM(land.shape, land.dtype), jax.ShapeDtypeStruct((8, LANES), F32)),
        in_specs=(HBM_SPEC, HBM_SPEC),
        out_specs=(SEM_SPEC, SEM_SPEC, HBM_SPEC, HBM_SPEC, pl.BlockSpec(memory_space=pltpu.VMEM)),
        input_output_aliases={0: 2, 1: 3}, compiler_params=pltpu.CompilerParams(has_side_effects=DATAFLOW),
    )(pltpu.with_memory_space_constraint(w, pltpu.HBM), pltpu.with_memory_space_constraint(land, pltpu.HBM))


def _late_gather_wait(send_sems, recv_sems, w_thru, land_thru, after):
    def body(w_ref, land_ref, send_sems, recv_sems, after_ref, w_dead, got_ref):
        x, y, c, chips = _place()
        for r, chip in enumerate(chips):
            cp = pltpu.make_async_remote_copy(src_ref=w_ref, dst_ref=land_ref.at[r], send_sem=send_sems.at[r],
                                              recv_sem=recv_sems.at[r], device_id=(*chip, c), device_id_type=MESH)
            cp.wait_send()
            cp.wait_recv()

    return pl.pallas_call(
        body, name="late_gather_wait",
        out_shape=(pltpu.HBM(w_thru.shape, w_thru.dtype), pltpu.HBM(land_thru.shape, land_thru.dtype)),
        in_specs=(HBM_SPEC, HBM_SPEC, SEM_SPEC, SEM_SPEC, pl.BlockSpec(memory_space=pl.ANY)),
        out_specs=(HBM_SPEC, HBM_SPEC), input_output_aliases={0: 0, 1: 1},
        compiler_params=pltpu.CompilerParams(has_side_effects=DATAFLOW),
    )(w_thru, land_thru, send_sems, recv_sems, after)


def _flat_tile(rows):
    return next(t for t in (2048, 1024, 512, 256, 128, 64, 32, 16) if rows % t == 0)


def _sibling_swap_half(g, tag):
    half = g.shape[1] // 2

    def body(g_ref, o_ref, send_sem, recv_sem):
        x, y, c, _ = _place()
        theirs = pl.multiple_of((1 - c) * half, 8)
        cp = pltpu.make_async_remote_copy(src_ref=g_ref.at[:, pl.ds(theirs, half), :], dst_ref=o_ref,
                                          send_sem=send_sem, recv_sem=recv_sem, device_id=(x, y, 1 - c),
                                          device_id_type=MESH)
        cp.start()
        cp.wait()

    return pl.pallas_call(
        body, out_shape=jax.ShapeDtypeStruct((N_CHIPS, half, FLAT_W), F32), in_specs=[ANY], out_specs=ANY,
        scratch_shapes=[pltpu.SemaphoreType.DMA, pltpu.SemaphoreType.DMA],
        compiler_params=pltpu.CompilerParams(has_side_effects=True), name=f"rs_sibling_swap_{tag}",
    )(g)


def _pair_sum(g, other, c_arr, tag):
    half = other.shape[1]
    T = _flat_tile(half)

    def body(c_ref, g_ref, o_ref, s_ref):
        s_ref[...] = (g_ref[...] + o_ref[...]).astype(BF16)

    nb = half // T
    return pl.pallas_call(
        body, out_shape=jax.ShapeDtypeStruct((N_CHIPS, half, FLAT_W), BF16),
        grid_spec=pltpu.PrefetchScalarGridSpec(
            num_scalar_prefetch=1, grid=(N_CHIPS, nb),
            in_specs=[pl.BlockSpec((None, T, FLAT_W), lambda k, i, c: (k, c[0] * nb + i, 0)),
                      pl.BlockSpec((None, T, FLAT_W), lambda k, i, c: (k, i, 0))],
            out_specs=pl.BlockSpec((None, T, FLAT_W), lambda k, i, c: (k, i, 0))),
        compiler_params=_cparams(("parallel", "parallel")), name=f"rs_pair_sum_{tag}",
    )(c_arr, g, other)


def _chip_exchange(s, tag):
    def body(s_ref, o_ref, send_sems, recv_sems):
        x, y, c, chips = _place()
        cps = []
        for r, (cx, cy) in enumerate(chips):
            cps.append(pltpu.make_async_remote_copy(
                src_ref=s_ref.at[2 * cx + cy], dst_ref=o_ref.at[r], send_sem=send_sems.at[r],
                recv_sem=recv_sems.at[r], device_id=(cx, cy, c), device_id_type=MESH))
        for cp in cps:
            cp.start()
        for cp in cps:
            cp.wait()

    return pl.pallas_call(
        body, out_shape=jax.ShapeDtypeStruct((3,) + s.shape[1:], BF16), in_specs=[ANY], out_specs=ANY,
        scratch_shapes=[pltpu.SemaphoreType.DMA((3,)), pltpu.SemaphoreType.DMA((3,))],
        compiler_params=pltpu.CompilerParams(has_side_effects=True), name=f"rs_chip_exchange_{tag}",
    )(s)


def _chip_exchange_start(s, land):
    def body(s_ref, land_ref, send_sems, recv_sems, s_thru, land_thru, token):
        x, y, c, chips = _place()
        for r, (cx, cy) in enumerate(chips):
            pltpu.make_async_remote_copy(src_ref=s_ref.at[2 * cx + cy], dst_ref=land_ref.at[r], send_sem=send_sems.at[r],
                                         recv_sem=recv_sems.at[r], device_id=(cx, cy, c), device_id_type=MESH).start()
        token[...] = jnp.zeros_like(token)

    return pl.pallas_call(
        body, name="rs_exchange_start",
        out_shape=(pltpu.SemaphoreType.DMA((3,)), pltpu.SemaphoreType.DMA((3,)), pltpu.HBM(s.shape, s.dtype),
                   pltpu.HBM(land.shape, land.dtype), jax.ShapeDtypeStruct((8, LANES), F32)),
        in_specs=(HBM_SPEC, HBM_SPEC),
        out_specs=(SEM_SPEC, SEM_SPEC, HBM_SPEC, HBM_SPEC, pl.BlockSpec(memory_space=pltpu.VMEM)),
        input_output_aliases={0: 2, 1: 3}, compiler_params=pltpu.CompilerParams(has_side_effects=DATAFLOW),
    )(pltpu.with_memory_space_constraint(s, pltpu.HBM), pltpu.with_memory_space_constraint(land, pltpu.HBM))


def _chip_exchange_wait(send_sems, recv_sems, s_thru, land_thru, after):
    def body(s_ref, land_ref, send_sems, recv_sems, after_ref, s_done, got_ref):
        x, y, c, chips = _place()
        for r, (cx, cy) in enumerate(chips):
            cp = pltpu.make_async_remote_copy(src_ref=s_ref.at[2 * cx + cy], dst_ref=land_ref.at[r],
                                              send_sem=send_sems.at[r], recv_sem=recv_sems.at[r], device_id=(cx, cy, c),
                                              device_id_type=MESH)
            cp.wait_send()
            cp.wait_recv()

    return pl.pallas_call(
        body, name="rs_exchange_wait",
        out_shape=(pltpu.HBM(s_thru.shape, s_thru.dtype), pltpu.HBM(land_thru.shape, land_thru.dtype)),
        in_specs=(HBM_SPEC, HBM_SPEC, SEM_SPEC, SEM_SPEC, pl.BlockSpec(memory_space=pl.ANY)),
        out_specs=(HBM_SPEC, HBM_SPEC), input_output_aliases={0: 0, 1: 1},
        compiler_params=pltpu.CompilerParams(has_side_effects=DATAFLOW),
    )(s_thru, land_thru, send_sems, recv_sems, after)


def _chip_sum(s, r, k_arr, tag):
    half = s.shape[1]
    T = _flat_tile(half)

    def body(k_ref, s_ref, r_ref, o_ref):
        o_ref[...] = ((s_ref[...].astype(F32) + r_ref[0].astype(F32)) + r_ref[1].astype(F32)) + r_ref[2].astype(F32)

    return pl.pallas_call(
        body, out_shape=jax.ShapeDtypeStruct((half, FLAT_W), F32),
        grid_spec=pltpu.PrefetchScalarGridSpec(
            num_scalar_prefetch=1, grid=(half // T,),
            in_specs=[pl.BlockSpec((None, T, FLAT_W), lambda i, k: (k[0], i, 0)),
                      pl.BlockSpec((3, T, FLAT_W), lambda i, k: (0, i, 0))],
            out_specs=pl.BlockSpec((T, FLAT_W), lambda i, k: (i, 0))),
        compiler_params=_cparams(("parallel",)), name=f"rs_chip_sum_{tag}",
    )(k_arr, s, r)


def _sibling_send(t, tag):
    def body(t_ref, o_ref, send_sem, recv_sem):
        x, y, c, _ = _place()
        cp = pltpu.make_async_remote_copy(src_ref=t_ref, dst_ref=o_ref, send_sem=send_sem, recv_sem=recv_sem,
                                          device_id=(x, y, 1 - c), device_id_type=MESH)
        cp.start()
        cp.wait()

    return pl.pallas_call(
        body, out_shape=jax.ShapeDtypeStruct(t.shape, F32), in_specs=[ANY], out_specs=ANY,
        scratch_shapes=[pltpu.SemaphoreType.DMA, pltpu.SemaphoreType.DMA],
        compiler_params=pltpu.CompilerParams(has_side_effects=True), name=f"rs_sibling_send_{tag}",
    )(t)


def _allreduce_small(v):
    def body(v_ref, o_ref, buf, send_sems, recv_sems):
        x, y, c, _ = _place()
        me = 4 * x + 2 * y + c
        buf[me] = v_ref[...]
        cps = []
        for mask in range(1, 8):
            a, b, d = (mask >> 2) & 1, (mask >> 1) & 1, mask & 1
            peer = (x + a - 2 * a * x, y + b - 2 * b * y, c + d - 2 * d * c)
            cps.append(pltpu.make_async_remote_copy(
                src_ref=v_ref, dst_ref=buf.at[me], send_sem=send_sems.at[mask - 1], recv_sem=recv_sems.at[mask - 1],
                device_id=peer, device_id_type=MESH))
        for cp in cps:
            cp.start()
        for cp in cps:
            cp.wait()
        total = buf[0]
        for dev in range(1, 8):
            total = total + buf[dev]
        o_ref[...] = total

    vm = pl.BlockSpec(memory_space=pltpu.VMEM)
    return pl.pallas_call(
        body, out_shape=jax.ShapeDtypeStruct((SMALL_ROWS, 1024), F32), in_specs=[vm], out_specs=vm,
        scratch_shapes=[pltpu.VMEM((8, SMALL_ROWS, 1024), F32), pltpu.SemaphoreType.DMA((7,)),
                        pltpu.SemaphoreType.DMA((7,))],
        compiler_params=pltpu.CompilerParams(has_side_effects=True), name="allreduce_small",
    )(v)


def _col_to_row(t):
    return t.reshape(t.shape[0], 1, S)


def _residue_rows(t, d, inverse=False):
    if d == 1:
        return t
    shape = (d, S // d) if inverse else (S // d, d)
    return t.reshape(shape + t.shape[1:]).transpose(1, 0, 2).reshape(t.shape)


def _residue_vecs(t, d, inverse=False):
    if d == 1:
        return t
    shape = (d, S // d) if inverse else (S // d, d)
    return t.reshape((t.shape[0],) + shape).transpose(0, 2, 1).reshape(t.shape)


def _ffn_fwd(x, g, w_up, cw, cb, w_down, tag):
    (h,) = _rms_fwd(x, [g], f"{tag}_norm")
    u = _mm(h, w_up, mode="nn", tm=1024, tn=1408, tk=1024, o_split=2, name=f"{tag}_up")
    act = _ffn_act_fwd(u, cw, cb, f"{tag}_act")
    x_out = _mm(act, w_down, mode="nn", tm=1024, tn=512, tk=FF, res=x, name=f"{tag}_down")
    return x_out, (h, u, act)


def _ffn_bwd(x, g, w_up, cw, cb, w_down, saved, dx, dxb, tag):
    h, u, act = saved
    d_w_down = _mm(act, dxb, mode="tn", tm=1408, tn=512, tk=1024, name=f"{tag}_dwdown")
    dact = _mm(dxb, w_down, mode="nt", tm=1024, tn=1408, tk=1024, name=f"{tag}_dact")
    duc, dwb = _ffn_act_bwd(u, dact, cw, cb, f"{tag}_dgate")
    du = _ffn_conv_bwd(duc, cw, f"{tag}_dconv")
    d_w_up = _mm(h, du, mode="tn", tm=1024, tn=1408, tk=1024, b_split=2, name=f"{tag}_dwup")
    dh = _mm(du, w_up, mode="nt", tm=1024, tn=512, tk=1408, a_split=2, name=f"{tag}_dh")
    dx_new, dxb_new, (dg,) = _rms_bwd(x, dx, [(g, dh)], f"{tag}_dnorm")
    d_cw = dwb[:, 0:3, :].transpose(1, 0, 2).reshape(3, 2 * FF)
    d_cb = dwb[:, 3, :].reshape(2 * FF)
    return dx_new, dxb_new, dict(w_up=d_w_up, w_down=d_w_down, conv_w=d_cw, conv_b=d_cb, norm_g=dg.reshape(D))


def _local_step(x, target, p, late_weights, late_grads_ready):
    g = {}
    (h1,) = _rms_fwd(x, [p["mix_norm_g"][0]], "a_norm")
    w_qkv = p["a_w_in"][:, :QKV_W]
    w_f = jnp.pad(p["a_w_in"][:, QKV_W:], ((0, 0), (0, LANES - A_HEADS)))
    b_f = jnp.pad(p["a_b_f"].reshape(1, A_HEADS), ((0, 0), (0, LANES - A_HEADS)))
    qkv = _mm(h1, w_qkv, mode="nn", tm=1024, tn=512, tk=1024, out_dtype=BF16, name="a_qkv")
    pf = _mm(h1, w_f, mode="nn", tm=1024, tn=LANES, tk=1024, name="a_gate")
    aug_q, aug_k = _fgate_fwd(pf, b_f, "a_gate_scan")
    oa2, lse_a = _fox_pair_fwd(qkv, aug_q, aug_k, "a_attn")
    x1 = _mm(oa2, p["a_w_out"], mode="nn", tm=1024, tn=512, tk=1024, res=x, name="a_out")
    p = {**p, **late_weights(x1)}
    x2, ffn0 = _ffn_fwd(x1, p["ffn_norm_g"][0], p["ffn_w_up"][0], p["conv_w"][0], p["conv_b"][0], p["ffn_w_down"][0], "f0")
    hk, h3 = _rms_fwd(x2, [p["kv_norm_g"], p["mix_norm_g"][1]], "kv_b_norm")
    kvb = _mm(hk, p["w_kv"], mode="nn", tm=1024, tn=512, tk=1024, out_dtype=BF16, name="kv_proj")
    qb = _mm(h3, p["b_w_q"], mode="nn", tm=1024, tn=512, tk=1024, out_dtype=BF16, name="b_q")
    dil_in = []
    for gi, (_, d) in enumerate(B_GROUPS):
        if d == 1:
            dil_in.append((qb, kvb, kvb, gi * NPG, gi * NPG, (3 + gi) * NPG))
        else:
            qg = _residue_rows(qb[:, gi * GROUP_W:(gi + 1) * GROUP_W], d)
            kvg = _residue_rows(kvb.reshape(S, 2, 3, GROUP_W)[:, :, gi, :].reshape(S, 2 * GROUP_W), d)
            dil_in.append((qg, kvg, kvg, 0, 0, NPG))
    o_g, lse_g = [], []
    for gi, (_, d) in enumerate(B_GROUPS):
        qg, kg, vg, qoff, koff, voff = dil_in[gi]
        og, lg = _dil_pair_fwd(gi, qg, kg, vg, qoff, koff, voff, f"b_attn{gi}")
        o_g.append(_residue_rows(og, d, inverse=True))
        lse_g.append(_residue_vecs(lg, d, inverse=True))
    ob, ob2, lse_b = _dil_pair_merge(o_g, lse_g, "b_merge")
    x3 = _mm(ob2, p["b_w_out"], mode="nn", tm=1024, tn=512, tk=B_OUT_W, res=x2, name="b_out")
    x4, ffn1 = _ffn_fwd(x3, p["ffn_norm_g"][1], p["ffn_w_up"][1], p["conv_w"][1], p["conv_b"][1], p["ffn_w_down"][1], "f1")
    loss, dx, dxb, dg_final = _loss_head(x4, p["final_norm_g"], target, "loss_head")
    g["final_norm_g"] = dg_final.reshape(D)

    dx, dxb, gf1 = _ffn_bwd(x3, p["ffn_norm_g"][1], p["ffn_w_up"][1], p["conv_w"][1], p["conv_b"][1], p["ffn_w_down"][1],
                            ffn1, dx, dxb, "f1")
    g["b_w_out"] = _mm(ob2, dxb, mode="tn", tm=B_OUT_W, tn=512, tk=1024, name="b_dwout")
    dob = _mm(dxb, p["b_w_out"], mode="nt", tm=1024, tn=B_OUT_W, tk=1024, name="b_do")
    delta_b = _pair_rowdot(dob, ob, "b_delta")
    dob16 = dob.astype(BF16)
    stats_b = jnp.concatenate([lse_b, delta_b], axis=0)
    dq_g, dk_g, dv_g = [], [], []
    for gi, (_, d) in enumerate(B_GROUPS):
        qg, kg, vg, qoff, koff, voff = dil_in[gi]
        dog, stats_d = _residue_rows(dob16, d), _residue_vecs(stats_b, d)
        dqd = _dil_pair_dq(gi, qg, kg, vg, qoff, koff, voff, dog, stats_d, f"b_dq{gi}")
        dkd, dvd = _dil_pair_dkv(gi, qg, kg, vg, qoff, koff, voff, dog, _col_to_row(stats_d), f"b_dkv{gi}")
        dq_g.append(_residue_rows(dqd, d, inverse=True))
        dk_g.append(_residue_rows(dkd, d, inverse=True))
        dv_g.append(_residue_rows(dvd, d, inverse=True))
    dqb = jnp.concatenate(dq_g, axis=1)
    dkvb = jnp.concatenate(dk_g + dv_g, axis=1)
    g["b_w_q"] = _mm(h3, dqb, mode="tn", tm=1024, tn=512, tk=1024, name="b_dwq")
    dh3 = _mm(dqb, p["b_w_q"], mode="nt", tm=1024, tn=512, tk=B_Q_W, name="b_dh")
    g["w_kv"] = _mm(hk, dkvb, mode="tn", tm=1024, tn=512, tk=1024, name="kv_dw")
    dhk = _mm(dkvb, p["w_kv"], mode="nt", tm=1024, tn=512, tk=1536, name="kv_dh")
    dx, dxb, (dg_mix1, dg_kv) = _rms_bwd(x2, dx, [(p["mix_norm_g"][1], dh3), (p["kv_norm_g"], dhk)], "b_dnorm")
    g["kv_norm_g"] = dg_kv.reshape(D)
    dx, dxb, gf0 = _ffn_bwd(x1, p["ffn_norm_g"][0], p["ffn_w_up"][0], p["conv_w"][0], p["conv_b"][0], p["ffn_w_down"][0],
                            ffn0, dx, dxb, "f0")
    g["ffn_w_up"] = [gf0["w_up"], gf1["w_up"]]
    g["ffn_w_down"] = [gf0["w_down"], gf1["w_down"]]
    g["ffn_conv_w"] = jnp.stack([gf0["conv_w"], gf1["conv_w"]])
    token = late_grads_ready(g)
    a_w_out_t = p["a_w_out"] + token[0, 0].astype(BF16)
    g["a_w_out"] = _mm(oa2, dxb, mode="tn", tm=1024, tn=512, tk=1024, name="a_dwout")
    doa = _mm(dxb, a_w_out_t, mode="nt", tm=1024, tn=512, tk=1024, name="a_do")
    delta_a = _pair_rowdot(doa, oa2, "a_delta")
    dqa, dka, dva, dck, dcq = _fox_pair_bwd(qkv, doa, _col_to_row(lse_a), _col_to_row(delta_a), aug_q, aug_k, "a_dattn")
    dqkv = jnp.concatenate([dqa, dka, dva], axis=1)
    pad_heads = lambda t: jnp.pad(t.reshape(A_HEADS, S).T, ((0, 0), (0, LANES - A_HEADS)))
    dpf, db_f = _fgate_bwd(pf, b_f, pad_heads(dck), pad_heads(dcq), "a_dgate_scan")
    g["a_b_f"] = db_f[:, :A_HEADS]
    d_w_qkv = _mm(h1, dqkv, mode="tn", tm=1024, tn=512, tk=1024, name="a_dwqkv")
    d_w_f = _mm(h1, dpf, mode="tn", tm=1024, tn=LANES, tk=1024, name="a_dwgate")
    g["a_w_in"] = jnp.concatenate([d_w_qkv, d_w_f[:, :A_HEADS]], axis=1)
    dh1 = _mm(dqkv, w_qkv, mode="nt", tm=1024, tn=512, tk=1536, name="a_dh")
    dh1 = _mm(dpf, w_f, mode="nt", tm=1024, tn=512, tk=LANES, res=dh1, name="a_dh_gate")
    dx, _, (dg_mix0,) = _rms_bwd(x, dx, [(p["mix_norm_g"][0], dh1)], "a_dnorm")

    g["mix_norm_g"] = jnp.stack([dg_mix0.reshape(D), dg_mix1.reshape(D)])
    g["ffn_norm_g"] = jnp.stack([gf0["norm_g"], gf1["norm_g"]])
    g["ffn_conv_b"] = jnp.stack([gf0["conv_b"], gf1["conv_b"]])
    return loss[0, 0], dx, g


_SHARD_SHAPES = {"a_w_in": (1, 1024, 772), "a_w_out": (1, 256, 1024), "b_w_q": (1, 1024, 384), "b_w_out": (1, 512, 256),
                 "w_kv": (1024, 768), "ffn_w_up": (2, 1024, 1408), "ffn_w_down": (2, 704, 1024), "ffn_conv_w": (2, 3, 1408)}
_SMALL = (("kv_norm_g", (1024,)), ("mix_norm_g", (2, 1024)), ("ffn_norm_g", (2, 1024)), ("final_norm_g", (1024,)),
          ("a_b_f", (1, 16)), ("ffn_conv_b", (2, 5632)))


def _unslabs(rows, L, R, C, rpad):
    nc = -(-C // FLAT_W)
    return rows.reshape(L, nc, rpad, FLAT_W).transpose(0, 2, 1, 3).reshape(L, rpad, nc * FLAT_W)[:, :R, :C]


_SEG_RT = {"ffn_w_down": 704, "a_w_in": 1024, "a_w_out": 256, "b_w_q": 1024, "b_w_out": 512, "w_kv": 1024,
           "ffn_w_up": 1024, "ffn_conv_w": 16}
_ROW_SHARDED = ("a_w_out", "ffn_w_down")


_LAYOUTS = {"early": ("a_w_in", "a_w_out"), "late": ("ffn_w_down", "b_w_q", "b_w_out", "w_kv", "ffn_w_up", "ffn_conv_w"),
            "grad_early": ("a_w_in", "a_w_out"),
            "grad_late": ("ffn_w_up", "ffn_w_down", "b_w_q", "w_kv", "b_w_out", "ffn_conv_w")}
_GRAD_ROWS = {"grad_early": 10240, "grad_late": 45056}


def _layout_rows(layout):
    used = sum(_seg_rows(*s) for s in _SEGS if s[0] in _LAYOUTS[layout])
    rows = _GRAD_ROWS.get(layout, used)
    assert rows >= used
    return rows


def _grad_layout(name):
    return "grad_early" if name in _LAYOUTS["grad_early"] else "grad_late"


def _seg(name, layout=None):
    layout = layout or _grad_layout(name)
    off = 0
    for s in sorted((s for s in _SEGS if s[0] in _LAYOUTS[layout]), key=lambda s: _LAYOUTS[layout].index(s[0])):
        _, L, R, C, rpad = s
        if layout in _GRAD_ROWS:
            per_layer = -(-C // FLAT_W) * rpad
            off = -(-off // per_layer) * per_layer
        if s[0] == name:
            rt = _SEG_RT[name]
            assert off % rt == 0 and rpad % rt == 0
            half = _layout_rows(layout) // 2
            assert off + _seg_rows(*s) <= 2 * half
            assert layout not in _GRAD_ROWS or half % rt == 0 or off + _seg_rows(*s) <= half
            return dict(L=L, R=R, C=C, rpad=rpad, nc=-(-C // FLAT_W), rt=rt, off=off, ni=rpad // rt, half=half)
        off += _seg_rows(*s)
    raise KeyError(name)


def _flat_block(sg, term=0):
    base = (sg["off"] + term * sg["L"] * sg["nc"] * sg["rpad"]) // sg["rt"]
    return lambda l, j, i: base + (l * sg["nc"] + j) * sg["ni"] + i


def _native3(t, name):
    sg = _seg(name)
    t = t.reshape(sg["L"], sg["R"], sg["C"])
    return jnp.pad(t, ((0, 0), (0, sg["rpad"] - sg["R"]), (0, 0))) if sg["rpad"] != sg["R"] else t


def _slab_pack(flat, t, name, layout, term=None):
    sg = _seg(name, layout)
    rt = sg["rt"]
    rb = _flat_block(sg, term or 0)

    def body(*refs):
        t_ref, o_ref = refs[-2], refs[-1]
        val = t_ref[...]
        o_ref[...] = val.astype(BF16) if term is None else _split3(val)[term]

    in_specs = [pl.BlockSpec((None, rt, FLAT_W), lambda l, j, i: (l, i, j))]
    args = [t]
    if flat is not None:
        in_specs, args = [ANY] + in_specs, [flat] + args
    return pl.pallas_call(
        body, out_shape=jax.ShapeDtypeStruct((_layout_rows(layout), FLAT_W), BF16), grid=(sg["L"], sg["nc"], sg["ni"]),
        in_specs=in_specs, out_specs=pl.BlockSpec((rt, FLAT_W), lambda l, j, i: (rb(l, j, i), 0)),
        input_output_aliases={0: 0} if flat is not None else {},
        compiler_params=_cparams(("parallel", "parallel", "parallel")), name=f"pack_{name}_{term or 0}",
    )(*args)


def _full_spec(sg, name):
    rt, nc, ni = sg["rt"], sg["nc"], sg["ni"]
    if name in _ROW_SHARDED:
        return (sg["L"], N_CHIPS * sg["R"], sg["C"]), pl.BlockSpec((None, rt, FLAT_W), lambda k, l, j, i: (l, k * ni + i, j))
    return ((sg["L"], sg["rpad"], N_CHIPS * nc * FLAT_W),
            pl.BlockSpec((None, rt, FLAT_W), lambda k, l, j, i: (l, i, k * nc + j)))


def _slab_unpack(gathered, slots, name, layout, own=None):
    sg = _seg(name, layout)
    rb = _flat_block(sg)
    shape, _ = _full_spec(sg, name)
    rt, nc, ni = sg["rt"], sg["nc"], sg["ni"]
    width = nc * FLAT_W
    last = gathered.shape[0] - 1

    def body(*refs):
        s_ref, o_ref = refs[0], refs[-1]
        is_own = s_ref[pl.program_id(0)] == OWN_SLOT
        for j in range(nc):
            val = refs[1 + j][...]
            if own is not None:
                val = jnp.where(is_own, refs[1 + nc + j][...], val)
            o_ref[:, j * FLAT_W:(j + 1) * FLAT_W] = val

    if name in _ROW_SHARDED:
        o_spec = pl.BlockSpec((None, rt, width), lambda k, l, i, s: (l, k * ni + i, 0))
    else:
        o_spec = pl.BlockSpec((None, rt, width), lambda k, l, i, s: (l, i, k))
    in_specs = [pl.BlockSpec((None, rt, FLAT_W), lambda k, l, i, s, j=j: (jnp.minimum(s[k], last), rb(l, j, i), 0))
                for j in range(nc)]
    args = [gathered] * nc
    if own is not None:
        in_specs += [pl.BlockSpec((rt, FLAT_W), lambda k, l, i, s, j=j: (rb(l, j, i), 0)) for j in range(nc)]
        args += [own] * nc
    return pl.pallas_call(
        body, out_shape=jax.ShapeDtypeStruct(shape, BF16),
        grid_spec=pltpu.PrefetchScalarGridSpec(num_scalar_prefetch=1, grid=(N_CHIPS, sg["L"], ni), in_specs=in_specs,
                                               out_specs=o_spec),
        compiler_params=_cparams(("parallel",) * 3), name=f"unpack_{name}",
    )(slots, *args)


def _slab_pack_grad(flat4, g, name, layer=None):
    sg = _seg(name)
    rows = _layout_rows(_grad_layout(name))
    shape, _ = _full_spec(sg, name)
    n_layers = sg["L"] if layer is None else 1
    assert g.shape == (n_layers,) + shape[1:], (name, g.shape, shape)
    rt, nc = sg["rt"], sg["nc"]
    assert sg["ni"] == 1 and sg["off"] % (nc * rt) == 0
    base = sg["off"] // (nc * rt) + (layer or 0)

    def body(*refs):
        g_ref, o_ref = refs[-2], refs[-1]
        for j in range(nc):
            o_ref[j * rt:(j + 1) * rt, :] = g_ref[:, j * FLAT_W:(j + 1) * FLAT_W]

    if name in _ROW_SHARDED:
        spec = pl.BlockSpec((None, rt, nc * FLAT_W), lambda k, l: (l, k, 0))
    else:
        spec = pl.BlockSpec((None, rt, nc * FLAT_W), lambda k, l: (l, 0, k))
    in_specs, args = [spec], [g]
    if flat4 is not None:
        in_specs, args = [pl.BlockSpec(memory_space=pl.ANY)] + in_specs, [flat4] + args
    return pl.pallas_call(
        body, out_shape=jax.ShapeDtypeStruct((N_CHIPS, rows, FLAT_W), F32), grid=(N_CHIPS, n_layers),
        in_specs=in_specs, out_specs=pl.BlockSpec((None, nc * rt, FLAT_W), lambda k, l: (k, base + l, 0)),
        input_output_aliases={0: 0} if flat4 is not None else {},
        compiler_params=_cparams(("parallel",) * 2), name=f"packgrad_{name}_{layer or 0}",
    )(*args)


def _adamw_shard(w, m, v, g_mine, g_other, c_arr, name):
    sg = _seg(name)
    rt = sg["rt"]
    rb = _flat_block(sg)
    per_half = sg["half"] // rt

    def half_of(l, j, i):
        return (rb(l, j, i) * rt) // sg["half"]

    def body(c_ref, w_ref, m_ref, v_ref, gm_ref, go_ref, g_ref, d_ref, mo_ref, vo_ref):
        is_mine = half_of(pl.program_id(0), pl.program_id(1), pl.program_id(2)) == c_ref[0]
        gv = jnp.where(is_mine, gm_ref[...], go_ref[...])
        g_ref[...] = gv
        d_ref[...], mo_ref[...], vo_ref[...] = _adam_update(w_ref[...], gv, m_ref[...], v_ref[...])

    nat = pl.BlockSpec((None, rt, FLAT_W), lambda l, j, i, c: (l, i, j))
    half = pl.BlockSpec((rt, FLAT_W), lambda l, j, i, c: (rb(l, j, i) - half_of(l, j, i) * per_half, 0))
    sds = jax.ShapeDtypeStruct(w.shape, F32)
    return pl.pallas_call(
        body, out_shape=[sds] * 4,
        grid_spec=pltpu.PrefetchScalarGridSpec(num_scalar_prefetch=1, grid=(sg["L"], sg["nc"], sg["ni"]),
                                               in_specs=[nat, nat, nat, half, half], out_specs=[nat] * 4),
        compiler_params=_cparams(("parallel", "parallel", "parallel")), name=f"adamw_{name}",
    )(c_arr, w, m, v, g_mine, g_other)


def _pack_small(vals, loss=None):
    parts = [vals[name].astype(F32).reshape(-1) for name, _ in _SMALL]
    if loss is not None:
        parts.append(loss.reshape(1))
    flat = jnp.concatenate(parts)
    return jnp.pad(flat, (0, SMALL_ROWS * 1024 - flat.shape[0])).reshape(SMALL_ROWS, 1024)


def _unpack_small(flat):
    flat = flat.reshape(-1)
    out = {}
    o = 0
    for name, shape in _SMALL:
        n = int(np.prod(shape))
        out[name] = flat[o:o + n].reshape(shape)
        o += n
    return out, flat[o]


_BIG = ("a_w_in", "a_w_out", "b_w_q", "b_w_out", "w_kv", "ffn_w_up", "ffn_w_down", "ffn_conv_w")
A_IN_PAD = 896


def _pack_weights(w, layout):
    flat = None
    for name in _LAYOUTS[layout]:
        t = _native3(w[name], name)
        for term in ((0, 1, 2) if name == "ffn_conv_w" else (None,)):
            flat = _slab_pack(flat, t, name, layout, term)
    return flat


def _early_weights(gathered, slots):
    a_in = _slab_unpack(gathered, slots, "a_w_in", "early")
    a_in = a_in.reshape(D, N_CHIPS, A_IN_PAD)[:, :, :772].reshape(D, N_CHIPS * 772)
    return dict(a_w_in=a_in, a_w_out=_slab_unpack(gathered, slots, "a_w_out", "early")[0])


def _late_weights(landed, slots, own):
    full = {name: _slab_unpack(landed, slots, name, "late", own) for name in _LAYOUTS["late"] if name != "ffn_conv_w"}
    sg = _seg("ffn_conv_w", "late")
    n1 = sg["nc"] * sg["rpad"]
    conv = slice(sg["off"], sg["off"] + CONV_TERMS * n1)
    conv_rows = jnp.concatenate([landed[:, conv], own[None, conv]], axis=0)
    per_chip = []
    for k in range(N_CHIPS):
        rows = lax.dynamic_index_in_dim(conv_rows, slots[k], axis=0, keepdims=False)
        terms = [_unslabs(rows[i * n1:(i + 1) * n1], 1, sg["R"], sg["C"], sg["rpad"]).astype(F32) for i in range(CONV_TERMS)]
        per_chip.append((terms[0] + terms[1]) + terms[2])
    cw = jnp.concatenate(per_chip, axis=2).reshape(2, 3, 2, FF).transpose(0, 2, 1, 3)
    return dict(b_w_q=full["b_w_q"][0], b_w_out=full["b_w_out"][0], w_kv=full["w_kv"][0], ffn_w_up=full["ffn_w_up"],
                ffn_w_down=full["ffn_w_down"], conv_w=cw)


def _shard_grads(g, layout):
    def full(name):
        if name == "a_w_in":
            a_in = jnp.pad(g[name].reshape(D, N_CHIPS, 772), ((0, 0), (0, 0), (0, A_IN_PAD - 772)))
            return a_in.reshape(1, D, N_CHIPS * A_IN_PAD)
        if name == "ffn_conv_w":
            sgc = _seg(name)
            return jnp.pad(g[name].reshape(1, sgc["R"], 2 * FF), ((0, 0), (0, sgc["rpad"] - sgc["R"]), (0, 0)))
        return g[name] if g[name].ndim == 3 else g[name][None]

    flat4 = None
    for name in _LAYOUTS[layout]:
        if isinstance(g[name], (list, tuple)):
            for layer, t in enumerate(g[name]):
                flat4 = _slab_pack_grad(flat4, t[None], name, layer)
        else:
            flat4 = _slab_pack_grad(flat4, full(name), name)
    return flat4


_WEIGHTS = ["a_w_in", "a_b_f", "a_w_out", "b_w_q", "b_w_out", "kv_norm_g", "w_kv", "mix_norm_g", "ffn_norm_g", "ffn_w_up",
            "ffn_conv_w", "ffn_conv_b", "ffn_w_down", "final_norm_g"]


def kernel(x, a_w_in, a_b_f, a_w_out, b_w_q, b_w_out, kv_norm_g, w_kv, mix_norm_g, ffn_norm_g, ffn_w_up, ffn_conv_w, ffn_conv_b, ffn_w_down, final_norm_g, loss_target, m_a_w_in, m_a_b_f, m_a_w_out, m_b_w_q, m_b_w_out, m_kv_norm_g, m_w_kv, m_mix_norm_g, m_ffn_norm_g, m_ffn_w_up, m_ffn_conv_w, m_ffn_conv_b, m_ffn_w_down, m_final_norm_g, v_a_w_in, v_a_b_f, v_a_w_out, v_b_w_q, v_b_w_out, v_kv_norm_g, v_w_kv, v_mix_norm_g, v_ffn_norm_g, v_ffn_w_up, v_ffn_conv_w, v_ffn_conv_b, v_ffn_w_down, v_final_norm_g):
    w = dict(a_w_in=a_w_in, a_b_f=a_b_f, a_w_out=a_w_out, b_w_q=b_w_q, b_w_out=b_w_out, kv_norm_g=kv_norm_g, w_kv=w_kv,
             mix_norm_g=mix_norm_g, ffn_norm_g=ffn_norm_g, ffn_w_up=ffn_w_up, ffn_conv_w=ffn_conv_w, ffn_conv_b=ffn_conv_b,
             ffn_w_down=ffn_w_down, final_norm_g=final_norm_g)
    m = dict(a_w_in=m_a_w_in, a_b_f=m_a_b_f, a_w_out=m_a_w_out, b_w_q=m_b_w_q, b_w_out=m_b_w_out, kv_norm_g=m_kv_norm_g,
             w_kv=m_w_kv, mix_norm_g=m_mix_norm_g, ffn_norm_g=m_ffn_norm_g, ffn_w_up=m_ffn_w_up, ffn_conv_w=m_ffn_conv_w,
             ffn_conv_b=m_ffn_conv_b, ffn_w_down=m_ffn_w_down, final_norm_g=m_final_norm_g)
    v = dict(a_w_in=v_a_w_in, a_b_f=v_a_b_f, a_w_out=v_a_w_out, b_w_q=v_b_w_q, b_w_out=v_b_w_out, kv_norm_g=v_kv_norm_g,
             w_kv=v_w_kv, mix_norm_g=v_mix_norm_g, ffn_norm_g=v_ffn_norm_g, ffn_w_up=v_ffn_w_up, ffn_conv_w=v_ffn_conv_w,
             ffn_conv_b=v_ffn_conv_b, ffn_w_down=v_ffn_w_down, final_norm_g=v_final_norm_g)

    c_arr = lax.axis_index("c").astype(jnp.int32).reshape(1)
    k_arr = (2 * lax.axis_index("x") + lax.axis_index("y")).astype(jnp.int32).reshape(1)
    xi, yi = lax.axis_index("x"), lax.axis_index("y")
    late_slots = jnp.stack([jnp.where(k == k_arr[0], OWN_SLOT, 2 * ((k & 1) ^ yi) + ((k >> 1) ^ xi) - 1)
                            for k in range(N_CHIPS)]).astype(jnp.int32)
    w_late = _pack_weights(w, "late")
    land = lax.empty((OWN_SLOT,) + w_late.shape, BF16)
    send_sems, recv_sems, w_thru, land_thru, token = _late_gather_start(w_late, land)
    w_early = _pack_weights(w, "early")
    early = _allgather_shards(w_early, _place_own(w_early, k_arr, "early_place_own"))
    p = _early_weights(early, jnp.arange(N_CHIPS, dtype=jnp.int32))
    cb = ffn_conv_b.reshape(2, 2, 1, FF)
    p.update(a_b_f=a_b_f, kv_norm_g=kv_norm_g, mix_norm_g=mix_norm_g + token[0, 0], ffn_norm_g=ffn_norm_g,
             final_norm_g=final_norm_g, conv_b=cb)

    def late_weights(after):
        own, landed = _late_gather_wait(send_sems, recv_sems, w_thru, land_thru, after)
        return _late_weights(landed, late_slots, own)

    started = {}

    def late_grads_ready(g_so_far):
        gflat = _shard_grads(g_so_far, "grad_late")
        pair = _pair_sum(gflat, _sibling_swap_half(gflat, "late"), c_arr, "late")
        land = lax.empty((3,) + pair.shape[1:], BF16)
        *handles, token = _chip_exchange_start(pair, land)
        started["handles"] = handles
        return token

    loss_part, grad_x, g = _local_step(x[0], loss_target[0], p, late_weights, late_grads_ready)

    halves = {}
    pair, landed = _chip_exchange_wait(*started["handles"], grad_x)
    g_mine = _chip_sum(pair, landed, k_arr, "late")
    halves["grad_late"] = (g_mine, _sibling_send(g_mine, "late"))
    gflat = _shard_grads(g, "grad_early")
    pair = _pair_sum(gflat, _sibling_swap_half(gflat, "early"), c_arr, "early")
    g_mine = _chip_sum(pair, _chip_exchange(pair, "early"), k_arr, "early")
    halves["grad_early"] = (g_mine, _sibling_send(g_mine, "early"))
    small, loss = _unpack_small(_allreduce_small(_pack_small(g, loss_part)))

    big = [{}, {}, {}, {}]
    for name in _BIG:
        sg = _seg(name)
        g_mine, g_other = halves[_grad_layout(name)]
        res = _adamw_shard(_native3(w[name], name), _native3(m[name], name), _native3(v[name], name), g_mine, g_other,
                           c_arr, name)
        for store, t in zip(big, res):
            store[name] = t[:, :sg["R"], :].reshape(_SHARD_SHAPES[name])
    dws, mns, vns = _adamw(_pack_small(w), _pack_small(small), _pack_small(m), _pack_small(v), "adamw_small")
    sml = [small] + [_unpack_small(t)[0] for t in (dws, mns, vns)]
    outs = [loss, grad_x[None]]
    for b, s in zip(big, sml):
        outs += [b[n] if n in b else s[n] for n in _WEIGHTS]
    return tuple(outs)
```

```python
import numpy as np
import jax
import jax.numpy as jnp
from jax import lax
from jax.experimental import pallas as pl
from jax.experimental.pallas import tpu as pltpu

F32 = jnp.float32
BF16 = jnp.bfloat16
MESH = pl.DeviceIdType.MESH

S = 4096
D = 1024
A_HEADS = 16
HEAD_DIM = 64
QKV_W = 3 * A_HEADS * HEAD_DIM
B_GROUPS = ((128, 1), (512, 4), (2048, 16))
B_HPG = 8
B_Q_W = 3 * B_HPG * HEAD_DIM
B_OUT_W = B_HPG * HEAD_DIM
B_WIN = 128
FF = 2816
RMS_EPS = 1e-6
SCALE = HEAD_DIM ** -0.5
N_CHIPS = 4

ADAM_LR, ADAM_B1, ADAM_B2, ADAM_EPS, ADAM_WD, ADAM_STEP = 0.001, 0.9, 0.999, 1e-08, 0.01, 10

V7X_VMEM_LIMIT = 48 * 1024 * 1024
LANES = 128
NEG_INF = float("-inf")

FLAT_W = LANES
_SEGS = (("ffn_w_down", 2, 704, 1024, 704), ("a_w_in", 1, 1024, 772, 1024), ("a_w_out", 1, 256, 1024, 256),
         ("b_w_q", 1, 1024, 384, 1024), ("b_w_out", 1, 512, 256, 512), ("w_kv", 1, 1024, 768, 1024),
         ("ffn_w_up", 2, 1024, 1408, 1024), ("ffn_conv_w", 1, 6, 1408, 16))
CONV_TERMS = 3


def _seg_rows(name, L, R, C, rpad):
    return (CONV_TERMS if name == "ffn_conv_w" else 1) * L * (-(-C // FLAT_W)) * rpad


SMALL_ROWS = 24


def _cparams(sem=None, **kw):
    return pltpu.CompilerParams(dimension_semantics=sem, vmem_limit_bytes=V7X_VMEM_LIMIT, **kw)


_DN = {"nn": (((1,), (0,)), ((), ())), "nt": (((1,), (1,)), ((), ())), "tn": (((0,), (0,)), ((), ()))}


def _mm(a, b, *, mode, tm, tn, tk, name, out_dtype=F32, res=None, a_split=0, b_split=0, o_split=0):
    if mode == "tn":
        K = a.shape[0]
        M = a.shape[1]
    else:
        M = a.shape[-2]
        K = a.shape[-1] * (2 if a_split else 1)
    if mode == "nt":
        N = b.shape[0]
    else:
        N = b.shape[-1] * (2 if b_split else 1)
    assert M % tm == 0 and N % tn == 0 and K % tk == 0, (name, M, N, K, tm, tn, tk)
    nk = K // tk

    if mode == "tn":
        a_spec = pl.BlockSpec((tk, tm), lambda i, j, k: (k, i))
    elif a_split:
        a_spec = pl.BlockSpec((None, tm, tk), lambda i, j, k: (k // a_split, i, k % a_split))
    else:
        a_spec = pl.BlockSpec((tm, tk), lambda i, j, k: (i, k))
    if mode == "nt":
        b_spec = pl.BlockSpec((tn, tk), lambda i, j, k: (j, k))
    elif b_split:
        b_spec = pl.BlockSpec((None, tk, tn), lambda i, j, k: (j // b_split, k, j % b_split))
    else:
        b_spec = pl.BlockSpec((tk, tn), lambda i, j, k: (k, j))
    if o_split:
        o_spec = pl.BlockSpec((None, tm, tn), lambda i, j, k: (j // o_split, i, j % o_split))
        out_shape = jax.ShapeDtypeStruct((2, M, N // 2), out_dtype)
    else:
        o_spec = pl.BlockSpec((tm, tn), lambda i, j, k: (i, j))
        out_shape = jax.ShapeDtypeStruct((M, N), out_dtype)
    in_specs = [a_spec, b_spec]
    args = [a, b]
    if res is not None:
        in_specs.append(pl.BlockSpec((tm, tn), lambda i, j, k: (i, j)))
        args.append(res)

    def body(*refs):
        if res is not None:
            a_ref, b_ref, r_ref, o_ref = refs[:4]
        else:
            a_ref, b_ref, o_ref = refs[:3]
            r_ref = None
        p = lax.dot_general(a_ref[...].astype(BF16), b_ref[...].astype(BF16), _DN[mode], preferred_element_type=F32)

        def finish(r):
            if r_ref is not None:
                r = r + r_ref[...]
            o_ref[...] = r.astype(out_dtype)

        if nk == 1:
            finish(p)
        else:
            acc = refs[-1]
            k = pl.program_id(2)

            @pl.when(k == 0)
            def _():
                acc[...] = p

            @pl.when(k > 0)
            def _():
                acc[...] += p

            @pl.when(k == nk - 1)
            def _():
                finish(acc[...])

    return pl.pallas_call(
        body, out_shape=out_shape, grid=(M // tm, N // tn, nk), in_specs=in_specs, out_specs=o_spec,
        scratch_shapes=[pltpu.VMEM((tm, tn), F32)] if nk > 1 else [],
        compiler_params=_cparams(("parallel", "parallel", "arbitrary")), name=name,
    )(*args)


NORM_ROWS = 256


def _rms_fwd(x, gains, name):
    n = len(gains)

    def body(x_ref, *refs):
        xv = x_ref[...]
        y = xv * lax.rsqrt(jnp.mean(xv * xv, axis=-1, keepdims=True) + RMS_EPS)
        for g_ref, o_ref in zip(refs[:n], refs[n:]):
            o_ref[...] = (y * g_ref[...]).astype(BF16)

    row = pl.BlockSpec((NORM_ROWS, D), lambda i: (i, 0))
    return pl.pallas_call(
        body, out_shape=[jax.ShapeDtypeStruct((S, D), BF16)] * n, grid=(S // NORM_ROWS,),
        in_specs=[row] + [pl.BlockSpec((1, D), lambda i: (0, 0))] * n, out_specs=[row] * n,
        compiler_params=_cparams(("parallel",)), name=name,
    )(x, *[g.reshape(1, D) for g in gains])


def _rms_bwd(x, dres, pairs, name):
    n = len(pairs)

    def body(*refs):
        x_ref, dres_ref = refs[0], refs[1]
        g_refs = refs[2:2 + 2 * n:2]
        dh_refs = refs[3:3 + 2 * n:2]
        dx_ref, dxb_ref = refs[2 + 2 * n], refs[3 + 2 * n]
        dg_refs = refs[4 + 2 * n:]
        i = pl.program_id(0)
        xv = x_ref[...]
        r = lax.rsqrt(jnp.mean(xv * xv, axis=-1, keepdims=True) + RMS_EPS)
        y = xv * r
        dx = dres_ref[...]
        for g_ref, dh_ref, dg_ref in zip(g_refs, dh_refs, dg_refs):
            dh = dh_ref[...]
            dy = dh * g_ref[...]
            dx = dx + r * (dy - y * jnp.mean(dy * y, axis=-1, keepdims=True))
            part = jnp.sum(dh * y, axis=0, keepdims=True)

            @pl.when(i == 0)
            def _():
                dg_ref[...] = part

            @pl.when(i > 0)
            def _():
                dg_ref[...] += part

        dx_ref[...] = dx
        dxb_ref[...] = dx.astype(BF16)

    row = pl.BlockSpec((NORM_ROWS, D), lambda i: (i, 0))
    vec = pl.BlockSpec((1, D), lambda i: (0, 0))
    in_specs = [row, row]
    args = [x, dres]
    for g, dh in pairs:
        in_specs += [vec, row]
        args += [g.reshape(1, D), dh]
    outs = pl.pallas_call(
        body,
        out_shape=[jax.ShapeDtypeStruct((S, D), F32), jax.ShapeDtypeStruct((S, D), BF16)]
        + [jax.ShapeDtypeStruct((1, D), F32)] * n,
        grid=(S // NORM_ROWS,), in_specs=in_specs, out_specs=[row, row] + [vec] * n,
        compiler_params=_cparams(("arbitrary",)), name=name,
    )(*args)
    return outs[0], outs[1], list(outs[2:])


def _loss_head(x, g, target, name):
    def body(x_ref, g_ref, t_ref, loss_ref, dx_ref, dxb_ref, dg_ref):
        i = pl.program_id(0)
        xv = x_ref[...]
        gv = g_ref[...]
        r = lax.rsqrt(jnp.mean(xv * xv, axis=-1, keepdims=True) + RMS_EPS)
        y = xv * r
        err = y * gv - t_ref[...]
        lpart = jnp.broadcast_to(jnp.sum(err * err, keepdims=True) * (0.5 / D), (1, LANES))
        dh = err * (1.0 / D)
        dy = dh * gv
        dx = r * (dy - y * jnp.mean(dy * y, axis=-1, keepdims=True))
        part = jnp.sum(dh * y, axis=0, keepdims=True)

        @pl.when(i == 0)
        def _():
            dg_ref[...] = part
            loss_ref[...] = lpart

        @pl.when(i > 0)
        def _():
            dg_ref[...] += part
            loss_ref[...] += lpart

        dx_ref[...] = dx
        dxb_ref[...] = dx.astype(BF16)

    row = pl.BlockSpec((NORM_ROWS, D), lambda i: (i, 0))
    vec = pl.BlockSpec((1, D), lambda i: (0, 0))
    return pl.pallas_call(
        body,
        out_shape=[jax.ShapeDtypeStruct((1, LANES), F32), jax.ShapeDtypeStruct((S, D), F32),
                   jax.ShapeDtypeStruct((S, D), BF16), jax.ShapeDtypeStruct((1, D), F32)],
        grid=(S // NORM_ROWS,), in_specs=[row, vec, row],
        out_specs=[pl.BlockSpec((1, LANES), lambda i: (0, 0)), row, row, vec],
        compiler_params=_cparams(("arbitrary",)), name=name,
    )(x, g.reshape(1, D), target)


SCAN_ROWS = 256


def _split3(v):
    hi = v.astype(BF16)
    r1 = v - hi.astype(F32)
    mid = r1.astype(BF16)
    lo = (r1 - mid.astype(F32)).astype(BF16)
    return hi, mid, lo


def _tri_dot(tri, v):
    hi, mid, lo = _split3(v)
    dn = _DN["nn"]
    return (lax.dot_general(tri, hi, dn, preferred_element_type=F32)
            + lax.dot_general(tri, mid, dn, preferred_element_type=F32)
            + lax.dot_general(tri, lo, dn, preferred_element_type=F32))


def _log_sigmoid(z):
    return jnp.minimum(z, 0.0) - jnp.log(1.0 + jnp.exp(-jnp.abs(z)))


GATE_LANES = 6


def _gate_lane_tables():
    pq = np.zeros((3 * LANES, A_HEADS * HEAD_DIM), np.float32)
    pk = np.zeros((3 * LANES, A_HEADS * HEAD_DIM), np.float32)
    one_q = np.zeros((1, A_HEADS * HEAD_DIM), np.float32)
    one_k = np.zeros((1, A_HEADS * HEAD_DIM), np.float32)
    for h in range(A_HEADS):
        pos = (h // 2) * LANES + (HEAD_DIM if h % 2 == 0 else 0)
        for term in range(3):
            pq[term * LANES + h, pos + term] = 1.0
            pk[term * LANES + h, pos + 3 + term] = -1.0
        one_q[0, pos + 3:pos + GATE_LANES] = 1.0
        one_k[0, pos:pos + 3] = 1.0
    return jnp.asarray(pq, BF16), jnp.asarray(pk, BF16), jnp.asarray(one_q), jnp.asarray(one_k)


def _fgate_fwd(pf, bias, name):
    tri = jnp.tril(jnp.ones((SCAN_ROWS, SCAN_ROWS), F32)).astype(BF16)
    pq, pk, one_q, one_k = _gate_lane_tables()

    def body(pf_ref, b_ref, tri_ref, pq_ref, pk_ref, oq_ref, ok_ref, aq_ref, ak_ref, c_sc):
        carry = jnp.zeros((1, LANES), F32)
        for blk in range(S // SCAN_ROWS):
            rows = pl.ds(blk * SCAN_ROWS, SCAN_ROWS)
            lf = _log_sigmoid(pf_ref[rows, :] + b_ref[...])
            c_sc[...] = _tri_dot(tri_ref[...], lf) + carry
            carry = c_sc[pl.ds(SCAN_ROWS - 1, 1), :]
            terms = jnp.concatenate(_split3(c_sc[...]), axis=1)
            aq = lax.dot_general(terms, pq_ref[...], _DN["nn"], preferred_element_type=F32) + oq_ref[...]
            ak = lax.dot_general(terms, pk_ref[...], _DN["nn"], preferred_element_type=F32) + ok_ref[...]
            aq_ref[rows, :] = aq.astype(BF16)
            ak_ref[rows, :] = ak.astype(BF16)

    wide = jax.ShapeDtypeStruct((S, A_HEADS * HEAD_DIM), BF16)
    return pl.pallas_call(
        body, out_shape=[wide, wide], scratch_shapes=[pltpu.VMEM((SCAN_ROWS, LANES), F32)],
        compiler_params=_cparams(), name=name,
    )(pf, bias, tri, pq, pk, one_q, one_k)


def _fgate_bwd(pf, bias, dc_key, dc_query, name):
    triu = jnp.triu(jnp.ones((SCAN_ROWS, SCAN_ROWS), F32)).astype(BF16)

    def body(pf_ref, b_ref, dck_ref, dcq_ref, tri_ref, dpf_ref, db_ref, dlf_ref):
        carry = jnp.zeros((1, LANES), F32)
        db = jnp.zeros((1, LANES), F32)
        lane = lax.broadcasted_iota(jnp.int32, (SCAN_ROWS, LANES), 1)
        for blk in reversed(range(S // SCAN_ROWS)):
            rows = pl.ds(blk * SCAN_ROWS, SCAN_ROWS)
            dc = dck_ref[rows, :] + dcq_ref[rows, :]
            dlf_ref[rows, :] = _tri_dot(tri_ref[...], dc) + carry
            carry = dlf_ref[pl.ds(blk * SCAN_ROWS, 1), :]
            z = pf_ref[rows, :] + b_ref[...]
            e = jnp.exp(-jnp.abs(z))
            sig_neg = jnp.where(z >= 0.0, e, 1.0) / (1.0 + e)
            dz = jnp.where(lane < A_HEADS, dlf_ref[rows, :] * sig_neg, 0.0)
            dpf_ref[rows, :] = dz.astype(BF16)
            db = db + jnp.sum(dz, axis=0, keepdims=True)
        db_ref[...] = db

    return pl.pallas_call(
        body, out_shape=[jax.ShapeDtypeStruct((S, LANES), BF16), jax.ShapeDtypeStruct((1, LANES), F32)],
        scratch_shapes=[pltpu.VMEM((S, LANES), F32)],
        compiler_params=_cparams(), name=name,
    )(pf, bias, dc_key, dc_query, triu)


FOX_T = 512


def _first_head(shape):
    return lax.broadcasted_iota(jnp.int32, shape, len(shape) - 1) < HEAD_DIM


def _each_head(x, lo):
    zero = jnp.zeros_like(x)
    return jnp.where(lo, x, zero), jnp.where(lo, zero, x)


def _fox_pair_fwd(qkv, aug_q, aug_k, name):
    T = FOX_T
    nq = S // T
    NP = A_HEADS // 2

    def body(q_ref, k_ref, v_ref, aq_ref, ak_ref, o_ref, lse_ref, m_sc, l_sc, acc_sc):
        i = pl.program_id(1)
        j = pl.program_id(2)
        lo = _first_head((T, LANES))

        @pl.when(j == 0)
        def _():
            m_sc[...] = jnp.full((2, T, LANES), NEG_INF, F32)
            l_sc[...] = jnp.zeros((2, T, LANES), F32)
            acc_sc[...] = jnp.zeros((T, LANES), F32)

        def step(diagonal):
            qs = q_ref[...] * jnp.asarray(SCALE, BF16)
            aq, ak, kv = aq_ref[...], ak_ref[...], k_ref[...]
            q2 = (jnp.where(lo, qs, aq), jnp.where(lo, aq, qs))
            k2 = (jnp.where(lo, kv, ak), jnp.where(lo, ak, kv))
            if diagonal:
                causal = lax.broadcasted_iota(jnp.int32, (T, T), 0) >= lax.broadcasted_iota(jnp.int32, (T, T), 1)
            pv, alphas = None, []
            for h, vh in enumerate(_each_head(v_ref[...], lo)):
                s = lax.dot_general(q2[h], k2[h], _DN["nt"], preferred_element_type=F32)
                if diagonal:
                    s = jnp.where(causal, s, NEG_INF)
                m_prev = m_sc[h]
                m_new = jnp.maximum(m_prev, jnp.max(s, axis=1, keepdims=True))
                alpha = jnp.exp(m_prev - m_new)
                p = jnp.exp(s - jnp.tile(m_new, (1, T // LANES)))
                l_sc[h] = alpha * l_sc[h] + jnp.sum(p, axis=1, keepdims=True)
                m_sc[h] = m_new
                d = lax.dot_general(p.astype(BF16), vh, _DN["nn"], preferred_element_type=F32)
                pv = d if pv is None else pv + d
                alphas.append(alpha)
            acc_sc[...] = jnp.where(lo, alphas[0], alphas[1]) * acc_sc[...] + pv

        @pl.when(j < i)
        def _():
            step(False)

        @pl.when(j == i)
        def _():
            step(True)
            o_ref[...] = (acc_sc[...] * jnp.where(lo, 1.0 / l_sc[0], 1.0 / l_sc[1])).astype(BF16)
            for h in range(2):
                lse_ref[h] = (m_sc[h] + jnp.log(l_sc[h]))[:, 0:1]

    qs_ = pl.BlockSpec((T, LANES), lambda p, i, j: (i, p))
    ks = pl.BlockSpec((T, LANES), lambda p, i, j: (jnp.minimum(i, j), NP + p))
    vs = pl.BlockSpec((T, LANES), lambda p, i, j: (jnp.minimum(i, j), 2 * NP + p))
    aks = pl.BlockSpec((T, LANES), lambda p, i, j: (jnp.minimum(i, j), p))
    col = pl.BlockSpec((2, T, 1), lambda p, i, j: (p, i, 0))
    return pl.pallas_call(
        body, out_shape=[jax.ShapeDtypeStruct((S, A_HEADS * HEAD_DIM), BF16), jax.ShapeDtypeStruct((A_HEADS, S, 1), F32)],
        grid=(NP, nq, nq), in_specs=[qs_, ks, vs, qs_, aks], out_specs=[qs_, col],
        scratch_shapes=[pltpu.VMEM((2, T, LANES), F32), pltpu.VMEM((2, T, LANES), F32), pltpu.VMEM((T, LANES), F32)],
        compiler_params=_cparams(("parallel", "parallel", "arbitrary")), name=name,
    )(qkv, qkv, qkv, aug_q, aug_k)


def _fox_pair_bwd(qkv, do, lse_row, delta_row, aug_q, aug_k, name):
    T = FOX_T
    nq = S // T
    NP = A_HEADS // 2

    def body(q_ref, k_ref, v_ref, do_ref, lse_ref, dl_ref, aq_ref, ak_ref, dq_ref, dk_ref, dv_ref, dc_ref, dcq_ref,
             dq_sc, dk_sc, dv_sc, dc_sc):
        j = pl.program_id(1)
        i = pl.program_id(2)
        lo = _first_head((T, LANES))

        @pl.when(jnp.logical_and(j == 0, i == 0))
        def _():
            dq_sc[...] = jnp.zeros((S, LANES), F32)
            dcq_ref[...] = jnp.zeros((2, nq, 1, T), F32)

        @pl.when(i == j)
        def _():
            dk_sc[...] = jnp.zeros((T, LANES), F32)
            dv_sc[...] = jnp.zeros((T, LANES), F32)
            dc_sc[...] = jnp.zeros((2, T, 1), F32)

        def step(diagonal):
            qv = q_ref[...]
            kv = k_ref[...]
            dov = do_ref[...].astype(BF16)
            qs = qv * jnp.asarray(SCALE, BF16)
            aq, ak = aq_ref[...], ak_ref[...]
            q2 = (jnp.where(lo, qs, aq), jnp.where(lo, aq, qs))
            k2 = (jnp.where(lo, kv, ak), jnp.where(lo, ak, kv))
            if diagonal:
                causal = lax.broadcasted_iota(jnp.int32, (T, T), 1) >= lax.broadcasted_iota(jnp.int32, (T, T), 0)
            dv = dk = dq = None
            for h, (kh, vh, qh, doh) in enumerate(zip(_each_head(kv, lo), _each_head(v_ref[...], lo),
                                                      _each_head(qv, lo), _each_head(dov, lo))):
                st = lax.dot_general(k2[h], q2[h], _DN["nt"], preferred_element_type=F32)
                if diagonal:
                    st = jnp.where(causal, st, NEG_INF)
                pt = jnp.exp(st - lse_ref[h])
                d = lax.dot_general(pt.astype(BF16), doh, _DN["nn"], preferred_element_type=F32)
                dv = d if dv is None else dv + d
                dpt = lax.dot_general(vh, dov, _DN["nt"], preferred_element_type=F32)
                dst = pt * (dpt - dl_ref[h])
                dc_sc[h] -= jnp.sum(dst, axis=1, keepdims=True)
                dcq_ref[h, i] += jnp.sum(dst, axis=0, keepdims=True)
                dsb = (dst * SCALE).astype(BF16)
                d = lax.dot_general(dsb, qh, _DN["nn"], preferred_element_type=F32)
                dk = d if dk is None else dk + d
                d = lax.dot_general(dsb, kh, _DN["tn"], preferred_element_type=F32)
                dq = d if dq is None else dq + d
            dv_sc[...] += dv
            dk_sc[...] += dk
            rows = pl.ds(pl.multiple_of(i * T, T), T)
            dq_sc[rows, :] += dq

        @pl.when(i > j)
        def _():
            step(False)

        @pl.when(i == j)
        def _():
            step(True)

        @pl.when(i == nq - 1)
        def _():
            dk_ref[...] = dk_sc[...].astype(BF16)
            dv_ref[...] = dv_sc[...].astype(BF16)
            dc_ref[...] = dc_sc[...]

        @pl.when(jnp.logical_and(j == nq - 1, i == nq - 1))
        def _():
            dq_ref[...] = dq_sc[...].astype(BF16)

    qs = pl.BlockSpec((T, LANES), lambda p, j, i: (jnp.maximum(i, j), p))
    qrow = pl.BlockSpec((2, 1, T), lambda p, j, i: (p, 0, jnp.maximum(i, j)))
    ks = pl.BlockSpec((T, LANES), lambda p, j, i: (j, NP + p))
    vs = pl.BlockSpec((T, LANES), lambda p, j, i: (j, 2 * NP + p))
    kout = pl.BlockSpec((T, LANES), lambda p, j, i: (j, p))
    kcol = pl.BlockSpec((2, T, 1), lambda p, j, i: (p, j, 0))
    dqs = pl.BlockSpec((S, LANES), lambda p, j, i: (0, p))
    dcqs = pl.BlockSpec((2, nq, 1, T), lambda p, j, i: (p, 0, 0, 0))
    wide = jax.ShapeDtypeStruct((S, A_HEADS * HEAD_DIM), BF16)
    return pl.pallas_call(
        body,
        out_shape=[wide, wide, wide, jax.ShapeDtypeStruct((A_HEADS, S, 1), F32),
                   jax.ShapeDtypeStruct((A_HEADS, nq, 1, T), F32)],
        grid=(NP, nq, nq), in_specs=[qs, ks, vs, qs, qrow, qrow, qs, kout], out_specs=[dqs, kout, kout, kcol, dcqs],
        scratch_shapes=[pltpu.VMEM((S, LANES), F32), pltpu.VMEM((T, LANES), F32), pltpu.VMEM((T, LANES), F32),
                        pltpu.VMEM((2, T, 1), F32)],
        compiler_params=_cparams(("parallel", "arbitrary", "arbitrary")), name=name,
    )(qkv, qkv, qkv, do, lse_row, delta_row, aug_q, aug_k)


def _pair_rowdot(a, b, name):
    n = a.shape[1] // HEAD_DIM
    T = 1024

    def body(a_ref, b_ref, o_ref):
        prod = a_ref[...].astype(F32) * b_ref[...].astype(F32)
        lo = _first_head(prod.shape)
        o_ref[0] = jnp.sum(jnp.where(lo, prod, 0.0), axis=1, keepdims=True)
        o_ref[1] = jnp.sum(jnp.where(lo, 0.0, prod), axis=1, keepdims=True)

    blk = pl.BlockSpec((T, LANES), lambda p, i: (i, p))
    return pl.pallas_call(
        body, out_shape=jax.ShapeDtypeStruct((n, S, 1), F32), grid=(n // 2, S // T), in_specs=[blk, blk],
        out_specs=pl.BlockSpec((2, T, 1), lambda p, i: (p, i, 0)),
        compiler_params=_cparams(("parallel", "parallel")), name=name,
    )(a, b)


W = B_WIN
N_HG = 3 * B_HPG
N_BLK = S // W


def _dil_tables():
    slopes = np.exp2((-8.0 * np.arange(1, N_HG + 1, dtype=np.float32) / N_HG).astype(np.float32)).astype(np.float32)
    dil = np.repeat(np.array([d for _, d in B_GROUPS], np.float32), B_HPG)
    coef = (slopes * dil).astype(np.float32)
    nbs = np.repeat(np.array([S // d // W for _, d in B_GROUPS], np.int32), B_HPG)
    return jnp.asarray(coef), jnp.asarray(nbs)


DIL_SUB = 8
DIL_ROWS = DIL_SUB * W
DIL_STEPS = S // DIL_ROWS


def _dil_bias(coef, transposed):
    row = lax.broadcasted_iota(jnp.int32, (W, 2 * W), 0)
    col = lax.broadcasted_iota(jnp.int32, (W, 2 * W), 1)
    dist = (col - row) if transposed else (row + W - col)
    valid = jnp.logical_and(dist >= 0, dist <= W)
    return jnp.where(valid, -coef * dist.astype(F32), NEG_INF), col


NPG = B_HPG // 2
GROUP_W = B_HPG * HEAD_DIM


def _dil_pair_specs(qoff, koff, voff):
    prev_blk = lambda n: jnp.maximum(n * DIL_SUB - 1, 0)
    next_blk = lambda n: jnp.minimum((n + 1) * DIL_SUB, N_BLK - 1)
    return dict(
        o=pl.BlockSpec((DIL_ROWS, LANES), lambda h, n: (n, h)),
        o_next=pl.BlockSpec((W, LANES), lambda h, n: (next_blk(n), h)),
        q=pl.BlockSpec((DIL_ROWS, LANES), lambda h, n: (n, qoff + h)),
        q_next=pl.BlockSpec((W, LANES), lambda h, n: (next_blk(n), qoff + h)),
        k=pl.BlockSpec((DIL_ROWS, LANES), lambda h, n: (n, koff + h)),
        k_prev=pl.BlockSpec((W, LANES), lambda h, n: (prev_blk(n), koff + h)),
        v=pl.BlockSpec((DIL_ROWS, LANES), lambda h, n: (n, voff + h)),
        v_prev=pl.BlockSpec((W, LANES), lambda h, n: (prev_blk(n), voff + h)),
        col=pl.BlockSpec((2, DIL_ROWS, 1), lambda h, n: (h, n, 0)),
        col2=pl.BlockSpec((2, DIL_ROWS, 1), lambda h, n: (NPG + h, n, 0)),
        row=pl.BlockSpec((2, 1, DIL_ROWS), lambda h, n: (h, 0, n)),
        row_next=pl.BlockSpec((2, 1, W), lambda h, n: (h, 0, next_blk(n))),
        row2=pl.BlockSpec((2, 1, DIL_ROWS), lambda h, n: (NPG + h, 0, n)),
        row2_next=pl.BlockSpec((2, 1, W), lambda h, n: (NPG + h, 0, next_blk(n))),
        smem=pl.BlockSpec(memory_space=pltpu.SMEM))


def _dil_pair_fwd(g, q, k, v, qoff, koff, voff, name):
    coef_t, nbs_t = _dil_tables()

    def body(coef_ref, nbs_ref, q_ref, kh_ref, k_ref, vh_ref, v_ref, o_ref, lse_ref, kf, vf):
        hp = pl.program_id(0)
        n = pl.program_id(1)
        nbs = nbs_ref[B_HPG * g + 2 * hp]
        kf[0:W, :] = kh_ref[...]
        kf[W:, :] = k_ref[...]
        vf[0:W, :] = vh_ref[...]
        vf[W:, :] = v_ref[...]
        biases = [_dil_bias(coef_ref[B_HPG * g + 2 * hp + h], False) for h in range(2)]
        col = biases[0][1]
        lo = _first_head((W, LANES))
        lo2 = _first_head((2 * W, LANES))
        for b in range(DIL_SUB):
            first = lax.rem(n * DIL_SUB + b, nbs) == 0
            rows = slice(b * W, (b + 1) * W)
            both = slice(b * W, (b + 2) * W)
            qv = q_ref[rows, :]
            acc, inv = None, []
            for h, (kh, vh) in enumerate(zip(_each_head(kf[both, :], lo2), _each_head(vf[both, :], lo2))):
                s = lax.dot_general(qv, kh, _DN["nt"], preferred_element_type=F32) * SCALE + biases[h][0]
                s = jnp.where(jnp.logical_and(first, col < W), NEG_INF, s)
                m = jnp.max(s, axis=1, keepdims=True)
                p = jnp.exp(s - m)
                l = jnp.sum(p, axis=1, keepdims=True)
                d = lax.dot_general(p.astype(BF16), vh, _DN["nn"], preferred_element_type=F32)
                acc = d if acc is None else acc + d
                inv.append(1.0 / l)
                lse_ref[h, rows, :] = m + jnp.log(l)
            o_ref[rows, :] = acc * jnp.where(lo, inv[0], inv[1])

    sp = _dil_pair_specs(qoff, koff, voff)
    return pl.pallas_call(
        body, out_shape=[jax.ShapeDtypeStruct((S, GROUP_W), F32), jax.ShapeDtypeStruct((B_HPG, S, 1), F32)],
        grid=(NPG, DIL_STEPS), in_specs=[sp["smem"], sp["smem"], sp["q"], sp["k_prev"], sp["k"], sp["v_prev"], sp["v"]],
        out_specs=[sp["o"], sp["col"]], scratch_shapes=[pltpu.VMEM((DIL_ROWS + W, LANES), BF16)] * 2,
        compiler_params=_cparams(("parallel", "parallel")), name=name,
    )(coef_t, nbs_t, q, k, k, v, v)


def _dil_pair_merge(os, lses, name):
    T = 1024

    def body(o0_ref, o1_ref, o2_ref, l0_ref, l1_ref, l2_ref, om_ref, omb_ref, l_ref):
        lo = _first_head((T, LANES))
        weights = []
        for h in range(2):
            l0, l1, l2 = l0_ref[h], l1_ref[h], l2_ref[h]
            m = jnp.maximum(jnp.maximum(l0, l1), l2)
            e0, e1, e2 = jnp.exp(l0 - m), jnp.exp(l1 - m), jnp.exp(l2 - m)
            den = e0 + e1 + e2
            weights.append((e0 / den, e1 / den, e2 / den))
            l_ref[h] = m + jnp.log(den)
        om = (jnp.where(lo, weights[0][0], weights[1][0]) * o0_ref[...]
              + jnp.where(lo, weights[0][1], weights[1][1]) * o1_ref[...]
              + jnp.where(lo, weights[0][2], weights[1][2]) * o2_ref[...])
        om_ref[...] = om
        omb_ref[...] = om.astype(BF16)

    ob = pl.BlockSpec((T, LANES), lambda p, i: (i, p))
    lb = pl.BlockSpec((2, T, 1), lambda p, i: (p, i, 0))
    return pl.pallas_call(
        body,
        out_shape=[jax.ShapeDtypeStruct((S, B_OUT_W), F32), jax.ShapeDtypeStruct((S, B_OUT_W), BF16),
                   jax.ShapeDtypeStruct((B_HPG, S, 1), F32)],
        grid=(NPG, S // T), in_specs=[ob] * 3 + [lb] * 3, out_specs=[ob, ob, lb],
        compiler_params=_cparams(("parallel", "parallel")), name=name,
    )(*os, *lses)


def _dil_pair_dq(g, q, k, v, qoff, koff, voff, do, stats, name):
    coef_t, nbs_t = _dil_tables()

    def body(coef_ref, nbs_ref, q_ref, kh_ref, k_ref, vh_ref, v_ref, do_ref, l_ref, d_ref, dq_ref, kf, vf):
        hp = pl.program_id(0)
        n = pl.program_id(1)
        nbs = nbs_ref[B_HPG * g + 2 * hp]
        kf[0:W, :] = kh_ref[...]
        kf[W:, :] = k_ref[...]
        vf[0:W, :] = vh_ref[...]
        vf[W:, :] = v_ref[...]
        biases = [_dil_bias(coef_ref[B_HPG * g + 2 * hp + h], False) for h in range(2)]
        col = biases[0][1]
        lo2 = _first_head((2 * W, LANES))
        for b in range(DIL_SUB):
            first = lax.rem(n * DIL_SUB + b, nbs) == 0
            rows = slice(b * W, (b + 1) * W)
            both = slice(b * W, (b + 2) * W)
            qv = q_ref[rows, :]
            dov = do_ref[rows, :]
            acc = None
            for h, (kh, vh) in enumerate(zip(_each_head(kf[both, :], lo2), _each_head(vf[both, :], lo2))):
                s = lax.dot_general(qv, kh, _DN["nt"], preferred_element_type=F32) * SCALE + biases[h][0]
                s = jnp.where(jnp.logical_and(first, col < W), NEG_INF, s)
                p = jnp.exp(s - l_ref[h, rows, :])
                dp = lax.dot_general(dov, vh, _DN["nt"], preferred_element_type=F32)
                ds = (p * (dp - d_ref[h, rows, :]) * SCALE).astype(BF16)
                d = lax.dot_general(ds, kh, _DN["nn"], preferred_element_type=F32)
                acc = d if acc is None else acc + d
            dq_ref[rows, :] = acc.astype(BF16)

    sp = _dil_pair_specs(qoff, koff, voff)
    return pl.pallas_call(
        body, out_shape=jax.ShapeDtypeStruct((S, GROUP_W), BF16), grid=(NPG, DIL_STEPS),
        in_specs=[sp["smem"], sp["smem"], sp["q"], sp["k_prev"], sp["k"], sp["v_prev"], sp["v"], sp["o"], sp["col"],
                  sp["col2"]],
        out_specs=sp["o"], scratch_shapes=[pltpu.VMEM((DIL_ROWS + W, LANES), BF16)] * 2,
        compiler_params=_cparams(("parallel", "parallel")), name=name,
    )(coef_t, nbs_t, q, k, k, v, v, do, stats, stats)


def _dil_pair_dkv(g, q, k, v, qoff, koff, voff, do, stats, name):
    coef_t, nbs_t = _dil_tables()

    def body(coef_ref, nbs_ref, k_ref, v_ref, q_ref, qn_ref, do_ref, don_ref, l_ref, ln_ref, d_ref, dn_ref,
             dk_ref, dv_ref, qf, dof, lf, df):
        hp = pl.program_id(0)
        n = pl.program_id(1)
        nbs = nbs_ref[B_HPG * g + 2 * hp]
        qf[0:DIL_ROWS, :] = q_ref[...]
        qf[DIL_ROWS:, :] = qn_ref[...]
        dof[0:DIL_ROWS, :] = do_ref[...]
        dof[DIL_ROWS:, :] = don_ref[...]
        lf[:, :, 0:DIL_ROWS] = l_ref[...]
        lf[:, :, DIL_ROWS:] = ln_ref[...]
        df[:, :, 0:DIL_ROWS] = d_ref[...]
        df[:, :, DIL_ROWS:] = dn_ref[...]
        biases = [_dil_bias(coef_ref[B_HPG * g + 2 * hp + h], True) for h in range(2)]
        col = biases[0][1]
        lo = _first_head((W, LANES))
        lo2 = _first_head((2 * W, LANES))
        for b in range(DIL_SUB):
            no_next = lax.rem(n * DIL_SUB + b + 1, nbs) == 0
            rows = slice(b * W, (b + 1) * W)
            both = slice(b * W, (b + 2) * W)
            dd = dof[both, :]
            dk = dv = None
            for h, (kh, vh, qh, ddh) in enumerate(zip(_each_head(k_ref[rows, :], lo), _each_head(v_ref[rows, :], lo),
                                                      _each_head(qf[both, :], lo2), _each_head(dd, lo2))):
                st = lax.dot_general(kh, qh, _DN["nt"], preferred_element_type=F32) * SCALE + biases[h][0]
                st = jnp.where(jnp.logical_and(no_next, col >= W), NEG_INF, st)
                pt = jnp.exp(st - lf[h, :, both])
                d = lax.dot_general(pt.astype(BF16), ddh, _DN["nn"], preferred_element_type=F32)
                dv = d if dv is None else dv + d
                dpt = lax.dot_general(vh, dd, _DN["nt"], preferred_element_type=F32)
                dst = (pt * (dpt - df[h, :, both]) * SCALE).astype(BF16)
                d = lax.dot_general(dst, qh, _DN["nn"], preferred_element_type=F32)
                dk = d if dk is None else dk + d
            dk_ref[rows, :] = dk.astype(BF16)
            dv_ref[rows, :] = dv.astype(BF16)

    sp = _dil_pair_specs(qoff, koff, voff)
    wide = jax.ShapeDtypeStruct((S, GROUP_W), BF16)
    return pl.pallas_call(
        body, out_shape=[wide, wide], grid=(NPG, DIL_STEPS),
        in_specs=[sp["smem"], sp["smem"], sp["k"], sp["v"], sp["q"], sp["q_next"], sp["o"], sp["o_next"], sp["row"],
                  sp["row_next"], sp["row2"], sp["row2_next"]],
        out_specs=[sp["o"], sp["o"]],
        scratch_shapes=[pltpu.VMEM((DIL_ROWS + W, LANES), BF16)] * 2 + [pltpu.VMEM((2, 1, DIL_ROWS + W), F32)] * 2,
        compiler_params=_cparams(("parallel", "parallel")), name=name,
    )(coef_t, nbs_t, k, v, q, q, do, do, stats, stats, stats, stats)


FFN_ROWS = 512
FFN_COLS = 256
HALO = 8


def _shifted(u, halo, back):
    T = u.shape[0]
    rows = lax.broadcasted_iota(jnp.int32, u.shape, 0)
    if back:
        s1 = jnp.where(rows == 0, halo[HALO - 1:HALO, :], pltpu.roll(u, 1, 0))
        s2 = jnp.where(rows == 0, halo[HALO - 2:HALO - 1, :],
                       jnp.where(rows == 1, halo[HALO - 1:HALO, :], pltpu.roll(u, 2, 0)))
    else:
        s1 = jnp.where(rows == T - 1, halo[0:1, :], pltpu.roll(u, T - 1, 0))
        s2 = jnp.where(rows == T - 1, halo[1:2, :],
                       jnp.where(rows == T - 2, halo[0:1, :], pltpu.roll(u, T - 2, 0)))
    return s1, s2


def _conv_parts(u_ref, h_ref, w_ref, b_ref, first):
    out = []
    for p in range(2):
        u = u_ref[p]
        halo = jnp.where(first, 0.0, h_ref[p])
        u1, u2 = _shifted(u, halo, True)
        w = w_ref[p]
        out.append((w[0:1, :] * u2 + w[1:2, :] * u1 + w[2:3, :] * u + b_ref[p], u1, u2, u))
    return out


def _ffn_specs():
    T, C = FFN_ROWS, FFN_COLS
    blk = pl.BlockSpec((2, T, C), lambda j, i: (0, i, j))
    prev = pl.BlockSpec((2, HALO, C), lambda j, i: (0, jnp.maximum(i * (T // HALO) - 1, 0), j))
    nxt = pl.BlockSpec((2, HALO, C), lambda j, i: (0, jnp.minimum((i + 1) * (T // HALO), S // HALO - 1), j))
    wsp = pl.BlockSpec((2, 3, C), lambda j, i: (0, 0, j))
    bsp = pl.BlockSpec((2, 1, C), lambda j, i: (0, 0, j))
    one = pl.BlockSpec((T, C), lambda j, i: (i, j))
    return blk, prev, nxt, wsp, bsp, one


def _ffn_act_fwd(u, w, b, name):
    blk, prev, _, wsp, bsp, one = _ffn_specs()

    def body(u_ref, h_ref, w_ref, b_ref, o_ref):
        (a, _, _, _), (g, _, _, _) = _conv_parts(u_ref, h_ref, w_ref, b_ref, pl.program_id(1) == 0)
        o_ref[...] = (g / (1.0 + jnp.exp(-g)) * a).astype(BF16)

    return pl.pallas_call(
        body, out_shape=jax.ShapeDtypeStruct((S, FF), BF16), grid=(FF // FFN_COLS, S // FFN_ROWS),
        in_specs=[blk, prev, wsp, bsp], out_specs=one,
        compiler_params=_cparams(("parallel", "parallel")), name=name,
    )(u, u, w, b)


def _ffn_act_bwd(u, dact, w, b, name):
    blk, prev, _, wsp, bsp, one = _ffn_specs()

    def body(u_ref, h_ref, da_ref, w_ref, b_ref, duc_ref, dwb_ref):
        i = pl.program_id(1)
        (a, a1, a2, a0), (g, g1, g2, g0) = _conv_parts(u_ref, h_ref, w_ref, b_ref, i == 0)
        dact_v = da_ref[...]
        sg = 1.0 / (1.0 + jnp.exp(-g))
        d_a = dact_v * (g * sg)
        d_g = dact_v * a * (sg * (1.0 + g * (1.0 - sg)))
        duc_ref[0] = d_a
        duc_ref[1] = d_g

        @pl.when(i == 0)
        def _():
            dwb_ref[...] = jnp.zeros(dwb_ref.shape, F32)

        for p, (d, s2, s1, s0) in enumerate(((d_a, a2, a1, a0), (d_g, g2, g1, g0))):
            dwb_ref[p, 0:1, :] += jnp.sum(d * s2, axis=0, keepdims=True)
            dwb_ref[p, 1:2, :] += jnp.sum(d * s1, axis=0, keepdims=True)
            dwb_ref[p, 2:3, :] += jnp.sum(d * s0, axis=0, keepdims=True)
            dwb_ref[p, 3:4, :] += jnp.sum(d, axis=0, keepdims=True)

    return pl.pallas_call(
        body, out_shape=[jax.ShapeDtypeStruct((2, S, FF), F32), jax.ShapeDtypeStruct((2, 8, FF), F32)],
        grid=(FF // FFN_COLS, S // FFN_ROWS), in_specs=[blk, prev, one, wsp, bsp],
        out_specs=[blk, pl.BlockSpec((2, 8, FFN_COLS), lambda j, i: (0, 0, j))],
        compiler_params=_cparams(("parallel", "arbitrary")), name=name,
    )(u, u, dact, w, b)


def _ffn_conv_bwd(duc, w, name):
    blk, _, nxt, wsp, _, _ = _ffn_specs()
    last = S // FFN_ROWS - 1

    def body(d_ref, h_ref, w_ref, du_ref):
        is_last = pl.program_id(1) == last
        for p in range(2):
            d = d_ref[p]
            halo = jnp.where(is_last, 0.0, h_ref[p])
            d1, d2 = _shifted(d, halo, False)
            wv = w_ref[p]
            du_ref[p] = (wv[2:3, :] * d + wv[1:2, :] * d1 + wv[0:1, :] * d2).astype(BF16)

    return pl.pallas_call(
        body, out_shape=jax.ShapeDtypeStruct((2, S, FF), BF16), grid=(FF // FFN_COLS, S // FFN_ROWS),
        in_specs=[blk, nxt, wsp], out_specs=blk,
        compiler_params=_cparams(("parallel", "parallel")), name=name,
    )(duc, duc, w)


def _adam_update(w, gv, m, v):
    c1 = 1.0 / (1.0 - ADAM_B1 ** ADAM_STEP)
    c2 = 1.0 / (1.0 - ADAM_B2 ** ADAM_STEP)
    mn = ADAM_B1 * m + (1.0 - ADAM_B1) * gv
    vn = ADAM_B2 * v + (1.0 - ADAM_B2) * (gv * gv)
    return -ADAM_LR * ((mn * c1) / (jnp.sqrt(vn * c2) + ADAM_EPS) + ADAM_WD * w), mn, vn


def _adamw(w, g, m, v, name):
    rows = w.shape[0]
    T = 8
    for cand in (256, 128, 64, 32, 16, 8):
        if rows % cand == 0:
            T = cand
            break

    def body(w_ref, g_ref, m_ref, v_ref, d_ref, mo_ref, vo_ref):
        d_ref[...], mo_ref[...], vo_ref[...] = _adam_update(w_ref[...], g_ref[...], m_ref[...], v_ref[...])

    blk = pl.BlockSpec((T, w.shape[1]), lambda i: (i, 0))
    sds = jax.ShapeDtypeStruct(w.shape, F32)
    return pl.pallas_call(
        body, out_shape=[sds, sds, sds], grid=(rows // T,), in_specs=[blk] * 4, out_specs=[blk] * 3,
        compiler_params=_cparams(("parallel",)), name=name,
    )(w, g, m, v)


ANY = pl.BlockSpec(memory_space=pl.ANY)


def _place():
    x, y, c = lax.axis_index("x"), lax.axis_index("y"), lax.axis_index("c")
    chips = [(1 - x, y), (x, 1 - y), (1 - x, 1 - y)]
    return x, y, c, chips


def _place_own(w, slot_arr, name):
    rows = w.shape[0]
    T = 16
    for cand in (2048, 1024, 512, 256, 128, 64, 32, 16):
        if rows % cand == 0:
            T = cand
            break

    def body(k_ref, w_ref, o_ref):
        o_ref[...] = w_ref[...]

    return pl.pallas_call(
        body, out_shape=jax.ShapeDtypeStruct((N_CHIPS, rows, FLAT_W), w.dtype),
        grid_spec=pltpu.PrefetchScalarGridSpec(
            num_scalar_prefetch=1, grid=(rows // T,),
            in_specs=[pl.BlockSpec((T, FLAT_W), lambda i, k: (i, 0))],
            out_specs=pl.BlockSpec((None, T, FLAT_W), lambda i, k: (k[0], i, 0))),
        compiler_params=_cparams(("parallel",)), name=name,
    )(slot_arr, w)


def _allgather_shards(w, buf):
    half_rows = w.shape[0] // 2
    assert half_rows % 16 == 0

    def body(w_ref, buf_ref, g_ref, send_sems, recv_sems):
        x, y, c, chips = _place()
        myk = 2 * x + y
        sibling = (x, y, 1 - c)
        h0 = pl.multiple_of(c * half_rows, 16)
        h1 = pl.multiple_of((1 - c) * half_rows, 16)

        def half(k, start):
            return g_ref.at[k, pl.ds(start, half_rows), :]

        def rcopy(sem, src, dst, to):
            return pltpu.make_async_remote_copy(src_ref=src, dst_ref=dst, send_sem=send_sems.at[sem],
                                                recv_sem=recv_sems.at[sem], device_id=to, device_id_type=MESH)

        ici = [rcopy(r, w_ref.at[pl.ds(h0, half_rows), :], half(myk, h0), (*chip, c)) for r, chip in enumerate(chips)]
        for cp in ici:
            cp.start()
        ks = [2 * cx + cy for cx, cy in chips]
        fwd = [rcopy(3 + r, half(ks[r], h0), half(ks[r], h0), sibling) for r in range(3)]
        for r in range(3):
            rcopy(r, half(ks[r], h0), half(ks[r], h0), (*chips[r], c)).wait_recv()
            fwd[r].start()
        for r in range(3):
            rcopy(3 + r, half(ks[r], h1), half(ks[r], h1), sibling).wait_recv()
        for cp in ici + fwd:
            cp.wait_send()

    return pl.pallas_call(
        body, out_shape=jax.ShapeDtypeStruct(buf.shape, w.dtype), in_specs=[ANY, ANY], out_specs=ANY,
        scratch_shapes=[pltpu.SemaphoreType.DMA((6,)), pltpu.SemaphoreType.DMA((6,))],
        input_output_aliases={1: 0},
        compiler_params=pltpu.CompilerParams(has_side_effects=True), name="allgather_shards",
    )(w, buf)


HBM_SPEC = pl.BlockSpec(memory_space=pltpu.HBM)
SEM_SPEC = pl.BlockSpec(memory_space=pltpu.SEMAPHORE)
DATAFLOW = pltpu.SideEffectType.DATAFLOW_SIDE_EFFECTING
OWN_SLOT = 3


def _late_gather_start(w, land):
    def body(w_ref, land_ref, send_sems, recv_sems, w_thru, land_thru, token):
        x, y, c, chips = _place()
        for r, chip in enumerate(chips):
            pltpu.make_async_remote_copy(src_ref=w_ref, dst_ref=land_ref.at[r], send_sem=send_sems.at[r],
                                         recv_sem=recv_sems.at[r], device_id=(*chip, c), device_id_type=MESH).start()
        token[...] = jnp.zeros_like(token)

    return pl.pallas_call(
        body, name="late_gather_start",
        out_shape=(pltpu.SemaphoreType.DMA((3,)), pltpu.SemaphoreType.DMA((3,)), pltpu.HBM(w.shape, w.dtype),
                   pltpu.HBM(land.shape, land.dtype), jax.ShapeDtypeStruct((8, LANES), F32)),
        in_specs=(HBM_SPEC, HBM_SPEC),
        out_specs=(SEM_SPEC, SEM_SPEC, HBM_SPEC, HBM_SPEC, pl.BlockSpec(memory_space=pltpu.VMEM)),
        input_output_aliases={0: 2, 1: 3}, compiler_params=pltpu.CompilerParams(has_side_effects=DATAFLOW),
    )(pltpu.with_memory_space_constraint(w, pltpu.HBM), pltpu.with_memory_space_constraint(land, pltpu.HBM))


def _late_gather_wait(send_sems, recv_sems, w_thru, land_thru, after):
    def body(w_ref, land_ref, send_sems, recv_sems, after_ref, w_dead, got_ref):
        x, y, c, chips = _place()
        for r, chip in enumerate(chips):
            cp = pltpu.make_async_remote_copy(src_ref=w_ref, dst_ref=land_ref.at[r], send_sem=send_sems.at[r],
                                              recv_sem=recv_sems.at[r], device_id=(*chip, c), device_id_type=MESH)
            cp.wait_send()
            cp.wait_recv()

    return pl.pallas_call(
        body, name="late_gather_wait",
        out_shape=(pltpu.HBM(w_thru.shape, w_thru.dtype), pltpu.HBM(land_thru.shape, land_thru.dtype)),
        in_specs=(HBM_SPEC, HBM_SPEC, SEM_SPEC, SEM_SPEC, pl.BlockSpec(memory_space=pl.ANY)),
        out_specs=(HBM_SPEC, HBM_SPEC), input_output_aliases={0: 0, 1: 1},
        compiler_params=pltpu.CompilerParams(has_side_effects=DATAFLOW),
    )(w_thru, land_thru, send_sems, recv_sems, after)


def _flat_tile(rows):
    return next(t for t in (2048, 1024, 512, 256, 128, 64, 32, 16) if rows % t == 0)


def _sibling_swap_half(g, tag):
    half = g.shape[1] // 2

    def body(g_ref, o_ref, send_sem, recv_sem):
        x, y, c, _ = _place()
        theirs = pl.multiple_of((1 - c) * half, 8)
        cp = pltpu.make_async_remote_copy(src_ref=g_ref.at[:, pl.ds(theirs, half), :], dst_ref=o_ref,
                                          send_sem=send_sem, recv_sem=recv_sem, device_id=(x, y, 1 - c),
                                          device_id_type=MESH)
        cp.start()
        cp.wait()

    return pl.pallas_call(
        body, out_shape=jax.ShapeDtypeStruct((N_CHIPS, half, FLAT_W), F32), in_specs=[ANY], out_specs=ANY,
        scratch_shapes=[pltpu.SemaphoreType.DMA, pltpu.SemaphoreType.DMA],
        compiler_params=pltpu.CompilerParams(has_side_effects=True), name=f"rs_sibling_swap_{tag}",
    )(g)


def _sibling_swap_start(g, land):
    half = g.shape[1] // 2

    def body(g_ref, land_ref, send_sem, recv_sem, g_thru, land_thru, token):
        x, y, c, _ = _place()
        theirs = pl.multiple_of((1 - c) * half, 8)
        pltpu.make_async_remote_copy(src_ref=g_ref.at[:, pl.ds(theirs, half), :], dst_ref=land_ref, send_sem=send_sem,
                                     recv_sem=recv_sem, device_id=(x, y, 1 - c), device_id_type=MESH).start()
        token[...] = jnp.zeros_like(token)

    return pl.pallas_call(
        body, name="rs_swap_start",
        out_shape=(pltpu.SemaphoreType.DMA(()), pltpu.SemaphoreType.DMA(()), pltpu.HBM(g.shape, g.dtype),
                   pltpu.HBM(land.shape, land.dtype), jax.ShapeDtypeStruct((8, LANES), F32)),
        in_specs=(HBM_SPEC, HBM_SPEC),
        out_specs=(SEM_SPEC, SEM_SPEC, HBM_SPEC, HBM_SPEC, pl.BlockSpec(memory_space=pltpu.VMEM)),
        input_output_aliases={0: 2, 1: 3}, compiler_params=pltpu.CompilerParams(has_side_effects=DATAFLOW),
    )(pltpu.with_memory_space_constraint(g, pltpu.HBM), pltpu.with_memory_space_constraint(land, pltpu.HBM))


def _sibling_swap_wait(send_sem, recv_sem, g_thru, land_thru, after):
    half = land_thru.shape[1]

    def body(g_ref, land_ref, send_sem, recv_sem, after_ref, g_done, got_ref):
        x, y, c, _ = _place()
        theirs = pl.multiple_of((1 - c) * half, 8)
        cp = pltpu.make_async_remote_copy(src_ref=g_ref.at[:, pl.ds(theirs, half), :], dst_ref=land_ref,
                                          send_sem=send_sem, recv_sem=recv_sem, device_id=(x, y, 1 - c),
                                          device_id_type=MESH)
        cp.wait_send()
        cp.wait_recv()

    return pl.pallas_call(
        body, name="rs_swap_wait",
        out_shape=(pltpu.HBM(g_thru.shape, g_thru.dtype), pltpu.HBM(land_thru.shape, land_thru.dtype)),
        in_specs=(HBM_SPEC, HBM_SPEC, SEM_SPEC, SEM_SPEC, pl.BlockSpec(memory_space=pl.ANY)),
        out_specs=(HBM_SPEC, HBM_SPEC), input_output_aliases={0: 0, 1: 1},
        compiler_params=pltpu.CompilerParams(has_side_effects=DATAFLOW),
    )(g_thru, land_thru, send_sem, recv_sem, after)


def _pair_sum(g, other, c_arr, tag):
    half = other.shape[1]
    T = _flat_tile(half)

    def body(c_ref, g_ref, o_ref, s_ref):
        s_ref[...] = (g_ref[...] + o_ref[...]).astype(BF16)

    nb = half // T
    return pl.pallas_call(
        body, out_shape=jax.ShapeDtypeStruct((N_CHIPS, half, FLAT_W), BF16),
        grid_spec=pltpu.PrefetchScalarGridSpec(
            num_scalar_prefetch=1, grid=(N_CHIPS, nb),
            in_specs=[pl.BlockSpec((None, T, FLAT_W), lambda k, i, c: (k, c[0] * nb + i, 0)),
                      pl.BlockSpec((None, T, FLAT_W), lambda k, i, c: (k, i, 0))],
            out_specs=pl.BlockSpec((None, T, FLAT_W), lambda k, i, c: (k, i, 0))),
        compiler_params=_cparams(("parallel", "parallel")), name=f"rs_pair_sum_{tag}",
    )(c_arr, g, other)


def _chip_exchange(s, tag):
    def body(s_ref, o_ref, send_sems, recv_sems):
        x, y, c, chips = _place()
        cps = []
        for r, (cx, cy) in enumerate(chips):
            cps.append(pltpu.make_async_remote_copy(
                src_ref=s_ref.at[2 * cx + cy], dst_ref=o_ref.at[r], send_sem=send_sems.at[r],
                recv_sem=recv_sems.at[r], device_id=(cx, cy, c), device_id_type=MESH))
        for cp in cps:
            cp.start()
        for cp in cps:
            cp.wait()

    return pl.pallas_call(
        body, out_shape=jax.ShapeDtypeStruct((3,) + s.shape[1:], BF16), in_specs=[ANY], out_specs=ANY,
        scratch_shapes=[pltpu.SemaphoreType.DMA((3,)), pltpu.SemaphoreType.DMA((3,))],
        compiler_params=pltpu.CompilerParams(has_side_effects=True), name=f"rs_chip_exchange_{tag}",
    )(s)


def _chip_exchange_start(s, land):
    def body(s_ref, land_ref, send_sems, recv_sems, s_thru, land_thru, token):
        x, y, c, chips = _place()
        for r, (cx, cy) in enumerate(chips):
            pltpu.make_async_remote_copy(src_ref=s_ref.at[2 * cx + cy], dst_ref=land_ref.at[r], send_sem=send_sems.at[r],
                                         recv_sem=recv_sems.at[r], device_id=(cx, cy, c), device_id_type=MESH).start()
        token[...] = jnp.zeros_like(token)

    return pl.pallas_call(
        body, name="rs_exchange_start",
        out_shape=(pltpu.SemaphoreType.DMA((3,)), pltpu.SemaphoreType.DMA((3,)), pltpu.HBM(s.shape, s.dtype),
                   pltpu.HBM(land.shape, land.dtype), jax.ShapeDtypeStruct((8, LANES), F32)),
        in_specs=(HBM_SPEC, HBM_SPEC),
        out_specs=(SEM_SPEC, SEM_SPEC, HBM_SPEC, HBM_SPEC, pl.BlockSpec(memory_space=pltpu.VMEM)),
        input_output_aliases={0: 2, 1: 3}, compiler_params=pltpu.CompilerParams(has_side_effects=DATAFLOW),
    )(pltpu.with_memory_space_constraint(s, pltpu.HBM), pltpu.with_memory_space_constraint(land, pltpu.HBM))


def _chip_exchange_wait(send_sems, recv_sems, s_thru, land_thru, after):
    def body(s_ref, land_ref, send_sems, recv_sems, after_ref, s_done, got_ref):
        x, y, c, chips = _place()
        for r, (cx, cy) in enumerate(chips):
            cp = pltpu.make_async_remote_copy(src_ref=s_ref.at[2 * cx + cy], dst_ref=land_ref.at[r],
                                              send_sem=send_sems.at[r], recv_sem=recv_sems.at[r], device_id=(cx, cy, c),
                                              device_id_type=MESH)
            cp.wait_send()
            cp.wait_recv()

    return pl.pallas_call(
        body, name="rs_exchange_wait",
        out_shape=(pltpu.HBM(s_thru.shape, s_thru.dtype), pltpu.HBM(land_thru.shape, land_thru.dtype)),
        in_specs=(HBM_SPEC, HBM_SPEC, SEM_SPEC, SEM_SPEC, pl.BlockSpec(memory_space=pl.ANY)),
        out_specs=(HBM_SPEC, HBM_SPEC), input_output_aliases={0: 0, 1: 1},
        compiler_params=pltpu.CompilerParams(has_side_effects=DATAFLOW),
    )(s_thru, land_thru, send_sems, recv_sems, after)


def _chip_sum(s, r, k_arr, tag):
    half = s.shape[1]
    T = _flat_tile(half)

    def body(k_ref, s_ref, r_ref, o_ref):
        o_ref[...] = ((s_ref[...].astype(F32) + r_ref[0].astype(F32)) + r_ref[1].astype(F32)) + r_ref[2].astype(F32)

    return pl.pallas_call(
        body, out_shape=jax.ShapeDtypeStruct((half, FLAT_W), F32),
        grid_spec=pltpu.PrefetchScalarGridSpec(
            num_scalar_prefetch=1, grid=(half // T,),
            in_specs=[pl.BlockSpec((None, T, FLAT_W), lambda i, k: (k[0], i, 0)),
                      pl.BlockSpec((3, T, FLAT_W), lambda i, k: (0, i, 0))],
            out_specs=pl.BlockSpec((T, FLAT_W), lambda i, k: (i, 0))),
        compiler_params=_cparams(("parallel",)), name=f"rs_chip_sum_{tag}",
    )(k_arr, s, r)


def _sibling_send(t, tag):
    def body(t_ref, o_ref, send_sem, recv_sem):
        x, y, c, _ = _place()
        cp = pltpu.make_async_remote_copy(src_ref=t_ref, dst_ref=o_ref, send_sem=send_sem, recv_sem=recv_sem,
                                          device_id=(x, y, 1 - c), device_id_type=MESH)
        cp.start()
        cp.wait()

    return pl.pallas_call(
        body, out_shape=jax.ShapeDtypeStruct(t.shape, F32), in_specs=[ANY], out_specs=ANY,
        scratch_shapes=[pltpu.SemaphoreType.DMA, pltpu.SemaphoreType.DMA],
        compiler_params=pltpu.CompilerParams(has_side_effects=True), name=f"rs_sibling_send_{tag}",
    )(t)


def _allreduce_small(v):
    def body(v_ref, o_ref, buf, send_sems, recv_sems):
        x, y, c, _ = _place()
        me = 4 * x + 2 * y + c
        buf[me] = v_ref[...]
        cps = []
        for mask in range(1, 8):
            a, b, d = (mask >> 2) & 1, (mask >> 1) & 1, mask & 1
            peer = (x + a - 2 * a * x, y + b - 2 * b * y, c + d - 2 * d * c)
            cps.append(pltpu.make_async_remote_copy(
                src_ref=v_ref, dst_ref=buf.at[me], send_sem=send_sems.at[mask - 1], recv_sem=recv_sems.at[mask - 1],
                device_id=peer, device_id_type=MESH))
        for cp in cps:
            cp.start()
        for cp in cps:
            cp.wait()
        total = buf[0]
        for dev in range(1, 8):
            total = total + buf[dev]
        o_ref[...] = total

    vm = pl.BlockSpec(memory_space=pltpu.VMEM)
    return pl.pallas_call(
        body, out_shape=jax.ShapeDtypeStruct((SMALL_ROWS, 1024), F32), in_specs=[vm], out_specs=vm,
        scratch_shapes=[pltpu.VMEM((8, SMALL_ROWS, 1024), F32), pltpu.SemaphoreType.DMA((7,)),
                        pltpu.SemaphoreType.DMA((7,))],
        compiler_params=pltpu.CompilerParams(has_side_effects=True), name="allreduce_small",
    )(v)


def _col_to_row(t):
    return t.reshape(t.shape[0], 1, S)


def _residue_rows(t, d, inverse=False):
    if d == 1:
        return t
    shape = (d, S // d) if inverse else (S // d, d)
    return t.reshape(shape + t.shape[1:]).transpose(1, 0, 2).reshape(t.shape)


def _residue_vecs(t, d, inverse=False):
    if d == 1:
        return t
    shape = (d, S // d) if inverse else (S // d, d)
    return t.reshape((t.shape[0],) + shape).transpose(0, 2, 1).reshape(t.shape)


def _ffn_fwd(x, g, w_up, cw, cb, w_down, tag):
    (h,) = _rms_fwd(x, [g], f"{tag}_norm")
    u = _mm(h, w_up, mode="nn", tm=1024, tn=1408, tk=1024, o_split=2, name=f"{tag}_up")
    act = _ffn_act_fwd(u, cw, cb, f"{tag}_act")
    x_out = _mm(act, w_down, mode="nn", tm=1024, tn=512, tk=FF, res=x, name=f"{tag}_down")
    return x_out, (h, u, act)


def _ffn_bwd(x, g, w_up, cw, cb, w_down, saved, dx, dxb, tag):
    h, u, act = saved
    d_w_down = _mm(act, dxb, mode="tn", tm=1408, tn=512, tk=1024, name=f"{tag}_dwdown")
    dact = _mm(dxb, w_down, mode="nt", tm=1024, tn=1408, tk=1024, name=f"{tag}_dact")
    duc, dwb = _ffn_act_bwd(u, dact, cw, cb, f"{tag}_dgate")
    du = _ffn_conv_bwd(duc, cw, f"{tag}_dconv")
    d_w_up = _mm(h, du, mode="tn", tm=1024, tn=1408, tk=1024, b_split=2, name=f"{tag}_dwup")
    dh = _mm(du, w_up, mode="nt", tm=1024, tn=512, tk=1408, a_split=2, name=f"{tag}_dh")
    dx_new, dxb_new, (dg,) = _rms_bwd(x, dx, [(g, dh)], f"{tag}_dnorm")
    d_cw = dwb[:, 0:3, :].transpose(1, 0, 2).reshape(3, 2 * FF)
    d_cb = dwb[:, 3, :].reshape(2 * FF)
    return dx_new, dxb_new, dict(w_up=d_w_up, w_down=d_w_down, conv_w=d_cw, conv_b=d_cb, norm_g=dg.reshape(D))


def _local_step(x, target, p, late_weights, late_grads_ready, late_grads_continue):
    g = {}
    (h1,) = _rms_fwd(x, [p["mix_norm_g"][0]], "a_norm")
    w_qkv = p["a_w_in"][:, :QKV_W]
    w_f = jnp.pad(p["a_w_in"][:, QKV_W:], ((0, 0), (0, LANES - A_HEADS)))
    b_f = jnp.pad(p["a_b_f"].reshape(1, A_HEADS), ((0, 0), (0, LANES - A_HEADS)))
    qkv = _mm(h1, w_qkv, mode="nn", tm=1024, tn=512, tk=1024, out_dtype=BF16, name="a_qkv")
    pf = _mm(h1, w_f, mode="nn", tm=1024, tn=LANES, tk=1024, name="a_gate")
    aug_q, aug_k = _fgate_fwd(pf, b_f, "a_gate_scan")
    oa2, lse_a = _fox_pair_fwd(qkv, aug_q, aug_k, "a_attn")
    x1 = _mm(oa2, p["a_w_out"], mode="nn", tm=1024, tn=512, tk=1024, res=x, name="a_out")
    p = {**p, **late_weights(x1)}
    x2, ffn0 = _ffn_fwd(x1, p["ffn_norm_g"][0], p["ffn_w_up"][0], p["conv_w"][0], p["conv_b"][0], p["ffn_w_down"][0], "f0")
    hk, h3 = _rms_fwd(x2, [p["kv_norm_g"], p["mix_norm_g"][1]], "kv_b_norm")
    kvb = _mm(hk, p["w_kv"], mode="nn", tm=1024, tn=512, tk=1024, out_dtype=BF16, name="kv_proj")
    qb = _mm(h3, p["b_w_q"], mode="nn", tm=1024, tn=512, tk=1024, out_dtype=BF16, name="b_q")
    dil_in = []
    for gi, (_, d) in enumerate(B_GROUPS):
        if d == 1:
            dil_in.append((qb, kvb, kvb, gi * NPG, gi * NPG, (3 + gi) * NPG))
        else:
            qg = _residue_rows(qb[:, gi * GROUP_W:(gi + 1) * GROUP_W], d)
            kvg = _residue_rows(kvb.reshape(S, 2, 3, GROUP_W)[:, :, gi, :].reshape(S, 2 * GROUP_W), d)
            dil_in.append((qg, kvg, kvg, 0, 0, NPG))
    o_g, lse_g = [], []
    for gi, (_, d) in enumerate(B_GROUPS):
        qg, kg, vg, qoff, koff, voff = dil_in[gi]
        og, lg = _dil_pair_fwd(gi, qg, kg, vg, qoff, koff, voff, f"b_attn{gi}")
        o_g.append(_residue_rows(og, d, inverse=True))
        lse_g.append(_residue_vecs(lg, d, inverse=True))
    ob, ob2, lse_b = _dil_pair_merge(o_g, lse_g, "b_merge")
    x3 = _mm(ob2, p["b_w_out"], mode="nn", tm=1024, tn=512, tk=B_OUT_W, res=x2, name="b_out")
    x4, ffn1 = _ffn_fwd(x3, p["ffn_norm_g"][1], p["ffn_w_up"][1], p["conv_w"][1], p["conv_b"][1], p["ffn_w_down"][1], "f1")
    loss, dx, dxb, dg_final = _loss_head(x4, p["final_norm_g"], target, "loss_head")
    g["final_norm_g"] = dg_final.reshape(D)

    dx, dxb, gf1 = _ffn_bwd(x3, p["ffn_norm_g"][1], p["ffn_w_up"][1], p["conv_w"][1], p["conv_b"][1], p["ffn_w_down"][1],
                            ffn1, dx, dxb, "f1")
    g["b_w_out"] = _mm(ob2, dxb, mode="tn", tm=B_OUT_W, tn=512, tk=1024, name="b_dwout")
    dob = _mm(dxb, p["b_w_out"], mode="nt", tm=1024, tn=B_OUT_W, tk=1024, name="b_do")
    delta_b = _pair_rowdot(dob, ob, "b_delta")
    dob16 = dob.astype(BF16)
    stats_b = jnp.concatenate([lse_b, delta_b], axis=0)
    dq_g, dk_g, dv_g = [], [], []
    for gi, (_, d) in enumerate(B_GROUPS):
        qg, kg, vg, qoff, koff, voff = dil_in[gi]
        dog, stats_d = _residue_rows(dob16, d), _residue_vecs(stats_b, d)
        dqd = _dil_pair_dq(gi, qg, kg, vg, qoff, koff, voff, dog, stats_d, f"b_dq{gi}")
        dkd, dvd = _dil_pair_dkv(gi, qg, kg, vg, qoff, koff, voff, dog, _col_to_row(stats_d), f"b_dkv{gi}")
        dq_g.append(_residue_rows(dqd, d, inverse=True))
        dk_g.append(_residue_rows(dkd, d, inverse=True))
        dv_g.append(_residue_rows(dvd, d, inverse=True))
    dqb = jnp.concatenate(dq_g, axis=1)
    dkvb = jnp.concatenate(dk_g + dv_g, axis=1)
    g["b_w_q"] = _mm(h3, dqb, mode="tn", tm=1024, tn=512, tk=1024, name="b_dwq")
    dh3 = _mm(dqb, p["b_w_q"], mode="nt", tm=1024, tn=512, tk=B_Q_W, name="b_dh")
    g["w_kv"] = _mm(hk, dkvb, mode="tn", tm=1024, tn=512, tk=1024, name="kv_dw")
    dhk = _mm(dkvb, p["w_kv"], mode="nt", tm=1024, tn=512, tk=1536, name="kv_dh")
    dx, dxb, (dg_mix1, dg_kv) = _rms_bwd(x2, dx, [(p["mix_norm_g"][1], dh3), (p["kv_norm_g"], dhk)], "b_dnorm")
    g["kv_norm_g"] = dg_kv.reshape(D)
    dx, dxb, gf0 = _ffn_bwd(x1, p["ffn_norm_g"][0], p["ffn_w_up"][0], p["conv_w"][0], p["conv_b"][0], p["ffn_w_down"][0],
                            ffn0, dx, dxb, "f0")
    g["ffn_w_up"] = [gf0["w_up"], gf1["w_up"]]
    g["ffn_w_down"] = [gf0["w_down"], gf1["w_down"]]
    g["ffn_conv_w"] = jnp.stack([gf0["conv_w"], gf1["conv_w"]])
    token = late_grads_ready(g)
    a_w_out_t = p["a_w_out"] + token[0, 0].astype(BF16)
    g["a_w_out"] = _mm(oa2, dxb, mode="tn", tm=1024, tn=512, tk=1024, name="a_dwout")
    doa = _mm(dxb, a_w_out_t, mode="nt", tm=1024, tn=512, tk=1024, name="a_do")
    delta_a = _pair_rowdot(doa, oa2, "a_delta")
    token = late_grads_continue(delta_a)
    delta_row = _col_to_row(delta_a) + token[0, 0]
    dqa, dka, dva, dck, dcq = _fox_pair_bwd(qkv, doa, _col_to_row(lse_a), delta_row, aug_q, aug_k, "a_dattn")
    dqkv = jnp.concatenate([dqa, dka, dva], axis=1)
    pad_heads = lambda t: jnp.pad(t.reshape(A_HEADS, S).T, ((0, 0), (0, LANES - A_HEADS)))
    dpf, db_f = _fgate_bwd(pf, b_f, pad_heads(dck), pad_heads(dcq), "a_dgate_scan")
    g["a_b_f"] = db_f[:, :A_HEADS]
    d_w_qkv = _mm(h1, dqkv, mode="tn", tm=1024, tn=512, tk=1024, name="a_dwqkv")
    d_w_f = _mm(h1, dpf, mode="tn", tm=1024, tn=LANES, tk=1024, name="a_dwgate")
    g["a_w_in"] = jnp.concatenate([d_w_qkv, d_w_f[:, :A_HEADS]], axis=1)
    dh1 = _mm(dqkv, w_qkv, mode="nt", tm=1024, tn=512, tk=1536, name="a_dh")
    dh1 = _mm(dpf, w_f, mode="nt", tm=1024, tn=512, tk=LANES, res=dh1, name="a_dh_gate")
    dx, _, (dg_mix0,) = _rms_bwd(x, dx, [(p["mix_norm_g"][0], dh1)], "a_dnorm")

    g["mix_norm_g"] = jnp.stack([dg_mix0.reshape(D), dg_mix1.reshape(D)])
    g["ffn_norm_g"] = jnp.stack([gf0["norm_g"], gf1["norm_g"]])
    g["ffn_conv_b"] = jnp.stack([gf0["conv_b"], gf1["conv_b"]])
    return loss[0, 0], dx, g


_SHARD_SHAPES = {"a_w_in": (1, 1024, 772), "a_w_out": (1, 256, 1024), "b_w_q": (1, 1024, 384), "b_w_out": (1, 512, 256),
                 "w_kv": (1024, 768), "ffn_w_up": (2, 1024, 1408), "ffn_w_down": (2, 704, 1024), "ffn_conv_w": (2, 3, 1408)}
_SMALL = (("kv_norm_g", (1024,)), ("mix_norm_g", (2, 1024)), ("ffn_norm_g", (2, 1024)), ("final_norm_g", (1024,)),
          ("a_b_f", (1, 16)), ("ffn_conv_b", (2, 5632)))


def _unslabs(rows, L, R, C, rpad):
    nc = -(-C // FLAT_W)
    return rows.reshape(L, nc, rpad, FLAT_W).transpose(0, 2, 1, 3).reshape(L, rpad, nc * FLAT_W)[:, :R, :C]


_SEG_RT = {"ffn_w_down": 704, "a_w_in": 1024, "a_w_out": 256, "b_w_q": 1024, "b_w_out": 512, "w_kv": 1024,
           "ffn_w_up": 1024, "ffn_conv_w": 16}
_ROW_SHARDED = ("a_w_out", "ffn_w_down")


_LAYOUTS = {"early": ("a_w_in", "a_w_out"), "late": ("ffn_w_down", "b_w_q", "b_w_out", "w_kv", "ffn_w_up", "ffn_conv_w"),
            "grad_early": ("a_w_in", "a_w_out"),
            "grad_late": ("ffn_w_up", "ffn_w_down", "b_w_q", "w_kv", "b_w_out", "ffn_conv_w")}
_GRAD_ROWS = {"grad_early": 10240, "grad_late": 45056}


def _layout_rows(layout):
    used = sum(_seg_rows(*s) for s in _SEGS if s[0] in _LAYOUTS[layout])
    rows = _GRAD_ROWS.get(layout, used)
    assert rows >= used
    return rows


def _grad_layout(name):
    return "grad_early" if name in _LAYOUTS["grad_early"] else "grad_late"


def _seg(name, layout=None):
    layout = layout or _grad_layout(name)
    off = 0
    for s in sorted((s for s in _SEGS if s[0] in _LAYOUTS[layout]), key=lambda s: _LAYOUTS[layout].index(s[0])):
        _, L, R, C, rpad = s
        if layout in _GRAD_ROWS:
            per_layer = -(-C // FLAT_W) * rpad
            off = -(-off // per_layer) * per_layer
        if s[0] == name:
            rt = _SEG_RT[name]
            assert off % rt == 0 and rpad % rt == 0
            half = _layout_rows(layout) // 2
            assert off + _seg_rows(*s) <= 2 * half
            assert layout not in _GRAD_ROWS or half % rt == 0 or off + _seg_rows(*s) <= half
            return dict(L=L, R=R, C=C, rpad=rpad, nc=-(-C // FLAT_W), rt=rt, off=off, ni=rpad // rt, half=half)
        off += _seg_rows(*s)
    raise KeyError(name)


def _flat_block(sg, term=0):
    base = (sg["off"] + term * sg["L"] * sg["nc"] * sg["rpad"]) // sg["rt"]
    return lambda l, j, i: base + (l * sg["nc"] + j) * sg["ni"] + i


def _native3(t, name):
    sg = _seg(name)
    t = t.reshape(sg["L"], sg["R"], sg["C"])
    return jnp.pad(t, ((0, 0), (0, sg["rpad"] - sg["R"]), (0, 0))) if sg["rpad"] != sg["R"] else t


def _slab_pack(flat, t, name, layout, term=None):
    sg = _seg(name, layout)
    rt = sg["rt"]
    rb = _flat_block(sg, term or 0)

    def body(*refs):
        t_ref, o_ref = refs[-2], refs[-1]
        val = t_ref[...]
        o_ref[...] = val.astype(BF16) if term is None else _split3(val)[term]

    in_specs = [pl.BlockSpec((None, rt, FLAT_W), lambda l, j, i: (l, i, j))]
    args = [t]
    if flat is not None:
        in_specs, args = [ANY] + in_specs, [flat] + args
    return pl.pallas_call(
        body, out_shape=jax.ShapeDtypeStruct((_layout_rows(layout), FLAT_W), BF16), grid=(sg["L"], sg["nc"], sg["ni"]),
        in_specs=in_specs, out_specs=pl.BlockSpec((rt, FLAT_W), lambda l, j, i: (rb(l, j, i), 0)),
        input_output_aliases={0: 0} if flat is not None else {},
        compiler_params=_cparams(("parallel", "parallel", "parallel")), name=f"pack_{name}_{term or 0}",
    )(*args)


def _full_spec(sg, name):
    rt, nc, ni = sg["rt"], sg["nc"], sg["ni"]
    if name in _ROW_SHARDED:
        return (sg["L"], N_CHIPS * sg["R"], sg["C"]), pl.BlockSpec((None, rt, FLAT_W), lambda k, l, j, i: (l, k * ni + i, j))
    return ((sg["L"], sg["rpad"], N_CHIPS * nc * FLAT_W),
            pl.BlockSpec((None, rt, FLAT_W), lambda k, l, j, i: (l, i, k * nc + j)))


def _slab_unpack(gathered, slots, name, layout, own=None):
    sg = _seg(name, layout)
    rb = _flat_block(sg)
    shape, _ = _full_spec(sg, name)
    rt, nc, ni = sg["rt"], sg["nc"], sg["ni"]
    width = nc * FLAT_W
    last = gathered.shape[0] - 1

    def body(*refs):
        s_ref, o_ref = refs[0], refs[-1]
        is_own = s_ref[pl.program_id(0)] == OWN_SLOT
        for j in range(nc):
            val = refs[1 + j][...]
            if own is not None:
                val = jnp.where(is_own, refs[1 + nc + j][...], val)
            o_ref[:, j * FLAT_W:(j + 1) * FLAT_W] = val

    if name in _ROW_SHARDED:
        o_spec = pl.BlockSpec((None, rt, width), lambda k, l, i, s: (l, k * ni + i, 0))
    else:
        o_spec = pl.BlockSpec((None, rt, width), lambda k, l, i, s: (l, i, k))
    in_specs = [pl.BlockSpec((None, rt, FLAT_W), lambda k, l, i, s, j=j: (jnp.minimum(s[k], last), rb(l, j, i), 0))
                for j in range(nc)]
    args = [gathered] * nc
    if own is not None:
        in_specs += [pl.BlockSpec((rt, FLAT_W), lambda k, l, i, s, j=j: (rb(l, j, i), 0)) for j in range(nc)]
        args += [own] * nc
    return pl.pallas_call(
        body, out_shape=jax.ShapeDtypeStruct(shape, BF16),
        grid_spec=pltpu.PrefetchScalarGridSpec(num_scalar_prefetch=1, grid=(N_CHIPS, sg["L"], ni), in_specs=in_specs,
                                               out_specs=o_spec),
        compiler_params=_cparams(("parallel",) * 3), name=f"unpack_{name}",
    )(slots, *args)


def _slab_pack_grad(flat4, g, name, layer=None):
    sg = _seg(name)
    rows = _layout_rows(_grad_layout(name))
    shape, _ = _full_spec(sg, name)
    n_layers = sg["L"] if layer is None else 1
    assert g.shape == (n_layers,) + shape[1:], (name, g.shape, shape)
    rt, nc = sg["rt"], sg["nc"]
    assert sg["ni"] == 1 and sg["off"] % (nc * rt) == 0
    base = sg["off"] // (nc * rt) + (layer or 0)

    def body(*refs):
        g_ref, o_ref = refs[-2], refs[-1]
        for j in range(nc):
            o_ref[j * rt:(j + 1) * rt, :] = g_ref[:, j * FLAT_W:(j + 1) * FLAT_W]

    if name in _ROW_SHARDED:
        spec = pl.BlockSpec((None, rt, nc * FLAT_W), lambda k, l: (l, k, 0))
    else:
        spec = pl.BlockSpec((None, rt, nc * FLAT_W), lambda k, l: (l, 0, k))
    in_specs, args = [spec], [g]
    if flat4 is not None:
        in_specs, args = [pl.BlockSpec(memory_space=pl.ANY)] + in_specs, [flat4] + args
    return pl.pallas_call(
        body, out_shape=jax.ShapeDtypeStruct((N_CHIPS, rows, FLAT_W), F32), grid=(N_CHIPS, n_layers),
        in_specs=in_specs, out_specs=pl.BlockSpec((None, nc * rt, FLAT_W), lambda k, l: (k, base + l, 0)),
        input_output_aliases={0: 0} if flat4 is not None else {},
        compiler_params=_cparams(("parallel",) * 2), name=f"packgrad_{name}_{layer or 0}",
    )(*args)


def _adamw_shard(w, m, v, g_mine, g_other, c_arr, name):
    sg = _seg(name)
    rt = sg["rt"]
    rb = _flat_block(sg)
    per_half = sg["half"] // rt

    def half_of(l, j, i):
        return (rb(l, j, i) * rt) // sg["half"]

    def body(c_ref, w_ref, m_ref, v_ref, gm_ref, go_ref, g_ref, d_ref, mo_ref, vo_ref):
        is_mine = half_of(pl.program_id(0), pl.program_id(1), pl.program_id(2)) == c_ref[0]
        gv = jnp.where(is_mine, gm_ref[...], go_ref[...])
        g_ref[...] = gv
        d_ref[...], mo_ref[...], vo_ref[...] = _adam_update(w_ref[...], gv, m_ref[...], v_ref[...])

    nat = pl.BlockSpec((None, rt, FLAT_W), lambda l, j, i, c: (l, i, j))
    half = pl.BlockSpec((rt, FLAT_W), lambda l, j, i, c: (rb(l, j, i) - half_of(l, j, i) * per_half, 0))
    sds = jax.ShapeDtypeStruct(w.shape, F32)
    return pl.pallas_call(
        body, out_shape=[sds] * 4,
        grid_spec=pltpu.PrefetchScalarGridSpec(num_scalar_prefetch=1, grid=(sg["L"], sg["nc"], sg["ni"]),
                                               in_specs=[nat, nat, nat, half, half], out_specs=[nat] * 4),
        compiler_params=_cparams(("parallel", "parallel", "parallel")), name=f"adamw_{name}",
    )(c_arr, w, m, v, g_mine, g_other)


def _pack_small(vals, loss=None):
    parts = [vals[name].astype(F32).reshape(-1) for name, _ in _SMALL]
    if loss is not None:
        parts.append(loss.reshape(1))
    flat = jnp.concatenate(parts)
    return jnp.pad(flat, (0, SMALL_ROWS * 1024 - flat.shape[0])).reshape(SMALL_ROWS, 1024)


def _unpack_small(flat):
    flat = flat.reshape(-1)
    out = {}
    o = 0
    for name, shape in _SMALL:
        n = int(np.prod(shape))
        out[name] = flat[o:o + n].reshape(shape)
        o += n
    return out, flat[o]


_BIG = ("a_w_in", "a_w_out", "b_w_q", "b_w_out", "w_kv", "ffn_w_up", "ffn_w_down", "ffn_conv_w")
A_IN_PAD = 896


def _pack_weights(w, layout):
    flat = None
    for name in _LAYOUTS[layout]:
        t = _native3(w[name], name)
        for term in ((0, 1, 2) if name == "ffn_conv_w" else (None,)):
            flat = _slab_pack(flat, t, name, layout, term)
    return flat


def _early_weights(gathered, slots):
    a_in = _slab_unpack(gathered, slots, "a_w_in", "early")
    a_in = a_in.reshape(D, N_CHIPS, A_IN_PAD)[:, :, :772].reshape(D, N_CHIPS * 772)
    return dict(a_w_in=a_in, a_w_out=_slab_unpack(gathered, slots, "a_w_out", "early")[0])


def _late_weights(landed, slots, own):
    full = {name: _slab_unpack(landed, slots, name, "late", own) for name in _LAYOUTS["late"] if name != "ffn_conv_w"}
    sg = _seg("ffn_conv_w", "late")
    n1 = sg["nc"] * sg["rpad"]
    conv = slice(sg["off"], sg["off"] + CONV_TERMS * n1)
    conv_rows = jnp.concatenate([landed[:, conv], own[None, conv]], axis=0)
    per_chip = []
    for k in range(N_CHIPS):
        rows = lax.dynamic_index_in_dim(conv_rows, slots[k], axis=0, keepdims=False)
        terms = [_unslabs(rows[i * n1:(i + 1) * n1], 1, sg["R"], sg["C"], sg["rpad"]).astype(F32) for i in range(CONV_TERMS)]
        per_chip.append((terms[0] + terms[1]) + terms[2])
    cw = jnp.concatenate(per_chip, axis=2).reshape(2, 3, 2, FF).transpose(0, 2, 1, 3)
    return dict(b_w_q=full["b_w_q"][0], b_w_out=full["b_w_out"][0], w_kv=full["w_kv"][0], ffn_w_up=full["ffn_w_up"],
                ffn_w_down=full["ffn_w_down"], conv_w=cw)


def _shard_grads(g, layout):
    def full(name):
        if name == "a_w_in":
            a_in = jnp.pad(g[name].reshape(D, N_CHIPS, 772), ((0, 0), (0, 0), (0, A_IN_PAD - 772)))
            return a_in.reshape(1, D, N_CHIPS * A_IN_PAD)
        if name == "ffn_conv_w":
            sgc = _seg(name)
            return jnp.pad(g[name].reshape(1, sgc["R"], 2 * FF), ((0, 0), (0, sgc["rpad"] - sgc["R"]), (0, 0)))
        return g[name] if g[name].ndim == 3 else g[name][None]

    flat4 = None
    for name in _LAYOUTS[layout]:
        if isinstance(g[name], (list, tuple)):
            for layer, t in enumerate(g[name]):
                flat4 = _slab_pack_grad(flat4, t[None], name, layer)
        else:
            flat4 = _slab_pack_grad(flat4, full(name), name)
    return flat4


_WEIGHTS = ["a_w_in", "a_b_f", "a_w_out", "b_w_q", "b_w_out", "kv_norm_g", "w_kv", "mix_norm_g", "ffn_norm_g", "ffn_w_up",
            "ffn_conv_w", "ffn_conv_b", "ffn_w_down", "final_norm_g"]


def kernel(x, a_w_in, a_b_f, a_w_out, b_w_q, b_w_out, kv_norm_g, w_kv, mix_norm_g, ffn_norm_g, ffn_w_up, ffn_conv_w, ffn_conv_b, ffn_w_down, final_norm_g, loss_target, m_a_w_in, m_a_b_f, m_a_w_out, m_b_w_q, m_b_w_out, m_kv_norm_g, m_w_kv, m_mix_norm_g, m_ffn_norm_g, m_ffn_w_up, m_ffn_conv_w, m_ffn_conv_b, m_ffn_w_down, m_final_norm_g, v_a_w_in, v_a_b_f, v_a_w_out, v_b_w_q, v_b_w_out, v_kv_norm_g, v_w_kv, v_mix_norm_g, v_ffn_norm_g, v_ffn_w_up, v_ffn_conv_w, v_ffn_conv_b, v_ffn_w_down, v_final_norm_g):
    w = dict(a_w_in=a_w_in, a_b_f=a_b_f, a_w_out=a_w_out, b_w_q=b_w_q, b_w_out=b_w_out, kv_norm_g=kv_norm_g, w_kv=w_kv,
             mix_norm_g=mix_norm_g, ffn_norm_g=ffn_norm_g, ffn_w_up=ffn_w_up, ffn_conv_w=ffn_conv_w, ffn_conv_b=ffn_conv_b,
             ffn_w_down=ffn_w_down, final_norm_g=final_norm_g)
    m = dict(a_w_in=m_a_w_in, a_b_f=m_a_b_f, a_w_out=m_a_w_out, b_w_q=m_b_w_q, b_w_out=m_b_w_out, kv_norm_g=m_kv_norm_g,
             w_kv=m_w_kv, mix_norm_g=m_mix_norm_g, ffn_norm_g=m_ffn_norm_g, ffn_w_up=m_ffn_w_up, ffn_conv_w=m_ffn_conv_w,
             ffn_conv_b=m_ffn_conv_b, ffn_w_down=m_ffn_w_down, final_norm_g=m_final_norm_g)
    v = dict(a_w_in=v_a_w_in, a_b_f=v_a_b_f, a_w_out=v_a_w_out, b_w_q=v_b_w_q, b_w_out=v_b_w_out, kv_norm_g=v_kv_norm_g,
             w_kv=v_w_kv, mix_norm_g=v_mix_norm_g, ffn_norm_g=v_ffn_norm_g, ffn_w_up=v_ffn_w_up, ffn_conv_w=v_ffn_conv_w,
             ffn_conv_b=v_ffn_conv_b, ffn_w_down=v_ffn_w_down, final_norm_g=v_final_norm_g)

    c_arr = lax.axis_index("c").astype(jnp.int32).reshape(1)
    k_arr = (2 * lax.axis_index("x") + lax.axis_index("y")).astype(jnp.int32).reshape(1)
    xi, yi = lax.axis_index("x"), lax.axis_index("y")
    late_slots = jnp.stack([jnp.where(k == k_arr[0], OWN_SLOT, 2 * ((k & 1) ^ yi) + ((k >> 1) ^ xi) - 1)
                            for k in range(N_CHIPS)]).astype(jnp.int32)
    w_late = _pack_weights(w, "late")
    land = lax.empty((OWN_SLOT,) + w_late.shape, BF16)
    send_sems, recv_sems, w_thru, land_thru, token = _late_gather_start(w_late, land)
    w_early = _pack_weights(w, "early")
    early = _allgather_shards(w_early, _place_own(w_early, k_arr, "early_place_own"))
    p = _early_weights(early, jnp.arange(N_CHIPS, dtype=jnp.int32))
    cb = ffn_conv_b.reshape(2, 2, 1, FF)
    p.update(a_b_f=a_b_f, kv_norm_g=kv_norm_g, mix_norm_g=mix_norm_g + token[0, 0], ffn_norm_g=ffn_norm_g,
             final_norm_g=final_norm_g, conv_b=cb)

    def late_weights(after):
        own, landed = _late_gather_wait(send_sems, recv_sems, w_thru, land_thru, after)
        return _late_weights(landed, late_slots, own)

    started = {}

    def late_grads_ready(g_so_far):
        gflat = _shard_grads(g_so_far, "grad_late")
        land = lax.empty((N_CHIPS, gflat.shape[1] // 2, FLAT_W), F32)
        *handles, token = _sibling_swap_start(gflat, land)
        started["swap"] = handles
        return token

    def late_grads_continue(after):
        gflat, other = _sibling_swap_wait(*started["swap"], after)
        pair = _pair_sum(gflat, other, c_arr, "late")
        land = lax.empty((3,) + pair.shape[1:], BF16)
        *handles, token = _chip_exchange_start(pair, land)
        started["handles"] = handles
        return token

    loss_part, grad_x, g = _local_step(x[0], loss_target[0], p, late_weights, late_grads_ready, late_grads_continue)

    halves = {}
    pair, landed = _chip_exchange_wait(*started["handles"], grad_x)
    g_mine = _chip_sum(pair, landed, k_arr, "late")
    halves["grad_late"] = (g_mine, _sibling_send(g_mine, "late"))
    gflat = _shard_grads(g, "grad_early")
    pair = _pair_sum(gflat, _sibling_swap_half(gflat, "early"), c_arr, "early")
    g_mine = _chip_sum(pair, _chip_exchange(pair, "early"), k_arr, "early")
    halves["grad_early"] = (g_mine, _sibling_send(g_mine, "early"))
    small, loss = _unpack_small(_allreduce_small(_pack_small(g, loss_part)))

    big = [{}, {}, {}, {}]
    for name in _BIG:
        sg = _seg(name)
        g_mine, g_other = halves[_grad_layout(name)]
        res = _adamw_shard(_native3(w[name], name), _native3(m[name], name), _native3(v[name], name), g_mine, g_other,
                           c_arr, name)
        for store, t in zip(big, res):
            store[name] = t[:, :sg["R"], :].reshape(_SHARD_SHAPES[name])
    dws, mns, vns = _adamw(_pack_small(w), _pack_small(small), _pack_small(m), _pack_small(v), "adamw_small")
    sml = [small] + [_unpack_small(t)[0] for t in (dws, mns, vns)]
    outs = [loss, grad_x[None]]
    for b, s in zip(big, sml):
        outs += [b[n] if n in b else s[n] for n in _WEIGHTS]
    return tuple(outs)
```

```python
import numpy as np
import jax
import jax.numpy as jnp
from jax import lax
from jax.experimental import pallas as pl
from jax.experimental.pallas import tpu as pltpu

F32 = jnp.float32
BF16 = jnp.bfloat16
MESH = pl.DeviceIdType.MESH

S = 4096
D = 1024
A_HEADS = 16
HEAD_DIM = 64
QKV_W = 3 * A_HEADS * HEAD_DIM
B_GROUPS = ((128, 1), (512, 4), (2048, 16))
B_HPG = 8
B_Q_W = 3 * B_HPG * HEAD_DIM
B_OUT_W = B_HPG * HEAD_DIM
B_WIN = 128
FF = 2816
RMS_EPS = 1e-6
SCALE = HEAD_DIM ** -0.5
N_CHIPS = 4

ADAM_LR, ADAM_B1, ADAM_B2, ADAM_EPS, ADAM_WD, ADAM_STEP = 0.001, 0.9, 0.999, 1e-08, 0.01, 10

V7X_VMEM_LIMIT = 48 * 1024 * 1024
LANES = 128
NEG_INF = float("-inf")

FLAT_W = LANES
_SEGS = (("ffn_w_down", 2, 704, 1024, 704), ("a_w_in", 1, 1024, 772, 1024), ("a_w_out", 1, 256, 1024, 256),
         ("b_w_q", 1, 1024, 384, 1024), ("b_w_out", 1, 512, 256, 512), ("w_kv", 1, 1024, 768, 1024),
         ("ffn_w_up", 2, 1024, 1408, 1024), ("ffn_conv_w", 1, 6, 1408, 16))
CONV_TERMS = 3


def _seg_rows(name, L, R, C, rpad):
    return (CONV_TERMS if name == "ffn_conv_w" else 1) * L * (-(-C // FLAT_W)) * rpad


SMALL_ROWS = 24


def _cparams(sem=None, **kw):
    return pltpu.CompilerParams(dimension_semantics=sem, vmem_limit_bytes=V7X_VMEM_LIMIT, **kw)


_DN = {"nn": (((1,), (0,)), ((), ())), "nt": (((1,), (1,)), ((), ())), "tn": (((0,), (0,)), ((), ()))}


def _mm(a, b, *, mode, tm, tn, tk, name, out_dtype=F32, res=None, a_split=0, b_split=0, o_split=0):
    if mode == "tn":
        K = a.shape[0]
        M = a.shape[1]
    else:
        M = a.shape[-2]
        K = a.shape[-1] * (2 if a_split else 1)
    if mode == "nt":
        N = b.shape[0]
    else:
        N = b.shape[-1] * (2 if b_split else 1)
    assert M % tm == 0 and N % tn == 0 and K % tk == 0, (name, M, N, K, tm, tn, tk)
    nk = K // tk

    if mode == "tn":
        a_spec = pl.BlockSpec((tk, tm), lambda i, j, k: (k, i))
    elif a_split:
        a_spec = pl.BlockSpec((None, tm, tk), lambda i, j, k: (k // a_split, i, k % a_split))
    else:
        a_spec = pl.BlockSpec((tm, tk), lambda i, j, k: (i, k))
    if mode == "nt":
        b_spec = pl.BlockSpec((tn, tk), lambda i, j, k: (j, k))
    elif b_split:
        b_spec = pl.BlockSpec((None, tk, tn), lambda i, j, k: (j // b_split, k, j % b_split))
    else:
        b_spec = pl.BlockSpec((tk, tn), lambda i, j, k: (k, j))
    if o_split:
        o_spec = pl.BlockSpec((None, tm, tn), lambda i, j, k: (j // o_split, i, j % o_split))
        out_shape = jax.ShapeDtypeStruct((2, M, N // 2), out_dtype)
    else:
        o_spec = pl.BlockSpec((tm, tn), lambda i, j, k: (i, j))
        out_shape = jax.ShapeDtypeStruct((M, N), out_dtype)
    in_specs = [a_spec, b_spec]
    args = [a, b]
    if res is not None:
        in_specs.append(pl.BlockSpec((tm, tn), lambda i, j, k: (i, j)))
        args.append(res)

    def body(*refs):
        if res is not None:
            a_ref, b_ref, r_ref, o_ref = refs[:4]
        else:
            a_ref, b_ref, o_ref = refs[:3]
            r_ref = None
        p = lax.dot_general(a_ref[...].astype(BF16), b_ref[...].astype(BF16), _DN[mode], preferred_element_type=F32)

        def finish(r):
            if r_ref is not None:
                r = r + r_ref[...]
            o_ref[...] = r.astype(out_dtype)

        if nk == 1:
            finish(p)
        else:
            acc = refs[-1]
            k = pl.program_id(2)

            @pl.when(k == 0)
            def _():
                acc[...] = p

            @pl.when(k > 0)
            def _():
                acc[...] += p

            @pl.when(k == nk - 1)
            def _():
                finish(acc[...])

    return pl.pallas_call(
        body, out_shape=out_shape, grid=(M // tm, N // tn, nk), in_specs=in_specs, out_specs=o_spec,
        scratch_shapes=[pltpu.VMEM((tm, tn), F32)] if nk > 1 else [],
        compiler_params=_cparams(("parallel", "parallel", "arbitrary")), name=name,
    )(*args)


NORM_ROWS = 256


def _rms_fwd(x, gains, name):
    n = len(gains)

    def body(x_ref, *refs):
        xv = x_ref[...]
        y = xv * lax.rsqrt(jnp.mean(xv * xv, axis=-1, keepdims=True) + RMS_EPS)
        for g_ref, o_ref in zip(refs[:n], refs[n:]):
            o_ref[...] = (y * g_ref[...]).astype(BF16)

    row = pl.BlockSpec((NORM_ROWS, D), lambda i: (i, 0))
    return pl.pallas_call(
        body, out_shape=[jax.ShapeDtypeStruct((S, D), BF16)] * n, grid=(S // NORM_ROWS,),
        in_specs=[row] + [pl.BlockSpec((1, D), lambda i: (0, 0))] * n, out_specs=[row] * n,
        compiler_params=_cparams(("parallel",)), name=name,
    )(x, *[g.reshape(1, D) for g in gains])


def _rms_bwd(x, dres, pairs, name):
    n = len(pairs)

    def body(*refs):
        x_ref, dres_ref = refs[0], refs[1]
        g_refs = refs[2:2 + 2 * n:2]
        dh_refs = refs[3:3 + 2 * n:2]
        dx_ref, dxb_ref = refs[2 + 2 * n], refs[3 + 2 * n]
        dg_refs = refs[4 + 2 * n:]
        i = pl.program_id(0)
        xv = x_ref[...]
        r = lax.rsqrt(jnp.mean(xv * xv, axis=-1, keepdims=True) + RMS_EPS)
        y = xv * r
        dx = dres_ref[...]
        for g_ref, dh_ref, dg_ref in zip(g_refs, dh_refs, dg_refs):
            dh = dh_ref[...]
            dy = dh * g_ref[...]
            dx = dx + r * (dy - y * jnp.mean(dy * y, axis=-1, keepdims=True))
            part = jnp.sum(dh * y, axis=0, keepdims=True)

            @pl.when(i == 0)
            def _():
                dg_ref[...] = part

            @pl.when(i > 0)
            def _():
                dg_ref[...] += part

        dx_ref[...] = dx
        dxb_ref[...] = dx.astype(BF16)

    row = pl.BlockSpec((NORM_ROWS, D), lambda i: (i, 0))
    vec = pl.BlockSpec((1, D), lambda i: (0, 0))
    in_specs = [row, row]
    args = [x, dres]
    for g, dh in pairs:
        in_specs += [vec, row]
        args += [g.reshape(1, D), dh]
    outs = pl.pallas_call(
        body,
        out_shape=[jax.ShapeDtypeStruct((S, D), F32), jax.ShapeDtypeStruct((S, D), BF16)]
        + [jax.ShapeDtypeStruct((1, D), F32)] * n,
        grid=(S // NORM_ROWS,), in_specs=in_specs, out_specs=[row, row] + [vec] * n,
        compiler_params=_cparams(("arbitrary",)), name=name,
    )(*args)
    return outs[0], outs[1], list(outs[2:])


def _loss_head(x, g, target, name):
    def body(x_ref, g_ref, t_ref, loss_ref, dx_ref, dxb_ref, dg_ref):
        i = pl.program_id(0)
        xv = x_ref[...]
        gv = g_ref[...]
        r = lax.rsqrt(jnp.mean(xv * xv, axis=-1, keepdims=True) + RMS_EPS)
        y = xv * r
        err = y * gv - t_ref[...]
        lpart = jnp.broadcast_to(jnp.sum(err * err, keepdims=True) * (0.5 / D), (1, LANES))
        dh = err * (1.0 / D)
        dy = dh * gv
        dx = r * (dy - y * jnp.mean(dy * y, axis=-1, keepdims=True))
        part = jnp.sum(dh * y, axis=0, keepdims=True)

        @pl.when(i == 0)
        def _():
            dg_ref[...] = part
            loss_ref[...] = lpart

        @pl.when(i > 0)
        def _():
            dg_ref[...] += part
            loss_ref[...] += lpart

        dx_ref[...] = dx
        dxb_ref[...] = dx.astype(BF16)

    row = pl.BlockSpec((NORM_ROWS, D), lambda i: (i, 0))
    vec = pl.BlockSpec((1, D), lambda i: (0, 0))
    return pl.pallas_call(
        body,
        out_shape=[jax.ShapeDtypeStruct((1, LANES), F32), jax.ShapeDtypeStruct((S, D), F32),
                   jax.ShapeDtypeStruct((S, D), BF16), jax.ShapeDtypeStruct((1, D), F32)],
        grid=(S // NORM_ROWS,), in_specs=[row, vec, row],
        out_specs=[pl.BlockSpec((1, LANES), lambda i: (0, 0)), row, row, vec],
        compiler_params=_cparams(("arbitrary",)), name=name,
    )(x, g.reshape(1, D), target)


SCAN_ROWS = 256


def _split3(v):
    hi = v.astype(BF16)
    r1 = v - hi.astype(F32)
    mid = r1.astype(BF16)
    lo = (r1 - mid.astype(F32)).astype(BF16)
    return hi, mid, lo


def _tri_dot(tri, v):
    hi, mid, lo = _split3(v)
    dn = _DN["nn"]
    return (lax.dot_general(tri, hi, dn, preferred_element_type=F32)
            + lax.dot_general(tri, mid, dn, preferred_element_type=F32)
            + lax.dot_general(tri, lo, dn, preferred_element_type=F32))


def _log_sigmoid(z):
    return jnp.minimum(z, 0.0) - jnp.log(1.0 + jnp.exp(-jnp.abs(z)))


GATE_LANES = 6


def _gate_lane_tables():
    pq = np.zeros((3 * LANES, A_HEADS * HEAD_DIM), np.float32)
    pk = np.zeros((3 * LANES, A_HEADS * HEAD_DIM), np.float32)
    one_q = np.zeros((1, A_HEADS * HEAD_DIM), np.float32)
    one_k = np.zeros((1, A_HEADS * HEAD_DIM), np.float32)
    for h in range(A_HEADS):
        pos = (h // 2) * LANES + (HEAD_DIM if h % 2 == 0 else 0)
        for term in range(3):
            pq[term * LANES + h, pos + term] = 1.0
            pk[term * LANES + h, pos + 3 + term] = -1.0
        one_q[0, pos + 3:pos + GATE_LANES] = 1.0
        one_k[0, pos:pos + 3] = 1.0
    return jnp.asarray(pq, BF16), jnp.asarray(pk, BF16), jnp.asarray(one_q), jnp.asarray(one_k)


def _fgate_fwd(pf, bias, name):
    tri = jnp.tril(jnp.ones((SCAN_ROWS, SCAN_ROWS), F32)).astype(BF16)
    pq, pk, one_q, one_k = _gate_lane_tables()

    def body(pf_ref, b_ref, tri_ref, pq_ref, pk_ref, oq_ref, ok_ref, aq_ref, ak_ref, c_sc):
        carry = jnp.zeros((1, LANES), F32)
        for blk in range(S // SCAN_ROWS):
            rows = pl.ds(blk * SCAN_ROWS, SCAN_ROWS)
            lf = _log_sigmoid(pf_ref[rows, :] + b_ref[...])
            c_sc[...] = _tri_dot(tri_ref[...], lf) + carry
            carry = c_sc[pl.ds(SCAN_ROWS - 1, 1), :]
            terms = jnp.concatenate(_split3(c_sc[...]), axis=1)
            aq = lax.dot_general(terms, pq_ref[...], _DN["nn"], preferred_element_type=F32) + oq_ref[...]
            ak = lax.dot_general(terms, pk_ref[...], _DN["nn"], preferred_element_type=F32) + ok_ref[...]
            aq_ref[rows, :] = aq.astype(BF16)
            ak_ref[rows, :] = ak.astype(BF16)

    wide = jax.ShapeDtypeStruct((S, A_HEADS * HEAD_DIM), BF16)
    return pl.pallas_call(
        body, out_shape=[wide, wide], scratch_shapes=[pltpu.VMEM((SCAN_ROWS, LANES), F32)],
        compiler_params=_cparams(), name=name,
    )(pf, bias, tri, pq, pk, one_q, one_k)


def _fgate_bwd(pf, bias, dc_key, dc_query, name):
    triu = jnp.triu(jnp.ones((SCAN_ROWS, SCAN_ROWS), F32)).astype(BF16)

    def body(pf_ref, b_ref, dck_ref, dcq_ref, tri_ref, dpf_ref, db_ref, dlf_ref):
        carry = jnp.zeros((1, LANES), F32)
        db = jnp.zeros((1, LANES), F32)
        lane = lax.broadcasted_iota(jnp.int32, (SCAN_ROWS, LANES), 1)
        for blk in reversed(range(S // SCAN_ROWS)):
            rows = pl.ds(blk * SCAN_ROWS, SCAN_ROWS)
            dc = dck_ref[rows, :] + dcq_ref[rows, :]
            dlf_ref[rows, :] = _tri_dot(tri_ref[...], dc) + carry
            carry = dlf_ref[pl.ds(blk * SCAN_ROWS, 1), :]
            z = pf_ref[rows, :] + b_ref[...]
            e = jnp.exp(-jnp.abs(z))
            sig_neg = jnp.where(z >= 0.0, e, 1.0) / (1.0 + e)
            dz = jnp.where(lane < A_HEADS, dlf_ref[rows, :] * sig_neg, 0.0)
            dpf_ref[rows, :] = dz.astype(BF16)
            db = db + jnp.sum(dz, axis=0, keepdims=True)
        db_ref[...] = db

    return pl.pallas_call(
        body, out_shape=[jax.ShapeDtypeStruct((S, LANES), BF16), jax.ShapeDtypeStruct((1, LANES), F32)],
        scratch_shapes=[pltpu.VMEM((S, LANES), F32)],
        compiler_params=_cparams(), name=name,
    )(pf, bias, dc_key, dc_query, triu)


FOX_T = 512


def _first_head(shape):
    return lax.broadcasted_iota(jnp.int32, shape, len(shape) - 1) < HEAD_DIM


def _each_head(x, lo):
    zero = jnp.zeros_like(x)
    return jnp.where(lo, x, zero), jnp.where(lo, zero, x)


def _fox_pair_fwd(qkv, aug_q, aug_k, name):
    T = FOX_T
    nq = S // T
    NP = A_HEADS // 2

    def body(q_ref, k_ref, v_ref, aq_ref, ak_ref, o_ref, lse_ref, m_sc, l_sc, acc_sc):
        i = pl.program_id(1)
        j = pl.program_id(2)
        lo = _first_head((T, LANES))

        @pl.when(j == 0)
        def _():
            m_sc[...] = jnp.full((2, T, LANES), NEG_INF, F32)
            l_sc[...] = jnp.zeros((2, T, LANES), F32)
            acc_sc[...] = jnp.zeros((T, LANES), F32)

        def step(diagonal):
            qs = q_ref[...] * jnp.asarray(SCALE, BF16)
            aq, ak, kv = aq_ref[...], ak_ref[...], k_ref[...]
            q2 = (jnp.where(lo, qs, aq), jnp.where(lo, aq, qs))
            k2 = (jnp.where(lo, kv, ak), jnp.where(lo, ak, kv))
            if diagonal:
                causal = lax.broadcasted_iota(jnp.int32, (T, T), 0) >= lax.broadcasted_iota(jnp.int32, (T, T), 1)
            pv, alphas = None, []
            for h, vh in enumerate(_each_head(v_ref[...], lo)):
                s = lax.dot_general(q2[h], k2[h], _DN["nt"], preferred_element_type=F32)
                if diagonal:
                    s = jnp.where(causal, s, NEG_INF)
                m_prev = m_sc[h]
                m_new = jnp.maximum(m_prev, jnp.max(s, axis=1, keepdims=True))
                alpha = jnp.exp(m_prev - m_new)
                p = jnp.exp(s - jnp.tile(m_new, (1, T // LANES)))
                l_sc[h] = alpha * l_sc[h] + jnp.sum(p, axis=1, keepdims=True)
                m_sc[h] = m_new
                d = lax.dot_general(p.astype(BF16), vh, _DN["nn"], preferred_element_type=F32)
                pv = d if pv is None else pv + d
                alphas.append(alpha)
            acc_sc[...] = jnp.where(lo, alphas[0], alphas[1]) * acc_sc[...] + pv

        @pl.when(j < i)
        def _():
            step(False)

        @pl.when(j == i)
        def _():
            step(True)
            o_ref[...] = (acc_sc[...] * jnp.where(lo, 1.0 / l_sc[0], 1.0 / l_sc[1])).astype(BF16)
            for h in range(2):
                lse_ref[h] = (m_sc[h] + jnp.log(l_sc[h]))[:, 0:1]

    qs_ = pl.BlockSpec((T, LANES), lambda p, i, j: (i, p))
    ks = pl.BlockSpec((T, LANES), lambda p, i, j: (jnp.minimum(i, j), NP + p))
    vs = pl.BlockSpec((T, LANES), lambda p, i, j: (jnp.minimum(i, j), 2 * NP + p))
    aks = pl.BlockSpec((T, LANES), lambda p, i, j: (jnp.minimum(i, j), p))
    col = pl.BlockSpec((2, T, 1), lambda p, i, j: (p, i, 0))
    return pl.pallas_call(
        body, out_shape=[jax.ShapeDtypeStruct((S, A_HEADS * HEAD_DIM), BF16), jax.ShapeDtypeStruct((A_HEADS, S, 1), F32)],
        grid=(NP, nq, nq), in_specs=[qs_, ks, vs, qs_, aks], out_specs=[qs_, col],
        scratch_shapes=[pltpu.VMEM((2, T, LANES), F32), pltpu.VMEM((2, T, LANES), F32), pltpu.VMEM((T, LANES), F32)],
        compiler_params=_cparams(("parallel", "parallel", "arbitrary")), name=name,
    )(qkv, qkv, qkv, aug_q, aug_k)


def _fox_pair_bwd(qkv, do, lse_row, delta_row, aug_q, aug_k, name):
    T = FOX_T
    nq = S // T
    NP = A_HEADS // 2

    def body(q_ref, k_ref, v_ref, do_ref, lse_ref, dl_ref, aq_ref, ak_ref, dq_ref, dk_ref, dv_ref, dc_ref, dcq_ref,
             dq_sc, dk_sc, dv_sc, dc_sc):
        j = pl.program_id(1)
        i = pl.program_id(2)
        lo = _first_head((T, LANES))

        @pl.when(jnp.logical_and(j == 0, i == 0))
        def _():
            dq_sc[...] = jnp.zeros((S, LANES), F32)
            dcq_ref[...] = jnp.zeros((2, nq, 1, T), F32)

        @pl.when(i == j)
        def _():
            dk_sc[...] = jnp.zeros((T, LANES), F32)
            dv_sc[...] = jnp.zeros((T, LANES), F32)
            dc_sc[...] = jnp.zeros((2, T, 1), F32)

        def step(diagonal):
            qv = q_ref[...]
            kv = k_ref[...]
            dov = do_ref[...].astype(BF16)
            qs = qv * jnp.asarray(SCALE, BF16)
            aq, ak = aq_ref[...], ak_ref[...]
            q2 = (jnp.where(lo, qs, aq), jnp.where(lo, aq, qs))
            k2 = (jnp.where(lo, kv, ak), jnp.where(lo, ak, kv))
            if diagonal:
                causal = lax.broadcasted_iota(jnp.int32, (T, T), 1) >= lax.broadcasted_iota(jnp.int32, (T, T), 0)
            dv = dk = dq = None
            for h, (kh, vh, qh, doh) in enumerate(zip(_each_head(kv, lo), _each_head(v_ref[...], lo),
                                                      _each_head(qv, lo), _each_head(dov, lo))):
                st = lax.dot_general(k2[h], q2[h], _DN["nt"], preferred_element_type=F32)
                if diagonal:
                    st = jnp.where(causal, st, NEG_INF)
                pt = jnp.exp(st - lse_ref[h])
                d = lax.dot_general(pt.astype(BF16), doh, _DN["nn"], preferred_element_type=F32)
                dv = d if dv is None else dv + d
                dpt = lax.dot_general(vh, dov, _DN["nt"], preferred_element_type=F32)
                dst = pt * (dpt - dl_ref[h])
                dc_sc[h] -= jnp.sum(dst, axis=1, keepdims=True)
                dcq_ref[h, i] += jnp.sum(dst, axis=0, keepdims=True)
                dsb = (dst * SCALE).astype(BF16)
                d = lax.dot_general(dsb, qh, _DN["nn"], preferred_element_type=F32)
                dk = d if dk is None else dk + d
                d = lax.dot_general(dsb, kh, _DN["tn"], preferred_element_type=F32)
                dq = d if dq is None else dq + d
            dv_sc[...] += dv
            dk_sc[...] += dk
            rows = pl.ds(pl.multiple_of(i * T, T), T)
            dq_sc[rows, :] += dq

        @pl.when(i > j)
        def _():
            step(False)

        @pl.when(i == j)
        def _():
            step(True)

        @pl.when(i == nq - 1)
        def _():
            dk_ref[...] = dk_sc[...].astype(BF16)
            dv_ref[...] = dv_sc[...].astype(BF16)
            dc_ref[...] = dc_sc[...]

        @pl.when(jnp.logical_and(j == nq - 1, i == nq - 1))
        def _():
            dq_ref[...] = dq_sc[...].astype(BF16)

    qs = pl.BlockSpec((T, LANES), lambda p, j, i: (jnp.maximum(i, j), p))
    qrow = pl.BlockSpec((2, 1, T), lambda p, j, i: (p, 0, jnp.maximum(i, j)))
    ks = pl.BlockSpec((T, LANES), lambda p, j, i: (j, NP + p))
    vs = pl.BlockSpec((T, LANES), lambda p, j, i: (j, 2 * NP + p))
    kout = pl.BlockSpec((T, LANES), lambda p, j, i: (j, p))
    kcol = pl.BlockSpec((2, T, 1), lambda p, j, i: (p, j, 0))
    dqs = pl.BlockSpec((S, LANES), lambda p, j, i: (0, p))
    dcqs = pl.BlockSpec((2, nq, 1, T), lambda p, j, i: (p, 0, 0, 0))
    wide = jax.ShapeDtypeStruct((S, A_HEADS * HEAD_DIM), BF16)
    return pl.pallas_call(
        body,
        out_shape=[wide, wide, wide, jax.ShapeDtypeStruct((A_HEADS, S, 1), F32),
                   jax.ShapeDtypeStruct((A_HEADS, nq, 1, T), F32)],
        grid=(NP, nq, nq), in_specs=[qs, ks, vs, qs, qrow, qrow, qs, kout], out_specs=[dqs, kout, kout, kcol, dcqs],
        scratch_shapes=[pltpu.VMEM((S, LANES), F32), pltpu.VMEM((T, LANES), F32), pltpu.VMEM((T, LANES), F32),
                        pltpu.VMEM((2, T, 1), F32)],
        compiler_params=_cparams(("parallel", "arbitrary", "arbitrary")), name=name,
    )(qkv, qkv, qkv, do, lse_row, delta_row, aug_q, aug_k)


def _pair_rowdot(a, b, name):
    n = a.shape[1] // HEAD_DIM
    T = 1024

    def body(a_ref, b_ref, o_ref):
        prod = a_ref[...].astype(F32) * b_ref[...].astype(F32)
        lo = _first_head(prod.shape)
        o_ref[0] = jnp.sum(jnp.where(lo, prod, 0.0), axis=1, keepdims=True)
        o_ref[1] = jnp.sum(jnp.where(lo, 0.0, prod), axis=1, keepdims=True)

    blk = pl.BlockSpec((T, LANES), lambda p, i: (i, p))
    return pl.pallas_call(
        body, out_shape=jax.ShapeDtypeStruct((n, S, 1), F32), grid=(n // 2, S // T), in_specs=[blk, blk],
        out_specs=pl.BlockSpec((2, T, 1), lambda p, i: (p, i, 0)),
        compiler_params=_cparams(("parallel", "parallel")), name=name,
    )(a, b)


W = B_WIN
N_HG = 3 * B_HPG
N_BLK = S // W


def _dil_tables():
    slopes = np.exp2((-8.0 * np.arange(1, N_HG + 1, dtype=np.float32) / N_HG).astype(np.float32)).astype(np.float32)
    dil = np.repeat(np.array([d for _, d in B_GROUPS], np.float32), B_HPG)
    coef = (slopes * dil).astype(np.float32)
    nbs = np.repeat(np.array([S // d // W for _, d in B_GROUPS], np.int32), B_HPG)
    return jnp.asarray(coef), jnp.asarray(nbs)


DIL_SUB = 8
DIL_ROWS = DIL_SUB * W
DIL_STEPS = S // DIL_ROWS


def _dil_bias(coef, transposed):
    row = lax.broadcasted_iota(jnp.int32, (W, 2 * W), 0)
    col = lax.broadcasted_iota(jnp.int32, (W, 2 * W), 1)
    dist = (col - row) if transposed else (row + W - col)
    valid = jnp.logical_and(dist >= 0, dist <= W)
    return jnp.where(valid, -coef * dist.astype(F32), NEG_INF), col


NPG = B_HPG // 2
GROUP_W = B_HPG * HEAD_DIM


def _dil_pair_specs(qoff, koff, voff):
    prev_blk = lambda n: jnp.maximum(n * DIL_SUB - 1, 0)
    next_blk = lambda n: jnp.minimum((n + 1) * DIL_SUB, N_BLK - 1)
    return dict(
        o=pl.BlockSpec((DIL_ROWS, LANES), lambda h, n: (n, h)),
        o_next=pl.BlockSpec((W, LANES), lambda h, n: (next_blk(n), h)),
        q=pl.BlockSpec((DIL_ROWS, LANES), lambda h, n: (n, qoff + h)),
        q_next=pl.BlockSpec((W, LANES), lambda h, n: (next_blk(n), qoff + h)),
        k=pl.BlockSpec((DIL_ROWS, LANES), lambda h, n: (n, koff + h)),
        k_prev=pl.BlockSpec((W, LANES), lambda h, n: (prev_blk(n), koff + h)),
        v=pl.BlockSpec((DIL_ROWS, LANES), lambda h, n: (n, voff + h)),
        v_prev=pl.BlockSpec((W, LANES), lambda h, n: (prev_blk(n), voff + h)),
        col=pl.BlockSpec((2, DIL_ROWS, 1), lambda h, n: (h, n, 0)),
        col2=pl.BlockSpec((2, DIL_ROWS, 1), lambda h, n: (NPG + h, n, 0)),
        row=pl.BlockSpec((2, 1, DIL_ROWS), lambda h, n: (h, 0, n)),
        row_next=pl.BlockSpec((2, 1, W), lambda h, n: (h, 0, next_blk(n))),
        row2=pl.BlockSpec((2, 1, DIL_ROWS), lambda h, n: (NPG + h, 0, n)),
        row2_next=pl.BlockSpec((2, 1, W), lambda h, n: (NPG + h, 0, next_blk(n))),
        smem=pl.BlockSpec(memory_space=pltpu.SMEM))


def _dil_pair_fwd(g, q, k, v, qoff, koff, voff, name):
    coef_t, nbs_t = _dil_tables()

    def body(coef_ref, nbs_ref, q_ref, kh_ref, k_ref, vh_ref, v_ref, o_ref, lse_ref, kf, vf):
        hp = pl.program_id(0)
        n = pl.program_id(1)
        nbs = nbs_ref[B_HPG * g + 2 * hp]
        kf[0:W, :] = kh_ref[...]
        kf[W:, :] = k_ref[...]
        vf[0:W, :] = vh_ref[...]
        vf[W:, :] = v_ref[...]
        biases = [_dil_bias(coef_ref[B_HPG * g + 2 * hp + h], False) for h in range(2)]
        col = biases[0][1]
        lo = _first_head((W, LANES))
        lo2 = _first_head((2 * W, LANES))
        for b in range(DIL_SUB):
            first = lax.rem(n * DIL_SUB + b, nbs) == 0
            rows = slice(b * W, (b + 1) * W)
            both = slice(b * W, (b + 2) * W)
            qv = q_ref[rows, :]
            acc, inv = None, []
            for h, (kh, vh) in enumerate(zip(_each_head(kf[both, :], lo2), _each_head(vf[both, :], lo2))):
                s = lax.dot_general(qv, kh, _DN["nt"], preferred_element_type=F32) * SCALE + biases[h][0]
                s = jnp.where(jnp.logical_and(first, col < W), NEG_INF, s)
                m = jnp.max(s, axis=1, keepdims=True)
                p = jnp.exp(s - m)
                l = jnp.sum(p, axis=1, keepdims=True)
                d = lax.dot_general(p.astype(BF16), vh, _DN["nn"], preferred_element_type=F32)
                acc = d if acc is None else acc + d
                inv.append(1.0 / l)
                lse_ref[h, rows, :] = m + jnp.log(l)
            o_ref[rows, :] = acc * jnp.where(lo, inv[0], inv[1])

    sp = _dil_pair_specs(qoff, koff, voff)
    return pl.pallas_call(
        body, out_shape=[jax.ShapeDtypeStruct((S, GROUP_W), F32), jax.ShapeDtypeStruct((B_HPG, S, 1), F32)],
        grid=(NPG, DIL_STEPS), in_specs=[sp["smem"], sp["smem"], sp["q"], sp["k_prev"], sp["k"], sp["v_prev"], sp["v"]],
        out_specs=[sp["o"], sp["col"]], scratch_shapes=[pltpu.VMEM((DIL_ROWS + W, LANES), BF16)] * 2,
        compiler_params=_cparams(("parallel", "parallel")), name=name,
    )(coef_t, nbs_t, q, k, k, v, v)


def _dil_pair_merge(os, lses, name):
    T = 1024

    def body(o0_ref, o1_ref, o2_ref, l0_ref, l1_ref, l2_ref, om_ref, omb_ref, l_ref):
        lo = _first_head((T, LANES))
        weights = []
        for h in range(2):
            l0, l1, l2 = l0_ref[h], l1_ref[h], l2_ref[h]
            m = jnp.maximum(jnp.maximum(l0, l1), l2)
            e0, e1, e2 = jnp.exp(l0 - m), jnp.exp(l1 - m), jnp.exp(l2 - m)
            den = e0 + e1 + e2
            weights.append((e0 / den, e1 / den, e2 / den))
            l_ref[h] = m + jnp.log(den)
        om = (jnp.where(lo, weights[0][0], weights[1][0]) * o0_ref[...]
              + jnp.where(lo, weights[0][1], weights[1][1]) * o1_ref[...]
              + jnp.where(lo, weights[0][2], weights[1][2]) * o2_ref[...])
        om_ref[...] = om
        omb_ref[...] = om.astype(BF16)

    ob = pl.BlockSpec((T, LANES), lambda p, i: (i, p))
    lb = pl.BlockSpec((2, T, 1), lambda p, i: (p, i, 0))
    return pl.pallas_call(
        body,
        out_shape=[jax.ShapeDtypeStruct((S, B_OUT_W), F32), jax.ShapeDtypeStruct((S, B_OUT_W), BF16),
                   jax.ShapeDtypeStruct((B_HPG, S, 1), F32)],
        grid=(NPG, S // T), in_specs=[ob] * 3 + [lb] * 3, out_specs=[ob, ob, lb],
        compiler_params=_cparams(("parallel", "parallel")), name=name,
    )(*os, *lses)


def _dil_pair_dq(g, q, k, v, qoff, koff, voff, do, stats, name):
    coef_t, nbs_t = _dil_tables()

    def body(coef_ref, nbs_ref, q_ref, kh_ref, k_ref, vh_ref, v_ref, do_ref, l_ref, d_ref, dq_ref, kf, vf):
        hp = pl.program_id(0)
        n = pl.program_id(1)
        nbs = nbs_ref[B_HPG * g + 2 * hp]
        kf[0:W, :] = kh_ref[...]
        kf[W:, :] = k_ref[...]
        vf[0:W, :] = vh_ref[...]
        vf[W:, :] = v_ref[...]
        biases = [_dil_bias(coef_ref[B_HPG * g + 2 * hp + h], False) for h in range(2)]
        col = biases[0][1]
        lo2 = _first_head((2 * W, LANES))
        for b in range(DIL_SUB):
            first = lax.rem(n * DIL_SUB + b, nbs) == 0
            rows = slice(b * W, (b + 1) * W)
            both = slice(b * W, (b + 2) * W)
            qv = q_ref[rows, :]
            dov = do_ref[rows, :]
            acc = None
            for h, (kh, vh) in enumerate(zip(_each_head(kf[both, :], lo2), _each_head(vf[both, :], lo2))):
                s = lax.dot_general(qv, kh, _DN["nt"], preferred_element_type=F32) * SCALE + biases[h][0]
                s = jnp.where(jnp.logical_and(first, col < W), NEG_INF, s)
                p = jnp.exp(s - l_ref[h, rows, :])
                dp = lax.dot_general(dov, vh, _DN["nt"], preferred_element_type=F32)
                ds = (p * (dp - d_ref[h, rows, :]) * SCALE).astype(BF16)
                d = lax.dot_general(ds, kh, _DN["nn"], preferred_element_type=F32)
                acc = d if acc is None else acc + d
            dq_ref[rows, :] = acc.astype(BF16)

    sp = _dil_pair_specs(qoff, koff, voff)
    return pl.pallas_call(
        body, out_shape=jax.ShapeDtypeStruct((S, GROUP_W), BF16), grid=(NPG, DIL_STEPS),
        in_specs=[sp["smem"], sp["smem"], sp["q"], sp["k_prev"], sp["k"], sp["v_prev"], sp["v"], sp["o"], sp["col"],
                  sp["col2"]],
        out_specs=sp["o"], scratch_shapes=[pltpu.VMEM((DIL_ROWS + W, LANES), BF16)] * 2,
        compiler_params=_cparams(("parallel", "parallel")), name=name,
    )(coef_t, nbs_t, q, k, k, v, v, do, stats, stats)


def _dil_pair_dkv(g, q, k, v, qoff, koff, voff, do, stats, name):
    coef_t, nbs_t = _dil_tables()

    def body(coef_ref, nbs_ref, k_ref, v_ref, q_ref, qn_ref, do_ref, don_ref, l_ref, ln_ref, d_ref, dn_ref,
             dk_ref, dv_ref, qf, dof, lf, df):
        hp = pl.program_id(0)
        n = pl.program_id(1)
        nbs = nbs_ref[B_HPG * g + 2 * hp]
        qf[0:DIL_ROWS, :] = q_ref[...]
        qf[DIL_ROWS:, :] = qn_ref[...]
        dof[0:DIL_ROWS, :] = do_ref[...]
        dof[DIL_ROWS:, :] = don_ref[...]
        lf[:, :, 0:DIL_ROWS] = l_ref[...]
        lf[:, :, DIL_ROWS:] = ln_ref[...]
        df[:, :, 0:DIL_ROWS] = d_ref[...]
        df[:, :, DIL_ROWS:] = dn_ref[...]
        biases = [_dil_bias(coef_ref[B_HPG * g + 2 * hp + h], True) for h in range(2)]
        col = biases[0][1]
        lo = _first_head((W, LANES))
        lo2 = _first_head((2 * W, LANES))
        for b in range(DIL_SUB):
            no_next = lax.rem(n * DIL_SUB + b + 1, nbs) == 0
            rows = slice(b * W, (b + 1) * W)
            both = slice(b * W, (b + 2) * W)
            dd = dof[both, :]
            dk = dv = None
            for h, (kh, vh, qh, ddh) in enumerate(zip(_each_head(k_ref[rows, :], lo), _each_head(v_ref[rows, :], lo),
                                                      _each_head(qf[both, :], lo2), _each_head(dd, lo2))):
                st = lax.dot_general(kh, qh, _DN["nt"], preferred_element_type=F32) * SCALE + biases[h][0]
                st = jnp.where(jnp.logical_and(no_next, col >= W), NEG_INF, st)
                pt = jnp.exp(st - lf[h, :, both])
                d = lax.dot_general(pt.astype(BF16), ddh, _DN["nn"], preferred_element_type=F32)
                dv = d if dv is None else dv + d
                dpt = lax.dot_general(vh, dd, _DN["nt"], preferred_element_type=F32)
                dst = (pt * (dpt - df[h, :, both]) * SCALE).astype(BF16)
                d = lax.dot_general(dst, qh, _DN["nn"], preferred_element_type=F32)
                dk = d if dk is None else dk + d
            dk_ref[rows, :] = dk.astype(BF16)
            dv_ref[rows, :] = dv.astype(BF16)

    sp = _dil_pair_specs(qoff, koff, voff)
    wide = jax.ShapeDtypeStruct((S, GROUP_W), BF16)
    return pl.pallas_call(
        body, out_shape=[wide, wide], grid=(NPG, DIL_STEPS),
        in_specs=[sp["smem"], sp["smem"], sp["k"], sp["v"], sp["q"], sp["q_next"], sp["o"], sp["o_next"], sp["row"],
                  sp["row_next"], sp["row2"], sp["row2_next"]],
        out_specs=[sp["o"], sp["o"]],
        scratch_shapes=[pltpu.VMEM((DIL_ROWS + W, LANES), BF16)] * 2 + [pltpu.VMEM((2, 1, DIL_ROWS + W), F32)] * 2,
        compiler_params=_cparams(("parallel", "parallel")), name=name,
    )(coef_t, nbs_t, k, v, q, q, do, do, stats, stats, stats, stats)


FFN_ROWS = 512
FFN_COLS = 256
HALO = 8


def _shifted(u, halo, back):
    T = u.shape[0]
    rows = lax.broadcasted_iota(jnp.int32, u.shape, 0)
    if back:
        s1 = jnp.where(rows == 0, halo[HALO - 1:HALO, :], pltpu.roll(u, 1, 0))
        s2 = jnp.where(rows == 0, halo[HALO - 2:HALO - 1, :],
                       jnp.where(rows == 1, halo[HALO - 1:HALO, :], pltpu.roll(u, 2, 0)))
    else:
        s1 = jnp.where(rows == T - 1, halo[0:1, :], pltpu.roll(u, T - 1, 0))
        s2 = jnp.where(rows == T - 1, halo[1:2, :],
                       jnp.where(rows == T - 2, halo[0:1, :], pltpu.roll(u, T - 2, 0)))
    return s1, s2


def _conv_parts(u_ref, h_ref, w_ref, b_ref, first):
    out = []
    for p in range(2):
        u = u_ref[p]
        halo = jnp.where(first, 0.0, h_ref[p])
        u1, u2 = _shifted(u, halo, True)
        w = w_ref[p]
        out.append((w[0:1, :] * u2 + w[1:2, :] * u1 + w[2:3, :] * u + b_ref[p], u1, u2, u))
    return out


def _ffn_specs():
    T, C = FFN_ROWS, FFN_COLS
    blk = pl.BlockSpec((2, T, C), lambda j, i: (0, i, j))
    prev = pl.BlockSpec((2, HALO, C), lambda j, i: (0, jnp.maximum(i * (T // HALO) - 1, 0), j))
    nxt = pl.BlockSpec((2, HALO, C), lambda j, i: (0, jnp.minimum((i + 1) * (T // HALO), S // HALO - 1), j))
    wsp = pl.BlockSpec((2, 3, C), lambda j, i: (0, 0, j))
    bsp = pl.BlockSpec((2, 1, C), lambda j, i: (0, 0, j))
    one = pl.BlockSpec((T, C), lambda j, i: (i, j))
    return blk, prev, nxt, wsp, bsp, one


def _ffn_act_fwd(u, w, b, name):
    blk, prev, _, wsp, bsp, one = _ffn_specs()

    def body(u_ref, h_ref, w_ref, b_ref, o_ref):
        (a, _, _, _), (g, _, _, _) = _conv_parts(u_ref, h_ref, w_ref, b_ref, pl.program_id(1) == 0)
        o_ref[...] = (g / (1.0 + jnp.exp(-g)) * a).astype(BF16)

    return pl.pallas_call(
        body, out_shape=jax.ShapeDtypeStruct((S, FF), BF16), grid=(FF // FFN_COLS, S // FFN_ROWS),
        in_specs=[blk, prev, wsp, bsp], out_specs=one,
        compiler_params=_cparams(("parallel", "parallel")), name=name,
    )(u, u, w, b)


def _ffn_act_bwd(u, dact, w, b, name):
    blk, prev, _, wsp, bsp, one = _ffn_specs()

    def body(u_ref, h_ref, da_ref, w_ref, b_ref, duc_ref, dwb_ref):
        i = pl.program_id(1)
        (a, a1, a2, a0), (g, g1, g2, g0) = _conv_parts(u_ref, h_ref, w_ref, b_ref, i == 0)
        dact_v = da_ref[...]
        sg = 1.0 / (1.0 + jnp.exp(-g))
        d_a = dact_v * (g * sg)
        d_g = dact_v * a * (sg * (1.0 + g * (1.0 - sg)))
        duc_ref[0] = d_a
        duc_ref[1] = d_g

        @pl.when(i == 0)
        def _():
            dwb_ref[...] = jnp.zeros(dwb_ref.shape, F32)

        for p, (d, s2, s1, s0) in enumerate(((d_a, a2, a1, a0), (d_g, g2, g1, g0))):
            dwb_ref[p, 0:1, :] += jnp.sum(d * s2, axis=0, keepdims=True)
            dwb_ref[p, 1:2, :] += jnp.sum(d * s1, axis=0, keepdims=True)
            dwb_ref[p, 2:3, :] += jnp.sum(d * s0, axis=0, keepdims=True)
            dwb_ref[p, 3:4, :] += jnp.sum(d, axis=0, keepdims=True)

    return pl.pallas_call(
        body, out_shape=[jax.ShapeDtypeStruct((2, S, FF), F32), jax.ShapeDtypeStruct((2, 8, FF), F32)],
        grid=(FF // FFN_COLS, S // FFN_ROWS), in_specs=[blk, prev, one, wsp, bsp],
        out_specs=[blk, pl.BlockSpec((2, 8, FFN_COLS), lambda j, i: (0, 0, j))],
        compiler_params=_cparams(("parallel", "arbitrary")), name=name,
    )(u, u, dact, w, b)


def _ffn_conv_bwd(duc, w, name):
    blk, _, nxt, wsp, _, _ = _ffn_specs()
    last = S // FFN_ROWS - 1

    def body(d_ref, h_ref, w_ref, du_ref):
        is_last = pl.program_id(1) == last
        for p in range(2):
            d = d_ref[p]
            halo = jnp.where(is_last, 0.0, h_ref[p])
            d1, d2 = _shifted(d, halo, False)
            wv = w_ref[p]
            du_ref[p] = (wv[2:3, :] * d + wv[1:2, :] * d1 + wv[0:1, :] * d2).astype(BF16)

    return pl.pallas_call(
        body, out_shape=jax.ShapeDtypeStruct((2, S, FF), BF16), grid=(FF // FFN_COLS, S // FFN_ROWS),
        in_specs=[blk, nxt, wsp], out_specs=blk,
        compiler_params=_cparams(("parallel", "parallel")), name=name,
    )(duc, duc, w)


def _adam_update(w, gv, m, v):
    c1 = 1.0 / (1.0 - ADAM_B1 ** ADAM_STEP)
    c2 = 1.0 / (1.0 - ADAM_B2 ** ADAM_STEP)
    mn = ADAM_B1 * m + (1.0 - ADAM_B1) * gv
    vn = ADAM_B2 * v + (1.0 - ADAM_B2) * (gv * gv)
    return -ADAM_LR * ((mn * c1) / (jnp.sqrt(vn * c2) + ADAM_EPS) + ADAM_WD * w), mn, vn


def _adamw(w, g, m, v, name):
    rows = w.shape[0]
    T = 8
    for cand in (256, 128, 64, 32, 16, 8):
        if rows % cand == 0:
            T = cand
            break

    def body(w_ref, g_ref, m_ref, v_ref, d_ref, mo_ref, vo_ref):
        d_ref[...], mo_ref[...], vo_ref[...] = _adam_update(w_ref[...], g_ref[...], m_ref[...], v_ref[...])

    blk = pl.BlockSpec((T, w.shape[1]), lambda i: (i, 0))
    sds = jax.ShapeDtypeStruct(w.shape, F32)
    return pl.pallas_call(
        body, out_shape=[sds, sds, sds], grid=(rows // T,), in_specs=[blk] * 4, out_specs=[blk] * 3,
        compiler_params=_cparams(("parallel",)), name=name,
    )(w, g, m, v)


ANY = pl.BlockSpec(memory_space=pl.ANY)


def _place():
    x, y, c = lax.axis_index("x"), lax.axis_index("y"), lax.axis_index("c")
    chips = [(1 - x, y), (x, 1 - y), (1 - x, 1 - y)]
    return x, y, c, chips


def _place_own(w, slot_arr, name):
    rows = w.shape[0]
    T = 16
    for cand in (2048, 1024, 512, 256, 128, 64, 32, 16):
        if rows % cand == 0:
            T = cand
            break

    def body(k_ref, w_ref, o_ref):
        o_ref[...] = w_ref[...]

    return pl.pallas_call(
        body, out_shape=jax.ShapeDtypeStruct((N_CHIPS, rows, FLAT_W), w.dtype),
        grid_spec=pltpu.PrefetchScalarGridSpec(
            num_scalar_prefetch=1, grid=(rows // T,),
            in_specs=[pl.BlockSpec((T, FLAT_W), lambda i, k: (i, 0))],
            out_specs=pl.BlockSpec((None, T, FLAT_W), lambda i, k: (k[0], i, 0))),
        compiler_params=_cparams(("parallel",)), name=name,
    )(slot_arr, w)


def _allgather_shards(w, buf):
    half_rows = w.shape[0] // 2
    assert half_rows % 16 == 0

    def body(w_ref, buf_ref, g_ref, send_sems, recv_sems):
        x, y, c, chips = _place()
        myk = 2 * x + y
        sibling = (x, y, 1 - c)
        h0 = pl.multiple_of(c * half_rows, 16)
        h1 = pl.multiple_of((1 - c) * half_rows, 16)

        def half(k, start):
            return g_ref.at[k, pl.ds(start, half_rows), :]

        def rcopy(sem, src, dst, to):
            return pltpu.make_async_remote_copy(src_ref=src, dst_ref=dst, send_sem=send_sems.at[sem],
                                                recv_sem=recv_sems.at[sem], device_id=to, device_id_type=MESH)

        ici = [rcopy(r, w_ref.at[pl.ds(h0, half_rows), :], half(myk, h0), (*chip, c)) for r, chip in enumerate(chips)]
        for cp in ici:
            cp.start()
        ks = [2 * cx + cy for cx, cy in chips]
        fwd = [rcopy(3 + r, half(ks[r], h0), half(ks[r], h0), sibling) for r in range(3)]
        for r in range(3):
            rcopy(r, half(ks[r], h0), half(ks[r], h0), (*chips[r], c)).wait_recv()
            fwd[r].start()
        for r in range(3):
            rcopy(3 + r, half(ks[r], h1), half(ks[r], h1), sibling).wait_recv()
        for cp in ici + fwd:
            cp.wait_send()

    return pl.pallas_call(
        body, out_shape=jax.ShapeDtypeStruct(buf.shape, w.dtype), in_specs=[ANY, ANY], out_specs=ANY,
        scratch_shapes=[pltpu.SemaphoreType.DMA((6,)), pltpu.SemaphoreType.DMA((6,))],
        input_output_aliases={1: 0},
        compiler_params=pltpu.CompilerParams(has_side_effects=True), name="allgather_shards",
    )(w, buf)


HBM_SPEC = pl.BlockSpec(memory_space=pltpu.HBM)
SEM_SPEC = pl.BlockSpec(memory_space=pltpu.SEMAPHORE)
DATAFLOW = pltpu.SideEffectType.DATAFLOW_SIDE_EFFECTING
OWN_SLOT = 3


def _late_gather_start(w, land):
    def body(w_ref, land_ref, send_sems, recv_sems, w_thru, land_thru, token):
        x, y, c, chips = _place()
        for r, chip in enumerate(chips):
            pltpu.make_async_remote_copy(src_ref=w_ref, dst_ref=land_ref.at[r], send_sem=send_sems.at[r],
                                         recv_sem=recv_sems.at[r], device_id=(*chip, c), device_id_type=MESH).start()
        token[...] = jnp.zeros_like(token)

    return pl.pallas_call(
        body, name="late_gather_start",
        out_shape=(pltpu.SemaphoreType.DMA((3,)), pltpu.SemaphoreType.DMA((3,)), pltpu.HBM(w.shape, w.dtype),
                   pltpu.HBM(land.shape, land.dtype), jax.ShapeDtypeStruct((8, LANES), F32)),
        in_specs=(HBM_SPEC, HBM_SPEC),
        out_specs=(SEM_SPEC, SEM_SPEC, HBM_SPEC, HBM_SPEC, pl.BlockSpec(memory_space=pltpu.VMEM)),
        input_output_aliases={0: 2, 1: 3}, compiler_params=pltpu.CompilerParams(has_side_effects=DATAFLOW),
    )(pltpu.with_memory_space_constraint(w, pltpu.HBM), pltpu.with_memory_space_constraint(land, pltpu.HBM))


def _late_gather_wait(send_sems, recv_sems, w_thru, land_thru, after):
    def body(w_ref, land_ref, send_sems, recv_sems, after_ref, w_dead, got_ref):
        x, y, c, chips = _place()
        for r, chip in enumerate(chips):
            cp = pltpu.make_async_remote_copy(src_ref=w_ref, dst_ref=land_ref.at[r], send_sem=send_sems.at[r],
                                              recv_sem=recv_sems.at[r], device_id=(*chip, c), device_id_type=MESH)
            cp.wait_send()
            cp.wait_recv()

    return pl.pallas_call(
        body, name="late_gather_wait",
        out_shape=(pltpu.HBM(w_thru.shape, w_thru.dtype), pltpu.HBM(land_thru.shape, land_thru.dtype)),
        in_specs=(HBM_SPEC, HBM_SPEC, SEM_SPEC, SEM_SPEC, pl.BlockSpec(memory_space=pl.ANY)),
        out_specs=(HBM_SPEC, HBM_SPEC), input_output_aliases={0: 0, 1: 1},
        compiler_params=pltpu.CompilerParams(has_side_effects=DATAFLOW),
    )(w_thru, land_thru, send_sems, recv_sems, after)


def _flat_tile(rows):
    return next(t for t in (2048, 1024, 512, 256, 128, 64, 32, 16) if rows % t == 0)


def _sibling_swap_half(g, tag):
    half = g.shape[1] // 2

    def body(g_ref, o_ref, send_sem, recv_sem):
        x, y, c, _ = _place()
        theirs = pl.multiple_of((1 - c) * half, 8)
        cp = pltpu.make_async_remote_copy(src_ref=g_ref.at[:, pl.ds(theirs, half), :], dst_ref=o_ref,
                                          send_sem=send_sem, recv_sem=recv_sem, device_id=(x, y, 1 - c),
                                          device_id_type=MESH)
        cp.start()
        cp.wait()

    return pl.pallas_call(
        body, out_shape=jax.ShapeDtypeStruct((N_CHIPS, half, FLAT_W), F32), in_specs=[ANY], out_specs=ANY,
        scratch_shapes=[pltpu.SemaphoreType.DMA, pltpu.SemaphoreType.DMA],
        compiler_params=pltpu.CompilerParams(has_side_effects=True), name=f"rs_sibling_swap_{tag}",
    )(g)


def _sibling_swap_start(g, land):
    half = g.shape[1] // 2

    def body(g_ref, land_ref, send_sem, recv_sem, g_thru, land_thru, token):
        x, y, c, _ = _place()
        theirs = pl.multiple_of((1 - c) * half, 8)
        pltpu.make_async_remote_copy(src_ref=g_ref.at[:, pl.ds(theirs, half), :], dst_ref=land_ref, send_sem=send_sem,
                                     recv_sem=recv_sem, device_id=(x, y, 1 - c), device_id_type=MESH).start()
        token[...] = jnp.zeros_like(token)

    return pl.pallas_call(
        body, name="rs_swap_start",
        out_shape=(pltpu.SemaphoreType.DMA(()), pltpu.SemaphoreType.DMA(()), pltpu.HBM(g.shape, g.dtype),
                   pltpu.HBM(land.shape, land.dtype), jax.ShapeDtypeStruct((8, LANES), F32)),
        in_specs=(HBM_SPEC, HBM_SPEC),
        out_specs=(SEM_SPEC, SEM_SPEC, HBM_SPEC, HBM_SPEC, pl.BlockSpec(memory_space=pltpu.VMEM)),
        input_output_aliases={0: 2, 1: 3}, compiler_params=pltpu.CompilerParams(has_side_effects=DATAFLOW),
    )(pltpu.with_memory_space_constraint(g, pltpu.HBM), pltpu.with_memory_space_constraint(land, pltpu.HBM))


def _sibling_swap_wait(send_sem, recv_sem, g_thru, land_thru, after):
    half = land_thru.shape[1]

    def body(g_ref, land_ref, send_sem, recv_sem, after_ref, g_done, got_ref):
        x, y, c, _ = _place()
        theirs = pl.multiple_of((1 - c) * half, 8)
        cp = pltpu.make_async_remote_copy(src_ref=g_ref.at[:, pl.ds(theirs, half), :], dst_ref=land_ref,
                                          send_sem=send_sem, recv_sem=recv_sem, device_id=(x, y, 1 - c),
                                          device_id_type=MESH)
        cp.wait_send()
        cp.wait_recv()

    return pl.pallas_call(
        body, name="rs_swap_wait",
        out_shape=(pltpu.HBM(g_thru.shape, g_thru.dtype), pltpu.HBM(land_thru.shape, land_thru.dtype)),
        in_specs=(HBM_SPEC, HBM_SPEC, SEM_SPEC, SEM_SPEC, pl.BlockSpec(memory_space=pl.ANY)),
        out_specs=(HBM_SPEC, HBM_SPEC), input_output_aliases={0: 0, 1: 1},
        compiler_params=pltpu.CompilerParams(has_side_effects=DATAFLOW),
    )(g_thru, land_thru, send_sem, recv_sem, after)


def _pair_sum(g, other, c_arr, tag):
    half = other.shape[1]
    T = _flat_tile(half)

    def body(c_ref, g_ref, o_ref, s_ref):
        s_ref[...] = (g_ref[...] + o_ref[...]).astype(BF16)

    nb = half // T
    return pl.pallas_call(
        body, out_shape=jax.ShapeDtypeStruct((N_CHIPS, half, FLAT_W), BF16),
        grid_spec=pltpu.PrefetchScalarGridSpec(
            num_scalar_prefetch=1, grid=(N_CHIPS, nb),
            in_specs=[pl.BlockSpec((None, T, FLAT_W), lambda k, i, c: (k, c[0] * nb + i, 0)),
                      pl.BlockSpec((None, T, FLAT_W), lambda k, i, c: (k, i, 0))],
            out_specs=pl.BlockSpec((None, T, FLAT_W), lambda k, i, c: (k, i, 0))),
        compiler_params=_cparams(("parallel", "parallel")), name=f"rs_pair_sum_{tag}",
    )(c_arr, g, other)


def _chip_exchange_start(s, land, tag):
    def body(s_ref, land_ref, send_sems, recv_sems, s_thru, land_thru, token):
        x, y, c, chips = _place()
        for r, (cx, cy) in enumerate(chips):
            pltpu.make_async_remote_copy(src_ref=s_ref.at[2 * cx + cy], dst_ref=land_ref.at[r], send_sem=send_sems.at[r],
                                         recv_sem=recv_sems.at[r], device_id=(cx, cy, c), device_id_type=MESH).start()
        token[...] = jnp.zeros_like(token)

    return pl.pallas_call(
        body, name=f"rs_exchange_start_{tag}",
        out_shape=(pltpu.SemaphoreType.DMA((3,)), pltpu.SemaphoreType.DMA((3,)), pltpu.HBM(s.shape, s.dtype),
                   pltpu.HBM(land.shape, land.dtype), jax.ShapeDtypeStruct((8, LANES), F32)),
        in_specs=(HBM_SPEC, HBM_SPEC),
        out_specs=(SEM_SPEC, SEM_SPEC, HBM_SPEC, HBM_SPEC, pl.BlockSpec(memory_space=pltpu.VMEM)),
        input_output_aliases={0: 2, 1: 3}, compiler_params=pltpu.CompilerParams(has_side_effects=DATAFLOW),
    )(pltpu.with_memory_space_constraint(s, pltpu.HBM), pltpu.with_memory_space_constraint(land, pltpu.HBM))


def _chip_exchange_wait(send_sems, recv_sems, s_thru, land_thru, after, tag):
    def body(s_ref, land_ref, send_sems, recv_sems, after_ref, s_done, got_ref):
        x, y, c, chips = _place()
        for r, (cx, cy) in enumerate(chips):
            cp = pltpu.make_async_remote_copy(src_ref=s_ref.at[2 * cx + cy], dst_ref=land_ref.at[r],
                                              send_sem=send_sems.at[r], recv_sem=recv_sems.at[r], device_id=(cx, cy, c),
                                              device_id_type=MESH)
            cp.wait_send()
            cp.wait_recv()

    return pl.pallas_call(
        body, name=f"rs_exchange_wait_{tag}",
        out_shape=(pltpu.HBM(s_thru.shape, s_thru.dtype), pltpu.HBM(land_thru.shape, land_thru.dtype)),
        in_specs=(HBM_SPEC, HBM_SPEC, SEM_SPEC, SEM_SPEC, pl.BlockSpec(memory_space=pl.ANY)),
        out_specs=(HBM_SPEC, HBM_SPEC), input_output_aliases={0: 0, 1: 1},
        compiler_params=pltpu.CompilerParams(has_side_effects=DATAFLOW),
    )(s_thru, land_thru, send_sems, recv_sems, after)


def _chip_sum(s, r, k_arr, tag):
    half = s.shape[1]
    T = _flat_tile(half)

    def body(k_ref, s_ref, r_ref, o_ref):
        o_ref[...] = ((s_ref[...].astype(F32) + r_ref[0].astype(F32)) + r_ref[1].astype(F32)) + r_ref[2].astype(F32)

    return pl.pallas_call(
        body, out_shape=jax.ShapeDtypeStruct((half, FLAT_W), F32),
        grid_spec=pltpu.PrefetchScalarGridSpec(
            num_scalar_prefetch=1, grid=(half // T,),
            in_specs=[pl.BlockSpec((None, T, FLAT_W), lambda i, k: (k[0], i, 0)),
                      pl.BlockSpec((3, T, FLAT_W), lambda i, k: (0, i, 0))],
            out_specs=pl.BlockSpec((T, FLAT_W), lambda i, k: (i, 0))),
        compiler_params=_cparams(("parallel",)), name=f"rs_chip_sum_{tag}",
    )(k_arr, s, r)


def _sibling_send(t, tag):
    def body(t_ref, o_ref, send_sem, recv_sem):
        x, y, c, _ = _place()
        cp = pltpu.make_async_remote_copy(src_ref=t_ref, dst_ref=o_ref, send_sem=send_sem, recv_sem=recv_sem,
                                          device_id=(x, y, 1 - c), device_id_type=MESH)
        cp.start()
        cp.wait()

    return pl.pallas_call(
        body, out_shape=jax.ShapeDtypeStruct(t.shape, F32), in_specs=[ANY], out_specs=ANY,
        scratch_shapes=[pltpu.SemaphoreType.DMA, pltpu.SemaphoreType.DMA],
        compiler_params=pltpu.CompilerParams(has_side_effects=True), name=f"rs_sibling_send_{tag}",
    )(t)


def _allreduce_small(v):
    def body(v_ref, o_ref, buf, send_sems, recv_sems):
        x, y, c, _ = _place()
        me = 4 * x + 2 * y + c
        buf[me] = v_ref[...]
        cps = []
        for mask in range(1, 8):
            a, b, d = (mask >> 2) & 1, (mask >> 1) & 1, mask & 1
            peer = (x + a - 2 * a * x, y + b - 2 * b * y, c + d - 2 * d * c)
            cps.append(pltpu.make_async_remote_copy(
                src_ref=v_ref, dst_ref=buf.at[me], send_sem=send_sems.at[mask - 1], recv_sem=recv_sems.at[mask - 1],
                device_id=peer, device_id_type=MESH))
        for cp in cps:
            cp.start()
        for cp in cps:
            cp.wait()
        total = buf[0]
        for dev in range(1, 8):
            total = total + buf[dev]
        o_ref[...] = total

    vm = pl.BlockSpec(memory_space=pltpu.VMEM)
    return pl.pallas_call(
        body, out_shape=jax.ShapeDtypeStruct((SMALL_ROWS, 1024), F32), in_specs=[vm], out_specs=vm,
        scratch_shapes=[pltpu.VMEM((8, SMALL_ROWS, 1024), F32), pltpu.SemaphoreType.DMA((7,)),
                        pltpu.SemaphoreType.DMA((7,))],
        compiler_params=pltpu.CompilerParams(has_side_effects=True), name="allreduce_small",
    )(v)


def _col_to_row(t):
    return t.reshape(t.shape[0], 1, S)


def _residue_rows(t, d, inverse=False):
    if d == 1:
        return t
    shape = (d, S // d) if inverse else (S // d, d)
    return t.reshape(shape + t.shape[1:]).transpose(1, 0, 2).reshape(t.shape)


def _residue_vecs(t, d, inverse=False):
    if d == 1:
        return t
    shape = (d, S // d) if inverse else (S // d, d)
    return t.reshape((t.shape[0],) + shape).transpose(0, 2, 1).reshape(t.shape)


def _ffn_fwd(x, g, w_up, cw, cb, w_down, tag):
    (h,) = _rms_fwd(x, [g], f"{tag}_norm")
    u = _mm(h, w_up, mode="nn", tm=1024, tn=1408, tk=1024, o_split=2, name=f"{tag}_up")
    act = _ffn_act_fwd(u, cw, cb, f"{tag}_act")
    x_out = _mm(act, w_down, mode="nn", tm=1024, tn=512, tk=FF, res=x, name=f"{tag}_down")
    return x_out, (h, u, act)


def _ffn_bwd(x, g, w_up, cw, cb, w_down, saved, dx, dxb, tag):
    h, u, act = saved
    d_w_down = _mm(act, dxb, mode="tn", tm=1408, tn=512, tk=1024, name=f"{tag}_dwdown")
    dact = _mm(dxb, w_down, mode="nt", tm=1024, tn=1408, tk=1024, name=f"{tag}_dact")
    duc, dwb = _ffn_act_bwd(u, dact, cw, cb, f"{tag}_dgate")
    du = _ffn_conv_bwd(duc, cw, f"{tag}_dconv")
    d_w_up = _mm(h, du, mode="tn", tm=1024, tn=1408, tk=1024, b_split=2, name=f"{tag}_dwup")
    dh = _mm(du, w_up, mode="nt", tm=1024, tn=512, tk=1408, a_split=2, name=f"{tag}_dh")
    dx_new, dxb_new, (dg,) = _rms_bwd(x, dx, [(g, dh)], f"{tag}_dnorm")
    d_cw = dwb[:, 0:3, :].transpose(1, 0, 2).reshape(3, 2 * FF)
    d_cb = dwb[:, 3, :].reshape(2 * FF)
    return dx_new, dxb_new, dict(w_up=d_w_up, w_down=d_w_down, conv_w=d_cw, conv_b=d_cb, norm_g=dg.reshape(D))


def _local_step(x, target, p, late_weights, late_grads_ready, late_grads_continue):
    g = {}
    (h1,) = _rms_fwd(x, [p["mix_norm_g"][0]], "a_norm")
    w_qkv = p["a_w_in"][:, :QKV_W]
    w_f = jnp.pad(p["a_w_in"][:, QKV_W:], ((0, 0), (0, LANES - A_HEADS)))
    b_f = jnp.pad(p["a_b_f"].reshape(1, A_HEADS), ((0, 0), (0, LANES - A_HEADS)))
    qkv = _mm(h1, w_qkv, mode="nn", tm=1024, tn=512, tk=1024, out_dtype=BF16, name="a_qkv")
    pf = _mm(h1, w_f, mode="nn", tm=1024, tn=LANES, tk=1024, name="a_gate")
    aug_q, aug_k = _fgate_fwd(pf, b_f, "a_gate_scan")
    oa2, lse_a = _fox_pair_fwd(qkv, aug_q, aug_k, "a_attn")
    x1 = _mm(oa2, p["a_w_out"], mode="nn", tm=1024, tn=512, tk=1024, res=x, name="a_out")
    p = {**p, **late_weights(x1)}
    x2, ffn0 = _ffn_fwd(x1, p["ffn_norm_g"][0], p["ffn_w_up"][0], p["conv_w"][0], p["conv_b"][0], p["ffn_w_down"][0], "f0")
    hk, h3 = _rms_fwd(x2, [p["kv_norm_g"], p["mix_norm_g"][1]], "kv_b_norm")
    kvb = _mm(hk, p["w_kv"], mode="nn", tm=1024, tn=512, tk=1024, out_dtype=BF16, name="kv_proj")
    qb = _mm(h3, p["b_w_q"], mode="nn", tm=1024, tn=512, tk=1024, out_dtype=BF16, name="b_q")
    dil_in = []
    for gi, (_, d) in enumerate(B_GROUPS):
        if d == 1:
            dil_in.append((qb, kvb, kvb, gi * NPG, gi * NPG, (3 + gi) * NPG))
        else:
            qg = _residue_rows(qb[:, gi * GROUP_W:(gi + 1) * GROUP_W], d)
            kvg = _residue_rows(kvb.reshape(S, 2, 3, GROUP_W)[:, :, gi, :].reshape(S, 2 * GROUP_W), d)
            dil_in.append((qg, kvg, kvg, 0, 0, NPG))
    o_g, lse_g = [], []
    for gi, (_, d) in enumerate(B_GROUPS):
        qg, kg, vg, qoff, koff, voff = dil_in[gi]
        og, lg = _dil_pair_fwd(gi, qg, kg, vg, qoff, koff, voff, f"b_attn{gi}")
        o_g.append(_residue_rows(og, d, inverse=True))
        lse_g.append(_residue_vecs(lg, d, inverse=True))
    ob, ob2, lse_b = _dil_pair_merge(o_g, lse_g, "b_merge")
    x3 = _mm(ob2, p["b_w_out"], mode="nn", tm=1024, tn=512, tk=B_OUT_W, res=x2, name="b_out")
    x4, ffn1 = _ffn_fwd(x3, p["ffn_norm_g"][1], p["ffn_w_up"][1], p["conv_w"][1], p["conv_b"][1], p["ffn_w_down"][1], "f1")
    loss, dx, dxb, dg_final = _loss_head(x4, p["final_norm_g"], target, "loss_head")
    g["final_norm_g"] = dg_final.reshape(D)

    dx, dxb, gf1 = _ffn_bwd(x3, p["ffn_norm_g"][1], p["ffn_w_up"][1], p["conv_w"][1], p["conv_b"][1], p["ffn_w_down"][1],
                            ffn1, dx, dxb, "f1")
    g["b_w_out"] = _mm(ob2, dxb, mode="tn", tm=B_OUT_W, tn=512, tk=1024, name="b_dwout")
    dob = _mm(dxb, p["b_w_out"], mode="nt", tm=1024, tn=B_OUT_W, tk=1024, name="b_do")
    delta_b = _pair_rowdot(dob, ob, "b_delta")
    dob16 = dob.astype(BF16)
    stats_b = jnp.concatenate([lse_b, delta_b], axis=0)
    dq_g, dk_g, dv_g = [], [], []
    for gi, (_, d) in enumerate(B_GROUPS):
        qg, kg, vg, qoff, koff, voff = dil_in[gi]
        dog, stats_d = _residue_rows(dob16, d), _residue_vecs(stats_b, d)
        dqd = _dil_pair_dq(gi, qg, kg, vg, qoff, koff, voff, dog, stats_d, f"b_dq{gi}")
        dkd, dvd = _dil_pair_dkv(gi, qg, kg, vg, qoff, koff, voff, dog, _col_to_row(stats_d), f"b_dkv{gi}")
        dq_g.append(_residue_rows(dqd, d, inverse=True))
        dk_g.append(_residue_rows(dkd, d, inverse=True))
        dv_g.append(_residue_rows(dvd, d, inverse=True))
    dqb = jnp.concatenate(dq_g, axis=1)
    dkvb = jnp.concatenate(dk_g + dv_g, axis=1)
    g["b_w_q"] = _mm(h3, dqb, mode="tn", tm=1024, tn=512, tk=1024, name="b_dwq")
    dh3 = _mm(dqb, p["b_w_q"], mode="nt", tm=1024, tn=512, tk=B_Q_W, name="b_dh")
    g["w_kv"] = _mm(hk, dkvb, mode="tn", tm=1024, tn=512, tk=1024, name="kv_dw")
    dhk = _mm(dkvb, p["w_kv"], mode="nt", tm=1024, tn=512, tk=1536, name="kv_dh")
    dx, dxb, (dg_mix1, dg_kv) = _rms_bwd(x2, dx, [(p["mix_norm_g"][1], dh3), (p["kv_norm_g"], dhk)], "b_dnorm")
    g["kv_norm_g"] = dg_kv.reshape(D)
    dx, dxb, gf0 = _ffn_bwd(x1, p["ffn_norm_g"][0], p["ffn_w_up"][0], p["conv_w"][0], p["conv_b"][0], p["ffn_w_down"][0],
                            ffn0, dx, dxb, "f0")
    g["ffn_w_up"] = [gf0["w_up"], gf1["w_up"]]
    g["ffn_w_down"] = [gf0["w_down"], gf1["w_down"]]
    g["ffn_conv_w"] = jnp.stack([gf0["conv_w"], gf1["conv_w"]])
    token = late_grads_ready(g)
    a_w_out_t = p["a_w_out"] + token[0, 0].astype(BF16)
    g["a_w_out"] = _mm(oa2, dxb, mode="tn", tm=1024, tn=512, tk=1024, name="a_dwout")
    doa = _mm(dxb, a_w_out_t, mode="nt", tm=1024, tn=512, tk=1024, name="a_do")
    delta_a = _pair_rowdot(doa, oa2, "a_delta")
    token = late_grads_continue(delta_a)
    delta_row = _col_to_row(delta_a) + token[0, 0]
    dqa, dka, dva, dck, dcq = _fox_pair_bwd(qkv, doa, _col_to_row(lse_a), delta_row, aug_q, aug_k, "a_dattn")
    dqkv = jnp.concatenate([dqa, dka, dva], axis=1)
    pad_heads = lambda t: jnp.pad(t.reshape(A_HEADS, S).T, ((0, 0), (0, LANES - A_HEADS)))
    dpf, db_f = _fgate_bwd(pf, b_f, pad_heads(dck), pad_heads(dcq), "a_dgate_scan")
    g["a_b_f"] = db_f[:, :A_HEADS]
    d_w_qkv = _mm(h1, dqkv, mode="tn", tm=1024, tn=512, tk=1024, name="a_dwqkv")
    d_w_f = _mm(h1, dpf, mode="tn", tm=1024, tn=LANES, tk=1024, name="a_dwgate")
    g["a_w_in"] = jnp.concatenate([d_w_qkv, d_w_f[:, :A_HEADS]], axis=1)
    dh1 = _mm(dqkv, w_qkv, mode="nt", tm=1024, tn=512, tk=1536, name="a_dh")
    dh1 = _mm(dpf, w_f, mode="nt", tm=1024, tn=512, tk=LANES, res=dh1, name="a_dh_gate")
    dx, _, (dg_mix0,) = _rms_bwd(x, dx, [(p["mix_norm_g"][0], dh1)], "a_dnorm")

    g["mix_norm_g"] = jnp.stack([dg_mix0.reshape(D), dg_mix1.reshape(D)])
    g["ffn_norm_g"] = jnp.stack([gf0["norm_g"], gf1["norm_g"]])
    g["ffn_conv_b"] = jnp.stack([gf0["conv_b"], gf1["conv_b"]])
    return loss[0, 0], dx, g


_SHARD_SHAPES = {"a_w_in": (1, 1024, 772), "a_w_out": (1, 256, 1024), "b_w_q": (1, 1024, 384), "b_w_out": (1, 512, 256),
                 "w_kv": (1024, 768), "ffn_w_up": (2, 1024, 1408), "ffn_w_down": (2, 704, 1024), "ffn_conv_w": (2, 3, 1408)}
_SMALL = (("kv_norm_g", (1024,)), ("mix_norm_g", (2, 1024)), ("ffn_norm_g", (2, 1024)), ("final_norm_g", (1024,)),
          ("a_b_f", (1, 16)), ("ffn_conv_b", (2, 5632)))


def _unslabs(rows, L, R, C, rpad):
    nc = -(-C // FLAT_W)
    return rows.reshape(L, nc, rpad, FLAT_W).transpose(0, 2, 1, 3).reshape(L, rpad, nc * FLAT_W)[:, :R, :C]


_SEG_RT = {"ffn_w_down": 704, "a_w_in": 1024, "a_w_out": 256, "b_w_q": 1024, "b_w_out": 512, "w_kv": 1024,
           "ffn_w_up": 1024, "ffn_conv_w": 16}
_ROW_SHARDED = ("a_w_out", "ffn_w_down")


_LAYOUTS = {"early": ("a_w_in", "a_w_out"), "late": ("ffn_w_down", "b_w_q", "b_w_out", "w_kv", "ffn_w_up", "ffn_conv_w"),
            "grad_early": ("a_w_in", "a_w_out"),
            "grad_late": ("ffn_w_up", "ffn_w_down", "b_w_q", "w_kv", "b_w_out", "ffn_conv_w")}
_GRAD_ROWS = {"grad_early": 10240, "grad_late": 45056}


def _layout_rows(layout):
    used = sum(_seg_rows(*s) for s in _SEGS if s[0] in _LAYOUTS[layout])
    rows = _GRAD_ROWS.get(layout, used)
    assert rows >= used
    return rows


def _grad_layout(name):
    return "grad_early" if name in _LAYOUTS["grad_early"] else "grad_late"


def _seg(name, layout=None):
    layout = layout or _grad_layout(name)
    off = 0
    for s in sorted((s for s in _SEGS if s[0] in _LAYOUTS[layout]), key=lambda s: _LAYOUTS[layout].index(s[0])):
        _, L, R, C, rpad = s
        if layout in _GRAD_ROWS:
            per_layer = -(-C // FLAT_W) * rpad
            off = -(-off // per_layer) * per_layer
        if s[0] == name:
            rt = _SEG_RT[name]
            assert off % rt == 0 and rpad % rt == 0
            half = _layout_rows(layout) // 2
            assert off + _seg_rows(*s) <= 2 * half
            assert layout not in _GRAD_ROWS or half % rt == 0 or off + _seg_rows(*s) <= half
            return dict(L=L, R=R, C=C, rpad=rpad, nc=-(-C // FLAT_W), rt=rt, off=off, ni=rpad // rt, half=half)
        off += _seg_rows(*s)
    raise KeyError(name)


def _flat_block(sg, term=0):
    base = (sg["off"] + term * sg["L"] * sg["nc"] * sg["rpad"]) // sg["rt"]
    return lambda l, j, i: base + (l * sg["nc"] + j) * sg["ni"] + i


def _native3(t, name):
    sg = _seg(name)
    t = t.reshape(sg["L"], sg["R"], sg["C"])
    return jnp.pad(t, ((0, 0), (0, sg["rpad"] - sg["R"]), (0, 0))) if sg["rpad"] != sg["R"] else t


def _slab_pack(flat, t, name, layout, term=None):
    sg = _seg(name, layout)
    rt = sg["rt"]
    rb = _flat_block(sg, term or 0)

    def body(*refs):
        t_ref, o_ref = refs[-2], refs[-1]
        val = t_ref[...]
        o_ref[...] = val.astype(BF16) if term is None else _split3(val)[term]

    in_specs = [pl.BlockSpec((None, rt, FLAT_W), lambda l, j, i: (l, i, j))]
    args = [t]
    if flat is not None:
        in_specs, args = [ANY] + in_specs, [flat] + args
    return pl.pallas_call(
        body, out_shape=jax.ShapeDtypeStruct((_layout_rows(layout), FLAT_W), BF16), grid=(sg["L"], sg["nc"], sg["ni"]),
        in_specs=in_specs, out_specs=pl.BlockSpec((rt, FLAT_W), lambda l, j, i: (rb(l, j, i), 0)),
        input_output_aliases={0: 0} if flat is not None else {},
        compiler_params=_cparams(("parallel", "parallel", "parallel")), name=f"pack_{name}_{term or 0}",
    )(*args)


def _full_spec(sg, name):
    rt, nc, ni = sg["rt"], sg["nc"], sg["ni"]
    if name in _ROW_SHARDED:
        return (sg["L"], N_CHIPS * sg["R"], sg["C"]), pl.BlockSpec((None, rt, FLAT_W), lambda k, l, j, i: (l, k * ni + i, j))
    return ((sg["L"], sg["rpad"], N_CHIPS * nc * FLAT_W),
            pl.BlockSpec((None, rt, FLAT_W), lambda k, l, j, i: (l, i, k * nc + j)))


def _slab_unpack(gathered, slots, name, layout, own=None):
    sg = _seg(name, layout)
    rb = _flat_block(sg)
    shape, _ = _full_spec(sg, name)
    rt, nc, ni = sg["rt"], sg["nc"], sg["ni"]
    width = nc * FLAT_W
    last = gathered.shape[0] - 1

    def body(*refs):
        s_ref, o_ref = refs[0], refs[-1]
        is_own = s_ref[pl.program_id(0)] == OWN_SLOT
        for j in range(nc):
            val = refs[1 + j][...]
            if own is not None:
                val = jnp.where(is_own, refs[1 + nc + j][...], val)
            o_ref[:, j * FLAT_W:(j + 1) * FLAT_W] = val

    if name in _ROW_SHARDED:
        o_spec = pl.BlockSpec((None, rt, width), lambda k, l, i, s: (l, k * ni + i, 0))
    else:
        o_spec = pl.BlockSpec((None, rt, width), lambda k, l, i, s: (l, i, k))
    in_specs = [pl.BlockSpec((None, rt, FLAT_W), lambda k, l, i, s, j=j: (jnp.minimum(s[k], last), rb(l, j, i), 0))
                for j in range(nc)]
    args = [gathered] * nc
    if own is not None:
        in_specs += [pl.BlockSpec((rt, FLAT_W), lambda k, l, i, s, j=j: (rb(l, j, i), 0)) for j in range(nc)]
        args += [own] * nc
    return pl.pallas_call(
        body, out_shape=jax.ShapeDtypeStruct(shape, BF16),
        grid_spec=pltpu.PrefetchScalarGridSpec(num_scalar_prefetch=1, grid=(N_CHIPS, sg["L"], ni), in_specs=in_specs,
                                               out_specs=o_spec),
        compiler_params=_cparams(("parallel",) * 3), name=f"unpack_{name}",
    )(slots, *args)


def _slab_pack_grad(flat4, g, name, layer=None):
    sg = _seg(name)
    rows = _layout_rows(_grad_layout(name))
    shape, _ = _full_spec(sg, name)
    n_layers = sg["L"] if layer is None else 1
    assert g.shape == (n_layers,) + shape[1:], (name, g.shape, shape)
    rt, nc = sg["rt"], sg["nc"]
    assert sg["ni"] == 1 and sg["off"] % (nc * rt) == 0
    base = sg["off"] // (nc * rt) + (layer or 0)

    def body(*refs):
        g_ref, o_ref = refs[-2], refs[-1]
        for j in range(nc):
            o_ref[j * rt:(j + 1) * rt, :] = g_ref[:, j * FLAT_W:(j + 1) * FLAT_W]

    if name in _ROW_SHARDED:
        spec = pl.BlockSpec((None, rt, nc * FLAT_W), lambda k, l: (l, k, 0))
    else:
        spec = pl.BlockSpec((None, rt, nc * FLAT_W), lambda k, l: (l, 0, k))
    in_specs, args = [spec], [g]
    if flat4 is not None:
        in_specs, args = [pl.BlockSpec(memory_space=pl.ANY)] + in_specs, [flat4] + args
    return pl.pallas_call(
        body, out_shape=jax.ShapeDtypeStruct((N_CHIPS, rows, FLAT_W), F32), grid=(N_CHIPS, n_layers),
        in_specs=in_specs, out_specs=pl.BlockSpec((None, nc * rt, FLAT_W), lambda k, l: (k, base + l, 0)),
        input_output_aliases={0: 0} if flat4 is not None else {},
        compiler_params=_cparams(("parallel",) * 2), name=f"packgrad_{name}_{layer or 0}",
    )(*args)


def _adamw_shard(w, m, v, g_mine, g_other, c_arr, name):
    sg = _seg(name)
    rt = sg["rt"]
    rb = _flat_block(sg)
    per_half = sg["half"] // rt

    def half_of(l, j, i):
        return (rb(l, j, i) * rt) // sg["half"]

    def body(c_ref, w_ref, m_ref, v_ref, gm_ref, go_ref, g_ref, d_ref, mo_ref, vo_ref):
        is_mine = half_of(pl.program_id(0), pl.program_id(1), pl.program_id(2)) == c_ref[0]
        gv = jnp.where(is_mine, gm_ref[...], go_ref[...])
        g_ref[...] = gv
        d_ref[...], mo_ref[...], vo_ref[...] = _adam_update(w_ref[...], gv, m_ref[...], v_ref[...])

    nat = pl.BlockSpec((None, rt, FLAT_W), lambda l, j, i, c: (l, i, j))
    half = pl.BlockSpec((rt, FLAT_W), lambda l, j, i, c: (rb(l, j, i) - half_of(l, j, i) * per_half, 0))
    sds = jax.ShapeDtypeStruct(w.shape, F32)
    return pl.pallas_call(
        body, out_shape=[sds] * 4,
        grid_spec=pltpu.PrefetchScalarGridSpec(num_scalar_prefetch=1, grid=(sg["L"], sg["nc"], sg["ni"]),
                                               in_specs=[nat, nat, nat, half, half], out_specs=[nat] * 4),
        compiler_params=_cparams(("parallel", "parallel", "parallel")), name=f"adamw_{name}",
    )(c_arr, w, m, v, g_mine, g_other)


def _pack_small(vals, loss=None):
    parts = [vals[name].astype(F32).reshape(-1) for name, _ in _SMALL]
    if loss is not None:
        parts.append(loss.reshape(1))
    flat = jnp.concatenate(parts)
    return jnp.pad(flat, (0, SMALL_ROWS * 1024 - flat.shape[0])).reshape(SMALL_ROWS, 1024)


def _unpack_small(flat):
    flat = flat.reshape(-1)
    out = {}
    o = 0
    for name, shape in _SMALL:
        n = int(np.prod(shape))
        out[name] = flat[o:o + n].reshape(shape)
        o += n
    return out, flat[o]


_BIG = ("a_w_in", "a_w_out", "b_w_q", "b_w_out", "w_kv", "ffn_w_up", "ffn_w_down", "ffn_conv_w")
A_IN_PAD = 896


def _pack_weights(w, layout):
    flat = None
    for name in _LAYOUTS[layout]:
        t = _native3(w[name], name)
        for term in ((0, 1, 2) if name == "ffn_conv_w" else (None,)):
            flat = _slab_pack(flat, t, name, layout, term)
    return flat


def _early_weights(gathered, slots):
    a_in = _slab_unpack(gathered, slots, "a_w_in", "early")
    a_in = a_in.reshape(D, N_CHIPS, A_IN_PAD)[:, :, :772].reshape(D, N_CHIPS * 772)
    return dict(a_w_in=a_in, a_w_out=_slab_unpack(gathered, slots, "a_w_out", "early")[0])


def _late_weights(landed, slots, own):
    full = {name: _slab_unpack(landed, slots, name, "late", own) for name in _LAYOUTS["late"] if name != "ffn_conv_w"}
    sg = _seg("ffn_conv_w", "late")
    n1 = sg["nc"] * sg["rpad"]
    conv = slice(sg["off"], sg["off"] + CONV_TERMS * n1)
    conv_rows = jnp.concatenate([landed[:, conv], own[None, conv]], axis=0)
    per_chip = []
    for k in range(N_CHIPS):
        rows = lax.dynamic_index_in_dim(conv_rows, slots[k], axis=0, keepdims=False)
        terms = [_unslabs(rows[i * n1:(i + 1) * n1], 1, sg["R"], sg["C"], sg["rpad"]).astype(F32) for i in range(CONV_TERMS)]
        per_chip.append((terms[0] + terms[1]) + terms[2])
    cw = jnp.concatenate(per_chip, axis=2).reshape(2, 3, 2, FF).transpose(0, 2, 1, 3)
    return dict(b_w_q=full["b_w_q"][0], b_w_out=full["b_w_out"][0], w_kv=full["w_kv"][0], ffn_w_up=full["ffn_w_up"],
                ffn_w_down=full["ffn_w_down"], conv_w=cw)


def _shard_grads(g, layout):
    def full(name):
        if name == "a_w_in":
            a_in = jnp.pad(g[name].reshape(D, N_CHIPS, 772), ((0, 0), (0, 0), (0, A_IN_PAD - 772)))
            return a_in.reshape(1, D, N_CHIPS * A_IN_PAD)
        if name == "ffn_conv_w":
            sgc = _seg(name)
            return jnp.pad(g[name].reshape(1, sgc["R"], 2 * FF), ((0, 0), (0, sgc["rpad"] - sgc["R"]), (0, 0)))
        return g[name] if g[name].ndim == 3 else g[name][None]

    flat4 = None
    for name in _LAYOUTS[layout]:
        if isinstance(g[name], (list, tuple)):
            for layer, t in enumerate(g[name]):
                flat4 = _slab_pack_grad(flat4, t[None], name, layer)
        else:
            flat4 = _slab_pack_grad(flat4, full(name), name)
    return flat4


_WEIGHTS = ["a_w_in", "a_b_f", "a_w_out", "b_w_q", "b_w_out", "kv_norm_g", "w_kv", "mix_norm_g", "ffn_norm_g", "ffn_w_up",
            "ffn_conv_w", "ffn_conv_b", "ffn_w_down", "final_norm_g"]


def kernel(x, a_w_in, a_b_f, a_w_out, b_w_q, b_w_out, kv_norm_g, w_kv, mix_norm_g, ffn_norm_g, ffn_w_up, ffn_conv_w, ffn_conv_b, ffn_w_down, final_norm_g, loss_target, m_a_w_in, m_a_b_f, m_a_w_out, m_b_w_q, m_b_w_out, m_kv_norm_g, m_w_kv, m_mix_norm_g, m_ffn_norm_g, m_ffn_w_up, m_ffn_conv_w, m_ffn_conv_b, m_ffn_w_down, m_final_norm_g, v_a_w_in, v_a_b_f, v_a_w_out, v_b_w_q, v_b_w_out, v_kv_norm_g, v_w_kv, v_mix_norm_g, v_ffn_norm_g, v_ffn_w_up, v_ffn_conv_w, v_ffn_conv_b, v_ffn_w_down, v_final_norm_g):
    w = dict(a_w_in=a_w_in, a_b_f=a_b_f, a_w_out=a_w_out, b_w_q=b_w_q, b_w_out=b_w_out, kv_norm_g=kv_norm_g, w_kv=w_kv,
             mix_norm_g=mix_norm_g, ffn_norm_g=ffn_norm_g, ffn_w_up=ffn_w_up, ffn_conv_w=ffn_conv_w, ffn_conv_b=ffn_conv_b,
             ffn_w_down=ffn_w_down, final_norm_g=final_norm_g)
    m = dict(a_w_in=m_a_w_in, a_b_f=m_a_b_f, a_w_out=m_a_w_out, b_w_q=m_b_w_q, b_w_out=m_b_w_out, kv_norm_g=m_kv_norm_g,
             w_kv=m_w_kv, mix_norm_g=m_mix_norm_g, ffn_norm_g=m_ffn_norm_g, ffn_w_up=m_ffn_w_up, ffn_conv_w=m_ffn_conv_w,
             ffn_conv_b=m_ffn_conv_b, ffn_w_down=m_ffn_w_down, final_norm_g=m_final_norm_g)
    v = dict(a_w_in=v_a_w_in, a_b_f=v_a_b_f, a_w_out=v_a_w_out, b_w_q=v_b_w_q, b_w_out=v_b_w_out, kv_norm_g=v_kv_norm_g,
             w_kv=v_w_kv, mix_norm_g=v_mix_norm_g, ffn_norm_g=v_ffn_norm_g, ffn_w_up=v_ffn_w_up, ffn_conv_w=v_ffn_conv_w,
             ffn_conv_b=v_ffn_conv_b, ffn_w_down=v_ffn_w_down, final_norm_g=v_final_norm_g)

    c_arr = lax.axis_index("c").astype(jnp.int32).reshape(1)
    k_arr = (2 * lax.axis_index("x") + lax.axis_index("y")).astype(jnp.int32).reshape(1)
    xi, yi = lax.axis_index("x"), lax.axis_index("y")
    late_slots = jnp.stack([jnp.where(k == k_arr[0], OWN_SLOT, 2 * ((k & 1) ^ yi) + ((k >> 1) ^ xi) - 1)
                            for k in range(N_CHIPS)]).astype(jnp.int32)
    w_late = _pack_weights(w, "late")
    land = lax.empty((OWN_SLOT,) + w_late.shape, BF16)
    send_sems, recv_sems, w_thru, land_thru, token = _late_gather_start(w_late, land)
    w_early = _pack_weights(w, "early")
    early = _allgather_shards(w_early, _place_own(w_early, k_arr, "early_place_own"))
    p = _early_weights(early, jnp.arange(N_CHIPS, dtype=jnp.int32))
    cb = ffn_conv_b.reshape(2, 2, 1, FF)
    p.update(a_b_f=a_b_f, kv_norm_g=kv_norm_g, mix_norm_g=mix_norm_g + token[0, 0], ffn_norm_g=ffn_norm_g,
             final_norm_g=final_norm_g, conv_b=cb)

    def late_weights(after):
        own, landed = _late_gather_wait(send_sems, recv_sems, w_thru, land_thru, after)
        return _late_weights(landed, late_slots, own)

    started = {}

    def late_grads_ready(g_so_far):
        gflat = _shard_grads(g_so_far, "grad_late")
        land = lax.empty((N_CHIPS, gflat.shape[1] // 2, FLAT_W), F32)
        *handles, token = _sibling_swap_start(gflat, land)
        started["swap"] = handles
        return token

    def late_grads_continue(after):
        gflat, other = _sibling_swap_wait(*started["swap"], after)
        pair = _pair_sum(gflat, other, c_arr, "late")
        land = lax.empty((3,) + pair.shape[1:], BF16)
        *handles, token = _chip_exchange_start(pair, land, "late")
        started["handles"] = handles
        return token

    loss_part, grad_x, g = _local_step(x[0], loss_target[0], p, late_weights, late_grads_ready, late_grads_continue)

    gflat = _shard_grads(g, "grad_early")
    pair_e = _pair_sum(gflat, _sibling_swap_half(gflat, "early"), c_arr, "early")
    *early_handles, token = _chip_exchange_start(pair_e, lax.empty((3,) + pair_e.shape[1:], BF16), "early")

    big = [{}, {}, {}, {}]

    def adamw_group(layout, g_mine):
        g_other = _sibling_send(g_mine, layout)
        for name in _LAYOUTS[layout]:
            sg = _seg(name)
            res = _adamw_shard(_native3(w[name], name), _native3(m[name], name), _native3(v[name], name), g_mine,
                               g_other, c_arr, name)
            for store, t in zip(big, res):
                store[name] = t[:, :sg["R"], :].reshape(_SHARD_SHAPES[name])
        return res[1]

    pair, landed = _chip_exchange_wait(*started["handles"], token, "late")
    last = adamw_group("grad_late", _chip_sum(pair, landed, k_arr, "late"))
    small, loss = _unpack_small(_allreduce_small(_pack_small(g, loss_part)))
    dws, mns, vns = _adamw(_pack_small(w), _pack_small(small), _pack_small(m), _pack_small(v), "adamw_small")
    pair_e, landed_e = _chip_exchange_wait(*early_handles, last, "early")
    adamw_group("grad_early", _chip_sum(pair_e, landed_e, k_arr, "early"))
    sml = [small] + [_unpack_small(t)[0] for t in (dws, mns, vns)]
    outs = [loss, grad_x[None]]
    for b, s in zip(big, sml):
        outs += [b[n] if n in b else s[n] for n in _WEIGHTS]
    return tuple(outs)
```

```python
import numpy as np
import jax
import jax.numpy as jnp
from jax import lax
from jax.experimental import pallas as pl
from jax.experimental.pallas import tpu as pltpu

F32 = jnp.float32
BF16 = jnp.bfloat16
MESH = pl.DeviceIdType.MESH

S = 4096
D = 1024
A_HEADS = 16
HEAD_DIM = 64
QKV_W = 3 * A_HEADS * HEAD_DIM
B_GROUPS = ((128, 1), (512, 4), (2048, 16))
B_HPG = 8
B_Q_W = 3 * B_HPG * HEAD_DIM
B_OUT_W = B_HPG * HEAD_DIM
B_WIN = 128
FF = 2816
RMS_EPS = 1e-6
SCALE = HEAD_DIM ** -0.5
N_CHIPS = 4

ADAM_LR, ADAM_B1, ADAM_B2, ADAM_EPS, ADAM_WD, ADAM_STEP = 0.001, 0.9, 0.999, 1e-08, 0.01, 10

V7X_VMEM_LIMIT = 48 * 1024 * 1024
LANES = 128
NEG_INF = float("-inf")

FLAT_W = LANES
_SEGS = (("ffn_w_down", 2, 704, 1024, 704), ("a_w_in", 1, 1024, 772, 1024), ("a_w_out", 1, 256, 1024, 256),
         ("b_w_q", 1, 1024, 384, 1024), ("b_w_out", 1, 512, 256, 512), ("w_kv", 1, 1024, 768, 1024),
         ("ffn_w_up", 2, 1024, 1408, 1024), ("ffn_conv_w", 1, 6, 1408, 16))
CONV_TERMS = 3


def _seg_rows(name, L, R, C, rpad):
    return (CONV_TERMS if name == "ffn_conv_w" else 1) * L * (-(-C // FLAT_W)) * rpad


SMALL_ROWS = 24


def _cparams(sem=None, **kw):
    return pltpu.CompilerParams(dimension_semantics=sem, vmem_limit_bytes=V7X_VMEM_LIMIT, **kw)


_DN = {"nn": (((1,), (0,)), ((), ())), "nt": (((1,), (1,)), ((), ())), "tn": (((0,), (0,)), ((), ()))}


def _mm(a, b, *, mode, tm, tn, tk, name, out_dtype=F32, res=None, a_split=0, b_split=0, o_split=0):
    if mode == "tn":
        K = a.shape[0]
        M = a.shape[1]
    else:
        M = a.shape[-2]
        K = a.shape[-1] * (2 if a_split else 1)
    if mode == "nt":
        N = b.shape[0]
    else:
        N = b.shape[-1] * (2 if b_split else 1)
    assert M % tm == 0 and N % tn == 0 and K % tk == 0, (name, M, N, K, tm, tn, tk)
    nk = K // tk

    if mode == "tn":
        a_spec = pl.BlockSpec((tk, tm), lambda i, j, k: (k, i))
    elif a_split:
        a_spec = pl.BlockSpec((None, tm, tk), lambda i, j, k: (k // a_split, i, k % a_split))
    else:
        a_spec = pl.BlockSpec((tm, tk), lambda i, j, k: (i, k))
    if mode == "nt":
        b_spec = pl.BlockSpec((tn, tk), lambda i, j, k: (j, k))
    elif b_split:
        b_spec = pl.BlockSpec((None, tk, tn), lambda i, j, k: (j // b_split, k, j % b_split))
    else:
        b_spec = pl.BlockSpec((tk, tn), lambda i, j, k: (k, j))
    if o_split:
        o_spec = pl.BlockSpec((None, tm, tn), lambda i, j, k: (j // o_split, i, j % o_split))
        out_shape = jax.ShapeDtypeStruct((2, M, N // 2), out_dtype)
    else:
        o_spec = pl.BlockSpec((tm, tn), lambda i, j, k: (i, j))
        out_shape = jax.ShapeDtypeStruct((M, N), out_dtype)
    in_specs = [a_spec, b_spec]
    args = [a, b]
    if res is not None:
        in_specs.append(pl.BlockSpec((tm, tn), lambda i, j, k: (i, j)))
        args.append(res)

    def body(*refs):
        if res is not None:
            a_ref, b_ref, r_ref, o_ref = refs[:4]
        else:
            a_ref, b_ref, o_ref = refs[:3]
            r_ref = None
        p = lax.dot_general(a_ref[...].astype(BF16), b_ref[...].astype(BF16), _DN[mode], preferred_element_type=F32)

        def finish(r):
            if r_ref is not None:
                r = r + r_ref[...]
            o_ref[...] = r.astype(out_dtype)

        if nk == 1:
            finish(p)
        else:
            acc = refs[-1]
            k = pl.program_id(2)

            @pl.when(k == 0)
            def _():
                acc[...] = p

            @pl.when(k > 0)
            def _():
                acc[...] += p

            @pl.when(k == nk - 1)
            def _():
                finish(acc[...])

    return pl.pallas_call(
        body, out_shape=out_shape, grid=(M // tm, N // tn, nk), in_specs=in_specs, out_specs=o_spec,
        scratch_shapes=[pltpu.VMEM((tm, tn), F32)] if nk > 1 else [],
        compiler_params=_cparams(("parallel", "parallel", "arbitrary")), name=name,
    )(*args)


NORM_ROWS = 256


def _rms_fwd(x, gains, name):
    n = len(gains)

    def body(x_ref, *refs):
        xv = x_ref[...]
        y = xv * lax.rsqrt(jnp.mean(xv * xv, axis=-1, keepdims=True) + RMS_EPS)
        for g_ref, o_ref in zip(refs[:n], refs[n:]):
            o_ref[...] = (y * g_ref[...]).astype(BF16)

    row = pl.BlockSpec((NORM_ROWS, D), lambda i: (i, 0))
    return pl.pallas_call(
        body, out_shape=[jax.ShapeDtypeStruct((S, D), BF16)] * n, grid=(S // NORM_ROWS,),
        in_specs=[row] + [pl.BlockSpec((1, D), lambda i: (0, 0))] * n, out_specs=[row] * n,
        compiler_params=_cparams(("parallel",)), name=name,
    )(x, *[g.reshape(1, D) for g in gains])


def _rms_bwd(x, dres, pairs, name):
    n = len(pairs)

    def body(*refs):
        x_ref, dres_ref = refs[0], refs[1]
        g_refs = refs[2:2 + 2 * n:2]
        dh_refs = refs[3:3 + 2 * n:2]
        dx_ref, dxb_ref = refs[2 + 2 * n], refs[3 + 2 * n]
        dg_refs = refs[4 + 2 * n:]
        i = pl.program_id(0)
        xv = x_ref[...]
        r = lax.rsqrt(jnp.mean(xv * xv, axis=-1, keepdims=True) + RMS_EPS)
        y = xv * r
        dx = dres_ref[...]
        for g_ref, dh_ref, dg_ref in zip(g_refs, dh_refs, dg_refs):
            dh = dh_ref[...]
            dy = dh * g_ref[...]
            dx = dx + r * (dy - y * jnp.mean(dy * y, axis=-1, keepdims=True))
            part = jnp.sum(dh * y, axis=0, keepdims=True)

            @pl.when(i == 0)
            def _():
                dg_ref[...] = part

            @pl.when(i > 0)
            def _():
                dg_ref[...] += part

        dx_ref[...] = dx
        dxb_ref[...] = dx.astype(BF16)

    row = pl.BlockSpec((NORM_ROWS, D), lambda i: (i, 0))
    vec = pl.BlockSpec((1, D), lambda i: (0, 0))
    in_specs = [row, row]
    args = [x, dres]
    for g, dh in pairs:
        in_specs += [vec, row]
        args += [g.reshape(1, D), dh]
    outs = pl.pallas_call(
        body,
        out_shape=[jax.ShapeDtypeStruct((S, D), F32), jax.ShapeDtypeStruct((S, D), BF16)]
        + [jax.ShapeDtypeStruct((1, D), F32)] * n,
        grid=(S // NORM_ROWS,), in_specs=in_specs, out_specs=[row, row] + [vec] * n,
        compiler_params=_cparams(("arbitrary",)), name=name,
    )(*args)
    return outs[0], outs[1], list(outs[2:])


def _loss_head(x, g, target, name):
    def body(x_ref, g_ref, t_ref, loss_ref, dx_ref, dxb_ref, dg_ref):
        i = pl.program_id(0)
        xv = x_ref[...]
        gv = g_ref[...]
        r = lax.rsqrt(jnp.mean(xv * xv, axis=-1, keepdims=True) + RMS_EPS)
        y = xv * r
        err = y * gv - t_ref[...]
        lpart = jnp.broadcast_to(jnp.sum(err * err, keepdims=True) * (0.5 / D), (1, LANES))
        dh = err * (1.0 / D)
        dy = dh * gv
        dx = r * (dy - y * jnp.mean(dy * y, axis=-1, keepdims=True))
        part = jnp.sum(dh * y, axis=0, keepdims=True)

        @pl.when(i == 0)
        def _():
            dg_ref[...] = part
            loss_ref[...] = lpart

        @pl.when(i > 0)
        def _():
            dg_ref[...] += part
            loss_ref[...] += lpart

        dx_ref[...] = dx
        dxb_ref[...] = dx.astype(BF16)

    row = pl.BlockSpec((NORM_ROWS, D), lambda i: (i, 0))
    vec = pl.BlockSpec((1, D), lambda i: (0, 0))
    return pl.pallas_call(
        body,
        out_shape=[jax.ShapeDtypeStruct((1, LANES), F32), jax.ShapeDtypeStruct((S, D), F32),
                   jax.ShapeDtypeStruct((S, D), BF16), jax.ShapeDtypeStruct((1, D), F32)],
        grid=(S // NORM_ROWS,), in_specs=[row, vec, row],
        out_specs=[pl.BlockSpec((1, LANES), lambda i: (0, 0)), row, row, vec],
        compiler_params=_cparams(("arbitrary",)), name=name,
    )(x, g.reshape(1, D), target)


SCAN_ROWS = 256


def _split3(v):
    hi = v.astype(BF16)
    r1 = v - hi.astype(F32)
    mid = r1.astype(BF16)
    lo = (r1 - mid.astype(F32)).astype(BF16)
    return hi, mid, lo


def _tri_dot(tri, v):
    hi, mid, lo = _split3(v)
    dn = _DN["nn"]
    return (lax.dot_general(tri, hi, dn, preferred_element_type=F32)
            + lax.dot_general(tri, mid, dn, preferred_element_type=F32)
            + lax.dot_general(tri, lo, dn, preferred_element_type=F32))


def _log_sigmoid(z):
    return jnp.minimum(z, 0.0) - jnp.log(1.0 + jnp.exp(-jnp.abs(z)))


GATE_LANES = 6


def _gate_lane_tables():
    pq = np.zeros((3 * LANES, A_HEADS * HEAD_DIM), np.float32)
    pk = np.zeros((3 * LANES, A_HEADS * HEAD_DIM), np.float32)
    one_q = np.zeros((1, A_HEADS * HEAD_DIM), np.float32)
    one_k = np.zeros((1, A_HEADS * HEAD_DIM), np.float32)
    for h in range(A_HEADS):
        pos = (h // 2) * LANES + (HEAD_DIM if h % 2 == 0 else 0)
        for term in range(3):
            pq[term * LANES + h, pos + term] = 1.0
            pk[term * LANES + h, pos + 3 + term] = -1.0
        one_q[0, pos + 3:pos + GATE_LANES] = 1.0
        one_k[0, pos:pos + 3] = 1.0
    return jnp.asarray(pq, BF16), jnp.asarray(pk, BF16), jnp.asarray(one_q), jnp.asarray(one_k)


def _fgate_fwd(pf, bias, name):
    tri = jnp.tril(jnp.ones((SCAN_ROWS, SCAN_ROWS), F32)).astype(BF16)
    pq, pk, one_q, one_k = _gate_lane_tables()

    def body(pf_ref, b_ref, tri_ref, pq_ref, pk_ref, oq_ref, ok_ref, aq_ref, ak_ref, c_sc):
        carry = jnp.zeros((1, LANES), F32)
        for blk in range(S // SCAN_ROWS):
            rows = pl.ds(blk * SCAN_ROWS, SCAN_ROWS)
            lf = _log_sigmoid(pf_ref[rows, :] + b_ref[...])
            c_sc[...] = _tri_dot(tri_ref[...], lf) + carry
            carry = c_sc[pl.ds(SCAN_ROWS - 1, 1), :]
            terms = jnp.concatenate(_split3(c_sc[...]), axis=1)
            aq = lax.dot_general(terms, pq_ref[...], _DN["nn"], preferred_element_type=F32) + oq_ref[...]
            ak = lax.dot_general(terms, pk_ref[...], _DN["nn"], preferred_element_type=F32) + ok_ref[...]
            aq_ref[rows, :] = aq.astype(BF16)
            ak_ref[rows, :] = ak.astype(BF16)

    wide = jax.ShapeDtypeStruct((S, A_HEADS * HEAD_DIM), BF16)
    return pl.pallas_call(
        body, out_shape=[wide, wide], scratch_shapes=[pltpu.VMEM((SCAN_ROWS, LANES), F32)],
        compiler_params=_cparams(), name=name,
    )(pf, bias, tri, pq, pk, one_q, one_k)


def _fgate_bwd(pf, bias, dc_key, dc_query, name):
    triu = jnp.triu(jnp.ones((SCAN_ROWS, SCAN_ROWS), F32)).astype(BF16)

    def body(pf_ref, b_ref, dck_ref, dcq_ref, tri_ref, dpf_ref, db_ref, dlf_ref):
        carry = jnp.zeros((1, LANES), F32)
        db = jnp.zeros((1, LANES), F32)
        lane = lax.broadcasted_iota(jnp.int32, (SCAN_ROWS, LANES), 1)
        for blk in reversed(range(S // SCAN_ROWS)):
            rows = pl.ds(blk * SCAN_ROWS, SCAN_ROWS)
            dc = dck_ref[rows, :] + dcq_ref[rows, :]
            dlf_ref[rows, :] = _tri_dot(tri_ref[...], dc) + carry
            carry = dlf_ref[pl.ds(blk * SCAN_ROWS, 1), :]
            z = pf_ref[rows, :] + b_ref[...]
            e = jnp.exp(-jnp.abs(z))
            sig_neg = jnp.where(z >= 0.0, e, 1.0) / (1.0 + e)
            dz = jnp.where(lane < A_HEADS, dlf_ref[rows, :] * sig_neg, 0.0)
            dpf_ref[rows, :] = dz.astype(BF16)
            db = db + jnp.sum(dz, axis=0, keepdims=True)
        db_ref[...] = db

    return pl.pallas_call(
        body, out_shape=[jax.ShapeDtypeStruct((S, LANES), BF16), jax.ShapeDtypeStruct((1, LANES), F32)],
        scratch_shapes=[pltpu.VMEM((S, LANES), F32)],
        compiler_params=_cparams(), name=name,
    )(pf, bias, dc_key, dc_query, triu)


FOX_T = 512


def _first_head(shape):
    return lax.broadcasted_iota(jnp.int32, shape, len(shape) - 1) < HEAD_DIM


def _each_head(x, lo):
    zero = jnp.zeros_like(x)
    return jnp.where(lo, x, zero), jnp.where(lo, zero, x)


def _fox_pair_fwd(qkv, aug_q, aug_k, name):
    T = FOX_T
    nq = S // T
    NP = A_HEADS // 2

    def body(q_ref, k_ref, v_ref, aq_ref, ak_ref, o_ref, lse_ref, m_sc, l_sc, acc_sc):
        i = pl.program_id(1)
        j = pl.program_id(2)
        lo = _first_head((T, LANES))

        @pl.when(j == 0)
        def _():
            m_sc[...] = jnp.full((2, T, LANES), NEG_INF, F32)
            l_sc[...] = jnp.zeros((2, T, LANES), F32)
            acc_sc[...] = jnp.zeros((T, LANES), F32)

        def step(diagonal):
            qs = q_ref[...] * jnp.asarray(SCALE, BF16)
            aq, ak, kv = aq_ref[...], ak_ref[...], k_ref[...]
            q2 = (jnp.where(lo, qs, aq), jnp.where(lo, aq, qs))
            k2 = (jnp.where(lo, kv, ak), jnp.where(lo, ak, kv))
            if diagonal:
                causal = lax.broadcasted_iota(jnp.int32, (T, T), 0) >= lax.broadcasted_iota(jnp.int32, (T, T), 1)
            pv, alphas = None, []
            for h, vh in enumerate(_each_head(v_ref[...], lo)):
                s = lax.dot_general(q2[h], k2[h], _DN["nt"], preferred_element_type=F32)
                if diagonal:
                    s = jnp.where(causal, s, NEG_INF)
                m_prev = m_sc[h]
                m_new = jnp.maximum(m_prev, jnp.max(s, axis=1, keepdims=True))
                alpha = jnp.exp(m_prev - m_new)
                p = jnp.exp(s - jnp.tile(m_new, (1, T // LANES)))
                l_sc[h] = alpha * l_sc[h] + jnp.sum(p, axis=1, keepdims=True)
                m_sc[h] = m_new
                d = lax.dot_general(p.astype(BF16), vh, _DN["nn"], preferred_element_type=F32)
                pv = d if pv is None else pv + d
                alphas.append(alpha)
            acc_sc[...] = jnp.where(lo, alphas[0], alphas[1]) * acc_sc[...] + pv

        @pl.when(j < i)
        def _():
            step(False)

        @pl.when(j == i)
        def _():
            step(True)
            o_ref[...] = (acc_sc[...] * jnp.where(lo, 1.0 / l_sc[0], 1.0 / l_sc[1])).astype(BF16)
            for h in range(2):
                lse_ref[h] = (m_sc[h] + jnp.log(l_sc[h]))[:, 0:1]

    qs_ = pl.BlockSpec((T, LANES), lambda p, i, j: (i, p))
    ks = pl.BlockSpec((T, LANES), lambda p, i, j: (jnp.minimum(i, j), NP + p))
    vs = pl.BlockSpec((T, LANES), lambda p, i, j: (jnp.minimum(i, j), 2 * NP + p))
    aks = pl.BlockSpec((T, LANES), lambda p, i, j: (jnp.minimum(i, j), p))
    col = pl.BlockSpec((2, T, 1), lambda p, i, j: (p, i, 0))
    return pl.pallas_call(
        body, out_shape=[jax.ShapeDtypeStruct((S, A_HEADS * HEAD_DIM), BF16), jax.ShapeDtypeStruct((A_HEADS, S, 1), F32)],
        grid=(NP, nq, nq), in_specs=[qs_, ks, vs, qs_, aks], out_specs=[qs_, col],
        scratch_shapes=[pltpu.VMEM((2, T, LANES), F32), pltpu.VMEM((2, T, LANES), F32), pltpu.VMEM((T, LANES), F32)],
        compiler_params=_cparams(("parallel", "parallel", "arbitrary")), name=name,
    )(qkv, qkv, qkv, aug_q, aug_k)


def _fox_pair_bwd(qkv, do, lse_row, delta_row, aug_q, aug_k, name):
    T = FOX_T
    nq = S // T
    NP = A_HEADS // 2

    def body(q_ref, k_ref, v_ref, do_ref, lse_ref, dl_ref, aq_ref, ak_ref, dq_ref, dk_ref, dv_ref, dc_ref, dcq_ref,
             dq_sc, dk_sc, dv_sc, dc_sc):
        j = pl.program_id(1)
        i = pl.program_id(2)
        lo = _first_head((T, LANES))

        @pl.when(jnp.logical_and(j == 0, i == 0))
        def _():
            dq_sc[...] = jnp.zeros((S, LANES), F32)
            dcq_ref[...] = jnp.zeros((2, nq, 1, T), F32)

        @pl.when(i == j)
        def _():
            dk_sc[...] = jnp.zeros((T, LANES), F32)
            dv_sc[...] = jnp.zeros((T, LANES), F32)
            dc_sc[...] = jnp.zeros((2, T, 1), F32)

        def step(diagonal):
            qv = q_ref[...]
            kv = k_ref[...]
            dov = do_ref[...].astype(BF16)
            qs = qv * jnp.asarray(SCALE, BF16)
            aq, ak = aq_ref[...], ak_ref[...]
            q2 = (jnp.where(lo, qs, aq), jnp.where(lo, aq, qs))
            k2 = (jnp.where(lo, kv, ak), jnp.where(lo, ak, kv))
            if diagonal:
                causal = lax.broadcasted_iota(jnp.int32, (T, T), 1) >= lax.broadcasted_iota(jnp.int32, (T, T), 0)
            dv = dk = dq = None
            for h, (kh, vh, qh, doh) in enumerate(zip(_each_head(kv, lo), _each_head(v_ref[...], lo),
                                                      _each_head(qv, lo), _each_head(dov, lo))):
                st = lax.dot_general(k2[h], q2[h], _DN["nt"], preferred_element_type=F32)
                if diagonal:
                    st = jnp.where(causal, st, NEG_INF)
                pt = jnp.exp(st - lse_ref[h])
                d = lax.dot_general(pt.astype(BF16), doh, _DN["nn"], preferred_element_type=F32)
                dv = d if dv is None else dv + d
                dpt = lax.dot_general(vh, dov, _DN["nt"], preferred_element_type=F32)
                dst = pt * (dpt - dl_ref[h])
                dc_sc[h] -= jnp.sum(dst, axis=1, keepdims=True)
                dcq_ref[h, i] += jnp.sum(dst, axis=0, keepdims=True)
                dsb = (dst * SCALE).astype(BF16)
                d = lax.dot_general(dsb, qh, _DN["nn"], preferred_element_type=F32)
                dk = d if dk is None else dk + d
                d = lax.dot_general(dsb, kh, _DN["tn"], preferred_element_type=F32)
                dq = d if dq is None else dq + d
            dv_sc[...] += dv
            dk_sc[...] += dk
            rows = pl.ds(pl.multiple_of(i * T, T), T)
            dq_sc[rows, :] += dq

        @pl.when(i > j)
        def _():
            step(False)

        @pl.when(i == j)
        def _():
            step(True)

        @pl.when(i == nq - 1)
        def _():
            dk_ref[...] = dk_sc[...].astype(BF16)
            dv_ref[...] = dv_sc[...].astype(BF16)
            dc_ref[...] = dc_sc[...]

        @pl.when(jnp.logical_and(j == nq - 1, i == nq - 1))
        def _():
            dq_ref[...] = dq_sc[...].astype(BF16)

    qs = pl.BlockSpec((T, LANES), lambda p, j, i: (jnp.maximum(i, j), p))
    qrow = pl.BlockSpec((2, 1, T), lambda p, j, i: (p, 0, jnp.maximum(i, j)))
    ks = pl.BlockSpec((T, LANES), lambda p, j, i: (j, NP + p))
    vs = pl.BlockSpec((T, LANES), lambda p, j, i: (j, 2 * NP + p))
    kout = pl.BlockSpec((T, LANES), lambda p, j, i: (j, p))
    kcol = pl.BlockSpec((2, T, 1), lambda p, j, i: (p, j, 0))
    dqs = pl.BlockSpec((S, LANES), lambda p, j, i: (0, p))
    dcqs = pl.BlockSpec((2, nq, 1, T), lambda p, j, i: (p, 0, 0, 0))
    wide = jax.ShapeDtypeStruct((S, A_HEADS * HEAD_DIM), BF16)
    return pl.pallas_call(
        body,
        out_shape=[wide, wide, wide, jax.ShapeDtypeStruct((A_HEADS, S, 1), F32),
                   jax.ShapeDtypeStruct((A_HEADS, nq, 1, T), F32)],
        grid=(NP, nq, nq), in_specs=[qs, ks, vs, qs, qrow, qrow, qs, kout], out_specs=[dqs, kout, kout, kcol, dcqs],
        scratch_shapes=[pltpu.VMEM((S, LANES), F32), pltpu.VMEM((T, LANES), F32), pltpu.VMEM((T, LANES), F32),
                        pltpu.VMEM((2, T, 1), F32)],
        compiler_params=_cparams(("parallel", "arbitrary", "arbitrary")), name=name,
    )(qkv, qkv, qkv, do, lse_row, delta_row, aug_q, aug_k)


def _pair_rowdot(a, b, name):
    n = a.shape[1] // HEAD_DIM
    T = 1024

    def body(a_ref, b_ref, o_ref):
        prod = a_ref[...].astype(F32) * b_ref[...].astype(F32)
        lo = _first_head(prod.shape)
        o_ref[0] = jnp.sum(jnp.where(lo, prod, 0.0), axis=1, keepdims=True)
        o_ref[1] = jnp.sum(jnp.where(lo, 0.0, prod), axis=1, keepdims=True)

    blk = pl.BlockSpec((T, LANES), lambda p, i: (i, p))
    return pl.pallas_call(
        body, out_shape=jax.ShapeDtypeStruct((n, S, 1), F32), grid=(n // 2, S // T), in_specs=[blk, blk],
        out_specs=pl.BlockSpec((2, T, 1), lambda p, i: (p, i, 0)),
        compiler_params=_cparams(("parallel", "parallel")), name=name,
    )(a, b)


W = B_WIN
N_HG = 3 * B_HPG
N_BLK = S // W


def _dil_tables():
    slopes = np.exp2((-8.0 * np.arange(1, N_HG + 1, dtype=np.float32) / N_HG).astype(np.float32)).astype(np.float32)
    dil = np.repeat(np.array([d for _, d in B_GROUPS], np.float32), B_HPG)
    coef = (slopes * dil).astype(np.float32)
    nbs = np.repeat(np.array([S // d // W for _, d in B_GROUPS], np.int32), B_HPG)
    return jnp.asarray(coef), jnp.asarray(nbs)


DIL_SUB = 8
DIL_ROWS = DIL_SUB * W
DIL_STEPS = S // DIL_ROWS


def _dil_bias(coef, transposed):
    row = lax.broadcasted_iota(jnp.int32, (W, 2 * W), 0)
    col = lax.broadcasted_iota(jnp.int32, (W, 2 * W), 1)
    dist = (col - row) if transposed else (row + W - col)
    valid = jnp.logical_and(dist >= 0, dist <= W)
    return jnp.where(valid, -coef * dist.astype(F32), NEG_INF), col


NPG = B_HPG // 2
GROUP_W = B_HPG * HEAD_DIM


def _dil_pair_specs(qoff, koff, voff):
    prev_blk = lambda n: jnp.maximum(n * DIL_SUB - 1, 0)
    next_blk = lambda n: jnp.minimum((n + 1) * DIL_SUB, N_BLK - 1)
    return dict(
        o=pl.BlockSpec((DIL_ROWS, LANES), lambda h, n: (n, h)),
        o_next=pl.BlockSpec((W, LANES), lambda h, n: (next_blk(n), h)),
        q=pl.BlockSpec((DIL_ROWS, LANES), lambda h, n: (n, qoff + h)),
        q_next=pl.BlockSpec((W, LANES), lambda h, n: (next_blk(n), qoff + h)),
        k=pl.BlockSpec((DIL_ROWS, LANES), lambda h, n: (n, koff + h)),
        k_prev=pl.BlockSpec((W, LANES), lambda h, n: (prev_blk(n), koff + h)),
        v=pl.BlockSpec((DIL_ROWS, LANES), lambda h, n: (n, voff + h)),
        v_prev=pl.BlockSpec((W, LANES), lambda h, n: (prev_blk(n), voff + h)),
        col=pl.BlockSpec((2, DIL_ROWS, 1), lambda h, n: (h, n, 0)),
        col2=pl.BlockSpec((2, DIL_ROWS, 1), lambda h, n: (NPG + h, n, 0)),
        row=pl.BlockSpec((2, 1, DIL_ROWS), lambda h, n: (h, 0, n)),
        row_next=pl.BlockSpec((2, 1, W), lambda h, n: (h, 0, next_blk(n))),
        row2=pl.BlockSpec((2, 1, DIL_ROWS), lambda h, n: (NPG + h, 0, n)),
        row2_next=pl.BlockSpec((2, 1, W), lambda h, n: (NPG + h, 0, next_blk(n))),
        smem=pl.BlockSpec(memory_space=pltpu.SMEM))


def _dil_pair_fwd(g, q, k, v, qoff, koff, voff, name):
    coef_t, nbs_t = _dil_tables()

    def body(coef_ref, nbs_ref, q_ref, kh_ref, k_ref, vh_ref, v_ref, o_ref, lse_ref, kf, vf):
        hp = pl.program_id(0)
        n = pl.program_id(1)
        nbs = nbs_ref[B_HPG * g + 2 * hp]
        kf[0:W, :] = kh_ref[...]
        kf[W:, :] = k_ref[...]
        vf[0:W, :] = vh_ref[...]
        vf[W:, :] = v_ref[...]
        biases = [_dil_bias(coef_ref[B_HPG * g + 2 * hp + h], False) for h in range(2)]
        col = biases[0][1]
        lo = _first_head((W, LANES))
        lo2 = _first_head((2 * W, LANES))
        for b in range(DIL_SUB):
            first = lax.rem(n * DIL_SUB + b, nbs) == 0
            rows = slice(b * W, (b + 1) * W)
            both = slice(b * W, (b + 2) * W)
            qv = q_ref[rows, :]
            acc, inv = None, []
            for h, (kh, vh) in enumerate(zip(_each_head(kf[both, :], lo2), _each_head(vf[both, :], lo2))):
                s = lax.dot_general(qv, kh, _DN["nt"], preferred_element_type=F32) * SCALE + biases[h][0]
                s = jnp.where(jnp.logical_and(first, col < W), NEG_INF, s)
                m = jnp.max(s, axis=1, keepdims=True)
                p = jnp.exp(s - m)
                l = jnp.sum(p, axis=1, keepdims=True)
                d = lax.dot_general(p.astype(BF16), vh, _DN["nn"], preferred_element_type=F32)
                acc = d if acc is None else acc + d
                inv.append(1.0 / l)
                lse_ref[h, rows, :] = m + jnp.log(l)
            o_ref[rows, :] = acc * jnp.where(lo, inv[0], inv[1])

    sp = _dil_pair_specs(qoff, koff, voff)
    return pl.pallas_call(
        body, out_shape=[jax.ShapeDtypeStruct((S, GROUP_W), F32), jax.ShapeDtypeStruct((B_HPG, S, 1), F32)],
        grid=(NPG, DIL_STEPS), in_specs=[sp["smem"], sp["smem"], sp["q"], sp["k_prev"], sp["k"], sp["v_prev"], sp["v"]],
        out_specs=[sp["o"], sp["col"]], scratch_shapes=[pltpu.VMEM((DIL_ROWS + W, LANES), BF16)] * 2,
        compiler_params=_cparams(("parallel", "parallel")), name=name,
    )(coef_t, nbs_t, q, k, k, v, v)


def _dil_pair_merge(os, lses, name):
    T = 1024

    def body(o0_ref, o1_ref, o2_ref, l0_ref, l1_ref, l2_ref, om_ref, omb_ref, l_ref):
        lo = _first_head((T, LANES))
        weights = []
        for h in range(2):
            l0, l1, l2 = l0_ref[h], l1_ref[h], l2_ref[h]
            m = jnp.maximum(jnp.maximum(l0, l1), l2)
            e0, e1, e2 = jnp.exp(l0 - m), jnp.exp(l1 - m), jnp.exp(l2 - m)
            den = e0 + e1 + e2
            weights.append((e0 / den, e1 / den, e2 / den))
            l_ref[h] = m + jnp.log(den)
        om = (jnp.where(lo, weights[0][0], weights[1][0]) * o0_ref[...]
              + jnp.where(lo, weights[0][1], weights[1][1]) * o1_ref[...]
              + jnp.where(lo, weights[0][2], weights[1][2]) * o2_ref[...])
        om_ref[...] = om
        omb_ref[...] = om.astype(BF16)

    ob = pl.BlockSpec((T, LANES), lambda p, i: (i, p))
    lb = pl.BlockSpec((2, T, 1), lambda p, i: (p, i, 0))
    return pl.pallas_call(
        body,
        out_shape=[jax.ShapeDtypeStruct((S, B_OUT_W), F32), jax.ShapeDtypeStruct((S, B_OUT_W), BF16),
                   jax.ShapeDtypeStruct((B_HPG, S, 1), F32)],
        grid=(NPG, S // T), in_specs=[ob] * 3 + [lb] * 3, out_specs=[ob, ob, lb],
        compiler_params=_cparams(("parallel", "parallel")), name=name,
    )(*os, *lses)


def _dil_pair_dq(g, q, k, v, qoff, koff, voff, do, stats, name):
    coef_t, nbs_t = _dil_tables()

    def body(coef_ref, nbs_ref, q_ref, kh_ref, k_ref, vh_ref, v_ref, do_ref, l_ref, d_ref, dq_ref, kf, vf):
        hp = pl.program_id(0)
        n = pl.program_id(1)
        nbs = nbs_ref[B_HPG * g + 2 * hp]
        kf[0:W, :] = kh_ref[...]
        kf[W:, :] = k_ref[...]
        vf[0:W, :] = vh_ref[...]
        vf[W:, :] = v_ref[...]
        biases = [_dil_bias(coef_ref[B_HPG * g + 2 * hp + h], False) for h in range(2)]
        col = biases[0][1]
        lo2 = _first_head((2 * W, LANES))
        for b in range(DIL_SUB):
            first = lax.rem(n * DIL_SUB + b, nbs) == 0
            rows = slice(b * W, (b + 1) * W)
            both = slice(b * W, (b + 2) * W)
            qv = q_ref[rows, :]
            dov = do_ref[rows, :]
            acc = None
            for h, (kh, vh) in enumerate(zip(_each_head(kf[both, :], lo2), _each_head(vf[both, :], lo2))):
                s = lax.dot_general(qv, kh, _DN["nt"], preferred_element_type=F32) * SCALE + biases[h][0]
                s = jnp.where(jnp.logical_and(first, col < W), NEG_INF, s)
                p = jnp.exp(s - l_ref[h, rows, :])
                dp = lax.dot_general(dov, vh, _DN["nt"], preferred_element_type=F32)
                ds = (p * (dp - d_ref[h, rows, :]) * SCALE).astype(BF16)
                d = lax.dot_general(ds, kh, _DN["nn"], preferred_element_type=F32)
                acc = d if acc is None else acc + d
            dq_ref[rows, :] = acc.astype(BF16)

    sp = _dil_pair_specs(qoff, koff, voff)
    return pl.pallas_call(
        body, out_shape=jax.ShapeDtypeStruct((S, GROUP_W), BF16), grid=(NPG, DIL_STEPS),
        in_specs=[sp["smem"], sp["smem"], sp["q"], sp["k_prev"], sp["k"], sp["v_prev"], sp["v"], sp["o"], sp["col"],
                  sp["col2"]],
        out_specs=sp["o"], scratch_shapes=[pltpu.VMEM((DIL_ROWS + W, LANES), BF16)] * 2,
        compiler_params=_cparams(("parallel", "parallel")), name=name,
    )(coef_t, nbs_t, q, k, k, v, v, do, stats, stats)


def _dil_pair_dkv(g, q, k, v, qoff, koff, voff, do, stats, name):
    coef_t, nbs_t = _dil_tables()

    def body(coef_ref, nbs_ref, k_ref, v_ref, q_ref, qn_ref, do_ref, don_ref, l_ref, ln_ref, d_ref, dn_ref,
             dk_ref, dv_ref, qf, dof, lf, df):
        hp = pl.program_id(0)
        n = pl.program_id(1)
        nbs = nbs_ref[B_HPG * g + 2 * hp]
        qf[0:DIL_ROWS, :] = q_ref[...]
        qf[DIL_ROWS:, :] = qn_ref[...]
        dof[0:DIL_ROWS, :] = do_ref[...]
        dof[DIL_ROWS:, :] = don_ref[...]
        lf[:, :, 0:DIL_ROWS] = l_ref[...]
        lf[:, :, DIL_ROWS:] = ln_ref[...]
        df[:, :, 0:DIL_ROWS] = d_ref[...]
        df[:, :, DIL_ROWS:] = dn_ref[...]
        biases = [_dil_bias(coef_ref[B_HPG * g + 2 * hp + h], True) for h in range(2)]
        col = biases[0][1]
        lo = _first_head((W, LANES))
        lo2 = _first_head((2 * W, LANES))
        for b in range(DIL_SUB):
            no_next = lax.rem(n * DIL_SUB + b + 1, nbs) == 0
            rows = slice(b * W, (b + 1) * W)
            both = slice(b * W, (b + 2) * W)
            dd = dof[both, :]
            dk = dv = None
            for h, (kh, vh, qh, ddh) in enumerate(zip(_each_head(k_ref[rows, :], lo), _each_head(v_ref[rows, :], lo),
                                                      _each_head(qf[both, :], lo2), _each_head(dd, lo2))):
                st = lax.dot_general(kh, qh, _DN["nt"], preferred_element_type=F32) * SCALE + biases[h][0]
                st = jnp.where(jnp.logical_and(no_next, col >= W), NEG_INF, st)
                pt = jnp.exp(st - lf[h, :, both])
                d = lax.dot_general(pt.astype(BF16), ddh, _DN["nn"], preferred_element_type=F32)
                dv = d if dv is None else dv + d
                dpt = lax.dot_general(vh, dd, _DN["nt"], preferred_element_type=F32)
                dst = (pt * (dpt - df[h, :, both]) * SCALE).astype(BF16)
                d = lax.dot_general(dst, qh, _DN["nn"], preferred_element_type=F32)
                dk = d if dk is None else dk + d
            dk_ref[rows, :] = dk.astype(BF16)
            dv_ref[rows, :] = dv.astype(BF16)

    sp = _dil_pair_specs(qoff, koff, voff)
    wide = jax.ShapeDtypeStruct((S, GROUP_W), BF16)
    return pl.pallas_call(
        body, out_shape=[wide, wide], grid=(NPG, DIL_STEPS),
        in_specs=[sp["smem"], sp["smem"], sp["k"], sp["v"], sp["q"], sp["q_next"], sp["o"], sp["o_next"], sp["row"],
                  sp["row_next"], sp["row2"], sp["row2_next"]],
        out_specs=[sp["o"], sp["o"]],
        scratch_shapes=[pltpu.VMEM((DIL_ROWS + W, LANES), BF16)] * 2 + [pltpu.VMEM((2, 1, DIL_ROWS + W), F32)] * 2,
        compiler_params=_cparams(("parallel", "parallel")), name=name,
    )(coef_t, nbs_t, k, v, q, q, do, do, stats, stats, stats, stats)


FFN_ROWS = 512
FFN_COLS = 256
HALO = 8


def _shifted(u, halo, back):
    T = u.shape[0]
    rows = lax.broadcasted_iota(jnp.int32, u.shape, 0)
    if back:
        s1 = jnp.where(rows == 0, halo[HALO - 1:HALO, :], pltpu.roll(u, 1, 0))
        s2 = jnp.where(rows == 0, halo[HALO - 2:HALO - 1, :],
                       jnp.where(rows == 1, halo[HALO - 1:HALO, :], pltpu.roll(u, 2, 0)))
    else:
        s1 = jnp.where(rows == T - 1, halo[0:1, :], pltpu.roll(u, T - 1, 0))
        s2 = jnp.where(rows == T - 1, halo[1:2, :],
                       jnp.where(rows == T - 2, halo[0:1, :], pltpu.roll(u, T - 2, 0)))
    return s1, s2


def _conv_parts(u_ref, h_ref, w_ref, b_ref, first):
    out = []
    for p in range(2):
        u = u_ref[p]
        halo = jnp.where(first, 0.0, h_ref[p])
        u1, u2 = _shifted(u, halo, True)
        w = w_ref[p]
        out.append((w[0:1, :] * u2 + w[1:2, :] * u1 + w[2:3, :] * u + b_ref[p], u1, u2, u))
    return out


def _ffn_specs():
    T, C = FFN_ROWS, FFN_COLS
    blk = pl.BlockSpec((2, T, C), lambda j, i: (0, i, j))
    prev = pl.BlockSpec((2, HALO, C), lambda j, i: (0, jnp.maximum(i * (T // HALO) - 1, 0), j))
    nxt = pl.BlockSpec((2, HALO, C), lambda j, i: (0, jnp.minimum((i + 1) * (T // HALO), S // HALO - 1), j))
    wsp = pl.BlockSpec((2, 3, C), lambda j, i: (0, 0, j))
    bsp = pl.BlockSpec((2, 1, C), lambda j, i: (0, 0, j))
    one = pl.BlockSpec((T, C), lambda j, i: (i, j))
    return blk, prev, nxt, wsp, bsp, one


def _ffn_act_fwd(u, w, b, name):
    blk, prev, _, wsp, bsp, one = _ffn_specs()

    def body(u_ref, h_ref, w_ref, b_ref, o_ref):
        (a, _, _, _), (g, _, _, _) = _conv_parts(u_ref, h_ref, w_ref, b_ref, pl.program_id(1) == 0)
        o_ref[...] = (g / (1.0 + jnp.exp(-g)) * a).astype(BF16)

    return pl.pallas_call(
        body, out_shape=jax.ShapeDtypeStruct((S, FF), BF16), grid=(FF // FFN_COLS, S // FFN_ROWS),
        in_specs=[blk, prev, wsp, bsp], out_specs=one,
        compiler_params=_cparams(("parallel", "parallel")), name=name,
    )(u, u, w, b)


def _ffn_act_bwd(u, dact, w, b, name):
    blk, prev, _, wsp, bsp, one = _ffn_specs()

    def body(u_ref, h_ref, da_ref, w_ref, b_ref, duc_ref, dwb_ref):
        i = pl.program_id(1)
        (a, a1, a2, a0), (g, g1, g2, g0) = _conv_parts(u_ref, h_ref, w_ref, b_ref, i == 0)
        dact_v = da_ref[...]
        sg = 1.0 / (1.0 + jnp.exp(-g))
        d_a = dact_v * (g * sg)
        d_g = dact_v * a * (sg * (1.0 + g * (1.0 - sg)))
        duc_ref[0] = d_a
        duc_ref[1] = d_g

        @pl.when(i == 0)
        def _():
            dwb_ref[...] = jnp.zeros(dwb_ref.shape, F32)

        for p, (d, s2, s1, s0) in enumerate(((d_a, a2, a1, a0), (d_g, g2, g1, g0))):
            dwb_ref[p, 0:1, :] += jnp.sum(d * s2, axis=0, keepdims=True)
            dwb_ref[p, 1:2, :] += jnp.sum(d * s1, axis=0, keepdims=True)
            dwb_ref[p, 2:3, :] += jnp.sum(d * s0, axis=0, keepdims=True)
            dwb_ref[p, 3:4, :] += jnp.sum(d, axis=0, keepdims=True)

    return pl.pallas_call(
        body, out_shape=[jax.ShapeDtypeStruct((2, S, FF), F32), jax.ShapeDtypeStruct((2, 8, FF), F32)],
        grid=(FF // FFN_COLS, S // FFN_ROWS), in_specs=[blk, prev, one, wsp, bsp],
        out_specs=[blk, pl.BlockSpec((2, 8, FFN_COLS), lambda j, i: (0, 0, j))],
        compiler_params=_cparams(("parallel", "arbitrary")), name=name,
    )(u, u, dact, w, b)


def _ffn_conv_bwd(duc, w, name):
    blk, _, nxt, wsp, _, _ = _ffn_specs()
    last = S // FFN_ROWS - 1

    def body(d_ref, h_ref, w_ref, du_ref):
        is_last = pl.program_id(1) == last
        for p in range(2):
            d = d_ref[p]
            halo = jnp.where(is_last, 0.0, h_ref[p])
            d1, d2 = _shifted(d, halo, False)
            wv = w_ref[p]
            du_ref[p] = (wv[2:3, :] * d + wv[1:2, :] * d1 + wv[0:1, :] * d2).astype(BF16)

    return pl.pallas_call(
        body, out_shape=jax.ShapeDtypeStruct((2, S, FF), BF16), grid=(FF // FFN_COLS, S // FFN_ROWS),
        in_specs=[blk, nxt, wsp], out_specs=blk,
        compiler_params=_cparams(("parallel", "parallel")), name=name,
    )(duc, duc, w)


def _adam_update(w, gv, m, v):
    c1 = 1.0 / (1.0 - ADAM_B1 ** ADAM_STEP)
    c2 = 1.0 / (1.0 - ADAM_B2 ** ADAM_STEP)
    mn = ADAM_B1 * m + (1.0 - ADAM_B1) * gv
    vn = ADAM_B2 * v + (1.0 - ADAM_B2) * (gv * gv)
    return -ADAM_LR * ((mn * c1) / (jnp.sqrt(vn * c2) + ADAM_EPS) + ADAM_WD * w), mn, vn


def _adamw(w, g, m, v, name):
    rows = w.shape[0]
    T = 8
    for cand in (256, 128, 64, 32, 16, 8):
        if rows % cand == 0:
            T = cand
            break

    def body(w_ref, g_ref, m_ref, v_ref, d_ref, mo_ref, vo_ref):
        d_ref[...], mo_ref[...], vo_ref[...] = _adam_update(w_ref[...], g_ref[...], m_ref[...], v_ref[...])

    blk = pl.BlockSpec((T, w.shape[1]), lambda i: (i, 0))
    sds = jax.ShapeDtypeStruct(w.shape, F32)
    return pl.pallas_call(
        body, out_shape=[sds, sds, sds], grid=(rows // T,), in_specs=[blk] * 4, out_specs=[blk] * 3,
        compiler_params=_cparams(("parallel",)), name=name,
    )(w, g, m, v)


ANY = pl.BlockSpec(memory_space=pl.ANY)


def _place():
    x, y, c = lax.axis_index("x"), lax.axis_index("y"), lax.axis_index("c")
    chips = [(1 - x, y), (x, 1 - y), (1 - x, 1 - y)]
    return x, y, c, chips


def _place_own(w, slot_arr, name):
    rows = w.shape[0]
    T = 16
    for cand in (2048, 1024, 512, 256, 128, 64, 32, 16):
        if rows % cand == 0:
            T = cand
            break

    def body(k_ref, w_ref, o_ref):
        o_ref[...] = w_ref[...]

    return pl.pallas_call(
        body, out_shape=jax.ShapeDtypeStruct((N_CHIPS, rows, FLAT_W), w.dtype),
        grid_spec=pltpu.PrefetchScalarGridSpec(
            num_scalar_prefetch=1, grid=(rows // T,),
            in_specs=[pl.BlockSpec((T, FLAT_W), lambda i, k: (i, 0))],
            out_specs=pl.BlockSpec((None, T, FLAT_W), lambda i, k: (k[0], i, 0))),
        compiler_params=_cparams(("parallel",)), name=name,
    )(slot_arr, w)


def _allgather_shards(w, buf):
    half_rows = w.shape[0] // 2
    assert half_rows % 16 == 0

    def body(w_ref, buf_ref, g_ref, send_sems, recv_sems):
        x, y, c, chips = _place()
        myk = 2 * x + y
        sibling = (x, y, 1 - c)
        h0 = pl.multiple_of(c * half_rows, 16)
        h1 = pl.multiple_of((1 - c) * half_rows, 16)

        def half(k, start):
            return g_ref.at[k, pl.ds(start, half_rows), :]

        def rcopy(sem, src, dst, to):
            return pltpu.make_async_remote_copy(src_ref=src, dst_ref=dst, send_sem=send_sems.at[sem],
                                                recv_sem=recv_sems.at[sem], device_id=to, device_id_type=MESH)

        ici = [rcopy(r, w_ref.at[pl.ds(h0, half_rows), :], half(myk, h0), (*chip, c)) for r, chip in enumerate(chips)]
        for cp in ici:
            cp.start()
        ks = [2 * cx + cy for cx, cy in chips]
        fwd = [rcopy(3 + r, half(ks[r], h0), half(ks[r], h0), sibling) for r in range(3)]
        for r in range(3):
            rcopy(r, half(ks[r], h0), half(ks[r], h0), (*chips[r], c)).wait_recv()
            fwd[r].start()
        for r in range(3):
            rcopy(3 + r, half(ks[r], h1), half(ks[r], h1), sibling).wait_recv()
        for cp in ici + fwd:
            cp.wait_send()

    return pl.pallas_call(
        body, out_shape=jax.ShapeDtypeStruct(buf.shape, w.dtype), in_specs=[ANY, ANY], out_specs=ANY,
        scratch_shapes=[pltpu.SemaphoreType.DMA((6,)), pltpu.SemaphoreType.DMA((6,))],
        input_output_aliases={1: 0},
        compiler_params=pltpu.CompilerParams(has_side_effects=True), name="allgather_shards",
    )(w, buf)


HBM_SPEC = pl.BlockSpec(memory_space=pltpu.HBM)
SEM_SPEC = pl.BlockSpec(memory_space=pltpu.SEMAPHORE)
DATAFLOW = pltpu.SideEffectType.DATAFLOW_SIDE_EFFECTING
OWN_SLOT = 3


def _late_gather_start(w, land):
    def body(w_ref, land_ref, send_sems, recv_sems, w_thru, land_thru, token):
        x, y, c, chips = _place()
        for r, chip in enumerate(chips):
            pltpu.make_async_remote_copy(src_ref=w_ref, dst_ref=land_ref.at[r], send_sem=send_sems.at[r],
                                         recv_sem=recv_sems.at[r], device_id=(*chip, c), device_id_type=MESH).start()
        token[...] = jnp.zeros_like(token)

    return pl.pallas_call(
        body, name="late_gather_start",
        out_shape=(pltpu.SemaphoreType.DMA((3,)), pltpu.SemaphoreType.DMA((3,)), pltpu.HBM(w.shape, w.dtype),
                   pltpu.HBM(land.shape, land.dtype), jax.ShapeDtypeStruct((8, LANES), F32)),
        in_specs=(HBM_SPEC, HBM_SPEC),
        out_specs=(SEM_SPEC, SEM_SPEC, HBM_SPEC, HBM_SPEC, pl.BlockSpec(memory_space=pltpu.VMEM)),
        input_output_aliases={0: 2, 1: 3}, compiler_params=pltpu.CompilerParams(has_side_effects=DATAFLOW),
    )(pltpu.with_memory_space_constraint(w, pltpu.HBM), pltpu.with_memory_space_constraint(land, pltpu.HBM))


def _late_gather_wait(send_sems, recv_sems, w_thru, land_thru, after):
    def body(w_ref, land_ref, send_sems, recv_sems, after_ref, w_dead, got_ref):
        x, y, c, chips = _place()
        for r, chip in enumerate(chips):
            cp = pltpu.make_async_remote_copy(src_ref=w_ref, dst_ref=land_ref.at[r], send_sem=send_sems.at[r],
                                              recv_sem=recv_sems.at[r], device_id=(*chip, c), device_id_type=MESH)
            cp.wait_send()
            cp.wait_recv()

    return pl.pallas_call(
        body, name="late_gather_wait",
        out_shape=(pltpu.HBM(w_thru.shape, w_thru.dtype), pltpu.HBM(land_thru.shape, land_thru.dtype)),
        in_specs=(HBM_SPEC, HBM_SPEC, SEM_SPEC, SEM_SPEC, pl.BlockSpec(memory_space=pl.ANY)),
        out_specs=(HBM_SPEC, HBM_SPEC), input_output_aliases={0: 0, 1: 1},
        compiler_params=pltpu.CompilerParams(has_side_effects=DATAFLOW),
    )(w_thru, land_thru, send_sems, recv_sems, after)


def _flat_tile(rows):
    return next(t for t in (2048, 1024, 512, 256, 128, 64, 32, 16) if rows % t == 0)


def _sibling_swap_half(g, tag):
    half = g.shape[1] // 2

    def body(g_ref, o_ref, send_sem, recv_sem):
        x, y, c, _ = _place()
        theirs = pl.multiple_of((1 - c) * half, 8)
        cp = pltpu.make_async_remote_copy(src_ref=g_ref.at[:, pl.ds(theirs, half), :], dst_ref=o_ref,
                                          send_sem=send_sem, recv_sem=recv_sem, device_id=(x, y, 1 - c),
                                          device_id_type=MESH)
        cp.start()
        cp.wait()

    return pl.pallas_call(
        body, out_shape=jax.ShapeDtypeStruct((N_CHIPS, half, FLAT_W), F32), in_specs=[ANY], out_specs=ANY,
        scratch_shapes=[pltpu.SemaphoreType.DMA, pltpu.SemaphoreType.DMA],
        compiler_params=pltpu.CompilerParams(has_side_effects=True), name=f"rs_sibling_swap_{tag}",
    )(g)


def _sibling_swap_start(g, land):
    half = g.shape[1] // 2

    def body(g_ref, land_ref, send_sem, recv_sem, g_thru, land_thru, token):
        x, y, c, _ = _place()
        theirs = pl.multiple_of((1 - c) * half, 8)
        pltpu.make_async_remote_copy(src_ref=g_ref.at[:, pl.ds(theirs, half), :], dst_ref=land_ref, send_sem=send_sem,
                                     recv_sem=recv_sem, device_id=(x, y, 1 - c), device_id_type=MESH).start()
        token[...] = jnp.zeros_like(token)

    return pl.pallas_call(
        body, name="rs_swap_start",
        out_shape=(pltpu.SemaphoreType.DMA(()), pltpu.SemaphoreType.DMA(()), pltpu.HBM(g.shape, g.dtype),
                   pltpu.HBM(land.shape, land.dtype), jax.ShapeDtypeStruct((8, LANES), F32)),
        in_specs=(HBM_SPEC, HBM_SPEC),
        out_specs=(SEM_SPEC, SEM_SPEC, HBM_SPEC, HBM_SPEC, pl.BlockSpec(memory_space=pltpu.VMEM)),
        input_output_aliases={0: 2, 1: 3}, compiler_params=pltpu.CompilerParams(has_side_effects=DATAFLOW),
    )(pltpu.with_memory_space_constraint(g, pltpu.HBM), pltpu.with_memory_space_constraint(land, pltpu.HBM))


def _sibling_swap_wait(send_sem, recv_sem, g_thru, land_thru, after):
    half = land_thru.shape[1]

    def body(g_ref, land_ref, send_sem, recv_sem, after_ref, g_done, got_ref):
        x, y, c, _ = _place()
        theirs = pl.multiple_of((1 - c) * half, 8)
        cp = pltpu.make_async_remote_copy(src_ref=g_ref.at[:, pl.ds(theirs, half), :], dst_ref=land_ref,
                                          send_sem=send_sem, recv_sem=recv_sem, device_id=(x, y, 1 - c),
                                          device_id_type=MESH)
        cp.wait_send()
        cp.wait_recv()

    return pl.pallas_call(
        body, name="rs_swap_wait",
        out_shape=(pltpu.HBM(g_thru.shape, g_thru.dtype), pltpu.HBM(land_thru.shape, land_thru.dtype)),
        in_specs=(HBM_SPEC, HBM_SPEC, SEM_SPEC, SEM_SPEC, pl.BlockSpec(memory_space=pl.ANY)),
        out_specs=(HBM_SPEC, HBM_SPEC), input_output_aliases={0: 0, 1: 1},
        compiler_params=pltpu.CompilerParams(has_side_effects=DATAFLOW),
    )(g_thru, land_thru, send_sem, recv_sem, after)


def _pair_sum(g, other, c_arr, tag):
    half = other.shape[1]
    T = _flat_tile(half)

    def body(c_ref, g_ref, o_ref, s_ref):
        s_ref[...] = (g_ref[...] + o_ref[...]).astype(BF16)

    nb = half // T
    return pl.pallas_call(
        body, out_shape=jax.ShapeDtypeStruct((N_CHIPS, half, FLAT_W), BF16),
        grid_spec=pltpu.PrefetchScalarGridSpec(
            num_scalar_prefetch=1, grid=(N_CHIPS, nb),
            in_specs=[pl.BlockSpec((None, T, FLAT_W), lambda k, i, c: (k, c[0] * nb + i, 0)),
                      pl.BlockSpec((None, T, FLAT_W), lambda k, i, c: (k, i, 0))],
            out_specs=pl.BlockSpec((None, T, FLAT_W), lambda k, i, c: (k, i, 0))),
        compiler_params=_cparams(("parallel", "parallel")), name=f"rs_pair_sum_{tag}",
    )(c_arr, g, other)


def _chip_exchange_start(s, land, tag):
    def body(s_ref, land_ref, send_sems, recv_sems, s_thru, land_thru, token):
        x, y, c, chips = _place()
        for r, (cx, cy) in enumerate(chips):
            pltpu.make_async_remote_copy(src_ref=s_ref.at[2 * cx + cy], dst_ref=land_ref.at[r], send_sem=send_sems.at[r],
                                         recv_sem=recv_sems.at[r], device_id=(cx, cy, c), device_id_type=MESH).start()
        token[...] = jnp.zeros_like(token)

    return pl.pallas_call(
        body, name=f"rs_exchange_start_{tag}",
        out_shape=(pltpu.SemaphoreType.DMA((3,)), pltpu.SemaphoreType.DMA((3,)), pltpu.HBM(s.shape, s.dtype),
                   pltpu.HBM(land.shape, land.dtype), jax.ShapeDtypeStruct((8, LANES), F32)),
        in_specs=(HBM_SPEC, HBM_SPEC),
        out_specs=(SEM_SPEC, SEM_SPEC, HBM_SPEC, HBM_SPEC, pl.BlockSpec(memory_space=pltpu.VMEM)),
        input_output_aliases={0: 2, 1: 3}, compiler_params=pltpu.CompilerParams(has_side_effects=DATAFLOW),
    )(pltpu.with_memory_space_constraint(s, pltpu.HBM), pltpu.with_memory_space_constraint(land, pltpu.HBM))


def _chip_exchange_wait(send_sems, recv_sems, s_thru, land_thru, after, tag):
    def body(s_ref, land_ref, send_sems, recv_sems, after_ref, s_done, got_ref):
        x, y, c, chips = _place()
        for r, (cx, cy) in enumerate(chips):
            cp = pltpu.make_async_remote_copy(src_ref=s_ref.at[2 * cx + cy], dst_ref=land_ref.at[r],
                                              send_sem=send_sems.at[r], recv_sem=recv_sems.at[r], device_id=(cx, cy, c),
                                              device_id_type=MESH)
            cp.wait_send()
            cp.wait_recv()

    return pl.pallas_call(
        body, name=f"rs_exchange_wait_{tag}",
        out_shape=(pltpu.HBM(s_thru.shape, s_thru.dtype), pltpu.HBM(land_thru.shape, land_thru.dtype)),
        in_specs=(HBM_SPEC, HBM_SPEC, SEM_SPEC, SEM_SPEC, pl.BlockSpec(memory_space=pl.ANY)),
        out_specs=(HBM_SPEC, HBM_SPEC), input_output_aliases={0: 0, 1: 1},
        compiler_params=pltpu.CompilerParams(has_side_effects=DATAFLOW),
    )(s_thru, land_thru, send_sems, recv_sems, after)


def _chip_sum(s, r, k_arr, tag):
    half = s.shape[1]
    T = _flat_tile(half)

    def body(k_ref, s_ref, r_ref, o_ref):
        o_ref[...] = ((s_ref[...].astype(F32) + r_ref[0].astype(F32)) + r_ref[1].astype(F32)) + r_ref[2].astype(F32)

    return pl.pallas_call(
        body, out_shape=jax.ShapeDtypeStruct((half, FLAT_W), F32),
        grid_spec=pltpu.PrefetchScalarGridSpec(
            num_scalar_prefetch=1, grid=(half // T,),
            in_specs=[pl.BlockSpec((None, T, FLAT_W), lambda i, k: (k[0], i, 0)),
                      pl.BlockSpec((3, T, FLAT_W), lambda i, k: (0, i, 0))],
            out_specs=pl.BlockSpec((T, FLAT_W), lambda i, k: (i, 0))),
        compiler_params=_cparams(("parallel",)), name=f"rs_chip_sum_{tag}",
    )(k_arr, s, r)


def _sibling_send(t, tag):
    def body(t_ref, o_ref, send_sem, recv_sem):
        x, y, c, _ = _place()
        cp = pltpu.make_async_remote_copy(src_ref=t_ref, dst_ref=o_ref, send_sem=send_sem, recv_sem=recv_sem,
                                          device_id=(x, y, 1 - c), device_id_type=MESH)
        cp.start()
        cp.wait()

    return pl.pallas_call(
        body, out_shape=jax.ShapeDtypeStruct(t.shape, F32), in_specs=[ANY], out_specs=ANY,
        scratch_shapes=[pltpu.SemaphoreType.DMA, pltpu.SemaphoreType.DMA],
        compiler_params=pltpu.CompilerParams(has_side_effects=True), name=f"rs_sibling_send_{tag}",
    )(t)


def _allreduce_small(v):
    def body(v_ref, o_ref, buf, send_sems, recv_sems):
        x, y, c, _ = _place()
        me = 4 * x + 2 * y + c
        buf[me] = v_ref[...]
        cps = []
        for mask in range(1, 8):
            a, b, d = (mask >> 2) & 1, (mask >> 1) & 1, mask & 1
            peer = (x + a - 2 * a * x, y + b - 2 * b * y, c + d - 2 * d * c)
            cps.append(pltpu.make_async_remote_copy(
                src_ref=v_ref, dst_ref=buf.at[me], send_sem=send_sems.at[mask - 1], recv_sem=recv_sems.at[mask - 1],
                device_id=peer, device_id_type=MESH))
        for cp in cps:
            cp.start()
        for cp in cps:
            cp.wait()
        total = buf[0]
        for dev in range(1, 8):
            total = total + buf[dev]
        o_ref[...] = total

    vm = pl.BlockSpec(memory_space=pltpu.VMEM)
    return pl.pallas_call(
        body, out_shape=jax.ShapeDtypeStruct((SMALL_ROWS, 1024), F32), in_specs=[vm], out_specs=vm,
        scratch_shapes=[pltpu.VMEM((8, SMALL_ROWS, 1024), F32), pltpu.SemaphoreType.DMA((7,)),
                        pltpu.SemaphoreType.DMA((7,))],
        compiler_params=pltpu.CompilerParams(has_side_effects=True), name="allreduce_small",
    )(v)


def _col_to_row(t):
    return t.reshape(t.shape[0], 1, S)


def _residue_rows(t, d, inverse=False):
    if d == 1:
        return t
    shape = (d, S // d) if inverse else (S // d, d)
    return t.reshape(shape + t.shape[1:]).transpose(1, 0, 2).reshape(t.shape)


def _residue_vecs(t, d, inverse=False):
    if d == 1:
        return t
    shape = (d, S // d) if inverse else (S // d, d)
    return t.reshape((t.shape[0],) + shape).transpose(0, 2, 1).reshape(t.shape)


def _ffn_fwd(x, g, w_up, cw, cb, w_down, tag):
    (h,) = _rms_fwd(x, [g], f"{tag}_norm")
    u = _mm(h, w_up, mode="nn", tm=1024, tn=1408, tk=1024, o_split=2, name=f"{tag}_up")
    act = _ffn_act_fwd(u, cw, cb, f"{tag}_act")
    x_out = _mm(act, w_down, mode="nn", tm=1024, tn=512, tk=FF, res=x, name=f"{tag}_down")
    return x_out, (h, u, act)


def _ffn_bwd(x, g, w_up, cw, cb, w_down, saved, dx, dxb, tag):
    h, u, act = saved
    d_w_down = _mm(act, dxb, mode="tn", tm=1408, tn=512, tk=1024, name=f"{tag}_dwdown")
    dact = _mm(dxb, w_down, mode="nt", tm=1024, tn=1408, tk=1024, name=f"{tag}_dact")
    duc, dwb = _ffn_act_bwd(u, dact, cw, cb, f"{tag}_dgate")
    du = _ffn_conv_bwd(duc, cw, f"{tag}_dconv")
    d_w_up = _mm(h, du, mode="tn", tm=1024, tn=1408, tk=1024, b_split=2, name=f"{tag}_dwup")
    dh = _mm(du, w_up, mode="nt", tm=1024, tn=512, tk=FF, a_split=1, name=f"{tag}_dh")
    dx_new, dxb_new, (dg,) = _rms_bwd(x, dx, [(g, dh)], f"{tag}_dnorm")
    d_cw = dwb[:, 0:3, :].transpose(1, 0, 2).reshape(3, 2 * FF)
    d_cb = dwb[:, 3, :].reshape(2 * FF)
    return dx_new, dxb_new, dict(w_up=d_w_up, w_down=d_w_down, conv_w=d_cw, conv_b=d_cb, norm_g=dg.reshape(D))


def _local_step(x, target, p, late_weights, late_grads_ready, late_grads_continue):
    g = {}
    (h1,) = _rms_fwd(x, [p["mix_norm_g"][0]], "a_norm")
    w_qkv = p["a_w_in"][:, :QKV_W]
    w_f = jnp.pad(p["a_w_in"][:, QKV_W:], ((0, 0), (0, LANES - A_HEADS)))
    b_f = jnp.pad(p["a_b_f"].reshape(1, A_HEADS), ((0, 0), (0, LANES - A_HEADS)))
    qkv = _mm(h1, w_qkv, mode="nn", tm=1024, tn=512, tk=1024, out_dtype=BF16, name="a_qkv")
    pf = _mm(h1, w_f, mode="nn", tm=1024, tn=LANES, tk=1024, name="a_gate")
    aug_q, aug_k = _fgate_fwd(pf, b_f, "a_gate_scan")
    oa2, lse_a = _fox_pair_fwd(qkv, aug_q, aug_k, "a_attn")
    x1 = _mm(oa2, p["a_w_out"], mode="nn", tm=1024, tn=512, tk=1024, res=x, name="a_out")
    p = {**p, **late_weights(x1)}
    x2, ffn0 = _ffn_fwd(x1, p["ffn_norm_g"][0], p["ffn_w_up"][0], p["conv_w"][0], p["conv_b"][0], p["ffn_w_down"][0], "f0")
    hk, h3 = _rms_fwd(x2, [p["kv_norm_g"], p["mix_norm_g"][1]], "kv_b_norm")
    kvb = _mm(hk, p["w_kv"], mode="nn", tm=1024, tn=512, tk=1024, out_dtype=BF16, name="kv_proj")
    qb = _mm(h3, p["b_w_q"], mode="nn", tm=1024, tn=512, tk=1024, out_dtype=BF16, name="b_q")
    dil_in = []
    for gi, (_, d) in enumerate(B_GROUPS):
        if d == 1:
            dil_in.append((qb, kvb, kvb, gi * NPG, gi * NPG, (3 + gi) * NPG))
        else:
            qg = _residue_rows(qb[:, gi * GROUP_W:(gi + 1) * GROUP_W], d)
            kvg = _residue_rows(kvb.reshape(S, 2, 3, GROUP_W)[:, :, gi, :].reshape(S, 2 * GROUP_W), d)
            dil_in.append((qg, kvg, kvg, 0, 0, NPG))
    o_g, lse_g = [], []
    for gi, (_, d) in enumerate(B_GROUPS):
        qg, kg, vg, qoff, koff, voff = dil_in[gi]
        og, lg = _dil_pair_fwd(gi, qg, kg, vg, qoff, koff, voff, f"b_attn{gi}")
        o_g.append(_residue_rows(og, d, inverse=True))
        lse_g.append(_residue_vecs(lg, d, inverse=True))
    ob, ob2, lse_b = _dil_pair_merge(o_g, lse_g, "b_merge")
    x3 = _mm(ob2, p["b_w_out"], mode="nn", tm=1024, tn=512, tk=B_OUT_W, res=x2, name="b_out")
    x4, ffn1 = _ffn_fwd(x3, p["ffn_norm_g"][1], p["ffn_w_up"][1], p["conv_w"][1], p["conv_b"][1], p["ffn_w_down"][1], "f1")
    loss, dx, dxb, dg_final = _loss_head(x4, p["final_norm_g"], target, "loss_head")
    g["final_norm_g"] = dg_final.reshape(D)

    dx, dxb, gf1 = _ffn_bwd(x3, p["ffn_norm_g"][1], p["ffn_w_up"][1], p["conv_w"][1], p["conv_b"][1], p["ffn_w_down"][1],
                            ffn1, dx, dxb, "f1")
    g["b_w_out"] = _mm(ob2, dxb, mode="tn", tm=B_OUT_W, tn=512, tk=1024, name="b_dwout")
    dob = _mm(dxb, p["b_w_out"], mode="nt", tm=1024, tn=B_OUT_W, tk=1024, name="b_do")
    delta_b = _pair_rowdot(dob, ob, "b_delta")
    dob16 = dob.astype(BF16)
    stats_b = jnp.concatenate([lse_b, delta_b], axis=0)
    dq_g, dk_g, dv_g = [], [], []
    for gi, (_, d) in enumerate(B_GROUPS):
        qg, kg, vg, qoff, koff, voff = dil_in[gi]
        dog, stats_d = _residue_rows(dob16, d), _residue_vecs(stats_b, d)
        dqd = _dil_pair_dq(gi, qg, kg, vg, qoff, koff, voff, dog, stats_d, f"b_dq{gi}")
        dkd, dvd = _dil_pair_dkv(gi, qg, kg, vg, qoff, koff, voff, dog, _col_to_row(stats_d), f"b_dkv{gi}")
        dq_g.append(_residue_rows(dqd, d, inverse=True))
        dk_g.append(_residue_rows(dkd, d, inverse=True))
        dv_g.append(_residue_rows(dvd, d, inverse=True))
    dqb = jnp.concatenate(dq_g, axis=1)
    dkvb = jnp.concatenate(dk_g + dv_g, axis=1)
    g["b_w_q"] = _mm(h3, dqb, mode="tn", tm=1024, tn=512, tk=1024, name="b_dwq")
    dh3 = _mm(dqb, p["b_w_q"], mode="nt", tm=1024, tn=512, tk=B_Q_W, name="b_dh")
    g["w_kv"] = _mm(hk, dkvb, mode="tn", tm=1024, tn=512, tk=1024, name="kv_dw")
    dhk = _mm(dkvb, p["w_kv"], mode="nt", tm=1024, tn=512, tk=3072, name="kv_dh")
    dx, dxb, (dg_mix1, dg_kv) = _rms_bwd(x2, dx, [(p["mix_norm_g"][1], dh3), (p["kv_norm_g"], dhk)], "b_dnorm")
    g["kv_norm_g"] = dg_kv.reshape(D)
    dx, dxb, gf0 = _ffn_bwd(x1, p["ffn_norm_g"][0], p["ffn_w_up"][0], p["conv_w"][0], p["conv_b"][0], p["ffn_w_down"][0],
                            ffn0, dx, dxb, "f0")
    g["ffn_w_up"] = [gf0["w_up"], gf1["w_up"]]
    g["ffn_w_down"] = [gf0["w_down"], gf1["w_down"]]
    g["ffn_conv_w"] = jnp.stack([gf0["conv_w"], gf1["conv_w"]])
    token = late_grads_ready(g)
    a_w_out_t = p["a_w_out"] + token[0, 0].astype(BF16)
    g["a_w_out"] = _mm(oa2, dxb, mode="tn", tm=1024, tn=512, tk=1024, name="a_dwout")
    doa = _mm(dxb, a_w_out_t, mode="nt", tm=1024, tn=512, tk=1024, name="a_do")
    delta_a = _pair_rowdot(doa, oa2, "a_delta")
    token = late_grads_continue(delta_a)
    delta_row = _col_to_row(delta_a) + token[0, 0]
    dqa, dka, dva, dck, dcq = _fox_pair_bwd(qkv, doa, _col_to_row(lse_a), delta_row, aug_q, aug_k, "a_dattn")
    dqkv = jnp.concatenate([dqa, dka, dva], axis=1)
    pad_heads = lambda t: jnp.pad(t.reshape(A_HEADS, S).T, ((0, 0), (0, LANES - A_HEADS)))
    dpf, db_f = _fgate_bwd(pf, b_f, pad_heads(dck), pad_heads(dcq), "a_dgate_scan")
    g["a_b_f"] = db_f[:, :A_HEADS]
    d_w_qkv = _mm(h1, dqkv, mode="tn", tm=1024, tn=512, tk=1024, name="a_dwqkv")
    d_w_f = _mm(h1, dpf, mode="tn", tm=1024, tn=LANES, tk=1024, name="a_dwgate")
    g["a_w_in"] = jnp.concatenate([d_w_qkv, d_w_f[:, :A_HEADS]], axis=1)
    dh1 = _mm(dqkv, w_qkv, mode="nt", tm=1024, tn=512, tk=3072, name="a_dh")
    dh1 = _mm(dpf, w_f, mode="nt", tm=1024, tn=512, tk=LANES, res=dh1, name="a_dh_gate")
    dx, _, (dg_mix0,) = _rms_bwd(x, dx, [(p["mix_norm_g"][0], dh1)], "a_dnorm")

    g["mix_norm_g"] = jnp.stack([dg_mix0.reshape(D), dg_mix1.reshape(D)])
    g["ffn_norm_g"] = jnp.stack([gf0["norm_g"], gf1["norm_g"]])
    g["ffn_conv_b"] = jnp.stack([gf0["conv_b"], gf1["conv_b"]])
    return loss[0, 0], dx, g


_SHARD_SHAPES = {"a_w_in": (1, 1024, 772), "a_w_out": (1, 256, 1024), "b_w_q": (1, 1024, 384), "b_w_out": (1, 512, 256),
                 "w_kv": (1024, 768), "ffn_w_up": (2, 1024, 1408), "ffn_w_down": (2, 704, 1024), "ffn_conv_w": (2, 3, 1408)}
_SMALL = (("kv_norm_g", (1024,)), ("mix_norm_g", (2, 1024)), ("ffn_norm_g", (2, 1024)), ("final_norm_g", (1024,)),
          ("a_b_f", (1, 16)), ("ffn_conv_b", (2, 5632)))


def _unslabs(rows, L, R, C, rpad):
    nc = -(-C // FLAT_W)
    return rows.reshape(L, nc, rpad, FLAT_W).transpose(0, 2, 1, 3).reshape(L, rpad, nc * FLAT_W)[:, :R, :C]


_SEG_RT = {"ffn_w_down": 704, "a_w_in": 1024, "a_w_out": 256, "b_w_q": 1024, "b_w_out": 512, "w_kv": 1024,
           "ffn_w_up": 1024, "ffn_conv_w": 16}
_ROW_SHARDED = ("a_w_out", "ffn_w_down")


_LAYOUTS = {"early": ("a_w_in", "a_w_out"), "late": ("ffn_w_down", "b_w_q", "b_w_out", "w_kv", "ffn_w_up", "ffn_conv_w"),
            "grad_early": ("a_w_in", "a_w_out"),
            "grad_late": ("ffn_w_up", "ffn_w_down", "b_w_q", "w_kv", "b_w_out", "ffn_conv_w")}
_GRAD_ROWS = {"grad_early": 10240, "grad_late": 45056}


def _layout_rows(layout):
    used = sum(_seg_rows(*s) for s in _SEGS if s[0] in _LAYOUTS[layout])
    rows = _GRAD_ROWS.get(layout, used)
    assert rows >= used
    return rows


def _grad_layout(name):
    return "grad_early" if name in _LAYOUTS["grad_early"] else "grad_late"


def _seg(name, layout=None):
    layout = layout or _grad_layout(name)
    off = 0
    for s in sorted((s for s in _SEGS if s[0] in _LAYOUTS[layout]), key=lambda s: _LAYOUTS[layout].index(s[0])):
        _, L, R, C, rpad = s
        if layout in _GRAD_ROWS:
            per_layer = -(-C // FLAT_W) * rpad
            off = -(-off // per_layer) * per_layer
        if s[0] == name:
            rt = _SEG_RT[name]
            assert off % rt == 0 and rpad % rt == 0
            half = _layout_rows(layout) // 2
            assert off + _seg_rows(*s) <= 2 * half
            assert layout not in _GRAD_ROWS or half % rt == 0 or off + _seg_rows(*s) <= half
            return dict(L=L, R=R, C=C, rpad=rpad, nc=-(-C // FLAT_W), rt=rt, off=off, ni=rpad // rt, half=half)
        off += _seg_rows(*s)
    raise KeyError(name)


def _flat_block(sg, term=0):
    base = (sg["off"] + term * sg["L"] * sg["nc"] * sg["rpad"]) // sg["rt"]
    return lambda l, j, i: base + (l * sg["nc"] + j) * sg["ni"] + i


def _native3(t, name):
    sg = _seg(name)
    t = t.reshape(sg["L"], sg["R"], sg["C"])
    return jnp.pad(t, ((0, 0), (0, sg["rpad"] - sg["R"]), (0, 0))) if sg["rpad"] != sg["R"] else t


def _slab_pack(flat, t, name, layout, term=None):
    sg = _seg(name, layout)
    rt = sg["rt"]
    rb = _flat_block(sg, term or 0)

    def body(*refs):
        t_ref, o_ref = refs[-2], refs[-1]
        val = t_ref[...]
        o_ref[...] = val.astype(BF16) if term is None else _split3(val)[term]

    in_specs = [pl.BlockSpec((None, rt, FLAT_W), lambda l, j, i: (l, i, j))]
    args = [t]
    if flat is not None:
        in_specs, args = [ANY] + in_specs, [flat] + args
    return pl.pallas_call(
        body, out_shape=jax.ShapeDtypeStruct((_layout_rows(layout), FLAT_W), BF16), grid=(sg["L"], sg["nc"], sg["ni"]),
        in_specs=in_specs, out_specs=pl.BlockSpec((rt, FLAT_W), lambda l, j, i: (rb(l, j, i), 0)),
        input_output_aliases={0: 0} if flat is not None else {},
        compiler_params=_cparams(("parallel", "parallel", "parallel")), name=f"pack_{name}_{term or 0}",
    )(*args)


def _full_spec(sg, name):
    rt, nc, ni = sg["rt"], sg["nc"], sg["ni"]
    if name in _ROW_SHARDED:
        return (sg["L"], N_CHIPS * sg["R"], sg["C"]), pl.BlockSpec((None, rt, FLAT_W), lambda k, l, j, i: (l, k * ni + i, j))
    return ((sg["L"], sg["rpad"], N_CHIPS * nc * FLAT_W),
            pl.BlockSpec((None, rt, FLAT_W), lambda k, l, j, i: (l, i, k * nc + j)))


def _slab_unpack(gathered, slots, name, layout, own=None):
    sg = _seg(name, layout)
    rb = _flat_block(sg)
    shape, _ = _full_spec(sg, name)
    rt, nc, ni = sg["rt"], sg["nc"], sg["ni"]
    width = nc * FLAT_W
    last = gathered.shape[0] - 1

    def body(*refs):
        s_ref, o_ref = refs[0], refs[-1]
        is_own = s_ref[pl.program_id(0)] == OWN_SLOT
        for j in range(nc):
            val = refs[1 + j][...]
            if own is not None:
                val = jnp.where(is_own, refs[1 + nc + j][...], val)
            o_ref[:, j * FLAT_W:(j + 1) * FLAT_W] = val

    if name in _ROW_SHARDED:
        o_spec = pl.BlockSpec((None, rt, width), lambda k, l, i, s: (l, k * ni + i, 0))
    else:
        o_spec = pl.BlockSpec((None, rt, width), lambda k, l, i, s: (l, i, k))
    in_specs = [pl.BlockSpec((None, rt, FLAT_W), lambda k, l, i, s, j=j: (jnp.minimum(s[k], last), rb(l, j, i), 0))
                for j in range(nc)]
    args = [gathered] * nc
    if own is not None:
        in_specs += [pl.BlockSpec((rt, FLAT_W), lambda k, l, i, s, j=j: (rb(l, j, i), 0)) for j in range(nc)]
        args += [own] * nc
    return pl.pallas_call(
        body, out_shape=jax.ShapeDtypeStruct(shape, BF16),
        grid_spec=pltpu.PrefetchScalarGridSpec(num_scalar_prefetch=1, grid=(N_CHIPS, sg["L"], ni), in_specs=in_specs,
                                               out_specs=o_spec),
        compiler_params=_cparams(("parallel",) * 3), name=f"unpack_{name}",
    )(slots, *args)


def _slab_pack_grad(flat4, g, name, layer=None):
    sg = _seg(name)
    rows = _layout_rows(_grad_layout(name))
    shape, _ = _full_spec(sg, name)
    n_layers = sg["L"] if layer is None else 1
    assert g.shape == (n_layers,) + shape[1:], (name, g.shape, shape)
    rt, nc = sg["rt"], sg["nc"]
    assert sg["ni"] == 1 and sg["off"] % (nc * rt) == 0
    base = sg["off"] // (nc * rt) + (layer or 0)

    def body(*refs):
        g_ref, o_ref = refs[-2], refs[-1]
        for j in range(nc):
            o_ref[j * rt:(j + 1) * rt, :] = g_ref[:, j * FLAT_W:(j + 1) * FLAT_W]

    if name in _ROW_SHARDED:
        spec = pl.BlockSpec((None, rt, nc * FLAT_W), lambda k, l: (l, k, 0))
    else:
        spec = pl.BlockSpec((None, rt, nc * FLAT_W), lambda k, l: (l, 0, k))
    in_specs, args = [spec], [g]
    if flat4 is not None:
        in_specs, args = [pl.BlockSpec(memory_space=pl.ANY)] + in_specs, [flat4] + args
    return pl.pallas_call(
        body, out_shape=jax.ShapeDtypeStruct((N_CHIPS, rows, FLAT_W), F32), grid=(N_CHIPS, n_layers),
        in_specs=in_specs, out_specs=pl.BlockSpec((None, nc * rt, FLAT_W), lambda k, l: (k, base + l, 0)),
        input_output_aliases={0: 0} if flat4 is not None else {},
        compiler_params=_cparams(("parallel",) * 2), name=f"packgrad_{name}_{layer or 0}",
    )(*args)


def _adamw_shard(w, m, v, g_mine, g_other, c_arr, name):
    sg = _seg(name)
    rt = sg["rt"]
    rb = _flat_block(sg)
    per_half = sg["half"] // rt

    def half_of(l, j, i):
        return (rb(l, j, i) * rt) // sg["half"]

    def body(c_ref, w_ref, m_ref, v_ref, gm_ref, go_ref, g_ref, d_ref, mo_ref, vo_ref):
        is_mine = half_of(pl.program_id(0), pl.program_id(1), pl.program_id(2)) == c_ref[0]
        gv = jnp.where(is_mine, gm_ref[...], go_ref[...])
        g_ref[...] = gv
        d_ref[...], mo_ref[...], vo_ref[...] = _adam_update(w_ref[...], gv, m_ref[...], v_ref[...])

    nat = pl.BlockSpec((None, rt, FLAT_W), lambda l, j, i, c: (l, i, j))
    half = pl.BlockSpec((rt, FLAT_W), lambda l, j, i, c: (rb(l, j, i) - half_of(l, j, i) * per_half, 0))
    sds = jax.ShapeDtypeStruct(w.shape, F32)
    return pl.pallas_call(
        body, out_shape=[sds] * 4,
        grid_spec=pltpu.PrefetchScalarGridSpec(num_scalar_prefetch=1, grid=(sg["L"], sg["nc"], sg["ni"]),
                                               in_specs=[nat, nat, nat, half, half], out_specs=[nat] * 4),
        compiler_params=_cparams(("parallel", "parallel", "parallel")), name=f"adamw_{name}",
    )(c_arr, w, m, v, g_mine, g_other)


def _pack_small(vals, loss=None):
    parts = [vals[name].astype(F32).reshape(-1) for name, _ in _SMALL]
    if loss is not None:
        parts.append(loss.reshape(1))
    flat = jnp.concatenate(parts)
    return jnp.pad(flat, (0, SMALL_ROWS * 1024 - flat.shape[0])).reshape(SMALL_ROWS, 1024)


def _unpack_small(flat):
    flat = flat.reshape(-1)
    out = {}
    o = 0
    for name, shape in _SMALL:
        n = int(np.prod(shape))
        out[name] = flat[o:o + n].reshape(shape)
        o += n
    return out, flat[o]


_BIG = ("a_w_in", "a_w_out", "b_w_q", "b_w_out", "w_kv", "ffn_w_up", "ffn_w_down", "ffn_conv_w")
A_IN_PAD = 896


def _pack_weights(w, layout):
    flat = None
    for name in _LAYOUTS[layout]:
        t = _native3(w[name], name)
        for term in ((0, 1, 2) if name == "ffn_conv_w" else (None,)):
            flat = _slab_pack(flat, t, name, layout, term)
    return flat


def _early_weights(gathered, slots):
    a_in = _slab_unpack(gathered, slots, "a_w_in", "early")
    a_in = a_in.reshape(D, N_CHIPS, A_IN_PAD)[:, :, :772].reshape(D, N_CHIPS * 772)
    return dict(a_w_in=a_in, a_w_out=_slab_unpack(gathered, slots, "a_w_out", "early")[0])


def _late_weights(landed, slots, own):
    full = {name: _slab_unpack(landed, slots, name, "late", own) for name in _LAYOUTS["late"] if name != "ffn_conv_w"}
    sg = _seg("ffn_conv_w", "late")
    n1 = sg["nc"] * sg["rpad"]
    conv = slice(sg["off"], sg["off"] + CONV_TERMS * n1)
    conv_rows = jnp.concatenate([landed[:, conv], own[None, conv]], axis=0)
    per_chip = []
    for k in range(N_CHIPS):
        rows = lax.dynamic_index_in_dim(conv_rows, slots[k], axis=0, keepdims=False)
        terms = [_unslabs(rows[i * n1:(i + 1) * n1], 1, sg["R"], sg["C"], sg["rpad"]).astype(F32) for i in range(CONV_TERMS)]
        per_chip.append((terms[0] + terms[1]) + terms[2])
    cw = jnp.concatenate(per_chip, axis=2).reshape(2, 3, 2, FF).transpose(0, 2, 1, 3)
    return dict(b_w_q=full["b_w_q"][0], b_w_out=full["b_w_out"][0], w_kv=full["w_kv"][0], ffn_w_up=full["ffn_w_up"],
                ffn_w_down=full["ffn_w_down"], conv_w=cw)


def _shard_grads(g, layout):
    def full(name):
        if name == "a_w_in":
            a_in = jnp.pad(g[name].reshape(D, N_CHIPS, 772), ((0, 0), (0, 0), (0, A_IN_PAD - 772)))
            return a_in.reshape(1, D, N_CHIPS * A_IN_PAD)
        if name == "ffn_conv_w":
            sgc = _seg(name)
            return jnp.pad(g[name].reshape(1, sgc["R"], 2 * FF), ((0, 0), (0, sgc["rpad"] - sgc["R"]), (0, 0)))
        return g[name] if g[name].ndim == 3 else g[name][None]

    flat4 = None
    for name in _LAYOUTS[layout]:
        if isinstance(g[name], (list, tuple)):
            for layer, t in enumerate(g[name]):
                flat4 = _slab_pack_grad(flat4, t[None], name, layer)
        else:
            flat4 = _slab_pack_grad(flat4, full(name), name)
    return flat4


_WEIGHTS = ["a_w_in", "a_b_f", "a_w_out", "b_w_q", "b_w_out", "kv_norm_g", "w_kv", "mix_norm_g", "ffn_norm_g", "ffn_w_up",
            "ffn_conv_w", "ffn_conv_b", "ffn_w_down", "final_norm_g"]


def kernel(x, a_w_in, a_b_f, a_w_out, b_w_q, b_w_out, kv_norm_g, w_kv, mix_norm_g, ffn_norm_g, ffn_w_up, ffn_conv_w, ffn_conv_b, ffn_w_down, final_norm_g, loss_target, m_a_w_in, m_a_b_f, m_a_w_out, m_b_w_q, m_b_w_out, m_kv_norm_g, m_w_kv, m_mix_norm_g, m_ffn_norm_g, m_ffn_w_up, m_ffn_conv_w, m_ffn_conv_b, m_ffn_w_down, m_final_norm_g, v_a_w_in, v_a_b_f, v_a_w_out, v_b_w_q, v_b_w_out, v_kv_norm_g, v_w_kv, v_mix_norm_g, v_ffn_norm_g, v_ffn_w_up, v_ffn_conv_w, v_ffn_conv_b, v_ffn_w_down, v_final_norm_g):
    w = dict(a_w_in=a_w_in, a_b_f=a_b_f, a_w_out=a_w_out, b_w_q=b_w_q, b_w_out=b_w_out, kv_norm_g=kv_norm_g, w_kv=w_kv,
             mix_norm_g=mix_norm_g, ffn_norm_g=ffn_norm_g, ffn_w_up=ffn_w_up, ffn_conv_w=ffn_conv_w, ffn_conv_b=ffn_conv_b,
             ffn_w_down=ffn_w_down, final_norm_g=final_norm_g)
    m = dict(a_w_in=m_a_w_in, a_b_f=m_a_b_f, a_w_out=m_a_w_out, b_w_q=m_b_w_q, b_w_out=m_b_w_out, kv_norm_g=m_kv_norm_g,
             w_kv=m_w_kv, mix_norm_g=m_mix_norm_g, ffn_norm_g=m_ffn_norm_g, ffn_w_up=m_ffn_w_up, ffn_conv_w=m_ffn_conv_w,
             ffn_conv_b=m_ffn_conv_b, ffn_w_down=m_ffn_w_down, final_norm_g=m_final_norm_g)
    v = dict(a_w_in=v_a_w_in, a_b_f=v_a_b_f, a_w_out=v_a_w_out, b_w_q=v_b_w_q, b_w_out=v_b_w_out, kv_norm_g=v_kv_norm_g,
             w_kv=v_w_kv, mix_norm_g=v_mix_norm_g, ffn_norm_g=v_ffn_norm_g, ffn_w_up=v_ffn_w_up, ffn_conv_w=v_ffn_conv_w,
             ffn_conv_b=v_ffn_conv_b, ffn_w_down=v_ffn_w_down, final_norm_g=v_final_norm_g)

    c_arr = lax.axis_index("c").astype(jnp.int32).reshape(1)
    k_arr = (2 * lax.axis_index("x") + lax.axis_index("y")).astype(jnp.int32).reshape(1)
    xi, yi = lax.axis_index("x"), lax.axis_index("y")
    late_slots = jnp.stack([jnp.where(k == k_arr[0], OWN_SLOT, 2 * ((k & 1) ^ yi) + ((k >> 1) ^ xi) - 1)
                            for k in range(N_CHIPS)]).astype(jnp.int32)
    w_late = _pack_weights(w, "late")
    land = lax.empty((OWN_SLOT,) + w_late.shape, BF16)
    send_sems, recv_sems, w_thru, land_thru, token = _late_gather_start(w_late, land)
    w_early = _pack_weights(w, "early")
    early = _allgather_shards(w_early, _place_own(w_early, k_arr, "early_place_own"))
    p = _early_weights(early, jnp.arange(N_CHIPS, dtype=jnp.int32))
    cb = ffn_conv_b.reshape(2, 2, 1, FF)
    p.update(a_b_f=a_b_f, kv_norm_g=kv_norm_g, mix_norm_g=mix_norm_g + token[0, 0], ffn_norm_g=ffn_norm_g,
             final_norm_g=final_norm_g, conv_b=cb)

    def late_weights(after):
        own, landed = _late_gather_wait(send_sems, recv_sems, w_thru, land_thru, after)
        return _late_weights(landed, late_slots, own)

    started = {}

    def late_grads_ready(g_so_far):
        gflat = _shard_grads(g_so_far, "grad_late")
        land = lax.empty((N_CHIPS, gflat.shape[1] // 2, FLAT_W), F32)
        *handles, token = _sibling_swap_start(gflat, land)
        started["swap"] = handles
        return token

    def late_grads_continue(after):
        gflat, other = _sibling_swap_wait(*started["swap"], after)
        pair = _pair_sum(gflat, other, c_arr, "late")
        land = lax.empty((3,) + pair.shape[1:], BF16)
        *handles, token = _chip_exchange_start(pair, land, "late")
        started["handles"] = handles
        return token

    loss_part, grad_x, g = _local_step(x[0], loss_target[0], p, late_weights, late_grads_ready, late_grads_continue)

    gflat = _shard_grads(g, "grad_early")
    pair_e = _pair_sum(gflat, _sibling_swap_half(gflat, "early"), c_arr, "early")
    *early_handles, token = _chip_exchange_start(pair_e, lax.empty((3,) + pair_e.shape[1:], BF16), "early")

    big = [{}, {}, {}, {}]

    def adamw_group(layout, g_mine):
        g_other = _sibling_send(g_mine, layout)
        for name in _LAYOUTS[layout]:
            sg = _seg(name)
            res = _adamw_shard(_native3(w[name], name), _native3(m[name], name), _native3(v[name], name), g_mine,
                               g_other, c_arr, name)
            for store, t in zip(big, res):
                store[name] = t[:, :sg["R"], :].reshape(_SHARD_SHAPES[name])
        return res[1]

    pair, landed = _chip_exchange_wait(*started["handles"], token, "late")
    last = adamw_group("grad_late", _chip_sum(pair, landed, k_arr, "late"))
    small, loss = _unpack_small(_allreduce_small(_pack_small(g, loss_part)))
    dws, mns, vns = _adamw(_pack_small(w), _pack_small(small), _pack_small(m), _pack_small(v), "adamw_small")
    pair_e, landed_e = _chip_exchange_wait(*early_handles, last, "early")
    adamw_group("grad_early", _chip_sum(pair_e, landed_e, k_arr, "early"))
    sml = [small] + [_unpack_small(t)[0] for t in (dws, mns, vns)]
    outs = [loss, grad_x[None]]
    for b, s in zip(big, sml):
        outs += [b[n] if n in b else s[n] for n in _WEIGHTS]
    return tuple(outs)
```

```python
import numpy as np
import jax
import jax.numpy as jnp
from jax import lax
from jax.experimental import pallas as pl
from jax.experimental.pallas import tpu as pltpu

F32 = jnp.float32
BF16 = jnp.bfloat16
MESH = pl.DeviceIdType.MESH

S = 4096
D = 1024
A_HEADS = 16
HEAD_DIM = 64
QKV_W = 3 * A_HEADS * HEAD_DIM
B_GROUPS = ((128, 1), (512, 4), (2048, 16))
B_HPG = 8
B_Q_W = 3 * B_HPG * HEAD_DIM
B_OUT_W = B_HPG * HEAD_DIM
B_WIN = 128
FF = 2816
RMS_EPS = 1e-6
SCALE = HEAD_DIM ** -0.5
N_CHIPS = 4

ADAM_LR, ADAM_B1, ADAM_B2, ADAM_EPS, ADAM_WD, ADAM_STEP = 0.001, 0.9, 0.999, 1e-08, 0.01, 10

V7X_VMEM_LIMIT = 48 * 1024 * 1024
LANES = 128
NEG_INF = float("-inf")

FLAT_W = LANES
_SEGS = (("ffn_w_down", 2, 704, 1024, 704), ("a_w_in", 1, 1024, 772, 1024), ("a_w_out", 1, 256, 1024, 256),
         ("b_w_q", 1, 1024, 384, 1024), ("b_w_out", 1, 512, 256, 512), ("w_kv", 1, 1024, 768, 1024),
         ("ffn_w_up", 2, 1024, 1408, 1024), ("ffn_conv_w", 1, 6, 1408, 16))
CONV_TERMS = 3


def _seg_rows(name, L, R, C, rpad):
    return (CONV_TERMS if name == "ffn_conv_w" else 1) * L * (-(-C // FLAT_W)) * rpad


SMALL_ROWS = 24


def _cparams(sem=None, **kw):
    return pltpu.CompilerParams(dimension_semantics=sem, vmem_limit_bytes=V7X_VMEM_LIMIT, **kw)


_DN = {"nn": (((1,), (0,)), ((), ())), "nt": (((1,), (1,)), ((), ())), "tn": (((0,), (0,)), ((), ()))}


def _mm(a, b, *, mode, tm, tn, tk, name, out_dtype=F32, res=None, a_split=0, b_split=0, o_split=0):
    if mode == "tn":
        K = a.shape[0]
        M = a.shape[1]
    else:
        M = a.shape[-2]
        K = a.shape[-1] * (2 if a_split else 1)
    if mode == "nt":
        N = b.shape[0]
    else:
        N = b.shape[-1] * (2 if b_split else 1)
    assert M % tm == 0 and N % tn == 0 and K % tk == 0, (name, M, N, K, tm, tn, tk)
    nk = K // tk

    if mode == "tn":
        a_spec = pl.BlockSpec((tk, tm), lambda i, j, k: (k, i))
    elif a_split:
        a_spec = pl.BlockSpec((None, tm, tk), lambda i, j, k: (k // a_split, i, k % a_split))
    else:
        a_spec = pl.BlockSpec((tm, tk), lambda i, j, k: (i, k))
    if mode == "nt":
        b_spec = pl.BlockSpec((tn, tk), lambda i, j, k: (j, k))
    elif b_split:
        b_spec = pl.BlockSpec((None, tk, tn), lambda i, j, k: (j // b_split, k, j % b_split))
    else:
        b_spec = pl.BlockSpec((tk, tn), lambda i, j, k: (k, j))
    if o_split:
        o_spec = pl.BlockSpec((None, tm, tn), lambda i, j, k: (j // o_split, i, j % o_split))
        out_shape = jax.ShapeDtypeStruct((2, M, N // 2), out_dtype)
    else:
        o_spec = pl.BlockSpec((tm, tn), lambda i, j, k: (i, j))
        out_shape = jax.ShapeDtypeStruct((M, N), out_dtype)
    in_specs = [a_spec, b_spec]
    args = [a, b]
    if res is not None:
        in_specs.append(pl.BlockSpec((tm, tn), lambda i, j, k: (i, j)))
        args.append(res)

    def body(*refs):
        if res is not None:
            a_ref, b_ref, r_ref, o_ref = refs[:4]
        else:
            a_ref, b_ref, o_ref = refs[:3]
            r_ref = None
        p = lax.dot_general(a_ref[...].astype(BF16), b_ref[...].astype(BF16), _DN[mode], preferred_element_type=F32)

        def finish(r):
            if r_ref is not None:
                r = r + r_ref[...]
            o_ref[...] = r.astype(out_dtype)

        if nk == 1:
            finish(p)
        else:
            acc = refs[-1]
            k = pl.program_id(2)

            @pl.when(k == 0)
            def _():
                acc[...] = p

            @pl.when(k > 0)
            def _():
                acc[...] += p

            @pl.when(k == nk - 1)
            def _():
                finish(acc[...])

    return pl.pallas_call(
        body, out_shape=out_shape, grid=(M // tm, N // tn, nk), in_specs=in_specs, out_specs=o_spec,
        scratch_shapes=[pltpu.VMEM((tm, tn), F32)] if nk > 1 else [],
        compiler_params=_cparams(("parallel", "parallel", "arbitrary")), name=name,
    )(*args)


NORM_ROWS = 256


def _rms_fwd(x, gains, name):
    n = len(gains)

    def body(x_ref, *refs):
        xv = x_ref[...]
        y = xv * lax.rsqrt(jnp.mean(xv * xv, axis=-1, keepdims=True) + RMS_EPS)
        for g_ref, o_ref in zip(refs[:n], refs[n:]):
            o_ref[...] = (y * g_ref[...]).astype(BF16)

    row = pl.BlockSpec((NORM_ROWS, D), lambda i: (i, 0))
    return pl.pallas_call(
        body, out_shape=[jax.ShapeDtypeStruct((S, D), BF16)] * n, grid=(S // NORM_ROWS,),
        in_specs=[row] + [pl.BlockSpec((1, D), lambda i: (0, 0))] * n, out_specs=[row] * n,
        compiler_params=_cparams(("parallel",)), name=name,
    )(x, *[g.reshape(1, D) for g in gains])


def _rms_bwd(x, dres, pairs, name):
    n = len(pairs)

    def body(*refs):
        x_ref, dres_ref = refs[0], refs[1]
        g_refs = refs[2:2 + 2 * n:2]
        dh_refs = refs[3:3 + 2 * n:2]
        dx_ref, dxb_ref = refs[2 + 2 * n], refs[3 + 2 * n]
        dg_refs = refs[4 + 2 * n:]
        i = pl.program_id(0)
        xv = x_ref[...]
        r = lax.rsqrt(jnp.mean(xv * xv, axis=-1, keepdims=True) + RMS_EPS)
        y = xv * r
        dx = dres_ref[...]
        for g_ref, dh_ref, dg_ref in zip(g_refs, dh_refs, dg_refs):
            dh = dh_ref[...]
            dy = dh * g_ref[...]
            dx = dx + r * (dy - y * jnp.mean(dy * y, axis=-1, keepdims=True))
            part = jnp.sum(dh * y, axis=0, keepdims=True)

            @pl.when(i == 0)
            def _():
                dg_ref[...] = part

            @pl.when(i > 0)
            def _():
                dg_ref[...] += part

        dx_ref[...] = dx
        dxb_ref[...] = dx.astype(BF16)

    row = pl.BlockSpec((NORM_ROWS, D), lambda i: (i, 0))
    vec = pl.BlockSpec((1, D), lambda i: (0, 0))
    in_specs = [row, row]
    args = [x, dres]
    for g, dh in pairs:
        in_specs += [vec, row]
        args += [g.reshape(1, D), dh]
    outs = pl.pallas_call(
        body,
        out_shape=[jax.ShapeDtypeStruct((S, D), F32), jax.ShapeDtypeStruct((S, D), BF16)]
        + [jax.ShapeDtypeStruct((1, D), F32)] * n,
        grid=(S // NORM_ROWS,), in_specs=in_specs, out_specs=[row, row] + [vec] * n,
        compiler_params=_cparams(("arbitrary",)), name=name,
    )(*args)
    return outs[0], outs[1], list(outs[2:])


def _loss_head(x, g, target, name):
    def body(x_ref, g_ref, t_ref, loss_ref, dx_ref, dxb_ref, dg_ref):
        i = pl.program_id(0)
        xv = x_ref[...]
        gv = g_ref[...]
        r = lax.rsqrt(jnp.mean(xv * xv, axis=-1, keepdims=True) + RMS_EPS)
        y = xv * r
        err = y * gv - t_ref[...]
        lpart = jnp.broadcast_to(jnp.sum(err * err, keepdims=True) * (0.5 / D), (1, LANES))
        dh = err * (1.0 / D)
        dy = dh * gv
        dx = r * (dy - y * jnp.mean(dy * y, axis=-1, keepdims=True))
        part = jnp.sum(dh * y, axis=0, keepdims=True)

        @pl.when(i == 0)
        def _():
            dg_ref[...] = part
            loss_ref[...] = lpart

        @pl.when(i > 0)
        def _():
            dg_ref[...] += part
            loss_ref[...] += lpart

        dx_ref[...] = dx
        dxb_ref[...] = dx.astype(BF16)

    row = pl.BlockSpec((NORM_ROWS, D), lambda i: (i, 0))
    vec = pl.BlockSpec((1, D), lambda i: (0, 0))
    return pl.pallas_call(
        body,
        out_shape=[jax.ShapeDtypeStruct((1, LANES), F32), jax.ShapeDtypeStruct((S, D), F32),
                   jax.ShapeDtypeStruct((S, D), BF16), jax.ShapeDtypeStruct((1, D), F32)],
        grid=(S // NORM_ROWS,), in_specs=[row, vec, row],
        out_specs=[pl.BlockSpec((1, LANES), lambda i: (0, 0)), row, row, vec],
        compiler_params=_cparams(("arbitrary",)), name=name,
    )(x, g.reshape(1, D), target)


SCAN_ROWS = 256


def _split3(v):
    hi = v.astype(BF16)
    r1 = v - hi.astype(F32)
    mid = r1.astype(BF16)
    lo = (r1 - mid.astype(F32)).astype(BF16)
    return hi, mid, lo


def _tri_dot(tri, v):
    hi, mid, lo = _split3(v)
    dn = _DN["nn"]
    return (lax.dot_general(tri, hi, dn, preferred_element_type=F32)
            + lax.dot_general(tri, mid, dn, preferred_element_type=F32)
            + lax.dot_general(tri, lo, dn, preferred_element_type=F32))


def _log_sigmoid(z):
    return jnp.minimum(z, 0.0) - jnp.log(1.0 + jnp.exp(-jnp.abs(z)))


GATE_LANES = 6


def _gate_lane_tables():
    pq = np.zeros((3 * LANES, A_HEADS * HEAD_DIM), np.float32)
    pk = np.zeros((3 * LANES, A_HEADS * HEAD_DIM), np.float32)
    one_q = np.zeros((1, A_HEADS * HEAD_DIM), np.float32)
    one_k = np.zeros((1, A_HEADS * HEAD_DIM), np.float32)
    for h in range(A_HEADS):
        pos = (h // 2) * LANES + (HEAD_DIM if h % 2 == 0 else 0)
        for term in range(3):
            pq[term * LANES + h, pos + term] = 1.0
            pk[term * LANES + h, pos + 3 + term] = -1.0
        one_q[0, pos + 3:pos + GATE_LANES] = 1.0
        one_k[0, pos:pos + 3] = 1.0
    return jnp.asarray(pq, BF16), jnp.asarray(pk, BF16), jnp.asarray(one_q), jnp.asarray(one_k)


def _fgate_fwd(pf, bias, name):
    tri = jnp.tril(jnp.ones((SCAN_ROWS, SCAN_ROWS), F32)).astype(BF16)
    pq, pk, one_q, one_k = _gate_lane_tables()

    def body(pf_ref, b_ref, tri_ref, pq_ref, pk_ref, oq_ref, ok_ref, aq_ref, ak_ref, c_sc):
        carry = jnp.zeros((1, LANES), F32)
        for blk in range(S // SCAN_ROWS):
            rows = pl.ds(blk * SCAN_ROWS, SCAN_ROWS)
            lf = _log_sigmoid(pf_ref[rows, :] + b_ref[...])
            c_sc[...] = _tri_dot(tri_ref[...], lf) + carry
            carry = c_sc[pl.ds(SCAN_ROWS - 1, 1), :]
            terms = jnp.concatenate(_split3(c_sc[...]), axis=1)
            aq = lax.dot_general(terms, pq_ref[...], _DN["nn"], preferred_element_type=F32) + oq_ref[...]
            ak = lax.dot_general(terms, pk_ref[...], _DN["nn"], preferred_element_type=F32) + ok_ref[...]
            aq_ref[rows, :] = aq.astype(BF16)
            ak_ref[rows, :] = ak.astype(BF16)

    wide = jax.ShapeDtypeStruct((S, A_HEADS * HEAD_DIM), BF16)
    return pl.pallas_call(
        body, out_shape=[wide, wide], scratch_shapes=[pltpu.VMEM((SCAN_ROWS, LANES), F32)],
        compiler_params=_cparams(), name=name,
    )(pf, bias, tri, pq, pk, one_q, one_k)


def _fgate_bwd(pf, bias, dc_key, dc_query, name):
    triu = jnp.triu(jnp.ones((SCAN_ROWS, SCAN_ROWS), F32)).astype(BF16)

    def body(pf_ref, b_ref, dck_ref, dcq_ref, tri_ref, dpf_ref, db_ref, dlf_ref):
        carry = jnp.zeros((1, LANES), F32)
        db = jnp.zeros((1, LANES), F32)
        lane = lax.broadcasted_iota(jnp.int32, (SCAN_ROWS, LANES), 1)
        for blk in reversed(range(S // SCAN_ROWS)):
            rows = pl.ds(blk * SCAN_ROWS, SCAN_ROWS)
            dc = dck_ref[rows, :] + dcq_ref[rows, :]
            dlf_ref[rows, :] = _tri_dot(tri_ref[...], dc) + carry
            carry = dlf_ref[pl.ds(blk * SCAN_ROWS, 1), :]
            z = pf_ref[rows, :] + b_ref[...]
            e = jnp.exp(-jnp.abs(z))
            sig_neg = jnp.where(z >= 0.0, e, 1.0) / (1.0 + e)
            dz = jnp.where(lane < A_HEADS, dlf_ref[rows, :] * sig_neg, 0.0)
            dpf_ref[rows, :] = dz.astype(BF16)
            db = db + jnp.sum(dz, axis=0, keepdims=True)
        db_ref[...] = db

    return pl.pallas_call(
        body, out_shape=[jax.ShapeDtypeStruct((S, LANES), BF16), jax.ShapeDtypeStruct((1, LANES), F32)],
        scratch_shapes=[pltpu.VMEM((S, LANES), F32)],
        compiler_params=_cparams(), name=name,
    )(pf, bias, dc_key, dc_query, triu)


FOX_T = 512


def _first_head(shape):
    return lax.broadcasted_iota(jnp.int32, shape, len(shape) - 1) < HEAD_DIM


def _each_head(x, lo):
    zero = jnp.zeros_like(x)
    return jnp.where(lo, x, zero), jnp.where(lo, zero, x)


def _fox_pair_fwd(qkv, aug_q, aug_k, name):
    T = FOX_T
    nq = S // T
    NP = A_HEADS // 2

    def body(q_ref, k_ref, v_ref, aq_ref, ak_ref, o_ref, lse_ref, m_sc, l_sc, acc_sc):
        i = pl.program_id(1)
        j = pl.program_id(2)
        lo = _first_head((T, LANES))

        @pl.when(j == 0)
        def _():
            m_sc[...] = jnp.full((2, T, LANES), NEG_INF, F32)
            l_sc[...] = jnp.zeros((2, T, LANES), F32)
            acc_sc[...] = jnp.zeros((T, LANES), F32)

        def step(diagonal):
            qs = q_ref[...] * jnp.asarray(SCALE, BF16)
            aq, ak, kv = aq_ref[...], ak_ref[...], k_ref[...]
            q2 = (jnp.where(lo, qs, aq), jnp.where(lo, aq, qs))
            k2 = (jnp.where(lo, kv, ak), jnp.where(lo, ak, kv))
            if diagonal:
                causal = lax.broadcasted_iota(jnp.int32, (T, T), 0) >= lax.broadcasted_iota(jnp.int32, (T, T), 1)
            pv, alphas = None, []
            for h, vh in enumerate(_each_head(v_ref[...], lo)):
                s = lax.dot_general(q2[h], k2[h], _DN["nt"], preferred_element_type=F32)
                if diagonal:
                    s = jnp.where(causal, s, NEG_INF)
                m_prev = m_sc[h]
                m_new = jnp.maximum(m_prev, jnp.max(s, axis=1, keepdims=True))
                alpha = jnp.exp(m_prev - m_new)
                p = jnp.exp(s - jnp.tile(m_new, (1, T // LANES)))
                l_sc[h] = alpha * l_sc[h] + jnp.sum(p, axis=1, keepdims=True)
                m_sc[h] = m_new
                d = lax.dot_general(p.astype(BF16), vh, _DN["nn"], preferred_element_type=F32)
                pv = d if pv is None else pv + d
                alphas.append(alpha)
            acc_sc[...] = jnp.where(lo, alphas[0], alphas[1]) * acc_sc[...] + pv

        @pl.when(j < i)
        def _():
            step(False)

        @pl.when(j == i)
        def _():
            step(True)
            o_ref[...] = (acc_sc[...] * jnp.where(lo, 1.0 / l_sc[0], 1.0 / l_sc[1])).astype(BF16)
            for h in range(2):
                lse_ref[h] = (m_sc[h] + jnp.log(l_sc[h]))[:, 0:1]

    qs_ = pl.BlockSpec((T, LANES), lambda p, i, j: (i, p))
    ks = pl.BlockSpec((T, LANES), lambda p, i, j: (jnp.minimum(i, j), NP + p))
    vs = pl.BlockSpec((T, LANES), lambda p, i, j: (jnp.minimum(i, j), 2 * NP + p))
    aks = pl.BlockSpec((T, LANES), lambda p, i, j: (jnp.minimum(i, j), p))
    col = pl.BlockSpec((2, T, 1), lambda p, i, j: (p, i, 0))
    return pl.pallas_call(
        body, out_shape=[jax.ShapeDtypeStruct((S, A_HEADS * HEAD_DIM), BF16), jax.ShapeDtypeStruct((A_HEADS, S, 1), F32)],
        grid=(NP, nq, nq), in_specs=[qs_, ks, vs, qs_, aks], out_specs=[qs_, col],
        scratch_shapes=[pltpu.VMEM((2, T, LANES), F32), pltpu.VMEM((2, T, LANES), F32), pltpu.VMEM((T, LANES), F32)],
        compiler_params=_cparams(("parallel", "parallel", "arbitrary")), name=name,
    )(qkv, qkv, qkv, aug_q, aug_k)


def _fox_pair_bwd(qkv, do, lse_row, delta_row, aug_q, aug_k, name):
    T = FOX_T
    nq = S // T
    NP = A_HEADS // 2

    def body(q_ref, k_ref, v_ref, do_ref, lse_ref, dl_ref, aq_ref, ak_ref, dq_ref, dk_ref, dv_ref, dc_ref, dcq_ref,
             dq_sc, dk_sc, dv_sc, dc_sc):
        j = pl.program_id(1)
        i = pl.program_id(2)
        lo = _first_head((T, LANES))

        @pl.when(jnp.logical_and(j == 0, i == 0))
        def _():
            dq_sc[...] = jnp.zeros((S, LANES), F32)
            dcq_ref[...] = jnp.zeros((2, nq, 1, T), F32)

        @pl.when(i == j)
        def _():
            dk_sc[...] = jnp.zeros((T, LANES), F32)
            dv_sc[...] = jnp.zeros((T, LANES), F32)
            dc_sc[...] = jnp.zeros((2, T, 1), F32)

        def step(diagonal):
            qv = q_ref[...]
            kv = k_ref[...]
            dov = do_ref[...].astype(BF16)
            qs = qv * jnp.asarray(SCALE, BF16)
            aq, ak = aq_ref[...], ak_ref[...]
            q2 = (jnp.where(lo, qs, aq), jnp.where(lo, aq, qs))
            k2 = (jnp.where(lo, kv, ak), jnp.where(lo, ak, kv))
            if diagonal:
                causal = lax.broadcasted_iota(jnp.int32, (T, T), 1) >= lax.broadcasted_iota(jnp.int32, (T, T), 0)
            dv = dk = dq = None
            for h, (kh, vh, qh, doh) in enumerate(zip(_each_head(kv, lo), _each_head(v_ref[...], lo),
                                                      _each_head(qv, lo), _each_head(dov, lo))):
                st = lax.dot_general(k2[h], q2[h], _DN["nt"], preferred_element_type=F32)
                if diagonal:
                    st = jnp.where(causal, st, NEG_INF)
                pt = jnp.exp(st - lse_ref[h])
                d = lax.dot_general(pt.astype(BF16), doh, _DN["nn"], preferred_element_type=F32)
                dv = d if dv is None else dv + d
                dpt = lax.dot_general(vh, dov, _DN["nt"], preferred_element_type=F32)
                dst = pt * (dpt - dl_ref[h])
                dc_sc[h] -= jnp.sum(dst, axis=1, keepdims=True)
                dcq_ref[h, i] += jnp.sum(dst, axis=0, keepdims=True)
                dsb = (dst * SCALE).astype(BF16)
                d = lax.dot_general(dsb, qh, _DN["nn"], preferred_element_type=F32)
                dk = d if dk is None else dk + d
                d = lax.dot_general(dsb, kh, _DN["tn"], preferred_element_type=F32)
                dq = d if dq is None else dq + d
            dv_sc[...] += dv
            dk_sc[...] += dk
            rows = pl.ds(pl.multiple_of(i * T, T), T)
            dq_sc[rows, :] += dq

        @pl.when(i > j)
        def _():
            step(False)

        @pl.when(i == j)
        def _():
            step(True)

        @pl.when(i == nq - 1)
        def _():
            dk_ref[...] = dk_sc[...].astype(BF16)
            dv_ref[...] = dv_sc[...].astype(BF16)
            dc_ref[...] = dc_sc[...]

        @pl.when(jnp.logical_and(j == nq - 1, i == nq - 1))
        def _():
            dq_ref[...] = dq_sc[...].astype(BF16)

    qs = pl.BlockSpec((T, LANES), lambda p, j, i: (jnp.maximum(i, j), p))
    qrow = pl.BlockSpec((2, 1, T), lambda p, j, i: (p, 0, jnp.maximum(i, j)))
    ks = pl.BlockSpec((T, LANES), lambda p, j, i: (j, NP + p))
    vs = pl.BlockSpec((T, LANES), lambda p, j, i: (j, 2 * NP + p))
    kout = pl.BlockSpec((T, LANES), lambda p, j, i: (j, p))
    kcol = pl.BlockSpec((2, T, 1), lambda p, j, i: (p, j, 0))
    dqs = pl.BlockSpec((S, LANES), lambda p, j, i: (0, p))
    dcqs = pl.BlockSpec((2, nq, 1, T), lambda p, j, i: (p, 0, 0, 0))
    wide = jax.ShapeDtypeStruct((S, A_HEADS * HEAD_DIM), BF16)
    return pl.pallas_call(
        body,
        out_shape=[wide, wide, wide, jax.ShapeDtypeStruct((A_HEADS, S, 1), F32),
                   jax.ShapeDtypeStruct((A_HEADS, nq, 1, T), F32)],
        grid=(NP, nq, nq), in_specs=[qs, ks, vs, qs, qrow, qrow, qs, kout], out_specs=[dqs, kout, kout, kcol, dcqs],
        scratch_shapes=[pltpu.VMEM((S, LANES), F32), pltpu.VMEM((T, LANES), F32), pltpu.VMEM((T, LANES), F32),
                        pltpu.VMEM((2, T, 1), F32)],
        compiler_params=_cparams(("parallel", "arbitrary", "arbitrary")), name=name,
    )(qkv, qkv, qkv, do, lse_row, delta_row, aug_q, aug_k)


def _pair_rowdot(a, b, name):
    n = a.shape[1] // HEAD_DIM
    T = 1024

    def body(a_ref, b_ref, o_ref):
        prod = a_ref[...].astype(F32) * b_ref[...].astype(F32)
        lo = _first_head(prod.shape)
        o_ref[0] = jnp.sum(jnp.where(lo, prod, 0.0), axis=1, keepdims=True)
        o_ref[1] = jnp.sum(jnp.where(lo, 0.0, prod), axis=1, keepdims=True)

    blk = pl.BlockSpec((T, LANES), lambda p, i: (i, p))
    return pl.pallas_call(
        body, out_shape=jax.ShapeDtypeStruct((n, S, 1), F32), grid=(n // 2, S // T), in_specs=[blk, blk],
        out_specs=pl.BlockSpec((2, T, 1), lambda p, i: (p, i, 0)),
        compiler_params=_cparams(("parallel", "parallel")), name=name,
    )(a, b)


W = B_WIN
N_HG = 3 * B_HPG
N_BLK = S // W


def _dil_tables():
    slopes = np.exp2((-8.0 * np.arange(1, N_HG + 1, dtype=np.float32) / N_HG).astype(np.float32)).astype(np.float32)
    dil = np.repeat(np.array([d for _, d in B_GROUPS], np.float32), B_HPG)
    coef = (slopes * dil).astype(np.float32)
    nbs = np.repeat(np.array([S // d // W for _, d in B_GROUPS], np.int32), B_HPG)
    return jnp.asarray(coef), jnp.asarray(nbs)


DIL_SUB = 8
DIL_ROWS = DIL_SUB * W
DIL_STEPS = S // DIL_ROWS


def _dil_bias(coef, transposed):
    row = lax.broadcasted_iota(jnp.int32, (W, 2 * W), 0)
    col = lax.broadcasted_iota(jnp.int32, (W, 2 * W), 1)
    dist = (col - row) if transposed else (row + W - col)
    valid = jnp.logical_and(dist >= 0, dist <= W)
    return jnp.where(valid, -coef * dist.astype(F32), NEG_INF), col


NPG = B_HPG // 2
GROUP_W = B_HPG * HEAD_DIM


def _dil_pair_specs(qoff, koff, voff):
    prev_blk = lambda n: jnp.maximum(n * DIL_SUB - 1, 0)
    next_blk = lambda n: jnp.minimum((n + 1) * DIL_SUB, N_BLK - 1)
    return dict(
        o=pl.BlockSpec((DIL_ROWS, LANES), lambda h, n: (n, h)),
        o_next=pl.BlockSpec((W, LANES), lambda h, n: (next_blk(n), h)),
        q=pl.BlockSpec((DIL_ROWS, LANES), lambda h, n: (n, qoff + h)),
        q_next=pl.BlockSpec((W, LANES), lambda h, n: (next_blk(n), qoff + h)),
        k=pl.BlockSpec((DIL_ROWS, LANES), lambda h, n: (n, koff + h)),
        k_prev=pl.BlockSpec((W, LANES), lambda h, n: (prev_blk(n), koff + h)),
        v=pl.BlockSpec((DIL_ROWS, LANES), lambda h, n: (n, voff + h)),
        v_prev=pl.BlockSpec((W, LANES), lambda h, n: (prev_blk(n), voff + h)),
        col=pl.BlockSpec((2, DIL_ROWS, 1), lambda h, n: (h, n, 0)),
        col2=pl.BlockSpec((2, DIL_ROWS, 1), lambda h, n: (NPG + h, n, 0)),
        row=pl.BlockSpec((2, 1, DIL_ROWS), lambda h, n: (h, 0, n)),
        row_next=pl.BlockSpec((2, 1, W), lambda h, n: (h, 0, next_blk(n))),
        row2=pl.BlockSpec((2, 1, DIL_ROWS), lambda h, n: (NPG + h, 0, n)),
        row2_next=pl.BlockSpec((2, 1, W), lambda h, n: (NPG + h, 0, next_blk(n))),
        smem=pl.BlockSpec(memory_space=pltpu.SMEM))


def _dil_pair_fwd(g, q, k, v, qoff, koff, voff, name):
    coef_t, nbs_t = _dil_tables()

    def body(coef_ref, nbs_ref, q_ref, kh_ref, k_ref, vh_ref, v_ref, o_ref, lse_ref, kf, vf):
        hp = pl.program_id(0)
        n = pl.program_id(1)
        nbs = nbs_ref[B_HPG * g + 2 * hp]
        kf[0:W, :] = kh_ref[...]
        kf[W:, :] = k_ref[...]
        vf[0:W, :] = vh_ref[...]
        vf[W:, :] = v_ref[...]
        biases = [_dil_bias(coef_ref[B_HPG * g + 2 * hp + h], False) for h in range(2)]
        col = biases[0][1]
        lo = _first_head((W, LANES))
        lo2 = _first_head((2 * W, LANES))
        for b in range(DIL_SUB):
            first = lax.rem(n * DIL_SUB + b, nbs) == 0
            rows = slice(b * W, (b + 1) * W)
            both = slice(b * W, (b + 2) * W)
            qv = q_ref[rows, :]
            acc, inv = None, []
            for h, (kh, vh) in enumerate(zip(_each_head(kf[both, :], lo2), _each_head(vf[both, :], lo2))):
                s = lax.dot_general(qv, kh, _DN["nt"], preferred_element_type=F32) * SCALE + biases[h][0]
                s = jnp.where(jnp.logical_and(first, col < W), NEG_INF, s)
                m = jnp.max(s, axis=1, keepdims=True)
                p = jnp.exp(s - m)
                l = jnp.sum(p, axis=1, keepdims=True)
                d = lax.dot_general(p.astype(BF16), vh, _DN["nn"], preferred_element_type=F32)
                acc = d if acc is None else acc + d
                inv.append(1.0 / l)
                lse_ref[h, rows, :] = m + jnp.log(l)
            o_ref[rows, :] = acc * jnp.where(lo, inv[0], inv[1])

    sp = _dil_pair_specs(qoff, koff, voff)
    return pl.pallas_call(
        body, out_shape=[jax.ShapeDtypeStruct((S, GROUP_W), F32), jax.ShapeDtypeStruct((B_HPG, S, 1), F32)],
        grid=(NPG, DIL_STEPS), in_specs=[sp["smem"], sp["smem"], sp["q"], sp["k_prev"], sp["k"], sp["v_prev"], sp["v"]],
        out_specs=[sp["o"], sp["col"]], scratch_shapes=[pltpu.VMEM((DIL_ROWS + W, LANES), BF16)] * 2,
        compiler_params=_cparams(("parallel", "parallel")), name=name,
    )(coef_t, nbs_t, q, k, k, v, v)


def _dil_pair_merge(os, lses, name):
    T = 1024

    def body(o0_ref, o1_ref, o2_ref, l0_ref, l1_ref, l2_ref, om_ref, omb_ref, l_ref):
        lo = _first_head((T, LANES))
        weights = []
        for h in range(2):
            l0, l1, l2 = l0_ref[h], l1_ref[h], l2_ref[h]
            m = jnp.maximum(jnp.maximum(l0, l1), l2)
            e0, e1, e2 = jnp.exp(l0 - m), jnp.exp(l1 - m), jnp.exp(l2 - m)
            den = e0 + e1 + e2
            weights.append((e0 / den, e1 / den, e2 / den))
            l_ref[h] = m + jnp.log(den)
        om = (jnp.where(lo, weights[0][0], weights[1][0]) * o0_ref[...]
              + jnp.where(lo, weights[0][1], weights[1][1]) * o1_ref[...]
              + jnp.where(lo, weights[0][2], weights[1][2]) * o2_ref[...])
        om_ref[...] = om
        omb_ref[...] = om.astype(BF16)

    ob = pl.BlockSpec((T, LANES), lambda p, i: (i, p))
    lb = pl.BlockSpec((2, T, 1), lambda p, i: (p, i, 0))
    return pl.pallas_call(
        body,
        out_shape=[jax.ShapeDtypeStruct((S, B_OUT_W), F32), jax.ShapeDtypeStruct((S, B_OUT_W), BF16),
                   jax.ShapeDtypeStruct((B_HPG, S, 1), F32)],
        grid=(NPG, S // T), in_specs=[ob] * 3 + [lb] * 3, out_specs=[ob, ob, lb],
        compiler_params=_cparams(("parallel", "parallel")), name=name,
    )(*os, *lses)


def _dil_pair_dq(g, q, k, v, qoff, koff, voff, do, stats, name):
    coef_t, nbs_t = _dil_tables()

    def body(coef_ref, nbs_ref, q_ref, kh_ref, k_ref, vh_ref, v_ref, do_ref, l_ref, d_ref, dq_ref, kf, vf):
        hp = pl.program_id(0)
        n = pl.program_id(1)
        nbs = nbs_ref[B_HPG * g + 2 * hp]
        kf[0:W, :] = kh_ref[...]
        kf[W:, :] = k_ref[...]
        vf[0:W, :] = vh_ref[...]
        vf[W:, :] = v_ref[...]
        biases = [_dil_bias(coef_ref[B_HPG * g + 2 * hp + h], False) for h in range(2)]
        col = biases[0][1]
        lo2 = _first_head((2 * W, LANES))
        for b in range(DIL_SUB):
            first = lax.rem(n * DIL_SUB + b, nbs) == 0
            rows = slice(b * W, (b + 1) * W)
            both = slice(b * W, (b + 2) * W)
            qv = q_ref[rows, :]
            dov = do_ref[rows, :]
            acc = None
            for h, (kh, vh) in enumerate(zip(_each_head(kf[both, :], lo2), _each_head(vf[both, :], lo2))):
                s = lax.dot_general(qv, kh, _DN["nt"], preferred_element_type=F32) * SCALE + biases[h][0]
                s = jnp.where(jnp.logical_and(first, col < W), NEG_INF, s)
                p = jnp.exp(s - l_ref[h, rows, :])
                dp = lax.dot_general(dov, vh, _DN["nt"], preferred_element_type=F32)
                ds = (p * (dp - d_ref[h, rows, :]) * SCALE).astype(BF16)
                d = lax.dot_general(ds, kh, _DN["nn"], preferred_element_type=F32)
                acc = d if acc is None else acc + d
            dq_ref[rows, :] = acc.astype(BF16)

    sp = _dil_pair_specs(qoff, koff, voff)
    return pl.pallas_call(
        body, out_shape=jax.ShapeDtypeStruct((S, GROUP_W), BF16), grid=(NPG, DIL_STEPS),
        in_specs=[sp["smem"], sp["smem"], sp["q"], sp["k_prev"], sp["k"], sp["v_prev"], sp["v"], sp["o"], sp["col"],
                  sp["col2"]],
        out_specs=sp["o"], scratch_shapes=[pltpu.VMEM((DIL_ROWS + W, LANES), BF16)] * 2,
        compiler_params=_cparams(("parallel", "parallel")), name=name,
    )(coef_t, nbs_t, q, k, k, v, v, do, stats, stats)


def _dil_pair_dkv(g, q, k, v, qoff, koff, voff, do, stats, name):
    coef_t, nbs_t = _dil_tables()

    def body(coef_ref, nbs_ref, k_ref, v_ref, q_ref, qn_ref, do_ref, don_ref, l_ref, ln_ref, d_ref, dn_ref,
             dk_ref, dv_ref, qf, dof, lf, df):
        hp = pl.program_id(0)
        n = pl.program_id(1)
        nbs = nbs_ref[B_HPG * g + 2 * hp]
        qf[0:DIL_ROWS, :] = q_ref[...]
        qf[DIL_ROWS:, :] = qn_ref[...]
        dof[0:DIL_ROWS, :] = do_ref[...]
        dof[DIL_ROWS:, :] = don_ref[...]
        lf[:, :, 0:DIL_ROWS] = l_ref[...]
        lf[:, :, DIL_ROWS:] = ln_ref[...]
        df[:, :, 0:DIL_ROWS] = d_ref[...]
        df[:, :, DIL_ROWS:] = dn_ref[...]
        biases = [_dil_bias(coef_ref[B_HPG * g + 2 * hp + h], True) for h in range(2)]
        col = biases[0][1]
        lo = _first_head((W, LANES))
        lo2 = _first_head((2 * W, LANES))
        for b in range(DIL_SUB):
            no_next = lax.rem(n * DIL_SUB + b + 1, nbs) == 0
            rows = slice(b * W, (b + 1) * W)
            both = slice(b * W, (b + 2) * W)
            dd = dof[both, :]
            dk = dv = None
            for h, (kh, vh, qh, ddh) in enumerate(zip(_each_head(k_ref[rows, :], lo), _each_head(v_ref[rows, :], lo),
                                                      _each_head(qf[both, :], lo2), _each_head(dd, lo2))):
                st = lax.dot_general(kh, qh, _DN["nt"], preferred_element_type=F32) * SCALE + biases[h][0]
                st = jnp.where(jnp.logical_and(no_next, col >= W), NEG_INF, st)
                pt = jnp.exp(st - lf[h, :, both])
                d = lax.dot_general(pt.astype(BF16), ddh, _DN["nn"], preferred_element_type=F32)
                dv = d if dv is None else dv + d
                dpt = lax.dot_general(vh, dd, _DN["nt"], preferred_element_type=F32)
                dst = (pt * (dpt - df[h, :, both]) * SCALE).astype(BF16)
                d = lax.dot_general(dst, qh, _DN["nn"], preferred_element_type=F32)
                dk = d if dk is None else dk + d
            dk_ref[rows, :] = dk.astype(BF16)
            dv_ref[rows, :] = dv.astype(BF16)

    sp = _dil_pair_specs(qoff, koff, voff)
    wide = jax.ShapeDtypeStruct((S, GROUP_W), BF16)
    return pl.pallas_call(
        body, out_shape=[wide, wide], grid=(NPG, DIL_STEPS),
        in_specs=[sp["smem"], sp["smem"], sp["k"], sp["v"], sp["q"], sp["q_next"], sp["o"], sp["o_next"], sp["row"],
                  sp["row_next"], sp["row2"], sp["row2_next"]],
        out_specs=[sp["o"], sp["o"]],
        scratch_shapes=[pltpu.VMEM((DIL_ROWS + W, LANES), BF16)] * 2 + [pltpu.VMEM((2, 1, DIL_ROWS + W), F32)] * 2,
        compiler_params=_cparams(("parallel", "parallel")), name=name,
    )(coef_t, nbs_t, k, v, q, q, do, do, stats, stats, stats, stats)


FFN_ROWS = 512
FFN_COLS = 256
HALO = 8


def _shifted(u, halo, back):
    T = u.shape[0]
    rows = lax.broadcasted_iota(jnp.int32, u.shape, 0)
    if back:
        s1 = jnp.where(rows == 0, halo[HALO - 1:HALO, :], pltpu.roll(u, 1, 0))
        s2 = jnp.where(rows == 0, halo[HALO - 2:HALO - 1, :],
                       jnp.where(rows == 1, halo[HALO - 1:HALO, :], pltpu.roll(u, 2, 0)))
    else:
        s1 = jnp.where(rows == T - 1, halo[0:1, :], pltpu.roll(u, T - 1, 0))
        s2 = jnp.where(rows == T - 1, halo[1:2, :],
                       jnp.where(rows == T - 2, halo[0:1, :], pltpu.roll(u, T - 2, 0)))
    return s1, s2


def _conv_parts(u_ref, h_ref, w_ref, b_ref, first):
    out = []
    for p in range(2):
        u = u_ref[p]
        halo = jnp.where(first, 0.0, h_ref[p])
        u1, u2 = _shifted(u, halo, True)
        w = w_ref[p]
        out.append((w[0:1, :] * u2 + w[1:2, :] * u1 + w[2:3, :] * u + b_ref[p], u1, u2, u))
    return out


def _ffn_specs():
    T, C = FFN_ROWS, FFN_COLS
    blk = pl.BlockSpec((2, T, C), lambda j, i: (0, i, j))
    prev = pl.BlockSpec((2, HALO, C), lambda j, i: (0, jnp.maximum(i * (T // HALO) - 1, 0), j))
    nxt = pl.BlockSpec((2, HALO, C), lambda j, i: (0, jnp.minimum((i + 1) * (T // HALO), S // HALO - 1), j))
    wsp = pl.BlockSpec((2, 3, C), lambda j, i: (0, 0, j))
    bsp = pl.BlockSpec((2, 1, C), lambda j, i: (0, 0, j))
    one = pl.BlockSpec((T, C), lambda j, i: (i, j))
    return blk, prev, nxt, wsp, bsp, one


def _ffn_act_fwd(u, w, b, name):
    blk, prev, _, wsp, bsp, one = _ffn_specs()

    def body(u_ref, h_ref, w_ref, b_ref, o_ref):
        (a, _, _, _), (g, _, _, _) = _conv_parts(u_ref, h_ref, w_ref, b_ref, pl.program_id(1) == 0)
        o_ref[...] = (g / (1.0 + jnp.exp(-g)) * a).astype(BF16)

    return pl.pallas_call(
        body, out_shape=jax.ShapeDtypeStruct((S, FF), BF16), grid=(FF // FFN_COLS, S // FFN_ROWS),
        in_specs=[blk, prev, wsp, bsp], out_specs=one,
        compiler_params=_cparams(("parallel", "parallel")), name=name,
    )(u, u, w, b)


def _ffn_act_bwd(u, dact, w, b, name):
    blk, prev, _, wsp, bsp, one = _ffn_specs()

    def body(u_ref, h_ref, da_ref, w_ref, b_ref, duc_ref, dwb_ref):
        i = pl.program_id(1)
        (a, a1, a2, a0), (g, g1, g2, g0) = _conv_parts(u_ref, h_ref, w_ref, b_ref, i == 0)
        dact_v = da_ref[...]
        sg = 1.0 / (1.0 + jnp.exp(-g))
        d_a = dact_v * (g * sg)
        d_g = dact_v * a * (sg * (1.0 + g * (1.0 - sg)))
        duc_ref[0] = d_a
        duc_ref[1] = d_g

        @pl.when(i == 0)
        def _():
            dwb_ref[...] = jnp.zeros(dwb_ref.shape, F32)

        for p, (d, s2, s1, s0) in enumerate(((d_a, a2, a1, a0), (d_g, g2, g1, g0))):
            dwb_ref[p, 0:1, :] += jnp.sum(d * s2, axis=0, keepdims=True)
            dwb_ref[p, 1:2, :] += jnp.sum(d * s1, axis=0, keepdims=True)
            dwb_ref[p, 2:3, :] += jnp.sum(d * s0, axis=0, keepdims=True)
            dwb_ref[p, 3:4, :] += jnp.sum(d, axis=0, keepdims=True)

    return pl.pallas_call(
        body, out_shape=[jax.ShapeDtypeStruct((2, S, FF), F32), jax.ShapeDtypeStruct((2, 8, FF), F32)],
        grid=(FF // FFN_COLS, S // FFN_ROWS), in_specs=[blk, prev, one, wsp, bsp],
        out_specs=[blk, pl.BlockSpec((2, 8, FFN_COLS), lambda j, i: (0, 0, j))],
        compiler_params=_cparams(("parallel", "arbitrary")), name=name,
    )(u, u, dact, w, b)


def _ffn_conv_bwd(duc, w, name):
    blk, _, nxt, wsp, _, _ = _ffn_specs()
    last = S // FFN_ROWS - 1

    def body(d_ref, h_ref, w_ref, du_ref):
        is_last = pl.program_id(1) == last
        for p in range(2):
            d = d_ref[p]
            halo = jnp.where(is_last, 0.0, h_ref[p])
            d1, d2 = _shifted(d, halo, False)
            wv = w_ref[p]
            du_ref[p] = (wv[2:3, :] * d + wv[1:2, :] * d1 + wv[0:1, :] * d2).astype(BF16)

    return pl.pallas_call(
        body, out_shape=jax.ShapeDtypeStruct((2, S, FF), BF16), grid=(FF // FFN_COLS, S // FFN_ROWS),
        in_specs=[blk, nxt, wsp], out_specs=blk,
        compiler_params=_cparams(("parallel", "parallel")), name=name,
    )(duc, duc, w)


def _adam_update(w, gv, m, v):
    c1 = 1.0 / (1.0 - ADAM_B1 ** ADAM_STEP)
    c2 = 1.0 / (1.0 - ADAM_B2 ** ADAM_STEP)
    mn = ADAM_B1 * m + (1.0 - ADAM_B1) * gv
    vn = ADAM_B2 * v + (1.0 - ADAM_B2) * (gv * gv)
    return -ADAM_LR * ((mn * c1) / (jnp.sqrt(vn * c2) + ADAM_EPS) + ADAM_WD * w), mn, vn


def _adamw(w, g, m, v, name):
    rows = w.shape[0]
    T = 8
    for cand in (256, 128, 64, 32, 16, 8):
        if rows % cand == 0:
            T = cand
            break

    def body(w_ref, g_ref, m_ref, v_ref, d_ref, mo_ref, vo_ref):
        d_ref[...], mo_ref[...], vo_ref[...] = _adam_update(w_ref[...], g_ref[...], m_ref[...], v_ref[...])

    blk = pl.BlockSpec((T, w.shape[1]), lambda i: (i, 0))
    sds = jax.ShapeDtypeStruct(w.shape, F32)
    return pl.pallas_call(
        body, out_shape=[sds, sds, sds], grid=(rows // T,), in_specs=[blk] * 4, out_specs=[blk] * 3,
        compiler_params=_cparams(("parallel",)), name=name,
    )(w, g, m, v)


ANY = pl.BlockSpec(memory_space=pl.ANY)


def _place():
    x, y, c = lax.axis_index("x"), lax.axis_index("y"), lax.axis_index("c")
    chips = [(1 - x, y), (x, 1 - y), (1 - x, 1 - y)]
    return x, y, c, chips


def _place_own(w, slot_arr, name):
    rows = w.shape[0]
    T = 16
    for cand in (2048, 1024, 512, 256, 128, 64, 32, 16):
        if rows % cand == 0:
            T = cand
            break

    def body(k_ref, w_ref, o_ref):
        o_ref[...] = w_ref[...]

    return pl.pallas_call(
        body, out_shape=jax.ShapeDtypeStruct((N_CHIPS, rows, FLAT_W), w.dtype),
        grid_spec=pltpu.PrefetchScalarGridSpec(
            num_scalar_prefetch=1, grid=(rows // T,),
            in_specs=[pl.BlockSpec((T, FLAT_W), lambda i, k: (i, 0))],
            out_specs=pl.BlockSpec((None, T, FLAT_W), lambda i, k: (k[0], i, 0))),
        compiler_params=_cparams(("parallel",)), name=name,
    )(slot_arr, w)


def _allgather_shards(w, buf):
    half_rows = w.shape[0] // 2
    assert half_rows % 16 == 0

    def body(w_ref, buf_ref, g_ref, send_sems, recv_sems):
        x, y, c, chips = _place()
        myk = 2 * x + y
        sibling = (x, y, 1 - c)
        h0 = pl.multiple_of(c * half_rows, 16)
        h1 = pl.multiple_of((1 - c) * half_rows, 16)

        def half(k, start):
            return g_ref.at[k, pl.ds(start, half_rows), :]

        def rcopy(sem, src, dst, to):
            return pltpu.make_async_remote_copy(src_ref=src, dst_ref=dst, send_sem=send_sems.at[sem],
                                                recv_sem=recv_sems.at[sem], device_id=to, device_id_type=MESH)

        ici = [rcopy(r, w_ref.at[pl.ds(h0, half_rows), :], half(myk, h0), (*chip, c)) for r, chip in enumerate(chips)]
        for cp in ici:
            cp.start()
        ks = [2 * cx + cy for cx, cy in chips]
        fwd = [rcopy(3 + r, half(ks[r], h0), half(ks[r], h0), sibling) for r in range(3)]
        for r in range(3):
            rcopy(r, half(ks[r], h0), half(ks[r], h0), (*chips[r], c)).wait_recv()
            fwd[r].start()
        for r in range(3):
            rcopy(3 + r, half(ks[r], h1), half(ks[r], h1), sibling).wait_recv()
        for cp in ici + fwd:
            cp.wait_send()

    return pl.pallas_call(
        body, out_shape=jax.ShapeDtypeStruct(buf.shape, w.dtype), in_specs=[ANY, ANY], out_specs=ANY,
        scratch_shapes=[pltpu.SemaphoreType.DMA((6,)), pltpu.SemaphoreType.DMA((6,))],
        input_output_aliases={1: 0},
        compiler_params=pltpu.CompilerParams(has_side_effects=True), name="allgather_shards",
    )(w, buf)


HBM_SPEC = pl.BlockSpec(memory_space=pltpu.HBM)
SEM_SPEC = pl.BlockSpec(memory_space=pltpu.SEMAPHORE)
DATAFLOW = pltpu.SideEffectType.DATAFLOW_SIDE_EFFECTING
OWN_SLOT = 3


def _late_gather_start(w, land):
    def body(w_ref, land_ref, send_sems, recv_sems, w_thru, land_thru, token):
        x, y, c, chips = _place()
        for r, chip in enumerate(chips):
            pltpu.make_async_remote_copy(src_ref=w_ref, dst_ref=land_ref.at[r], send_sem=send_sems.at[r],
                                         recv_sem=recv_sems.at[r], device_id=(*chip, c), device_id_type=MESH).start()
        token[...] = jnp.zeros_like(token)

    return pl.pallas_call(
        body, name="late_gather_start",
        out_shape=(pltpu.SemaphoreType.DMA((3,)), pltpu.SemaphoreType.DMA((3,)), pltpu.HBM(w.shape, w.dtype),
                   pltpu.HBM(land.shape, land.dtype), jax.ShapeDtypeStruct((8, LANES), F32)),
        in_specs=(HBM_SPEC, HBM_SPEC),
        out_specs=(SEM_SPEC, SEM_SPEC, HBM_SPEC, HBM_SPEC, pl.BlockSpec(memory_space=pltpu.VMEM)),
        input_output_aliases={0: 2, 1: 3}, compiler_params=pltpu.CompilerParams(has_side_effects=DATAFLOW),
    )(pltpu.with_memory_space_constraint(w, pltpu.HBM), pltpu.with_memory_space_constraint(land, pltpu.HBM))


def _late_gather_wait(send_sems, recv_sems, w_thru, land_thru, after):
    def body(w_ref, land_ref, send_sems, recv_sems, after_ref, w_dead, got_ref):
        x, y, c, chips = _place()
        for r, chip in enumerate(chips):
            cp = pltpu.make_async_remote_copy(src_ref=w_ref, dst_ref=land_ref.at[r], send_sem=send_sems.at[r],
                                              recv_sem=recv_sems.at[r], device_id=(*chip, c), device_id_type=MESH)
            cp.wait_send()
            cp.wait_recv()

    return pl.pallas_call(
        body, name="late_gather_wait",
        out_shape=(pltpu.HBM(w_thru.shape, w_thru.dtype), pltpu.HBM(land_thru.shape, land_thru.dtype)),
        in_specs=(HBM_SPEC, HBM_SPEC, SEM_SPEC, SEM_SPEC, pl.BlockSpec(memory_space=pl.ANY)),
        out_specs=(HBM_SPEC, HBM_SPEC), input_output_aliases={0: 0, 1: 1},
        compiler_params=pltpu.CompilerParams(has_side_effects=DATAFLOW),
    )(w_thru, land_thru, send_sems, recv_sems, after)


def _flat_tile(rows):
    return next(t for t in (2048, 1024, 512, 256, 128, 64, 32, 16) if rows % t == 0)


def _sibling_swap_half(g, tag):
    half = g.shape[1] // 2

    def body(g_ref, o_ref, send_sem, recv_sem):
        x, y, c, _ = _place()
        theirs = pl.multiple_of((1 - c) * half, 8)
        cp = pltpu.make_async_remote_copy(src_ref=g_ref.at[:, pl.ds(theirs, half), :], dst_ref=o_ref,
                                          send_sem=send_sem, recv_sem=recv_sem, device_id=(x, y, 1 - c),
                                          device_id_type=MESH)
        cp.start()
        cp.wait()

    return pl.pallas_call(
        body, out_shape=jax.ShapeDtypeStruct((N_CHIPS, half, FLAT_W), F32), in_specs=[ANY], out_specs=ANY,
        scratch_shapes=[pltpu.SemaphoreType.DMA, pltpu.SemaphoreType.DMA],
        compiler_params=pltpu.CompilerParams(has_side_effects=True), name=f"rs_sibling_swap_{tag}",
    )(g)


def _sibling_swap_start(g, land):
    half = g.shape[1] // 2

    def body(g_ref, land_ref, send_sem, recv_sem, g_thru, land_thru, token):
        x, y, c, _ = _place()
        theirs = pl.multiple_of((1 - c) * half, 8)
        pltpu.make_async_remote_copy(src_ref=g_ref.at[:, pl.ds(theirs, half), :], dst_ref=land_ref, send_sem=send_sem,
                                     recv_sem=recv_sem, device_id=(x, y, 1 - c), device_id_type=MESH).start()
        token[...] = jnp.zeros_like(token)

    return pl.pallas_call(
        body, name="rs_swap_start",
        out_shape=(pltpu.SemaphoreType.DMA(()), pltpu.SemaphoreType.DMA(()), pltpu.HBM(g.shape, g.dtype),
                   pltpu.HBM(land.shape, land.dtype), jax.ShapeDtypeStruct((8, LANES), F32)),
        in_specs=(HBM_SPEC, HBM_SPEC),
        out_specs=(SEM_SPEC, SEM_SPEC, HBM_SPEC, HBM_SPEC, pl.BlockSpec(memory_space=pltpu.VMEM)),
        input_output_aliases={0: 2, 1: 3}, compiler_params=pltpu.CompilerParams(has_side_effects=DATAFLOW),
    )(pltpu.with_memory_space_constraint(g, pltpu.HBM), pltpu.with_memory_space_constraint(land, pltpu.HBM))


def _sibling_swap_wait(send_sem, recv_sem, g_thru, land_thru, after):
    half = land_thru.shape[1]

    def body(g_ref, land_ref, send_sem, recv_sem, after_ref, g_done, got_ref):
        x, y, c, _ = _place()
        theirs = pl.multiple_of((1 - c) * half, 8)
        cp = pltpu.make_async_remote_copy(src_ref=g_ref.at[:, pl.ds(theirs, half), :], dst_ref=land_ref,
                                          send_sem=send_sem, recv_sem=recv_sem, device_id=(x, y, 1 - c),
                                          device_id_type=MESH)
        cp.wait_send()
        cp.wait_recv()

    return pl.pallas_call(
        body, name="rs_swap_wait",
        out_shape=(pltpu.HBM(g_thru.shape, g_thru.dtype), pltpu.HBM(land_thru.shape, land_thru.dtype)),
        in_specs=(HBM_SPEC, HBM_SPEC, SEM_SPEC, SEM_SPEC, pl.BlockSpec(memory_space=pl.ANY)),
        out_specs=(HBM_SPEC, HBM_SPEC), input_output_aliases={0: 0, 1: 1},
        compiler_params=pltpu.CompilerParams(has_side_effects=DATAFLOW),
    )(g_thru, land_thru, send_sem, recv_sem, after)


def _pair_sum(g, other, c_arr, tag):
    half = other.shape[1]
    T = _flat_tile(half)

    def body(c_ref, g_ref, o_ref, s_ref):
        s_ref[...] = (g_ref[...] + o_ref[...]).astype(BF16)

    nb = half // T
    return pl.pallas_call(
        body, out_shape=jax.ShapeDtypeStruct((N_CHIPS, half, FLAT_W), BF16),
        grid_spec=pltpu.PrefetchScalarGridSpec(
            num_scalar_prefetch=1, grid=(N_CHIPS, nb),
            in_specs=[pl.BlockSpec((None, T, FLAT_W), lambda k, i, c: (k, c[0] * nb + i, 0)),
                      pl.BlockSpec((None, T, FLAT_W), lambda k, i, c: (k, i, 0))],
            out_specs=pl.BlockSpec((None, T, FLAT_W), lambda k, i, c: (k, i, 0))),
        compiler_params=_cparams(("parallel", "parallel")), name=f"rs_pair_sum_{tag}",
    )(c_arr, g, other)


def _chip_exchange_start(s, land, tag):
    def body(s_ref, land_ref, send_sems, recv_sems, s_thru, land_thru, token):
        x, y, c, chips = _place()
        for r, (cx, cy) in enumerate(chips):
            pltpu.make_async_remote_copy(src_ref=s_ref.at[2 * cx + cy], dst_ref=land_ref.at[r], send_sem=send_sems.at[r],
                                         recv_sem=recv_sems.at[r], device_id=(cx, cy, c), device_id_type=MESH).start()
        token[...] = jnp.zeros_like(token)

    return pl.pallas_call(
        body, name=f"rs_exchange_start_{tag}",
        out_shape=(pltpu.SemaphoreType.DMA((3,)), pltpu.SemaphoreType.DMA((3,)), pltpu.HBM(s.shape, s.dtype),
                   pltpu.HBM(land.shape, land.dtype), jax.ShapeDtypeStruct((8, LANES), F32)),
        in_specs=(HBM_SPEC, HBM_SPEC),
        out_specs=(SEM_SPEC, SEM_SPEC, HBM_SPEC, HBM_SPEC, pl.BlockSpec(memory_space=pltpu.VMEM)),
        input_output_aliases={0: 2, 1: 3}, compiler_params=pltpu.CompilerParams(has_side_effects=DATAFLOW),
    )(pltpu.with_memory_space_constraint(s, pltpu.HBM), pltpu.with_memory_space_constraint(land, pltpu.HBM))


def _chip_exchange_wait(send_sems, recv_sems, s_thru, land_thru, after, tag):
    def body(s_ref, land_ref, send_sems, recv_sems, after_ref, s_done, got_ref):
        x, y, c, chips = _place()
        for r, (cx, cy) in enumerate(chips):
            cp = pltpu.make_async_remote_copy(src_ref=s_ref.at[2 * cx + cy], dst_ref=land_ref.at[r],
                                              send_sem=send_sems.at[r], recv_sem=recv_sems.at[r], device_id=(cx, cy, c),
                                              device_id_type=MESH)
            cp.wait_send()
            cp.wait_recv()

    return pl.pallas_call(
        body, name=f"rs_exchange_wait_{tag}",
        out_shape=(pltpu.HBM(s_thru.shape, s_thru.dtype), pltpu.HBM(land_thru.shape, land_thru.dtype)),
        in_specs=(HBM_SPEC, HBM_SPEC, SEM_SPEC, SEM_SPEC, pl.BlockSpec(memory_space=pl.ANY)),
        out_specs=(HBM_SPEC, HBM_SPEC), input_output_aliases={0: 0, 1: 1},
        compiler_params=pltpu.CompilerParams(has_side_effects=DATAFLOW),
    )(s_thru, land_thru, send_sems, recv_sems, after)


def _chip_sum(s, r, k_arr, tag):
    half = s.shape[1]
    T = _flat_tile(half)

    def body(k_ref, s_ref, r_ref, o_ref):
        o_ref[...] = ((s_ref[...].astype(F32) + r_ref[0].astype(F32)) + r_ref[1].astype(F32)) + r_ref[2].astype(F32)

    return pl.pallas_call(
        body, out_shape=jax.ShapeDtypeStruct((half, FLAT_W), F32),
        grid_spec=pltpu.PrefetchScalarGridSpec(
            num_scalar_prefetch=1, grid=(half // T,),
            in_specs=[pl.BlockSpec((None, T, FLAT_W), lambda i, k: (k[0], i, 0)),
                      pl.BlockSpec((3, T, FLAT_W), lambda i, k: (0, i, 0))],
            out_specs=pl.BlockSpec((T, FLAT_W), lambda i, k: (i, 0))),
        compiler_params=_cparams(("parallel",)), name=f"rs_chip_sum_{tag}",
    )(k_arr, s, r)


def _sibling_send(t, tag):
    def body(t_ref, o_ref, send_sem, recv_sem):
        x, y, c, _ = _place()
        cp = pltpu.make_async_remote_copy(src_ref=t_ref, dst_ref=o_ref, send_sem=send_sem, recv_sem=recv_sem,
                                          device_id=(x, y, 1 - c), device_id_type=MESH)
        cp.start()
        cp.wait()

    return pl.pallas_call(
        body, out_shape=jax.ShapeDtypeStruct(t.shape, F32), in_specs=[ANY], out_specs=ANY,
        scratch_shapes=[pltpu.SemaphoreType.DMA, pltpu.SemaphoreType.DMA],
        compiler_params=pltpu.CompilerParams(has_side_effects=True), name=f"rs_sibling_send_{tag}",
    )(t)


def _allreduce_small(v):
    def body(v_ref, o_ref, buf, send_sems, recv_sems):
        x, y, c, _ = _place()
        me = 4 * x + 2 * y + c
        buf[me] = v_ref[...]
        cps = []
        for mask in range(1, 8):
            a, b, d = (mask >> 2) & 1, (mask >> 1) & 1, mask & 1
            peer = (x + a - 2 * a * x, y + b - 2 * b * y, c + d - 2 * d * c)
            cps.append(pltpu.make_async_remote_copy(
                src_ref=v_ref, dst_ref=buf.at[me], send_sem=send_sems.at[mask - 1], recv_sem=recv_sems.at[mask - 1],
                device_id=peer, device_id_type=MESH))
        for cp in cps:
            cp.start()
        for cp in cps:
            cp.wait()
        total = buf[0]
        for dev in range(1, 8):
            total = total + buf[dev]
        o_ref[...] = total

    vm = pl.BlockSpec(memory_space=pltpu.VMEM)
    return pl.pallas_call(
        body, out_shape=jax.ShapeDtypeStruct((SMALL_ROWS, 1024), F32), in_specs=[vm], out_specs=vm,
        scratch_shapes=[pltpu.VMEM((8, SMALL_ROWS, 1024), F32), pltpu.SemaphoreType.DMA((7,)),
                        pltpu.SemaphoreType.DMA((7,))],
        compiler_params=pltpu.CompilerParams(has_side_effects=True), name="allreduce_small",
    )(v)


def _col_to_row(t):
    return t.reshape(t.shape[0], 1, S)


def _residue_rows(t, d, inverse=False):
    if d == 1:
        return t
    shape = (d, S // d) if inverse else (S // d, d)
    return t.reshape(shape + t.shape[1:]).transpose(1, 0, 2).reshape(t.shape)


def _residue_vecs(t, d, inverse=False):
    if d == 1:
        return t
    shape = (d, S // d) if inverse else (S // d, d)
    return t.reshape((t.shape[0],) + shape).transpose(0, 2, 1).reshape(t.shape)


def _ffn_fwd(x, g, w_up, cw, cb, w_down, tag):
    (h,) = _rms_fwd(x, [g], f"{tag}_norm")
    u = _mm(h, w_up, mode="nn", tm=1024, tn=1408, tk=1024, o_split=2, name=f"{tag}_up")
    act = _ffn_act_fwd(u, cw, cb, f"{tag}_act")
    x_out = _mm(act, w_down, mode="nn", tm=1024, tn=512, tk=FF, res=x, name=f"{tag}_down")
    return x_out, (h, u, act)


def _ffn_bwd(x, g, w_up, cw, cb, w_down, saved, dx, dxb, tag):
    h, u, act = saved
    d_w_down = _mm(act, dxb, mode="tn", tm=1408, tn=512, tk=2048, name=f"{tag}_dwdown")
    dact = _mm(dxb, w_down, mode="nt", tm=1024, tn=1408, tk=1024, name=f"{tag}_dact")
    duc, dwb = _ffn_act_bwd(u, dact, cw, cb, f"{tag}_dgate")
    du = _ffn_conv_bwd(duc, cw, f"{tag}_dconv")
    d_w_up = _mm(h, du, mode="tn", tm=1024, tn=1408, tk=2048, b_split=2, name=f"{tag}_dwup")
    dh = _mm(du, w_up, mode="nt", tm=1024, tn=512, tk=FF, a_split=1, name=f"{tag}_dh")
    dx_new, dxb_new, (dg,) = _rms_bwd(x, dx, [(g, dh)], f"{tag}_dnorm")
    d_cw = dwb[:, 0:3, :].transpose(1, 0, 2).reshape(3, 2 * FF)
    d_cb = dwb[:, 3, :].reshape(2 * FF)
    return dx_new, dxb_new, dict(w_up=d_w_up, w_down=d_w_down, conv_w=d_cw, conv_b=d_cb, norm_g=dg.reshape(D))


def _local_step(x, target, p, late_weights, late_grads_ready, late_grads_continue):
    g = {}
    (h1,) = _rms_fwd(x, [p["mix_norm_g"][0]], "a_norm")
    w_qkv = p["a_w_in"][:, :QKV_W]
    w_f = jnp.pad(p["a_w_in"][:, QKV_W:], ((0, 0), (0, LANES - A_HEADS)))
    b_f = jnp.pad(p["a_b_f"].reshape(1, A_HEADS), ((0, 0), (0, LANES - A_HEADS)))
    qkv = _mm(h1, w_qkv, mode="nn", tm=1024, tn=512, tk=1024, out_dtype=BF16, name="a_qkv")
    pf = _mm(h1, w_f, mode="nn", tm=1024, tn=LANES, tk=1024, name="a_gate")
    aug_q, aug_k = _fgate_fwd(pf, b_f, "a_gate_scan")
    oa2, lse_a = _fox_pair_fwd(qkv, aug_q, aug_k, "a_attn")
    x1 = _mm(oa2, p["a_w_out"], mode="nn", tm=1024, tn=512, tk=1024, res=x, name="a_out")
    p = {**p, **late_weights(x1)}
    x2, ffn0 = _ffn_fwd(x1, p["ffn_norm_g"][0], p["ffn_w_up"][0], p["conv_w"][0], p["conv_b"][0], p["ffn_w_down"][0], "f0")
    hk, h3 = _rms_fwd(x2, [p["kv_norm_g"], p["mix_norm_g"][1]], "kv_b_norm")
    kvb = _mm(hk, p["w_kv"], mode="nn", tm=1024, tn=512, tk=1024, out_dtype=BF16, name="kv_proj")
    qb = _mm(h3, p["b_w_q"], mode="nn", tm=1024, tn=512, tk=1024, out_dtype=BF16, name="b_q")
    dil_in = []
    for gi, (_, d) in enumerate(B_GROUPS):
        if d == 1:
            dil_in.append((qb, kvb, kvb, gi * NPG, gi * NPG, (3 + gi) * NPG))
        else:
            qg = _residue_rows(qb[:, gi * GROUP_W:(gi + 1) * GROUP_W], d)
            kvg = _residue_rows(kvb.reshape(S, 2, 3, GROUP_W)[:, :, gi, :].reshape(S, 2 * GROUP_W), d)
            dil_in.append((qg, kvg, kvg, 0, 0, NPG))
    o_g, lse_g = [], []
    for gi, (_, d) in enumerate(B_GROUPS):
        qg, kg, vg, qoff, koff, voff = dil_in[gi]
        og, lg = _dil_pair_fwd(gi, qg, kg, vg, qoff, koff, voff, f"b_attn{gi}")
        o_g.append(_residue_rows(og, d, inverse=True))
        lse_g.append(_residue_vecs(lg, d, inverse=True))
    ob, ob2, lse_b = _dil_pair_merge(o_g, lse_g, "b_merge")
    x3 = _mm(ob2, p["b_w_out"], mode="nn", tm=1024, tn=512, tk=B_OUT_W, res=x2, name="b_out")
    x4, ffn1 = _ffn_fwd(x3, p["ffn_norm_g"][1], p["ffn_w_up"][1], p["conv_w"][1], p["conv_b"][1], p["ffn_w_down"][1], "f1")
    loss, dx, dxb, dg_final = _loss_head(x4, p["final_norm_g"], target, "loss_head")
    g["final_norm_g"] = dg_final.reshape(D)

    dx, dxb, gf1 = _ffn_bwd(x3, p["ffn_norm_g"][1], p["ffn_w_up"][1], p["conv_w"][1], p["conv_b"][1], p["ffn_w_down"][1],
                            ffn1, dx, dxb, "f1")
    g["b_w_out"] = _mm(ob2, dxb, mode="tn", tm=B_OUT_W, tn=512, tk=1024, name="b_dwout")
    dob = _mm(dxb, p["b_w_out"], mode="nt", tm=1024, tn=B_OUT_W, tk=1024, name="b_do")
    delta_b = _pair_rowdot(dob, ob, "b_delta")
    dob16 = dob.astype(BF16)
    stats_b = jnp.concatenate([lse_b, delta_b], axis=0)
    dq_g, dk_g, dv_g = [], [], []
    for gi, (_, d) in enumerate(B_GROUPS):
        qg, kg, vg, qoff, koff, voff = dil_in[gi]
        dog, stats_d = _residue_rows(dob16, d), _residue_vecs(stats_b, d)
        dqd = _dil_pair_dq(gi, qg, kg, vg, qoff, koff, voff, dog, stats_d, f"b_dq{gi}")
        dkd, dvd = _dil_pair_dkv(gi, qg, kg, vg, qoff, koff, voff, dog, _col_to_row(stats_d), f"b_dkv{gi}")
        dq_g.append(_residue_rows(dqd, d, inverse=True))
        dk_g.append(_residue_rows(dkd, d, inverse=True))
        dv_g.append(_residue_rows(dvd, d, inverse=True))
    dqb = jnp.concatenate(dq_g, axis=1)
    dkvb = jnp.concatenate(dk_g + dv_g, axis=1)
    g["b_w_q"] = _mm(h3, dqb, mode="tn", tm=1024, tn=512, tk=2048, name="b_dwq")
    dh3 = _mm(dqb, p["b_w_q"], mode="nt", tm=1024, tn=512, tk=B_Q_W, name="b_dh")
    g["w_kv"] = _mm(hk, dkvb, mode="tn", tm=1024, tn=512, tk=2048, name="kv_dw")
    dhk = _mm(dkvb, p["w_kv"], mode="nt", tm=1024, tn=512, tk=3072, name="kv_dh")
    dx, dxb, (dg_mix1, dg_kv) = _rms_bwd(x2, dx, [(p["mix_norm_g"][1], dh3), (p["kv_norm_g"], dhk)], "b_dnorm")
    g["kv_norm_g"] = dg_kv.reshape(D)
    dx, dxb, gf0 = _ffn_bwd(x1, p["ffn_norm_g"][0], p["ffn_w_up"][0], p["conv_w"][0], p["conv_b"][0], p["ffn_w_down"][0],
                            ffn0, dx, dxb, "f0")
    g["ffn_w_up"] = [gf0["w_up"], gf1["w_up"]]
    g["ffn_w_down"] = [gf0["w_down"], gf1["w_down"]]
    g["ffn_conv_w"] = jnp.stack([gf0["conv_w"], gf1["conv_w"]])
    token = late_grads_ready(g)
    a_w_out_t = p["a_w_out"] + token[0, 0].astype(BF16)
    g["a_w_out"] = _mm(oa2, dxb, mode="tn", tm=1024, tn=512, tk=2048, name="a_dwout")
    doa = _mm(dxb, a_w_out_t, mode="nt", tm=1024, tn=512, tk=1024, name="a_do")
    delta_a = _pair_rowdot(doa, oa2, "a_delta")
    token = late_grads_continue(delta_a)
    delta_row = _col_to_row(delta_a) + token[0, 0]
    dqa, dka, dva, dck, dcq = _fox_pair_bwd(qkv, doa, _col_to_row(lse_a), delta_row, aug_q, aug_k, "a_dattn")
    dqkv = jnp.concatenate([dqa, dka, dva], axis=1)
    pad_heads = lambda t: jnp.pad(t.reshape(A_HEADS, S).T, ((0, 0), (0, LANES - A_HEADS)))
    dpf, db_f = _fgate_bwd(pf, b_f, pad_heads(dck), pad_heads(dcq), "a_dgate_scan")
    g["a_b_f"] = db_f[:, :A_HEADS]
    d_w_qkv = _mm(h1, dqkv, mode="tn", tm=1024, tn=512, tk=2048, name="a_dwqkv")
    d_w_f = _mm(h1, dpf, mode="tn", tm=1024, tn=LANES, tk=1024, name="a_dwgate")
    g["a_w_in"] = jnp.concatenate([d_w_qkv, d_w_f[:, :A_HEADS]], axis=1)
    dh1 = _mm(dqkv, w_qkv, mode="nt", tm=1024, tn=512, tk=3072, name="a_dh")
    dh1 = _mm(dpf, w_f, mode="nt", tm=1024, tn=512, tk=LANES, res=dh1, name="a_dh_gate")
    dx, _, (dg_mix0,) = _rms_bwd(x, dx, [(p["mix_norm_g"][0], dh1)], "a_dnorm")

    g["mix_norm_g"] = jnp.stack([dg_mix0.reshape(D), dg_mix1.reshape(D)])
    g["ffn_norm_g"] = jnp.stack([gf0["norm_g"], gf1["norm_g"]])
    g["ffn_conv_b"] = jnp.stack([gf0["conv_b"], gf1["conv_b"]])
    return loss[0, 0], dx, g


_SHARD_SHAPES = {"a_w_in": (1, 1024, 772), "a_w_out": (1, 256, 1024), "b_w_q": (1, 1024, 384), "b_w_out": (1, 512, 256),
                 "w_kv": (1024, 768), "ffn_w_up": (2, 1024, 1408), "ffn_w_down": (2, 704, 1024), "ffn_conv_w": (2, 3, 1408)}
_SMALL = (("kv_norm_g", (1024,)), ("mix_norm_g", (2, 1024)), ("ffn_norm_g", (2, 1024)), ("final_norm_g", (1024,)),
          ("a_b_f", (1, 16)), ("ffn_conv_b", (2, 5632)))


def _unslabs(rows, L, R, C, rpad):
    nc = -(-C // FLAT_W)
    return rows.reshape(L, nc, rpad, FLAT_W).transpose(0, 2, 1, 3).reshape(L, rpad, nc * FLAT_W)[:, :R, :C]


_SEG_RT = {"ffn_w_down": 704, "a_w_in": 1024, "a_w_out": 256, "b_w_q": 1024, "b_w_out": 512, "w_kv": 1024,
           "ffn_w_up": 1024, "ffn_conv_w": 16}
_ROW_SHARDED = ("a_w_out", "ffn_w_down")


_LAYOUTS = {"early": ("a_w_in", "a_w_out"), "late": ("ffn_w_down", "b_w_q", "b_w_out", "w_kv", "ffn_w_up", "ffn_conv_w"),
            "grad_early": ("a_w_in", "a_w_out"),
            "grad_late": ("ffn_w_up", "ffn_w_down", "b_w_q", "w_kv", "b_w_out", "ffn_conv_w")}
_GRAD_ROWS = {"grad_early": 10240, "grad_late": 45056}


def _layout_rows(layout):
    used = sum(_seg_rows(*s) for s in _SEGS if s[0] in _LAYOUTS[layout])
    rows = _GRAD_ROWS.get(layout, used)
    assert rows >= used
    return rows


def _grad_layout(name):
    return "grad_early" if name in _LAYOUTS["grad_early"] else "grad_late"


def _seg(name, layout=None):
    layout = layout or _grad_layout(name)
    off = 0
    for s in sorted((s for s in _SEGS if s[0] in _LAYOUTS[layout]), key=lambda s: _LAYOUTS[layout].index(s[0])):
        _, L, R, C, rpad = s
        if layout in _GRAD_ROWS:
            per_layer = -(-C // FLAT_W) * rpad
            off = -(-off // per_layer) * per_layer
        if s[0] == name:
            rt = _SEG_RT[name]
            assert off % rt == 0 and rpad % rt == 0
            half = _layout_rows(layout) // 2
            assert off + _seg_rows(*s) <= 2 * half
            assert layout not in _GRAD_ROWS or half % rt == 0 or off + _seg_rows(*s) <= half
            return dict(L=L, R=R, C=C, rpad=rpad, nc=-(-C // FLAT_W), rt=rt, off=off, ni=rpad // rt, half=half)
        off += _seg_rows(*s)
    raise KeyError(name)


def _flat_block(sg, term=0):
    base = (sg["off"] + term * sg["L"] * sg["nc"] * sg["rpad"]) // sg["rt"]
    return lambda l, j, i: base + (l * sg["nc"] + j) * sg["ni"] + i


def _native3(t, name):
    sg = _seg(name)
    t = t.reshape(sg["L"], sg["R"], sg["C"])
    return jnp.pad(t, ((0, 0), (0, sg["rpad"] - sg["R"]), (0, 0))) if sg["rpad"] != sg["R"] else t


def _slab_pack(flat, t, name, layout, term=None):
    sg = _seg(name, layout)
    rt = sg["rt"]
    rb = _flat_block(sg, term or 0)

    def body(*refs):
        t_ref, o_ref = refs[-2], refs[-1]
        val = t_ref[...]
        o_ref[...] = val.astype(BF16) if term is None else _split3(val)[term]

    in_specs = [pl.BlockSpec((None, rt, FLAT_W), lambda l, j, i: (l, i, j))]
    args = [t]
    if flat is not None:
        in_specs, args = [ANY] + in_specs, [flat] + args
    return pl.pallas_call(
        body, out_shape=jax.ShapeDtypeStruct((_layout_rows(layout), FLAT_W), BF16), grid=(sg["L"], sg["nc"], sg["ni"]),
        in_specs=in_specs, out_specs=pl.BlockSpec((rt, FLAT_W), lambda l, j, i: (rb(l, j, i), 0)),
        input_output_aliases={0: 0} if flat is not None else {},
        compiler_params=_cparams(("parallel", "parallel", "parallel")), name=f"pack_{name}_{term or 0}",
    )(*args)


def _full_spec(sg, name):
    rt, nc, ni = sg["rt"], sg["nc"], sg["ni"]
    if name in _ROW_SHARDED:
        return (sg["L"], N_CHIPS * sg["R"], sg["C"]), pl.BlockSpec((None, rt, FLAT_W), lambda k, l, j, i: (l, k * ni + i, j))
    return ((sg["L"], sg["rpad"], N_CHIPS * nc * FLAT_W),
            pl.BlockSpec((None, rt, FLAT_W), lambda k, l, j, i: (l, i, k * nc + j)))


def _slab_unpack(gathered, slots, name, layout, own=None):
    sg = _seg(name, layout)
    rb = _flat_block(sg)
    shape, _ = _full_spec(sg, name)
    rt, nc, ni = sg["rt"], sg["nc"], sg["ni"]
    width = nc * FLAT_W
    last = gathered.shape[0] - 1

    def body(*refs):
        s_ref, o_ref = refs[0], refs[-1]
        is_own = s_ref[pl.program_id(0)] == OWN_SLOT
        for j in range(nc):
            val = refs[1 + j][...]
            if own is not None:
                val = jnp.where(is_own, refs[1 + nc + j][...], val)
            o_ref[:, j * FLAT_W:(j + 1) * FLAT_W] = val

    if name in _ROW_SHARDED:
        o_spec = pl.BlockSpec((None, rt, width), lambda k, l, i, s: (l, k * ni + i, 0))
    else:
        o_spec = pl.BlockSpec((None, rt, width), lambda k, l, i, s: (l, i, k))
    in_specs = [pl.BlockSpec((None, rt, FLAT_W), lambda k, l, i, s, j=j: (jnp.minimum(s[k], last), rb(l, j, i), 0))
                for j in range(nc)]
    args = [gathered] * nc
    if own is not None:
        in_specs += [pl.BlockSpec((rt, FLAT_W), lambda k, l, i, s, j=j: (rb(l, j, i), 0)) for j in range(nc)]
        args += [own] * nc
    return pl.pallas_call(
        body, out_shape=jax.ShapeDtypeStruct(shape, BF16),
        grid_spec=pltpu.PrefetchScalarGridSpec(num_scalar_prefetch=1, grid=(N_CHIPS, sg["L"], ni), in_specs=in_specs,
                                               out_specs=o_spec),
        compiler_params=_cparams(("parallel",) * 3), name=f"unpack_{name}",
    )(slots, *args)


def _slab_pack_grad(flat4, g, name, layer=None):
    sg = _seg(name)
    rows = _layout_rows(_grad_layout(name))
    shape, _ = _full_spec(sg, name)
    n_layers = sg["L"] if layer is None else 1
    assert g.shape == (n_layers,) + shape[1:], (name, g.shape, shape)
    rt, nc = sg["rt"], sg["nc"]
    assert sg["ni"] == 1 and sg["off"] % (nc * rt) == 0
    base = sg["off"] // (nc * rt) + (layer or 0)

    def body(*refs):
        g_ref, o_ref = refs[-2], refs[-1]
        for j in range(nc):
            o_ref[j * rt:(j + 1) * rt, :] = g_ref[:, j * FLAT_W:(j + 1) * FLAT_W]

    if name in _ROW_SHARDED:
        spec = pl.BlockSpec((None, rt, nc * FLAT_W), lambda k, l: (l, k, 0))
    else:
        spec = pl.BlockSpec((None, rt, nc * FLAT_W), lambda k, l: (l, 0, k))
    in_specs, args = [spec], [g]
    if flat4 is not None:
        in_specs, args = [pl.BlockSpec(memory_space=pl.ANY)] + in_specs, [flat4] + args
    return pl.pallas_call(
        body, out_shape=jax.ShapeDtypeStruct((N_CHIPS, rows, FLAT_W), F32), grid=(N_CHIPS, n_layers),
        in_specs=in_specs, out_specs=pl.BlockSpec((None, nc * rt, FLAT_W), lambda k, l: (k, base + l, 0)),
        input_output_aliases={0: 0} if flat4 is not None else {},
        compiler_params=_cparams(("parallel",) * 2), name=f"packgrad_{name}_{layer or 0}",
    )(*args)


def _adamw_shard(w, m, v, g_mine, g_other, c_arr, name):
    sg = _seg(name)
    rt = sg["rt"]
    rb = _flat_block(sg)
    per_half = sg["half"] // rt

    def half_of(l, j, i):
        return (rb(l, j, i) * rt) // sg["half"]

    def body(c_ref, w_ref, m_ref, v_ref, gm_ref, go_ref, g_ref, d_ref, mo_ref, vo_ref):
        is_mine = half_of(pl.program_id(0), pl.program_id(1), pl.program_id(2)) == c_ref[0]
        gv = jnp.where(is_mine, gm_ref[...], go_ref[...])
        g_ref[...] = gv
        d_ref[...], mo_ref[...], vo_ref[...] = _adam_update(w_ref[...], gv, m_ref[...], v_ref[...])

    nat = pl.BlockSpec((None, rt, FLAT_W), lambda l, j, i, c: (l, i, j))
    half = pl.BlockSpec((rt, FLAT_W), lambda l, j, i, c: (rb(l, j, i) - half_of(l, j, i) * per_half, 0))
    sds = jax.ShapeDtypeStruct(w.shape, F32)
    return pl.pallas_call(
        body, out_shape=[sds] * 4,
        grid_spec=pltpu.PrefetchScalarGridSpec(num_scalar_prefetch=1, grid=(sg["L"], sg["nc"], sg["ni"]),
                                               in_specs=[nat, nat, nat, half, half], out_specs=[nat] * 4),
        compiler_params=_cparams(("parallel", "parallel", "parallel")), name=f"adamw_{name}",
    )(c_arr, w, m, v, g_mine, g_other)


def _pack_small(vals, loss=None):
    parts = [vals[name].astype(F32).reshape(-1) for name, _ in _SMALL]
    if loss is not None:
        parts.append(loss.reshape(1))
    flat = jnp.concatenate(parts)
    return jnp.pad(flat, (0, SMALL_ROWS * 1024 - flat.shape[0])).reshape(SMALL_ROWS, 1024)


def _unpack_small(flat):
    flat = flat.reshape(-1)
    out = {}
    o = 0
    for name, shape in _SMALL:
        n = int(np.prod(shape))
        out[name] = flat[o:o + n].reshape(shape)
        o += n
    return out, flat[o]


_BIG = ("a_w_in", "a_w_out", "b_w_q", "b_w_out", "w_kv", "ffn_w_up", "ffn_w_down", "ffn_conv_w")
A_IN_PAD = 896


def _pack_weights(w, layout):
    flat = None
    for name in _LAYOUTS[layout]:
        t = _native3(w[name], name)
        for term in ((0, 1, 2) if name == "ffn_conv_w" else (None,)):
            flat = _slab_pack(flat, t, name, layout, term)
    return flat


def _early_weights(gathered, slots):
    a_in = _slab_unpack(gathered, slots, "a_w_in", "early")
    a_in = a_in.reshape(D, N_CHIPS, A_IN_PAD)[:, :, :772].reshape(D, N_CHIPS * 772)
    return dict(a_w_in=a_in, a_w_out=_slab_unpack(gathered, slots, "a_w_out", "early")[0])


def _late_weights(landed, slots, own):
    full = {name: _slab_unpack(landed, slots, name, "late", own) for name in _LAYOUTS["late"] if name != "ffn_conv_w"}
    sg = _seg("ffn_conv_w", "late")
    n1 = sg["nc"] * sg["rpad"]
    conv = slice(sg["off"], sg["off"] + CONV_TERMS * n1)
    conv_rows = jnp.concatenate([landed[:, conv], own[None, conv]], axis=0)
    per_chip = []
    for k in range(N_CHIPS):
        rows = lax.dynamic_index_in_dim(conv_rows, slots[k], axis=0, keepdims=False)
        terms = [_unslabs(rows[i * n1:(i + 1) * n1], 1, sg["R"], sg["C"], sg["rpad"]).astype(F32) for i in range(CONV_TERMS)]
        per_chip.append((terms[0] + terms[1]) + terms[2])
    cw = jnp.concatenate(per_chip, axis=2).reshape(2, 3, 2, FF).transpose(0, 2, 1, 3)
    return dict(b_w_q=full["b_w_q"][0], b_w_out=full["b_w_out"][0], w_kv=full["w_kv"][0], ffn_w_up=full["ffn_w_up"],
                ffn_w_down=full["ffn_w_down"], conv_w=cw)


def _shard_grads(g, layout):
    def full(name):
        if name == "a_w_in":
            a_in = jnp.pad(g[name].reshape(D, N_CHIPS, 772), ((0, 0), (0, 0), (0, A_IN_PAD - 772)))
            return a_in.reshape(1, D, N_CHIPS * A_IN_PAD)
        if name == "ffn_conv_w":
            sgc = _seg(name)
            return jnp.pad(g[name].reshape(1, sgc["R"], 2 * FF), ((0, 0), (0, sgc["rpad"] - sgc["R"]), (0, 0)))
        return g[name] if g[name].ndim == 3 else g[name][None]

    flat4 = None
    for name in _LAYOUTS[layout]:
        if isinstance(g[name], (list, tuple)):
            for layer, t in enumerate(g[name]):
                flat4 = _slab_pack_grad(flat4, t[None], name, layer)
        else:
            flat4 = _slab_pack_grad(flat4, full(name), name)
    return flat4


_WEIGHTS = ["a_w_in", "a_b_f", "a_w_out", "b_w_q", "b_w_out", "kv_norm_g", "w_kv", "mix_norm_g", "ffn_norm_g", "ffn_w_up",
            "ffn_conv_w", "ffn_conv_b", "ffn_w_down", "final_norm_g"]


def kernel(x, a_w_in, a_b_f, a_w_out, b_w_q, b_w_out, kv_norm_g, w_kv, mix_norm_g, ffn_norm_g, ffn_w_up, ffn_conv_w, ffn_conv_b, ffn_w_down, final_norm_g, loss_target, m_a_w_in, m_a_b_f, m_a_w_out, m_b_w_q, m_b_w_out, m_kv_norm_g, m_w_kv, m_mix_norm_g, m_ffn_norm_g, m_ffn_w_up, m_ffn_conv_w, m_ffn_conv_b, m_ffn_w_down, m_final_norm_g, v_a_w_in, v_a_b_f, v_a_w_out, v_b_w_q, v_b_w_out, v_kv_norm_g, v_w_kv, v_mix_norm_g, v_ffn_norm_g, v_ffn_w_up, v_ffn_conv_w, v_ffn_conv_b, v_ffn_w_down, v_final_norm_g):
    w = dict(a_w_in=a_w_in, a_b_f=a_b_f, a_w_out=a_w_out, b_w_q=b_w_q, b_w_out=b_w_out, kv_norm_g=kv_norm_g, w_kv=w_kv,
             mix_norm_g=mix_norm_g, ffn_norm_g=ffn_norm_g, ffn_w_up=ffn_w_up, ffn_conv_w=ffn_conv_w, ffn_conv_b=ffn_conv_b,
             ffn_w_down=ffn_w_down, final_norm_g=final_norm_g)
    m = dict(a_w_in=m_a_w_in, a_b_f=m_a_b_f, a_w_out=m_a_w_out, b_w_q=m_b_w_q, b_w_out=m_b_w_out, kv_norm_g=m_kv_norm_g,
             w_kv=m_w_kv, mix_norm_g=m_mix_norm_g, ffn_norm_g=m_ffn_norm_g, ffn_w_up=m_ffn_w_up, ffn_conv_w=m_ffn_conv_w,
             ffn_conv_b=m_ffn_conv_b, ffn_w_down=m_ffn_w_down, final_norm_g=m_final_norm_g)
    v = dict(a_w_in=v_a_w_in, a_b_f=v_a_b_f, a_w_out=v_a_w_out, b_w_q=v_b_w_q, b_w_out=v_b_w_out, kv_norm_g=v_kv_norm_g,
             w_kv=v_w_kv, mix_norm_g=v_mix_norm_g, ffn_norm_g=v_ffn_norm_g, ffn_w_up=v_ffn_w_up, ffn_conv_w=v_ffn_conv_w,
             ffn_conv_b=v_ffn_conv_b, ffn_w_down=v_ffn_w_down, final_norm_g=v_final_norm_g)

    c_arr = lax.axis_index("c").astype(jnp.int32).reshape(1)
    k_arr = (2 * lax.axis_index("x") + lax.axis_index("y")).astype(jnp.int32).reshape(1)
    xi, yi = lax.axis_index("x"), lax.axis_index("y")
    late_slots = jnp.stack([jnp.where(k == k_arr[0], OWN_SLOT, 2 * ((k & 1) ^ yi) + ((k >> 1) ^ xi) - 1)
                            for k in range(N_CHIPS)]).astype(jnp.int32)
    w_late = _pack_weights(w, "late")
    land = lax.empty((OWN_SLOT,) + w_late.shape, BF16)
    send_sems, recv_sems, w_thru, land_thru, token = _late_gather_start(w_late, land)
    w_early = _pack_weights(w, "early")
    early = _allgather_shards(w_early, _place_own(w_early, k_arr, "early_place_own"))
    p = _early_weights(early, jnp.arange(N_CHIPS, dtype=jnp.int32))
    cb = ffn_conv_b.reshape(2, 2, 1, FF)
    p.update(a_b_f=a_b_f, kv_norm_g=kv_norm_g, mix_norm_g=mix_norm_g + token[0, 0], ffn_norm_g=ffn_norm_g,
             final_norm_g=final_norm_g, conv_b=cb)

    def late_weights(after):
        own, landed = _late_gather_wait(send_sems, recv_sems, w_thru, land_thru, after)
        return _late_weights(landed, late_slots, own)

    started = {}

    def late_grads_ready(g_so_far):
        gflat = _shard_grads(g_so_far, "grad_late")
        land = lax.empty((N_CHIPS, gflat.shape[1] // 2, FLAT_W), F32)
        *handles, token = _sibling_swap_start(gflat, land)
        started["swap"] = handles
        return token

    def late_grads_continue(after):
        gflat, other = _sibling_swap_wait(*started["swap"], after)
        pair = _pair_sum(gflat, other, c_arr, "late")
        land = lax.empty((3,) + pair.shape[1:], BF16)
        *handles, token = _chip_exchange_start(pair, land, "late")
        started["handles"] = handles
        return token

    loss_part, grad_x, g = _local_step(x[0], loss_target[0], p, late_weights, late_grads_ready, late_grads_continue)

    gflat = _shard_grads(g, "grad_early")
    pair_e = _pair_sum(gflat, _sibling_swap_half(gflat, "early"), c_arr, "early")
    *early_handles, token = _chip_exchange_start(pair_e, lax.empty((3,) + pair_e.shape[1:], BF16), "early")

    big = [{}, {}, {}, {}]

    def adamw_group(layout, g_mine):
        g_other = _sibling_send(g_mine, layout)
        for name in _LAYOUTS[layout]:
            sg = _seg(name)
            res = _adamw_shard(_native3(w[name], name), _native3(m[name], name), _native3(v[name], name), g_mine,
                               g_other, c_arr, name)
            for store, t in zip(big, res):
                store[name] = t[:, :sg["R"], :].reshape(_SHARD_SHAPES[name])
        return res[1]

    pair, landed = _chip_exchange_wait(*started["handles"], token, "late")
    last = adamw_group("grad_late", _chip_sum(pair, landed, k_arr, "late"))
    small, loss = _unpack_small(_allreduce_small(_pack_small(g, loss_part)))
    dws, mns, vns = _adamw(_pack_small(w), _pack_small(small), _pack_small(m), _pack_small(v), "adamw_small")
    pair_e, landed_e = _chip_exchange_wait(*early_handles, last, "early")
    adamw_group("grad_early", _chip_sum(pair_e, landed_e, k_arr, "early"))
    sml = [small] + [_unpack_small(t)[0] for t in (dws, mns, vns)]
    outs = [loss, grad_x[None]]
    for b, s in zip(big, sml):
        outs += [b[n] if n in b else s[n] for n in _WEIGHTS]
    return tuple(outs)
```

```python
import numpy as np
import jax
import jax.numpy as jnp
from jax import lax
from jax.experimental import pallas as pl
from jax.experimental.pallas import tpu as pltpu

F32 = jnp.float32
BF16 = jnp.bfloat16
MESH = pl.DeviceIdType.MESH

S = 4096
D = 1024
A_HEADS = 16
HEAD_DIM = 64
QKV_W = 3 * A_HEADS * HEAD_DIM
B_GROUPS = ((128, 1), (512, 4), (2048, 16))
B_HPG = 8
B_Q_W = 3 * B_HPG * HEAD_DIM
B_OUT_W = B_HPG * HEAD_DIM
B_WIN = 128
FF = 2816
RMS_EPS = 1e-6
SCALE = HEAD_DIM ** -0.5
N_CHIPS = 4

ADAM_LR, ADAM_B1, ADAM_B2, ADAM_EPS, ADAM_WD, ADAM_STEP = 0.001, 0.9, 0.999, 1e-08, 0.01, 10

V7X_VMEM_LIMIT = 48 * 1024 * 1024
LANES = 128
NEG_INF = float("-inf")

FLAT_W = LANES
_SEGS = (("ffn_w_down", 2, 704, 1024, 704), ("a_w_in", 1, 1024, 772, 1024), ("a_w_out", 1, 256, 1024, 256),
         ("b_w_q", 1, 1024, 384, 1024), ("b_w_out", 1, 512, 256, 512), ("w_kv", 1, 1024, 768, 1024),
         ("ffn_w_up", 2, 1024, 1408, 1024), ("ffn_conv_w", 1, 6, 1408, 16))
CONV_TERMS = 3


def _seg_rows(name, L, R, C, rpad):
    return (CONV_TERMS if name == "ffn_conv_w" else 1) * L * (-(-C // FLAT_W)) * rpad


SMALL_ROWS = 24


def _cparams(sem=None, **kw):
    return pltpu.CompilerParams(dimension_semantics=sem, vmem_limit_bytes=V7X_VMEM_LIMIT, **kw)


_DN = {"nn": (((1,), (0,)), ((), ())), "nt": (((1,), (1,)), ((), ())), "tn": (((0,), (0,)), ((), ()))}


def _mm(a, b, *, mode, tm, tn, tk, name, out_dtype=F32, res=None, a_split=0, b_split=0, o_split=0):
    if mode == "tn":
        K = a.shape[0]
        M = a.shape[1]
    else:
        M = a.shape[-2]
        K = a.shape[-1] * (2 if a_split else 1)
    if mode == "nt":
        N = b.shape[0]
    else:
        N = b.shape[-1] * (2 if b_split else 1)
    assert M % tm == 0 and N % tn == 0 and K % tk == 0, (name, M, N, K, tm, tn, tk)
    nk = K // tk

    if mode == "tn":
        a_spec = pl.BlockSpec((tk, tm), lambda i, j, k: (k, i))
    elif a_split:
        a_spec = pl.BlockSpec((None, tm, tk), lambda i, j, k: (k // a_split, i, k % a_split))
    else:
        a_spec = pl.BlockSpec((tm, tk), lambda i, j, k: (i, k))
    if mode == "nt":
        b_spec = pl.BlockSpec((tn, tk), lambda i, j, k: (j, k))
    elif b_split:
        b_spec = pl.BlockSpec((None, tk, tn), lambda i, j, k: (j // b_split, k, j % b_split))
    else:
        b_spec = pl.BlockSpec((tk, tn), lambda i, j, k: (k, j))
    if o_split:
        o_spec = pl.BlockSpec((None, tm, tn), lambda i, j, k: (j // o_split, i, j % o_split))
        out_shape = jax.ShapeDtypeStruct((2, M, N // 2), out_dtype)
    else:
        o_spec = pl.BlockSpec((tm, tn), lambda i, j, k: (i, j))
        out_shape = jax.ShapeDtypeStruct((M, N), out_dtype)
    in_specs = [a_spec, b_spec]
    args = [a, b]
    if res is not None:
        in_specs.append(pl.BlockSpec((tm, tn), lambda i, j, k: (i, j)))
        args.append(res)

    def body(*refs):
        if res is not None:
            a_ref, b_ref, r_ref, o_ref = refs[:4]
        else:
            a_ref, b_ref, o_ref = refs[:3]
            r_ref = None
        p = lax.dot_general(a_ref[...].astype(BF16), b_ref[...].astype(BF16), _DN[mode], preferred_element_type=F32)

        def finish(r):
            if r_ref is not None:
                r = r + r_ref[...]
            o_ref[...] = r.astype(out_dtype)

        if nk == 1:
            finish(p)
        else:
            acc = refs[-1]
            k = pl.program_id(2)

            @pl.when(k == 0)
            def _():
                acc[...] = p

            @pl.when(k > 0)
            def _():
                acc[...] += p

            @pl.when(k == nk - 1)
            def _():
                finish(acc[...])

    return pl.pallas_call(
        body, out_shape=out_shape, grid=(M // tm, N // tn, nk), in_specs=in_specs, out_specs=o_spec,
        scratch_shapes=[pltpu.VMEM((tm, tn), F32)] if nk > 1 else [],
        compiler_params=_cparams(("parallel", "parallel", "arbitrary")), name=name,
    )(*args)


NORM_ROWS = 256


def _rms_fwd(x, gains, name):
    n = len(gains)

    def body(x_ref, *refs):
        xv = x_ref[...]
        y = xv * lax.rsqrt(jnp.mean(xv * xv, axis=-1, keepdims=True) + RMS_EPS)
        for g_ref, o_ref in zip(refs[:n], refs[n:]):
            o_ref[...] = (y * g_ref[...]).astype(BF16)

    row = pl.BlockSpec((NORM_ROWS, D), lambda i: (i, 0))
    return pl.pallas_call(
        body, out_shape=[jax.ShapeDtypeStruct((S, D), BF16)] * n, grid=(S // NORM_ROWS,),
        in_specs=[row] + [pl.BlockSpec((1, D), lambda i: (0, 0))] * n, out_specs=[row] * n,
        compiler_params=_cparams(("parallel",)), name=name,
    )(x, *[g.reshape(1, D) for g in gains])


def _rms_bwd(x, dres, pairs, name):
    n = len(pairs)

    def body(*refs):
        x_ref, dres_ref = refs[0], refs[1]
        g_refs = refs[2:2 + 2 * n:2]
        dh_refs = refs[3:3 + 2 * n:2]
        dx_ref, dxb_ref = refs[2 + 2 * n], refs[3 + 2 * n]
        dg_refs = refs[4 + 2 * n:]
        i = pl.program_id(0)
        xv = x_ref[...]
        r = lax.rsqrt(jnp.mean(xv * xv, axis=-1, keepdims=True) + RMS_EPS)
        y = xv * r
        dx = dres_ref[...]
        for g_ref, dh_ref, dg_ref in zip(g_refs, dh_refs, dg_refs):
            dh = dh_ref[...]
            dy = dh * g_ref[...]
            dx = dx + r * (dy - y * jnp.mean(dy * y, axis=-1, keepdims=True))
            part = jnp.sum(dh * y, axis=0, keepdims=True)

            @pl.when(i == 0)
            def _():
                dg_ref[...] = part

            @pl.when(i > 0)
            def _():
                dg_ref[...] += part

        dx_ref[...] = dx
        dxb_ref[...] = dx.astype(BF16)

    row = pl.BlockSpec((NORM_ROWS, D), lambda i: (i, 0))
    vec = pl.BlockSpec((1, D), lambda i: (0, 0))
    in_specs = [row, row]
    args = [x, dres]
    for g, dh in pairs:
        in_specs += [vec, row]
        args += [g.reshape(1, D), dh]
    outs = pl.pallas_call(
        body,
        out_shape=[jax.ShapeDtypeStruct((S, D), F32), jax.ShapeDtypeStruct((S, D), BF16)]
        + [jax.ShapeDtypeStruct((1, D), F32)] * n,
        grid=(S // NORM_ROWS,), in_specs=in_specs, out_specs=[row, row] + [vec] * n,
        compiler_params=_cparams(("arbitrary",)), name=name,
    )(*args)
    return outs[0], outs[1], list(outs[2:])


def _loss_head(x, g, target, name):
    def body(x_ref, g_ref, t_ref, loss_ref, dx_ref, dxb_ref, dg_ref):
        i = pl.program_id(0)
        xv = x_ref[...]
        gv = g_ref[...]
        r = lax.rsqrt(jnp.mean(xv * xv, axis=-1, keepdims=True) + RMS_EPS)
        y = xv * r
        err = y * gv - t_ref[...]
        lpart = jnp.broadcast_to(jnp.sum(err * err, keepdims=True) * (0.5 / D), (1, LANES))
        dh = err * (1.0 / D)
        dy = dh * gv
        dx = r * (dy - y * jnp.mean(dy * y, axis=-1, keepdims=True))
        part = jnp.sum(dh * y, axis=0, keepdims=True)

        @pl.when(i == 0)
        def _():
            dg_ref[...] = part
            loss_ref[...] = lpart

        @pl.when(i > 0)
        def _():
            dg_ref[...] += part
            loss_ref[...] += lpart

        dx_ref[...] = dx
        dxb_ref[...] = dx.astype(BF16)

    row = pl.BlockSpec((NORM_ROWS, D), lambda i: (i, 0))
    vec = pl.BlockSpec((1, D), lambda i: (0, 0))
    return pl.pallas_call(
        body,
        out_shape=[jax.ShapeDtypeStruct((1, LANES), F32), jax.ShapeDtypeStruct((S, D), F32),
                   jax.ShapeDtypeStruct((S, D), BF16), jax.ShapeDtypeStruct((1, D), F32)],
        grid=(S // NORM_ROWS,), in_specs=[row, vec, row],
        out_specs=[pl.BlockSpec((1, LANES), lambda i: (0, 0)), row, row, vec],
        compiler_params=_cparams(("arbitrary",)), name=name,
    )(x, g.reshape(1, D), target)


SCAN_ROWS = 256


def _split3(v):
    hi = v.astype(BF16)
    r1 = v - hi.astype(F32)
    mid = r1.astype(BF16)
    lo = (r1 - mid.astype(F32)).astype(BF16)
    return hi, mid, lo


def _tri_dot(tri, v):
    hi, mid, lo = _split3(v)
    dn = _DN["nn"]
    return (lax.dot_general(tri, hi, dn, preferred_element_type=F32)
            + lax.dot_general(tri, mid, dn, preferred_element_type=F32)
            + lax.dot_general(tri, lo, dn, preferred_element_type=F32))


def _log_sigmoid(z):
    return jnp.minimum(z, 0.0) - jnp.log(1.0 + jnp.exp(-jnp.abs(z)))


GATE_LANES = 6


def _gate_lane_tables():
    pq = np.zeros((3 * LANES, A_HEADS * HEAD_DIM), np.float32)
    pk = np.zeros((3 * LANES, A_HEADS * HEAD_DIM), np.float32)
    one_q = np.zeros((1, A_HEADS * HEAD_DIM), np.float32)
    one_k = np.zeros((1, A_HEADS * HEAD_DIM), np.float32)
    for h in range(A_HEADS):
        pos = (h // 2) * LANES + (HEAD_DIM if h % 2 == 0 else 0)
        for term in range(3):
            pq[term * LANES + h, pos + term] = 1.0
            pk[term * LANES + h, pos + 3 + term] = -1.0
        one_q[0, pos + 3:pos + GATE_LANES] = 1.0
        one_k[0, pos:pos + 3] = 1.0
    return jnp.asarray(pq, BF16), jnp.asarray(pk, BF16), jnp.asarray(one_q), jnp.asarray(one_k)


def _fgate_fwd(pf, bias, name):
    tri = jnp.tril(jnp.ones((SCAN_ROWS, SCAN_ROWS), F32)).astype(BF16)
    pq, pk, one_q, one_k = _gate_lane_tables()

    def body(pf_ref, b_ref, tri_ref, pq_ref, pk_ref, oq_ref, ok_ref, aq_ref, ak_ref, c_sc):
        carry = jnp.zeros((1, LANES), F32)
        for blk in range(S // SCAN_ROWS):
            rows = pl.ds(blk * SCAN_ROWS, SCAN_ROWS)
            lf = _log_sigmoid(pf_ref[rows, :] + b_ref[...])
            c_sc[...] = _tri_dot(tri_ref[...], lf) + carry
            carry = c_sc[pl.ds(SCAN_ROWS - 1, 1), :]
            terms = jnp.concatenate(_split3(c_sc[...]), axis=1)
            aq = lax.dot_general(terms, pq_ref[...], _DN["nn"], preferred_element_type=F32) + oq_ref[...]
            ak = lax.dot_general(terms, pk_ref[...], _DN["nn"], preferred_element_type=F32) + ok_ref[...]
            aq_ref[rows, :] = aq.astype(BF16)
            ak_ref[rows, :] = ak.astype(BF16)

    wide = jax.ShapeDtypeStruct((S, A_HEADS * HEAD_DIM), BF16)
    return pl.pallas_call(
        body, out_shape=[wide, wide], scratch_shapes=[pltpu.VMEM((SCAN_ROWS, LANES), F32)],
        compiler_params=_cparams(), name=name,
    )(pf, bias, tri, pq, pk, one_q, one_k)


def _fgate_bwd(pf, bias, dc_key, dc_query, name):
    triu = jnp.triu(jnp.ones((SCAN_ROWS, SCAN_ROWS), F32)).astype(BF16)

    def body(pf_ref, b_ref, dck_ref, dcq_ref, tri_ref, dpf_ref, db_ref, dlf_ref):
        carry = jnp.zeros((1, LANES), F32)
        db = jnp.zeros((1, LANES), F32)
        lane = lax.broadcasted_iota(jnp.int32, (SCAN_ROWS, LANES), 1)
        for blk in reversed(range(S // SCAN_ROWS)):
            rows = pl.ds(blk * SCAN_ROWS, SCAN_ROWS)
            dc = dck_ref[rows, :] + dcq_ref[rows, :]
            dlf_ref[rows, :] = _tri_dot(tri_ref[...], dc) + carry
            carry = dlf_ref[pl.ds(blk * SCAN_ROWS, 1), :]
            z = pf_ref[rows, :] + b_ref[...]
            e = jnp.exp(-jnp.abs(z))
            sig_neg = jnp.where(z >= 0.0, e, 1.0) / (1.0 + e)
            dz = jnp.where(lane < A_HEADS, dlf_ref[rows, :] * sig_neg, 0.0)
            dpf_ref[rows, :] = dz.astype(BF16)
            db = db + jnp.sum(dz, axis=0, keepdims=True)
        db_ref[...] = db

    return pl.pallas_call(
        body, out_shape=[jax.ShapeDtypeStruct((S, LANES), BF16), jax.ShapeDtypeStruct((1, LANES), F32)],
        scratch_shapes=[pltpu.VMEM((S, LANES), F32)],
        compiler_params=_cparams(), name=name,
    )(pf, bias, dc_key, dc_query, triu)


FOX_T = 512


def _first_head(shape):
    return lax.broadcasted_iota(jnp.int32, shape, len(shape) - 1) < HEAD_DIM


def _each_head(x, lo):
    zero = jnp.zeros_like(x)
    return jnp.where(lo, x, zero), jnp.where(lo, zero, x)


def _fox_pair_fwd(qkv, aug_q, aug_k, name):
    T = FOX_T
    nq = S // T
    NP = A_HEADS // 2

    def body(q_ref, k_ref, v_ref, aq_ref, ak_ref, o_ref, lse_ref, m_sc, l_sc, acc_sc):
        i = pl.program_id(1)
        j = pl.program_id(2)
        lo = _first_head((T, LANES))

        @pl.when(j == 0)
        def _():
            m_sc[...] = jnp.full((2, T, LANES), NEG_INF, F32)
            l_sc[...] = jnp.zeros((2, T, LANES), F32)
            acc_sc[...] = jnp.zeros((T, LANES), F32)

        def step(diagonal):
            qs = q_ref[...] * jnp.asarray(SCALE, BF16)
            aq, ak, kv = aq_ref[...], ak_ref[...], k_ref[...]
            q2 = (jnp.where(lo, qs, aq), jnp.where(lo, aq, qs))
            k2 = (jnp.where(lo, kv, ak), jnp.where(lo, ak, kv))
            if diagonal:
                causal = lax.broadcasted_iota(jnp.int32, (T, T), 0) >= lax.broadcasted_iota(jnp.int32, (T, T), 1)
            pv, alphas = None, []
            for h, vh in enumerate(_each_head(v_ref[...], lo)):
                s = lax.dot_general(q2[h], k2[h], _DN["nt"], preferred_element_type=F32)
                if diagonal:
                    s = jnp.where(causal, s, NEG_INF)
                m_prev = m_sc[h]
                m_new = jnp.maximum(m_prev, jnp.max(s, axis=1, keepdims=True))
                alpha = jnp.exp(m_prev - m_new)
                p = jnp.exp(s - jnp.tile(m_new, (1, T // LANES)))
                l_sc[h] = alpha * l_sc[h] + jnp.sum(p, axis=1, keepdims=True)
                m_sc[h] = m_new
                d = lax.dot_general(p.astype(BF16), vh, _DN["nn"], preferred_element_type=F32)
                pv = d if pv is None else pv + d
                alphas.append(alpha)
            acc_sc[...] = jnp.where(lo, alphas[0], alphas[1]) * acc_sc[...] + pv

        @pl.when(j < i)
        def _():
            step(False)

        @pl.when(j == i)
        def _():
            step(True)
            o_ref[...] = (acc_sc[...] * jnp.where(lo, 1.0 / l_sc[0], 1.0 / l_sc[1])).astype(BF16)
            for h in range(2):
                lse_ref[h] = (m_sc[h] + jnp.log(l_sc[h]))[:, 0:1]

    qs_ = pl.BlockSpec((T, LANES), lambda p, i, j: (i, p))
    ks = pl.BlockSpec((T, LANES), lambda p, i, j: (jnp.minimum(i, j), NP + p))
    vs = pl.BlockSpec((T, LANES), lambda p, i, j: (jnp.minimum(i, j), 2 * NP + p))
    aks = pl.BlockSpec((T, LANES), lambda p, i, j: (jnp.minimum(i, j), p))
    col = pl.BlockSpec((2, T, 1), lambda p, i, j: (p, i, 0))
    return pl.pallas_call(
        body, out_shape=[jax.ShapeDtypeStruct((S, A_HEADS * HEAD_DIM), BF16), jax.ShapeDtypeStruct((A_HEADS, S, 1), F32)],
        grid=(NP, nq, nq), in_specs=[qs_, ks, vs, qs_, aks], out_specs=[qs_, col],
        scratch_shapes=[pltpu.VMEM((2, T, LANES), F32), pltpu.VMEM((2, T, LANES), F32), pltpu.VMEM((T, LANES), F32)],
        compiler_params=_cparams(("parallel", "parallel", "arbitrary")), name=name,
    )(qkv, qkv, qkv, aug_q, aug_k)


def _fox_pair_bwd(qkv, do, lse_row, delta_row, aug_q, aug_k, name):
    T = FOX_T
    nq = S // T
    NP = A_HEADS // 2

    def body(q_ref, k_ref, v_ref, do_ref, lse_ref, dl_ref, aq_ref, ak_ref, dq_ref, dk_ref, dv_ref, dc_ref, dcq_ref,
             dq_sc, dk_sc, dv_sc, dc_sc):
        j = pl.program_id(1)
        i = pl.program_id(2)
        lo = _first_head((T, LANES))

        @pl.when(jnp.logical_and(j == 0, i == 0))
        def _():
            dq_sc[...] = jnp.zeros((S, LANES), F32)
            dcq_ref[...] = jnp.zeros((2, nq, 1, T), F32)

        @pl.when(i == j)
        def _():
            dk_sc[...] = jnp.zeros((T, LANES), F32)
            dv_sc[...] = jnp.zeros((T, LANES), F32)
            dc_sc[...] = jnp.zeros((2, T, 1), F32)

        def step(diagonal):
            qv = q_ref[...]
            kv = k_ref[...]
            dov = do_ref[...].astype(BF16)
            qs = qv * jnp.asarray(SCALE, BF16)
            aq, ak = aq_ref[...], ak_ref[...]
            q2 = (jnp.where(lo, qs, aq), jnp.where(lo, aq, qs))
            k2 = (jnp.where(lo, kv, ak), jnp.where(lo, ak, kv))
            if diagonal:
                causal = lax.broadcasted_iota(jnp.int32, (T, T), 1) >= lax.broadcasted_iota(jnp.int32, (T, T), 0)
            dv = dk = dq = None
            for h, (kh, vh, qh, doh) in enumerate(zip(_each_head(kv, lo), _each_head(v_ref[...], lo),
                                                      _each_head(qv, lo), _each_head(dov, lo))):
                st = lax.dot_general(k2[h], q2[h], _DN["nt"], preferred_element_type=F32)
                if diagonal:
                    st = jnp.where(causal, st, NEG_INF)
                pt = jnp.exp(st - lse_ref[h])
                d = lax.dot_general(pt.astype(BF16), doh, _DN["nn"], preferred_element_type=F32)
                dv = d if dv is None else dv + d
                dpt = lax.dot_general(vh, dov, _DN["nt"], preferred_element_type=F32)
                dst = pt * (dpt - dl_ref[h])
                dc_sc[h] -= jnp.sum(dst, axis=1, keepdims=True)
                dcq_ref[h, i] += jnp.sum(dst, axis=0, keepdims=True)
                dsb = (dst * SCALE).astype(BF16)
                d = lax.dot_general(dsb, qh, _DN["nn"], preferred_element_type=F32)
                dk = d if dk is None else dk + d
                d = lax.dot_general(dsb, kh, _DN["tn"], preferred_element_type=F32)
                dq = d if dq is None else dq + d
            dv_sc[...] += dv
            dk_sc[...] += dk
            rows = pl.ds(pl.multiple_of(i * T, T), T)
            dq_sc[rows, :] += dq

        @pl.when(i > j)
        def _():
            step(False)

        @pl.when(i == j)
        def _():
            step(True)

        @pl.when(i == nq - 1)
        def _():
            dk_ref[...] = dk_sc[...].astype(BF16)
            dv_ref[...] = dv_sc[...].astype(BF16)
            dc_ref[...] = dc_sc[...]

        @pl.when(jnp.logical_and(j == nq - 1, i == nq - 1))
        def _():
            dq_ref[...] = dq_sc[...].astype(BF16)

    qs = pl.BlockSpec((T, LANES), lambda p, j, i: (jnp.maximum(i, j), p))
    qrow = pl.BlockSpec((2, 1, T), lambda p, j, i: (p, 0, jnp.maximum(i, j)))
    ks = pl.BlockSpec((T, LANES), lambda p, j, i: (j, NP + p))
    vs = pl.BlockSpec((T, LANES), lambda p, j, i: (j, 2 * NP + p))
    kout = pl.BlockSpec((T, LANES), lambda p, j, i: (j, p))
    kcol = pl.BlockSpec((2, T, 1), lambda p, j, i: (p, j, 0))
    dqs = pl.BlockSpec((S, LANES), lambda p, j, i: (0, p))
    dcqs = pl.BlockSpec((2, nq, 1, T), lambda p, j, i: (p, 0, 0, 0))
    wide = jax.ShapeDtypeStruct((S, A_HEADS * HEAD_DIM), BF16)
    return pl.pallas_call(
        body,
        out_shape=[wide, wide, wide, jax.ShapeDtypeStruct((A_HEADS, S, 1), F32),
                   jax.ShapeDtypeStruct((A_HEADS, nq, 1, T), F32)],
        grid=(NP, nq, nq), in_specs=[qs, ks, vs, qs, qrow, qrow, qs, kout], out_specs=[dqs, kout, kout, kcol, dcqs],
        scratch_shapes=[pltpu.VMEM((S, LANES), F32), pltpu.VMEM((T, LANES), F32), pltpu.VMEM((T, LANES), F32),
                        pltpu.VMEM((2, T, 1), F32)],
        compiler_params=_cparams(("parallel", "arbitrary", "arbitrary")), name=name,
    )(qkv, qkv, qkv, do, lse_row, delta_row, aug_q, aug_k)


def _pair_rowdot(a, b, name):
    n = a.shape[1] // HEAD_DIM
    T = 1024

    def body(a_ref, b_ref, o_ref):
        prod = a_ref[...].astype(F32) * b_ref[...].astype(F32)
        lo = _first_head(prod.shape)
        o_ref[0] = jnp.sum(jnp.where(lo, prod, 0.0), axis=1, keepdims=True)
        o_ref[1] = jnp.sum(jnp.where(lo, 0.0, prod), axis=1, keepdims=True)

    blk = pl.BlockSpec((T, LANES), lambda p, i: (i, p))
    return pl.pallas_call(
        body, out_shape=jax.ShapeDtypeStruct((n, S, 1), F32), grid=(n // 2, S // T), in_specs=[blk, blk],
        out_specs=pl.BlockSpec((2, T, 1), lambda p, i: (p, i, 0)),
        compiler_params=_cparams(("parallel", "parallel")), name=name,
    )(a, b)


W = B_WIN
N_HG = 3 * B_HPG
N_BLK = S // W


def _dil_tables():
    slopes = np.exp2((-8.0 * np.arange(1, N_HG + 1, dtype=np.float32) / N_HG).astype(np.float32)).astype(np.float32)
    dil = np.repeat(np.array([d for _, d in B_GROUPS], np.float32), B_HPG)
    coef = (slopes * dil).astype(np.float32)
    nbs = np.repeat(np.array([S // d // W for _, d in B_GROUPS], np.int32), B_HPG)
    return jnp.asarray(coef), jnp.asarray(nbs)


DIL_SUB = 8
DIL_ROWS = DIL_SUB * W
DIL_STEPS = S // DIL_ROWS


def _dil_bias(coef, transposed):
    row = lax.broadcasted_iota(jnp.int32, (W, 2 * W), 0)
    col = lax.broadcasted_iota(jnp.int32, (W, 2 * W), 1)
    dist = (col - row) if transposed else (row + W - col)
    valid = jnp.logical_and(dist >= 0, dist <= W)
    return jnp.where(valid, -coef * dist.astype(F32), NEG_INF), col


NPG = B_HPG // 2
GROUP_W = B_HPG * HEAD_DIM


def _dil_pair_specs(qoff, koff, voff):
    prev_blk = lambda n: jnp.maximum(n * DIL_SUB - 1, 0)
    next_blk = lambda n: jnp.minimum((n + 1) * DIL_SUB, N_BLK - 1)
    return dict(
        o=pl.BlockSpec((DIL_ROWS, LANES), lambda h, n: (n, h)),
        o_next=pl.BlockSpec((W, LANES), lambda h, n: (next_blk(n), h)),
        q=pl.BlockSpec((DIL_ROWS, LANES), lambda h, n: (n, qoff + h)),
        q_next=pl.BlockSpec((W, LANES), lambda h, n: (next_blk(n), qoff + h)),
        k=pl.BlockSpec((DIL_ROWS, LANES), lambda h, n: (n, koff + h)),
        k_prev=pl.BlockSpec((W, LANES), lambda h, n: (prev_blk(n), koff + h)),
        v=pl.BlockSpec((DIL_ROWS, LANES), lambda h, n: (n, voff + h)),
        v_prev=pl.BlockSpec((W, LANES), lambda h, n: (prev_blk(n), voff + h)),
        col=pl.BlockSpec((2, DIL_ROWS, 1), lambda h, n: (h, n, 0)),
        col2=pl.BlockSpec((2, DIL_ROWS, 1), lambda h, n: (NPG + h, n, 0)),
        row=pl.BlockSpec((2, 1, DIL_ROWS), lambda h, n: (h, 0, n)),
        row_next=pl.BlockSpec((2, 1, W), lambda h, n: (h, 0, next_blk(n))),
        row2=pl.BlockSpec((2, 1, DIL_ROWS), lambda h, n: (NPG + h, 0, n)),
        row2_next=pl.BlockSpec((2, 1, W), lambda h, n: (NPG + h, 0, next_blk(n))),
        smem=pl.BlockSpec(memory_space=pltpu.SMEM))


def _dil_pair_fwd(g, q, k, v, qoff, koff, voff, name):
    coef_t, nbs_t = _dil_tables()

    def body(coef_ref, nbs_ref, q_ref, kh_ref, k_ref, vh_ref, v_ref, o_ref, lse_ref, kf, vf):
        hp = pl.program_id(0)
        n = pl.program_id(1)
        nbs = nbs_ref[B_HPG * g + 2 * hp]
        kf[0:W, :] = kh_ref[...]
        kf[W:, :] = k_ref[...]
        vf[0:W, :] = vh_ref[...]
        vf[W:, :] = v_ref[...]
        biases = [_dil_bias(coef_ref[B_HPG * g + 2 * hp + h], False) for h in range(2)]
        col = biases[0][1]
        lo = _first_head((W, LANES))
        lo2 = _first_head((2 * W, LANES))
        for b in range(DIL_SUB):
            first = lax.rem(n * DIL_SUB + b, nbs) == 0
            rows = slice(b * W, (b + 1) * W)
            both = slice(b * W, (b + 2) * W)
            qv = q_ref[rows, :]
            acc, inv = None, []
            for h, (kh, vh) in enumerate(zip(_each_head(kf[both, :], lo2), _each_head(vf[both, :], lo2))):
                s = lax.dot_general(qv, kh, _DN["nt"], preferred_element_type=F32) * SCALE + biases[h][0]
                s = jnp.where(jnp.logical_and(first, col < W), NEG_INF, s)
                m = jnp.max(s, axis=1, keepdims=True)
                p = jnp.exp(s - m)
                l = jnp.sum(p, axis=1, keepdims=True)
                d = lax.dot_general(p.astype(BF16), vh, _DN["nn"], preferred_element_type=F32)
                acc = d if acc is None else acc + d
                inv.append(1.0 / l)
                lse_ref[h, rows, :] = m + jnp.log(l)
            o_ref[rows, :] = acc * jnp.where(lo, inv[0], inv[1])

    sp = _dil_pair_specs(qoff, koff, voff)
    return pl.pallas_call(
        body, out_shape=[jax.ShapeDtypeStruct((S, GROUP_W), F32), jax.ShapeDtypeStruct((B_HPG, S, 1), F32)],
        grid=(NPG, DIL_STEPS), in_specs=[sp["smem"], sp["smem"], sp["q"], sp["k_prev"], sp["k"], sp["v_prev"], sp["v"]],
        out_specs=[sp["o"], sp["col"]], scratch_shapes=[pltpu.VMEM((DIL_ROWS + W, LANES), BF16)] * 2,
        compiler_params=_cparams(("parallel", "parallel")), name=name,
    )(coef_t, nbs_t, q, k, k, v, v)


def _dil_pair_merge(os, lses, name):
    T = 1024

    def body(o0_ref, o1_ref, o2_ref, l0_ref, l1_ref, l2_ref, om_ref, omb_ref, l_ref):
        lo = _first_head((T, LANES))
        weights = []
        for h in range(2):
            l0, l1, l2 = l0_ref[h], l1_ref[h], l2_ref[h]
            m = jnp.maximum(jnp.maximum(l0, l1), l2)
            e0, e1, e2 = jnp.exp(l0 - m), jnp.exp(l1 - m), jnp.exp(l2 - m)
            den = e0 + e1 + e2
            weights.append((e0 / den, e1 / den, e2 / den))
            l_ref[h] = m + jnp.log(den)
        om = (jnp.where(lo, weights[0][0], weights[1][0]) * o0_ref[...]
              + jnp.where(lo, weights[0][1], weights[1][1]) * o1_ref[...]
              + jnp.where(lo, weights[0][2], weights[1][2]) * o2_ref[...])
        om_ref[...] = om
        omb_ref[...] = om.astype(BF16)

    ob = pl.BlockSpec((T, LANES), lambda p, i: (i, p))
    lb = pl.BlockSpec((2, T, 1), lambda p, i: (p, i, 0))
    return pl.pallas_call(
        body,
        out_shape=[jax.ShapeDtypeStruct((S, B_OUT_W), F32), jax.ShapeDtypeStruct((S, B_OUT_W), BF16),
                   jax.ShapeDtypeStruct((B_HPG, S, 1), F32)],
        grid=(NPG, S // T), in_specs=[ob] * 3 + [lb] * 3, out_specs=[ob, ob, lb],
        compiler_params=_cparams(("parallel", "parallel")), name=name,
    )(*os, *lses)


def _dil_pair_dq(g, q, k, v, qoff, koff, voff, do, stats, name):
    coef_t, nbs_t = _dil_tables()

    def body(coef_ref, nbs_ref, q_ref, kh_ref, k_ref, vh_ref, v_ref, do_ref, l_ref, d_ref, dq_ref, kf, vf):
        hp = pl.program_id(0)
        n = pl.program_id(1)
        nbs = nbs_ref[B_HPG * g + 2 * hp]
        kf[0:W, :] = kh_ref[...]
        kf[W:, :] = k_ref[...]
        vf[0:W, :] = vh_ref[...]
        vf[W:, :] = v_ref[...]
        biases = [_dil_bias(coef_ref[B_HPG * g + 2 * hp + h], False) for h in range(2)]
        col = biases[0][1]
        lo2 = _first_head((2 * W, LANES))
        for b in range(DIL_SUB):
            first = lax.rem(n * DIL_SUB + b, nbs) == 0
            rows = slice(b * W, (b + 1) * W)
            both = slice(b * W, (b + 2) * W)
            qv = q_ref[rows, :]
            dov = do_ref[rows, :]
            acc = None
            for h, (kh, vh) in enumerate(zip(_each_head(kf[both, :], lo2), _each_head(vf[both, :], lo2))):
                s = lax.dot_general(qv, kh, _DN["nt"], preferred_element_type=F32) * SCALE + biases[h][0]
                s = jnp.where(jnp.logical_and(first, col < W), NEG_INF, s)
                p = jnp.exp(s - l_ref[h, rows, :])
                dp = lax.dot_general(dov, vh, _DN["nt"], preferred_element_type=F32)
                ds = (p * (dp - d_ref[h, rows, :]) * SCALE).astype(BF16)
                d = lax.dot_general(ds, kh, _DN["nn"], preferred_element_type=F32)
                acc = d if acc is None else acc + d
            dq_ref[rows, :] = acc.astype(BF16)

    sp = _dil_pair_specs(qoff, koff, voff)
    return pl.pallas_call(
        body, out_shape=jax.ShapeDtypeStruct((S, GROUP_W), BF16), grid=(NPG, DIL_STEPS),
        in_specs=[sp["smem"], sp["smem"], sp["q"], sp["k_prev"], sp["k"], sp["v_prev"], sp["v"], sp["o"], sp["col"],
                  sp["col2"]],
        out_specs=sp["o"], scratch_shapes=[pltpu.VMEM((DIL_ROWS + W, LANES), BF16)] * 2,
        compiler_params=_cparams(("parallel", "parallel")), name=name,
    )(coef_t, nbs_t, q, k, k, v, v, do, stats, stats)


def _dil_pair_dkv(g, q, k, v, qoff, koff, voff, do, stats, name):
    coef_t, nbs_t = _dil_tables()

    def body(coef_ref, nbs_ref, k_ref, v_ref, q_ref, qn_ref, do_ref, don_ref, l_ref, ln_ref, d_ref, dn_ref,
             dk_ref, dv_ref, qf, dof, lf, df):
        hp = pl.program_id(0)
        n = pl.program_id(1)
        nbs = nbs_ref[B_HPG * g + 2 * hp]
        qf[0:DIL_ROWS, :] = q_ref[...]
        qf[DIL_ROWS:, :] = qn_ref[...]
        dof[0:DIL_ROWS, :] = do_ref[...]
        dof[DIL_ROWS:, :] = don_ref[...]
        lf[:, :, 0:DIL_ROWS] = l_ref[...]
        lf[:, :, DIL_ROWS:] = ln_ref[...]
        df[:, :, 0:DIL_ROWS] = d_ref[...]
        df[:, :, DIL_ROWS:] = dn_ref[...]
        biases = [_dil_bias(coef_ref[B_HPG * g + 2 * hp + h], True) for h in range(2)]
        col = biases[0][1]
        lo = _first_head((W, LANES))
        lo2 = _first_head((2 * W, LANES))
        for b in range(DIL_SUB):
            no_next = lax.rem(n * DIL_SUB + b + 1, nbs) == 0
            rows = slice(b * W, (b + 1) * W)
            both = slice(b * W, (b + 2) * W)
            dd = dof[both, :]
            dk = dv = None
            for h, (kh, vh, qh, ddh) in enumerate(zip(_each_head(k_ref[rows, :], lo), _each_head(v_ref[rows, :], lo),
                                                      _each_head(qf[both, :], lo2), _each_head(dd, lo2))):
                st = lax.dot_general(kh, qh, _DN["nt"], preferred_element_type=F32) * SCALE + biases[h][0]
                st = jnp.where(jnp.logical_and(no_next, col >= W), NEG_INF, st)
                pt = jnp.exp(st - lf[h, :, both])
                d = lax.dot_general(pt.astype(BF16), ddh, _DN["nn"], preferred_element_type=F32)
                dv = d if dv is None else dv + d
                dpt = lax.dot_general(vh, dd, _DN["nt"], preferred_element_type=F32)
                dst = (pt * (dpt - df[h, :, both]) * SCALE).astype(BF16)
                d = lax.dot_general(dst, qh, _DN["nn"], preferred_element_type=F32)
                dk = d if dk is None else dk + d
            dk_ref[rows, :] = dk.astype(BF16)
            dv_ref[rows, :] = dv.astype(BF16)

    sp = _dil_pair_specs(qoff, koff, voff)
    wide = jax.ShapeDtypeStruct((S, GROUP_W), BF16)
    return pl.pallas_call(
        body, out_shape=[wide, wide], grid=(NPG, DIL_STEPS),
        in_specs=[sp["smem"], sp["smem"], sp["k"], sp["v"], sp["q"], sp["q_next"], sp["o"], sp["o_next"], sp["row"],
                  sp["row_next"], sp["row2"], sp["row2_next"]],
        out_specs=[sp["o"], sp["o"]],
        scratch_shapes=[pltpu.VMEM((DIL_ROWS + W, LANES), BF16)] * 2 + [pltpu.VMEM((2, 1, DIL_ROWS + W), F32)] * 2,
        compiler_params=_cparams(("parallel", "parallel")), name=name,
    )(coef_t, nbs_t, k, v, q, q, do, do, stats, stats, stats, stats)


FFN_ROWS = 512
FFN_COLS = 256
HALO = 8


def _shifted(u, halo, back):
    T = u.shape[0]
    rows = lax.broadcasted_iota(jnp.int32, u.shape, 0)
    if back:
        s1 = jnp.where(rows == 0, halo[HALO - 1:HALO, :], pltpu.roll(u, 1, 0))
        s2 = jnp.where(rows == 0, halo[HALO - 2:HALO - 1, :],
                       jnp.where(rows == 1, halo[HALO - 1:HALO, :], pltpu.roll(u, 2, 0)))
    else:
        s1 = jnp.where(rows == T - 1, halo[0:1, :], pltpu.roll(u, T - 1, 0))
        s2 = jnp.where(rows == T - 1, halo[1:2, :],
                       jnp.where(rows == T - 2, halo[0:1, :], pltpu.roll(u, T - 2, 0)))
    return s1, s2


def _conv_parts(u_ref, h_ref, w_ref, b_ref, first):
    out = []
    for p in range(2):
        u = u_ref[p]
        halo = jnp.where(first, 0.0, h_ref[p])
        u1, u2 = _shifted(u, halo, True)
        w = w_ref[p]
        out.append((w[0:1, :] * u2 + w[1:2, :] * u1 + w[2:3, :] * u + b_ref[p], u1, u2, u))
    return out


def _ffn_specs():
    T, C = FFN_ROWS, FFN_COLS
    blk = pl.BlockSpec((2, T, C), lambda j, i: (0, i, j))
    prev = pl.BlockSpec((2, HALO, C), lambda j, i: (0, jnp.maximum(i * (T // HALO) - 1, 0), j))
    nxt = pl.BlockSpec((2, HALO, C), lambda j, i: (0, jnp.minimum((i + 1) * (T // HALO), S // HALO - 1), j))
    wsp = pl.BlockSpec((2, 3, C), lambda j, i: (0, 0, j))
    bsp = pl.BlockSpec((2, 1, C), lambda j, i: (0, 0, j))
    one = pl.BlockSpec((T, C), lambda j, i: (i, j))
    return blk, prev, nxt, wsp, bsp, one


def _ffn_act_fwd(u, w, b, name):
    blk, prev, _, wsp, bsp, one = _ffn_specs()

    def body(u_ref, h_ref, w_ref, b_ref, o_ref):
        (a, _, _, _), (g, _, _, _) = _conv_parts(u_ref, h_ref, w_ref, b_ref, pl.program_id(1) == 0)
        o_ref[...] = (g / (1.0 + jnp.exp(-g)) * a).astype(BF16)

    return pl.pallas_call(
        body, out_shape=jax.ShapeDtypeStruct((S, FF), BF16), grid=(FF // FFN_COLS, S // FFN_ROWS),
        in_specs=[blk, prev, wsp, bsp], out_specs=one,
        compiler_params=_cparams(("parallel", "parallel")), name=name,
    )(u, u, w, b)


def _ffn_act_bwd(u, dact, w, b, name):
    blk, prev, _, wsp, bsp, one = _ffn_specs()

    def body(u_ref, h_ref, da_ref, w_ref, b_ref, duc_ref, dwb_ref):
        i = pl.program_id(1)
        (a, a1, a2, a0), (g, g1, g2, g0) = _conv_parts(u_ref, h_ref, w_ref, b_ref, i == 0)
        dact_v = da_ref[...]
        sg = 1.0 / (1.0 + jnp.exp(-g))
        d_a = dact_v * (g * sg)
        d_g = dact_v * a * (sg * (1.0 + g * (1.0 - sg)))
        duc_ref[0] = d_a
        duc_ref[1] = d_g

        @pl.when(i == 0)
        def _():
            dwb_ref[...] = jnp.zeros(dwb_ref.shape, F32)

        for p, (d, s2, s1, s0) in enumerate(((d_a, a2, a1, a0), (d_g, g2, g1, g0))):
            dwb_ref[p, 0:1, :] += jnp.sum(d * s2, axis=0, keepdims=True)
            dwb_ref[p, 1:2, :] += jnp.sum(d * s1, axis=0, keepdims=True)
            dwb_ref[p, 2:3, :] += jnp.sum(d * s0, axis=0, keepdims=True)
            dwb_ref[p, 3:4, :] += jnp.sum(d, axis=0, keepdims=True)

    return pl.pallas_call(
        body, out_shape=[jax.ShapeDtypeStruct((2, S, FF), F32), jax.ShapeDtypeStruct((2, 8, FF), F32)],
        grid=(FF // FFN_COLS, S // FFN_ROWS), in_specs=[blk, prev, one, wsp, bsp],
        out_specs=[blk, pl.BlockSpec((2, 8, FFN_COLS), lambda j, i: (0, 0, j))],
        compiler_params=_cparams(("parallel", "arbitrary")), name=name,
    )(u, u, dact, w, b)


def _ffn_conv_bwd(duc, w, name):
    blk, _, nxt, wsp, _, _ = _ffn_specs()
    last = S // FFN_ROWS - 1

    def body(d_ref, h_ref, w_ref, du_ref):
        is_last = pl.program_id(1) == last
        for p in range(2):
            d = d_ref[p]
            halo = jnp.where(is_last, 0.0, h_ref[p])
            d1, d2 = _shifted(d, halo, False)
            wv = w_ref[p]
            du_ref[p] = (wv[2:3, :] * d + wv[1:2, :] * d1 + wv[0:1, :] * d2).astype(BF16)

    return pl.pallas_call(
        body, out_shape=jax.ShapeDtypeStruct((2, S, FF), BF16), grid=(FF // FFN_COLS, S // FFN_ROWS),
        in_specs=[blk, nxt, wsp], out_specs=blk,
        compiler_params=_cparams(("parallel", "parallel")), name=name,
    )(duc, duc, w)


def _adam_update(w, gv, m, v):
    c1 = 1.0 / (1.0 - ADAM_B1 ** ADAM_STEP)
    c2 = 1.0 / (1.0 - ADAM_B2 ** ADAM_STEP)
    mn = ADAM_B1 * m + (1.0 - ADAM_B1) * gv
    vn = ADAM_B2 * v + (1.0 - ADAM_B2) * (gv * gv)
    return -ADAM_LR * ((mn * c1) / (jnp.sqrt(vn * c2) + ADAM_EPS) + ADAM_WD * w), mn, vn


def _adamw(w, g, m, v, name):
    rows = w.shape[0]
    T = 8
    for cand in (256, 128, 64, 32, 16, 8):
        if rows % cand == 0:
            T = cand
            break

    def body(w_ref, g_ref, m_ref, v_ref, d_ref, mo_ref, vo_ref):
        d_ref[...], mo_ref[...], vo_ref[...] = _adam_update(w_ref[...], g_ref[...], m_ref[...], v_ref[...])

    blk = pl.BlockSpec((T, w.shape[1]), lambda i: (i, 0))
    sds = jax.ShapeDtypeStruct(w.shape, F32)
    return pl.pallas_call(
        body, out_shape=[sds, sds, sds], grid=(rows // T,), in_specs=[blk] * 4, out_specs=[blk] * 3,
        compiler_params=_cparams(("parallel",)), name=name,
    )(w, g, m, v)


ANY = pl.BlockSpec(memory_space=pl.ANY)


def _place():
    x, y, c = lax.axis_index("x"), lax.axis_index("y"), lax.axis_index("c")
    chips = [(1 - x, y), (x, 1 - y), (1 - x, 1 - y)]
    return x, y, c, chips


def _place_own(w, slot_arr, name):
    rows = w.shape[0]
    T = 16
    for cand in (2048, 1024, 512, 256, 128, 64, 32, 16):
        if rows % cand == 0:
            T = cand
            break

    def body(k_ref, w_ref, o_ref):
        o_ref[...] = w_ref[...]

    return pl.pallas_call(
        body, out_shape=jax.ShapeDtypeStruct((N_CHIPS, rows, FLAT_W), w.dtype),
        grid_spec=pltpu.PrefetchScalarGridSpec(
            num_scalar_prefetch=1, grid=(rows // T,),
            in_specs=[pl.BlockSpec((T, FLAT_W), lambda i, k: (i, 0))],
            out_specs=pl.BlockSpec((None, T, FLAT_W), lambda i, k: (k[0], i, 0))),
        compiler_params=_cparams(("parallel",)), name=name,
    )(slot_arr, w)


def _allgather_shards(w, buf):
    half_rows = w.shape[0] // 2
    assert half_rows % 16 == 0

    def body(w_ref, buf_ref, g_ref, send_sems, recv_sems):
        x, y, c, chips = _place()
        myk = 2 * x + y
        sibling = (x, y, 1 - c)
        h0 = pl.multiple_of(c * half_rows, 16)
        h1 = pl.multiple_of((1 - c) * half_rows, 16)

        def half(k, start):
            return g_ref.at[k, pl.ds(start, half_rows), :]

        def rcopy(sem, src, dst, to):
            return pltpu.make_async_remote_copy(src_ref=src, dst_ref=dst, send_sem=send_sems.at[sem],
                                                recv_sem=recv_sems.at[sem], device_id=to, device_id_type=MESH)

        ici = [rcopy(r, w_ref.at[pl.ds(h0, half_rows), :], half(myk, h0), (*chip, c)) for r, chip in enumerate(chips)]
        for cp in ici:
            cp.start()
        ks = [2 * cx + cy for cx, cy in chips]
        fwd = [rcopy(3 + r, half(ks[r], h0), half(ks[r], h0), sibling) for r in range(3)]
        for r in range(3):
            rcopy(r, half(ks[r], h0), half(ks[r], h0), (*chips[r], c)).wait_recv()
            fwd[r].start()
        for r in range(3):
            rcopy(3 + r, half(ks[r], h1), half(ks[r], h1), sibling).wait_recv()
        for cp in ici + fwd:
            cp.wait_send()

    return pl.pallas_call(
        body, out_shape=jax.ShapeDtypeStruct(buf.shape, w.dtype), in_specs=[ANY, ANY], out_specs=ANY,
        scratch_shapes=[pltpu.SemaphoreType.DMA((6,)), pltpu.SemaphoreType.DMA((6,))],
        input_output_aliases={1: 0},
        compiler_params=pltpu.CompilerParams(has_side_effects=True), name="allgather_shards",
    )(w, buf)


HBM_SPEC = pl.BlockSpec(memory_space=pltpu.HBM)
SEM_SPEC = pl.BlockSpec(memory_space=pltpu.SEMAPHORE)
DATAFLOW = pltpu.SideEffectType.DATAFLOW_SIDE_EFFECTING
OWN_SLOT = 3


def _late_gather_start(w, land):
    def body(w_ref, land_ref, send_sems, recv_sems, w_thru, land_thru, token):
        x, y, c, chips = _place()
        for r, chip in enumerate(chips):
            pltpu.make_async_remote_copy(src_ref=w_ref, dst_ref=land_ref.at[r], send_sem=send_sems.at[r],
                                         recv_sem=recv_sems.at[r], device_id=(*chip, c), device_id_type=MESH).start()
        token[...] = jnp.zeros_like(token)

    return pl.pallas_call(
        body, name="late_gather_start",
        out_shape=(pltpu.SemaphoreType.DMA((3,)), pltpu.SemaphoreType.DMA((3,)), pltpu.HBM(w.shape, w.dtype),
                   pltpu.HBM(land.shape, land.dtype), jax.ShapeDtypeStruct((8, LANES), F32)),
        in_specs=(HBM_SPEC, HBM_SPEC),
        out_specs=(SEM_SPEC, SEM_SPEC, HBM_SPEC, HBM_SPEC, pl.BlockSpec(memory_space=pltpu.VMEM)),
        input_output_aliases={0: 2, 1: 3}, compiler_params=pltpu.CompilerParams(has_side_effects=DATAFLOW),
    )(pltpu.with_memory_space_constraint(w, pltpu.HBM), pltpu.with_memory_space_constraint(land, pltpu.HBM))


def _late_gather_wait(send_sems, recv_sems, w_thru, land_thru, after):
    def body(w_ref, land_ref, send_sems, recv_sems, after_ref, w_dead, got_ref):
        x, y, c, chips = _place()
        for r, chip in enumerate(chips):
            cp = pltpu.make_async_remote_copy(src_ref=w_ref, dst_ref=land_ref.at[r], send_sem=send_sems.at[r],
                                              recv_sem=recv_sems.at[r], device_id=(*chip, c), device_id_type=MESH)
            cp.wait_send()
            cp.wait_recv()

    return pl.pallas_call(
        body, name="late_gather_wait",
        out_shape=(pltpu.HBM(w_thru.shape, w_thru.dtype), pltpu.HBM(land_thru.shape, land_thru.dtype)),
        in_specs=(HBM_SPEC, HBM_SPEC, SEM_SPEC, SEM_SPEC, pl.BlockSpec(memory_space=pl.ANY)),
        out_specs=(HBM_SPEC, HBM_SPEC), input_output_aliases={0: 0, 1: 1},
        compiler_params=pltpu.CompilerParams(has_side_effects=DATAFLOW),
    )(w_thru, land_thru, send_sems, recv_sems, after)


def _flat_tile(rows):
    return next(t for t in (2048, 1024, 512, 256, 128, 64, 32, 16) if rows % t == 0)


def _sibling_swap_half(g, tag):
    half = g.shape[1] // 2

    def body(g_ref, o_ref, send_sem, recv_sem):
        x, y, c, _ = _place()
        theirs = pl.multiple_of((1 - c) * half, 8)
        cp = pltpu.make_async_remote_copy(src_ref=g_ref.at[:, pl.ds(theirs, half), :], dst_ref=o_ref,
                                          send_sem=send_sem, recv_sem=recv_sem, device_id=(x, y, 1 - c),
                                          device_id_type=MESH)
        cp.start()
        cp.wait()

    return pl.pallas_call(
        body, out_shape=jax.ShapeDtypeStruct((N_CHIPS, half, FLAT_W), F32), in_specs=[ANY], out_specs=ANY,
        scratch_shapes=[pltpu.SemaphoreType.DMA, pltpu.SemaphoreType.DMA],
        compiler_params=pltpu.CompilerParams(has_side_effects=True), name=f"rs_sibling_swap_{tag}",
    )(g)


def _sibling_swap_start(g, land):
    half = g.shape[1] // 2

    def body(g_ref, land_ref, send_sem, recv_sem, g_thru, land_thru, token):
        x, y, c, _ = _place()
        theirs = pl.multiple_of((1 - c) * half, 8)
        pltpu.make_async_remote_copy(src_ref=g_ref.at[:, pl.ds(theirs, half), :], dst_ref=land_ref, send_sem=send_sem,
                                     recv_sem=recv_sem, device_id=(x, y, 1 - c), device_id_type=MESH).start()
        token[...] = jnp.zeros_like(token)

    return pl.pallas_call(
        body, name="rs_swap_start",
        out_shape=(pltpu.SemaphoreType.DMA(()), pltpu.SemaphoreType.DMA(()), pltpu.HBM(g.shape, g.dtype),
                   pltpu.HBM(land.shape, land.dtype), jax.ShapeDtypeStruct((8, LANES), F32)),
        in_specs=(HBM_SPEC, HBM_SPEC),
        out_specs=(SEM_SPEC, SEM_SPEC, HBM_SPEC, HBM_SPEC, pl.BlockSpec(memory_space=pltpu.VMEM)),
        input_output_aliases={0: 2, 1: 3}, compiler_params=pltpu.CompilerParams(has_side_effects=DATAFLOW),
    )(pltpu.with_memory_space_constraint(g, pltpu.HBM), pltpu.with_memory_space_constraint(land, pltpu.HBM))


def _sibling_swap_wait(send_sem, recv_sem, g_thru, land_thru, after):
    half = land_thru.shape[1]

    def body(g_ref, land_ref, send_sem, recv_sem, after_ref, g_done, got_ref):
        x, y, c, _ = _place()
        theirs = pl.multiple_of((1 - c) * half, 8)
        cp = pltpu.make_async_remote_copy(src_ref=g_ref.at[:, pl.ds(theirs, half), :], dst_ref=land_ref,
                                          send_sem=send_sem, recv_sem=recv_sem, device_id=(x, y, 1 - c),
                                          device_id_type=MESH)
        cp.wait_send()
        cp.wait_recv()

    return pl.pallas_call(
        body, name="rs_swap_wait",
        out_shape=(pltpu.HBM(g_thru.shape, g_thru.dtype), pltpu.HBM(land_thru.shape, land_thru.dtype)),
        in_specs=(HBM_SPEC, HBM_SPEC, SEM_SPEC, SEM_SPEC, pl.BlockSpec(memory_space=pl.ANY)),
        out_specs=(HBM_SPEC, HBM_SPEC), input_output_aliases={0: 0, 1: 1},
        compiler_params=pltpu.CompilerParams(has_side_effects=DATAFLOW),
    )(g_thru, land_thru, send_sem, recv_sem, after)


def _pair_sum(g, other, c_arr, tag):
    half = other.shape[1]
    T = _flat_tile(half)

    def body(c_ref, g_ref, o_ref, s_ref):
        s_ref[...] = (g_ref[...] + o_ref[...]).astype(BF16)

    nb = half // T
    return pl.pallas_call(
        body, out_shape=jax.ShapeDtypeStruct((N_CHIPS, half, FLAT_W), BF16),
        grid_spec=pltpu.PrefetchScalarGridSpec(
            num_scalar_prefetch=1, grid=(N_CHIPS, nb),
            in_specs=[pl.BlockSpec((None, T, FLAT_W), lambda k, i, c: (k, c[0] * nb + i, 0)),
                      pl.BlockSpec((None, T, FLAT_W), lambda k, i, c: (k, i, 0))],
            out_specs=pl.BlockSpec((None, T, FLAT_W), lambda k, i, c: (k, i, 0))),
        compiler_params=_cparams(("parallel", "parallel")), name=f"rs_pair_sum_{tag}",
    )(c_arr, g, other)


def _chip_exchange_start(s, land, tag):
    def body(s_ref, land_ref, send_sems, recv_sems, s_thru, land_thru, token):
        x, y, c, chips = _place()
        for r, (cx, cy) in enumerate(chips):
            pltpu.make_async_remote_copy(src_ref=s_ref.at[2 * cx + cy], dst_ref=land_ref.at[r], send_sem=send_sems.at[r],
                                         recv_sem=recv_sems.at[r], device_id=(cx, cy, c), device_id_type=MESH).start()
        token[...] = jnp.zeros_like(token)

    return pl.pallas_call(
        body, name=f"rs_exchange_start_{tag}",
        out_shape=(pltpu.SemaphoreType.DMA((3,)), pltpu.SemaphoreType.DMA((3,)), pltpu.HBM(s.shape, s.dtype),
                   pltpu.HBM(land.shape, land.dtype), jax.ShapeDtypeStruct((8, LANES), F32)),
        in_specs=(HBM_SPEC, HBM_SPEC),
        out_specs=(SEM_SPEC, SEM_SPEC, HBM_SPEC, HBM_SPEC, pl.BlockSpec(memory_space=pltpu.VMEM)),
        input_output_aliases={0: 2, 1: 3}, compiler_params=pltpu.CompilerParams(has_side_effects=DATAFLOW),
    )(pltpu.with_memory_space_constraint(s, pltpu.HBM), pltpu.with_memory_space_constraint(land, pltpu.HBM))


def _chip_exchange_wait(send_sems, recv_sems, s_thru, land_thru, after, tag):
    def body(s_ref, land_ref, send_sems, recv_sems, after_ref, s_done, got_ref):
        x, y, c, chips = _place()
        for r, (cx, cy) in enumerate(chips):
            cp = pltpu.make_async_remote_copy(src_ref=s_ref.at[2 * cx + cy], dst_ref=land_ref.at[r],
                                              send_sem=send_sems.at[r], recv_sem=recv_sems.at[r], device_id=(cx, cy, c),
                                              device_id_type=MESH)
            cp.wait_send()
            cp.wait_recv()

    return pl.pallas_call(
        body, name=f"rs_exchange_wait_{tag}",
        out_shape=(pltpu.HBM(s_thru.shape, s_thru.dtype), pltpu.HBM(land_thru.shape, land_thru.dtype)),
        in_specs=(HBM_SPEC, HBM_SPEC, SEM_SPEC, SEM_SPEC, pl.BlockSpec(memory_space=pl.ANY)),
        out_specs=(HBM_SPEC, HBM_SPEC), input_output_aliases={0: 0, 1: 1},
        compiler_params=pltpu.CompilerParams(has_side_effects=DATAFLOW),
    )(s_thru, land_thru, send_sems, recv_sems, after)


def _chip_sum(s, r, k_arr, tag):
    half = s.shape[1]
    T = _flat_tile(half)

    def body(k_ref, s_ref, r_ref, o_ref):
        o_ref[...] = ((s_ref[...].astype(F32) + r_ref[0].astype(F32)) + r_ref[1].astype(F32)) + r_ref[2].astype(F32)

    return pl.pallas_call(
        body, out_shape=jax.ShapeDtypeStruct((half, FLAT_W), F32),
        grid_spec=pltpu.PrefetchScalarGridSpec(
            num_scalar_prefetch=1, grid=(half // T,),
            in_specs=[pl.BlockSpec((None, T, FLAT_W), lambda i, k: (k[0], i, 0)),
                      pl.BlockSpec((3, T, FLAT_W), lambda i, k: (0, i, 0))],
            out_specs=pl.BlockSpec((T, FLAT_W), lambda i, k: (i, 0))),
        compiler_params=_cparams(("parallel",)), name=f"rs_chip_sum_{tag}",
    )(k_arr, s, r)


def _sibling_send(t, tag):
    def body(t_ref, o_ref, send_sem, recv_sem):
        x, y, c, _ = _place()
        cp = pltpu.make_async_remote_copy(src_ref=t_ref, dst_ref=o_ref, send_sem=send_sem, recv_sem=recv_sem,
                                          device_id=(x, y, 1 - c), device_id_type=MESH)
        cp.start()
        cp.wait()

    return pl.pallas_call(
        body, out_shape=jax.ShapeDtypeStruct(t.shape, F32), in_specs=[ANY], out_specs=ANY,
        scratch_shapes=[pltpu.SemaphoreType.DMA, pltpu.SemaphoreType.DMA],
        compiler_params=pltpu.CompilerParams(has_side_effects=True), name=f"rs_sibling_send_{tag}",
    )(t)


def _allreduce_small(v):
    def body(v_ref, o_ref, buf, send_sems, recv_sems):
        x, y, c, _ = _place()
        me = 4 * x + 2 * y + c
        buf[me] = v_ref[...]
        cps = []
        for mask in range(1, 8):
            a, b, d = (mask >> 2) & 1, (mask >> 1) & 1, mask & 1
            peer = (x + a - 2 * a * x, y + b - 2 * b * y, c + d - 2 * d * c)
            cps.append(pltpu.make_async_remote_copy(
                src_ref=v_ref, dst_ref=buf.at[me], send_sem=send_sems.at[mask - 1], recv_sem=recv_sems.at[mask - 1],
                device_id=peer, device_id_type=MESH))
        for cp in cps:
            cp.start()
        for cp in cps:
            cp.wait()
        total = buf[0]
        for dev in range(1, 8):
            total = total + buf[dev]
        o_ref[...] = total

    vm = pl.BlockSpec(memory_space=pltpu.VMEM)
    return pl.pallas_call(
        body, out_shape=jax.ShapeDtypeStruct((SMALL_ROWS, 1024), F32), in_specs=[vm], out_specs=vm,
        scratch_shapes=[pltpu.VMEM((8, SMALL_ROWS, 1024), F32), pltpu.SemaphoreType.DMA((7,)),
                        pltpu.SemaphoreType.DMA((7,))],
        compiler_params=pltpu.CompilerParams(has_side_effects=True), name="allreduce_small",
    )(v)


def _col_to_row(t):
    return t.reshape(t.shape[0], 1, S)


def _residue_rows(t, d, inverse=False):
    if d == 1:
        return t
    shape = (d, S // d) if inverse else (S // d, d)
    return t.reshape(shape + t.shape[1:]).transpose(1, 0, 2).reshape(t.shape)


def _residue_vecs(t, d, inverse=False):
    if d == 1:
        return t
    shape = (d, S // d) if inverse else (S // d, d)
    return t.reshape((t.shape[0],) + shape).transpose(0, 2, 1).reshape(t.shape)


def _ffn_fwd(x, g, w_up, cw, cb, w_down, tag):
    (h,) = _rms_fwd(x, [g], f"{tag}_norm")
    u = _mm(h, w_up, mode="nn", tm=1024, tn=1408, tk=1024, o_split=2, name=f"{tag}_up")
    act = _ffn_act_fwd(u, cw, cb, f"{tag}_act")
    x_out = _mm(act, w_down, mode="nn", tm=1024, tn=512, tk=FF, res=x, name=f"{tag}_down")
    return x_out, (h, u, act)


def _ffn_bwd(x, g, w_up, cw, cb, w_down, saved, dx, dxb, tag):
    h, u, act = saved
    d_w_down = _mm(act, dxb, mode="tn", tm=1408, tn=512, tk=2048, name=f"{tag}_dwdown")
    dact = _mm(dxb, w_down, mode="nt", tm=1024, tn=1408, tk=1024, name=f"{tag}_dact")
    duc, dwb = _ffn_act_bwd(u, dact, cw, cb, f"{tag}_dgate")
    du = _ffn_conv_bwd(duc, cw, f"{tag}_dconv")
    d_w_up = _mm(h, du, mode="tn", tm=1024, tn=1408, tk=2048, b_split=2, name=f"{tag}_dwup")
    dh = _mm(du, w_up, mode="nt", tm=1024, tn=512, tk=FF, a_split=1, name=f"{tag}_dh")
    dx_new, dxb_new, (dg,) = _rms_bwd(x, dx, [(g, dh)], f"{tag}_dnorm")
    d_cw = dwb[:, 0:3, :].transpose(1, 0, 2).reshape(3, 2 * FF)
    d_cb = dwb[:, 3, :].reshape(2 * FF)
    return dx_new, dxb_new, dict(w_up=d_w_up, w_down=d_w_down, conv_w=d_cw, conv_b=d_cb, norm_g=dg.reshape(D))


def _local_step(x, target, p, late_weights, late_grads_ready, late_grads_continue):
    g = {}
    (h1,) = _rms_fwd(x, [p["mix_norm_g"][0]], "a_norm")
    w_qkv = p["a_w_in"][:, :QKV_W]
    w_f = jnp.pad(p["a_w_in"][:, QKV_W:], ((0, 0), (0, LANES - A_HEADS)))
    b_f = jnp.pad(p["a_b_f"].reshape(1, A_HEADS), ((0, 0), (0, LANES - A_HEADS)))
    qkv = _mm(h1, w_qkv, mode="nn", tm=1024, tn=512, tk=1024, out_dtype=BF16, name="a_qkv")
    pf = _mm(h1, w_f, mode="nn", tm=1024, tn=LANES, tk=1024, name="a_gate")
    aug_q, aug_k = _fgate_fwd(pf, b_f, "a_gate_scan")
    oa2, lse_a = _fox_pair_fwd(qkv, aug_q, aug_k, "a_attn")
    x1 = _mm(oa2, p["a_w_out"], mode="nn", tm=1024, tn=512, tk=1024, res=x, name="a_out")
    p = {**p, **late_weights(x1)}
    x2, ffn0 = _ffn_fwd(x1, p["ffn_norm_g"][0], p["ffn_w_up"][0], p["conv_w"][0], p["conv_b"][0], p["ffn_w_down"][0], "f0")
    hk, h3 = _rms_fwd(x2, [p["kv_norm_g"], p["mix_norm_g"][1]], "kv_b_norm")
    kvb = _mm(hk, p["w_kv"], mode="nn", tm=1024, tn=512, tk=1024, out_dtype=BF16, name="kv_proj")
    qb = _mm(h3, p["b_w_q"], mode="nn", tm=1024, tn=512, tk=1024, out_dtype=BF16, name="b_q")
    dil_in = []
    for gi, (_, d) in enumerate(B_GROUPS):
        if d == 1:
            dil_in.append((qb, kvb, kvb, gi * NPG, gi * NPG, (3 + gi) * NPG))
        else:
            qg = _residue_rows(qb[:, gi * GROUP_W:(gi + 1) * GROUP_W], d)
            kvg = _residue_rows(kvb.reshape(S, 2, 3, GROUP_W)[:, :, gi, :].reshape(S, 2 * GROUP_W), d)
            dil_in.append((qg, kvg, kvg, 0, 0, NPG))
    o_g, lse_g = [], []
    for gi, (_, d) in enumerate(B_GROUPS):
        qg, kg, vg, qoff, koff, voff = dil_in[gi]
        og, lg = _dil_pair_fwd(gi, qg, kg, vg, qoff, koff, voff, f"b_attn{gi}")
        o_g.append(_residue_rows(og, d, inverse=True))
        lse_g.append(_residue_vecs(lg, d, inverse=True))
    ob, ob2, lse_b = _dil_pair_merge(o_g, lse_g, "b_merge")
    x3 = _mm(ob2, p["b_w_out"], mode="nn", tm=1024, tn=512, tk=B_OUT_W, res=x2, name="b_out")
    x4, ffn1 = _ffn_fwd(x3, p["ffn_norm_g"][1], p["ffn_w_up"][1], p["conv_w"][1], p["conv_b"][1], p["ffn_w_down"][1], "f1")
    loss, dx, dxb, dg_final = _loss_head(x4, p["final_norm_g"], target, "loss_head")
    g["final_norm_g"] = dg_final.reshape(D)

    dx, dxb, gf1 = _ffn_bwd(x3, p["ffn_norm_g"][1], p["ffn_w_up"][1], p["conv_w"][1], p["conv_b"][1], p["ffn_w_down"][1],
                            ffn1, dx, dxb, "f1")
    g["b_w_out"] = _mm(ob2, dxb, mode="tn", tm=B_OUT_W, tn=512, tk=1024, name="b_dwout")
    dob = _mm(dxb, p["b_w_out"], mode="nt", tm=1024, tn=B_OUT_W, tk=1024, name="b_do")
    delta_b = _pair_rowdot(dob, ob, "b_delta")
    dob16 = dob.astype(BF16)
    stats_b = jnp.concatenate([lse_b, delta_b], axis=0)
    dq_g, dk_g, dv_g = [], [], []
    for gi, (_, d) in enumerate(B_GROUPS):
        qg, kg, vg, qoff, koff, voff = dil_in[gi]
        dog, stats_d = _residue_rows(dob16, d), _residue_vecs(stats_b, d)
        dqd = _dil_pair_dq(gi, qg, kg, vg, qoff, koff, voff, dog, stats_d, f"b_dq{gi}")
        dkd, dvd = _dil_pair_dkv(gi, qg, kg, vg, qoff, koff, voff, dog, _col_to_row(stats_d), f"b_dkv{gi}")
        dq_g.append(_residue_rows(dqd, d, inverse=True))
        dk_g.append(_residue_rows(dkd, d, inverse=True))
        dv_g.append(_residue_rows(dvd, d, inverse=True))
    dqb = jnp.concatenate(dq_g, axis=1)
    dkvb = jnp.concatenate(dk_g + dv_g, axis=1)
    g["b_w_q"] = _mm(h3, dqb, mode="tn", tm=1024, tn=512, tk=S, name="b_dwq")
    dh3 = _mm(dqb, p["b_w_q"], mode="nt", tm=1024, tn=512, tk=B_Q_W, name="b_dh")
    g["w_kv"] = _mm(hk, dkvb, mode="tn", tm=1024, tn=512, tk=S, name="kv_dw")
    dhk = _mm(dkvb, p["w_kv"], mode="nt", tm=1024, tn=512, tk=3072, name="kv_dh")
    dx, dxb, (dg_mix1, dg_kv) = _rms_bwd(x2, dx, [(p["mix_norm_g"][1], dh3), (p["kv_norm_g"], dhk)], "b_dnorm")
    g["kv_norm_g"] = dg_kv.reshape(D)
    dx, dxb, gf0 = _ffn_bwd(x1, p["ffn_norm_g"][0], p["ffn_w_up"][0], p["conv_w"][0], p["conv_b"][0], p["ffn_w_down"][0],
                            ffn0, dx, dxb, "f0")
    g["ffn_w_up"] = [gf0["w_up"], gf1["w_up"]]
    g["ffn_w_down"] = [gf0["w_down"], gf1["w_down"]]
    g["ffn_conv_w"] = jnp.stack([gf0["conv_w"], gf1["conv_w"]])
    token = late_grads_ready(g)
    a_w_out_t = p["a_w_out"] + token[0, 0].astype(BF16)
    g["a_w_out"] = _mm(oa2, dxb, mode="tn", tm=1024, tn=512, tk=S, name="a_dwout")
    doa = _mm(dxb, a_w_out_t, mode="nt", tm=1024, tn=512, tk=1024, name="a_do")
    delta_a = _pair_rowdot(doa, oa2, "a_delta")
    token = late_grads_continue(delta_a)
    delta_row = _col_to_row(delta_a) + token[0, 0]
    dqa, dka, dva, dck, dcq = _fox_pair_bwd(qkv, doa, _col_to_row(lse_a), delta_row, aug_q, aug_k, "a_dattn")
    dqkv = jnp.concatenate([dqa, dka, dva], axis=1)
    pad_heads = lambda t: jnp.pad(t.reshape(A_HEADS, S).T, ((0, 0), (0, LANES - A_HEADS)))
    dpf, db_f = _fgate_bwd(pf, b_f, pad_heads(dck), pad_heads(dcq), "a_dgate_scan")
    g["a_b_f"] = db_f[:, :A_HEADS]
    d_w_qkv = _mm(h1, dqkv, mode="tn", tm=1024, tn=512, tk=S, name="a_dwqkv")
    d_w_f = _mm(h1, dpf, mode="tn", tm=1024, tn=LANES, tk=1024, name="a_dwgate")
    g["a_w_in"] = jnp.concatenate([d_w_qkv, d_w_f[:, :A_HEADS]], axis=1)
    dh1 = _mm(dqkv, w_qkv, mode="nt", tm=1024, tn=512, tk=3072, name="a_dh")
    dh1 = _mm(dpf, w_f, mode="nt", tm=1024, tn=512, tk=LANES, res=dh1, name="a_dh_gate")
    dx, _, (dg_mix0,) = _rms_bwd(x, dx, [(p["mix_norm_g"][0], dh1)], "a_dnorm")

    g["mix_norm_g"] = jnp.stack([dg_mix0.reshape(D), dg_mix1.reshape(D)])
    g["ffn_norm_g"] = jnp.stack([gf0["norm_g"], gf1["norm_g"]])
    g["ffn_conv_b"] = jnp.stack([gf0["conv_b"], gf1["conv_b"]])
    return loss[0, 0], dx, g


_SHARD_SHAPES = {"a_w_in": (1, 1024, 772), "a_w_out": (1, 256, 1024), "b_w_q": (1, 1024, 384), "b_w_out": (1, 512, 256),
                 "w_kv": (1024, 768), "ffn_w_up": (2, 1024, 1408), "ffn_w_down": (2, 704, 1024), "ffn_conv_w": (2, 3, 1408)}
_SMALL = (("kv_norm_g", (1024,)), ("mix_norm_g", (2, 1024)), ("ffn_norm_g", (2, 1024)), ("final_norm_g", (1024,)),
          ("a_b_f", (1, 16)), ("ffn_conv_b", (2, 5632)))


def _unslabs(rows, L, R, C, rpad):
    nc = -(-C // FLAT_W)
    return rows.reshape(L, nc, rpad, FLAT_W).transpose(0, 2, 1, 3).reshape(L, rpad, nc * FLAT_W)[:, :R, :C]


_SEG_RT = {"ffn_w_down": 704, "a_w_in": 1024, "a_w_out": 256, "b_w_q": 1024, "b_w_out": 512, "w_kv": 1024,
           "ffn_w_up": 1024, "ffn_conv_w": 16}
_ROW_SHARDED = ("a_w_out", "ffn_w_down")


_LAYOUTS = {"early": ("a_w_in", "a_w_out"), "late": ("ffn_w_down", "b_w_q", "b_w_out", "w_kv", "ffn_w_up", "ffn_conv_w"),
            "grad_early": ("a_w_in", "a_w_out"),
            "grad_late": ("ffn_w_up", "ffn_w_down", "b_w_q", "w_kv", "b_w_out", "ffn_conv_w")}
_GRAD_ROWS = {"grad_early": 10240, "grad_late": 45056}


def _layout_rows(layout):
    used = sum(_seg_rows(*s) for s in _SEGS if s[0] in _LAYOUTS[layout])
    rows = _GRAD_ROWS.get(layout, used)
    assert rows >= used
    return rows


def _grad_layout(name):
    return "grad_early" if name in _LAYOUTS["grad_early"] else "grad_late"


def _seg(name, layout=None):
    layout = layout or _grad_layout(name)
    off = 0
    for s in sorted((s for s in _SEGS if s[0] in _LAYOUTS[layout]), key=lambda s: _LAYOUTS[layout].index(s[0])):
        _, L, R, C, rpad = s
        if layout in _GRAD_ROWS:
            per_layer = -(-C // FLAT_W) * rpad
            off = -(-off // per_layer) * per_layer
        if s[0] == name:
            rt = _SEG_RT[name]
            assert off % rt == 0 and rpad % rt == 0
            half = _layout_rows(layout) // 2
            assert off + _seg_rows(*s) <= 2 * half
            assert layout not in _GRAD_ROWS or half % rt == 0 or off + _seg_rows(*s) <= half
            return dict(L=L, R=R, C=C, rpad=rpad, nc=-(-C // FLAT_W), rt=rt, off=off, ni=rpad // rt, half=half)
        off += _seg_rows(*s)
    raise KeyError(name)


def _flat_block(sg, term=0):
    base = (sg["off"] + term * sg["L"] * sg["nc"] * sg["rpad"]) // sg["rt"]
    return lambda l, j, i: base + (l * sg["nc"] + j) * sg["ni"] + i


def _native3(t, name):
    sg = _seg(name)
    t = t.reshape(sg["L"], sg["R"], sg["C"])
    return jnp.pad(t, ((0, 0), (0, sg["rpad"] - sg["R"]), (0, 0))) if sg["rpad"] != sg["R"] else t


def _slab_pack(flat, t, name, layout, term=None):
    sg = _seg(name, layout)
    rt = sg["rt"]
    rb = _flat_block(sg, term or 0)

    def body(*refs):
        t_ref, o_ref = refs[-2], refs[-1]
        val = t_ref[...]
        o_ref[...] = val.astype(BF16) if term is None else _split3(val)[term]

    in_specs = [pl.BlockSpec((None, rt, FLAT_W), lambda l, j, i: (l, i, j))]
    args = [t]
    if flat is not None:
        in_specs, args = [ANY] + in_specs, [flat] + args
    return pl.pallas_call(
        body, out_shape=jax.ShapeDtypeStruct((_layout_rows(layout), FLAT_W), BF16), grid=(sg["L"], sg["nc"], sg["ni"]),
        in_specs=in_specs, out_specs=pl.BlockSpec((rt, FLAT_W), lambda l, j, i: (rb(l, j, i), 0)),
        input_output_aliases={0: 0} if flat is not None else {},
        compiler_params=_cparams(("parallel", "parallel", "parallel")), name=f"pack_{name}_{term or 0}",
    )(*args)


def _full_spec(sg, name):
    rt, nc, ni = sg["rt"], sg["nc"], sg["ni"]
    if name in _ROW_SHARDED:
        return (sg["L"], N_CHIPS * sg["R"], sg["C"]), pl.BlockSpec((None, rt, FLAT_W), lambda k, l, j, i: (l, k * ni + i, j))
    return ((sg["L"], sg["rpad"], N_CHIPS * nc * FLAT_W),
            pl.BlockSpec((None, rt, FLAT_W), lambda k, l, j, i: (l, i, k * nc + j)))


def _slab_unpack(gathered, slots, name, layout, own=None):
    sg = _seg(name, layout)
    rb = _flat_block(sg)
    shape, _ = _full_spec(sg, name)
    rt, nc, ni = sg["rt"], sg["nc"], sg["ni"]
    width = nc * FLAT_W
    last = gathered.shape[0] - 1

    def body(*refs):
        s_ref, o_ref = refs[0], refs[-1]
        is_own = s_ref[pl.program_id(0)] == OWN_SLOT
        for j in range(nc):
            val = refs[1 + j][...]
            if own is not None:
                val = jnp.where(is_own, refs[1 + nc + j][...], val)
            o_ref[:, j * FLAT_W:(j + 1) * FLAT_W] = val

    if name in _ROW_SHARDED:
        o_spec = pl.BlockSpec((None, rt, width), lambda k, l, i, s: (l, k * ni + i, 0))
    else:
        o_spec = pl.BlockSpec((None, rt, width), lambda k, l, i, s: (l, i, k))
    in_specs = [pl.BlockSpec((None, rt, FLAT_W), lambda k, l, i, s, j=j: (jnp.minimum(s[k], last), rb(l, j, i), 0))
                for j in range(nc)]
    args = [gathered] * nc
    if own is not None:
        in_specs += [pl.BlockSpec((rt, FLAT_W), lambda k, l, i, s, j=j: (rb(l, j, i), 0)) for j in range(nc)]
        args += [own] * nc
    return pl.pallas_call(
        body, out_shape=jax.ShapeDtypeStruct(shape, BF16),
        grid_spec=pltpu.PrefetchScalarGridSpec(num_scalar_prefetch=1, grid=(N_CHIPS, sg["L"], ni), in_specs=in_specs,
                                               out_specs=o_spec),
        compiler_params=_cparams(("parallel",) * 3), name=f"unpack_{name}",
    )(slots, *args)


def _slab_pack_grad(flat4, g, name, layer=None):
    sg = _seg(name)
    rows = _layout_rows(_grad_layout(name))
    shape, _ = _full_spec(sg, name)
    n_layers = sg["L"] if layer is None else 1
    assert g.shape == (n_layers,) + shape[1:], (name, g.shape, shape)
    rt, nc = sg["rt"], sg["nc"]
    assert sg["ni"] == 1 and sg["off"] % (nc * rt) == 0
    base = sg["off"] // (nc * rt) + (layer or 0)

    def body(*refs):
        g_ref, o_ref = refs[-2], refs[-1]
        for j in range(nc):
            o_ref[j * rt:(j + 1) * rt, :] = g_ref[:, j * FLAT_W:(j + 1) * FLAT_W]

    if name in _ROW_SHARDED:
        spec = pl.BlockSpec((None, rt, nc * FLAT_W), lambda k, l: (l, k, 0))
    else:
        spec = pl.BlockSpec((None, rt, nc * FLAT_W), lambda k, l: (l, 0, k))
    in_specs, args = [spec], [g]
    if flat4 is not None:
        in_specs, args = [pl.BlockSpec(memory_space=pl.ANY)] + in_specs, [flat4] + args
    return pl.pallas_call(
        body, out_shape=jax.ShapeDtypeStruct((N_CHIPS, rows, FLAT_W), F32), grid=(N_CHIPS, n_layers),
        in_specs=in_specs, out_specs=pl.BlockSpec((None, nc * rt, FLAT_W), lambda k, l: (k, base + l, 0)),
        input_output_aliases={0: 0} if flat4 is not None else {},
        compiler_params=_cparams(("parallel",) * 2), name=f"packgrad_{name}_{layer or 0}",
    )(*args)


def _adamw_shard(w, m, v, g_mine, g_other, c_arr, name):
    sg = _seg(name)
    rt = sg["rt"]
    rb = _flat_block(sg)
    per_half = sg["half"] // rt

    def half_of(l, j, i):
        return (rb(l, j, i) * rt) // sg["half"]

    def body(c_ref, w_ref, m_ref, v_ref, gm_ref, go_ref, g_ref, d_ref, mo_ref, vo_ref):
        is_mine = half_of(pl.program_id(0), pl.program_id(1), pl.program_id(2)) == c_ref[0]
        gv = jnp.where(is_mine, gm_ref[...], go_ref[...])
        g_ref[...] = gv
        d_ref[...], mo_ref[...], vo_ref[...] = _adam_update(w_ref[...], gv, m_ref[...], v_ref[...])

    nat = pl.BlockSpec((None, rt, FLAT_W), lambda l, j, i, c: (l, i, j))
    half = pl.BlockSpec((rt, FLAT_W), lambda l, j, i, c: (rb(l, j, i) - half_of(l, j, i) * per_half, 0))
    sds = jax.ShapeDtypeStruct(w.shape, F32)
    return pl.pallas_call(
        body, out_shape=[sds] * 4,
        grid_spec=pltpu.PrefetchScalarGridSpec(num_scalar_prefetch=1, grid=(sg["L"], sg["nc"], sg["ni"]),
                                               in_specs=[nat, nat, nat, half, half], out_specs=[nat] * 4),
        compiler_params=_cparams(("parallel", "parallel", "parallel")), name=f"adamw_{name}",
    )(c_arr, w, m, v, g_mine, g_other)


def _pack_small(vals, loss=None):
    parts = [vals[name].astype(F32).reshape(-1) for name, _ in _SMALL]
    if loss is not None:
        parts.append(loss.reshape(1))
    flat = jnp.concatenate(parts)
    return jnp.pad(flat, (0, SMALL_ROWS * 1024 - flat.shape[0])).reshape(SMALL_ROWS, 1024)


def _unpack_small(flat):
    flat = flat.reshape(-1)
    out = {}
    o = 0
    for name, shape in _SMALL:
        n = int(np.prod(shape))
        out[name] = flat[o:o + n].reshape(shape)
        o += n
    return out, flat[o]


_BIG = ("a_w_in", "a_w_out", "b_w_q", "b_w_out", "w_kv", "ffn_w_up", "ffn_w_down", "ffn_conv_w")
A_IN_PAD = 896


def _pack_weights(w, layout):
    flat = None
    for name in _LAYOUTS[layout]:
        t = _native3(w[name], name)
        for term in ((0, 1, 2) if name == "ffn_conv_w" else (None,)):
            flat = _slab_pack(flat, t, name, layout, term)
    return flat


def _early_weights(gathered, slots):
    a_in = _slab_unpack(gathered, slots, "a_w_in", "early")
    a_in = a_in.reshape(D, N_CHIPS, A_IN_PAD)[:, :, :772].reshape(D, N_CHIPS * 772)
    return dict(a_w_in=a_in, a_w_out=_slab_unpack(gathered, slots, "a_w_out", "early")[0])


def _late_weights(landed, slots, own):
    full = {name: _slab_unpack(landed, slots, name, "late", own) for name in _LAYOUTS["late"] if name != "ffn_conv_w"}
    sg = _seg("ffn_conv_w", "late")
    n1 = sg["nc"] * sg["rpad"]
    conv = slice(sg["off"], sg["off"] + CONV_TERMS * n1)
    conv_rows = jnp.concatenate([landed[:, conv], own[None, conv]], axis=0)
    per_chip = []
    for k in range(N_CHIPS):
        rows = lax.dynamic_index_in_dim(conv_rows, slots[k], axis=0, keepdims=False)
        terms = [_unslabs(rows[i * n1:(i + 1) * n1], 1, sg["R"], sg["C"], sg["rpad"]).astype(F32) for i in range(CONV_TERMS)]
        per_chip.append((terms[0] + terms[1]) + terms[2])
    cw = jnp.concatenate(per_chip, axis=2).reshape(2, 3, 2, FF).transpose(0, 2, 1, 3)
    return dict(b_w_q=full["b_w_q"][0], b_w_out=full["b_w_out"][0], w_kv=full["w_kv"][0], ffn_w_up=full["ffn_w_up"],
                ffn_w_down=full["ffn_w_down"], conv_w=cw)


def _shard_grads(g, layout):
    def full(name):
        if name == "a_w_in":
            a_in = jnp.pad(g[name].reshape(D, N_CHIPS, 772), ((0, 0), (0, 0), (0, A_IN_PAD - 772)))
            return a_in.reshape(1, D, N_CHIPS * A_IN_PAD)
        if name == "ffn_conv_w":
            sgc = _seg(name)
            return jnp.pad(g[name].reshape(1, sgc["R"], 2 * FF), ((0, 0), (0, sgc["rpad"] - sgc["R"]), (0, 0)))
        return g[name] if g[name].ndim == 3 else g[name][None]

    flat4 = None
    for name in _LAYOUTS[layout]:
        if isinstance(g[name], (list, tuple)):
            for layer, t in enumerate(g[name]):
                flat4 = _slab_pack_grad(flat4, t[None], name, layer)
        else:
            flat4 = _slab_pack_grad(flat4, full(name), name)
    return flat4


_WEIGHTS = ["a_w_in", "a_b_f", "a_w_out", "b_w_q", "b_w_out", "kv_norm_g", "w_kv", "mix_norm_g", "ffn_norm_g", "ffn_w_up",
            "ffn_conv_w", "ffn_conv_b", "ffn_w_down", "final_norm_g"]


def kernel(x, a_w_in, a_b_f, a_w_out, b_w_q, b_w_out, kv_norm_g, w_kv, mix_norm_g, ffn_norm_g, ffn_w_up, ffn_conv_w, ffn_conv_b, ffn_w_down, final_norm_g, loss_target, m_a_w_in, m_a_b_f, m_a_w_out, m_b_w_q, m_b_w_out, m_kv_norm_g, m_w_kv, m_mix_norm_g, m_ffn_norm_g, m_ffn_w_up, m_ffn_conv_w, m_ffn_conv_b, m_ffn_w_down, m_final_norm_g, v_a_w_in, v_a_b_f, v_a_w_out, v_b_w_q, v_b_w_out, v_kv_norm_g, v_w_kv, v_mix_norm_g, v_ffn_norm_g, v_ffn_w_up, v_ffn_conv_w, v_ffn_conv_b, v_ffn_w_down, v_final_norm_g):
    w = dict(a_w_in=a_w_in, a_b_f=a_b_f, a_w_out=a_w_out, b_w_q=b_w_q, b_w_out=b_w_out, kv_norm_g=kv_norm_g, w_kv=w_kv,
             mix_norm_g=mix_norm_g, ffn_norm_g=ffn_norm_g, ffn_w_up=ffn_w_up, ffn_conv_w=ffn_conv_w, ffn_conv_b=ffn_conv_b,
             ffn_w_down=ffn_w_down, final_norm_g=final_norm_g)
    m = dict(a_w_in=m_a_w_in, a_b_f=m_a_b_f, a_w_out=m_a_w_out, b_w_q=m_b_w_q, b_w_out=m_b_w_out, kv_norm_g=m_kv_norm_g,
             w_kv=m_w_kv, mix_norm_g=m_mix_norm_g, ffn_norm_g=m_ffn_norm_g, ffn_w_up=m_ffn_w_up, ffn_conv_w=m_ffn_conv_w,
             ffn_conv_b=m_ffn_conv_b, ffn_w_down=m_ffn_w_down, final_norm_g=m_final_norm_g)
    v = dict(a_w_in=v_a_w_in, a_b_f=v_a_b_f, a_w_out=v_a_w_out, b_w_q=v_b_w_q, b_w_out=v_b_w_out, kv_norm_g=v_kv_norm_g,
             w_kv=v_w_kv, mix_norm_g=v_mix_norm_g, ffn_norm_g=v_ffn_norm_g, ffn_w_up=v_ffn_w_up, ffn_conv_w=v_ffn_conv_w,
             ffn_conv_b=v_ffn_conv_b, ffn_w_down=v_ffn_w_down, final_norm_g=v_final_norm_g)

    c_arr = lax.axis_index("c").astype(jnp.int32).reshape(1)
    k_arr = (2 * lax.axis_index("x") + lax.axis_index("y")).astype(jnp.int32).reshape(1)
    xi, yi = lax.axis_index("x"), lax.axis_index("y")
    late_slots = jnp.stack([jnp.where(k == k_arr[0], OWN_SLOT, 2 * ((k & 1) ^ yi) + ((k >> 1) ^ xi) - 1)
                            for k in range(N_CHIPS)]).astype(jnp.int32)
    w_late = _pack_weights(w, "late")
    land = lax.empty((OWN_SLOT,) + w_late.shape, BF16)
    send_sems, recv_sems, w_thru, land_thru, token = _late_gather_start(w_late, land)
    w_early = _pack_weights(w, "early")
    early = _allgather_shards(w_early, _place_own(w_early, k_arr, "early_place_own"))
    p = _early_weights(early, jnp.arange(N_CHIPS, dtype=jnp.int32))
    cb = ffn_conv_b.reshape(2, 2, 1, FF)
    p.update(a_b_f=a_b_f, kv_norm_g=kv_norm_g, mix_norm_g=mix_norm_g + token[0, 0], ffn_norm_g=ffn_norm_g,
             final_norm_g=final_norm_g, conv_b=cb)

    def late_weights(after):
        own, landed = _late_gather_wait(send_sems, recv_sems, w_thru, land_thru, after)
        return _late_weights(landed, late_slots, own)

    started = {}

    def late_grads_ready(g_so_far):
        gflat = _shard_grads(g_so_far, "grad_late")
        land = lax.empty((N_CHIPS, gflat.shape[1] // 2, FLAT_W), F32)
        *handles, token = _sibling_swap_start(gflat, land)
        started["swap"] = handles
        return token

    def late_grads_continue(after):
        gflat, other = _sibling_swap_wait(*started["swap"], after)
        pair = _pair_sum(gflat, other, c_arr, "late")
        land = lax.empty((3,) + pair.shape[1:], BF16)
        *handles, token = _chip_exchange_start(pair, land, "late")
        started["handles"] = handles
        return token

    loss_part, grad_x, g = _local_step(x[0], loss_target[0], p, late_weights, late_grads_ready, late_grads_continue)

    gflat = _shard_grads(g, "grad_early")
    pair_e = _pair_sum(gflat, _sibling_swap_half(gflat, "early"), c_arr, "early")
    *early_handles, token = _chip_exchange_start(pair_e, lax.empty((3,) + pair_e.shape[1:], BF16), "early")

    big = [{}, {}, {}, {}]

    def adamw_group(layout, g_mine):
        g_other = _sibling_send(g_mine, layout)
        for name in _LAYOUTS[layout]:
            sg = _seg(name)
            res = _adamw_shard(_native3(w[name], name), _native3(m[name], name), _native3(v[name], name), g_mine,
                               g_other, c_arr, name)
            for store, t in zip(big, res):
                store[name] = t[:, :sg["R"], :].reshape(_SHARD_SHAPES[name])
        return res[1]

    pair, landed = _chip_exchange_wait(*started["handles"], token, "late")
    last = adamw_group("grad_late", _chip_sum(pair, landed, k_arr, "late"))
    small, loss = _unpack_small(_allreduce_small(_pack_small(g, loss_part)))
    dws, mns, vns = _adamw(_pack_small(w), _pack_small(small), _pack_small(m), _pack_small(v), "adamw_small")
    pair_e, landed_e = _chip_exchange_wait(*early_handles, last, "early")
    adamw_group("grad_early", _chip_sum(pair_e, landed_e, k_arr, "early"))
    sml = [small] + [_unpack_small(t)[0] for t in (dws, mns, vns)]
    outs = [loss, grad_x[None]]
    for b, s in zip(big, sml):
        outs += [b[n] if n in b else s[n] for n in _WEIGHTS]
    return tuple(outs)
```

```python
import numpy as np
import jax
import jax.numpy as jnp
from jax import lax
from jax.experimental import pallas as pl
from jax.experimental.pallas import tpu as pltpu

F32 = jnp.float32
BF16 = jnp.bfloat16
MESH = pl.DeviceIdType.MESH

S = 4096
D = 1024
A_HEADS = 16
HEAD_DIM = 64
QKV_W = 3 * A_HEADS * HEAD_DIM
B_GROUPS = ((128, 1), (512, 4), (2048, 16))
B_HPG = 8
B_Q_W = 3 * B_HPG * HEAD_DIM
B_OUT_W = B_HPG * HEAD_DIM
B_WIN = 128
FF = 2816
RMS_EPS = 1e-6
SCALE = HEAD_DIM ** -0.5
N_CHIPS = 4

ADAM_LR, ADAM_B1, ADAM_B2, ADAM_EPS, ADAM_WD, ADAM_STEP = 0.001, 0.9, 0.999, 1e-08, 0.01, 10

V7X_VMEM_LIMIT = 48 * 1024 * 1024
LANES = 128
NEG_INF = float("-inf")

FLAT_W = LANES
_SEGS = (("ffn_w_down", 2, 704, 1024, 704), ("a_w_in", 1, 1024, 772, 1024), ("a_w_out", 1, 256, 1024, 256),
         ("b_w_q", 1, 1024, 384, 1024), ("b_w_out", 1, 512, 256, 512), ("w_kv", 1, 1024, 768, 1024),
         ("ffn_w_up", 2, 1024, 1408, 1024), ("ffn_conv_w", 1, 6, 1408, 16))
CONV_TERMS = 3


def _seg_rows(name, L, R, C, rpad):
    return (CONV_TERMS if name == "ffn_conv_w" else 1) * L * (-(-C // FLAT_W)) * rpad


SMALL_ROWS = 24


def _cparams(sem=None, **kw):
    return pltpu.CompilerParams(dimension_semantics=sem, vmem_limit_bytes=V7X_VMEM_LIMIT, **kw)


_DN = {"nn": (((1,), (0,)), ((), ())), "nt": (((1,), (1,)), ((), ())), "tn": (((0,), (0,)), ((), ()))}


def _mm(a, b, *, mode, tm, tn, tk, name, out_dtype=F32, res=None, a_split=0, b_split=0, o_split=0):
    if mode == "tn":
        K = a.shape[0]
        M = a.shape[1]
    else:
        M = a.shape[-2]
        K = a.shape[-1] * (2 if a_split else 1)
    if mode == "nt":
        N = b.shape[0]
    else:
        N = b.shape[-1] * (2 if b_split else 1)
    assert M % tm == 0 and N % tn == 0 and K % tk == 0, (name, M, N, K, tm, tn, tk)
    nk = K // tk

    if mode == "tn":
        a_spec = pl.BlockSpec((tk, tm), lambda i, j, k: (k, i))
    elif a_split:
        a_spec = pl.BlockSpec((None, tm, tk), lambda i, j, k: (k // a_split, i, k % a_split))
    else:
        a_spec = pl.BlockSpec((tm, tk), lambda i, j, k: (i, k))
    if mode == "nt":
        b_spec = pl.BlockSpec((tn, tk), lambda i, j, k: (j, k))
    elif b_split:
        b_spec = pl.BlockSpec((None, tk, tn), lambda i, j, k: (j // b_split, k, j % b_split))
    else:
        b_spec = pl.BlockSpec((tk, tn), lambda i, j, k: (k, j))
    if o_split:
        o_spec = pl.BlockSpec((None, tm, tn), lambda i, j, k: (j // o_split, i, j % o_split))
        out_shape = jax.ShapeDtypeStruct((2, M, N // 2), out_dtype)
    else:
        o_spec = pl.BlockSpec((tm, tn), lambda i, j, k: (i, j))
        out_shape = jax.ShapeDtypeStruct((M, N), out_dtype)
    in_specs = [a_spec, b_spec]
    args = [a, b]
    if res is not None:
        in_specs.append(pl.BlockSpec((tm, tn), lambda i, j, k: (i, j)))
        args.append(res)

    def body(*refs):
        if res is not None:
            a_ref, b_ref, r_ref, o_ref = refs[:4]
        else:
            a_ref, b_ref, o_ref = refs[:3]
            r_ref = None
        p = lax.dot_general(a_ref[...].astype(BF16), b_ref[...].astype(BF16), _DN[mode], preferred_element_type=F32)

        def finish(r):
            if r_ref is not None:
                r = r + r_ref[...]
            o_ref[...] = r.astype(out_dtype)

        if nk == 1:
            finish(p)
        else:
            acc = refs[-1]
            k = pl.program_id(2)

            @pl.when(k == 0)
            def _():
                acc[...] = p

            @pl.when(k > 0)
            def _():
                acc[...] += p

            @pl.when(k == nk - 1)
            def _():
                finish(acc[...])

    return pl.pallas_call(
        body, out_shape=out_shape, grid=(M // tm, N // tn, nk), in_specs=in_specs, out_specs=o_spec,
        scratch_shapes=[pltpu.VMEM((tm, tn), F32)] if nk > 1 else [],
        compiler_params=_cparams(("parallel", "parallel", "arbitrary")), name=name,
    )(*args)


NORM_ROWS = 256


def _rms_fwd(x, gains, name):
    n = len(gains)

    def body(x_ref, *refs):
        xv = x_ref[...]
        y = xv * lax.rsqrt(jnp.mean(xv * xv, axis=-1, keepdims=True) + RMS_EPS)
        for g_ref, o_ref in zip(refs[:n], refs[n:]):
            o_ref[...] = (y * g_ref[...]).astype(BF16)

    row = pl.BlockSpec((NORM_ROWS, D), lambda i: (i, 0))
    return pl.pallas_call(
        body, out_shape=[jax.ShapeDtypeStruct((S, D), BF16)] * n, grid=(S // NORM_ROWS,),
        in_specs=[row] + [pl.BlockSpec((1, D), lambda i: (0, 0))] * n, out_specs=[row] * n,
        compiler_params=_cparams(("parallel",)), name=name,
    )(x, *[g.reshape(1, D) for g in gains])


def _rms_bwd(x, dres, pairs, name):
    n = len(pairs)

    def body(*refs):
        x_ref, dres_ref = refs[0], refs[1]
        g_refs = refs[2:2 + 2 * n:2]
        dh_refs = refs[3:3 + 2 * n:2]
        dx_ref, dxb_ref = refs[2 + 2 * n], refs[3 + 2 * n]
        dg_refs = refs[4 + 2 * n:]
        i = pl.program_id(0)
        xv = x_ref[...]
        r = lax.rsqrt(jnp.mean(xv * xv, axis=-1, keepdims=True) + RMS_EPS)
        y = xv * r
        dx = dres_ref[...]
        for g_ref, dh_ref, dg_ref in zip(g_refs, dh_refs, dg_refs):
            dh = dh_ref[...]
            dy = dh * g_ref[...]
            dx = dx + r * (dy - y * jnp.mean(dy * y, axis=-1, keepdims=True))
            part = jnp.sum(dh * y, axis=0, keepdims=True)

            @pl.when(i == 0)
            def _():
                dg_ref[...] = part

            @pl.when(i > 0)
            def _():
                dg_ref[...] += part

        dx_ref[...] = dx
        dxb_ref[...] = dx.astype(BF16)

    row = pl.BlockSpec((NORM_ROWS, D), lambda i: (i, 0))
    vec = pl.BlockSpec((1, D), lambda i: (0, 0))
    in_specs = [row, row]
    args = [x, dres]
    for g, dh in pairs:
        in_specs += [vec, row]
        args += [g.reshape(1, D), dh]
    outs = pl.pallas_call(
        body,
        out_shape=[jax.ShapeDtypeStruct((S, D), F32), jax.ShapeDtypeStruct((S, D), BF16)]
        + [jax.ShapeDtypeStruct((1, D), F32)] * n,
        grid=(S // NORM_ROWS,), in_specs=in_specs, out_specs=[row, row] + [vec] * n,
        compiler_params=_cparams(("arbitrary",)), name=name,
    )(*args)
    return outs[0], outs[1], list(outs[2:])


def _loss_head(x, g, target, name):
    def body(x_ref, g_ref, t_ref, loss_ref, dx_ref, dxb_ref, dg_ref):
        i = pl.program_id(0)
        xv = x_ref[...]
        gv = g_ref[...]
        r = lax.rsqrt(jnp.mean(xv * xv, axis=-1, keepdims=True) + RMS_EPS)
        y = xv * r
        err = y * gv - t_ref[...]
        lpart = jnp.broadcast_to(jnp.sum(err * err, keepdims=True) * (0.5 / D), (1, LANES))
        dh = err * (1.0 / D)
        dy = dh * gv
        dx = r * (dy - y * jnp.mean(dy * y, axis=-1, keepdims=True))
        part = jnp.sum(dh * y, axis=0, keepdims=True)

        @pl.when(i == 0)
        def _():
            dg_ref[...] = part
            loss_ref[...] = lpart

        @pl.when(i > 0)
        def _():
            dg_ref[...] += part
            loss_ref[...] += lpart

        dx_ref[...] = dx
        dxb_ref[...] = dx.astype(BF16)

    row = pl.BlockSpec((NORM_ROWS, D), lambda i: (i, 0))
    vec = pl.BlockSpec((1, D), lambda i: (0, 0))
    return pl.pallas_call(
        body,
        out_shape=[jax.ShapeDtypeStruct((1, LANES), F32), jax.ShapeDtypeStruct((S, D), F32),
                   jax.ShapeDtypeStruct((S, D), BF16), jax.ShapeDtypeStruct((1, D), F32)],
        grid=(S // NORM_ROWS,), in_specs=[row, vec, row],
        out_specs=[pl.BlockSpec((1, LANES), lambda i: (0, 0)), row, row, vec],
        compiler_params=_cparams(("arbitrary",)), name=name,
    )(x, g.reshape(1, D), target)


SCAN_ROWS = 256


def _split3(v):
    hi = v.astype(BF16)
    r1 = v - hi.astype(F32)
    mid = r1.astype(BF16)
    lo = (r1 - mid.astype(F32)).astype(BF16)
    return hi, mid, lo


def _tri_dot(tri, v):
    hi, mid, lo = _split3(v)
    dn = _DN["nn"]
    return (lax.dot_general(tri, hi, dn, preferred_element_type=F32)
            + lax.dot_general(tri, mid, dn, preferred_element_type=F32)
            + lax.dot_general(tri, lo, dn, preferred_element_type=F32))


def _log_sigmoid(z):
    return jnp.minimum(z, 0.0) - jnp.log(1.0 + jnp.exp(-jnp.abs(z)))


GATE_LANES = 6


def _gate_lane_tables():
    pq = np.zeros((3 * LANES, A_HEADS * HEAD_DIM), np.float32)
    pk = np.zeros((3 * LANES, A_HEADS * HEAD_DIM), np.float32)
    one_q = np.zeros((1, A_HEADS * HEAD_DIM), np.float32)
    one_k = np.zeros((1, A_HEADS * HEAD_DIM), np.float32)
    for h in range(A_HEADS):
        pos = (h // 2) * LANES + (HEAD_DIM if h % 2 == 0 else 0)
        for term in range(3):
            pq[term * LANES + h, pos + term] = 1.0
            pk[term * LANES + h, pos + 3 + term] = -1.0
        one_q[0, pos + 3:pos + GATE_LANES] = 1.0
        one_k[0, pos:pos + 3] = 1.0
    return jnp.asarray(pq, BF16), jnp.asarray(pk, BF16), jnp.asarray(one_q), jnp.asarray(one_k)


def _fgate_fwd(pf, bias, name):
    tri = jnp.tril(jnp.ones((SCAN_ROWS, SCAN_ROWS), F32)).astype(BF16)
    pq, pk, one_q, one_k = _gate_lane_tables()

    def body(pf_ref, b_ref, tri_ref, pq_ref, pk_ref, oq_ref, ok_ref, aq_ref, ak_ref, c_sc):
        carry = jnp.zeros((1, LANES), F32)
        for blk in range(S // SCAN_ROWS):
            rows = pl.ds(blk * SCAN_ROWS, SCAN_ROWS)
            lf = _log_sigmoid(pf_ref[rows, :] + b_ref[...])
            c_sc[...] = _tri_dot(tri_ref[...], lf) + carry
            carry = c_sc[pl.ds(SCAN_ROWS - 1, 1), :]
            terms = jnp.concatenate(_split3(c_sc[...]), axis=1)
            aq = lax.dot_general(terms, pq_ref[...], _DN["nn"], preferred_element_type=F32) + oq_ref[...]
            ak = lax.dot_general(terms, pk_ref[...], _DN["nn"], preferred_element_type=F32) + ok_ref[...]
            aq_ref[rows, :] = aq.astype(BF16)
            ak_ref[rows, :] = ak.astype(BF16)

    wide = jax.ShapeDtypeStruct((S, A_HEADS * HEAD_DIM), BF16)
    return pl.pallas_call(
        body, out_shape=[wide, wide], scratch_shapes=[pltpu.VMEM((SCAN_ROWS, LANES), F32)],
        compiler_params=_cparams(), name=name,
    )(pf, bias, tri, pq, pk, one_q, one_k)


def _fgate_bwd(pf, bias, dc_key, dc_query, name):
    triu = jnp.triu(jnp.ones((SCAN_ROWS, SCAN_ROWS), F32)).astype(BF16)

    def body(pf_ref, b_ref, dck_ref, dcq_ref, tri_ref, dpf_ref, db_ref, dlf_ref):
        carry = jnp.zeros((1, LANES), F32)
        db = jnp.zeros((1, LANES), F32)
        lane = lax.broadcasted_iota(jnp.int32, (SCAN_ROWS, LANES), 1)
        for blk in reversed(range(S // SCAN_ROWS)):
            rows = pl.ds(blk * SCAN_ROWS, SCAN_ROWS)
            dc = dck_ref[rows, :] + dcq_ref[rows, :]
            dlf_ref[rows, :] = _tri_dot(tri_ref[...], dc) + carry
            carry = dlf_ref[pl.ds(blk * SCAN_ROWS, 1), :]
            z = pf_ref[rows, :] + b_ref[...]
            e = jnp.exp(-jnp.abs(z))
            sig_neg = jnp.where(z >= 0.0, e, 1.0) / (1.0 + e)
            dz = jnp.where(lane < A_HEADS, dlf_ref[rows, :] * sig_neg, 0.0)
            dpf_ref[rows, :] = dz.astype(BF16)
            db = db + jnp.sum(dz, axis=0, keepdims=True)
        db_ref[...] = db

    return pl.pallas_call(
        body, out_shape=[jax.ShapeDtypeStruct((S, LANES), BF16), jax.ShapeDtypeStruct((1, LANES), F32)],
        scratch_shapes=[pltpu.VMEM((S, LANES), F32)],
        compiler_params=_cparams(), name=name,
    )(pf, bias, dc_key, dc_query, triu)


FOX_T = 512


def _first_head(shape):
    return lax.broadcasted_iota(jnp.int32, shape, len(shape) - 1) < HEAD_DIM


def _each_head(x, lo):
    zero = jnp.zeros_like(x)
    return jnp.where(lo, x, zero), jnp.where(lo, zero, x)


def _fox_pair_fwd(qkv, aug_q, aug_k, name):
    T = FOX_T
    nq = S // T
    NP = A_HEADS // 2

    def body(q_ref, k_ref, v_ref, aq_ref, ak_ref, o_ref, lse_ref, m_sc, l_sc, acc_sc):
        i = pl.program_id(1)
        j = pl.program_id(2)
        lo = _first_head((T, LANES))

        @pl.when(j == 0)
        def _():
            m_sc[...] = jnp.full((2, T, LANES), NEG_INF, F32)
            l_sc[...] = jnp.zeros((2, T, LANES), F32)
            acc_sc[...] = jnp.zeros((T, LANES), F32)

        def step(diagonal):
            qs = q_ref[...] * jnp.asarray(SCALE, BF16)
            aq, ak, kv = aq_ref[...], ak_ref[...], k_ref[...]
            q2 = (jnp.where(lo, qs, aq), jnp.where(lo, aq, qs))
            k2 = (jnp.where(lo, kv, ak), jnp.where(lo, ak, kv))
            if diagonal:
                causal = lax.broadcasted_iota(jnp.int32, (T, T), 0) >= lax.broadcasted_iota(jnp.int32, (T, T), 1)
            pv, alphas = None, []
            for h, vh in enumerate(_each_head(v_ref[...], lo)):
                s = lax.dot_general(q2[h], k2[h], _DN["nt"], preferred_element_type=F32)
                if diagonal:
                    s = jnp.where(causal, s, NEG_INF)
                m_prev = m_sc[h]
                m_new = jnp.maximum(m_prev, jnp.max(s, axis=1, keepdims=True))
                alpha = jnp.exp(m_prev - m_new)
                p = jnp.exp(s - jnp.tile(m_new, (1, T // LANES)))
                l_sc[h] = alpha * l_sc[h] + jnp.sum(p, axis=1, keepdims=True)
                m_sc[h] = m_new
                d = lax.dot_general(p.astype(BF16), vh, _DN["nn"], preferred_element_type=F32)
                pv = d if pv is None else pv + d
                alphas.append(alpha)
            acc_sc[...] = jnp.where(lo, alphas[0], alphas[1]) * acc_sc[...] + pv

        @pl.when(j < i)
        def _():
            step(False)

        @pl.when(j == i)
        def _():
            step(True)
            o_ref[...] = (acc_sc[...] * jnp.where(lo, 1.0 / l_sc[0], 1.0 / l_sc[1])).astype(BF16)
            for h in range(2):
                lse_ref[h] = (m_sc[h] + jnp.log(l_sc[h]))[:, 0:1]

    qs_ = pl.BlockSpec((T, LANES), lambda p, i, j: (i, p))
    ks = pl.BlockSpec((T, LANES), lambda p, i, j: (jnp.minimum(i, j), NP + p))
    vs = pl.BlockSpec((T, LANES), lambda p, i, j: (jnp.minimum(i, j), 2 * NP + p))
    aks = pl.BlockSpec((T, LANES), lambda p, i, j: (jnp.minimum(i, j), p))
    col = pl.BlockSpec((2, T, 1), lambda p, i, j: (p, i, 0))
    return pl.pallas_call(
        body, out_shape=[jax.ShapeDtypeStruct((S, A_HEADS * HEAD_DIM), BF16), jax.ShapeDtypeStruct((A_HEADS, S, 1), F32)],
        grid=(NP, nq, nq), in_specs=[qs_, ks, vs, qs_, aks], out_specs=[qs_, col],
        scratch_shapes=[pltpu.VMEM((2, T, LANES), F32), pltpu.VMEM((2, T, LANES), F32), pltpu.VMEM((T, LANES), F32)],
        compiler_params=_cparams(("parallel", "parallel", "arbitrary")), name=name,
    )(qkv, qkv, qkv, aug_q, aug_k)


def _fox_pair_bwd(qkv, do, lse_row, delta_row, aug_q, aug_k, name):
    T = FOX_T
    nq = S // T
    NP = A_HEADS // 2

    def body(q_ref, k_ref, v_ref, do_ref, lse_ref, dl_ref, aq_ref, ak_ref, dq_ref, dk_ref, dv_ref, dc_ref, dcq_ref,
             dq_sc, dk_sc, dv_sc, dc_sc):
        j = pl.program_id(1)
        i = pl.program_id(2)
        lo = _first_head((T, LANES))

        @pl.when(jnp.logical_and(j == 0, i == 0))
        def _():
            dq_sc[...] = jnp.zeros((S, LANES), F32)
            dcq_ref[...] = jnp.zeros((2, nq, 1, T), F32)

        @pl.when(i == j)
        def _():
            dk_sc[...] = jnp.zeros((T, LANES), F32)
            dv_sc[...] = jnp.zeros((T, LANES), F32)
            dc_sc[...] = jnp.zeros((2, T, 1), F32)

        def step(diagonal):
            qv = q_ref[...]
            kv = k_ref[...]
            dov = do_ref[...].astype(BF16)
            qs = qv * jnp.asarray(SCALE, BF16)
            aq, ak = aq_ref[...], ak_ref[...]
            q2 = (jnp.where(lo, qs, aq), jnp.where(lo, aq, qs))
            k2 = (jnp.where(lo, kv, ak), jnp.where(lo, ak, kv))
            if diagonal:
                causal = lax.broadcasted_iota(jnp.int32, (T, T), 1) >= lax.broadcasted_iota(jnp.int32, (T, T), 0)
            dv = dk = dq = None
            for h, (kh, vh, qh, doh) in enumerate(zip(_each_head(kv, lo), _each_head(v_ref[...], lo),
                                                      _each_head(qv, lo), _each_head(dov, lo))):
                st = lax.dot_general(k2[h], q2[h], _DN["nt"], preferred_element_type=F32)
                if diagonal:
                    st = jnp.where(causal, st, NEG_INF)
                pt = jnp.exp(st - lse_ref[h])
                d = lax.dot_general(pt.astype(BF16), doh, _DN["nn"], preferred_element_type=F32)
                dv = d if dv is None else dv + d
                dpt = lax.dot_general(vh, dov, _DN["nt"], preferred_element_type=F32)
                dst = pt * (dpt - dl_ref[h])
                dc_sc[h] -= jnp.sum(dst, axis=1, keepdims=True)
                dcq_ref[h, i] += jnp.sum(dst, axis=0, keepdims=True)
                dsb = (dst * SCALE).astype(BF16)
                d = lax.dot_general(dsb, qh, _DN["nn"], preferred_element_type=F32)
                dk = d if dk is None else dk + d
                d = lax.dot_general(dsb, kh, _DN["tn"], preferred_element_type=F32)
                dq = d if dq is None else dq + d
            dv_sc[...] += dv
            dk_sc[...] += dk
            rows = pl.ds(pl.multiple_of(i * T, T), T)
            dq_sc[rows, :] += dq

        @pl.when(i > j)
        def _():
            step(False)

        @pl.when(i == j)
        def _():
            step(True)

        @pl.when(i == nq - 1)
        def _():
            dk_ref[...] = dk_sc[...].astype(BF16)
            dv_ref[...] = dv_sc[...].astype(BF16)
            dc_ref[...] = dc_sc[...]

        @pl.when(jnp.logical_and(j == nq - 1, i == nq - 1))
        def _():
            dq_ref[...] = dq_sc[...].astype(BF16)

    qs = pl.BlockSpec((T, LANES), lambda p, j, i: (jnp.maximum(i, j), p))
    qrow = pl.BlockSpec((2, 1, T), lambda p, j, i: (p, 0, jnp.maximum(i, j)))
    ks = pl.BlockSpec((T, LANES), lambda p, j, i: (j, NP + p))
    vs = pl.BlockSpec((T, LANES), lambda p, j, i: (j, 2 * NP + p))
    kout = pl.BlockSpec((T, LANES), lambda p, j, i: (j, p))
    kcol = pl.BlockSpec((2, T, 1), lambda p, j, i: (p, j, 0))
    dqs = pl.BlockSpec((S, LANES), lambda p, j, i: (0, p))
    dcqs = pl.BlockSpec((2, nq, 1, T), lambda p, j, i: (p, 0, 0, 0))
    wide = jax.ShapeDtypeStruct((S, A_HEADS * HEAD_DIM), BF16)
    return pl.pallas_call(
        body,
        out_shape=[wide, wide, wide, jax.ShapeDtypeStruct((A_HEADS, S, 1), F32),
                   jax.ShapeDtypeStruct((A_HEADS, nq, 1, T), F32)],
        grid=(NP, nq, nq), in_specs=[qs, ks, vs, qs, qrow, qrow, qs, kout], out_specs=[dqs, kout, kout, kcol, dcqs],
        scratch_shapes=[pltpu.VMEM((S, LANES), F32), pltpu.VMEM((T, LANES), F32), pltpu.VMEM((T, LANES), F32),
                        pltpu.VMEM((2, T, 1), F32)],
        compiler_params=_cparams(("parallel", "arbitrary", "arbitrary")), name=name,
    )(qkv, qkv, qkv, do, lse_row, delta_row, aug_q, aug_k)


def _pair_rowdot(a, b, name):
    n = a.shape[1] // HEAD_DIM
    T = 1024

    def body(a_ref, b_ref, o_ref):
        prod = a_ref[...].astype(F32) * b_ref[...].astype(F32)
        lo = _first_head(prod.shape)
        o_ref[0] = jnp.sum(jnp.where(lo, prod, 0.0), axis=1, keepdims=True)
        o_ref[1] = jnp.sum(jnp.where(lo, 0.0, prod), axis=1, keepdims=True)

    blk = pl.BlockSpec((T, LANES), lambda p, i: (i, p))
    return pl.pallas_call(
        body, out_shape=jax.ShapeDtypeStruct((n, S, 1), F32), grid=(n // 2, S // T), in_specs=[blk, blk],
        out_specs=pl.BlockSpec((2, T, 1), lambda p, i: (p, i, 0)),
        compiler_params=_cparams(("parallel", "parallel")), name=name,
    )(a, b)


W = B_WIN
N_HG = 3 * B_HPG
N_BLK = S // W


def _dil_tables():
    slopes = np.exp2((-8.0 * np.arange(1, N_HG + 1, dtype=np.float32) / N_HG).astype(np.float32)).astype(np.float32)
    dil = np.repeat(np.array([d for _, d in B_GROUPS], np.float32), B_HPG)
    coef = (slopes * dil).astype(np.float32)
    nbs = np.repeat(np.array([S // d // W for _, d in B_GROUPS], np.int32), B_HPG)
    return jnp.asarray(coef), jnp.asarray(nbs)


DIL_SUB = 8
DIL_ROWS = DIL_SUB * W
DIL_STEPS = S // DIL_ROWS


def _dil_bias(coef, transposed):
    row = lax.broadcasted_iota(jnp.int32, (W, 2 * W), 0)
    col = lax.broadcasted_iota(jnp.int32, (W, 2 * W), 1)
    dist = (col - row) if transposed else (row + W - col)
    valid = jnp.logical_and(dist >= 0, dist <= W)
    return jnp.where(valid, -coef * dist.astype(F32), NEG_INF), col


NPG = B_HPG // 2
GROUP_W = B_HPG * HEAD_DIM


def _dil_pair_specs(qoff, koff, voff):
    prev_blk = lambda n: jnp.maximum(n * DIL_SUB - 1, 0)
    next_blk = lambda n: jnp.minimum((n + 1) * DIL_SUB, N_BLK - 1)
    return dict(
        o=pl.BlockSpec((DIL_ROWS, LANES), lambda h, n: (n, h)),
        o_next=pl.BlockSpec((W, LANES), lambda h, n: (next_blk(n), h)),
        q=pl.BlockSpec((DIL_ROWS, LANES), lambda h, n: (n, qoff + h)),
        q_next=pl.BlockSpec((W, LANES), lambda h, n: (next_blk(n), qoff + h)),
        k=pl.BlockSpec((DIL_ROWS, LANES), lambda h, n: (n, koff + h)),
        k_prev=pl.BlockSpec((W, LANES), lambda h, n: (prev_blk(n), koff + h)),
        v=pl.BlockSpec((DIL_ROWS, LANES), lambda h, n: (n, voff + h)),
        v_prev=pl.BlockSpec((W, LANES), lambda h, n: (prev_blk(n), voff + h)),
        col=pl.BlockSpec((2, DIL_ROWS, 1), lambda h, n: (h, n, 0)),
        col2=pl.BlockSpec((2, DIL_ROWS, 1), lambda h, n: (NPG + h, n, 0)),
        row=pl.BlockSpec((2, 1, DIL_ROWS), lambda h, n: (h, 0, n)),
        row_next=pl.BlockSpec((2, 1, W), lambda h, n: (h, 0, next_blk(n))),
        row2=pl.BlockSpec((2, 1, DIL_ROWS), lambda h, n: (NPG + h, 0, n)),
        row2_next=pl.BlockSpec((2, 1, W), lambda h, n: (NPG + h, 0, next_blk(n))),
        smem=pl.BlockSpec(memory_space=pltpu.SMEM))


def _dil_pair_fwd(g, q, k, v, qoff, koff, voff, name):
    coef_t, nbs_t = _dil_tables()

    def body(coef_ref, nbs_ref, q_ref, kh_ref, k_ref, vh_ref, v_ref, o_ref, lse_ref, kf, vf):
        hp = pl.program_id(0)
        n = pl.program_id(1)
        nbs = nbs_ref[B_HPG * g + 2 * hp]
        kf[0:W, :] = kh_ref[...]
        kf[W:, :] = k_ref[...]
        vf[0:W, :] = vh_ref[...]
        vf[W:, :] = v_ref[...]
        biases = [_dil_bias(coef_ref[B_HPG * g + 2 * hp + h], False) for h in range(2)]
        col = biases[0][1]
        lo = _first_head((W, LANES))
        lo2 = _first_head((2 * W, LANES))
        for b in range(DIL_SUB):
            first = lax.rem(n * DIL_SUB + b, nbs) == 0
            rows = slice(b * W, (b + 1) * W)
            both = slice(b * W, (b + 2) * W)
            qv = q_ref[rows, :]
            acc, inv = None, []
            for h, (kh, vh) in enumerate(zip(_each_head(kf[both, :], lo2), _each_head(vf[both, :], lo2))):
                s = lax.dot_general(qv, kh, _DN["nt"], preferred_element_type=F32) * SCALE + biases[h][0]
                s = jnp.where(jnp.logical_and(first, col < W), NEG_INF, s)
                m = jnp.max(s, axis=1, keepdims=True)
                p = jnp.exp(s - m)
                l = jnp.sum(p, axis=1, keepdims=True)
                d = lax.dot_general(p.astype(BF16), vh, _DN["nn"], preferred_element_type=F32)
                acc = d if acc is None else acc + d
                inv.append(1.0 / l)
                lse_ref[h, rows, :] = m + jnp.log(l)
            o_ref[rows, :] = acc * jnp.where(lo, inv[0], inv[1])

    sp = _dil_pair_specs(qoff, koff, voff)
    return pl.pallas_call(
        body, out_shape=[jax.ShapeDtypeStruct((S, GROUP_W), F32), jax.ShapeDtypeStruct((B_HPG, S, 1), F32)],
        grid=(NPG, DIL_STEPS), in_specs=[sp["smem"], sp["smem"], sp["q"], sp["k_prev"], sp["k"], sp["v_prev"], sp["v"]],
        out_specs=[sp["o"], sp["col"]], scratch_shapes=[pltpu.VMEM((DIL_ROWS + W, LANES), BF16)] * 2,
        compiler_params=_cparams(("parallel", "parallel")), name=name,
    )(coef_t, nbs_t, q, k, k, v, v)


def _dil_pair_merge(os, lses, name):
    T = 1024

    def body(o0_ref, o1_ref, o2_ref, l0_ref, l1_ref, l2_ref, om_ref, omb_ref, l_ref):
        lo = _first_head((T, LANES))
        weights = []
        for h in range(2):
            l0, l1, l2 = l0_ref[h], l1_ref[h], l2_ref[h]
            m = jnp.maximum(jnp.maximum(l0, l1), l2)
            e0, e1, e2 = jnp.exp(l0 - m), jnp.exp(l1 - m), jnp.exp(l2 - m)
            den = e0 + e1 + e2
            weights.append((e0 / den, e1 / den, e2 / den))
            l_ref[h] = m + jnp.log(den)
        om = (jnp.where(lo, weights[0][0], weights[1][0]) * o0_ref[...]
              + jnp.where(lo, weights[0][1], weights[1][1]) * o1_ref[...]
              + jnp.where(lo, weights[0][2], weights[1][2]) * o2_ref[...])
        om_ref[...] = om
        omb_ref[...] = om.astype(BF16)

    ob = pl.BlockSpec((T, LANES), lambda p, i: (i, p))
    lb = pl.BlockSpec((2, T, 1), lambda p, i: (p, i, 0))
    return pl.pallas_call(
        body,
        out_shape=[jax.ShapeDtypeStruct((S, B_OUT_W), F32), jax.ShapeDtypeStruct((S, B_OUT_W), BF16),
                   jax.ShapeDtypeStruct((B_HPG, S, 1), F32)],
        grid=(NPG, S // T), in_specs=[ob] * 3 + [lb] * 3, out_specs=[ob, ob, lb],
        compiler_params=_cparams(("parallel", "parallel")), name=name,
    )(*os, *lses)


def _dil_pair_dq(g, q, k, v, qoff, koff, voff, do, stats, name):
    coef_t, nbs_t = _dil_tables()

    def body(coef_ref, nbs_ref, q_ref, kh_ref, k_ref, vh_ref, v_ref, do_ref, l_ref, d_ref, dq_ref, kf, vf):
        hp = pl.program_id(0)
        n = pl.program_id(1)
        nbs = nbs_ref[B_HPG * g + 2 * hp]
        kf[0:W, :] = kh_ref[...]
        kf[W:, :] = k_ref[...]
        vf[0:W, :] = vh_ref[...]
        vf[W:, :] = v_ref[...]
        biases = [_dil_bias(coef_ref[B_HPG * g + 2 * hp + h], False) for h in range(2)]
        col = biases[0][1]
        lo2 = _first_head((2 * W, LANES))
        for b in range(DIL_SUB):
            first = lax.rem(n * DIL_SUB + b, nbs) == 0
            rows = slice(b * W, (b + 1) * W)
            both = slice(b * W, (b + 2) * W)
            qv = q_ref[rows, :]
            dov = do_ref[rows, :]
            acc = None
            for h, (kh, vh) in enumerate(zip(_each_head(kf[both, :], lo2), _each_head(vf[both, :], lo2))):
                s = lax.dot_general(qv, kh, _DN["nt"], preferred_element_type=F32) * SCALE + biases[h][0]
                s = jnp.where(jnp.logical_and(first, col < W), NEG_INF, s)
                p = jnp.exp(s - l_ref[h, rows, :])
                dp = lax.dot_general(dov, vh, _DN["nt"], preferred_element_type=F32)
                ds = (p * (dp - d_ref[h, rows, :]) * SCALE).astype(BF16)
                d = lax.dot_general(ds, kh, _DN["nn"], preferred_element_type=F32)
                acc = d if acc is None else acc + d
            dq_ref[rows, :] = acc.astype(BF16)

    sp = _dil_pair_specs(qoff, koff, voff)
    return pl.pallas_call(
        body, out_shape=jax.ShapeDtypeStruct((S, GROUP_W), BF16), grid=(NPG, DIL_STEPS),
        in_specs=[sp["smem"], sp["smem"], sp["q"], sp["k_prev"], sp["k"], sp["v_prev"], sp["v"], sp["o"], sp["col"],
                  sp["col2"]],
        out_specs=sp["o"], scratch_shapes=[pltpu.VMEM((DIL_ROWS + W, LANES), BF16)] * 2,
        compiler_params=_cparams(("parallel", "parallel")), name=name,
    )(coef_t, nbs_t, q, k, k, v, v, do, stats, stats)


def _dil_pair_dkv(g, q, k, v, qoff, koff, voff, do, stats, name):
    coef_t, nbs_t = _dil_tables()

    def body(coef_ref, nbs_ref, k_ref, v_ref, q_ref, qn_ref, do_ref, don_ref, l_ref, ln_ref, d_ref, dn_ref,
             dk_ref, dv_ref, qf, dof, lf, df):
        hp = pl.program_id(0)
        n = pl.program_id(1)
        nbs = nbs_ref[B_HPG * g + 2 * hp]
        qf[0:DIL_ROWS, :] = q_ref[...]
        qf[DIL_ROWS:, :] = qn_ref[...]
        dof[0:DIL_ROWS, :] = do_ref[...]
        dof[DIL_ROWS:, :] = don_ref[...]
        lf[:, :, 0:DIL_ROWS] = l_ref[...]
        lf[:, :, DIL_ROWS:] = ln_ref[...]
        df[:, :, 0:DIL_ROWS] = d_ref[...]
        df[:, :, DIL_ROWS:] = dn_ref[...]
        biases = [_dil_bias(coef_ref[B_HPG * g + 2 * hp + h], True) for h in range(2)]
        col = biases[0][1]
        lo = _first_head((W, LANES))
        lo2 = _first_head((2 * W, LANES))
        for b in range(DIL_SUB):
            no_next = lax.rem(n * DIL_SUB + b + 1, nbs) == 0
            rows = slice(b * W, (b + 1) * W)
            both = slice(b * W, (b + 2) * W)
            dd = dof[both, :]
            dk = dv = None
            for h, (kh, vh, qh, ddh) in enumerate(zip(_each_head(k_ref[rows, :], lo), _each_head(v_ref[rows, :], lo),
                                                      _each_head(qf[both, :], lo2), _each_head(dd, lo2))):
                st = lax.dot_general(kh, qh, _DN["nt"], preferred_element_type=F32) * SCALE + biases[h][0]
                st = jnp.where(jnp.logical_and(no_next, col >= W), NEG_INF, st)
                pt = jnp.exp(st - lf[h, :, both])
                d = lax.dot_general(pt.astype(BF16), ddh, _DN["nn"], preferred_element_type=F32)
                dv = d if dv is None else dv + d
                dpt = lax.dot_general(vh, dd, _DN["nt"], preferred_element_type=F32)
                dst = (pt * (dpt - df[h, :, both]) * SCALE).astype(BF16)
                d = lax.dot_general(dst, qh, _DN["nn"], preferred_element_type=F32)
                dk = d if dk is None else dk + d
            dk_ref[rows, :] = dk.astype(BF16)
            dv_ref[rows, :] = dv.astype(BF16)

    sp = _dil_pair_specs(qoff, koff, voff)
    wide = jax.ShapeDtypeStruct((S, GROUP_W), BF16)
    return pl.pallas_call(
        body, out_shape=[wide, wide], grid=(NPG, DIL_STEPS),
        in_specs=[sp["smem"], sp["smem"], sp["k"], sp["v"], sp["q"], sp["q_next"], sp["o"], sp["o_next"], sp["row"],
                  sp["row_next"], sp["row2"], sp["row2_next"]],
        out_specs=[sp["o"], sp["o"]],
        scratch_shapes=[pltpu.VMEM((DIL_ROWS + W, LANES), BF16)] * 2 + [pltpu.VMEM((2, 1, DIL_ROWS + W), F32)] * 2,
        compiler_params=_cparams(("parallel", "parallel")), name=name,
    )(coef_t, nbs_t, k, v, q, q, do, do, stats, stats, stats, stats)


FFN_ROWS = 512
FFN_COLS = 256
HALO = 8


def _shifted(u, halo, back):
    T = u.shape[0]
    rows = lax.broadcasted_iota(jnp.int32, u.shape, 0)
    if back:
        s1 = jnp.where(rows == 0, halo[HALO - 1:HALO, :], pltpu.roll(u, 1, 0))
        s2 = jnp.where(rows == 0, halo[HALO - 2:HALO - 1, :],
                       jnp.where(rows == 1, halo[HALO - 1:HALO, :], pltpu.roll(u, 2, 0)))
    else:
        s1 = jnp.where(rows == T - 1, halo[0:1, :], pltpu.roll(u, T - 1, 0))
        s2 = jnp.where(rows == T - 1, halo[1:2, :],
                       jnp.where(rows == T - 2, halo[0:1, :], pltpu.roll(u, T - 2, 0)))
    return s1, s2


def _conv_parts(u_ref, h_ref, w_ref, b_ref, first):
    out = []
    for p in range(2):
        u = u_ref[p]
        halo = jnp.where(first, 0.0, h_ref[p])
        u1, u2 = _shifted(u, halo, True)
        w = w_ref[p]
        out.append((w[0:1, :] * u2 + w[1:2, :] * u1 + w[2:3, :] * u + b_ref[p], u1, u2, u))
    return out


def _ffn_specs():
    T, C = FFN_ROWS, FFN_COLS
    blk = pl.BlockSpec((2, T, C), lambda j, i: (0, i, j))
    prev = pl.BlockSpec((2, HALO, C), lambda j, i: (0, jnp.maximum(i * (T // HALO) - 1, 0), j))
    nxt = pl.BlockSpec((2, HALO, C), lambda j, i: (0, jnp.minimum((i + 1) * (T // HALO), S // HALO - 1), j))
    wsp = pl.BlockSpec((2, 3, C), lambda j, i: (0, 0, j))
    bsp = pl.BlockSpec((2, 1, C), lambda j, i: (0, 0, j))
    one = pl.BlockSpec((T, C), lambda j, i: (i, j))
    return blk, prev, nxt, wsp, bsp, one


def _ffn_act_fwd(u, w, b, name):
    blk, prev, _, wsp, bsp, one = _ffn_specs()

    def body(u_ref, h_ref, w_ref, b_ref, o_ref):
        (a, _, _, _), (g, _, _, _) = _conv_parts(u_ref, h_ref, w_ref, b_ref, pl.program_id(1) == 0)
        o_ref[...] = (g / (1.0 + jnp.exp(-g)) * a).astype(BF16)

    return pl.pallas_call(
        body, out_shape=jax.ShapeDtypeStruct((S, FF), BF16), grid=(FF // FFN_COLS, S // FFN_ROWS),
        in_specs=[blk, prev, wsp, bsp], out_specs=one,
        compiler_params=_cparams(("parallel", "parallel")), name=name,
    )(u, u, w, b)


def _ffn_act_bwd(u, dact, w, b, name):
    blk, prev, _, wsp, bsp, one = _ffn_specs()

    def body(u_ref, h_ref, da_ref, w_ref, b_ref, duc_ref, dwb_ref):
        i = pl.program_id(1)
        (a, a1, a2, a0), (g, g1, g2, g0) = _conv_parts(u_ref, h_ref, w_ref, b_ref, i == 0)
        dact_v = da_ref[...]
        sg = 1.0 / (1.0 + jnp.exp(-g))
        d_a = dact_v * (g * sg)
        d_g = dact_v * a * (sg * (1.0 + g * (1.0 - sg)))
        duc_ref[0] = d_a
        duc_ref[1] = d_g

        @pl.when(i == 0)
        def _():
            dwb_ref[...] = jnp.zeros(dwb_ref.shape, F32)

        for p, (d, s2, s1, s0) in enumerate(((d_a, a2, a1, a0), (d_g, g2, g1, g0))):
            dwb_ref[p, 0:1, :] += jnp.sum(d * s2, axis=0, keepdims=True)
            dwb_ref[p, 1:2, :] += jnp.sum(d * s1, axis=0, keepdims=True)
            dwb_ref[p, 2:3, :] += jnp.sum(d * s0, axis=0, keepdims=True)
            dwb_ref[p, 3:4, :] += jnp.sum(d, axis=0, keepdims=True)

    return pl.pallas_call(
        body, out_shape=[jax.ShapeDtypeStruct((2, S, FF), F32), jax.ShapeDtypeStruct((2, 8, FF), F32)],
        grid=(FF // FFN_COLS, S // FFN_ROWS), in_specs=[blk, prev, one, wsp, bsp],
        out_specs=[blk, pl.BlockSpec((2, 8, FFN_COLS), lambda j, i: (0, 0, j))],
        compiler_params=_cparams(("parallel", "arbitrary")), name=name,
    )(u, u, dact, w, b)


def _ffn_conv_bwd(duc, w, name):
    blk, _, nxt, wsp, _, _ = _ffn_specs()
    last = S // FFN_ROWS - 1

    def body(d_ref, h_ref, w_ref, du_ref):
        is_last = pl.program_id(1) == last
        for p in range(2):
            d = d_ref[p]
            halo = jnp.where(is_last, 0.0, h_ref[p])
            d1, d2 = _shifted(d, halo, False)
            wv = w_ref[p]
            du_ref[p] = (wv[2:3, :] * d + wv[1:2, :] * d1 + wv[0:1, :] * d2).astype(BF16)

    return pl.pallas_call(
        body, out_shape=jax.ShapeDtypeStruct((2, S, FF), BF16), grid=(FF // FFN_COLS, S // FFN_ROWS),
        in_specs=[blk, nxt, wsp], out_specs=blk,
        compiler_params=_cparams(("parallel", "parallel")), name=name,
    )(duc, duc, w)


def _adam_update(w, gv, m, v):
    c1 = 1.0 / (1.0 - ADAM_B1 ** ADAM_STEP)
    c2 = 1.0 / (1.0 - ADAM_B2 ** ADAM_STEP)
    mn = ADAM_B1 * m + (1.0 - ADAM_B1) * gv
    vn = ADAM_B2 * v + (1.0 - ADAM_B2) * (gv * gv)
    return -ADAM_LR * ((mn * c1) / (jnp.sqrt(vn * c2) + ADAM_EPS) + ADAM_WD * w), mn, vn


def _adamw(w, g, m, v, name):
    rows = w.shape[0]
    T = 8
    for cand in (256, 128, 64, 32, 16, 8):
        if rows % cand == 0:
            T = cand
            break

    def body(w_ref, g_ref, m_ref, v_ref, d_ref, mo_ref, vo_ref):
        d_ref[...], mo_ref[...], vo_ref[...] = _adam_update(w_ref[...], g_ref[...], m_ref[...], v_ref[...])

    blk = pl.BlockSpec((T, w.shape[1]), lambda i: (i, 0))
    sds = jax.ShapeDtypeStruct(w.shape, F32)
    return pl.pallas_call(
        body, out_shape=[sds, sds, sds], grid=(rows // T,), in_specs=[blk] * 4, out_specs=[blk] * 3,
        compiler_params=_cparams(("parallel",)), name=name,
    )(w, g, m, v)


ANY = pl.BlockSpec(memory_space=pl.ANY)


def _place():
    x, y, c = lax.axis_index("x"), lax.axis_index("y"), lax.axis_index("c")
    chips = [(1 - x, y), (x, 1 - y), (1 - x, 1 - y)]
    return x, y, c, chips


def _place_own(w, slot_arr, name):
    rows = w.shape[0]
    T = 16
    for cand in (2048, 1024, 512, 256, 128, 64, 32, 16):
        if rows % cand == 0:
            T = cand
            break

    def body(k_ref, w_ref, o_ref):
        o_ref[...] = w_ref[...]

    return pl.pallas_call(
        body, out_shape=jax.ShapeDtypeStruct((N_CHIPS, rows, FLAT_W), w.dtype),
        grid_spec=pltpu.PrefetchScalarGridSpec(
            num_scalar_prefetch=1, grid=(rows // T,),
            in_specs=[pl.BlockSpec((T, FLAT_W), lambda i, k: (i, 0))],
            out_specs=pl.BlockSpec((None, T, FLAT_W), lambda i, k: (k[0], i, 0))),
        compiler_params=_cparams(("parallel",)), name=name,
    )(slot_arr, w)


def _allgather_shards(w, buf):
    half_rows = w.shape[0] // 2
    assert half_rows % 16 == 0

    def body(w_ref, buf_ref, g_ref, send_sems, recv_sems):
        x, y, c, chips = _place()
        myk = 2 * x + y
        sibling = (x, y, 1 - c)
        h0 = pl.multiple_of(c * half_rows, 16)
        h1 = pl.multiple_of((1 - c) * half_rows, 16)

        def half(k, start):
            return g_ref.at[k, pl.ds(start, half_rows), :]

        def rcopy(sem, src, dst, to):
            return pltpu.make_async_remote_copy(src_ref=src, dst_ref=dst, send_sem=send_sems.at[sem],
                                                recv_sem=recv_sems.at[sem], device_id=to, device_id_type=MESH)

        ici = [rcopy(r, w_ref.at[pl.ds(h0, half_rows), :], half(myk, h0), (*chip, c)) for r, chip in enumerate(chips)]
        for cp in ici:
            cp.start()
        ks = [2 * cx + cy for cx, cy in chips]
        fwd = [rcopy(3 + r, half(ks[r], h0), half(ks[r], h0), sibling) for r in range(3)]
        for r in range(3):
            rcopy(r, half(ks[r], h0), half(ks[r], h0), (*chips[r], c)).wait_recv()
            fwd[r].start()
        for r in range(3):
            rcopy(3 + r, half(ks[r], h1), half(ks[r], h1), sibling).wait_recv()
        for cp in ici + fwd:
            cp.wait_send()

    return pl.pallas_call(
        body, out_shape=jax.ShapeDtypeStruct(buf.shape, w.dtype), in_specs=[ANY, ANY], out_specs=ANY,
        scratch_shapes=[pltpu.SemaphoreType.DMA((6,)), pltpu.SemaphoreType.DMA((6,))],
        input_output_aliases={1: 0},
        compiler_params=pltpu.CompilerParams(has_side_effects=True), name="allgather_shards",
    )(w, buf)


HBM_SPEC = pl.BlockSpec(memory_space=pltpu.HBM)
SEM_SPEC = pl.BlockSpec(memory_space=pltpu.SEMAPHORE)
DATAFLOW = pltpu.SideEffectType.DATAFLOW_SIDE_EFFECTING
OWN_SLOT = 3


def _late_gather_start(w, land):
    def body(w_ref, land_ref, send_sems, recv_sems, w_thru, land_thru, token):
        x, y, c, chips = _place()
        for r, chip in enumerate(chips):
            pltpu.make_async_remote_copy(src_ref=w_ref, dst_ref=land_ref.at[r], send_sem=send_sems.at[r],
                                         recv_sem=recv_sems.at[r], device_id=(*chip, c), device_id_type=MESH).start()
        token[...] = jnp.zeros_like(token)

    return pl.pallas_call(
        body, name="late_gather_start",
        out_shape=(pltpu.SemaphoreType.DMA((3,)), pltpu.SemaphoreType.DMA((3,)), pltpu.HBM(w.shape, w.dtype),
                   pltpu.HBM(land.shape, land.dtype), jax.ShapeDtypeStruct((8, LANES), F32)),
        in_specs=(HBM_SPEC, HBM_SPEC),
        out_specs=(SEM_SPEC, SEM_SPEC, HBM_SPEC, HBM_SPEC, pl.BlockSpec(memory_space=pltpu.VMEM)),
        input_output_aliases={0: 2, 1: 3}, compiler_params=pltpu.CompilerParams(has_side_effects=DATAFLOW),
    )(pltpu.with_memory_space_constraint(w, pltpu.HBM), pltpu.with_memory_space_constraint(land, pltpu.HBM))


def _late_gather_wait(send_sems, recv_sems, w_thru, land_thru, after):
    def body(w_ref, land_ref, send_sems, recv_sems, after_ref, w_dead, got_ref):
        x, y, c, chips = _place()
        for r, chip in enumerate(chips):
            cp = pltpu.make_async_remote_copy(src_ref=w_ref, dst_ref=land_ref.at[r], send_sem=send_sems.at[r],
                                              recv_sem=recv_sems.at[r], device_id=(*chip, c), device_id_type=MESH)
            cp.wait_send()
            cp.wait_recv()

    return pl.pallas_call(
        body, name="late_gather_wait",
        out_shape=(pltpu.HBM(w_thru.shape, w_thru.dtype), pltpu.HBM(land_thru.shape, land_thru.dtype)),
        in_specs=(HBM_SPEC, HBM_SPEC, SEM_SPEC, SEM_SPEC, pl.BlockSpec(memory_space=pl.ANY)),
        out_specs=(HBM_SPEC, HBM_SPEC), input_output_aliases={0: 0, 1: 1},
        compiler_params=pltpu.CompilerParams(has_side_effects=DATAFLOW),
    )(w_thru, land_thru, send_sems, recv_sems, after)


def _flat_tile(rows):
    return next(t for t in (2048, 1024, 512, 256, 128, 64, 32, 16) if rows % t == 0)


def _sibling_swap_half(g, tag):
    half = g.shape[1] // 2

    def body(g_ref, o_ref, send_sem, recv_sem):
        x, y, c, _ = _place()
        theirs = pl.multiple_of((1 - c) * half, 8)
        cp = pltpu.make_async_remote_copy(src_ref=g_ref.at[:, pl.ds(theirs, half), :], dst_ref=o_ref,
                                          send_sem=send_sem, recv_sem=recv_sem, device_id=(x, y, 1 - c),
                                          device_id_type=MESH)
        cp.start()
        cp.wait()

    return pl.pallas_call(
        body, out_shape=jax.ShapeDtypeStruct((N_CHIPS, half, FLAT_W), F32), in_specs=[ANY], out_specs=ANY,
        scratch_shapes=[pltpu.SemaphoreType.DMA, pltpu.SemaphoreType.DMA],
        compiler_params=pltpu.CompilerParams(has_side_effects=True), name=f"rs_sibling_swap_{tag}",
    )(g)


def _sibling_swap_start(g, land):
    half = g.shape[1] // 2

    def body(g_ref, land_ref, send_sem, recv_sem, g_thru, land_thru, token):
        x, y, c, _ = _place()
        theirs = pl.multiple_of((1 - c) * half, 8)
        pltpu.make_async_remote_copy(src_ref=g_ref.at[:, pl.ds(theirs, half), :], dst_ref=land_ref, send_sem=send_sem,
                                     recv_sem=recv_sem, device_id=(x, y, 1 - c), device_id_type=MESH).start()
        token[...] = jnp.zeros_like(token)

    return pl.pallas_call(
        body, name="rs_swap_start",
        out_shape=(pltpu.SemaphoreType.DMA(()), pltpu.SemaphoreType.DMA(()), pltpu.HBM(g.shape, g.dtype),
                   pltpu.HBM(land.shape, land.dtype), jax.ShapeDtypeStruct((8, LANES), F32)),
        in_specs=(HBM_SPEC, HBM_SPEC),
        out_specs=(SEM_SPEC, SEM_SPEC, HBM_SPEC, HBM_SPEC, pl.BlockSpec(memory_space=pltpu.VMEM)),
        input_output_aliases={0: 2, 1: 3}, compiler_params=pltpu.CompilerParams(has_side_effects=DATAFLOW),
    )(pltpu.with_memory_space_constraint(g, pltpu.HBM), pltpu.with_memory_space_constraint(land, pltpu.HBM))


def _sibling_swap_wait(send_sem, recv_sem, g_thru, land_thru, after):
    half = land_thru.shape[1]

    def body(g_ref, land_ref, send_sem, recv_sem, after_ref, g_done, got_ref):
        x, y, c, _ = _place()
        theirs = pl.multiple_of((1 - c) * half, 8)
        cp = pltpu.make_async_remote_copy(src_ref=g_ref.at[:, pl.ds(theirs, half), :], dst_ref=land_ref,
                                          send_sem=send_sem, recv_sem=recv_sem, device_id=(x, y, 1 - c),
                                          device_id_type=MESH)
        cp.wait_send()
        cp.wait_recv()

    return pl.pallas_call(
        body, name="rs_swap_wait",
        out_shape=(pltpu.HBM(g_thru.shape, g_thru.dtype), pltpu.HBM(land_thru.shape, land_thru.dtype)),
        in_specs=(HBM_SPEC, HBM_SPEC, SEM_SPEC, SEM_SPEC, pl.BlockSpec(memory_space=pl.ANY)),
        out_specs=(HBM_SPEC, HBM_SPEC), input_output_aliases={0: 0, 1: 1},
        compiler_params=pltpu.CompilerParams(has_side_effects=DATAFLOW),
    )(g_thru, land_thru, send_sem, recv_sem, after)


def _pair_sum(g, other, c_arr, tag):
    half = other.shape[1]
    T = _flat_tile(half)

    def body(c_ref, g_ref, o_ref, s_ref):
        s_ref[...] = (g_ref[...] + o_ref[...]).astype(BF16)

    nb = half // T
    return pl.pallas_call(
        body, out_shape=jax.ShapeDtypeStruct((N_CHIPS, half, FLAT_W), BF16),
        grid_spec=pltpu.PrefetchScalarGridSpec(
            num_scalar_prefetch=1, grid=(N_CHIPS, nb),
            in_specs=[pl.BlockSpec((None, T, FLAT_W), lambda k, i, c: (k, c[0] * nb + i, 0)),
                      pl.BlockSpec((None, T, FLAT_W), lambda k, i, c: (k, i, 0))],
            out_specs=pl.BlockSpec((None, T, FLAT_W), lambda k, i, c: (k, i, 0))),
        compiler_params=_cparams(("parallel", "parallel")), name=f"rs_pair_sum_{tag}",
    )(c_arr, g, other)


def _chip_exchange_start(s, land, tag):
    def body(s_ref, land_ref, send_sems, recv_sems, s_thru, land_thru, token):
        x, y, c, chips = _place()
        for r, (cx, cy) in enumerate(chips):
            pltpu.make_async_remote_copy(src_ref=s_ref.at[2 * cx + cy], dst_ref=land_ref.at[r], send_sem=send_sems.at[r],
                                         recv_sem=recv_sems.at[r], device_id=(cx, cy, c), device_id_type=MESH).start()
        token[...] = jnp.zeros_like(token)

    return pl.pallas_call(
        body, name=f"rs_exchange_start_{tag}",
        out_shape=(pltpu.SemaphoreType.DMA((3,)), pltpu.SemaphoreType.DMA((3,)), pltpu.HBM(s.shape, s.dtype),
                   pltpu.HBM(land.shape, land.dtype), jax.ShapeDtypeStruct((8, LANES), F32)),
        in_specs=(HBM_SPEC, HBM_SPEC),
        out_specs=(SEM_SPEC, SEM_SPEC, HBM_SPEC, HBM_SPEC, pl.BlockSpec(memory_space=pltpu.VMEM)),
        input_output_aliases={0: 2, 1: 3}, compiler_params=pltpu.CompilerParams(has_side_effects=DATAFLOW),
    )(pltpu.with_memory_space_constraint(s, pltpu.HBM), pltpu.with_memory_space_constraint(land, pltpu.HBM))


def _chip_exchange_wait(send_sems, recv_sems, s_thru, land_thru, after, tag):
    def body(s_ref, land_ref, send_sems, recv_sems, after_ref, s_done, got_ref):
        x, y, c, chips = _place()
        for r, (cx, cy) in enumerate(chips):
            cp = pltpu.make_async_remote_copy(src_ref=s_ref.at[2 * cx + cy], dst_ref=land_ref.at[r],
                                              send_sem=send_sems.at[r], recv_sem=recv_sems.at[r], device_id=(cx, cy, c),
                                              device_id_type=MESH)
            cp.wait_send()
            cp.wait_recv()

    return pl.pallas_call(
        body, name=f"rs_exchange_wait_{tag}",
        out_shape=(pltpu.HBM(s_thru.shape, s_thru.dtype), pltpu.HBM(land_thru.shape, land_thru.dtype)),
        in_specs=(HBM_SPEC, HBM_SPEC, SEM_SPEC, SEM_SPEC, pl.BlockSpec(memory_space=pl.ANY)),
        out_specs=(HBM_SPEC, HBM_SPEC), input_output_aliases={0: 0, 1: 1},
        compiler_params=pltpu.CompilerParams(has_side_effects=DATAFLOW),
    )(s_thru, land_thru, send_sems, recv_sems, after)


def _chip_sum(s, r, k_arr, tag):
    half = s.shape[1]
    T = _flat_tile(half)

    def body(k_ref, s_ref, r_ref, o_ref):
        o_ref[...] = ((s_ref[...].astype(F32) + r_ref[0].astype(F32)) + r_ref[1].astype(F32)) + r_ref[2].astype(F32)

    return pl.pallas_call(
        body, out_shape=jax.ShapeDtypeStruct((half, FLAT_W), F32),
        grid_spec=pltpu.PrefetchScalarGridSpec(
            num_scalar_prefetch=1, grid=(half // T,),
            in_specs=[pl.BlockSpec((None, T, FLAT_W), lambda i, k: (k[0], i, 0)),
                      pl.BlockSpec((3, T, FLAT_W), lambda i, k: (0, i, 0))],
            out_specs=pl.BlockSpec((T, FLAT_W), lambda i, k: (i, 0))),
        compiler_params=_cparams(("parallel",)), name=f"rs_chip_sum_{tag}",
    )(k_arr, s, r)


def _sibling_send(t, tag):
    def body(t_ref, o_ref, send_sem, recv_sem):
        x, y, c, _ = _place()
        cp = pltpu.make_async_remote_copy(src_ref=t_ref, dst_ref=o_ref, send_sem=send_sem, recv_sem=recv_sem,
                                          device_id=(x, y, 1 - c), device_id_type=MESH)
        cp.start()
        cp.wait()

    return pl.pallas_call(
        body, out_shape=jax.ShapeDtypeStruct(t.shape, F32), in_specs=[ANY], out_specs=ANY,
        scratch_shapes=[pltpu.SemaphoreType.DMA, pltpu.SemaphoreType.DMA],
        compiler_params=pltpu.CompilerParams(has_side_effects=True), name=f"rs_sibling_send_{tag}",
    )(t)


def _allreduce_small(v):
    def body(v_ref, o_ref, buf, send_sems, recv_sems):
        x, y, c, _ = _place()
        me = 4 * x + 2 * y + c
        buf[me] = v_ref[...]
        cps = []
        for mask in range(1, 8):
            a, b, d = (mask >> 2) & 1, (mask >> 1) & 1, mask & 1
            peer = (x + a - 2 * a * x, y + b - 2 * b * y, c + d - 2 * d * c)
            cps.append(pltpu.make_async_remote_copy(
                src_ref=v_ref, dst_ref=buf.at[me], send_sem=send_sems.at[mask - 1], recv_sem=recv_sems.at[mask - 1],
                device_id=peer, device_id_type=MESH))
        for cp in cps:
            cp.start()
        for cp in cps:
            cp.wait()
        total = buf[0]
        for dev in range(1, 8):
            total = total + buf[dev]
        o_ref[...] = total

    vm = pl.BlockSpec(memory_space=pltpu.VMEM)
    return pl.pallas_call(
        body, out_shape=jax.ShapeDtypeStruct((SMALL_ROWS, 1024), F32), in_specs=[vm], out_specs=vm,
        scratch_shapes=[pltpu.VMEM((8, SMALL_ROWS, 1024), F32), pltpu.SemaphoreType.DMA((7,)),
                        pltpu.SemaphoreType.DMA((7,))],
        compiler_params=pltpu.CompilerParams(has_side_effects=True), name="allreduce_small",
    )(v)


def _col_to_row(t):
    return t.reshape(t.shape[0], 1, S)


def _residue_rows(t, d, inverse=False):
    if d == 1:
        return t
    shape = (d, S // d) if inverse else (S // d, d)
    return t.reshape(shape + t.shape[1:]).transpose(1, 0, 2).reshape(t.shape)


def _residue_vecs(t, d, inverse=False):
    if d == 1:
        return t
    shape = (d, S // d) if inverse else (S // d, d)
    return t.reshape((t.shape[0],) + shape).transpose(0, 2, 1).reshape(t.shape)


def _ffn_fwd(x, g, w_up, cw, cb, w_down, tag):
    (h,) = _rms_fwd(x, [g], f"{tag}_norm")
    u = _mm(h, w_up, mode="nn", tm=1024, tn=1408, tk=1024, o_split=2, name=f"{tag}_up")
    act = _ffn_act_fwd(u, cw, cb, f"{tag}_act")
    x_out = _mm(act, w_down, mode="nn", tm=1024, tn=512, tk=FF, res=x, name=f"{tag}_down")
    return x_out, (h, u, act)


def _ffn_bwd(x, g, w_up, cw, cb, w_down, saved, dx, dxb, tag):
    h, u, act = saved
    d_w_down = _mm(act, dxb, mode="tn", tm=1408, tn=512, tk=2048, name=f"{tag}_dwdown")
    dact = _mm(dxb, w_down, mode="nt", tm=1024, tn=1408, tk=1024, name=f"{tag}_dact")
    duc, dwb = _ffn_act_bwd(u, dact, cw, cb, f"{tag}_dgate")
    du = _ffn_conv_bwd(duc, cw, f"{tag}_dconv")
    d_w_up = _mm(h, du, mode="tn", tm=1024, tn=1408, tk=2048, b_split=2, name=f"{tag}_dwup")
    dh = _mm(du, w_up, mode="nt", tm=1024, tn=512, tk=FF, a_split=1, name=f"{tag}_dh")
    dx_new, dxb_new, (dg,) = _rms_bwd(x, dx, [(g, dh)], f"{tag}_dnorm")
    d_cw = dwb[:, 0:3, :].transpose(1, 0, 2).reshape(3, 2 * FF)
    d_cb = dwb[:, 3, :].reshape(2 * FF)
    return dx_new, dxb_new, dict(w_up=d_w_up, w_down=d_w_down, conv_w=d_cw, conv_b=d_cb, norm_g=dg.reshape(D))


def _local_step(x, target, p, late_weights, late_grads_ready, late_grads_continue):
    g = {}
    (h1,) = _rms_fwd(x, [p["mix_norm_g"][0]], "a_norm")
    w_qkv = p["a_w_in"][:, :QKV_W]
    w_f = jnp.pad(p["a_w_in"][:, QKV_W:], ((0, 0), (0, LANES - A_HEADS)))
    b_f = jnp.pad(p["a_b_f"].reshape(1, A_HEADS), ((0, 0), (0, LANES - A_HEADS)))
    qkv = _mm(h1, w_qkv, mode="nn", tm=2048, tn=512, tk=1024, out_dtype=BF16, name="a_qkv")
    pf = _mm(h1, w_f, mode="nn", tm=1024, tn=LANES, tk=1024, name="a_gate")
    aug_q, aug_k = _fgate_fwd(pf, b_f, "a_gate_scan")
    oa2, lse_a = _fox_pair_fwd(qkv, aug_q, aug_k, "a_attn")
    x1 = _mm(oa2, p["a_w_out"], mode="nn", tm=1024, tn=512, tk=1024, res=x, name="a_out")
    p = {**p, **late_weights(x1)}
    x2, ffn0 = _ffn_fwd(x1, p["ffn_norm_g"][0], p["ffn_w_up"][0], p["conv_w"][0], p["conv_b"][0], p["ffn_w_down"][0], "f0")
    hk, h3 = _rms_fwd(x2, [p["kv_norm_g"], p["mix_norm_g"][1]], "kv_b_norm")
    kvb = _mm(hk, p["w_kv"], mode="nn", tm=2048, tn=512, tk=1024, out_dtype=BF16, name="kv_proj")
    qb = _mm(h3, p["b_w_q"], mode="nn", tm=2048, tn=512, tk=1024, out_dtype=BF16, name="b_q")
    dil_in = []
    for gi, (_, d) in enumerate(B_GROUPS):
        if d == 1:
            dil_in.append((qb, kvb, kvb, gi * NPG, gi * NPG, (3 + gi) * NPG))
        else:
            qg = _residue_rows(qb[:, gi * GROUP_W:(gi + 1) * GROUP_W], d)
            kvg = _residue_rows(kvb.reshape(S, 2, 3, GROUP_W)[:, :, gi, :].reshape(S, 2 * GROUP_W), d)
            dil_in.append((qg, kvg, kvg, 0, 0, NPG))
    o_g, lse_g = [], []
    for gi, (_, d) in enumerate(B_GROUPS):
        qg, kg, vg, qoff, koff, voff = dil_in[gi]
        og, lg = _dil_pair_fwd(gi, qg, kg, vg, qoff, koff, voff, f"b_attn{gi}")
        o_g.append(_residue_rows(og, d, inverse=True))
        lse_g.append(_residue_vecs(lg, d, inverse=True))
    ob, ob2, lse_b = _dil_pair_merge(o_g, lse_g, "b_merge")
    x3 = _mm(ob2, p["b_w_out"], mode="nn", tm=1024, tn=512, tk=B_OUT_W, res=x2, name="b_out")
    x4, ffn1 = _ffn_fwd(x3, p["ffn_norm_g"][1], p["ffn_w_up"][1], p["conv_w"][1], p["conv_b"][1], p["ffn_w_down"][1], "f1")
    loss, dx, dxb, dg_final = _loss_head(x4, p["final_norm_g"], target, "loss_head")
    g["final_norm_g"] = dg_final.reshape(D)

    dx, dxb, gf1 = _ffn_bwd(x3, p["ffn_norm_g"][1], p["ffn_w_up"][1], p["conv_w"][1], p["conv_b"][1], p["ffn_w_down"][1],
                            ffn1, dx, dxb, "f1")
    g["b_w_out"] = _mm(ob2, dxb, mode="tn", tm=B_OUT_W, tn=512, tk=1024, name="b_dwout")
    dob = _mm(dxb, p["b_w_out"], mode="nt", tm=1024, tn=B_OUT_W, tk=1024, name="b_do")
    delta_b = _pair_rowdot(dob, ob, "b_delta")
    dob16 = dob.astype(BF16)
    stats_b = jnp.concatenate([lse_b, delta_b], axis=0)
    dq_g, dk_g, dv_g = [], [], []
    for gi, (_, d) in enumerate(B_GROUPS):
        qg, kg, vg, qoff, koff, voff = dil_in[gi]
        dog, stats_d = _residue_rows(dob16, d), _residue_vecs(stats_b, d)
        dqd = _dil_pair_dq(gi, qg, kg, vg, qoff, koff, voff, dog, stats_d, f"b_dq{gi}")
        dkd, dvd = _dil_pair_dkv(gi, qg, kg, vg, qoff, koff, voff, dog, _col_to_row(stats_d), f"b_dkv{gi}")
        dq_g.append(_residue_rows(dqd, d, inverse=True))
        dk_g.append(_residue_rows(dkd, d, inverse=True))
        dv_g.append(_residue_rows(dvd, d, inverse=True))
    dqb = jnp.concatenate(dq_g, axis=1)
    dkvb = jnp.concatenate(dk_g + dv_g, axis=1)
    g["b_w_q"] = _mm(h3, dqb, mode="tn", tm=1024, tn=512, tk=S, name="b_dwq")
    dh3 = _mm(dqb, p["b_w_q"], mode="nt", tm=2048, tn=512, tk=B_Q_W, name="b_dh")
    g["w_kv"] = _mm(hk, dkvb, mode="tn", tm=1024, tn=512, tk=S, name="kv_dw")
    dhk = _mm(dkvb, p["w_kv"], mode="nt", tm=1024, tn=512, tk=3072, name="kv_dh")
    dx, dxb, (dg_mix1, dg_kv) = _rms_bwd(x2, dx, [(p["mix_norm_g"][1], dh3), (p["kv_norm_g"], dhk)], "b_dnorm")
    g["kv_norm_g"] = dg_kv.reshape(D)
    dx, dxb, gf0 = _ffn_bwd(x1, p["ffn_norm_g"][0], p["ffn_w_up"][0], p["conv_w"][0], p["conv_b"][0], p["ffn_w_down"][0],
                            ffn0, dx, dxb, "f0")
    g["ffn_w_up"] = [gf0["w_up"], gf1["w_up"]]
    g["ffn_w_down"] = [gf0["w_down"], gf1["w_down"]]
    g["ffn_conv_w"] = jnp.stack([gf0["conv_w"], gf1["conv_w"]])
    token = late_grads_ready(g)
    a_w_out_t = p["a_w_out"] + token[0, 0].astype(BF16)
    g["a_w_out"] = _mm(oa2, dxb, mode="tn", tm=1024, tn=512, tk=S, name="a_dwout")
    doa = _mm(dxb, a_w_out_t, mode="nt", tm=2048, tn=512, tk=1024, name="a_do")
    delta_a = _pair_rowdot(doa, oa2, "a_delta")
    token = late_grads_continue(delta_a)
    delta_row = _col_to_row(delta_a) + token[0, 0]
    dqa, dka, dva, dck, dcq = _fox_pair_bwd(qkv, doa, _col_to_row(lse_a), delta_row, aug_q, aug_k, "a_dattn")
    dqkv = jnp.concatenate([dqa, dka, dva], axis=1)
    pad_heads = lambda t: jnp.pad(t.reshape(A_HEADS, S).T, ((0, 0), (0, LANES - A_HEADS)))
    dpf, db_f = _fgate_bwd(pf, b_f, pad_heads(dck), pad_heads(dcq), "a_dgate_scan")
    g["a_b_f"] = db_f[:, :A_HEADS]
    d_w_qkv = _mm(h1, dqkv, mode="tn", tm=1024, tn=512, tk=S, name="a_dwqkv")
    d_w_f = _mm(h1, dpf, mode="tn", tm=1024, tn=LANES, tk=1024, name="a_dwgate")
    g["a_w_in"] = jnp.concatenate([d_w_qkv, d_w_f[:, :A_HEADS]], axis=1)
    dh1 = _mm(dqkv, w_qkv, mode="nt", tm=1024, tn=512, tk=3072, name="a_dh")
    dh1 = _mm(dpf, w_f, mode="nt", tm=1024, tn=512, tk=LANES, res=dh1, name="a_dh_gate")
    dx, _, (dg_mix0,) = _rms_bwd(x, dx, [(p["mix_norm_g"][0], dh1)], "a_dnorm")

    g["mix_norm_g"] = jnp.stack([dg_mix0.reshape(D), dg_mix1.reshape(D)])
    g["ffn_norm_g"] = jnp.stack([gf0["norm_g"], gf1["norm_g"]])
    g["ffn_conv_b"] = jnp.stack([gf0["conv_b"], gf1["conv_b"]])
    return loss[0, 0], dx, g


_SHARD_SHAPES = {"a_w_in": (1, 1024, 772), "a_w_out": (1, 256, 1024), "b_w_q": (1, 1024, 384), "b_w_out": (1, 512, 256),
                 "w_kv": (1024, 768), "ffn_w_up": (2, 1024, 1408), "ffn_w_down": (2, 704, 1024), "ffn_conv_w": (2, 3, 1408)}
_SMALL = (("kv_norm_g", (1024,)), ("mix_norm_g", (2, 1024)), ("ffn_norm_g", (2, 1024)), ("final_norm_g", (1024,)),
          ("a_b_f", (1, 16)), ("ffn_conv_b", (2, 5632)))


def _unslabs(rows, L, R, C, rpad):
    nc = -(-C // FLAT_W)
    return rows.reshape(L, nc, rpad, FLAT_W).transpose(0, 2, 1, 3).reshape(L, rpad, nc * FLAT_W)[:, :R, :C]


_SEG_RT = {"ffn_w_down": 704, "a_w_in": 1024, "a_w_out": 256, "b_w_q": 1024, "b_w_out": 512, "w_kv": 1024,
           "ffn_w_up": 1024, "ffn_conv_w": 16}
_ROW_SHARDED = ("a_w_out", "ffn_w_down")


_LAYOUTS = {"early": ("a_w_in", "a_w_out"), "late": ("ffn_w_down", "b_w_q", "b_w_out", "w_kv", "ffn_w_up", "ffn_conv_w"),
            "grad_early": ("a_w_in", "a_w_out"),
            "grad_late": ("ffn_w_up", "ffn_w_down", "b_w_q", "w_kv", "b_w_out", "ffn_conv_w")}
_GRAD_ROWS = {"grad_early": 10240, "grad_late": 45056}


def _layout_rows(layout):
    used = sum(_seg_rows(*s) for s in _SEGS if s[0] in _LAYOUTS[layout])
    rows = _GRAD_ROWS.get(layout, used)
    assert rows >= used
    return rows


def _grad_layout(name):
    return "grad_early" if name in _LAYOUTS["grad_early"] else "grad_late"


def _seg(name, layout=None):
    layout = layout or _grad_layout(name)
    off = 0
    for s in sorted((s for s in _SEGS if s[0] in _LAYOUTS[layout]), key=lambda s: _LAYOUTS[layout].index(s[0])):
        _, L, R, C, rpad = s
        if layout in _GRAD_ROWS:
            per_layer = -(-C // FLAT_W) * rpad
            off = -(-off // per_layer) * per_layer
        if s[0] == name:
            rt = _SEG_RT[name]
            assert off % rt == 0 and rpad % rt == 0
            half = _layout_rows(layout) // 2
            assert off + _seg_rows(*s) <= 2 * half
            assert layout not in _GRAD_ROWS or half % rt == 0 or off + _seg_rows(*s) <= half
            return dict(L=L, R=R, C=C, rpad=rpad, nc=-(-C // FLAT_W), rt=rt, off=off, ni=rpad // rt, half=half)
        off += _seg_rows(*s)
    raise KeyError(name)


def _flat_block(sg, term=0):
    base = (sg["off"] + term * sg["L"] * sg["nc"] * sg["rpad"]) // sg["rt"]
    return lambda l, j, i: base + (l * sg["nc"] + j) * sg["ni"] + i


def _native3(t, name):
    sg = _seg(name)
    t = t.reshape(sg["L"], sg["R"], sg["C"])
    return jnp.pad(t, ((0, 0), (0, sg["rpad"] - sg["R"]), (0, 0))) if sg["rpad"] != sg["R"] else t


def _slab_pack(flat, t, name, layout, term=None):
    sg = _seg(name, layout)
    rt = sg["rt"]
    rb = _flat_block(sg, term or 0)

    def body(*refs):
        t_ref, o_ref = refs[-2], refs[-1]
        val = t_ref[...]
        o_ref[...] = val.astype(BF16) if term is None else _split3(val)[term]

    in_specs = [pl.BlockSpec((None, rt, FLAT_W), lambda l, j, i: (l, i, j))]
    args = [t]
    if flat is not None:
        in_specs, args = [ANY] + in_specs, [flat] + args
    return pl.pallas_call(
        body, out_shape=jax.ShapeDtypeStruct((_layout_rows(layout), FLAT_W), BF16), grid=(sg["L"], sg["nc"], sg["ni"]),
        in_specs=in_specs, out_specs=pl.BlockSpec((rt, FLAT_W), lambda l, j, i: (rb(l, j, i), 0)),
        input_output_aliases={0: 0} if flat is not None else {},
        compiler_params=_cparams(("parallel", "parallel", "parallel")), name=f"pack_{name}_{term or 0}",
    )(*args)


def _full_spec(sg, name):
    rt, nc, ni = sg["rt"], sg["nc"], sg["ni"]
    if name in _ROW_SHARDED:
        return (sg["L"], N_CHIPS * sg["R"], sg["C"]), pl.BlockSpec((None, rt, FLAT_W), lambda k, l, j, i: (l, k * ni + i, j))
    return ((sg["L"], sg["rpad"], N_CHIPS * nc * FLAT_W),
            pl.BlockSpec((None, rt, FLAT_W), lambda k, l, j, i: (l, i, k * nc + j)))


def _slab_unpack(gathered, slots, name, layout, own=None):
    sg = _seg(name, layout)
    rb = _flat_block(sg)
    shape, _ = _full_spec(sg, name)
    rt, nc, ni = sg["rt"], sg["nc"], sg["ni"]
    width = nc * FLAT_W
    last = gathered.shape[0] - 1

    def body(*refs):
        s_ref, o_ref = refs[0], refs[-1]
        is_own = s_ref[pl.program_id(0)] == OWN_SLOT
        for j in range(nc):
            val = refs[1 + j][...]
            if own is not None:
                val = jnp.where(is_own, refs[1 + nc + j][...], val)
            o_ref[:, j * FLAT_W:(j + 1) * FLAT_W] = val

    if name in _ROW_SHARDED:
        o_spec = pl.BlockSpec((None, rt, width), lambda k, l, i, s: (l, k * ni + i, 0))
    else:
        o_spec = pl.BlockSpec((None, rt, width), lambda k, l, i, s: (l, i, k))
    in_specs = [pl.BlockSpec((None, rt, FLAT_W), lambda k, l, i, s, j=j: (jnp.minimum(s[k], last), rb(l, j, i), 0))
                for j in range(nc)]
    args = [gathered] * nc
    if own is not None:
        in_specs += [pl.BlockSpec((rt, FLAT_W), lambda k, l, i, s, j=j: (rb(l, j, i), 0)) for j in range(nc)]
        args += [own] * nc
    return pl.pallas_call(
        body, out_shape=jax.ShapeDtypeStruct(shape, BF16),
        grid_spec=pltpu.PrefetchScalarGridSpec(num_scalar_prefetch=1, grid=(N_CHIPS, sg["L"], ni), in_specs=in_specs,
                                               out_specs=o_spec),
        compiler_params=_cparams(("parallel",) * 3), name=f"unpack_{name}",
    )(slots, *args)


def _slab_pack_grad(flat4, g, name, layer=None):
    sg = _seg(name)
    rows = _layout_rows(_grad_layout(name))
    shape, _ = _full_spec(sg, name)
    n_layers = sg["L"] if layer is None else 1
    assert g.shape == (n_layers,) + shape[1:], (name, g.shape, shape)
    rt, nc = sg["rt"], sg["nc"]
    assert sg["ni"] == 1 and sg["off"] % (nc * rt) == 0
    base = sg["off"] // (nc * rt) + (layer or 0)

    def body(*refs):
        g_ref, o_ref = refs[-2], refs[-1]
        for j in range(nc):
            o_ref[j * rt:(j + 1) * rt, :] = g_ref[:, j * FLAT_W:(j + 1) * FLAT_W]

    if name in _ROW_SHARDED:
        spec = pl.BlockSpec((None, rt, nc * FLAT_W), lambda k, l: (l, k, 0))
    else:
        spec = pl.BlockSpec((None, rt, nc * FLAT_W), lambda k, l: (l, 0, k))
    in_specs, args = [spec], [g]
    if flat4 is not None:
        in_specs, args = [pl.BlockSpec(memory_space=pl.ANY)] + in_specs, [flat4] + args
    return pl.pallas_call(
        body, out_shape=jax.ShapeDtypeStruct((N_CHIPS, rows, FLAT_W), F32), grid=(N_CHIPS, n_layers),
        in_specs=in_specs, out_specs=pl.BlockSpec((None, nc * rt, FLAT_W), lambda k, l: (k, base + l, 0)),
        input_output_aliases={0: 0} if flat4 is not None else {},
        compiler_params=_cparams(("parallel",) * 2), name=f"packgrad_{name}_{layer or 0}",
    )(*args)


def _adamw_shard(w, m, v, g_mine, g_other, c_arr, name):
    sg = _seg(name)
    rt = sg["rt"]
    rb = _flat_block(sg)
    per_half = sg["half"] // rt

    def half_of(l, j, i):
        return (rb(l, j, i) * rt) // sg["half"]

    def body(c_ref, w_ref, m_ref, v_ref, gm_ref, go_ref, g_ref, d_ref, mo_ref, vo_ref):
        is_mine = half_of(pl.program_id(0), pl.program_id(1), pl.program_id(2)) == c_ref[0]
        gv = jnp.where(is_mine, gm_ref[...], go_ref[...])
        g_ref[...] = gv
        d_ref[...], mo_ref[...], vo_ref[...] = _adam_update(w_ref[...], gv, m_ref[...], v_ref[...])

    nat = pl.BlockSpec((None, rt, FLAT_W), lambda l, j, i, c: (l, i, j))
    half = pl.BlockSpec((rt, FLAT_W), lambda l, j, i, c: (rb(l, j, i) - half_of(l, j, i) * per_half, 0))
    sds = jax.ShapeDtypeStruct(w.shape, F32)
    return pl.pallas_call(
        body, out_shape=[sds] * 4,
        grid_spec=pltpu.PrefetchScalarGridSpec(num_scalar_prefetch=1, grid=(sg["L"], sg["nc"], sg["ni"]),
                                               in_specs=[nat, nat, nat, half, half], out_specs=[nat] * 4),
        compiler_params=_cparams(("parallel", "parallel", "parallel")), name=f"adamw_{name}",
    )(c_arr, w, m, v, g_mine, g_other)


def _pack_small(vals, loss=None):
    parts = [vals[name].astype(F32).reshape(-1) for name, _ in _SMALL]
    if loss is not None:
        parts.append(loss.reshape(1))
    flat = jnp.concatenate(parts)
    return jnp.pad(flat, (0, SMALL_ROWS * 1024 - flat.shape[0])).reshape(SMALL_ROWS, 1024)


def _unpack_small(flat):
    flat = flat.reshape(-1)
    out = {}
    o = 0
    for name, shape in _SMALL:
        n = int(np.prod(shape))
        out[name] = flat[o:o + n].reshape(shape)
        o += n
    return out, flat[o]


_BIG = ("a_w_in", "a_w_out", "b_w_q", "b_w_out", "w_kv", "ffn_w_up", "ffn_w_down", "ffn_conv_w")
A_IN_PAD = 896


def _pack_weights(w, layout):
    flat = None
    for name in _LAYOUTS[layout]:
        t = _native3(w[name], name)
        for term in ((0, 1, 2) if name == "ffn_conv_w" else (None,)):
            flat = _slab_pack(flat, t, name, layout, term)
    return flat


def _early_weights(gathered, slots):
    a_in = _slab_unpack(gathered, slots, "a_w_in", "early")
    a_in = a_in.reshape(D, N_CHIPS, A_IN_PAD)[:, :, :772].reshape(D, N_CHIPS * 772)
    return dict(a_w_in=a_in, a_w_out=_slab_unpack(gathered, slots, "a_w_out", "early")[0])


def _late_weights(landed, slots, own):
    full = {name: _slab_unpack(landed, slots, name, "late", own) for name in _LAYOUTS["late"] if name != "ffn_conv_w"}
    sg = _seg("ffn_conv_w", "late")
    n1 = sg["nc"] * sg["rpad"]
    conv = slice(sg["off"], sg["off"] + CONV_TERMS * n1)
    conv_rows = jnp.concatenate([landed[:, conv], own[None, conv]], axis=0)
    per_chip = []
    for k in range(N_CHIPS):
        rows = lax.dynamic_index_in_dim(conv_rows, slots[k], axis=0, keepdims=False)
        terms = [_unslabs(rows[i * n1:(i + 1) * n1], 1, sg["R"], sg["C"], sg["rpad"]).astype(F32) for i in range(CONV_TERMS)]
        per_chip.append((terms[0] + terms[1]) + terms[2])
    cw = jnp.concatenate(per_chip, axis=2).reshape(2, 3, 2, FF).transpose(0, 2, 1, 3)
    return dict(b_w_q=full["b_w_q"][0], b_w_out=full["b_w_out"][0], w_kv=full["w_kv"][0], ffn_w_up=full["ffn_w_up"],
                ffn_w_down=full["ffn_w_down"], conv_w=cw)


def _shard_grads(g, layout):
    def full(name):
        if name == "a_w_in":
            a_in = jnp.pad(g[name].reshape(D, N_CHIPS, 772), ((0, 0), (0, 0), (0, A_IN_PAD - 772)))
            return a_in.reshape(1, D, N_CHIPS * A_IN_PAD)
        if name == "ffn_conv_w":
            sgc = _seg(name)
            return jnp.pad(g[name].reshape(1, sgc["R"], 2 * FF), ((0, 0), (0, sgc["rpad"] - sgc["R"]), (0, 0)))
        return g[name] if g[name].ndim == 3 else g[name][None]

    flat4 = None
    for name in _LAYOUTS[layout]:
        if isinstance(g[name], (list, tuple)):
            for layer, t in enumerate(g[name]):
                flat4 = _slab_pack_grad(flat4, t[None], name, layer)
        else:
            flat4 = _slab_pack_grad(flat4, full(name), name)
    return flat4


_WEIGHTS = ["a_w_in", "a_b_f", "a_w_out", "b_w_q", "b_w_out", "kv_norm_g", "w_kv", "mix_norm_g", "ffn_norm_g", "ffn_w_up",
            "ffn_conv_w", "ffn_conv_b", "ffn_w_down", "final_norm_g"]


def kernel(x, a_w_in, a_b_f, a_w_out, b_w_q, b_w_out, kv_norm_g, w_kv, mix_norm_g, ffn_norm_g, ffn_w_up, ffn_conv_w, ffn_conv_b, ffn_w_down, final_norm_g, loss_target, m_a_w_in, m_a_b_f, m_a_w_out, m_b_w_q, m_b_w_out, m_kv_norm_g, m_w_kv, m_mix_norm_g, m_ffn_norm_g, m_ffn_w_up, m_ffn_conv_w, m_ffn_conv_b, m_ffn_w_down, m_final_norm_g, v_a_w_in, v_a_b_f, v_a_w_out, v_b_w_q, v_b_w_out, v_kv_norm_g, v_w_kv, v_mix_norm_g, v_ffn_norm_g, v_ffn_w_up, v_ffn_conv_w, v_ffn_conv_b, v_ffn_w_down, v_final_norm_g):
    w = dict(a_w_in=a_w_in, a_b_f=a_b_f, a_w_out=a_w_out, b_w_q=b_w_q, b_w_out=b_w_out, kv_norm_g=kv_norm_g, w_kv=w_kv,
             mix_norm_g=mix_norm_g, ffn_norm_g=ffn_norm_g, ffn_w_up=ffn_w_up, ffn_conv_w=ffn_conv_w, ffn_conv_b=ffn_conv_b,
             ffn_w_down=ffn_w_down, final_norm_g=final_norm_g)
    m = dict(a_w_in=m_a_w_in, a_b_f=m_a_b_f, a_w_out=m_a_w_out, b_w_q=m_b_w_q, b_w_out=m_b_w_out, kv_norm_g=m_kv_norm_g,
             w_kv=m_w_kv, mix_norm_g=m_mix_norm_g, ffn_norm_g=m_ffn_norm_g, ffn_w_up=m_ffn_w_up, ffn_conv_w=m_ffn_conv_w,
             ffn_conv_b=m_ffn_conv_b, ffn_w_down=m_ffn_w_down, final_norm_g=m_final_norm_g)
    v = dict(a_w_in=v_a_w_in, a_b_f=v_a_b_f, a_w_out=v_a_w_out, b_w_q=v_b_w_q, b_w_out=v_b_w_out, kv_norm_g=v_kv_norm_g,
             w_kv=v_w_kv, mix_norm_g=v_mix_norm_g, ffn_norm_g=v_ffn_norm_g, ffn_w_up=v_ffn_w_up, ffn_conv_w=v_ffn_conv_w,
             ffn_conv_b=v_ffn_conv_b, ffn_w_down=v_ffn_w_down, final_norm_g=v_final_norm_g)

    c_arr = lax.axis_index("c").astype(jnp.int32).reshape(1)
    k_arr = (2 * lax.axis_index("x") + lax.axis_index("y")).astype(jnp.int32).reshape(1)
    xi, yi = lax.axis_index("x"), lax.axis_index("y")
    late_slots = jnp.stack([jnp.where(k == k_arr[0], OWN_SLOT, 2 * ((k & 1) ^ yi) + ((k >> 1) ^ xi) - 1)
                            for k in range(N_CHIPS)]).astype(jnp.int32)
    w_late = _pack_weights(w, "late")
    land = lax.empty((OWN_SLOT,) + w_late.shape, BF16)
    send_sems, recv_sems, w_thru, land_thru, token = _late_gather_start(w_late, land)
    w_early = _pack_weights(w, "early")
    early = _allgather_shards(w_early, _place_own(w_early, k_arr, "early_place_own"))
    p = _early_weights(early, jnp.arange(N_CHIPS, dtype=jnp.int32))
    cb = ffn_conv_b.reshape(2, 2, 1, FF)
    p.update(a_b_f=a_b_f, kv_norm_g=kv_norm_g, mix_norm_g=mix_norm_g + token[0, 0], ffn_norm_g=ffn_norm_g,
             final_norm_g=final_norm_g, conv_b=cb)

    def late_weights(after):
        own, landed = _late_gather_wait(send_sems, recv_sems, w_thru, land_thru, after)
        return _late_weights(landed, late_slots, own)

    started = {}

    def late_grads_ready(g_so_far):
        gflat = _shard_grads(g_so_far, "grad_late")
        land = lax.empty((N_CHIPS, gflat.shape[1] // 2, FLAT_W), F32)
        *handles, token = _sibling_swap_start(gflat, land)
        started["swap"] = handles
        return token

    def late_grads_continue(after):
        gflat, other = _sibling_swap_wait(*started["swap"], after)
        pair = _pair_sum(gflat, other, c_arr, "late")
        land = lax.empty((3,) + pair.shape[1:], BF16)
        *handles, token = _chip_exchange_start(pair, land, "late")
        started["handles"] = handles
        return token

    loss_part, grad_x, g = _local_step(x[0], loss_target[0], p, late_weights, late_grads_ready, late_grads_continue)

    gflat = _shard_grads(g, "grad_early")
    pair_e = _pair_sum(gflat, _sibling_swap_half(gflat, "early"), c_arr, "early")
    *early_handles, token = _chip_exchange_start(pair_e, lax.empty((3,) + pair_e.shape[1:], BF16), "early")

    big = [{}, {}, {}, {}]

    def adamw_group(layout, g_mine):
        g_other = _sibling_send(g_mine, layout)
        for name in _LAYOUTS[layout]:
            sg = _seg(name)
            res = _adamw_shard(_native3(w[name], name), _native3(m[name], name), _native3(v[name], name), g_mine,
                               g_other, c_arr, name)
            for store, t in zip(big, res):
                store[name] = t[:, :sg["R"], :].reshape(_SHARD_SHAPES[name])
        return res[1]

    pair, landed = _chip_exchange_wait(*started["handles"], token, "late")
    last = adamw_group("grad_late", _chip_sum(pair, landed, k_arr, "late"))
    small, loss = _unpack_small(_allreduce_small(_pack_small(g, loss_part)))
    dws, mns, vns = _adamw(_pack_small(w), _pack_small(small), _pack_small(m), _pack_small(v), "adamw_small")
    pair_e, landed_e = _chip_exchange_wait(*early_handles, last, "early")
    adamw_group("grad_early", _chip_sum(pair_e, landed_e, k_arr, "early"))
    sml = [small] + [_unpack_small(t)[0] for t in (dws, mns, vns)]
    outs = [loss, grad_x[None]]
    for b, s in zip(big, sml):
        outs += [b[n] if n in b else s[n] for n in _WEIGHTS]
    return tuple(outs)
```

```python
import numpy as np
import jax
import jax.numpy as jnp
from jax import lax
from jax.experimental import pallas as pl
from jax.experimental.pallas import tpu as pltpu

F32 = jnp.float32
BF16 = jnp.bfloat16
MESH = pl.DeviceIdType.MESH

S = 4096
D = 1024
A_HEADS = 16
HEAD_DIM = 64
QKV_W = 3 * A_HEADS * HEAD_DIM
B_GROUPS = ((128, 1), (512, 4), (2048, 16))
B_HPG = 8
B_Q_W = 3 * B_HPG * HEAD_DIM
B_OUT_W = B_HPG * HEAD_DIM
B_WIN = 128
FF = 2816
RMS_EPS = 1e-6
SCALE = HEAD_DIM ** -0.5
N_CHIPS = 4

ADAM_LR, ADAM_B1, ADAM_B2, ADAM_EPS, ADAM_WD, ADAM_STEP = 0.001, 0.9, 0.999, 1e-08, 0.01, 10

V7X_VMEM_LIMIT = 48 * 1024 * 1024
LANES = 128
NEG_INF = float("-inf")

FLAT_W = LANES
_SEGS = (("ffn_w_down", 2, 704, 1024, 704), ("a_w_in", 1, 1024, 772, 1024), ("a_w_out", 1, 256, 1024, 256),
         ("b_w_q", 1, 1024, 384, 1024), ("b_w_out", 1, 512, 256, 512), ("w_kv", 1, 1024, 768, 1024),
         ("ffn_w_up", 2, 1024, 1408, 1024), ("ffn_conv_w", 1, 6, 1408, 16))
CONV_TERMS = 3


def _seg_rows(name, L, R, C, rpad):
    return (CONV_TERMS if name == "ffn_conv_w" else 1) * L * (-(-C // FLAT_W)) * rpad


SMALL_ROWS = 24


def _cparams(sem=None, **kw):
    return pltpu.CompilerParams(dimension_semantics=sem, vmem_limit_bytes=V7X_VMEM_LIMIT, **kw)


_DN = {"nn": (((1,), (0,)), ((), ())), "nt": (((1,), (1,)), ((), ())), "tn": (((0,), (0,)), ((), ()))}


def _mm(a, b, *, mode, tm, tn, tk, name, out_dtype=F32, res=None, a_split=0, b_split=0, o_split=0):
    if mode == "tn":
        K = a.shape[0]
        M = a.shape[1]
    else:
        M = a.shape[-2]
        K = a.shape[-1] * (2 if a_split else 1)
    if mode == "nt":
        N = b.shape[0]
    else:
        N = b.shape[-1] * (2 if b_split else 1)
    assert M % tm == 0 and N % tn == 0 and K % tk == 0, (name, M, N, K, tm, tn, tk)
    nk = K // tk

    if mode == "tn":
        a_spec = pl.BlockSpec((tk, tm), lambda i, j, k: (k, i))
    elif a_split:
        a_spec = pl.BlockSpec((None, tm, tk), lambda i, j, k: (k // a_split, i, k % a_split))
    else:
        a_spec = pl.BlockSpec((tm, tk), lambda i, j, k: (i, k))
    if mode == "nt":
        b_spec = pl.BlockSpec((tn, tk), lambda i, j, k: (j, k))
    elif b_split:
        b_spec = pl.BlockSpec((None, tk, tn), lambda i, j, k: (j // b_split, k, j % b_split))
    else:
        b_spec = pl.BlockSpec((tk, tn), lambda i, j, k: (k, j))
    if o_split:
        o_spec = pl.BlockSpec((None, tm, tn), lambda i, j, k: (j // o_split, i, j % o_split))
        out_shape = jax.ShapeDtypeStruct((2, M, N // 2), out_dtype)
    else:
        o_spec = pl.BlockSpec((tm, tn), lambda i, j, k: (i, j))
        out_shape = jax.ShapeDtypeStruct((M, N), out_dtype)
    in_specs = [a_spec, b_spec]
    args = [a, b]
    if res is not None:
        in_specs.append(pl.BlockSpec((tm, tn), lambda i, j, k: (i, j)))
        args.append(res)

    def body(*refs):
        if res is not None:
            a_ref, b_ref, r_ref, o_ref = refs[:4]
        else:
            a_ref, b_ref, o_ref = refs[:3]
            r_ref = None
        p = lax.dot_general(a_ref[...].astype(BF16), b_ref[...].astype(BF16), _DN[mode], preferred_element_type=F32)

        def finish(r):
            if r_ref is not None:
                r = r + r_ref[...]
            o_ref[...] = r.astype(out_dtype)

        if nk == 1:
            finish(p)
        else:
            acc = refs[-1]
            k = pl.program_id(2)

            @pl.when(k == 0)
            def _():
                acc[...] = p

            @pl.when(k > 0)
            def _():
                acc[...] += p

            @pl.when(k == nk - 1)
            def _():
                finish(acc[...])

    return pl.pallas_call(
        body, out_shape=out_shape, grid=(M // tm, N // tn, nk), in_specs=in_specs, out_specs=o_spec,
        scratch_shapes=[pltpu.VMEM((tm, tn), F32)] if nk > 1 else [],
        compiler_params=_cparams(("parallel", "parallel", "arbitrary")), name=name,
    )(*args)


NORM_ROWS = 256


def _rms_fwd(x, gains, name):
    n = len(gains)

    def body(x_ref, *refs):
        xv = x_ref[...]
        y = xv * lax.rsqrt(jnp.mean(xv * xv, axis=-1, keepdims=True) + RMS_EPS)
        for g_ref, o_ref in zip(refs[:n], refs[n:]):
            o_ref[...] = (y * g_ref[...]).astype(BF16)

    row = pl.BlockSpec((NORM_ROWS, D), lambda i: (i, 0))
    return pl.pallas_call(
        body, out_shape=[jax.ShapeDtypeStruct((S, D), BF16)] * n, grid=(S // NORM_ROWS,),
        in_specs=[row] + [pl.BlockSpec((1, D), lambda i: (0, 0))] * n, out_specs=[row] * n,
        compiler_params=_cparams(("parallel",)), name=name,
    )(x, *[g.reshape(1, D) for g in gains])


def _rms_bwd(x, dres, pairs, name):
    n = len(pairs)

    def body(*refs):
        x_ref, dres_ref = refs[0], refs[1]
        g_refs = refs[2:2 + 2 * n:2]
        dh_refs = refs[3:3 + 2 * n:2]
        dx_ref, dxb_ref = refs[2 + 2 * n], refs[3 + 2 * n]
        dg_refs = refs[4 + 2 * n:]
        i = pl.program_id(0)
        xv = x_ref[...]
        r = lax.rsqrt(jnp.mean(xv * xv, axis=-1, keepdims=True) + RMS_EPS)
        y = xv * r
        dx = dres_ref[...]
        for g_ref, dh_ref, dg_ref in zip(g_refs, dh_refs, dg_refs):
            dh = dh_ref[...]
            dy = dh * g_ref[...]
            dx = dx + r * (dy - y * jnp.mean(dy * y, axis=-1, keepdims=True))
            part = jnp.sum(dh * y, axis=0, keepdims=True)

            @pl.when(i == 0)
            def _():
                dg_ref[...] = part

            @pl.when(i > 0)
            def _():
                dg_ref[...] += part

        dx_ref[...] = dx
        dxb_ref[...] = dx.astype(BF16)

    row = pl.BlockSpec((NORM_ROWS, D), lambda i: (i, 0))
    vec = pl.BlockSpec((1, D), lambda i: (0, 0))
    in_specs = [row, row]
    args = [x, dres]
    for g, dh in pairs:
        in_specs += [vec, row]
        args += [g.reshape(1, D), dh]
    outs = pl.pallas_call(
        body,
        out_shape=[jax.ShapeDtypeStruct((S, D), F32), jax.ShapeDtypeStruct((S, D), BF16)]
        + [jax.ShapeDtypeStruct((1, D), F32)] * n,
        grid=(S // NORM_ROWS,), in_specs=in_specs, out_specs=[row, row] + [vec] * n,
        compiler_params=_cparams(("arbitrary",)), name=name,
    )(*args)
    return outs[0], outs[1], list(outs[2:])


def _loss_head(x, g, target, name):
    def body(x_ref, g_ref, t_ref, loss_ref, dx_ref, dxb_ref, dg_ref):
        i = pl.program_id(0)
        xv = x_ref[...]
        gv = g_ref[...]
        r = lax.rsqrt(jnp.mean(xv * xv, axis=-1, keepdims=True) + RMS_EPS)
        y = xv * r
        err = y * gv - t_ref[...]
        lpart = jnp.broadcast_to(jnp.sum(err * err, keepdims=True) * (0.5 / D), (1, LANES))
        dh = err * (1.0 / D)
        dy = dh * gv
        dx = r * (dy - y * jnp.mean(dy * y, axis=-1, keepdims=True))
        part = jnp.sum(dh * y, axis=0, keepdims=True)

        @pl.when(i == 0)
        def _():
            dg_ref[...] = part
            loss_ref[...] = lpart

        @pl.when(i > 0)
        def _():
            dg_ref[...] += part
            loss_ref[...] += lpart

        dx_ref[...] = dx
        dxb_ref[...] = dx.astype(BF16)

    row = pl.BlockSpec((NORM_ROWS, D), lambda i: (i, 0))
    vec = pl.BlockSpec((1, D), lambda i: (0, 0))
    return pl.pallas_call(
        body,
        out_shape=[jax.ShapeDtypeStruct((1, LANES), F32), jax.ShapeDtypeStruct((S, D), F32),
                   jax.ShapeDtypeStruct((S, D), BF16), jax.ShapeDtypeStruct((1, D), F32)],
        grid=(S // NORM_ROWS,), in_specs=[row, vec, row],
        out_specs=[pl.BlockSpec((1, LANES), lambda i: (0, 0)), row, row, vec],
        compiler_params=_cparams(("arbitrary",)), name=name,
    )(x, g.reshape(1, D), target)


SCAN_ROWS = 256


def _split3(v):
    hi = v.astype(BF16)
    r1 = v - hi.astype(F32)
    mid = r1.astype(BF16)
    lo = (r1 - mid.astype(F32)).astype(BF16)
    return hi, mid, lo


def _tri_dot(tri, v):
    hi, mid, lo = _split3(v)
    dn = _DN["nn"]
    return (lax.dot_general(tri, hi, dn, preferred_element_type=F32)
            + lax.dot_general(tri, mid, dn, preferred_element_type=F32)
            + lax.dot_general(tri, lo, dn, preferred_element_type=F32))


def _log_sigmoid(z):
    return jnp.minimum(z, 0.0) - jnp.log(1.0 + jnp.exp(-jnp.abs(z)))


GATE_LANES = 6


def _gate_lane_tables():
    pq = np.zeros((3 * LANES, A_HEADS * HEAD_DIM), np.float32)
    pk = np.zeros((3 * LANES, A_HEADS * HEAD_DIM), np.float32)
    one_q = np.zeros((1, A_HEADS * HEAD_DIM), np.float32)
    one_k = np.zeros((1, A_HEADS * HEAD_DIM), np.float32)
    for h in range(A_HEADS):
        pos = (h // 2) * LANES + (HEAD_DIM if h % 2 == 0 else 0)
        for term in range(3):
            pq[term * LANES + h, pos + term] = 1.0
            pk[term * LANES + h, pos + 3 + term] = -1.0
        one_q[0, pos + 3:pos + GATE_LANES] = 1.0
        one_k[0, pos:pos + 3] = 1.0
    return jnp.asarray(pq, BF16), jnp.asarray(pk, BF16), jnp.asarray(one_q), jnp.asarray(one_k)


def _fgate_fwd(pf, bias, name):
    tri = jnp.tril(jnp.ones((SCAN_ROWS, SCAN_ROWS), F32)).astype(BF16)
    pq, pk, one_q, one_k = _gate_lane_tables()

    def body(pf_ref, b_ref, tri_ref, pq_ref, pk_ref, oq_ref, ok_ref, aq_ref, ak_ref, c_sc):
        carry = jnp.zeros((1, LANES), F32)
        for blk in range(S // SCAN_ROWS):
            rows = pl.ds(blk * SCAN_ROWS, SCAN_ROWS)
            lf = _log_sigmoid(pf_ref[rows, :] + b_ref[...])
            c_sc[...] = _tri_dot(tri_ref[...], lf) + carry
            carry = c_sc[pl.ds(SCAN_ROWS - 1, 1), :]
            terms = jnp.concatenate(_split3(c_sc[...]), axis=1)
            aq = lax.dot_general(terms, pq_ref[...], _DN["nn"], preferred_element_type=F32) + oq_ref[...]
            ak = lax.dot_general(terms, pk_ref[...], _DN["nn"], preferred_element_type=F32) + ok_ref[...]
            aq_ref[rows, :] = aq.astype(BF16)
            ak_ref[rows, :] = ak.astype(BF16)

    wide = jax.ShapeDtypeStruct((S, A_HEADS * HEAD_DIM), BF16)
    return pl.pallas_call(
        body, out_shape=[wide, wide], scratch_shapes=[pltpu.VMEM((SCAN_ROWS, LANES), F32)],
        compiler_params=_cparams(), name=name,
    )(pf, bias, tri, pq, pk, one_q, one_k)


def _fgate_bwd(pf, bias, dc_key, dc_query, name):
    triu = jnp.triu(jnp.ones((SCAN_ROWS, SCAN_ROWS), F32)).astype(BF16)

    def body(pf_ref, b_ref, dck_ref, dcq_ref, tri_ref, dpf_ref, db_ref, dlf_ref):
        carry = jnp.zeros((1, LANES), F32)
        db = jnp.zeros((1, LANES), F32)
        lane = lax.broadcasted_iota(jnp.int32, (SCAN_ROWS, LANES), 1)
        for blk in reversed(range(S // SCAN_ROWS)):
            rows = pl.ds(blk * SCAN_ROWS, SCAN_ROWS)
            dc = dck_ref[rows, :] + dcq_ref[rows, :]
            dlf_ref[rows, :] = _tri_dot(tri_ref[...], dc) + carry
            carry = dlf_ref[pl.ds(blk * SCAN_ROWS, 1), :]
            z = pf_ref[rows, :] + b_ref[...]
            e = jnp.exp(-jnp.abs(z))
            sig_neg = jnp.where(z >= 0.0, e, 1.0) / (1.0 + e)
            dz = jnp.where(lane < A_HEADS, dlf_ref[rows, :] * sig_neg, 0.0)
            dpf_ref[rows, :] = dz.astype(BF16)
            db = db + jnp.sum(dz, axis=0, keepdims=True)
        db_ref[...] = db

    return pl.pallas_call(
        body, out_shape=[jax.ShapeDtypeStruct((S, LANES), BF16), jax.ShapeDtypeStruct((1, LANES), F32)],
        scratch_shapes=[pltpu.VMEM((S, LANES), F32)],
        compiler_params=_cparams(), name=name,
    )(pf, bias, dc_key, dc_query, triu)


FOX_T = 512


def _first_head(shape):
    return lax.broadcasted_iota(jnp.int32, shape, len(shape) - 1) < HEAD_DIM


def _each_head(x, lo):
    zero = jnp.zeros_like(x)
    return jnp.where(lo, x, zero), jnp.where(lo, zero, x)


def _fox_pair_fwd(qkv, aug_q, aug_k, name):
    T = FOX_T
    nq = S // T
    NP = A_HEADS // 2

    def body(q_ref, k_ref, v_ref, aq_ref, ak_ref, o_ref, lse_ref, m_sc, l_sc, acc_sc):
        i = pl.program_id(1)
        j = pl.program_id(2)
        lo = _first_head((T, LANES))

        @pl.when(j == 0)
        def _():
            m_sc[...] = jnp.full((2, T, LANES), NEG_INF, F32)
            l_sc[...] = jnp.zeros((2, T, LANES), F32)
            acc_sc[...] = jnp.zeros((T, LANES), F32)

        def step(diagonal):
            qs = q_ref[...] * jnp.asarray(SCALE, BF16)
            aq, ak, kv = aq_ref[...], ak_ref[...], k_ref[...]
            q2 = (jnp.where(lo, qs, aq), jnp.where(lo, aq, qs))
            k2 = (jnp.where(lo, kv, ak), jnp.where(lo, ak, kv))
            if diagonal:
                causal = lax.broadcasted_iota(jnp.int32, (T, T), 0) >= lax.broadcasted_iota(jnp.int32, (T, T), 1)
            pv, alphas = None, []
            for h, vh in enumerate(_each_head(v_ref[...], lo)):
                s = lax.dot_general(q2[h], k2[h], _DN["nt"], preferred_element_type=F32)
                if diagonal:
                    s = jnp.where(causal, s, NEG_INF)
                m_prev = m_sc[h]
                m_new = jnp.maximum(m_prev, jnp.max(s, axis=1, keepdims=True))
                alpha = jnp.exp(m_prev - m_new)
                p = jnp.exp(s - jnp.tile(m_new, (1, T // LANES)))
                l_sc[h] = alpha * l_sc[h] + jnp.sum(p, axis=1, keepdims=True)
                m_sc[h] = m_new
                d = lax.dot_general(p.astype(BF16), vh, _DN["nn"], preferred_element_type=F32)
                pv = d if pv is None else pv + d
                alphas.append(alpha)
            acc_sc[...] = jnp.where(lo, alphas[0], alphas[1]) * acc_sc[...] + pv

        @pl.when(j < i)
        def _():
            step(False)

        @pl.when(j == i)
        def _():
            step(True)
            o_ref[...] = (acc_sc[...] * jnp.where(lo, 1.0 / l_sc[0], 1.0 / l_sc[1])).astype(BF16)
            for h in range(2):
                lse_ref[h] = (m_sc[h] + jnp.log(l_sc[h]))[:, 0:1]

    qs_ = pl.BlockSpec((T, LANES), lambda p, i, j: (i, p))
    ks = pl.BlockSpec((T, LANES), lambda p, i, j: (jnp.minimum(i, j), NP + p))
    vs = pl.BlockSpec((T, LANES), lambda p, i, j: (jnp.minimum(i, j), 2 * NP + p))
    aks = pl.BlockSpec((T, LANES), lambda p, i, j: (jnp.minimum(i, j), p))
    col = pl.BlockSpec((2, T, 1), lambda p, i, j: (p, i, 0))
    return pl.pallas_call(
        body, out_shape=[jax.ShapeDtypeStruct((S, A_HEADS * HEAD_DIM), BF16), jax.ShapeDtypeStruct((A_HEADS, S, 1), F32)],
        grid=(NP, nq, nq), in_specs=[qs_, ks, vs, qs_, aks], out_specs=[qs_, col],
        scratch_shapes=[pltpu.VMEM((2, T, LANES), F32), pltpu.VMEM((2, T, LANES), F32), pltpu.VMEM((T, LANES), F32)],
        compiler_params=_cparams(("parallel", "parallel", "arbitrary")), name=name,
    )(qkv, qkv, qkv, aug_q, aug_k)


def _fox_pair_bwd(qkv, do, lse_row, delta_row, aug_q, aug_k, name):
    T = FOX_T
    nq = S // T
    NP = A_HEADS // 2

    def body(q_ref, k_ref, v_ref, do_ref, lse_ref, dl_ref, aq_ref, ak_ref, dq_ref, dk_ref, dv_ref, dc_ref, dcq_ref,
             dq_sc, dk_sc, dv_sc, dc_sc):
        j = pl.program_id(1)
        i = pl.program_id(2)
        lo = _first_head((T, LANES))

        @pl.when(jnp.logical_and(j == 0, i == 0))
        def _():
            dq_sc[...] = jnp.zeros((S, LANES), F32)
            dcq_ref[...] = jnp.zeros((2, nq, 1, T), F32)

        @pl.when(i == j)
        def _():
            dk_sc[...] = jnp.zeros((T, LANES), F32)
            dv_sc[...] = jnp.zeros((T, LANES), F32)
            dc_sc[...] = jnp.zeros((2, T, 1), F32)

        def step(diagonal):
            qv = q_ref[...]
            kv = k_ref[...]
            dov = do_ref[...].astype(BF16)
            qs = qv * jnp.asarray(SCALE, BF16)
            aq, ak = aq_ref[...], ak_ref[...]
            q2 = (jnp.where(lo, qs, aq), jnp.where(lo, aq, qs))
            k2 = (jnp.where(lo, kv, ak), jnp.where(lo, ak, kv))
            if diagonal:
                causal = lax.broadcasted_iota(jnp.int32, (T, T), 1) >= lax.broadcasted_iota(jnp.int32, (T, T), 0)
            dv = dk = dq = None
            for h, (kh, vh, qh, doh) in enumerate(zip(_each_head(kv, lo), _each_head(v_ref[...], lo),
                                                      _each_head(qv, lo), _each_head(dov, lo))):
                st = lax.dot_general(k2[h], q2[h], _DN["nt"], preferred_element_type=F32)
                if diagonal:
                    st = jnp.where(causal, st, NEG_INF)
                pt = jnp.exp(st - lse_ref[h])
                d = lax.dot_general(pt.astype(BF16), doh, _DN["nn"], preferred_element_type=F32)
                dv = d if dv is None else dv + d
                dpt = lax.dot_general(vh, dov, _DN["nt"], preferred_element_type=F32)
                dst = pt * (dpt - dl_ref[h])
                dc_sc[h] -= jnp.sum(dst, axis=1, keepdims=True)
                dcq_ref[h, i] += jnp.sum(dst, axis=0, keepdims=True)
                dsb = (dst * SCALE).astype(BF16)
                d = lax.dot_general(dsb, qh, _DN["nn"], preferred_element_type=F32)
                dk = d if dk is None else dk + d
                d = lax.dot_general(dsb, kh, _DN["tn"], preferred_element_type=F32)
                dq = d if dq is None else dq + d
            dv_sc[...] += dv
            dk_sc[...] += dk
            rows = pl.ds(pl.multiple_of(i * T, T), T)
            dq_sc[rows, :] += dq

        @pl.when(i > j)
        def _():
            step(False)

        @pl.when(i == j)
        def _():
            step(True)

        @pl.when(i == nq - 1)
        def _():
            dk_ref[...] = dk_sc[...].astype(BF16)
            dv_ref[...] = dv_sc[...].astype(BF16)
            dc_ref[...] = dc_sc[...]

        @pl.when(jnp.logical_and(j == nq - 1, i == nq - 1))
        def _():
            dq_ref[...] = dq_sc[...].astype(BF16)

    qs = pl.BlockSpec((T, LANES), lambda p, j, i: (jnp.maximum(i, j), p))
    qrow = pl.BlockSpec((2, 1, T), lambda p, j, i: (p, 0, jnp.maximum(i, j)))
    ks = pl.BlockSpec((T, LANES), lambda p, j, i: (j, NP + p))
    vs = pl.BlockSpec((T, LANES), lambda p, j, i: (j, 2 * NP + p))
    kout = pl.BlockSpec((T, LANES), lambda p, j, i: (j, p))
    kcol = pl.BlockSpec((2, T, 1), lambda p, j, i: (p, j, 0))
    dqs = pl.BlockSpec((S, LANES), lambda p, j, i: (0, p))
    dcqs = pl.BlockSpec((2, nq, 1, T), lambda p, j, i: (p, 0, 0, 0))
    wide = jax.ShapeDtypeStruct((S, A_HEADS * HEAD_DIM), BF16)
    return pl.pallas_call(
        body,
        out_shape=[wide, wide, wide, jax.ShapeDtypeStruct((A_HEADS, S, 1), F32),
                   jax.ShapeDtypeStruct((A_HEADS, nq, 1, T), F32)],
        grid=(NP, nq, nq), in_specs=[qs, ks, vs, qs, qrow, qrow, qs, kout], out_specs=[dqs, kout, kout, kcol, dcqs],
        scratch_shapes=[pltpu.VMEM((S, LANES), F32), pltpu.VMEM((T, LANES), F32), pltpu.VMEM((T, LANES), F32),
                        pltpu.VMEM((2, T, 1), F32)],
        compiler_params=_cparams(("parallel", "arbitrary", "arbitrary")), name=name,
    )(qkv, qkv, qkv, do, lse_row, delta_row, aug_q, aug_k)


def _pair_rowdot(a, b, name):
    n = a.shape[1] // HEAD_DIM
    T = 1024

    def body(a_ref, b_ref, o_ref):
        prod = a_ref[...].astype(F32) * b_ref[...].astype(F32)
        lo = _first_head(prod.shape)
        o_ref[0] = jnp.sum(jnp.where(lo, prod, 0.0), axis=1, keepdims=True)
        o_ref[1] = jnp.sum(jnp.where(lo, 0.0, prod), axis=1, keepdims=True)

    blk = pl.BlockSpec((T, LANES), lambda p, i: (i, p))
    return pl.pallas_call(
        body, out_shape=jax.ShapeDtypeStruct((n, S, 1), F32), grid=(n // 2, S // T), in_specs=[blk, blk],
        out_specs=pl.BlockSpec((2, T, 1), lambda p, i: (p, i, 0)),
        compiler_params=_cparams(("parallel", "parallel")), name=name,
    )(a, b)


W = B_WIN
N_HG = 3 * B_HPG
N_BLK = S // W


def _dil_tables():
    slopes = np.exp2((-8.0 * np.arange(1, N_HG + 1, dtype=np.float32) / N_HG).astype(np.float32)).astype(np.float32)
    dil = np.repeat(np.array([d for _, d in B_GROUPS], np.float32), B_HPG)
    coef = (slopes * dil).astype(np.float32)
    nbs = np.repeat(np.array([S // d // W for _, d in B_GROUPS], np.int32), B_HPG)
    return jnp.asarray(coef), jnp.asarray(nbs)


DIL_SUB = 8
DIL_ROWS = DIL_SUB * W
DIL_STEPS = S // DIL_ROWS


def _dil_bias(coef, transposed):
    row = lax.broadcasted_iota(jnp.int32, (W, 2 * W), 0)
    col = lax.broadcasted_iota(jnp.int32, (W, 2 * W), 1)
    dist = (col - row) if transposed else (row + W - col)
    valid = jnp.logical_and(dist >= 0, dist <= W)
    return jnp.where(valid, -coef * dist.astype(F32), NEG_INF), col


NPG = B_HPG // 2
GROUP_W = B_HPG * HEAD_DIM


def _dil_pair_specs(qoff, koff, voff):
    prev_blk = lambda n: jnp.maximum(n * DIL_SUB - 1, 0)
    next_blk = lambda n: jnp.minimum((n + 1) * DIL_SUB, N_BLK - 1)
    return dict(
        o=pl.BlockSpec((DIL_ROWS, LANES), lambda h, n: (n, h)),
        o_next=pl.BlockSpec((W, LANES), lambda h, n: (next_blk(n), h)),
        q=pl.BlockSpec((DIL_ROWS, LANES), lambda h, n: (n, qoff + h)),
        q_next=pl.BlockSpec((W, LANES), lambda h, n: (next_blk(n), qoff + h)),
        k=pl.BlockSpec((DIL_ROWS, LANES), lambda h, n: (n, koff + h)),
        k_prev=pl.BlockSpec((W, LANES), lambda h, n: (prev_blk(n), koff + h)),
        v=pl.BlockSpec((DIL_ROWS, LANES), lambda h, n: (n, voff + h)),
        v_prev=pl.BlockSpec((W, LANES), lambda h, n: (prev_blk(n), voff + h)),
        col=pl.BlockSpec((2, DIL_ROWS, 1), lambda h, n: (h, n, 0)),
        col2=pl.BlockSpec((2, DIL_ROWS, 1), lambda h, n: (NPG + h, n, 0)),
        row=pl.BlockSpec((2, 1, DIL_ROWS), lambda h, n: (h, 0, n)),
        row_next=pl.BlockSpec((2, 1, W), lambda h, n: (h, 0, next_blk(n))),
        row2=pl.BlockSpec((2, 1, DIL_ROWS), lambda h, n: (NPG + h, 0, n)),
        row2_next=pl.BlockSpec((2, 1, W), lambda h, n: (NPG + h, 0, next_blk(n))),
        smem=pl.BlockSpec(memory_space=pltpu.SMEM))


def _dil_pair_fwd(g, q, k, v, qoff, koff, voff, name):
    coef_t, nbs_t = _dil_tables()

    def body(coef_ref, nbs_ref, q_ref, kh_ref, k_ref, vh_ref, v_ref, o_ref, lse_ref, kf, vf):
        hp = pl.program_id(0)
        n = pl.program_id(1)
        nbs = nbs_ref[B_HPG * g + 2 * hp]
        kf[0:W, :] = kh_ref[...]
        kf[W:, :] = k_ref[...]
        vf[0:W, :] = vh_ref[...]
        vf[W:, :] = v_ref[...]
        biases = [_dil_bias(coef_ref[B_HPG * g + 2 * hp + h], False) for h in range(2)]
        col = biases[0][1]
        lo = _first_head((W, LANES))
        lo2 = _first_head((2 * W, LANES))
        for b in range(DIL_SUB):
            first = lax.rem(n * DIL_SUB + b, nbs) == 0
            rows = slice(b * W, (b + 1) * W)
            both = slice(b * W, (b + 2) * W)
            qv = q_ref[rows, :]
            acc, inv = None, []
            for h, (kh, vh) in enumerate(zip(_each_head(kf[both, :], lo2), _each_head(vf[both, :], lo2))):
                s = lax.dot_general(qv, kh, _DN["nt"], preferred_element_type=F32) * SCALE + biases[h][0]
                s = jnp.where(jnp.logical_and(first, col < W), NEG_INF, s)
                m = jnp.max(s, axis=1, keepdims=True)
                p = jnp.exp(s - m)
                l = jnp.sum(p, axis=1, keepdims=True)
                d = lax.dot_general(p.astype(BF16), vh, _DN["nn"], preferred_element_type=F32)
                acc = d if acc is None else acc + d
                inv.append(1.0 / l)
                lse_ref[h, rows, :] = m + jnp.log(l)
            o_ref[rows, :] = acc * jnp.where(lo, inv[0], inv[1])

    sp = _dil_pair_specs(qoff, koff, voff)
    return pl.pallas_call(
        body, out_shape=[jax.ShapeDtypeStruct((S, GROUP_W), F32), jax.ShapeDtypeStruct((B_HPG, S, 1), F32)],
        grid=(NPG, DIL_STEPS), in_specs=[sp["smem"], sp["smem"], sp["q"], sp["k_prev"], sp["k"], sp["v_prev"], sp["v"]],
        out_specs=[sp["o"], sp["col"]], scratch_shapes=[pltpu.VMEM((DIL_ROWS + W, LANES), BF16)] * 2,
        compiler_params=_cparams(("parallel", "parallel")), name=name,
    )(coef_t, nbs_t, q, k, k, v, v)


def _dil_pair_merge(os, lses, name):
    T = 1024

    def body(o0_ref, o1_ref, o2_ref, l0_ref, l1_ref, l2_ref, om_ref, omb_ref, l_ref):
        lo = _first_head((T, LANES))
        weights = []
        for h in range(2):
            l0, l1, l2 = l0_ref[h], l1_ref[h], l2_ref[h]
            m = jnp.maximum(jnp.maximum(l0, l1), l2)
            e0, e1, e2 = jnp.exp(l0 - m), jnp.exp(l1 - m), jnp.exp(l2 - m)
            den = e0 + e1 + e2
            weights.append((e0 / den, e1 / den, e2 / den))
            l_ref[h] = m + jnp.log(den)
        om = (jnp.where(lo, weights[0][0], weights[1][0]) * o0_ref[...]
              + jnp.where(lo, weights[0][1], weights[1][1]) * o1_ref[...]
              + jnp.where(lo, weights[0][2], weights[1][2]) * o2_ref[...])
        om_ref[...] = om
        omb_ref[...] = om.astype(BF16)

    ob = pl.BlockSpec((T, LANES), lambda p, i: (i, p))
    lb = pl.BlockSpec((2, T, 1), lambda p, i: (p, i, 0))
    return pl.pallas_call(
        body,
        out_shape=[jax.ShapeDtypeStruct((S, B_OUT_W), F32), jax.ShapeDtypeStruct((S, B_OUT_W), BF16),
                   jax.ShapeDtypeStruct((B_HPG, S, 1), F32)],
        grid=(NPG, S // T), in_specs=[ob] * 3 + [lb] * 3, out_specs=[ob, ob, lb],
        compiler_params=_cparams(("parallel", "parallel")), name=name,
    )(*os, *lses)


def _dil_pair_dq(g, q, k, v, qoff, koff, voff, do, stats, name):
    coef_t, nbs_t = _dil_tables()

    def body(coef_ref, nbs_ref, q_ref, kh_ref, k_ref, vh_ref, v_ref, do_ref, l_ref, d_ref, dq_ref, kf, vf):
        hp = pl.program_id(0)
        n = pl.program_id(1)
        nbs = nbs_ref[B_HPG * g + 2 * hp]
        kf[0:W, :] = kh_ref[...]
        kf[W:, :] = k_ref[...]
        vf[0:W, :] = vh_ref[...]
        vf[W:, :] = v_ref[...]
        biases = [_dil_bias(coef_ref[B_HPG * g + 2 * hp + h], False) for h in range(2)]
        col = biases[0][1]
        lo2 = _first_head((2 * W, LANES))
        for b in range(DIL_SUB):
            first = lax.rem(n * DIL_SUB + b, nbs) == 0
            rows = slice(b * W, (b + 1) * W)
            both = slice(b * W, (b + 2) * W)
            qv = q_ref[rows, :]
            dov = do_ref[rows, :]
            acc = None
            for h, (kh, vh) in enumerate(zip(_each_head(kf[both, :], lo2), _each_head(vf[both, :], lo2))):
                s = lax.dot_general(qv, kh, _DN["nt"], preferred_element_type=F32) * SCALE + biases[h][0]
                s = jnp.where(jnp.logical_and(first, col < W), NEG_INF, s)
                p = jnp.exp(s - l_ref[h, rows, :])
                dp = lax.dot_general(dov, vh, _DN["nt"], preferred_element_type=F32)
                ds = (p * (dp - d_ref[h, rows, :]) * SCALE).astype(BF16)
                d = lax.dot_general(ds, kh, _DN["nn"], preferred_element_type=F32)
                acc = d if acc is None else acc + d
            dq_ref[rows, :] = acc.astype(BF16)

    sp = _dil_pair_specs(qoff, koff, voff)
    return pl.pallas_call(
        body, out_shape=jax.ShapeDtypeStruct((S, GROUP_W), BF16), grid=(NPG, DIL_STEPS),
        in_specs=[sp["smem"], sp["smem"], sp["q"], sp["k_prev"], sp["k"], sp["v_prev"], sp["v"], sp["o"], sp["col"],
                  sp["col2"]],
        out_specs=sp["o"], scratch_shapes=[pltpu.VMEM((DIL_ROWS + W, LANES), BF16)] * 2,
        compiler_params=_cparams(("parallel", "parallel")), name=name,
    )(coef_t, nbs_t, q, k, k, v, v, do, stats, stats)


def _dil_pair_dkv(g, q, k, v, qoff, koff, voff, do, stats, name):
    coef_t, nbs_t = _dil_tables()

    def body(coef_ref, nbs_ref, k_ref, v_ref, q_ref, qn_ref, do_ref, don_ref, l_ref, ln_ref, d_ref, dn_ref,
             dk_ref, dv_ref, qf, dof, lf, df):
        hp = pl.program_id(0)
        n = pl.program_id(1)
        nbs = nbs_ref[B_HPG * g + 2 * hp]
        qf[0:DIL_ROWS, :] = q_ref[...]
        qf[DIL_ROWS:, :] = qn_ref[...]
        dof[0:DIL_ROWS, :] = do_ref[...]
        dof[DIL_ROWS:, :] = don_ref[...]
        lf[:, :, 0:DIL_ROWS] = l_ref[...]
        lf[:, :, DIL_ROWS:] = ln_ref[...]
        df[:, :, 0:DIL_ROWS] = d_ref[...]
        df[:, :, DIL_ROWS:] = dn_ref[...]
        biases = [_dil_bias(coef_ref[B_HPG * g + 2 * hp + h], True) for h in range(2)]
        col = biases[0][1]
        lo = _first_head((W, LANES))
        lo2 = _first_head((2 * W, LANES))
        for b in range(DIL_SUB):
            no_next = lax.rem(n * DIL_SUB + b + 1, nbs) == 0
            rows = slice(b * W, (b + 1) * W)
            both = slice(b * W, (b + 2) * W)
            dd = dof[both, :]
            dk = dv = None
            for h, (kh, vh, qh, ddh) in enumerate(zip(_each_head(k_ref[rows, :], lo), _each_head(v_ref[rows, :], lo),
                                                      _each_head(qf[both, :], lo2), _each_head(dd, lo2))):
                st = lax.dot_general(kh, qh, _DN["nt"], preferred_element_type=F32) * SCALE + biases[h][0]
                st = jnp.where(jnp.logical_and(no_next, col >= W), NEG_INF, st)
                pt = jnp.exp(st - lf[h, :, both])
                d = lax.dot_general(pt.astype(BF16), ddh, _DN["nn"], preferred_element_type=F32)
                dv = d if dv is None else dv + d
                dpt = lax.dot_general(vh, dd, _DN["nt"], preferred_element_type=F32)
                dst = (pt * (dpt - df[h, :, both]) * SCALE).astype(BF16)
                d = lax.dot_general(dst, qh, _DN["nn"], preferred_element_type=F32)
                dk = d if dk is None else dk + d
            dk_ref[rows, :] = dk.astype(BF16)
            dv_ref[rows, :] = dv.astype(BF16)

    sp = _dil_pair_specs(qoff, koff, voff)
    wide = jax.ShapeDtypeStruct((S, GROUP_W), BF16)
    return pl.pallas_call(
        body, out_shape=[wide, wide], grid=(NPG, DIL_STEPS),
        in_specs=[sp["smem"], sp["smem"], sp["k"], sp["v"], sp["q"], sp["q_next"], sp["o"], sp["o_next"], sp["row"],
                  sp["row_next"], sp["row2"], sp["row2_next"]],
        out_specs=[sp["o"], sp["o"]],
        scratch_shapes=[pltpu.VMEM((DIL_ROWS + W, LANES), BF16)] * 2 + [pltpu.VMEM((2, 1, DIL_ROWS + W), F32)] * 2,
        compiler_params=_cparams(("parallel", "parallel")), name=name,
    )(coef_t, nbs_t, k, v, q, q, do, do, stats, stats, stats, stats)


FFN_ROWS = 512
FFN_COLS = 256
HALO = 8


def _shifted(u, halo, back):
    T = u.shape[0]
    rows = lax.broadcasted_iota(jnp.int32, u.shape, 0)
    if back:
        s1 = jnp.where(rows == 0, halo[HALO - 1:HALO, :], pltpu.roll(u, 1, 0))
        s2 = jnp.where(rows == 0, halo[HALO - 2:HALO - 1, :],
                       jnp.where(rows == 1, halo[HALO - 1:HALO, :], pltpu.roll(u, 2, 0)))
    else:
        s1 = jnp.where(rows == T - 1, halo[0:1, :], pltpu.roll(u, T - 1, 0))
        s2 = jnp.where(rows == T - 1, halo[1:2, :],
                       jnp.where(rows == T - 2, halo[0:1, :], pltpu.roll(u, T - 2, 0)))
    return s1, s2


def _conv_parts(u_ref, h_ref, w_ref, b_ref, first):
    out = []
    for p in range(2):
        u = u_ref[p]
        halo = jnp.where(first, 0.0, h_ref[p])
        u1, u2 = _shifted(u, halo, True)
        w = w_ref[p]
        out.append((w[0:1, :] * u2 + w[1:2, :] * u1 + w[2:3, :] * u + b_ref[p], u1, u2, u))
    return out


def _ffn_specs():
    T, C = FFN_ROWS, FFN_COLS
    blk = pl.BlockSpec((2, T, C), lambda j, i: (0, i, j))
    prev = pl.BlockSpec((2, HALO, C), lambda j, i: (0, jnp.maximum(i * (T // HALO) - 1, 0), j))
    nxt = pl.BlockSpec((2, HALO, C), lambda j, i: (0, jnp.minimum((i + 1) * (T // HALO), S // HALO - 1), j))
    wsp = pl.BlockSpec((2, 3, C), lambda j, i: (0, 0, j))
    bsp = pl.BlockSpec((2, 1, C), lambda j, i: (0, 0, j))
    one = pl.BlockSpec((T, C), lambda j, i: (i, j))
    return blk, prev, nxt, wsp, bsp, one


def _ffn_act_fwd(u, w, b, name):
    blk, prev, _, wsp, bsp, one = _ffn_specs()

    def body(u_ref, h_ref, w_ref, b_ref, o_ref):
        (a, _, _, _), (g, _, _, _) = _conv_parts(u_ref, h_ref, w_ref, b_ref, pl.program_id(1) == 0)
        o_ref[...] = (g / (1.0 + jnp.exp(-g)) * a).astype(BF16)

    return pl.pallas_call(
        body, out_shape=jax.ShapeDtypeStruct((S, FF), BF16), grid=(FF // FFN_COLS, S // FFN_ROWS),
        in_specs=[blk, prev, wsp, bsp], out_specs=one,
        compiler_params=_cparams(("parallel", "parallel")), name=name,
    )(u, u, w, b)


def _ffn_act_bwd(u, dact, w, b, name):
    blk, prev, _, wsp, bsp, one = _ffn_specs()

    def body(u_ref, h_ref, da_ref, w_ref, b_ref, duc_ref, dwb_ref):
        i = pl.program_id(1)
        (a, a1, a2, a0), (g, g1, g2, g0) = _conv_parts(u_ref, h_ref, w_ref, b_ref, i == 0)
        dact_v = da_ref[...]
        sg = 1.0 / (1.0 + jnp.exp(-g))
        d_a = dact_v * (g * sg)
        d_g = dact_v * a * (sg * (1.0 + g * (1.0 - sg)))
        duc_ref[0] = d_a
        duc_ref[1] = d_g

        @pl.when(i == 0)
        def _():
            dwb_ref[...] = jnp.zeros(dwb_ref.shape, F32)

        for p, (d, s2, s1, s0) in enumerate(((d_a, a2, a1, a0), (d_g, g2, g1, g0))):
            dwb_ref[p, 0:1, :] += jnp.sum(d * s2, axis=0, keepdims=True)
            dwb_ref[p, 1:2, :] += jnp.sum(d * s1, axis=0, keepdims=True)
            dwb_ref[p, 2:3, :] += jnp.sum(d * s0, axis=0, keepdims=True)
            dwb_ref[p, 3:4, :] += jnp.sum(d, axis=0, keepdims=True)

    return pl.pallas_call(
        body, out_shape=[jax.ShapeDtypeStruct((2, S, FF), F32), jax.ShapeDtypeStruct((2, 8, FF), F32)],
        grid=(FF // FFN_COLS, S // FFN_ROWS), in_specs=[blk, prev, one, wsp, bsp],
        out_specs=[blk, pl.BlockSpec((2, 8, FFN_COLS), lambda j, i: (0, 0, j))],
        compiler_params=_cparams(("parallel", "arbitrary")), name=name,
    )(u, u, dact, w, b)


def _ffn_conv_bwd(duc, w, name):
    blk, _, nxt, wsp, _, _ = _ffn_specs()
    last = S // FFN_ROWS - 1

    def body(d_ref, h_ref, w_ref, du_ref):
        is_last = pl.program_id(1) == last
        for p in range(2):
            d = d_ref[p]
            halo = jnp.where(is_last, 0.0, h_ref[p])
            d1, d2 = _shifted(d, halo, False)
            wv = w_ref[p]
            du_ref[p] = (wv[2:3, :] * d + wv[1:2, :] * d1 + wv[0:1, :] * d2).astype(BF16)

    return pl.pallas_call(
        body, out_shape=jax.ShapeDtypeStruct((2, S, FF), BF16), grid=(FF // FFN_COLS, S // FFN_ROWS),
        in_specs=[blk, nxt, wsp], out_specs=blk,
        compiler_params=_cparams(("parallel", "parallel")), name=name,
    )(duc, duc, w)


def _adam_update(w, gv, m, v):
    c1 = 1.0 / (1.0 - ADAM_B1 ** ADAM_STEP)
    c2 = 1.0 / (1.0 - ADAM_B2 ** ADAM_STEP)
    mn = ADAM_B1 * m + (1.0 - ADAM_B1) * gv
    vn = ADAM_B2 * v + (1.0 - ADAM_B2) * (gv * gv)
    return -ADAM_LR * ((mn * c1) / (jnp.sqrt(vn * c2) + ADAM_EPS) + ADAM_WD * w), mn, vn


def _adamw(w, g, m, v, name):
    rows = w.shape[0]
    T = 8
    for cand in (256, 128, 64, 32, 16, 8):
        if rows % cand == 0:
            T = cand
            break

    def body(w_ref, g_ref, m_ref, v_ref, d_ref, mo_ref, vo_ref):
        d_ref[...], mo_ref[...], vo_ref[...] = _adam_update(w_ref[...], g_ref[...], m_ref[...], v_ref[...])

    blk = pl.BlockSpec((T, w.shape[1]), lambda i: (i, 0))
    sds = jax.ShapeDtypeStruct(w.shape, F32)
    return pl.pallas_call(
        body, out_shape=[sds, sds, sds], grid=(rows // T,), in_specs=[blk] * 4, out_specs=[blk] * 3,
        compiler_params=_cparams(("parallel",)), name=name,
    )(w, g, m, v)


ANY = pl.BlockSpec(memory_space=pl.ANY)


def _place():
    x, y, c = lax.axis_index("x"), lax.axis_index("y"), lax.axis_index("c")
    chips = [(1 - x, y), (x, 1 - y), (1 - x, 1 - y)]
    return x, y, c, chips


def _place_own(w, slot_arr, name):
    rows = w.shape[0]
    T = 16
    for cand in (2048, 1024, 512, 256, 128, 64, 32, 16):
        if rows % cand == 0:
            T = cand
            break

    def body(k_ref, w_ref, o_ref):
        o_ref[...] = w_ref[...]

    return pl.pallas_call(
        body, out_shape=jax.ShapeDtypeStruct((N_CHIPS, rows, FLAT_W), w.dtype),
        grid_spec=pltpu.PrefetchScalarGridSpec(
            num_scalar_prefetch=1, grid=(rows // T,),
            in_specs=[pl.BlockSpec((T, FLAT_W), lambda i, k: (i, 0))],
            out_specs=pl.BlockSpec((None, T, FLAT_W), lambda i, k: (k[0], i, 0))),
        compiler_params=_cparams(("parallel",)), name=name,
    )(slot_arr, w)


def _allgather_shards(w, buf):
    half_rows = w.shape[0] // 2
    assert half_rows % 16 == 0

    def body(w_ref, buf_ref, g_ref, send_sems, recv_sems):
        x, y, c, chips = _place()
        myk = 2 * x + y
        sibling = (x, y, 1 - c)
        h0 = pl.multiple_of(c * half_rows, 16)
        h1 = pl.multiple_of((1 - c) * half_rows, 16)

        def half(k, start):
            return g_ref.at[k, pl.ds(start, half_rows), :]

        def rcopy(sem, src, dst, to):
            return pltpu.make_async_remote_copy(src_ref=src, dst_ref=dst, send_sem=send_sems.at[sem],
                                                recv_sem=recv_sems.at[sem], device_id=to, device_id_type=MESH)

        ici = [rcopy(r, w_ref.at[pl.ds(h0, half_rows), :], half(myk, h0), (*chip, c)) for r, chip in enumerate(chips)]
        for cp in ici:
            cp.start()
        ks = [2 * cx + cy for cx, cy in chips]
        fwd = [rcopy(3 + r, half(ks[r], h0), half(ks[r], h0), sibling) for r in range(3)]
        for r in range(3):
            rcopy(r, half(ks[r], h0), half(ks[r], h0), (*chips[r], c)).wait_recv()
            fwd[r].start()
        for r in range(3):
            rcopy(3 + r, half(ks[r], h1), half(ks[r], h1), sibling).wait_recv()
        for cp in ici + fwd:
            cp.wait_send()

    return pl.pallas_call(
        body, out_shape=jax.ShapeDtypeStruct(buf.shape, w.dtype), in_specs=[ANY, ANY], out_specs=ANY,
        scratch_shapes=[pltpu.SemaphoreType.DMA((6,)), pltpu.SemaphoreType.DMA((6,))],
        input_output_aliases={1: 0},
        compiler_params=pltpu.CompilerParams(has_side_effects=True), name="allgather_shards",
    )(w, buf)


HBM_SPEC = pl.BlockSpec(memory_space=pltpu.HBM)
SEM_SPEC = pl.BlockSpec(memory_space=pltpu.SEMAPHORE)
DATAFLOW = pltpu.SideEffectType.DATAFLOW_SIDE_EFFECTING
OWN_SLOT = 3


def _late_gather_start(w, land):
    def body(w_ref, land_ref, send_sems, recv_sems, w_thru, land_thru, token):
        x, y, c, chips = _place()
        for r, chip in enumerate(chips):
            pltpu.make_async_remote_copy(src_ref=w_ref, dst_ref=land_ref.at[r], send_sem=send_sems.at[r],
                                         recv_sem=recv_sems.at[r], device_id=(*chip, c), device_id_type=MESH).start()
        token[...] = jnp.zeros_like(token)

    return pl.pallas_call(
        body, name="late_gather_start",
        out_shape=(pltpu.SemaphoreType.DMA((3,)), pltpu.SemaphoreType.DMA((3,)), pltpu.HBM(w.shape, w.dtype),
                   pltpu.HBM(land.shape, land.dtype), jax.ShapeDtypeStruct((8, LANES), F32)),
        in_specs=(HBM_SPEC, HBM_SPEC),
        out_specs=(SEM_SPEC, SEM_SPEC, HBM_SPEC, HBM_SPEC, pl.BlockSpec(memory_space=pltpu.VMEM)),
        input_output_aliases={0: 2, 1: 3}, compiler_params=pltpu.CompilerParams(has_side_effects=DATAFLOW),
    )(pltpu.with_memory_space_constraint(w, pltpu.HBM), pltpu.with_memory_space_constraint(land, pltpu.HBM))


def _late_gather_wait(send_sems, recv_sems, w_thru, land_thru, after):
    def body(w_ref, land_ref, send_sems, recv_sems, after_ref, w_dead, got_ref):
        x, y, c, chips = _place()
        for r, chip in enumerate(chips):
            cp = pltpu.make_async_remote_copy(src_ref=w_ref, dst_ref=land_ref.at[r], send_sem=send_sems.at[r],
                                              recv_sem=recv_sems.at[r], device_id=(*chip, c), device_id_type=MESH)
            cp.wait_send()
            cp.wait_recv()

    return pl.pallas_call(
        body, name="late_gather_wait",
        out_shape=(pltpu.HBM(w_thru.shape, w_thru.dtype), pltpu.HBM(land_thru.shape, land_thru.dtype)),
        in_specs=(HBM_SPEC, HBM_SPEC, SEM_SPEC, SEM_SPEC, pl.BlockSpec(memory_space=pl.ANY)),
        out_specs=(HBM_SPEC, HBM_SPEC), input_output_aliases={0: 0, 1: 1},
        compiler_params=pltpu.CompilerParams(has_side_effects=DATAFLOW),
    )(w_thru, land_thru, send_sems, recv_sems, after)


def _flat_tile(rows):
    return next(t for t in (2048, 1024, 512, 256, 128, 64, 32, 16) if rows % t == 0)


def _sibling_swap_half(g, tag):
    half = g.shape[1] // 2

    def body(g_ref, o_ref, send_sem, recv_sem):
        x, y, c, _ = _place()
        theirs = pl.multiple_of((1 - c) * half, 8)
        cp = pltpu.make_async_remote_copy(src_ref=g_ref.at[:, pl.ds(theirs, half), :], dst_ref=o_ref,
                                          send_sem=send_sem, recv_sem=recv_sem, device_id=(x, y, 1 - c),
                                          device_id_type=MESH)
        cp.start()
        cp.wait()

    return pl.pallas_call(
        body, out_shape=jax.ShapeDtypeStruct((N_CHIPS, half, FLAT_W), F32), in_specs=[ANY], out_specs=ANY,
        scratch_shapes=[pltpu.SemaphoreType.DMA, pltpu.SemaphoreType.DMA],
        compiler_params=pltpu.CompilerParams(has_side_effects=True), name=f"rs_sibling_swap_{tag}",
    )(g)


def _sibling_swap_start(g, land):
    half = g.shape[1] // 2

    def body(g_ref, land_ref, send_sem, recv_sem, g_thru, land_thru, token):
        x, y, c, _ = _place()
        theirs = pl.multiple_of((1 - c) * half, 8)
        pltpu.make_async_remote_copy(src_ref=g_ref.at[:, pl.ds(theirs, half), :], dst_ref=land_ref, send_sem=send_sem,
                                     recv_sem=recv_sem, device_id=(x, y, 1 - c), device_id_type=MESH).start()
        token[...] = jnp.zeros_like(token)

    return pl.pallas_call(
        body, name="rs_swap_start",
        out_shape=(pltpu.SemaphoreType.DMA(()), pltpu.SemaphoreType.DMA(()), pltpu.HBM(g.shape, g.dtype),
                   pltpu.HBM(land.shape, land.dtype), jax.ShapeDtypeStruct((8, LANES), F32)),
        in_specs=(HBM_SPEC, HBM_SPEC),
        out_specs=(SEM_SPEC, SEM_SPEC, HBM_SPEC, HBM_SPEC, pl.BlockSpec(memory_space=pltpu.VMEM)),
        input_output_aliases={0: 2, 1: 3}, compiler_params=pltpu.CompilerParams(has_side_effects=DATAFLOW),
    )(pltpu.with_memory_space_constraint(g, pltpu.HBM), pltpu.with_memory_space_constraint(land, pltpu.HBM))


def _sibling_swap_wait(send_sem, recv_sem, g_thru, land_thru, after):
    half = land_thru.shape[1]

    def body(g_ref, land_ref, send_sem, recv_sem, after_ref, g_done, got_ref):
        x, y, c, _ = _place()
        theirs = pl.multiple_of((1 - c) * half, 8)
        cp = pltpu.make_async_remote_copy(src_ref=g_ref.at[:, pl.ds(theirs, half), :], dst_ref=land_ref,
                                          send_sem=send_sem, recv_sem=recv_sem, device_id=(x, y, 1 - c),
                                          device_id_type=MESH)
        cp.wait_send()
        cp.wait_recv()

    return pl.pallas_call(
        body, name="rs_swap_wait",
        out_shape=(pltpu.HBM(g_thru.shape, g_thru.dtype), pltpu.HBM(land_thru.shape, land_thru.dtype)),
        in_specs=(HBM_SPEC, HBM_SPEC, SEM_SPEC, SEM_SPEC, pl.BlockSpec(memory_space=pl.ANY)),
        out_specs=(HBM_SPEC, HBM_SPEC), input_output_aliases={0: 0, 1: 1},
        compiler_params=pltpu.CompilerParams(has_side_effects=DATAFLOW),
    )(g_thru, land_thru, send_sem, recv_sem, after)


def _pair_sum(g, other, c_arr, tag):
    half = other.shape[1]
    T = _flat_tile(half)

    def body(c_ref, g_ref, o_ref, s_ref):
        s_ref[...] = (g_ref[...] + o_ref[...]).astype(BF16)

    nb = half // T
    return pl.pallas_call(
        body, out_shape=jax.ShapeDtypeStruct((N_CHIPS, half, FLAT_W), BF16),
        grid_spec=pltpu.PrefetchScalarGridSpec(
            num_scalar_prefetch=1, grid=(N_CHIPS, nb),
            in_specs=[pl.BlockSpec((None, T, FLAT_W), lambda k, i, c: (k, c[0] * nb + i, 0)),
                      pl.BlockSpec((None, T, FLAT_W), lambda k, i, c: (k, i, 0))],
            out_specs=pl.BlockSpec((None, T, FLAT_W), lambda k, i, c: (k, i, 0))),
        compiler_params=_cparams(("parallel", "parallel")), name=f"rs_pair_sum_{tag}",
    )(c_arr, g, other)


def _chip_exchange_start(s, land, tag):
    def body(s_ref, land_ref, send_sems, recv_sems, s_thru, land_thru, token):
        x, y, c, chips = _place()
        for r, (cx, cy) in enumerate(chips):
            pltpu.make_async_remote_copy(src_ref=s_ref.at[2 * cx + cy], dst_ref=land_ref.at[r], send_sem=send_sems.at[r],
                                         recv_sem=recv_sems.at[r], device_id=(cx, cy, c), device_id_type=MESH).start()
        token[...] = jnp.zeros_like(token)

    return pl.pallas_call(
        body, name=f"rs_exchange_start_{tag}",
        out_shape=(pltpu.SemaphoreType.DMA((3,)), pltpu.SemaphoreType.DMA((3,)), pltpu.HBM(s.shape, s.dtype),
                   pltpu.HBM(land.shape, land.dtype), jax.ShapeDtypeStruct((8, LANES), F32)),
        in_specs=(HBM_SPEC, HBM_SPEC),
        out_specs=(SEM_SPEC, SEM_SPEC, HBM_SPEC, HBM_SPEC, pl.BlockSpec(memory_space=pltpu.VMEM)),
        input_output_aliases={0: 2, 1: 3}, compiler_params=pltpu.CompilerParams(has_side_effects=DATAFLOW),
    )(pltpu.with_memory_space_constraint(s, pltpu.HBM), pltpu.with_memory_space_constraint(land, pltpu.HBM))


def _chip_exchange_wait(send_sems, recv_sems, s_thru, land_thru, after, tag):
    def body(s_ref, land_ref, send_sems, recv_sems, after_ref, s_done, got_ref):
        x, y, c, chips = _place()
        for r, (cx, cy) in enumerate(chips):
            cp = pltpu.make_async_remote_copy(src_ref=s_ref.at[2 * cx + cy], dst_ref=land_ref.at[r],
                                              send_sem=send_sems.at[r], recv_sem=recv_sems.at[r], device_id=(cx, cy, c),
                                              device_id_type=MESH)
            cp.wait_send()
            cp.wait_recv()

    return pl.pallas_call(
        body, name=f"rs_exchange_wait_{tag}",
        out_shape=(pltpu.HBM(s_thru.shape, s_thru.dtype), pltpu.HBM(land_thru.shape, land_thru.dtype)),
        in_specs=(HBM_SPEC, HBM_SPEC, SEM_SPEC, SEM_SPEC, pl.BlockSpec(memory_space=pl.ANY)),
        out_specs=(HBM_SPEC, HBM_SPEC), input_output_aliases={0: 0, 1: 1},
        compiler_params=pltpu.CompilerParams(has_side_effects=DATAFLOW),
    )(s_thru, land_thru, send_sems, recv_sems, after)


def _chip_sum(s, r, k_arr, tag):
    half = s.shape[1]
    T = _flat_tile(half)

    def body(k_ref, s_ref, r_ref, o_ref):
        o_ref[...] = ((s_ref[...].astype(F32) + r_ref[0].astype(F32)) + r_ref[1].astype(F32)) + r_ref[2].astype(F32)

    return pl.pallas_call(
        body, out_shape=jax.ShapeDtypeStruct((half, FLAT_W), F32),
        grid_spec=pltpu.PrefetchScalarGridSpec(
            num_scalar_prefetch=1, grid=(half // T,),
            in_specs=[pl.BlockSpec((None, T, FLAT_W), lambda i, k: (k[0], i, 0)),
                      pl.BlockSpec((3, T, FLAT_W), lambda i, k: (0, i, 0))],
            out_specs=pl.BlockSpec((T, FLAT_W), lambda i, k: (i, 0))),
        compiler_params=_cparams(("parallel",)), name=f"rs_chip_sum_{tag}",
    )(k_arr, s, r)


def _sibling_send(t, tag):
    def body(t_ref, o_ref, send_sem, recv_sem):
        x, y, c, _ = _place()
        cp = pltpu.make_async_remote_copy(src_ref=t_ref, dst_ref=o_ref, send_sem=send_sem, recv_sem=recv_sem,
                                          device_id=(x, y, 1 - c), device_id_type=MESH)
        cp.start()
        cp.wait()

    return pl.pallas_call(
        body, out_shape=jax.ShapeDtypeStruct(t.shape, F32), in_specs=[ANY], out_specs=ANY,
        scratch_shapes=[pltpu.SemaphoreType.DMA, pltpu.SemaphoreType.DMA],
        compiler_params=pltpu.CompilerParams(has_side_effects=True), name=f"rs_sibling_send_{tag}",
    )(t)


def _allreduce_small(v):
    def body(v_ref, o_ref, buf, send_sems, recv_sems):
        x, y, c, _ = _place()
        me = 4 * x + 2 * y + c
        buf[me] = v_ref[...]
        cps = []
        for mask in range(1, 8):
            a, b, d = (mask >> 2) & 1, (mask >> 1) & 1, mask & 1
            peer = (x + a - 2 * a * x, y + b - 2 * b * y, c + d - 2 * d * c)
            cps.append(pltpu.make_async_remote_copy(
                src_ref=v_ref, dst_ref=buf.at[me], send_sem=send_sems.at[mask - 1], recv_sem=recv_sems.at[mask - 1],
                device_id=peer, device_id_type=MESH))
        for cp in cps:
            cp.start()
        for cp in cps:
            cp.wait()
        total = buf[0]
        for dev in range(1, 8):
            total = total + buf[dev]
        o_ref[...] = total

    vm = pl.BlockSpec(memory_space=pltpu.VMEM)
    return pl.pallas_call(
        body, out_shape=jax.ShapeDtypeStruct((SMALL_ROWS, 1024), F32), in_specs=[vm], out_specs=vm,
        scratch_shapes=[pltpu.VMEM((8, SMALL_ROWS, 1024), F32), pltpu.SemaphoreType.DMA((7,)),
                        pltpu.SemaphoreType.DMA((7,))],
        compiler_params=pltpu.CompilerParams(has_side_effects=True), name="allreduce_small",
    )(v)


def _col_to_row(t):
    return t.reshape(t.shape[0], 1, S)


def _residue_rows(t, d, inverse=False):
    if d == 1:
        return t
    shape = (d, S // d) if inverse else (S // d, d)
    return t.reshape(shape + t.shape[1:]).transpose(1, 0, 2).reshape(t.shape)


def _residue_vecs(t, d, inverse=False):
    if d == 1:
        return t
    shape = (d, S // d) if inverse else (S // d, d)
    return t.reshape((t.shape[0],) + shape).transpose(0, 2, 1).reshape(t.shape)


def _ffn_fwd(x, g, w_up, cw, cb, w_down, tag):
    (h,) = _rms_fwd(x, [g], f"{tag}_norm")
    u = _mm(h, w_up, mode="nn", tm=1024, tn=1408, tk=1024, o_split=2, name=f"{tag}_up")
    act = _ffn_act_fwd(u, cw, cb, f"{tag}_act")
    x_out = _mm(act, w_down, mode="nn", tm=1024, tn=512, tk=FF, res=x, name=f"{tag}_down")
    return x_out, (h, u, act)


def _ffn_bwd(x, g, w_up, cw, cb, w_down, saved, dx, dxb, tag):
    h, u, act = saved
    d_w_down = _mm(act, dxb, mode="tn", tm=1408, tn=512, tk=2048, name=f"{tag}_dwdown")
    dact = _mm(dxb, w_down, mode="nt", tm=1024, tn=1408, tk=1024, name=f"{tag}_dact")
    duc, dwb = _ffn_act_bwd(u, dact, cw, cb, f"{tag}_dgate")
    du = _ffn_conv_bwd(duc, cw, f"{tag}_dconv")
    d_w_up = _mm(h, du, mode="tn", tm=1024, tn=1408, tk=2048, b_split=2, name=f"{tag}_dwup")
    dh = _mm(du, w_up, mode="nt", tm=1024, tn=512, tk=FF, a_split=1, name=f"{tag}_dh")
    dx_new, dxb_new, (dg,) = _rms_bwd(x, dx, [(g, dh)], f"{tag}_dnorm")
    d_cw = dwb[:, 0:3, :].transpose(1, 0, 2).reshape(3, 2 * FF)
    d_cb = dwb[:, 3, :].reshape(2 * FF)
    return dx_new, dxb_new, dict(w_up=d_w_up, w_down=d_w_down, conv_w=d_cw, conv_b=d_cb, norm_g=dg.reshape(D))


def _local_step(x, target, p, late_weights, late_grads_ready, late_grads_continue):
    g = {}
    (h1,) = _rms_fwd(x, [p["mix_norm_g"][0]], "a_norm")
    w_qkv = p["a_w_in"][:, :QKV_W]
    w_f = jnp.pad(p["a_w_in"][:, QKV_W:], ((0, 0), (0, LANES - A_HEADS)))
    b_f = jnp.pad(p["a_b_f"].reshape(1, A_HEADS), ((0, 0), (0, LANES - A_HEADS)))
    qkv = _mm(h1, w_qkv, mode="nn", tm=2048, tn=512, tk=1024, out_dtype=BF16, name="a_qkv")
    pf = _mm(h1, w_f, mode="nn", tm=1024, tn=LANES, tk=1024, name="a_gate")
    aug_q, aug_k = _fgate_fwd(pf, b_f, "a_gate_scan")
    oa2, lse_a = _fox_pair_fwd(qkv, aug_q, aug_k, "a_attn")
    x1 = _mm(oa2, p["a_w_out"], mode="nn", tm=2048, tn=512, tk=1024, res=x, name="a_out")
    p = {**p, **late_weights(x1)}
    x2, ffn0 = _ffn_fwd(x1, p["ffn_norm_g"][0], p["ffn_w_up"][0], p["conv_w"][0], p["conv_b"][0], p["ffn_w_down"][0], "f0")
    hk, h3 = _rms_fwd(x2, [p["kv_norm_g"], p["mix_norm_g"][1]], "kv_b_norm")
    kvb = _mm(hk, p["w_kv"], mode="nn", tm=2048, tn=512, tk=1024, out_dtype=BF16, name="kv_proj")
    qb = _mm(h3, p["b_w_q"], mode="nn", tm=2048, tn=512, tk=1024, out_dtype=BF16, name="b_q")
    dil_in = []
    for gi, (_, d) in enumerate(B_GROUPS):
        if d == 1:
            dil_in.append((qb, kvb, kvb, gi * NPG, gi * NPG, (3 + gi) * NPG))
        else:
            qg = _residue_rows(qb[:, gi * GROUP_W:(gi + 1) * GROUP_W], d)
            kvg = _residue_rows(kvb.reshape(S, 2, 3, GROUP_W)[:, :, gi, :].reshape(S, 2 * GROUP_W), d)
            dil_in.append((qg, kvg, kvg, 0, 0, NPG))
    o_g, lse_g = [], []
    for gi, (_, d) in enumerate(B_GROUPS):
        qg, kg, vg, qoff, koff, voff = dil_in[gi]
        og, lg = _dil_pair_fwd(gi, qg, kg, vg, qoff, koff, voff, f"b_attn{gi}")
        o_g.append(_residue_rows(og, d, inverse=True))
        lse_g.append(_residue_vecs(lg, d, inverse=True))
    ob, ob2, lse_b = _dil_pair_merge(o_g, lse_g, "b_merge")
    x3 = _mm(ob2, p["b_w_out"], mode="nn", tm=2048, tn=512, tk=B_OUT_W, res=x2, name="b_out")
    x4, ffn1 = _ffn_fwd(x3, p["ffn_norm_g"][1], p["ffn_w_up"][1], p["conv_w"][1], p["conv_b"][1], p["ffn_w_down"][1], "f1")
    loss, dx, dxb, dg_final = _loss_head(x4, p["final_norm_g"], target, "loss_head")
    g["final_norm_g"] = dg_final.reshape(D)

    dx, dxb, gf1 = _ffn_bwd(x3, p["ffn_norm_g"][1], p["ffn_w_up"][1], p["conv_w"][1], p["conv_b"][1], p["ffn_w_down"][1],
                            ffn1, dx, dxb, "f1")
    g["b_w_out"] = _mm(ob2, dxb, mode="tn", tm=B_OUT_W, tn=512, tk=1024, name="b_dwout")
    dob = _mm(dxb, p["b_w_out"], mode="nt", tm=2048, tn=B_OUT_W, tk=1024, name="b_do")
    delta_b = _pair_rowdot(dob, ob, "b_delta")
    dob16 = dob.astype(BF16)
    stats_b = jnp.concatenate([lse_b, delta_b], axis=0)
    dq_g, dk_g, dv_g = [], [], []
    for gi, (_, d) in enumerate(B_GROUPS):
        qg, kg, vg, qoff, koff, voff = dil_in[gi]
        dog, stats_d = _residue_rows(dob16, d), _residue_vecs(stats_b, d)
        dqd = _dil_pair_dq(gi, qg, kg, vg, qoff, koff, voff, dog, stats_d, f"b_dq{gi}")
        dkd, dvd = _dil_pair_dkv(gi, qg, kg, vg, qoff, koff, voff, dog, _col_to_row(stats_d), f"b_dkv{gi}")
        dq_g.append(_residue_rows(dqd, d, inverse=True))
        dk_g.append(_residue_rows(dkd, d, inverse=True))
        dv_g.append(_residue_rows(dvd, d, inverse=True))
    dqb = jnp.concatenate(dq_g, axis=1)
    dkvb = jnp.concatenate(dk_g + dv_g, axis=1)
    g["b_w_q"] = _mm(h3, dqb, mode="tn", tm=1024, tn=512, tk=S, name="b_dwq")
    dh3 = _mm(dqb, p["b_w_q"], mode="nt", tm=2048, tn=512, tk=B_Q_W, name="b_dh")
    g["w_kv"] = _mm(hk, dkvb, mode="tn", tm=1024, tn=512, tk=S, name="kv_dw")
    dhk = _mm(dkvb, p["w_kv"], mode="nt", tm=1024, tn=512, tk=3072, name="kv_dh")
    dx, dxb, (dg_mix1, dg_kv) = _rms_bwd(x2, dx, [(p["mix_norm_g"][1], dh3), (p["kv_norm_g"], dhk)], "b_dnorm")
    g["kv_norm_g"] = dg_kv.reshape(D)
    dx, dxb, gf0 = _ffn_bwd(x1, p["ffn_norm_g"][0], p["ffn_w_up"][0], p["conv_w"][0], p["conv_b"][0], p["ffn_w_down"][0],
                            ffn0, dx, dxb, "f0")
    g["ffn_w_up"] = [gf0["w_up"], gf1["w_up"]]
    g["ffn_w_down"] = [gf0["w_down"], gf1["w_down"]]
    g["ffn_conv_w"] = jnp.stack([gf0["conv_w"], gf1["conv_w"]])
    token = late_grads_ready(g)
    a_w_out_t = p["a_w_out"] + token[0, 0].astype(BF16)
    g["a_w_out"] = _mm(oa2, dxb, mode="tn", tm=1024, tn=512, tk=S, name="a_dwout")
    doa = _mm(dxb, a_w_out_t, mode="nt", tm=2048, tn=512, tk=1024, name="a_do")
    delta_a = _pair_rowdot(doa, oa2, "a_delta")
    token = late_grads_continue(delta_a)
    delta_row = _col_to_row(delta_a) + token[0, 0]
    dqa, dka, dva, dck, dcq = _fox_pair_bwd(qkv, doa, _col_to_row(lse_a), delta_row, aug_q, aug_k, "a_dattn")
    dqkv = jnp.concatenate([dqa, dka, dva], axis=1)
    pad_heads = lambda t: jnp.pad(t.reshape(A_HEADS, S).T, ((0, 0), (0, LANES - A_HEADS)))
    dpf, db_f = _fgate_bwd(pf, b_f, pad_heads(dck), pad_heads(dcq), "a_dgate_scan")
    g["a_b_f"] = db_f[:, :A_HEADS]
    d_w_qkv = _mm(h1, dqkv, mode="tn", tm=1024, tn=512, tk=S, name="a_dwqkv")
    d_w_f = _mm(h1, dpf, mode="tn", tm=1024, tn=LANES, tk=1024, name="a_dwgate")
    g["a_w_in"] = jnp.concatenate([d_w_qkv, d_w_f[:, :A_HEADS]], axis=1)
    dh1 = _mm(dqkv, w_qkv, mode="nt", tm=1024, tn=512, tk=3072, name="a_dh")
    dh1 = _mm(dpf, w_f, mode="nt", tm=1024, tn=512, tk=LANES, res=dh1, name="a_dh_gate")
    dx, _, (dg_mix0,) = _rms_bwd(x, dx, [(p["mix_norm_g"][0], dh1)], "a_dnorm")

    g["mix_norm_g"] = jnp.stack([dg_mix0.reshape(D), dg_mix1.reshape(D)])
    g["ffn_norm_g"] = jnp.stack([gf0["norm_g"], gf1["norm_g"]])
    g["ffn_conv_b"] = jnp.stack([gf0["conv_b"], gf1["conv_b"]])
    return loss[0, 0], dx, g


_SHARD_SHAPES = {"a_w_in": (1, 1024, 772), "a_w_out": (1, 256, 1024), "b_w_q": (1, 1024, 384), "b_w_out": (1, 512, 256),
                 "w_kv": (1024, 768), "ffn_w_up": (2, 1024, 1408), "ffn_w_down": (2, 704, 1024), "ffn_conv_w": (2, 3, 1408)}
_SMALL = (("kv_norm_g", (1024,)), ("mix_norm_g", (2, 1024)), ("ffn_norm_g", (2, 1024)), ("final_norm_g", (1024,)),
          ("a_b_f", (1, 16)), ("ffn_conv_b", (2, 5632)))


def _unslabs(rows, L, R, C, rpad):
    nc = -(-C // FLAT_W)
    return rows.reshape(L, nc, rpad, FLAT_W).transpose(0, 2, 1, 3).reshape(L, rpad, nc * FLAT_W)[:, :R, :C]


_SEG_RT = {"ffn_w_down": 704, "a_w_in": 1024, "a_w_out": 256, "b_w_q": 1024, "b_w_out": 512, "w_kv": 1024,
           "ffn_w_up": 1024, "ffn_conv_w": 16}
_ROW_SHARDED = ("a_w_out", "ffn_w_down")


_LAYOUTS = {"early": ("a_w_in", "a_w_out"), "late": ("ffn_w_down", "b_w_q", "b_w_out", "w_kv", "ffn_w_up", "ffn_conv_w"),
            "grad_early": ("a_w_in", "a_w_out"),
            "grad_late": ("ffn_w_up", "ffn_w_down", "b_w_q", "w_kv", "b_w_out", "ffn_conv_w")}
_GRAD_ROWS = {"grad_early": 10240, "grad_late": 45056}


def _layout_rows(layout):
    used = sum(_seg_rows(*s) for s in _SEGS if s[0] in _LAYOUTS[layout])
    rows = _GRAD_ROWS.get(layout, used)
    assert rows >= used
    return rows


def _grad_layout(name):
    return "grad_early" if name in _LAYOUTS["grad_early"] else "grad_late"


def _seg(name, layout=None):
    layout = layout or _grad_layout(name)
    off = 0
    for s in sorted((s for s in _SEGS if s[0] in _LAYOUTS[layout]), key=lambda s: _LAYOUTS[layout].index(s[0])):
        _, L, R, C, rpad = s
        if layout in _GRAD_ROWS:
            per_layer = -(-C // FLAT_W) * rpad
            off = -(-off // per_layer) * per_layer
        if s[0] == name:
            rt = _SEG_RT[name]
            assert off % rt == 0 and rpad % rt == 0
            half = _layout_rows(layout) // 2
            assert off + _seg_rows(*s) <= 2 * half
            assert layout not in _GRAD_ROWS or half % rt == 0 or off + _seg_rows(*s) <= half
            return dict(L=L, R=R, C=C, rpad=rpad, nc=-(-C // FLAT_W), rt=rt, off=off, ni=rpad // rt, half=half)
        off += _seg_rows(*s)
    raise KeyError(name)


def _flat_block(sg, term=0):
    base = (sg["off"] + term * sg["L"] * sg["nc"] * sg["rpad"]) // sg["rt"]
    return lambda l, j, i: base + (l * sg["nc"] + j) * sg["ni"] + i


def _native3(t, name):
    sg = _seg(name)
    t = t.reshape(sg["L"], sg["R"], sg["C"])
    return jnp.pad(t, ((0, 0), (0, sg["rpad"] - sg["R"]), (0, 0))) if sg["rpad"] != sg["R"] else t


def _slab_pack(flat, t, name, layout, term=None):
    sg = _seg(name, layout)
    rt = sg["rt"]
    rb = _flat_block(sg, term or 0)

    def body(*refs):
        t_ref, o_ref = refs[-2], refs[-1]
        val = t_ref[...]
        o_ref[...] = val.astype(BF16) if term is None else _split3(val)[term]

    in_specs = [pl.BlockSpec((None, rt, FLAT_W), lambda l, j, i: (l, i, j))]
    args = [t]
    if flat is not None:
        in_specs, args = [ANY] + in_specs, [flat] + args
    return pl.pallas_call(
        body, out_shape=jax.ShapeDtypeStruct((_layout_rows(layout), FLAT_W), BF16), grid=(sg["L"], sg["nc"], sg["ni"]),
        in_specs=in_specs, out_specs=pl.BlockSpec((rt, FLAT_W), lambda l, j, i: (rb(l, j, i), 0)),
        input_output_aliases={0: 0} if flat is not None else {},
        compiler_params=_cparams(("parallel", "parallel", "parallel")), name=f"pack_{name}_{term or 0}",
    )(*args)


def _full_spec(sg, name):
    rt, nc, ni = sg["rt"], sg["nc"], sg["ni"]
    if name in _ROW_SHARDED:
        return (sg["L"], N_CHIPS * sg["R"], sg["C"]), pl.BlockSpec((None, rt, FLAT_W), lambda k, l, j, i: (l, k * ni + i, j))
    return ((sg["L"], sg["rpad"], N_CHIPS * nc * FLAT_W),
            pl.BlockSpec((None, rt, FLAT_W), lambda k, l, j, i: (l, i, k * nc + j)))


def _slab_unpack(gathered, slots, name, layout, own=None):
    sg = _seg(name, layout)
    rb = _flat_block(sg)
    shape, _ = _full_spec(sg, name)
    rt, nc, ni = sg["rt"], sg["nc"], sg["ni"]
    width = nc * FLAT_W
    last = gathered.shape[0] - 1

    def body(*refs):
        s_ref, o_ref = refs[0], refs[-1]
        is_own = s_ref[pl.program_id(0)] == OWN_SLOT
        for j in range(nc):
            val = refs[1 + j][...]
            if own is not None:
                val = jnp.where(is_own, refs[1 + nc + j][...], val)
            o_ref[:, j * FLAT_W:(j + 1) * FLAT_W] = val

    if name in _ROW_SHARDED:
        o_spec = pl.BlockSpec((None, rt, width), lambda k, l, i, s: (l, k * ni + i, 0))
    else:
        o_spec = pl.BlockSpec((None, rt, width), lambda k, l, i, s: (l, i, k))
    in_specs = [pl.BlockSpec((None, rt, FLAT_W), lambda k, l, i, s, j=j: (jnp.minimum(s[k], last), rb(l, j, i), 0))
                for j in range(nc)]
    args = [gathered] * nc
    if own is not None:
        in_specs += [pl.BlockSpec((rt, FLAT_W), lambda k, l, i, s, j=j: (rb(l, j, i), 0)) for j in range(nc)]
        args += [own] * nc
    return pl.pallas_call(
        body, out_shape=jax.ShapeDtypeStruct(shape, BF16),
        grid_spec=pltpu.PrefetchScalarGridSpec(num_scalar_prefetch=1, grid=(N_CHIPS, sg["L"], ni), in_specs=in_specs,
                                               out_specs=o_spec),
        compiler_params=_cparams(("parallel",) * 3), name=f"unpack_{name}",
    )(slots, *args)


def _slab_pack_grad(flat4, g, name, layer=None):
    sg = _seg(name)
    rows = _layout_rows(_grad_layout(name))
    shape, _ = _full_spec(sg, name)
    n_layers = sg["L"] if layer is None else 1
    assert g.shape == (n_layers,) + shape[1:], (name, g.shape, shape)
    rt, nc = sg["rt"], sg["nc"]
    assert sg["ni"] == 1 and sg["off"] % (nc * rt) == 0
    base = sg["off"] // (nc * rt) + (layer or 0)

    def body(*refs):
        g_ref, o_ref = refs[-2], refs[-1]
        for j in range(nc):
            o_ref[j * rt:(j + 1) * rt, :] = g_ref[:, j * FLAT_W:(j + 1) * FLAT_W]

    if name in _ROW_SHARDED:
        spec = pl.BlockSpec((None, rt, nc * FLAT_W), lambda k, l: (l, k, 0))
    else:
        spec = pl.BlockSpec((None, rt, nc * FLAT_W), lambda k, l: (l, 0, k))
    in_specs, args = [spec], [g]
    if flat4 is not None:
        in_specs, args = [pl.BlockSpec(memory_space=pl.ANY)] + in_specs, [flat4] + args
    return pl.pallas_call(
        body, out_shape=jax.ShapeDtypeStruct((N_CHIPS, rows, FLAT_W), F32), grid=(N_CHIPS, n_layers),
        in_specs=in_specs, out_specs=pl.BlockSpec((None, nc * rt, FLAT_W), lambda k, l: (k, base + l, 0)),
        input_output_aliases={0: 0} if flat4 is not None else {},
        compiler_params=_cparams(("parallel",) * 2), name=f"packgrad_{name}_{layer or 0}",
    )(*args)


def _adamw_shard(w, m, v, g_mine, g_other, c_arr, name):
    sg = _seg(name)
    rt = sg["rt"]
    rb = _flat_block(sg)
    per_half = sg["half"] // rt

    def half_of(l, j, i):
        return (rb(l, j, i) * rt) // sg["half"]

    def body(c_ref, w_ref, m_ref, v_ref, gm_ref, go_ref, g_ref, d_ref, mo_ref, vo_ref):
        is_mine = half_of(pl.program_id(0), pl.program_id(1), pl.program_id(2)) == c_ref[0]
        gv = jnp.where(is_mine, gm_ref[...], go_ref[...])
        g_ref[...] = gv
        d_ref[...], mo_ref[...], vo_ref[...] = _adam_update(w_ref[...], gv, m_ref[...], v_ref[...])

    nat = pl.BlockSpec((None, rt, FLAT_W), lambda l, j, i, c: (l, i, j))
    half = pl.BlockSpec((rt, FLAT_W), lambda l, j, i, c: (rb(l, j, i) - half_of(l, j, i) * per_half, 0))
    sds = jax.ShapeDtypeStruct(w.shape, F32)
    return pl.pallas_call(
        body, out_shape=[sds] * 4,
        grid_spec=pltpu.PrefetchScalarGridSpec(num_scalar_prefetch=1, grid=(sg["L"], sg["nc"], sg["ni"]),
                                               in_specs=[nat, nat, nat, half, half], out_specs=[nat] * 4),
        compiler_params=_cparams(("parallel", "parallel", "parallel")), name=f"adamw_{name}",
    )(c_arr, w, m, v, g_mine, g_other)


def _pack_small(vals, loss=None):
    parts = [vals[name].astype(F32).reshape(-1) for name, _ in _SMALL]
    if loss is not None:
        parts.append(loss.reshape(1))
    flat = jnp.concatenate(parts)
    return jnp.pad(flat, (0, SMALL_ROWS * 1024 - flat.shape[0])).reshape(SMALL_ROWS, 1024)


def _unpack_small(flat):
    flat = flat.reshape(-1)
    out = {}
    o = 0
    for name, shape in _SMALL:
        n = int(np.prod(shape))
        out[name] = flat[o:o + n].reshape(shape)
        o += n
    return out, flat[o]


_BIG = ("a_w_in", "a_w_out", "b_w_q", "b_w_out", "w_kv", "ffn_w_up", "ffn_w_down", "ffn_conv_w")
A_IN_PAD = 896


def _pack_weights(w, layout):
    flat = None
    for name in _LAYOUTS[layout]:
        t = _native3(w[name], name)
        for term in ((0, 1, 2) if name == "ffn_conv_w" else (None,)):
            flat = _slab_pack(flat, t, name, layout, term)
    return flat


def _early_weights(gathered, slots):
    a_in = _slab_unpack(gathered, slots, "a_w_in", "early")
    a_in = a_in.reshape(D, N_CHIPS, A_IN_PAD)[:, :, :772].reshape(D, N_CHIPS * 772)
    return dict(a_w_in=a_in, a_w_out=_slab_unpack(gathered, slots, "a_w_out", "early")[0])


def _late_weights(landed, slots, own):
    full = {name: _slab_unpack(landed, slots, name, "late", own) for name in _LAYOUTS["late"] if name != "ffn_conv_w"}
    sg = _seg("ffn_conv_w", "late")
    n1 = sg["nc"] * sg["rpad"]
    conv = slice(sg["off"], sg["off"] + CONV_TERMS * n1)
    conv_rows = jnp.concatenate([landed[:, conv], own[None, conv]], axis=0)
    per_chip = []
    for k in range(N_CHIPS):
        rows = lax.dynamic_index_in_dim(conv_rows, slots[k], axis=0, keepdims=False)
        terms = [_unslabs(rows[i * n1:(i + 1) * n1], 1, sg["R"], sg["C"], sg["rpad"]).astype(F32) for i in range(CONV_TERMS)]
        per_chip.append((terms[0] + terms[1]) + terms[2])
    cw = jnp.concatenate(per_chip, axis=2).reshape(2, 3, 2, FF).transpose(0, 2, 1, 3)
    return dict(b_w_q=full["b_w_q"][0], b_w_out=full["b_w_out"][0], w_kv=full["w_kv"][0], ffn_w_up=full["ffn_w_up"],
                ffn_w_down=full["ffn_w_down"], conv_w=cw)


def _shard_grads(g, layout):
    def full(name):
        if name == "a_w_in":
            a_in = jnp.pad(g[name].reshape(D, N_CHIPS, 772), ((0, 0), (0, 0), (0, A_IN_PAD - 772)))
            return a_in.reshape(1, D, N_CHIPS * A_IN_PAD)
        if name == "ffn_conv_w":
            sgc = _seg(name)
            return jnp.pad(g[name].reshape(1, sgc["R"], 2 * FF), ((0, 0), (0, sgc["rpad"] - sgc["R"]), (0, 0)))
        return g[name] if g[name].ndim == 3 else g[name][None]

    flat4 = None
    for name in _LAYOUTS[layout]:
        if isinstance(g[name], (list, tuple)):
            for layer, t in enumerate(g[name]):
                flat4 = _slab_pack_grad(flat4, t[None], name, layer)
        else:
            flat4 = _slab_pack_grad(flat4, full(name), name)
    return flat4


_WEIGHTS = ["a_w_in", "a_b_f", "a_w_out", "b_w_q", "b_w_out", "kv_norm_g", "w_kv", "mix_norm_g", "ffn_norm_g", "ffn_w_up",
            "ffn_conv_w", "ffn_conv_b", "ffn_w_down", "final_norm_g"]


def kernel(x, a_w_in, a_b_f, a_w_out, b_w_q, b_w_out, kv_norm_g, w_kv, mix_norm_g, ffn_norm_g, ffn_w_up, ffn_conv_w, ffn_conv_b, ffn_w_down, final_norm_g, loss_target, m_a_w_in, m_a_b_f, m_a_w_out, m_b_w_q, m_b_w_out, m_kv_norm_g, m_w_kv, m_mix_norm_g, m_ffn_norm_g, m_ffn_w_up, m_ffn_conv_w, m_ffn_conv_b, m_ffn_w_down, m_final_norm_g, v_a_w_in, v_a_b_f, v_a_w_out, v_b_w_q, v_b_w_out, v_kv_norm_g, v_w_kv, v_mix_norm_g, v_ffn_norm_g, v_ffn_w_up, v_ffn_conv_w, v_ffn_conv_b, v_ffn_w_down, v_final_norm_g):
    w = dict(a_w_in=a_w_in, a_b_f=a_b_f, a_w_out=a_w_out, b_w_q=b_w_q, b_w_out=b_w_out, kv_norm_g=kv_norm_g, w_kv=w_kv,
             mix_norm_g=mix_norm_g, ffn_norm_g=ffn_norm_g, ffn_w_up=ffn_w_up, ffn_conv_w=ffn_conv_w, ffn_conv_b=ffn_conv_b,
             ffn_w_down=ffn_w_down, final_norm_g=final_norm_g)
    m = dict(a_w_in=m_a_w_in, a_b_f=m_a_b_f, a_w_out=m_a_w_out, b_w_q=m_b_w_q, b_w_out=m_b_w_out, kv_norm_g=m_kv_norm_g,
             w_kv=m_w_kv, mix_norm_g=m_mix_norm_g, ffn_norm_g=m_ffn_norm_g, ffn_w_up=m_ffn_w_up, ffn_conv_w=m_ffn_conv_w,
             ffn_conv_b=m_ffn_conv_b, ffn_w_down=m_ffn_w_down, final_norm_g=m_final_norm_g)
    v = dict(a_w_in=v_a_w_in, a_b_f=v_a_b_f, a_w_out=v_a_w_out, b_w_q=v_b_w_q, b_w_out=v_b_w_out, kv_norm_g=v_kv_norm_g,
             w_kv=v_w_kv, mix_norm_g=v_mix_norm_g, ffn_norm_g=v_ffn_norm_g, ffn_w_up=v_ffn_w_up, ffn_conv_w=v_ffn_conv_w,
             ffn_conv_b=v_ffn_conv_b, ffn_w_down=v_ffn_w_down, final_norm_g=v_final_norm_g)

    c_arr = lax.axis_index("c").astype(jnp.int32).reshape(1)
    k_arr = (2 * lax.axis_index("x") + lax.axis_index("y")).astype(jnp.int32).reshape(1)
    xi, yi = lax.axis_index("x"), lax.axis_index("y")
    late_slots = jnp.stack([jnp.where(k == k_arr[0], OWN_SLOT, 2 * ((k & 1) ^ yi) + ((k >> 1) ^ xi) - 1)
                            for k in range(N_CHIPS)]).astype(jnp.int32)
    w_late = _pack_weights(w, "late")
    land = lax.empty((OWN_SLOT,) + w_late.shape, BF16)
    send_sems, recv_sems, w_thru, land_thru, token = _late_gather_start(w_late, land)
    w_early = _pack_weights(w, "early")
    early = _allgather_shards(w_early, _place_own(w_early, k_arr, "early_place_own"))
    p = _early_weights(early, jnp.arange(N_CHIPS, dtype=jnp.int32))
    cb = ffn_conv_b.reshape(2, 2, 1, FF)
    p.update(a_b_f=a_b_f, kv_norm_g=kv_norm_g, mix_norm_g=mix_norm_g + token[0, 0], ffn_norm_g=ffn_norm_g,
             final_norm_g=final_norm_g, conv_b=cb)

    def late_weights(after):
        own, landed = _late_gather_wait(send_sems, recv_sems, w_thru, land_thru, after)
        return _late_weights(landed, late_slots, own)

    started = {}

    def late_grads_ready(g_so_far):
        gflat = _shard_grads(g_so_far, "grad_late")
        land = lax.empty((N_CHIPS, gflat.shape[1] // 2, FLAT_W), F32)
        *handles, token = _sibling_swap_start(gflat, land)
        started["swap"] = handles
        return token

    def late_grads_continue(after):
        gflat, other = _sibling_swap_wait(*started["swap"], after)
        pair = _pair_sum(gflat, other, c_arr, "late")
        land = lax.empty((3,) + pair.shape[1:], BF16)
        *handles, token = _chip_exchange_start(pair, land, "late")
        started["handles"] = handles
        return token

    loss_part, grad_x, g = _local_step(x[0], loss_target[0], p, late_weights, late_grads_ready, late_grads_continue)

    gflat = _shard_grads(g, "grad_early")
    pair_e = _pair_sum(gflat, _sibling_swap_half(gflat, "early"), c_arr, "early")
    *early_handles, token = _chip_exchange_start(pair_e, lax.empty((3,) + pair_e.shape[1:], BF16), "early")

    big = [{}, {}, {}, {}]

    def adamw_group(layout, g_mine):
        g_other = _sibling_send(g_mine, layout)
        for name in _LAYOUTS[layout]:
            sg = _seg(name)
            res = _adamw_shard(_native3(w[name], name), _native3(m[name], name), _native3(v[name], name), g_mine,
                               g_other, c_arr, name)
            for store, t in zip(big, res):
                store[name] = t[:, :sg["R"], :].reshape(_SHARD_SHAPES[name])
        return res[1]

    pair, landed = _chip_exchange_wait(*started["handles"], token, "late")
    last = adamw_group("grad_late", _chip_sum(pair, landed, k_arr, "late"))
    small, loss = _unpack_small(_allreduce_small(_pack_small(g, loss_part)))
    dws, mns, vns = _adamw(_pack_small(w), _pack_small(small), _pack_small(m), _pack_small(v), "adamw_small")
    pair_e, landed_e = _chip_exchange_wait(*early_handles, last, "early")
    adamw_group("grad_early", _chip_sum(pair_e, landed_e, k_arr, "early"))
    sml = [small] + [_unpack_small(t)[0] for t in (dws, mns, vns)]
    outs = [loss, grad_x[None]]
    for b, s in zip(big, sml):
        outs += [b[n] if n in b else s[n] for n in _WEIGHTS]
    return tuple(outs)
```
